```python
import jax, jax.numpy as jnp
from jax import lax
import numpy as np

D_MODEL = 1024
BATCH = 8
SEQ = 4096
DEPTH = 1

CHUNK = 64
MEM_LEN = 256
GLA_HEADS = 4
GLA_DK = D_MODEL // 2
GLA_DV = D_MODEL
GLA_HDK = GLA_DK // GLA_HEADS
GLA_HDV = GLA_DV // GLA_HEADS
GLA_GATE_RANK = 16
GLA_GATE_TEMP = 16.0
POOL_WINDOWS = (2, 4, 8, 16)
POOL_GROUPS = len(POOL_WINDOWS)
POOL_WIDTH = D_MODEL // 2
POOL_GROUP_DIM = POOL_WIDTH // POOL_GROUPS
XA_HEADS = 4
XA_HEAD_DIM = 128
XA_WIDTH = XA_HEADS * XA_HEAD_DIM
N_BRANCH = 3
D_FF = 2816
EPS = 1e-6

IN_SPLITS = (GLA_DK, GLA_DK, GLA_DV, GLA_DV, GLA_GATE_RANK, POOL_WIDTH, XA_WIDTH, N_BRANCH * D_MODEL)
IN_WIDTH = sum(IN_SPLITS)

kernel_name = "hybrid_gla_pool_memxattn_macaron_block"


def rms_norm(x, g):
    xf = x.astype(jnp.float32)
    y = xf * lax.rsqrt(jnp.mean(xf * xf, axis=-1, keepdims=True) + EPS)
    return (y * g.astype(jnp.float32)).astype(x.dtype)


def swiglu(h, w_in, w_out):
    a, b = jnp.split(h @ w_in, 2, axis=-1)
    return (jax.nn.silu(a) * b) @ w_out


def gla_chunked(q, k, v, log_a):
    B, S = q.shape[0], q.shape[1]
    nc = S // CHUNK

    def to_chunks(t):
        return t.reshape(B, nc, CHUNK, GLA_HEADS, t.shape[-1]).transpose(1, 0, 3, 2, 4)

    qc, kc, vc = to_chunks(q), to_chunks(k), to_chunks(v)
    b = jnp.cumsum(to_chunks(log_a.astype(jnp.float32)), axis=3)
    b_end = b[:, :, :, -1:, :]
    kt = (kc.astype(jnp.float32) * jnp.exp(b_end - b)).astype(v.dtype)
    decay = jnp.exp(b_end[:, :, :, 0, :]).astype(v.dtype)

    def step(state, inp):
        q_c, k_c, v_c, d_c = inp
        state = d_c[..., None] * state + jnp.einsum('bhck,bhcv->bhkv', k_c, v_c)
        o = jnp.einsum('bhck,bhkv->bhcv', q_c, state)
        return state, o

    s0 = jnp.zeros((B, GLA_HEADS, GLA_HDK, GLA_HDV), v.dtype)
    _, o = lax.scan(step, s0, (qc, kt, vc, decay))
    return o.transpose(1, 0, 3, 2, 4).reshape(B, S, GLA_DV)


def multiscale_pool(p, w_pool, pool_scale):
    B, S, _ = p.shape
    pg = p.reshape(B, S, POOL_GROUPS, POOL_GROUP_DIM).astype(jnp.float32)
    c0 = jnp.concatenate([jnp.zeros((B, 1, POOL_GROUPS, POOL_GROUP_DIM), jnp.float32),
                          jnp.cumsum(pg, axis=1)], axis=1)
    pos = jnp.arange(1, S + 1, dtype=jnp.float32)
    outs = []
    for g, w in enumerate(POOL_WINDOWS):
        cg = c0[:, :, g]
        lag = jnp.concatenate([jnp.zeros((B, w - 1, POOL_GROUP_DIM), jnp.float32), cg[:, :S + 1 - w]], axis=1)
        cnt = jnp.minimum(pos, float(w))[None, :, None]
        outs.append((cg[:, 1:] - lag) / cnt - pg[:, :, g])
    mixed = jnp.stack(outs, axis=2).astype(p.dtype)
    y = jnp.einsum('bsgc,gcd->bsgd', mixed, w_pool).reshape(B, S, POOL_WIDTH)
    return y * pool_scale


def memory_cross_attention(xq, mem_n, w_mem_kv):
    B, S, _ = xq.shape
    M = mem_n.shape[1]
    q = xq.reshape(B, S, XA_HEADS, XA_HEAD_DIM)
    k, v = jnp.split(mem_n @ w_mem_kv, 2, axis=-1)
    k = k.reshape(B, M, XA_HEADS, XA_HEAD_DIM)
    v = v.reshape(B, M, XA_HEADS, XA_HEAD_DIM)
    s = jnp.einsum('bshd,bmhd->bhsm', q, k).astype(jnp.float32) * (XA_HEAD_DIM ** -0.5)
    pr = jax.nn.softmax(s, axis=-1).astype(v.dtype)
    return jnp.einsum('bhsm,bmhd->bshd', pr, v).reshape(B, S, XA_WIDTH)


def token_mixing(h, mem, w_in, w_fu, b_f, gla_norm_g, w_pool, pool_scale, mem_norm_g, w_mem_kv,
                 w_up_gla, w_up_pool, w_up_xattn, w_o):
    B, S, _ = h.shape
    idx = np.cumsum(IN_SPLITS)[:-1].tolist()
    q, k, v, g_out, f_low, p_in, xq, gates = jnp.split(h @ w_in, idx, axis=-1)
    q = q.reshape(B, S, GLA_HEADS, GLA_HDK) * (GLA_HDK ** -0.5)
    k = k.reshape(B, S, GLA_HEADS, GLA_HDK)
    v = v.reshape(B, S, GLA_HEADS, GLA_HDV)
    f = (f_low @ w_fu + b_f).astype(jnp.float32)
    log_a = (jax.nn.log_sigmoid(f) / GLA_GATE_TEMP).reshape(B, S, GLA_HEADS, GLA_HDK)
    o = gla_chunked(q, k, v, log_a).reshape(B, S, GLA_HEADS, GLA_HDV)
    o = rms_norm(o, gla_norm_g.reshape(GLA_HEADS, GLA_HDV)).reshape(B, S, GLA_DV)
    y_a = (o * jax.nn.silu(g_out)) @ w_up_gla
    y_b = multiscale_pool(p_in, w_pool, pool_scale) @ w_up_pool
    y_c = memory_cross_attention(xq, rms_norm(mem, mem_norm_g), w_mem_kv) @ w_up_xattn
    gt = jax.nn.sigmoid(gates.reshape(B, S, N_BRANCH, D_MODEL))
    merged = gt[:, :, 0] * y_a + gt[:, :, 1] * y_b + gt[:, :, 2] * y_c
    return merged @ w_o


def _fwd_setup_inputs(seed: int = 0) -> dict:
    key = jax.random.key(seed)
    ks = jax.random.split(key, 32)
    L = DEPTH

    def dense(k, shape, fan_in):
        return jax.random.normal(k, shape, jnp.float32) * (fan_in ** -0.5)

    def gain(k, n):
        return 1.0 + 0.02 * jax.random.normal(k, (L, n), jnp.float32)

    return {
        "x": jax.random.normal(ks[0], (BATCH, SEQ, D_MODEL), jnp.float32),
        "mem": jax.random.normal(ks[1], (BATCH, MEM_LEN, D_MODEL), jnp.float32),
        "ffn1_pre_g": gain(ks[2], D_MODEL),
        "ffn1_w_in": dense(ks[3], (L, D_MODEL, 2 * D_FF), D_MODEL),
        "ffn1_w_out": dense(ks[4], (L, D_FF, D_MODEL), D_FF),
        "ffn1_post_g": gain(ks[5], D_MODEL),
        "mix_pre_g": gain(ks[6], D_MODEL),
        "w_in": dense(ks[7], (L, D_MODEL, IN_WIDTH), D_MODEL),
        "w_fu": dense(ks[8], (L, GLA_GATE_RANK, GLA_DK), GLA_GATE_RANK),
        "b_f": 0.1 * jax.random.normal(ks[9], (L, GLA_DK), jnp.float32),
        "gla_norm_g": gain(ks[10], GLA_DV),
        "w_pool": dense(ks[11], (L, POOL_GROUPS, POOL_GROUP_DIM, POOL_GROUP_DIM), POOL_GROUP_DIM),
        "pool_scale": gain(ks[12], POOL_WIDTH),
        "mem_norm_g": gain(ks[13], D_MODEL),
        "w_mem_kv": dense(ks[14], (L, D_MODEL, 2 * XA_WIDTH), D_MODEL),
        "w_up_gla": dense(ks[15], (L, GLA_DV, D_MODEL), GLA_DV),
        "w_up_pool": dense(ks[16], (L, POOL_WIDTH, D_MODEL), POOL_WIDTH),
        "w_up_xattn": dense(ks[17], (L, XA_WIDTH, D_MODEL), XA_WIDTH),
        "w_o": dense(ks[18], (L, D_MODEL, D_MODEL), D_MODEL),
        "mix_post_g": gain(ks[19], D_MODEL),
        "ffn2_pre_g": gain(ks[20], D_MODEL),
        "ffn2_w_in": dense(ks[21], (L, D_MODEL, 2 * D_FF), D_MODEL),
        "ffn2_w_out": dense(ks[22], (L, D_FF, D_MODEL), D_FF),
        "ffn2_post_g": gain(ks[23], D_MODEL),
        "final_g": gain(ks[24], D_MODEL),
    }


def _fwd_reference(x, mem, ffn1_pre_g, ffn1_w_in, ffn1_w_out, ffn1_post_g, mix_pre_g, w_in, w_fu, b_f,
              gla_norm_g, w_pool, pool_scale, mem_norm_g, w_mem_kv, w_up_gla, w_up_pool, w_up_xattn,
              w_o, mix_post_g, ffn2_pre_g, ffn2_w_in, ffn2_w_out, ffn2_post_g, final_g):
    for l in range(DEPTH):
        x = x + 0.5 * rms_norm(swiglu(rms_norm(x, ffn1_pre_g[l]), ffn1_w_in[l], ffn1_w_out[l]), ffn1_post_g[l])
        h = rms_norm(x, mix_pre_g[l])
        y = token_mixing(h, mem, w_in[l], w_fu[l], b_f[l], gla_norm_g[l], w_pool[l], pool_scale[l],
                         mem_norm_g[l], w_mem_kv[l], w_up_gla[l], w_up_pool[l], w_up_xattn[l], w_o[l])
        x = x + rms_norm(y, mix_post_g[l])
        x = x + 0.5 * rms_norm(swiglu(rms_norm(x, ffn2_pre_g[l]), ffn2_w_in[l], ffn2_w_out[l]), ffn2_post_g[l])
        x = rms_norm(x, final_g[l])
    return x


import jax as _jax
import jax.numpy as _jnp

TWIN_FORMAT = 'train_step'
FWD_PARAMS = ['x', 'mem', 'ffn1_pre_g', 'ffn1_w_in', 'ffn1_w_out', 'ffn1_post_g', 'mix_pre_g', 'w_in', 'w_fu', 'b_f', 'gla_norm_g', 'w_pool', 'pool_scale', 'mem_norm_g', 'w_mem_kv', 'w_up_gla', 'w_up_pool', 'w_up_xattn', 'w_o', 'mix_post_g', 'ffn2_pre_g', 'ffn2_w_in', 'ffn2_w_out', 'ffn2_post_g', 'final_g']
TWIN_WEIGHTS = ['ffn1_pre_g', 'ffn1_w_in', 'ffn1_w_out', 'ffn1_post_g', 'mix_pre_g', 'w_in', 'w_fu', 'b_f', 'gla_norm_g', 'w_pool', 'pool_scale', 'mem_norm_g', 'w_mem_kv', 'w_up_gla', 'w_up_pool', 'w_up_xattn', 'w_o', 'mix_post_g', 'ffn2_pre_g', 'ffn2_w_in', 'ffn2_w_out', 'ffn2_post_g', 'final_g']
TWIN_DIFF_INPUT = 'x'
TWIN_INPUTS = ['x', 'mem', 'ffn1_pre_g', 'ffn1_w_in', 'ffn1_w_out', 'ffn1_post_g', 'mix_pre_g', 'w_in', 'w_fu', 'b_f', 'gla_norm_g', 'w_pool', 'pool_scale', 'mem_norm_g', 'w_mem_kv', 'w_up_gla', 'w_up_pool', 'w_up_xattn', 'w_o', 'mix_post_g', 'ffn2_pre_g', 'ffn2_w_in', 'ffn2_w_out', 'ffn2_post_g', 'final_g', 'loss_target', 'm_ffn1_pre_g', 'm_ffn1_w_in', 'm_ffn1_w_out', 'm_ffn1_post_g', 'm_mix_pre_g', 'm_w_in', 'm_w_fu', 'm_b_f', 'm_gla_norm_g', 'm_w_pool', 'm_pool_scale', 'm_mem_norm_g', 'm_w_mem_kv', 'm_w_up_gla', 'm_w_up_pool', 'm_w_up_xattn', 'm_w_o', 'm_mix_post_g', 'm_ffn2_pre_g', 'm_ffn2_w_in', 'm_ffn2_w_out', 'm_ffn2_post_g', 'm_final_g', 'v_ffn1_pre_g', 'v_ffn1_w_in', 'v_ffn1_w_out', 'v_ffn1_post_g', 'v_mix_pre_g', 'v_w_in', 'v_w_fu', 'v_b_f', 'v_gla_norm_g', 'v_w_pool', 'v_pool_scale', 'v_mem_norm_g', 'v_w_mem_kv', 'v_w_up_gla', 'v_w_up_pool', 'v_w_up_xattn', 'v_w_o', 'v_mix_post_g', 'v_ffn2_pre_g', 'v_ffn2_w_in', 'v_ffn2_w_out', 'v_ffn2_post_g', 'v_final_g']
TWIN_OUTPUTS = ['loss', 'grad_x', 'grad_ffn1_pre_g', 'grad_ffn1_w_in', 'grad_ffn1_w_out', 'grad_ffn1_post_g', 'grad_mix_pre_g', 'grad_w_in', 'grad_w_fu', 'grad_b_f', 'grad_gla_norm_g', 'grad_w_pool', 'grad_pool_scale', 'grad_mem_norm_g', 'grad_w_mem_kv', 'grad_w_up_gla', 'grad_w_up_pool', 'grad_w_up_xattn', 'grad_w_o', 'grad_mix_post_g', 'grad_ffn2_pre_g', 'grad_ffn2_w_in', 'grad_ffn2_w_out', 'grad_ffn2_post_g', 'grad_final_g', 'delta_ffn1_pre_g', 'delta_ffn1_w_in', 'delta_ffn1_w_out', 'delta_ffn1_post_g', 'delta_mix_pre_g', 'delta_w_in', 'delta_w_fu', 'delta_b_f', 'delta_gla_norm_g', 'delta_w_pool', 'delta_pool_scale', 'delta_mem_norm_g', 'delta_w_mem_kv', 'delta_w_up_gla', 'delta_w_up_pool', 'delta_w_up_xattn', 'delta_w_o', 'delta_mix_post_g', 'delta_ffn2_pre_g', 'delta_ffn2_w_in', 'delta_ffn2_w_out', 'delta_ffn2_post_g', 'delta_final_g', 'new_m_ffn1_pre_g', 'new_m_ffn1_w_in', 'new_m_ffn1_w_out', 'new_m_ffn1_post_g', 'new_m_mix_pre_g', 'new_m_w_in', 'new_m_w_fu', 'new_m_b_f', 'new_m_gla_norm_g', 'new_m_w_pool', 'new_m_pool_scale', 'new_m_mem_norm_g', 'new_m_w_mem_kv', 'new_m_w_up_gla', 'new_m_w_up_pool', 'new_m_w_up_xattn', 'new_m_w_o', 'new_m_mix_post_g', 'new_m_ffn2_pre_g', 'new_m_ffn2_w_in', 'new_m_ffn2_w_out', 'new_m_ffn2_post_g', 'new_m_final_g', 'new_v_ffn1_pre_g', 'new_v_ffn1_w_in', 'new_v_ffn1_w_out', 'new_v_ffn1_post_g', 'new_v_mix_pre_g', 'new_v_w_in', 'new_v_w_fu', 'new_v_b_f', 'new_v_gla_norm_g', 'new_v_w_pool', 'new_v_pool_scale', 'new_v_mem_norm_g', 'new_v_w_mem_kv', 'new_v_w_up_gla', 'new_v_w_up_pool', 'new_v_w_up_xattn', 'new_v_w_o', 'new_v_mix_post_g', 'new_v_ffn2_pre_g', 'new_v_ffn2_w_in', 'new_v_ffn2_w_out', 'new_v_ffn2_post_g', 'new_v_final_g']
TWIN_LEAF_KINDS = {'loss': 'loss', 'grad_x': 'grad_x', 'grad_ffn1_pre_g': 'grad_w', 'grad_ffn1_w_in': 'grad_w', 'grad_ffn1_w_out': 'grad_w', 'grad_ffn1_post_g': 'grad_w', 'grad_mix_pre_g': 'grad_w', 'grad_w_in': 'grad_w', 'grad_w_fu': 'grad_w', 'grad_b_f': 'grad_w', 'grad_gla_norm_g': 'grad_w', 'grad_w_pool': 'grad_w', 'grad_pool_scale': 'grad_w', 'grad_mem_norm_g': 'grad_w', 'grad_w_mem_kv': 'grad_w', 'grad_w_up_gla': 'grad_w', 'grad_w_up_pool': 'grad_w', 'grad_w_up_xattn': 'grad_w', 'grad_w_o': 'grad_w', 'grad_mix_post_g': 'grad_w', 'grad_ffn2_pre_g': 'grad_w', 'grad_ffn2_w_in': 'grad_w', 'grad_ffn2_w_out': 'grad_w', 'grad_ffn2_post_g': 'grad_w', 'grad_final_g': 'grad_w', 'delta_ffn1_pre_g': 'delta_w', 'delta_ffn1_w_in': 'delta_w', 'delta_ffn1_w_out': 'delta_w', 'delta_ffn1_post_g': 'delta_w', 'delta_mix_pre_g': 'delta_w', 'delta_w_in': 'delta_w', 'delta_w_fu': 'delta_w', 'delta_b_f': 'delta_w', 'delta_gla_norm_g': 'delta_w', 'delta_w_pool': 'delta_w', 'delta_pool_scale': 'delta_w', 'delta_mem_norm_g': 'delta_w', 'delta_w_mem_kv': 'delta_w', 'delta_w_up_gla': 'delta_w', 'delta_w_up_pool': 'delta_w', 'delta_w_up_xattn': 'delta_w', 'delta_w_o': 'delta_w', 'delta_mix_post_g': 'delta_w', 'delta_ffn2_pre_g': 'delta_w', 'delta_ffn2_w_in': 'delta_w', 'delta_ffn2_w_out': 'delta_w', 'delta_ffn2_post_g': 'delta_w', 'delta_final_g': 'delta_w', 'new_m_ffn1_pre_g': 'new_m', 'new_m_ffn1_w_in': 'new_m', 'new_m_ffn1_w_out': 'new_m', 'new_m_ffn1_post_g': 'new_m', 'new_m_mix_pre_g': 'new_m', 'new_m_w_in': 'new_m', 'new_m_w_fu': 'new_m', 'new_m_b_f': 'new_m', 'new_m_gla_norm_g': 'new_m', 'new_m_w_pool': 'new_m', 'new_m_pool_scale': 'new_m', 'new_m_mem_norm_g': 'new_m', 'new_m_w_mem_kv': 'new_m', 'new_m_w_up_gla': 'new_m', 'new_m_w_up_pool': 'new_m', 'new_m_w_up_xattn': 'new_m', 'new_m_w_o': 'new_m', 'new_m_mix_post_g': 'new_m', 'new_m_ffn2_pre_g': 'new_m', 'new_m_ffn2_w_in': 'new_m', 'new_m_ffn2_w_out': 'new_m', 'new_m_ffn2_post_g': 'new_m', 'new_m_final_g': 'new_m', 'new_v_ffn1_pre_g': 'new_v', 'new_v_ffn1_w_in': 'new_v', 'new_v_ffn1_w_out': 'new_v', 'new_v_ffn1_post_g': 'new_v', 'new_v_mix_pre_g': 'new_v', 'new_v_w_in': 'new_v', 'new_v_w_fu': 'new_v', 'new_v_b_f': 'new_v', 'new_v_gla_norm_g': 'new_v', 'new_v_w_pool': 'new_v', 'new_v_pool_scale': 'new_v', 'new_v_mem_norm_g': 'new_v', 'new_v_w_mem_kv': 'new_v', 'new_v_w_up_gla': 'new_v', 'new_v_w_up_pool': 'new_v', 'new_v_w_up_xattn': 'new_v', 'new_v_w_o': 'new_v', 'new_v_mix_post_g': 'new_v', 'new_v_ffn2_pre_g': 'new_v', 'new_v_ffn2_w_in': 'new_v', 'new_v_ffn2_w_out': 'new_v', 'new_v_ffn2_post_g': 'new_v', 'new_v_final_g': 'new_v'}


def _forward(args):
    return _fwd_reference(*[args[k] for k in FWD_PARAMS])


def _output_shape():
    out = _jax.eval_shape(lambda: _forward(_fwd_setup_inputs(0)))
    return out.shape, out.dtype

N_MICROBATCH = 1
ADAM_LR = 0.001
ADAM_B1 = 0.9
ADAM_B2 = 0.999
ADAM_EPS = 1e-08
ADAM_WD = 0.01
ADAM_STEP = 10
PER_EXAMPLE_BATCH_AXIS = {'x': 0, 'mem': 0, 'loss_target': 0}
SHARED_INPUTS = []
_WEIGHT_DTYPES = {'ffn1_pre_g': _jnp.float32, 'ffn1_w_in': _jnp.float32, 'ffn1_w_out': _jnp.float32, 'ffn1_post_g': _jnp.float32, 'mix_pre_g': _jnp.float32, 'w_in': _jnp.float32, 'w_fu': _jnp.float32, 'b_f': _jnp.float32, 'gla_norm_g': _jnp.float32, 'w_pool': _jnp.float32, 'pool_scale': _jnp.float32, 'mem_norm_g': _jnp.float32, 'w_mem_kv': _jnp.float32, 'w_up_gla': _jnp.float32, 'w_up_pool': _jnp.float32, 'w_up_xattn': _jnp.float32, 'w_o': _jnp.float32, 'mix_post_g': _jnp.float32, 'ffn2_pre_g': _jnp.float32, 'ffn2_w_in': _jnp.float32, 'ffn2_w_out': _jnp.float32, 'ffn2_post_g': _jnp.float32, 'final_g': _jnp.float32}
MOMENT_SCALE = {'ffn1_pre_g': 1.491872e-01, 'ffn1_w_in': 6.289169e-02, 'ffn1_w_out': 1.028251e-01, 'ffn1_post_g': 1.246592e-01, 'mix_pre_g': 1.970347e-01, 'w_in': 6.927675e-02, 'w_fu': 1.153636e-02, 'b_f': 5.223051e-02, 'gla_norm_g': 7.014906e-02, 'w_pool': 1.458527e-01, 'pool_scale': 1.460683e-01, 'mem_norm_g': 1.847256e-02, 'w_mem_kv': 1.751976e-02, 'w_up_gla': 7.028155e-02, 'w_up_pool': 1.033809e-01, 'w_up_xattn': 1.260715e-02, 'w_o': 1.247919e-01, 'mix_post_g': 3.479560e-01, 'ffn2_pre_g': 8.203759e-02, 'ffn2_w_in': 3.435606e-02, 'ffn2_w_out': 5.597830e-02, 'ffn2_post_g': 9.724801e-02, 'final_g': 3.202101e+01}


def _to_microbatches(a, axis):
    t = _jnp.moveaxis(a, axis, 0)
    t = t.reshape((N_MICROBATCH, t.shape[0] // N_MICROBATCH) + t.shape[1:])
    return _jnp.moveaxis(t, 1, axis + 1)


def setup_inputs(seed: int = 0) -> dict:
    inp = _fwd_setup_inputs(seed)
    key = _jax.random.fold_in(_jax.random.key(seed), 7919)
    shape, _ = _output_shape()
    out = dict(inp)
    out["loss_target"] = _jax.random.normal(_jax.random.fold_in(key, 0), shape, _jnp.float32)
    for i, name in enumerate(TWIN_WEIGHTS):
        w = inp[name].astype(_jnp.float32)
        if MOMENT_SCALE is None:
            s = _jnp.sqrt(_jnp.mean(_jnp.square(w)) + 1e-30)
        else:
            s = MOMENT_SCALE[name]
        km, kv = _jax.random.split(_jax.random.fold_in(key, i + 1))
        out[name] = w
        out["m_" + name] = s * _jax.random.normal(km, w.shape, _jnp.float32)
        out["v_" + name] = (s * s) * _jax.random.uniform(kv, w.shape, _jnp.float32, 0.5, 1.5)
    if N_MICROBATCH > 1:
        for name, axis in PER_EXAMPLE_BATCH_AXIS.items():
            out[name] = _to_microbatches(out[name], axis)
    return {'x': out['x'], 'mem': out['mem'], 'ffn1_pre_g': out['ffn1_pre_g'], 'ffn1_w_in': out['ffn1_w_in'], 'ffn1_w_out': out['ffn1_w_out'], 'ffn1_post_g': out['ffn1_post_g'], 'mix_pre_g': out['mix_pre_g'], 'w_in': out['w_in'], 'w_fu': out['w_fu'], 'b_f': out['b_f'], 'gla_norm_g': out['gla_norm_g'], 'w_pool': out['w_pool'], 'pool_scale': out['pool_scale'], 'mem_norm_g': out['mem_norm_g'], 'w_mem_kv': out['w_mem_kv'], 'w_up_gla': out['w_up_gla'], 'w_up_pool': out['w_up_pool'], 'w_up_xattn': out['w_up_xattn'], 'w_o': out['w_o'], 'mix_post_g': out['mix_post_g'], 'ffn2_pre_g': out['ffn2_pre_g'], 'ffn2_w_in': out['ffn2_w_in'], 'ffn2_w_out': out['ffn2_w_out'], 'ffn2_post_g': out['ffn2_post_g'], 'final_g': out['final_g'], 'loss_target': out['loss_target'], 'm_ffn1_pre_g': out['m_ffn1_pre_g'], 'm_ffn1_w_in': out['m_ffn1_w_in'], 'm_ffn1_w_out': out['m_ffn1_w_out'], 'm_ffn1_post_g': out['m_ffn1_post_g'], 'm_mix_pre_g': out['m_mix_pre_g'], 'm_w_in': out['m_w_in'], 'm_w_fu': out['m_w_fu'], 'm_b_f': out['m_b_f'], 'm_gla_norm_g': out['m_gla_norm_g'], 'm_w_pool': out['m_w_pool'], 'm_pool_scale': out['m_pool_scale'], 'm_mem_norm_g': out['m_mem_norm_g'], 'm_w_mem_kv': out['m_w_mem_kv'], 'm_w_up_gla': out['m_w_up_gla'], 'm_w_up_pool': out['m_w_up_pool'], 'm_w_up_xattn': out['m_w_up_xattn'], 'm_w_o': out['m_w_o'], 'm_mix_post_g': out['m_mix_post_g'], 'm_ffn2_pre_g': out['m_ffn2_pre_g'], 'm_ffn2_w_in': out['m_ffn2_w_in'], 'm_ffn2_w_out': out['m_ffn2_w_out'], 'm_ffn2_post_g': out['m_ffn2_post_g'], 'm_final_g': out['m_final_g'], 'v_ffn1_pre_g': out['v_ffn1_pre_g'], 'v_ffn1_w_in': out['v_ffn1_w_in'], 'v_ffn1_w_out': out['v_ffn1_w_out'], 'v_ffn1_post_g': out['v_ffn1_post_g'], 'v_mix_pre_g': out['v_mix_pre_g'], 'v_w_in': out['v_w_in'], 'v_w_fu': out['v_w_fu'], 'v_b_f': out['v_b_f'], 'v_gla_norm_g': out['v_gla_norm_g'], 'v_w_pool': out['v_w_pool'], 'v_pool_scale': out['v_pool_scale'], 'v_mem_norm_g': out['v_mem_norm_g'], 'v_w_mem_kv': out['v_w_mem_kv'], 'v_w_up_gla': out['v_w_up_gla'], 'v_w_up_pool': out['v_w_up_pool'], 'v_w_up_xattn': out['v_w_up_xattn'], 'v_w_o': out['v_w_o'], 'v_mix_post_g': out['v_mix_post_g'], 'v_ffn2_pre_g': out['v_ffn2_pre_g'], 'v_ffn2_w_in': out['v_ffn2_w_in'], 'v_ffn2_w_out': out['v_ffn2_w_out'], 'v_ffn2_post_g': out['v_ffn2_post_g'], 'v_final_g': out['v_final_g']}


def _loss(weights, diff, rest, loss_target):
    with _jax.named_scope("forward"):
        args = {**rest, TWIN_DIFF_INPUT: diff, **{k: w.astype(_WEIGHT_DTYPES[k]) for k, w in weights.items()}}
        y = _forward(args)
    with _jax.named_scope("loss_head"):
        err = _jnp.square(y.astype(_jnp.float32) - loss_target)
        return 0.5 * _jnp.sum(_jnp.mean(err, axis=-1)) if err.ndim else 0.5 * err


def _adamw(w, g, m, v):
    m = ADAM_B1 * m + (1.0 - ADAM_B1) * g
    v = ADAM_B2 * v + (1.0 - ADAM_B2) * _jnp.square(g)
    m_hat = m / (1.0 - ADAM_B1 ** ADAM_STEP)
    v_hat = v / (1.0 - ADAM_B2 ** ADAM_STEP)
    delta = -ADAM_LR * (m_hat / (_jnp.sqrt(v_hat) + ADAM_EPS) + ADAM_WD * w)
    return delta, m, v


def reference(x, mem, ffn1_pre_g, ffn1_w_in, ffn1_w_out, ffn1_post_g, mix_pre_g, w_in, w_fu, b_f, gla_norm_g, w_pool, pool_scale, mem_norm_g, w_mem_kv, w_up_gla, w_up_pool, w_up_xattn, w_o, mix_post_g, ffn2_pre_g, ffn2_w_in, ffn2_w_out, ffn2_post_g, final_g, loss_target, m_ffn1_pre_g, m_ffn1_w_in, m_ffn1_w_out, m_ffn1_post_g, m_mix_pre_g, m_w_in, m_w_fu, m_b_f, m_gla_norm_g, m_w_pool, m_pool_scale, m_mem_norm_g, m_w_mem_kv, m_w_up_gla, m_w_up_pool, m_w_up_xattn, m_w_o, m_mix_post_g, m_ffn2_pre_g, m_ffn2_w_in, m_ffn2_w_out, m_ffn2_post_g, m_final_g, v_ffn1_pre_g, v_ffn1_w_in, v_ffn1_w_out, v_ffn1_post_g, v_mix_pre_g, v_w_in, v_w_fu, v_b_f, v_gla_norm_g, v_w_pool, v_pool_scale, v_mem_norm_g, v_w_mem_kv, v_w_up_gla, v_w_up_pool, v_w_up_xattn, v_w_o, v_mix_post_g, v_ffn2_pre_g, v_ffn2_w_in, v_ffn2_w_out, v_ffn2_post_g, v_final_g):
    given = dict(x=x, mem=mem, ffn1_pre_g=ffn1_pre_g, ffn1_w_in=ffn1_w_in, ffn1_w_out=ffn1_w_out, ffn1_post_g=ffn1_post_g, mix_pre_g=mix_pre_g, w_in=w_in, w_fu=w_fu, b_f=b_f, gla_norm_g=gla_norm_g, w_pool=w_pool, pool_scale=pool_scale, mem_norm_g=mem_norm_g, w_mem_kv=w_mem_kv, w_up_gla=w_up_gla, w_up_pool=w_up_pool, w_up_xattn=w_up_xattn, w_o=w_o, mix_post_g=mix_post_g, ffn2_pre_g=ffn2_pre_g, ffn2_w_in=ffn2_w_in, ffn2_w_out=ffn2_w_out, ffn2_post_g=ffn2_post_g, final_g=final_g, loss_target=loss_target, m_ffn1_pre_g=m_ffn1_pre_g, m_ffn1_w_in=m_ffn1_w_in, m_ffn1_w_out=m_ffn1_w_out, m_ffn1_post_g=m_ffn1_post_g, m_mix_pre_g=m_mix_pre_g, m_w_in=m_w_in, m_w_fu=m_w_fu, m_b_f=m_b_f, m_gla_norm_g=m_gla_norm_g, m_w_pool=m_w_pool, m_pool_scale=m_pool_scale, m_mem_norm_g=m_mem_norm_g, m_w_mem_kv=m_w_mem_kv, m_w_up_gla=m_w_up_gla, m_w_up_pool=m_w_up_pool, m_w_up_xattn=m_w_up_xattn, m_w_o=m_w_o, m_mix_post_g=m_mix_post_g, m_ffn2_pre_g=m_ffn2_pre_g, m_ffn2_w_in=m_ffn2_w_in, m_ffn2_w_out=m_ffn2_w_out, m_ffn2_post_g=m_ffn2_post_g, m_final_g=m_final_g, v_ffn1_pre_g=v_ffn1_pre_g, v_ffn1_w_in=v_ffn1_w_in, v_ffn1_w_out=v_ffn1_w_out, v_ffn1_post_g=v_ffn1_post_g, v_mix_pre_g=v_mix_pre_g, v_w_in=v_w_in, v_w_fu=v_w_fu, v_b_f=v_b_f, v_gla_norm_g=v_gla_norm_g, v_w_pool=v_w_pool, v_pool_scale=v_pool_scale, v_mem_norm_g=v_mem_norm_g, v_w_mem_kv=v_w_mem_kv, v_w_up_gla=v_w_up_gla, v_w_up_pool=v_w_up_pool, v_w_up_xattn=v_w_up_xattn, v_w_o=v_w_o, v_mix_post_g=v_mix_post_g, v_ffn2_pre_g=v_ffn2_pre_g, v_ffn2_w_in=v_ffn2_w_in, v_ffn2_w_out=v_ffn2_w_out, v_ffn2_post_g=v_ffn2_post_g, v_final_g=v_final_g)
    weights = {n: given[n] for n in TWIN_WEIGHTS}
    shared = {n: given[n] for n in SHARED_INPUTS}
    per_example = {n: given[n] for n in ['x', 'mem']}
    grad_fn = _jax.value_and_grad(_loss, argnums=(0, 1))

    def one_microbatch(ex, loss_target):
        ex = dict(ex)
        diff = ex.pop(TWIN_DIFF_INPUT)
        return grad_fn(weights, diff, {**shared, **ex}, loss_target)

    if N_MICROBATCH == 1:
        loss, (grad_w, grad_x) = one_microbatch(per_example, given["loss_target"])
    else:
        def body(carry, xs):
            loss_sum, grad_sum = carry
            l_k, (gw_k, gx_k) = one_microbatch(xs[0], xs[1])
            with _jax.named_scope("update"):
                return (loss_sum + l_k, _jax.tree.map(_jnp.add, grad_sum, gw_k)), gx_k

        init = (_jnp.zeros((), _jnp.float32), _jax.tree.map(_jnp.zeros_like, weights))
        (loss, grad_w), grad_x = _jax.lax.scan(body, init, (per_example, given["loss_target"]))
    with _jax.named_scope("update"):
        delta_w, new_m, new_v = {}, {}, {}
        for n in TWIN_WEIGHTS:
            delta_w[n], new_m[n], new_v[n] = _adamw(weights[n], grad_w[n], given["m_" + n], given["v_" + n])
    return (loss, grad_x, *[grad_w[n] for n in TWIN_WEIGHTS], *[delta_w[n] for n in TWIN_WEIGHTS],
            *[new_m[n] for n in TWIN_WEIGHTS], *[new_v[n] for n in TWIN_WEIGHTS])
```

```python
import functools

import jax
import jax.numpy as jnp
from jax import lax
from jax.experimental import pallas as pl
from jax.experimental.pallas import tpu as pltpu

F32 = jnp.float32
BF16 = jnp.bfloat16
MESH = pl.DeviceIdType.MESH
HIGHEST = lax.Precision.HIGHEST

D = 1024
DFF = 2816
CHUNK = 64
HEADS = 4
HDK = 128
HDV = 256
GATE_TEMP = 16.0
POOL_WINDOWS = (2, 4, 8, 16)
POOL_HALO = 16
XA_HEADS = 4
XA_HD = 128
EPS = 1e-6
Q_SCALE = HDK ** -0.5
XA_SCALE = XA_HD ** -0.5
PG_Q, PG_K, PG_V, PG_G, PG_F, PG_W = 0, 512, 1024, 2048, 3072, 3200
GATE_RANK = 16
ADAM_LR, ADAM_B1, ADAM_B2, ADAM_EPS, ADAM_WD, ADAM_STEP = 0.001, 0.9, 0.999, 1e-08, 0.01, 10

VMEM_LIMIT = 48 * 1024 * 1024
LANE = 128
TS_ROW = 256
TS_GLA = 512
TS_POOL = 512
TS_XA = 512


def _params(sem):
    return pltpu.CompilerParams(dimension_semantics=sem, vmem_limit_bytes=VMEM_LIMIT)


def _tile(n, cap, unit=LANE):
    if n <= cap:
        return n
    best = None
    for t in range(unit, cap + 1, unit):
        if n % t == 0:
            best = t
    assert best is not None, (n, cap)
    return best


def _sigmoid(x):
    return 1.0 / (1.0 + jnp.exp(-x))


def _log_sigmoid(x):
    return jnp.minimum(x, 0.0) - jnp.log(1.0 + jnp.exp(-jnp.abs(x)))


def _rms(x):
    r = lax.rsqrt(jnp.mean(x * x, axis=-1, keepdims=True) + EPS)
    return x * r, r


def _rows(ts, w):
    return pl.BlockSpec((ts, w), lambda i: (i, 0))


def _fixed(shape):
    nd = len(shape)
    return pl.BlockSpec(shape, lambda i: (0,) * nd)


def _mm(a, b, *, ta=False, tb=False, out_dtype=F32, tm=512, tn=1024, tk=1024, shards=1, name):
    m, kdim = (a.shape[1], a.shape[0]) if ta else a.shape
    n = b.shape[0] if tb else b.shape[1]
    assert (b.shape[1] if tb else b.shape[0]) == kdim, (a.shape, b.shape, ta, tb)
    tm = _tile(m, tm)
    tn = n // shards if shards > 1 else _tile(n, tn)
    tk = _tile(kdim, tk)
    nk = kdim // tk
    dims = (((0 if ta else 1,), (1 if tb else 0,)), ((), ()))

    def body(a_ref, b_ref, o_ref, *acc):
        part = lax.dot_general(a_ref[...], b_ref[...], dims, preferred_element_type=F32)
        if nk == 1:
            o_ref[...] = part.astype(o_ref.dtype)
            return
        acc_ref, = acc
        k = pl.program_id(2)

        @pl.when(k == 0)
        def _():
            acc_ref[...] = part

        @pl.when(k > 0)
        def _():
            acc_ref[...] += part

        @pl.when(k == nk - 1)
        def _():
            o_ref[...] = acc_ref[...].astype(o_ref.dtype)

    a_spec = pl.BlockSpec((tk, tm), lambda i, j, k: (k, i)) if ta else pl.BlockSpec((tm, tk), lambda i, j, k: (i, k))
    b_spec = pl.BlockSpec((tn, tk), lambda i, j, k: (j, k)) if tb else pl.BlockSpec((tk, tn), lambda i, j, k: (k, j))
    if shards > 1:
        out_shape = jax.ShapeDtypeStruct((shards, m, tn), out_dtype)
        o_spec = pl.BlockSpec((None, tm, tn), lambda i, j, k: (j, i, 0))
    else:
        out_shape = jax.ShapeDtypeStruct((m, n), out_dtype)
        o_spec = pl.BlockSpec((tm, tn), lambda i, j, k: (i, j))
    return pl.pallas_call(
        body, grid=(m // tm, n // tn, nk), in_specs=[a_spec, b_spec], out_specs=o_spec, out_shape=out_shape,
        scratch_shapes=[pltpu.VMEM((tm, tn), F32)] if nk > 1 else [],
        compiler_params=_params(("parallel", "parallel", "arbitrary")), name=name,
    )(a, b)


def _norm_fwd(x, g, out_dtype, name):
    s, d = x.shape
    ts = _tile(s, TS_ROW, 8)

    def body(x_ref, g_ref, o_ref):
        xh, _ = _rms(x_ref[...])
        o_ref[...] = (xh * g_ref[...]).astype(o_ref.dtype)

    return pl.pallas_call(
        body, grid=(s // ts,), in_specs=[_rows(ts, d), _fixed((1, d))], out_specs=_rows(ts, d),
        out_shape=jax.ShapeDtypeStruct((s, d), out_dtype), compiler_params=_params(("parallel",)), name=name,
    )(x, g)


def _resid_norm_fwd(x, f, g_post, alpha, g_next, name):
    s, d = x.shape
    ts = _tile(s, TS_ROW, 8)
    with_h = g_next is not None

    def body(x_ref, f_ref, gp_ref, *rest):
        fh, _ = _rms(f_ref[...])
        xn = x_ref[...] + alpha * (fh * gp_ref[...])
        if with_h:
            gn_ref, xo_ref, h_ref = rest
            xh, _ = _rms(xn)
            h_ref[...] = (xh * gn_ref[...]).astype(h_ref.dtype)
        else:
            xo_ref, = rest
        xo_ref[...] = xn

    ins = [x, f, g_post] + ([g_next] if with_h else [])
    in_specs = [_rows(ts, d), _rows(ts, d), _fixed((1, d))] + ([_fixed((1, d))] if with_h else [])
    out_shape = [jax.ShapeDtypeStruct((s, d), F32)] + ([jax.ShapeDtypeStruct((s, d), BF16)] if with_h else [])
    out_specs = [_rows(ts, d)] + ([_rows(ts, d)] if with_h else [])
    out = pl.pallas_call(
        body, grid=(s // ts,), in_specs=in_specs, out_specs=out_specs, out_shape=out_shape,
        compiler_params=_params(("parallel",)), name=name,
    )(*ins)
    return (out[0], out[1]) if with_h else (out[0], None)


def _rms_bwd(x, g, dys, dres, alpha, out_dtype, name):
    s, d = x.shape
    ts = _tile(s, TS_ROW, 8)
    ndy = len(dys)
    with_res = dres is not None

    def body(x_ref, g_ref, *rest):
        dy_refs = rest[:ndy]
        rest = rest[ndy:]
        if with_res:
            dres_ref, dx_ref, dg_ref = rest
        else:
            dx_ref, dg_ref = rest
        xh, r = _rms(x_ref[...])
        dy = dy_refs[0][...].astype(F32)
        for ref in dy_refs[1:]:
            dy = dy + ref[...].astype(F32)
        dy = dy * alpha

        @pl.when(pl.program_id(0) == 0)
        def _():
            dg_ref[...] = jnp.zeros_like(dg_ref)

        dg_ref[...] += jnp.sum(dy * xh, axis=0, keepdims=True)
        dyg = dy * g_ref[...]
        dx = r * (dyg - xh * jnp.mean(dyg * xh, axis=-1, keepdims=True))
        if with_res:
            dx = dx + dres_ref[...]
        dx_ref[...] = dx.astype(dx_ref.dtype)

    ins = [x, g] + list(dys) + ([dres] if with_res else [])
    in_specs = [_rows(ts, d), _fixed((1, d))] + [_rows(ts, d)] * (ndy + int(with_res))
    return pl.pallas_call(
        body, grid=(s // ts,), in_specs=in_specs, out_specs=[_rows(ts, d), _fixed((1, d))],
        out_shape=[jax.ShapeDtypeStruct((s, d), out_dtype), jax.ShapeDtypeStruct((1, d), F32)],
        compiler_params=_params(("arbitrary",)), name=name,
    )(*ins)


def _loss_bwd(x, g, target, name):
    s, d = x.shape
    ts = _tile(s, TS_ROW, 8)

    def body(x_ref, g_ref, t_ref, dx_ref, dg_ref, loss_ref):
        xh, r = _rms(x_ref[...])
        gv = g_ref[...]
        diff = xh * gv - t_ref[...]

        @pl.when(pl.program_id(0) == 0)
        def _():
            dg_ref[...] = jnp.zeros_like(dg_ref)
            loss_ref[...] = jnp.zeros_like(loss_ref)

        sq = jnp.sum(diff * diff, axis=1, keepdims=True)
        loss_ref[...] += (0.5 / d) * jnp.sum(sq, axis=0, keepdims=True)
        dy = diff * (1.0 / d)
        dg_ref[...] += jnp.sum(dy * xh, axis=0, keepdims=True)
        dyg = dy * gv
        dx_ref[...] = r * (dyg - xh * jnp.mean(dyg * xh, axis=-1, keepdims=True))

    return pl.pallas_call(
        body, grid=(s // ts,), in_specs=[_rows(ts, d), _fixed((1, d)), _rows(ts, d)],
        out_specs=[_rows(ts, d), _fixed((1, d)), _fixed((8, LANE))],
        out_shape=[jax.ShapeDtypeStruct((s, d), F32), jax.ShapeDtypeStruct((1, d), F32), jax.ShapeDtypeStruct((8, LANE), F32)],
        compiler_params=_params(("arbitrary",)), name=name,
    )(x, g, target)


def _swiglu_fwd(ab, name):
    s = ab.shape[0]
    ts = _tile(s, TS_ROW, 8)

    def body(a_ref, b_ref, u_ref):
        a = a_ref[...].astype(F32)
        u_ref[...] = (a * _sigmoid(a) * b_ref[...].astype(F32)).astype(u_ref.dtype)

    return pl.pallas_call(
        body, grid=(s // ts,),
        in_specs=[pl.BlockSpec((ts, DFF), lambda i: (i, 0)), pl.BlockSpec((ts, DFF), lambda i: (i, 1))],
        out_specs=_rows(ts, DFF), out_shape=jax.ShapeDtypeStruct((s, DFF), BF16),
        compiler_params=_params(("parallel",)), name=name,
    )(ab, ab)


def _swiglu_bwd(ab, du, name):
    s = ab.shape[0]
    ts = _tile(s, TS_ROW, 8)

    def body(a_ref, b_ref, du_ref, dab_ref):
        a = a_ref[...].astype(F32)
        b = b_ref[...].astype(F32)
        dy = du_ref[...].astype(F32)
        sig = _sigmoid(a)
        dab_ref[:, 0:DFF] = (dy * b * (sig * (1.0 + a * (1.0 - sig)))).astype(dab_ref.dtype)
        dab_ref[:, DFF:2 * DFF] = (dy * a * sig).astype(dab_ref.dtype)

    return pl.pallas_call(
        body, grid=(s // ts,),
        in_specs=[pl.BlockSpec((ts, DFF), lambda i: (i, 0)), pl.BlockSpec((ts, DFF), lambda i: (i, 1)), _rows(ts, DFF)],
        out_specs=_rows(ts, 2 * DFF), out_shape=jax.ShapeDtypeStruct((s, 2 * DFF), BF16),
        compiler_params=_params(("parallel",)), name=name,
    )(ab, ab, du)


def _tri(strict):
    r = lax.broadcasted_iota(jnp.int32, (CHUNK, CHUNK), 0)
    c = lax.broadcasted_iota(jnp.int32, (CHUNK, CHUNK), 1)
    return (r > c).astype(F32) if strict else (r >= c).astype(F32)


def _gla_fwd(pg, wfu, b_f, gnorm, name):
    s = pg.shape[0]
    ts = _tile(s, TS_GLA, CHUNK)
    cpb = ts // CHUNK
    nc = s // CHUNK

    def body(pg_ref, wfu_ref, bf_ref, gn_ref, ya_ref, sp_ref, st_ref, la_ref):
        @pl.when(pl.program_id(0) == 0)
        def _():
            st_ref[...] = jnp.zeros_like(st_ref)

        f = jnp.dot(pg_ref[:, PG_F:PG_W], wfu_ref[...], preferred_element_type=F32) + bf_ref[...]
        la_ref[...] = _log_sigmoid(f) * (1.0 / GATE_TEMP)
        tri = _tri(False)

        def chunk(ci, carry):
            rows = pl.ds(pl.multiple_of(ci * CHUNK, CHUNK), CHUNK)
            b = jnp.dot(tri, la_ref[rows, :], precision=HIGHEST, preferred_element_type=F32)
            bend = jnp.sum(la_ref[rows, :], axis=0, keepdims=True)
            e = jnp.exp(bend - b)
            dec = jnp.exp(bend)
            for hd in range(HEADS):
                kc = slice(hd * HDK, (hd + 1) * HDK)
                vc = slice(hd * HDV, (hd + 1) * HDV)
                q = pg_ref[rows, PG_Q + hd * HDK:PG_Q + (hd + 1) * HDK]
                k = pg_ref[rows, PG_K + hd * HDK:PG_K + (hd + 1) * HDK]
                v = pg_ref[rows, PG_V + hd * HDV:PG_V + (hd + 1) * HDV]
                go = pg_ref[rows, PG_G + hd * HDV:PG_G + (hd + 1) * HDV].astype(F32)
                kt = (k.astype(F32) * e[:, kc]).astype(BF16)
                prev = st_ref[hd]
                sp_ref[ci, hd] = prev
                st = prev * dec[:, kc] + lax.dot_general(v, kt, (((0,), (0,)), ((), ())), preferred_element_type=F32)
                st_ref[hd] = st
                qs = (q.astype(F32) * Q_SCALE).astype(BF16)
                o = lax.dot_general(qs, st.astype(BF16), (((1,), (1,)), ((), ())), preferred_element_type=F32)
                oh, _ = _rms(o)
                ya_ref[rows, vc] = (oh * gn_ref[:, vc] * (go * _sigmoid(go))).astype(ya_ref.dtype)
            return carry

        lax.fori_loop(0, cpb, chunk, 0)

    return pl.pallas_call(
        body, grid=(s // ts,),
        in_specs=[_rows(ts, PG_W), _fixed((LANE, HEADS * HDK)), _fixed((1, HEADS * HDK)), _fixed((1, HEADS * HDV))],
        out_specs=[_rows(ts, HEADS * HDV), pl.BlockSpec((cpb, HEADS, HDV, HDK), lambda i: (i, 0, 0, 0))],
        out_shape=[jax.ShapeDtypeStruct((s, HEADS * HDV), BF16), jax.ShapeDtypeStruct((nc, HEADS, HDV, HDK), F32)],
        scratch_shapes=[pltpu.VMEM((HEADS, HDV, HDK), F32), pltpu.VMEM((ts, HEADS * HDK), F32)],
        compiler_params=_params(("arbitrary",)), name=name,
    )(pg, wfu, b_f, gnorm)


def _gla_bwd(pg, sp, dya, wfu, b_f, gnorm, name):
    s = pg.shape[0]
    ts = _tile(s, TS_GLA, CHUNK)
    cpb = ts // CHUNK
    nblk = s // ts

    def body(pg_ref, sp_ref, dya_ref, wfu_ref, bf_ref, gn_ref, dpg_ref, dwfu_ref, dbf_ref, dgn_ref, dst_ref, la_ref, sg_ref, df_ref):
        @pl.when(pl.program_id(0) == 0)
        def _():
            dst_ref[...] = jnp.zeros_like(dst_ref)
            dwfu_ref[...] = jnp.zeros_like(dwfu_ref)
            dbf_ref[...] = jnp.zeros_like(dbf_ref)
            dgn_ref[...] = jnp.zeros_like(dgn_ref)

        flow = pg_ref[:, PG_F:PG_W]
        f = jnp.dot(flow, wfu_ref[...], preferred_element_type=F32) + bf_ref[...]
        la_ref[...] = _log_sigmoid(f) * (1.0 / GATE_TEMP)
        sg_ref[...] = _sigmoid(-f) * (1.0 / GATE_TEMP)
        tri = _tri(False)
        tri_strict = _tri(True)

        def chunk(t, carry):
            ci = cpb - 1 - t
            rows = pl.ds(pl.multiple_of(ci * CHUNK, CHUNK), CHUNK)
            b = jnp.dot(tri, la_ref[rows, :], precision=HIGHEST, preferred_element_type=F32)
            bend = jnp.sum(la_ref[rows, :], axis=0, keepdims=True)
            e = jnp.exp(bend - b)
            dec = jnp.exp(bend)
            for hd in range(HEADS):
                kc = slice(hd * HDK, (hd + 1) * HDK)
                vc = slice(hd * HDV, (hd + 1) * HDV)
                q = pg_ref[rows, PG_Q + hd * HDK:PG_Q + (hd + 1) * HDK]
                k = pg_ref[rows, PG_K + hd * HDK:PG_K + (hd + 1) * HDK]
                v = pg_ref[rows, PG_V + hd * HDV:PG_V + (hd + 1) * HDV]
                go = pg_ref[rows, PG_G + hd * HDV:PG_G + (hd + 1) * HDV].astype(F32)
                eh = e[:, kc]
                dech = dec[:, kc]
                ktf = k.astype(F32) * eh
                kt = ktf.astype(BF16)
                prev = sp_ref[ci, hd]
                st = prev * dech + lax.dot_general(v, kt, (((0,), (0,)), ((), ())), preferred_element_type=F32)
                st_b = st.astype(BF16)
                qs = (q.astype(F32) * Q_SCALE).astype(BF16)
                o = lax.dot_general(qs, st_b, (((1,), (1,)), ((), ())), preferred_element_type=F32)
                oh, r = _rms(o)
                gh = gn_ref[:, vc]
                sig = _sigmoid(go)
                dy = dya_ref[rows, vc].astype(F32)
                don = dy * (go * sig)
                dgo = dy * (oh * gh) * (sig * (1.0 + go * (1.0 - sig)))
                dgn_ref[:, vc] += jnp.sum(don * oh, axis=0, keepdims=True)
                dong = don * gh
                do = (r * (dong - oh * jnp.mean(dong * oh, axis=-1, keepdims=True))).astype(BF16)
                dst = dst_ref[hd] + lax.dot_general(do, qs, (((0,), (0,)), ((), ())), preferred_element_type=F32)
                dst_b = dst.astype(BF16)
                dq = jnp.dot(do, st_b, preferred_element_type=F32) * Q_SCALE
                dkt = jnp.dot(v, dst_b, preferred_element_type=F32)
                dv = lax.dot_general(kt, dst_b, (((1,), (1,)), ((), ())), preferred_element_type=F32)
                dd = jnp.sum(dst * prev, axis=0, keepdims=True)
                dla = jnp.dot(tri_strict, dkt * ktf, precision=HIGHEST, preferred_element_type=F32) + dd * dech
                df_ref[rows, kc] = dla * sg_ref[rows, kc]
                dst_ref[hd] = dst * dech
                dpg_ref[rows, PG_Q + hd * HDK:PG_Q + (hd + 1) * HDK] = dq.astype(dpg_ref.dtype)
                dpg_ref[rows, PG_K + hd * HDK:PG_K + (hd + 1) * HDK] = (dkt * eh).astype(dpg_ref.dtype)
                dpg_ref[rows, PG_V + hd * HDV:PG_V + (hd + 1) * HDV] = dv.astype(dpg_ref.dtype)
                dpg_ref[rows, PG_G + hd * HDV:PG_G + (hd + 1) * HDV] = dgo.astype(dpg_ref.dtype)
            return carry

        lax.fori_loop(0, cpb, chunk, 0)
        df = df_ref[...]
        df_b = df.astype(BF16)
        dpg_ref[:, PG_F:PG_W] = lax.dot_general(df_b, wfu_ref[...], (((1,), (1,)), ((), ())), preferred_element_type=F32).astype(dpg_ref.dtype)
        dwfu_ref[...] += lax.dot_general(flow, df_b, (((0,), (0,)), ((), ())), preferred_element_type=F32)
        dbf_ref[...] += jnp.sum(df, axis=0, keepdims=True)

    rev = lambda i: (nblk - 1 - i, 0)
    return pl.pallas_call(
        body, grid=(nblk,),
        in_specs=[pl.BlockSpec((ts, PG_W), rev), pl.BlockSpec((cpb, HEADS, HDV, HDK), lambda i: (nblk - 1 - i, 0, 0, 0)),
                  pl.BlockSpec((ts, HEADS * HDV), rev), _fixed((LANE, HEADS * HDK)), _fixed((1, HEADS * HDK)), _fixed((1, HEADS * HDV))],
        out_specs=[pl.BlockSpec((ts, PG_W), rev), _fixed((LANE, HEADS * HDK)), _fixed((1, HEADS * HDK)), _fixed((1, HEADS * HDV))],
        out_shape=[jax.ShapeDtypeStruct((s, PG_W), BF16), jax.ShapeDtypeStruct((LANE, HEADS * HDK), F32),
                   jax.ShapeDtypeStruct((1, HEADS * HDK), F32), jax.ShapeDtypeStruct((1, HEADS * HDV), F32)],
        scratch_shapes=[pltpu.VMEM((HEADS, HDV, HDK), F32), pltpu.VMEM((ts, HEADS * HDK), F32),
                        pltpu.VMEM((ts, HEADS * HDK), F32), pltpu.VMEM((ts, HEADS * HDK), F32)],
        compiler_params=_params(("arbitrary",)), name=name,
    )(pg, sp, dya, wfu, b_f, gnorm)


def _window_sums(ext, sign):
    n = ext.shape[0]
    sums = {1: ext}
    w = 1
    while w < POOL_WINDOWS[-1]:
        sums[2 * w] = sums[w] + pltpu.roll(sums[w], w if sign > 0 else n - w, 0)
        w *= 2
    return [sums[POOL_WINDOWS[g]][:, g * LANE:(g + 1) * LANE] for g in range(len(POOL_WINDOWS))]


def _pool_counts(row0, n):
    pos = (row0 + lax.broadcasted_iota(jnp.int32, (n, 1), 0) + 1).astype(F32)
    return [jnp.minimum(pos, float(w)) for w in POOL_WINDOWS]


def _pool_fwd(ppx, w_pool, pool_scale, name):
    s = ppx.shape[0]
    ts = _tile(s, TS_POOL, POOL_HALO)
    hb = ts // POOL_HALO
    pw = len(POOL_WINDOWS) * LANE

    def body(p_ref, halo_ref, w_ref, sc_ref, y_ref, ext_ref):
        i = pl.program_id(0)
        p = p_ref[...].astype(F32)
        ext_ref[0:POOL_HALO, :] = jnp.where(i > 0, halo_ref[...].astype(F32), 0.0)
        ext_ref[POOL_HALO:, :] = p
        sums = _window_sums(ext_ref[...], +1)
        cnt = _pool_counts(i * ts, ts)
        for g in range(len(POOL_WINDOWS)):
            cols = slice(g * LANE, (g + 1) * LANE)
            mixed = sums[g][POOL_HALO:, :] / cnt[g] - p[:, cols]
            y = jnp.dot(mixed.astype(BF16), w_ref[g], preferred_element_type=F32)
            y_ref[:, cols] = (y * sc_ref[:, cols]).astype(y_ref.dtype)

    return pl.pallas_call(
        body, grid=(s // ts,),
        in_specs=[pl.BlockSpec((ts, pw), lambda i: (i, 0)), pl.BlockSpec((POOL_HALO, pw), lambda i: (jnp.maximum(i * hb - 1, 0), 0)),
                  _fixed((len(POOL_WINDOWS), LANE, LANE)), _fixed((1, pw))],
        out_specs=_rows(ts, pw), out_shape=jax.ShapeDtypeStruct((s, pw), BF16),
        scratch_shapes=[pltpu.VMEM((ts + POOL_HALO, pw), F32)],
        compiler_params=_params(("parallel",)), name=name,
    )(ppx, ppx, w_pool, pool_scale)


def _pool_bwd(dyb, ppx, w_pool, pool_scale, name):
    s = ppx.shape[0]
    ts = _tile(s, TS_POOL, POOL_HALO)
    hb = ts // POOL_HALO
    nblk = s // ts
    last_halo = s // POOL_HALO - 1
    ng = len(POOL_WINDOWS)
    pw = ng * LANE

    def body(p_ref, halo_ref, dy_ref, dyn_ref, w_ref, sc_ref, dp_ref, dw_ref, dsc_ref, ext_ref, dext_ref, dm_ref):
        i = pl.program_id(0)

        @pl.when(i == 0)
        def _():
            dw_ref[...] = jnp.zeros_like(dw_ref)
            dsc_ref[...] = jnp.zeros_like(dsc_ref)

        p = p_ref[...].astype(F32)
        ext_ref[0:POOL_HALO, :] = jnp.where(i > 0, halo_ref[...].astype(F32), 0.0)
        ext_ref[POOL_HALO:, :] = p
        sums = _window_sums(ext_ref[...], +1)
        cnt = _pool_counts(i * ts, ts + POOL_HALO)
        sc = sc_ref[...]
        dy = dy_ref[...].astype(F32)
        dyn = jnp.where(i < nblk - 1, dyn_ref[...].astype(F32), 0.0)
        for g in range(ng):
            cols = slice(g * LANE, (g + 1) * LANE)
            wg = w_ref[g]
            mixed = (sums[g][POOL_HALO:, :] / cnt[g][0:ts] - p[:, cols]).astype(BF16)
            ypre = jnp.dot(mixed, wg, preferred_element_type=F32)
            dsc_ref[:, cols] += jnp.sum(dy[:, cols] * ypre, axis=0, keepdims=True)
            dyp = (dy[:, cols] * sc[:, cols]).astype(BF16)
            dypn = (dyn[:, cols] * sc[:, cols]).astype(BF16)
            dw_ref[g] += lax.dot_general(mixed, dyp, (((0,), (0,)), ((), ())), preferred_element_type=F32)
            dm = lax.dot_general(dyp, wg, (((1,), (1,)), ((), ())), preferred_element_type=F32)
            dmn = lax.dot_general(dypn, wg, (((1,), (1,)), ((), ())), preferred_element_type=F32)
            dext_ref[0:ts, cols] = dm / cnt[g][0:ts]
            dext_ref[ts:, cols] = dmn / cnt[g][ts:]
            dm_ref[:, cols] = dm
        lead = _window_sums(dext_ref[...], -1)
        for g in range(ng):
            cols = slice(g * LANE, (g + 1) * LANE)
            dp_ref[:, cols] = (lead[g][0:ts, :] - dm_ref[:, cols]).astype(dp_ref.dtype)

    return pl.pallas_call(
        body, grid=(nblk,),
        in_specs=[pl.BlockSpec((ts, pw), lambda i: (i, 0)), pl.BlockSpec((POOL_HALO, pw), lambda i: (jnp.maximum(i * hb - 1, 0), 0)),
                  pl.BlockSpec((ts, pw), lambda i: (i, 0)), pl.BlockSpec((POOL_HALO, pw), lambda i: (jnp.minimum((i + 1) * hb, last_halo), 0)),
                  _fixed((ng, LANE, LANE)), _fixed((1, pw))],
        out_specs=[_rows(ts, pw), _fixed((ng, LANE, LANE)), _fixed((1, pw))],
        out_shape=[jax.ShapeDtypeStruct((s, pw), BF16), jax.ShapeDtypeStruct((ng, LANE, LANE), F32), jax.ShapeDtypeStruct((1, pw), F32)],
        scratch_shapes=[pltpu.VMEM((ts + POOL_HALO, pw), F32), pltpu.VMEM((ts + POOL_HALO, pw), F32), pltpu.VMEM((ts, pw), F32)],
        compiler_params=_params(("arbitrary",)), name=name,
    )(ppx, ppx, dyb, dyb, w_pool, pool_scale)


def _xattn_fwd(ppx, kv, name):
    s = ppx.shape[0]
    m = kv.shape[0]
    ts = _tile(s, TS_XA, 8)
    xw = XA_HEADS * XA_HD

    def body(q_ref, kv_ref, o_ref):
        for hd in range(XA_HEADS):
            cols = slice(hd * XA_HD, (hd + 1) * XA_HD)
            k = kv_ref[:, hd * XA_HD:(hd + 1) * XA_HD]
            v = kv_ref[:, xw + hd * XA_HD:xw + (hd + 1) * XA_HD]
            sc = lax.dot_general(q_ref[:, cols], k, (((1,), (1,)), ((), ())), preferred_element_type=F32) * XA_SCALE
            ex = jnp.exp(sc - jnp.max(sc, axis=-1, keepdims=True))
            pr = ex / jnp.sum(ex, axis=-1, keepdims=True)
            o_ref[:, cols] = jnp.dot(pr.astype(BF16), v, preferred_element_type=F32).astype(o_ref.dtype)

    return pl.pallas_call(
        body, grid=(s // ts,), in_specs=[pl.BlockSpec((ts, xw), lambda i: (i, 1)), _fixed((m, 2 * xw))],
        out_specs=_rows(ts, xw), out_shape=jax.ShapeDtypeStruct((s, xw), BF16),
        compiler_params=_params(("parallel",)), name=name,
    )(ppx, kv)


def _xattn_bwd(dxc, ppx, kv, name):
    s = ppx.shape[0]
    m = kv.shape[0]
    ts = _tile(s, TS_XA, 8)
    xw = XA_HEADS * XA_HD

    def body(do_ref, q_ref, kv_ref, dq_ref, dkv_ref):
        @pl.when(pl.program_id(0) == 0)
        def _():
            dkv_ref[...] = jnp.zeros_like(dkv_ref)

        for hd in range(XA_HEADS):
            cols = slice(hd * XA_HD, (hd + 1) * XA_HD)
            vcols = slice(xw + hd * XA_HD, xw + (hd + 1) * XA_HD)
            q = q_ref[:, cols]
            k = kv_ref[:, cols]
            v = kv_ref[:, vcols]
            do = do_ref[:, cols]
            sc = lax.dot_general(q, k, (((1,), (1,)), ((), ())), preferred_element_type=F32) * XA_SCALE
            ex = jnp.exp(sc - jnp.max(sc, axis=-1, keepdims=True))
            pr = ex / jnp.sum(ex, axis=-1, keepdims=True)
            dpr = lax.dot_general(do, v, (((1,), (1,)), ((), ())), preferred_element_type=F32)
            dsc = (pr * (dpr - jnp.sum(dpr * pr, axis=-1, keepdims=True)) * XA_SCALE).astype(BF16)
            dq_ref[:, cols] = jnp.dot(dsc, k, preferred_element_type=F32).astype(dq_ref.dtype)
            dkv_ref[:, cols] += lax.dot_general(dsc, q, (((0,), (0,)), ((), ())), preferred_element_type=F32)
            dkv_ref[:, vcols] += lax.dot_general(pr.astype(BF16), do, (((0,), (0,)), ((), ())), preferred_element_type=F32)

    return pl.pallas_call(
        body, grid=(s // ts,), in_specs=[_rows(ts, xw), pl.BlockSpec((ts, xw), lambda i: (i, 1)), _fixed((m, 2 * xw))],
        out_specs=[_rows(ts, xw), _fixed((m, 2 * xw))],
        out_shape=[jax.ShapeDtypeStruct((s, xw), BF16), jax.ShapeDtypeStruct((m, 2 * xw), F32)],
        compiler_params=_params(("arbitrary",)), name=name,
    )(dxc, ppx, kv)


def _merge_fwd(pgt, ya, yb, yc, name):
    s = pgt.shape[0]
    ts = _tile(s, TS_ROW, 8)

    def body(gt_ref, ya_ref, yb_ref, yc_ref, o_ref):
        acc = _sigmoid(gt_ref[:, 0:D].astype(F32)) * ya_ref[...].astype(F32)
        acc = acc + _sigmoid(gt_ref[:, D:2 * D].astype(F32)) * yb_ref[...].astype(F32)
        acc = acc + _sigmoid(gt_ref[:, 2 * D:3 * D].astype(F32)) * yc_ref[...].astype(F32)
        o_ref[...] = acc.astype(o_ref.dtype)

    return pl.pallas_call(
        body, grid=(s // ts,), in_specs=[_rows(ts, 3 * D)] + [_rows(ts, D)] * 3, out_specs=_rows(ts, D),
        out_shape=jax.ShapeDtypeStruct((s, D), BF16), compiler_params=_params(("parallel",)), name=name,
    )(pgt, ya, yb, yc)


def _merge_bwd(dmerged, pgt, ya, yb, yc, name):
    s = pgt.shape[0]
    ts = _tile(s, TS_ROW, 8)

    def body(dm_ref, gt_ref, ya_ref, yb_ref, yc_ref, dya_ref, dyb_ref, dyc_ref, dgt_ref):
        dm = dm_ref[...].astype(F32)
        for j, (y_ref, dy_ref) in enumerate(((ya_ref, dya_ref), (yb_ref, dyb_ref), (yc_ref, dyc_ref))):
            sig = _sigmoid(gt_ref[:, j * D:(j + 1) * D].astype(F32))
            dy_ref[...] = (dm * sig).astype(dy_ref.dtype)
            dgt_ref[:, j * D:(j + 1) * D] = (dm * y_ref[...].astype(F32) * sig * (1.0 - sig)).astype(dgt_ref.dtype)

    return pl.pallas_call(
        body, grid=(s // ts,), in_specs=[_rows(ts, D), _rows(ts, 3 * D)] + [_rows(ts, D)] * 3,
        out_specs=[_rows(ts, D)] * 3 + [_rows(ts, 3 * D)],
        out_shape=[jax.ShapeDtypeStruct((s, D), BF16)] * 3 + [jax.ShapeDtypeStruct((s, 3 * D), BF16)],
        compiler_params=_params(("parallel",)), name=name,
    )(dmerged, pgt, ya, yb, yc)


def _adamw(w, g, m, v, name):
    r, c = w.shape
    tr = _tile(r, 256, 8)

    def body(w_ref, g_ref, m_ref, v_ref, d_ref, mo_ref, vo_ref):
        gv = g_ref[...]
        mn = ADAM_B1 * m_ref[...] + (1.0 - ADAM_B1) * gv
        vn = ADAM_B2 * v_ref[...] + (1.0 - ADAM_B2) * (gv * gv)
        m_hat = mn / (1.0 - ADAM_B1 ** ADAM_STEP)
        v_hat = vn / (1.0 - ADAM_B2 ** ADAM_STEP)
        d_ref[...] = -ADAM_LR * (m_hat / (jnp.sqrt(v_hat) + ADAM_EPS) + ADAM_WD * w_ref[...])
        mo_ref[...] = mn
        vo_ref[...] = vn

    return pl.pallas_call(
        body, grid=(r // tr,), in_specs=[_rows(tr, c)] * 4, out_specs=[_rows(tr, c)] * 3,
        out_shape=[jax.ShapeDtypeStruct((r, c), F32)] * 3, compiler_params=_params(("parallel",)), name=name,
    )(w, g, m, v)


ANY = pl.BlockSpec(memory_space=pl.ANY)


def _place():
    x, y, c = lax.axis_index("x"), lax.axis_index("y"), lax.axis_index("c")
    chips = [(1 - x, y), (x, 1 - y), (1 - x, 1 - y)]
    return x, y, c, chips


def _half(c, rows):
    h = rows // 2
    return pl.ds(pl.multiple_of(c * h, 8), h)


def _gather_shards(shards, name):
    n = len(shards)

    def body(*refs):
        ins, outs = refs[:n], refs[n:2 * n]
        send_ici, recv_ici, send_d2d, recv_d2d, local_sem = refs[2 * n:]
        x, y, c, chips = _place()
        me = 2 * x + y
        sibling = (x, y, 1 - c)

        def ici(w, p, chip_of_block, to):
            rows = _half(c, ins[w].shape[0])
            return pltpu.make_async_remote_copy(
                src_ref=ins[w].at[rows], dst_ref=outs[w].at[chip_of_block, rows],
                send_sem=send_ici.at[w, p], recv_sem=recv_ici.at[w, p], device_id=to, device_id_type=MESH)

        def d2d(w, p, chip_of_block, half_of):
            rows = _half(half_of, ins[w].shape[0])
            return pltpu.make_async_remote_copy(
                src_ref=outs[w].at[chip_of_block, rows], dst_ref=outs[w].at[chip_of_block, rows],
                send_sem=send_d2d.at[w, p], recv_sem=recv_d2d.at[w, p], device_id=sibling, device_id_type=MESH)

        own = [pltpu.make_async_copy(ins[w], outs[w].at[me], local_sem.at[w]) for w in range(n)]
        for cp in own:
            cp.start()
        sends = [ici(w, p, me, (*chip, c)) for p, chip in enumerate(chips) for w in range(n)]
        for cp in sends:
            cp.start()
        passed = []
        for p, (px, py) in enumerate(chips):
            for w in range(n):
                ici(w, p, 2 * px + py, (px, py, c)).wait_recv()
                fwd = d2d(w, p, 2 * px + py, c)
                fwd.start()
                passed.append(fwd)
        for p, (px, py) in enumerate(chips):
            for w in range(n):
                d2d(w, p, 2 * px + py, 1 - c).wait_recv()
        for cp in sends + passed:
            cp.wait_send()
        for cp in own:
            cp.wait()

    return pl.pallas_call(
        body, in_specs=[ANY] * n, out_specs=[ANY] * n,
        out_shape=[jax.ShapeDtypeStruct((4,) + a.shape, a.dtype) for a in shards],
        scratch_shapes=[pltpu.SemaphoreType.DMA((n, 3))] * 4 + [pltpu.SemaphoreType.DMA((n,))],
        compiler_params=pltpu.CompilerParams(has_side_effects=True), name=name,
    )(*shards)


def _pair_exchange(grads, name):
    n = len(grads)

    def body(*refs):
        ins, outs = refs[:n], refs[n:2 * n]
        send_sem, recv_sem = refs[2 * n:]
        x, y, c, _ = _place()
        copies = []
        for w in range(n):
            rows = _half(1 - c, ins[w].shape[1])
            copies.append(pltpu.make_async_remote_copy(
                src_ref=ins[w].at[:, rows], dst_ref=outs[w], send_sem=send_sem.at[w], recv_sem=recv_sem.at[w],
                device_id=(x, y, 1 - c), device_id_type=MESH))
        for cp in copies:
            cp.start()
        for cp in copies:
            cp.wait()

    return pl.pallas_call(
        body, in_specs=[ANY] * n, out_specs=[ANY] * n,
        out_shape=[jax.ShapeDtypeStruct((4, a.shape[1] // 2, a.shape[2]), a.dtype) for a in grads],
        scratch_shapes=[pltpu.SemaphoreType.DMA((n,))] * 2,
        compiler_params=pltpu.CompilerParams(has_side_effects=True), name=name,
    )(*grads)


def _pair_sum(g, got, c_arr, name):
    _, r, cols = g.shape
    h = r // 2
    th = _tile(h, 256, 16)
    nb = h // th

    def body(c_ref, g_ref, got_ref, o_ref):
        o_ref[...] = (g_ref[...].astype(F32) + got_ref[...].astype(F32)).astype(o_ref.dtype)

    return pl.pallas_call(
        body,
        grid_spec=pltpu.PrefetchScalarGridSpec(
            num_scalar_prefetch=1, grid=(4, nb),
            in_specs=[pl.BlockSpec((None, th, cols), lambda j, i, c_ref: (j, c_ref[0] * nb + i, 0)),
                      pl.BlockSpec((None, th, cols), lambda j, i, c_ref: (j, i, 0))],
            out_specs=pl.BlockSpec((None, th, cols), lambda j, i, c_ref: (j, i, 0))),
        out_shape=jax.ShapeDtypeStruct((4, h, cols), BF16),
        compiler_params=_params(("parallel", "parallel")), name=name,
    )(c_arr, g, got)


def _chip_exchange(parts, name):
    n = len(parts)

    def body(*refs):
        ins, outs = refs[:n], refs[n:2 * n]
        send_sem, recv_sem = refs[2 * n:]
        x, y, c, chips = _place()
        copies = []
        for p, (px, py) in enumerate(chips):
            for w in range(n):
                copies.append(pltpu.make_async_remote_copy(
                    src_ref=ins[w].at[2 * px + py], dst_ref=outs[w].at[p], send_sem=send_sem.at[w, p], recv_sem=recv_sem.at[w, p],
                    device_id=(px, py, c), device_id_type=MESH))
        for cp in copies:
            cp.start()
        for cp in copies:
            cp.wait()

    return pl.pallas_call(
        body, in_specs=[ANY] * n, out_specs=[ANY] * n,
        out_shape=[jax.ShapeDtypeStruct((3,) + a.shape[1:], a.dtype) for a in parts],
        scratch_shapes=[pltpu.SemaphoreType.DMA((n, 3))] * 2,
        compiler_params=pltpu.CompilerParams(has_side_effects=True), name=name,
    )(*parts)


def _chip_sum(part, got, me_arr, name):
    _, h, cols = part.shape
    th = _tile(h, 256, 16)

    def body(me_ref, p_ref, got_ref, o_ref):
        acc = p_ref[...].astype(F32)
        for p in range(3):
            acc = acc + got_ref[p].astype(F32)
        o_ref[...] = acc

    return pl.pallas_call(
        body,
        grid_spec=pltpu.PrefetchScalarGridSpec(
            num_scalar_prefetch=1, grid=(h // th,),
            in_specs=[pl.BlockSpec((None, th, cols), lambda i, me_ref: (me_ref[0], i, 0)),
                      pl.BlockSpec((3, th, cols), lambda i, me_ref: (0, i, 0))],
            out_specs=pl.BlockSpec((th, cols), lambda i, me_ref: (i, 0))),
        out_shape=jax.ShapeDtypeStruct((h, cols), F32),
        compiler_params=_params(("parallel",)), name=name,
    )(me_arr, part, got)


def _pair_join(halves, name):
    n = len(halves)

    def body(*refs):
        ins, outs = refs[:n], refs[n:2 * n]
        send_sem, recv_sem, local_sem = refs[2 * n:]
        x, y, c, _ = _place()
        remote, local = [], []
        for w in range(n):
            rows = _half(c, 2 * ins[w].shape[0])
            local.append(pltpu.make_async_copy(ins[w], outs[w].at[rows], local_sem.at[w]))
            remote.append(pltpu.make_async_remote_copy(
                src_ref=ins[w], dst_ref=outs[w].at[rows], send_sem=send_sem.at[w], recv_sem=recv_sem.at[w],
                device_id=(x, y, 1 - c), device_id_type=MESH))
        for cp in local + remote:
            cp.start()
        for cp in remote:
            cp.wait()
        for cp in local:
            cp.wait()

    return pl.pallas_call(
        body, in_specs=[ANY] * n, out_specs=[ANY] * n,
        out_shape=[jax.ShapeDtypeStruct((2 * a.shape[0], a.shape[1]), a.dtype) for a in halves],
        scratch_shapes=[pltpu.SemaphoreType.DMA((n,))] * 3,
        compiler_params=pltpu.CompilerParams(has_side_effects=True), name=name,
    )(*halves)


def _all_sum(pack, name):
    r, cols = pack.shape

    def body(x_ref, o_ref, all_ref, send_sems, recv_sems):
        x, y, c, chips = _place()
        me, sibling = (x, y, c), (x, y, 1 - c)

        def slot(px, py, pc):
            return all_ref.at[4 * px + 2 * py + pc]

        def copy(k, block, to, src=None):
            return pltpu.make_async_remote_copy(
                src_ref=slot(*block) if src is None else src, dst_ref=slot(*block),
                send_sem=send_sems.at[k], recv_sem=recv_sems.at[k], device_id=to, device_id_type=MESH)

        all_ref[4 * x + 2 * y + c] = x_ref[...]
        first = [copy(0, me, sibling, src=x_ref)]
        first += [copy(1 + j, me, (*chip, c), src=x_ref) for j, chip in enumerate(chips)]
        for cp in first:
            cp.start()
        passed = [copy(4 + j, (*chip, c), sibling) for j, chip in enumerate(chips)]
        for j, chip in enumerate(chips):
            copy(1 + j, (*chip, c), me).wait_recv()
            passed[j].start()
        copy(0, sibling, me).wait_recv()
        for j, chip in enumerate(chips):
            copy(4 + j, (*chip, 1 - c), me).wait_recv()
        for cp in first + passed:
            cp.wait_send()
        acc = all_ref[0]
        for k in range(1, 8):
            acc = acc + all_ref[k]
        o_ref[...] = acc

    return pl.pallas_call(
        body, in_specs=[pl.BlockSpec(memory_space=pltpu.VMEM)], out_specs=pl.BlockSpec(memory_space=pltpu.VMEM),
        out_shape=jax.ShapeDtypeStruct((r, cols), F32),
        scratch_shapes=[pltpu.VMEM((8, r, cols), F32), pltpu.SemaphoreType.DMA((7,)), pltpu.SemaphoreType.DMA((7,))],
        compiler_params=pltpu.CompilerParams(has_side_effects=True, vmem_limit_bytes=VMEM_LIMIT), name=name,
    )(pack)


def _ffn_fwd(x_norm, w_in, w_out, tag):
    ab = _mm(x_norm, w_in, out_dtype=BF16, tn=1408, name=tag + "_in")
    u = _swiglu_fwd(ab, name=tag + "_swiglu")
    f = _mm(u, w_out, tk=DFF, name=tag + "_out")
    return ab, u, f


def _ffn_bwd(dz, x_norm, ab, u, w_in, w_out, tag):
    du = _mm(dz, w_out, tb=True, out_dtype=BF16, tn=1408, name=tag + "_out_dx")
    dw_out = _mm(u, dz, ta=True, out_dtype=BF16, tm=1408, tk=512, name=tag + "_out_dw")
    dab = _swiglu_bwd(ab, du, name=tag + "_swiglu_bwd")
    dxn = _mm(dab, w_in, tb=True, tk=1408, name=tag + "_in_dx")
    dw_in = _mm(x_norm, dab, ta=True, out_dtype=BF16, tm=1024, tk=512, shards=4, name=tag + "_in_dw")
    return dxn, dw_in, dw_out


def _local_step(x, mem, target, small, big):
    h1 = _norm_fwd(x, small["ffn1_pre_g"], BF16, name="ffn1_pre")
    ab1, u1, f1 = _ffn_fwd(h1, big["ffn1_w_in"], big["ffn1_w_out"], "ffn1")
    x1, h = _resid_norm_fwd(x, f1, small["ffn1_post_g"], 0.5, small["mix_pre_g"], name="ffn1_post")
    pg = _mm(h, big["w_gla"], out_dtype=BF16, tn=PG_W, name="mix_in_gla")
    ppx = _mm(h, big["w_px"], out_dtype=BF16, name="mix_in_px")
    pgt = _mm(h, big["w_gates"], out_dtype=BF16, name="mix_in_gates")
    mem_n = _norm_fwd(mem, small["mem_norm_g"], BF16, name="mem_norm")
    kv = _mm(mem_n, big["w_mem_kv"], out_dtype=BF16, name="mem_kv")
    ya_in, sp = _gla_fwd(pg, small["w_fu_pad"], small["b_f"], small["gla_norm_g"], name="gla_fwd")
    yb_in = _pool_fwd(ppx, small["w_pool_b"], small["pool_scale"], name="pool_fwd")
    xc = _xattn_fwd(ppx, kv, name="xattn_fwd")
    ya = _mm(ya_in, big["w_up_gla"], out_dtype=BF16, name="up_gla")
    yb = _mm(yb_in, big["w_up_pool"], out_dtype=BF16, name="up_pool")
    yc = _mm(xc, big["w_up_xattn"], out_dtype=BF16, name="up_xattn")
    merged = _merge_fwd(pgt, ya, yb, yc, name="merge_fwd")
    ymix = _mm(merged, big["w_o"], name="mix_out")
    x2, h2 = _resid_norm_fwd(x1, ymix, small["mix_post_g"], 1.0, small["ffn2_pre_g"], name="mix_post")
    ab2, u2, f2 = _ffn_fwd(h2, big["ffn2_w_in"], big["ffn2_w_out"], "ffn2")
    x3, _ = _resid_norm_fwd(x2, f2, small["ffn2_post_g"], 0.5, None, name="ffn2_post")
    gs, gb = {}, {}
    dx3, gs["final_g"], loss = _loss_bwd(x3, small["final_g"], target, name="loss")
    dz2, gs["ffn2_post_g"] = _rms_bwd(f2, small["ffn2_post_g"], [dx3], None, 0.5, BF16, name="ffn2_post_bwd")
    dh2, gb["ffn2_w_in"], gb["ffn2_w_out"] = _ffn_bwd(dz2, h2, ab2, u2, big["ffn2_w_in"], big["ffn2_w_out"], "ffn2")
    dx2, gs["ffn2_pre_g"] = _rms_bwd(x2, small["ffn2_pre_g"], [dh2], dx3, 1.0, F32, name="ffn2_pre_bwd")
    dy, gs["mix_post_g"] = _rms_bwd(ymix, small["mix_post_g"], [dx2], None, 1.0, BF16, name="mix_post_bwd")
    dmerged = _mm(dy, big["w_o"], tb=True, out_dtype=BF16, name="mix_out_dx")
    gb["w_o"] = _mm(merged, dy, ta=True, out_dtype=BF16, tk=512, name="mix_out_dw")
    dya, dyb, dyc, dgt = _merge_bwd(dmerged, pgt, ya, yb, yc, name="merge_bwd")
    dya_in = _mm(dya, big["w_up_gla"], tb=True, out_dtype=BF16, name="up_gla_dx")
    gb["w_up_gla"] = _mm(ya_in, dya, ta=True, out_dtype=BF16, tk=512, name="up_gla_dw")
    dyb_in = _mm(dyb, big["w_up_pool"], tb=True, out_dtype=BF16, name="up_pool_dx")
    gb["w_up_pool"] = _mm(yb_in, dyb, ta=True, out_dtype=BF16, tk=512, shards=4, name="up_pool_dw")
    dxc = _mm(dyc, big["w_up_xattn"], tb=True, out_dtype=BF16, name="up_xattn_dx")
    gb["w_up_xattn"] = _mm(xc, dyc, ta=True, out_dtype=BF16, tk=512, shards=4, name="up_xattn_dw")
    dpg, gs["w_fu_pad"], gs["b_f"], gs["gla_norm_g"] = _gla_bwd(pg, sp, dya_in, small["w_fu_pad"], small["b_f"], small["gla_norm_g"], name="gla_bwd")
    dp, gs["w_pool"], gs["pool_scale"] = _pool_bwd(dyb_in, ppx, small["w_pool_b"], small["pool_scale"], name="pool_bwd")
    dxq, dkv = _xattn_bwd(dxc, ppx, kv, name="xattn_bwd")
    dkv = dkv.astype(BF16)
    gb["w_mem_kv"] = _mm(mem_n, dkv, ta=True, out_dtype=BF16, name="mem_kv_dw")
    dmem_n = _mm(dkv, big["w_mem_kv"], tb=True, name="mem_kv_dx")
    _, gs["mem_norm_g"] = _rms_bwd(mem, small["mem_norm_g"], [dmem_n], None, 1.0, BF16, name="mem_norm_bwd")
    dh_parts = [
        _mm(dpg, big["w_gla"], tb=True, tk=PG_W, name="mix_in_gla_dx"),
        _mm(dp, big["w_p"], tb=True, name="mix_in_p_dx"),
        _mm(dxq, big["w_xq"], tb=True, name="mix_in_xq_dx"),
        _mm(dgt, big["w_gates"], tb=True, tk=1536, name="mix_in_gates_dx"),
    ]
    gb["w_gla"] = _mm(h, dpg, ta=True, out_dtype=BF16, tk=512, tn=PG_W, name="mix_in_gla_dw")
    gb["w_p"] = _mm(h, dp, ta=True, out_dtype=BF16, tk=512, name="mix_in_p_dw")
    gb["w_xq"] = _mm(h, dxq, ta=True, out_dtype=BF16, tk=512, name="mix_in_xq_dw")
    gb["w_gates"] = _mm(h, dgt, ta=True, out_dtype=BF16, tk=512, tn=1536, name="mix_in_gates_dw")
    dx1, gs["mix_pre_g"] = _rms_bwd(x1, small["mix_pre_g"], dh_parts, dx2, 1.0, F32, name="mix_pre_bwd")
    dz1, gs["ffn1_post_g"] = _rms_bwd(f1, small["ffn1_post_g"], [dx1], None, 0.5, BF16, name="ffn1_post_bwd")
    dh1, gb["ffn1_w_in"], gb["ffn1_w_out"] = _ffn_bwd(dz1, h1, ab1, u1, big["ffn1_w_in"], big["ffn1_w_out"], "ffn1")
    dx0, gs["ffn1_pre_g"] = _rms_bwd(x, small["ffn1_pre_g"], [dh1], dx1, 1.0, F32, name="ffn1_pre_bwd")
    return loss, dx0, gs, gb


BIG = ("ffn1_w_in", "ffn1_w_out", "w_in", "w_mem_kv", "w_up_gla", "w_up_pool", "w_up_xattn", "w_o", "ffn2_w_in", "ffn2_w_out")
COL_SHARDED = ("ffn1_w_in", "w_in", "w_up_pool", "w_up_xattn", "ffn2_w_in")
GAINS = ("ffn1_pre_g", "ffn1_post_g", "mix_pre_g", "gla_norm_g", "mem_norm_g", "mix_post_g", "ffn2_pre_g", "ffn2_post_g", "final_g")
WEIGHTS = ("ffn1_pre_g", "ffn1_w_in", "ffn1_w_out", "ffn1_post_g", "mix_pre_g", "w_in", "w_fu", "b_f", "gla_norm_g", "w_pool",
           "pool_scale", "mem_norm_g", "w_mem_kv", "w_up_gla", "w_up_pool", "w_up_xattn", "w_o", "mix_post_g", "ffn2_pre_g",
           "ffn2_w_in", "ffn2_w_out", "ffn2_post_g", "final_g")
IN_GLA, IN_F, IN_PX, IN_GATES, IN_END = 0, 3072, 3088, 4112, 7184
PACK_ROWS = 96


def _cols_from_shards(g):
    return jnp.transpose(g, (1, 0, 2)).reshape(g.shape[1], 4 * g.shape[2])


def _pack_small(t):
    rows = [t[n].reshape(1, D) for n in GAINS]
    rows.append(jnp.concatenate([t["b_f"].reshape(1, 512), t["pool_scale"].reshape(1, 512)], axis=1))
    rows.append(t["w_pool"].reshape(64, D))
    rows.append(t["w_fu"].reshape(8, D))
    used = len(GAINS) + 1 + 64 + 8
    rows.append(jnp.zeros((PACK_ROWS - used, D), F32))
    return jnp.concatenate(rows, axis=0)


def _unpack_small(p):
    out = {n: p[i:i + 1] for i, n in enumerate(GAINS)}
    k = len(GAINS)
    out["b_f"] = p[k:k + 1, 0:512]
    out["pool_scale"] = p[k:k + 1, 512:1024]
    out["w_pool"] = p[k + 1:k + 65].reshape(4, LANE, LANE)
    out["w_fu"] = p[k + 65:k + 73].reshape(GATE_RANK, 512)
    return out


def kernel(x, mem, ffn1_pre_g, ffn1_w_in, ffn1_w_out, ffn1_post_g, mix_pre_g, w_in, w_fu, b_f, gla_norm_g, w_pool, pool_scale, mem_norm_g, w_mem_kv, w_up_gla, w_up_pool, w_up_xattn, w_o, mix_post_g, ffn2_pre_g, ffn2_w_in, ffn2_w_out, ffn2_post_g, final_g, loss_target, m_ffn1_pre_g, m_ffn1_w_in, m_ffn1_w_out, m_ffn1_post_g, m_mix_pre_g, m_w_in, m_w_fu, m_b_f, m_gla_norm_g, m_w_pool, m_pool_scale, m_mem_norm_g, m_w_mem_kv, m_w_up_gla, m_w_up_pool, m_w_up_xattn, m_w_o, m_mix_post_g, m_ffn2_pre_g, m_ffn2_w_in, m_ffn2_w_out, m_ffn2_post_g, m_final_g, v_ffn1_pre_g, v_ffn1_w_in, v_ffn1_w_out, v_ffn1_post_g, v_mix_pre_g, v_w_in, v_w_fu, v_b_f, v_gla_norm_g, v_w_pool, v_pool_scale, v_mem_norm_g, v_w_mem_kv, v_w_up_gla, v_w_up_pool, v_w_up_xattn, v_w_o, v_mix_post_g, v_ffn2_pre_g, v_ffn2_w_in, v_ffn2_w_out, v_ffn2_post_g, v_final_g):
    args = dict(locals())
    w = {n: args[n][0] for n in WEIGHTS}
    m = {n: args["m_" + n][0] for n in WEIGHTS}
    v = {n: args["v_" + n][0] for n in WEIGHTS}
    xi, yi, ci = lax.axis_index("x"), lax.axis_index("y"), lax.axis_index("c")
    chip = 2 * xi + yi

    shards = [w[n].astype(BF16) for n in BIG] + [w["w_fu"]]
    gathered = _gather_shards(shards, name="gather_weights")
    full = {}
    for n, g in zip(BIG, gathered[:-1]):
        full[n] = _cols_from_shards(g) if n in COL_SHARDED else g.reshape(4 * g.shape[1], g.shape[2])
    w_fu_full = _cols_from_shards(gathered[-1])
    big = {n: full[n] for n in BIG if n != "w_in"}
    wi = full["w_in"]
    big["w_gla"] = jnp.concatenate([wi[:, IN_GLA:IN_PX], jnp.zeros((D, PG_W - IN_PX), BF16)], axis=1)
    big["w_px"] = wi[:, IN_PX:IN_GATES]
    big["w_p"] = wi[:, IN_PX:IN_PX + 512]
    big["w_xq"] = wi[:, IN_PX + 512:IN_GATES]
    big["w_gates"] = wi[:, IN_GATES:IN_END]
    small = {n: w[n].reshape(1, D) for n in GAINS}
    small["b_f"] = w["b_f"].reshape(1, 512)
    small["pool_scale"] = w["pool_scale"].reshape(1, 512)
    small["w_pool_b"] = w["w_pool"].astype(BF16)
    small["w_fu_pad"] = jnp.concatenate([w_fu_full, jnp.zeros((LANE - GATE_RANK, 512), F32)], axis=0).astype(BF16)

    loss, grad_x, gs, gb = _local_step(x[0], mem[0], loss_target[0], small, big)
    loss = lax.psum(loss[0, 0], ("x", "y", "c"))

    gs["w_fu"] = gs.pop("w_fu_pad")[0:GATE_RANK]
    small_sum = _unpack_small(_all_sum(_pack_small(gs), name="sum_small_grads"))
    dwi = jnp.concatenate([gb.pop("w_gla")[:, 0:IN_PX], gb.pop("w_p"), gb.pop("w_xq"), gb.pop("w_gates")], axis=1)
    gb["w_in"] = jnp.transpose(dwi.reshape(D, 4, IN_END // 4), (1, 0, 2))
    for n in BIG:
        if n not in COL_SHARDED:
            gb[n] = gb[n].reshape(4, gb[n].shape[0] // 4, gb[n].shape[1])
    contrib = [gb[n] for n in BIG]
    c_arr = jnp.reshape(ci, (1,)).astype(jnp.int32)
    chip_arr = jnp.reshape(chip, (1,)).astype(jnp.int32)
    from_sibling = _pair_exchange(contrib, name="grads_pair_exchange")
    pair = [_pair_sum(g, got, c_arr, name="grads_pair_sum_" + n) for n, g, got in zip(BIG, contrib, from_sibling)]
    from_chips = _chip_exchange(pair, name="grads_chip_exchange")
    halves = [_chip_sum(p, got, chip_arr, name="grads_chip_sum_" + n) for n, p, got in zip(BIG, pair, from_chips)]
    reduced = dict(zip(BIG, _pair_join(halves, name="grads_pair_join")))

    grads, delta, new_m, new_v = {}, {}, {}, {}
    for n in BIG:
        grads[n] = reduced[n]
        delta[n], new_m[n], new_v[n] = _adamw(w[n], reduced[n], m[n], v[n], name="adamw_" + n)
    small_names = GAINS + ("b_f", "pool_scale", "w_pool")
    w_fu_grad = lax.dynamic_slice_in_dim(small_sum["w_fu"], chip * LANE, LANE, axis=1)
    packs = []
    for t in (w, m, v):
        t = dict(t)
        t["w_fu"] = jnp.zeros((GATE_RANK, 512), F32)
        packs.append(_pack_small(t))
    sd, sm, sv = (_unpack_small(p) for p in _adamw(packs[0], _pack_small(small_sum), packs[1], packs[2], name="adamw_small"))
    for n in small_names:
        shape = w[n].shape
        grads[n] = small_sum[n].reshape(shape)
        delta[n], new_m[n], new_v[n] = sd[n].reshape(shape), sm[n].reshape(shape), sv[n].reshape(shape)
    grads["w_fu"] = w_fu_grad
    delta["w_fu"], new_m["w_fu"], new_v["w_fu"] = _adamw(w["w_fu"], w_fu_grad, m["w_fu"], v["w_fu"], name="adamw_w_fu")

    outs = [loss, grad_x[None]]
    for group in (grads, delta, new_m, new_v):
        outs += [group[n][None] for n in WEIGHTS]
    return tuple(outs)
```

```python
import functools

import jax
import jax.numpy as jnp
from jax import lax
from jax.experimental import pallas as pl
from jax.experimental.pallas import tpu as pltpu

F32 = jnp.float32
BF16 = jnp.bfloat16
MESH = pl.DeviceIdType.MESH
HIGHEST = lax.Precision.HIGHEST

D = 1024
DFF = 2816
CHUNK = 64
HEADS = 4
HDK = 128
HDV = 256
GATE_TEMP = 16.0
POOL_WINDOWS = (2, 4, 8, 16)
POOL_HALO = 16
XA_HEADS = 4
XA_HD = 128
EPS = 1e-6
Q_SCALE = HDK ** -0.5
XA_SCALE = XA_HD ** -0.5
PG_Q, PG_K, PG_V, PG_G, PG_F, PG_W = 0, 512, 1024, 2048, 3072, 3200
GATE_RANK = 16
ADAM_LR, ADAM_B1, ADAM_B2, ADAM_EPS, ADAM_WD, ADAM_STEP = 0.001, 0.9, 0.999, 1e-08, 0.01, 10

VMEM_LIMIT = 48 * 1024 * 1024
LANE = 128
TS_ROW = 256
TS_GLA = 512
TS_POOL = 512
TS_XA = 512


def _params(sem):
    return pltpu.CompilerParams(dimension_semantics=sem, vmem_limit_bytes=VMEM_LIMIT)


def _tile(n, cap, unit=LANE):
    if n <= cap:
        return n
    best = None
    for t in range(unit, cap + 1, unit):
        if n % t == 0:
            best = t
    assert best is not None, (n, cap)
    return best


def _sigmoid(x):
    return 1.0 / (1.0 + jnp.exp(-x))


def _log_sigmoid(x):
    return jnp.minimum(x, 0.0) - jnp.log(1.0 + jnp.exp(-jnp.abs(x)))


def _rms(x):
    r = lax.rsqrt(jnp.mean(x * x, axis=-1, keepdims=True) + EPS)
    return x * r, r


def _rows(ts, w):
    return pl.BlockSpec((ts, w), lambda i: (i, 0))


def _fixed(shape):
    nd = len(shape)
    return pl.BlockSpec(shape, lambda i: (0,) * nd)


def _mm(a, b, *, ta=False, tb=False, out_dtype=F32, tm=512, tn=1024, tk=1024, shards=1, name):
    m, kdim = (a.shape[1], a.shape[0]) if ta else a.shape
    n = b.shape[0] if tb else b.shape[1]
    assert (b.shape[1] if tb else b.shape[0]) == kdim, (a.shape, b.shape, ta, tb)
    tm = _tile(m, tm)
    tn = n // shards if shards > 1 else _tile(n, tn)
    tk = _tile(kdim, tk)
    nk = kdim // tk
    dims = (((0 if ta else 1,), (1 if tb else 0,)), ((), ()))

    def body(a_ref, b_ref, o_ref, *acc):
        part = lax.dot_general(a_ref[...], b_ref[...], dims, preferred_element_type=F32)
        if nk == 1:
            o_ref[...] = part.astype(o_ref.dtype)
            return
        acc_ref, = acc
        k = pl.program_id(2)

        @pl.when(k == 0)
        def _():
            acc_ref[...] = part

        @pl.when(k > 0)
        def _():
            acc_ref[...] += part

        @pl.when(k == nk - 1)
        def _():
            o_ref[...] = acc_ref[...].astype(o_ref.dtype)

    a_spec = pl.BlockSpec((tk, tm), lambda i, j, k: (k, i)) if ta else pl.BlockSpec((tm, tk), lambda i, j, k: (i, k))
    b_spec = pl.BlockSpec((tn, tk), lambda i, j, k: (j, k)) if tb else pl.BlockSpec((tk, tn), lambda i, j, k: (k, j))
    if shards > 1:
        out_shape = jax.ShapeDtypeStruct((shards, m, tn), out_dtype)
        o_spec = pl.BlockSpec((None, tm, tn), lambda i, j, k: (j, i, 0))
    else:
        out_shape = jax.ShapeDtypeStruct((m, n), out_dtype)
        o_spec = pl.BlockSpec((tm, tn), lambda i, j, k: (i, j))
    return pl.pallas_call(
        body, grid=(m // tm, n // tn, nk), in_specs=[a_spec, b_spec], out_specs=o_spec, out_shape=out_shape,
        scratch_shapes=[pltpu.VMEM((tm, tn), F32)] if nk > 1 else [],
        compiler_params=_params(("parallel", "parallel", "arbitrary")), name=name,
    )(a, b)


def _norm_fwd(x, g, out_dtype, name):
    s, d = x.shape
    ts = _tile(s, TS_ROW, 8)

    def body(x_ref, g_ref, o_ref):
        xh, _ = _rms(x_ref[...])
        o_ref[...] = (xh * g_ref[...]).astype(o_ref.dtype)

    return pl.pallas_call(
        body, grid=(s // ts,), in_specs=[_rows(ts, d), _fixed((1, d))], out_specs=_rows(ts, d),
        out_shape=jax.ShapeDtypeStruct((s, d), out_dtype), compiler_params=_params(("parallel",)), name=name,
    )(x, g)


def _resid_norm_fwd(x, f, g_post, alpha, g_next, name):
    s, d = x.shape
    ts = _tile(s, TS_ROW, 8)
    with_h = g_next is not None

    def body(x_ref, f_ref, gp_ref, *rest):
        fh, _ = _rms(f_ref[...])
        xn = x_ref[...] + alpha * (fh * gp_ref[...])
        if with_h:
            gn_ref, xo_ref, h_ref = rest
            xh, _ = _rms(xn)
            h_ref[...] = (xh * gn_ref[...]).astype(h_ref.dtype)
        else:
            xo_ref, = rest
        xo_ref[...] = xn

    ins = [x, f, g_post] + ([g_next] if with_h else [])
    in_specs = [_rows(ts, d), _rows(ts, d), _fixed((1, d))] + ([_fixed((1, d))] if with_h else [])
    out_shape = [jax.ShapeDtypeStruct((s, d), F32)] + ([jax.ShapeDtypeStruct((s, d), BF16)] if with_h else [])
    out_specs = [_rows(ts, d)] + ([_rows(ts, d)] if with_h else [])
    out = pl.pallas_call(
        body, grid=(s // ts,), in_specs=in_specs, out_specs=out_specs, out_shape=out_shape,
        compiler_params=_params(("parallel",)), name=name,
    )(*ins)
    return (out[0], out[1]) if with_h else (out[0], None)


def _rms_bwd(x, g, dys, dres, alpha, out_dtype, name):
    s, d = x.shape
    ts = _tile(s, TS_ROW, 8)
    ndy = len(dys)
    with_res = dres is not None

    def body(x_ref, g_ref, *rest):
        dy_refs = rest[:ndy]
        rest = rest[ndy:]
        if with_res:
            dres_ref, dx_ref, dg_ref = rest
        else:
            dx_ref, dg_ref = rest
        xh, r = _rms(x_ref[...])
        dy = dy_refs[0][...].astype(F32)
        for ref in dy_refs[1:]:
            dy = dy + ref[...].astype(F32)
        dy = dy * alpha

        @pl.when(pl.program_id(0) == 0)
        def _():
            dg_ref[...] = jnp.zeros_like(dg_ref)

        dg_ref[...] += jnp.sum(dy * xh, axis=0, keepdims=True)
        dyg = dy * g_ref[...]
        dx = r * (dyg - xh * jnp.mean(dyg * xh, axis=-1, keepdims=True))
        if with_res:
            dx = dx + dres_ref[...]
        dx_ref[...] = dx.astype(dx_ref.dtype)

    ins = [x, g] + list(dys) + ([dres] if with_res else [])
    in_specs = [_rows(ts, d), _fixed((1, d))] + [_rows(ts, d)] * (ndy + int(with_res))
    return pl.pallas_call(
        body, grid=(s // ts,), in_specs=in_specs, out_specs=[_rows(ts, d), _fixed((1, d))],
        out_shape=[jax.ShapeDtypeStruct((s, d), out_dtype), jax.ShapeDtypeStruct((1, d), F32)],
        compiler_params=_params(("arbitrary",)), name=name,
    )(*ins)


def _loss_bwd(x, g, target, name):
    s, d = x.shape
    ts = _tile(s, TS_ROW, 8)

    def body(x_ref, g_ref, t_ref, dx_ref, dg_ref, loss_ref):
        xh, r = _rms(x_ref[...])
        gv = g_ref[...]
        diff = xh * gv - t_ref[...]

        @pl.when(pl.program_id(0) == 0)
        def _():
            dg_ref[...] = jnp.zeros_like(dg_ref)
            loss_ref[...] = jnp.zeros_like(loss_ref)

        sq = jnp.sum(diff * diff, axis=1, keepdims=True)
        loss_ref[...] += (0.5 / d) * jnp.sum(sq, axis=0, keepdims=True)
        dy = diff * (1.0 / d)
        dg_ref[...] += jnp.sum(dy * xh, axis=0, keepdims=True)
        dyg = dy * gv
        dx_ref[...] = r * (dyg - xh * jnp.mean(dyg * xh, axis=-1, keepdims=True))

    return pl.pallas_call(
        body, grid=(s // ts,), in_specs=[_rows(ts, d), _fixed((1, d)), _rows(ts, d)],
        out_specs=[_rows(ts, d), _fixed((1, d)), _fixed((8, LANE))],
        out_shape=[jax.ShapeDtypeStruct((s, d), F32), jax.ShapeDtypeStruct((1, d), F32), jax.ShapeDtypeStruct((8, LANE), F32)],
        compiler_params=_params(("arbitrary",)), name=name,
    )(x, g, target)


def _swiglu_fwd(ab, name):
    s = ab.shape[0]
    ts = _tile(s, TS_ROW, 8)

    def body(a_ref, b_ref, u_ref):
        a = a_ref[...].astype(F32)
        u_ref[...] = (a * _sigmoid(a) * b_ref[...].astype(F32)).astype(u_ref.dtype)

    return pl.pallas_call(
        body, grid=(s // ts,),
        in_specs=[pl.BlockSpec((ts, DFF), lambda i: (i, 0)), pl.BlockSpec((ts, DFF), lambda i: (i, 1))],
        out_specs=_rows(ts, DFF), out_shape=jax.ShapeDtypeStruct((s, DFF), BF16),
        compiler_params=_params(("parallel",)), name=name,
    )(ab, ab)


def _swiglu_bwd(ab, du, name):
    s = ab.shape[0]
    ts = _tile(s, TS_ROW, 8)

    def body(a_ref, b_ref, du_ref, dab_ref):
        a = a_ref[...].astype(F32)
        b = b_ref[...].astype(F32)
        dy = du_ref[...].astype(F32)
        sig = _sigmoid(a)
        dab_ref[:, 0:DFF] = (dy * b * (sig * (1.0 + a * (1.0 - sig)))).astype(dab_ref.dtype)
        dab_ref[:, DFF:2 * DFF] = (dy * a * sig).astype(dab_ref.dtype)

    return pl.pallas_call(
        body, grid=(s // ts,),
        in_specs=[pl.BlockSpec((ts, DFF), lambda i: (i, 0)), pl.BlockSpec((ts, DFF), lambda i: (i, 1)), _rows(ts, DFF)],
        out_specs=_rows(ts, 2 * DFF), out_shape=jax.ShapeDtypeStruct((s, 2 * DFF), BF16),
        compiler_params=_params(("parallel",)), name=name,
    )(ab, ab, du)


def _tri(strict):
    r = lax.broadcasted_iota(jnp.int32, (CHUNK, CHUNK), 0)
    c = lax.broadcasted_iota(jnp.int32, (CHUNK, CHUNK), 1)
    return (r > c).astype(F32) if strict else (r >= c).astype(F32)


def _gla_fwd(pg, wfu, b_f, gnorm, name):
    s = pg.shape[0]
    ts = _tile(s, TS_GLA, CHUNK)
    cpb = ts // CHUNK
    nc = s // CHUNK

    def body(pg_ref, wfu_ref, bf_ref, gn_ref, ya_ref, sp_ref, st_ref, la_ref):
        @pl.when(pl.program_id(0) == 0)
        def _():
            st_ref[...] = jnp.zeros_like(st_ref)

        f = jnp.dot(pg_ref[:, PG_F:PG_W], wfu_ref[...], preferred_element_type=F32) + bf_ref[...]
        la_ref[...] = _log_sigmoid(f) * (1.0 / GATE_TEMP)
        tri = _tri(False)

        def chunk(ci, carry):
            rows = pl.ds(pl.multiple_of(ci * CHUNK, CHUNK), CHUNK)
            b = jnp.dot(tri, la_ref[rows, :], precision=HIGHEST, preferred_element_type=F32)
            bend = jnp.sum(la_ref[rows, :], axis=0, keepdims=True)
            e = jnp.exp(bend - b)
            dec = jnp.exp(bend)
            for hd in range(HEADS):
                kc = slice(hd * HDK, (hd + 1) * HDK)
                vc = slice(hd * HDV, (hd + 1) * HDV)
                q = pg_ref[rows, PG_Q + hd * HDK:PG_Q + (hd + 1) * HDK]
                k = pg_ref[rows, PG_K + hd * HDK:PG_K + (hd + 1) * HDK]
                v = pg_ref[rows, PG_V + hd * HDV:PG_V + (hd + 1) * HDV]
                go = pg_ref[rows, PG_G + hd * HDV:PG_G + (hd + 1) * HDV].astype(F32)
                kt = (k.astype(F32) * e[:, kc]).astype(BF16)
                prev = st_ref[hd]
                sp_ref[ci, hd] = prev
                st = prev * dec[:, kc] + lax.dot_general(v, kt, (((0,), (0,)), ((), ())), preferred_element_type=F32)
                st_ref[hd] = st
                qs = (q.astype(F32) * Q_SCALE).astype(BF16)
                o = lax.dot_general(qs, st.astype(BF16), (((1,), (1,)), ((), ())), preferred_element_type=F32)
                oh, _ = _rms(o)
                ya_ref[rows, vc] = (oh * gn_ref[:, vc] * (go * _sigmoid(go))).astype(ya_ref.dtype)
            return carry

        lax.fori_loop(0, cpb, chunk, 0)

    return pl.pallas_call(
        body, grid=(s // ts,),
        in_specs=[_rows(ts, PG_W), _fixed((LANE, HEADS * HDK)), _fixed((1, HEADS * HDK)), _fixed((1, HEADS * HDV))],
        out_specs=[_rows(ts, HEADS * HDV), pl.BlockSpec((cpb, HEADS, HDV, HDK), lambda i: (i, 0, 0, 0))],
        out_shape=[jax.ShapeDtypeStruct((s, HEADS * HDV), BF16), jax.ShapeDtypeStruct((nc, HEADS, HDV, HDK), F32)],
        scratch_shapes=[pltpu.VMEM((HEADS, HDV, HDK), F32), pltpu.VMEM((ts, HEADS * HDK), F32)],
        compiler_params=_params(("arbitrary",)), name=name,
    )(pg, wfu, b_f, gnorm)


def _gla_bwd(pg, sp, dya, wfu, b_f, gnorm, name):
    s = pg.shape[0]
    ts = _tile(s, TS_GLA, CHUNK)
    cpb = ts // CHUNK
    nblk = s // ts

    def body(pg_ref, sp_ref, dya_ref, wfu_ref, bf_ref, gn_ref, dpg_ref, dwfu_ref, dbf_ref, dgn_ref, dst_ref, la_ref, sg_ref, df_ref):
        @pl.when(pl.program_id(0) == 0)
        def _():
            dst_ref[...] = jnp.zeros_like(dst_ref)
            dwfu_ref[...] = jnp.zeros_like(dwfu_ref)
            dbf_ref[...] = jnp.zeros_like(dbf_ref)
            dgn_ref[...] = jnp.zeros_like(dgn_ref)

        flow = pg_ref[:, PG_F:PG_W]
        f = jnp.dot(flow, wfu_ref[...], preferred_element_type=F32) + bf_ref[...]
        la_ref[...] = _log_sigmoid(f) * (1.0 / GATE_TEMP)
        sg_ref[...] = _sigmoid(-f) * (1.0 / GATE_TEMP)
        tri = _tri(False)
        tri_strict = _tri(True)

        def chunk(t, carry):
            ci = cpb - 1 - t
            rows = pl.ds(pl.multiple_of(ci * CHUNK, CHUNK), CHUNK)
            b = jnp.dot(tri, la_ref[rows, :], precision=HIGHEST, preferred_element_type=F32)
            bend = jnp.sum(la_ref[rows, :], axis=0, keepdims=True)
            e = jnp.exp(bend - b)
            dec = jnp.exp(bend)
            for hd in range(HEADS):
                kc = slice(hd * HDK, (hd + 1) * HDK)
                vc = slice(hd * HDV, (hd + 1) * HDV)
                q = pg_ref[rows, PG_Q + hd * HDK:PG_Q + (hd + 1) * HDK]
                k = pg_ref[rows, PG_K + hd * HDK:PG_K + (hd + 1) * HDK]
                v = pg_ref[rows, PG_V + hd * HDV:PG_V + (hd + 1) * HDV]
                go = pg_ref[rows, PG_G + hd * HDV:PG_G + (hd + 1) * HDV].astype(F32)
                eh = e[:, kc]
                dech = dec[:, kc]
                ktf = k.astype(F32) * eh
                kt = ktf.astype(BF16)
                prev = sp_ref[ci, hd]
                st = prev * dech + lax.dot_general(v, kt, (((0,), (0,)), ((), ())), preferred_element_type=F32)
                st_b = st.astype(BF16)
                qs = (q.astype(F32) * Q_SCALE).astype(BF16)
                o = lax.dot_general(qs, st_b, (((1,), (1,)), ((), ())), preferred_element_type=F32)
                oh, r = _rms(o)
                gh = gn_ref[:, vc]
                sig = _sigmoid(go)
                dy = dya_ref[rows, vc].astype(F32)
                don = dy * (go * sig)
                dgo = dy * (oh * gh) * (sig * (1.0 + go * (1.0 - sig)))
                dgn_ref[:, vc] += jnp.sum(don * oh, axis=0, keepdims=True)
                dong = don * gh
                do = (r * (dong - oh * jnp.mean(dong * oh, axis=-1, keepdims=True))).astype(BF16)
                dst = dst_ref[hd] + lax.dot_general(do, qs, (((0,), (0,)), ((), ())), preferred_element_type=F32)
                dst_b = dst.astype(BF16)
                dq = jnp.dot(do, st_b, preferred_element_type=F32) * Q_SCALE
                dkt = jnp.dot(v, dst_b, preferred_element_type=F32)
                dv = lax.dot_general(kt, dst_b, (((1,), (1,)), ((), ())), preferred_element_type=F32)
                dd = jnp.sum(dst * prev, axis=0, keepdims=True)
                dla = jnp.dot(tri_strict, dkt * ktf, precision=HIGHEST, preferred_element_type=F32) + dd * dech
                df_ref[rows, kc] = dla * sg_ref[rows, kc]
                dst_ref[hd] = dst * dech
                dpg_ref[rows, PG_Q + hd * HDK:PG_Q + (hd + 1) * HDK] = dq.astype(dpg_ref.dtype)
                dpg_ref[rows, PG_K + hd * HDK:PG_K + (hd + 1) * HDK] = (dkt * eh).astype(dpg_ref.dtype)
                dpg_ref[rows, PG_V + hd * HDV:PG_V + (hd + 1) * HDV] = dv.astype(dpg_ref.dtype)
                dpg_ref[rows, PG_G + hd * HDV:PG_G + (hd + 1) * HDV] = dgo.astype(dpg_ref.dtype)
            return carry

        lax.fori_loop(0, cpb, chunk, 0)
        df = df_ref[...]
        df_b = df.astype(BF16)
        dpg_ref[:, PG_F:PG_W] = lax.dot_general(df_b, wfu_ref[...], (((1,), (1,)), ((), ())), preferred_element_type=F32).astype(dpg_ref.dtype)
        dwfu_ref[...] += lax.dot_general(flow, df_b, (((0,), (0,)), ((), ())), preferred_element_type=F32)
        dbf_ref[...] += jnp.sum(df, axis=0, keepdims=True)

    rev = lambda i: (nblk - 1 - i, 0)
    return pl.pallas_call(
        body, grid=(nblk,),
        in_specs=[pl.BlockSpec((ts, PG_W), rev), pl.BlockSpec((cpb, HEADS, HDV, HDK), lambda i: (nblk - 1 - i, 0, 0, 0)),
                  pl.BlockSpec((ts, HEADS * HDV), rev), _fixed((LANE, HEADS * HDK)), _fixed((1, HEADS * HDK)), _fixed((1, HEADS * HDV))],
        out_specs=[pl.BlockSpec((ts, PG_W), rev), _fixed((LANE, HEADS * HDK)), _fixed((1, HEADS * HDK)), _fixed((1, HEADS * HDV))],
        out_shape=[jax.ShapeDtypeStruct((s, PG_W), BF16), jax.ShapeDtypeStruct((LANE, HEADS * HDK), F32),
                   jax.ShapeDtypeStruct((1, HEADS * HDK), F32), jax.ShapeDtypeStruct((1, HEADS * HDV), F32)],
        scratch_shapes=[pltpu.VMEM((HEADS, HDV, HDK), F32), pltpu.VMEM((ts, HEADS * HDK), F32),
                        pltpu.VMEM((ts, HEADS * HDK), F32), pltpu.VMEM((ts, HEADS * HDK), F32)],
        compiler_params=_params(("arbitrary",)), name=name,
    )(pg, sp, dya, wfu, b_f, gnorm)


def _window_sums(ext, sign):
    n = ext.shape[0]
    sums = {1: ext}
    w = 1
    while w < POOL_WINDOWS[-1]:
        sums[2 * w] = sums[w] + pltpu.roll(sums[w], w if sign > 0 else n - w, 0)
        w *= 2
    return [sums[POOL_WINDOWS[g]][:, g * LANE:(g + 1) * LANE] for g in range(len(POOL_WINDOWS))]


def _pool_counts(row0, n):
    pos = (row0 + lax.broadcasted_iota(jnp.int32, (n, 1), 0) + 1).astype(F32)
    return [jnp.minimum(pos, float(w)) for w in POOL_WINDOWS]


def _pool_fwd(ppx, w_pool, pool_scale, name):
    s = ppx.shape[0]
    ts = _tile(s, TS_POOL, POOL_HALO)
    hb = ts // POOL_HALO
    pw = len(POOL_WINDOWS) * LANE

    def body(p_ref, halo_ref, w_ref, sc_ref, y_ref, ext_ref):
        i = pl.program_id(0)
        p = p_ref[...].astype(F32)
        ext_ref[0:POOL_HALO, :] = jnp.where(i > 0, halo_ref[...].astype(F32), 0.0)
        ext_ref[POOL_HALO:, :] = p
        sums = _window_sums(ext_ref[...], +1)
        cnt = _pool_counts(i * ts, ts)
        for g in range(len(POOL_WINDOWS)):
            cols = slice(g * LANE, (g + 1) * LANE)
            mixed = sums[g][POOL_HALO:, :] / cnt[g] - p[:, cols]
            y = jnp.dot(mixed.astype(BF16), w_ref[g], preferred_element_type=F32)
            y_ref[:, cols] = (y * sc_ref[:, cols]).astype(y_ref.dtype)

    return pl.pallas_call(
        body, grid=(s // ts,),
        in_specs=[pl.BlockSpec((ts, pw), lambda i: (i, 0)), pl.BlockSpec((POOL_HALO, pw), lambda i: (jnp.maximum(i * hb - 1, 0), 0)),
                  _fixed((len(POOL_WINDOWS), LANE, LANE)), _fixed((1, pw))],
        out_specs=_rows(ts, pw), out_shape=jax.ShapeDtypeStruct((s, pw), BF16),
        scratch_shapes=[pltpu.VMEM((ts + POOL_HALO, pw), F32)],
        compiler_params=_params(("parallel",)), name=name,
    )(ppx, ppx, w_pool, pool_scale)


def _pool_bwd(dyb, ppx, w_pool, pool_scale, name):
    s = ppx.shape[0]
    ts = _tile(s, TS_POOL, POOL_HALO)
    hb = ts // POOL_HALO
    nblk = s // ts
    last_halo = s // POOL_HALO - 1
    ng = len(POOL_WINDOWS)
    pw = ng * LANE

    def body(p_ref, halo_ref, dy_ref, dyn_ref, w_ref, sc_ref, dp_ref, dw_ref, dsc_ref, ext_ref, dext_ref, dm_ref):
        i = pl.program_id(0)

        @pl.when(i == 0)
        def _():
            dw_ref[...] = jnp.zeros_like(dw_ref)
            dsc_ref[...] = jnp.zeros_like(dsc_ref)

        p = p_ref[...].astype(F32)
        ext_ref[0:POOL_HALO, :] = jnp.where(i > 0, halo_ref[...].astype(F32), 0.0)
        ext_ref[POOL_HALO:, :] = p
        sums = _window_sums(ext_ref[...], +1)
        cnt = _pool_counts(i * ts, ts + POOL_HALO)
        sc = sc_ref[...]
        dy = dy_ref[...].astype(F32)
        dyn = jnp.where(i < nblk - 1, dyn_ref[...].astype(F32), 0.0)
        for g in range(ng):
            cols = slice(g * LANE, (g + 1) * LANE)
            wg = w_ref[g]
            mixed = (sums[g][POOL_HALO:, :] / cnt[g][0:ts] - p[:, cols]).astype(BF16)
            ypre = jnp.dot(mixed, wg, preferred_element_type=F32)
            dsc_ref[:, cols] += jnp.sum(dy[:, cols] * ypre, axis=0, keepdims=True)
            dyp = (dy[:, cols] * sc[:, cols]).astype(BF16)
            dypn = (dyn[:, cols] * sc[:, cols]).astype(BF16)
            dw_ref[g] += lax.dot_general(mixed, dyp, (((0,), (0,)), ((), ())), preferred_element_type=F32)
            dm = lax.dot_general(dyp, wg, (((1,), (1,)), ((), ())), preferred_element_type=F32)
            dmn = lax.dot_general(dypn, wg, (((1,), (1,)), ((), ())), preferred_element_type=F32)
            dext_ref[0:ts, cols] = dm / cnt[g][0:ts]
            dext_ref[ts:, cols] = dmn / cnt[g][ts:]
            dm_ref[:, cols] = dm
        lead = _window_sums(dext_ref[...], -1)
        for g in range(ng):
            cols = slice(g * LANE, (g + 1) * LANE)
            dp_ref[:, cols] = (lead[g][0:ts, :] - dm_ref[:, cols]).astype(dp_ref.dtype)

    return pl.pallas_call(
        body, grid=(nblk,),
        in_specs=[pl.BlockSpec((ts, pw), lambda i: (i, 0)), pl.BlockSpec((POOL_HALO, pw), lambda i: (jnp.maximum(i * hb - 1, 0), 0)),
                  pl.BlockSpec((ts, pw), lambda i: (i, 0)), pl.BlockSpec((POOL_HALO, pw), lambda i: (jnp.minimum((i + 1) * hb, last_halo), 0)),
                  _fixed((ng, LANE, LANE)), _fixed((1, pw))],
        out_specs=[_rows(ts, pw), _fixed((ng, LANE, LANE)), _fixed((1, pw))],
        out_shape=[jax.ShapeDtypeStruct((s, pw), BF16), jax.ShapeDtypeStruct((ng, LANE, LANE), F32), jax.ShapeDtypeStruct((1, pw), F32)],
        scratch_shapes=[pltpu.VMEM((ts + POOL_HALO, pw), F32), pltpu.VMEM((ts + POOL_HALO, pw), F32), pltpu.VMEM((ts, pw), F32)],
        compiler_params=_params(("arbitrary",)), name=name,
    )(ppx, ppx, dyb, dyb, w_pool, pool_scale)


def _xattn_fwd(ppx, kv, name):
    s = ppx.shape[0]
    m = kv.shape[0]
    ts = _tile(s, TS_XA, 8)
    xw = XA_HEADS * XA_HD

    def body(q_ref, kv_ref, o_ref):
        for hd in range(XA_HEADS):
            cols = slice(hd * XA_HD, (hd + 1) * XA_HD)
            k = kv_ref[:, hd * XA_HD:(hd + 1) * XA_HD]
            v = kv_ref[:, xw + hd * XA_HD:xw + (hd + 1) * XA_HD]
            sc = lax.dot_general(q_ref[:, cols], k, (((1,), (1,)), ((), ())), preferred_element_type=F32) * XA_SCALE
            ex = jnp.exp(sc - jnp.max(sc, axis=-1, keepdims=True))
            pr = ex / jnp.sum(ex, axis=-1, keepdims=True)
            o_ref[:, cols] = jnp.dot(pr.astype(BF16), v, preferred_element_type=F32).astype(o_ref.dtype)

    return pl.pallas_call(
        body, grid=(s // ts,), in_specs=[pl.BlockSpec((ts, xw), lambda i: (i, 1)), _fixed((m, 2 * xw))],
        out_specs=_rows(ts, xw), out_shape=jax.ShapeDtypeStruct((s, xw), BF16),
        compiler_params=_params(("parallel",)), name=name,
    )(ppx, kv)


def _xattn_bwd(dxc, ppx, kv, name):
    s = ppx.shape[0]
    m = kv.shape[0]
    ts = _tile(s, TS_XA, 8)
    xw = XA_HEADS * XA_HD

    def body(do_ref, q_ref, kv_ref, dq_ref, dkv_ref):
        @pl.when(pl.program_id(0) == 0)
        def _():
            dkv_ref[...] = jnp.zeros_like(dkv_ref)

        for hd in range(XA_HEADS):
            cols = slice(hd * XA_HD, (hd + 1) * XA_HD)
            vcols = slice(xw + hd * XA_HD, xw + (hd + 1) * XA_HD)
            q = q_ref[:, cols]
            k = kv_ref[:, cols]
            v = kv_ref[:, vcols]
            do = do_ref[:, cols]
            sc = lax.dot_general(q, k, (((1,), (1,)), ((), ())), preferred_element_type=F32) * XA_SCALE
            ex = jnp.exp(sc - jnp.max(sc, axis=-1, keepdims=True))
            pr = ex / jnp.sum(ex, axis=-1, keepdims=True)
            dpr = lax.dot_general(do, v, (((1,), (1,)), ((), ())), preferred_element_type=F32)
            dsc = (pr * (dpr - jnp.sum(dpr * pr, axis=-1, keepdims=True)) * XA_SCALE).astype(BF16)
            dq_ref[:, cols] = jnp.dot(dsc, k, preferred_element_type=F32).astype(dq_ref.dtype)
            dkv_ref[:, cols] += lax.dot_general(dsc, q, (((0,), (0,)), ((), ())), preferred_element_type=F32)
            dkv_ref[:, vcols] += lax.dot_general(pr.astype(BF16), do, (((0,), (0,)), ((), ())), preferred_element_type=F32)

    return pl.pallas_call(
        body, grid=(s // ts,), in_specs=[_rows(ts, xw), pl.BlockSpec((ts, xw), lambda i: (i, 1)), _fixed((m, 2 * xw))],
        out_specs=[_rows(ts, xw), _fixed((m, 2 * xw))],
        out_shape=[jax.ShapeDtypeStruct((s, xw), BF16), jax.ShapeDtypeStruct((m, 2 * xw), F32)],
        compiler_params=_params(("arbitrary",)), name=name,
    )(dxc, ppx, kv)


def _merge_fwd(pgt, ya, yb, yc, name):
    s = pgt.shape[0]
    ts = _tile(s, TS_ROW, 8)

    def body(gt_ref, ya_ref, yb_ref, yc_ref, o_ref):
        acc = _sigmoid(gt_ref[:, 0:D].astype(F32)) * ya_ref[...].astype(F32)
        acc = acc + _sigmoid(gt_ref[:, D:2 * D].astype(F32)) * yb_ref[...].astype(F32)
        acc = acc + _sigmoid(gt_ref[:, 2 * D:3 * D].astype(F32)) * yc_ref[...].astype(F32)
        o_ref[...] = acc.astype(o_ref.dtype)

    return pl.pallas_call(
        body, grid=(s // ts,), in_specs=[_rows(ts, 3 * D)] + [_rows(ts, D)] * 3, out_specs=_rows(ts, D),
        out_shape=jax.ShapeDtypeStruct((s, D), BF16), compiler_params=_params(("parallel",)), name=name,
    )(pgt, ya, yb, yc)


def _merge_bwd(dmerged, pgt, ya, yb, yc, name):
    s = pgt.shape[0]
    ts = _tile(s, TS_ROW, 8)

    def body(dm_ref, gt_ref, ya_ref, yb_ref, yc_ref, dya_ref, dyb_ref, dyc_ref, dgt_ref):
        dm = dm_ref[...].astype(F32)
        for j, (y_ref, dy_ref) in enumerate(((ya_ref, dya_ref), (yb_ref, dyb_ref), (yc_ref, dyc_ref))):
            sig = _sigmoid(gt_ref[:, j * D:(j + 1) * D].astype(F32))
            dy_ref[...] = (dm * sig).astype(dy_ref.dtype)
            dgt_ref[:, j * D:(j + 1) * D] = (dm * y_ref[...].astype(F32) * sig * (1.0 - sig)).astype(dgt_ref.dtype)

    return pl.pallas_call(
        body, grid=(s // ts,), in_specs=[_rows(ts, D), _rows(ts, 3 * D)] + [_rows(ts, D)] * 3,
        out_specs=[_rows(ts, D)] * 3 + [_rows(ts, 3 * D)],
        out_shape=[jax.ShapeDtypeStruct((s, D), BF16)] * 3 + [jax.ShapeDtypeStruct((s, 3 * D), BF16)],
        compiler_params=_params(("parallel",)), name=name,
    )(dmerged, pgt, ya, yb, yc)


def _adamw(w, g, m, v, name):
    r, c = w.shape
    tr = _tile(r, 256, 8)

    def body(w_ref, g_ref, m_ref, v_ref, d_ref, mo_ref, vo_ref):
        gv = g_ref[...]
        mn = ADAM_B1 * m_ref[...] + (1.0 - ADAM_B1) * gv
        vn = ADAM_B2 * v_ref[...] + (1.0 - ADAM_B2) * (gv * gv)
        m_hat = mn / (1.0 - ADAM_B1 ** ADAM_STEP)
        v_hat = vn / (1.0 - ADAM_B2 ** ADAM_STEP)
        d_ref[...] = -ADAM_LR * (m_hat / (jnp.sqrt(v_hat) + ADAM_EPS) + ADAM_WD * w_ref[...])
        mo_ref[...] = mn
        vo_ref[...] = vn

    return pl.pallas_call(
        body, grid=(r // tr,), in_specs=[_rows(tr, c)] * 4, out_specs=[_rows(tr, c)] * 3,
        out_shape=[jax.ShapeDtypeStruct((r, c), F32)] * 3, compiler_params=_params(("parallel",)), name=name,
    )(w, g, m, v)


ANY = pl.BlockSpec(memory_space=pl.ANY)


def _place():
    x, y, c = lax.axis_index("x"), lax.axis_index("y"), lax.axis_index("c")
    chips = [(1 - x, y), (x, 1 - y), (1 - x, 1 - y)]
    return x, y, c, chips


def _half(c, rows):
    h = rows // 2
    return pl.ds(pl.multiple_of(c * h, 8), h)


def _place_shard(shard, chip_arr, out_dtype, name):
    r, cols = shard.shape
    tr = _tile(r, 256, 16)

    def body(chip_ref, s_ref, o_ref):
        o_ref[...] = s_ref[...].astype(o_ref.dtype)

    return pl.pallas_call(
        body,
        grid_spec=pltpu.PrefetchScalarGridSpec(
            num_scalar_prefetch=1, grid=(r // tr,),
            in_specs=[pl.BlockSpec((tr, cols), lambda i, chip_ref: (i, 0))],
            out_specs=pl.BlockSpec((None, tr, cols), lambda i, chip_ref: (chip_ref[0], i, 0))),
        out_shape=jax.ShapeDtypeStruct((4, r, cols), out_dtype),
        compiler_params=_params(("parallel",)), name=name,
    )(chip_arr, shard)


def _gather_shards(bufs, name):
    n = len(bufs)

    def body(*refs):
        outs = refs[n:2 * n]
        send_ici, recv_ici, send_d2d, recv_d2d = refs[2 * n:]
        x, y, c, chips = _place()
        me = 2 * x + y
        sibling = (x, y, 1 - c)

        def ici(w, p, chip_of_block, to):
            rows = _half(c, outs[w].shape[1])
            block = outs[w].at[chip_of_block, rows]
            return pltpu.make_async_remote_copy(
                src_ref=block, dst_ref=block, send_sem=send_ici.at[w, p], recv_sem=recv_ici.at[w, p], device_id=to, device_id_type=MESH)

        def d2d(w, p, chip_of_block, half_of):
            rows = _half(half_of, outs[w].shape[1])
            block = outs[w].at[chip_of_block, rows]
            return pltpu.make_async_remote_copy(
                src_ref=block, dst_ref=block, send_sem=send_d2d.at[w, p], recv_sem=recv_d2d.at[w, p], device_id=sibling, device_id_type=MESH)

        sends = [ici(w, p, me, (*chip, c)) for p, chip in enumerate(chips) for w in range(n)]
        for cp in sends:
            cp.start()
        passed = []
        for p, (px, py) in enumerate(chips):
            for w in range(n):
                ici(w, p, 2 * px + py, (px, py, c)).wait_recv()
                fwd = d2d(w, p, 2 * px + py, c)
                fwd.start()
                passed.append(fwd)
        for p, (px, py) in enumerate(chips):
            for w in range(n):
                d2d(w, p, 2 * px + py, 1 - c).wait_recv()
        for cp in sends + passed:
            cp.wait_send()

    return pl.pallas_call(
        body, in_specs=[ANY] * n, out_specs=[ANY] * n,
        out_shape=[jax.ShapeDtypeStruct(a.shape, a.dtype) for a in bufs],
        input_output_aliases={w: w for w in range(n)},
        scratch_shapes=[pltpu.SemaphoreType.DMA((n, 3))] * 4,
        compiler_params=pltpu.CompilerParams(has_side_effects=True), name=name,
    )(*bufs)


def _pair_exchange(grads, name):
    n = len(grads)

    def body(*refs):
        ins, outs = refs[:n], refs[n:2 * n]
        send_sem, recv_sem = refs[2 * n:]
        x, y, c, _ = _place()
        copies = []
        for w in range(n):
            rows = _half(1 - c, ins[w].shape[1])
            copies.append(pltpu.make_async_remote_copy(
                src_ref=ins[w].at[:, rows], dst_ref=outs[w], send_sem=send_sem.at[w], recv_sem=recv_sem.at[w],
                device_id=(x, y, 1 - c), device_id_type=MESH))
        for cp in copies:
            cp.start()
        for cp in copies:
            cp.wait()

    return pl.pallas_call(
        body, in_specs=[ANY] * n, out_specs=[ANY] * n,
        out_shape=[jax.ShapeDtypeStruct((4, a.shape[1] // 2, a.shape[2]), a.dtype) for a in grads],
        scratch_shapes=[pltpu.SemaphoreType.DMA((n,))] * 2,
        compiler_params=pltpu.CompilerParams(has_side_effects=True), name=name,
    )(*grads)


def _pair_sum(g, got, c_arr, name):
    _, r, cols = g.shape
    h = r // 2
    th = _tile(h, 256, 16)
    nb = h // th

    def body(c_ref, g_ref, got_ref, o_ref):
        o_ref[...] = (g_ref[...].astype(F32) + got_ref[...].astype(F32)).astype(o_ref.dtype)

    return pl.pallas_call(
        body,
        grid_spec=pltpu.PrefetchScalarGridSpec(
            num_scalar_prefetch=1, grid=(4, nb),
            in_specs=[pl.BlockSpec((None, th, cols), lambda j, i, c_ref: (j, c_ref[0] * nb + i, 0)),
                      pl.BlockSpec((None, th, cols), lambda j, i, c_ref: (j, i, 0))],
            out_specs=pl.BlockSpec((None, th, cols), lambda j, i, c_ref: (j, i, 0))),
        out_shape=jax.ShapeDtypeStruct((4, h, cols), BF16),
        compiler_params=_params(("parallel", "parallel")), name=name,
    )(c_arr, g, got)


def _chip_exchange(parts, name):
    n = len(parts)

    def body(*refs):
        ins, outs = refs[:n], refs[n:2 * n]
        send_sem, recv_sem = refs[2 * n:]
        x, y, c, chips = _place()
        copies = []
        for p, (px, py) in enumerate(chips):
            for w in range(n):
                copies.append(pltpu.make_async_remote_copy(
                    src_ref=ins[w].at[2 * px + py], dst_ref=outs[w].at[p], send_sem=send_sem.at[w, p], recv_sem=recv_sem.at[w, p],
                    device_id=(px, py, c), device_id_type=MESH))
        for cp in copies:
            cp.start()
        for cp in copies:
            cp.wait()

    return pl.pallas_call(
        body, in_specs=[ANY] * n, out_specs=[ANY] * n,
        out_shape=[jax.ShapeDtypeStruct((3,) + a.shape[1:], a.dtype) for a in parts],
        scratch_shapes=[pltpu.SemaphoreType.DMA((n, 3))] * 2,
        compiler_params=pltpu.CompilerParams(has_side_effects=True), name=name,
    )(*parts)


def _chip_sum(part, got, place_arr, name):
    _, h, cols = part.shape
    th = _tile(h, 256, 16)
    nb = h // th

    def body(place_ref, p_ref, got_ref, o_ref):
        acc = p_ref[...].astype(F32)
        for p in range(3):
            acc = acc + got_ref[p].astype(F32)
        o_ref[...] = acc

    return pl.pallas_call(
        body,
        grid_spec=pltpu.PrefetchScalarGridSpec(
            num_scalar_prefetch=1, grid=(nb,),
            in_specs=[pl.BlockSpec((None, th, cols), lambda i, place_ref: (place_ref[0], i, 0)),
                      pl.BlockSpec((3, th, cols), lambda i, place_ref: (0, i, 0))],
            out_specs=pl.BlockSpec((th, cols), lambda i, place_ref: (place_ref[1] * nb + i, 0))),
        out_shape=jax.ShapeDtypeStruct((2 * h, cols), F32),
        compiler_params=_params(("parallel",)), name=name,
    )(place_arr, part, got)


def _pair_join(bufs, name):
    n = len(bufs)

    def body(*refs):
        outs = refs[n:2 * n]
        send_sem, recv_sem = refs[2 * n:]
        x, y, c, _ = _place()
        copies = []
        for w in range(n):
            block = outs[w].at[_half(c, outs[w].shape[0])]
            copies.append(pltpu.make_async_remote_copy(
                src_ref=block, dst_ref=block, send_sem=send_sem.at[w], recv_sem=recv_sem.at[w],
                device_id=(x, y, 1 - c), device_id_type=MESH))
        for cp in copies:
            cp.start()
        for w, cp in enumerate(copies):
            cp.wait_send()
            block = outs[w].at[_half(1 - c, outs[w].shape[0])]
            pltpu.make_async_remote_copy(
                src_ref=block, dst_ref=block, send_sem=send_sem.at[w], recv_sem=recv_sem.at[w],
                device_id=(x, y, 1 - c), device_id_type=MESH).wait_recv()

    return pl.pallas_call(
        body, in_specs=[ANY] * n, out_specs=[ANY] * n,
        out_shape=[jax.ShapeDtypeStruct(a.shape, a.dtype) for a in bufs],
        input_output_aliases={w: w for w in range(n)},
        scratch_shapes=[pltpu.SemaphoreType.DMA((n,))] * 2,
        compiler_params=pltpu.CompilerParams(has_side_effects=True), name=name,
    )(*bufs)


def _all_sum(pack, name):
    r, cols = pack.shape

    def body(x_ref, o_ref, all_ref, send_sems, recv_sems):
        x, y, c, chips = _place()
        me, sibling = (x, y, c), (x, y, 1 - c)

        def slot(px, py, pc):
            return all_ref.at[4 * px + 2 * py + pc]

        def copy(k, block, to, src=None):
            return pltpu.make_async_remote_copy(
                src_ref=slot(*block) if src is None else src, dst_ref=slot(*block),
                send_sem=send_sems.at[k], recv_sem=recv_sems.at[k], device_id=to, device_id_type=MESH)

        all_ref[4 * x + 2 * y + c] = x_ref[...]
        first = [copy(0, me, sibling, src=x_ref)]
        first += [copy(1 + j, me, (*chip, c), src=x_ref) for j, chip in enumerate(chips)]
        for cp in first:
            cp.start()
        passed = [copy(4 + j, (*chip, c), sibling) for j, chip in enumerate(chips)]
        for j, chip in enumerate(chips):
            copy(1 + j, (*chip, c), me).wait_recv()
            passed[j].start()
        copy(0, sibling, me).wait_recv()
        for j, chip in enumerate(chips):
            copy(4 + j, (*chip, 1 - c), me).wait_recv()
        for cp in first + passed:
            cp.wait_send()
        acc = all_ref[0]
        for k in range(1, 8):
            acc = acc + all_ref[k]
        o_ref[...] = acc

    return pl.pallas_call(
        body, in_specs=[pl.BlockSpec(memory_space=pltpu.VMEM)], out_specs=pl.BlockSpec(memory_space=pltpu.VMEM),
        out_shape=jax.ShapeDtypeStruct((r, cols), F32),
        scratch_shapes=[pltpu.VMEM((8, r, cols), F32), pltpu.SemaphoreType.DMA((7,)), pltpu.SemaphoreType.DMA((7,))],
        compiler_params=pltpu.CompilerParams(has_side_effects=True, vmem_limit_bytes=VMEM_LIMIT), name=name,
    )(pack)


def _ffn_fwd(x_norm, w_in, w_out, tag):
    ab = _mm(x_norm, w_in, out_dtype=BF16, tn=1408, name=tag + "_in")
    u = _swiglu_fwd(ab, name=tag + "_swiglu")
    f = _mm(u, w_out, tk=DFF, name=tag + "_out")
    return ab, u, f


def _ffn_bwd(dz, x_norm, ab, u, w_in, w_out, tag):
    du = _mm(dz, w_out, tb=True, out_dtype=BF16, tn=1408, name=tag + "_out_dx")
    dw_out = _mm(u, dz, ta=True, out_dtype=BF16, tm=1408, tk=512, name=tag + "_out_dw")
    dab = _swiglu_bwd(ab, du, name=tag + "_swiglu_bwd")
    dxn = _mm(dab, w_in, tb=True, tk=1408, name=tag + "_in_dx")
    dw_in = _mm(x_norm, dab, ta=True, out_dtype=BF16, tm=1024, tk=512, shards=4, name=tag + "_in_dw")
    return dxn, dw_in, dw_out


def _local_step(x, mem, target, small, big):
    h1 = _norm_fwd(x, small["ffn1_pre_g"], BF16, name="ffn1_pre")
    ab1, u1, f1 = _ffn_fwd(h1, big["ffn1_w_in"], big["ffn1_w_out"], "ffn1")
    x1, h = _resid_norm_fwd(x, f1, small["ffn1_post_g"], 0.5, small["mix_pre_g"], name="ffn1_post")
    pg = _mm(h, big["w_gla"], out_dtype=BF16, tn=PG_W, name="mix_in_gla")
    ppx = _mm(h, big["w_px"], out_dtype=BF16, name="mix_in_px")
    pgt = _mm(h, big["w_gates"], out_dtype=BF16, name="mix_in_gates")
    mem_n = _norm_fwd(mem, small["mem_norm_g"], BF16, name="mem_norm")
    kv = _mm(mem_n, big["w_mem_kv"], out_dtype=BF16, name="mem_kv")
    ya_in, sp = _gla_fwd(pg, small["w_fu_pad"], small["b_f"], small["gla_norm_g"], name="gla_fwd")
    yb_in = _pool_fwd(ppx, small["w_pool_b"], small["pool_scale"], name="pool_fwd")
    xc = _xattn_fwd(ppx, kv, name="xattn_fwd")
    ya = _mm(ya_in, big["w_up_gla"], out_dtype=BF16, name="up_gla")
    yb = _mm(yb_in, big["w_up_pool"], out_dtype=BF16, name="up_pool")
    yc = _mm(xc, big["w_up_xattn"], out_dtype=BF16, name="up_xattn")
    merged = _merge_fwd(pgt, ya, yb, yc, name="merge_fwd")
    ymix = _mm(merged, big["w_o"], name="mix_out")
    x2, h2 = _resid_norm_fwd(x1, ymix, small["mix_post_g"], 1.0, small["ffn2_pre_g"], name="mix_post")
    ab2, u2, f2 = _ffn_fwd(h2, big["ffn2_w_in"], big["ffn2_w_out"], "ffn2")
    x3, _ = _resid_norm_fwd(x2, f2, small["ffn2_post_g"], 0.5, None, name="ffn2_post")
    gs, gb = {}, {}
    dx3, gs["final_g"], loss = _loss_bwd(x3, small["final_g"], target, name="loss")
    dz2, gs["ffn2_post_g"] = _rms_bwd(f2, small["ffn2_post_g"], [dx3], None, 0.5, BF16, name="ffn2_post_bwd")
    dh2, gb["ffn2_w_in"], gb["ffn2_w_out"] = _ffn_bwd(dz2, h2, ab2, u2, big["ffn2_w_in"], big["ffn2_w_out"], "ffn2")
    dx2, gs["ffn2_pre_g"] = _rms_bwd(x2, small["ffn2_pre_g"], [dh2], dx3, 1.0, F32, name="ffn2_pre_bwd")
    dy, gs["mix_post_g"] = _rms_bwd(ymix, small["mix_post_g"], [dx2], None, 1.0, BF16, name="mix_post_bwd")
    dmerged = _mm(dy, big["w_o"], tb=True, out_dtype=BF16, name="mix_out_dx")
    gb["w_o"] = _mm(merged, dy, ta=True, out_dtype=BF16, tk=512, name="mix_out_dw")
    dya, dyb, dyc, dgt = _merge_bwd(dmerged, pgt, ya, yb, yc, name="merge_bwd")
    dya_in = _mm(dya, big["w_up_gla"], tb=True, out_dtype=BF16, name="up_gla_dx")
    gb["w_up_gla"] = _mm(ya_in, dya, ta=True, out_dtype=BF16, tk=512, name="up_gla_dw")
    dyb_in = _mm(dyb, big["w_up_pool"], tb=True, out_dtype=BF16, name="up_pool_dx")
    gb["w_up_pool"] = _mm(yb_in, dyb, ta=True, out_dtype=BF16, tk=512, shards=4, name="up_pool_dw")
    dxc = _mm(dyc, big["w_up_xattn"], tb=True, out_dtype=BF16, name="up_xattn_dx")
    gb["w_up_xattn"] = _mm(xc, dyc, ta=True, out_dtype=BF16, tk=512, shards=4, name="up_xattn_dw")
    dpg, gs["w_fu_pad"], gs["b_f"], gs["gla_norm_g"] = _gla_bwd(pg, sp, dya_in, small["w_fu_pad"], small["b_f"], small["gla_norm_g"], name="gla_bwd")
    dp, gs["w_pool"], gs["pool_scale"] = _pool_bwd(dyb_in, ppx, small["w_pool_b"], small["pool_scale"], name="pool_bwd")
    dxq, dkv = _xattn_bwd(dxc, ppx, kv, name="xattn_bwd")
    dkv = dkv.astype(BF16)
    gb["w_mem_kv"] = _mm(mem_n, dkv, ta=True, out_dtype=BF16, name="mem_kv_dw")
    dmem_n = _mm(dkv, big["w_mem_kv"], tb=True, name="mem_kv_dx")
    _, gs["mem_norm_g"] = _rms_bwd(mem, small["mem_norm_g"], [dmem_n], None, 1.0, BF16, name="mem_norm_bwd")
    dh_parts = [
        _mm(dpg, big["w_gla"], tb=True, tk=PG_W, name="mix_in_gla_dx"),
        _mm(dp, big["w_p"], tb=True, name="mix_in_p_dx"),
        _mm(dxq, big["w_xq"], tb=True, name="mix_in_xq_dx"),
        _mm(dgt, big["w_gates"], tb=True, tk=1536, name="mix_in_gates_dx"),
    ]
    gb["w_gla"] = _mm(h, dpg, ta=True, out_dtype=BF16, tk=512, tn=PG_W, name="mix_in_gla_dw")
    gb["w_p"] = _mm(h, dp, ta=True, out_dtype=BF16, tk=512, name="mix_in_p_dw")
    gb["w_xq"] = _mm(h, dxq, ta=True, out_dtype=BF16, tk=512, name="mix_in_xq_dw")
    gb["w_gates"] = _mm(h, dgt, ta=True, out_dtype=BF16, tk=512, tn=1536, name="mix_in_gates_dw")
    dx1, gs["mix_pre_g"] = _rms_bwd(x1, small["mix_pre_g"], dh_parts, dx2, 1.0, F32, name="mix_pre_bwd")
    dz1, gs["ffn1_post_g"] = _rms_bwd(f1, small["ffn1_post_g"], [dx1], None, 0.5, BF16, name="ffn1_post_bwd")
    dh1, gb["ffn1_w_in"], gb["ffn1_w_out"] = _ffn_bwd(dz1, h1, ab1, u1, big["ffn1_w_in"], big["ffn1_w_out"], "ffn1")
    dx0, gs["ffn1_pre_g"] = _rms_bwd(x, small["ffn1_pre_g"], [dh1], dx1, 1.0, F32, name="ffn1_pre_bwd")
    return loss, dx0, gs, gb


BIG = ("ffn1_w_in", "ffn1_w_out", "w_in", "w_mem_kv", "w_up_gla", "w_up_pool", "w_up_xattn", "w_o", "ffn2_w_in", "ffn2_w_out")
COL_SHARDED = ("ffn1_w_in", "w_in", "w_up_pool", "w_up_xattn", "ffn2_w_in")
GAINS = ("ffn1_pre_g", "ffn1_post_g", "mix_pre_g", "gla_norm_g", "mem_norm_g", "mix_post_g", "ffn2_pre_g", "ffn2_post_g", "final_g")
WEIGHTS = ("ffn1_pre_g", "ffn1_w_in", "ffn1_w_out", "ffn1_post_g", "mix_pre_g", "w_in", "w_fu", "b_f", "gla_norm_g", "w_pool",
           "pool_scale", "mem_norm_g", "w_mem_kv", "w_up_gla", "w_up_pool", "w_up_xattn", "w_o", "mix_post_g", "ffn2_pre_g",
           "ffn2_w_in", "ffn2_w_out", "ffn2_post_g", "final_g")
IN_GLA, IN_F, IN_PX, IN_GATES, IN_END = 0, 3072, 3088, 4112, 7184
PACK_ROWS = 96


def _cols_from_shards(g):
    return jnp.transpose(g, (1, 0, 2)).reshape(g.shape[1], 4 * g.shape[2])


def _pack_small(t):
    single = jnp.zeros((PACK_ROWS - 72, D), F32)
    for i, n in enumerate(GAINS):
        single = single.at[i].set(t[n].reshape(D))
    k = len(GAINS)
    single = single.at[k, 0:512].set(t["b_f"].reshape(512))
    single = single.at[k, 512:1024].set(t["pool_scale"].reshape(512))
    return jnp.concatenate([t["w_pool"].reshape(64, D), t["w_fu"].reshape(8, D), single], axis=0)


def _unpack_small(p):
    out = {n: p[72 + i:73 + i] for i, n in enumerate(GAINS)}
    k = 72 + len(GAINS)
    out["b_f"] = p[k:k + 1, 0:512]
    out["pool_scale"] = p[k:k + 1, 512:1024]
    out["w_pool"] = p[0:64].reshape(4, LANE, LANE)
    out["w_fu"] = p[64:72].reshape(GATE_RANK, 512)
    return out


def kernel(x, mem, ffn1_pre_g, ffn1_w_in, ffn1_w_out, ffn1_post_g, mix_pre_g, w_in, w_fu, b_f, gla_norm_g, w_pool, pool_scale, mem_norm_g, w_mem_kv, w_up_gla, w_up_pool, w_up_xattn, w_o, mix_post_g, ffn2_pre_g, ffn2_w_in, ffn2_w_out, ffn2_post_g, final_g, loss_target, m_ffn1_pre_g, m_ffn1_w_in, m_ffn1_w_out, m_ffn1_post_g, m_mix_pre_g, m_w_in, m_w_fu, m_b_f, m_gla_norm_g, m_w_pool, m_pool_scale, m_mem_norm_g, m_w_mem_kv, m_w_up_gla, m_w_up_pool, m_w_up_xattn, m_w_o, m_mix_post_g, m_ffn2_pre_g, m_ffn2_w_in, m_ffn2_w_out, m_ffn2_post_g, m_final_g, v_ffn1_pre_g, v_ffn1_w_in, v_ffn1_w_out, v_ffn1_post_g, v_mix_pre_g, v_w_in, v_w_fu, v_b_f, v_gla_norm_g, v_w_pool, v_pool_scale, v_mem_norm_g, v_w_mem_kv, v_w_up_gla, v_w_up_pool, v_w_up_xattn, v_w_o, v_mix_post_g, v_ffn2_pre_g, v_ffn2_w_in, v_ffn2_w_out, v_ffn2_post_g, v_final_g):
    args = dict(locals())
    w = {n: args[n][0] for n in WEIGHTS}
    m = {n: args["m_" + n][0] for n in WEIGHTS}
    v = {n: args["v_" + n][0] for n in WEIGHTS}
    xi, yi, ci = lax.axis_index("x"), lax.axis_index("y"), lax.axis_index("c")
    chip = 2 * xi + yi

    c_arr = jnp.reshape(ci, (1,)).astype(jnp.int32)
    chip_arr = jnp.reshape(chip, (1,)).astype(jnp.int32)
    place_arr = jnp.stack([chip, ci]).astype(jnp.int32)
    placed = [_place_shard(w[n], chip_arr, BF16, name="place_" + n) for n in BIG]
    placed.append(_place_shard(w["w_fu"], chip_arr, F32, name="place_w_fu"))
    gathered = _gather_shards(placed, name="gather_weights")
    full = {}
    for n, g in zip(BIG, gathered[:-1]):
        full[n] = _cols_from_shards(g) if n in COL_SHARDED else g.reshape(4 * g.shape[1], g.shape[2])
    w_fu_full = _cols_from_shards(gathered[-1])
    big = {n: full[n] for n in BIG if n != "w_in"}
    wi = full["w_in"]
    big["w_gla"] = jnp.concatenate([wi[:, IN_GLA:IN_PX], jnp.zeros((D, PG_W - IN_PX), BF16)], axis=1)
    big["w_px"] = wi[:, IN_PX:IN_GATES]
    big["w_p"] = wi[:, IN_PX:IN_PX + 512]
    big["w_xq"] = wi[:, IN_PX + 512:IN_GATES]
    big["w_gates"] = wi[:, IN_GATES:IN_END]
    small = {n: w[n].reshape(1, D) for n in GAINS}
    small["b_f"] = w["b_f"].reshape(1, 512)
    small["pool_scale"] = w["pool_scale"].reshape(1, 512)
    small["w_pool_b"] = w["w_pool"].astype(BF16)
    small["w_fu_pad"] = jnp.concatenate([w_fu_full, jnp.zeros((LANE - GATE_RANK, 512), F32)], axis=0).astype(BF16)

    loss, grad_x, gs, gb = _local_step(x[0], mem[0], loss_target[0], small, big)
    loss = lax.psum(loss[0, 0], ("x", "y", "c"))

    gs["w_fu"] = gs.pop("w_fu_pad")[0:GATE_RANK]
    small_sum = _unpack_small(_all_sum(_pack_small(gs), name="sum_small_grads"))
    dwi = jnp.concatenate([gb.pop("w_gla")[:, 0:IN_PX], gb.pop("w_p"), gb.pop("w_xq"), gb.pop("w_gates")], axis=1)
    gb["w_in"] = jnp.transpose(dwi.reshape(D, 4, IN_END // 4), (1, 0, 2))
    for n in BIG:
        if n not in COL_SHARDED:
            gb[n] = gb[n].reshape(4, gb[n].shape[0] // 4, gb[n].shape[1])
    contrib = [gb[n] for n in BIG]
    from_sibling = _pair_exchange(contrib, name="grads_pair_exchange")
    pair = [_pair_sum(g, got, c_arr, name="grads_pair_sum_" + n) for n, g, got in zip(BIG, contrib, from_sibling)]
    from_chips = _chip_exchange(pair, name="grads_chip_exchange")
    halves = [_chip_sum(p, got, place_arr, name="grads_chip_sum_" + n) for n, p, got in zip(BIG, pair, from_chips)]
    reduced = dict(zip(BIG, _pair_join(halves, name="grads_pair_join")))

    grads, delta, new_m, new_v = {}, {}, {}, {}
    for n in BIG:
        grads[n] = reduced[n]
        delta[n], new_m[n], new_v[n] = _adamw(w[n], reduced[n], m[n], v[n], name="adamw_" + n)
    small_names = GAINS + ("b_f", "pool_scale", "w_pool")
    w_fu_grad = lax.dynamic_slice_in_dim(small_sum["w_fu"], chip * LANE, LANE, axis=1)
    packs = []
    for t in (w, m, v):
        t = dict(t)
        t["w_fu"] = jnp.zeros((GATE_RANK, 512), F32)
        packs.append(_pack_small(t))
    sd, sm, sv = (_unpack_small(p) for p in _adamw(packs[0], _pack_small(small_sum), packs[1], packs[2], name="adamw_small"))
    for n in small_names:
        shape = w[n].shape
        grads[n] = small_sum[n].reshape(shape)
        delta[n], new_m[n], new_v[n] = sd[n].reshape(shape), sm[n].reshape(shape), sv[n].reshape(shape)
    grads["w_fu"] = w_fu_grad
    delta["w_fu"], new_m["w_fu"], new_v["w_fu"] = _adamw(w["w_fu"], w_fu_grad, m["w_fu"], v["w_fu"], name="adamw_w_fu")

    outs = [loss, grad_x[None]]
    for group in (grads, delta, new_m, new_v):
        outs += [group[n][None] for n in WEIGHTS]
    return tuple(outs)
```

```python
import functools

import jax
import jax.numpy as jnp
from jax import lax
from jax.experimental import pallas as pl
from jax.experimental.pallas import tpu as pltpu

F32 = jnp.float32
BF16 = jnp.bfloat16
MESH = pl.DeviceIdType.MESH
HIGHEST = lax.Precision.HIGHEST

D = 1024
DFF = 2816
CHUNK = 64
HEADS = 4
HDK = 128
HDV = 256
GATE_TEMP = 16.0
POOL_WINDOWS = (2, 4, 8, 16)
POOL_HALO = 16
XA_HEADS = 4
XA_HD = 128
EPS = 1e-6
Q_SCALE = HDK ** -0.5
XA_SCALE = XA_HD ** -0.5
PG_Q, PG_K, PG_V, PG_G, PG_F, PG_W = 0, 512, 1024, 2048, 3072, 3200
GATE_RANK = 16
ADAM_LR, ADAM_B1, ADAM_B2, ADAM_EPS, ADAM_WD, ADAM_STEP = 0.001, 0.9, 0.999, 1e-08, 0.01, 10

VMEM_LIMIT = 48 * 1024 * 1024
LANE = 128
TS_ROW = 256
TS_GLA = 512
TS_POOL = 512
TS_XA = 512


def _params(sem):
    return pltpu.CompilerParams(dimension_semantics=sem, vmem_limit_bytes=VMEM_LIMIT)


def _tile(n, cap, unit=LANE):
    if n <= cap:
        return n
    best = None
    for t in range(unit, cap + 1, unit):
        if n % t == 0:
            best = t
    assert best is not None, (n, cap)
    return best


def _sigmoid(x):
    return 1.0 / (1.0 + jnp.exp(-x))


def _log_sigmoid(x):
    return jnp.minimum(x, 0.0) - jnp.log(1.0 + jnp.exp(-jnp.abs(x)))


def _rms(x):
    r = lax.rsqrt(jnp.mean(x * x, axis=-1, keepdims=True) + EPS)
    return x * r, r


def _rows(ts, w):
    return pl.BlockSpec((ts, w), lambda i: (i, 0))


def _fixed(shape):
    nd = len(shape)
    return pl.BlockSpec(shape, lambda i: (0,) * nd)


def _mm(a, b, *, ta=False, tb=False, out_dtype=F32, tm=1024, tn=1024, tk=1024, shards=1, name):
    m, kdim = (a.shape[1], a.shape[0]) if ta else a.shape
    n = b.shape[0] if tb else b.shape[1]
    assert (b.shape[1] if tb else b.shape[0]) == kdim, (a.shape, b.shape, ta, tb)
    tm = _tile(m, tm)
    tn = n // shards if shards > 1 else _tile(n, tn)
    tk = _tile(kdim, tk)
    nk = kdim // tk
    dims = (((0 if ta else 1,), (1 if tb else 0,)), ((), ()))

    def body(a_ref, b_ref, o_ref, *acc):
        part = lax.dot_general(a_ref[...], b_ref[...], dims, preferred_element_type=F32)
        if nk == 1:
            o_ref[...] = part.astype(o_ref.dtype)
            return
        acc_ref, = acc
        k = pl.program_id(2)

        @pl.when(k == 0)
        def _():
            acc_ref[...] = part

        @pl.when(k > 0)
        def _():
            acc_ref[...] += part

        @pl.when(k == nk - 1)
        def _():
            o_ref[...] = acc_ref[...].astype(o_ref.dtype)

    a_spec = pl.BlockSpec((tk, tm), lambda i, j, k: (k, i)) if ta else pl.BlockSpec((tm, tk), lambda i, j, k: (i, k))
    b_spec = pl.BlockSpec((tn, tk), lambda i, j, k: (j, k)) if tb else pl.BlockSpec((tk, tn), lambda i, j, k: (k, j))
    if shards > 1:
        out_shape = jax.ShapeDtypeStruct((shards, m, tn), out_dtype)
        o_spec = pl.BlockSpec((None, tm, tn), lambda i, j, k: (j, i, 0))
    else:
        out_shape = jax.ShapeDtypeStruct((m, n), out_dtype)
        o_spec = pl.BlockSpec((tm, tn), lambda i, j, k: (i, j))
    return pl.pallas_call(
        body, grid=(m // tm, n // tn, nk), in_specs=[a_spec, b_spec], out_specs=o_spec, out_shape=out_shape,
        scratch_shapes=[pltpu.VMEM((tm, tn), F32)] if nk > 1 else [],
        compiler_params=_params(("parallel", "parallel", "arbitrary")), name=name,
    )(a, b)


def _norm_fwd(x, g, out_dtype, name):
    s, d = x.shape
    ts = _tile(s, TS_ROW, 8)

    def body(x_ref, g_ref, o_ref):
        xh, _ = _rms(x_ref[...])
        o_ref[...] = (xh * g_ref[...]).astype(o_ref.dtype)

    return pl.pallas_call(
        body, grid=(s // ts,), in_specs=[_rows(ts, d), _fixed((1, d))], out_specs=_rows(ts, d),
        out_shape=jax.ShapeDtypeStruct((s, d), out_dtype), compiler_params=_params(("parallel",)), name=name,
    )(x, g)


def _resid_norm_fwd(x, f, g_post, alpha, g_next, name):
    s, d = x.shape
    ts = _tile(s, TS_ROW, 8)
    with_h = g_next is not None

    def body(x_ref, f_ref, gp_ref, *rest):
        fh, _ = _rms(f_ref[...])
        xn = x_ref[...] + alpha * (fh * gp_ref[...])
        if with_h:
            gn_ref, xo_ref, h_ref = rest
            xh, _ = _rms(xn)
            h_ref[...] = (xh * gn_ref[...]).astype(h_ref.dtype)
        else:
            xo_ref, = rest
        xo_ref[...] = xn

    ins = [x, f, g_post] + ([g_next] if with_h else [])
    in_specs = [_rows(ts, d), _rows(ts, d), _fixed((1, d))] + ([_fixed((1, d))] if with_h else [])
    out_shape = [jax.ShapeDtypeStruct((s, d), F32)] + ([jax.ShapeDtypeStruct((s, d), BF16)] if with_h else [])
    out_specs = [_rows(ts, d)] + ([_rows(ts, d)] if with_h else [])
    out = pl.pallas_call(
        body, grid=(s // ts,), in_specs=in_specs, out_specs=out_specs, out_shape=out_shape,
        compiler_params=_params(("parallel",)), name=name,
    )(*ins)
    return (out[0], out[1]) if with_h else (out[0], None)


def _rms_bwd(x, g, dys, dres, alpha, out_dtype, name):
    s, d = x.shape
    ts = _tile(s, TS_ROW, 8)
    ndy = len(dys)
    with_res = dres is not None

    def body(x_ref, g_ref, *rest):
        dy_refs = rest[:ndy]
        rest = rest[ndy:]
        if with_res:
            dres_ref, dx_ref, dg_ref = rest
        else:
            dx_ref, dg_ref = rest
        xh, r = _rms(x_ref[...])
        dy = dy_refs[0][...].astype(F32)
        for ref in dy_refs[1:]:
            dy = dy + ref[...].astype(F32)
        dy = dy * alpha

        @pl.when(pl.program_id(0) == 0)
        def _():
            dg_ref[...] = jnp.zeros_like(dg_ref)

        dg_ref[...] += jnp.sum(dy * xh, axis=0, keepdims=True)
        dyg = dy * g_ref[...]
        dx = r * (dyg - xh * jnp.mean(dyg * xh, axis=-1, keepdims=True))
        if with_res:
            dx = dx + dres_ref[...]
        dx_ref[...] = dx.astype(dx_ref.dtype)

    ins = [x, g] + list(dys) + ([dres] if with_res else [])
    in_specs = [_rows(ts, d), _fixed((1, d))] + [_rows(ts, d)] * (ndy + int(with_res))
    return pl.pallas_call(
        body, grid=(s // ts,), in_specs=in_specs, out_specs=[_rows(ts, d), _fixed((1, d))],
        out_shape=[jax.ShapeDtypeStruct((s, d), out_dtype), jax.ShapeDtypeStruct((1, d), F32)],
        compiler_params=_params(("arbitrary",)), name=name,
    )(*ins)


def _loss_bwd(x, g, target, name):
    s, d = x.shape
    ts = _tile(s, TS_ROW, 8)

    def body(x_ref, g_ref, t_ref, dx_ref, dg_ref, loss_ref):
        xh, r = _rms(x_ref[...])
        gv = g_ref[...]
        diff = xh * gv - t_ref[...]

        @pl.when(pl.program_id(0) == 0)
        def _():
            dg_ref[...] = jnp.zeros_like(dg_ref)
            loss_ref[...] = jnp.zeros_like(loss_ref)

        sq = jnp.sum(diff * diff, axis=1, keepdims=True)
        loss_ref[...] += (0.5 / d) * jnp.sum(sq, axis=0, keepdims=True)
        dy = diff * (1.0 / d)
        dg_ref[...] += jnp.sum(dy * xh, axis=0, keepdims=True)
        dyg = dy * gv
        dx_ref[...] = r * (dyg - xh * jnp.mean(dyg * xh, axis=-1, keepdims=True))

    return pl.pallas_call(
        body, grid=(s // ts,), in_specs=[_rows(ts, d), _fixed((1, d)), _rows(ts, d)],
        out_specs=[_rows(ts, d), _fixed((1, d)), _fixed((8, LANE))],
        out_shape=[jax.ShapeDtypeStruct((s, d), F32), jax.ShapeDtypeStruct((1, d), F32), jax.ShapeDtypeStruct((8, LANE), F32)],
        compiler_params=_params(("arbitrary",)), name=name,
    )(x, g, target)


def _swiglu_fwd(ab, name):
    s = ab.shape[0]
    ts = _tile(s, TS_ROW, 8)

    def body(a_ref, b_ref, u_ref):
        a = a_ref[...].astype(F32)
        u_ref[...] = (a * _sigmoid(a) * b_ref[...].astype(F32)).astype(u_ref.dtype)

    return pl.pallas_call(
        body, grid=(s // ts,),
        in_specs=[pl.BlockSpec((ts, DFF), lambda i: (i, 0)), pl.BlockSpec((ts, DFF), lambda i: (i, 1))],
        out_specs=_rows(ts, DFF), out_shape=jax.ShapeDtypeStruct((s, DFF), BF16),
        compiler_params=_params(("parallel",)), name=name,
    )(ab, ab)


def _swiglu_bwd(ab, du, name):
    s = ab.shape[0]
    ts = _tile(s, TS_ROW, 8)

    def body(a_ref, b_ref, du_ref, dab_ref):
        a = a_ref[...].astype(F32)
        b = b_ref[...].astype(F32)
        dy = du_ref[...].astype(F32)
        sig = _sigmoid(a)
        dab_ref[:, 0:DFF] = (dy * b * (sig * (1.0 + a * (1.0 - sig)))).astype(dab_ref.dtype)
        dab_ref[:, DFF:2 * DFF] = (dy * a * sig).astype(dab_ref.dtype)

    return pl.pallas_call(
        body, grid=(s // ts,),
        in_specs=[pl.BlockSpec((ts, DFF), lambda i: (i, 0)), pl.BlockSpec((ts, DFF), lambda i: (i, 1)), _rows(ts, DFF)],
        out_specs=_rows(ts, 2 * DFF), out_shape=jax.ShapeDtypeStruct((s, 2 * DFF), BF16),
        compiler_params=_params(("parallel",)), name=name,
    )(ab, ab, du)


def _tri(strict):
    r = lax.broadcasted_iota(jnp.int32, (CHUNK, CHUNK), 0)
    c = lax.broadcasted_iota(jnp.int32, (CHUNK, CHUNK), 1)
    return (r > c).astype(F32) if strict else (r >= c).astype(F32)


def _gla_fwd(pg, wfu, b_f, gnorm, name):
    s = pg.shape[0]
    ts = _tile(s, TS_GLA, CHUNK)
    cpb = ts // CHUNK
    nc = s // CHUNK

    def body(pg_ref, wfu_ref, bf_ref, gn_ref, ya_ref, sp_ref, st_ref, la_ref, dec_ref, u_ref):
        @pl.when(pl.program_id(0) == 0)
        def _():
            st_ref[...] = jnp.zeros_like(st_ref)

        f = jnp.dot(pg_ref[:, PG_F:PG_W], wfu_ref[...], preferred_element_type=F32) + bf_ref[...]
        la_ref[...] = _log_sigmoid(f) * (1.0 / GATE_TEMP)
        tri = _tri(False)
        chunks = [slice(ci * CHUNK, (ci + 1) * CHUNK) for ci in range(cpb)]
        for ci, rows in enumerate(chunks):
            la = la_ref[rows, :]
            b = jnp.dot(tri, la, precision=HIGHEST, preferred_element_type=F32)
            bend = jnp.sum(la, axis=0, keepdims=True)
            e = jnp.exp(bend - b)
            dec_ref[ci:ci + 1, :] = jnp.exp(bend)
            for hd in range(HEADS):
                k = pg_ref[rows, PG_K + hd * HDK:PG_K + (hd + 1) * HDK]
                v = pg_ref[rows, PG_V + hd * HDV:PG_V + (hd + 1) * HDV]
                kt = (k.astype(F32) * e[:, hd * HDK:(hd + 1) * HDK]).astype(BF16)
                u_ref[ci, hd] = lax.dot_general(v, kt, (((0,), (0,)), ((), ())), preferred_element_type=F32)
        for ci in range(cpb):
            for hd in range(HEADS):
                prev = st_ref[hd]
                sp_ref[ci, hd] = prev
                st = prev * dec_ref[ci:ci + 1, hd * HDK:(hd + 1) * HDK] + u_ref[ci, hd]
                st_ref[hd] = st
                u_ref[ci, hd] = st
        for ci, rows in enumerate(chunks):
            for hd in range(HEADS):
                vc = slice(hd * HDV, (hd + 1) * HDV)
                q = pg_ref[rows, PG_Q + hd * HDK:PG_Q + (hd + 1) * HDK]
                go = pg_ref[rows, PG_G + hd * HDV:PG_G + (hd + 1) * HDV].astype(F32)
                qs = (q.astype(F32) * Q_SCALE).astype(BF16)
                o = lax.dot_general(qs, u_ref[ci, hd].astype(BF16), (((1,), (1,)), ((), ())), preferred_element_type=F32)
                oh, _ = _rms(o)
                ya_ref[rows, vc] = (oh * gn_ref[:, vc] * (go * _sigmoid(go))).astype(ya_ref.dtype)

    return pl.pallas_call(
        body, grid=(s // ts,),
        in_specs=[_rows(ts, PG_W), _fixed((LANE, HEADS * HDK)), _fixed((1, HEADS * HDK)), _fixed((1, HEADS * HDV))],
        out_specs=[_rows(ts, HEADS * HDV), pl.BlockSpec((cpb, HEADS, HDV, HDK), lambda i: (i, 0, 0, 0))],
        out_shape=[jax.ShapeDtypeStruct((s, HEADS * HDV), BF16), jax.ShapeDtypeStruct((nc, HEADS, HDV, HDK), F32)],
        scratch_shapes=[pltpu.VMEM((HEADS, HDV, HDK), F32), pltpu.VMEM((ts, HEADS * HDK), F32),
                        pltpu.VMEM((max(cpb, 8), HEADS * HDK), F32), pltpu.VMEM((cpb, HEADS, HDV, HDK), F32)],
        compiler_params=_params(("arbitrary",)), name=name,
    )(pg, wfu, b_f, gnorm)


def _gla_bwd(pg, sp, dya, wfu, b_f, gnorm, name):
    s = pg.shape[0]
    ts = _tile(s, TS_GLA, CHUNK)
    cpb = ts // CHUNK
    nblk = s // ts

    def body(pg_ref, sp_ref, dya_ref, wfu_ref, bf_ref, gn_ref, dpg_ref, dwfu_ref, dbf_ref, dgn_ref,
             dst_ref, la_ref, sg_ref, df_ref, e_ref, ktf_ref, dec_ref, g_ref):
        @pl.when(pl.program_id(0) == 0)
        def _():
            dst_ref[...] = jnp.zeros_like(dst_ref)
            dwfu_ref[...] = jnp.zeros_like(dwfu_ref)
            dbf_ref[...] = jnp.zeros_like(dbf_ref)
            dgn_ref[...] = jnp.zeros_like(dgn_ref)

        flow = pg_ref[:, PG_F:PG_W]
        f = jnp.dot(flow, wfu_ref[...], preferred_element_type=F32) + bf_ref[...]
        la_ref[...] = _log_sigmoid(f) * (1.0 / GATE_TEMP)
        sg_ref[...] = _sigmoid(-f) * (1.0 / GATE_TEMP)
        tri = _tri(False)
        tri_strict = _tri(True)
        chunks = [slice(ci * CHUNK, (ci + 1) * CHUNK) for ci in range(cpb)]
        for ci, rows in enumerate(chunks):
            la = la_ref[rows, :]
            b = jnp.dot(tri, la, precision=HIGHEST, preferred_element_type=F32)
            bend = jnp.sum(la, axis=0, keepdims=True)
            e = jnp.exp(bend - b)
            e_ref[rows, :] = e
            dec = jnp.exp(bend)
            dec_ref[ci:ci + 1, :] = dec
            for hd in range(HEADS):
                kc = slice(hd * HDK, (hd + 1) * HDK)
                vc = slice(hd * HDV, (hd + 1) * HDV)
                q = pg_ref[rows, PG_Q + hd * HDK:PG_Q + (hd + 1) * HDK]
                k = pg_ref[rows, PG_K + hd * HDK:PG_K + (hd + 1) * HDK]
                v = pg_ref[rows, PG_V + hd * HDV:PG_V + (hd + 1) * HDV]
                go = pg_ref[rows, PG_G + hd * HDV:PG_G + (hd + 1) * HDV].astype(F32)
                ktf = k.astype(F32) * e[:, kc]
                ktf_ref[rows, kc] = ktf
                st = sp_ref[ci, hd] * dec[:, kc] + lax.dot_general(v, ktf.astype(BF16), (((0,), (0,)), ((), ())), preferred_element_type=F32)
                st_b = st.astype(BF16)
                qs = (q.astype(F32) * Q_SCALE).astype(BF16)
                o = lax.dot_general(qs, st_b, (((1,), (1,)), ((), ())), preferred_element_type=F32)
                oh, r = _rms(o)
                gh = gn_ref[:, vc]
                sig = _sigmoid(go)
                dy = dya_ref[rows, vc].astype(F32)
                don = dy * (go * sig)
                dgn_ref[:, vc] += jnp.sum(don * oh, axis=0, keepdims=True)
                dong = don * gh
                do = (r * (dong - oh * jnp.mean(dong * oh, axis=-1, keepdims=True))).astype(BF16)
                g_ref[ci, hd] = lax.dot_general(do, qs, (((0,), (0,)), ((), ())), preferred_element_type=F32)
                dq = jnp.dot(do, st_b, preferred_element_type=F32) * Q_SCALE
                dpg_ref[rows, PG_Q + hd * HDK:PG_Q + (hd + 1) * HDK] = dq.astype(dpg_ref.dtype)
                dgo = dy * (oh * gh) * (sig * (1.0 + go * (1.0 - sig)))
                dpg_ref[rows, PG_G + hd * HDV:PG_G + (hd + 1) * HDV] = dgo.astype(dpg_ref.dtype)
        for ci in reversed(range(cpb)):
            for hd in range(HEADS):
                dst = dst_ref[hd] + g_ref[ci, hd]
                g_ref[ci, hd] = dst
                dst_ref[hd] = dst * dec_ref[ci:ci + 1, hd * HDK:(hd + 1) * HDK]
        for ci, rows in enumerate(chunks):
            for hd in range(HEADS):
                kc = slice(hd * HDK, (hd + 1) * HDK)
                v = pg_ref[rows, PG_V + hd * HDV:PG_V + (hd + 1) * HDV]
                ktf = ktf_ref[rows, kc]
                dst = g_ref[ci, hd]
                dst_b = dst.astype(BF16)
                dkt = jnp.dot(v, dst_b, preferred_element_type=F32)
                dv = lax.dot_general(ktf.astype(BF16), dst_b, (((1,), (1,)), ((), ())), preferred_element_type=F32)
                dd = jnp.sum(dst * sp_ref[ci, hd], axis=0, keepdims=True)
                dla = jnp.dot(tri_strict, dkt * ktf, precision=HIGHEST, preferred_element_type=F32) + dd * dec_ref[ci:ci + 1, kc]
                df_ref[rows, kc] = dla * sg_ref[rows, kc]
                dpg_ref[rows, PG_K + hd * HDK:PG_K + (hd + 1) * HDK] = (dkt * e_ref[rows, kc]).astype(dpg_ref.dtype)
                dpg_ref[rows, PG_V + hd * HDV:PG_V + (hd + 1) * HDV] = dv.astype(dpg_ref.dtype)
        df = df_ref[...]
        df_b = df.astype(BF16)
        dpg_ref[:, PG_F:PG_W] = lax.dot_general(df_b, wfu_ref[...], (((1,), (1,)), ((), ())), preferred_element_type=F32).astype(dpg_ref.dtype)
        dwfu_ref[...] += lax.dot_general(flow, df_b, (((0,), (0,)), ((), ())), preferred_element_type=F32)
        dbf_ref[...] += jnp.sum(df, axis=0, keepdims=True)

    rev = lambda i: (nblk - 1 - i, 0)
    return pl.pallas_call(
        body, grid=(nblk,),
        in_specs=[pl.BlockSpec((ts, PG_W), rev), pl.BlockSpec((cpb, HEADS, HDV, HDK), lambda i: (nblk - 1 - i, 0, 0, 0)),
                  pl.BlockSpec((ts, HEADS * HDV), rev), _fixed((LANE, HEADS * HDK)), _fixed((1, HEADS * HDK)), _fixed((1, HEADS * HDV))],
        out_specs=[pl.BlockSpec((ts, PG_W), rev), _fixed((LANE, HEADS * HDK)), _fixed((1, HEADS * HDK)), _fixed((1, HEADS * HDV))],
        out_shape=[jax.ShapeDtypeStruct((s, PG_W), BF16), jax.ShapeDtypeStruct((LANE, HEADS * HDK), F32),
                   jax.ShapeDtypeStruct((1, HEADS * HDK), F32), jax.ShapeDtypeStruct((1, HEADS * HDV), F32)],
        scratch_shapes=[pltpu.VMEM((HEADS, HDV, HDK), F32)] + [pltpu.VMEM((ts, HEADS * HDK), F32)] * 5
        + [pltpu.VMEM((max(cpb, 8), HEADS * HDK), F32), pltpu.VMEM((cpb, HEADS, HDV, HDK), F32)],
        compiler_params=_params(("arbitrary",)), name=name,
    )(pg, sp, dya, wfu, b_f, gnorm)


def _window_sums(ext, sign):
    n = ext.shape[0]
    sums = {1: ext}
    w = 1
    while w < POOL_WINDOWS[-1]:
        sums[2 * w] = sums[w] + pltpu.roll(sums[w], w if sign > 0 else n - w, 0)
        w *= 2
    return [sums[POOL_WINDOWS[g]][:, g * LANE:(g + 1) * LANE] for g in range(len(POOL_WINDOWS))]


def _pool_counts(row0, n):
    pos = (row0 + lax.broadcasted_iota(jnp.int32, (n, 1), 0) + 1).astype(F32)
    return [jnp.minimum(pos, float(w)) for w in POOL_WINDOWS]


def _pool_fwd(ppx, w_pool, pool_scale, name):
    s = ppx.shape[0]
    ts = _tile(s, TS_POOL, POOL_HALO)
    hb = ts // POOL_HALO
    pw = len(POOL_WINDOWS) * LANE

    def body(p_ref, halo_ref, w_ref, sc_ref, y_ref, ext_ref):
        i = pl.program_id(0)
        p = p_ref[...].astype(F32)
        ext_ref[0:POOL_HALO, :] = jnp.where(i > 0, halo_ref[...].astype(F32), 0.0)
        ext_ref[POOL_HALO:, :] = p
        sums = _window_sums(ext_ref[...], +1)
        cnt = _pool_counts(i * ts, ts)
        for g in range(len(POOL_WINDOWS)):
            cols = slice(g * LANE, (g + 1) * LANE)
            mixed = sums[g][POOL_HALO:, :] / cnt[g] - p[:, cols]
            y = jnp.dot(mixed.astype(BF16), w_ref[g], preferred_element_type=F32)
            y_ref[:, cols] = (y * sc_ref[:, cols]).astype(y_ref.dtype)

    return pl.pallas_call(
        body, grid=(s // ts,),
        in_specs=[pl.BlockSpec((ts, pw), lambda i: (i, 0)), pl.BlockSpec((POOL_HALO, pw), lambda i: (jnp.maximum(i * hb - 1, 0), 0)),
                  _fixed((len(POOL_WINDOWS), LANE, LANE)), _fixed((1, pw))],
        out_specs=_rows(ts, pw), out_shape=jax.ShapeDtypeStruct((s, pw), BF16),
        scratch_shapes=[pltpu.VMEM((ts + POOL_HALO, pw), F32)],
        compiler_params=_params(("parallel",)), name=name,
    )(ppx, ppx, w_pool, pool_scale)


def _pool_bwd(dyb, ppx, w_pool, pool_scale, name):
    s = ppx.shape[0]
    ts = _tile(s, TS_POOL, POOL_HALO)
    hb = ts // POOL_HALO
    nblk = s // ts
    last_halo = s // POOL_HALO - 1
    ng = len(POOL_WINDOWS)
    pw = ng * LANE

    def body(p_ref, halo_ref, dy_ref, dyn_ref, w_ref, sc_ref, dp_ref, dw_ref, dsc_ref, ext_ref, dext_ref, dm_ref):
        i = pl.program_id(0)

        @pl.when(i == 0)
        def _():
            dw_ref[...] = jnp.zeros_like(dw_ref)
            dsc_ref[...] = jnp.zeros_like(dsc_ref)

        p = p_ref[...].astype(F32)
        ext_ref[0:POOL_HALO, :] = jnp.where(i > 0, halo_ref[...].astype(F32), 0.0)
        ext_ref[POOL_HALO:, :] = p
        sums = _window_sums(ext_ref[...], +1)
        cnt = _pool_counts(i * ts, ts + POOL_HALO)
        sc = sc_ref[...]
        dy = dy_ref[...].astype(F32)
        dyn = jnp.where(i < nblk - 1, dyn_ref[...].astype(F32), 0.0)
        for g in range(ng):
            cols = slice(g * LANE, (g + 1) * LANE)
            wg = w_ref[g]
            mixed = (sums[g][POOL_HALO:, :] / cnt[g][0:ts] - p[:, cols]).astype(BF16)
            ypre = jnp.dot(mixed, wg, preferred_element_type=F32)
            dsc_ref[:, cols] += jnp.sum(dy[:, cols] * ypre, axis=0, keepdims=True)
            dyp = (dy[:, cols] * sc[:, cols]).astype(BF16)
            dypn = (dyn[:, cols] * sc[:, cols]).astype(BF16)
            dw_ref[g] += lax.dot_general(mixed, dyp, (((0,), (0,)), ((), ())), preferred_element_type=F32)
            dm = lax.dot_general(dyp, wg, (((1,), (1,)), ((), ())), preferred_element_type=F32)
            dmn = lax.dot_general(dypn, wg, (((1,), (1,)), ((), ())), preferred_element_type=F32)
            dext_ref[0:ts, cols] = dm / cnt[g][0:ts]
            dext_ref[ts:, cols] = dmn / cnt[g][ts:]
            dm_ref[:, cols] = dm
        lead = _window_sums(dext_ref[...], -1)
        for g in range(ng):
            cols = slice(g * LANE, (g + 1) * LANE)
            dp_ref[:, cols] = (lead[g][0:ts, :] - dm_ref[:, cols]).astype(dp_ref.dtype)

    return pl.pallas_call(
        body, grid=(nblk,),
        in_specs=[pl.BlockSpec((ts, pw), lambda i: (i, 0)), pl.BlockSpec((POOL_HALO, pw), lambda i: (jnp.maximum(i * hb - 1, 0), 0)),
                  pl.BlockSpec((ts, pw), lambda i: (i, 0)), pl.BlockSpec((POOL_HALO, pw), lambda i: (jnp.minimum((i + 1) * hb, last_halo), 0)),
                  _fixed((ng, LANE, LANE)), _fixed((1, pw))],
        out_specs=[_rows(ts, pw), _fixed((ng, LANE, LANE)), _fixed((1, pw))],
        out_shape=[jax.ShapeDtypeStruct((s, pw), BF16), jax.ShapeDtypeStruct((ng, LANE, LANE), F32), jax.ShapeDtypeStruct((1, pw), F32)],
        scratch_shapes=[pltpu.VMEM((ts + POOL_HALO, pw), F32), pltpu.VMEM((ts + POOL_HALO, pw), F32), pltpu.VMEM((ts, pw), F32)],
        compiler_params=_params(("arbitrary",)), name=name,
    )(ppx, ppx, dyb, dyb, w_pool, pool_scale)


def _xattn_fwd(ppx, kv, name):
    s = ppx.shape[0]
    m = kv.shape[0]
    ts = _tile(s, TS_XA, 8)
    xw = XA_HEADS * XA_HD

    def body(q_ref, kv_ref, o_ref):
        for hd in range(XA_HEADS):
            cols = slice(hd * XA_HD, (hd + 1) * XA_HD)
            k = kv_ref[:, hd * XA_HD:(hd + 1) * XA_HD]
            v = kv_ref[:, xw + hd * XA_HD:xw + (hd + 1) * XA_HD]
            sc = lax.dot_general(q_ref[:, cols], k, (((1,), (1,)), ((), ())), preferred_element_type=F32) * XA_SCALE
            ex = jnp.exp(sc - jnp.max(sc, axis=-1, keepdims=True))
            pr = ex / jnp.sum(ex, axis=-1, keepdims=True)
            o_ref[:, cols] = jnp.dot(pr.astype(BF16), v, preferred_element_type=F32).astype(o_ref.dtype)

    return pl.pallas_call(
        body, grid=(s // ts,), in_specs=[pl.BlockSpec((ts, xw), lambda i: (i, 1)), _fixed((m, 2 * xw))],
        out_specs=_rows(ts, xw), out_shape=jax.ShapeDtypeStruct((s, xw), BF16),
        compiler_params=_params(("parallel",)), name=name,
    )(ppx, kv)


def _xattn_bwd(dxc, ppx, kv, name):
    s = ppx.shape[0]
    m = kv.shape[0]
    ts = _tile(s, TS_XA, 8)
    xw = XA_HEADS * XA_HD

    def body(do_ref, q_ref, kv_ref, dq_ref, dkv_ref):
        @pl.when(pl.program_id(0) == 0)
        def _():
            dkv_ref[...] = jnp.zeros_like(dkv_ref)

        for hd in range(XA_HEADS):
            cols = slice(hd * XA_HD, (hd + 1) * XA_HD)
            vcols = slice(xw + hd * XA_HD, xw + (hd + 1) * XA_HD)
            q = q_ref[:, cols]
            k = kv_ref[:, cols]
            v = kv_ref[:, vcols]
            do = do_ref[:, cols]
            sc = lax.dot_general(q, k, (((1,), (1,)), ((), ())), preferred_element_type=F32) * XA_SCALE
            ex = jnp.exp(sc - jnp.max(sc, axis=-1, keepdims=True))
            pr = ex / jnp.sum(ex, axis=-1, keepdims=True)
            dpr = lax.dot_general(do, v, (((1,), (1,)), ((), ())), preferred_element_type=F32)
            dsc = (pr * (dpr - jnp.sum(dpr * pr, axis=-1, keepdims=True)) * XA_SCALE).astype(BF16)
            dq_ref[:, cols] = jnp.dot(dsc, k, preferred_element_type=F32).astype(dq_ref.dtype)
            dkv_ref[:, cols] += lax.dot_general(dsc, q, (((0,), (0,)), ((), ())), preferred_element_type=F32)
            dkv_ref[:, vcols] += lax.dot_general(pr.astype(BF16), do, (((0,), (0,)), ((), ())), preferred_element_type=F32)

    return pl.pallas_call(
        body, grid=(s // ts,), in_specs=[_rows(ts, xw), pl.BlockSpec((ts, xw), lambda i: (i, 1)), _fixed((m, 2 * xw))],
        out_specs=[_rows(ts, xw), _fixed((m, 2 * xw))],
        out_shape=[jax.ShapeDtypeStruct((s, xw), BF16), jax.ShapeDtypeStruct((m, 2 * xw), F32)],
        compiler_params=_params(("arbitrary",)), name=name,
    )(dxc, ppx, kv)


def _merge_fwd(pgt, ya, yb, yc, name):
    s = pgt.shape[0]
    ts = _tile(s, TS_ROW, 8)

    def body(gt_ref, ya_ref, yb_ref, yc_ref, o_ref):
        acc = _sigmoid(gt_ref[:, 0:D].astype(F32)) * ya_ref[...].astype(F32)
        acc = acc + _sigmoid(gt_ref[:, D:2 * D].astype(F32)) * yb_ref[...].astype(F32)
        acc = acc + _sigmoid(gt_ref[:, 2 * D:3 * D].astype(F32)) * yc_ref[...].astype(F32)
        o_ref[...] = acc.astype(o_ref.dtype)

    return pl.pallas_call(
        body, grid=(s // ts,), in_specs=[_rows(ts, 3 * D)] + [_rows(ts, D)] * 3, out_specs=_rows(ts, D),
        out_shape=jax.ShapeDtypeStruct((s, D), BF16), compiler_params=_params(("parallel",)), name=name,
    )(pgt, ya, yb, yc)


def _merge_bwd(dmerged, pgt, ya, yb, yc, name):
    s = pgt.shape[0]
    ts = _tile(s, TS_ROW, 8)

    def body(dm_ref, gt_ref, ya_ref, yb_ref, yc_ref, dya_ref, dyb_ref, dyc_ref, dgt_ref):
        dm = dm_ref[...].astype(F32)
        for j, (y_ref, dy_ref) in enumerate(((ya_ref, dya_ref), (yb_ref, dyb_ref), (yc_ref, dyc_ref))):
            sig = _sigmoid(gt_ref[:, j * D:(j + 1) * D].astype(F32))
            dy_ref[...] = (dm * sig).astype(dy_ref.dtype)
            dgt_ref[:, j * D:(j + 1) * D] = (dm * y_ref[...].astype(F32) * sig * (1.0 - sig)).astype(dgt_ref.dtype)

    return pl.pallas_call(
        body, grid=(s // ts,), in_specs=[_rows(ts, D), _rows(ts, 3 * D)] + [_rows(ts, D)] * 3,
        out_specs=[_rows(ts, D)] * 3 + [_rows(ts, 3 * D)],
        out_shape=[jax.ShapeDtypeStruct((s, D), BF16)] * 3 + [jax.ShapeDtypeStruct((s, 3 * D), BF16)],
        compiler_params=_params(("parallel",)), name=name,
    )(dmerged, pgt, ya, yb, yc)


def _adamw(w, g, m, v, name):
    r, c = w.shape[-2:]
    tr = _tile(r, 256, 8)

    def spec(a):
        return _rows(tr, c) if a.ndim == 2 else pl.BlockSpec((None, tr, c), lambda i: (0, i, 0))

    def body(w_ref, g_ref, m_ref, v_ref, d_ref, mo_ref, vo_ref):
        gv = g_ref[...]
        mn = ADAM_B1 * m_ref[...] + (1.0 - ADAM_B1) * gv
        vn = ADAM_B2 * v_ref[...] + (1.0 - ADAM_B2) * (gv * gv)
        m_hat = mn / (1.0 - ADAM_B1 ** ADAM_STEP)
        v_hat = vn / (1.0 - ADAM_B2 ** ADAM_STEP)
        d_ref[...] = -ADAM_LR * (m_hat / (jnp.sqrt(v_hat) + ADAM_EPS) + ADAM_WD * w_ref[...])
        mo_ref[...] = mn
        vo_ref[...] = vn

    return pl.pallas_call(
        body, grid=(r // tr,), in_specs=[spec(a) for a in (w, g, m, v)], out_specs=[spec(w)] * 3,
        out_shape=[jax.ShapeDtypeStruct(w.shape, F32)] * 3, compiler_params=_params(("parallel",)), name=name,
    )(w, g, m, v)


ANY = pl.BlockSpec(memory_space=pl.ANY)


def _place():
    x, y, c = lax.axis_index("x"), lax.axis_index("y"), lax.axis_index("c")
    chips = [(1 - x, y), (x, 1 - y), (1 - x, 1 - y)]
    return x, y, c, chips


def _half(c, rows):
    h = rows // 2
    return pl.ds(pl.multiple_of(c * h, 8), h)


def _place_shard(shard, chip_arr, out_dtype, name):
    _, r, cols = shard.shape
    tr = _tile(r, 256, 16)

    def body(chip_ref, s_ref, o_ref):
        o_ref[...] = s_ref[...].astype(o_ref.dtype)

    return pl.pallas_call(
        body,
        grid_spec=pltpu.PrefetchScalarGridSpec(
            num_scalar_prefetch=1, grid=(r // tr,),
            in_specs=[pl.BlockSpec((None, tr, cols), lambda i, chip_ref: (0, i, 0))],
            out_specs=pl.BlockSpec((None, tr, cols), lambda i, chip_ref: (chip_ref[0], i, 0))),
        out_shape=jax.ShapeDtypeStruct((4, r, cols), out_dtype),
        compiler_params=_params(("parallel",)), name=name,
    )(chip_arr, shard)


def _gather_shards(bufs, name):
    n = len(bufs)

    def body(*refs):
        outs = refs[n:2 * n]
        send_ici, recv_ici, send_d2d, recv_d2d = refs[2 * n:]
        x, y, c, chips = _place()
        me = 2 * x + y
        sibling = (x, y, 1 - c)

        def ici(w, p, chip_of_block, to):
            rows = _half(c, outs[w].shape[1])
            block = outs[w].at[chip_of_block, rows]
            return pltpu.make_async_remote_copy(
                src_ref=block, dst_ref=block, send_sem=send_ici.at[w, p], recv_sem=recv_ici.at[w, p], device_id=to, device_id_type=MESH)

        def d2d(w, p, chip_of_block, half_of):
            rows = _half(half_of, outs[w].shape[1])
            block = outs[w].at[chip_of_block, rows]
            return pltpu.make_async_remote_copy(
                src_ref=block, dst_ref=block, send_sem=send_d2d.at[w, p], recv_sem=recv_d2d.at[w, p], device_id=sibling, device_id_type=MESH)

        sends = [ici(w, p, me, (*chip, c)) for p, chip in enumerate(chips) for w in range(n)]
        for cp in sends:
            cp.start()
        passed = []
        for p, (px, py) in enumerate(chips):
            for w in range(n):
                ici(w, p, 2 * px + py, (px, py, c)).wait_recv()
                fwd = d2d(w, p, 2 * px + py, c)
                fwd.start()
                passed.append(fwd)
        for p, (px, py) in enumerate(chips):
            for w in range(n):
                d2d(w, p, 2 * px + py, 1 - c).wait_recv()
        for cp in sends + passed:
            cp.wait_send()

    return pl.pallas_call(
        body, in_specs=[ANY] * n, out_specs=[ANY] * n,
        out_shape=[jax.ShapeDtypeStruct(a.shape, a.dtype) for a in bufs],
        input_output_aliases={w: w for w in range(n)},
        scratch_shapes=[pltpu.SemaphoreType.DMA((n, 3))] * 4,
        compiler_params=pltpu.CompilerParams(has_side_effects=True), name=name,
    )(*bufs)


def _pair_exchange(grads, name):
    n = len(grads)

    def body(*refs):
        ins, outs = refs[:n], refs[n:2 * n]
        send_sem, recv_sem = refs[2 * n:]
        x, y, c, _ = _place()
        copies = []
        for w in range(n):
            rows = _half(1 - c, ins[w].shape[1])
            copies.append(pltpu.make_async_remote_copy(
                src_ref=ins[w].at[:, rows], dst_ref=outs[w], send_sem=send_sem.at[w], recv_sem=recv_sem.at[w],
                device_id=(x, y, 1 - c), device_id_type=MESH))
        for cp in copies:
            cp.start()
        for cp in copies:
            cp.wait()

    return pl.pallas_call(
        body, in_specs=[ANY] * n, out_specs=[ANY] * n,
        out_shape=[jax.ShapeDtypeStruct((4, a.shape[1] // 2, a.shape[2]), a.dtype) for a in grads],
        scratch_shapes=[pltpu.SemaphoreType.DMA((n,))] * 2,
        compiler_params=pltpu.CompilerParams(has_side_effects=True), name=name,
    )(*grads)


def _pair_sum(g, got, c_arr, name):
    _, r, cols = g.shape
    h = r // 2
    th = _tile(h, 256, 16)
    nb = h // th

    def body(c_ref, g_ref, got_ref, o_ref):
        o_ref[...] = (g_ref[...].astype(F32) + got_ref[...].astype(F32)).astype(o_ref.dtype)

    return pl.pallas_call(
        body,
        grid_spec=pltpu.PrefetchScalarGridSpec(
            num_scalar_prefetch=1, grid=(4, nb),
            in_specs=[pl.BlockSpec((None, th, cols), lambda j, i, c_ref: (j, c_ref[0] * nb + i, 0)),
                      pl.BlockSpec((None, th, cols), lambda j, i, c_ref: (j, i, 0))],
            out_specs=pl.BlockSpec((None, th, cols), lambda j, i, c_ref: (j, i, 0))),
        out_shape=jax.ShapeDtypeStruct((4, h, cols), BF16),
        compiler_params=_params(("parallel", "parallel")), name=name,
    )(c_arr, g, got)


def _chip_exchange(parts, name):
    n = len(parts)

    def body(*refs):
        ins, outs = refs[:n], refs[n:2 * n]
        send_sem, recv_sem = refs[2 * n:]
        x, y, c, chips = _place()
        copies = []
        for p, (px, py) in enumerate(chips):
            for w in range(n):
                copies.append(pltpu.make_async_remote_copy(
                    src_ref=ins[w].at[2 * px + py], dst_ref=outs[w].at[p], send_sem=send_sem.at[w, p], recv_sem=recv_sem.at[w, p],
                    device_id=(px, py, c), device_id_type=MESH))
        for cp in copies:
            cp.start()
        for cp in copies:
            cp.wait()

    return pl.pallas_call(
        body, in_specs=[ANY] * n, out_specs=[ANY] * n,
        out_shape=[jax.ShapeDtypeStruct((3,) + a.shape[1:], a.dtype) for a in parts],
        scratch_shapes=[pltpu.SemaphoreType.DMA((n, 3))] * 2,
        compiler_params=pltpu.CompilerParams(has_side_effects=True), name=name,
    )(*parts)


def _chip_sum(part, got, place_arr, name):
    _, h, cols = part.shape
    th = _tile(h, 256, 16)
    nb = h // th

    def body(place_ref, p_ref, got_ref, o_ref):
        acc = p_ref[...].astype(F32)
        for p in range(3):
            acc = acc + got_ref[p].astype(F32)
        o_ref[...] = acc

    return pl.pallas_call(
        body,
        grid_spec=pltpu.PrefetchScalarGridSpec(
            num_scalar_prefetch=1, grid=(nb,),
            in_specs=[pl.BlockSpec((None, th, cols), lambda i, place_ref: (place_ref[0], i, 0)),
                      pl.BlockSpec((3, th, cols), lambda i, place_ref: (0, i, 0))],
            out_specs=pl.BlockSpec((th, cols), lambda i, place_ref: (place_ref[1] * nb + i, 0))),
        out_shape=jax.ShapeDtypeStruct((2 * h, cols), F32),
        compiler_params=_params(("parallel",)), name=name,
    )(place_arr, part, got)


def _pair_join(bufs, name):
    n = len(bufs)

    def body(*refs):
        outs = refs[n:2 * n]
        send_sem, recv_sem = refs[2 * n:]
        x, y, c, _ = _place()
        copies = []
        for w in range(n):
            block = outs[w].at[_half(c, outs[w].shape[0])]
            copies.append(pltpu.make_async_remote_copy(
                src_ref=block, dst_ref=block, send_sem=send_sem.at[w], recv_sem=recv_sem.at[w],
                device_id=(x, y, 1 - c), device_id_type=MESH))
        for cp in copies:
            cp.start()
        for w, cp in enumerate(copies):
            cp.wait_send()
            block = outs[w].at[_half(1 - c, outs[w].shape[0])]
            pltpu.make_async_remote_copy(
                src_ref=block, dst_ref=block, send_sem=send_sem.at[w], recv_sem=recv_sem.at[w],
                device_id=(x, y, 1 - c), device_id_type=MESH).wait_recv()

    return pl.pallas_call(
        body, in_specs=[ANY] * n, out_specs=[ANY] * n,
        out_shape=[jax.ShapeDtypeStruct(a.shape, a.dtype) for a in bufs],
        input_output_aliases={w: w for w in range(n)},
        scratch_shapes=[pltpu.SemaphoreType.DMA((n,))] * 2,
        compiler_params=pltpu.CompilerParams(has_side_effects=True), name=name,
    )(*bufs)


def _all_sum(pack, name):
    r, cols = pack.shape

    def body(x_ref, o_ref, all_ref, send_sems, recv_sems):
        x, y, c, chips = _place()
        me, sibling = (x, y, c), (x, y, 1 - c)

        def slot(px, py, pc):
            return all_ref.at[4 * px + 2 * py + pc]

        def copy(k, block, to, src=None):
            return pltpu.make_async_remote_copy(
                src_ref=slot(*block) if src is None else src, dst_ref=slot(*block),
                send_sem=send_sems.at[k], recv_sem=recv_sems.at[k], device_id=to, device_id_type=MESH)

        all_ref[4 * x + 2 * y + c] = x_ref[...]
        first = [copy(0, me, sibling, src=x_ref)]
        first += [copy(1 + j, me, (*chip, c), src=x_ref) for j, chip in enumerate(chips)]
        for cp in first:
            cp.start()
        passed = [copy(4 + j, (*chip, c), sibling) for j, chip in enumerate(chips)]
        for j, chip in enumerate(chips):
            copy(1 + j, (*chip, c), me).wait_recv()
            passed[j].start()
        copy(0, sibling, me).wait_recv()
        for j, chip in enumerate(chips):
            copy(4 + j, (*chip, 1 - c), me).wait_recv()
        for cp in first + passed:
            cp.wait_send()
        acc = all_ref[0]
        for k in range(1, 8):
            acc = acc + all_ref[k]
        o_ref[...] = acc

    return pl.pallas_call(
        body, in_specs=[pl.BlockSpec(memory_space=pltpu.VMEM)], out_specs=pl.BlockSpec(memory_space=pltpu.VMEM),
        out_shape=jax.ShapeDtypeStruct((r, cols), F32),
        scratch_shapes=[pltpu.VMEM((8, r, cols), F32), pltpu.SemaphoreType.DMA((7,)), pltpu.SemaphoreType.DMA((7,))],
        compiler_params=pltpu.CompilerParams(has_side_effects=True, vmem_limit_bytes=VMEM_LIMIT), name=name,
    )(pack)


def _ffn_fwd(x_norm, w_in, w_out, tag):
    ab = _mm(x_norm, w_in, out_dtype=BF16, tn=1408, name=tag + "_in")
    u = _swiglu_fwd(ab, name=tag + "_swiglu")
    f = _mm(u, w_out, tk=DFF, name=tag + "_out")
    return ab, u, f


def _ffn_bwd(dz, x_norm, ab, u, w_in, w_out, tag):
    du = _mm(dz, w_out, tb=True, out_dtype=BF16, tn=1408, name=tag + "_out_dx")
    dw_out = _mm(u, dz, ta=True, out_dtype=BF16, tm=1408, tk=2048, name=tag + "_out_dw")
    dab = _swiglu_bwd(ab, du, name=tag + "_swiglu_bwd")
    dxn = _mm(dab, w_in, tb=True, tk=2816, name=tag + "_in_dx")
    dw_in = _mm(x_norm, dab, ta=True, out_dtype=BF16, tm=512, tk=4096, shards=4, name=tag + "_in_dw")
    return dxn, dw_in, dw_out


def _local_step(x, mem, target, small, big):
    h1 = _norm_fwd(x, small["ffn1_pre_g"], BF16, name="ffn1_pre")
    ab1, u1, f1 = _ffn_fwd(h1, big["ffn1_w_in"], big["ffn1_w_out"], "ffn1")
    x1, h = _resid_norm_fwd(x, f1, small["ffn1_post_g"], 0.5, small["mix_pre_g"], name="ffn1_post")
    pg = _mm(h, big["w_gla"], out_dtype=BF16, tn=PG_W, name="mix_in_gla")
    ppx = _mm(h, big["w_px"], out_dtype=BF16, name="mix_in_px")
    pgt = _mm(h, big["w_gates"], out_dtype=BF16, tn=1536, name="mix_in_gates")
    mem_n = _norm_fwd(mem, small["mem_norm_g"], BF16, name="mem_norm")
    kv = _mm(mem_n, big["w_mem_kv"], out_dtype=BF16, name="mem_kv")
    ya_in, sp = _gla_fwd(pg, small["w_fu_pad"], small["b_f"], small["gla_norm_g"], name="gla_fwd")
    yb_in = _pool_fwd(ppx, small["w_pool_b"], small["pool_scale"], name="pool_fwd")
    xc = _xattn_fwd(ppx, kv, name="xattn_fwd")
    ya = _mm(ya_in, big["w_up_gla"], out_dtype=BF16, name="up_gla")
    yb = _mm(yb_in, big["w_up_pool"], out_dtype=BF16, name="up_pool")
    yc = _mm(xc, big["w_up_xattn"], out_dtype=BF16, name="up_xattn")
    merged = _merge_fwd(pgt, ya, yb, yc, name="merge_fwd")
    ymix = _mm(merged, big["w_o"], name="mix_out")
    x2, h2 = _resid_norm_fwd(x1, ymix, small["mix_post_g"], 1.0, small["ffn2_pre_g"], name="mix_post")
    ab2, u2, f2 = _ffn_fwd(h2, big["ffn2_w_in"], big["ffn2_w_out"], "ffn2")
    x3, _ = _resid_norm_fwd(x2, f2, small["ffn2_post_g"], 0.5, None, name="ffn2_post")
    gs, gb = {}, {}
    dx3, gs["final_g"], loss = _loss_bwd(x3, small["final_g"], target, name="loss")
    dz2, gs["ffn2_post_g"] = _rms_bwd(f2, small["ffn2_post_g"], [dx3], None, 0.5, BF16, name="ffn2_post_bwd")
    dh2, gb["ffn2_w_in"], gb["ffn2_w_out"] = _ffn_bwd(dz2, h2, ab2, u2, big["ffn2_w_in"], big["ffn2_w_out"], "ffn2")
    dx2, gs["ffn2_pre_g"] = _rms_bwd(x2, small["ffn2_pre_g"], [dh2], dx3, 1.0, F32, name="ffn2_pre_bwd")
    dy, gs["mix_post_g"] = _rms_bwd(ymix, small["mix_post_g"], [dx2], None, 1.0, BF16, name="mix_post_bwd")
    dmerged = _mm(dy, big["w_o"], tb=True, out_dtype=BF16, name="mix_out_dx")
    gb["w_o"] = _mm(merged, dy, ta=True, out_dtype=BF16, tm=512, tk=4096, name="mix_out_dw")
    dya, dyb, dyc, dgt = _merge_bwd(dmerged, pgt, ya, yb, yc, name="merge_bwd")
    dya_in = _mm(dya, big["w_up_gla"], tb=True, out_dtype=BF16, name="up_gla_dx")
    gb["w_up_gla"] = _mm(ya_in, dya, ta=True, out_dtype=BF16, tm=512, tk=4096, name="up_gla_dw")
    dyb_in = _mm(dyb, big["w_up_pool"], tb=True, out_dtype=BF16, name="up_pool_dx")
    gb["w_up_pool"] = _mm(yb_in, dyb, ta=True, out_dtype=BF16, tm=512, tk=4096, shards=4, name="up_pool_dw")
    dxc = _mm(dyc, big["w_up_xattn"], tb=True, out_dtype=BF16, name="up_xattn_dx")
    gb["w_up_xattn"] = _mm(xc, dyc, ta=True, out_dtype=BF16, tm=512, tk=4096, shards=4, name="up_xattn_dw")
    dpg, gs["w_fu_pad"], gs["b_f"], gs["gla_norm_g"] = _gla_bwd(pg, sp, dya_in, small["w_fu_pad"], small["b_f"], small["gla_norm_g"], name="gla_bwd")
    dp, gs["w_pool"], gs["pool_scale"] = _pool_bwd(dyb_in, ppx, small["w_pool_b"], small["pool_scale"], name="pool_bwd")
    dxq, dkv = _xattn_bwd(dxc, ppx, kv, name="xattn_bwd")
    dkv = dkv.astype(BF16)
    gb["w_mem_kv"] = _mm(mem_n, dkv, ta=True, out_dtype=BF16, name="mem_kv_dw")
    dmem_n = _mm(dkv, big["w_mem_kv"], tb=True, name="mem_kv_dx")
    _, gs["mem_norm_g"] = _rms_bwd(mem, small["mem_norm_g"], [dmem_n], None, 1.0, BF16, name="mem_norm_bwd")
    dh_parts = [
        _mm(dpg, big["w_gla"], tb=True, tk=PG_W, name="mix_in_gla_dx"),
        _mm(dp, big["w_p"], tb=True, name="mix_in_p_dx"),
        _mm(dxq, big["w_xq"], tb=True, name="mix_in_xq_dx"),
        _mm(dgt, big["w_gates"], tb=True, tk=3072, name="mix_in_gates_dx"),
    ]
    gb["w_gla"] = _mm(h, dpg, ta=True, out_dtype=BF16, tm=512, tk=4096, tn=640, name="mix_in_gla_dw")
    gb["w_p"] = _mm(h, dp, ta=True, out_dtype=BF16, tm=512, tk=4096, name="mix_in_p_dw")
    gb["w_xq"] = _mm(h, dxq, ta=True, out_dtype=BF16, tm=512, tk=4096, name="mix_in_xq_dw")
    gb["w_gates"] = _mm(h, dgt, ta=True, out_dtype=BF16, tm=512, tk=4096, tn=1024, name="mix_in_gates_dw")
    dx1, gs["mix_pre_g"] = _rms_bwd(x1, small["mix_pre_g"], dh_parts, dx2, 1.0, F32, name="mix_pre_bwd")
    dz1, gs["ffn1_post_g"] = _rms_bwd(f1, small["ffn1_post_g"], [dx1], None, 0.5, BF16, name="ffn1_post_bwd")
    dh1, gb["ffn1_w_in"], gb["ffn1_w_out"] = _ffn_bwd(dz1, h1, ab1, u1, big["ffn1_w_in"], big["ffn1_w_out"], "ffn1")
    dx0, gs["ffn1_pre_g"] = _rms_bwd(x, small["ffn1_pre_g"], [dh1], dx1, 1.0, F32, name="ffn1_pre_bwd")
    return loss, dx0, gs, gb


BIG = ("ffn1_w_in", "ffn1_w_out", "w_in", "w_mem_kv", "w_up_gla", "w_up_pool", "w_up_xattn", "w_o", "ffn2_w_in", "ffn2_w_out")
COL_SHARDED = ("ffn1_w_in", "w_in", "w_up_pool", "w_up_xattn", "ffn2_w_in")
GAINS = ("ffn1_pre_g", "ffn1_post_g", "mix_pre_g", "gla_norm_g", "mem_norm_g", "mix_post_g", "ffn2_pre_g", "ffn2_post_g", "final_g")
WEIGHTS = ("ffn1_pre_g", "ffn1_w_in", "ffn1_w_out", "ffn1_post_g", "mix_pre_g", "w_in", "w_fu", "b_f", "gla_norm_g", "w_pool",
           "pool_scale", "mem_norm_g", "w_mem_kv", "w_up_gla", "w_up_pool", "w_up_xattn", "w_o", "mix_post_g", "ffn2_pre_g",
           "ffn2_w_in", "ffn2_w_out", "ffn2_post_g", "final_g")
IN_GLA, IN_F, IN_PX, IN_GATES, IN_END = 0, 3072, 3088, 4112, 7184
PACK_ROWS = 96


def _cols_from_shards(g):
    return jnp.transpose(g, (1, 0, 2)).reshape(g.shape[1], 4 * g.shape[2])


def _pack_small(t):
    single = jnp.zeros((PACK_ROWS - 72, D), F32)
    for i, n in enumerate(GAINS):
        single = single.at[i].set(t[n].reshape(D))
    k = len(GAINS)
    single = single.at[k, 0:512].set(t["b_f"].reshape(512))
    single = single.at[k, 512:1024].set(t["pool_scale"].reshape(512))
    return jnp.concatenate([t["w_pool"].reshape(64, D), t["w_fu"].reshape(8, D), single], axis=0)


def _unpack_small(p):
    out = {n: p[72 + i:73 + i] for i, n in enumerate(GAINS)}
    k = 72 + len(GAINS)
    out["b_f"] = p[k:k + 1, 0:512]
    out["pool_scale"] = p[k:k + 1, 512:1024]
    out["w_pool"] = p[0:64].reshape(4, LANE, LANE)
    out["w_fu"] = p[64:72].reshape(GATE_RANK, 512)
    return out


def kernel(x, mem, ffn1_pre_g, ffn1_w_in, ffn1_w_out, ffn1_post_g, mix_pre_g, w_in, w_fu, b_f, gla_norm_g, w_pool, pool_scale, mem_norm_g, w_mem_kv, w_up_gla, w_up_pool, w_up_xattn, w_o, mix_post_g, ffn2_pre_g, ffn2_w_in, ffn2_w_out, ffn2_post_g, final_g, loss_target, m_ffn1_pre_g, m_ffn1_w_in, m_ffn1_w_out, m_ffn1_post_g, m_mix_pre_g, m_w_in, m_w_fu, m_b_f, m_gla_norm_g, m_w_pool, m_pool_scale, m_mem_norm_g, m_w_mem_kv, m_w_up_gla, m_w_up_pool, m_w_up_xattn, m_w_o, m_mix_post_g, m_ffn2_pre_g, m_ffn2_w_in, m_ffn2_w_out, m_ffn2_post_g, m_final_g, v_ffn1_pre_g, v_ffn1_w_in, v_ffn1_w_out, v_ffn1_post_g, v_mix_pre_g, v_w_in, v_w_fu, v_b_f, v_gla_norm_g, v_w_pool, v_pool_scale, v_mem_norm_g, v_w_mem_kv, v_w_up_gla, v_w_up_pool, v_w_up_xattn, v_w_o, v_mix_post_g, v_ffn2_pre_g, v_ffn2_w_in, v_ffn2_w_out, v_ffn2_post_g, v_final_g):
    args = dict(locals())
    w = {n: args[n][0] for n in WEIGHTS}
    m = {n: args["m_" + n][0] for n in WEIGHTS}
    v = {n: args["v_" + n][0] for n in WEIGHTS}
    xi, yi, ci = lax.axis_index("x"), lax.axis_index("y"), lax.axis_index("c")
    chip = 2 * xi + yi

    c_arr = jnp.reshape(ci, (1,)).astype(jnp.int32)
    chip_arr = jnp.reshape(chip, (1,)).astype(jnp.int32)
    place_arr = jnp.stack([chip, ci]).astype(jnp.int32)
    placed = [_place_shard(args[n], chip_arr, BF16, name="place_" + n) for n in BIG]
    placed.append(_place_shard(args["w_fu"], chip_arr, F32, name="place_w_fu"))
    gathered = _gather_shards(placed, name="gather_weights")
    full = {}
    for n, g in zip(BIG, gathered[:-1]):
        full[n] = _cols_from_shards(g) if n in COL_SHARDED else g.reshape(4 * g.shape[1], g.shape[2])
    w_fu_full = _cols_from_shards(gathered[-1])
    big = {n: full[n] for n in BIG if n != "w_in"}
    wi = full["w_in"]
    big["w_gla"] = jnp.concatenate([wi[:, IN_GLA:IN_PX], jnp.zeros((D, PG_W - IN_PX), BF16)], axis=1)
    big["w_px"] = wi[:, IN_PX:IN_GATES]
    big["w_p"] = wi[:, IN_PX:IN_PX + 512]
    big["w_xq"] = wi[:, IN_PX + 512:IN_GATES]
    big["w_gates"] = wi[:, IN_GATES:IN_END]
    small = {n: w[n].reshape(1, D) for n in GAINS}
    small["b_f"] = w["b_f"].reshape(1, 512)
    small["pool_scale"] = w["pool_scale"].reshape(1, 512)
    small["w_pool_b"] = w["w_pool"].astype(BF16)
    small["w_fu_pad"] = jnp.concatenate([w_fu_full, jnp.zeros((LANE - GATE_RANK, 512), F32)], axis=0).astype(BF16)

    loss, grad_x, gs, gb = _local_step(x[0], mem[0], loss_target[0], small, big)
    loss = lax.psum(loss[0, 0], ("x", "y", "c"))

    gs["w_fu"] = gs.pop("w_fu_pad")[0:GATE_RANK]
    small_sum = _unpack_small(_all_sum(_pack_small(gs), name="sum_small_grads"))
    dwi = jnp.concatenate([gb.pop("w_gla")[:, 0:IN_PX], gb.pop("w_p"), gb.pop("w_xq"), gb.pop("w_gates")], axis=1)
    gb["w_in"] = jnp.transpose(dwi.reshape(D, 4, IN_END // 4), (1, 0, 2))
    for n in BIG:
        if n not in COL_SHARDED:
            gb[n] = gb[n].reshape(4, gb[n].shape[0] // 4, gb[n].shape[1])
    contrib = [gb[n] for n in BIG]
    from_sibling = _pair_exchange(contrib, name="grads_pair_exchange")
    pair = [_pair_sum(g, got, c_arr, name="grads_pair_sum_" + n) for n, g, got in zip(BIG, contrib, from_sibling)]
    from_chips = _chip_exchange(pair, name="grads_chip_exchange")
    halves = [_chip_sum(p, got, place_arr, name="grads_chip_sum_" + n) for n, p, got in zip(BIG, pair, from_chips)]
    reduced = dict(zip(BIG, _pair_join(halves, name="grads_pair_join")))

    grads, delta, new_m, new_v = {}, {}, {}, {}
    for n in BIG:
        grads[n] = reduced[n][None]
        delta[n], new_m[n], new_v[n] = _adamw(args[n], reduced[n], args["m_" + n], args["v_" + n], name="adamw_" + n)
    small_names = GAINS + ("b_f", "pool_scale", "w_pool")
    w_fu_grad = lax.dynamic_slice_in_dim(small_sum["w_fu"], chip * LANE, LANE, axis=1)
    packs = []
    for t in (w, m, v):
        t = dict(t)
        t["w_fu"] = jnp.zeros((GATE_RANK, 512), F32)
        packs.append(_pack_small(t))
    sd, sm, sv = (_unpack_small(p) for p in _adamw(packs[0], _pack_small(small_sum), packs[1], packs[2], name="adamw_small"))
    for n in small_names:
        shape = args[n].shape
        grads[n] = small_sum[n].reshape(shape)
        delta[n], new_m[n], new_v[n] = sd[n].reshape(shape), sm[n].reshape(shape), sv[n].reshape(shape)
    grads["w_fu"] = w_fu_grad[None]
    delta["w_fu"], new_m["w_fu"], new_v["w_fu"] = _adamw(args["w_fu"], w_fu_grad, args["m_w_fu"], args["v_w_fu"], name="adamw_w_fu")

    outs = [loss, grad_x[None]]
    for group in (grads, delta, new_m, new_v):
        outs += [group[n] for n in WEIGHTS]
    return tuple(outs)
```

```python
import functools

import jax
import jax.numpy as jnp
from jax import lax
from jax.experimental import pallas as pl
from jax.experimental.pallas import tpu as pltpu

F32 = jnp.float32
BF16 = jnp.bfloat16
MESH = pl.DeviceIdType.MESH
HIGHEST = lax.Precision.HIGHEST

D = 1024
DFF = 2816
CHUNK = 64
HEADS = 4
HDK = 128
HDV = 256
GATE_TEMP = 16.0
POOL_WINDOWS = (2, 4, 8, 16)
POOL_HALO = 16
XA_HEADS = 4
XA_HD = 128
EPS = 1e-6
Q_SCALE = HDK ** -0.5
XA_SCALE = XA_HD ** -0.5
PG_Q, PG_K, PG_V, PG_G, PG_F, PG_W = 0, 512, 1024, 2048, 3072, 3200
GATE_RANK = 16
ADAM_LR, ADAM_B1, ADAM_B2, ADAM_EPS, ADAM_WD, ADAM_STEP = 0.001, 0.9, 0.999, 1e-08, 0.01, 10

VMEM_LIMIT = 48 * 1024 * 1024
LANE = 128
TS_ROW = 256
TS_GLA = 512
TS_POOL = 512
TS_XA = 512


def _params(sem):
    return pltpu.CompilerParams(dimension_semantics=sem, vmem_limit_bytes=VMEM_LIMIT)


def _tile(n, cap, unit=LANE):
    if n <= cap:
        return n
    best = None
    for t in range(unit, cap + 1, unit):
        if n % t == 0:
            best = t
    assert best is not None, (n, cap)
    return best


def _sigmoid(x):
    return 1.0 / (1.0 + jnp.exp(-x))


def _log_sigmoid(x):
    return jnp.minimum(x, 0.0) - jnp.log(1.0 + jnp.exp(-jnp.abs(x)))


def _rms(x):
    r = lax.rsqrt(jnp.mean(x * x, axis=-1, keepdims=True) + EPS)
    return x * r, r


def _rows(ts, w):
    return pl.BlockSpec((ts, w), lambda i: (i, 0))


def _fixed(shape):
    nd = len(shape)
    return pl.BlockSpec(shape, lambda i: (0,) * nd)


def _mm(a, b, *, ta=False, tb=False, out_dtype=F32, tm=1024, tn=1024, tk=1024, shards=1, after=(), name):
    m, kdim = (a.shape[1], a.shape[0]) if ta else a.shape
    n = b.shape[0] if tb else b.shape[1]
    assert (b.shape[1] if tb else b.shape[0]) == kdim, (a.shape, b.shape, ta, tb)
    tm = _tile(m, tm)
    tn = n // shards if shards > 1 else _tile(n, tn)
    tk = _tile(kdim, tk)
    nk = kdim // tk
    dims = (((0 if ta else 1,), (1 if tb else 0,)), ((), ()))

    def body(a_ref, b_ref, *rest):
        o_ref, *acc = rest[len(after):]
        part = lax.dot_general(a_ref[...], b_ref[...], dims, preferred_element_type=F32)
        if nk == 1:
            o_ref[...] = part.astype(o_ref.dtype)
            return
        acc_ref, = acc
        k = pl.program_id(2)

        @pl.when(k == 0)
        def _():
            acc_ref[...] = part

        @pl.when(k > 0)
        def _():
            acc_ref[...] += part

        @pl.when(k == nk - 1)
        def _():
            o_ref[...] = acc_ref[...].astype(o_ref.dtype)

    a_spec = pl.BlockSpec((tk, tm), lambda i, j, k: (k, i)) if ta else pl.BlockSpec((tm, tk), lambda i, j, k: (i, k))
    b_spec = pl.BlockSpec((tn, tk), lambda i, j, k: (j, k)) if tb else pl.BlockSpec((tk, tn), lambda i, j, k: (k, j))
    if shards > 1:
        out_shape = jax.ShapeDtypeStruct((shards, m, tn), out_dtype)
        o_spec = pl.BlockSpec((None, tm, tn), lambda i, j, k: (j, i, 0))
    else:
        out_shape = jax.ShapeDtypeStruct((m, n), out_dtype)
        o_spec = pl.BlockSpec((tm, tn), lambda i, j, k: (i, j))
    return pl.pallas_call(
        body, grid=(m // tm, n // tn, nk), in_specs=[a_spec, b_spec] + [ANY] * len(after), out_specs=o_spec, out_shape=out_shape,
        scratch_shapes=[pltpu.VMEM((tm, tn), F32)] if nk > 1 else [],
        compiler_params=_params(("parallel", "parallel", "arbitrary")), name=name,
    )(a, b, *after)


def _norm_fwd(x, g, out_dtype, name, after=()):
    s, d = x.shape
    ts = _tile(s, TS_ROW, 8)

    def body(x_ref, g_ref, *rest):
        o_ref = rest[len(after)]
        xh, _ = _rms(x_ref[...])
        o_ref[...] = (xh * g_ref[...]).astype(o_ref.dtype)

    return pl.pallas_call(
        body, grid=(s // ts,), in_specs=[_rows(ts, d), _fixed((1, d))] + [ANY] * len(after), out_specs=_rows(ts, d),
        out_shape=jax.ShapeDtypeStruct((s, d), out_dtype), compiler_params=_params(("parallel",)), name=name,
    )(x, g, *after)


def _resid_norm_fwd(x, f, g_post, alpha, g_next, name):
    s, d = x.shape
    ts = _tile(s, TS_ROW, 8)
    with_h = g_next is not None

    def body(x_ref, f_ref, gp_ref, *rest):
        fh, _ = _rms(f_ref[...])
        xn = x_ref[...] + alpha * (fh * gp_ref[...])
        if with_h:
            gn_ref, xo_ref, h_ref = rest
            xh, _ = _rms(xn)
            h_ref[...] = (xh * gn_ref[...]).astype(h_ref.dtype)
        else:
            xo_ref, = rest
        xo_ref[...] = xn

    ins = [x, f, g_post] + ([g_next] if with_h else [])
    in_specs = [_rows(ts, d), _rows(ts, d), _fixed((1, d))] + ([_fixed((1, d))] if with_h else [])
    out_shape = [jax.ShapeDtypeStruct((s, d), F32)] + ([jax.ShapeDtypeStruct((s, d), BF16)] if with_h else [])
    out_specs = [_rows(ts, d)] + ([_rows(ts, d)] if with_h else [])
    out = pl.pallas_call(
        body, grid=(s // ts,), in_specs=in_specs, out_specs=out_specs, out_shape=out_shape,
        compiler_params=_params(("parallel",)), name=name,
    )(*ins)
    return (out[0], out[1]) if with_h else (out[0], None)


def _rms_bwd(x, g, dys, dres, alpha, out_dtype, name):
    s, d = x.shape
    ts = _tile(s, TS_ROW, 8)
    ndy = len(dys)
    with_res = dres is not None

    def body(x_ref, g_ref, *rest):
        dy_refs = rest[:ndy]
        rest = rest[ndy:]
        if with_res:
            dres_ref, dx_ref, dg_ref = rest
        else:
            dx_ref, dg_ref = rest
        xh, r = _rms(x_ref[...])
        dy = dy_refs[0][...].astype(F32)
        for ref in dy_refs[1:]:
            dy = dy + ref[...].astype(F32)
        dy = dy * alpha

        @pl.when(pl.program_id(0) == 0)
        def _():
            dg_ref[...] = jnp.zeros_like(dg_ref)

        dg_ref[...] += jnp.sum(dy * xh, axis=0, keepdims=True)
        dyg = dy * g_ref[...]
        dx = r * (dyg - xh * jnp.mean(dyg * xh, axis=-1, keepdims=True))
        if with_res:
            dx = dx + dres_ref[...]
        dx_ref[...] = dx.astype(dx_ref.dtype)

    ins = [x, g] + list(dys) + ([dres] if with_res else [])
    in_specs = [_rows(ts, d), _fixed((1, d))] + [_rows(ts, d)] * (ndy + int(with_res))
    return pl.pallas_call(
        body, grid=(s // ts,), in_specs=in_specs, out_specs=[_rows(ts, d), _fixed((1, d))],
        out_shape=[jax.ShapeDtypeStruct((s, d), out_dtype), jax.ShapeDtypeStruct((1, d), F32)],
        compiler_params=_params(("arbitrary",)), name=name,
    )(*ins)


def _loss_bwd(x, g, target, name):
    s, d = x.shape
    ts = _tile(s, TS_ROW, 8)

    def body(x_ref, g_ref, t_ref, dx_ref, dg_ref, loss_ref):
        xh, r = _rms(x_ref[...])
        gv = g_ref[...]
        diff = xh * gv - t_ref[...]

        @pl.when(pl.program_id(0) == 0)
        def _():
            dg_ref[...] = jnp.zeros_like(dg_ref)
            loss_ref[...] = jnp.zeros_like(loss_ref)

        sq = jnp.sum(diff * diff, axis=1, keepdims=True)
        loss_ref[...] += (0.5 / d) * jnp.sum(sq, axis=0, keepdims=True)
        dy = diff * (1.0 / d)
        dg_ref[...] += jnp.sum(dy * xh, axis=0, keepdims=True)
        dyg = dy * gv
        dx_ref[...] = r * (dyg - xh * jnp.mean(dyg * xh, axis=-1, keepdims=True))

    return pl.pallas_call(
        body, grid=(s // ts,), in_specs=[_rows(ts, d), _fixed((1, d)), _rows(ts, d)],
        out_specs=[_rows(ts, d), _fixed((1, d)), _fixed((8, LANE))],
        out_shape=[jax.ShapeDtypeStruct((s, d), F32), jax.ShapeDtypeStruct((1, d), F32), jax.ShapeDtypeStruct((8, LANE), F32)],
        compiler_params=_params(("arbitrary",)), name=name,
    )(x, g, target)


def _swiglu_fwd(ab, name):
    s = ab.shape[0]
    ts = _tile(s, TS_ROW, 8)

    def body(a_ref, b_ref, u_ref):
        a = a_ref[...].astype(F32)
        u_ref[...] = (a * _sigmoid(a) * b_ref[...].astype(F32)).astype(u_ref.dtype)

    return pl.pallas_call(
        body, grid=(s // ts,),
        in_specs=[pl.BlockSpec((ts, DFF), lambda i: (i, 0)), pl.BlockSpec((ts, DFF), lambda i: (i, 1))],
        out_specs=_rows(ts, DFF), out_shape=jax.ShapeDtypeStruct((s, DFF), BF16),
        compiler_params=_params(("parallel",)), name=name,
    )(ab, ab)


def _swiglu_bwd(ab, du, name):
    s = ab.shape[0]
    ts = _tile(s, TS_ROW, 8)

    def body(a_ref, b_ref, du_ref, dab_ref):
        a = a_ref[...].astype(F32)
        b = b_ref[...].astype(F32)
        dy = du_ref[...].astype(F32)
        sig = _sigmoid(a)
        dab_ref[:, 0:DFF] = (dy * b * (sig * (1.0 + a * (1.0 - sig)))).astype(dab_ref.dtype)
        dab_ref[:, DFF:2 * DFF] = (dy * a * sig).astype(dab_ref.dtype)

    return pl.pallas_call(
        body, grid=(s // ts,),
        in_specs=[pl.BlockSpec((ts, DFF), lambda i: (i, 0)), pl.BlockSpec((ts, DFF), lambda i: (i, 1)), _rows(ts, DFF)],
        out_specs=_rows(ts, 2 * DFF), out_shape=jax.ShapeDtypeStruct((s, 2 * DFF), BF16),
        compiler_params=_params(("parallel",)), name=name,
    )(ab, ab, du)


def _tri(strict):
    r = lax.broadcasted_iota(jnp.int32, (CHUNK, CHUNK), 0)
    c = lax.broadcasted_iota(jnp.int32, (CHUNK, CHUNK), 1)
    return (r > c).astype(F32) if strict else (r >= c).astype(F32)


def _gla_fwd(pg, wfu, b_f, gnorm, name):
    s = pg.shape[0]
    ts = _tile(s, TS_GLA, CHUNK)
    cpb = ts // CHUNK
    nc = s // CHUNK

    def body(pg_ref, wfu_ref, bf_ref, gn_ref, ya_ref, sp_ref, st_ref, la_ref, dec_ref, u_ref):
        @pl.when(pl.program_id(0) == 0)
        def _():
            st_ref[...] = jnp.zeros_like(st_ref)

        f = jnp.dot(pg_ref[:, PG_F:PG_W], wfu_ref[...], preferred_element_type=F32) + bf_ref[...]
        la_ref[...] = _log_sigmoid(f) * (1.0 / GATE_TEMP)
        tri = _tri(False)
        chunks = [slice(ci * CHUNK, (ci + 1) * CHUNK) for ci in range(cpb)]
        for ci, rows in enumerate(chunks):
            la = la_ref[rows, :]
            b = jnp.dot(tri, la, precision=HIGHEST, preferred_element_type=F32)
            bend = jnp.sum(la, axis=0, keepdims=True)
            e = jnp.exp(bend - b)
            dec_ref[ci:ci + 1, :] = jnp.exp(bend)
            for hd in range(HEADS):
                k = pg_ref[rows, PG_K + hd * HDK:PG_K + (hd + 1) * HDK]
                v = pg_ref[rows, PG_V + hd * HDV:PG_V + (hd + 1) * HDV]
                kt = (k.astype(F32) * e[:, hd * HDK:(hd + 1) * HDK]).astype(BF16)
                u_ref[ci, hd] = lax.dot_general(v, kt, (((0,), (0,)), ((), ())), preferred_element_type=F32)
        for ci in range(cpb):
            for hd in range(HEADS):
                prev = st_ref[hd]
                sp_ref[ci, hd] = prev
                st = prev * dec_ref[ci:ci + 1, hd * HDK:(hd + 1) * HDK] + u_ref[ci, hd]
                st_ref[hd] = st
                u_ref[ci, hd] = st
        for ci, rows in enumerate(chunks):
            for hd in range(HEADS):
                vc = slice(hd * HDV, (hd + 1) * HDV)
                q = pg_ref[rows, PG_Q + hd * HDK:PG_Q + (hd + 1) * HDK]
                go = pg_ref[rows, PG_G + hd * HDV:PG_G + (hd + 1) * HDV].astype(F32)
                qs = (q.astype(F32) * Q_SCALE).astype(BF16)
                o = lax.dot_general(qs, u_ref[ci, hd].astype(BF16), (((1,), (1,)), ((), ())), preferred_element_type=F32)
                oh, _ = _rms(o)
                ya_ref[rows, vc] = (oh * gn_ref[:, vc] * (go * _sigmoid(go))).astype(ya_ref.dtype)

    return pl.pallas_call(
        body, grid=(s // ts,),
        in_specs=[_rows(ts, PG_W), _fixed((LANE, HEADS * HDK)), _fixed((1, HEADS * HDK)), _fixed((1, HEADS * HDV))],
        out_specs=[_rows(ts, HEADS * HDV), pl.BlockSpec((cpb, HEADS, HDV, HDK), lambda i: (i, 0, 0, 0))],
        out_shape=[jax.ShapeDtypeStruct((s, HEADS * HDV), BF16), jax.ShapeDtypeStruct((nc, HEADS, HDV, HDK), F32)],
        scratch_shapes=[pltpu.VMEM((HEADS, HDV, HDK), F32), pltpu.VMEM((ts, HEADS * HDK), F32),
                        pltpu.VMEM((max(cpb, 8), HEADS * HDK), F32), pltpu.VMEM((cpb, HEADS, HDV, HDK), F32)],
        compiler_params=_params(("arbitrary",)), name=name,
    )(pg, wfu, b_f, gnorm)


def _gla_bwd(pg, sp, dya, wfu, b_f, gnorm, name):
    s = pg.shape[0]
    ts = _tile(s, TS_GLA, CHUNK)
    cpb = ts // CHUNK
    nblk = s // ts

    def body(pg_ref, sp_ref, dya_ref, wfu_ref, bf_ref, gn_ref, dpg_ref, dwfu_ref, dbf_ref, dgn_ref,
             dst_ref, la_ref, sg_ref, df_ref, e_ref, ktf_ref, dec_ref, g_ref):
        @pl.when(pl.program_id(0) == 0)
        def _():
            dst_ref[...] = jnp.zeros_like(dst_ref)
            dwfu_ref[...] = jnp.zeros_like(dwfu_ref)
            dbf_ref[...] = jnp.zeros_like(dbf_ref)
            dgn_ref[...] = jnp.zeros_like(dgn_ref)

        flow = pg_ref[:, PG_F:PG_W]
        f = jnp.dot(flow, wfu_ref[...], preferred_element_type=F32) + bf_ref[...]
        la_ref[...] = _log_sigmoid(f) * (1.0 / GATE_TEMP)
        sg_ref[...] = _sigmoid(-f) * (1.0 / GATE_TEMP)
        tri = _tri(False)
        tri_strict = _tri(True)
        chunks = [slice(ci * CHUNK, (ci + 1) * CHUNK) for ci in range(cpb)]
        for ci, rows in enumerate(chunks):
            la = la_ref[rows, :]
            b = jnp.dot(tri, la, precision=HIGHEST, preferred_element_type=F32)
            bend = jnp.sum(la, axis=0, keepdims=True)
            e = jnp.exp(bend - b)
            e_ref[rows, :] = e
            dec = jnp.exp(bend)
            dec_ref[ci:ci + 1, :] = dec
            for hd in range(HEADS):
                kc = slice(hd * HDK, (hd + 1) * HDK)
                vc = slice(hd * HDV, (hd + 1) * HDV)
                q = pg_ref[rows, PG_Q + hd * HDK:PG_Q + (hd + 1) * HDK]
                k = pg_ref[rows, PG_K + hd * HDK:PG_K + (hd + 1) * HDK]
                v = pg_ref[rows, PG_V + hd * HDV:PG_V + (hd + 1) * HDV]
                go = pg_ref[rows, PG_G + hd * HDV:PG_G + (hd + 1) * HDV].astype(F32)
                ktf = k.astype(F32) * e[:, kc]
                ktf_ref[rows, kc] = ktf
                st = sp_ref[ci, hd] * dec[:, kc] + lax.dot_general(v, ktf.astype(BF16), (((0,), (0,)), ((), ())), preferred_element_type=F32)
                st_b = st.astype(BF16)
                qs = (q.astype(F32) * Q_SCALE).astype(BF16)
                o = lax.dot_general(qs, st_b, (((1,), (1,)), ((), ())), preferred_element_type=F32)
                oh, r = _rms(o)
                gh = gn_ref[:, vc]
                sig = _sigmoid(go)
                dy = dya_ref[rows, vc].astype(F32)
                don = dy * (go * sig)
                dgn_ref[:, vc] += jnp.sum(don * oh, axis=0, keepdims=True)
                dong = don * gh
                do = (r * (dong - oh * jnp.mean(dong * oh, axis=-1, keepdims=True))).astype(BF16)
                g_ref[ci, hd] = lax.dot_general(do, qs, (((0,), (0,)), ((), ())), preferred_element_type=F32)
                dq = jnp.dot(do, st_b, preferred_element_type=F32) * Q_SCALE
                dpg_ref[rows, PG_Q + hd * HDK:PG_Q + (hd + 1) * HDK] = dq.astype(dpg_ref.dtype)
                dgo = dy * (oh * gh) * (sig * (1.0 + go * (1.0 - sig)))
                dpg_ref[rows, PG_G + hd * HDV:PG_G + (hd + 1) * HDV] = dgo.astype(dpg_ref.dtype)
        for ci in reversed(range(cpb)):
            for hd in range(HEADS):
                dst = dst_ref[hd] + g_ref[ci, hd]
                g_ref[ci, hd] = dst
                dst_ref[hd] = dst * dec_ref[ci:ci + 1, hd * HDK:(hd + 1) * HDK]
        for ci, rows in enumerate(chunks):
            for hd in range(HEADS):
                kc = slice(hd * HDK, (hd + 1) * HDK)
                v = pg_ref[rows, PG_V + hd * HDV:PG_V + (hd + 1) * HDV]
                ktf = ktf_ref[rows, kc]
                dst = g_ref[ci, hd]
                dst_b = dst.astype(BF16)
                dkt = jnp.dot(v, dst_b, preferred_element_type=F32)
                dv = lax.dot_general(ktf.astype(BF16), dst_b, (((1,), (1,)), ((), ())), preferred_element_type=F32)
                dd = jnp.sum(dst * sp_ref[ci, hd], axis=0, keepdims=True)
                dla = jnp.dot(tri_strict, dkt * ktf, precision=HIGHEST, preferred_element_type=F32) + dd * dec_ref[ci:ci + 1, kc]
                df_ref[rows, kc] = dla * sg_ref[rows, kc]
                dpg_ref[rows, PG_K + hd * HDK:PG_K + (hd + 1) * HDK] = (dkt * e_ref[rows, kc]).astype(dpg_ref.dtype)
                dpg_ref[rows, PG_V + hd * HDV:PG_V + (hd + 1) * HDV] = dv.astype(dpg_ref.dtype)
        df = df_ref[...]
        df_b = df.astype(BF16)
        dpg_ref[:, PG_F:PG_W] = lax.dot_general(df_b, wfu_ref[...], (((1,), (1,)), ((), ())), preferred_element_type=F32).astype(dpg_ref.dtype)
        dwfu_ref[...] += lax.dot_general(flow, df_b, (((0,), (0,)), ((), ())), preferred_element_type=F32)
        dbf_ref[...] += jnp.sum(df, axis=0, keepdims=True)

    rev = lambda i: (nblk - 1 - i, 0)
    return pl.pallas_call(
        body, grid=(nblk,),
        in_specs=[pl.BlockSpec((ts, PG_W), rev), pl.BlockSpec((cpb, HEADS, HDV, HDK), lambda i: (nblk - 1 - i, 0, 0, 0)),
                  pl.BlockSpec((ts, HEADS * HDV), rev), _fixed((LANE, HEADS * HDK)), _fixed((1, HEADS * HDK)), _fixed((1, HEADS * HDV))],
        out_specs=[pl.BlockSpec((ts, PG_W), rev), _fixed((LANE, HEADS * HDK)), _fixed((1, HEADS * HDK)), _fixed((1, HEADS * HDV))],
        out_shape=[jax.ShapeDtypeStruct((s, PG_W), BF16), jax.ShapeDtypeStruct((LANE, HEADS * HDK), F32),
                   jax.ShapeDtypeStruct((1, HEADS * HDK), F32), jax.ShapeDtypeStruct((1, HEADS * HDV), F32)],
        scratch_shapes=[pltpu.VMEM((HEADS, HDV, HDK), F32)] + [pltpu.VMEM((ts, HEADS * HDK), F32)] * 5
        + [pltpu.VMEM((max(cpb, 8), HEADS * HDK), F32), pltpu.VMEM((cpb, HEADS, HDV, HDK), F32)],
        compiler_params=_params(("arbitrary",)), name=name,
    )(pg, sp, dya, wfu, b_f, gnorm)


def _window_sums(ext, sign):
    n = ext.shape[0]
    sums = {1: ext}
    w = 1
    while w < POOL_WINDOWS[-1]:
        sums[2 * w] = sums[w] + pltpu.roll(sums[w], w if sign > 0 else n - w, 0)
        w *= 2
    return [sums[POOL_WINDOWS[g]][:, g * LANE:(g + 1) * LANE] for g in range(len(POOL_WINDOWS))]


def _pool_counts(row0, n):
    pos = (row0 + lax.broadcasted_iota(jnp.int32, (n, 1), 0) + 1).astype(F32)
    return [jnp.minimum(pos, float(w)) for w in POOL_WINDOWS]


def _pool_fwd(ppx, w_pool, pool_scale, name):
    s = ppx.shape[0]
    ts = _tile(s, TS_POOL, POOL_HALO)
    hb = ts // POOL_HALO
    pw = len(POOL_WINDOWS) * LANE

    def body(p_ref, halo_ref, w_ref, sc_ref, y_ref, ext_ref):
        i = pl.program_id(0)
        p = p_ref[...].astype(F32)
        ext_ref[0:POOL_HALO, :] = jnp.where(i > 0, halo_ref[...].astype(F32), 0.0)
        ext_ref[POOL_HALO:, :] = p
        sums = _window_sums(ext_ref[...], +1)
        cnt = _pool_counts(i * ts, ts)
        for g in range(len(POOL_WINDOWS)):
            cols = slice(g * LANE, (g + 1) * LANE)
            mixed = sums[g][POOL_HALO:, :] / cnt[g] - p[:, cols]
            y = jnp.dot(mixed.astype(BF16), w_ref[g], preferred_element_type=F32)
            y_ref[:, cols] = (y * sc_ref[:, cols]).astype(y_ref.dtype)

    return pl.pallas_call(
        body, grid=(s // ts,),
        in_specs=[pl.BlockSpec((ts, pw), lambda i: (i, 0)), pl.BlockSpec((POOL_HALO, pw), lambda i: (jnp.maximum(i * hb - 1, 0), 0)),
                  _fixed((len(POOL_WINDOWS), LANE, LANE)), _fixed((1, pw))],
        out_specs=_rows(ts, pw), out_shape=jax.ShapeDtypeStruct((s, pw), BF16),
        scratch_shapes=[pltpu.VMEM((ts + POOL_HALO, pw), F32)],
        compiler_params=_params(("parallel",)), name=name,
    )(ppx, ppx, w_pool, pool_scale)


def _pool_bwd(dyb, ppx, w_pool, pool_scale, name):
    s = ppx.shape[0]
    ts = _tile(s, TS_POOL, POOL_HALO)
    hb = ts // POOL_HALO
    nblk = s // ts
    last_halo = s // POOL_HALO - 1
    ng = len(POOL_WINDOWS)
    pw = ng * LANE

    def body(p_ref, halo_ref, dy_ref, dyn_ref, w_ref, sc_ref, dp_ref, dw_ref, dsc_ref, ext_ref, dext_ref, dm_ref):
        i = pl.program_id(0)

        @pl.when(i == 0)
        def _():
            dw_ref[...] = jnp.zeros_like(dw_ref)
            dsc_ref[...] = jnp.zeros_like(dsc_ref)

        p = p_ref[...].astype(F32)
        ext_ref[0:POOL_HALO, :] = jnp.where(i > 0, halo_ref[...].astype(F32), 0.0)
        ext_ref[POOL_HALO:, :] = p
        sums = _window_sums(ext_ref[...], +1)
        cnt = _pool_counts(i * ts, ts + POOL_HALO)
        sc = sc_ref[...]
        dy = dy_ref[...].astype(F32)
        dyn = jnp.where(i < nblk - 1, dyn_ref[...].astype(F32), 0.0)
        for g in range(ng):
            cols = slice(g * LANE, (g + 1) * LANE)
            wg = w_ref[g]
            mixed = (sums[g][POOL_HALO:, :] / cnt[g][0:ts] - p[:, cols]).astype(BF16)
            ypre = jnp.dot(mixed, wg, preferred_element_type=F32)
            dsc_ref[:, cols] += jnp.sum(dy[:, cols] * ypre, axis=0, keepdims=True)
            dyp = (dy[:, cols] * sc[:, cols]).astype(BF16)
            dypn = (dyn[:, cols] * sc[:, cols]).astype(BF16)
            dw_ref[g] += lax.dot_general(mixed, dyp, (((0,), (0,)), ((), ())), preferred_element_type=F32)
            dm = lax.dot_general(dyp, wg, (((1,), (1,)), ((), ())), preferred_element_type=F32)
            dmn = lax.dot_general(dypn, wg, (((1,), (1,)), ((), ())), preferred_element_type=F32)
            dext_ref[0:ts, cols] = dm / cnt[g][0:ts]
            dext_ref[ts:, cols] = dmn / cnt[g][ts:]
            dm_ref[:, cols] = dm
        lead = _window_sums(dext_ref[...], -1)
        for g in range(ng):
            cols = slice(g * LANE, (g + 1) * LANE)
            dp_ref[:, cols] = (lead[g][0:ts, :] - dm_ref[:, cols]).astype(dp_ref.dtype)

    return pl.pallas_call(
        body, grid=(nblk,),
        in_specs=[pl.BlockSpec((ts, pw), lambda i: (i, 0)), pl.BlockSpec((POOL_HALO, pw), lambda i: (jnp.maximum(i * hb - 1, 0), 0)),
                  pl.BlockSpec((ts, pw), lambda i: (i, 0)), pl.BlockSpec((POOL_HALO, pw), lambda i: (jnp.minimum((i + 1) * hb, last_halo), 0)),
                  _fixed((ng, LANE, LANE)), _fixed((1, pw))],
        out_specs=[_rows(ts, pw), _fixed((ng, LANE, LANE)), _fixed((1, pw))],
        out_shape=[jax.ShapeDtypeStruct((s, pw), BF16), jax.ShapeDtypeStruct((ng, LANE, LANE), F32), jax.ShapeDtypeStruct((1, pw), F32)],
        scratch_shapes=[pltpu.VMEM((ts + POOL_HALO, pw), F32), pltpu.VMEM((ts + POOL_HALO, pw), F32), pltpu.VMEM((ts, pw), F32)],
        compiler_params=_params(("arbitrary",)), name=name,
    )(ppx, ppx, dyb, dyb, w_pool, pool_scale)


def _xattn_fwd(ppx, kv, name):
    s = ppx.shape[0]
    m = kv.shape[0]
    ts = _tile(s, TS_XA, 8)
    xw = XA_HEADS * XA_HD

    def body(q_ref, kv_ref, o_ref):
        for hd in range(XA_HEADS):
            cols = slice(hd * XA_HD, (hd + 1) * XA_HD)
            k = kv_ref[:, hd * XA_HD:(hd + 1) * XA_HD]
            v = kv_ref[:, xw + hd * XA_HD:xw + (hd + 1) * XA_HD]
            sc = lax.dot_general(q_ref[:, cols], k, (((1,), (1,)), ((), ())), preferred_element_type=F32) * XA_SCALE
            ex = jnp.exp(sc - jnp.max(sc, axis=-1, keepdims=True))
            pr = ex / jnp.sum(ex, axis=-1, keepdims=True)
            o_ref[:, cols] = jnp.dot(pr.astype(BF16), v, preferred_element_type=F32).astype(o_ref.dtype)

    return pl.pallas_call(
        body, grid=(s // ts,), in_specs=[pl.BlockSpec((ts, xw), lambda i: (i, 1)), _fixed((m, 2 * xw))],
        out_specs=_rows(ts, xw), out_shape=jax.ShapeDtypeStruct((s, xw), BF16),
        compiler_params=_params(("parallel",)), name=name,
    )(ppx, kv)


def _xattn_bwd(dxc, ppx, kv, name):
    s = ppx.shape[0]
    m = kv.shape[0]
    ts = _tile(s, TS_XA, 8)
    xw = XA_HEADS * XA_HD

    def body(do_ref, q_ref, kv_ref, dq_ref, dkv_ref):
        @pl.when(pl.program_id(0) == 0)
        def _():
            dkv_ref[...] = jnp.zeros_like(dkv_ref)

        for hd in range(XA_HEADS):
            cols = slice(hd * XA_HD, (hd + 1) * XA_HD)
            vcols = slice(xw + hd * XA_HD, xw + (hd + 1) * XA_HD)
            q = q_ref[:, cols]
            k = kv_ref[:, cols]
            v = kv_ref[:, vcols]
            do = do_ref[:, cols]
            sc = lax.dot_general(q, k, (((1,), (1,)), ((), ())), preferred_element_type=F32) * XA_SCALE
            ex = jnp.exp(sc - jnp.max(sc, axis=-1, keepdims=True))
            pr = ex / jnp.sum(ex, axis=-1, keepdims=True)
            dpr = lax.dot_general(do, v, (((1,), (1,)), ((), ())), preferred_element_type=F32)
            dsc = (pr * (dpr - jnp.sum(dpr * pr, axis=-1, keepdims=True)) * XA_SCALE).astype(BF16)
            dq_ref[:, cols] = jnp.dot(dsc, k, preferred_element_type=F32).astype(dq_ref.dtype)
            dkv_ref[:, cols] += lax.dot_general(dsc, q, (((0,), (0,)), ((), ())), preferred_element_type=F32)
            dkv_ref[:, vcols] += lax.dot_general(pr.astype(BF16), do, (((0,), (0,)), ((), ())), preferred_element_type=F32)

    return pl.pallas_call(
        body, grid=(s // ts,), in_specs=[_rows(ts, xw), pl.BlockSpec((ts, xw), lambda i: (i, 1)), _fixed((m, 2 * xw))],
        out_specs=[_rows(ts, xw), _fixed((m, 2 * xw))],
        out_shape=[jax.ShapeDtypeStruct((s, xw), BF16), jax.ShapeDtypeStruct((m, 2 * xw), F32)],
        compiler_params=_params(("arbitrary",)), name=name,
    )(dxc, ppx, kv)


def _merge_fwd(pgt, ya, yb, yc, name):
    s = pgt.shape[0]
    ts = _tile(s, TS_ROW, 8)

    def body(gt_ref, ya_ref, yb_ref, yc_ref, o_ref):
        acc = _sigmoid(gt_ref[:, 0:D].astype(F32)) * ya_ref[...].astype(F32)
        acc = acc + _sigmoid(gt_ref[:, D:2 * D].astype(F32)) * yb_ref[...].astype(F32)
        acc = acc + _sigmoid(gt_ref[:, 2 * D:3 * D].astype(F32)) * yc_ref[...].astype(F32)
        o_ref[...] = acc.astype(o_ref.dtype)

    return pl.pallas_call(
        body, grid=(s // ts,), in_specs=[_rows(ts, 3 * D)] + [_rows(ts, D)] * 3, out_specs=_rows(ts, D),
        out_shape=jax.ShapeDtypeStruct((s, D), BF16), compiler_params=_params(("parallel",)), name=name,
    )(pgt, ya, yb, yc)


def _merge_bwd(dmerged, pgt, ya, yb, yc, name):
    s = pgt.shape[0]
    ts = _tile(s, TS_ROW, 8)

    def body(dm_ref, gt_ref, ya_ref, yb_ref, yc_ref, dya_ref, dyb_ref, dyc_ref, dgt_ref):
        dm = dm_ref[...].astype(F32)
        for j, (y_ref, dy_ref) in enumerate(((ya_ref, dya_ref), (yb_ref, dyb_ref), (yc_ref, dyc_ref))):
            sig = _sigmoid(gt_ref[:, j * D:(j + 1) * D].astype(F32))
            dy_ref[...] = (dm * sig).astype(dy_ref.dtype)
            dgt_ref[:, j * D:(j + 1) * D] = (dm * y_ref[...].astype(F32) * sig * (1.0 - sig)).astype(dgt_ref.dtype)

    return pl.pallas_call(
        body, grid=(s // ts,), in_specs=[_rows(ts, D), _rows(ts, 3 * D)] + [_rows(ts, D)] * 3,
        out_specs=[_rows(ts, D)] * 3 + [_rows(ts, 3 * D)],
        out_shape=[jax.ShapeDtypeStruct((s, D), BF16)] * 3 + [jax.ShapeDtypeStruct((s, 3 * D), BF16)],
        compiler_params=_params(("parallel",)), name=name,
    )(dmerged, pgt, ya, yb, yc)


def _adamw(w, g, m, v, name):
    r, c = w.shape[-2:]
    tr = _tile(r, 256, 8)

    def spec(a):
        return _rows(tr, c) if a.ndim == 2 else pl.BlockSpec((None, tr, c), lambda i: (0, i, 0))

    def body(w_ref, g_ref, m_ref, v_ref, d_ref, mo_ref, vo_ref):
        gv = g_ref[...]
        mn = ADAM_B1 * m_ref[...] + (1.0 - ADAM_B1) * gv
        vn = ADAM_B2 * v_ref[...] + (1.0 - ADAM_B2) * (gv * gv)
        m_hat = mn / (1.0 - ADAM_B1 ** ADAM_STEP)
        v_hat = vn / (1.0 - ADAM_B2 ** ADAM_STEP)
        d_ref[...] = -ADAM_LR * (m_hat / (jnp.sqrt(v_hat) + ADAM_EPS) + ADAM_WD * w_ref[...])
        mo_ref[...] = mn
        vo_ref[...] = vn

    return pl.pallas_call(
        body, grid=(r // tr,), in_specs=[spec(a) for a in (w, g, m, v)], out_specs=[spec(w)] * 3,
        out_shape=[jax.ShapeDtypeStruct(w.shape, F32)] * 3, compiler_params=_params(("parallel",)), name=name,
    )(w, g, m, v)


ANY = pl.BlockSpec(memory_space=pl.ANY)


def _place():
    x, y, c = lax.axis_index("x"), lax.axis_index("y"), lax.axis_index("c")
    chips = [(1 - x, y), (x, 1 - y), (1 - x, 1 - y)]
    return x, y, c, chips


def _half(c, rows):
    h = rows // 2
    return pl.ds(pl.multiple_of(c * h, 8), h)


def _place_shard(shard, chip_arr, out_dtype, name):
    _, r, cols = shard.shape
    tr = _tile(r, 256, 16)

    def body(chip_ref, s_ref, o_ref):
        o_ref[...] = s_ref[...].astype(o_ref.dtype)

    return pl.pallas_call(
        body,
        grid_spec=pltpu.PrefetchScalarGridSpec(
            num_scalar_prefetch=1, grid=(r // tr,),
            in_specs=[pl.BlockSpec((None, tr, cols), lambda i, chip_ref: (0, i, 0))],
            out_specs=pl.BlockSpec((None, tr, cols), lambda i, chip_ref: (chip_ref[0], i, 0))),
        out_shape=jax.ShapeDtypeStruct((4, r, cols), out_dtype),
        compiler_params=_params(("parallel",)), name=name,
    )(chip_arr, shard)


def _gather_shards(bufs, name):
    n = len(bufs)

    def body(*refs):
        outs = refs[n:2 * n]
        send_ici, recv_ici, send_d2d, recv_d2d = refs[2 * n:]
        x, y, c, chips = _place()
        me = 2 * x + y
        sibling = (x, y, 1 - c)

        def ici(w, p, chip_of_block, to):
            rows = _half(c, outs[w].shape[1])
            block = outs[w].at[chip_of_block, rows]
            return pltpu.make_async_remote_copy(
                src_ref=block, dst_ref=block, send_sem=send_ici.at[w, p], recv_sem=recv_ici.at[w, p], device_id=to, device_id_type=MESH)

        def d2d(w, p, chip_of_block, half_of):
            rows = _half(half_of, outs[w].shape[1])
            block = outs[w].at[chip_of_block, rows]
            return pltpu.make_async_remote_copy(
                src_ref=block, dst_ref=block, send_sem=send_d2d.at[w, p], recv_sem=recv_d2d.at[w, p], device_id=sibling, device_id_type=MESH)

        sends = [ici(w, p, me, (*chip, c)) for p, chip in enumerate(chips) for w in range(n)]
        for cp in sends:
            cp.start()
        passed = []
        for p, (px, py) in enumerate(chips):
            for w in range(n):
                ici(w, p, 2 * px + py, (px, py, c)).wait_recv()
                fwd = d2d(w, p, 2 * px + py, c)
                fwd.start()
                passed.append(fwd)
        for p, (px, py) in enumerate(chips):
            for w in range(n):
                d2d(w, p, 2 * px + py, 1 - c).wait_recv()
        for cp in sends + passed:
            cp.wait_send()

    return pl.pallas_call(
        body, in_specs=[ANY] * n, out_specs=[ANY] * n,
        out_shape=[jax.ShapeDtypeStruct(a.shape, a.dtype) for a in bufs],
        input_output_aliases={w: w for w in range(n)},
        scratch_shapes=[pltpu.SemaphoreType.DMA((n, 3))] * 4,
        compiler_params=pltpu.CompilerParams(has_side_effects=True), name=name,
    )(*bufs)


HBM = pl.BlockSpec(memory_space=pltpu.HBM)
SEM = pl.BlockSpec(memory_space=pltpu.SEMAPHORE)
EFFECT = pltpu.SideEffectType.DATAFLOW_SIDE_EFFECTING


def _in_hbm(arrays):
    return [pltpu.with_memory_space_constraint(a, pltpu.HBM) for a in arrays]


def _gather_start(bufs, after, name):
    n, na = len(bufs), len(after)

    def body(*refs):
        send_sem, recv_sem = refs[n + na], refs[n + na + 1]
        outs = refs[n + na + 2:2 * n + na + 2]
        token = refs[2 * n + na + 2]
        x, y, c, chips = _place()
        me = 2 * x + y
        for p, chip in enumerate(chips):
            for w in range(n):
                block = outs[w].at[me, _half(c, outs[w].shape[1])]
                pltpu.make_async_remote_copy(
                    src_ref=block, dst_ref=block, send_sem=send_sem, recv_sem=recv_sem,
                    device_id=(*chip, c), device_id_type=MESH).start()
        token[...] = jnp.zeros_like(token)

    out = pl.pallas_call(
        body, name=name, in_specs=[HBM] * n + [ANY] * na,
        out_specs=[SEM, SEM] + [HBM] * n + [pl.BlockSpec(memory_space=pltpu.VMEM)],
        out_shape=[pltpu.SemaphoreType.DMA(()), pltpu.SemaphoreType.DMA(())]
        + [pltpu.HBM(a.shape, a.dtype) for a in bufs] + [jax.ShapeDtypeStruct((8, LANE), F32)],
        input_output_aliases={w: w + 2 for w in range(n)},
        compiler_params=pltpu.CompilerParams(has_side_effects=EFFECT),
    )(*_in_hbm(bufs), *after)
    return out[0], out[1], list(out[2:2 + n]), out[2 + n]


def _gather_pass(bufs, send_sem, recv_sem, after, name):
    n, na = len(bufs), len(after)

    def body(*refs):
        send1, recv1 = refs[n], refs[n + 1]
        send2, recv2 = refs[n + 2 + na], refs[n + 3 + na]
        outs = refs[n + 4 + na:2 * n + 4 + na]
        x, y, c, chips = _place()
        me = 2 * x + y
        arrivals = [(w, px, py) for px, py in chips for w in range(n)]
        for w, px, py in arrivals:
            rows = _half(c, outs[w].shape[1])
            first = pltpu.make_async_remote_copy(
                src_ref=outs[w].at[me, rows], dst_ref=outs[w].at[2 * px + py, rows], send_sem=send1, recv_sem=recv1,
                device_id=(px, py, c), device_id_type=MESH)
            first.wait_send()
            first.wait_recv()
        for w, px, py in arrivals:
            arrived = outs[w].at[2 * px + py, _half(c, outs[w].shape[1])]
            pltpu.make_async_remote_copy(
                src_ref=arrived, dst_ref=arrived, send_sem=send2, recv_sem=recv2,
                device_id=(x, y, 1 - c), device_id_type=MESH).start()

    out = pl.pallas_call(
        body, name=name, in_specs=[HBM] * n + [SEM, SEM] + [ANY] * na,
        out_specs=[SEM, SEM] + [HBM] * n,
        out_shape=[pltpu.SemaphoreType.DMA(()), pltpu.SemaphoreType.DMA(())] + [pltpu.HBM(a.shape, a.dtype) for a in bufs],
        input_output_aliases={w: w + 2 for w in range(n)},
        compiler_params=pltpu.CompilerParams(has_side_effects=EFFECT),
    )(*bufs, send_sem, recv_sem, *after)
    return out[0], out[1], list(out[2:])


def _gather_finish(bufs, send_sem, recv_sem, after, name):
    n, na = len(bufs), len(after)

    def body(*refs):
        send2, recv2 = refs[n], refs[n + 1]
        outs = refs[n + 2 + na:2 * n + 2 + na]
        x, y, c, chips = _place()
        for p, (px, py) in enumerate(chips):
            for w in range(n):
                r = outs[w].shape[1]
                passed = pltpu.make_async_remote_copy(
                    src_ref=outs[w].at[2 * px + py, _half(c, r)], dst_ref=outs[w].at[2 * px + py, _half(1 - c, r)],
                    send_sem=send2, recv_sem=recv2, device_id=(x, y, 1 - c), device_id_type=MESH)
                passed.wait_send()
                passed.wait_recv()

    out = pl.pallas_call(
        body, name=name, in_specs=[HBM] * n + [SEM, SEM] + [ANY] * na, out_specs=[HBM] * n,
        out_shape=[pltpu.HBM(a.shape, a.dtype) for a in bufs],
        input_output_aliases={w: w for w in range(n)},
        compiler_params=pltpu.CompilerParams(has_side_effects=EFFECT),
    )(*bufs, send_sem, recv_sem, *after)
    return list(out)


def _pair_exchange(grads, name):
    n = len(grads)

    def body(*refs):
        ins, outs = refs[:n], refs[n:2 * n]
        send_sem, recv_sem = refs[2 * n:]
        x, y, c, _ = _place()
        copies = []
        for w in range(n):
            rows = _half(1 - c, ins[w].shape[1])
            copies.append(pltpu.make_async_remote_copy(
                src_ref=ins[w].at[:, rows], dst_ref=outs[w], send_sem=send_sem.at[w], recv_sem=recv_sem.at[w],
                device_id=(x, y, 1 - c), device_id_type=MESH))
        for cp in copies:
            cp.start()
        for cp in copies:
            cp.wait()

    return pl.pallas_call(
        body, in_specs=[ANY] * n, out_specs=[ANY] * n,
        out_shape=[jax.ShapeDtypeStruct((4, a.shape[1] // 2, a.shape[2]), a.dtype) for a in grads],
        scratch_shapes=[pltpu.SemaphoreType.DMA((n,))] * 2,
        compiler_params=pltpu.CompilerParams(has_side_effects=True), name=name,
    )(*grads)


def _pair_sum(g, got, c_arr, name):
    _, r, cols = g.shape
    h = r // 2
    th = _tile(h, 256, 16)
    nb = h // th

    def body(c_ref, g_ref, got_ref, o_ref):
        o_ref[...] = (g_ref[...].astype(F32) + got_ref[...].astype(F32)).astype(o_ref.dtype)

    return pl.pallas_call(
        body,
        grid_spec=pltpu.PrefetchScalarGridSpec(
            num_scalar_prefetch=1, grid=(4, nb),
            in_specs=[pl.BlockSpec((None, th, cols), lambda j, i, c_ref: (j, c_ref[0] * nb + i, 0)),
                      pl.BlockSpec((None, th, cols), lambda j, i, c_ref: (j, i, 0))],
            out_specs=pl.BlockSpec((None, th, cols), lambda j, i, c_ref: (j, i, 0))),
        out_shape=jax.ShapeDtypeStruct((4, h, cols), BF16),
        compiler_params=_params(("parallel", "parallel")), name=name,
    )(c_arr, g, got)


def _chip_exchange(parts, name):
    n = len(parts)

    def body(*refs):
        ins, outs = refs[:n], refs[n:2 * n]
        send_sem, recv_sem = refs[2 * n:]
        x, y, c, chips = _place()
        copies = []
        for p, (px, py) in enumerate(chips):
            for w in range(n):
                copies.append(pltpu.make_async_remote_copy(
                    src_ref=ins[w].at[2 * px + py], dst_ref=outs[w].at[p], send_sem=send_sem.at[w, p], recv_sem=recv_sem.at[w, p],
                    device_id=(px, py, c), device_id_type=MESH))
        for cp in copies:
            cp.start()
        for cp in copies:
            cp.wait()

    return pl.pallas_call(
        body, in_specs=[ANY] * n, out_specs=[ANY] * n,
        out_shape=[jax.ShapeDtypeStruct((3,) + a.shape[1:], a.dtype) for a in parts],
        scratch_shapes=[pltpu.SemaphoreType.DMA((n, 3))] * 2,
        compiler_params=pltpu.CompilerParams(has_side_effects=True), name=name,
    )(*parts)


def _chip_sum(part, got, place_arr, name):
    _, h, cols = part.shape
    th = _tile(h, 256, 16)
    nb = h // th

    def body(place_ref, p_ref, got_ref, o_ref):
        acc = p_ref[...].astype(F32)
        for p in range(3):
            acc = acc + got_ref[p].astype(F32)
        o_ref[...] = acc

    return pl.pallas_call(
        body,
        grid_spec=pltpu.PrefetchScalarGridSpec(
            num_scalar_prefetch=1, grid=(nb,),
            in_specs=[pl.BlockSpec((None, th, cols), lambda i, place_ref: (place_ref[0], i, 0)),
                      pl.BlockSpec((3, th, cols), lambda i, place_ref: (0, i, 0))],
            out_specs=pl.BlockSpec((th, cols), lambda i, place_ref: (place_ref[1] * nb + i, 0))),
        out_shape=jax.ShapeDtypeStruct((2 * h, cols), F32),
        compiler_params=_params(("parallel",)), name=name,
    )(place_arr, part, got)


def _pair_join(bufs, name):
    n = len(bufs)

    def body(*refs):
        outs = refs[n:2 * n]
        send_sem, recv_sem = refs[2 * n:]
        x, y, c, _ = _place()
        copies = []
        for w in range(n):
            block = outs[w].at[_half(c, outs[w].shape[0])]
            copies.append(pltpu.make_async_remote_copy(
                src_ref=block, dst_ref=block, send_sem=send_sem.at[w], recv_sem=recv_sem.at[w],
                device_id=(x, y, 1 - c), device_id_type=MESH))
        for cp in copies:
            cp.start()
        for w, cp in enumerate(copies):
            cp.wait_send()
            block = outs[w].at[_half(1 - c, outs[w].shape[0])]
            pltpu.make_async_remote_copy(
                src_ref=block, dst_ref=block, send_sem=send_sem.at[w], recv_sem=recv_sem.at[w],
                device_id=(x, y, 1 - c), device_id_type=MESH).wait_recv()

    return pl.pallas_call(
        body, in_specs=[ANY] * n, out_specs=[ANY] * n,
        out_shape=[jax.ShapeDtypeStruct(a.shape, a.dtype) for a in bufs],
        input_output_aliases={w: w for w in range(n)},
        scratch_shapes=[pltpu.SemaphoreType.DMA((n,))] * 2,
        compiler_params=pltpu.CompilerParams(has_side_effects=True), name=name,
    )(*bufs)


def _all_sum(pack, name):
    r, cols = pack.shape

    def body(x_ref, o_ref, all_ref, send_sems, recv_sems):
        x, y, c, chips = _place()
        me, sibling = (x, y, c), (x, y, 1 - c)

        def slot(px, py, pc):
            return all_ref.at[4 * px + 2 * py + pc]

        def copy(k, block, to, src=None):
            return pltpu.make_async_remote_copy(
                src_ref=slot(*block) if src is None else src, dst_ref=slot(*block),
                send_sem=send_sems.at[k], recv_sem=recv_sems.at[k], device_id=to, device_id_type=MESH)

        all_ref[4 * x + 2 * y + c] = x_ref[...]
        first = [copy(0, me, sibling, src=x_ref)]
        first += [copy(1 + j, me, (*chip, c), src=x_ref) for j, chip in enumerate(chips)]
        for cp in first:
            cp.start()
        passed = [copy(4 + j, (*chip, c), sibling) for j, chip in enumerate(chips)]
        for j, chip in enumerate(chips):
            copy(1 + j, (*chip, c), me).wait_recv()
            passed[j].start()
        copy(0, sibling, me).wait_recv()
        for j, chip in enumerate(chips):
            copy(4 + j, (*chip, 1 - c), me).wait_recv()
        for cp in first + passed:
            cp.wait_send()
        acc = all_ref[0]
        for k in range(1, 8):
            acc = acc + all_ref[k]
        o_ref[...] = acc

    return pl.pallas_call(
        body, in_specs=[pl.BlockSpec(memory_space=pltpu.VMEM)], out_specs=pl.BlockSpec(memory_space=pltpu.VMEM),
        out_shape=jax.ShapeDtypeStruct((r, cols), F32),
        scratch_shapes=[pltpu.VMEM((8, r, cols), F32), pltpu.SemaphoreType.DMA((7,)), pltpu.SemaphoreType.DMA((7,))],
        compiler_params=pltpu.CompilerParams(has_side_effects=True, vmem_limit_bytes=VMEM_LIMIT), name=name,
    )(pack)


def _ffn_fwd(x_norm, w_in, w_out, tag, between=None):
    ab = _mm(x_norm, w_in, out_dtype=BF16, tn=1408, name=tag + "_in")
    u = _swiglu_fwd(ab, name=tag + "_swiglu")
    if between is not None:
        between(u)
    f = _mm(u, w_out, tk=DFF, name=tag + "_out")
    return ab, u, f


def _ffn_bwd(dz, x_norm, ab, u, w_in, w_out, tag):
    du = _mm(dz, w_out, tb=True, out_dtype=BF16, tn=1408, name=tag + "_out_dx")
    dw_out = _mm(u, dz, ta=True, out_dtype=BF16, tm=1408, tk=2048, name=tag + "_out_dw")
    dab = _swiglu_bwd(ab, du, name=tag + "_swiglu_bwd")
    dxn = _mm(dab, w_in, tb=True, tk=2816, name=tag + "_in_dx")
    dw_in = _mm(x_norm, dab, ta=True, out_dtype=BF16, tm=512, tk=4096, shards=4, name=tag + "_in_dw")
    return dxn, dw_in, dw_out


def _local_step(x, mem, target, small, gather):
    big, behind = gather("ffn1", ())
    h1 = _norm_fwd(x, small["ffn1_pre_g"], BF16, name="ffn1_pre", after=behind)
    ab1, u1, f1 = _ffn_fwd(h1, big["ffn1_w_in"], big["ffn1_w_out"], "ffn1", between=lambda u: gather("mix_pass", (u,)))
    more, behind = gather("mix", (f1,))
    big.update(more)
    small = dict(small, w_fu_pad=big["w_fu_pad"])
    x1, h = _resid_norm_fwd(x, f1, small["ffn1_post_g"], 0.5, small["mix_pre_g"], name="ffn1_post")
    pg = _mm(h, big["w_gla"], out_dtype=BF16, tn=PG_W, after=behind, name="mix_in_gla")
    ppx = _mm(h, big["w_px"], out_dtype=BF16, name="mix_in_px")
    pgt = _mm(h, big["w_gates"], out_dtype=BF16, tn=1536, name="mix_in_gates")
    mem_n = _norm_fwd(mem, small["mem_norm_g"], BF16, name="mem_norm")
    kv = _mm(mem_n, big["w_mem_kv"], out_dtype=BF16, name="mem_kv")
    ya_in, sp = _gla_fwd(pg, small["w_fu_pad"], small["b_f"], small["gla_norm_g"], name="gla_fwd")
    yb_in = _pool_fwd(ppx, small["w_pool_b"], small["pool_scale"], name="pool_fwd")
    xc = _xattn_fwd(ppx, kv, name="xattn_fwd")
    ya = _mm(ya_in, big["w_up_gla"], out_dtype=BF16, name="up_gla")
    yb = _mm(yb_in, big["w_up_pool"], out_dtype=BF16, name="up_pool")
    yc = _mm(xc, big["w_up_xattn"], out_dtype=BF16, name="up_xattn")
    merged = _merge_fwd(pgt, ya, yb, yc, name="merge_fwd")
    gather("ffn2_pass", (merged,))
    ymix = _mm(merged, big["w_o"], name="mix_out")
    more, _ = gather("ffn2", (ymix,))
    big.update(more)
    x2, h2 = _resid_norm_fwd(x1, ymix, small["mix_post_g"], 1.0, small["ffn2_pre_g"], name="mix_post")
    ab2, u2, f2 = _ffn_fwd(h2, big["ffn2_w_in"], big["ffn2_w_out"], "ffn2")
    x3, _ = _resid_norm_fwd(x2, f2, small["ffn2_post_g"], 0.5, None, name="ffn2_post")
    gs, gb = {}, {}
    dx3, gs["final_g"], loss = _loss_bwd(x3, small["final_g"], target, name="loss")
    dz2, gs["ffn2_post_g"] = _rms_bwd(f2, small["ffn2_post_g"], [dx3], None, 0.5, BF16, name="ffn2_post_bwd")
    dh2, gb["ffn2_w_in"], gb["ffn2_w_out"] = _ffn_bwd(dz2, h2, ab2, u2, big["ffn2_w_in"], big["ffn2_w_out"], "ffn2")
    dx2, gs["ffn2_pre_g"] = _rms_bwd(x2, small["ffn2_pre_g"], [dh2], dx3, 1.0, F32, name="ffn2_pre_bwd")
    dy, gs["mix_post_g"] = _rms_bwd(ymix, small["mix_post_g"], [dx2], None, 1.0, BF16, name="mix_post_bwd")
    dmerged = _mm(dy, big["w_o"], tb=True, out_dtype=BF16, name="mix_out_dx")
    gb["w_o"] = _mm(merged, dy, ta=True, out_dtype=BF16, tm=512, tk=4096, name="mix_out_dw")
    dya, dyb, dyc, dgt = _merge_bwd(dmerged, pgt, ya, yb, yc, name="merge_bwd")
    dya_in = _mm(dya, big["w_up_gla"], tb=True, out_dtype=BF16, name="up_gla_dx")
    gb["w_up_gla"] = _mm(ya_in, dya, ta=True, out_dtype=BF16, tm=512, tk=4096, name="up_gla_dw")
    dyb_in = _mm(dyb, big["w_up_pool"], tb=True, out_dtype=BF16, name="up_pool_dx")
    gb["w_up_pool"] = _mm(yb_in, dyb, ta=True, out_dtype=BF16, tm=512, tk=4096, shards=4, name="up_pool_dw")
    dxc = _mm(dyc, big["w_up_xattn"], tb=True, out_dtype=BF16, name="up_xattn_dx")
    gb["w_up_xattn"] = _mm(xc, dyc, ta=True, out_dtype=BF16, tm=512, tk=4096, shards=4, name="up_xattn_dw")
    dpg, gs["w_fu_pad"], gs["b_f"], gs["gla_norm_g"] = _gla_bwd(pg, sp, dya_in, small["w_fu_pad"], small["b_f"], small["gla_norm_g"], name="gla_bwd")
    dp, gs["w_pool"], gs["pool_scale"] = _pool_bwd(dyb_in, ppx, small["w_pool_b"], small["pool_scale"], name="pool_bwd")
    dxq, dkv = _xattn_bwd(dxc, ppx, kv, name="xattn_bwd")
    dkv = dkv.astype(BF16)
    gb["w_mem_kv"] = _mm(mem_n, dkv, ta=True, out_dtype=BF16, name="mem_kv_dw")
    dmem_n = _mm(dkv, big["w_mem_kv"], tb=True, name="mem_kv_dx")
    _, gs["mem_norm_g"] = _rms_bwd(mem, small["mem_norm_g"], [dmem_n], None, 1.0, BF16, name="mem_norm_bwd")
    dh_parts = [
        _mm(dpg, big["w_gla"], tb=True, tk=PG_W, name="mix_in_gla_dx"),
        _mm(dp, big["w_p"], tb=True, name="mix_in_p_dx"),
        _mm(dxq, big["w_xq"], tb=True, name="mix_in_xq_dx"),
        _mm(dgt, big["w_gates"], tb=True, tk=3072, name="mix_in_gates_dx"),
    ]
    gb["w_gla"] = _mm(h, dpg, ta=True, out_dtype=BF16, tm=512, tk=4096, tn=640, name="mix_in_gla_dw")
    gb["w_p"] = _mm(h, dp, ta=True, out_dtype=BF16, tm=512, tk=4096, name="mix_in_p_dw")
    gb["w_xq"] = _mm(h, dxq, ta=True, out_dtype=BF16, tm=512, tk=4096, name="mix_in_xq_dw")
    gb["w_gates"] = _mm(h, dgt, ta=True, out_dtype=BF16, tm=512, tk=4096, tn=1024, name="mix_in_gates_dw")
    dx1, gs["mix_pre_g"] = _rms_bwd(x1, small["mix_pre_g"], dh_parts, dx2, 1.0, F32, name="mix_pre_bwd")
    dz1, gs["ffn1_post_g"] = _rms_bwd(f1, small["ffn1_post_g"], [dx1], None, 0.5, BF16, name="ffn1_post_bwd")
    dh1, gb["ffn1_w_in"], gb["ffn1_w_out"] = _ffn_bwd(dz1, h1, ab1, u1, big["ffn1_w_in"], big["ffn1_w_out"], "ffn1")
    dx0, gs["ffn1_pre_g"] = _rms_bwd(x, small["ffn1_pre_g"], [dh1], dx1, 1.0, F32, name="ffn1_pre_bwd")
    return loss, dx0, gs, gb


BIG = ("ffn1_w_in", "ffn1_w_out", "w_in", "w_mem_kv", "w_up_gla", "w_up_pool", "w_up_xattn", "w_o", "ffn2_w_in", "ffn2_w_out")
COL_SHARDED = ("ffn1_w_in", "w_in", "w_up_pool", "w_up_xattn", "ffn2_w_in")
GATHER_GROUPS = {"ffn1": ("ffn1_w_in", "ffn1_w_out"),
                 "mix": ("w_in", "w_mem_kv", "w_up_gla", "w_up_pool", "w_up_xattn", "w_o", "w_fu"),
                 "ffn2": ("ffn2_w_in", "ffn2_w_out")}
GAINS = ("ffn1_pre_g", "ffn1_post_g", "mix_pre_g", "gla_norm_g", "mem_norm_g", "mix_post_g", "ffn2_pre_g", "ffn2_post_g", "final_g")
WEIGHTS = ("ffn1_pre_g", "ffn1_w_in", "ffn1_w_out", "ffn1_post_g", "mix_pre_g", "w_in", "w_fu", "b_f", "gla_norm_g", "w_pool",
           "pool_scale", "mem_norm_g", "w_mem_kv", "w_up_gla", "w_up_pool", "w_up_xattn", "w_o", "mix_post_g", "ffn2_pre_g",
           "ffn2_w_in", "ffn2_w_out", "ffn2_post_g", "final_g")
IN_GLA, IN_F, IN_PX, IN_GATES, IN_END = 0, 3072, 3088, 4112, 7184
PACK_ROWS = 96


def _cols_from_shards(g):
    return jnp.transpose(g, (1, 0, 2)).reshape(g.shape[1], 4 * g.shape[2])


def _pack_small(t):
    single = jnp.zeros((PACK_ROWS - 72, D), F32)
    for i, n in enumerate(GAINS):
        single = single.at[i].set(t[n].reshape(D))
    k = len(GAINS)
    single = single.at[k, 0:512].set(t["b_f"].reshape(512))
    single = single.at[k, 512:1024].set(t["pool_scale"].reshape(512))
    return jnp.concatenate([t["w_pool"].reshape(64, D), t["w_fu"].reshape(8, D), single], axis=0)


def _unpack_small(p):
    out = {n: p[72 + i:73 + i] for i, n in enumerate(GAINS)}
    k = 72 + len(GAINS)
    out["b_f"] = p[k:k + 1, 0:512]
    out["pool_scale"] = p[k:k + 1, 512:1024]
    out["w_pool"] = p[0:64].reshape(4, LANE, LANE)
    out["w_fu"] = p[64:72].reshape(GATE_RANK, 512)
    return out


def kernel(x, mem, ffn1_pre_g, ffn1_w_in, ffn1_w_out, ffn1_post_g, mix_pre_g, w_in, w_fu, b_f, gla_norm_g, w_pool, pool_scale, mem_norm_g, w_mem_kv, w_up_gla, w_up_pool, w_up_xattn, w_o, mix_post_g, ffn2_pre_g, ffn2_w_in, ffn2_w_out, ffn2_post_g, final_g, loss_target, m_ffn1_pre_g, m_ffn1_w_in, m_ffn1_w_out, m_ffn1_post_g, m_mix_pre_g, m_w_in, m_w_fu, m_b_f, m_gla_norm_g, m_w_pool, m_pool_scale, m_mem_norm_g, m_w_mem_kv, m_w_up_gla, m_w_up_pool, m_w_up_xattn, m_w_o, m_mix_post_g, m_ffn2_pre_g, m_ffn2_w_in, m_ffn2_w_out, m_ffn2_post_g, m_final_g, v_ffn1_pre_g, v_ffn1_w_in, v_ffn1_w_out, v_ffn1_post_g, v_mix_pre_g, v_w_in, v_w_fu, v_b_f, v_gla_norm_g, v_w_pool, v_pool_scale, v_mem_norm_g, v_w_mem_kv, v_w_up_gla, v_w_up_pool, v_w_up_xattn, v_w_o, v_mix_post_g, v_ffn2_pre_g, v_ffn2_w_in, v_ffn2_w_out, v_ffn2_post_g, v_final_g):
    args = dict(locals())
    w = {n: args[n][0] for n in WEIGHTS}
    m = {n: args["m_" + n][0] for n in WEIGHTS}
    v = {n: args["v_" + n][0] for n in WEIGHTS}
    xi, yi, ci = lax.axis_index("x"), lax.axis_index("y"), lax.axis_index("c")
    chip = 2 * xi + yi

    c_arr = jnp.reshape(ci, (1,)).astype(jnp.int32)
    chip_arr = jnp.reshape(chip, (1,)).astype(jnp.int32)
    place_arr = jnp.stack([chip, ci]).astype(jnp.int32)
    placed = {n: _place_shard(args[n], chip_arr, BF16, name="place_" + n) for n in BIG}
    placed["w_fu"] = _place_shard(args["w_fu"], chip_arr, F32, name="place_w_fu")
    inflight = {}

    def relayout(names, gathered):
        out = {}
        for n, g in zip(names, gathered):
            if n == "w_fu":
                w_fu_full = _cols_from_shards(g)
                out["w_fu_pad"] = jnp.concatenate([w_fu_full, jnp.zeros((LANE - GATE_RANK, 512), F32)], axis=0).astype(BF16)
            elif n == "w_in":
                wi = _cols_from_shards(g)
                out["w_gla"] = jnp.concatenate([wi[:, IN_GLA:IN_PX], jnp.zeros((D, PG_W - IN_PX), BF16)], axis=1)
                out["w_px"] = wi[:, IN_PX:IN_GATES]
                out["w_p"] = wi[:, IN_PX:IN_PX + 512]
                out["w_xq"] = wi[:, IN_PX + 512:IN_GATES]
                out["w_gates"] = wi[:, IN_GATES:IN_END]
            else:
                out[n] = _cols_from_shards(g) if n in COL_SHARDED else g.reshape(4 * g.shape[1], g.shape[2])
        return out

    def start(group, after):
        inflight[group] = _gather_start([placed[n] for n in GATHER_GROUPS[group]], after, name="gather_" + group + "_start")

    def gather(step, after):
        group = step.split("_")[0]
        if step.endswith("_pass") or step == "ffn1":
            if step == "ffn1":
                start(group, ())
            send, recv, bufs, _ = inflight[group]
            inflight[group] = _gather_pass(bufs, send, recv, after, name="gather_" + group + "_pass")
            if step != "ffn1":
                return None
        send, recv, bufs = inflight.pop(group)
        bufs = _gather_finish(bufs, send, recv, after, name="gather_" + group + "_finish")
        following = {"ffn1": "mix", "mix": "ffn2"}.get(group)
        behind = ()
        if following is not None:
            start(following, (bufs[0],))
            behind = (inflight[following][3],)
        return relayout(GATHER_GROUPS[group], bufs), behind

    small = {n: w[n].reshape(1, D) for n in GAINS}
    small["b_f"] = w["b_f"].reshape(1, 512)
    small["pool_scale"] = w["pool_scale"].reshape(1, 512)
    small["w_pool_b"] = w["w_pool"].astype(BF16)

    loss, grad_x, gs, gb = _local_step(x[0], mem[0], loss_target[0], small, gather)
    loss = lax.psum(loss[0, 0], ("x", "y", "c"))

    gs["w_fu"] = gs.pop("w_fu_pad")[0:GATE_RANK]
    small_sum = _unpack_small(_all_sum(_pack_small(gs), name="sum_small_grads"))
    dwi = jnp.concatenate([gb.pop("w_gla")[:, 0:IN_PX], gb.pop("w_p"), gb.pop("w_xq"), gb.pop("w_gates")], axis=1)
    gb["w_in"] = jnp.transpose(dwi.reshape(D, 4, IN_END // 4), (1, 0, 2))
    for n in BIG:
        if n not in COL_SHARDED:
            gb[n] = gb[n].reshape(4, gb[n].shape[0] // 4, gb[n].shape[1])
    contrib = [gb[n] for n in BIG]
    from_sibling = _pair_exchange(contrib, name="grads_pair_exchange")
    pair = [_pair_sum(g, got, c_arr, name="grads_pair_sum_" + n) for n, g, got in zip(BIG, contrib, from_sibling)]
    from_chips = _chip_exchange(pair, name="grads_chip_exchange")
    halves = [_chip_sum(p, got, place_arr, name="grads_chip_sum_" + n) for n, p, got in zip(BIG, pair, from_chips)]
    reduced = dict(zip(BIG, _pair_join(halves, name="grads_pair_join")))

    grads, delta, new_m, new_v = {}, {}, {}, {}
    for n in BIG:
        grads[n] = reduced[n][None]
        delta[n], new_m[n], new_v[n] = _adamw(args[n], reduced[n], args["m_" + n], args["v_" + n], name="adamw_" + n)
    small_names = GAINS + ("b_f", "pool_scale", "w_pool")
    w_fu_grad = lax.dynamic_slice_in_dim(small_sum["w_fu"], chip * LANE, LANE, axis=1)
    packs = []
    for t in (w, m, v):
        t = dict(t)
        t["w_fu"] = jnp.zeros((GATE_RANK, 512), F32)
        packs.append(_pack_small(t))
    sd, sm, sv = (_unpack_small(p) for p in _adamw(packs[0], _pack_small(small_sum), packs[1], packs[2], name="adamw_small"))
    for n in small_names:
        shape = args[n].shape
        grads[n] = small_sum[n].reshape(shape)
        delta[n], new_m[n], new_v[n] = sd[n].reshape(shape), sm[n].reshape(shape), sv[n].reshape(shape)
    grads["w_fu"] = w_fu_grad[None]
    delta["w_fu"], new_m["w_fu"], new_v["w_fu"] = _adamw(args["w_fu"], w_fu_grad, args["m_w_fu"], args["v_w_fu"], name="adamw_w_fu")

    outs = [loss, grad_x[None]]
    for group in (grads, delta, new_m, new_v):
        outs += [group[n] for n in WEIGHTS]
    return tuple(outs)
```

```python
import functools

import jax
import jax.numpy as jnp
from jax import lax
from jax.experimental import pallas as pl
from jax.experimental.pallas import tpu as pltpu

F32 = jnp.float32
BF16 = jnp.bfloat16
MESH = pl.DeviceIdType.MESH
HIGHEST = lax.Precision.HIGHEST

D = 1024
DFF = 2816
CHUNK = 64
HEADS = 4
HDK = 128
HDV = 256
GATE_TEMP = 16.0
POOL_WINDOWS = (2, 4, 8, 16)
POOL_HALO = 16
XA_HEADS = 4
XA_HD = 128
EPS = 1e-6
Q_SCALE = HDK ** -0.5
XA_SCALE = XA_HD ** -0.5
PG_Q, PG_K, PG_V, PG_G, PG_F, PG_W = 0, 512, 1024, 2048, 3072, 3200
GATE_RANK = 16
ADAM_LR, ADAM_B1, ADAM_B2, ADAM_EPS, ADAM_WD, ADAM_STEP = 0.001, 0.9, 0.999, 1e-08, 0.01, 10

VMEM_LIMIT = 48 * 1024 * 1024
LANE = 128
TS_ROW = 256
TS_GLA = 512
TS_POOL = 512
TS_XA = 512


def _params(sem):
    return pltpu.CompilerParams(dimension_semantics=sem, vmem_limit_bytes=VMEM_LIMIT)


def _tile(n, cap, unit=LANE):
    if n <= cap:
        return n
    best = None
    for t in range(unit, cap + 1, unit):
        if n % t == 0:
            best = t
    assert best is not None, (n, cap)
    return best


def _sigmoid(x):
    return 1.0 / (1.0 + jnp.exp(-x))


def _log_sigmoid(x):
    return jnp.minimum(x, 0.0) - jnp.log(1.0 + jnp.exp(-jnp.abs(x)))


def _rms(x):
    r = lax.rsqrt(jnp.mean(x * x, axis=-1, keepdims=True) + EPS)
    return x * r, r


def _rows(ts, w):
    return pl.BlockSpec((ts, w), lambda i: (i, 0))


def _fixed(shape):
    nd = len(shape)
    return pl.BlockSpec(shape, lambda i: (0,) * nd)


def _mm(a, b, *, ta=False, tb=False, out_dtype=F32, tm=1024, tn=1024, tk=1024, shards=1, after=(), name):
    m, kdim = (a.shape[1], a.shape[0]) if ta else a.shape
    n = b.shape[0] if tb else b.shape[1]
    assert (b.shape[1] if tb else b.shape[0]) == kdim, (a.shape, b.shape, ta, tb)
    tm = _tile(m, tm)
    tn = n // shards if shards > 1 else _tile(n, tn)
    tk = _tile(kdim, tk)
    nk = kdim // tk
    dims = (((0 if ta else 1,), (1 if tb else 0,)), ((), ()))

    def body(a_ref, b_ref, *rest):
        o_ref, *acc = rest[len(after):]
        part = lax.dot_general(a_ref[...], b_ref[...], dims, preferred_element_type=F32)
        if nk == 1:
            o_ref[...] = part.astype(o_ref.dtype)
            return
        acc_ref, = acc
        k = pl.program_id(2)

        @pl.when(k == 0)
        def _():
            acc_ref[...] = part

        @pl.when(k > 0)
        def _():
            acc_ref[...] += part

        @pl.when(k == nk - 1)
        def _():
            o_ref[...] = acc_ref[...].astype(o_ref.dtype)

    a_spec = pl.BlockSpec((tk, tm), lambda i, j, k: (k, i)) if ta else pl.BlockSpec((tm, tk), lambda i, j, k: (i, k))
    b_spec = pl.BlockSpec((tn, tk), lambda i, j, k: (j, k)) if tb else pl.BlockSpec((tk, tn), lambda i, j, k: (k, j))
    if shards > 1:
        out_shape = jax.ShapeDtypeStruct((shards, m, tn), out_dtype)
        o_spec = pl.BlockSpec((None, tm, tn), lambda i, j, k: (j, i, 0))
    else:
        out_shape = jax.ShapeDtypeStruct((m, n), out_dtype)
        o_spec = pl.BlockSpec((tm, tn), lambda i, j, k: (i, j))
    return pl.pallas_call(
        body, grid=(m // tm, n // tn, nk), in_specs=[a_spec, b_spec] + [ANY] * len(after), out_specs=o_spec, out_shape=out_shape,
        scratch_shapes=[pltpu.VMEM((tm, tn), F32)] if nk > 1 else [],
        compiler_params=_params(("parallel", "parallel", "arbitrary")), name=name,
    )(a, b, *after)


def _norm_fwd(x, g, out_dtype, name, after=()):
    s, d = x.shape
    ts = _tile(s, TS_ROW, 8)

    def body(x_ref, g_ref, *rest):
        o_ref = rest[len(after)]
        xh, _ = _rms(x_ref[...])
        o_ref[...] = (xh * g_ref[...]).astype(o_ref.dtype)

    return pl.pallas_call(
        body, grid=(s // ts,), in_specs=[_rows(ts, d), _fixed((1, d))] + [ANY] * len(after), out_specs=_rows(ts, d),
        out_shape=jax.ShapeDtypeStruct((s, d), out_dtype), compiler_params=_params(("parallel",)), name=name,
    )(x, g, *after)


def _resid_norm_fwd(x, f, g_post, alpha, g_next, name):
    s, d = x.shape
    ts = _tile(s, TS_ROW, 8)
    with_h = g_next is not None

    def body(x_ref, f_ref, gp_ref, *rest):
        fh, _ = _rms(f_ref[...])
        xn = x_ref[...] + alpha * (fh * gp_ref[...])
        if with_h:
            gn_ref, xo_ref, h_ref = rest
            xh, _ = _rms(xn)
            h_ref[...] = (xh * gn_ref[...]).astype(h_ref.dtype)
        else:
            xo_ref, = rest
        xo_ref[...] = xn

    ins = [x, f, g_post] + ([g_next] if with_h else [])
    in_specs = [_rows(ts, d), _rows(ts, d), _fixed((1, d))] + ([_fixed((1, d))] if with_h else [])
    out_shape = [jax.ShapeDtypeStruct((s, d), F32)] + ([jax.ShapeDtypeStruct((s, d), BF16)] if with_h else [])
    out_specs = [_rows(ts, d)] + ([_rows(ts, d)] if with_h else [])
    out = pl.pallas_call(
        body, grid=(s // ts,), in_specs=in_specs, out_specs=out_specs, out_shape=out_shape,
        compiler_params=_params(("parallel",)), name=name,
    )(*ins)
    return (out[0], out[1]) if with_h else (out[0], None)


def _rms_bwd(x, g, dys, dres, alpha, out_dtype, name):
    s, d = x.shape
    ts = _tile(s, TS_ROW, 8)
    ndy = len(dys)
    with_res = dres is not None

    def body(x_ref, g_ref, *rest):
        dy_refs = rest[:ndy]
        rest = rest[ndy:]
        if with_res:
            dres_ref, dx_ref, dg_ref = rest
        else:
            dx_ref, dg_ref = rest
        xh, r = _rms(x_ref[...])
        dy = dy_refs[0][...].astype(F32)
        for ref in dy_refs[1:]:
            dy = dy + ref[...].astype(F32)
        dy = dy * alpha

        @pl.when(pl.program_id(0) == 0)
        def _():
            dg_ref[...] = jnp.zeros_like(dg_ref)

        dg_ref[...] += jnp.sum(dy * xh, axis=0, keepdims=True)
        dyg = dy * g_ref[...]
        dx = r * (dyg - xh * jnp.mean(dyg * xh, axis=-1, keepdims=True))
        if with_res:
            dx = dx + dres_ref[...]
        dx_ref[...] = dx.astype(dx_ref.dtype)

    ins = [x, g] + list(dys) + ([dres] if with_res else [])
    in_specs = [_rows(ts, d), _fixed((1, d))] + [_rows(ts, d)] * (ndy + int(with_res))
    return pl.pallas_call(
        body, grid=(s // ts,), in_specs=in_specs, out_specs=[_rows(ts, d), _fixed((1, d))],
        out_shape=[jax.ShapeDtypeStruct((s, d), out_dtype), jax.ShapeDtypeStruct((1, d), F32)],
        compiler_params=_params(("arbitrary",)), name=name,
    )(*ins)


def _loss_bwd(x, g, target, name):
    s, d = x.shape
    ts = _tile(s, TS_ROW, 8)

    def body(x_ref, g_ref, t_ref, dx_ref, dg_ref, loss_ref):
        xh, r = _rms(x_ref[...])
        gv = g_ref[...]
        diff = xh * gv - t_ref[...]

        @pl.when(pl.program_id(0) == 0)
        def _():
            dg_ref[...] = jnp.zeros_like(dg_ref)
            loss_ref[...] = jnp.zeros_like(loss_ref)

        sq = jnp.sum(diff * diff, axis=1, keepdims=True)
        loss_ref[...] += (0.5 / d) * jnp.sum(sq, axis=0, keepdims=True)
        dy = diff * (1.0 / d)
        dg_ref[...] += jnp.sum(dy * xh, axis=0, keepdims=True)
        dyg = dy * gv
        dx_ref[...] = r * (dyg - xh * jnp.mean(dyg * xh, axis=-1, keepdims=True))

    return pl.pallas_call(
        body, grid=(s // ts,), in_specs=[_rows(ts, d), _fixed((1, d)), _rows(ts, d)],
        out_specs=[_rows(ts, d), _fixed((1, d)), _fixed((8, LANE))],
        out_shape=[jax.ShapeDtypeStruct((s, d), F32), jax.ShapeDtypeStruct((1, d), F32), jax.ShapeDtypeStruct((8, LANE), F32)],
        compiler_params=_params(("arbitrary",)), name=name,
    )(x, g, target)


def _swiglu_fwd(ab, name):
    s = ab.shape[0]
    ts = _tile(s, TS_ROW, 8)

    def body(a_ref, b_ref, u_ref):
        a = a_ref[...].astype(F32)
        u_ref[...] = (a * _sigmoid(a) * b_ref[...].astype(F32)).astype(u_ref.dtype)

    return pl.pallas_call(
        body, grid=(s // ts,),
        in_specs=[pl.BlockSpec((ts, DFF), lambda i: (i, 0)), pl.BlockSpec((ts, DFF), lambda i: (i, 1))],
        out_specs=_rows(ts, DFF), out_shape=jax.ShapeDtypeStruct((s, DFF), BF16),
        compiler_params=_params(("parallel",)), name=name,
    )(ab, ab)


def _swiglu_bwd(ab, du, name):
    s = ab.shape[0]
    ts = _tile(s, TS_ROW, 8)

    def body(a_ref, b_ref, du_ref, dab_ref):
        a = a_ref[...].astype(F32)
        b = b_ref[...].astype(F32)
        dy = du_ref[...].astype(F32)
        sig = _sigmoid(a)
        dab_ref[:, 0:DFF] = (dy * b * (sig * (1.0 + a * (1.0 - sig)))).astype(dab_ref.dtype)
        dab_ref[:, DFF:2 * DFF] = (dy * a * sig).astype(dab_ref.dtype)

    return pl.pallas_call(
        body, grid=(s // ts,),
        in_specs=[pl.BlockSpec((ts, DFF), lambda i: (i, 0)), pl.BlockSpec((ts, DFF), lambda i: (i, 1)), _rows(ts, DFF)],
        out_specs=_rows(ts, 2 * DFF), out_shape=jax.ShapeDtypeStruct((s, 2 * DFF), BF16),
        compiler_params=_params(("parallel",)), name=name,
    )(ab, ab, du)


def _tri(strict):
    r = lax.broadcasted_iota(jnp.int32, (CHUNK, CHUNK), 0)
    c = lax.broadcasted_iota(jnp.int32, (CHUNK, CHUNK), 1)
    return (r > c).astype(F32) if strict else (r >= c).astype(F32)


def _gla_fwd(pg, wfu, b_f, gnorm, name):
    s = pg.shape[0]
    ts = _tile(s, TS_GLA, CHUNK)
    cpb = ts // CHUNK
    nc = s // CHUNK

    def body(pg_ref, wfu_ref, bf_ref, gn_ref, ya_ref, sp_ref, st_ref, la_ref, dec_ref, u_ref):
        @pl.when(pl.program_id(0) == 0)
        def _():
            st_ref[...] = jnp.zeros_like(st_ref)

        f = jnp.dot(pg_ref[:, PG_F:PG_W], wfu_ref[...], preferred_element_type=F32) + bf_ref[...]
        la_ref[...] = _log_sigmoid(f) * (1.0 / GATE_TEMP)
        tri = _tri(False)
        chunks = [slice(ci * CHUNK, (ci + 1) * CHUNK) for ci in range(cpb)]
        for ci, rows in enumerate(chunks):
            la = la_ref[rows, :]
            b = jnp.dot(tri, la, precision=HIGHEST, preferred_element_type=F32)
            bend = jnp.sum(la, axis=0, keepdims=True)
            e = jnp.exp(bend - b)
            dec_ref[ci:ci + 1, :] = jnp.exp(bend)
            for hd in range(HEADS):
                k = pg_ref[rows, PG_K + hd * HDK:PG_K + (hd + 1) * HDK]
                v = pg_ref[rows, PG_V + hd * HDV:PG_V + (hd + 1) * HDV]
                kt = (k.astype(F32) * e[:, hd * HDK:(hd + 1) * HDK]).astype(BF16)
                u_ref[ci, hd] = lax.dot_general(v, kt, (((0,), (0,)), ((), ())), preferred_element_type=F32)
        for ci in range(cpb):
            for hd in range(HEADS):
                prev = st_ref[hd]
                sp_ref[ci, hd] = prev
                st = prev * dec_ref[ci:ci + 1, hd * HDK:(hd + 1) * HDK] + u_ref[ci, hd]
                st_ref[hd] = st
                u_ref[ci, hd] = st
        for ci, rows in enumerate(chunks):
            for hd in range(HEADS):
                vc = slice(hd * HDV, (hd + 1) * HDV)
                q = pg_ref[rows, PG_Q + hd * HDK:PG_Q + (hd + 1) * HDK]
                go = pg_ref[rows, PG_G + hd * HDV:PG_G + (hd + 1) * HDV].astype(F32)
                qs = (q.astype(F32) * Q_SCALE).astype(BF16)
                o = lax.dot_general(qs, u_ref[ci, hd].astype(BF16), (((1,), (1,)), ((), ())), preferred_element_type=F32)
                oh, _ = _rms(o)
                ya_ref[rows, vc] = (oh * gn_ref[:, vc] * (go * _sigmoid(go))).astype(ya_ref.dtype)

    return pl.pallas_call(
        body, grid=(s // ts,),
        in_specs=[_rows(ts, PG_W), _fixed((LANE, HEADS * HDK)), _fixed((1, HEADS * HDK)), _fixed((1, HEADS * HDV))],
        out_specs=[_rows(ts, HEADS * HDV), pl.BlockSpec((cpb, HEADS, HDV, HDK), lambda i: (i, 0, 0, 0))],
        out_shape=[jax.ShapeDtypeStruct((s, HEADS * HDV), BF16), jax.ShapeDtypeStruct((nc, HEADS, HDV, HDK), F32)],
        scratch_shapes=[pltpu.VMEM((HEADS, HDV, HDK), F32), pltpu.VMEM((ts, HEADS * HDK), F32),
                        pltpu.VMEM((max(cpb, 8), HEADS * HDK), F32), pltpu.VMEM((cpb, HEADS, HDV, HDK), F32)],
        compiler_params=_params(("arbitrary",)), name=name,
    )(pg, wfu, b_f, gnorm)


def _gla_bwd(pg, sp, dya, wfu, b_f, gnorm, name):
    s = pg.shape[0]
    ts = _tile(s, TS_GLA, CHUNK)
    cpb = ts // CHUNK
    nblk = s // ts

    def body(pg_ref, sp_ref, dya_ref, wfu_ref, bf_ref, gn_ref, dpg_ref, dwfu_ref, dbf_ref, dgn_ref,
             dst_ref, la_ref, sg_ref, df_ref, e_ref, ktf_ref, dec_ref, g_ref):
        @pl.when(pl.program_id(0) == 0)
        def _():
            dst_ref[...] = jnp.zeros_like(dst_ref)
            dwfu_ref[...] = jnp.zeros_like(dwfu_ref)
            dbf_ref[...] = jnp.zeros_like(dbf_ref)
            dgn_ref[...] = jnp.zeros_like(dgn_ref)

        flow = pg_ref[:, PG_F:PG_W]
        f = jnp.dot(flow, wfu_ref[...], preferred_element_type=F32) + bf_ref[...]
        la_ref[...] = _log_sigmoid(f) * (1.0 / GATE_TEMP)
        sg_ref[...] = _sigmoid(-f) * (1.0 / GATE_TEMP)
        tri = _tri(False)
        tri_strict = _tri(True)
        chunks = [slice(ci * CHUNK, (ci + 1) * CHUNK) for ci in range(cpb)]
        for ci, rows in enumerate(chunks):
            la = la_ref[rows, :]
            b = jnp.dot(tri, la, precision=HIGHEST, preferred_element_type=F32)
            bend = jnp.sum(la, axis=0, keepdims=True)
            e = jnp.exp(bend - b)
            e_ref[rows, :] = e
            dec = jnp.exp(bend)
            dec_ref[ci:ci + 1, :] = dec
            for hd in range(HEADS):
                kc = slice(hd * HDK, (hd + 1) * HDK)
                vc = slice(hd * HDV, (hd + 1) * HDV)
                q = pg_ref[rows, PG_Q + hd * HDK:PG_Q + (hd + 1) * HDK]
                k = pg_ref[rows, PG_K + hd * HDK:PG_K + (hd + 1) * HDK]
                v = pg_ref[rows, PG_V + hd * HDV:PG_V + (hd + 1) * HDV]
                go = pg_ref[rows, PG_G + hd * HDV:PG_G + (hd + 1) * HDV].astype(F32)
                ktf = k.astype(F32) * e[:, kc]
                ktf_ref[rows, kc] = ktf
                st = sp_ref[ci, hd] * dec[:, kc] + lax.dot_general(v, ktf.astype(BF16), (((0,), (0,)), ((), ())), preferred_element_type=F32)
                st_b = st.astype(BF16)
                qs = (q.astype(F32) * Q_SCALE).astype(BF16)
                o = lax.dot_general(qs, st_b, (((1,), (1,)), ((), ())), preferred_element_type=F32)
                oh, r = _rms(o)
                gh = gn_ref[:, vc]
                sig = _sigmoid(go)
                dy = dya_ref[rows, vc].astype(F32)
                don = dy * (go * sig)
                dgn_ref[:, vc] += jnp.sum(don * oh, axis=0, keepdims=True)
                dong = don * gh
                do = (r * (dong - oh * jnp.mean(dong * oh, axis=-1, keepdims=True))).astype(BF16)
                g_ref[ci, hd] = lax.dot_general(do, qs, (((0,), (0,)), ((), ())), preferred_element_type=F32)
                dq = jnp.dot(do, st_b, preferred_element_type=F32) * Q_SCALE
                dpg_ref[rows, PG_Q + hd * HDK:PG_Q + (hd + 1) * HDK] = dq.astype(dpg_ref.dtype)
                dgo = dy * (oh * gh) * (sig * (1.0 + go * (1.0 - sig)))
                dpg_ref[rows, PG_G + hd * HDV:PG_G + (hd + 1) * HDV] = dgo.astype(dpg_ref.dtype)
        for ci in reversed(range(cpb)):
            for hd in range(HEADS):
                dst = dst_ref[hd] + g_ref[ci, hd]
                g_ref[ci, hd] = dst
                dst_ref[hd] = dst * dec_ref[ci:ci + 1, hd * HDK:(hd + 1) * HDK]
        for ci, rows in enumerate(chunks):
            for hd in range(HEADS):
                kc = slice(hd * HDK, (hd + 1) * HDK)
                v = pg_ref[rows, PG_V + hd * HDV:PG_V + (hd + 1) * HDV]
                ktf = ktf_ref[rows, kc]
                dst = g_ref[ci, hd]
                dst_b = dst.astype(BF16)
                dkt = jnp.dot(v, dst_b, preferred_element_type=F32)
                dv = lax.dot_general(ktf.astype(BF16), dst_b, (((1,), (1,)), ((), ())), preferred_element_type=F32)
                dd = jnp.sum(dst * sp_ref[ci, hd], axis=0, keepdims=True)
                dla = jnp.dot(tri_strict, dkt * ktf, precision=HIGHEST, preferred_element_type=F32) + dd * dec_ref[ci:ci + 1, kc]
                df_ref[rows, kc] = dla * sg_ref[rows, kc]
                dpg_ref[rows, PG_K + hd * HDK:PG_K + (hd + 1) * HDK] = (dkt * e_ref[rows, kc]).astype(dpg_ref.dtype)
                dpg_ref[rows, PG_V + hd * HDV:PG_V + (hd + 1) * HDV] = dv.astype(dpg_ref.dtype)
        df = df_ref[...]
        df_b = df.astype(BF16)
        dpg_ref[:, PG_F:PG_W] = lax.dot_general(df_b, wfu_ref[...], (((1,), (1,)), ((), ())), preferred_element_type=F32).astype(dpg_ref.dtype)
        dwfu_ref[...] += lax.dot_general(flow, df_b, (((0,), (0,)), ((), ())), preferred_element_type=F32)
        dbf_ref[...] += jnp.sum(df, axis=0, keepdims=True)

    rev = lambda i: (nblk - 1 - i, 0)
    return pl.pallas_call(
        body, grid=(nblk,),
        in_specs=[pl.BlockSpec((ts, PG_W), rev), pl.BlockSpec((cpb, HEADS, HDV, HDK), lambda i: (nblk - 1 - i, 0, 0, 0)),
                  pl.BlockSpec((ts, HEADS * HDV), rev), _fixed((LANE, HEADS * HDK)), _fixed((1, HEADS * HDK)), _fixed((1, HEADS * HDV))],
        out_specs=[pl.BlockSpec((ts, PG_W), rev), _fixed((LANE, HEADS * HDK)), _fixed((1, HEADS * HDK)), _fixed((1, HEADS * HDV))],
        out_shape=[jax.ShapeDtypeStruct((s, PG_W), BF16), jax.ShapeDtypeStruct((LANE, HEADS * HDK), F32),
                   jax.ShapeDtypeStruct((1, HEADS * HDK), F32), jax.ShapeDtypeStruct((1, HEADS * HDV), F32)],
        scratch_shapes=[pltpu.VMEM((HEADS, HDV, HDK), F32)] + [pltpu.VMEM((ts, HEADS * HDK), F32)] * 5
        + [pltpu.VMEM((max(cpb, 8), HEADS * HDK), F32), pltpu.VMEM((cpb, HEADS, HDV, HDK), F32)],
        compiler_params=_params(("arbitrary",)), name=name,
    )(pg, sp, dya, wfu, b_f, gnorm)


def _window_sums(ext, sign):
    n = ext.shape[0]
    sums = {1: ext}
    w = 1
    while w < POOL_WINDOWS[-1]:
        sums[2 * w] = sums[w] + pltpu.roll(sums[w], w if sign > 0 else n - w, 0)
        w *= 2
    return [sums[POOL_WINDOWS[g]][:, g * LANE:(g + 1) * LANE] for g in range(len(POOL_WINDOWS))]


def _pool_counts(row0, n):
    pos = (row0 + lax.broadcasted_iota(jnp.int32, (n, 1), 0) + 1).astype(F32)
    return [jnp.minimum(pos, float(w)) for w in POOL_WINDOWS]


def _pool_fwd(ppx, w_pool, pool_scale, name):
    s = ppx.shape[0]
    ts = _tile(s, TS_POOL, POOL_HALO)
    hb = ts // POOL_HALO
    pw = len(POOL_WINDOWS) * LANE

    def body(p_ref, halo_ref, w_ref, sc_ref, y_ref, ext_ref):
        i = pl.program_id(0)
        p = p_ref[...].astype(F32)
        ext_ref[0:POOL_HALO, :] = jnp.where(i > 0, halo_ref[...].astype(F32), 0.0)
        ext_ref[POOL_HALO:, :] = p
        sums = _window_sums(ext_ref[...], +1)
        cnt = _pool_counts(i * ts, ts)
        for g in range(len(POOL_WINDOWS)):
            cols = slice(g * LANE, (g + 1) * LANE)
            mixed = sums[g][POOL_HALO:, :] / cnt[g] - p[:, cols]
            y = jnp.dot(mixed.astype(BF16), w_ref[g], preferred_element_type=F32)
            y_ref[:, cols] = (y * sc_ref[:, cols]).astype(y_ref.dtype)

    return pl.pallas_call(
        body, grid=(s // ts,),
        in_specs=[pl.BlockSpec((ts, pw), lambda i: (i, 0)), pl.BlockSpec((POOL_HALO, pw), lambda i: (jnp.maximum(i * hb - 1, 0), 0)),
                  _fixed((len(POOL_WINDOWS), LANE, LANE)), _fixed((1, pw))],
        out_specs=_rows(ts, pw), out_shape=jax.ShapeDtypeStruct((s, pw), BF16),
        scratch_shapes=[pltpu.VMEM((ts + POOL_HALO, pw), F32)],
        compiler_params=_params(("parallel",)), name=name,
    )(ppx, ppx, w_pool, pool_scale)


def _pool_bwd(dyb, ppx, w_pool, pool_scale, name):
    s = ppx.shape[0]
    ts = _tile(s, TS_POOL, POOL_HALO)
    hb = ts // POOL_HALO
    nblk = s // ts
    last_halo = s // POOL_HALO - 1
    ng = len(POOL_WINDOWS)
    pw = ng * LANE

    def body(p_ref, halo_ref, dy_ref, dyn_ref, w_ref, sc_ref, dp_ref, dw_ref, dsc_ref, ext_ref, dext_ref, dm_ref):
        i = pl.program_id(0)

        @pl.when(i == 0)
        def _():
            dw_ref[...] = jnp.zeros_like(dw_ref)
            dsc_ref[...] = jnp.zeros_like(dsc_ref)

        p = p_ref[...].astype(F32)
        ext_ref[0:POOL_HALO, :] = jnp.where(i > 0, halo_ref[...].astype(F32), 0.0)
        ext_ref[POOL_HALO:, :] = p
        sums = _window_sums(ext_ref[...], +1)
        cnt = _pool_counts(i * ts, ts + POOL_HALO)
        sc = sc_ref[...]
        dy = dy_ref[...].astype(F32)
        dyn = jnp.where(i < nblk - 1, dyn_ref[...].astype(F32), 0.0)
        for g in range(ng):
            cols = slice(g * LANE, (g + 1) * LANE)
            wg = w_ref[g]
            mixed = (sums[g][POOL_HALO:, :] / cnt[g][0:ts] - p[:, cols]).astype(BF16)
            ypre = jnp.dot(mixed, wg, preferred_element_type=F32)
            dsc_ref[:, cols] += jnp.sum(dy[:, cols] * ypre, axis=0, keepdims=True)
            dyp = (dy[:, cols] * sc[:, cols]).astype(BF16)
            dypn = (dyn[:, cols] * sc[:, cols]).astype(BF16)
            dw_ref[g] += lax.dot_general(mixed, dyp, (((0,), (0,)), ((), ())), preferred_element_type=F32)
            dm = lax.dot_general(dyp, wg, (((1,), (1,)), ((), ())), preferred_element_type=F32)
            dmn = lax.dot_general(dypn, wg, (((1,), (1,)), ((), ())), preferred_element_type=F32)
            dext_ref[0:ts, cols] = dm / cnt[g][0:ts]
            dext_ref[ts:, cols] = dmn / cnt[g][ts:]
            dm_ref[:, cols] = dm
        lead = _window_sums(dext_ref[...], -1)
        for g in range(ng):
            cols = slice(g * LANE, (g + 1) * LANE)
            dp_ref[:, cols] = (lead[g][0:ts, :] - dm_ref[:, cols]).astype(dp_ref.dtype)

    return pl.pallas_call(
        body, grid=(nblk,),
        in_specs=[pl.BlockSpec((ts, pw), lambda i: (i, 0)), pl.BlockSpec((POOL_HALO, pw), lambda i: (jnp.maximum(i * hb - 1, 0), 0)),
                  pl.BlockSpec((ts, pw), lambda i: (i, 0)), pl.BlockSpec((POOL_HALO, pw), lambda i: (jnp.minimum((i + 1) * hb, last_halo), 0)),
                  _fixed((ng, LANE, LANE)), _fixed((1, pw))],
        out_specs=[_rows(ts, pw), _fixed((ng, LANE, LANE)), _fixed((1, pw))],
        out_shape=[jax.ShapeDtypeStruct((s, pw), BF16), jax.ShapeDtypeStruct((ng, LANE, LANE), F32), jax.ShapeDtypeStruct((1, pw), F32)],
        scratch_shapes=[pltpu.VMEM((ts + POOL_HALO, pw), F32), pltpu.VMEM((ts + POOL_HALO, pw), F32), pltpu.VMEM((ts, pw), F32)],
        compiler_params=_params(("arbitrary",)), name=name,
    )(ppx, ppx, dyb, dyb, w_pool, pool_scale)


def _xattn_fwd(ppx, kv, name):
    s = ppx.shape[0]
    m = kv.shape[0]
    ts = _tile(s, TS_XA, 8)
    xw = XA_HEADS * XA_HD

    def body(q_ref, kv_ref, o_ref):
        for hd in range(XA_HEADS):
            cols = slice(hd * XA_HD, (hd + 1) * XA_HD)
            k = kv_ref[:, hd * XA_HD:(hd + 1) * XA_HD]
            v = kv_ref[:, xw + hd * XA_HD:xw + (hd + 1) * XA_HD]
            sc = lax.dot_general(q_ref[:, cols], k, (((1,), (1,)), ((), ())), preferred_element_type=F32) * XA_SCALE
            ex = jnp.exp(sc - jnp.max(sc, axis=-1, keepdims=True))
            pr = ex / jnp.sum(ex, axis=-1, keepdims=True)
            o_ref[:, cols] = jnp.dot(pr.astype(BF16), v, preferred_element_type=F32).astype(o_ref.dtype)

    return pl.pallas_call(
        body, grid=(s // ts,), in_specs=[pl.BlockSpec((ts, xw), lambda i: (i, 1)), _fixed((m, 2 * xw))],
        out_specs=_rows(ts, xw), out_shape=jax.ShapeDtypeStruct((s, xw), BF16),
        compiler_params=_params(("parallel",)), name=name,
    )(ppx, kv)


def _xattn_bwd(dxc, ppx, kv, name):
    s = ppx.shape[0]
    m = kv.shape[0]
    ts = _tile(s, TS_XA, 8)
    xw = XA_HEADS * XA_HD

    def body(do_ref, q_ref, kv_ref, dq_ref, dkv_ref):
        @pl.when(pl.program_id(0) == 0)
        def _():
            dkv_ref[...] = jnp.zeros_like(dkv_ref)

        for hd in range(XA_HEADS):
            cols = slice(hd * XA_HD, (hd + 1) * XA_HD)
            vcols = slice(xw + hd * XA_HD, xw + (hd + 1) * XA_HD)
            q = q_ref[:, cols]
            k = kv_ref[:, cols]
            v = kv_ref[:, vcols]
            do = do_ref[:, cols]
            sc = lax.dot_general(q, k, (((1,), (1,)), ((), ())), preferred_element_type=F32) * XA_SCALE
            ex = jnp.exp(sc - jnp.max(sc, axis=-1, keepdims=True))
            pr = ex / jnp.sum(ex, axis=-1, keepdims=True)
            dpr = lax.dot_general(do, v, (((1,), (1,)), ((), ())), preferred_element_type=F32)
            dsc = (pr * (dpr - jnp.sum(dpr * pr, axis=-1, keepdims=True)) * XA_SCALE).astype(BF16)
            dq_ref[:, cols] = jnp.dot(dsc, k, preferred_element_type=F32).astype(dq_ref.dtype)
            dkv_ref[:, cols] += lax.dot_general(dsc, q, (((0,), (0,)), ((), ())), preferred_element_type=F32)
            dkv_ref[:, vcols] += lax.dot_general(pr.astype(BF16), do, (((0,), (0,)), ((), ())), preferred_element_type=F32)

    return pl.pallas_call(
        body, grid=(s // ts,), in_specs=[_rows(ts, xw), pl.BlockSpec((ts, xw), lambda i: (i, 1)), _fixed((m, 2 * xw))],
        out_specs=[_rows(ts, xw), _fixed((m, 2 * xw))],
        out_shape=[jax.ShapeDtypeStruct((s, xw), BF16), jax.ShapeDtypeStruct((m, 2 * xw), F32)],
        compiler_params=_params(("arbitrary",)), name=name,
    )(dxc, ppx, kv)


def _merge_fwd(pgt, ya, yb, yc, name):
    s = pgt.shape[0]
    ts = _tile(s, TS_ROW, 8)

    def body(gt_ref, ya_ref, yb_ref, yc_ref, o_ref):
        acc = _sigmoid(gt_ref[:, 0:D].astype(F32)) * ya_ref[...].astype(F32)
        acc = acc + _sigmoid(gt_ref[:, D:2 * D].astype(F32)) * yb_ref[...].astype(F32)
        acc = acc + _sigmoid(gt_ref[:, 2 * D:3 * D].astype(F32)) * yc_ref[...].astype(F32)
        o_ref[...] = acc.astype(o_ref.dtype)

    return pl.pallas_call(
        body, grid=(s // ts,), in_specs=[_rows(ts, 3 * D)] + [_rows(ts, D)] * 3, out_specs=_rows(ts, D),
        out_shape=jax.ShapeDtypeStruct((s, D), BF16), compiler_params=_params(("parallel",)), name=name,
    )(pgt, ya, yb, yc)


def _merge_bwd(dmerged, pgt, ya, yb, yc, name):
    s = pgt.shape[0]
    ts = _tile(s, TS_ROW, 8)

    def body(dm_ref, gt_ref, ya_ref, yb_ref, yc_ref, dya_ref, dyb_ref, dyc_ref, dgt_ref):
        dm = dm_ref[...].astype(F32)
        for j, (y_ref, dy_ref) in enumerate(((ya_ref, dya_ref), (yb_ref, dyb_ref), (yc_ref, dyc_ref))):
            sig = _sigmoid(gt_ref[:, j * D:(j + 1) * D].astype(F32))
            dy_ref[...] = (dm * sig).astype(dy_ref.dtype)
            dgt_ref[:, j * D:(j + 1) * D] = (dm * y_ref[...].astype(F32) * sig * (1.0 - sig)).astype(dgt_ref.dtype)

    return pl.pallas_call(
        body, grid=(s // ts,), in_specs=[_rows(ts, D), _rows(ts, 3 * D)] + [_rows(ts, D)] * 3,
        out_specs=[_rows(ts, D)] * 3 + [_rows(ts, 3 * D)],
        out_shape=[jax.ShapeDtypeStruct((s, D), BF16)] * 3 + [jax.ShapeDtypeStruct((s, 3 * D), BF16)],
        compiler_params=_params(("parallel",)), name=name,
    )(dmerged, pgt, ya, yb, yc)


def _adamw(w, g, m, v, name):
    r, c = w.shape[-2:]
    tr = _tile(r, 256, 8)

    def spec(a):
        return _rows(tr, c) if a.ndim == 2 else pl.BlockSpec((None, tr, c), lambda i: (0, i, 0))

    def body(w_ref, g_ref, m_ref, v_ref, d_ref, mo_ref, vo_ref):
        gv = g_ref[...]
        mn = ADAM_B1 * m_ref[...] + (1.0 - ADAM_B1) * gv
        vn = ADAM_B2 * v_ref[...] + (1.0 - ADAM_B2) * (gv * gv)
        m_hat = mn / (1.0 - ADAM_B1 ** ADAM_STEP)
        v_hat = vn / (1.0 - ADAM_B2 ** ADAM_STEP)
        d_ref[...] = -ADAM_LR * (m_hat / (jnp.sqrt(v_hat) + ADAM_EPS) + ADAM_WD * w_ref[...])
        mo_ref[...] = mn
        vo_ref[...] = vn

    return pl.pallas_call(
        body, grid=(r // tr,), in_specs=[spec(a) for a in (w, g, m, v)], out_specs=[spec(w)] * 3,
        out_shape=[jax.ShapeDtypeStruct(w.shape, F32)] * 3, compiler_params=_params(("parallel",)), name=name,
    )(w, g, m, v)


ANY = pl.BlockSpec(memory_space=pl.ANY)


def _place():
    x, y, c = lax.axis_index("x"), lax.axis_index("y"), lax.axis_index("c")
    chips = [(1 - x, y), (x, 1 - y), (1 - x, 1 - y)]
    return x, y, c, chips


def _half(c, rows):
    h = rows // 2
    return pl.ds(pl.multiple_of(c * h, 8), h)


def _place_shard(shard, chip_arr, out_dtype, name):
    _, r, cols = shard.shape
    tr = _tile(r, 256, 16)

    def body(chip_ref, s_ref, o_ref):
        o_ref[...] = s_ref[...].astype(o_ref.dtype)

    return pl.pallas_call(
        body,
        grid_spec=pltpu.PrefetchScalarGridSpec(
            num_scalar_prefetch=1, grid=(r // tr,),
            in_specs=[pl.BlockSpec((None, tr, cols), lambda i, chip_ref: (0, i, 0))],
            out_specs=pl.BlockSpec((None, tr, cols), lambda i, chip_ref: (chip_ref[0], i, 0))),
        out_shape=jax.ShapeDtypeStruct((4, r, cols), out_dtype),
        compiler_params=_params(("parallel",)), name=name,
    )(chip_arr, shard)


def _gather_shards(bufs, name):
    n = len(bufs)

    def body(*refs):
        outs = refs[n:2 * n]
        send_ici, recv_ici, send_d2d, recv_d2d = refs[2 * n:]
        x, y, c, chips = _place()
        me = 2 * x + y
        sibling = (x, y, 1 - c)

        def ici(w, p, chip_of_block, to):
            rows = _half(c, outs[w].shape[1])
            block = outs[w].at[chip_of_block, rows]
            return pltpu.make_async_remote_copy(
                src_ref=block, dst_ref=block, send_sem=send_ici.at[w, p], recv_sem=recv_ici.at[w, p], device_id=to, device_id_type=MESH)

        def d2d(w, p, chip_of_block, half_of):
            rows = _half(half_of, outs[w].shape[1])
            block = outs[w].at[chip_of_block, rows]
            return pltpu.make_async_remote_copy(
                src_ref=block, dst_ref=block, send_sem=send_d2d.at[w, p], recv_sem=recv_d2d.at[w, p], device_id=sibling, device_id_type=MESH)

        sends = [ici(w, p, me, (*chip, c)) for p, chip in enumerate(chips) for w in range(n)]
        for cp in sends:
            cp.start()
        passed = []
        for p, (px, py) in enumerate(chips):
            for w in range(n):
                ici(w, p, 2 * px + py, (px, py, c)).wait_recv()
                fwd = d2d(w, p, 2 * px + py, c)
                fwd.start()
                passed.append(fwd)
        for p, (px, py) in enumerate(chips):
            for w in range(n):
                d2d(w, p, 2 * px + py, 1 - c).wait_recv()
        for cp in sends + passed:
            cp.wait_send()

    return pl.pallas_call(
        body, in_specs=[ANY] * n, out_specs=[ANY] * n,
        out_shape=[jax.ShapeDtypeStruct(a.shape, a.dtype) for a in bufs],
        input_output_aliases={w: w for w in range(n)},
        scratch_shapes=[pltpu.SemaphoreType.DMA((n, 3))] * 4,
        compiler_params=pltpu.CompilerParams(has_side_effects=True), name=name,
    )(*bufs)


HBM = pl.BlockSpec(memory_space=pltpu.HBM)
SEM = pl.BlockSpec(memory_space=pltpu.SEMAPHORE)
EFFECT = pltpu.SideEffectType.DATAFLOW_SIDE_EFFECTING


def _in_hbm(arrays):
    return [pltpu.with_memory_space_constraint(a, pltpu.HBM) for a in arrays]


def _gather_start(bufs, after, name):
    n, na = len(bufs), len(after)

    def body(*refs):
        send_sem, recv_sem = refs[n + na], refs[n + na + 1]
        outs = refs[n + na + 2:2 * n + na + 2]
        token = refs[2 * n + na + 2]
        x, y, c, chips = _place()
        me = 2 * x + y
        for p, chip in enumerate(chips):
            for w in range(n):
                block = outs[w].at[me, _half(c, outs[w].shape[1])]
                pltpu.make_async_remote_copy(
                    src_ref=block, dst_ref=block, send_sem=send_sem, recv_sem=recv_sem,
                    device_id=(*chip, c), device_id_type=MESH).start()
        token[...] = jnp.zeros_like(token)

    out = pl.pallas_call(
        body, name=name, in_specs=[HBM] * n + [ANY] * na,
        out_specs=[SEM, SEM] + [HBM] * n + [pl.BlockSpec(memory_space=pltpu.VMEM)],
        out_shape=[pltpu.SemaphoreType.DMA(()), pltpu.SemaphoreType.DMA(())]
        + [pltpu.HBM(a.shape, a.dtype) for a in bufs] + [jax.ShapeDtypeStruct((8, LANE), F32)],
        input_output_aliases={w: w + 2 for w in range(n)},
        compiler_params=pltpu.CompilerParams(has_side_effects=EFFECT),
    )(*_in_hbm(bufs), *after)
    return out[0], out[1], list(out[2:2 + n]), out[2 + n]


def _gather_pass(bufs, send_sem, recv_sem, after, name):
    n, na = len(bufs), len(after)

    def body(*refs):
        send1, recv1 = refs[n], refs[n + 1]
        send2, recv2 = refs[n + 2 + na], refs[n + 3 + na]
        outs = refs[n + 4 + na:2 * n + 4 + na]
        x, y, c, chips = _place()
        me = 2 * x + y
        arrivals = [(w, px, py) for px, py in chips for w in range(n)]
        for w, px, py in arrivals:
            rows = _half(c, outs[w].shape[1])
            first = pltpu.make_async_remote_copy(
                src_ref=outs[w].at[me, rows], dst_ref=outs[w].at[2 * px + py, rows], send_sem=send1, recv_sem=recv1,
                device_id=(px, py, c), device_id_type=MESH)
            first.wait_send()
            first.wait_recv()
        for w, px, py in arrivals:
            arrived = outs[w].at[2 * px + py, _half(c, outs[w].shape[1])]
            pltpu.make_async_remote_copy(
                src_ref=arrived, dst_ref=arrived, send_sem=send2, recv_sem=recv2,
                device_id=(x, y, 1 - c), device_id_type=MESH).start()

    out = pl.pallas_call(
        body, name=name, in_specs=[HBM] * n + [SEM, SEM] + [ANY] * na,
        out_specs=[SEM, SEM] + [HBM] * n,
        out_shape=[pltpu.SemaphoreType.DMA(()), pltpu.SemaphoreType.DMA(())] + [pltpu.HBM(a.shape, a.dtype) for a in bufs],
        input_output_aliases={w: w + 2 for w in range(n)},
        compiler_params=pltpu.CompilerParams(has_side_effects=EFFECT),
    )(*bufs, send_sem, recv_sem, *after)
    return out[0], out[1], list(out[2:])


def _gather_finish(bufs, send_sem, recv_sem, after, name):
    n, na = len(bufs), len(after)

    def body(*refs):
        send2, recv2 = refs[n], refs[n + 1]
        outs = refs[n + 2 + na:2 * n + 2 + na]
        x, y, c, chips = _place()
        for p, (px, py) in enumerate(chips):
            for w in range(n):
                r = outs[w].shape[1]
                passed = pltpu.make_async_remote_copy(
                    src_ref=outs[w].at[2 * px + py, _half(c, r)], dst_ref=outs[w].at[2 * px + py, _half(1 - c, r)],
                    send_sem=send2, recv_sem=recv2, device_id=(x, y, 1 - c), device_id_type=MESH)
                passed.wait_send()
                passed.wait_recv()

    out = pl.pallas_call(
        body, name=name, in_specs=[HBM] * n + [SEM, SEM] + [ANY] * na, out_specs=[HBM] * n,
        out_shape=[pltpu.HBM(a.shape, a.dtype) for a in bufs],
        input_output_aliases={w: w for w in range(n)},
        compiler_params=pltpu.CompilerParams(has_side_effects=EFFECT),
    )(*bufs, send_sem, recv_sem, *after)
    return list(out)


def _pair_exchange(grads, name):
    n = len(grads)

    def body(*refs):
        ins, outs = refs[:n], refs[n:2 * n]
        send_sem, recv_sem = refs[2 * n:]
        x, y, c, _ = _place()
        copies = []
        for w in range(n):
            rows = _half(1 - c, ins[w].shape[1])
            copies.append(pltpu.make_async_remote_copy(
                src_ref=ins[w].at[:, rows], dst_ref=outs[w], send_sem=send_sem.at[w], recv_sem=recv_sem.at[w],
                device_id=(x, y, 1 - c), device_id_type=MESH))
        for cp in copies:
            cp.start()
        for cp in copies:
            cp.wait()

    return pl.pallas_call(
        body, in_specs=[ANY] * n, out_specs=[ANY] * n,
        out_shape=[jax.ShapeDtypeStruct((4, a.shape[1] // 2, a.shape[2]), a.dtype) for a in grads],
        scratch_shapes=[pltpu.SemaphoreType.DMA((n,))] * 2,
        compiler_params=pltpu.CompilerParams(has_side_effects=True), name=name,
    )(*grads)


def _pair_sum(g, got, c_arr, name):
    _, r, cols = g.shape
    h = r // 2
    th = _tile(h, 256, 16)
    nb = h // th

    def body(c_ref, g_ref, got_ref, o_ref):
        o_ref[...] = (g_ref[...].astype(F32) + got_ref[...].astype(F32)).astype(o_ref.dtype)

    return pl.pallas_call(
        body,
        grid_spec=pltpu.PrefetchScalarGridSpec(
            num_scalar_prefetch=1, grid=(4, nb),
            in_specs=[pl.BlockSpec((None, th, cols), lambda j, i, c_ref: (j, c_ref[0] * nb + i, 0)),
                      pl.BlockSpec((None, th, cols), lambda j, i, c_ref: (j, i, 0))],
            out_specs=pl.BlockSpec((None, th, cols), lambda j, i, c_ref: (j, i, 0))),
        out_shape=jax.ShapeDtypeStruct((4, h, cols), BF16),
        compiler_params=_params(("parallel", "parallel")), name=name,
    )(c_arr, g, got)


def _chip_exchange(parts, name):
    n = len(parts)

    def body(*refs):
        ins, outs = refs[:n], refs[n:2 * n]
        send_sem, recv_sem = refs[2 * n:]
        x, y, c, chips = _place()
        copies = []
        for p, (px, py) in enumerate(chips):
            for w in range(n):
                copies.append(pltpu.make_async_remote_copy(
                    src_ref=ins[w].at[2 * px + py], dst_ref=outs[w].at[p], send_sem=send_sem.at[w, p], recv_sem=recv_sem.at[w, p],
                    device_id=(px, py, c), device_id_type=MESH))
        for cp in copies:
            cp.start()
        for cp in copies:
            cp.wait()

    return pl.pallas_call(
        body, in_specs=[ANY] * n, out_specs=[ANY] * n,
        out_shape=[jax.ShapeDtypeStruct((3,) + a.shape[1:], a.dtype) for a in parts],
        scratch_shapes=[pltpu.SemaphoreType.DMA((n, 3))] * 2,
        compiler_params=pltpu.CompilerParams(has_side_effects=True), name=name,
    )(*parts)


def _chip_exchange_start(parts, after, name):
    n, na = len(parts), len(after)
    lands = [lax.empty((3,) + a.shape[1:], a.dtype) for a in parts]

    def body(*refs):
        send_sem, recv_sem = refs[2 * n + na], refs[2 * n + na + 1]
        srcs = refs[2 * n + na + 2:3 * n + na + 2]
        dsts = refs[3 * n + na + 2:4 * n + na + 2]
        token = refs[4 * n + na + 2]
        x, y, c, chips = _place()
        for p, (px, py) in enumerate(chips):
            for w in range(n):
                pltpu.make_async_remote_copy(
                    src_ref=srcs[w].at[2 * px + py], dst_ref=dsts[w].at[p], send_sem=send_sem, recv_sem=recv_sem,
                    device_id=(px, py, c), device_id_type=MESH).start()
        token[...] = jnp.zeros_like(token)

    out = pl.pallas_call(
        body, name=name, in_specs=[HBM] * (2 * n) + [ANY] * na,
        out_specs=[SEM, SEM] + [HBM] * (2 * n) + [pl.BlockSpec(memory_space=pltpu.VMEM)],
        out_shape=[pltpu.SemaphoreType.DMA(()), pltpu.SemaphoreType.DMA(())]
        + [pltpu.HBM(a.shape, a.dtype) for a in parts + lands] + [jax.ShapeDtypeStruct((8, LANE), F32)],
        input_output_aliases={w: w + 2 for w in range(2 * n)},
        compiler_params=pltpu.CompilerParams(has_side_effects=EFFECT),
    )(*_in_hbm(parts), *_in_hbm(lands), *after)
    return out[0], out[1], list(out[2:2 + n]), list(out[2 + n:2 + 2 * n]), out[2 + 2 * n]


def _chip_exchange_finish(parts, lands, send_sem, recv_sem, after, name):
    n, na = len(parts), len(after)

    def body(*refs):
        send, recv = refs[2 * n], refs[2 * n + 1]
        srcs = refs[2 * n + 2 + na:3 * n + 2 + na]
        dsts = refs[3 * n + 2 + na:4 * n + 2 + na]
        x, y, c, chips = _place()
        for p, (px, py) in enumerate(chips):
            for w in range(n):
                copy = pltpu.make_async_remote_copy(
                    src_ref=srcs[w].at[2 * px + py], dst_ref=dsts[w].at[p], send_sem=send, recv_sem=recv,
                    device_id=(px, py, c), device_id_type=MESH)
                copy.wait_send()
                copy.wait_recv()

    out = pl.pallas_call(
        body, name=name, in_specs=[HBM] * (2 * n) + [SEM, SEM] + [ANY] * na, out_specs=[HBM] * (2 * n),
        out_shape=[pltpu.HBM(a.shape, a.dtype) for a in parts + lands],
        input_output_aliases={w: w for w in range(2 * n)},
        compiler_params=pltpu.CompilerParams(has_side_effects=EFFECT),
    )(*parts, *lands, send_sem, recv_sem, *after)
    return list(out[:n]), list(out[n:])


def _chip_sum(part, got, place_arr, name):
    _, h, cols = part.shape
    th = _tile(h, 256, 16)
    nb = h // th

    def body(place_ref, p_ref, got_ref, o_ref):
        acc = p_ref[...].astype(F32)
        for p in range(3):
            acc = acc + got_ref[p].astype(F32)
        o_ref[...] = acc

    return pl.pallas_call(
        body,
        grid_spec=pltpu.PrefetchScalarGridSpec(
            num_scalar_prefetch=1, grid=(nb,),
            in_specs=[pl.BlockSpec((None, th, cols), lambda i, place_ref: (place_ref[0], i, 0)),
                      pl.BlockSpec((3, th, cols), lambda i, place_ref: (0, i, 0))],
            out_specs=pl.BlockSpec((th, cols), lambda i, place_ref: (place_ref[1] * nb + i, 0))),
        out_shape=jax.ShapeDtypeStruct((2 * h, cols), F32),
        compiler_params=_params(("parallel",)), name=name,
    )(place_arr, part, got)


def _pair_join(bufs, name):
    n = len(bufs)

    def body(*refs):
        outs = refs[n:2 * n]
        send_sem, recv_sem = refs[2 * n:]
        x, y, c, _ = _place()
        copies = []
        for w in range(n):
            block = outs[w].at[_half(c, outs[w].shape[0])]
            copies.append(pltpu.make_async_remote_copy(
                src_ref=block, dst_ref=block, send_sem=send_sem.at[w], recv_sem=recv_sem.at[w],
                device_id=(x, y, 1 - c), device_id_type=MESH))
        for cp in copies:
            cp.start()
        for w, cp in enumerate(copies):
            cp.wait_send()
            block = outs[w].at[_half(1 - c, outs[w].shape[0])]
            pltpu.make_async_remote_copy(
                src_ref=block, dst_ref=block, send_sem=send_sem.at[w], recv_sem=recv_sem.at[w],
                device_id=(x, y, 1 - c), device_id_type=MESH).wait_recv()

    return pl.pallas_call(
        body, in_specs=[ANY] * n, out_specs=[ANY] * n,
        out_shape=[jax.ShapeDtypeStruct(a.shape, a.dtype) for a in bufs],
        input_output_aliases={w: w for w in range(n)},
        scratch_shapes=[pltpu.SemaphoreType.DMA((n,))] * 2,
        compiler_params=pltpu.CompilerParams(has_side_effects=True), name=name,
    )(*bufs)


def _all_sum(pack, name):
    r, cols = pack.shape

    def body(x_ref, o_ref, all_ref, send_sems, recv_sems):
        x, y, c, chips = _place()
        me, sibling = (x, y, c), (x, y, 1 - c)

        def slot(px, py, pc):
            return all_ref.at[4 * px + 2 * py + pc]

        def copy(k, block, to, src=None):
            return pltpu.make_async_remote_copy(
                src_ref=slot(*block) if src is None else src, dst_ref=slot(*block),
                send_sem=send_sems.at[k], recv_sem=recv_sems.at[k], device_id=to, device_id_type=MESH)

        all_ref[4 * x + 2 * y + c] = x_ref[...]
        first = [copy(0, me, sibling, src=x_ref)]
        first += [copy(1 + j, me, (*chip, c), src=x_ref) for j, chip in enumerate(chips)]
        for cp in first:
            cp.start()
        passed = [copy(4 + j, (*chip, c), sibling) for j, chip in enumerate(chips)]
        for j, chip in enumerate(chips):
            copy(1 + j, (*chip, c), me).wait_recv()
            passed[j].start()
        copy(0, sibling, me).wait_recv()
        for j, chip in enumerate(chips):
            copy(4 + j, (*chip, 1 - c), me).wait_recv()
        for cp in first + passed:
            cp.wait_send()
        acc = all_ref[0]
        for k in range(1, 8):
            acc = acc + all_ref[k]
        o_ref[...] = acc

    return pl.pallas_call(
        body, in_specs=[pl.BlockSpec(memory_space=pltpu.VMEM)], out_specs=pl.BlockSpec(memory_space=pltpu.VMEM),
        out_shape=jax.ShapeDtypeStruct((r, cols), F32),
        scratch_shapes=[pltpu.VMEM((8, r, cols), F32), pltpu.SemaphoreType.DMA((7,)), pltpu.SemaphoreType.DMA((7,))],
        compiler_params=pltpu.CompilerParams(has_side_effects=True, vmem_limit_bytes=VMEM_LIMIT), name=name,
    )(pack)


def _ffn_fwd(x_norm, w_in, w_out, tag, between=None):
    ab = _mm(x_norm, w_in, out_dtype=BF16, tn=1408, name=tag + "_in")
    u = _swiglu_fwd(ab, name=tag + "_swiglu")
    behind = (between(u) if between is not None else None) or ()
    f = _mm(u, w_out, tk=DFF, after=behind, name=tag + "_out")
    return ab, u, f


def _ffn_bwd(dz, x_norm, ab, u, w_in, w_out, tag, emit, after=()):
    dw_out = _mm(u, dz, ta=True, out_dtype=BF16, tm=1408, tk=2048, after=after, name=tag + "_out_dw")
    behind = emit(tag + "_w_out", dw_out)
    du = _mm(dz, w_out, tb=True, out_dtype=BF16, tn=1408, after=behind, name=tag + "_out_dx")
    dab = _swiglu_bwd(ab, du, name=tag + "_swiglu_bwd")
    dw_in = _mm(x_norm, dab, ta=True, out_dtype=BF16, tm=512, tk=4096, shards=4, name=tag + "_in_dw")
    behind = emit(tag + "_w_in", dw_in)
    return _mm(dab, w_in, tb=True, tk=2816, after=behind, name=tag + "_in_dx")


def _local_step(x, mem, target, small, gather, emit):
    big, behind = gather("ffn1", ())
    h1 = _norm_fwd(x, small["ffn1_pre_g"], BF16, name="ffn1_pre", after=behind)
    ab1, u1, f1 = _ffn_fwd(h1, big["ffn1_w_in"], big["ffn1_w_out"], "ffn1", between=lambda u: gather("mix_pass", (u,)))
    more, behind = gather("mix", (f1,))
    big.update(more)
    small = dict(small, w_fu_pad=big["w_fu_pad"])
    x1, h = _resid_norm_fwd(x, f1, small["ffn1_post_g"], 0.5, small["mix_pre_g"], name="ffn1_post")
    pg = _mm(h, big["w_gla"], out_dtype=BF16, tn=PG_W, after=behind, name="mix_in_gla")
    ppx = _mm(h, big["w_px"], out_dtype=BF16, name="mix_in_px")
    pgt = _mm(h, big["w_gates"], out_dtype=BF16, tn=1536, name="mix_in_gates")
    mem_n = _norm_fwd(mem, small["mem_norm_g"], BF16, name="mem_norm")
    kv = _mm(mem_n, big["w_mem_kv"], out_dtype=BF16, name="mem_kv")
    ya_in, sp = _gla_fwd(pg, small["w_fu_pad"], small["b_f"], small["gla_norm_g"], name="gla_fwd")
    yb_in = _pool_fwd(ppx, small["w_pool_b"], small["pool_scale"], name="pool_fwd")
    xc = _xattn_fwd(ppx, kv, name="xattn_fwd")
    ya = _mm(ya_in, big["w_up_gla"], out_dtype=BF16, name="up_gla")
    yb = _mm(yb_in, big["w_up_pool"], out_dtype=BF16, name="up_pool")
    yc = _mm(xc, big["w_up_xattn"], out_dtype=BF16, name="up_xattn")
    merged = _merge_fwd(pgt, ya, yb, yc, name="merge_fwd")
    gather("ffn2_pass", (merged,))
    ymix = _mm(merged, big["w_o"], name="mix_out")
    more, _ = gather("ffn2", (ymix,))
    big.update(more)
    x2, h2 = _resid_norm_fwd(x1, ymix, small["mix_post_g"], 1.0, small["ffn2_pre_g"], name="mix_post")
    ab2, u2, f2 = _ffn_fwd(h2, big["ffn2_w_in"], big["ffn2_w_out"], "ffn2")
    x3, _ = _resid_norm_fwd(x2, f2, small["ffn2_post_g"], 0.5, None, name="ffn2_post")
    gs = {}
    dx3, gs["final_g"], loss = _loss_bwd(x3, small["final_g"], target, name="loss")
    dz2, gs["ffn2_post_g"] = _rms_bwd(f2, small["ffn2_post_g"], [dx3], None, 0.5, BF16, name="ffn2_post_bwd")
    dh2 = _ffn_bwd(dz2, h2, ab2, u2, big["ffn2_w_in"], big["ffn2_w_out"], "ffn2", emit)
    dx2, gs["ffn2_pre_g"] = _rms_bwd(x2, small["ffn2_pre_g"], [dh2], dx3, 1.0, F32, name="ffn2_pre_bwd")
    dy, gs["mix_post_g"] = _rms_bwd(ymix, small["mix_post_g"], [dx2], None, 1.0, BF16, name="mix_post_bwd")
    dmerged = _mm(dy, big["w_o"], tb=True, out_dtype=BF16, name="mix_out_dx")
    emit("w_o", _mm(merged, dy, ta=True, out_dtype=BF16, tm=512, tk=4096, name="mix_out_dw"))
    dya, dyb, dyc, dgt = _merge_bwd(dmerged, pgt, ya, yb, yc, name="merge_bwd")
    dya_in = _mm(dya, big["w_up_gla"], tb=True, out_dtype=BF16, name="up_gla_dx")
    emit("w_up_gla", _mm(ya_in, dya, ta=True, out_dtype=BF16, tm=512, tk=4096, name="up_gla_dw"))
    dyb_in = _mm(dyb, big["w_up_pool"], tb=True, out_dtype=BF16, name="up_pool_dx")
    emit("w_up_pool", _mm(yb_in, dyb, ta=True, out_dtype=BF16, tm=512, tk=4096, shards=4, name="up_pool_dw"))
    dxc = _mm(dyc, big["w_up_xattn"], tb=True, out_dtype=BF16, name="up_xattn_dx")
    emit("w_up_xattn", _mm(xc, dyc, ta=True, out_dtype=BF16, tm=512, tk=4096, shards=4, name="up_xattn_dw"))
    dpg, gs["w_fu_pad"], gs["b_f"], gs["gla_norm_g"] = _gla_bwd(pg, sp, dya_in, small["w_fu_pad"], small["b_f"], small["gla_norm_g"], name="gla_bwd")
    dp, gs["w_pool"], gs["pool_scale"] = _pool_bwd(dyb_in, ppx, small["w_pool_b"], small["pool_scale"], name="pool_bwd")
    dxq, dkv = _xattn_bwd(dxc, ppx, kv, name="xattn_bwd")
    dkv = dkv.astype(BF16)
    emit("w_mem_kv", _mm(mem_n, dkv, ta=True, out_dtype=BF16, name="mem_kv_dw"))
    dmem_n = _mm(dkv, big["w_mem_kv"], tb=True, name="mem_kv_dx")
    _, gs["mem_norm_g"] = _rms_bwd(mem, small["mem_norm_g"], [dmem_n], None, 1.0, BF16, name="mem_norm_bwd")
    emit("w_gla", _mm(h, dpg, ta=True, out_dtype=BF16, tm=512, tk=4096, tn=640, name="mix_in_gla_dw"))
    emit("w_p", _mm(h, dp, ta=True, out_dtype=BF16, tm=512, tk=4096, name="mix_in_p_dw"))
    emit("w_xq", _mm(h, dxq, ta=True, out_dtype=BF16, tm=512, tk=4096, name="mix_in_xq_dw"))
    behind = emit("w_gates", _mm(h, dgt, ta=True, out_dtype=BF16, tm=512, tk=4096, tn=1024, name="mix_in_gates_dw"))
    dh_parts = [
        _mm(dpg, big["w_gla"], tb=True, tk=PG_W, after=behind, name="mix_in_gla_dx"),
        _mm(dp, big["w_p"], tb=True, name="mix_in_p_dx"),
        _mm(dxq, big["w_xq"], tb=True, name="mix_in_xq_dx"),
        _mm(dgt, big["w_gates"], tb=True, tk=3072, name="mix_in_gates_dx"),
    ]
    dx1, gs["mix_pre_g"] = _rms_bwd(x1, small["mix_pre_g"], dh_parts, dx2, 1.0, F32, name="mix_pre_bwd")
    dz1, gs["ffn1_post_g"] = _rms_bwd(f1, small["ffn1_post_g"], [dx1], None, 0.5, BF16, name="ffn1_post_bwd")
    dh1 = _ffn_bwd(dz1, h1, ab1, u1, big["ffn1_w_in"], big["ffn1_w_out"], "ffn1", emit)
    dx0, gs["ffn1_pre_g"] = _rms_bwd(x, small["ffn1_pre_g"], [dh1], dx1, 1.0, F32, name="ffn1_pre_bwd")
    return loss, dx0, gs


BIG = ("ffn1_w_in", "ffn1_w_out", "w_in", "w_mem_kv", "w_up_gla", "w_up_pool", "w_up_xattn", "w_o", "ffn2_w_in", "ffn2_w_out")
COL_SHARDED = ("ffn1_w_in", "w_in", "w_up_pool", "w_up_xattn", "ffn2_w_in")
GATHER_GROUPS = {"ffn1": ("ffn1_w_in", "ffn1_w_out"),
                 "mix": ("w_in", "w_mem_kv", "w_up_gla", "w_up_pool", "w_up_xattn", "w_o", "w_fu"),
                 "ffn2": ("ffn2_w_in", "ffn2_w_out")}
REDUCE_GROUPS = {"ffn2": ("ffn2_w_out", "ffn2_w_in"),
                 "mix": ("w_o", "w_up_gla", "w_up_pool", "w_up_xattn", "w_mem_kv", "w_gla", "w_p", "w_xq", "w_gates"),
                 "ffn1_out": ("ffn1_w_out",),
                 "ffn1_in": ("ffn1_w_in",)}
GAINS = ("ffn1_pre_g", "ffn1_post_g", "mix_pre_g", "gla_norm_g", "mem_norm_g", "mix_post_g", "ffn2_pre_g", "ffn2_post_g", "final_g")
WEIGHTS = ("ffn1_pre_g", "ffn1_w_in", "ffn1_w_out", "ffn1_post_g", "mix_pre_g", "w_in", "w_fu", "b_f", "gla_norm_g", "w_pool",
           "pool_scale", "mem_norm_g", "w_mem_kv", "w_up_gla", "w_up_pool", "w_up_xattn", "w_o", "mix_post_g", "ffn2_pre_g",
           "ffn2_w_in", "ffn2_w_out", "ffn2_post_g", "final_g")
IN_GLA, IN_F, IN_PX, IN_GATES, IN_END = 0, 3072, 3088, 4112, 7184
PACK_ROWS = 96


def _cols_from_shards(g):
    return jnp.transpose(g, (1, 0, 2)).reshape(g.shape[1], 4 * g.shape[2])


def _pack_small(t):
    single = jnp.zeros((PACK_ROWS - 72, D), F32)
    for i, n in enumerate(GAINS):
        single = single.at[i].set(t[n].reshape(D))
    k = len(GAINS)
    single = single.at[k, 0:512].set(t["b_f"].reshape(512))
    single = single.at[k, 512:1024].set(t["pool_scale"].reshape(512))
    return jnp.concatenate([t["w_pool"].reshape(64, D), t["w_fu"].reshape(8, D), single], axis=0)


def _unpack_small(p):
    out = {n: p[72 + i:73 + i] for i, n in enumerate(GAINS)}
    k = 72 + len(GAINS)
    out["b_f"] = p[k:k + 1, 0:512]
    out["pool_scale"] = p[k:k + 1, 512:1024]
    out["w_pool"] = p[0:64].reshape(4, LANE, LANE)
    out["w_fu"] = p[64:72].reshape(GATE_RANK, 512)
    return out


def kernel(x, mem, ffn1_pre_g, ffn1_w_in, ffn1_w_out, ffn1_post_g, mix_pre_g, w_in, w_fu, b_f, gla_norm_g, w_pool, pool_scale, mem_norm_g, w_mem_kv, w_up_gla, w_up_pool, w_up_xattn, w_o, mix_post_g, ffn2_pre_g, ffn2_w_in, ffn2_w_out, ffn2_post_g, final_g, loss_target, m_ffn1_pre_g, m_ffn1_w_in, m_ffn1_w_out, m_ffn1_post_g, m_mix_pre_g, m_w_in, m_w_fu, m_b_f, m_gla_norm_g, m_w_pool, m_pool_scale, m_mem_norm_g, m_w_mem_kv, m_w_up_gla, m_w_up_pool, m_w_up_xattn, m_w_o, m_mix_post_g, m_ffn2_pre_g, m_ffn2_w_in, m_ffn2_w_out, m_ffn2_post_g, m_final_g, v_ffn1_pre_g, v_ffn1_w_in, v_ffn1_w_out, v_ffn1_post_g, v_mix_pre_g, v_w_in, v_w_fu, v_b_f, v_gla_norm_g, v_w_pool, v_pool_scale, v_mem_norm_g, v_w_mem_kv, v_w_up_gla, v_w_up_pool, v_w_up_xattn, v_w_o, v_mix_post_g, v_ffn2_pre_g, v_ffn2_w_in, v_ffn2_w_out, v_ffn2_post_g, v_final_g):
    args = dict(locals())
    w = {n: args[n][0] for n in WEIGHTS}
    m = {n: args["m_" + n][0] for n in WEIGHTS}
    v = {n: args["v_" + n][0] for n in WEIGHTS}
    xi, yi, ci = lax.axis_index("x"), lax.axis_index("y"), lax.axis_index("c")
    chip = 2 * xi + yi

    c_arr = jnp.reshape(ci, (1,)).astype(jnp.int32)
    chip_arr = jnp.reshape(chip, (1,)).astype(jnp.int32)
    place_arr = jnp.stack([chip, ci]).astype(jnp.int32)
    placed = {n: _place_shard(args[n], chip_arr, BF16, name="place_" + n) for n in BIG}
    placed["w_fu"] = _place_shard(args["w_fu"], chip_arr, F32, name="place_w_fu")
    inflight = {}

    def relayout(names, gathered):
        out = {}
        for n, g in zip(names, gathered):
            if n == "w_fu":
                w_fu_full = _cols_from_shards(g)
                out["w_fu_pad"] = jnp.concatenate([w_fu_full, jnp.zeros((LANE - GATE_RANK, 512), F32)], axis=0).astype(BF16)
            elif n == "w_in":
                wi = _cols_from_shards(g)
                out["w_gla"] = jnp.concatenate([wi[:, IN_GLA:IN_PX], jnp.zeros((D, PG_W - IN_PX), BF16)], axis=1)
                out["w_px"] = wi[:, IN_PX:IN_GATES]
                out["w_p"] = wi[:, IN_PX:IN_PX + 512]
                out["w_xq"] = wi[:, IN_PX + 512:IN_GATES]
                out["w_gates"] = wi[:, IN_GATES:IN_END]
            else:
                out[n] = _cols_from_shards(g) if n in COL_SHARDED else g.reshape(4 * g.shape[1], g.shape[2])
        return out

    def start(group, after):
        inflight[group] = _gather_start([placed[n] for n in GATHER_GROUPS[group]], after, name="gather_" + group + "_start")

    def gather(step, after):
        group = step.split("_")[0]
        if step.endswith("_pass") or step == "ffn1":
            if step == "ffn1":
                start(group, ())
            send, recv, bufs, _ = inflight[group]
            inflight[group] = _gather_pass(bufs, send, recv, after, name="gather_" + group + "_pass")
            if step != "ffn1":
                return (inflight[group][2][0],)
        send, recv, bufs = inflight.pop(group)
        bufs = _gather_finish(bufs, send, recv, after, name="gather_" + group + "_finish")
        following = {"ffn1": "mix", "mix": "ffn2"}.get(group)
        behind = ()
        if following is not None:
            start(following, (bufs[0],))
            behind = (inflight[following][3],)
        return relayout(GATHER_GROUPS[group], bufs), behind

    small = {n: w[n].reshape(1, D) for n in GAINS}
    small["b_f"] = w["b_f"].reshape(1, 512)
    small["pool_scale"] = w["pool_scale"].reshape(1, 512)
    small["w_pool_b"] = w["w_pool"].astype(BF16)

    pending, travelling = {}, {}

    def emit(name, grad):
        pending[name] = grad
        group = next((g for g, names in REDUCE_GROUPS.items() if name == names[-1]), None)
        if group is None:
            return ()
        gb = {n: pending.pop(n) for n in REDUCE_GROUPS[group]}
        if group == "mix":
            dwi = jnp.concatenate([gb.pop("w_gla")[:, 0:IN_PX], gb.pop("w_p"), gb.pop("w_xq"), gb.pop("w_gates")], axis=1)
            gb["w_in"] = jnp.transpose(dwi.reshape(D, 4, IN_END // 4), (1, 0, 2))
        names = list(gb)
        contrib = [gb[n] if n in COL_SHARDED else gb[n].reshape(4, gb[n].shape[0] // 4, gb[n].shape[1]) for n in names]
        from_sibling = _pair_exchange(contrib, name="grads_" + group + "_pair_exchange")
        pair = [_pair_sum(g, got, c_arr, name="grads_pair_sum_" + n) for n, g, got in zip(names, contrib, from_sibling)]
        send, recv, pair, lands, token = _chip_exchange_start(pair, (), name="grads_" + group + "_chip_start")
        travelling[group] = (names, send, recv, pair, lands)
        return (token,)

    loss, grad_x, gs = _local_step(x[0], mem[0], loss_target[0], small, gather, emit)
    loss = lax.psum(loss[0, 0], ("x", "y", "c"))

    gs["w_fu"] = gs.pop("w_fu_pad")[0:GATE_RANK]
    small_sum = _unpack_small(_all_sum(_pack_small(gs), name="sum_small_grads"))
    halves = {}
    for group, (names, send, recv, pair, lands) in travelling.items():
        pair, from_chips = _chip_exchange_finish(pair, lands, send, recv, (grad_x,), name="grads_" + group + "_chip_finish")
        for n, p, got in zip(names, pair, from_chips):
            halves[n] = _chip_sum(p, got, place_arr, name="grads_chip_sum_" + n)
    reduced = dict(zip(BIG, _pair_join([halves[n] for n in BIG], name="grads_pair_join")))

    grads, delta, new_m, new_v = {}, {}, {}, {}
    for n in BIG:
        grads[n] = reduced[n][None]
        delta[n], new_m[n], new_v[n] = _adamw(args[n], reduced[n], args["m_" + n], args["v_" + n], name="adamw_" + n)
    small_names = GAINS + ("b_f", "pool_scale", "w_pool")
    w_fu_grad = lax.dynamic_slice_in_dim(small_sum["w_fu"], chip * LANE, LANE, axis=1)
    packs = []
    for t in (w, m, v):
        t = dict(t)
        t["w_fu"] = jnp.zeros((GATE_RANK, 512), F32)
        packs.append(_pack_small(t))
    sd, sm, sv = (_unpack_small(p) for p in _adamw(packs[0], _pack_small(small_sum), packs[1], packs[2], name="adamw_small"))
    for n in small_names:
        shape = args[n].shape
        grads[n] = small_sum[n].reshape(shape)
        delta[n], new_m[n], new_v[n] = sd[n].reshape(shape), sm[n].reshape(shape), sv[n].reshape(shape)
    grads["w_fu"] = w_fu_grad[None]
    delta["w_fu"], new_m["w_fu"], new_v["w_fu"] = _adamw(args["w_fu"], w_fu_grad, args["m_w_fu"], args["v_w_fu"], name="adamw_w_fu")

    outs = [loss, grad_x[None]]
    for group in (grads, delta, new_m, new_v):
        outs += [group[n] for n in WEIGHTS]
    return tuple(outs)
```

```python
import functools

import jax
import jax.numpy as jnp
from jax import lax
from jax.experimental import pallas as pl
from jax.experimental.pallas import tpu as pltpu

F32 = jnp.float32
BF16 = jnp.bfloat16
MESH = pl.DeviceIdType.MESH
HIGHEST = lax.Precision.HIGHEST

D = 1024
DFF = 2816
CHUNK = 64
HEADS = 4
HDK = 128
HDV = 256
GATE_TEMP = 16.0
POOL_WINDOWS = (2, 4, 8, 16)
POOL_HALO = 16
XA_HEADS = 4
XA_HD = 128
EPS = 1e-6
Q_SCALE = HDK ** -0.5
XA_SCALE = XA_HD ** -0.5
PG_Q, PG_K, PG_V, PG_G, PG_F, PG_W = 0, 512, 1024, 2048, 3072, 3200
GATE_RANK = 16
ADAM_LR, ADAM_B1, ADAM_B2, ADAM_EPS, ADAM_WD, ADAM_STEP = 0.001, 0.9, 0.999, 1e-08, 0.01, 10

VMEM_LIMIT = 48 * 1024 * 1024
LANE = 128
TS_ROW = 256
TS_GLA = 512
TS_POOL = 512
TS_XA = 512


def _params(sem):
    return pltpu.CompilerParams(dimension_semantics=sem, vmem_limit_bytes=VMEM_LIMIT)


def _tile(n, cap, unit=LANE):
    if n <= cap:
        return n
    best = None
    for t in range(unit, cap + 1, unit):
        if n % t == 0:
            best = t
    assert best is not None, (n, cap)
    return best


def _sigmoid(x):
    return 1.0 / (1.0 + jnp.exp(-x))


def _log_sigmoid(x):
    return jnp.minimum(x, 0.0) - jnp.log(1.0 + jnp.exp(-jnp.abs(x)))


def _rms(x):
    r = lax.rsqrt(jnp.mean(x * x, axis=-1, keepdims=True) + EPS)
    return x * r, r


def _rows(ts, w):
    return pl.BlockSpec((ts, w), lambda i: (i, 0))


def _fixed(shape):
    nd = len(shape)
    return pl.BlockSpec(shape, lambda i: (0,) * nd)


def _mm(a, b, *, ta=False, tb=False, out_dtype=F32, tm=1024, tn=1024, tk=1024, shards=1, after=(), name):
    m, kdim = (a.shape[1], a.shape[0]) if ta else a.shape
    n = b.shape[0] if tb else b.shape[1]
    assert (b.shape[1] if tb else b.shape[0]) == kdim, (a.shape, b.shape, ta, tb)
    tm = _tile(m, tm)
    tn = n // shards if shards > 1 else _tile(n, tn)
    tk = _tile(kdim, tk)
    nk = kdim // tk
    dims = (((0 if ta else 1,), (1 if tb else 0,)), ((), ()))

    def body(a_ref, b_ref, *rest):
        o_ref, *acc = rest[len(after):]
        part = lax.dot_general(a_ref[...], b_ref[...], dims, preferred_element_type=F32)
        if nk == 1:
            o_ref[...] = part.astype(o_ref.dtype)
            return
        acc_ref, = acc
        k = pl.program_id(2)

        @pl.when(k == 0)
        def _():
            acc_ref[...] = part

        @pl.when(k > 0)
        def _():
            acc_ref[...] += part

        @pl.when(k == nk - 1)
        def _():
            o_ref[...] = acc_ref[...].astype(o_ref.dtype)

    a_spec = pl.BlockSpec((tk, tm), lambda i, j, k: (k, i)) if ta else pl.BlockSpec((tm, tk), lambda i, j, k: (i, k))
    b_spec = pl.BlockSpec((tn, tk), lambda i, j, k: (j, k)) if tb else pl.BlockSpec((tk, tn), lambda i, j, k: (k, j))
    if shards > 1:
        out_shape = jax.ShapeDtypeStruct((shards, m, tn), out_dtype)
        o_spec = pl.BlockSpec((None, tm, tn), lambda i, j, k: (j, i, 0))
    else:
        out_shape = jax.ShapeDtypeStruct((m, n), out_dtype)
        o_spec = pl.BlockSpec((tm, tn), lambda i, j, k: (i, j))
    return pl.pallas_call(
        body, grid=(m // tm, n // tn, nk), in_specs=[a_spec, b_spec] + [ANY] * len(after), out_specs=o_spec, out_shape=out_shape,
        scratch_shapes=[pltpu.VMEM((tm, tn), F32)] if nk > 1 else [],
        compiler_params=_params(("parallel", "parallel", "arbitrary")), name=name,
    )(a, b, *after)


def _norm_fwd(x, g, out_dtype, name, after=()):
    s, d = x.shape
    ts = _tile(s, TS_ROW, 8)

    def body(x_ref, g_ref, *rest):
        o_ref = rest[len(after)]
        xh, _ = _rms(x_ref[...])
        o_ref[...] = (xh * g_ref[...]).astype(o_ref.dtype)

    return pl.pallas_call(
        body, grid=(s // ts,), in_specs=[_rows(ts, d), _fixed((1, d))] + [ANY] * len(after), out_specs=_rows(ts, d),
        out_shape=jax.ShapeDtypeStruct((s, d), out_dtype), compiler_params=_params(("parallel",)), name=name,
    )(x, g, *after)


def _resid_norm_fwd(x, f, g_post, alpha, g_next, name):
    s, d = x.shape
    ts = _tile(s, TS_ROW, 8)
    with_h = g_next is not None

    def body(x_ref, f_ref, gp_ref, *rest):
        fh, _ = _rms(f_ref[...])
        xn = x_ref[...] + alpha * (fh * gp_ref[...])
        if with_h:
            gn_ref, xo_ref, h_ref = rest
            xh, _ = _rms(xn)
            h_ref[...] = (xh * gn_ref[...]).astype(h_ref.dtype)
        else:
            xo_ref, = rest
        xo_ref[...] = xn

    ins = [x, f, g_post] + ([g_next] if with_h else [])
    in_specs = [_rows(ts, d), _rows(ts, d), _fixed((1, d))] + ([_fixed((1, d))] if with_h else [])
    out_shape = [jax.ShapeDtypeStruct((s, d), F32)] + ([jax.ShapeDtypeStruct((s, d), BF16)] if with_h else [])
    out_specs = [_rows(ts, d)] + ([_rows(ts, d)] if with_h else [])
    out = pl.pallas_call(
        body, grid=(s // ts,), in_specs=in_specs, out_specs=out_specs, out_shape=out_shape,
        compiler_params=_params(("parallel",)), name=name,
    )(*ins)
    return (out[0], out[1]) if with_h else (out[0], None)


def _rms_bwd(x, g, dys, dres, alpha, out_dtype, name):
    s, d = x.shape
    ts = _tile(s, TS_ROW, 8)
    ndy = len(dys)
    with_res = dres is not None

    def body(x_ref, g_ref, *rest):
        dy_refs = rest[:ndy]
        rest = rest[ndy:]
        if with_res:
            dres_ref, dx_ref, dg_ref = rest
        else:
            dx_ref, dg_ref = rest
        xh, r = _rms(x_ref[...])
        dy = dy_refs[0][...].astype(F32)
        for ref in dy_refs[1:]:
            dy = dy + ref[...].astype(F32)
        dy = dy * alpha

        @pl.when(pl.program_id(0) == 0)
        def _():
            dg_ref[...] = jnp.zeros_like(dg_ref)

        dg_ref[...] += jnp.sum(dy * xh, axis=0, keepdims=True)
        dyg = dy * g_ref[...]
        dx = r * (dyg - xh * jnp.mean(dyg * xh, axis=-1, keepdims=True))
        if with_res:
            dx = dx + dres_ref[...]
        dx_ref[...] = dx.astype(dx_ref.dtype)

    ins = [x, g] + list(dys) + ([dres] if with_res else [])
    in_specs = [_rows(ts, d), _fixed((1, d))] + [_rows(ts, d)] * (ndy + int(with_res))
    return pl.pallas_call(
        body, grid=(s // ts,), in_specs=in_specs, out_specs=[_rows(ts, d), _fixed((1, d))],
        out_shape=[jax.ShapeDtypeStruct((s, d), out_dtype), jax.ShapeDtypeStruct((1, d), F32)],
        compiler_params=_params(("arbitrary",)), name=name,
    )(*ins)


def _loss_bwd(x, g, target, name):
    s, d = x.shape
    ts = _tile(s, TS_ROW, 8)

    def body(x_ref, g_ref, t_ref, dx_ref, dg_ref, loss_ref):
        xh, r = _rms(x_ref[...])
        gv = g_ref[...]
        diff = xh * gv - t_ref[...]

        @pl.when(pl.program_id(0) == 0)
        def _():
            dg_ref[...] = jnp.zeros_like(dg_ref)
            loss_ref[...] = jnp.zeros_like(loss_ref)

        sq = jnp.sum(diff * diff, axis=1, keepdims=True)
        loss_ref[...] += (0.5 / d) * jnp.sum(sq, axis=0, keepdims=True)
        dy = diff * (1.0 / d)
        dg_ref[...] += jnp.sum(dy * xh, axis=0, keepdims=True)
        dyg = dy * gv
        dx_ref[...] = r * (dyg - xh * jnp.mean(dyg * xh, axis=-1, keepdims=True))

    return pl.pallas_call(
        body, grid=(s // ts,), in_specs=[_rows(ts, d), _fixed((1, d)), _rows(ts, d)],
        out_specs=[_rows(ts, d), _fixed((1, d)), _fixed((8, LANE))],
        out_shape=[jax.ShapeDtypeStruct((s, d), F32), jax.ShapeDtypeStruct((1, d), F32), jax.ShapeDtypeStruct((8, LANE), F32)],
        compiler_params=_params(("arbitrary",)), name=name,
    )(x, g, target)


def _swiglu_fwd(ab, name):
    s = ab.shape[0]
    ts = _tile(s, TS_ROW, 8)

    def body(a_ref, b_ref, u_ref):
        a = a_ref[...].astype(F32)
        u_ref[...] = (a * _sigmoid(a) * b_ref[...].astype(F32)).astype(u_ref.dtype)

    return pl.pallas_call(
        body, grid=(s // ts,),
        in_specs=[pl.BlockSpec((ts, DFF), lambda i: (i, 0)), pl.BlockSpec((ts, DFF), lambda i: (i, 1))],
        out_specs=_rows(ts, DFF), out_shape=jax.ShapeDtypeStruct((s, DFF), BF16),
        compiler_params=_params(("parallel",)), name=name,
    )(ab, ab)


def _swiglu_bwd(ab, du, name):
    s = ab.shape[0]
    ts = _tile(s, TS_ROW, 8)

    def body(a_ref, b_ref, du_ref, dab_ref):
        a = a_ref[...].astype(F32)
        b = b_ref[...].astype(F32)
        dy = du_ref[...].astype(F32)
        sig = _sigmoid(a)
        dab_ref[:, 0:DFF] = (dy * b * (sig * (1.0 + a * (1.0 - sig)))).astype(dab_ref.dtype)
        dab_ref[:, DFF:2 * DFF] = (dy * a * sig).astype(dab_ref.dtype)

    return pl.pallas_call(
        body, grid=(s // ts,),
        in_specs=[pl.BlockSpec((ts, DFF), lambda i: (i, 0)), pl.BlockSpec((ts, DFF), lambda i: (i, 1)), _rows(ts, DFF)],
        out_specs=_rows(ts, 2 * DFF), out_shape=jax.ShapeDtypeStruct((s, 2 * DFF), BF16),
        compiler_params=_params(("parallel",)), name=name,
    )(ab, ab, du)


def _tri(strict):
    r = lax.broadcasted_iota(jnp.int32, (CHUNK, CHUNK), 0)
    c = lax.broadcasted_iota(jnp.int32, (CHUNK, CHUNK), 1)
    return (r > c).astype(F32) if strict else (r >= c).astype(F32)


def _gla_fwd(pg, wfu, b_f, gnorm, name):
    s = pg.shape[0]
    ts = _tile(s, TS_GLA, CHUNK)
    cpb = ts // CHUNK
    nc = s // CHUNK

    def body(pg_ref, wfu_ref, bf_ref, gn_ref, ya_ref, sp_ref, st_ref, la_ref, dec_ref, u_ref):
        @pl.when(pl.program_id(0) == 0)
        def _():
            st_ref[...] = jnp.zeros_like(st_ref)

        f = jnp.dot(pg_ref[:, PG_F:PG_W], wfu_ref[...], preferred_element_type=F32) + bf_ref[...]
        la_ref[...] = _log_sigmoid(f) * (1.0 / GATE_TEMP)
        tri = _tri(False)
        chunks = [slice(ci * CHUNK, (ci + 1) * CHUNK) for ci in range(cpb)]
        for ci, rows in enumerate(chunks):
            la = la_ref[rows, :]
            b = jnp.dot(tri, la, precision=HIGHEST, preferred_element_type=F32)
            bend = jnp.sum(la, axis=0, keepdims=True)
            e = jnp.exp(bend - b)
            dec_ref[ci:ci + 1, :] = jnp.exp(bend)
            for hd in range(HEADS):
                k = pg_ref[rows, PG_K + hd * HDK:PG_K + (hd + 1) * HDK]
                v = pg_ref[rows, PG_V + hd * HDV:PG_V + (hd + 1) * HDV]
                kt = (k.astype(F32) * e[:, hd * HDK:(hd + 1) * HDK]).astype(BF16)
                u_ref[ci, hd] = lax.dot_general(v, kt, (((0,), (0,)), ((), ())), preferred_element_type=F32)
        for ci in range(cpb):
            for hd in range(HEADS):
                prev = st_ref[hd]
                sp_ref[ci, hd] = prev
                st = prev * dec_ref[ci:ci + 1, hd * HDK:(hd + 1) * HDK] + u_ref[ci, hd]
                st_ref[hd] = st
                u_ref[ci, hd] = st
        for ci, rows in enumerate(chunks):
            for hd in range(HEADS):
                vc = slice(hd * HDV, (hd + 1) * HDV)
                q = pg_ref[rows, PG_Q + hd * HDK:PG_Q + (hd + 1) * HDK]
                go = pg_ref[rows, PG_G + hd * HDV:PG_G + (hd + 1) * HDV].astype(F32)
                qs = (q.astype(F32) * Q_SCALE).astype(BF16)
                o = lax.dot_general(qs, u_ref[ci, hd].astype(BF16), (((1,), (1,)), ((), ())), preferred_element_type=F32)
                oh, _ = _rms(o)
                ya_ref[rows, vc] = (oh * gn_ref[:, vc] * (go * _sigmoid(go))).astype(ya_ref.dtype)

    return pl.pallas_call(
        body, grid=(s // ts,),
        in_specs=[_rows(ts, PG_W), _fixed((LANE, HEADS * HDK)), _fixed((1, HEADS * HDK)), _fixed((1, HEADS * HDV))],
        out_specs=[_rows(ts, HEADS * HDV), pl.BlockSpec((cpb, HEADS, HDV, HDK), lambda i: (i, 0, 0, 0))],
        out_shape=[jax.ShapeDtypeStruct((s, HEADS * HDV), BF16), jax.ShapeDtypeStruct((nc, HEADS, HDV, HDK), F32)],
        scratch_shapes=[pltpu.VMEM((HEADS, HDV, HDK), F32), pltpu.VMEM((ts, HEADS * HDK), F32),
                        pltpu.VMEM((max(cpb, 8), HEADS * HDK), F32), pltpu.VMEM((cpb, HEADS, HDV, HDK), F32)],
        compiler_params=_params(("arbitrary",)), name=name,
    )(pg, wfu, b_f, gnorm)


def _gla_bwd(pg, sp, dya, wfu, b_f, gnorm, name):
    s = pg.shape[0]
    ts = _tile(s, TS_GLA, CHUNK)
    cpb = ts // CHUNK
    nblk = s // ts

    def body(pg_ref, sp_ref, dya_ref, wfu_ref, bf_ref, gn_ref, dpg_ref, dwfu_ref, dbf_ref, dgn_ref,
             dst_ref, la_ref, sg_ref, df_ref, e_ref, ktf_ref, dec_ref, g_ref):
        @pl.when(pl.program_id(0) == 0)
        def _():
            dst_ref[...] = jnp.zeros_like(dst_ref)
            dwfu_ref[...] = jnp.zeros_like(dwfu_ref)
            dbf_ref[...] = jnp.zeros_like(dbf_ref)
            dgn_ref[...] = jnp.zeros_like(dgn_ref)

        flow = pg_ref[:, PG_F:PG_W]
        f = jnp.dot(flow, wfu_ref[...], preferred_element_type=F32) + bf_ref[...]
        la_ref[...] = _log_sigmoid(f) * (1.0 / GATE_TEMP)
        sg_ref[...] = _sigmoid(-f) * (1.0 / GATE_TEMP)
        tri = _tri(False)
        tri_strict = _tri(True)
        chunks = [slice(ci * CHUNK, (ci + 1) * CHUNK) for ci in range(cpb)]
        for ci, rows in enumerate(chunks):
            la = la_ref[rows, :]
            b = jnp.dot(tri, la, precision=HIGHEST, preferred_element_type=F32)
            bend = jnp.sum(la, axis=0, keepdims=True)
            e = jnp.exp(bend - b)
            e_ref[rows, :] = e
            dec = jnp.exp(bend)
            dec_ref[ci:ci + 1, :] = dec
            for hd in range(HEADS):
                kc = slice(hd * HDK, (hd + 1) * HDK)
                vc = slice(hd * HDV, (hd + 1) * HDV)
                q = pg_ref[rows, PG_Q + hd * HDK:PG_Q + (hd + 1) * HDK]
                k = pg_ref[rows, PG_K + hd * HDK:PG_K + (hd + 1) * HDK]
                v = pg_ref[rows, PG_V + hd * HDV:PG_V + (hd + 1) * HDV]
                go = pg_ref[rows, PG_G + hd * HDV:PG_G + (hd + 1) * HDV].astype(F32)
                ktf = k.astype(F32) * e[:, kc]
                ktf_ref[rows, kc] = ktf
                st = sp_ref[ci, hd] * dec[:, kc] + lax.dot_general(v, ktf.astype(BF16), (((0,), (0,)), ((), ())), preferred_element_type=F32)
                st_b = st.astype(BF16)
                qs = (q.astype(F32) * Q_SCALE).astype(BF16)
                o = lax.dot_general(qs, st_b, (((1,), (1,)), ((), ())), preferred_element_type=F32)
                oh, r = _rms(o)
                gh = gn_ref[:, vc]
                sig = _sigmoid(go)
                dy = dya_ref[rows, vc].astype(F32)
                don = dy * (go * sig)
                dgn_ref[:, vc] += jnp.sum(don * oh, axis=0, keepdims=True)
                dong = don * gh
                do = (r * (dong - oh * jnp.mean(dong * oh, axis=-1, keepdims=True))).astype(BF16)
                g_ref[ci, hd] = lax.dot_general(do, qs, (((0,), (0,)), ((), ())), preferred_element_type=F32)
                dq = jnp.dot(do, st_b, preferred_element_type=F32) * Q_SCALE
                dpg_ref[rows, PG_Q + hd * HDK:PG_Q + (hd + 1) * HDK] = dq.astype(dpg_ref.dtype)
                dgo = dy * (oh * gh) * (sig * (1.0 + go * (1.0 - sig)))
                dpg_ref[rows, PG_G + hd * HDV:PG_G + (hd + 1) * HDV] = dgo.astype(dpg_ref.dtype)
        for ci in reversed(range(cpb)):
            for hd in range(HEADS):
                dst = dst_ref[hd] + g_ref[ci, hd]
                g_ref[ci, hd] = dst
                dst_ref[hd] = dst * dec_ref[ci:ci + 1, hd * HDK:(hd + 1) * HDK]
        for ci, rows in enumerate(chunks):
            for hd in range(HEADS):
                kc = slice(hd * HDK, (hd + 1) * HDK)
                v = pg_ref[rows, PG_V + hd * HDV:PG_V + (hd + 1) * HDV]
                ktf = ktf_ref[rows, kc]
                dst = g_ref[ci, hd]
                dst_b = dst.astype(BF16)
                dkt = jnp.dot(v, dst_b, preferred_element_type=F32)
                dv = lax.dot_general(ktf.astype(BF16), dst_b, (((1,), (1,)), ((), ())), preferred_element_type=F32)
                dd = jnp.sum(dst * sp_ref[ci, hd], axis=0, keepdims=True)
                dla = jnp.dot(tri_strict, dkt * ktf, precision=HIGHEST, preferred_element_type=F32) + dd * dec_ref[ci:ci + 1, kc]
                df_ref[rows, kc] = dla * sg_ref[rows, kc]
                dpg_ref[rows, PG_K + hd * HDK:PG_K + (hd + 1) * HDK] = (dkt * e_ref[rows, kc]).astype(dpg_ref.dtype)
                dpg_ref[rows, PG_V + hd * HDV:PG_V + (hd + 1) * HDV] = dv.astype(dpg_ref.dtype)
        df = df_ref[...]
        df_b = df.astype(BF16)
        dpg_ref[:, PG_F:PG_W] = lax.dot_general(df_b, wfu_ref[...], (((1,), (1,)), ((), ())), preferred_element_type=F32).astype(dpg_ref.dtype)
        dwfu_ref[...] += lax.dot_general(flow, df_b, (((0,), (0,)), ((), ())), preferred_element_type=F32)
        dbf_ref[...] += jnp.sum(df, axis=0, keepdims=True)

    rev = lambda i: (nblk - 1 - i, 0)
    return pl.pallas_call(
        body, grid=(nblk,),
        in_specs=[pl.BlockSpec((ts, PG_W), rev), pl.BlockSpec((cpb, HEADS, HDV, HDK), lambda i: (nblk - 1 - i, 0, 0, 0)),
                  pl.BlockSpec((ts, HEADS * HDV), rev), _fixed((LANE, HEADS * HDK)), _fixed((1, HEADS * HDK)), _fixed((1, HEADS * HDV))],
        out_specs=[pl.BlockSpec((ts, PG_W), rev), _fixed((LANE, HEADS * HDK)), _fixed((1, HEADS * HDK)), _fixed((1, HEADS * HDV))],
        out_shape=[jax.ShapeDtypeStruct((s, PG_W), BF16), jax.ShapeDtypeStruct((LANE, HEADS * HDK), F32),
                   jax.ShapeDtypeStruct((1, HEADS * HDK), F32), jax.ShapeDtypeStruct((1, HEADS * HDV), F32)],
        scratch_shapes=[pltpu.VMEM((HEADS, HDV, HDK), F32)] + [pltpu.VMEM((ts, HEADS * HDK), F32)] * 5
        + [pltpu.VMEM((max(cpb, 8), HEADS * HDK), F32), pltpu.VMEM((cpb, HEADS, HDV, HDK), F32)],
        compiler_params=_params(("arbitrary",)), name=name,
    )(pg, sp, dya, wfu, b_f, gnorm)


def _window_sums(ext, sign):
    n = ext.shape[0]
    sums = {1: ext}
    w = 1
    while w < POOL_WINDOWS[-1]:
        sums[2 * w] = sums[w] + pltpu.roll(sums[w], w if sign > 0 else n - w, 0)
        w *= 2
    return [sums[POOL_WINDOWS[g]][:, g * LANE:(g + 1) * LANE] for g in range(len(POOL_WINDOWS))]


def _pool_counts(row0, n):
    pos = (row0 + lax.broadcasted_iota(jnp.int32, (n, 1), 0) + 1).astype(F32)
    return [jnp.minimum(pos, float(w)) for w in POOL_WINDOWS]


def _pool_fwd(ppx, w_pool, pool_scale, name):
    s = ppx.shape[0]
    ts = _tile(s, TS_POOL, POOL_HALO)
    hb = ts // POOL_HALO
    pw = len(POOL_WINDOWS) * LANE

    def body(p_ref, halo_ref, w_ref, sc_ref, y_ref, ext_ref):
        i = pl.program_id(0)
        p = p_ref[...].astype(F32)
        ext_ref[0:POOL_HALO, :] = jnp.where(i > 0, halo_ref[...].astype(F32), 0.0)
        ext_ref[POOL_HALO:, :] = p
        sums = _window_sums(ext_ref[...], +1)
        cnt = _pool_counts(i * ts, ts)
        for g in range(len(POOL_WINDOWS)):
            cols = slice(g * LANE, (g + 1) * LANE)
            mixed = sums[g][POOL_HALO:, :] / cnt[g] - p[:, cols]
            y = jnp.dot(mixed.astype(BF16), w_ref[g], preferred_element_type=F32)
            y_ref[:, cols] = (y * sc_ref[:, cols]).astype(y_ref.dtype)

    return pl.pallas_call(
        body, grid=(s // ts,),
        in_specs=[pl.BlockSpec((ts, pw), lambda i: (i, 0)), pl.BlockSpec((POOL_HALO, pw), lambda i: (jnp.maximum(i * hb - 1, 0), 0)),
                  _fixed((len(POOL_WINDOWS), LANE, LANE)), _fixed((1, pw))],
        out_specs=_rows(ts, pw), out_shape=jax.ShapeDtypeStruct((s, pw), BF16),
        scratch_shapes=[pltpu.VMEM((ts + POOL_HALO, pw), F32)],
        compiler_params=_params(("parallel",)), name=name,
    )(ppx, ppx, w_pool, pool_scale)


def _pool_bwd(dyb, ppx, w_pool, pool_scale, name):
    s = ppx.shape[0]
    ts = _tile(s, TS_POOL, POOL_HALO)
    hb = ts // POOL_HALO
    nblk = s // ts
    last_halo = s // POOL_HALO - 1
    ng = len(POOL_WINDOWS)
    pw = ng * LANE

    def body(p_ref, halo_ref, dy_ref, dyn_ref, w_ref, sc_ref, dp_ref, dw_ref, dsc_ref, ext_ref, dext_ref, dm_ref):
        i = pl.program_id(0)

        @pl.when(i == 0)
        def _():
            dw_ref[...] = jnp.zeros_like(dw_ref)
            dsc_ref[...] = jnp.zeros_like(dsc_ref)

        p = p_ref[...].astype(F32)
        ext_ref[0:POOL_HALO, :] = jnp.where(i > 0, halo_ref[...].astype(F32), 0.0)
        ext_ref[POOL_HALO:, :] = p
        sums = _window_sums(ext_ref[...], +1)
        cnt = _pool_counts(i * ts, ts + POOL_HALO)
        sc = sc_ref[...]
        dy = dy_ref[...].astype(F32)
        dyn = jnp.where(i < nblk - 1, dyn_ref[...].astype(F32), 0.0)
        for g in range(ng):
            cols = slice(g * LANE, (g + 1) * LANE)
            wg = w_ref[g]
            mixed = (sums[g][POOL_HALO:, :] / cnt[g][0:ts] - p[:, cols]).astype(BF16)
            ypre = jnp.dot(mixed, wg, preferred_element_type=F32)
            dsc_ref[:, cols] += jnp.sum(dy[:, cols] * ypre, axis=0, keepdims=True)
            dyp = (dy[:, cols] * sc[:, cols]).astype(BF16)
            dypn = (dyn[:, cols] * sc[:, cols]).astype(BF16)
            dw_ref[g] += lax.dot_general(mixed, dyp, (((0,), (0,)), ((), ())), preferred_element_type=F32)
            dm = lax.dot_general(dyp, wg, (((1,), (1,)), ((), ())), preferred_element_type=F32)
            dmn = lax.dot_general(dypn, wg, (((1,), (1,)), ((), ())), preferred_element_type=F32)
            dext_ref[0:ts, cols] = dm / cnt[g][0:ts]
            dext_ref[ts:, cols] = dmn / cnt[g][ts:]
            dm_ref[:, cols] = dm
        lead = _window_sums(dext_ref[...], -1)
        for g in range(ng):
            cols = slice(g * LANE, (g + 1) * LANE)
            dp_ref[:, cols] = (lead[g][0:ts, :] - dm_ref[:, cols]).astype(dp_ref.dtype)

    return pl.pallas_call(
        body, grid=(nblk,),
        in_specs=[pl.BlockSpec((ts, pw), lambda i: (i, 0)), pl.BlockSpec((POOL_HALO, pw), lambda i: (jnp.maximum(i * hb - 1, 0), 0)),
                  pl.BlockSpec((ts, pw), lambda i: (i, 0)), pl.BlockSpec((POOL_HALO, pw), lambda i: (jnp.minimum((i + 1) * hb, last_halo), 0)),
                  _fixed((ng, LANE, LANE)), _fixed((1, pw))],
        out_specs=[_rows(ts, pw), _fixed((ng, LANE, LANE)), _fixed((1, pw))],
        out_shape=[jax.ShapeDtypeStruct((s, pw), BF16), jax.ShapeDtypeStruct((ng, LANE, LANE), F32), jax.ShapeDtypeStruct((1, pw), F32)],
        scratch_shapes=[pltpu.VMEM((ts + POOL_HALO, pw), F32), pltpu.VMEM((ts + POOL_HALO, pw), F32), pltpu.VMEM((ts, pw), F32)],
        compiler_params=_params(("arbitrary",)), name=name,
    )(ppx, ppx, dyb, dyb, w_pool, pool_scale)


def _xattn_fwd(ppx, kv, name):
    s = ppx.shape[0]
    m = kv.shape[0]
    ts = _tile(s, TS_XA, 8)
    xw = XA_HEADS * XA_HD

    def body(q_ref, kv_ref, o_ref):
        for hd in range(XA_HEADS):
            cols = slice(hd * XA_HD, (hd + 1) * XA_HD)
            k = kv_ref[:, hd * XA_HD:(hd + 1) * XA_HD]
            v = kv_ref[:, xw + hd * XA_HD:xw + (hd + 1) * XA_HD]
            sc = lax.dot_general(q_ref[:, cols], k, (((1,), (1,)), ((), ())), preferred_element_type=F32) * XA_SCALE
            ex = jnp.exp(sc - jnp.max(sc, axis=-1, keepdims=True))
            pr = ex / jnp.sum(ex, axis=-1, keepdims=True)
            o_ref[:, cols] = jnp.dot(pr.astype(BF16), v, preferred_element_type=F32).astype(o_ref.dtype)

    return pl.pallas_call(
        body, grid=(s // ts,), in_specs=[pl.BlockSpec((ts, xw), lambda i: (i, 1)), _fixed((m, 2 * xw))],
        out_specs=_rows(ts, xw), out_shape=jax.ShapeDtypeStruct((s, xw), BF16),
        compiler_params=_params(("parallel",)), name=name,
    )(ppx, kv)


def _xattn_bwd(dxc, ppx, kv, name):
    s = ppx.shape[0]
    m = kv.shape[0]
    ts = _tile(s, TS_XA, 8)
    xw = XA_HEADS * XA_HD

    def body(do_ref, q_ref, kv_ref, dq_ref, dkv_ref):
        @pl.when(pl.program_id(0) == 0)
        def _():
            dkv_ref[...] = jnp.zeros_like(dkv_ref)

        for hd in range(XA_HEADS):
            cols = slice(hd * XA_HD, (hd + 1) * XA_HD)
            vcols = slice(xw + hd * XA_HD, xw + (hd + 1) * XA_HD)
            q = q_ref[:, cols]
            k = kv_ref[:, cols]
            v = kv_ref[:, vcols]
            do = do_ref[:, cols]
            sc = lax.dot_general(q, k, (((1,), (1,)), ((), ())), preferred_element_type=F32) * XA_SCALE
            ex = jnp.exp(sc - jnp.max(sc, axis=-1, keepdims=True))
            pr = ex / jnp.sum(ex, axis=-1, keepdims=True)
            dpr = lax.dot_general(do, v, (((1,), (1,)), ((), ())), preferred_element_type=F32)
            dsc = (pr * (dpr - jnp.sum(dpr * pr, axis=-1, keepdims=True)) * XA_SCALE).astype(BF16)
            dq_ref[:, cols] = jnp.dot(dsc, k, preferred_element_type=F32).astype(dq_ref.dtype)
            dkv_ref[:, cols] += lax.dot_general(dsc, q, (((0,), (0,)), ((), ())), preferred_element_type=F32)
            dkv_ref[:, vcols] += lax.dot_general(pr.astype(BF16), do, (((0,), (0,)), ((), ())), preferred_element_type=F32)

    return pl.pallas_call(
        body, grid=(s // ts,), in_specs=[_rows(ts, xw), pl.BlockSpec((ts, xw), lambda i: (i, 1)), _fixed((m, 2 * xw))],
        out_specs=[_rows(ts, xw), _fixed((m, 2 * xw))],
        out_shape=[jax.ShapeDtypeStruct((s, xw), BF16), jax.ShapeDtypeStruct((m, 2 * xw), F32)],
        compiler_params=_params(("arbitrary",)), name=name,
    )(dxc, ppx, kv)


def _merge_fwd(pgt, ya, yb, yc, name):
    s = pgt.shape[0]
    ts = _tile(s, TS_ROW, 8)

    def body(gt_ref, ya_ref, yb_ref, yc_ref, o_ref):
        acc = _sigmoid(gt_ref[:, 0:D].astype(F32)) * ya_ref[...].astype(F32)
        acc = acc + _sigmoid(gt_ref[:, D:2 * D].astype(F32)) * yb_ref[...].astype(F32)
        acc = acc + _sigmoid(gt_ref[:, 2 * D:3 * D].astype(F32)) * yc_ref[...].astype(F32)
        o_ref[...] = acc.astype(o_ref.dtype)

    return pl.pallas_call(
        body, grid=(s // ts,), in_specs=[_rows(ts, 3 * D)] + [_rows(ts, D)] * 3, out_specs=_rows(ts, D),
        out_shape=jax.ShapeDtypeStruct((s, D), BF16), compiler_params=_params(("parallel",)), name=name,
    )(pgt, ya, yb, yc)


def _merge_bwd(dmerged, pgt, ya, yb, yc, name):
    s = pgt.shape[0]
    ts = _tile(s, TS_ROW, 8)

    def body(dm_ref, gt_ref, ya_ref, yb_ref, yc_ref, dya_ref, dyb_ref, dyc_ref, dgt_ref):
        dm = dm_ref[...].astype(F32)
        for j, (y_ref, dy_ref) in enumerate(((ya_ref, dya_ref), (yb_ref, dyb_ref), (yc_ref, dyc_ref))):
            sig = _sigmoid(gt_ref[:, j * D:(j + 1) * D].astype(F32))
            dy_ref[...] = (dm * sig).astype(dy_ref.dtype)
            dgt_ref[:, j * D:(j + 1) * D] = (dm * y_ref[...].astype(F32) * sig * (1.0 - sig)).astype(dgt_ref.dtype)

    return pl.pallas_call(
        body, grid=(s // ts,), in_specs=[_rows(ts, D), _rows(ts, 3 * D)] + [_rows(ts, D)] * 3,
        out_specs=[_rows(ts, D)] * 3 + [_rows(ts, 3 * D)],
        out_shape=[jax.ShapeDtypeStruct((s, D), BF16)] * 3 + [jax.ShapeDtypeStruct((s, 3 * D), BF16)],
        compiler_params=_params(("parallel",)), name=name,
    )(dmerged, pgt, ya, yb, yc)


def _adamw(w, g, m, v, name):
    r, c = w.shape[-2:]
    tr, tc = _block_of(r, c)

    def spec(a):
        if a.ndim == 2:
            return pl.BlockSpec((tr, tc), lambda i, j: (i, j))
        return pl.BlockSpec((None, tr, tc), lambda i, j: (0, i, j))

    def body(w_ref, g_ref, m_ref, v_ref, d_ref, mo_ref, vo_ref):
        gv = g_ref[...]
        mn = ADAM_B1 * m_ref[...] + (1.0 - ADAM_B1) * gv
        vn = ADAM_B2 * v_ref[...] + (1.0 - ADAM_B2) * (gv * gv)
        m_hat = mn / (1.0 - ADAM_B1 ** ADAM_STEP)
        v_hat = vn / (1.0 - ADAM_B2 ** ADAM_STEP)
        d_ref[...] = -ADAM_LR * (m_hat / (jnp.sqrt(v_hat) + ADAM_EPS) + ADAM_WD * w_ref[...])
        mo_ref[...] = mn
        vo_ref[...] = vn

    return pl.pallas_call(
        body, grid=(r // tr, c // tc), in_specs=[spec(a) for a in (w, g, m, v)], out_specs=[spec(w)] * 3,
        out_shape=[jax.ShapeDtypeStruct(w.shape, F32)] * 3, compiler_params=_params(("parallel", "parallel")), name=name,
    )(w, g, m, v)


ANY = pl.BlockSpec(memory_space=pl.ANY)


def _place():
    x, y, c = lax.axis_index("x"), lax.axis_index("y"), lax.axis_index("c")
    chips = [(1 - x, y), (x, 1 - y), (1 - x, 1 - y)]
    return x, y, c, chips


def _half(c, rows):
    h = rows // 2
    return pl.ds(pl.multiple_of(c * h, 8), h)


def _by_cols(rows):
    return rows % 32 != 0 and rows != 16


def _half_of(ref, lead, c):
    r, cols = ref.shape[-2:]
    if _by_cols(r):
        return ref.at[(*lead, slice(None), pl.ds(pl.multiple_of(c * (cols // 2), LANE), cols // 2))]
    return ref.at[(*lead, pl.ds(pl.multiple_of(c * (r // 2), 8), r // 2))]


def _half_shape(shape):
    r, cols = shape[-2:]
    return shape[:-2] + ((r, cols // 2) if _by_cols(r) else (r // 2, cols))


def _block_of(r, cols, cap=256):
    if r % 16 == 0:
        return _tile(r, cap, 16), cols
    return r, _tile(cols, cap)


def _place_shard(shard, chip_arr, out_dtype, name):
    _, r, cols = shard.shape
    tr, tc = _block_of(r, cols)

    def body(chip_ref, s_ref, o_ref):
        o_ref[...] = s_ref[...].astype(o_ref.dtype)

    return pl.pallas_call(
        body,
        grid_spec=pltpu.PrefetchScalarGridSpec(
            num_scalar_prefetch=1, grid=(r // tr, cols // tc),
            in_specs=[pl.BlockSpec((None, tr, tc), lambda i, j, chip_ref: (0, i, j))],
            out_specs=pl.BlockSpec((None, tr, tc), lambda i, j, chip_ref: (chip_ref[0], i, j))),
        out_shape=jax.ShapeDtypeStruct((4, r, cols), out_dtype),
        compiler_params=_params(("parallel", "parallel")), name=name,
    )(chip_arr, shard)


def _gather_shards(bufs, name):
    n = len(bufs)

    def body(*refs):
        outs = refs[n:2 * n]
        send_ici, recv_ici, send_d2d, recv_d2d = refs[2 * n:]
        x, y, c, chips = _place()
        me = 2 * x + y
        sibling = (x, y, 1 - c)

        def ici(w, p, chip_of_block, to):
            rows = _half(c, outs[w].shape[1])
            block = outs[w].at[chip_of_block, rows]
            return pltpu.make_async_remote_copy(
                src_ref=block, dst_ref=block, send_sem=send_ici.at[w, p], recv_sem=recv_ici.at[w, p], device_id=to, device_id_type=MESH)

        def d2d(w, p, chip_of_block, half_of):
            rows = _half(half_of, outs[w].shape[1])
            block = outs[w].at[chip_of_block, rows]
            return pltpu.make_async_remote_copy(
                src_ref=block, dst_ref=block, send_sem=send_d2d.at[w, p], recv_sem=recv_d2d.at[w, p], device_id=sibling, device_id_type=MESH)

        sends = [ici(w, p, me, (*chip, c)) for p, chip in enumerate(chips) for w in range(n)]
        for cp in sends:
            cp.start()
        passed = []
        for p, (px, py) in enumerate(chips):
            for w in range(n):
                ici(w, p, 2 * px + py, (px, py, c)).wait_recv()
                fwd = d2d(w, p, 2 * px + py, c)
                fwd.start()
                passed.append(fwd)
        for p, (px, py) in enumerate(chips):
            for w in range(n):
                d2d(w, p, 2 * px + py, 1 - c).wait_recv()
        for cp in sends + passed:
            cp.wait_send()

    return pl.pallas_call(
        body, in_specs=[ANY] * n, out_specs=[ANY] * n,
        out_shape=[jax.ShapeDtypeStruct(a.shape, a.dtype) for a in bufs],
        input_output_aliases={w: w for w in range(n)},
        scratch_shapes=[pltpu.SemaphoreType.DMA((n, 3))] * 4,
        compiler_params=pltpu.CompilerParams(has_side_effects=True), name=name,
    )(*bufs)


HBM = pl.BlockSpec(memory_space=pltpu.HBM)
SEM = pl.BlockSpec(memory_space=pltpu.SEMAPHORE)
EFFECT = pltpu.SideEffectType.DATAFLOW_SIDE_EFFECTING


def _in_hbm(arrays):
    return [pltpu.with_memory_space_constraint(a, pltpu.HBM) for a in arrays]


def _gather_start(bufs, after, name):
    n, na = len(bufs), len(after)

    def body(*refs):
        send_sem, recv_sem = refs[n + na], refs[n + na + 1]
        outs = refs[n + na + 2:2 * n + na + 2]
        token = refs[2 * n + na + 2]
        x, y, c, chips = _place()
        me = 2 * x + y
        for p, chip in enumerate(chips):
            for w in range(n):
                block = _half_of(outs[w], (me,), c)
                pltpu.make_async_remote_copy(
                    src_ref=block, dst_ref=block, send_sem=send_sem, recv_sem=recv_sem,
                    device_id=(*chip, c), device_id_type=MESH).start()
        token[...] = jnp.zeros_like(token)

    out = pl.pallas_call(
        body, name=name, in_specs=[HBM] * n + [ANY] * na,
        out_specs=[SEM, SEM] + [HBM] * n + [pl.BlockSpec(memory_space=pltpu.VMEM)],
        out_shape=[pltpu.SemaphoreType.DMA(()), pltpu.SemaphoreType.DMA(())]
        + [pltpu.HBM(a.shape, a.dtype) for a in bufs] + [jax.ShapeDtypeStruct((8, LANE), F32)],
        input_output_aliases={w: w + 2 for w in range(n)},
        compiler_params=pltpu.CompilerParams(has_side_effects=EFFECT),
    )(*_in_hbm(bufs), *after)
    return out[0], out[1], list(out[2:2 + n]), out[2 + n]


def _gather_pass(bufs, send_sem, recv_sem, after, name):
    n, na = len(bufs), len(after)

    def body(*refs):
        send1, recv1 = refs[n], refs[n + 1]
        send2, recv2 = refs[n + 2 + na], refs[n + 3 + na]
        outs = refs[n + 4 + na:2 * n + 4 + na]
        x, y, c, chips = _place()
        me = 2 * x + y
        arrivals = [(w, px, py) for px, py in chips for w in range(n)]
        for w, px, py in arrivals:
            first = pltpu.make_async_remote_copy(
                src_ref=_half_of(outs[w], (me,), c), dst_ref=_half_of(outs[w], (2 * px + py,), c), send_sem=send1, recv_sem=recv1,
                device_id=(px, py, c), device_id_type=MESH)
            first.wait_send()
            first.wait_recv()
        for w, px, py in arrivals:
            arrived = _half_of(outs[w], (2 * px + py,), c)
            pltpu.make_async_remote_copy(
                src_ref=arrived, dst_ref=arrived, send_sem=send2, recv_sem=recv2,
                device_id=(x, y, 1 - c), device_id_type=MESH).start()

    out = pl.pallas_call(
        body, name=name, in_specs=[HBM] * n + [SEM, SEM] + [ANY] * na,
        out_specs=[SEM, SEM] + [HBM] * n,
        out_shape=[pltpu.SemaphoreType.DMA(()), pltpu.SemaphoreType.DMA(())] + [pltpu.HBM(a.shape, a.dtype) for a in bufs],
        input_output_aliases={w: w + 2 for w in range(n)},
        compiler_params=pltpu.CompilerParams(has_side_effects=EFFECT),
    )(*bufs, send_sem, recv_sem, *after)
    return out[0], out[1], list(out[2:])


def _gather_finish(bufs, send_sem, recv_sem, after, name):
    n, na = len(bufs), len(after)

    def body(*refs):
        send2, recv2 = refs[n], refs[n + 1]
        outs = refs[n + 2 + na:2 * n + 2 + na]
        x, y, c, chips = _place()
        for p, (px, py) in enumerate(chips):
            for w in range(n):
                passed = pltpu.make_async_remote_copy(
                    src_ref=_half_of(outs[w], (2 * px + py,), c), dst_ref=_half_of(outs[w], (2 * px + py,), 1 - c),
                    send_sem=send2, recv_sem=recv2, device_id=(x, y, 1 - c), device_id_type=MESH)
                passed.wait_send()
                passed.wait_recv()

    out = pl.pallas_call(
        body, name=name, in_specs=[HBM] * n + [SEM, SEM] + [ANY] * na, out_specs=[HBM] * n,
        out_shape=[pltpu.HBM(a.shape, a.dtype) for a in bufs],
        input_output_aliases={w: w for w in range(n)},
        compiler_params=pltpu.CompilerParams(has_side_effects=EFFECT),
    )(*bufs, send_sem, recv_sem, *after)
    return list(out)


def _pair_exchange(grads, name):
    n = len(grads)

    def body(*refs):
        ins, outs = refs[:n], refs[n:2 * n]
        send_sem, recv_sem = refs[2 * n:]
        x, y, c, _ = _place()
        copies = []
        for w in range(n):
            copies.append(pltpu.make_async_remote_copy(
                src_ref=_half_of(ins[w], (slice(None),), 1 - c), dst_ref=outs[w], send_sem=send_sem.at[w], recv_sem=recv_sem.at[w],
                device_id=(x, y, 1 - c), device_id_type=MESH))
        for cp in copies:
            cp.start()
        for cp in copies:
            cp.wait()

    return pl.pallas_call(
        body, in_specs=[ANY] * n, out_specs=[ANY] * n,
        out_shape=[jax.ShapeDtypeStruct(_half_shape(a.shape), a.dtype) for a in grads],
        scratch_shapes=[pltpu.SemaphoreType.DMA((n,))] * 2,
        compiler_params=pltpu.CompilerParams(has_side_effects=True), name=name,
    )(*grads)


def _pair_sum(g, got, c_arr, name):
    _, r, cols = g.shape
    hr, hc = _half_shape((r, cols))
    tr, tc = _block_of(hr, hc)
    nbr, nbc = hr // tr, hc // tc
    by_cols = _by_cols(r)

    def body(c_ref, g_ref, got_ref, o_ref):
        o_ref[...] = (g_ref[...].astype(F32) + got_ref[...].astype(F32)).astype(o_ref.dtype)

    def mine(j, i, k, c_ref):
        return (j, i, c_ref[0] * nbc + k) if by_cols else (j, c_ref[0] * nbr + i, k)

    return pl.pallas_call(
        body,
        grid_spec=pltpu.PrefetchScalarGridSpec(
            num_scalar_prefetch=1, grid=(4, nbr, nbc),
            in_specs=[pl.BlockSpec((None, tr, tc), mine),
                      pl.BlockSpec((None, tr, tc), lambda j, i, k, c_ref: (j, i, k))],
            out_specs=pl.BlockSpec((None, tr, tc), lambda j, i, k, c_ref: (j, i, k))),
        out_shape=jax.ShapeDtypeStruct((4, hr, hc), BF16),
        compiler_params=_params(("parallel", "parallel", "parallel")), name=name,
    )(c_arr, g, got)


def _chip_exchange(parts, name):
    n = len(parts)

    def body(*refs):
        ins, outs = refs[:n], refs[n:2 * n]
        send_sem, recv_sem = refs[2 * n:]
        x, y, c, chips = _place()
        copies = []
        for p, (px, py) in enumerate(chips):
            for w in range(n):
                copies.append(pltpu.make_async_remote_copy(
                    src_ref=ins[w].at[2 * px + py], dst_ref=outs[w].at[p], send_sem=send_sem.at[w, p], recv_sem=recv_sem.at[w, p],
                    device_id=(px, py, c), device_id_type=MESH))
        for cp in copies:
            cp.start()
        for cp in copies:
            cp.wait()

    return pl.pallas_call(
        body, in_specs=[ANY] * n, out_specs=[ANY] * n,
        out_shape=[jax.ShapeDtypeStruct((3,) + a.shape[1:], a.dtype) for a in parts],
        scratch_shapes=[pltpu.SemaphoreType.DMA((n, 3))] * 2,
        compiler_params=pltpu.CompilerParams(has_side_effects=True), name=name,
    )(*parts)


def _chip_exchange_start(parts, after, name):
    n, na = len(parts), len(after)
    lands = [lax.empty((3,) + a.shape[1:], a.dtype) for a in parts]

    def body(*refs):
        send_sem, recv_sem = refs[2 * n + na], refs[2 * n + na + 1]
        srcs = refs[2 * n + na + 2:3 * n + na + 2]
        dsts = refs[3 * n + na + 2:4 * n + na + 2]
        token = refs[4 * n + na + 2]
        x, y, c, chips = _place()
        for p, (px, py) in enumerate(chips):
            for w in range(n):
                pltpu.make_async_remote_copy(
                    src_ref=srcs[w].at[2 * px + py], dst_ref=dsts[w].at[p], send_sem=send_sem, recv_sem=recv_sem,
                    device_id=(px, py, c), device_id_type=MESH).start()
        token[...] = jnp.zeros_like(token)

    out = pl.pallas_call(
        body, name=name, in_specs=[HBM] * (2 * n) + [ANY] * na,
        out_specs=[SEM, SEM] + [HBM] * (2 * n) + [pl.BlockSpec(memory_space=pltpu.VMEM)],
        out_shape=[pltpu.SemaphoreType.DMA(()), pltpu.SemaphoreType.DMA(())]
        + [pltpu.HBM(a.shape, a.dtype) for a in parts + lands] + [jax.ShapeDtypeStruct((8, LANE), F32)],
        input_output_aliases={w: w + 2 for w in range(2 * n)},
        compiler_params=pltpu.CompilerParams(has_side_effects=EFFECT),
    )(*_in_hbm(parts), *_in_hbm(lands), *after)
    return out[0], out[1], list(out[2:2 + n]), list(out[2 + n:2 + 2 * n]), out[2 + 2 * n]


def _chip_exchange_finish(parts, lands, send_sem, recv_sem, after, name):
    n, na = len(parts), len(after)

    def body(*refs):
        send, recv = refs[2 * n], refs[2 * n + 1]
        srcs = refs[2 * n + 2 + na:3 * n + 2 + na]
        dsts = refs[3 * n + 2 + na:4 * n + 2 + na]
        x, y, c, chips = _place()
        for p, (px, py) in enumerate(chips):
            for w in range(n):
                copy = pltpu.make_async_remote_copy(
                    src_ref=srcs[w].at[2 * px + py], dst_ref=dsts[w].at[p], send_sem=send, recv_sem=recv,
                    device_id=(px, py, c), device_id_type=MESH)
                copy.wait_send()
                copy.wait_recv()

    out = pl.pallas_call(
        body, name=name, in_specs=[HBM] * (2 * n) + [SEM, SEM] + [ANY] * na, out_specs=[HBM] * (2 * n),
        out_shape=[pltpu.HBM(a.shape, a.dtype) for a in parts + lands],
        input_output_aliases={w: w for w in range(2 * n)},
        compiler_params=pltpu.CompilerParams(has_side_effects=EFFECT),
    )(*parts, *lands, send_sem, recv_sem, *after)
    return list(out[:n]), list(out[n:])


def _chip_sum(part, got, place_arr, name):
    _, hr, hc = part.shape
    by_cols = _by_cols(hr)
    tr, tc = _block_of(hr, hc)
    nbr, nbc = hr // tr, hc // tc

    def body(place_ref, p_ref, got_ref, o_ref):
        acc = p_ref[...].astype(F32)
        for p in range(3):
            acc = acc + got_ref[p].astype(F32)
        o_ref[...] = acc

    def mine(i, k, place_ref):
        return (i, place_ref[1] * nbc + k) if by_cols else (place_ref[1] * nbr + i, k)

    return pl.pallas_call(
        body,
        grid_spec=pltpu.PrefetchScalarGridSpec(
            num_scalar_prefetch=1, grid=(nbr, nbc),
            in_specs=[pl.BlockSpec((None, tr, tc), lambda i, k, place_ref: (place_ref[0], i, k)),
                      pl.BlockSpec((3, tr, tc), lambda i, k, place_ref: (0, i, k))],
            out_specs=pl.BlockSpec((tr, tc), mine)),
        out_shape=jax.ShapeDtypeStruct((hr, 2 * hc) if by_cols else (2 * hr, hc), F32),
        compiler_params=_params(("parallel", "parallel")), name=name,
    )(place_arr, part, got)


def _pair_join(bufs, name):
    n = len(bufs)

    def body(*refs):
        outs = refs[n:2 * n]
        send_sem, recv_sem = refs[2 * n:]
        x, y, c, _ = _place()
        copies = []
        for w in range(n):
            block = _half_of(outs[w], (), c)
            copies.append(pltpu.make_async_remote_copy(
                src_ref=block, dst_ref=block, send_sem=send_sem.at[w], recv_sem=recv_sem.at[w],
                device_id=(x, y, 1 - c), device_id_type=MESH))
        for cp in copies:
            cp.start()
        for w, cp in enumerate(copies):
            cp.wait_send()
            block = _half_of(outs[w], (), 1 - c)
            pltpu.make_async_remote_copy(
                src_ref=block, dst_ref=block, send_sem=send_sem.at[w], recv_sem=recv_sem.at[w],
                device_id=(x, y, 1 - c), device_id_type=MESH).wait_recv()

    return pl.pallas_call(
        body, in_specs=[ANY] * n, out_specs=[ANY] * n,
        out_shape=[jax.ShapeDtypeStruct(a.shape, a.dtype) for a in bufs],
        input_output_aliases={w: w for w in range(n)},
        scratch_shapes=[pltpu.SemaphoreType.DMA((n,))] * 2,
        compiler_params=pltpu.CompilerParams(has_side_effects=True), name=name,
    )(*bufs)


def _all_sum(pack, name):
    r, cols = pack.shape

    def body(x_ref, o_ref, all_ref, send_sems, recv_sems):
        x, y, c, chips = _place()
        me, sibling = (x, y, c), (x, y, 1 - c)

        def slot(px, py, pc):
            return all_ref.at[4 * px + 2 * py + pc]

        def copy(k, block, to, src=None):
            return pltpu.make_async_remote_copy(
                src_ref=slot(*block) if src is None else src, dst_ref=slot(*block),
                send_sem=send_sems.at[k], recv_sem=recv_sems.at[k], device_id=to, device_id_type=MESH)

        all_ref[4 * x + 2 * y + c] = x_ref[...]
        first = [copy(0, me, sibling, src=x_ref)]
        first += [copy(1 + j, me, (*chip, c), src=x_ref) for j, chip in enumerate(chips)]
        for cp in first:
            cp.start()
        passed = [copy(4 + j, (*chip, c), sibling) for j, chip in enumerate(chips)]
        for j, chip in enumerate(chips):
            copy(1 + j, (*chip, c), me).wait_recv()
            passed[j].start()
        copy(0, sibling, me).wait_recv()
        for j, chip in enumerate(chips):
            copy(4 + j, (*chip, 1 - c), me).wait_recv()
        for cp in first + passed:
            cp.wait_send()
        acc = all_ref[0]
        for k in range(1, 8):
            acc = acc + all_ref[k]
        o_ref[...] = acc

    return pl.pallas_call(
        body, in_specs=[pl.BlockSpec(memory_space=pltpu.VMEM)], out_specs=pl.BlockSpec(memory_space=pltpu.VMEM),
        out_shape=jax.ShapeDtypeStruct((r, cols), F32),
        scratch_shapes=[pltpu.VMEM((8, r, cols), F32), pltpu.SemaphoreType.DMA((7,)), pltpu.SemaphoreType.DMA((7,))],
        compiler_params=pltpu.CompilerParams(has_side_effects=True, vmem_limit_bytes=VMEM_LIMIT), name=name,
    )(pack)


def _ffn_fwd(x_norm, w_in, w_out, tag, between=None):
    ab = _mm(x_norm, w_in, out_dtype=BF16, tn=1408, name=tag + "_in")
    u = _swiglu_fwd(ab, name=tag + "_swiglu")
    behind = (between(u) if between is not None else None) or ()
    f = _mm(u, w_out, tk=DFF, after=behind, name=tag + "_out")
    return ab, u, f


def _ffn_bwd(dz, x_norm, ab, u, w_in, w_out, tag, emit, after=()):
    dw_out = _mm(u, dz, ta=True, out_dtype=BF16, tm=1408, tk=2048, after=after, name=tag + "_out_dw")
    behind = emit(tag + "_w_out", dw_out)
    du = _mm(dz, w_out, tb=True, out_dtype=BF16, tn=1408, after=behind, name=tag + "_out_dx")
    dab = _swiglu_bwd(ab, du, name=tag + "_swiglu_bwd")
    dw_in = _mm(x_norm, dab, ta=True, out_dtype=BF16, tm=512, tk=4096, shards=4, name=tag + "_in_dw")
    behind = emit(tag + "_w_in", dw_in)
    return _mm(dab, w_in, tb=True, tk=2816, after=behind, name=tag + "_in_dx")


def _local_step(x, mem, target, small, gather, emit):
    big, behind = gather("ffn1", ())
    h1 = _norm_fwd(x, small["ffn1_pre_g"], BF16, name="ffn1_pre", after=behind)
    ab1, u1, f1 = _ffn_fwd(h1, big["ffn1_w_in"], big["ffn1_w_out"], "ffn1", between=lambda u: gather("mix_pass", (u,)))
    more, behind = gather("mix", (f1,))
    big.update(more)
    small = dict(small, w_fu_pad=big["w_fu_pad"])
    x1, h = _resid_norm_fwd(x, f1, small["ffn1_post_g"], 0.5, small["mix_pre_g"], name="ffn1_post")
    pg = _mm(h, big["w_gla_t"], tb=True, out_dtype=BF16, tn=PG_W, after=behind, name="mix_in_gla")
    ppx = _mm(h, big["w_px_t"], tb=True, out_dtype=BF16, name="mix_in_px")
    pgt = _mm(h, big["w_gates_t"], tb=True, out_dtype=BF16, tn=1536, name="mix_in_gates")
    mem_n = _norm_fwd(mem, small["mem_norm_g"], BF16, name="mem_norm")
    kv = _mm(mem_n, big["w_mem_kv"], out_dtype=BF16, name="mem_kv")
    ya_in, sp = _gla_fwd(pg, small["w_fu_pad"], small["b_f"], small["gla_norm_g"], name="gla_fwd")
    yb_in = _pool_fwd(ppx, small["w_pool_b"], small["pool_scale"], name="pool_fwd")
    xc = _xattn_fwd(ppx, kv, name="xattn_fwd")
    ya = _mm(ya_in, big["w_up_gla"], out_dtype=BF16, name="up_gla")
    yb = _mm(yb_in, big["w_up_pool"], out_dtype=BF16, name="up_pool")
    yc = _mm(xc, big["w_up_xattn"], out_dtype=BF16, name="up_xattn")
    merged = _merge_fwd(pgt, ya, yb, yc, name="merge_fwd")
    gather("ffn2_pass", (merged,))
    ymix = _mm(merged, big["w_o"], name="mix_out")
    more, _ = gather("ffn2", (ymix,))
    big.update(more)
    x2, h2 = _resid_norm_fwd(x1, ymix, small["mix_post_g"], 1.0, small["ffn2_pre_g"], name="mix_post")
    ab2, u2, f2 = _ffn_fwd(h2, big["ffn2_w_in"], big["ffn2_w_out"], "ffn2")
    x3, _ = _resid_norm_fwd(x2, f2, small["ffn2_post_g"], 0.5, None, name="ffn2_post")
    gs = {}
    dx3, gs["final_g"], loss = _loss_bwd(x3, small["final_g"], target, name="loss")
    dz2, gs["ffn2_post_g"] = _rms_bwd(f2, small["ffn2_post_g"], [dx3], None, 0.5, BF16, name="ffn2_post_bwd")
    dh2 = _ffn_bwd(dz2, h2, ab2, u2, big["ffn2_w_in"], big["ffn2_w_out"], "ffn2", emit)
    dx2, gs["ffn2_pre_g"] = _rms_bwd(x2, small["ffn2_pre_g"], [dh2], dx3, 1.0, F32, name="ffn2_pre_bwd")
    dy, gs["mix_post_g"] = _rms_bwd(ymix, small["mix_post_g"], [dx2], None, 1.0, BF16, name="mix_post_bwd")
    dmerged = _mm(dy, big["w_o"], tb=True, out_dtype=BF16, name="mix_out_dx")
    emit("w_o", _mm(merged, dy, ta=True, out_dtype=BF16, tm=512, tk=4096, name="mix_out_dw"))
    dya, dyb, dyc, dgt = _merge_bwd(dmerged, pgt, ya, yb, yc, name="merge_bwd")
    dya_in = _mm(dya, big["w_up_gla"], tb=True, out_dtype=BF16, name="up_gla_dx")
    emit("w_up_gla", _mm(ya_in, dya, ta=True, out_dtype=BF16, tm=512, tk=4096, name="up_gla_dw"))
    dyb_in = _mm(dyb, big["w_up_pool"], tb=True, out_dtype=BF16, name="up_pool_dx")
    emit("w_up_pool", _mm(yb_in, dyb, ta=True, out_dtype=BF16, tm=512, tk=4096, shards=4, name="up_pool_dw"))
    dxc = _mm(dyc, big["w_up_xattn"], tb=True, out_dtype=BF16, name="up_xattn_dx")
    emit("w_up_xattn", _mm(xc, dyc, ta=True, out_dtype=BF16, tm=512, tk=4096, shards=4, name="up_xattn_dw"))
    dpg, gs["w_fu_pad"], gs["b_f"], gs["gla_norm_g"] = _gla_bwd(pg, sp, dya_in, small["w_fu_pad"], small["b_f"], small["gla_norm_g"], name="gla_bwd")
    dp, gs["w_pool"], gs["pool_scale"] = _pool_bwd(dyb_in, ppx, small["w_pool_b"], small["pool_scale"], name="pool_bwd")
    dxq, dkv = _xattn_bwd(dxc, ppx, kv, name="xattn_bwd")
    dkv = dkv.astype(BF16)
    emit("w_mem_kv", _mm(mem_n, dkv, ta=True, out_dtype=BF16, name="mem_kv_dw"))
    dmem_n = _mm(dkv, big["w_mem_kv"], tb=True, name="mem_kv_dx")
    _, gs["mem_norm_g"] = _rms_bwd(mem, small["mem_norm_g"], [dmem_n], None, 1.0, BF16, name="mem_norm_bwd")
    emit("w_gla", _mm(dpg, h, ta=True, out_dtype=BF16, tm=640, tk=4096, name="mix_in_gla_dw"))
    emit("w_p", _mm(dp, h, ta=True, out_dtype=BF16, tm=512, tk=4096, name="mix_in_p_dw"))
    emit("w_xq", _mm(dxq, h, ta=True, out_dtype=BF16, tm=512, tk=4096, name="mix_in_xq_dw"))
    behind = emit("w_gates", _mm(dgt, h, ta=True, out_dtype=BF16, tm=512, tk=4096, name="mix_in_gates_dw"))
    dh_parts = [
        _mm(dpg, big["w_gla_t"], tk=PG_W, after=behind, name="mix_in_gla_dx"),
        _mm(dp, big["w_p_t"], name="mix_in_p_dx"),
        _mm(dxq, big["w_xq_t"], name="mix_in_xq_dx"),
        _mm(dgt, big["w_gates_t"], tk=3072, name="mix_in_gates_dx"),
    ]
    dx1, gs["mix_pre_g"] = _rms_bwd(x1, small["mix_pre_g"], dh_parts, dx2, 1.0, F32, name="mix_pre_bwd")
    dz1, gs["ffn1_post_g"] = _rms_bwd(f1, small["ffn1_post_g"], [dx1], None, 0.5, BF16, name="ffn1_post_bwd")
    dh1 = _ffn_bwd(dz1, h1, ab1, u1, big["ffn1_w_in"], big["ffn1_w_out"], "ffn1", emit)
    dx0, gs["ffn1_pre_g"] = _rms_bwd(x, small["ffn1_pre_g"], [dh1], dx1, 1.0, F32, name="ffn1_pre_bwd")
    return loss, dx0, gs


BIG = ("ffn1_w_in", "ffn1_w_out", "w_in", "w_mem_kv", "w_up_gla", "w_up_pool", "w_up_xattn", "w_o", "ffn2_w_in", "ffn2_w_out")
COL_SHARDED = ("ffn1_w_in", "w_in", "w_up_pool", "w_up_xattn", "ffn2_w_in")
GATHER_GROUPS = {"ffn1": ("ffn1_w_in", "ffn1_w_out"),
                 "mix": ("w_in", "w_mem_kv", "w_up_gla", "w_up_pool", "w_up_xattn", "w_o", "w_fu"),
                 "ffn2": ("ffn2_w_in", "ffn2_w_out")}
REDUCE_GROUPS = {"ffn2": ("ffn2_w_out", "ffn2_w_in"),
                 "mix": ("w_o", "w_up_gla", "w_up_pool", "w_up_xattn", "w_mem_kv", "w_gla", "w_p", "w_xq", "w_gates"),
                 "ffn1_out": ("ffn1_w_out",),
                 "ffn1_in": ("ffn1_w_in",)}
GAINS = ("ffn1_pre_g", "ffn1_post_g", "mix_pre_g", "gla_norm_g", "mem_norm_g", "mix_post_g", "ffn2_pre_g", "ffn2_post_g", "final_g")
WEIGHTS = ("ffn1_pre_g", "ffn1_w_in", "ffn1_w_out", "ffn1_post_g", "mix_pre_g", "w_in", "w_fu", "b_f", "gla_norm_g", "w_pool",
           "pool_scale", "mem_norm_g", "w_mem_kv", "w_up_gla", "w_up_pool", "w_up_xattn", "w_o", "mix_post_g", "ffn2_pre_g",
           "ffn2_w_in", "ffn2_w_out", "ffn2_post_g", "final_g")
IN_GLA, IN_F, IN_PX, IN_GATES, IN_END = 0, 3072, 3088, 4112, 7184
PACK_ROWS = 96


def _cols_from_shards(g):
    return jnp.transpose(g, (1, 0, 2)).reshape(g.shape[1], 4 * g.shape[2])


def _pack_small(t):
    single = jnp.zeros((PACK_ROWS - 72, D), F32)
    for i, n in enumerate(GAINS):
        single = single.at[i].set(t[n].reshape(D))
    k = len(GAINS)
    single = single.at[k, 0:512].set(t["b_f"].reshape(512))
    single = single.at[k, 512:1024].set(t["pool_scale"].reshape(512))
    return jnp.concatenate([t["w_pool"].reshape(64, D), t["w_fu"].reshape(8, D), single], axis=0)


def _unpack_small(p):
    out = {n: p[72 + i:73 + i] for i, n in enumerate(GAINS)}
    k = 72 + len(GAINS)
    out["b_f"] = p[k:k + 1, 0:512]
    out["pool_scale"] = p[k:k + 1, 512:1024]
    out["w_pool"] = p[0:64].reshape(4, LANE, LANE)
    out["w_fu"] = p[64:72].reshape(GATE_RANK, 512)
    return out


def kernel(x, mem, ffn1_pre_g, ffn1_w_in, ffn1_w_out, ffn1_post_g, mix_pre_g, w_in, w_fu, b_f, gla_norm_g, w_pool, pool_scale, mem_norm_g, w_mem_kv, w_up_gla, w_up_pool, w_up_xattn, w_o, mix_post_g, ffn2_pre_g, ffn2_w_in, ffn2_w_out, ffn2_post_g, final_g, loss_target, m_ffn1_pre_g, m_ffn1_w_in, m_ffn1_w_out, m_ffn1_post_g, m_mix_pre_g, m_w_in, m_w_fu, m_b_f, m_gla_norm_g, m_w_pool, m_pool_scale, m_mem_norm_g, m_w_mem_kv, m_w_up_gla, m_w_up_pool, m_w_up_xattn, m_w_o, m_mix_post_g, m_ffn2_pre_g, m_ffn2_w_in, m_ffn2_w_out, m_ffn2_post_g, m_final_g, v_ffn1_pre_g, v_ffn1_w_in, v_ffn1_w_out, v_ffn1_post_g, v_mix_pre_g, v_w_in, v_w_fu, v_b_f, v_gla_norm_g, v_w_pool, v_pool_scale, v_mem_norm_g, v_w_mem_kv, v_w_up_gla, v_w_up_pool, v_w_up_xattn, v_w_o, v_mix_post_g, v_ffn2_pre_g, v_ffn2_w_in, v_ffn2_w_out, v_ffn2_post_g, v_final_g):
    args = dict(locals())
    w = {n: args[n][0] for n in WEIGHTS}
    m = {n: args["m_" + n][0] for n in WEIGHTS}
    v = {n: args["v_" + n][0] for n in WEIGHTS}
    xi, yi, ci = lax.axis_index("x"), lax.axis_index("y"), lax.axis_index("c")
    chip = 2 * xi + yi

    c_arr = jnp.reshape(ci, (1,)).astype(jnp.int32)
    chip_arr = jnp.reshape(chip, (1,)).astype(jnp.int32)
    place_arr = jnp.stack([chip, ci]).astype(jnp.int32)
    shard_of = {n: (jnp.transpose(args[n][0])[None] if n == "w_in" else args[n]) for n in BIG}
    placed = {n: _place_shard(shard_of[n], chip_arr, BF16, name="place_" + n) for n in BIG}
    placed["w_fu"] = _place_shard(args["w_fu"], chip_arr, F32, name="place_w_fu")
    inflight = {}

    def relayout(names, gathered):
        out = {}
        for n, g in zip(names, gathered):
            if n == "w_fu":
                w_fu_full = _cols_from_shards(g)
                out["w_fu_pad"] = jnp.concatenate([w_fu_full, jnp.zeros((LANE - GATE_RANK, 512), F32)], axis=0).astype(BF16)
            elif n == "w_in":
                wt = g.reshape(IN_END, D)
                out["w_gla_t"] = jnp.concatenate([wt[IN_GLA:IN_PX], jnp.zeros((PG_W - IN_PX, D), BF16)], axis=0)
                out["w_px_t"] = wt[IN_PX:IN_GATES]
                out["w_p_t"] = wt[IN_PX:IN_PX + 512]
                out["w_xq_t"] = wt[IN_PX + 512:IN_GATES]
                out["w_gates_t"] = wt[IN_GATES:IN_END]
            else:
                out[n] = _cols_from_shards(g) if n in COL_SHARDED else g.reshape(4 * g.shape[1], g.shape[2])
        return out

    def start(group, after):
        inflight[group] = _gather_start([placed[n] for n in GATHER_GROUPS[group]], after, name="gather_" + group + "_start")

    def gather(step, after):
        group = step.split("_")[0]
        if step.endswith("_pass") or step == "ffn1":
            if step == "ffn1":
                start(group, ())
            send, recv, bufs, _ = inflight[group]
            inflight[group] = _gather_pass(bufs, send, recv, after, name="gather_" + group + "_pass")
            if step != "ffn1":
                return (inflight[group][2][0],)
        send, recv, bufs = inflight.pop(group)
        bufs = _gather_finish(bufs, send, recv, after, name="gather_" + group + "_finish")
        following = {"ffn1": "mix", "mix": "ffn2"}.get(group)
        behind = ()
        if following is not None:
            start(following, (bufs[0],))
            behind = (inflight[following][3],)
        return relayout(GATHER_GROUPS[group], bufs), behind

    small = {n: w[n].reshape(1, D) for n in GAINS}
    small["b_f"] = w["b_f"].reshape(1, 512)
    small["pool_scale"] = w["pool_scale"].reshape(1, 512)
    small["w_pool_b"] = w["w_pool"].astype(BF16)

    pending, travelling = {}, {}

    def emit(name, grad):
        pending[name] = grad
        group = next((g for g, names in REDUCE_GROUPS.items() if name == names[-1]), None)
        if group is None:
            return ()
        gb = {n: pending.pop(n) for n in REDUCE_GROUPS[group]}
        if group == "mix":
            dwt = jnp.concatenate([gb.pop("w_gla")[0:IN_PX], gb.pop("w_p"), gb.pop("w_xq"), gb.pop("w_gates")], axis=0)
            gb["w_in"] = dwt.reshape(4, IN_END // 4, D)
        names = list(gb)
        contrib = [gb[n] if n in COL_SHARDED else gb[n].reshape(4, gb[n].shape[0] // 4, gb[n].shape[1]) for n in names]
        from_sibling = _pair_exchange(contrib, name="grads_" + group + "_pair_exchange")
        pair = [_pair_sum(g, got, c_arr, name="grads_pair_sum_" + n) for n, g, got in zip(names, contrib, from_sibling)]
        send, recv, pair, lands, token = _chip_exchange_start(pair, (), name="grads_" + group + "_chip_start")
        travelling[group] = (names, send, recv, pair, lands)
        return (token,)

    loss, grad_x, gs = _local_step(x[0], mem[0], loss_target[0], small, gather, emit)
    loss = lax.psum(loss[0, 0], ("x", "y", "c"))

    gs["w_fu"] = gs.pop("w_fu_pad")[0:GATE_RANK]
    small_sum = _unpack_small(_all_sum(_pack_small(gs), name="sum_small_grads"))
    halves = {}
    for group, (names, send, recv, pair, lands) in travelling.items():
        pair, from_chips = _chip_exchange_finish(pair, lands, send, recv, (grad_x,), name="grads_" + group + "_chip_finish")
        for n, p, got in zip(names, pair, from_chips):
            halves[n] = _chip_sum(p, got, place_arr, name="grads_chip_sum_" + n)
    reduced = dict(zip(BIG, _pair_join([halves[n] for n in BIG], name="grads_pair_join")))

    grads, delta, new_m, new_v = {}, {}, {}, {}
    for n in BIG:
        if n == "w_in":
            transposed = [jnp.transpose(args[k][0]) for k in (n, "m_" + n, "v_" + n)]
            updated = _adamw(transposed[0], reduced[n], transposed[1], transposed[2], name="adamw_" + n)
            grads[n] = jnp.transpose(reduced[n])[None]
            delta[n], new_m[n], new_v[n] = (jnp.transpose(a)[None] for a in updated)
            continue
        grads[n] = reduced[n][None]
        delta[n], new_m[n], new_v[n] = _adamw(args[n], reduced[n], args["m_" + n], args["v_" + n], name="adamw_" + n)
    small_names = GAINS + ("b_f", "pool_scale", "w_pool")
    w_fu_grad = lax.dynamic_slice_in_dim(small_sum["w_fu"], chip * LANE, LANE, axis=1)
    packs = []
    for t in (w, m, v):
        t = dict(t)
        t["w_fu"] = jnp.zeros((GATE_RANK, 512), F32)
        packs.append(_pack_small(t))
    sd, sm, sv = (_unpack_small(p) for p in _adamw(packs[0], _pack_small(small_sum), packs[1], packs[2], name="adamw_small"))
    for n in small_names:
        shape = args[n].shape
        grads[n] = small_sum[n].reshape(shape)
        delta[n], new_m[n], new_v[n] = sd[n].reshape(shape), sm[n].reshape(shape), sv[n].reshape(shape)
    grads["w_fu"] = w_fu_grad[None]
    delta["w_fu"], new_m["w_fu"], new_v["w_fu"] = _adamw(args["w_fu"], w_fu_grad, args["m_w_fu"], args["v_w_fu"], name="adamw_w_fu")

    outs = [loss, grad_x[None]]
    for group in (grads, delta, new_m, new_v):
        outs += [group[n] for n in WEIGHTS]
    return tuple(outs)
```

```python
import functools

import jax
import jax.numpy as jnp
from jax import lax
from jax.experimental import pallas as pl
from jax.experimental.pallas import tpu as pltpu

F32 = jnp.float32
BF16 = jnp.bfloat16
MESH = pl.DeviceIdType.MESH
HIGHEST = lax.Precision.HIGHEST

D = 1024
DFF = 2816
CHUNK = 64
HEADS = 4
HDK = 128
HDV = 256
GATE_TEMP = 16.0
POOL_WINDOWS = (2, 4, 8, 16)
POOL_HALO = 16
XA_HEADS = 4
XA_HD = 128
EPS = 1e-6
Q_SCALE = HDK ** -0.5
XA_SCALE = XA_HD ** -0.5
PG_Q, PG_K, PG_V, PG_G, PG_F, PG_W = 0, 512, 1024, 2048, 3072, 3200
GATE_RANK = 16
ADAM_LR, ADAM_B1, ADAM_B2, ADAM_EPS, ADAM_WD, ADAM_STEP = 0.001, 0.9, 0.999, 1e-08, 0.01, 10

VMEM_LIMIT = 48 * 1024 * 1024
LANE = 128
TS_ROW = 512
TS_GLA = 512
TS_POOL = 512
TS_XA = 512


def _params(sem):
    return pltpu.CompilerParams(dimension_semantics=sem, vmem_limit_bytes=VMEM_LIMIT)


def _tile(n, cap, unit=LANE):
    if n <= cap:
        return n
    best = None
    for t in range(unit, cap + 1, unit):
        if n % t == 0:
            best = t
    assert best is not None, (n, cap)
    return best


def _sigmoid(x):
    return 1.0 / (1.0 + jnp.exp(-x))


def _log_sigmoid(x):
    return jnp.minimum(x, 0.0) - jnp.log(1.0 + jnp.exp(-jnp.abs(x)))


def _rms(x):
    r = lax.rsqrt(jnp.mean(x * x, axis=-1, keepdims=True) + EPS)
    return x * r, r


def _rows(ts, w):
    return pl.BlockSpec((ts, w), lambda i: (i, 0))


def _fixed(shape):
    nd = len(shape)
    return pl.BlockSpec(shape, lambda i: (0,) * nd)


def _mm(a, b, *, ta=False, tb=False, out_dtype=F32, tm=1024, tn=1024, tk=1024, shards=1, after=(), name):
    m, kdim = (a.shape[1], a.shape[0]) if ta else a.shape
    n = b.shape[0] if tb else b.shape[1]
    assert (b.shape[1] if tb else b.shape[0]) == kdim, (a.shape, b.shape, ta, tb)
    tm = _tile(m, tm)
    tn = n // shards if shards > 1 else _tile(n, tn)
    tk = _tile(kdim, tk)
    nk = kdim // tk
    dims = (((0 if ta else 1,), (1 if tb else 0,)), ((), ()))

    def body(a_ref, b_ref, *rest):
        o_ref, *acc = rest[len(after):]
        part = lax.dot_general(a_ref[...], b_ref[...], dims, preferred_element_type=F32)
        if nk == 1:
            o_ref[...] = part.astype(o_ref.dtype)
            return
        acc_ref, = acc
        k = pl.program_id(2)

        @pl.when(k == 0)
        def _():
            acc_ref[...] = part

        @pl.when(k > 0)
        def _():
            acc_ref[...] += part

        @pl.when(k == nk - 1)
        def _():
            o_ref[...] = acc_ref[...].astype(o_ref.dtype)

    a_spec = pl.BlockSpec((tk, tm), lambda i, j, k: (k, i)) if ta else pl.BlockSpec((tm, tk), lambda i, j, k: (i, k))
    b_spec = pl.BlockSpec((tn, tk), lambda i, j, k: (j, k)) if tb else pl.BlockSpec((tk, tn), lambda i, j, k: (k, j))
    if shards > 1:
        out_shape = jax.ShapeDtypeStruct((shards, m, tn), out_dtype)
        o_spec = pl.BlockSpec((None, tm, tn), lambda i, j, k: (j, i, 0))
    else:
        out_shape = jax.ShapeDtypeStruct((m, n), out_dtype)
        o_spec = pl.BlockSpec((tm, tn), lambda i, j, k: (i, j))
    return pl.pallas_call(
        body, grid=(m // tm, n // tn, nk), in_specs=[a_spec, b_spec] + [ANY] * len(after), out_specs=o_spec, out_shape=out_shape,
        scratch_shapes=[pltpu.VMEM((tm, tn), F32)] if nk > 1 else [],
        compiler_params=_params(("parallel", "parallel", "arbitrary")), name=name,
    )(a, b, *after)


def _norm_fwd(x, g, out_dtype, name, after=()):
    s, d = x.shape
    ts = _tile(s, TS_ROW, 8)

    def body(x_ref, g_ref, *rest):
        o_ref = rest[len(after)]
        xh, _ = _rms(x_ref[...])
        o_ref[...] = (xh * g_ref[...]).astype(o_ref.dtype)

    return pl.pallas_call(
        body, grid=(s // ts,), in_specs=[_rows(ts, d), _fixed((1, d))] + [ANY] * len(after), out_specs=_rows(ts, d),
        out_shape=jax.ShapeDtypeStruct((s, d), out_dtype), compiler_params=_params(("parallel",)), name=name,
    )(x, g, *after)


def _resid_norm_fwd(x, f, g_post, alpha, g_next, name):
    s, d = x.shape
    ts = _tile(s, TS_ROW, 8)
    with_h = g_next is not None

    def body(x_ref, f_ref, gp_ref, *rest):
        fh, _ = _rms(f_ref[...])
        xn = x_ref[...] + alpha * (fh * gp_ref[...])
        if with_h:
            gn_ref, xo_ref, h_ref = rest
            xh, _ = _rms(xn)
            h_ref[...] = (xh * gn_ref[...]).astype(h_ref.dtype)
        else:
            xo_ref, = rest
        xo_ref[...] = xn

    ins = [x, f, g_post] + ([g_next] if with_h else [])
    in_specs = [_rows(ts, d), _rows(ts, d), _fixed((1, d))] + ([_fixed((1, d))] if with_h else [])
    out_shape = [jax.ShapeDtypeStruct((s, d), F32)] + ([jax.ShapeDtypeStruct((s, d), BF16)] if with_h else [])
    out_specs = [_rows(ts, d)] + ([_rows(ts, d)] if with_h else [])
    out = pl.pallas_call(
        body, grid=(s // ts,), in_specs=in_specs, out_specs=out_specs, out_shape=out_shape,
        compiler_params=_params(("parallel",)), name=name,
    )(*ins)
    return (out[0], out[1]) if with_h else (out[0], None)


def _rms_bwd(x, g, dys, dres, alpha, out_dtype, name):
    s, d = x.shape
    ts = _tile(s, TS_ROW, 8)
    ndy = len(dys)
    with_res = dres is not None

    def body(x_ref, g_ref, *rest):
        dy_refs = rest[:ndy]
        rest = rest[ndy:]
        if with_res:
            dres_ref, dx_ref, dg_ref = rest
        else:
            dx_ref, dg_ref = rest
        xh, r = _rms(x_ref[...])
        dy = dy_refs[0][...].astype(F32)
        for ref in dy_refs[1:]:
            dy = dy + ref[...].astype(F32)
        dy = dy * alpha

        @pl.when(pl.program_id(0) == 0)
        def _():
            dg_ref[...] = jnp.zeros_like(dg_ref)

        dg_ref[...] += jnp.sum(dy * xh, axis=0, keepdims=True)
        dyg = dy * g_ref[...]
        dx = r * (dyg - xh * jnp.mean(dyg * xh, axis=-1, keepdims=True))
        if with_res:
            dx = dx + dres_ref[...]
        dx_ref[...] = dx.astype(dx_ref.dtype)

    ins = [x, g] + list(dys) + ([dres] if with_res else [])
    in_specs = [_rows(ts, d), _fixed((1, d))] + [_rows(ts, d)] * (ndy + int(with_res))
    return pl.pallas_call(
        body, grid=(s // ts,), in_specs=in_specs, out_specs=[_rows(ts, d), _fixed((1, d))],
        out_shape=[jax.ShapeDtypeStruct((s, d), out_dtype), jax.ShapeDtypeStruct((1, d), F32)],
        compiler_params=_params(("arbitrary",)), name=name,
    )(*ins)


def _loss_bwd(x, g, target, name):
    s, d = x.shape
    ts = _tile(s, TS_ROW, 8)

    def body(x_ref, g_ref, t_ref, dx_ref, dg_ref, loss_ref):
        xh, r = _rms(x_ref[...])
        gv = g_ref[...]
        diff = xh * gv - t_ref[...]

        @pl.when(pl.program_id(0) == 0)
        def _():
            dg_ref[...] = jnp.zeros_like(dg_ref)
            loss_ref[...] = jnp.zeros_like(loss_ref)

        sq = jnp.sum(diff * diff, axis=1, keepdims=True)
        loss_ref[...] += (0.5 / d) * jnp.sum(sq, axis=0, keepdims=True)
        dy = diff * (1.0 / d)
        dg_ref[...] += jnp.sum(dy * xh, axis=0, keepdims=True)
        dyg = dy * gv
        dx_ref[...] = r * (dyg - xh * jnp.mean(dyg * xh, axis=-1, keepdims=True))

    return pl.pallas_call(
        body, grid=(s // ts,), in_specs=[_rows(ts, d), _fixed((1, d)), _rows(ts, d)],
        out_specs=[_rows(ts, d), _fixed((1, d)), _fixed((8, LANE))],
        out_shape=[jax.ShapeDtypeStruct((s, d), F32), jax.ShapeDtypeStruct((1, d), F32), jax.ShapeDtypeStruct((8, LANE), F32)],
        compiler_params=_params(("arbitrary",)), name=name,
    )(x, g, target)


def _swiglu_fwd(ab, name):
    s = ab.shape[0]
    ts = _tile(s, TS_ROW, 8)

    def body(a_ref, b_ref, u_ref):
        a = a_ref[...].astype(F32)
        u_ref[...] = (a * _sigmoid(a) * b_ref[...].astype(F32)).astype(u_ref.dtype)

    return pl.pallas_call(
        body, grid=(s // ts,),
        in_specs=[pl.BlockSpec((ts, DFF), lambda i: (i, 0)), pl.BlockSpec((ts, DFF), lambda i: (i, 1))],
        out_specs=_rows(ts, DFF), out_shape=jax.ShapeDtypeStruct((s, DFF), BF16),
        compiler_params=_params(("parallel",)), name=name,
    )(ab, ab)


def _swiglu_bwd(ab, du, name):
    s = ab.shape[0]
    ts = _tile(s, TS_ROW, 8)

    def body(a_ref, b_ref, du_ref, dab_ref):
        a = a_ref[...].astype(F32)
        b = b_ref[...].astype(F32)
        dy = du_ref[...].astype(F32)
        sig = _sigmoid(a)
        dab_ref[:, 0:DFF] = (dy * b * (sig * (1.0 + a * (1.0 - sig)))).astype(dab_ref.dtype)
        dab_ref[:, DFF:2 * DFF] = (dy * a * sig).astype(dab_ref.dtype)

    return pl.pallas_call(
        body, grid=(s // ts,),
        in_specs=[pl.BlockSpec((ts, DFF), lambda i: (i, 0)), pl.BlockSpec((ts, DFF), lambda i: (i, 1)), _rows(ts, DFF)],
        out_specs=_rows(ts, 2 * DFF), out_shape=jax.ShapeDtypeStruct((s, 2 * DFF), BF16),
        compiler_params=_params(("parallel",)), name=name,
    )(ab, ab, du)


def _tri(strict):
    r = lax.broadcasted_iota(jnp.int32, (CHUNK, CHUNK), 0)
    c = lax.broadcasted_iota(jnp.int32, (CHUNK, CHUNK), 1)
    return (r > c).astype(F32) if strict else (r >= c).astype(F32)


def _gla_fwd(pg, wfu, b_f, gnorm, name):
    s = pg.shape[0]
    ts = _tile(s, TS_GLA, CHUNK)
    cpb = ts // CHUNK
    nc = s // CHUNK

    def body(pg_ref, wfu_ref, bf_ref, gn_ref, ya_ref, sp_ref, st_ref, la_ref, dec_ref, u_ref):
        @pl.when(pl.program_id(0) == 0)
        def _():
            st_ref[...] = jnp.zeros_like(st_ref)

        f = jnp.dot(pg_ref[:, PG_F:PG_W], wfu_ref[...], preferred_element_type=F32) + bf_ref[...]
        la_ref[...] = _log_sigmoid(f) * (1.0 / GATE_TEMP)
        tri = _tri(False)
        chunks = [slice(ci * CHUNK, (ci + 1) * CHUNK) for ci in range(cpb)]
        for ci, rows in enumerate(chunks):
            la = la_ref[rows, :]
            b = jnp.dot(tri, la, precision=HIGHEST, preferred_element_type=F32)
            bend = jnp.sum(la, axis=0, keepdims=True)
            e = jnp.exp(bend - b)
            dec_ref[ci:ci + 1, :] = jnp.exp(bend)
            for hd in range(HEADS):
                k = pg_ref[rows, PG_K + hd * HDK:PG_K + (hd + 1) * HDK]
                v = pg_ref[rows, PG_V + hd * HDV:PG_V + (hd + 1) * HDV]
                kt = (k.astype(F32) * e[:, hd * HDK:(hd + 1) * HDK]).astype(BF16)
                u_ref[ci, hd] = lax.dot_general(v, kt, (((0,), (0,)), ((), ())), preferred_element_type=F32)
        for ci in range(cpb):
            for hd in range(HEADS):
                prev = st_ref[hd]
                sp_ref[ci, hd] = prev
                st = prev * dec_ref[ci:ci + 1, hd * HDK:(hd + 1) * HDK] + u_ref[ci, hd]
                st_ref[hd] = st
                u_ref[ci, hd] = st
        for ci, rows in enumerate(chunks):
            for hd in range(HEADS):
                vc = slice(hd * HDV, (hd + 1) * HDV)
                q = pg_ref[rows, PG_Q + hd * HDK:PG_Q + (hd + 1) * HDK]
                go = pg_ref[rows, PG_G + hd * HDV:PG_G + (hd + 1) * HDV].astype(F32)
                qs = (q.astype(F32) * Q_SCALE).astype(BF16)
                o = lax.dot_general(qs, u_ref[ci, hd].astype(BF16), (((1,), (1,)), ((), ())), preferred_element_type=F32)
                oh, _ = _rms(o)
                ya_ref[rows, vc] = (oh * gn_ref[:, vc] * (go * _sigmoid(go))).astype(ya_ref.dtype)

    return pl.pallas_call(
        body, grid=(s // ts,),
        in_specs=[_rows(ts, PG_W), _fixed((LANE, HEADS * HDK)), _fixed((1, HEADS * HDK)), _fixed((1, HEADS * HDV))],
        out_specs=[_rows(ts, HEADS * HDV), pl.BlockSpec((cpb, HEADS, HDV, HDK), lambda i: (i, 0, 0, 0))],
        out_shape=[jax.ShapeDtypeStruct((s, HEADS * HDV), BF16), jax.ShapeDtypeStruct((nc, HEADS, HDV, HDK), F32)],
        scratch_shapes=[pltpu.VMEM((HEADS, HDV, HDK), F32), pltpu.VMEM((ts, HEADS * HDK), F32),
                        pltpu.VMEM((max(cpb, 8), HEADS * HDK), F32), pltpu.VMEM((cpb, HEADS, HDV, HDK), F32)],
        compiler_params=_params(("arbitrary",)), name=name,
    )(pg, wfu, b_f, gnorm)


def _gla_bwd(pg, sp, dya, wfu, b_f, gnorm, name):
    s = pg.shape[0]
    ts = _tile(s, TS_GLA, CHUNK)
    cpb = ts // CHUNK
    nblk = s // ts

    def body(pg_ref, sp_ref, dya_ref, wfu_ref, bf_ref, gn_ref, dpg_ref, dwfu_ref, dbf_ref, dgn_ref,
             dst_ref, la_ref, sg_ref, df_ref, e_ref, ktf_ref, dec_ref, g_ref):
        @pl.when(pl.program_id(0) == 0)
        def _():
            dst_ref[...] = jnp.zeros_like(dst_ref)
            dwfu_ref[...] = jnp.zeros_like(dwfu_ref)
            dbf_ref[...] = jnp.zeros_like(dbf_ref)
            dgn_ref[...] = jnp.zeros_like(dgn_ref)

        flow = pg_ref[:, PG_F:PG_W]
        f = jnp.dot(flow, wfu_ref[...], preferred_element_type=F32) + bf_ref[...]
        la_ref[...] = _log_sigmoid(f) * (1.0 / GATE_TEMP)
        sg_ref[...] = _sigmoid(-f) * (1.0 / GATE_TEMP)
        tri = _tri(False)
        tri_strict = _tri(True)
        chunks = [slice(ci * CHUNK, (ci + 1) * CHUNK) for ci in range(cpb)]
        for ci, rows in enumerate(chunks):
            la = la_ref[rows, :]
            b = jnp.dot(tri, la, precision=HIGHEST, preferred_element_type=F32)
            bend = jnp.sum(la, axis=0, keepdims=True)
            e = jnp.exp(bend - b)
            e_ref[rows, :] = e
            dec = jnp.exp(bend)
            dec_ref[ci:ci + 1, :] = dec
            for hd in range(HEADS):
                kc = slice(hd * HDK, (hd + 1) * HDK)
                vc = slice(hd * HDV, (hd + 1) * HDV)
                q = pg_ref[rows, PG_Q + hd * HDK:PG_Q + (hd + 1) * HDK]
                k = pg_ref[rows, PG_K + hd * HDK:PG_K + (hd + 1) * HDK]
                v = pg_ref[rows, PG_V + hd * HDV:PG_V + (hd + 1) * HDV]
                go = pg_ref[rows, PG_G + hd * HDV:PG_G + (hd + 1) * HDV].astype(F32)
                ktf = k.astype(F32) * e[:, kc]
                ktf_ref[rows, kc] = ktf
                st = sp_ref[ci, hd] * dec[:, kc] + lax.dot_general(v, ktf.astype(BF16), (((0,), (0,)), ((), ())), preferred_element_type=F32)
                st_b = st.astype(BF16)
                qs = (q.astype(F32) * Q_SCALE).astype(BF16)
                o = lax.dot_general(qs, st_b, (((1,), (1,)), ((), ())), preferred_element_type=F32)
                oh, r = _rms(o)
                gh = gn_ref[:, vc]
                sig = _sigmoid(go)
                dy = dya_ref[rows, vc].astype(F32)
                don = dy * (go * sig)
                dgn_ref[:, vc] += jnp.sum(don * oh, axis=0, keepdims=True)
                dong = don * gh
                do = (r * (dong - oh * jnp.mean(dong * oh, axis=-1, keepdims=True))).astype(BF16)
                g_ref[ci, hd] = lax.dot_general(do, qs, (((0,), (0,)), ((), ())), preferred_element_type=F32)
                dq = jnp.dot(do, st_b, preferred_element_type=F32) * Q_SCALE
                dpg_ref[rows, PG_Q + hd * HDK:PG_Q + (hd + 1) * HDK] = dq.astype(dpg_ref.dtype)
                dgo = dy * (oh * gh) * (sig * (1.0 + go * (1.0 - sig)))
                dpg_ref[rows, PG_G + hd * HDV:PG_G + (hd + 1) * HDV] = dgo.astype(dpg_ref.dtype)
        for ci in reversed(range(cpb)):
            for hd in range(HEADS):
                dst = dst_ref[hd] + g_ref[ci, hd]
                g_ref[ci, hd] = dst
                dst_ref[hd] = dst * dec_ref[ci:ci + 1, hd * HDK:(hd + 1) * HDK]
        for ci, rows in enumerate(chunks):
            for hd in range(HEADS):
                kc = slice(hd * HDK, (hd + 1) * HDK)
                v = pg_ref[rows, PG_V + hd * HDV:PG_V + (hd + 1) * HDV]
                ktf = ktf_ref[rows, kc]
                dst = g_ref[ci, hd]
                dst_b = dst.astype(BF16)
                dkt = jnp.dot(v, dst_b, preferred_element_type=F32)
                dv = lax.dot_general(ktf.astype(BF16), dst_b, (((1,), (1,)), ((), ())), preferred_element_type=F32)
                dd = jnp.sum(dst * sp_ref[ci, hd], axis=0, keepdims=True)
                dla = jnp.dot(tri_strict, dkt * ktf, precision=HIGHEST, preferred_element_type=F32) + dd * dec_ref[ci:ci + 1, kc]
                df_ref[rows, kc] = dla * sg_ref[rows, kc]
                dpg_ref[rows, PG_K + hd * HDK:PG_K + (hd + 1) * HDK] = (dkt * e_ref[rows, kc]).astype(dpg_ref.dtype)
                dpg_ref[rows, PG_V + hd * HDV:PG_V + (hd + 1) * HDV] = dv.astype(dpg_ref.dtype)
        df = df_ref[...]
        df_b = df.astype(BF16)
        dpg_ref[:, PG_F:PG_W] = lax.dot_general(df_b, wfu_ref[...], (((1,), (1,)), ((), ())), preferred_element_type=F32).astype(dpg_ref.dtype)
        dwfu_ref[...] += lax.dot_general(flow, df_b, (((0,), (0,)), ((), ())), preferred_element_type=F32)
        dbf_ref[...] += jnp.sum(df, axis=0, keepdims=True)

    rev = lambda i: (nblk - 1 - i, 0)
    return pl.pallas_call(
        body, grid=(nblk,),
        in_specs=[pl.BlockSpec((ts, PG_W), rev), pl.BlockSpec((cpb, HEADS, HDV, HDK), lambda i: (nblk - 1 - i, 0, 0, 0)),
                  pl.BlockSpec((ts, HEADS * HDV), rev), _fixed((LANE, HEADS * HDK)), _fixed((1, HEADS * HDK)), _fixed((1, HEADS * HDV))],
        out_specs=[pl.BlockSpec((ts, PG_W), rev), _fixed((LANE, HEADS * HDK)), _fixed((1, HEADS * HDK)), _fixed((1, HEADS * HDV))],
        out_shape=[jax.ShapeDtypeStruct((s, PG_W), BF16), jax.ShapeDtypeStruct((LANE, HEADS * HDK), F32),
                   jax.ShapeDtypeStruct((1, HEADS * HDK), F32), jax.ShapeDtypeStruct((1, HEADS * HDV), F32)],
        scratch_shapes=[pltpu.VMEM((HEADS, HDV, HDK), F32)] + [pltpu.VMEM((ts, HEADS * HDK), F32)] * 5
        + [pltpu.VMEM((max(cpb, 8), HEADS * HDK), F32), pltpu.VMEM((cpb, HEADS, HDV, HDK), F32)],
        compiler_params=_params(("arbitrary",)), name=name,
    )(pg, sp, dya, wfu, b_f, gnorm)


def _window_sums(ext, sign):
    n = ext.shape[0]
    sums = {1: ext}
    w = 1
    while w < POOL_WINDOWS[-1]:
        sums[2 * w] = sums[w] + pltpu.roll(sums[w], w if sign > 0 else n - w, 0)
        w *= 2
    return [sums[POOL_WINDOWS[g]][:, g * LANE:(g + 1) * LANE] for g in range(len(POOL_WINDOWS))]


def _pool_counts(row0, n):
    pos = (row0 + lax.broadcasted_iota(jnp.int32, (n, 1), 0) + 1).astype(F32)
    return [jnp.minimum(pos, float(w)) for w in POOL_WINDOWS]


def _pool_fwd(ppx, w_pool, pool_scale, name):
    s = ppx.shape[0]
    ts = _tile(s, TS_POOL, POOL_HALO)
    hb = ts // POOL_HALO
    pw = len(POOL_WINDOWS) * LANE

    def body(p_ref, halo_ref, w_ref, sc_ref, y_ref, ext_ref):
        i = pl.program_id(0)
        p = p_ref[...].astype(F32)
        ext_ref[0:POOL_HALO, :] = jnp.where(i > 0, halo_ref[...].astype(F32), 0.0)
        ext_ref[POOL_HALO:, :] = p
        sums = _window_sums(ext_ref[...], +1)
        cnt = _pool_counts(i * ts, ts)
        for g in range(len(POOL_WINDOWS)):
            cols = slice(g * LANE, (g + 1) * LANE)
            mixed = sums[g][POOL_HALO:, :] / cnt[g] - p[:, cols]
            y = jnp.dot(mixed.astype(BF16), w_ref[g], preferred_element_type=F32)
            y_ref[:, cols] = (y * sc_ref[:, cols]).astype(y_ref.dtype)

    return pl.pallas_call(
        body, grid=(s // ts,),
        in_specs=[pl.BlockSpec((ts, pw), lambda i: (i, 0)), pl.BlockSpec((POOL_HALO, pw), lambda i: (jnp.maximum(i * hb - 1, 0), 0)),
                  _fixed((len(POOL_WINDOWS), LANE, LANE)), _fixed((1, pw))],
        out_specs=_rows(ts, pw), out_shape=jax.ShapeDtypeStruct((s, pw), BF16),
        scratch_shapes=[pltpu.VMEM((ts + POOL_HALO, pw), F32)],
        compiler_params=_params(("parallel",)), name=name,
    )(ppx, ppx, w_pool, pool_scale)


def _pool_bwd(dyb, ppx, w_pool, pool_scale, name):
    s = ppx.shape[0]
    ts = _tile(s, TS_POOL, POOL_HALO)
    hb = ts // POOL_HALO
    nblk = s // ts
    last_halo = s // POOL_HALO - 1
    ng = len(POOL_WINDOWS)
    pw = ng * LANE

    def body(p_ref, halo_ref, dy_ref, dyn_ref, w_ref, sc_ref, dp_ref, dw_ref, dsc_ref, ext_ref, dext_ref, dm_ref):
        i = pl.program_id(0)

        @pl.when(i == 0)
        def _():
            dw_ref[...] = jnp.zeros_like(dw_ref)
            dsc_ref[...] = jnp.zeros_like(dsc_ref)

        p = p_ref[...].astype(F32)
        ext_ref[0:POOL_HALO, :] = jnp.where(i > 0, halo_ref[...].astype(F32), 0.0)
        ext_ref[POOL_HALO:, :] = p
        sums = _window_sums(ext_ref[...], +1)
        cnt = _pool_counts(i * ts, ts + POOL_HALO)
        sc = sc_ref[...]
        dy = dy_ref[...].astype(F32)
        dyn = jnp.where(i < nblk - 1, dyn_ref[...].astype(F32), 0.0)
        for g in range(ng):
            cols = slice(g * LANE, (g + 1) * LANE)
            wg = w_ref[g]
            mixed = (sums[g][POOL_HALO:, :] / cnt[g][0:ts] - p[:, cols]).astype(BF16)
            ypre = jnp.dot(mixed, wg, preferred_element_type=F32)
            dsc_ref[:, cols] += jnp.sum(dy[:, cols] * ypre, axis=0, keepdims=True)
            dyp = (dy[:, cols] * sc[:, cols]).astype(BF16)
            dypn = (dyn[:, cols] * sc[:, cols]).astype(BF16)
            dw_ref[g] += lax.dot_general(mixed, dyp, (((0,), (0,)), ((), ())), preferred_element_type=F32)
            dm = lax.dot_general(dyp, wg, (((1,), (1,)), ((), ())), preferred_element_type=F32)
            dmn = lax.dot_general(dypn, wg, (((1,), (1,)), ((), ())), preferred_element_type=F32)
            dext_ref[0:ts, cols] = dm / cnt[g][0:ts]
            dext_ref[ts:, cols] = dmn / cnt[g][ts:]
            dm_ref[:, cols] = dm
        lead = _window_sums(dext_ref[...], -1)
        for g in range(ng):
            cols = slice(g * LANE, (g + 1) * LANE)
            dp_ref[:, cols] = (lead[g][0:ts, :] - dm_ref[:, cols]).astype(dp_ref.dtype)

    return pl.pallas_call(
        body, grid=(nblk,),
        in_specs=[pl.BlockSpec((ts, pw), lambda i: (i, 0)), pl.BlockSpec((POOL_HALO, pw), lambda i: (jnp.maximum(i * hb - 1, 0), 0)),
                  pl.BlockSpec((ts, pw), lambda i: (i, 0)), pl.BlockSpec((POOL_HALO, pw), lambda i: (jnp.minimum((i + 1) * hb, last_halo), 0)),
                  _fixed((ng, LANE, LANE)), _fixed((1, pw))],
        out_specs=[_rows(ts, pw), _fixed((ng, LANE, LANE)), _fixed((1, pw))],
        out_shape=[jax.ShapeDtypeStruct((s, pw), BF16), jax.ShapeDtypeStruct((ng, LANE, LANE), F32), jax.ShapeDtypeStruct((1, pw), F32)],
        scratch_shapes=[pltpu.VMEM((ts + POOL_HALO, pw), F32), pltpu.VMEM((ts + POOL_HALO, pw), F32), pltpu.VMEM((ts, pw), F32)],
        compiler_params=_params(("arbitrary",)), name=name,
    )(ppx, ppx, dyb, dyb, w_pool, pool_scale)


def _xattn_fwd(ppx, kv, name):
    s = ppx.shape[0]
    m = kv.shape[0]
    ts = _tile(s, TS_XA, 8)
    xw = XA_HEADS * XA_HD

    def body(q_ref, kv_ref, o_ref):
        for hd in range(XA_HEADS):
            cols = slice(hd * XA_HD, (hd + 1) * XA_HD)
            k = kv_ref[:, hd * XA_HD:(hd + 1) * XA_HD]
            v = kv_ref[:, xw + hd * XA_HD:xw + (hd + 1) * XA_HD]
            sc = lax.dot_general(q_ref[:, cols], k, (((1,), (1,)), ((), ())), preferred_element_type=F32) * XA_SCALE
            ex = jnp.exp(sc - jnp.max(sc, axis=-1, keepdims=True))
            pr = ex / jnp.sum(ex, axis=-1, keepdims=True)
            o_ref[:, cols] = jnp.dot(pr.astype(BF16), v, preferred_element_type=F32).astype(o_ref.dtype)

    return pl.pallas_call(
        body, grid=(s // ts,), in_specs=[pl.BlockSpec((ts, xw), lambda i: (i, 1)), _fixed((m, 2 * xw))],
        out_specs=_rows(ts, xw), out_shape=jax.ShapeDtypeStruct((s, xw), BF16),
        compiler_params=_params(("parallel",)), name=name,
    )(ppx, kv)


def _xattn_bwd(dxc, ppx, kv, name):
    s = ppx.shape[0]
    m = kv.shape[0]
    ts = _tile(s, TS_XA, 8)
    xw = XA_HEADS * XA_HD

    def body(do_ref, q_ref, kv_ref, dq_ref, dkv_ref):
        @pl.when(pl.program_id(0) == 0)
        def _():
            dkv_ref[...] = jnp.zeros_like(dkv_ref)

        for hd in range(XA_HEADS):
            cols = slice(hd * XA_HD, (hd + 1) * XA_HD)
            vcols = slice(xw + hd * XA_HD, xw + (hd + 1) * XA_HD)
            q = q_ref[:, cols]
            k = kv_ref[:, cols]
            v = kv_ref[:, vcols]
            do = do_ref[:, cols]
            sc = lax.dot_general(q, k, (((1,), (1,)), ((), ())), preferred_element_type=F32) * XA_SCALE
            ex = jnp.exp(sc - jnp.max(sc, axis=-1, keepdims=True))
            pr = ex / jnp.sum(ex, axis=-1, keepdims=True)
            dpr = lax.dot_general(do, v, (((1,), (1,)), ((), ())), preferred_element_type=F32)
            dsc = (pr * (dpr - jnp.sum(dpr * pr, axis=-1, keepdims=True)) * XA_SCALE).astype(BF16)
            dq_ref[:, cols] = jnp.dot(dsc, k, preferred_element_type=F32).astype(dq_ref.dtype)
            dkv_ref[:, cols] += lax.dot_general(dsc, q, (((0,), (0,)), ((), ())), preferred_element_type=F32)
            dkv_ref[:, vcols] += lax.dot_general(pr.astype(BF16), do, (((0,), (0,)), ((), ())), preferred_element_type=F32)

    return pl.pallas_call(
        body, grid=(s // ts,), in_specs=[_rows(ts, xw), pl.BlockSpec((ts, xw), lambda i: (i, 1)), _fixed((m, 2 * xw))],
        out_specs=[_rows(ts, xw), _fixed((m, 2 * xw))],
        out_shape=[jax.ShapeDtypeStruct((s, xw), BF16), jax.ShapeDtypeStruct((m, 2 * xw), F32)],
        compiler_params=_params(("arbitrary",)), name=name,
    )(dxc, ppx, kv)


def _merge_fwd(pgt, ya, yb, yc, name):
    s = pgt.shape[0]
    ts = _tile(s, TS_ROW, 8)

    def body(gt_ref, ya_ref, yb_ref, yc_ref, o_ref):
        acc = _sigmoid(gt_ref[:, 0:D].astype(F32)) * ya_ref[...].astype(F32)
        acc = acc + _sigmoid(gt_ref[:, D:2 * D].astype(F32)) * yb_ref[...].astype(F32)
        acc = acc + _sigmoid(gt_ref[:, 2 * D:3 * D].astype(F32)) * yc_ref[...].astype(F32)
        o_ref[...] = acc.astype(o_ref.dtype)

    return pl.pallas_call(
        body, grid=(s // ts,), in_specs=[_rows(ts, 3 * D)] + [_rows(ts, D)] * 3, out_specs=_rows(ts, D),
        out_shape=jax.ShapeDtypeStruct((s, D), BF16), compiler_params=_params(("parallel",)), name=name,
    )(pgt, ya, yb, yc)


def _merge_bwd(dmerged, pgt, ya, yb, yc, name):
    s = pgt.shape[0]
    ts = _tile(s, TS_ROW, 8)

    def body(dm_ref, gt_ref, ya_ref, yb_ref, yc_ref, dya_ref, dyb_ref, dyc_ref, dgt_ref):
        dm = dm_ref[...].astype(F32)
        for j, (y_ref, dy_ref) in enumerate(((ya_ref, dya_ref), (yb_ref, dyb_ref), (yc_ref, dyc_ref))):
            sig = _sigmoid(gt_ref[:, j * D:(j + 1) * D].astype(F32))
            dy_ref[...] = (dm * sig).astype(dy_ref.dtype)
            dgt_ref[:, j * D:(j + 1) * D] = (dm * y_ref[...].astype(F32) * sig * (1.0 - sig)).astype(dgt_ref.dtype)

    return pl.pallas_call(
        body, grid=(s // ts,), in_specs=[_rows(ts, D), _rows(ts, 3 * D)] + [_rows(ts, D)] * 3,
        out_specs=[_rows(ts, D)] * 3 + [_rows(ts, 3 * D)],
        out_shape=[jax.ShapeDtypeStruct((s, D), BF16)] * 3 + [jax.ShapeDtypeStruct((s, 3 * D), BF16)],
        compiler_params=_params(("parallel",)), name=name,
    )(dmerged, pgt, ya, yb, yc)


def _adam_math(w, g, m, v):
    mn = ADAM_B1 * m + (1.0 - ADAM_B1) * g
    vn = ADAM_B2 * v + (1.0 - ADAM_B2) * (g * g)
    m_hat = mn / (1.0 - ADAM_B1 ** ADAM_STEP)
    v_hat = vn / (1.0 - ADAM_B2 ** ADAM_STEP)
    return -ADAM_LR * (m_hat / (jnp.sqrt(v_hat) + ADAM_EPS) + ADAM_WD * w), mn, vn


def _adamw(w, g, m, v, name):
    r, c = w.shape[-2:]
    tr, tc = _block_of(r, c)

    def spec(a):
        if a.ndim == 2:
            return pl.BlockSpec((tr, tc), lambda i, j: (i, j))
        return pl.BlockSpec((None, tr, tc), lambda i, j: (0, i, j))

    def body(w_ref, g_ref, m_ref, v_ref, d_ref, mo_ref, vo_ref):
        d_ref[...], mo_ref[...], vo_ref[...] = _adam_math(w_ref[...], g_ref[...], m_ref[...], v_ref[...])

    return pl.pallas_call(
        body, grid=(r // tr, c // tc), in_specs=[spec(a) for a in (w, g, m, v)], out_specs=[spec(w)] * 3,
        out_shape=[jax.ShapeDtypeStruct(w.shape, F32)] * 3, compiler_params=_params(("parallel", "parallel")), name=name,
    )(w, g, m, v)


ANY = pl.BlockSpec(memory_space=pl.ANY)


def _place():
    x, y, c = lax.axis_index("x"), lax.axis_index("y"), lax.axis_index("c")
    chips = [(1 - x, y), (x, 1 - y), (1 - x, 1 - y)]
    return x, y, c, chips


def _half(c, rows):
    h = rows // 2
    return pl.ds(pl.multiple_of(c * h, 8), h)


def _by_cols(rows):
    return rows % 32 != 0 and rows != 16


def _half_of(ref, lead, c):
    r, cols = ref.shape[-2:]
    if _by_cols(r):
        return ref.at[(*lead, slice(None), pl.ds(pl.multiple_of(c * (cols // 2), LANE), cols // 2))]
    return ref.at[(*lead, pl.ds(pl.multiple_of(c * (r // 2), 8), r // 2))]


def _half_shape(shape):
    r, cols = shape[-2:]
    return shape[:-2] + ((r, cols // 2) if _by_cols(r) else (r // 2, cols))


def _block_of(r, cols, cap=256):
    if r % 16 == 0:
        return _tile(r, cap, 16), cols
    return r, _tile(cols, cap)


def _place_shard(shard, chip_arr, out_dtype, name):
    _, r, cols = shard.shape
    tr, tc = _block_of(r, cols)

    def body(chip_ref, s_ref, o_ref):
        o_ref[...] = s_ref[...].astype(o_ref.dtype)

    return pl.pallas_call(
        body,
        grid_spec=pltpu.PrefetchScalarGridSpec(
            num_scalar_prefetch=1, grid=(r // tr, cols // tc),
            in_specs=[pl.BlockSpec((None, tr, tc), lambda i, j, chip_ref: (0, i, j))],
            out_specs=pl.BlockSpec((None, tr, tc), lambda i, j, chip_ref: (chip_ref[0], i, j))),
        out_shape=jax.ShapeDtypeStruct((4, r, cols), out_dtype),
        compiler_params=_params(("parallel", "parallel")), name=name,
    )(chip_arr, shard)


def _gather_shards(bufs, name):
    n = len(bufs)

    def body(*refs):
        outs = refs[n:2 * n]
        send_ici, recv_ici, send_d2d, recv_d2d = refs[2 * n:]
        x, y, c, chips = _place()
        me = 2 * x + y
        sibling = (x, y, 1 - c)

        def ici(w, p, chip_of_block, to):
            rows = _half(c, outs[w].shape[1])
            block = outs[w].at[chip_of_block, rows]
            return pltpu.make_async_remote_copy(
                src_ref=block, dst_ref=block, send_sem=send_ici.at[w, p], recv_sem=recv_ici.at[w, p], device_id=to, device_id_type=MESH)

        def d2d(w, p, chip_of_block, half_of):
            rows = _half(half_of, outs[w].shape[1])
            block = outs[w].at[chip_of_block, rows]
            return pltpu.make_async_remote_copy(
                src_ref=block, dst_ref=block, send_sem=send_d2d.at[w, p], recv_sem=recv_d2d.at[w, p], device_id=sibling, device_id_type=MESH)

        sends = [ici(w, p, me, (*chip, c)) for p, chip in enumerate(chips) for w in range(n)]
        for cp in sends:
            cp.start()
        passed = []
        for p, (px, py) in enumerate(chips):
            for w in range(n):
                ici(w, p, 2 * px + py, (px, py, c)).wait_recv()
                fwd = d2d(w, p, 2 * px + py, c)
                fwd.start()
                passed.append(fwd)
        for p, (px, py) in enumerate(chips):
            for w in range(n):
                d2d(w, p, 2 * px + py, 1 - c).wait_recv()
        for cp in sends + passed:
            cp.wait_send()

    return pl.pallas_call(
        body, in_specs=[ANY] * n, out_specs=[ANY] * n,
        out_shape=[jax.ShapeDtypeStruct(a.shape, a.dtype) for a in bufs],
        input_output_aliases={w: w for w in range(n)},
        scratch_shapes=[pltpu.SemaphoreType.DMA((n, 3))] * 4,
        compiler_params=pltpu.CompilerParams(has_side_effects=True), name=name,
    )(*bufs)


HBM = pl.BlockSpec(memory_space=pltpu.HBM)
SEM = pl.BlockSpec(memory_space=pltpu.SEMAPHORE)
EFFECT = pltpu.SideEffectType.DATAFLOW_SIDE_EFFECTING


def _in_hbm(arrays):
    return [pltpu.with_memory_space_constraint(a, pltpu.HBM) for a in arrays]


def _gather_start(bufs, after, name):
    n, na = len(bufs), len(after)

    def body(*refs):
        send_sem, recv_sem = refs[n + na], refs[n + na + 1]
        outs = refs[n + na + 2:2 * n + na + 2]
        token = refs[2 * n + na + 2]
        x, y, c, chips = _place()
        me = 2 * x + y
        for p, chip in enumerate(chips):
            for w in range(n):
                block = _half_of(outs[w], (me,), c)
                pltpu.make_async_remote_copy(
                    src_ref=block, dst_ref=block, send_sem=send_sem, recv_sem=recv_sem,
                    device_id=(*chip, c), device_id_type=MESH).start()
        token[...] = jnp.zeros_like(token)

    out = pl.pallas_call(
        body, name=name, in_specs=[HBM] * n + [ANY] * na,
        out_specs=[SEM, SEM] + [HBM] * n + [pl.BlockSpec(memory_space=pltpu.VMEM)],
        out_shape=[pltpu.SemaphoreType.DMA(()), pltpu.SemaphoreType.DMA(())]
        + [pltpu.HBM(a.shape, a.dtype) for a in bufs] + [jax.ShapeDtypeStruct((8, LANE), F32)],
        input_output_aliases={w: w + 2 for w in range(n)},
        compiler_params=pltpu.CompilerParams(has_side_effects=EFFECT),
    )(*_in_hbm(bufs), *after)
    return out[0], out[1], list(out[2:2 + n]), out[2 + n]


def _gather_pass(bufs, send_sem, recv_sem, after, name):
    n, na = len(bufs), len(after)

    def body(*refs):
        send1, recv1 = refs[n], refs[n + 1]
        send2, recv2 = refs[n + 2 + na], refs[n + 3 + na]
        outs = refs[n + 4 + na:2 * n + 4 + na]
        x, y, c, chips = _place()
        me = 2 * x + y
        arrivals = [(w, px, py) for px, py in chips for w in range(n)]
        for w, px, py in arrivals:
            first = pltpu.make_async_remote_copy(
                src_ref=_half_of(outs[w], (me,), c), dst_ref=_half_of(outs[w], (2 * px + py,), c), send_sem=send1, recv_sem=recv1,
                device_id=(px, py, c), device_id_type=MESH)
            first.wait_send()
            first.wait_recv()
        for w, px, py in arrivals:
            arrived = _half_of(outs[w], (2 * px + py,), c)
            pltpu.make_async_remote_copy(
                src_ref=arrived, dst_ref=arrived, send_sem=send2, recv_sem=recv2,
                device_id=(x, y, 1 - c), device_id_type=MESH).start()

    out = pl.pallas_call(
        body, name=name, in_specs=[HBM] * n + [SEM, SEM] + [ANY] * na,
        out_specs=[SEM, SEM] + [HBM] * n,
        out_shape=[pltpu.SemaphoreType.DMA(()), pltpu.SemaphoreType.DMA(())] + [pltpu.HBM(a.shape, a.dtype) for a in bufs],
        input_output_aliases={w: w + 2 for w in range(n)},
        compiler_params=pltpu.CompilerParams(has_side_effects=EFFECT),
    )(*bufs, send_sem, recv_sem, *after)
    return out[0], out[1], list(out[2:])


def _gather_finish(bufs, send_sem, recv_sem, after, name):
    n, na = len(bufs), len(after)

    def body(*refs):
        send2, recv2 = refs[n], refs[n + 1]
        outs = refs[n + 2 + na:2 * n + 2 + na]
        x, y, c, chips = _place()
        for p, (px, py) in enumerate(chips):
            for w in range(n):
                passed = pltpu.make_async_remote_copy(
                    src_ref=_half_of(outs[w], (2 * px + py,), c), dst_ref=_half_of(outs[w], (2 * px + py,), 1 - c),
                    send_sem=send2, recv_sem=recv2, device_id=(x, y, 1 - c), device_id_type=MESH)
                passed.wait_send()
                passed.wait_recv()

    out = pl.pallas_call(
        body, name=name, in_specs=[HBM] * n + [SEM, SEM] + [ANY] * na, out_specs=[HBM] * n,
        out_shape=[pltpu.HBM(a.shape, a.dtype) for a in bufs],
        input_output_aliases={w: w for w in range(n)},
        compiler_params=pltpu.CompilerParams(has_side_effects=EFFECT),
    )(*bufs, send_sem, recv_sem, *after)
    return list(out)


def _pair_exchange(grads, name):
    n = len(grads)

    def body(*refs):
        ins, outs = refs[:n], refs[n:2 * n]
        send_sem, recv_sem = refs[2 * n:]
        x, y, c, _ = _place()
        copies = []
        for w in range(n):
            copies.append(pltpu.make_async_remote_copy(
                src_ref=_half_of(ins[w], (slice(None),), 1 - c), dst_ref=outs[w], send_sem=send_sem.at[w], recv_sem=recv_sem.at[w],
                device_id=(x, y, 1 - c), device_id_type=MESH))
        for cp in copies:
            cp.start()
        for cp in copies:
            cp.wait()

    return pl.pallas_call(
        body, in_specs=[ANY] * n, out_specs=[ANY] * n,
        out_shape=[jax.ShapeDtypeStruct(_half_shape(a.shape), a.dtype) for a in grads],
        scratch_shapes=[pltpu.SemaphoreType.DMA((n,))] * 2,
        compiler_params=pltpu.CompilerParams(has_side_effects=True), name=name,
    )(*grads)


def _pair_sum(g, got, c_arr, name):
    _, r, cols = g.shape
    hr, hc = _half_shape((r, cols))
    tr, tc = _block_of(hr, hc)
    nbr, nbc = hr // tr, hc // tc
    by_cols = _by_cols(r)

    def body(c_ref, g_ref, got_ref, o_ref):
        o_ref[...] = (g_ref[...].astype(F32) + got_ref[...].astype(F32)).astype(o_ref.dtype)

    def mine(j, i, k, c_ref):
        return (j, i, c_ref[0] * nbc + k) if by_cols else (j, c_ref[0] * nbr + i, k)

    return pl.pallas_call(
        body,
        grid_spec=pltpu.PrefetchScalarGridSpec(
            num_scalar_prefetch=1, grid=(4, nbr, nbc),
            in_specs=[pl.BlockSpec((None, tr, tc), mine),
                      pl.BlockSpec((None, tr, tc), lambda j, i, k, c_ref: (j, i, k))],
            out_specs=pl.BlockSpec((None, tr, tc), lambda j, i, k, c_ref: (j, i, k))),
        out_shape=jax.ShapeDtypeStruct((4, hr, hc), BF16),
        compiler_params=_params(("parallel", "parallel", "parallel")), name=name,
    )(c_arr, g, got)


def _chip_exchange(parts, name):
    n = len(parts)

    def body(*refs):
        ins, outs = refs[:n], refs[n:2 * n]
        send_sem, recv_sem = refs[2 * n:]
        x, y, c, chips = _place()
        copies = []
        for p, (px, py) in enumerate(chips):
            for w in range(n):
                copies.append(pltpu.make_async_remote_copy(
                    src_ref=ins[w].at[2 * px + py], dst_ref=outs[w].at[p], send_sem=send_sem.at[w, p], recv_sem=recv_sem.at[w, p],
                    device_id=(px, py, c), device_id_type=MESH))
        for cp in copies:
            cp.start()
        for cp in copies:
            cp.wait()

    return pl.pallas_call(
        body, in_specs=[ANY] * n, out_specs=[ANY] * n,
        out_shape=[jax.ShapeDtypeStruct((3,) + a.shape[1:], a.dtype) for a in parts],
        scratch_shapes=[pltpu.SemaphoreType.DMA((n, 3))] * 2,
        compiler_params=pltpu.CompilerParams(has_side_effects=True), name=name,
    )(*parts)


def _chip_exchange_start(parts, after, name):
    n, na = len(parts), len(after)
    lands = [lax.empty((3,) + a.shape[1:], a.dtype) for a in parts]

    def body(*refs):
        send_sem, recv_sem = refs[2 * n + na], refs[2 * n + na + 1]
        srcs = refs[2 * n + na + 2:3 * n + na + 2]
        dsts = refs[3 * n + na + 2:4 * n + na + 2]
        token = refs[4 * n + na + 2]
        x, y, c, chips = _place()
        for p, (px, py) in enumerate(chips):
            for w in range(n):
                pltpu.make_async_remote_copy(
                    src_ref=srcs[w].at[2 * px + py], dst_ref=dsts[w].at[p], send_sem=send_sem, recv_sem=recv_sem,
                    device_id=(px, py, c), device_id_type=MESH).start()
        token[...] = jnp.zeros_like(token)

    out = pl.pallas_call(
        body, name=name, in_specs=[HBM] * (2 * n) + [ANY] * na,
        out_specs=[SEM, SEM] + [HBM] * (2 * n) + [pl.BlockSpec(memory_space=pltpu.VMEM)],
        out_shape=[pltpu.SemaphoreType.DMA(()), pltpu.SemaphoreType.DMA(())]
        + [pltpu.HBM(a.shape, a.dtype) for a in parts + lands] + [jax.ShapeDtypeStruct((8, LANE), F32)],
        input_output_aliases={w: w + 2 for w in range(2 * n)},
        compiler_params=pltpu.CompilerParams(has_side_effects=EFFECT),
    )(*_in_hbm(parts), *_in_hbm(lands), *after)
    return out[0], out[1], list(out[2:2 + n]), list(out[2 + n:2 + 2 * n]), out[2 + 2 * n]


def _chip_exchange_finish(parts, lands, send_sem, recv_sem, after, name):
    n, na = len(parts), len(after)

    def body(*refs):
        send, recv = refs[2 * n], refs[2 * n + 1]
        srcs = refs[2 * n + 2 + na:3 * n + 2 + na]
        dsts = refs[3 * n + 2 + na:4 * n + 2 + na]
        x, y, c, chips = _place()
        for p, (px, py) in enumerate(chips):
            for w in range(n):
                copy = pltpu.make_async_remote_copy(
                    src_ref=srcs[w].at[2 * px + py], dst_ref=dsts[w].at[p], send_sem=send, recv_sem=recv,
                    device_id=(px, py, c), device_id_type=MESH)
                copy.wait_send()
                copy.wait_recv()

    out = pl.pallas_call(
        body, name=name, in_specs=[HBM] * (2 * n) + [SEM, SEM] + [ANY] * na, out_specs=[HBM] * (2 * n),
        out_shape=[pltpu.HBM(a.shape, a.dtype) for a in parts + lands],
        input_output_aliases={w: w for w in range(2 * n)},
        compiler_params=pltpu.CompilerParams(has_side_effects=EFFECT),
    )(*parts, *lands, send_sem, recv_sem, *after)
    return list(out[:n]), list(out[n:])


def _chip_sum(part, got, place_arr, name):
    _, hr, hc = part.shape
    by_cols = _by_cols(hr)
    tr, tc = _block_of(hr, hc)
    nbr, nbc = hr // tr, hc // tc

    def body(place_ref, p_ref, got_ref, o_ref):
        acc = p_ref[...].astype(F32)
        for p in range(3):
            acc = acc + got_ref[p].astype(F32)
        o_ref[...] = acc

    def mine(i, k, place_ref):
        return (i, place_ref[1] * nbc + k) if by_cols else (place_ref[1] * nbr + i, k)

    return pl.pallas_call(
        body,
        grid_spec=pltpu.PrefetchScalarGridSpec(
            num_scalar_prefetch=1, grid=(nbr, nbc),
            in_specs=[pl.BlockSpec((None, tr, tc), lambda i, k, place_ref: (place_ref[0], i, k)),
                      pl.BlockSpec((3, tr, tc), lambda i, k, place_ref: (0, i, k))],
            out_specs=pl.BlockSpec((tr, tc), mine)),
        out_shape=jax.ShapeDtypeStruct((hr, 2 * hc) if by_cols else (2 * hr, hc), F32),
        compiler_params=_params(("parallel", "parallel")), name=name,
    )(place_arr, part, got)


def _pair_join(bufs, name):
    n = len(bufs)

    def body(*refs):
        outs = refs[n:2 * n]
        send_sem, recv_sem = refs[2 * n:]
        x, y, c, _ = _place()
        copies = []
        for w in range(n):
            block = _half_of(outs[w], (), c)
            copies.append(pltpu.make_async_remote_copy(
                src_ref=block, dst_ref=block, send_sem=send_sem.at[w], recv_sem=recv_sem.at[w],
                device_id=(x, y, 1 - c), device_id_type=MESH))
        for cp in copies:
            cp.start()
        for w, cp in enumerate(copies):
            cp.wait_send()
            block = _half_of(outs[w], (), 1 - c)
            pltpu.make_async_remote_copy(
                src_ref=block, dst_ref=block, send_sem=send_sem.at[w], recv_sem=recv_sem.at[w],
                device_id=(x, y, 1 - c), device_id_type=MESH).wait_recv()

    return pl.pallas_call(
        body, in_specs=[ANY] * n, out_specs=[ANY] * n,
        out_shape=[jax.ShapeDtypeStruct(a.shape, a.dtype) for a in bufs],
        input_output_aliases={w: w for w in range(n)},
        scratch_shapes=[pltpu.SemaphoreType.DMA((n,))] * 2,
        compiler_params=pltpu.CompilerParams(has_side_effects=True), name=name,
    )(*bufs)


SMALL = ("ffn1_pre_g", "ffn1_post_g", "mix_pre_g", "gla_norm_g", "mem_norm_g", "mix_post_g", "ffn2_pre_g", "ffn2_post_g", "final_g",
         "b_f", "pool_scale", "w_pool", "w_fu")
N_GAINS = 9
SMALL_PACKS = ((16, D), (24, 512), (4 * LANE, LANE))
W_FU_ROW = 8


def _all_sum_small(gs, name):
    ins = [gs[n] for n in SMALL[:N_GAINS]] + [gs["b_f"], gs["pool_scale"], gs["w_fu_pad"], gs["w_pool"].reshape(4 * LANE, LANE)]

    def body(*refs):
        gain_refs = refs[:N_GAINS]
        bf_ref, ps_ref, wfu_ref, wp_ref = refs[N_GAINS:N_GAINS + 4]
        outs = refs[N_GAINS + 4:N_GAINS + 7]
        mine_a, mine_b, all_a, all_b, all_c, send_sems, recv_sems = refs[N_GAINS + 7:]
        mine_a[...] = jnp.zeros_like(mine_a)
        for i, ref in enumerate(gain_refs):
            mine_a[i:i + 1, :] = ref[...]
        mine_b[...] = jnp.zeros_like(mine_b)
        mine_b[0:1, :] = bf_ref[...]
        mine_b[1:2, :] = ps_ref[...]
        mine_b[W_FU_ROW:W_FU_ROW + GATE_RANK, :] = wfu_ref[0:GATE_RANK, :]
        packs = ((mine_a, all_a), (mine_b, all_b), (wp_ref, all_c))
        x, y, c, chips = _place()
        me, sibling = (x, y, c), (x, y, 1 - c)

        def copy(t, k, block, to, own=False):
            px, py, pc = block
            slot = packs[t][1].at[4 * px + 2 * py + pc]
            return pltpu.make_async_remote_copy(
                src_ref=packs[t][0] if own else slot, dst_ref=slot,
                send_sem=send_sems.at[t, k], recv_sem=recv_sems.at[t, k], device_id=to, device_id_type=MESH)

        started = []
        for t, (mine, everyone) in enumerate(packs):
            everyone[4 * x + 2 * y + c] = mine[...]
            started.append(copy(t, 0, me, sibling, own=True))
            started += [copy(t, 1 + j, me, (*chip, c), own=True) for j, chip in enumerate(chips)]
        for cp in started:
            cp.start()
        passed = []
        for j, chip in enumerate(chips):
            for t in range(len(packs)):
                copy(t, 1 + j, (*chip, c), me).wait_recv()
                fwd = copy(t, 4 + j, (*chip, c), sibling)
                fwd.start()
                passed.append(fwd)
        for t in range(len(packs)):
            copy(t, 0, sibling, me).wait_recv()
            for j, chip in enumerate(chips):
                copy(t, 4 + j, (*chip, 1 - c), me).wait_recv()
        for cp in started + passed:
            cp.wait_send()
        for (_, everyone), o_ref in zip(packs, outs):
            acc = everyone[0]
            for k in range(1, 8):
                acc = acc + everyone[k]
            o_ref[...] = acc

    vmem = pl.BlockSpec(memory_space=pltpu.VMEM)
    return pl.pallas_call(
        body, in_specs=[vmem] * len(ins), out_specs=[vmem] * 3,
        out_shape=[jax.ShapeDtypeStruct(shape, F32) for shape in SMALL_PACKS],
        scratch_shapes=[pltpu.VMEM(SMALL_PACKS[0], F32), pltpu.VMEM(SMALL_PACKS[1], F32)]
        + [pltpu.VMEM((8,) + shape, F32) for shape in SMALL_PACKS]
        + [pltpu.SemaphoreType.DMA((3, 7)), pltpu.SemaphoreType.DMA((3, 7))],
        compiler_params=pltpu.CompilerParams(has_side_effects=True, vmem_limit_bytes=VMEM_LIMIT), name=name,
    )(*ins)


def _adamw_small(sums, params, chip_arr, name):
    flat = [a for n in SMALL for a in params[n]]

    def body(chip_ref, a_ref, b_ref, c_ref, *refs):
        ins, outs = refs[:len(flat)], refs[len(flat):]
        for i, n in enumerate(SMALL):
            w_ref, m_ref, v_ref = ins[3 * i:3 * i + 3]
            g_ref, d_ref, mo_ref, vo_ref = outs[4 * i:4 * i + 4]
            if n == "w_pool":
                pieces = [((0, k), c_ref[k * LANE:(k + 1) * LANE, :]) for k in range(4)]
            elif n == "w_fu":
                mine = pl.ds(pl.multiple_of(chip_ref[0] * LANE, LANE), LANE)
                pieces = [((0,), b_ref[W_FU_ROW:W_FU_ROW + GATE_RANK, mine])]
            elif n == "b_f":
                pieces = [((), b_ref[0:1, :])]
            elif n == "pool_scale":
                pieces = [((), b_ref[1:2, :])]
            else:
                pieces = [((), a_ref[i:i + 1, :])]
            for at, g in pieces:
                d, mn, vn = _adam_math(w_ref[at], g, m_ref[at], v_ref[at])
                g_ref[at] = g
                d_ref[at] = d
                mo_ref[at] = mn
                vo_ref[at] = vn

    def whole(shape):
        return pl.BlockSpec(shape, lambda i, chip_ref: (0,) * len(shape))

    out = pl.pallas_call(
        body,
        grid_spec=pltpu.PrefetchScalarGridSpec(
            num_scalar_prefetch=1, grid=(1,),
            in_specs=[whole(a.shape) for a in list(sums) + flat],
            out_specs=[whole(params[n][0].shape) for n in SMALL for _ in range(4)]),
        out_shape=[jax.ShapeDtypeStruct(params[n][0].shape, F32) for n in SMALL for _ in range(4)],
        compiler_params=_params(("arbitrary",)), name=name,
    )(chip_arr, *sums, *flat)
    return {n: tuple(out[4 * i:4 * i + 4]) for i, n in enumerate(SMALL)}


def _ffn_fwd(x_norm, w_in, w_out, tag, between=None):
    ab = _mm(x_norm, w_in, out_dtype=BF16, tn=1408, name=tag + "_in")
    u = _swiglu_fwd(ab, name=tag + "_swiglu")
    behind = (between(u) if between is not None else None) or ()
    f = _mm(u, w_out, tk=DFF, after=behind, name=tag + "_out")
    return ab, u, f


def _ffn_bwd(dz, x_norm, ab, u, w_in, w_out, tag, emit, after=()):
    dw_out = _mm(u, dz, ta=True, out_dtype=BF16, tm=1408, tk=2048, after=after, name=tag + "_out_dw")
    behind = emit(tag + "_w_out", dw_out)
    du = _mm(dz, w_out, tb=True, out_dtype=BF16, tn=1408, after=behind, name=tag + "_out_dx")
    dab = _swiglu_bwd(ab, du, name=tag + "_swiglu_bwd")
    dw_in = _mm(x_norm, dab, ta=True, out_dtype=BF16, tm=512, tk=4096, shards=4, name=tag + "_in_dw")
    behind = emit(tag + "_w_in", dw_in)
    return _mm(dab, w_in, tb=True, tk=2816, after=behind, name=tag + "_in_dx")


def _local_step(x, mem, target, small, gather, emit):
    big, behind = gather("ffn1", ())
    h1 = _norm_fwd(x, small["ffn1_pre_g"], BF16, name="ffn1_pre", after=behind)
    ab1, u1, f1 = _ffn_fwd(h1, big["ffn1_w_in"], big["ffn1_w_out"], "ffn1", between=lambda u: gather("mix_pass", (u,)))
    more, behind = gather("mix", (f1,))
    big.update(more)
    small = dict(small, w_fu_pad=big["w_fu_pad"])
    x1, h = _resid_norm_fwd(x, f1, small["ffn1_post_g"], 0.5, small["mix_pre_g"], name="ffn1_post")
    pg = _mm(h, big["w_gla_t"], tb=True, out_dtype=BF16, tn=PG_W, after=behind, name="mix_in_gla")
    ppx = _mm(h, big["w_px_t"], tb=True, out_dtype=BF16, name="mix_in_px")
    pgt = _mm(h, big["w_gates_t"], tb=True, out_dtype=BF16, tn=1536, name="mix_in_gates")
    mem_n = _norm_fwd(mem, small["mem_norm_g"], BF16, name="mem_norm")
    kv = _mm(mem_n, big["w_mem_kv"], out_dtype=BF16, name="mem_kv")
    ya_in, sp = _gla_fwd(pg, small["w_fu_pad"], small["b_f"], small["gla_norm_g"], name="gla_fwd")
    yb_in = _pool_fwd(ppx, small["w_pool_b"], small["pool_scale"], name="pool_fwd")
    xc = _xattn_fwd(ppx, kv, name="xattn_fwd")
    ya = _mm(ya_in, big["w_up_gla"], out_dtype=BF16, name="up_gla")
    yb = _mm(yb_in, big["w_up_pool"], out_dtype=BF16, name="up_pool")
    yc = _mm(xc, big["w_up_xattn"], out_dtype=BF16, name="up_xattn")
    merged = _merge_fwd(pgt, ya, yb, yc, name="merge_fwd")
    gather("ffn2_pass", (merged,))
    ymix = _mm(merged, big["w_o"], name="mix_out")
    more, _ = gather("ffn2", (ymix,))
    big.update(more)
    x2, h2 = _resid_norm_fwd(x1, ymix, small["mix_post_g"], 1.0, small["ffn2_pre_g"], name="mix_post")
    ab2, u2, f2 = _ffn_fwd(h2, big["ffn2_w_in"], big["ffn2_w_out"], "ffn2")
    x3, _ = _resid_norm_fwd(x2, f2, small["ffn2_post_g"], 0.5, None, name="ffn2_post")
    gs = {}
    dx3, gs["final_g"], loss = _loss_bwd(x3, small["final_g"], target, name="loss")
    dz2, gs["ffn2_post_g"] = _rms_bwd(f2, small["ffn2_post_g"], [dx3], None, 0.5, BF16, name="ffn2_post_bwd")
    dh2 = _ffn_bwd(dz2, h2, ab2, u2, big["ffn2_w_in"], big["ffn2_w_out"], "ffn2", emit)
    dx2, gs["ffn2_pre_g"] = _rms_bwd(x2, small["ffn2_pre_g"], [dh2], dx3, 1.0, F32, name="ffn2_pre_bwd")
    dy, gs["mix_post_g"] = _rms_bwd(ymix, small["mix_post_g"], [dx2], None, 1.0, BF16, name="mix_post_bwd")
    dmerged = _mm(dy, big["w_o"], tb=True, out_dtype=BF16, name="mix_out_dx")
    emit("w_o", _mm(merged, dy, ta=True, out_dtype=BF16, tm=512, tk=4096, name="mix_out_dw"))
    dya, dyb, dyc, dgt = _merge_bwd(dmerged, pgt, ya, yb, yc, name="merge_bwd")
    dya_in = _mm(dya, big["w_up_gla"], tb=True, out_dtype=BF16, name="up_gla_dx")
    emit("w_up_gla", _mm(ya_in, dya, ta=True, out_dtype=BF16, tm=512, tk=4096, name="up_gla_dw"))
    dyb_in = _mm(dyb, big["w_up_pool"], tb=True, out_dtype=BF16, name="up_pool_dx")
    emit("w_up_pool", _mm(yb_in, dyb, ta=True, out_dtype=BF16, tm=512, tk=4096, shards=4, name="up_pool_dw"))
    dxc = _mm(dyc, big["w_up_xattn"], tb=True, out_dtype=BF16, name="up_xattn_dx")
    emit("w_up_xattn", _mm(xc, dyc, ta=True, out_dtype=BF16, tm=512, tk=4096, shards=4, name="up_xattn_dw"))
    dpg, gs["w_fu_pad"], gs["b_f"], gs["gla_norm_g"] = _gla_bwd(pg, sp, dya_in, small["w_fu_pad"], small["b_f"], small["gla_norm_g"], name="gla_bwd")
    dp, gs["w_pool"], gs["pool_scale"] = _pool_bwd(dyb_in, ppx, small["w_pool_b"], small["pool_scale"], name="pool_bwd")
    dxq, dkv = _xattn_bwd(dxc, ppx, kv, name="xattn_bwd")
    dkv = dkv.astype(BF16)
    emit("w_mem_kv", _mm(mem_n, dkv, ta=True, out_dtype=BF16, name="mem_kv_dw"))
    dmem_n = _mm(dkv, big["w_mem_kv"], tb=True, name="mem_kv_dx")
    _, gs["mem_norm_g"] = _rms_bwd(mem, small["mem_norm_g"], [dmem_n], None, 1.0, BF16, name="mem_norm_bwd")
    emit("w_gla", _mm(dpg, h, ta=True, out_dtype=BF16, tm=640, tk=4096, name="mix_in_gla_dw"))
    emit("w_p", _mm(dp, h, ta=True, out_dtype=BF16, tm=512, tk=4096, name="mix_in_p_dw"))
    emit("w_xq", _mm(dxq, h, ta=True, out_dtype=BF16, tm=512, tk=4096, name="mix_in_xq_dw"))
    behind = emit("w_gates", _mm(dgt, h, ta=True, out_dtype=BF16, tm=512, tk=4096, name="mix_in_gates_dw"))
    dh_parts = [
        _mm(dpg, big["w_gla_t"], tk=PG_W, after=behind, name="mix_in_gla_dx"),
        _mm(dp, big["w_p_t"], name="mix_in_p_dx"),
        _mm(dxq, big["w_xq_t"], name="mix_in_xq_dx"),
        _mm(dgt, big["w_gates_t"], tk=3072, name="mix_in_gates_dx"),
    ]
    dx1, gs["mix_pre_g"] = _rms_bwd(x1, small["mix_pre_g"], dh_parts, dx2, 1.0, F32, name="mix_pre_bwd")
    dz1, gs["ffn1_post_g"] = _rms_bwd(f1, small["ffn1_post_g"], [dx1], None, 0.5, BF16, name="ffn1_post_bwd")
    dh1 = _ffn_bwd(dz1, h1, ab1, u1, big["ffn1_w_in"], big["ffn1_w_out"], "ffn1", emit)
    dx0, gs["ffn1_pre_g"] = _rms_bwd(x, small["ffn1_pre_g"], [dh1], dx1, 1.0, F32, name="ffn1_pre_bwd")
    return loss, dx0, gs


BIG = ("ffn1_w_in", "ffn1_w_out", "w_in", "w_mem_kv", "w_up_gla", "w_up_pool", "w_up_xattn", "w_o", "ffn2_w_in", "ffn2_w_out")
COL_SHARDED = ("ffn1_w_in", "w_in", "w_up_pool", "w_up_xattn", "ffn2_w_in")
GATHER_GROUPS = {"ffn1": ("ffn1_w_in", "ffn1_w_out"),
                 "mix": ("w_in", "w_mem_kv", "w_up_gla", "w_up_pool", "w_up_xattn", "w_o", "w_fu"),
                 "ffn2": ("ffn2_w_in", "ffn2_w_out")}
REDUCE_GROUPS = {"ffn2": ("ffn2_w_out", "ffn2_w_in"),
                 "mix": ("w_o", "w_up_gla", "w_up_pool", "w_up_xattn", "w_mem_kv", "w_gla", "w_p", "w_xq", "w_gates"),
                 "ffn1_out": ("ffn1_w_out",),
                 "ffn1_in": ("ffn1_w_in",)}
GAINS = ("ffn1_pre_g", "ffn1_post_g", "mix_pre_g", "gla_norm_g", "mem_norm_g", "mix_post_g", "ffn2_pre_g", "ffn2_post_g", "final_g")
WEIGHTS = ("ffn1_pre_g", "ffn1_w_in", "ffn1_w_out", "ffn1_post_g", "mix_pre_g", "w_in", "w_fu", "b_f", "gla_norm_g", "w_pool",
           "pool_scale", "mem_norm_g", "w_mem_kv", "w_up_gla", "w_up_pool", "w_up_xattn", "w_o", "mix_post_g", "ffn2_pre_g",
           "ffn2_w_in", "ffn2_w_out", "ffn2_post_g", "final_g")
IN_GLA, IN_F, IN_PX, IN_GATES, IN_END = 0, 3072, 3088, 4112, 7184
def _cols_from_shards(g):
    return jnp.transpose(g, (1, 0, 2)).reshape(g.shape[1], 4 * g.shape[2])


def kernel(x, mem, ffn1_pre_g, ffn1_w_in, ffn1_w_out, ffn1_post_g, mix_pre_g, w_in, w_fu, b_f, gla_norm_g, w_pool, pool_scale, mem_norm_g, w_mem_kv, w_up_gla, w_up_pool, w_up_xattn, w_o, mix_post_g, ffn2_pre_g, ffn2_w_in, ffn2_w_out, ffn2_post_g, final_g, loss_target, m_ffn1_pre_g, m_ffn1_w_in, m_ffn1_w_out, m_ffn1_post_g, m_mix_pre_g, m_w_in, m_w_fu, m_b_f, m_gla_norm_g, m_w_pool, m_pool_scale, m_mem_norm_g, m_w_mem_kv, m_w_up_gla, m_w_up_pool, m_w_up_xattn, m_w_o, m_mix_post_g, m_ffn2_pre_g, m_ffn2_w_in, m_ffn2_w_out, m_ffn2_post_g, m_final_g, v_ffn1_pre_g, v_ffn1_w_in, v_ffn1_w_out, v_ffn1_post_g, v_mix_pre_g, v_w_in, v_w_fu, v_b_f, v_gla_norm_g, v_w_pool, v_pool_scale, v_mem_norm_g, v_w_mem_kv, v_w_up_gla, v_w_up_pool, v_w_up_xattn, v_w_o, v_mix_post_g, v_ffn2_pre_g, v_ffn2_w_in, v_ffn2_w_out, v_ffn2_post_g, v_final_g):
    args = dict(locals())
    w = {n: args[n][0] for n in WEIGHTS}
    m = {n: args["m_" + n][0] for n in WEIGHTS}
    v = {n: args["v_" + n][0] for n in WEIGHTS}
    xi, yi, ci = lax.axis_index("x"), lax.axis_index("y"), lax.axis_index("c")
    chip = 2 * xi + yi

    c_arr = jnp.reshape(ci, (1,)).astype(jnp.int32)
    chip_arr = jnp.reshape(chip, (1,)).astype(jnp.int32)
    place_arr = jnp.stack([chip, ci]).astype(jnp.int32)
    shard_of = {n: (jnp.transpose(args[n][0])[None] if n == "w_in" else args[n]) for n in BIG}
    placed = {n: _place_shard(shard_of[n], chip_arr, BF16, name="place_" + n) for n in BIG}
    placed["w_fu"] = _place_shard(args["w_fu"], chip_arr, F32, name="place_w_fu")
    inflight = {}

    def relayout(names, gathered):
        out = {}
        for n, g in zip(names, gathered):
            if n == "w_fu":
                w_fu_full = _cols_from_shards(g)
                out["w_fu_pad"] = jnp.concatenate([w_fu_full, jnp.zeros((LANE - GATE_RANK, 512), F32)], axis=0).astype(BF16)
            elif n == "w_in":
                wt = g.reshape(IN_END, D)
                out["w_gla_t"] = jnp.concatenate([wt[IN_GLA:IN_PX], jnp.zeros((PG_W - IN_PX, D), BF16)], axis=0)
                out["w_px_t"] = wt[IN_PX:IN_GATES]
                out["w_p_t"] = wt[IN_PX:IN_PX + 512]
                out["w_xq_t"] = wt[IN_PX + 512:IN_GATES]
                out["w_gates_t"] = wt[IN_GATES:IN_END]
            else:
                out[n] = _cols_from_shards(g) if n in COL_SHARDED else g.reshape(4 * g.shape[1], g.shape[2])
        return out

    def start(group, after):
        inflight[group] = _gather_start([placed[n] for n in GATHER_GROUPS[group]], after, name="gather_" + group + "_start")

    def gather(step, after):
        group = step.split("_")[0]
        if step.endswith("_pass") or step == "ffn1":
            if step == "ffn1":
                start(group, ())
            send, recv, bufs, _ = inflight[group]
            inflight[group] = _gather_pass(bufs, send, recv, after, name="gather_" + group + "_pass")
            if step != "ffn1":
                return (inflight[group][2][0],)
        send, recv, bufs = inflight.pop(group)
        bufs = _gather_finish(bufs, send, recv, after, name="gather_" + group + "_finish")
        following = {"ffn1": "mix", "mix": "ffn2"}.get(group)
        behind = ()
        if following is not None:
            start(following, (bufs[0],))
            behind = (inflight[following][3],)
        return relayout(GATHER_GROUPS[group], bufs), behind

    small = {n: w[n].reshape(1, D) for n in GAINS}
    small["b_f"] = w["b_f"].reshape(1, 512)
    small["pool_scale"] = w["pool_scale"].reshape(1, 512)
    small["w_pool_b"] = w["w_pool"].astype(BF16)

    pending, travelling = {}, {}

    def emit(name, grad):
        pending[name] = grad
        group = next((g for g, names in REDUCE_GROUPS.items() if name == names[-1]), None)
        if group is None:
            return ()
        gb = {n: pending.pop(n) for n in REDUCE_GROUPS[group]}
        if group == "mix":
            dwt = jnp.concatenate([gb.pop("w_gla")[0:IN_PX], gb.pop("w_p"), gb.pop("w_xq"), gb.pop("w_gates")], axis=0)
            gb["w_in"] = dwt.reshape(4, IN_END // 4, D)
        names = list(gb)
        contrib = [gb[n] if n in COL_SHARDED else gb[n].reshape(4, gb[n].shape[0] // 4, gb[n].shape[1]) for n in names]
        from_sibling = _pair_exchange(contrib, name="grads_" + group + "_pair_exchange")
        pair = [_pair_sum(g, got, c_arr, name="grads_pair_sum_" + n) for n, g, got in zip(names, contrib, from_sibling)]
        send, recv, pair, lands, token = _chip_exchange_start(pair, (), name="grads_" + group + "_chip_start")
        travelling[group] = (names, send, recv, pair, lands)
        return (token,)

    loss, grad_x, gs = _local_step(x[0], mem[0], loss_target[0], small, gather, emit)
    loss = lax.psum(loss[0, 0], ("x", "y", "c"))

    small_sums = _all_sum_small(gs, name="sum_small_grads")
    halves = {}
    for group, (names, send, recv, pair, lands) in travelling.items():
        pair, from_chips = _chip_exchange_finish(pair, lands, send, recv, (grad_x,), name="grads_" + group + "_chip_finish")
        for n, p, got in zip(names, pair, from_chips):
            halves[n] = _chip_sum(p, got, place_arr, name="grads_chip_sum_" + n)
    reduced = dict(zip(BIG, _pair_join([halves[n] for n in BIG], name="grads_pair_join")))

    grads, delta, new_m, new_v = {}, {}, {}, {}
    for n in BIG:
        if n == "w_in":
            transposed = [jnp.transpose(args[k][0]) for k in (n, "m_" + n, "v_" + n)]
            updated = _adamw(transposed[0], reduced[n], transposed[1], transposed[2], name="adamw_" + n)
            grads[n] = jnp.transpose(reduced[n])[None]
            delta[n], new_m[n], new_v[n] = (jnp.transpose(a)[None] for a in updated)
            continue
        grads[n] = reduced[n][None]
        delta[n], new_m[n], new_v[n] = _adamw(args[n], reduced[n], args["m_" + n], args["v_" + n], name="adamw_" + n)
    small_params = {n: (args[n], args["m_" + n], args["v_" + n]) for n in SMALL}
    for n, (g, d, mn, vn) in _adamw_small(small_sums, small_params, chip_arr, name="adamw_small").items():
        grads[n], delta[n], new_m[n], new_v[n] = g, d, mn, vn

    outs = [loss, grad_x[None]]
    for group in (grads, delta, new_m, new_v):
        outs += [group[n] for n in WEIGHTS]
    return tuple(outs)
```

```python
import functools

import jax
import jax.numpy as jnp
from jax import lax
from jax.experimental import pallas as pl
from jax.experimental.pallas import tpu as pltpu

F32 = jnp.float32
BF16 = jnp.bfloat16
MESH = pl.DeviceIdType.MESH
HIGHEST = lax.Precision.HIGHEST

D = 1024
DFF = 2816
CHUNK = 64
HEADS = 4
HDK = 128
HDV = 256
GATE_TEMP = 16.0
POOL_WINDOWS = (2, 4, 8, 16)
POOL_HALO = 16
XA_HEADS = 4
XA_HD = 128
EPS = 1e-6
Q_SCALE = HDK ** -0.5
XA_SCALE = XA_HD ** -0.5
PG_Q, PG_K, PG_V, PG_G, PG_F, PG_W = 0, 512, 1024, 2048, 3072, 3200
GATE_RANK = 16
ADAM_LR, ADAM_B1, ADAM_B2, ADAM_EPS, ADAM_WD, ADAM_STEP = 0.001, 0.9, 0.999, 1e-08, 0.01, 10

VMEM_LIMIT = 48 * 1024 * 1024
LANE = 128
TS_ROW = 512
TS_GLA = 512
TS_POOL = 512
TS_XA = 512


def _params(sem):
    return pltpu.CompilerParams(dimension_semantics=sem, vmem_limit_bytes=VMEM_LIMIT)


def _tile(n, cap, unit=LANE):
    if n <= cap:
        return n
    best = None
    for t in range(unit, cap + 1, unit):
        if n % t == 0:
            best = t
    assert best is not None, (n, cap)
    return best


def _sigmoid(x):
    return 1.0 / (1.0 + jnp.exp(-x))


def _log_sigmoid(x):
    return jnp.minimum(x, 0.0) - jnp.log(1.0 + jnp.exp(-jnp.abs(x)))


def _rms(x):
    r = lax.rsqrt(jnp.mean(x * x, axis=-1, keepdims=True) + EPS)
    return x * r, r


def _rows(ts, w):
    return pl.BlockSpec((ts, w), lambda i: (i, 0))


def _fixed(shape):
    nd = len(shape)
    return pl.BlockSpec(shape, lambda i: (0,) * nd)


def _mm(a, b, *, ta=False, tb=False, out_dtype=F32, tm=2048, tn=1024, tk=1024, shards=1, after=(), name):
    a_blocked, b_blocked = a.ndim == 3, b.ndim == 3
    assert not (a_blocked and ta) and not (b_blocked and tb)
    if a_blocked:
        m, kdim, tk = a.shape[1], a.shape[0] * a.shape[2], a.shape[2]
    else:
        m, kdim = (a.shape[1], a.shape[0]) if ta else a.shape
    if b_blocked:
        n, tn = b.shape[0] * b.shape[2], b.shape[2]
        assert b.shape[1] == kdim and shards in (1, b.shape[0])
    else:
        n = b.shape[0] if tb else b.shape[1]
        assert (b.shape[1] if tb else b.shape[0]) == kdim, (a.shape, b.shape, ta, tb)
        tn = n // shards if shards > 1 else _tile(n, tn)
    tm = _tile(m, tm)
    tk = tk if a_blocked else _tile(kdim, tk)
    nk = kdim // tk
    dims = (((0 if ta else 1,), (1 if tb else 0,)), ((), ()))

    def body(a_ref, b_ref, *rest):
        o_ref, *acc = rest[len(after):]
        part = lax.dot_general(a_ref[...], b_ref[...], dims, preferred_element_type=F32)
        if nk == 1:
            o_ref[...] = part.astype(o_ref.dtype)
            return
        acc_ref, = acc
        k = pl.program_id(2)

        @pl.when(k == 0)
        def _():
            acc_ref[...] = part

        @pl.when(k > 0)
        def _():
            acc_ref[...] += part

        @pl.when(k == nk - 1)
        def _():
            o_ref[...] = acc_ref[...].astype(o_ref.dtype)

    if a_blocked:
        a_spec = pl.BlockSpec((None, tm, tk), lambda i, j, k: (k, i, 0))
    else:
        a_spec = pl.BlockSpec((tk, tm), lambda i, j, k: (k, i)) if ta else pl.BlockSpec((tm, tk), lambda i, j, k: (i, k))
    if b_blocked:
        b_spec = pl.BlockSpec((None, tk, tn), lambda i, j, k: (j, k, 0))
    else:
        b_spec = pl.BlockSpec((tn, tk), lambda i, j, k: (j, k)) if tb else pl.BlockSpec((tk, tn), lambda i, j, k: (k, j))
    if shards > 1:
        out_shape = jax.ShapeDtypeStruct((shards, m, tn), out_dtype)
        o_spec = pl.BlockSpec((None, tm, tn), lambda i, j, k: (j, i, 0))
    else:
        out_shape = jax.ShapeDtypeStruct((m, n), out_dtype)
        o_spec = pl.BlockSpec((tm, tn), lambda i, j, k: (i, j))
    return pl.pallas_call(
        body, grid=(m // tm, n // tn, nk), in_specs=[a_spec, b_spec] + [ANY] * len(after), out_specs=o_spec, out_shape=out_shape,
        scratch_shapes=[pltpu.VMEM((tm, tn), F32)] if nk > 1 else [],
        compiler_params=_params(("parallel", "parallel", "arbitrary")), name=name,
    )(a, b, *after)


def _norm_fwd(x, g, out_dtype, name, after=()):
    s, d = x.shape
    ts = _tile(s, TS_ROW, 8)

    def body(x_ref, g_ref, *rest):
        o_ref = rest[len(after)]
        xh, _ = _rms(x_ref[...])
        o_ref[...] = (xh * g_ref[...]).astype(o_ref.dtype)

    return pl.pallas_call(
        body, grid=(s // ts,), in_specs=[_rows(ts, d), _fixed((1, d))] + [ANY] * len(after), out_specs=_rows(ts, d),
        out_shape=jax.ShapeDtypeStruct((s, d), out_dtype), compiler_params=_params(("parallel",)), name=name,
    )(x, g, *after)


def _resid_norm_fwd(x, f, g_post, alpha, g_next, name):
    s, d = x.shape
    ts = _tile(s, TS_ROW, 8)
    with_h = g_next is not None

    def body(x_ref, f_ref, gp_ref, *rest):
        fh, _ = _rms(f_ref[...])
        xn = x_ref[...] + alpha * (fh * gp_ref[...])
        if with_h:
            gn_ref, xo_ref, h_ref = rest
            xh, _ = _rms(xn)
            h_ref[...] = (xh * gn_ref[...]).astype(h_ref.dtype)
        else:
            xo_ref, = rest
        xo_ref[...] = xn

    ins = [x, f, g_post] + ([g_next] if with_h else [])
    in_specs = [_rows(ts, d), _rows(ts, d), _fixed((1, d))] + ([_fixed((1, d))] if with_h else [])
    out_shape = [jax.ShapeDtypeStruct((s, d), F32)] + ([jax.ShapeDtypeStruct((s, d), BF16)] if with_h else [])
    out_specs = [_rows(ts, d)] + ([_rows(ts, d)] if with_h else [])
    out = pl.pallas_call(
        body, grid=(s // ts,), in_specs=in_specs, out_specs=out_specs, out_shape=out_shape,
        compiler_params=_params(("parallel",)), name=name,
    )(*ins)
    return (out[0], out[1]) if with_h else (out[0], None)


def _rms_bwd(x, g, dys, dres, alpha, out_dtype, name):
    s, d = x.shape
    ts = _tile(s, TS_ROW, 8)
    ndy = len(dys)
    with_res = dres is not None

    def body(x_ref, g_ref, *rest):
        dy_refs = rest[:ndy]
        rest = rest[ndy:]
        if with_res:
            dres_ref, dx_ref, dg_ref = rest
        else:
            dx_ref, dg_ref = rest
        xh, r = _rms(x_ref[...])
        dy = dy_refs[0][...].astype(F32)
        for ref in dy_refs[1:]:
            dy = dy + ref[...].astype(F32)
        dy = dy * alpha

        @pl.when(pl.program_id(0) == 0)
        def _():
            dg_ref[...] = jnp.zeros_like(dg_ref)

        dg_ref[...] += jnp.sum(dy * xh, axis=0, keepdims=True)
        dyg = dy * g_ref[...]
        dx = r * (dyg - xh * jnp.mean(dyg * xh, axis=-1, keepdims=True))
        if with_res:
            dx = dx + dres_ref[...]
        dx_ref[...] = dx.astype(dx_ref.dtype)

    ins = [x, g] + list(dys) + ([dres] if with_res else [])
    in_specs = [_rows(ts, d), _fixed((1, d))] + [_rows(ts, d)] * (ndy + int(with_res))
    return pl.pallas_call(
        body, grid=(s // ts,), in_specs=in_specs, out_specs=[_rows(ts, d), _fixed((1, d))],
        out_shape=[jax.ShapeDtypeStruct((s, d), out_dtype), jax.ShapeDtypeStruct((1, d), F32)],
        compiler_params=_params(("arbitrary",)), name=name,
    )(*ins)


def _loss_bwd(x, g, target, name):
    s, d = x.shape
    ts = _tile(s, TS_ROW, 8)

    def body(x_ref, g_ref, t_ref, dx_ref, dg_ref, loss_ref):
        xh, r = _rms(x_ref[...])
        gv = g_ref[...]
        diff = xh * gv - t_ref[...]

        @pl.when(pl.program_id(0) == 0)
        def _():
            dg_ref[...] = jnp.zeros_like(dg_ref)
            loss_ref[...] = jnp.zeros_like(loss_ref)

        sq = jnp.sum(diff * diff, axis=1, keepdims=True)
        loss_ref[...] += (0.5 / d) * jnp.sum(sq, axis=0, keepdims=True)
        dy = diff * (1.0 / d)
        dg_ref[...] += jnp.sum(dy * xh, axis=0, keepdims=True)
        dyg = dy * gv
        dx_ref[...] = r * (dyg - xh * jnp.mean(dyg * xh, axis=-1, keepdims=True))

    return pl.pallas_call(
        body, grid=(s // ts,), in_specs=[_rows(ts, d), _fixed((1, d)), _rows(ts, d)],
        out_specs=[_rows(ts, d), _fixed((1, d)), _fixed((8, LANE))],
        out_shape=[jax.ShapeDtypeStruct((s, d), F32), jax.ShapeDtypeStruct((1, d), F32), jax.ShapeDtypeStruct((8, LANE), F32)],
        compiler_params=_params(("arbitrary",)), name=name,
    )(x, g, target)


HALF_FF = DFF // 2


def _ffn_in_swiglu(x_norm, w_in, name, tm=1024):
    s, d = x_norm.shape
    tm = _tile(s, tm)

    def body(x_ref, wa_ref, wb_ref, ab_ref, u_ref):
        xv = x_ref[...]
        a = jnp.dot(xv, wa_ref[...], preferred_element_type=F32)
        b = jnp.dot(xv, wb_ref[...], preferred_element_type=F32)
        ab_ref[0] = a.astype(ab_ref.dtype)
        ab_ref[1] = b.astype(ab_ref.dtype)
        u_ref[...] = (a * _sigmoid(a) * b).astype(u_ref.dtype)

    ab, u = pl.pallas_call(
        body, grid=(s // tm, 2),
        in_specs=[pl.BlockSpec((tm, d), lambda i, j: (i, 0)), pl.BlockSpec((d, HALF_FF), lambda i, j: (0, j)),
                  pl.BlockSpec((d, HALF_FF), lambda i, j: (0, 2 + j))],
        out_specs=[pl.BlockSpec((2, None, tm, HALF_FF), lambda i, j: (0, j, i, 0)), pl.BlockSpec((tm, HALF_FF), lambda i, j: (i, j))],
        out_shape=[jax.ShapeDtypeStruct((2, 2, s, HALF_FF), BF16), jax.ShapeDtypeStruct((s, DFF), BF16)],
        compiler_params=_params(("parallel", "parallel")), name=name,
    )(x_norm, w_in, w_in)
    return ab.reshape(4, s, HALF_FF), u


def _ffn_out_dx_swiglu(dz, w_out, ab, after, name, tm=1024):
    s, d = dz.shape
    tm = _tile(s, tm)

    def body(dz_ref, w_ref, ab_ref, *rest):
        dab_ref = rest[len(after)]
        du = lax.dot_general(dz_ref[...], w_ref[...], (((1,), (1,)), ((), ())), preferred_element_type=F32)
        a = ab_ref[0].astype(F32)
        b = ab_ref[1].astype(F32)
        sig = _sigmoid(a)
        dab_ref[0] = (du * b * (sig * (1.0 + a * (1.0 - sig)))).astype(dab_ref.dtype)
        dab_ref[1] = (du * a * sig).astype(dab_ref.dtype)

    halves = pl.BlockSpec((2, None, tm, HALF_FF), lambda i, j: (0, j, i, 0))
    dab = pl.pallas_call(
        body, grid=(s // tm, 2),
        in_specs=[pl.BlockSpec((tm, d), lambda i, j: (i, 0)), pl.BlockSpec((HALF_FF, d), lambda i, j: (j, 0)), halves] + [ANY] * len(after),
        out_specs=halves, out_shape=jax.ShapeDtypeStruct((2, 2, s, HALF_FF), BF16),
        compiler_params=_params(("parallel", "parallel")), name=name,
    )(dz, w_out, ab.reshape(2, 2, s, HALF_FF), *after)
    return dab.reshape(4, s, HALF_FF)


def _tri(strict):
    r = lax.broadcasted_iota(jnp.int32, (CHUNK, CHUNK), 0)
    c = lax.broadcasted_iota(jnp.int32, (CHUNK, CHUNK), 1)
    return (r > c).astype(F32) if strict else (r >= c).astype(F32)


def _gla_fwd(pg, wfu, b_f, gnorm, name):
    s = pg.shape[0]
    ts = _tile(s, TS_GLA, CHUNK)
    cpb = ts // CHUNK
    nc = s // CHUNK

    def body(pg_ref, wfu_ref, bf_ref, gn_ref, ya_ref, sp_ref, st_ref, la_ref, dec_ref, u_ref):
        @pl.when(pl.program_id(0) == 0)
        def _():
            st_ref[...] = jnp.zeros_like(st_ref)

        f = jnp.dot(pg_ref[:, PG_F:PG_W], wfu_ref[...], preferred_element_type=F32) + bf_ref[...]
        la_ref[...] = _log_sigmoid(f) * (1.0 / GATE_TEMP)
        tri = _tri(False)
        chunks = [slice(ci * CHUNK, (ci + 1) * CHUNK) for ci in range(cpb)]
        for ci, rows in enumerate(chunks):
            la = la_ref[rows, :]
            b = jnp.dot(tri, la, precision=HIGHEST, preferred_element_type=F32)
            bend = jnp.sum(la, axis=0, keepdims=True)
            e = jnp.exp(bend - b)
            dec_ref[ci:ci + 1, :] = jnp.exp(bend)
            for hd in range(HEADS):
                k = pg_ref[rows, PG_K + hd * HDK:PG_K + (hd + 1) * HDK]
                v = pg_ref[rows, PG_V + hd * HDV:PG_V + (hd + 1) * HDV]
                kt = (k.astype(F32) * e[:, hd * HDK:(hd + 1) * HDK]).astype(BF16)
                u_ref[ci, hd] = lax.dot_general(v, kt, (((0,), (0,)), ((), ())), preferred_element_type=F32)
        for ci in range(cpb):
            for hd in range(HEADS):
                prev = st_ref[hd]
                sp_ref[ci, hd] = prev
                st = prev * dec_ref[ci:ci + 1, hd * HDK:(hd + 1) * HDK] + u_ref[ci, hd]
                st_ref[hd] = st
                u_ref[ci, hd] = st
        for ci, rows in enumerate(chunks):
            for hd in range(HEADS):
                vc = slice(hd * HDV, (hd + 1) * HDV)
                q = pg_ref[rows, PG_Q + hd * HDK:PG_Q + (hd + 1) * HDK]
                go = pg_ref[rows, PG_G + hd * HDV:PG_G + (hd + 1) * HDV].astype(F32)
                qs = (q.astype(F32) * Q_SCALE).astype(BF16)
                o = lax.dot_general(qs, u_ref[ci, hd].astype(BF16), (((1,), (1,)), ((), ())), preferred_element_type=F32)
                oh, _ = _rms(o)
                ya_ref[rows, vc] = (oh * gn_ref[:, vc] * (go * _sigmoid(go))).astype(ya_ref.dtype)

    return pl.pallas_call(
        body, grid=(s // ts,),
        in_specs=[_rows(ts, PG_W), _fixed((LANE, HEADS * HDK)), _fixed((1, HEADS * HDK)), _fixed((1, HEADS * HDV))],
        out_specs=[_rows(ts, HEADS * HDV), pl.BlockSpec((cpb, HEADS, HDV, HDK), lambda i: (i, 0, 0, 0))],
        out_shape=[jax.ShapeDtypeStruct((s, HEADS * HDV), BF16), jax.ShapeDtypeStruct((nc, HEADS, HDV, HDK), F32)],
        scratch_shapes=[pltpu.VMEM((HEADS, HDV, HDK), F32), pltpu.VMEM((ts, HEADS * HDK), F32),
                        pltpu.VMEM((max(cpb, 8), HEADS * HDK), F32), pltpu.VMEM((cpb, HEADS, HDV, HDK), F32)],
        compiler_params=_params(("arbitrary",)), name=name,
    )(pg, wfu, b_f, gnorm)


def _gla_bwd(pg, sp, dya, wfu, b_f, gnorm, name):
    s = pg.shape[0]
    ts = _tile(s, TS_GLA, CHUNK)
    cpb = ts // CHUNK
    nblk = s // ts

    def body(pg_ref, sp_ref, dya_ref, wfu_ref, bf_ref, gn_ref, dpg_ref, dwfu_ref, dbf_ref, dgn_ref,
             dst_ref, la_ref, sg_ref, df_ref, e_ref, ktf_ref, dec_ref, g_ref):
        @pl.when(pl.program_id(0) == 0)
        def _():
            dst_ref[...] = jnp.zeros_like(dst_ref)
            dwfu_ref[...] = jnp.zeros_like(dwfu_ref)
            dbf_ref[...] = jnp.zeros_like(dbf_ref)
            dgn_ref[...] = jnp.zeros_like(dgn_ref)

        flow = pg_ref[:, PG_F:PG_W]
        f = jnp.dot(flow, wfu_ref[...], preferred_element_type=F32) + bf_ref[...]
        la_ref[...] = _log_sigmoid(f) * (1.0 / GATE_TEMP)
        sg_ref[...] = _sigmoid(-f) * (1.0 / GATE_TEMP)
        tri = _tri(False)
        tri_strict = _tri(True)
        chunks = [slice(ci * CHUNK, (ci + 1) * CHUNK) for ci in range(cpb)]
        for ci, rows in enumerate(chunks):
            la = la_ref[rows, :]
            b = jnp.dot(tri, la, precision=HIGHEST, preferred_element_type=F32)
            bend = jnp.sum(la, axis=0, keepdims=True)
            e = jnp.exp(bend - b)
            e_ref[rows, :] = e
            dec = jnp.exp(bend)
            dec_ref[ci:ci + 1, :] = dec
            for hd in range(HEADS):
                kc = slice(hd * HDK, (hd + 1) * HDK)
                vc = slice(hd * HDV, (hd + 1) * HDV)
                q = pg_ref[rows, PG_Q + hd * HDK:PG_Q + (hd + 1) * HDK]
                k = pg_ref[rows, PG_K + hd * HDK:PG_K + (hd + 1) * HDK]
                v = pg_ref[rows, PG_V + hd * HDV:PG_V + (hd + 1) * HDV]
                go = pg_ref[rows, PG_G + hd * HDV:PG_G + (hd + 1) * HDV].astype(F32)
                ktf = k.astype(F32) * e[:, kc]
                ktf_ref[rows, kc] = ktf
                st = sp_ref[ci, hd] * dec[:, kc] + lax.dot_general(v, ktf.astype(BF16), (((0,), (0,)), ((), ())), preferred_element_type=F32)
                st_b = st.astype(BF16)
                qs = (q.astype(F32) * Q_SCALE).astype(BF16)
                o = lax.dot_general(qs, st_b, (((1,), (1,)), ((), ())), preferred_element_type=F32)
                oh, r = _rms(o)
                gh = gn_ref[:, vc]
                sig = _sigmoid(go)
                dy = dya_ref[rows, vc].astype(F32)
                don = dy * (go * sig)
                dgn_ref[:, vc] += jnp.sum(don * oh, axis=0, keepdims=True)
                dong = don * gh
                do = (r * (dong - oh * jnp.mean(dong * oh, axis=-1, keepdims=True))).astype(BF16)
                g_ref[ci, hd] = lax.dot_general(do, qs, (((0,), (0,)), ((), ())), preferred_element_type=F32)
                dq = jnp.dot(do, st_b, preferred_element_type=F32) * Q_SCALE
                dpg_ref[rows, PG_Q + hd * HDK:PG_Q + (hd + 1) * HDK] = dq.astype(dpg_ref.dtype)
                dgo = dy * (oh * gh) * (sig * (1.0 + go * (1.0 - sig)))
                dpg_ref[rows, PG_G + hd * HDV:PG_G + (hd + 1) * HDV] = dgo.astype(dpg_ref.dtype)
        for ci in reversed(range(cpb)):
            for hd in range(HEADS):
                dst = dst_ref[hd] + g_ref[ci, hd]
                g_ref[ci, hd] = dst
                dst_ref[hd] = dst * dec_ref[ci:ci + 1, hd * HDK:(hd + 1) * HDK]
        for ci, rows in enumerate(chunks):
            for hd in range(HEADS):
                kc = slice(hd * HDK, (hd + 1) * HDK)
                v = pg_ref[rows, PG_V + hd * HDV:PG_V + (hd + 1) * HDV]
                ktf = ktf_ref[rows, kc]
                dst = g_ref[ci, hd]
                dst_b = dst.astype(BF16)
                dkt = jnp.dot(v, dst_b, preferred_element_type=F32)
                dv = lax.dot_general(ktf.astype(BF16), dst_b, (((1,), (1,)), ((), ())), preferred_element_type=F32)
                dd = jnp.sum(dst * sp_ref[ci, hd], axis=0, keepdims=True)
                dla = jnp.dot(tri_strict, dkt * ktf, precision=HIGHEST, preferred_element_type=F32) + dd * dec_ref[ci:ci + 1, kc]
                df_ref[rows, kc] = dla * sg_ref[rows, kc]
                dpg_ref[rows, PG_K + hd * HDK:PG_K + (hd + 1) * HDK] = (dkt * e_ref[rows, kc]).astype(dpg_ref.dtype)
                dpg_ref[rows, PG_V + hd * HDV:PG_V + (hd + 1) * HDV] = dv.astype(dpg_ref.dtype)
        df = df_ref[...]
        df_b = df.astype(BF16)
        dpg_ref[:, PG_F:PG_W] = lax.dot_general(df_b, wfu_ref[...], (((1,), (1,)), ((), ())), preferred_element_type=F32).astype(dpg_ref.dtype)
        dwfu_ref[...] += lax.dot_general(flow, df_b, (((0,), (0,)), ((), ())), preferred_element_type=F32)
        dbf_ref[...] += jnp.sum(df, axis=0, keepdims=True)

    rev = lambda i: (nblk - 1 - i, 0)
    return pl.pallas_call(
        body, grid=(nblk,),
        in_specs=[pl.BlockSpec((ts, PG_W), rev), pl.BlockSpec((cpb, HEADS, HDV, HDK), lambda i: (nblk - 1 - i, 0, 0, 0)),
                  pl.BlockSpec((ts, HEADS * HDV), rev), _fixed((LANE, HEADS * HDK)), _fixed((1, HEADS * HDK)), _fixed((1, HEADS * HDV))],
        out_specs=[pl.BlockSpec((ts, PG_W), rev), _fixed((LANE, HEADS * HDK)), _fixed((1, HEADS * HDK)), _fixed((1, HEADS * HDV))],
        out_shape=[jax.ShapeDtypeStruct((s, PG_W), BF16), jax.ShapeDtypeStruct((LANE, HEADS * HDK), F32),
                   jax.ShapeDtypeStruct((1, HEADS * HDK), F32), jax.ShapeDtypeStruct((1, HEADS * HDV), F32)],
        scratch_shapes=[pltpu.VMEM((HEADS, HDV, HDK), F32)] + [pltpu.VMEM((ts, HEADS * HDK), F32)] * 5
        + [pltpu.VMEM((max(cpb, 8), HEADS * HDK), F32), pltpu.VMEM((cpb, HEADS, HDV, HDK), F32)],
        compiler_params=_params(("arbitrary",)), name=name,
    )(pg, sp, dya, wfu, b_f, gnorm)


def _window_sums(ext, sign):
    n = ext.shape[0]
    sums = {1: ext}
    w = 1
    while w < POOL_WINDOWS[-1]:
        sums[2 * w] = sums[w] + pltpu.roll(sums[w], w if sign > 0 else n - w, 0)
        w *= 2
    return [sums[POOL_WINDOWS[g]][:, g * LANE:(g + 1) * LANE] for g in range(len(POOL_WINDOWS))]


def _pool_counts(row0, n):
    pos = (row0 + lax.broadcasted_iota(jnp.int32, (n, 1), 0) + 1).astype(F32)
    return [jnp.minimum(pos, float(w)) for w in POOL_WINDOWS]


def _pool_fwd(ppx, w_pool, pool_scale, name):
    s = ppx.shape[0]
    ts = _tile(s, TS_POOL, POOL_HALO)
    hb = ts // POOL_HALO
    pw = len(POOL_WINDOWS) * LANE

    def body(p_ref, halo_ref, w_ref, sc_ref, y_ref, ext_ref):
        i = pl.program_id(0)
        p = p_ref[...].astype(F32)
        ext_ref[0:POOL_HALO, :] = jnp.where(i > 0, halo_ref[...].astype(F32), 0.0)
        ext_ref[POOL_HALO:, :] = p
        sums = _window_sums(ext_ref[...], +1)
        cnt = _pool_counts(i * ts, ts)
        for g in range(len(POOL_WINDOWS)):
            cols = slice(g * LANE, (g + 1) * LANE)
            mixed = sums[g][POOL_HALO:, :] / cnt[g] - p[:, cols]
            y = jnp.dot(mixed.astype(BF16), w_ref[g], preferred_element_type=F32)
            y_ref[:, cols] = (y * sc_ref[:, cols]).astype(y_ref.dtype)

    return pl.pallas_call(
        body, grid=(s // ts,),
        in_specs=[pl.BlockSpec((ts, pw), lambda i: (i, 0)), pl.BlockSpec((POOL_HALO, pw), lambda i: (jnp.maximum(i * hb - 1, 0), 0)),
                  _fixed((len(POOL_WINDOWS), LANE, LANE)), _fixed((1, pw))],
        out_specs=_rows(ts, pw), out_shape=jax.ShapeDtypeStruct((s, pw), BF16),
        scratch_shapes=[pltpu.VMEM((ts + POOL_HALO, pw), F32)],
        compiler_params=_params(("parallel",)), name=name,
    )(ppx, ppx, w_pool, pool_scale)


def _pool_bwd(dyb, ppx, w_pool, pool_scale, name):
    s = ppx.shape[0]
    ts = _tile(s, TS_POOL, POOL_HALO)
    hb = ts // POOL_HALO
    nblk = s // ts
    last_halo = s // POOL_HALO - 1
    ng = len(POOL_WINDOWS)
    pw = ng * LANE

    def body(p_ref, halo_ref, dy_ref, dyn_ref, w_ref, sc_ref, dp_ref, dw_ref, dsc_ref, ext_ref, dext_ref, dm_ref):
        i = pl.program_id(0)

        @pl.when(i == 0)
        def _():
            dw_ref[...] = jnp.zeros_like(dw_ref)
            dsc_ref[...] = jnp.zeros_like(dsc_ref)

        p = p_ref[...].astype(F32)
        ext_ref[0:POOL_HALO, :] = jnp.where(i > 0, halo_ref[...].astype(F32), 0.0)
        ext_ref[POOL_HALO:, :] = p
        sums = _window_sums(ext_ref[...], +1)
        cnt = _pool_counts(i * ts, ts + POOL_HALO)
        sc = sc_ref[...]
        dy = dy_ref[...].astype(F32)
        dyn = jnp.where(i < nblk - 1, dyn_ref[...].astype(F32), 0.0)
        for g in range(ng):
            cols = slice(g * LANE, (g + 1) * LANE)
            wg = w_ref[g]
            mixed = (sums[g][POOL_HALO:, :] / cnt[g][0:ts] - p[:, cols]).astype(BF16)
            ypre = jnp.dot(mixed, wg, preferred_element_type=F32)
            dsc_ref[:, cols] += jnp.sum(dy[:, cols] * ypre, axis=0, keepdims=True)
            dyp = (dy[:, cols] * sc[:, cols]).astype(BF16)
            dypn = (dyn[:, cols] * sc[:, cols]).astype(BF16)
            dw_ref[g] += lax.dot_general(mixed, dyp, (((0,), (0,)), ((), ())), preferred_element_type=F32)
            dm = lax.dot_general(dyp, wg, (((1,), (1,)), ((), ())), preferred_element_type=F32)
            dmn = lax.dot_general(dypn, wg, (((1,), (1,)), ((), ())), preferred_element_type=F32)
            dext_ref[0:ts, cols] = dm / cnt[g][0:ts]
            dext_ref[ts:, cols] = dmn / cnt[g][ts:]
            dm_ref[:, cols] = dm
        lead = _window_sums(dext_ref[...], -1)
        for g in range(ng):
            cols = slice(g * LANE, (g + 1) * LANE)
            dp_ref[:, cols] = (lead[g][0:ts, :] - dm_ref[:, cols]).astype(dp_ref.dtype)

    return pl.pallas_call(
        body, grid=(nblk,),
        in_specs=[pl.BlockSpec((ts, pw), lambda i: (i, 0)), pl.BlockSpec((POOL_HALO, pw), lambda i: (jnp.maximum(i * hb - 1, 0), 0)),
                  pl.BlockSpec((ts, pw), lambda i: (i, 0)), pl.BlockSpec((POOL_HALO, pw), lambda i: (jnp.minimum((i + 1) * hb, last_halo), 0)),
                  _fixed((ng, LANE, LANE)), _fixed((1, pw))],
        out_specs=[_rows(ts, pw), _fixed((ng, LANE, LANE)), _fixed((1, pw))],
        out_shape=[jax.ShapeDtypeStruct((s, pw), BF16), jax.ShapeDtypeStruct((ng, LANE, LANE), F32), jax.ShapeDtypeStruct((1, pw), F32)],
        scratch_shapes=[pltpu.VMEM((ts + POOL_HALO, pw), F32), pltpu.VMEM((ts + POOL_HALO, pw), F32), pltpu.VMEM((ts, pw), F32)],
        compiler_params=_params(("arbitrary",)), name=name,
    )(ppx, ppx, dyb, dyb, w_pool, pool_scale)


def _xattn_fwd(ppx, kv, name):
    s = ppx.shape[0]
    m = kv.shape[0]
    ts = _tile(s, TS_XA, 8)
    xw = XA_HEADS * XA_HD

    def body(q_ref, kv_ref, o_ref):
        for hd in range(XA_HEADS):
            cols = slice(hd * XA_HD, (hd + 1) * XA_HD)
            k = kv_ref[:, hd * XA_HD:(hd + 1) * XA_HD]
            v = kv_ref[:, xw + hd * XA_HD:xw + (hd + 1) * XA_HD]
            sc = lax.dot_general(q_ref[:, cols], k, (((1,), (1,)), ((), ())), preferred_element_type=F32) * XA_SCALE
            ex = jnp.exp(sc - jnp.max(sc, axis=-1, keepdims=True))
            pr = ex / jnp.sum(ex, axis=-1, keepdims=True)
            o_ref[:, cols] = jnp.dot(pr.astype(BF16), v, preferred_element_type=F32).astype(o_ref.dtype)

    return pl.pallas_call(
        body, grid=(s // ts,), in_specs=[pl.BlockSpec((ts, xw), lambda i: (i, 1)), _fixed((m, 2 * xw))],
        out_specs=_rows(ts, xw), out_shape=jax.ShapeDtypeStruct((s, xw), BF16),
        compiler_params=_params(("parallel",)), name=name,
    )(ppx, kv)


def _xattn_bwd(dxc, ppx, kv, name):
    s = ppx.shape[0]
    m = kv.shape[0]
    ts = _tile(s, TS_XA, 8)
    xw = XA_HEADS * XA_HD

    def body(do_ref, q_ref, kv_ref, dq_ref, dkv_ref):
        @pl.when(pl.program_id(0) == 0)
        def _():
            dkv_ref[...] = jnp.zeros_like(dkv_ref)

        for hd in range(XA_HEADS):
            cols = slice(hd * XA_HD, (hd + 1) * XA_HD)
            vcols = slice(xw + hd * XA_HD, xw + (hd + 1) * XA_HD)
            q = q_ref[:, cols]
            k = kv_ref[:, cols]
            v = kv_ref[:, vcols]
            do = do_ref[:, cols]
            sc = lax.dot_general(q, k, (((1,), (1,)), ((), ())), preferred_element_type=F32) * XA_SCALE
            ex = jnp.exp(sc - jnp.max(sc, axis=-1, keepdims=True))
            pr = ex / jnp.sum(ex, axis=-1, keepdims=True)
            dpr = lax.dot_general(do, v, (((1,), (1,)), ((), ())), preferred_element_type=F32)
            dsc = (pr * (dpr - jnp.sum(dpr * pr, axis=-1, keepdims=True)) * XA_SCALE).astype(BF16)
            dq_ref[:, cols] = jnp.dot(dsc, k, preferred_element_type=F32).astype(dq_ref.dtype)
            dkv_ref[:, cols] += lax.dot_general(dsc, q, (((0,), (0,)), ((), ())), preferred_element_type=F32)
            dkv_ref[:, vcols] += lax.dot_general(pr.astype(BF16), do, (((0,), (0,)), ((), ())), preferred_element_type=F32)

    return pl.pallas_call(
        body, grid=(s // ts,), in_specs=[_rows(ts, xw), pl.BlockSpec((ts, xw), lambda i: (i, 1)), _fixed((m, 2 * xw))],
        out_specs=[_rows(ts, xw), _fixed((m, 2 * xw))],
        out_shape=[jax.ShapeDtypeStruct((s, xw), BF16), jax.ShapeDtypeStruct((m, 2 * xw), F32)],
        compiler_params=_params(("arbitrary",)), name=name,
    )(dxc, ppx, kv)


def _merge_fwd(pgt, ya, yb, yc, name):
    s = pgt.shape[0]
    ts = _tile(s, TS_ROW, 8)

    def body(gt_ref, ya_ref, yb_ref, yc_ref, o_ref):
        acc = _sigmoid(gt_ref[:, 0:D].astype(F32)) * ya_ref[...].astype(F32)
        acc = acc + _sigmoid(gt_ref[:, D:2 * D].astype(F32)) * yb_ref[...].astype(F32)
        acc = acc + _sigmoid(gt_ref[:, 2 * D:3 * D].astype(F32)) * yc_ref[...].astype(F32)
        o_ref[...] = acc.astype(o_ref.dtype)

    return pl.pallas_call(
        body, grid=(s // ts,), in_specs=[_rows(ts, 3 * D)] + [_rows(ts, D)] * 3, out_specs=_rows(ts, D),
        out_shape=jax.ShapeDtypeStruct((s, D), BF16), compiler_params=_params(("parallel",)), name=name,
    )(pgt, ya, yb, yc)


def _merge_bwd(dmerged, pgt, ya, yb, yc, name):
    s = pgt.shape[0]
    ts = _tile(s, TS_ROW, 8)

    def body(dm_ref, gt_ref, ya_ref, yb_ref, yc_ref, dya_ref, dyb_ref, dyc_ref, dgt_ref):
        dm = dm_ref[...].astype(F32)
        for j, (y_ref, dy_ref) in enumerate(((ya_ref, dya_ref), (yb_ref, dyb_ref), (yc_ref, dyc_ref))):
            sig = _sigmoid(gt_ref[:, j * D:(j + 1) * D].astype(F32))
            dy_ref[...] = (dm * sig).astype(dy_ref.dtype)
            dgt_ref[:, j * D:(j + 1) * D] = (dm * y_ref[...].astype(F32) * sig * (1.0 - sig)).astype(dgt_ref.dtype)

    return pl.pallas_call(
        body, grid=(s // ts,), in_specs=[_rows(ts, D), _rows(ts, 3 * D)] + [_rows(ts, D)] * 3,
        out_specs=[_rows(ts, D)] * 3 + [_rows(ts, 3 * D)],
        out_shape=[jax.ShapeDtypeStruct((s, D), BF16)] * 3 + [jax.ShapeDtypeStruct((s, 3 * D), BF16)],
        compiler_params=_params(("parallel",)), name=name,
    )(dmerged, pgt, ya, yb, yc)


def _adam_math(w, g, m, v):
    mn = ADAM_B1 * m + (1.0 - ADAM_B1) * g
    vn = ADAM_B2 * v + (1.0 - ADAM_B2) * (g * g)
    m_hat = mn / (1.0 - ADAM_B1 ** ADAM_STEP)
    v_hat = vn / (1.0 - ADAM_B2 ** ADAM_STEP)
    return -ADAM_LR * (m_hat / (jnp.sqrt(v_hat) + ADAM_EPS) + ADAM_WD * w), mn, vn


def _adamw(w, g, m, v, name):
    r, c = w.shape[-2:]
    tr, tc = _block_of(r, c)

    def spec(a):
        if a.ndim == 2:
            return pl.BlockSpec((tr, tc), lambda i, j: (i, j))
        return pl.BlockSpec((None, tr, tc), lambda i, j: (0, i, j))

    def body(w_ref, g_ref, m_ref, v_ref, d_ref, mo_ref, vo_ref):
        d_ref[...], mo_ref[...], vo_ref[...] = _adam_math(w_ref[...], g_ref[...], m_ref[...], v_ref[...])

    return pl.pallas_call(
        body, grid=(r // tr, c // tc), in_specs=[spec(a) for a in (w, g, m, v)], out_specs=[spec(w)] * 3,
        out_shape=[jax.ShapeDtypeStruct(w.shape, F32)] * 3, compiler_params=_params(("parallel", "parallel")), name=name,
    )(w, g, m, v)


ANY = pl.BlockSpec(memory_space=pl.ANY)


def _place():
    x, y, c = lax.axis_index("x"), lax.axis_index("y"), lax.axis_index("c")
    chips = [(1 - x, y), (x, 1 - y), (1 - x, 1 - y)]
    return x, y, c, chips


def _half(c, rows):
    h = rows // 2
    return pl.ds(pl.multiple_of(c * h, 8), h)


def _by_cols(rows):
    return rows % 32 != 0 and rows != 16


def _half_of(ref, lead, c):
    r, cols = ref.shape[-2:]
    if _by_cols(r):
        return ref.at[(*lead, slice(None), pl.ds(pl.multiple_of(c * (cols // 2), LANE), cols // 2))]
    return ref.at[(*lead, pl.ds(pl.multiple_of(c * (r // 2), 8), r // 2))]


def _half_shape(shape):
    r, cols = shape[-2:]
    return shape[:-2] + ((r, cols // 2) if _by_cols(r) else (r // 2, cols))


def _block_of(r, cols, cap=256):
    if r % 16 == 0:
        return _tile(r, cap, 16), cols
    return r, _tile(cols, cap)


def _place_shard(shard, chip_arr, out_dtype, name):
    _, r, cols = shard.shape
    tr, tc = _block_of(r, cols)

    def body(chip_ref, s_ref, o_ref):
        o_ref[...] = s_ref[...].astype(o_ref.dtype)

    return pl.pallas_call(
        body,
        grid_spec=pltpu.PrefetchScalarGridSpec(
            num_scalar_prefetch=1, grid=(r // tr, cols // tc),
            in_specs=[pl.BlockSpec((None, tr, tc), lambda i, j, chip_ref: (0, i, j))],
            out_specs=pl.BlockSpec((None, tr, tc), lambda i, j, chip_ref: (chip_ref[0], i, j))),
        out_shape=jax.ShapeDtypeStruct((4, r, cols), out_dtype),
        compiler_params=_params(("parallel", "parallel")), name=name,
    )(chip_arr, shard)


def _gather_shards(bufs, name):
    n = len(bufs)

    def body(*refs):
        outs = refs[n:2 * n]
        send_ici, recv_ici, send_d2d, recv_d2d = refs[2 * n:]
        x, y, c, chips = _place()
        me = 2 * x + y
        sibling = (x, y, 1 - c)

        def ici(w, p, chip_of_block, to):
            rows = _half(c, outs[w].shape[1])
            block = outs[w].at[chip_of_block, rows]
            return pltpu.make_async_remote_copy(
                src_ref=block, dst_ref=block, send_sem=send_ici.at[w, p], recv_sem=recv_ici.at[w, p], device_id=to, device_id_type=MESH)

        def d2d(w, p, chip_of_block, half_of):
            rows = _half(half_of, outs[w].shape[1])
            block = outs[w].at[chip_of_block, rows]
            return pltpu.make_async_remote_copy(
                src_ref=block, dst_ref=block, send_sem=send_d2d.at[w, p], recv_sem=recv_d2d.at[w, p], device_id=sibling, device_id_type=MESH)

        sends = [ici(w, p, me, (*chip, c)) for p, chip in enumerate(chips) for w in range(n)]
        for cp in sends:
            cp.start()
        passed = []
        for p, (px, py) in enumerate(chips):
            for w in range(n):
                ici(w, p, 2 * px + py, (px, py, c)).wait_recv()
                fwd = d2d(w, p, 2 * px + py, c)
                fwd.start()
                passed.append(fwd)
        for p, (px, py) in enumerate(chips):
            for w in range(n):
                d2d(w, p, 2 * px + py, 1 - c).wait_recv()
        for cp in sends + passed:
            cp.wait_send()

    return pl.pallas_call(
        body, in_specs=[ANY] * n, out_specs=[ANY] * n,
        out_shape=[jax.ShapeDtypeStruct(a.shape, a.dtype) for a in bufs],
        input_output_aliases={w: w for w in range(n)},
        scratch_shapes=[pltpu.SemaphoreType.DMA((n, 3))] * 4,
        compiler_params=pltpu.CompilerParams(has_side_effects=True), name=name,
    )(*bufs)


HBM = pl.BlockSpec(memory_space=pltpu.HBM)
SEM = pl.BlockSpec(memory_space=pltpu.SEMAPHORE)
EFFECT = pltpu.SideEffectType.DATAFLOW_SIDE_EFFECTING


def _in_hbm(arrays):
    return [pltpu.with_memory_space_constraint(a, pltpu.HBM) for a in arrays]


def _gather_start(bufs, after, name):
    n, na = len(bufs), len(after)

    def body(*refs):
        send_sem, recv_sem = refs[n + na], refs[n + na + 1]
        outs = refs[n + na + 2:2 * n + na + 2]
        token = refs[2 * n + na + 2]
        x, y, c, chips = _place()
        me = 2 * x + y
        for p, chip in enumerate(chips):
            for w in range(n):
                block = _half_of(outs[w], (me,), c)
                pltpu.make_async_remote_copy(
                    src_ref=block, dst_ref=block, send_sem=send_sem, recv_sem=recv_sem,
                    device_id=(*chip, c), device_id_type=MESH).start()
        token[...] = jnp.zeros_like(token)

    out = pl.pallas_call(
        body, name=name, in_specs=[HBM] * n + [ANY] * na,
        out_specs=[SEM, SEM] + [HBM] * n + [pl.BlockSpec(memory_space=pltpu.VMEM)],
        out_shape=[pltpu.SemaphoreType.DMA(()), pltpu.SemaphoreType.DMA(())]
        + [pltpu.HBM(a.shape, a.dtype) for a in bufs] + [jax.ShapeDtypeStruct((8, LANE), F32)],
        input_output_aliases={w: w + 2 for w in range(n)},
        compiler_params=pltpu.CompilerParams(has_side_effects=EFFECT),
    )(*_in_hbm(bufs), *after)
    return out[0], out[1], list(out[2:2 + n]), out[2 + n]


def _gather_pass(bufs, send_sem, recv_sem, after, name):
    n, na = len(bufs), len(after)

    def body(*refs):
        send1, recv1 = refs[n], refs[n + 1]
        send2, recv2 = refs[n + 2 + na], refs[n + 3 + na]
        outs = refs[n + 4 + na:2 * n + 4 + na]
        x, y, c, chips = _place()
        me = 2 * x + y
        arrivals = [(w, px, py) for px, py in chips for w in range(n)]
        for w, px, py in arrivals:
            first = pltpu.make_async_remote_copy(
                src_ref=_half_of(outs[w], (me,), c), dst_ref=_half_of(outs[w], (2 * px + py,), c), send_sem=send1, recv_sem=recv1,
                device_id=(px, py, c), device_id_type=MESH)
            first.wait_send()
            first.wait_recv()
        for w, px, py in arrivals:
            arrived = _half_of(outs[w], (2 * px + py,), c)
            pltpu.make_async_remote_copy(
                src_ref=arrived, dst_ref=arrived, send_sem=send2, recv_sem=recv2,
                device_id=(x, y, 1 - c), device_id_type=MESH).start()

    out = pl.pallas_call(
        body, name=name, in_specs=[HBM] * n + [SEM, SEM] + [ANY] * na,
        out_specs=[SEM, SEM] + [HBM] * n,
        out_shape=[pltpu.SemaphoreType.DMA(()), pltpu.SemaphoreType.DMA(())] + [pltpu.HBM(a.shape, a.dtype) for a in bufs],
        input_output_aliases={w: w + 2 for w in range(n)},
        compiler_params=pltpu.CompilerParams(has_side_effects=EFFECT),
    )(*bufs, send_sem, recv_sem, *after)
    return out[0], out[1], list(out[2:])


def _gather_finish(bufs, send_sem, recv_sem, after, name):
    n, na = len(bufs), len(after)

    def body(*refs):
        send2, recv2 = refs[n], refs[n + 1]
        outs = refs[n + 2 + na:2 * n + 2 + na]
        x, y, c, chips = _place()
        for p, (px, py) in enumerate(chips):
            for w in range(n):
                passed = pltpu.make_async_remote_copy(
                    src_ref=_half_of(outs[w], (2 * px + py,), c), dst_ref=_half_of(outs[w], (2 * px + py,), 1 - c),
                    send_sem=send2, recv_sem=recv2, device_id=(x, y, 1 - c), device_id_type=MESH)
                passed.wait_send()
                passed.wait_recv()

    out = pl.pallas_call(
        body, name=name, in_specs=[HBM] * n + [SEM, SEM] + [ANY] * na, out_specs=[HBM] * n,
        out_shape=[pltpu.HBM(a.shape, a.dtype) for a in bufs],
        input_output_aliases={w: w for w in range(n)},
        compiler_params=pltpu.CompilerParams(has_side_effects=EFFECT),
    )(*bufs, send_sem, recv_sem, *after)
    return list(out)


def _pair_exchange(grads, name):
    n = len(grads)

    def body(*refs):
        ins, outs = refs[:n], refs[n:2 * n]
        send_sem, recv_sem = refs[2 * n:]
        x, y, c, _ = _place()
        copies = []
        for w in range(n):
            copies.append(pltpu.make_async_remote_copy(
                src_ref=_half_of(ins[w], (slice(None),), 1 - c), dst_ref=outs[w], send_sem=send_sem.at[w], recv_sem=recv_sem.at[w],
                device_id=(x, y, 1 - c), device_id_type=MESH))
        for cp in copies:
            cp.start()
        for cp in copies:
            cp.wait()

    return pl.pallas_call(
        body, in_specs=[ANY] * n, out_specs=[ANY] * n,
        out_shape=[jax.ShapeDtypeStruct(_half_shape(a.shape), a.dtype) for a in grads],
        scratch_shapes=[pltpu.SemaphoreType.DMA((n,))] * 2,
        compiler_params=pltpu.CompilerParams(has_side_effects=True), name=name,
    )(*grads)


def _pair_sum(g, got, c_arr, name):
    _, r, cols = g.shape
    hr, hc = _half_shape((r, cols))
    tr, tc = _block_of(hr, hc)
    nbr, nbc = hr // tr, hc // tc
    by_cols = _by_cols(r)

    def body(c_ref, g_ref, got_ref, o_ref):
        o_ref[...] = (g_ref[...].astype(F32) + got_ref[...].astype(F32)).astype(o_ref.dtype)

    def mine(j, i, k, c_ref):
        return (j, i, c_ref[0] * nbc + k) if by_cols else (j, c_ref[0] * nbr + i, k)

    return pl.pallas_call(
        body,
        grid_spec=pltpu.PrefetchScalarGridSpec(
            num_scalar_prefetch=1, grid=(4, nbr, nbc),
            in_specs=[pl.BlockSpec((None, tr, tc), mine),
                      pl.BlockSpec((None, tr, tc), lambda j, i, k, c_ref: (j, i, k))],
            out_specs=pl.BlockSpec((None, tr, tc), lambda j, i, k, c_ref: (j, i, k))),
        out_shape=jax.ShapeDtypeStruct((4, hr, hc), BF16),
        compiler_params=_params(("parallel", "parallel", "parallel")), name=name,
    )(c_arr, g, got)


def _chip_exchange(parts, name):
    n = len(parts)

    def body(*refs):
        ins, outs = refs[:n], refs[n:2 * n]
        send_sem, recv_sem = refs[2 * n:]
        x, y, c, chips = _place()
        copies = []
        for p, (px, py) in enumerate(chips):
            for w in range(n):
                copies.append(pltpu.make_async_remote_copy(
                    src_ref=ins[w].at[2 * px + py], dst_ref=outs[w].at[p], send_sem=send_sem.at[w, p], recv_sem=recv_sem.at[w, p],
                    device_id=(px, py, c), device_id_type=MESH))
        for cp in copies:
            cp.start()
        for cp in copies:
            cp.wait()

    return pl.pallas_call(
        body, in_specs=[ANY] * n, out_specs=[ANY] * n,
        out_shape=[jax.ShapeDtypeStruct((3,) + a.shape[1:], a.dtype) for a in parts],
        scratch_shapes=[pltpu.SemaphoreType.DMA((n, 3))] * 2,
        compiler_params=pltpu.CompilerParams(has_side_effects=True), name=name,
    )(*parts)


def _chip_exchange_start(parts, after, name):
    n, na = len(parts), len(after)
    lands = [lax.empty((3,) + a.shape[1:], a.dtype) for a in parts]

    def body(*refs):
        send_sem, recv_sem = refs[2 * n + na], refs[2 * n + na + 1]
        srcs = refs[2 * n + na + 2:3 * n + na + 2]
        dsts = refs[3 * n + na + 2:4 * n + na + 2]
        token = refs[4 * n + na + 2]
        x, y, c, chips = _place()
        for p, (px, py) in enumerate(chips):
            for w in range(n):
                pltpu.make_async_remote_copy(
                    src_ref=srcs[w].at[2 * px + py], dst_ref=dsts[w].at[p], send_sem=send_sem, recv_sem=recv_sem,
                    device_id=(px, py, c), device_id_type=MESH).start()
        token[...] = jnp.zeros_like(token)

    out = pl.pallas_call(
        body, name=name, in_specs=[HBM] * (2 * n) + [ANY] * na,
        out_specs=[SEM, SEM] + [HBM] * (2 * n) + [pl.BlockSpec(memory_space=pltpu.VMEM)],
        out_shape=[pltpu.SemaphoreType.DMA(()), pltpu.SemaphoreType.DMA(())]
        + [pltpu.HBM(a.shape, a.dtype) for a in parts + lands] + [jax.ShapeDtypeStruct((8, LANE), F32)],
        input_output_aliases={w: w + 2 for w in range(2 * n)},
        compiler_params=pltpu.CompilerParams(has_side_effects=EFFECT),
    )(*_in_hbm(parts), *_in_hbm(lands), *after)
    return out[0], out[1], list(out[2:2 + n]), list(out[2 + n:2 + 2 * n]), out[2 + 2 * n]


def _chip_exchange_finish(parts, lands, send_sem, recv_sem, after, name):
    n, na = len(parts), len(after)

    def body(*refs):
        send, recv = refs[2 * n], refs[2 * n + 1]
        srcs = refs[2 * n + 2 + na:3 * n + 2 + na]
        dsts = refs[3 * n + 2 + na:4 * n + 2 + na]
        x, y, c, chips = _place()
        for p, (px, py) in enumerate(chips):
            for w in range(n):
                copy = pltpu.make_async_remote_copy(
                    src_ref=srcs[w].at[2 * px + py], dst_ref=dsts[w].at[p], send_sem=send, recv_sem=recv,
                    device_id=(px, py, c), device_id_type=MESH)
                copy.wait_send()
                copy.wait_recv()

    out = pl.pallas_call(
        body, name=name, in_specs=[HBM] * (2 * n) + [SEM, SEM] + [ANY] * na, out_specs=[HBM] * (2 * n),
        out_shape=[pltpu.HBM(a.shape, a.dtype) for a in parts + lands],
        input_output_aliases={w: w for w in range(2 * n)},
        compiler_params=pltpu.CompilerParams(has_side_effects=EFFECT),
    )(*parts, *lands, send_sem, recv_sem, *after)
    return list(out[:n]), list(out[n:])


def _chip_sum(part, got, place_arr, name):
    _, hr, hc = part.shape
    by_cols = _by_cols(hr)
    tr, tc = _block_of(hr, hc)
    nbr, nbc = hr // tr, hc // tc

    def body(place_ref, p_ref, got_ref, o_ref):
        acc = p_ref[...].astype(F32)
        for p in range(3):
            acc = acc + got_ref[p].astype(F32)
        o_ref[...] = acc

    def mine(i, k, place_ref):
        return (i, place_ref[1] * nbc + k) if by_cols else (place_ref[1] * nbr + i, k)

    return pl.pallas_call(
        body,
        grid_spec=pltpu.PrefetchScalarGridSpec(
            num_scalar_prefetch=1, grid=(nbr, nbc),
            in_specs=[pl.BlockSpec((None, tr, tc), lambda i, k, place_ref: (place_ref[0], i, k)),
                      pl.BlockSpec((3, tr, tc), lambda i, k, place_ref: (0, i, k))],
            out_specs=pl.BlockSpec((tr, tc), mine)),
        out_shape=jax.ShapeDtypeStruct((hr, 2 * hc) if by_cols else (2 * hr, hc), F32),
        compiler_params=_params(("parallel", "parallel")), name=name,
    )(place_arr, part, got)


def _pair_join(bufs, name):
    n = len(bufs)

    def body(*refs):
        outs = refs[n:2 * n]
        send_sem, recv_sem = refs[2 * n:]
        x, y, c, _ = _place()
        copies = []
        for w in range(n):
            block = _half_of(outs[w], (), c)
            copies.append(pltpu.make_async_remote_copy(
                src_ref=block, dst_ref=block, send_sem=send_sem.at[w], recv_sem=recv_sem.at[w],
                device_id=(x, y, 1 - c), device_id_type=MESH))
        for cp in copies:
            cp.start()
        for w, cp in enumerate(copies):
            cp.wait_send()
            block = _half_of(outs[w], (), 1 - c)
            pltpu.make_async_remote_copy(
                src_ref=block, dst_ref=block, send_sem=send_sem.at[w], recv_sem=recv_sem.at[w],
                device_id=(x, y, 1 - c), device_id_type=MESH).wait_recv()

    return pl.pallas_call(
        body, in_specs=[ANY] * n, out_specs=[ANY] * n,
        out_shape=[jax.ShapeDtypeStruct(a.shape, a.dtype) for a in bufs],
        input_output_aliases={w: w for w in range(n)},
        scratch_shapes=[pltpu.SemaphoreType.DMA((n,))] * 2,
        compiler_params=pltpu.CompilerParams(has_side_effects=True), name=name,
    )(*bufs)


SMALL = ("ffn1_pre_g", "ffn1_post_g", "mix_pre_g", "gla_norm_g", "mem_norm_g", "mix_post_g", "ffn2_pre_g", "ffn2_post_g", "final_g",
         "b_f", "pool_scale", "w_pool", "w_fu")
N_GAINS = 9
SMALL_PACKS = ((16, D), (24, 512), (4 * LANE, LANE))
W_FU_ROW = 8


def _all_sum_small(gs, name):
    ins = [gs[n] for n in SMALL[:N_GAINS]] + [gs["b_f"], gs["pool_scale"], gs["w_fu_pad"], gs["w_pool"].reshape(4 * LANE, LANE)]

    def body(*refs):
        gain_refs = refs[:N_GAINS]
        bf_ref, ps_ref, wfu_ref, wp_ref = refs[N_GAINS:N_GAINS + 4]
        outs = refs[N_GAINS + 4:N_GAINS + 7]
        mine_a, mine_b, all_a, all_b, all_c, send_sems, recv_sems = refs[N_GAINS + 7:]
        mine_a[...] = jnp.zeros_like(mine_a)
        for i, ref in enumerate(gain_refs):
            mine_a[i:i + 1, :] = ref[...]
        mine_b[...] = jnp.zeros_like(mine_b)
        mine_b[0:1, :] = bf_ref[...]
        mine_b[1:2, :] = ps_ref[...]
        mine_b[W_FU_ROW:W_FU_ROW + GATE_RANK, :] = wfu_ref[0:GATE_RANK, :]
        packs = ((mine_a, all_a), (mine_b, all_b), (wp_ref, all_c))
        x, y, c, chips = _place()
        me, sibling = (x, y, c), (x, y, 1 - c)

        def copy(t, k, block, to, own=False):
            px, py, pc = block
            slot = packs[t][1].at[4 * px + 2 * py + pc]
            return pltpu.make_async_remote_copy(
                src_ref=packs[t][0] if own else slot, dst_ref=slot,
                send_sem=send_sems.at[t, k], recv_sem=recv_sems.at[t, k], device_id=to, device_id_type=MESH)

        started = []
        for t, (mine, everyone) in enumerate(packs):
            everyone[4 * x + 2 * y + c] = mine[...]
            started.append(copy(t, 0, me, sibling, own=True))
            started += [copy(t, 1 + j, me, (*chip, c), own=True) for j, chip in enumerate(chips)]
        for cp in started:
            cp.start()
        passed = []
        for j, chip in enumerate(chips):
            for t in range(len(packs)):
                copy(t, 1 + j, (*chip, c), me).wait_recv()
                fwd = copy(t, 4 + j, (*chip, c), sibling)
                fwd.start()
                passed.append(fwd)
        for t in range(len(packs)):
            copy(t, 0, sibling, me).wait_recv()
            for j, chip in enumerate(chips):
                copy(t, 4 + j, (*chip, 1 - c), me).wait_recv()
        for cp in started + passed:
            cp.wait_send()
        for (_, everyone), o_ref in zip(packs, outs):
            acc = everyone[0]
            for k in range(1, 8):
                acc = acc + everyone[k]
            o_ref[...] = acc

    vmem = pl.BlockSpec(memory_space=pltpu.VMEM)
    return pl.pallas_call(
        body, in_specs=[vmem] * len(ins), out_specs=[vmem] * 3,
        out_shape=[jax.ShapeDtypeStruct(shape, F32) for shape in SMALL_PACKS],
        scratch_shapes=[pltpu.VMEM(SMALL_PACKS[0], F32), pltpu.VMEM(SMALL_PACKS[1], F32)]
        + [pltpu.VMEM((8,) + shape, F32) for shape in SMALL_PACKS]
        + [pltpu.SemaphoreType.DMA((3, 7)), pltpu.SemaphoreType.DMA((3, 7))],
        compiler_params=pltpu.CompilerParams(has_side_effects=True, vmem_limit_bytes=VMEM_LIMIT), name=name,
    )(*ins)


def _adamw_small(sums, params, chip_arr, name):
    flat = [a for n in SMALL for a in params[n]]

    def body(chip_ref, a_ref, b_ref, c_ref, *refs):
        ins, outs = refs[:len(flat)], refs[len(flat):]
        for i, n in enumerate(SMALL):
            w_ref, m_ref, v_ref = ins[3 * i:3 * i + 3]
            g_ref, d_ref, mo_ref, vo_ref = outs[4 * i:4 * i + 4]
            if n == "w_pool":
                pieces = [((0, k), c_ref[k * LANE:(k + 1) * LANE, :]) for k in range(4)]
            elif n == "w_fu":
                mine = pl.ds(pl.multiple_of(chip_ref[0] * LANE, LANE), LANE)
                pieces = [((0,), b_ref[W_FU_ROW:W_FU_ROW + GATE_RANK, mine])]
            elif n == "b_f":
                pieces = [((), b_ref[0:1, :])]
            elif n == "pool_scale":
                pieces = [((), b_ref[1:2, :])]
            else:
                pieces = [((), a_ref[i:i + 1, :])]
            for at, g in pieces:
                d, mn, vn = _adam_math(w_ref[at], g, m_ref[at], v_ref[at])
                g_ref[at] = g
                d_ref[at] = d
                mo_ref[at] = mn
                vo_ref[at] = vn

    def whole(shape):
        return pl.BlockSpec(shape, lambda i, chip_ref: (0,) * len(shape))

    out = pl.pallas_call(
        body,
        grid_spec=pltpu.PrefetchScalarGridSpec(
            num_scalar_prefetch=1, grid=(1,),
            in_specs=[whole(a.shape) for a in list(sums) + flat],
            out_specs=[whole(params[n][0].shape) for n in SMALL for _ in range(4)]),
        out_shape=[jax.ShapeDtypeStruct(params[n][0].shape, F32) for n in SMALL for _ in range(4)],
        compiler_params=_params(("arbitrary",)), name=name,
    )(chip_arr, *sums, *flat)
    return {n: tuple(out[4 * i:4 * i + 4]) for i, n in enumerate(SMALL)}


def _ffn_fwd(x_norm, w_in, w_out, tag, between=None):
    ab, u = _ffn_in_swiglu(x_norm, w_in, name=tag + "_in")
    behind = (between(u) if between is not None else None) or ()
    f = _mm(u, w_out, tm=1024, tk=DFF, after=behind, name=tag + "_out")
    return ab, u, f


def _ffn_bwd(dz, x_norm, ab, u, w_in, w_out, tag, emit, after=()):
    dw_out = _mm(u, dz, ta=True, out_dtype=BF16, tm=1408, tk=2048, after=after, name=tag + "_out_dw")
    behind = emit(tag + "_w_out", dw_out)
    dab = _ffn_out_dx_swiglu(dz, w_out, ab, behind, name=tag + "_out_dx")
    dw_in = _mm(x_norm, dab, ta=True, out_dtype=BF16, tm=512, tk=4096, shards=4, name=tag + "_in_dw")
    behind = emit(tag + "_w_in", dw_in)
    return _mm(dab, w_in, tb=True, tm=1024, after=behind, name=tag + "_in_dx")


def _local_step(x, mem, target, small, gather, emit):
    big, behind = gather("ffn1", ())
    h1 = _norm_fwd(x, small["ffn1_pre_g"], BF16, name="ffn1_pre", after=behind)
    ab1, u1, f1 = _ffn_fwd(h1, big["ffn1_w_in"], big["ffn1_w_out"], "ffn1", between=lambda u: gather("mix_pass", (u,)))
    more, behind = gather("mix", (f1,))
    big.update(more)
    small = dict(small, w_fu_pad=big["w_fu_pad"])
    x1, h = _resid_norm_fwd(x, f1, small["ffn1_post_g"], 0.5, small["mix_pre_g"], name="ffn1_post")
    pg = _mm(h, big["w_gla_t"], tb=True, out_dtype=BF16, tm=1024, tn=PG_W, after=behind, name="mix_in_gla")
    ppx = _mm(h, big["w_px_t"], tb=True, out_dtype=BF16, name="mix_in_px")
    pgt = _mm(h, big["w_gates_t"], tb=True, out_dtype=BF16, tn=1536, name="mix_in_gates")
    mem_n = _norm_fwd(mem, small["mem_norm_g"], BF16, name="mem_norm")
    kv = _mm(mem_n, big["w_mem_kv"], out_dtype=BF16, name="mem_kv")
    ya_in, sp = _gla_fwd(pg, small["w_fu_pad"], small["b_f"], small["gla_norm_g"], name="gla_fwd")
    yb_in = _pool_fwd(ppx, small["w_pool_b"], small["pool_scale"], name="pool_fwd")
    xc = _xattn_fwd(ppx, kv, name="xattn_fwd")
    ya = _mm(ya_in, big["w_up_gla"], out_dtype=BF16, name="up_gla")
    yb = _mm(yb_in, big["w_up_pool"], out_dtype=BF16, name="up_pool")
    yc = _mm(xc, big["w_up_xattn"], out_dtype=BF16, name="up_xattn")
    merged = _merge_fwd(pgt, ya, yb, yc, name="merge_fwd")
    gather("ffn2_pass", (merged,))
    ymix = _mm(merged, big["w_o"], name="mix_out")
    more, _ = gather("ffn2", (ymix,))
    big.update(more)
    x2, h2 = _resid_norm_fwd(x1, ymix, small["mix_post_g"], 1.0, small["ffn2_pre_g"], name="mix_post")
    ab2, u2, f2 = _ffn_fwd(h2, big["ffn2_w_in"], big["ffn2_w_out"], "ffn2")
    x3, _ = _resid_norm_fwd(x2, f2, small["ffn2_post_g"], 0.5, None, name="ffn2_post")
    gs = {}
    dx3, gs["final_g"], loss = _loss_bwd(x3, small["final_g"], target, name="loss")
    dz2, gs["ffn2_post_g"] = _rms_bwd(f2, small["ffn2_post_g"], [dx3], None, 0.5, BF16, name="ffn2_post_bwd")
    dh2 = _ffn_bwd(dz2, h2, ab2, u2, big["ffn2_w_in"], big["ffn2_w_out"], "ffn2", emit)
    dx2, gs["ffn2_pre_g"] = _rms_bwd(x2, small["ffn2_pre_g"], [dh2], dx3, 1.0, F32, name="ffn2_pre_bwd")
    dy, gs["mix_post_g"] = _rms_bwd(ymix, small["mix_post_g"], [dx2], None, 1.0, BF16, name="mix_post_bwd")
    dmerged = _mm(dy, big["w_o"], tb=True, out_dtype=BF16, name="mix_out_dx")
    emit("w_o", _mm(merged, dy, ta=True, out_dtype=BF16, tm=512, tk=4096, name="mix_out_dw"))
    dya, dyb, dyc, dgt = _merge_bwd(dmerged, pgt, ya, yb, yc, name="merge_bwd")
    dya_in = _mm(dya, big["w_up_gla"], tb=True, out_dtype=BF16, name="up_gla_dx")
    emit("w_up_gla", _mm(ya_in, dya, ta=True, out_dtype=BF16, tm=512, tk=4096, name="up_gla_dw"))
    dyb_in = _mm(dyb, big["w_up_pool"], tb=True, out_dtype=BF16, name="up_pool_dx")
    emit("w_up_pool", _mm(yb_in, dyb, ta=True, out_dtype=BF16, tm=512, tk=4096, shards=4, name="up_pool_dw"))
    dxc = _mm(dyc, big["w_up_xattn"], tb=True, out_dtype=BF16, name="up_xattn_dx")
    emit("w_up_xattn", _mm(xc, dyc, ta=True, out_dtype=BF16, tm=512, tk=4096, shards=4, name="up_xattn_dw"))
    dpg, gs["w_fu_pad"], gs["b_f"], gs["gla_norm_g"] = _gla_bwd(pg, sp, dya_in, small["w_fu_pad"], small["b_f"], small["gla_norm_g"], name="gla_bwd")
    dp, gs["w_pool"], gs["pool_scale"] = _pool_bwd(dyb_in, ppx, small["w_pool_b"], small["pool_scale"], name="pool_bwd")
    dxq, dkv = _xattn_bwd(dxc, ppx, kv, name="xattn_bwd")
    dkv = dkv.astype(BF16)
    emit("w_mem_kv", _mm(mem_n, dkv, ta=True, out_dtype=BF16, name="mem_kv_dw"))
    dmem_n = _mm(dkv, big["w_mem_kv"], tb=True, name="mem_kv_dx")
    _, gs["mem_norm_g"] = _rms_bwd(mem, small["mem_norm_g"], [dmem_n], None, 1.0, BF16, name="mem_norm_bwd")
    emit("w_gla", _mm(dpg, h, ta=True, out_dtype=BF16, tm=640, tk=4096, name="mix_in_gla_dw"))
    emit("w_p", _mm(dp, h, ta=True, out_dtype=BF16, tm=512, tk=4096, name="mix_in_p_dw"))
    emit("w_xq", _mm(dxq, h, ta=True, out_dtype=BF16, tm=512, tk=4096, name="mix_in_xq_dw"))
    behind = emit("w_gates", _mm(dgt, h, ta=True, out_dtype=BF16, tm=512, tk=4096, name="mix_in_gates_dw"))
    dh_parts = [
        _mm(dpg, big["w_gla_t"], tm=1024, tk=PG_W, after=behind, name="mix_in_gla_dx"),
        _mm(dp, big["w_p_t"], name="mix_in_p_dx"),
        _mm(dxq, big["w_xq_t"], name="mix_in_xq_dx"),
        _mm(dgt, big["w_gates_t"], tm=1024, tk=3072, name="mix_in_gates_dx"),
    ]
    dx1, gs["mix_pre_g"] = _rms_bwd(x1, small["mix_pre_g"], dh_parts, dx2, 1.0, F32, name="mix_pre_bwd")
    dz1, gs["ffn1_post_g"] = _rms_bwd(f1, small["ffn1_post_g"], [dx1], None, 0.5, BF16, name="ffn1_post_bwd")
    dh1 = _ffn_bwd(dz1, h1, ab1, u1, big["ffn1_w_in"], big["ffn1_w_out"], "ffn1", emit)
    dx0, gs["ffn1_pre_g"] = _rms_bwd(x, small["ffn1_pre_g"], [dh1], dx1, 1.0, F32, name="ffn1_pre_bwd")
    return loss, dx0, gs


BIG = ("ffn1_w_in", "ffn1_w_out", "w_in", "w_mem_kv", "w_up_gla", "w_up_pool", "w_up_xattn", "w_o", "ffn2_w_in", "ffn2_w_out")
COL_SHARDED = ("ffn1_w_in", "w_in", "w_up_pool", "w_up_xattn", "ffn2_w_in")
GATHER_GROUPS = {"ffn1": ("ffn1_w_in", "ffn1_w_out"),
                 "mix": ("w_in", "w_mem_kv", "w_up_gla", "w_up_pool", "w_up_xattn", "w_o", "w_fu"),
                 "ffn2": ("ffn2_w_in", "ffn2_w_out")}
REDUCE_GROUPS = {"ffn2": ("ffn2_w_out", "ffn2_w_in"),
                 "mix": ("w_o", "w_up_gla", "w_up_pool", "w_up_xattn", "w_mem_kv", "w_gla", "w_p", "w_xq", "w_gates"),
                 "ffn1_out": ("ffn1_w_out",),
                 "ffn1_in": ("ffn1_w_in",)}
GAINS = ("ffn1_pre_g", "ffn1_post_g", "mix_pre_g", "gla_norm_g", "mem_norm_g", "mix_post_g", "ffn2_pre_g", "ffn2_post_g", "final_g")
WEIGHTS = ("ffn1_pre_g", "ffn1_w_in", "ffn1_w_out", "ffn1_post_g", "mix_pre_g", "w_in", "w_fu", "b_f", "gla_norm_g", "w_pool",
           "pool_scale", "mem_norm_g", "w_mem_kv", "w_up_gla", "w_up_pool", "w_up_xattn", "w_o", "mix_post_g", "ffn2_pre_g",
           "ffn2_w_in", "ffn2_w_out", "ffn2_post_g", "final_g")
IN_GLA, IN_F, IN_PX, IN_GATES, IN_END = 0, 3072, 3088, 4112, 7184
def _cols_from_shards(g):
    return jnp.transpose(g, (1, 0, 2)).reshape(g.shape[1], 4 * g.shape[2])


def kernel(x, mem, ffn1_pre_g, ffn1_w_in, ffn1_w_out, ffn1_post_g, mix_pre_g, w_in, w_fu, b_f, gla_norm_g, w_pool, pool_scale, mem_norm_g, w_mem_kv, w_up_gla, w_up_pool, w_up_xattn, w_o, mix_post_g, ffn2_pre_g, ffn2_w_in, ffn2_w_out, ffn2_post_g, final_g, loss_target, m_ffn1_pre_g, m_ffn1_w_in, m_ffn1_w_out, m_ffn1_post_g, m_mix_pre_g, m_w_in, m_w_fu, m_b_f, m_gla_norm_g, m_w_pool, m_pool_scale, m_mem_norm_g, m_w_mem_kv, m_w_up_gla, m_w_up_pool, m_w_up_xattn, m_w_o, m_mix_post_g, m_ffn2_pre_g, m_ffn2_w_in, m_ffn2_w_out, m_ffn2_post_g, m_final_g, v_ffn1_pre_g, v_ffn1_w_in, v_ffn1_w_out, v_ffn1_post_g, v_mix_pre_g, v_w_in, v_w_fu, v_b_f, v_gla_norm_g, v_w_pool, v_pool_scale, v_mem_norm_g, v_w_mem_kv, v_w_up_gla, v_w_up_pool, v_w_up_xattn, v_w_o, v_mix_post_g, v_ffn2_pre_g, v_ffn2_w_in, v_ffn2_w_out, v_ffn2_post_g, v_final_g):
    args = dict(locals())
    w = {n: args[n][0] for n in WEIGHTS}
    m = {n: args["m_" + n][0] for n in WEIGHTS}
    v = {n: args["v_" + n][0] for n in WEIGHTS}
    xi, yi, ci = lax.axis_index("x"), lax.axis_index("y"), lax.axis_index("c")
    chip = 2 * xi + yi

    c_arr = jnp.reshape(ci, (1,)).astype(jnp.int32)
    chip_arr = jnp.reshape(chip, (1,)).astype(jnp.int32)
    place_arr = jnp.stack([chip, ci]).astype(jnp.int32)
    shard_of = {n: (jnp.transpose(args[n][0])[None] if n == "w_in" else args[n]) for n in BIG}
    placed = {n: _place_shard(shard_of[n], chip_arr, BF16, name="place_" + n) for n in BIG}
    placed["w_fu"] = _place_shard(args["w_fu"], chip_arr, F32, name="place_w_fu")
    inflight = {}

    def relayout(names, gathered):
        out = {}
        for n, g in zip(names, gathered):
            if n == "w_fu":
                w_fu_full = _cols_from_shards(g)
                out["w_fu_pad"] = jnp.concatenate([w_fu_full, jnp.zeros((LANE - GATE_RANK, 512), F32)], axis=0).astype(BF16)
            elif n == "w_in":
                wt = g.reshape(IN_END, D)
                out["w_gla_t"] = jnp.concatenate([wt[IN_GLA:IN_PX], jnp.zeros((PG_W - IN_PX, D), BF16)], axis=0)
                out["w_px_t"] = wt[IN_PX:IN_GATES]
                out["w_p_t"] = wt[IN_PX:IN_PX + 512]
                out["w_xq_t"] = wt[IN_PX + 512:IN_GATES]
                out["w_gates_t"] = wt[IN_GATES:IN_END]
            else:
                out[n] = _cols_from_shards(g) if n in COL_SHARDED else g.reshape(4 * g.shape[1], g.shape[2])
        return out

    def start(group, after):
        inflight[group] = _gather_start([placed[n] for n in GATHER_GROUPS[group]], after, name="gather_" + group + "_start")

    def gather(step, after):
        group = step.split("_")[0]
        if step.endswith("_pass") or step == "ffn1":
            if step == "ffn1":
                start(group, ())
            send, recv, bufs, _ = inflight[group]
            inflight[group] = _gather_pass(bufs, send, recv, after, name="gather_" + group + "_pass")
            if step != "ffn1":
                return (inflight[group][2][0],)
        send, recv, bufs = inflight.pop(group)
        bufs = _gather_finish(bufs, send, recv, after, name="gather_" + group + "_finish")
        following = {"ffn1": "mix", "mix": "ffn2"}.get(group)
        behind = ()
        if following is not None:
            start(following, (bufs[0],))
            behind = (inflight[following][3],)
        return relayout(GATHER_GROUPS[group], bufs), behind

    small = {n: w[n].reshape(1, D) for n in GAINS}
    small["b_f"] = w["b_f"].reshape(1, 512)
    small["pool_scale"] = w["pool_scale"].reshape(1, 512)
    small["w_pool_b"] = w["w_pool"].astype(BF16)

    pending, travelling = {}, {}

    def emit(name, grad):
        pending[name] = grad
        group = next((g for g, names in REDUCE_GROUPS.items() if name == names[-1]), None)
        if group is None:
            return ()
        gb = {n: pending.pop(n) for n in REDUCE_GROUPS[group]}
        if group == "mix":
            dwt = jnp.concatenate([gb.pop("w_gla")[0:IN_PX], gb.pop("w_p"), gb.pop("w_xq"), gb.pop("w_gates")], axis=0)
            gb["w_in"] = dwt.reshape(4, IN_END // 4, D)
        names = list(gb)
        contrib = [gb[n] if n in COL_SHARDED else gb[n].reshape(4, gb[n].shape[0] // 4, gb[n].shape[1]) for n in names]
        from_sibling = _pair_exchange(contrib, name="grads_" + group + "_pair_exchange")
        pair = [_pair_sum(g, got, c_arr, name="grads_pair_sum_" + n) for n, g, got in zip(names, contrib, from_sibling)]
        send, recv, pair, lands, token = _chip_exchange_start(pair, (), name="grads_" + group + "_chip_start")
        travelling[group] = (names, send, recv, pair, lands)
        return (token,)

    loss, grad_x, gs = _local_step(x[0], mem[0], loss_target[0], small, gather, emit)
    loss = lax.psum(loss[0, 0], ("x", "y", "c"))

    small_sums = _all_sum_small(gs, name="sum_small_grads")
    halves = {}
    for group, (names, send, recv, pair, lands) in travelling.items():
        pair, from_chips = _chip_exchange_finish(pair, lands, send, recv, (grad_x,), name="grads_" + group + "_chip_finish")
        for n, p, got in zip(names, pair, from_chips):
            halves[n] = _chip_sum(p, got, place_arr, name="grads_chip_sum_" + n)
    reduced = dict(zip(BIG, _pair_join([halves[n] for n in BIG], name="grads_pair_join")))

    grads, delta, new_m, new_v = {}, {}, {}, {}
    for n in BIG:
        if n == "w_in":
            transposed = [jnp.transpose(args[k][0]) for k in (n, "m_" + n, "v_" + n)]
            updated = _adamw(transposed[0], reduced[n], transposed[1], transposed[2], name="adamw_" + n)
            grads[n] = jnp.transpose(reduced[n])[None]
            delta[n], new_m[n], new_v[n] = (jnp.transpose(a)[None] for a in updated)
            continue
        grads[n] = reduced[n][None]
        delta[n], new_m[n], new_v[n] = _adamw(args[n], reduced[n], args["m_" + n], args["v_" + n], name="adamw_" + n)
    small_params = {n: (args[n], args["m_" + n], args["v_" + n]) for n in SMALL}
    for n, (g, d, mn, vn) in _adamw_small(small_sums, small_params, chip_arr, name="adamw_small").items():
        grads[n], delta[n], new_m[n], new_v[n] = g, d, mn, vn

    outs = [loss, grad_x[None]]
    for group in (grads, delta, new_m, new_v):
        outs += [group[n] for n in WEIGHTS]
    return tuple(outs)
```

```python
import functools

import jax
import jax.numpy as jnp
from jax import lax
from jax.experimental import pallas as pl
from jax.experimental.pallas import tpu as pltpu

F32 = jnp.float32
BF16 = jnp.bfloat16
MESH = pl.DeviceIdType.MESH
HIGHEST = lax.Precision.HIGHEST

D = 1024
DFF = 2816
CHUNK = 64
HEADS = 4
HDK = 128
HDV = 256
GATE_TEMP = 16.0
POOL_WINDOWS = (2, 4, 8, 16)
POOL_HALO = 16
XA_HEADS = 4
XA_HD = 128
EPS = 1e-6
Q_SCALE = HDK ** -0.5
XA_SCALE = XA_HD ** -0.5
PG_Q, PG_K, PG_V, PG_G, PG_F, PG_W = 0, 512, 1024, 2048, 3072, 3200
GATE_RANK = 16
ADAM_LR, ADAM_B1, ADAM_B2, ADAM_EPS, ADAM_WD, ADAM_STEP = 0.001, 0.9, 0.999, 1e-08, 0.01, 10

VMEM_LIMIT = 48 * 1024 * 1024
LANE = 128
TS_ROW = 512
TS_GLA = 512
TS_POOL = 512
TS_XA = 512


def _params(sem):
    return pltpu.CompilerParams(dimension_semantics=sem, vmem_limit_bytes=VMEM_LIMIT)


def _tile(n, cap, unit=LANE):
    if n <= cap:
        return n
    best = None
    for t in range(unit, cap + 1, unit):
        if n % t == 0:
            best = t
    assert best is not None, (n, cap)
    return best


def _sigmoid(x):
    return 1.0 / (1.0 + jnp.exp(-x))


def _log_sigmoid(x):
    return jnp.minimum(x, 0.0) - jnp.log(1.0 + jnp.exp(-jnp.abs(x)))


def _rms(x):
    r = lax.rsqrt(jnp.mean(x * x, axis=-1, keepdims=True) + EPS)
    return x * r, r


def _rows(ts, w):
    return pl.BlockSpec((ts, w), lambda i: (i, 0))


def _fixed(shape):
    nd = len(shape)
    return pl.BlockSpec(shape, lambda i: (0,) * nd)


def _mm(a, b, *, ta=False, tb=False, out_dtype=F32, tm=2048, tn=1024, tk=1024, shards=1, after=(), name):
    a_blocked, b_blocked = a.ndim == 3, b.ndim == 3
    assert not (a_blocked and ta) and not (b_blocked and tb)
    if a_blocked:
        m, kdim, tk = a.shape[1], a.shape[0] * a.shape[2], a.shape[2]
    else:
        m, kdim = (a.shape[1], a.shape[0]) if ta else a.shape
    if b_blocked:
        n, tn = b.shape[0] * b.shape[2], b.shape[2]
        assert b.shape[1] == kdim and shards in (1, b.shape[0])
    else:
        n = b.shape[0] if tb else b.shape[1]
        assert (b.shape[1] if tb else b.shape[0]) == kdim, (a.shape, b.shape, ta, tb)
        tn = n // shards if shards > 1 else _tile(n, tn)
    tm = _tile(m, tm)
    tk = tk if a_blocked else _tile(kdim, tk)
    kgroup = 2 if (a_blocked and tb and a.shape[0] % 2 == 0) else 1
    nk = kdim // (tk * kgroup)
    dims = (((0 if ta else 1,), (1 if tb else 0,)), ((), ()))

    def body(a_ref, b_ref, *rest):
        o_ref, *acc = rest[len(after):]
        if kgroup == 1:
            part = lax.dot_general(a_ref[...], b_ref[...], dims, preferred_element_type=F32)
        else:
            part = sum(lax.dot_general(a_ref[g], b_ref[:, g * tk:(g + 1) * tk], dims, preferred_element_type=F32) for g in range(kgroup))
        if nk == 1:
            o_ref[...] = part.astype(o_ref.dtype)
            return
        acc_ref, = acc
        k = pl.program_id(2)

        @pl.when(k == 0)
        def _():
            acc_ref[...] = part

        @pl.when(k > 0)
        def _():
            acc_ref[...] += part

        @pl.when(k == nk - 1)
        def _():
            o_ref[...] = acc_ref[...].astype(o_ref.dtype)

    if a_blocked and kgroup > 1:
        a_spec = pl.BlockSpec((kgroup, tm, tk), lambda i, j, k: (k, i, 0))
    elif a_blocked:
        a_spec = pl.BlockSpec((None, tm, tk), lambda i, j, k: (k, i, 0))
    else:
        a_spec = pl.BlockSpec((tk, tm), lambda i, j, k: (k, i)) if ta else pl.BlockSpec((tm, tk), lambda i, j, k: (i, k))
    if b_blocked:
        b_spec = pl.BlockSpec((None, tk, tn), lambda i, j, k: (j, k, 0))
    else:
        b_spec = pl.BlockSpec((tn, tk * kgroup), lambda i, j, k: (j, k)) if tb else pl.BlockSpec((tk, tn), lambda i, j, k: (k, j))
    if shards > 1:
        out_shape = jax.ShapeDtypeStruct((shards, m, tn), out_dtype)
        o_spec = pl.BlockSpec((None, tm, tn), lambda i, j, k: (j, i, 0))
    else:
        out_shape = jax.ShapeDtypeStruct((m, n), out_dtype)
        o_spec = pl.BlockSpec((tm, tn), lambda i, j, k: (i, j))
    return pl.pallas_call(
        body, grid=(m // tm, n // tn, nk), in_specs=[a_spec, b_spec] + [ANY] * len(after), out_specs=o_spec, out_shape=out_shape,
        scratch_shapes=[pltpu.VMEM((tm, tn), F32)] if nk > 1 else [],
        compiler_params=_params(("parallel", "parallel", "arbitrary")), name=name,
    )(a, b, *after)


def _norm_fwd(x, g, out_dtype, name, after=()):
    s, d = x.shape
    ts = _tile(s, TS_ROW, 8)

    def body(x_ref, g_ref, *rest):
        o_ref = rest[len(after)]
        xh, _ = _rms(x_ref[...])
        o_ref[...] = (xh * g_ref[...]).astype(o_ref.dtype)

    return pl.pallas_call(
        body, grid=(s // ts,), in_specs=[_rows(ts, d), _fixed((1, d))] + [ANY] * len(after), out_specs=_rows(ts, d),
        out_shape=jax.ShapeDtypeStruct((s, d), out_dtype), compiler_params=_params(("parallel",)), name=name,
    )(x, g, *after)


def _resid_norm_fwd(x, f, g_post, alpha, g_next, name, after=()):
    s, d = x.shape
    ts = _tile(s, TS_ROW, 8)
    with_h = g_next is not None

    def body(x_ref, f_ref, gp_ref, *rest):
        rest = rest[:1] + rest[1 + len(after):] if with_h else rest[len(after):]
        fh, _ = _rms(f_ref[...])
        xn = x_ref[...] + alpha * (fh * gp_ref[...])
        if with_h:
            gn_ref, xo_ref, h_ref = rest
            xh, _ = _rms(xn)
            h_ref[...] = (xh * gn_ref[...]).astype(h_ref.dtype)
        else:
            xo_ref, = rest
        xo_ref[...] = xn

    ins = [x, f, g_post] + ([g_next] if with_h else []) + list(after)
    in_specs = [_rows(ts, d), _rows(ts, d), _fixed((1, d))] + ([_fixed((1, d))] if with_h else []) + [ANY] * len(after)
    out_shape = [jax.ShapeDtypeStruct((s, d), F32)] + ([jax.ShapeDtypeStruct((s, d), BF16)] if with_h else [])
    out_specs = [_rows(ts, d)] + ([_rows(ts, d)] if with_h else [])
    out = pl.pallas_call(
        body, grid=(s // ts,), in_specs=in_specs, out_specs=out_specs, out_shape=out_shape,
        compiler_params=_params(("parallel",)), name=name,
    )(*ins)
    return (out[0], out[1]) if with_h else (out[0], None)


def _rms_bwd(x, g, dys, dres, alpha, out_dtype, name):
    s, d = x.shape
    ts = _tile(s, TS_ROW, 8)
    ndy = len(dys)
    with_res = dres is not None

    def body(x_ref, g_ref, *rest):
        dy_refs = rest[:ndy]
        rest = rest[ndy:]
        if with_res:
            dres_ref, dx_ref, dg_ref = rest
        else:
            dx_ref, dg_ref = rest
        xh, r = _rms(x_ref[...])
        dy = dy_refs[0][...].astype(F32)
        for ref in dy_refs[1:]:
            dy = dy + ref[...].astype(F32)
        dy = dy * alpha

        @pl.when(pl.program_id(0) == 0)
        def _():
            dg_ref[...] = jnp.zeros_like(dg_ref)

        dg_ref[...] += jnp.sum(dy * xh, axis=0, keepdims=True)
        dyg = dy * g_ref[...]
        dx = r * (dyg - xh * jnp.mean(dyg * xh, axis=-1, keepdims=True))
        if with_res:
            dx = dx + dres_ref[...]
        dx_ref[...] = dx.astype(dx_ref.dtype)

    ins = [x, g] + list(dys) + ([dres] if with_res else [])
    in_specs = [_rows(ts, d), _fixed((1, d))] + [_rows(ts, d)] * (ndy + int(with_res))
    return pl.pallas_call(
        body, grid=(s // ts,), in_specs=in_specs, out_specs=[_rows(ts, d), _fixed((1, d))],
        out_shape=[jax.ShapeDtypeStruct((s, d), out_dtype), jax.ShapeDtypeStruct((1, d), F32)],
        compiler_params=_params(("arbitrary",)), name=name,
    )(*ins)


def _loss_bwd(x, g, target, name):
    s, d = x.shape
    ts = _tile(s, TS_ROW, 8)

    def body(x_ref, g_ref, t_ref, dx_ref, dg_ref, loss_ref):
        xh, r = _rms(x_ref[...])
        gv = g_ref[...]
        diff = xh * gv - t_ref[...]

        @pl.when(pl.program_id(0) == 0)
        def _():
            dg_ref[...] = jnp.zeros_like(dg_ref)
            loss_ref[...] = jnp.zeros_like(loss_ref)

        sq = jnp.sum(diff * diff, axis=1, keepdims=True)
        loss_ref[...] += (0.5 / d) * jnp.sum(sq, axis=0, keepdims=True)
        dy = diff * (1.0 / d)
        dg_ref[...] += jnp.sum(dy * xh, axis=0, keepdims=True)
        dyg = dy * gv
        dx_ref[...] = r * (dyg - xh * jnp.mean(dyg * xh, axis=-1, keepdims=True))

    return pl.pallas_call(
        body, grid=(s // ts,), in_specs=[_rows(ts, d), _fixed((1, d)), _rows(ts, d)],
        out_specs=[_rows(ts, d), _fixed((1, d)), _fixed((8, LANE))],
        out_shape=[jax.ShapeDtypeStruct((s, d), F32), jax.ShapeDtypeStruct((1, d), F32), jax.ShapeDtypeStruct((8, LANE), F32)],
        compiler_params=_params(("arbitrary",)), name=name,
    )(x, g, target)


HALF_FF = DFF // 2


def _ffn_in_swiglu(x_norm, w_in, name, tm=1024):
    s, d = x_norm.shape
    tm = _tile(s, tm)

    def body(x_ref, wa_ref, wb_ref, ab_ref, u_ref):
        xv = x_ref[...]
        a = jnp.dot(xv, wa_ref[...], preferred_element_type=F32)
        b = jnp.dot(xv, wb_ref[...], preferred_element_type=F32)
        ab_ref[0] = a.astype(ab_ref.dtype)
        ab_ref[1] = b.astype(ab_ref.dtype)
        u_ref[...] = (a * _sigmoid(a) * b).astype(u_ref.dtype)

    ab, u = pl.pallas_call(
        body, grid=(s // tm, 2),
        in_specs=[pl.BlockSpec((tm, d), lambda i, j: (i, 0)), pl.BlockSpec((d, HALF_FF), lambda i, j: (0, j)),
                  pl.BlockSpec((d, HALF_FF), lambda i, j: (0, 2 + j))],
        out_specs=[pl.BlockSpec((2, None, tm, HALF_FF), lambda i, j: (0, j, i, 0)), pl.BlockSpec((tm, HALF_FF), lambda i, j: (i, j))],
        out_shape=[jax.ShapeDtypeStruct((2, 2, s, HALF_FF), BF16), jax.ShapeDtypeStruct((s, DFF), BF16)],
        compiler_params=_params(("parallel", "parallel")), name=name,
    )(x_norm, w_in, w_in)
    return ab.reshape(4, s, HALF_FF), u


def _ffn_out_dx_swiglu(dz, w_out, ab, after, name, tm=1024):
    s, d = dz.shape
    tm = _tile(s, tm)

    def body(dz_ref, w_ref, ab_ref, *rest):
        dab_ref = rest[len(after)]
        du = lax.dot_general(dz_ref[...], w_ref[...], (((1,), (1,)), ((), ())), preferred_element_type=F32)
        a = ab_ref[0].astype(F32)
        b = ab_ref[1].astype(F32)
        sig = _sigmoid(a)
        dab_ref[0] = (du * b * (sig * (1.0 + a * (1.0 - sig)))).astype(dab_ref.dtype)
        dab_ref[1] = (du * a * sig).astype(dab_ref.dtype)

    halves = pl.BlockSpec((2, None, tm, HALF_FF), lambda i, j: (0, j, i, 0))
    dab = pl.pallas_call(
        body, grid=(s // tm, 2),
        in_specs=[pl.BlockSpec((tm, d), lambda i, j: (i, 0)), pl.BlockSpec((HALF_FF, d), lambda i, j: (j, 0)), halves] + [ANY] * len(after),
        out_specs=halves, out_shape=jax.ShapeDtypeStruct((2, 2, s, HALF_FF), BF16),
        compiler_params=_params(("parallel", "parallel")), name=name,
    )(dz, w_out, ab.reshape(2, 2, s, HALF_FF), *after)
    return dab.reshape(4, s, HALF_FF)


def _tri(strict):
    r = lax.broadcasted_iota(jnp.int32, (CHUNK, CHUNK), 0)
    c = lax.broadcasted_iota(jnp.int32, (CHUNK, CHUNK), 1)
    return (r > c).astype(F32) if strict else (r >= c).astype(F32)


def _gla_fwd(pg, wfu, b_f, gnorm, name):
    s = pg.shape[0]
    ts = _tile(s, TS_GLA, CHUNK)
    cpb = ts // CHUNK
    nc = s // CHUNK

    def body(pg_ref, wfu_ref, bf_ref, gn_ref, ya_ref, sp_ref, st_ref, la_ref, dec_ref, u_ref):
        @pl.when(pl.program_id(0) == 0)
        def _():
            st_ref[...] = jnp.zeros_like(st_ref)

        f = jnp.dot(pg_ref[:, PG_F:PG_W], wfu_ref[...], preferred_element_type=F32) + bf_ref[...]
        la_ref[...] = _log_sigmoid(f) * (1.0 / GATE_TEMP)
        tri = _tri(False)
        chunks = [slice(ci * CHUNK, (ci + 1) * CHUNK) for ci in range(cpb)]
        for ci, rows in enumerate(chunks):
            la = la_ref[rows, :]
            b = jnp.dot(tri, la, precision=HIGHEST, preferred_element_type=F32)
            bend = jnp.sum(la, axis=0, keepdims=True)
            e = jnp.exp(bend - b)
            dec_ref[ci:ci + 1, :] = jnp.exp(bend)
            for hd in range(HEADS):
                k = pg_ref[rows, PG_K + hd * HDK:PG_K + (hd + 1) * HDK]
                v = pg_ref[rows, PG_V + hd * HDV:PG_V + (hd + 1) * HDV]
                kt = (k.astype(F32) * e[:, hd * HDK:(hd + 1) * HDK]).astype(BF16)
                u_ref[ci, hd] = lax.dot_general(v, kt, (((0,), (0,)), ((), ())), preferred_element_type=F32)
        for ci in range(cpb):
            for hd in range(HEADS):
                prev = st_ref[hd]
                sp_ref[ci, hd] = prev
                st = prev * dec_ref[ci:ci + 1, hd * HDK:(hd + 1) * HDK] + u_ref[ci, hd]
                st_ref[hd] = st
                u_ref[ci, hd] = st
        for ci, rows in enumerate(chunks):
            for hd in range(HEADS):
                vc = slice(hd * HDV, (hd + 1) * HDV)
                q = pg_ref[rows, PG_Q + hd * HDK:PG_Q + (hd + 1) * HDK]
                go = pg_ref[rows, PG_G + hd * HDV:PG_G + (hd + 1) * HDV].astype(F32)
                qs = (q.astype(F32) * Q_SCALE).astype(BF16)
                o = lax.dot_general(qs, u_ref[ci, hd].astype(BF16), (((1,), (1,)), ((), ())), preferred_element_type=F32)
                oh, _ = _rms(o)
                ya_ref[rows, vc] = (oh * gn_ref[:, vc] * (go * _sigmoid(go))).astype(ya_ref.dtype)

    return pl.pallas_call(
        body, grid=(s // ts,),
        in_specs=[_rows(ts, PG_W), _fixed((LANE, HEADS * HDK)), _fixed((1, HEADS * HDK)), _fixed((1, HEADS * HDV))],
        out_specs=[_rows(ts, HEADS * HDV), pl.BlockSpec((cpb, HEADS, HDV, HDK), lambda i: (i, 0, 0, 0))],
        out_shape=[jax.ShapeDtypeStruct((s, HEADS * HDV), BF16), jax.ShapeDtypeStruct((nc, HEADS, HDV, HDK), F32)],
        scratch_shapes=[pltpu.VMEM((HEADS, HDV, HDK), F32), pltpu.VMEM((ts, HEADS * HDK), F32),
                        pltpu.VMEM((max(cpb, 8), HEADS * HDK), F32), pltpu.VMEM((cpb, HEADS, HDV, HDK), F32)],
        compiler_params=_params(("arbitrary",)), name=name,
    )(pg, wfu, b_f, gnorm)


def _gla_bwd(pg, sp, dya, wfu, b_f, gnorm, name):
    s = pg.shape[0]
    ts = _tile(s, TS_GLA, CHUNK)
    cpb = ts // CHUNK
    nblk = s // ts

    def body(pg_ref, sp_ref, dya_ref, wfu_ref, bf_ref, gn_ref, dpg_ref, dwfu_ref, dbf_ref, dgn_ref,
             dst_ref, la_ref, sg_ref, df_ref, e_ref, ktf_ref, dec_ref, g_ref):
        @pl.when(pl.program_id(0) == 0)
        def _():
            dst_ref[...] = jnp.zeros_like(dst_ref)
            dwfu_ref[...] = jnp.zeros_like(dwfu_ref)
            dbf_ref[...] = jnp.zeros_like(dbf_ref)
            dgn_ref[...] = jnp.zeros_like(dgn_ref)

        flow = pg_ref[:, PG_F:PG_W]
        f = jnp.dot(flow, wfu_ref[...], preferred_element_type=F32) + bf_ref[...]
        la_ref[...] = _log_sigmoid(f) * (1.0 / GATE_TEMP)
        sg_ref[...] = _sigmoid(-f) * (1.0 / GATE_TEMP)
        tri = _tri(False)
        tri_strict = _tri(True)
        chunks = [slice(ci * CHUNK, (ci + 1) * CHUNK) for ci in range(cpb)]
        for ci, rows in enumerate(chunks):
            la = la_ref[rows, :]
            b = jnp.dot(tri, la, precision=HIGHEST, preferred_element_type=F32)
            bend = jnp.sum(la, axis=0, keepdims=True)
            e = jnp.exp(bend - b)
            e_ref[rows, :] = e
            dec = jnp.exp(bend)
            dec_ref[ci:ci + 1, :] = dec
            for hd in range(HEADS):
                kc = slice(hd * HDK, (hd + 1) * HDK)
                vc = slice(hd * HDV, (hd + 1) * HDV)
                q = pg_ref[rows, PG_Q + hd * HDK:PG_Q + (hd + 1) * HDK]
                k = pg_ref[rows, PG_K + hd * HDK:PG_K + (hd + 1) * HDK]
                v = pg_ref[rows, PG_V + hd * HDV:PG_V + (hd + 1) * HDV]
                go = pg_ref[rows, PG_G + hd * HDV:PG_G + (hd + 1) * HDV].astype(F32)
                ktf = k.astype(F32) * e[:, kc]
                ktf_ref[rows, kc] = ktf
                st = sp_ref[ci, hd] * dec[:, kc] + lax.dot_general(v, ktf.astype(BF16), (((0,), (0,)), ((), ())), preferred_element_type=F32)
                st_b = st.astype(BF16)
                qs = (q.astype(F32) * Q_SCALE).astype(BF16)
                o = lax.dot_general(qs, st_b, (((1,), (1,)), ((), ())), preferred_element_type=F32)
                oh, r = _rms(o)
                gh = gn_ref[:, vc]
                sig = _sigmoid(go)
                dy = dya_ref[rows, vc].astype(F32)
                don = dy * (go * sig)
                dgn_ref[:, vc] += jnp.sum(don * oh, axis=0, keepdims=True)
                dong = don * gh
                do = (r * (dong - oh * jnp.mean(dong * oh, axis=-1, keepdims=True))).astype(BF16)
                g_ref[ci, hd] = lax.dot_general(do, qs, (((0,), (0,)), ((), ())), preferred_element_type=F32)
                dq = jnp.dot(do, st_b, preferred_element_type=F32) * Q_SCALE
                dpg_ref[rows, PG_Q + hd * HDK:PG_Q + (hd + 1) * HDK] = dq.astype(dpg_ref.dtype)
                dgo = dy * (oh * gh) * (sig * (1.0 + go * (1.0 - sig)))
                dpg_ref[rows, PG_G + hd * HDV:PG_G + (hd + 1) * HDV] = dgo.astype(dpg_ref.dtype)
        for ci in reversed(range(cpb)):
            for hd in range(HEADS):
                dst = dst_ref[hd] + g_ref[ci, hd]
                g_ref[ci, hd] = dst
                dst_ref[hd] = dst * dec_ref[ci:ci + 1, hd * HDK:(hd + 1) * HDK]
        for ci, rows in enumerate(chunks):
            for hd in range(HEADS):
                kc = slice(hd * HDK, (hd + 1) * HDK)
                v = pg_ref[rows, PG_V + hd * HDV:PG_V + (hd + 1) * HDV]
                ktf = ktf_ref[rows, kc]
                dst = g_ref[ci, hd]
                dst_b = dst.astype(BF16)
                dkt = jnp.dot(v, dst_b, preferred_element_type=F32)
                dv = lax.dot_general(ktf.astype(BF16), dst_b, (((1,), (1,)), ((), ())), preferred_element_type=F32)
                dd = jnp.sum(dst * sp_ref[ci, hd], axis=0, keepdims=True)
                dla = jnp.dot(tri_strict, dkt * ktf, precision=HIGHEST, preferred_element_type=F32) + dd * dec_ref[ci:ci + 1, kc]
                df_ref[rows, kc] = dla * sg_ref[rows, kc]
                dpg_ref[rows, PG_K + hd * HDK:PG_K + (hd + 1) * HDK] = (dkt * e_ref[rows, kc]).astype(dpg_ref.dtype)
                dpg_ref[rows, PG_V + hd * HDV:PG_V + (hd + 1) * HDV] = dv.astype(dpg_ref.dtype)
        df = df_ref[...]
        df_b = df.astype(BF16)
        dpg_ref[:, PG_F:PG_W] = lax.dot_general(df_b, wfu_ref[...], (((1,), (1,)), ((), ())), preferred_element_type=F32).astype(dpg_ref.dtype)
        dwfu_ref[...] += lax.dot_general(flow, df_b, (((0,), (0,)), ((), ())), preferred_element_type=F32)
        dbf_ref[...] += jnp.sum(df, axis=0, keepdims=True)

    rev = lambda i: (nblk - 1 - i, 0)
    return pl.pallas_call(
        body, grid=(nblk,),
        in_specs=[pl.BlockSpec((ts, PG_W), rev), pl.BlockSpec((cpb, HEADS, HDV, HDK), lambda i: (nblk - 1 - i, 0, 0, 0)),
                  pl.BlockSpec((ts, HEADS * HDV), rev), _fixed((LANE, HEADS * HDK)), _fixed((1, HEADS * HDK)), _fixed((1, HEADS * HDV))],
        out_specs=[pl.BlockSpec((ts, PG_W), rev), _fixed((LANE, HEADS * HDK)), _fixed((1, HEADS * HDK)), _fixed((1, HEADS * HDV))],
        out_shape=[jax.ShapeDtypeStruct((s, PG_W), BF16), jax.ShapeDtypeStruct((LANE, HEADS * HDK), F32),
                   jax.ShapeDtypeStruct((1, HEADS * HDK), F32), jax.ShapeDtypeStruct((1, HEADS * HDV), F32)],
        scratch_shapes=[pltpu.VMEM((HEADS, HDV, HDK), F32)] + [pltpu.VMEM((ts, HEADS * HDK), F32)] * 5
        + [pltpu.VMEM((max(cpb, 8), HEADS * HDK), F32), pltpu.VMEM((cpb, HEADS, HDV, HDK), F32)],
        compiler_params=_params(("arbitrary",)), name=name,
    )(pg, sp, dya, wfu, b_f, gnorm)


def _window_sums(ext, sign):
    n = ext.shape[0]
    sums = {1: ext}
    w = 1
    while w < POOL_WINDOWS[-1]:
        sums[2 * w] = sums[w] + pltpu.roll(sums[w], w if sign > 0 else n - w, 0)
        w *= 2
    return [sums[POOL_WINDOWS[g]][:, g * LANE:(g + 1) * LANE] for g in range(len(POOL_WINDOWS))]


def _pool_counts(row0, n):
    pos = (row0 + lax.broadcasted_iota(jnp.int32, (n, 1), 0) + 1).astype(F32)
    return [jnp.minimum(pos, float(w)) for w in POOL_WINDOWS]


def _pool_fwd(ppx, w_pool, pool_scale, name):
    s = ppx.shape[0]
    ts = _tile(s, TS_POOL, POOL_HALO)
    hb = ts // POOL_HALO
    pw = len(POOL_WINDOWS) * LANE

    def body(p_ref, halo_ref, w_ref, sc_ref, y_ref, ext_ref):
        i = pl.program_id(0)
        p = p_ref[...].astype(F32)
        ext_ref[0:POOL_HALO, :] = jnp.where(i > 0, halo_ref[...].astype(F32), 0.0)
        ext_ref[POOL_HALO:, :] = p
        sums = _window_sums(ext_ref[...], +1)
        cnt = _pool_counts(i * ts, ts)
        for g in range(len(POOL_WINDOWS)):
            cols = slice(g * LANE, (g + 1) * LANE)
            mixed = sums[g][POOL_HALO:, :] / cnt[g] - p[:, cols]
            y = jnp.dot(mixed.astype(BF16), w_ref[g], preferred_element_type=F32)
            y_ref[:, cols] = (y * sc_ref[:, cols]).astype(y_ref.dtype)

    return pl.pallas_call(
        body, grid=(s // ts,),
        in_specs=[pl.BlockSpec((ts, pw), lambda i: (i, 0)), pl.BlockSpec((POOL_HALO, pw), lambda i: (jnp.maximum(i * hb - 1, 0), 0)),
                  _fixed((len(POOL_WINDOWS), LANE, LANE)), _fixed((1, pw))],
        out_specs=_rows(ts, pw), out_shape=jax.ShapeDtypeStruct((s, pw), BF16),
        scratch_shapes=[pltpu.VMEM((ts + POOL_HALO, pw), F32)],
        compiler_params=_params(("parallel",)), name=name,
    )(ppx, ppx, w_pool, pool_scale)


def _pool_bwd(dyb, ppx, w_pool, pool_scale, name):
    s = ppx.shape[0]
    ts = _tile(s, TS_POOL, POOL_HALO)
    hb = ts // POOL_HALO
    nblk = s // ts
    last_halo = s // POOL_HALO - 1
    ng = len(POOL_WINDOWS)
    pw = ng * LANE

    def body(p_ref, halo_ref, dy_ref, dyn_ref, w_ref, sc_ref, dp_ref, dw_ref, dsc_ref, ext_ref, dext_ref, dm_ref):
        i = pl.program_id(0)

        @pl.when(i == 0)
        def _():
            dw_ref[...] = jnp.zeros_like(dw_ref)
            dsc_ref[...] = jnp.zeros_like(dsc_ref)

        p = p_ref[...].astype(F32)
        ext_ref[0:POOL_HALO, :] = jnp.where(i > 0, halo_ref[...].astype(F32), 0.0)
        ext_ref[POOL_HALO:, :] = p
        sums = _window_sums(ext_ref[...], +1)
        cnt = _pool_counts(i * ts, ts + POOL_HALO)
        sc = sc_ref[...]
        dy = dy_ref[...].astype(F32)
        dyn = jnp.where(i < nblk - 1, dyn_ref[...].astype(F32), 0.0)
        for g in range(ng):
            cols = slice(g * LANE, (g + 1) * LANE)
            wg = w_ref[g]
            mixed = (sums[g][POOL_HALO:, :] / cnt[g][0:ts] - p[:, cols]).astype(BF16)
            ypre = jnp.dot(mixed, wg, preferred_element_type=F32)
            dsc_ref[:, cols] += jnp.sum(dy[:, cols] * ypre, axis=0, keepdims=True)
            dyp = (dy[:, cols] * sc[:, cols]).astype(BF16)
            dypn = (dyn[:, cols] * sc[:, cols]).astype(BF16)
            dw_ref[g] += lax.dot_general(mixed, dyp, (((0,), (0,)), ((), ())), preferred_element_type=F32)
            dm = lax.dot_general(dyp, wg, (((1,), (1,)), ((), ())), preferred_element_type=F32)
            dmn = lax.dot_general(dypn, wg, (((1,), (1,)), ((), ())), preferred_element_type=F32)
            dext_ref[0:ts, cols] = dm / cnt[g][0:ts]
            dext_ref[ts:, cols] = dmn / cnt[g][ts:]
            dm_ref[:, cols] = dm
        lead = _window_sums(dext_ref[...], -1)
        for g in range(ng):
            cols = slice(g * LANE, (g + 1) * LANE)
            dp_ref[:, cols] = (lead[g][0:ts, :] - dm_ref[:, cols]).astype(dp_ref.dtype)

    return pl.pallas_call(
        body, grid=(nblk,),
        in_specs=[pl.BlockSpec((ts, pw), lambda i: (i, 0)), pl.BlockSpec((POOL_HALO, pw), lambda i: (jnp.maximum(i * hb - 1, 0), 0)),
                  pl.BlockSpec((ts, pw), lambda i: (i, 0)), pl.BlockSpec((POOL_HALO, pw), lambda i: (jnp.minimum((i + 1) * hb, last_halo), 0)),
                  _fixed((ng, LANE, LANE)), _fixed((1, pw))],
        out_specs=[_rows(ts, pw), _fixed((ng, LANE, LANE)), _fixed((1, pw))],
        out_shape=[jax.ShapeDtypeStruct((s, pw), BF16), jax.ShapeDtypeStruct((ng, LANE, LANE), F32), jax.ShapeDtypeStruct((1, pw), F32)],
        scratch_shapes=[pltpu.VMEM((ts + POOL_HALO, pw), F32), pltpu.VMEM((ts + POOL_HALO, pw), F32), pltpu.VMEM((ts, pw), F32)],
        compiler_params=_params(("arbitrary",)), name=name,
    )(ppx, ppx, dyb, dyb, w_pool, pool_scale)


def _xattn_fwd(ppx, kv, name):
    s = ppx.shape[0]
    m = kv.shape[0]
    ts = _tile(s, TS_XA, 8)
    xw = XA_HEADS * XA_HD

    def body(q_ref, kv_ref, o_ref):
        for hd in range(XA_HEADS):
            cols = slice(hd * XA_HD, (hd + 1) * XA_HD)
            k = kv_ref[:, hd * XA_HD:(hd + 1) * XA_HD]
            v = kv_ref[:, xw + hd * XA_HD:xw + (hd + 1) * XA_HD]
            sc = lax.dot_general(q_ref[:, cols], k, (((1,), (1,)), ((), ())), preferred_element_type=F32) * XA_SCALE
            ex = jnp.exp(sc - jnp.max(sc, axis=-1, keepdims=True))
            pr = ex / jnp.sum(ex, axis=-1, keepdims=True)
            o_ref[:, cols] = jnp.dot(pr.astype(BF16), v, preferred_element_type=F32).astype(o_ref.dtype)

    return pl.pallas_call(
        body, grid=(s // ts,), in_specs=[pl.BlockSpec((ts, xw), lambda i: (i, 1)), _fixed((m, 2 * xw))],
        out_specs=_rows(ts, xw), out_shape=jax.ShapeDtypeStruct((s, xw), BF16),
        compiler_params=_params(("parallel",)), name=name,
    )(ppx, kv)


def _xattn_bwd(dxc, ppx, kv, name):
    s = ppx.shape[0]
    m = kv.shape[0]
    ts = _tile(s, TS_XA, 8)
    xw = XA_HEADS * XA_HD

    def body(do_ref, q_ref, kv_ref, dq_ref, dkv_ref):
        @pl.when(pl.program_id(0) == 0)
        def _():
            dkv_ref[...] = jnp.zeros_like(dkv_ref)

        for hd in range(XA_HEADS):
            cols = slice(hd * XA_HD, (hd + 1) * XA_HD)
            vcols = slice(xw + hd * XA_HD, xw + (hd + 1) * XA_HD)
            q = q_ref[:, cols]
            k = kv_ref[:, cols]
            v = kv_ref[:, vcols]
            do = do_ref[:, cols]
            sc = lax.dot_general(q, k, (((1,), (1,)), ((), ())), preferred_element_type=F32) * XA_SCALE
            ex = jnp.exp(sc - jnp.max(sc, axis=-1, keepdims=True))
            pr = ex / jnp.sum(ex, axis=-1, keepdims=True)
            dpr = lax.dot_general(do, v, (((1,), (1,)), ((), ())), preferred_element_type=F32)
            dsc = (pr * (dpr - jnp.sum(dpr * pr, axis=-1, keepdims=True)) * XA_SCALE).astype(BF16)
            dq_ref[:, cols] = jnp.dot(dsc, k, preferred_element_type=F32).astype(dq_ref.dtype)
            dkv_ref[:, cols] += lax.dot_general(dsc, q, (((0,), (0,)), ((), ())), preferred_element_type=F32)
            dkv_ref[:, vcols] += lax.dot_general(pr.astype(BF16), do, (((0,), (0,)), ((), ())), preferred_element_type=F32)

    return pl.pallas_call(
        body, grid=(s // ts,), in_specs=[_rows(ts, xw), pl.BlockSpec((ts, xw), lambda i: (i, 1)), _fixed((m, 2 * xw))],
        out_specs=[_rows(ts, xw), _fixed((m, 2 * xw))],
        out_shape=[jax.ShapeDtypeStruct((s, xw), BF16), jax.ShapeDtypeStruct((m, 2 * xw), F32)],
        compiler_params=_params(("arbitrary",)), name=name,
    )(dxc, ppx, kv)


def _merge_fwd(pgt, ya, yb, yc, name):
    s = pgt.shape[0]
    ts = _tile(s, TS_ROW, 8)

    def body(gt_ref, ya_ref, yb_ref, yc_ref, o_ref):
        acc = _sigmoid(gt_ref[:, 0:D].astype(F32)) * ya_ref[...].astype(F32)
        acc = acc + _sigmoid(gt_ref[:, D:2 * D].astype(F32)) * yb_ref[...].astype(F32)
        acc = acc + _sigmoid(gt_ref[:, 2 * D:3 * D].astype(F32)) * yc_ref[...].astype(F32)
        o_ref[...] = acc.astype(o_ref.dtype)

    return pl.pallas_call(
        body, grid=(s // ts,), in_specs=[_rows(ts, 3 * D)] + [_rows(ts, D)] * 3, out_specs=_rows(ts, D),
        out_shape=jax.ShapeDtypeStruct((s, D), BF16), compiler_params=_params(("parallel",)), name=name,
    )(pgt, ya, yb, yc)


def _merge_bwd(dmerged, pgt, ya, yb, yc, name):
    s = pgt.shape[0]
    ts = _tile(s, TS_ROW, 8)

    def body(dm_ref, gt_ref, ya_ref, yb_ref, yc_ref, dya_ref, dyb_ref, dyc_ref, dgt_ref):
        dm = dm_ref[...].astype(F32)
        for j, (y_ref, dy_ref) in enumerate(((ya_ref, dya_ref), (yb_ref, dyb_ref), (yc_ref, dyc_ref))):
            sig = _sigmoid(gt_ref[:, j * D:(j + 1) * D].astype(F32))
            dy_ref[...] = (dm * sig).astype(dy_ref.dtype)
            dgt_ref[:, j * D:(j + 1) * D] = (dm * y_ref[...].astype(F32) * sig * (1.0 - sig)).astype(dgt_ref.dtype)

    return pl.pallas_call(
        body, grid=(s // ts,), in_specs=[_rows(ts, D), _rows(ts, 3 * D)] + [_rows(ts, D)] * 3,
        out_specs=[_rows(ts, D)] * 3 + [_rows(ts, 3 * D)],
        out_shape=[jax.ShapeDtypeStruct((s, D), BF16)] * 3 + [jax.ShapeDtypeStruct((s, 3 * D), BF16)],
        compiler_params=_params(("parallel",)), name=name,
    )(dmerged, pgt, ya, yb, yc)


def _adam_math(w, g, m, v):
    mn = ADAM_B1 * m + (1.0 - ADAM_B1) * g
    vn = ADAM_B2 * v + (1.0 - ADAM_B2) * (g * g)
    m_hat = mn / (1.0 - ADAM_B1 ** ADAM_STEP)
    v_hat = vn / (1.0 - ADAM_B2 ** ADAM_STEP)
    return -ADAM_LR * (m_hat / (jnp.sqrt(v_hat) + ADAM_EPS) + ADAM_WD * w), mn, vn


def _adamw(w, g, m, v, name):
    r, c = w.shape[-2:]
    tr, tc = _block_of(r, c)

    def spec(a):
        if a.ndim == 2:
            return pl.BlockSpec((tr, tc), lambda i, j: (i, j))
        return pl.BlockSpec((None, tr, tc), lambda i, j: (0, i, j))

    def body(w_ref, g_ref, m_ref, v_ref, d_ref, mo_ref, vo_ref):
        d_ref[...], mo_ref[...], vo_ref[...] = _adam_math(w_ref[...], g_ref[...], m_ref[...], v_ref[...])

    return pl.pallas_call(
        body, grid=(r // tr, c // tc), in_specs=[spec(a) for a in (w, g, m, v)], out_specs=[spec(w)] * 3,
        out_shape=[jax.ShapeDtypeStruct(w.shape, F32)] * 3, compiler_params=_params(("parallel", "parallel")), name=name,
    )(w, g, m, v)


ANY = pl.BlockSpec(memory_space=pl.ANY)


def _place():
    x, y, c = lax.axis_index("x"), lax.axis_index("y"), lax.axis_index("c")
    chips = [(1 - x, y), (x, 1 - y), (1 - x, 1 - y)]
    return x, y, c, chips


def _half(c, rows):
    h = rows // 2
    return pl.ds(pl.multiple_of(c * h, 8), h)


def _by_cols(rows):
    return rows % 32 != 0 and rows != 16


def _half_of(ref, lead, c):
    r, cols = ref.shape[-2:]
    if _by_cols(r):
        return ref.at[(*lead, slice(None), pl.ds(pl.multiple_of(c * (cols // 2), LANE), cols // 2))]
    return ref.at[(*lead, pl.ds(pl.multiple_of(c * (r // 2), 8), r // 2))]


def _half_shape(shape):
    r, cols = shape[-2:]
    return shape[:-2] + ((r, cols // 2) if _by_cols(r) else (r // 2, cols))


def _block_of(r, cols, cap=256):
    if r % 16 == 0:
        return _tile(r, cap, 16), cols
    return r, _tile(cols, cap)


def _place_shard(shard, chip_arr, out_dtype, name):
    _, r, cols = shard.shape
    tr, tc = _block_of(r, cols)

    def body(chip_ref, s_ref, o_ref):
        o_ref[...] = s_ref[...].astype(o_ref.dtype)

    return pl.pallas_call(
        body,
        grid_spec=pltpu.PrefetchScalarGridSpec(
            num_scalar_prefetch=1, grid=(r // tr, cols // tc),
            in_specs=[pl.BlockSpec((None, tr, tc), lambda i, j, chip_ref: (0, i, j))],
            out_specs=pl.BlockSpec((None, tr, tc), lambda i, j, chip_ref: (chip_ref[0], i, j))),
        out_shape=jax.ShapeDtypeStruct((4, r, cols), out_dtype),
        compiler_params=_params(("parallel", "parallel")), name=name,
    )(chip_arr, shard)


def _gather_shards(bufs, name):
    n = len(bufs)

    def body(*refs):
        outs = refs[n:2 * n]
        send_ici, recv_ici, send_d2d, recv_d2d = refs[2 * n:]
        x, y, c, chips = _place()
        me = 2 * x + y
        sibling = (x, y, 1 - c)

        def ici(w, p, chip_of_block, to):
            rows = _half(c, outs[w].shape[1])
            block = outs[w].at[chip_of_block, rows]
            return pltpu.make_async_remote_copy(
                src_ref=block, dst_ref=block, send_sem=send_ici.at[w, p], recv_sem=recv_ici.at[w, p], device_id=to, device_id_type=MESH)

        def d2d(w, p, chip_of_block, half_of):
            rows = _half(half_of, outs[w].shape[1])
            block = outs[w].at[chip_of_block, rows]
            return pltpu.make_async_remote_copy(
                src_ref=block, dst_ref=block, send_sem=send_d2d.at[w, p], recv_sem=recv_d2d.at[w, p], device_id=sibling, device_id_type=MESH)

        sends = [ici(w, p, me, (*chip, c)) for p, chip in enumerate(chips) for w in range(n)]
        for cp in sends:
            cp.start()
        passed = []
        for p, (px, py) in enumerate(chips):
            for w in range(n):
                ici(w, p, 2 * px + py, (px, py, c)).wait_recv()
                fwd = d2d(w, p, 2 * px + py, c)
                fwd.start()
                passed.append(fwd)
        for p, (px, py) in enumerate(chips):
            for w in range(n):
                d2d(w, p, 2 * px + py, 1 - c).wait_recv()
        for cp in sends + passed:
            cp.wait_send()

    return pl.pallas_call(
        body, in_specs=[ANY] * n, out_specs=[ANY] * n,
        out_shape=[jax.ShapeDtypeStruct(a.shape, a.dtype) for a in bufs],
        input_output_aliases={w: w for w in range(n)},
        scratch_shapes=[pltpu.SemaphoreType.DMA((n, 3))] * 4,
        compiler_params=pltpu.CompilerParams(has_side_effects=True), name=name,
    )(*bufs)


HBM = pl.BlockSpec(memory_space=pltpu.HBM)
SEM = pl.BlockSpec(memory_space=pltpu.SEMAPHORE)
EFFECT = pltpu.SideEffectType.DATAFLOW_SIDE_EFFECTING


def _in_hbm(arrays):
    return [pltpu.with_memory_space_constraint(a, pltpu.HBM) for a in arrays]


def _gather_start(bufs, after, name):
    n, na = len(bufs), len(after)

    def body(*refs):
        send_sem, recv_sem = refs[n + na], refs[n + na + 1]
        outs = refs[n + na + 2:2 * n + na + 2]
        token = refs[2 * n + na + 2]
        x, y, c, chips = _place()
        me = 2 * x + y
        for p, chip in enumerate(chips):
            for w in range(n):
                block = _half_of(outs[w], (me,), c)
                pltpu.make_async_remote_copy(
                    src_ref=block, dst_ref=block, send_sem=send_sem, recv_sem=recv_sem,
                    device_id=(*chip, c), device_id_type=MESH).start()
        token[...] = jnp.zeros_like(token)

    out = pl.pallas_call(
        body, name=name, in_specs=[HBM] * n + [ANY] * na,
        out_specs=[SEM, SEM] + [HBM] * n + [pl.BlockSpec(memory_space=pltpu.VMEM)],
        out_shape=[pltpu.SemaphoreType.DMA(()), pltpu.SemaphoreType.DMA(())]
        + [pltpu.HBM(a.shape, a.dtype) for a in bufs] + [jax.ShapeDtypeStruct((8, LANE), F32)],
        input_output_aliases={w: w + 2 for w in range(n)},
        compiler_params=pltpu.CompilerParams(has_side_effects=EFFECT),
    )(*_in_hbm(bufs), *after)
    return out[0], out[1], list(out[2:2 + n]), out[2 + n]


def _gather_pass(bufs, send_sem, recv_sem, after, name):
    n, na = len(bufs), len(after)

    def body(*refs):
        send1, recv1 = refs[n], refs[n + 1]
        send2, recv2 = refs[n + 2 + na], refs[n + 3 + na]
        outs = refs[n + 4 + na:2 * n + 4 + na]
        x, y, c, chips = _place()
        me = 2 * x + y
        arrivals = [(w, px, py) for px, py in chips for w in range(n)]
        for w, px, py in arrivals:
            first = pltpu.make_async_remote_copy(
                src_ref=_half_of(outs[w], (me,), c), dst_ref=_half_of(outs[w], (2 * px + py,), c), send_sem=send1, recv_sem=recv1,
                device_id=(px, py, c), device_id_type=MESH)
            first.wait_send()
            first.wait_recv()
        for w, px, py in arrivals:
            arrived = _half_of(outs[w], (2 * px + py,), c)
            pltpu.make_async_remote_copy(
                src_ref=arrived, dst_ref=arrived, send_sem=send2, recv_sem=recv2,
                device_id=(x, y, 1 - c), device_id_type=MESH).start()

    out = pl.pallas_call(
        body, name=name, in_specs=[HBM] * n + [SEM, SEM] + [ANY] * na,
        out_specs=[SEM, SEM] + [HBM] * n,
        out_shape=[pltpu.SemaphoreType.DMA(()), pltpu.SemaphoreType.DMA(())] + [pltpu.HBM(a.shape, a.dtype) for a in bufs],
        input_output_aliases={w: w + 2 for w in range(n)},
        compiler_params=pltpu.CompilerParams(has_side_effects=EFFECT),
    )(*bufs, send_sem, recv_sem, *after)
    return out[0], out[1], list(out[2:])


def _gather_finish(bufs, send_sem, recv_sem, after, name):
    n, na = len(bufs), len(after)

    def body(*refs):
        send2, recv2 = refs[n], refs[n + 1]
        outs = refs[n + 2 + na:2 * n + 2 + na]
        x, y, c, chips = _place()
        for p, (px, py) in enumerate(chips):
            for w in range(n):
                passed = pltpu.make_async_remote_copy(
                    src_ref=_half_of(outs[w], (2 * px + py,), c), dst_ref=_half_of(outs[w], (2 * px + py,), 1 - c),
                    send_sem=send2, recv_sem=recv2, device_id=(x, y, 1 - c), device_id_type=MESH)
                passed.wait_send()
                passed.wait_recv()

    out = pl.pallas_call(
        body, name=name, in_specs=[HBM] * n + [SEM, SEM] + [ANY] * na, out_specs=[HBM] * n,
        out_shape=[pltpu.HBM(a.shape, a.dtype) for a in bufs],
        input_output_aliases={w: w for w in range(n)},
        compiler_params=pltpu.CompilerParams(has_side_effects=EFFECT),
    )(*bufs, send_sem, recv_sem, *after)
    return list(out)


def _pair_exchange(grads, name):
    n = len(grads)

    def body(*refs):
        ins, outs = refs[:n], refs[n:2 * n]
        send_sem, recv_sem = refs[2 * n:]
        x, y, c, _ = _place()
        copies = []
        for w in range(n):
            copies.append(pltpu.make_async_remote_copy(
                src_ref=_half_of(ins[w], (slice(None),), 1 - c), dst_ref=outs[w], send_sem=send_sem.at[w], recv_sem=recv_sem.at[w],
                device_id=(x, y, 1 - c), device_id_type=MESH))
        for cp in copies:
            cp.start()
        for cp in copies:
            cp.wait()

    return pl.pallas_call(
        body, in_specs=[ANY] * n, out_specs=[ANY] * n,
        out_shape=[jax.ShapeDtypeStruct(_half_shape(a.shape), a.dtype) for a in grads],
        scratch_shapes=[pltpu.SemaphoreType.DMA((n,))] * 2,
        compiler_params=pltpu.CompilerParams(has_side_effects=True), name=name,
    )(*grads)


def _pair_sum(g, got, c_arr, name):
    _, r, cols = g.shape
    hr, hc = _half_shape((r, cols))
    tr, tc = _block_of(hr, hc)
    nbr, nbc = hr // tr, hc // tc
    by_cols = _by_cols(r)

    def body(c_ref, g_ref, got_ref, o_ref):
        o_ref[...] = (g_ref[...].astype(F32) + got_ref[...].astype(F32)).astype(o_ref.dtype)

    def mine(j, i, k, c_ref):
        return (j, i, c_ref[0] * nbc + k) if by_cols else (j, c_ref[0] * nbr + i, k)

    return pl.pallas_call(
        body,
        grid_spec=pltpu.PrefetchScalarGridSpec(
            num_scalar_prefetch=1, grid=(4, nbr, nbc),
            in_specs=[pl.BlockSpec((None, tr, tc), mine),
                      pl.BlockSpec((None, tr, tc), lambda j, i, k, c_ref: (j, i, k))],
            out_specs=pl.BlockSpec((None, tr, tc), lambda j, i, k, c_ref: (j, i, k))),
        out_shape=jax.ShapeDtypeStruct((4, hr, hc), BF16),
        compiler_params=_params(("parallel", "parallel", "parallel")), name=name,
    )(c_arr, g, got)


def _chip_exchange(parts, name):
    n = len(parts)

    def body(*refs):
        ins, outs = refs[:n], refs[n:2 * n]
        send_sem, recv_sem = refs[2 * n:]
        x, y, c, chips = _place()
        copies = []
        for p, (px, py) in enumerate(chips):
            for w in range(n):
                copies.append(pltpu.make_async_remote_copy(
                    src_ref=ins[w].at[2 * px + py], dst_ref=outs[w].at[p], send_sem=send_sem.at[w, p], recv_sem=recv_sem.at[w, p],
                    device_id=(px, py, c), device_id_type=MESH))
        for cp in copies:
            cp.start()
        for cp in copies:
            cp.wait()

    return pl.pallas_call(
        body, in_specs=[ANY] * n, out_specs=[ANY] * n,
        out_shape=[jax.ShapeDtypeStruct((3,) + a.shape[1:], a.dtype) for a in parts],
        scratch_shapes=[pltpu.SemaphoreType.DMA((n, 3))] * 2,
        compiler_params=pltpu.CompilerParams(has_side_effects=True), name=name,
    )(*parts)


def _chip_exchange_start(parts, after, name):
    n, na = len(parts), len(after)
    lands = [lax.empty((3,) + a.shape[1:], a.dtype) for a in parts]

    def body(*refs):
        send_sem, recv_sem = refs[2 * n + na], refs[2 * n + na + 1]
        srcs = refs[2 * n + na + 2:3 * n + na + 2]
        dsts = refs[3 * n + na + 2:4 * n + na + 2]
        token = refs[4 * n + na + 2]
        x, y, c, chips = _place()
        for p, (px, py) in enumerate(chips):
            for w in range(n):
                pltpu.make_async_remote_copy(
                    src_ref=srcs[w].at[2 * px + py], dst_ref=dsts[w].at[p], send_sem=send_sem, recv_sem=recv_sem,
                    device_id=(px, py, c), device_id_type=MESH).start()
        token[...] = jnp.zeros_like(token)

    out = pl.pallas_call(
        body, name=name, in_specs=[HBM] * (2 * n) + [ANY] * na,
        out_specs=[SEM, SEM] + [HBM] * (2 * n) + [pl.BlockSpec(memory_space=pltpu.VMEM)],
        out_shape=[pltpu.SemaphoreType.DMA(()), pltpu.SemaphoreType.DMA(())]
        + [pltpu.HBM(a.shape, a.dtype) for a in parts + lands] + [jax.ShapeDtypeStruct((8, LANE), F32)],
        input_output_aliases={w: w + 2 for w in range(2 * n)},
        compiler_params=pltpu.CompilerParams(has_side_effects=EFFECT),
    )(*_in_hbm(parts), *_in_hbm(lands), *after)
    return out[0], out[1], list(out[2:2 + n]), list(out[2 + n:2 + 2 * n]), out[2 + 2 * n]


def _chip_exchange_finish(parts, lands, send_sem, recv_sem, after, name):
    n, na = len(parts), len(after)

    def body(*refs):
        send, recv = refs[2 * n], refs[2 * n + 1]
        srcs = refs[2 * n + 2 + na:3 * n + 2 + na]
        dsts = refs[3 * n + 2 + na:4 * n + 2 + na]
        x, y, c, chips = _place()
        for p, (px, py) in enumerate(chips):
            for w in range(n):
                copy = pltpu.make_async_remote_copy(
                    src_ref=srcs[w].at[2 * px + py], dst_ref=dsts[w].at[p], send_sem=send, recv_sem=recv,
                    device_id=(px, py, c), device_id_type=MESH)
                copy.wait_send()
                copy.wait_recv()

    out = pl.pallas_call(
        body, name=name, in_specs=[HBM] * (2 * n) + [SEM, SEM] + [ANY] * na, out_specs=[HBM] * (2 * n),
        out_shape=[pltpu.HBM(a.shape, a.dtype) for a in parts + lands],
        input_output_aliases={w: w for w in range(2 * n)},
        compiler_params=pltpu.CompilerParams(has_side_effects=EFFECT),
    )(*parts, *lands, send_sem, recv_sem, *after)
    return list(out[:n]), list(out[n:])


def _chip_sum(part, got, place_arr, name):
    _, hr, hc = part.shape
    by_cols = _by_cols(hr)
    tr, tc = _block_of(hr, hc)
    nbr, nbc = hr // tr, hc // tc

    def body(place_ref, p_ref, got_ref, o_ref):
        acc = p_ref[...].astype(F32)
        for p in range(3):
            acc = acc + got_ref[p].astype(F32)
        o_ref[...] = acc

    def mine(i, k, place_ref):
        return (i, place_ref[1] * nbc + k) if by_cols else (place_ref[1] * nbr + i, k)

    return pl.pallas_call(
        body,
        grid_spec=pltpu.PrefetchScalarGridSpec(
            num_scalar_prefetch=1, grid=(nbr, nbc),
            in_specs=[pl.BlockSpec((None, tr, tc), lambda i, k, place_ref: (place_ref[0], i, k)),
                      pl.BlockSpec((3, tr, tc), lambda i, k, place_ref: (0, i, k))],
            out_specs=pl.BlockSpec((tr, tc), mine)),
        out_shape=jax.ShapeDtypeStruct((hr, 2 * hc) if by_cols else (2 * hr, hc), F32),
        compiler_params=_params(("parallel", "parallel")), name=name,
    )(place_arr, part, got)


def _pair_join(bufs, name):
    n = len(bufs)

    def body(*refs):
        outs = refs[n:2 * n]
        send_sem, recv_sem = refs[2 * n:]
        x, y, c, _ = _place()
        copies = []
        for w in range(n):
            block = _half_of(outs[w], (), c)
            copies.append(pltpu.make_async_remote_copy(
                src_ref=block, dst_ref=block, send_sem=send_sem.at[w], recv_sem=recv_sem.at[w],
                device_id=(x, y, 1 - c), device_id_type=MESH))
        for cp in copies:
            cp.start()
        for w, cp in enumerate(copies):
            cp.wait_send()
            block = _half_of(outs[w], (), 1 - c)
            pltpu.make_async_remote_copy(
                src_ref=block, dst_ref=block, send_sem=send_sem.at[w], recv_sem=recv_sem.at[w],
                device_id=(x, y, 1 - c), device_id_type=MESH).wait_recv()

    return pl.pallas_call(
        body, in_specs=[ANY] * n, out_specs=[ANY] * n,
        out_shape=[jax.ShapeDtypeStruct(a.shape, a.dtype) for a in bufs],
        input_output_aliases={w: w for w in range(n)},
        scratch_shapes=[pltpu.SemaphoreType.DMA((n,))] * 2,
        compiler_params=pltpu.CompilerParams(has_side_effects=True), name=name,
    )(*bufs)


SMALL = ("ffn1_pre_g", "ffn1_post_g", "mix_pre_g", "gla_norm_g", "mem_norm_g", "mix_post_g", "ffn2_pre_g", "ffn2_post_g", "final_g",
         "b_f", "pool_scale", "w_pool", "w_fu")
N_GAINS = 9
SMALL_PACKS = ((16, D), (24, 512), (4 * LANE, LANE))
W_FU_ROW = 8


def _all_sum_small(gs, name):
    ins = [gs[n] for n in SMALL[:N_GAINS]] + [gs["b_f"], gs["pool_scale"], gs["w_fu_pad"], gs["w_pool"].reshape(4 * LANE, LANE)]

    def body(*refs):
        gain_refs = refs[:N_GAINS]
        bf_ref, ps_ref, wfu_ref, wp_ref = refs[N_GAINS:N_GAINS + 4]
        outs = refs[N_GAINS + 4:N_GAINS + 7]
        mine_a, mine_b, all_a, all_b, all_c, send_sems, recv_sems = refs[N_GAINS + 7:]
        mine_a[...] = jnp.zeros_like(mine_a)
        for i, ref in enumerate(gain_refs):
            mine_a[i:i + 1, :] = ref[...]
        mine_b[...] = jnp.zeros_like(mine_b)
        mine_b[0:1, :] = bf_ref[...]
        mine_b[1:2, :] = ps_ref[...]
        mine_b[W_FU_ROW:W_FU_ROW + GATE_RANK, :] = wfu_ref[0:GATE_RANK, :]
        packs = ((mine_a, all_a), (mine_b, all_b), (wp_ref, all_c))
        x, y, c, chips = _place()
        me, sibling = (x, y, c), (x, y, 1 - c)

        def copy(t, k, block, to, own=False):
            px, py, pc = block
            slot = packs[t][1].at[4 * px + 2 * py + pc]
            return pltpu.make_async_remote_copy(
                src_ref=packs[t][0] if own else slot, dst_ref=slot,
                send_sem=send_sems.at[t, k], recv_sem=recv_sems.at[t, k], device_id=to, device_id_type=MESH)

        started = []
        for t, (mine, everyone) in enumerate(packs):
            everyone[4 * x + 2 * y + c] = mine[...]
            started.append(copy(t, 0, me, sibling, own=True))
            started += [copy(t, 1 + j, me, (*chip, c), own=True) for j, chip in enumerate(chips)]
        for cp in started:
            cp.start()
        passed = []
        for j, chip in enumerate(chips):
            for t in range(len(packs)):
                copy(t, 1 + j, (*chip, c), me).wait_recv()
                fwd = copy(t, 4 + j, (*chip, c), sibling)
                fwd.start()
                passed.append(fwd)
        for t in range(len(packs)):
            copy(t, 0, sibling, me).wait_recv()
            for j, chip in enumerate(chips):
                copy(t, 4 + j, (*chip, 1 - c), me).wait_recv()
        for cp in started + passed:
            cp.wait_send()
        for (_, everyone), o_ref in zip(packs, outs):
            acc = everyone[0]
            for k in range(1, 8):
                acc = acc + everyone[k]
            o_ref[...] = acc

    vmem = pl.BlockSpec(memory_space=pltpu.VMEM)
    return pl.pallas_call(
        body, in_specs=[vmem] * len(ins), out_specs=[vmem] * 3,
        out_shape=[jax.ShapeDtypeStruct(shape, F32) for shape in SMALL_PACKS],
        scratch_shapes=[pltpu.VMEM(SMALL_PACKS[0], F32), pltpu.VMEM(SMALL_PACKS[1], F32)]
        + [pltpu.VMEM((8,) + shape, F32) for shape in SMALL_PACKS]
        + [pltpu.SemaphoreType.DMA((3, 7)), pltpu.SemaphoreType.DMA((3, 7))],
        compiler_params=pltpu.CompilerParams(has_side_effects=True, vmem_limit_bytes=VMEM_LIMIT), name=name,
    )(*ins)


def _adamw_small(sums, params, chip_arr, name):
    flat = [a for n in SMALL for a in params[n]]

    def body(chip_ref, a_ref, b_ref, c_ref, *refs):
        ins, outs = refs[:len(flat)], refs[len(flat):]
        for i, n in enumerate(SMALL):
            w_ref, m_ref, v_ref = ins[3 * i:3 * i + 3]
            g_ref, d_ref, mo_ref, vo_ref = outs[4 * i:4 * i + 4]
            if n == "w_pool":
                pieces = [((0, k), c_ref[k * LANE:(k + 1) * LANE, :]) for k in range(4)]
            elif n == "w_fu":
                mine = pl.ds(pl.multiple_of(chip_ref[0] * LANE, LANE), LANE)
                pieces = [((0,), b_ref[W_FU_ROW:W_FU_ROW + GATE_RANK, mine])]
            elif n == "b_f":
                pieces = [((), b_ref[0:1, :])]
            elif n == "pool_scale":
                pieces = [((), b_ref[1:2, :])]
            else:
                pieces = [((), a_ref[i:i + 1, :])]
            for at, g in pieces:
                d, mn, vn = _adam_math(w_ref[at], g, m_ref[at], v_ref[at])
                g_ref[at] = g
                d_ref[at] = d
                mo_ref[at] = mn
                vo_ref[at] = vn

    def whole(shape):
        return pl.BlockSpec(shape, lambda i, chip_ref: (0,) * len(shape))

    out = pl.pallas_call(
        body,
        grid_spec=pltpu.PrefetchScalarGridSpec(
            num_scalar_prefetch=1, grid=(1,),
            in_specs=[whole(a.shape) for a in list(sums) + flat],
            out_specs=[whole(params[n][0].shape) for n in SMALL for _ in range(4)]),
        out_shape=[jax.ShapeDtypeStruct(params[n][0].shape, F32) for n in SMALL for _ in range(4)],
        compiler_params=_params(("arbitrary",)), name=name,
    )(chip_arr, *sums, *flat)
    return {n: tuple(out[4 * i:4 * i + 4]) for i, n in enumerate(SMALL)}


def _ffn_fwd(x_norm, w_in, w_out, tag):
    ab, u = _ffn_in_swiglu(x_norm, w_in, name=tag + "_in")
    f = _mm(u, w_out, tm=1024, tk=DFF, name=tag + "_out")
    return ab, u, f


def _ffn_bwd(dz, x_norm, ab, u, w_in, w_out, tag, emit, after=()):
    dw_out = _mm(u, dz, ta=True, out_dtype=BF16, tm=1408, tk=2048, after=after, name=tag + "_out_dw")
    behind = emit(tag + "_w_out", dw_out)
    dab = _ffn_out_dx_swiglu(dz, w_out, ab, behind, name=tag + "_out_dx")
    dw_in = _mm(x_norm, dab, ta=True, out_dtype=BF16, tm=512, tk=4096, shards=4, name=tag + "_in_dw")
    behind = emit(tag + "_w_in", dw_in)
    return _mm(dab, w_in, tb=True, tm=1024, after=behind, name=tag + "_in_dx")


def _local_step(x, mem, target, small, gather, emit):
    behind = gather("start", "ffn1i", ())
    gather("start", "ffn1o", behind)
    gather("pass", "ffn1i", ())
    big = gather("finish", "ffn1i", ())
    behind = gather("start", "mixa", (big["ffn1_w_in"],))
    behind = gather("start", "mixb", behind)
    h1 = _norm_fwd(x, small["ffn1_pre_g"], BF16, name="ffn1_pre", after=behind)
    ab1, u1 = _ffn_in_swiglu(h1, big["ffn1_w_in"], name="ffn1_in")
    gather("pass", "ffn1o", (ab1,))
    big.update(gather("finish", "ffn1o", ()))
    f1 = _mm(u1, big["ffn1_w_out"], tm=1024, tk=DFF, name="ffn1_out")
    behind = gather("pass", "mixa", (f1,))
    x1, h = _resid_norm_fwd(x, f1, small["ffn1_post_g"], 0.5, small["mix_pre_g"], name="ffn1_post", after=behind)
    big.update(gather("finish", "mixa", (h,)))
    small = dict(small, w_fu_pad=big["w_fu_pad"])
    pg = _mm(h, big["w_gla_t"], tb=True, out_dtype=BF16, tm=1024, tn=PG_W, name="mix_in_gla")
    behind = gather("pass", "mixb", (pg,))
    ppx = _mm(h, big["w_px_t"], tb=True, out_dtype=BF16, after=behind, name="mix_in_px")
    pgt = _mm(h, big["w_gates_t"], tb=True, out_dtype=BF16, tn=1536, name="mix_in_gates")
    big.update(gather("finish", "mixb", (pgt,)))
    behind = gather("start", "ffn2", (big["w_o"],))
    mem_n = _norm_fwd(mem, small["mem_norm_g"], BF16, name="mem_norm", after=behind)
    kv = _mm(mem_n, big["w_mem_kv"], out_dtype=BF16, name="mem_kv")
    ya_in, sp = _gla_fwd(pg, small["w_fu_pad"], small["b_f"], small["gla_norm_g"], name="gla_fwd")
    yb_in = _pool_fwd(ppx, small["w_pool_b"], small["pool_scale"], name="pool_fwd")
    xc = _xattn_fwd(ppx, kv, name="xattn_fwd")
    ya = _mm(ya_in, big["w_up_gla"], out_dtype=BF16, name="up_gla")
    yb = _mm(yb_in, big["w_up_pool"], out_dtype=BF16, name="up_pool")
    yc = _mm(xc, big["w_up_xattn"], out_dtype=BF16, name="up_xattn")
    merged = _merge_fwd(pgt, ya, yb, yc, name="merge_fwd")
    behind = gather("pass", "ffn2", (merged,))
    ymix = _mm(merged, big["w_o"], after=behind, name="mix_out")
    big.update(gather("finish", "ffn2", (ymix,)))
    x2, h2 = _resid_norm_fwd(x1, ymix, small["mix_post_g"], 1.0, small["ffn2_pre_g"], name="mix_post")
    ab2, u2, f2 = _ffn_fwd(h2, big["ffn2_w_in"], big["ffn2_w_out"], "ffn2")
    x3, _ = _resid_norm_fwd(x2, f2, small["ffn2_post_g"], 0.5, None, name="ffn2_post")
    gs = {}
    dx3, gs["final_g"], loss = _loss_bwd(x3, small["final_g"], target, name="loss")
    dz2, gs["ffn2_post_g"] = _rms_bwd(f2, small["ffn2_post_g"], [dx3], None, 0.5, BF16, name="ffn2_post_bwd")
    dh2 = _ffn_bwd(dz2, h2, ab2, u2, big["ffn2_w_in"], big["ffn2_w_out"], "ffn2", emit)
    dx2, gs["ffn2_pre_g"] = _rms_bwd(x2, small["ffn2_pre_g"], [dh2], dx3, 1.0, F32, name="ffn2_pre_bwd")
    dy, gs["mix_post_g"] = _rms_bwd(ymix, small["mix_post_g"], [dx2], None, 1.0, BF16, name="mix_post_bwd")
    dmerged = _mm(dy, big["w_o"], tb=True, out_dtype=BF16, name="mix_out_dx")
    emit("w_o", _mm(merged, dy, ta=True, out_dtype=BF16, tm=512, tk=4096, name="mix_out_dw"))
    dya, dyb, dyc, dgt = _merge_bwd(dmerged, pgt, ya, yb, yc, name="merge_bwd")
    dya_in = _mm(dya, big["w_up_gla"], tb=True, out_dtype=BF16, name="up_gla_dx")
    emit("w_up_gla", _mm(ya_in, dya, ta=True, out_dtype=BF16, tm=512, tk=4096, name="up_gla_dw"))
    dyb_in = _mm(dyb, big["w_up_pool"], tb=True, out_dtype=BF16, name="up_pool_dx")
    emit("w_up_pool", _mm(yb_in, dyb, ta=True, out_dtype=BF16, tm=512, tk=4096, shards=4, name="up_pool_dw"))
    dxc = _mm(dyc, big["w_up_xattn"], tb=True, out_dtype=BF16, name="up_xattn_dx")
    emit("w_up_xattn", _mm(xc, dyc, ta=True, out_dtype=BF16, tm=512, tk=4096, shards=4, name="up_xattn_dw"))
    dpg, gs["w_fu_pad"], gs["b_f"], gs["gla_norm_g"] = _gla_bwd(pg, sp, dya_in, small["w_fu_pad"], small["b_f"], small["gla_norm_g"], name="gla_bwd")
    dp, gs["w_pool"], gs["pool_scale"] = _pool_bwd(dyb_in, ppx, small["w_pool_b"], small["pool_scale"], name="pool_bwd")
    dxq, dkv = _xattn_bwd(dxc, ppx, kv, name="xattn_bwd")
    dkv = dkv.astype(BF16)
    emit("w_mem_kv", _mm(mem_n, dkv, ta=True, out_dtype=BF16, name="mem_kv_dw"))
    dmem_n = _mm(dkv, big["w_mem_kv"], tb=True, name="mem_kv_dx")
    _, gs["mem_norm_g"] = _rms_bwd(mem, small["mem_norm_g"], [dmem_n], None, 1.0, BF16, name="mem_norm_bwd")
    emit("w_gla", _mm(dpg, h, ta=True, out_dtype=BF16, tm=640, tk=4096, name="mix_in_gla_dw"))
    emit("w_p", _mm(dp, h, ta=True, out_dtype=BF16, tm=512, tk=4096, name="mix_in_p_dw"))
    emit("w_xq", _mm(dxq, h, ta=True, out_dtype=BF16, tm=512, tk=4096, name="mix_in_xq_dw"))
    behind = emit("w_gates", _mm(dgt, h, ta=True, out_dtype=BF16, tm=512, tk=4096, name="mix_in_gates_dw"))
    dh_parts = [
        _mm(dpg, big["w_gla_t"], tm=1024, tk=PG_W, after=behind, name="mix_in_gla_dx"),
        _mm(dp, big["w_p_t"], name="mix_in_p_dx"),
        _mm(dxq, big["w_xq_t"], name="mix_in_xq_dx"),
        _mm(dgt, big["w_gates_t"], tm=1024, tk=3072, name="mix_in_gates_dx"),
    ]
    dx1, gs["mix_pre_g"] = _rms_bwd(x1, small["mix_pre_g"], dh_parts, dx2, 1.0, F32, name="mix_pre_bwd")
    dz1, gs["ffn1_post_g"] = _rms_bwd(f1, small["ffn1_post_g"], [dx1], None, 0.5, BF16, name="ffn1_post_bwd")
    dh1 = _ffn_bwd(dz1, h1, ab1, u1, big["ffn1_w_in"], big["ffn1_w_out"], "ffn1", emit)
    dx0, gs["ffn1_pre_g"] = _rms_bwd(x, small["ffn1_pre_g"], [dh1], dx1, 1.0, F32, name="ffn1_pre_bwd")
    return loss, dx0, gs


BIG = ("ffn1_w_in", "ffn1_w_out", "w_in", "w_mem_kv", "w_up_gla", "w_up_pool", "w_up_xattn", "w_o", "ffn2_w_in", "ffn2_w_out")
COL_SHARDED = ("ffn1_w_in", "w_in", "w_up_pool", "w_up_xattn", "ffn2_w_in")
GATHER_GROUPS = {"ffn1i": ("ffn1_w_in",), "ffn1o": ("ffn1_w_out",), "mixa": ("w_in", "w_fu"),
                 "mixb": ("w_mem_kv", "w_up_gla", "w_up_pool", "w_up_xattn", "w_o"), "ffn2": ("ffn2_w_in", "ffn2_w_out")}
REDUCE_GROUPS = {"ffn2": ("ffn2_w_out", "ffn2_w_in"),
                 "mix": ("w_o", "w_up_gla", "w_up_pool", "w_up_xattn", "w_mem_kv", "w_gla", "w_p", "w_xq", "w_gates"),
                 "ffn1_out": ("ffn1_w_out",),
                 "ffn1_in": ("ffn1_w_in",)}
GAINS = ("ffn1_pre_g", "ffn1_post_g", "mix_pre_g", "gla_norm_g", "mem_norm_g", "mix_post_g", "ffn2_pre_g", "ffn2_post_g", "final_g")
WEIGHTS = ("ffn1_pre_g", "ffn1_w_in", "ffn1_w_out", "ffn1_post_g", "mix_pre_g", "w_in", "w_fu", "b_f", "gla_norm_g", "w_pool",
           "pool_scale", "mem_norm_g", "w_mem_kv", "w_up_gla", "w_up_pool", "w_up_xattn", "w_o", "mix_post_g", "ffn2_pre_g",
           "ffn2_w_in", "ffn2_w_out", "ffn2_post_g", "final_g")
IN_GLA, IN_F, IN_PX, IN_GATES, IN_END = 0, 3072, 3088, 4112, 7184
def _cols_from_shards(g):
    return jnp.transpose(g, (1, 0, 2)).reshape(g.shape[1], 4 * g.shape[2])


def kernel(x, mem, ffn1_pre_g, ffn1_w_in, ffn1_w_out, ffn1_post_g, mix_pre_g, w_in, w_fu, b_f, gla_norm_g, w_pool, pool_scale, mem_norm_g, w_mem_kv, w_up_gla, w_up_pool, w_up_xattn, w_o, mix_post_g, ffn2_pre_g, ffn2_w_in, ffn2_w_out, ffn2_post_g, final_g, loss_target, m_ffn1_pre_g, m_ffn1_w_in, m_ffn1_w_out, m_ffn1_post_g, m_mix_pre_g, m_w_in, m_w_fu, m_b_f, m_gla_norm_g, m_w_pool, m_pool_scale, m_mem_norm_g, m_w_mem_kv, m_w_up_gla, m_w_up_pool, m_w_up_xattn, m_w_o, m_mix_post_g, m_ffn2_pre_g, m_ffn2_w_in, m_ffn2_w_out, m_ffn2_post_g, m_final_g, v_ffn1_pre_g, v_ffn1_w_in, v_ffn1_w_out, v_ffn1_post_g, v_mix_pre_g, v_w_in, v_w_fu, v_b_f, v_gla_norm_g, v_w_pool, v_pool_scale, v_mem_norm_g, v_w_mem_kv, v_w_up_gla, v_w_up_pool, v_w_up_xattn, v_w_o, v_mix_post_g, v_ffn2_pre_g, v_ffn2_w_in, v_ffn2_w_out, v_ffn2_post_g, v_final_g):
    args = dict(locals())
    w = {n: args[n][0] for n in WEIGHTS}
    m = {n: args["m_" + n][0] for n in WEIGHTS}
    v = {n: args["v_" + n][0] for n in WEIGHTS}
    xi, yi, ci = lax.axis_index("x"), lax.axis_index("y"), lax.axis_index("c")
    chip = 2 * xi + yi

    c_arr = jnp.reshape(ci, (1,)).astype(jnp.int32)
    chip_arr = jnp.reshape(chip, (1,)).astype(jnp.int32)
    place_arr = jnp.stack([chip, ci]).astype(jnp.int32)
    shard_of = {n: (jnp.transpose(args[n][0])[None] if n == "w_in" else args[n]) for n in BIG}
    placed = {n: _place_shard(shard_of[n], chip_arr, BF16, name="place_" + n) for n in BIG}
    placed["w_fu"] = _place_shard(args["w_fu"], chip_arr, F32, name="place_w_fu")
    inflight = {}

    def relayout(names, gathered):
        out = {}
        for n, g in zip(names, gathered):
            if n == "w_fu":
                w_fu_full = _cols_from_shards(g)
                out["w_fu_pad"] = jnp.concatenate([w_fu_full, jnp.zeros((LANE - GATE_RANK, 512), F32)], axis=0).astype(BF16)
            elif n == "w_in":
                wt = g.reshape(IN_END, D)
                out["w_gla_t"] = jnp.concatenate([wt[IN_GLA:IN_PX], jnp.zeros((PG_W - IN_PX, D), BF16)], axis=0)
                out["w_px_t"] = wt[IN_PX:IN_GATES]
                out["w_p_t"] = wt[IN_PX:IN_PX + 512]
                out["w_xq_t"] = wt[IN_PX + 512:IN_GATES]
                out["w_gates_t"] = wt[IN_GATES:IN_END]
            else:
                out[n] = _cols_from_shards(g) if n in COL_SHARDED else g.reshape(4 * g.shape[1], g.shape[2])
        return out

    def gather(op, group, after):
        names = GATHER_GROUPS[group]
        if op == "start":
            inflight[group] = _gather_start([placed[n] for n in names], after, name="gather_" + group + "_start")
            return (inflight[group][3],)
        if op == "pass":
            send, recv, bufs, _ = inflight[group]
            inflight[group] = _gather_pass(bufs, send, recv, after, name="gather_" + group + "_pass")
            return (inflight[group][2][0],)
        send, recv, bufs = inflight.pop(group)
        return relayout(names, _gather_finish(bufs, send, recv, after, name="gather_" + group + "_finish"))

    small = {n: w[n].reshape(1, D) for n in GAINS}
    small["b_f"] = w["b_f"].reshape(1, 512)
    small["pool_scale"] = w["pool_scale"].reshape(1, 512)
    small["w_pool_b"] = w["w_pool"].astype(BF16)

    pending, travelling = {}, {}

    def emit(name, grad):
        pending[name] = grad
        group = next((g for g, names in REDUCE_GROUPS.items() if name == names[-1]), None)
        if group is None:
            return ()
        gb = {n: pending.pop(n) for n in REDUCE_GROUPS[group]}
        if group == "mix":
            dwt = jnp.concatenate([gb.pop("w_gla")[0:IN_PX], gb.pop("w_p"), gb.pop("w_xq"), gb.pop("w_gates")], axis=0)
            gb["w_in"] = dwt.reshape(4, IN_END // 4, D)
        names = list(gb)
        contrib = [gb[n] if n in COL_SHARDED else gb[n].reshape(4, gb[n].shape[0] // 4, gb[n].shape[1]) for n in names]
        from_sibling = _pair_exchange(contrib, name="grads_" + group + "_pair_exchange")
        pair = [_pair_sum(g, got, c_arr, name="grads_pair_sum_" + n) for n, g, got in zip(names, contrib, from_sibling)]
        send, recv, pair, lands, token = _chip_exchange_start(pair, (), name="grads_" + group + "_chip_start")
        travelling[group] = (names, send, recv, pair, lands)
        return (token,)

    loss, grad_x, gs = _local_step(x[0], mem[0], loss_target[0], small, gather, emit)
    loss = lax.psum(loss[0, 0], ("x", "y", "c"))

    small_sums = _all_sum_small(gs, name="sum_small_grads")
    halves = {}
    for group, (names, send, recv, pair, lands) in travelling.items():
        pair, from_chips = _chip_exchange_finish(pair, lands, send, recv, (grad_x,), name="grads_" + group + "_chip_finish")
        for n, p, got in zip(names, pair, from_chips):
            halves[n] = _chip_sum(p, got, place_arr, name="grads_chip_sum_" + n)
    reduced = dict(zip(BIG, _pair_join([halves[n] for n in BIG], name="grads_pair_join")))

    grads, delta, new_m, new_v = {}, {}, {}, {}
    for n in BIG:
        if n == "w_in":
            transposed = [jnp.transpose(args[k][0]) for k in (n, "m_" + n, "v_" + n)]
            updated = _adamw(transposed[0], reduced[n], transposed[1], transposed[2], name="adamw_" + n)
            grads[n] = jnp.transpose(reduced[n])[None]
            delta[n], new_m[n], new_v[n] = (jnp.transpose(a)[None] for a in updated)
            continue
        grads[n] = reduced[n][None]
        delta[n], new_m[n], new_v[n] = _adamw(args[n], reduced[n], args["m_" + n], args["v_" + n], name="adamw_" + n)
    small_params = {n: (args[n], args["m_" + n], args["v_" + n]) for n in SMALL}
    for n, (g, d, mn, vn) in _adamw_small(small_sums, small_params, chip_arr, name="adamw_small").items():
        grads[n], delta[n], new_m[n], new_v[n] = g, d, mn, vn

    outs = [loss, grad_x[None]]
    for group in (grads, delta, new_m, new_v):
        outs += [group[n] for n in WEIGHTS]
    return tuple(outs)
```

```python
import functools

import jax
import jax.numpy as jnp
from jax import lax
from jax.experimental import pallas as pl
from jax.experimental.pallas import tpu as pltpu

F32 = jnp.float32
BF16 = jnp.bfloat16
MESH = pl.DeviceIdType.MESH
HIGHEST = lax.Precision.HIGHEST

D = 1024
DFF = 2816
CHUNK = 64
HEADS = 4
HDK = 128
HDV = 256
GATE_TEMP = 16.0
POOL_WINDOWS = (2, 4, 8, 16)
POOL_HALO = 16
XA_HEADS = 4
XA_HD = 128
EPS = 1e-6
Q_SCALE = HDK ** -0.5
XA_SCALE = XA_HD ** -0.5
PG_Q, PG_K, PG_V, PG_G, PG_F, PG_W = 0, 512, 1024, 2048, 3072, 3200
GATE_RANK = 16
ADAM_LR, ADAM_B1, ADAM_B2, ADAM_EPS, ADAM_WD, ADAM_STEP = 0.001, 0.9, 0.999, 1e-08, 0.01, 10

VMEM_LIMIT = 48 * 1024 * 1024
LANE = 128
TS_ROW = 512
TS_GLA = 512
TS_POOL = 512
TS_XA = 512


def _params(sem):
    return pltpu.CompilerParams(dimension_semantics=sem, vmem_limit_bytes=VMEM_LIMIT)


def _tile(n, cap, unit=LANE):
    if n <= cap:
        return n
    best = None
    for t in range(unit, cap + 1, unit):
        if n % t == 0:
            best = t
    assert best is not None, (n, cap)
    return best


def _sigmoid(x):
    return 1.0 / (1.0 + jnp.exp(-x))


def _log_sigmoid(x):
    return jnp.minimum(x, 0.0) - jnp.log(1.0 + jnp.exp(-jnp.abs(x)))


def _rms(x):
    r = lax.rsqrt(jnp.mean(x * x, axis=-1, keepdims=True) + EPS)
    return x * r, r


def _rows(ts, w):
    return pl.BlockSpec((ts, w), lambda i: (i, 0))


def _fixed(shape):
    nd = len(shape)
    return pl.BlockSpec(shape, lambda i: (0,) * nd)


def _mm(a, b, *, ta=False, tb=False, out_dtype=F32, tm=2048, tn=1024, tk=1024, shards=1, after=(), name):
    a_blocked, b_blocked = a.ndim == 3, b.ndim == 3
    assert not (a_blocked and ta) and not (b_blocked and tb)
    if a_blocked:
        m, kdim, tk = a.shape[1], a.shape[0] * a.shape[2], a.shape[2]
    else:
        m, kdim = (a.shape[1], a.shape[0]) if ta else a.shape
    if b_blocked:
        n, tn = b.shape[0] * b.shape[2], b.shape[2]
        assert b.shape[1] == kdim and shards in (1, b.shape[0])
    else:
        n = b.shape[0] if tb else b.shape[1]
        assert (b.shape[1] if tb else b.shape[0]) == kdim, (a.shape, b.shape, ta, tb)
        tn = n // shards if shards > 1 else _tile(n, tn)
    tm = _tile(m, tm)
    tk = tk if a_blocked else _tile(kdim, tk)
    kgroup = 2 if (a_blocked and tb and a.shape[0] % 2 == 0) else 1
    nk = kdim // (tk * kgroup)
    dims = (((0 if ta else 1,), (1 if tb else 0,)), ((), ()))

    def body(a_ref, b_ref, *rest):
        o_ref, *acc = rest[len(after):]
        if kgroup == 1:
            part = lax.dot_general(a_ref[...], b_ref[...], dims, preferred_element_type=F32)
        else:
            part = sum(lax.dot_general(a_ref[g], b_ref[:, g * tk:(g + 1) * tk], dims, preferred_element_type=F32) for g in range(kgroup))
        if nk == 1:
            o_ref[...] = part.astype(o_ref.dtype)
            return
        acc_ref, = acc
        k = pl.program_id(2)

        @pl.when(k == 0)
        def _():
            acc_ref[...] = part

        @pl.when(k > 0)
        def _():
            acc_ref[...] += part

        @pl.when(k == nk - 1)
        def _():
            o_ref[...] = acc_ref[...].astype(o_ref.dtype)

    if a_blocked and kgroup > 1:
        a_spec = pl.BlockSpec((kgroup, tm, tk), lambda i, j, k: (k, i, 0))
    elif a_blocked:
        a_spec = pl.BlockSpec((None, tm, tk), lambda i, j, k: (k, i, 0))
    else:
        a_spec = pl.BlockSpec((tk, tm), lambda i, j, k: (k, i)) if ta else pl.BlockSpec((tm, tk), lambda i, j, k: (i, k))
    if b_blocked:
        b_spec = pl.BlockSpec((None, tk, tn), lambda i, j, k: (j, k, 0))
    else:
        b_spec = pl.BlockSpec((tn, tk * kgroup), lambda i, j, k: (j, k)) if tb else pl.BlockSpec((tk, tn), lambda i, j, k: (k, j))
    if shards > 1:
        out_shape = jax.ShapeDtypeStruct((shards, m, tn), out_dtype)
        o_spec = pl.BlockSpec((None, tm, tn), lambda i, j, k: (j, i, 0))
    else:
        out_shape = jax.ShapeDtypeStruct((m, n), out_dtype)
        o_spec = pl.BlockSpec((tm, tn), lambda i, j, k: (i, j))
    return pl.pallas_call(
        body, grid=(m // tm, n // tn, nk), in_specs=[a_spec, b_spec] + [ANY] * len(after), out_specs=o_spec, out_shape=out_shape,
        scratch_shapes=[pltpu.VMEM((tm, tn), F32)] if nk > 1 else [],
        compiler_params=_params(("parallel", "parallel", "arbitrary")), name=name,
    )(a, b, *after)


def _norm_fwd(x, g, out_dtype, name, after=()):
    s, d = x.shape
    ts = _tile(s, TS_ROW, 8)

    def body(x_ref, g_ref, *rest):
        o_ref = rest[len(after)]
        xh, _ = _rms(x_ref[...])
        o_ref[...] = (xh * g_ref[...]).astype(o_ref.dtype)

    return pl.pallas_call(
        body, grid=(s // ts,), in_specs=[_rows(ts, d), _fixed((1, d))] + [ANY] * len(after), out_specs=_rows(ts, d),
        out_shape=jax.ShapeDtypeStruct((s, d), out_dtype), compiler_params=_params(("parallel",)), name=name,
    )(x, g, *after)


def _resid_norm_fwd(x, f, g_post, alpha, g_next, name, after=()):
    s, d = x.shape
    ts = _tile(s, TS_ROW, 8)
    with_h = g_next is not None

    def body(x_ref, f_ref, gp_ref, *rest):
        rest = rest[:1] + rest[1 + len(after):] if with_h else rest[len(after):]
        fh, _ = _rms(f_ref[...])
        xn = x_ref[...] + alpha * (fh * gp_ref[...])
        if with_h:
            gn_ref, xo_ref, h_ref = rest
            xh, _ = _rms(xn)
            h_ref[...] = (xh * gn_ref[...]).astype(h_ref.dtype)
        else:
            xo_ref, = rest
        xo_ref[...] = xn

    ins = [x, f, g_post] + ([g_next] if with_h else []) + list(after)
    in_specs = [_rows(ts, d), _rows(ts, d), _fixed((1, d))] + ([_fixed((1, d))] if with_h else []) + [ANY] * len(after)
    out_shape = [jax.ShapeDtypeStruct((s, d), F32)] + ([jax.ShapeDtypeStruct((s, d), BF16)] if with_h else [])
    out_specs = [_rows(ts, d)] + ([_rows(ts, d)] if with_h else [])
    out = pl.pallas_call(
        body, grid=(s // ts,), in_specs=in_specs, out_specs=out_specs, out_shape=out_shape,
        compiler_params=_params(("parallel",)), name=name,
    )(*ins)
    return (out[0], out[1]) if with_h else (out[0], None)


def _rms_bwd(x, g, dys, dres, alpha, out_dtype, name):
    s, d = x.shape
    ts = _tile(s, TS_ROW, 8)
    ndy = len(dys)
    with_res = dres is not None

    def body(x_ref, g_ref, *rest):
        dy_refs = rest[:ndy]
        rest = rest[ndy:]
        if with_res:
            dres_ref, dx_ref, dg_ref = rest
        else:
            dx_ref, dg_ref = rest
        xh, r = _rms(x_ref[...])
        dy = dy_refs[0][...].astype(F32)
        for ref in dy_refs[1:]:
            dy = dy + ref[...].astype(F32)
        dy = dy * alpha

        @pl.when(pl.program_id(0) == 0)
        def _():
            dg_ref[...] = jnp.zeros_like(dg_ref)

        dg_ref[...] += jnp.sum(dy * xh, axis=0, keepdims=True)
        dyg = dy * g_ref[...]
        dx = r * (dyg - xh * jnp.mean(dyg * xh, axis=-1, keepdims=True))
        if with_res:
            dx = dx + dres_ref[...]
        dx_ref[...] = dx.astype(dx_ref.dtype)

    ins = [x, g] + list(dys) + ([dres] if with_res else [])
    in_specs = [_rows(ts, d), _fixed((1, d))] + [_rows(ts, d)] * (ndy + int(with_res))
    return pl.pallas_call(
        body, grid=(s // ts,), in_specs=in_specs, out_specs=[_rows(ts, d), _fixed((1, d))],
        out_shape=[jax.ShapeDtypeStruct((s, d), out_dtype), jax.ShapeDtypeStruct((1, d), F32)],
        compiler_params=_params(("arbitrary",)), name=name,
    )(*ins)


def _loss_bwd(x, g, target, name):
    s, d = x.shape
    ts = _tile(s, TS_ROW, 8)

    def body(x_ref, g_ref, t_ref, dx_ref, dg_ref, loss_ref):
        xh, r = _rms(x_ref[...])
        gv = g_ref[...]
        diff = xh * gv - t_ref[...]

        @pl.when(pl.program_id(0) == 0)
        def _():
            dg_ref[...] = jnp.zeros_like(dg_ref)
            loss_ref[...] = jnp.zeros_like(loss_ref)

        sq = jnp.sum(diff * diff, axis=1, keepdims=True)
        loss_ref[...] += (0.5 / d) * jnp.sum(sq, axis=0, keepdims=True)
        dy = diff * (1.0 / d)
        dg_ref[...] += jnp.sum(dy * xh, axis=0, keepdims=True)
        dyg = dy * gv
        dx_ref[...] = r * (dyg - xh * jnp.mean(dyg * xh, axis=-1, keepdims=True))

    return pl.pallas_call(
        body, grid=(s // ts,), in_specs=[_rows(ts, d), _fixed((1, d)), _rows(ts, d)],
        out_specs=[_rows(ts, d), _fixed((1, d)), _fixed((8, LANE))],
        out_shape=[jax.ShapeDtypeStruct((s, d), F32), jax.ShapeDtypeStruct((1, d), F32), jax.ShapeDtypeStruct((8, LANE), F32)],
        compiler_params=_params(("arbitrary",)), name=name,
    )(x, g, target)


HALF_FF = DFF // 2


def _ffn_in_swiglu(x_norm, w_in, name, tm=1024):
    s, d = x_norm.shape
    tm = _tile(s, tm)

    def body(x_ref, wa_ref, wb_ref, ab_ref, u_ref):
        xv = x_ref[...]
        a = jnp.dot(xv, wa_ref[...], preferred_element_type=F32)
        b = jnp.dot(xv, wb_ref[...], preferred_element_type=F32)
        ab_ref[0] = a.astype(ab_ref.dtype)
        ab_ref[1] = b.astype(ab_ref.dtype)
        u_ref[...] = (a * _sigmoid(a) * b).astype(u_ref.dtype)

    ab, u = pl.pallas_call(
        body, grid=(s // tm, 2),
        in_specs=[pl.BlockSpec((tm, d), lambda i, j: (i, 0)), pl.BlockSpec((d, HALF_FF), lambda i, j: (0, j)),
                  pl.BlockSpec((d, HALF_FF), lambda i, j: (0, 2 + j))],
        out_specs=[pl.BlockSpec((2, None, tm, HALF_FF), lambda i, j: (0, j, i, 0)), pl.BlockSpec((tm, HALF_FF), lambda i, j: (i, j))],
        out_shape=[jax.ShapeDtypeStruct((2, 2, s, HALF_FF), BF16), jax.ShapeDtypeStruct((s, DFF), BF16)],
        compiler_params=_params(("parallel", "parallel")), name=name,
    )(x_norm, w_in, w_in)
    return ab.reshape(4, s, HALF_FF), u


def _ffn_out_dx_swiglu(dz, w_out, ab, after, name, tm=1024):
    s, d = dz.shape
    tm = _tile(s, tm)

    def body(dz_ref, w_ref, ab_ref, *rest):
        dab_ref = rest[len(after)]
        du = lax.dot_general(dz_ref[...], w_ref[...], (((1,), (1,)), ((), ())), preferred_element_type=F32)
        a = ab_ref[0].astype(F32)
        b = ab_ref[1].astype(F32)
        sig = _sigmoid(a)
        dab_ref[0] = (du * b * (sig * (1.0 + a * (1.0 - sig)))).astype(dab_ref.dtype)
        dab_ref[1] = (du * a * sig).astype(dab_ref.dtype)

    halves = pl.BlockSpec((2, None, tm, HALF_FF), lambda i, j: (0, j, i, 0))
    dab = pl.pallas_call(
        body, grid=(s // tm, 2),
        in_specs=[pl.BlockSpec((tm, d), lambda i, j: (i, 0)), pl.BlockSpec((HALF_FF, d), lambda i, j: (j, 0)), halves] + [ANY] * len(after),
        out_specs=halves, out_shape=jax.ShapeDtypeStruct((2, 2, s, HALF_FF), BF16),
        compiler_params=_params(("parallel", "parallel")), name=name,
    )(dz, w_out, ab.reshape(2, 2, s, HALF_FF), *after)
    return dab.reshape(4, s, HALF_FF)


def _tri(strict):
    r = lax.broadcasted_iota(jnp.int32, (CHUNK, CHUNK), 0)
    c = lax.broadcasted_iota(jnp.int32, (CHUNK, CHUNK), 1)
    return (r > c).astype(F32) if strict else (r >= c).astype(F32)


def _gla_fwd(pg, wfu, b_f, gnorm, name):
    s = pg.shape[0]
    ts = _tile(s, TS_GLA, CHUNK)
    cpb = ts // CHUNK
    nc = s // CHUNK

    def body(pg_ref, wfu_ref, bf_ref, gn_ref, ya_ref, sp_ref, so_ref, o_ref, st_ref, la_ref, dec_ref, u_ref):
        @pl.when(pl.program_id(0) == 0)
        def _():
            st_ref[...] = jnp.zeros_like(st_ref)

        f = jnp.dot(pg_ref[:, PG_F:PG_W], wfu_ref[...], preferred_element_type=F32) + bf_ref[...]
        la_ref[...] = _log_sigmoid(f) * (1.0 / GATE_TEMP)
        tri = _tri(False)
        chunks = [slice(ci * CHUNK, (ci + 1) * CHUNK) for ci in range(cpb)]
        for ci, rows in enumerate(chunks):
            la = la_ref[rows, :]
            b = jnp.dot(tri, la, precision=HIGHEST, preferred_element_type=F32)
            bend = jnp.sum(la, axis=0, keepdims=True)
            e = jnp.exp(bend - b)
            dec_ref[ci:ci + 1, :] = jnp.exp(bend)
            for hd in range(HEADS):
                k = pg_ref[rows, PG_K + hd * HDK:PG_K + (hd + 1) * HDK]
                v = pg_ref[rows, PG_V + hd * HDV:PG_V + (hd + 1) * HDV]
                kt = (k.astype(F32) * e[:, hd * HDK:(hd + 1) * HDK]).astype(BF16)
                u_ref[ci, hd] = lax.dot_general(v, kt, (((0,), (0,)), ((), ())), preferred_element_type=F32)
        for ci in range(cpb):
            for hd in range(HEADS):
                prev = st_ref[hd]
                sp_ref[ci, hd] = prev
                st = prev * dec_ref[ci:ci + 1, hd * HDK:(hd + 1) * HDK] + u_ref[ci, hd]
                st_ref[hd] = st
                so_ref[ci, hd] = st.astype(so_ref.dtype)
        for ci, rows in enumerate(chunks):
            for hd in range(HEADS):
                vc = slice(hd * HDV, (hd + 1) * HDV)
                q = pg_ref[rows, PG_Q + hd * HDK:PG_Q + (hd + 1) * HDK]
                go = pg_ref[rows, PG_G + hd * HDV:PG_G + (hd + 1) * HDV].astype(F32)
                qs = (q.astype(F32) * Q_SCALE).astype(BF16)
                o = lax.dot_general(qs, so_ref[ci, hd], (((1,), (1,)), ((), ())), preferred_element_type=F32)
                o_ref[rows, vc] = o
                oh, _ = _rms(o)
                ya_ref[rows, vc] = (oh * gn_ref[:, vc] * (go * _sigmoid(go))).astype(ya_ref.dtype)

    return pl.pallas_call(
        body, grid=(s // ts,),
        in_specs=[_rows(ts, PG_W), _fixed((LANE, HEADS * HDK)), _fixed((1, HEADS * HDK)), _fixed((1, HEADS * HDV))],
        out_specs=[_rows(ts, HEADS * HDV), pl.BlockSpec((cpb, HEADS, HDV, HDK), lambda i: (i, 0, 0, 0)),
                   pl.BlockSpec((cpb, HEADS, HDV, HDK), lambda i: (i, 0, 0, 0)), _rows(ts, HEADS * HDV)],
        out_shape=[jax.ShapeDtypeStruct((s, HEADS * HDV), BF16), jax.ShapeDtypeStruct((nc, HEADS, HDV, HDK), F32),
                   jax.ShapeDtypeStruct((nc, HEADS, HDV, HDK), BF16), jax.ShapeDtypeStruct((s, HEADS * HDV), F32)],
        scratch_shapes=[pltpu.VMEM((HEADS, HDV, HDK), F32), pltpu.VMEM((ts, HEADS * HDK), F32),
                        pltpu.VMEM((max(cpb, 8), HEADS * HDK), F32), pltpu.VMEM((cpb, HEADS, HDV, HDK), F32)],
        compiler_params=_params(("arbitrary",)), name=name,
    )(pg, wfu, b_f, gnorm)


def _gla_bwd(pg, sp, so, o, dya, wfu, b_f, gnorm, name):
    s = pg.shape[0]
    ts = _tile(s, TS_GLA, CHUNK)
    cpb = ts // CHUNK
    nblk = s // ts

    def body(pg_ref, sp_ref, so_ref, o_ref, dya_ref, wfu_ref, bf_ref, gn_ref, dpg_ref, dwfu_ref, dbf_ref, dgn_ref,
             dst_ref, la_ref, sg_ref, df_ref, e_ref, ktf_ref, dec_ref, g_ref):
        @pl.when(pl.program_id(0) == 0)
        def _():
            dst_ref[...] = jnp.zeros_like(dst_ref)
            dwfu_ref[...] = jnp.zeros_like(dwfu_ref)
            dbf_ref[...] = jnp.zeros_like(dbf_ref)
            dgn_ref[...] = jnp.zeros_like(dgn_ref)

        flow = pg_ref[:, PG_F:PG_W]
        f = jnp.dot(flow, wfu_ref[...], preferred_element_type=F32) + bf_ref[...]
        la_ref[...] = _log_sigmoid(f) * (1.0 / GATE_TEMP)
        sg_ref[...] = _sigmoid(-f) * (1.0 / GATE_TEMP)
        tri = _tri(False)
        tri_strict = _tri(True)
        chunks = [slice(ci * CHUNK, (ci + 1) * CHUNK) for ci in range(cpb)]
        for ci, rows in enumerate(chunks):
            la = la_ref[rows, :]
            b = jnp.dot(tri, la, precision=HIGHEST, preferred_element_type=F32)
            bend = jnp.sum(la, axis=0, keepdims=True)
            e = jnp.exp(bend - b)
            e_ref[rows, :] = e
            dec = jnp.exp(bend)
            dec_ref[ci:ci + 1, :] = dec
            for hd in range(HEADS):
                kc = slice(hd * HDK, (hd + 1) * HDK)
                vc = slice(hd * HDV, (hd + 1) * HDV)
                q = pg_ref[rows, PG_Q + hd * HDK:PG_Q + (hd + 1) * HDK]
                k = pg_ref[rows, PG_K + hd * HDK:PG_K + (hd + 1) * HDK]
                go = pg_ref[rows, PG_G + hd * HDV:PG_G + (hd + 1) * HDV].astype(F32)
                ktf_ref[rows, kc] = k.astype(F32) * e[:, kc]
                st_b = so_ref[ci, hd]
                qs = (q.astype(F32) * Q_SCALE).astype(BF16)
                oh, r = _rms(o_ref[rows, vc])
                gh = gn_ref[:, vc]
                sig = _sigmoid(go)
                dy = dya_ref[rows, vc].astype(F32)
                don = dy * (go * sig)
                dgn_ref[:, vc] += jnp.sum(don * oh, axis=0, keepdims=True)
                dong = don * gh
                do = (r * (dong - oh * jnp.mean(dong * oh, axis=-1, keepdims=True))).astype(BF16)
                g_ref[ci, hd] = lax.dot_general(do, qs, (((0,), (0,)), ((), ())), preferred_element_type=F32)
                dq = jnp.dot(do, st_b, preferred_element_type=F32) * Q_SCALE
                dpg_ref[rows, PG_Q + hd * HDK:PG_Q + (hd + 1) * HDK] = dq.astype(dpg_ref.dtype)
                dgo = dy * (oh * gh) * (sig * (1.0 + go * (1.0 - sig)))
                dpg_ref[rows, PG_G + hd * HDV:PG_G + (hd + 1) * HDV] = dgo.astype(dpg_ref.dtype)
        for ci in reversed(range(cpb)):
            for hd in range(HEADS):
                dst = dst_ref[hd] + g_ref[ci, hd]
                g_ref[ci, hd] = dst
                dst_ref[hd] = dst * dec_ref[ci:ci + 1, hd * HDK:(hd + 1) * HDK]
        for ci, rows in enumerate(chunks):
            for hd in range(HEADS):
                kc = slice(hd * HDK, (hd + 1) * HDK)
                v = pg_ref[rows, PG_V + hd * HDV:PG_V + (hd + 1) * HDV]
                ktf = ktf_ref[rows, kc]
                dst = g_ref[ci, hd]
                dst_b = dst.astype(BF16)
                dkt = jnp.dot(v, dst_b, preferred_element_type=F32)
                dv = lax.dot_general(ktf.astype(BF16), dst_b, (((1,), (1,)), ((), ())), preferred_element_type=F32)
                dd = jnp.sum(dst * sp_ref[ci, hd], axis=0, keepdims=True)
                dla = jnp.dot(tri_strict, dkt * ktf, precision=HIGHEST, preferred_element_type=F32) + dd * dec_ref[ci:ci + 1, kc]
                df_ref[rows, kc] = dla * sg_ref[rows, kc]
                dpg_ref[rows, PG_K + hd * HDK:PG_K + (hd + 1) * HDK] = (dkt * e_ref[rows, kc]).astype(dpg_ref.dtype)
                dpg_ref[rows, PG_V + hd * HDV:PG_V + (hd + 1) * HDV] = dv.astype(dpg_ref.dtype)
        df = df_ref[...]
        df_b = df.astype(BF16)
        dpg_ref[:, PG_F:PG_W] = lax.dot_general(df_b, wfu_ref[...], (((1,), (1,)), ((), ())), preferred_element_type=F32).astype(dpg_ref.dtype)
        dwfu_ref[...] += lax.dot_general(flow, df_b, (((0,), (0,)), ((), ())), preferred_element_type=F32)
        dbf_ref[...] += jnp.sum(df, axis=0, keepdims=True)

    rev = lambda i: (nblk - 1 - i, 0)
    return pl.pallas_call(
        body, grid=(nblk,),
        in_specs=[pl.BlockSpec((ts, PG_W), rev), pl.BlockSpec((cpb, HEADS, HDV, HDK), lambda i: (nblk - 1 - i, 0, 0, 0)),
                  pl.BlockSpec((cpb, HEADS, HDV, HDK), lambda i: (nblk - 1 - i, 0, 0, 0)), pl.BlockSpec((ts, HEADS * HDV), rev),
                  pl.BlockSpec((ts, HEADS * HDV), rev), _fixed((LANE, HEADS * HDK)), _fixed((1, HEADS * HDK)), _fixed((1, HEADS * HDV))],
        out_specs=[pl.BlockSpec((ts, PG_W), rev), _fixed((LANE, HEADS * HDK)), _fixed((1, HEADS * HDK)), _fixed((1, HEADS * HDV))],
        out_shape=[jax.ShapeDtypeStruct((s, PG_W), BF16), jax.ShapeDtypeStruct((LANE, HEADS * HDK), F32),
                   jax.ShapeDtypeStruct((1, HEADS * HDK), F32), jax.ShapeDtypeStruct((1, HEADS * HDV), F32)],
        scratch_shapes=[pltpu.VMEM((HEADS, HDV, HDK), F32)] + [pltpu.VMEM((ts, HEADS * HDK), F32)] * 5
        + [pltpu.VMEM((max(cpb, 8), HEADS * HDK), F32), pltpu.VMEM((cpb, HEADS, HDV, HDK), F32)],
        compiler_params=_params(("arbitrary",)), name=name,
    )(pg, sp, so, o, dya, wfu, b_f, gnorm)


def _window_sums(ext, sign):
    n = ext.shape[0]
    sums = {1: ext}
    w = 1
    while w < POOL_WINDOWS[-1]:
        sums[2 * w] = sums[w] + pltpu.roll(sums[w], w if sign > 0 else n - w, 0)
        w *= 2
    return [sums[POOL_WINDOWS[g]][:, g * LANE:(g + 1) * LANE] for g in range(len(POOL_WINDOWS))]


def _pool_counts(row0, n):
    pos = (row0 + lax.broadcasted_iota(jnp.int32, (n, 1), 0) + 1).astype(F32)
    return [jnp.minimum(pos, float(w)) for w in POOL_WINDOWS]


def _pool_fwd(ppx, w_pool, pool_scale, name):
    s = ppx.shape[0]
    ts = _tile(s, TS_POOL, POOL_HALO)
    hb = ts // POOL_HALO
    pw = len(POOL_WINDOWS) * LANE

    def body(p_ref, halo_ref, w_ref, sc_ref, y_ref, ext_ref):
        i = pl.program_id(0)
        p = p_ref[...].astype(F32)
        ext_ref[0:POOL_HALO, :] = jnp.where(i > 0, halo_ref[...].astype(F32), 0.0)
        ext_ref[POOL_HALO:, :] = p
        sums = _window_sums(ext_ref[...], +1)
        cnt = _pool_counts(i * ts, ts)
        for g in range(len(POOL_WINDOWS)):
            cols = slice(g * LANE, (g + 1) * LANE)
            mixed = sums[g][POOL_HALO:, :] / cnt[g] - p[:, cols]
            y = jnp.dot(mixed.astype(BF16), w_ref[g], preferred_element_type=F32)
            y_ref[:, cols] = (y * sc_ref[:, cols]).astype(y_ref.dtype)

    return pl.pallas_call(
        body, grid=(s // ts,),
        in_specs=[pl.BlockSpec((ts, pw), lambda i: (i, 0)), pl.BlockSpec((POOL_HALO, pw), lambda i: (jnp.maximum(i * hb - 1, 0), 0)),
                  _fixed((len(POOL_WINDOWS), LANE, LANE)), _fixed((1, pw))],
        out_specs=_rows(ts, pw), out_shape=jax.ShapeDtypeStruct((s, pw), BF16),
        scratch_shapes=[pltpu.VMEM((ts + POOL_HALO, pw), F32)],
        compiler_params=_params(("parallel",)), name=name,
    )(ppx, ppx, w_pool, pool_scale)


def _pool_bwd(dyb, ppx, w_pool, pool_scale, name):
    s = ppx.shape[0]
    ts = _tile(s, TS_POOL, POOL_HALO)
    hb = ts // POOL_HALO
    nblk = s // ts
    last_halo = s // POOL_HALO - 1
    ng = len(POOL_WINDOWS)
    pw = ng * LANE

    def body(p_ref, halo_ref, dy_ref, dyn_ref, w_ref, sc_ref, dp_ref, dw_ref, dsc_ref, ext_ref, dext_ref, dm_ref):
        i = pl.program_id(0)

        @pl.when(i == 0)
        def _():
            dw_ref[...] = jnp.zeros_like(dw_ref)
            dsc_ref[...] = jnp.zeros_like(dsc_ref)

        p = p_ref[...].astype(F32)
        ext_ref[0:POOL_HALO, :] = jnp.where(i > 0, halo_ref[...].astype(F32), 0.0)
        ext_ref[POOL_HALO:, :] = p
        sums = _window_sums(ext_ref[...], +1)
        cnt = _pool_counts(i * ts, ts + POOL_HALO)
        sc = sc_ref[...]
        dy = dy_ref[...].astype(F32)
        dyn = jnp.where(i < nblk - 1, dyn_ref[...].astype(F32), 0.0)
        for g in range(ng):
            cols = slice(g * LANE, (g + 1) * LANE)
            wg = w_ref[g]
            mixed = (sums[g][POOL_HALO:, :] / cnt[g][0:ts] - p[:, cols]).astype(BF16)
            ypre = jnp.dot(mixed, wg, preferred_element_type=F32)
            dsc_ref[:, cols] += jnp.sum(dy[:, cols] * ypre, axis=0, keepdims=True)
            dyp = (dy[:, cols] * sc[:, cols]).astype(BF16)
            dypn = (dyn[:, cols] * sc[:, cols]).astype(BF16)
            dw_ref[g] += lax.dot_general(mixed, dyp, (((0,), (0,)), ((), ())), preferred_element_type=F32)
            dm = lax.dot_general(dyp, wg, (((1,), (1,)), ((), ())), preferred_element_type=F32)
            dmn = lax.dot_general(dypn, wg, (((1,), (1,)), ((), ())), preferred_element_type=F32)
            dext_ref[0:ts, cols] = dm / cnt[g][0:ts]
            dext_ref[ts:, cols] = dmn / cnt[g][ts:]
            dm_ref[:, cols] = dm
        lead = _window_sums(dext_ref[...], -1)
        for g in range(ng):
            cols = slice(g * LANE, (g + 1) * LANE)
            dp_ref[:, cols] = (lead[g][0:ts, :] - dm_ref[:, cols]).astype(dp_ref.dtype)

    return pl.pallas_call(
        body, grid=(nblk,),
        in_specs=[pl.BlockSpec((ts, pw), lambda i: (i, 0)), pl.BlockSpec((POOL_HALO, pw), lambda i: (jnp.maximum(i * hb - 1, 0), 0)),
                  pl.BlockSpec((ts, pw), lambda i: (i, 0)), pl.BlockSpec((POOL_HALO, pw), lambda i: (jnp.minimum((i + 1) * hb, last_halo), 0)),
                  _fixed((ng, LANE, LANE)), _fixed((1, pw))],
        out_specs=[_rows(ts, pw), _fixed((ng, LANE, LANE)), _fixed((1, pw))],
        out_shape=[jax.ShapeDtypeStruct((s, pw), BF16), jax.ShapeDtypeStruct((ng, LANE, LANE), F32), jax.ShapeDtypeStruct((1, pw), F32)],
        scratch_shapes=[pltpu.VMEM((ts + POOL_HALO, pw), F32), pltpu.VMEM((ts + POOL_HALO, pw), F32), pltpu.VMEM((ts, pw), F32)],
        compiler_params=_params(("arbitrary",)), name=name,
    )(ppx, ppx, dyb, dyb, w_pool, pool_scale)


def _xattn_fwd(ppx, kv, name):
    s = ppx.shape[0]
    m = kv.shape[0]
    ts = _tile(s, TS_XA, 8)
    xw = XA_HEADS * XA_HD

    def body(q_ref, kv_ref, o_ref):
        for hd in range(XA_HEADS):
            cols = slice(hd * XA_HD, (hd + 1) * XA_HD)
            k = kv_ref[:, hd * XA_HD:(hd + 1) * XA_HD]
            v = kv_ref[:, xw + hd * XA_HD:xw + (hd + 1) * XA_HD]
            sc = lax.dot_general(q_ref[:, cols], k, (((1,), (1,)), ((), ())), preferred_element_type=F32) * XA_SCALE
            ex = jnp.exp(sc - jnp.max(sc, axis=-1, keepdims=True))
            pr = ex / jnp.sum(ex, axis=-1, keepdims=True)
            o_ref[:, cols] = jnp.dot(pr.astype(BF16), v, preferred_element_type=F32).astype(o_ref.dtype)

    return pl.pallas_call(
        body, grid=(s // ts,), in_specs=[pl.BlockSpec((ts, xw), lambda i: (i, 1)), _fixed((m, 2 * xw))],
        out_specs=_rows(ts, xw), out_shape=jax.ShapeDtypeStruct((s, xw), BF16),
        compiler_params=_params(("parallel",)), name=name,
    )(ppx, kv)


def _xattn_bwd(dxc, ppx, kv, name):
    s = ppx.shape[0]
    m = kv.shape[0]
    ts = _tile(s, TS_XA, 8)
    xw = XA_HEADS * XA_HD

    def body(do_ref, q_ref, kv_ref, dq_ref, dkv_ref):
        @pl.when(pl.program_id(0) == 0)
        def _():
            dkv_ref[...] = jnp.zeros_like(dkv_ref)

        for hd in range(XA_HEADS):
            cols = slice(hd * XA_HD, (hd + 1) * XA_HD)
            vcols = slice(xw + hd * XA_HD, xw + (hd + 1) * XA_HD)
            q = q_ref[:, cols]
            k = kv_ref[:, cols]
            v = kv_ref[:, vcols]
            do = do_ref[:, cols]
            sc = lax.dot_general(q, k, (((1,), (1,)), ((), ())), preferred_element_type=F32) * XA_SCALE
            ex = jnp.exp(sc - jnp.max(sc, axis=-1, keepdims=True))
            pr = ex / jnp.sum(ex, axis=-1, keepdims=True)
            dpr = lax.dot_general(do, v, (((1,), (1,)), ((), ())), preferred_element_type=F32)
            dsc = (pr * (dpr - jnp.sum(dpr * pr, axis=-1, keepdims=True)) * XA_SCALE).astype(BF16)
            dq_ref[:, cols] = jnp.dot(dsc, k, preferred_element_type=F32).astype(dq_ref.dtype)
            dkv_ref[:, cols] += lax.dot_general(dsc, q, (((0,), (0,)), ((), ())), preferred_element_type=F32)
            dkv_ref[:, vcols] += lax.dot_general(pr.astype(BF16), do, (((0,), (0,)), ((), ())), preferred_element_type=F32)

    return pl.pallas_call(
        body, grid=(s // ts,), in_specs=[_rows(ts, xw), pl.BlockSpec((ts, xw), lambda i: (i, 1)), _fixed((m, 2 * xw))],
        out_specs=[_rows(ts, xw), _fixed((m, 2 * xw))],
        out_shape=[jax.ShapeDtypeStruct((s, xw), BF16), jax.ShapeDtypeStruct((m, 2 * xw), F32)],
        compiler_params=_params(("arbitrary",)), name=name,
    )(dxc, ppx, kv)


def _merge_fwd(pgt, ya, yb, yc, name):
    s = pgt.shape[0]
    ts = _tile(s, TS_ROW, 8)

    def body(gt_ref, ya_ref, yb_ref, yc_ref, o_ref):
        acc = _sigmoid(gt_ref[:, 0:D].astype(F32)) * ya_ref[...].astype(F32)
        acc = acc + _sigmoid(gt_ref[:, D:2 * D].astype(F32)) * yb_ref[...].astype(F32)
        acc = acc + _sigmoid(gt_ref[:, 2 * D:3 * D].astype(F32)) * yc_ref[...].astype(F32)
        o_ref[...] = acc.astype(o_ref.dtype)

    return pl.pallas_call(
        body, grid=(s // ts,), in_specs=[_rows(ts, 3 * D)] + [_rows(ts, D)] * 3, out_specs=_rows(ts, D),
        out_shape=jax.ShapeDtypeStruct((s, D), BF16), compiler_params=_params(("parallel",)), name=name,
    )(pgt, ya, yb, yc)


def _merge_bwd(dmerged, pgt, ya, yb, yc, name):
    s = pgt.shape[0]
    ts = _tile(s, TS_ROW, 8)

    def body(dm_ref, gt_ref, ya_ref, yb_ref, yc_ref, dya_ref, dyb_ref, dyc_ref, dgt_ref):
        dm = dm_ref[...].astype(F32)
        for j, (y_ref, dy_ref) in enumerate(((ya_ref, dya_ref), (yb_ref, dyb_ref), (yc_ref, dyc_ref))):
            sig = _sigmoid(gt_ref[:, j * D:(j + 1) * D].astype(F32))
            dy_ref[...] = (dm * sig).astype(dy_ref.dtype)
            dgt_ref[:, j * D:(j + 1) * D] = (dm * y_ref[...].astype(F32) * sig * (1.0 - sig)).astype(dgt_ref.dtype)

    return pl.pallas_call(
        body, grid=(s // ts,), in_specs=[_rows(ts, D), _rows(ts, 3 * D)] + [_rows(ts, D)] * 3,
        out_specs=[_rows(ts, D)] * 3 + [_rows(ts, 3 * D)],
        out_shape=[jax.ShapeDtypeStruct((s, D), BF16)] * 3 + [jax.ShapeDtypeStruct((s, 3 * D), BF16)],
        compiler_params=_params(("parallel",)), name=name,
    )(dmerged, pgt, ya, yb, yc)


def _adam_math(w, g, m, v):
    mn = ADAM_B1 * m + (1.0 - ADAM_B1) * g
    vn = ADAM_B2 * v + (1.0 - ADAM_B2) * (g * g)
    m_hat = mn / (1.0 - ADAM_B1 ** ADAM_STEP)
    v_hat = vn / (1.0 - ADAM_B2 ** ADAM_STEP)
    return -ADAM_LR * (m_hat / (jnp.sqrt(v_hat) + ADAM_EPS) + ADAM_WD * w), mn, vn


def _adamw(w, g, m, v, name):
    r, c = w.shape[-2:]
    tr, tc = _block_of(r, c)

    def spec(a):
        if a.ndim == 2:
            return pl.BlockSpec((tr, tc), lambda i, j: (i, j))
        return pl.BlockSpec((None, tr, tc), lambda i, j: (0, i, j))

    def body(w_ref, g_ref, m_ref, v_ref, d_ref, mo_ref, vo_ref):
        d_ref[...], mo_ref[...], vo_ref[...] = _adam_math(w_ref[...], g_ref[...], m_ref[...], v_ref[...])

    return pl.pallas_call(
        body, grid=(r // tr, c // tc), in_specs=[spec(a) for a in (w, g, m, v)], out_specs=[spec(w)] * 3,
        out_shape=[jax.ShapeDtypeStruct(w.shape, F32)] * 3, compiler_params=_params(("parallel", "parallel")), name=name,
    )(w, g, m, v)


ANY = pl.BlockSpec(memory_space=pl.ANY)


def _place():
    x, y, c = lax.axis_index("x"), lax.axis_index("y"), lax.axis_index("c")
    chips = [(1 - x, y), (x, 1 - y), (1 - x, 1 - y)]
    return x, y, c, chips


def _half(c, rows):
    h = rows // 2
    return pl.ds(pl.multiple_of(c * h, 8), h)


def _by_cols(rows):
    return rows % 32 != 0 and rows != 16


def _half_of(ref, lead, c):
    r, cols = ref.shape[-2:]
    if _by_cols(r):
        return ref.at[(*lead, slice(None), pl.ds(pl.multiple_of(c * (cols // 2), LANE), cols // 2))]
    return ref.at[(*lead, pl.ds(pl.multiple_of(c * (r // 2), 8), r // 2))]


def _half_shape(shape):
    r, cols = shape[-2:]
    return shape[:-2] + ((r, cols // 2) if _by_cols(r) else (r // 2, cols))


def _block_of(r, cols, cap=256):
    if r % 16 == 0:
        return _tile(r, cap, 16), cols
    return r, _tile(cols, cap)


def _place_shard(shard, chip_arr, out_dtype, name):
    _, r, cols = shard.shape
    tr, tc = _block_of(r, cols)

    def body(chip_ref, s_ref, o_ref):
        o_ref[...] = s_ref[...].astype(o_ref.dtype)

    return pl.pallas_call(
        body,
        grid_spec=pltpu.PrefetchScalarGridSpec(
            num_scalar_prefetch=1, grid=(r // tr, cols // tc),
            in_specs=[pl.BlockSpec((None, tr, tc), lambda i, j, chip_ref: (0, i, j))],
            out_specs=pl.BlockSpec((None, tr, tc), lambda i, j, chip_ref: (chip_ref[0], i, j))),
        out_shape=jax.ShapeDtypeStruct((4, r, cols), out_dtype),
        compiler_params=_params(("parallel", "parallel")), name=name,
    )(chip_arr, shard)


def _gather_shards(bufs, name):
    n = len(bufs)

    def body(*refs):
        outs = refs[n:2 * n]
        send_ici, recv_ici, send_d2d, recv_d2d = refs[2 * n:]
        x, y, c, chips = _place()
        me = 2 * x + y
        sibling = (x, y, 1 - c)

        def ici(w, p, chip_of_block, to):
            rows = _half(c, outs[w].shape[1])
            block = outs[w].at[chip_of_block, rows]
            return pltpu.make_async_remote_copy(
                src_ref=block, dst_ref=block, send_sem=send_ici.at[w, p], recv_sem=recv_ici.at[w, p], device_id=to, device_id_type=MESH)

        def d2d(w, p, chip_of_block, half_of):
            rows = _half(half_of, outs[w].shape[1])
            block = outs[w].at[chip_of_block, rows]
            return pltpu.make_async_remote_copy(
                src_ref=block, dst_ref=block, send_sem=send_d2d.at[w, p], recv_sem=recv_d2d.at[w, p], device_id=sibling, device_id_type=MESH)

        sends = [ici(w, p, me, (*chip, c)) for p, chip in enumerate(chips) for w in range(n)]
        for cp in sends:
            cp.start()
        passed = []
        for p, (px, py) in enumerate(chips):
            for w in range(n):
                ici(w, p, 2 * px + py, (px, py, c)).wait_recv()
                fwd = d2d(w, p, 2 * px + py, c)
                fwd.start()
                passed.append(fwd)
        for p, (px, py) in enumerate(chips):
            for w in range(n):
                d2d(w, p, 2 * px + py, 1 - c).wait_recv()
        for cp in sends + passed:
            cp.wait_send()

    return pl.pallas_call(
        body, in_specs=[ANY] * n, out_specs=[ANY] * n,
        out_shape=[jax.ShapeDtypeStruct(a.shape, a.dtype) for a in bufs],
        input_output_aliases={w: w for w in range(n)},
        scratch_shapes=[pltpu.SemaphoreType.DMA((n, 3))] * 4,
        compiler_params=pltpu.CompilerParams(has_side_effects=True), name=name,
    )(*bufs)


HBM = pl.BlockSpec(memory_space=pltpu.HBM)
SEM = pl.BlockSpec(memory_space=pltpu.SEMAPHORE)
EFFECT = pltpu.SideEffectType.DATAFLOW_SIDE_EFFECTING


def _in_hbm(arrays):
    return [pltpu.with_memory_space_constraint(a, pltpu.HBM) for a in arrays]


def _gather_start(bufs, after, name):
    n, na = len(bufs), len(after)

    def body(*refs):
        send_sem, recv_sem = refs[n + na], refs[n + na + 1]
        outs = refs[n + na + 2:2 * n + na + 2]
        token = refs[2 * n + na + 2]
        x, y, c, chips = _place()
        me = 2 * x + y
        for p, chip in enumerate(chips):
            for w in range(n):
                block = _half_of(outs[w], (me,), c)
                pltpu.make_async_remote_copy(
                    src_ref=block, dst_ref=block, send_sem=send_sem, recv_sem=recv_sem,
                    device_id=(*chip, c), device_id_type=MESH).start()
        token[...] = jnp.zeros_like(token)

    out = pl.pallas_call(
        body, name=name, in_specs=[HBM] * n + [ANY] * na,
        out_specs=[SEM, SEM] + [HBM] * n + [pl.BlockSpec(memory_space=pltpu.VMEM)],
        out_shape=[pltpu.SemaphoreType.DMA(()), pltpu.SemaphoreType.DMA(())]
        + [pltpu.HBM(a.shape, a.dtype) for a in bufs] + [jax.ShapeDtypeStruct((8, LANE), F32)],
        input_output_aliases={w: w + 2 for w in range(n)},
        compiler_params=pltpu.CompilerParams(has_side_effects=EFFECT),
    )(*_in_hbm(bufs), *after)
    return out[0], out[1], list(out[2:2 + n]), out[2 + n]


def _gather_pass(bufs, send_sem, recv_sem, after, name):
    n, na = len(bufs), len(after)

    def body(*refs):
        send1, recv1 = refs[n], refs[n + 1]
        send2, recv2 = refs[n + 2 + na], refs[n + 3 + na]
        outs = refs[n + 4 + na:2 * n + 4 + na]
        x, y, c, chips = _place()
        me = 2 * x + y
        arrivals = [(w, px, py) for px, py in chips for w in range(n)]
        for w, px, py in arrivals:
            first = pltpu.make_async_remote_copy(
                src_ref=_half_of(outs[w], (me,), c), dst_ref=_half_of(outs[w], (2 * px + py,), c), send_sem=send1, recv_sem=recv1,
                device_id=(px, py, c), device_id_type=MESH)
            first.wait_send()
            first.wait_recv()
        for w, px, py in arrivals:
            arrived = _half_of(outs[w], (2 * px + py,), c)
            pltpu.make_async_remote_copy(
                src_ref=arrived, dst_ref=arrived, send_sem=send2, recv_sem=recv2,
                device_id=(x, y, 1 - c), device_id_type=MESH).start()

    out = pl.pallas_call(
        body, name=name, in_specs=[HBM] * n + [SEM, SEM] + [ANY] * na,
        out_specs=[SEM, SEM] + [HBM] * n,
        out_shape=[pltpu.SemaphoreType.DMA(()), pltpu.SemaphoreType.DMA(())] + [pltpu.HBM(a.shape, a.dtype) for a in bufs],
        input_output_aliases={w: w + 2 for w in range(n)},
        compiler_params=pltpu.CompilerParams(has_side_effects=EFFECT),
    )(*bufs, send_sem, recv_sem, *after)
    return out[0], out[1], list(out[2:])


def _gather_finish(bufs, send_sem, recv_sem, after, name):
    n, na = len(bufs), len(after)

    def body(*refs):
        send2, recv2 = refs[n], refs[n + 1]
        outs = refs[n + 2 + na:2 * n + 2 + na]
        x, y, c, chips = _place()
        for p, (px, py) in enumerate(chips):
            for w in range(n):
                passed = pltpu.make_async_remote_copy(
                    src_ref=_half_of(outs[w], (2 * px + py,), c), dst_ref=_half_of(outs[w], (2 * px + py,), 1 - c),
                    send_sem=send2, recv_sem=recv2, device_id=(x, y, 1 - c), device_id_type=MESH)
                passed.wait_send()
                passed.wait_recv()

    out = pl.pallas_call(
        body, name=name, in_specs=[HBM] * n + [SEM, SEM] + [ANY] * na, out_specs=[HBM] * n,
        out_shape=[pltpu.HBM(a.shape, a.dtype) for a in bufs],
        input_output_aliases={w: w for w in range(n)},
        compiler_params=pltpu.CompilerParams(has_side_effects=EFFECT),
    )(*bufs, send_sem, recv_sem, *after)
    return list(out)


def _pair_exchange(grads, name):
    n = len(grads)

    def body(*refs):
        ins, outs = refs[:n], refs[n:2 * n]
        send_sem, recv_sem = refs[2 * n:]
        x, y, c, _ = _place()
        copies = []
        for w in range(n):
            copies.append(pltpu.make_async_remote_copy(
                src_ref=_half_of(ins[w], (slice(None),), 1 - c), dst_ref=outs[w], send_sem=send_sem.at[w], recv_sem=recv_sem.at[w],
                device_id=(x, y, 1 - c), device_id_type=MESH))
        for cp in copies:
            cp.start()
        for cp in copies:
            cp.wait()

    return pl.pallas_call(
        body, in_specs=[ANY] * n, out_specs=[ANY] * n,
        out_shape=[jax.ShapeDtypeStruct(_half_shape(a.shape), a.dtype) for a in grads],
        scratch_shapes=[pltpu.SemaphoreType.DMA((n,))] * 2,
        compiler_params=pltpu.CompilerParams(has_side_effects=True), name=name,
    )(*grads)


def _pair_sum(g, got, c_arr, name):
    _, r, cols = g.shape
    hr, hc = _half_shape((r, cols))
    tr, tc = _block_of(hr, hc)
    nbr, nbc = hr // tr, hc // tc
    by_cols = _by_cols(r)

    def body(c_ref, g_ref, got_ref, o_ref):
        o_ref[...] = (g_ref[...].astype(F32) + got_ref[...].astype(F32)).astype(o_ref.dtype)

    def mine(j, i, k, c_ref):
        return (j, i, c_ref[0] * nbc + k) if by_cols else (j, c_ref[0] * nbr + i, k)

    return pl.pallas_call(
        body,
        grid_spec=pltpu.PrefetchScalarGridSpec(
            num_scalar_prefetch=1, grid=(4, nbr, nbc),
            in_specs=[pl.BlockSpec((None, tr, tc), mine),
                      pl.BlockSpec((None, tr, tc), lambda j, i, k, c_ref: (j, i, k))],
            out_specs=pl.BlockSpec((None, tr, tc), lambda j, i, k, c_ref: (j, i, k))),
        out_shape=jax.ShapeDtypeStruct((4, hr, hc), BF16),
        compiler_params=_params(("parallel", "parallel", "parallel")), name=name,
    )(c_arr, g, got)


def _chip_exchange(parts, name):
    n = len(parts)

    def body(*refs):
        ins, outs = refs[:n], refs[n:2 * n]
        send_sem, recv_sem = refs[2 * n:]
        x, y, c, chips = _place()
        copies = []
        for p, (px, py) in enumerate(chips):
            for w in range(n):
                copies.append(pltpu.make_async_remote_copy(
                    src_ref=ins[w].at[2 * px + py], dst_ref=outs[w].at[p], send_sem=send_sem.at[w, p], recv_sem=recv_sem.at[w, p],
                    device_id=(px, py, c), device_id_type=MESH))
        for cp in copies:
            cp.start()
        for cp in copies:
            cp.wait()

    return pl.pallas_call(
        body, in_specs=[ANY] * n, out_specs=[ANY] * n,
        out_shape=[jax.ShapeDtypeStruct((3,) + a.shape[1:], a.dtype) for a in parts],
        scratch_shapes=[pltpu.SemaphoreType.DMA((n, 3))] * 2,
        compiler_params=pltpu.CompilerParams(has_side_effects=True), name=name,
    )(*parts)


def _chip_exchange_start(parts, after, name):
    n, na = len(parts), len(after)
    lands = [lax.empty((3,) + a.shape[1:], a.dtype) for a in parts]

    def body(*refs):
        send_sem, recv_sem = refs[2 * n + na], refs[2 * n + na + 1]
        srcs = refs[2 * n + na + 2:3 * n + na + 2]
        dsts = refs[3 * n + na + 2:4 * n + na + 2]
        token = refs[4 * n + na + 2]
        x, y, c, chips = _place()
        for p, (px, py) in enumerate(chips):
            for w in range(n):
                pltpu.make_async_remote_copy(
                    src_ref=srcs[w].at[2 * px + py], dst_ref=dsts[w].at[p], send_sem=send_sem, recv_sem=recv_sem,
                    device_id=(px, py, c), device_id_type=MESH).start()
        token[...] = jnp.zeros_like(token)

    out = pl.pallas_call(
        body, name=name, in_specs=[HBM] * (2 * n) + [ANY] * na,
        out_specs=[SEM, SEM] + [HBM] * (2 * n) + [pl.BlockSpec(memory_space=pltpu.VMEM)],
        out_shape=[pltpu.SemaphoreType.DMA(()), pltpu.SemaphoreType.DMA(())]
        + [pltpu.HBM(a.shape, a.dtype) for a in parts + lands] + [jax.ShapeDtypeStruct((8, LANE), F32)],
        input_output_aliases={w: w + 2 for w in range(2 * n)},
        compiler_params=pltpu.CompilerParams(has_side_effects=EFFECT),
    )(*_in_hbm(parts), *_in_hbm(lands), *after)
    return out[0], out[1], list(out[2:2 + n]), list(out[2 + n:2 + 2 * n]), out[2 + 2 * n]


def _chip_exchange_finish(parts, lands, send_sem, recv_sem, after, name):
    n, na = len(parts), len(after)

    def body(*refs):
        send, recv = refs[2 * n], refs[2 * n + 1]
        srcs = refs[2 * n + 2 + na:3 * n + 2 + na]
        dsts = refs[3 * n + 2 + na:4 * n + 2 + na]
        x, y, c, chips = _place()
        for p, (px, py) in enumerate(chips):
            for w in range(n):
                copy = pltpu.make_async_remote_copy(
                    src_ref=srcs[w].at[2 * px + py], dst_ref=dsts[w].at[p], send_sem=send, recv_sem=recv,
                    device_id=(px, py, c), device_id_type=MESH)
                copy.wait_send()
                copy.wait_recv()

    out = pl.pallas_call(
        body, name=name, in_specs=[HBM] * (2 * n) + [SEM, SEM] + [ANY] * na, out_specs=[HBM] * (2 * n),
        out_shape=[pltpu.HBM(a.shape, a.dtype) for a in parts + lands],
        input_output_aliases={w: w for w in range(2 * n)},
        compiler_params=pltpu.CompilerParams(has_side_effects=EFFECT),
    )(*parts, *lands, send_sem, recv_sem, *after)
    return list(out[:n]), list(out[n:])


def _chip_sum(part, got, place_arr, name):
    _, hr, hc = part.shape
    by_cols = _by_cols(hr)
    tr, tc = _block_of(hr, hc)
    nbr, nbc = hr // tr, hc // tc

    def body(place_ref, p_ref, got_ref, o_ref):
        acc = p_ref[...].astype(F32)
        for p in range(3):
            acc = acc + got_ref[p].astype(F32)
        o_ref[...] = acc

    def mine(i, k, place_ref):
        return (i, place_ref[1] * nbc + k) if by_cols else (place_ref[1] * nbr + i, k)

    return pl.pallas_call(
        body,
        grid_spec=pltpu.PrefetchScalarGridSpec(
            num_scalar_prefetch=1, grid=(nbr, nbc),
            in_specs=[pl.BlockSpec((None, tr, tc), lambda i, k, place_ref: (place_ref[0], i, k)),
                      pl.BlockSpec((3, tr, tc), lambda i, k, place_ref: (0, i, k))],
            out_specs=pl.BlockSpec((tr, tc), mine)),
        out_shape=jax.ShapeDtypeStruct((hr, 2 * hc) if by_cols else (2 * hr, hc), F32),
        compiler_params=_params(("parallel", "parallel")), name=name,
    )(place_arr, part, got)


def _pair_join(bufs, name):
    n = len(bufs)

    def body(*refs):
        outs = refs[n:2 * n]
        send_sem, recv_sem = refs[2 * n:]
        x, y, c, _ = _place()
        copies = []
        for w in range(n):
            block = _half_of(outs[w], (), c)
            copies.append(pltpu.make_async_remote_copy(
                src_ref=block, dst_ref=block, send_sem=send_sem.at[w], recv_sem=recv_sem.at[w],
                device_id=(x, y, 1 - c), device_id_type=MESH))
        for cp in copies:
            cp.start()
        for w, cp in enumerate(copies):
            cp.wait_send()
            block = _half_of(outs[w], (), 1 - c)
            pltpu.make_async_remote_copy(
                src_ref=block, dst_ref=block, send_sem=send_sem.at[w], recv_sem=recv_sem.at[w],
                device_id=(x, y, 1 - c), device_id_type=MESH).wait_recv()

    return pl.pallas_call(
        body, in_specs=[ANY] * n, out_specs=[ANY] * n,
        out_shape=[jax.ShapeDtypeStruct(a.shape, a.dtype) for a in bufs],
        input_output_aliases={w: w for w in range(n)},
        scratch_shapes=[pltpu.SemaphoreType.DMA((n,))] * 2,
        compiler_params=pltpu.CompilerParams(has_side_effects=True), name=name,
    )(*bufs)


SMALL = ("ffn1_pre_g", "ffn1_post_g", "mix_pre_g", "gla_norm_g", "mem_norm_g", "mix_post_g", "ffn2_pre_g", "ffn2_post_g", "final_g",
         "b_f", "pool_scale", "w_pool", "w_fu")
N_GAINS = 9
SMALL_PACKS = ((16, D), (24, 512), (4 * LANE, LANE))
W_FU_ROW = 8


def _all_sum_small(gs, name):
    ins = [gs[n] for n in SMALL[:N_GAINS]] + [gs["b_f"], gs["pool_scale"], gs["w_fu_pad"], gs["w_pool"].reshape(4 * LANE, LANE)]

    def body(*refs):
        gain_refs = refs[:N_GAINS]
        bf_ref, ps_ref, wfu_ref, wp_ref = refs[N_GAINS:N_GAINS + 4]
        outs = refs[N_GAINS + 4:N_GAINS + 7]
        mine_a, mine_b, all_a, all_b, all_c, send_sems, recv_sems = refs[N_GAINS + 7:]
        mine_a[...] = jnp.zeros_like(mine_a)
        for i, ref in enumerate(gain_refs):
            mine_a[i:i + 1, :] = ref[...]
        mine_b[...] = jnp.zeros_like(mine_b)
        mine_b[0:1, :] = bf_ref[...]
        mine_b[1:2, :] = ps_ref[...]
        mine_b[W_FU_ROW:W_FU_ROW + GATE_RANK, :] = wfu_ref[0:GATE_RANK, :]
        packs = ((mine_a, all_a), (mine_b, all_b), (wp_ref, all_c))
        x, y, c, chips = _place()
        me, sibling = (x, y, c), (x, y, 1 - c)

        def copy(t, k, block, to, own=False):
            px, py, pc = block
            slot = packs[t][1].at[4 * px + 2 * py + pc]
            return pltpu.make_async_remote_copy(
                src_ref=packs[t][0] if own else slot, dst_ref=slot,
                send_sem=send_sems.at[t, k], recv_sem=recv_sems.at[t, k], device_id=to, device_id_type=MESH)

        started = []
        for t, (mine, everyone) in enumerate(packs):
            everyone[4 * x + 2 * y + c] = mine[...]
            started.append(copy(t, 0, me, sibling, own=True))
            started += [copy(t, 1 + j, me, (*chip, c), own=True) for j, chip in enumerate(chips)]
        for cp in started:
            cp.start()
        passed = []
        for j, chip in enumerate(chips):
            for t in range(len(packs)):
                copy(t, 1 + j, (*chip, c), me).wait_recv()
                fwd = copy(t, 4 + j, (*chip, c), sibling)
                fwd.start()
                passed.append(fwd)
        for t in range(len(packs)):
            copy(t, 0, sibling, me).wait_recv()
            for j, chip in enumerate(chips):
                copy(t, 4 + j, (*chip, 1 - c), me).wait_recv()
        for cp in started + passed:
            cp.wait_send()
        for (_, everyone), o_ref in zip(packs, outs):
            acc = everyone[0]
            for k in range(1, 8):
                acc = acc + everyone[k]
            o_ref[...] = acc

    vmem = pl.BlockSpec(memory_space=pltpu.VMEM)
    return pl.pallas_call(
        body, in_specs=[vmem] * len(ins), out_specs=[vmem] * 3,
        out_shape=[jax.ShapeDtypeStruct(shape, F32) for shape in SMALL_PACKS],
        scratch_shapes=[pltpu.VMEM(SMALL_PACKS[0], F32), pltpu.VMEM(SMALL_PACKS[1], F32)]
        + [pltpu.VMEM((8,) + shape, F32) for shape in SMALL_PACKS]
        + [pltpu.SemaphoreType.DMA((3, 7)), pltpu.SemaphoreType.DMA((3, 7))],
        compiler_params=pltpu.CompilerParams(has_side_effects=True, vmem_limit_bytes=VMEM_LIMIT), name=name,
    )(*ins)


def _adamw_small(sums, params, chip_arr, name):
    flat = [a for n in SMALL for a in params[n]]

    def body(chip_ref, a_ref, b_ref, c_ref, *refs):
        ins, outs = refs[:len(flat)], refs[len(flat):]
        for i, n in enumerate(SMALL):
            w_ref, m_ref, v_ref = ins[3 * i:3 * i + 3]
            g_ref, d_ref, mo_ref, vo_ref = outs[4 * i:4 * i + 4]
            if n == "w_pool":
                pieces = [((0, k), c_ref[k * LANE:(k + 1) * LANE, :]) for k in range(4)]
            elif n == "w_fu":
                mine = pl.ds(pl.multiple_of(chip_ref[0] * LANE, LANE), LANE)
                pieces = [((0,), b_ref[W_FU_ROW:W_FU_ROW + GATE_RANK, mine])]
            elif n == "b_f":
                pieces = [((), b_ref[0:1, :])]
            elif n == "pool_scale":
                pieces = [((), b_ref[1:2, :])]
            else:
                pieces = [((), a_ref[i:i + 1, :])]
            for at, g in pieces:
                d, mn, vn = _adam_math(w_ref[at], g, m_ref[at], v_ref[at])
                g_ref[at] = g
                d_ref[at] = d
                mo_ref[at] = mn
                vo_ref[at] = vn

    def whole(shape):
        return pl.BlockSpec(shape, lambda i, chip_ref: (0,) * len(shape))

    out = pl.pallas_call(
        body,
        grid_spec=pltpu.PrefetchScalarGridSpec(
            num_scalar_prefetch=1, grid=(1,),
            in_specs=[whole(a.shape) for a in list(sums) + flat],
            out_specs=[whole(params[n][0].shape) for n in SMALL for _ in range(4)]),
        out_shape=[jax.ShapeDtypeStruct(params[n][0].shape, F32) for n in SMALL for _ in range(4)],
        compiler_params=_params(("arbitrary",)), name=name,
    )(chip_arr, *sums, *flat)
    return {n: tuple(out[4 * i:4 * i + 4]) for i, n in enumerate(SMALL)}


def _ffn_fwd(x_norm, w_in, w_out, tag):
    ab, u = _ffn_in_swiglu(x_norm, w_in, name=tag + "_in")
    f = _mm(u, w_out, tm=1024, tk=DFF, name=tag + "_out")
    return ab, u, f


def _ffn_bwd(dz, x_norm, ab, u, w_in, w_out, tag, emit, after=()):
    dw_out = _mm(u, dz, ta=True, out_dtype=BF16, tm=1408, tk=2048, after=after, name=tag + "_out_dw")
    behind = emit(tag + "_w_out", dw_out)
    dab = _ffn_out_dx_swiglu(dz, w_out, ab, behind, name=tag + "_out_dx")
    dw_in = _mm(x_norm, dab, ta=True, out_dtype=BF16, tm=512, tk=4096, shards=4, name=tag + "_in_dw")
    behind = emit(tag + "_w_in", dw_in)
    return _mm(dab, w_in, tb=True, tm=1024, after=behind, name=tag + "_in_dx")


def _local_step(x, mem, target, small, gather, emit):
    behind = gather("start", "ffn1i", ())
    gather("start", "ffn1o", behind)
    gather("pass", "ffn1i", ())
    big = gather("finish", "ffn1i", ())
    behind = gather("start", "mixa", (big["ffn1_w_in"],))
    behind = gather("start", "mixb", behind)
    h1 = _norm_fwd(x, small["ffn1_pre_g"], BF16, name="ffn1_pre", after=behind)
    ab1, u1 = _ffn_in_swiglu(h1, big["ffn1_w_in"], name="ffn1_in")
    gather("pass", "ffn1o", (ab1,))
    big.update(gather("finish", "ffn1o", ()))
    f1 = _mm(u1, big["ffn1_w_out"], tm=1024, tk=DFF, name="ffn1_out")
    behind = gather("pass", "mixa", (f1,))
    x1, h = _resid_norm_fwd(x, f1, small["ffn1_post_g"], 0.5, small["mix_pre_g"], name="ffn1_post", after=behind)
    big.update(gather("finish", "mixa", (h,)))
    small = dict(small, w_fu_pad=big["w_fu_pad"])
    pg = _mm(h, big["w_gla_t"], tb=True, out_dtype=BF16, tm=1024, tn=PG_W, name="mix_in_gla")
    behind = gather("pass", "mixb", (pg,))
    ppx = _mm(h, big["w_px_t"], tb=True, out_dtype=BF16, after=behind, name="mix_in_px")
    pgt = _mm(h, big["w_gates_t"], tb=True, out_dtype=BF16, tn=1536, name="mix_in_gates")
    big.update(gather("finish", "mixb", (pgt,)))
    behind = gather("start", "ffn2", (big["w_o"],))
    mem_n = _norm_fwd(mem, small["mem_norm_g"], BF16, name="mem_norm", after=behind)
    kv = _mm(mem_n, big["w_mem_kv"], out_dtype=BF16, name="mem_kv")
    ya_in, sp, so, o_gla = _gla_fwd(pg, small["w_fu_pad"], small["b_f"], small["gla_norm_g"], name="gla_fwd")
    yb_in = _pool_fwd(ppx, small["w_pool_b"], small["pool_scale"], name="pool_fwd")
    xc = _xattn_fwd(ppx, kv, name="xattn_fwd")
    ya = _mm(ya_in, big["w_up_gla"], out_dtype=BF16, name="up_gla")
    yb = _mm(yb_in, big["w_up_pool"], out_dtype=BF16, name="up_pool")
    yc = _mm(xc, big["w_up_xattn"], out_dtype=BF16, name="up_xattn")
    merged = _merge_fwd(pgt, ya, yb, yc, name="merge_fwd")
    behind = gather("pass", "ffn2", (merged,))
    ymix = _mm(merged, big["w_o"], after=behind, name="mix_out")
    big.update(gather("finish", "ffn2", (ymix,)))
    x2, h2 = _resid_norm_fwd(x1, ymix, small["mix_post_g"], 1.0, small["ffn2_pre_g"], name="mix_post")
    ab2, u2, f2 = _ffn_fwd(h2, big["ffn2_w_in"], big["ffn2_w_out"], "ffn2")
    x3, _ = _resid_norm_fwd(x2, f2, small["ffn2_post_g"], 0.5, None, name="ffn2_post")
    gs = {}
    dx3, gs["final_g"], loss = _loss_bwd(x3, small["final_g"], target, name="loss")
    dz2, gs["ffn2_post_g"] = _rms_bwd(f2, small["ffn2_post_g"], [dx3], None, 0.5, BF16, name="ffn2_post_bwd")
    dh2 = _ffn_bwd(dz2, h2, ab2, u2, big["ffn2_w_in"], big["ffn2_w_out"], "ffn2", emit)
    dx2, gs["ffn2_pre_g"] = _rms_bwd(x2, small["ffn2_pre_g"], [dh2], dx3, 1.0, F32, name="ffn2_pre_bwd")
    dy, gs["mix_post_g"] = _rms_bwd(ymix, small["mix_post_g"], [dx2], None, 1.0, BF16, name="mix_post_bwd")
    dmerged = _mm(dy, big["w_o"], tb=True, out_dtype=BF16, name="mix_out_dx")
    emit("w_o", _mm(merged, dy, ta=True, out_dtype=BF16, tm=512, tk=4096, name="mix_out_dw"))
    dya, dyb, dyc, dgt = _merge_bwd(dmerged, pgt, ya, yb, yc, name="merge_bwd")
    dya_in = _mm(dya, big["w_up_gla"], tb=True, out_dtype=BF16, name="up_gla_dx")
    emit("w_up_gla", _mm(ya_in, dya, ta=True, out_dtype=BF16, tm=512, tk=4096, name="up_gla_dw"))
    dyb_in = _mm(dyb, big["w_up_pool"], tb=True, out_dtype=BF16, name="up_pool_dx")
    emit("w_up_pool", _mm(yb_in, dyb, ta=True, out_dtype=BF16, tm=512, tk=4096, shards=4, name="up_pool_dw"))
    dxc = _mm(dyc, big["w_up_xattn"], tb=True, out_dtype=BF16, name="up_xattn_dx")
    emit("w_up_xattn", _mm(xc, dyc, ta=True, out_dtype=BF16, tm=512, tk=4096, shards=4, name="up_xattn_dw"))
    dpg, gs["w_fu_pad"], gs["b_f"], gs["gla_norm_g"] = _gla_bwd(pg, sp, so, o_gla, dya_in, small["w_fu_pad"], small["b_f"], small["gla_norm_g"], name="gla_bwd")
    dp, gs["w_pool"], gs["pool_scale"] = _pool_bwd(dyb_in, ppx, small["w_pool_b"], small["pool_scale"], name="pool_bwd")
    dxq, dkv = _xattn_bwd(dxc, ppx, kv, name="xattn_bwd")
    dkv = dkv.astype(BF16)
    emit("w_mem_kv", _mm(mem_n, dkv, ta=True, out_dtype=BF16, name="mem_kv_dw"))
    dmem_n = _mm(dkv, big["w_mem_kv"], tb=True, name="mem_kv_dx")
    _, gs["mem_norm_g"] = _rms_bwd(mem, small["mem_norm_g"], [dmem_n], None, 1.0, BF16, name="mem_norm_bwd")
    emit("w_gla", _mm(dpg, h, ta=True, out_dtype=BF16, tm=640, tk=4096, name="mix_in_gla_dw"))
    emit("w_p", _mm(dp, h, ta=True, out_dtype=BF16, tm=512, tk=4096, name="mix_in_p_dw"))
    emit("w_xq", _mm(dxq, h, ta=True, out_dtype=BF16, tm=512, tk=4096, name="mix_in_xq_dw"))
    behind = emit("w_gates", _mm(dgt, h, ta=True, out_dtype=BF16, tm=512, tk=4096, name="mix_in_gates_dw"))
    dh_parts = [
        _mm(dpg, big["w_gla_t"], tm=1024, tk=PG_W, after=behind, name="mix_in_gla_dx"),
        _mm(dp, big["w_p_t"], name="mix_in_p_dx"),
        _mm(dxq, big["w_xq_t"], name="mix_in_xq_dx"),
        _mm(dgt, big["w_gates_t"], tm=1024, tk=3072, name="mix_in_gates_dx"),
    ]
    dx1, gs["mix_pre_g"] = _rms_bwd(x1, small["mix_pre_g"], dh_parts, dx2, 1.0, F32, name="mix_pre_bwd")
    dz1, gs["ffn1_post_g"] = _rms_bwd(f1, small["ffn1_post_g"], [dx1], None, 0.5, BF16, name="ffn1_post_bwd")
    dh1 = _ffn_bwd(dz1, h1, ab1, u1, big["ffn1_w_in"], big["ffn1_w_out"], "ffn1", emit)
    dx0, gs["ffn1_pre_g"] = _rms_bwd(x, small["ffn1_pre_g"], [dh1], dx1, 1.0, F32, name="ffn1_pre_bwd")
    return loss, dx0, gs


BIG = ("ffn1_w_in", "ffn1_w_out", "w_in", "w_mem_kv", "w_up_gla", "w_up_pool", "w_up_xattn", "w_o", "ffn2_w_in", "ffn2_w_out")
COL_SHARDED = ("ffn1_w_in", "w_in", "w_up_pool", "w_up_xattn", "ffn2_w_in")
GATHER_GROUPS = {"ffn1i": ("ffn1_w_in",), "ffn1o": ("ffn1_w_out",), "mixa": ("w_in", "w_fu"),
                 "mixb": ("w_mem_kv", "w_up_gla", "w_up_pool", "w_up_xattn", "w_o"), "ffn2": ("ffn2_w_in", "ffn2_w_out")}
REDUCE_GROUPS = {"ffn2": ("ffn2_w_out", "ffn2_w_in"),
                 "mix": ("w_o", "w_up_gla", "w_up_pool", "w_up_xattn", "w_mem_kv", "w_gla", "w_p", "w_xq", "w_gates"),
                 "ffn1_out": ("ffn1_w_out",),
                 "ffn1_in": ("ffn1_w_in",)}
GAINS = ("ffn1_pre_g", "ffn1_post_g", "mix_pre_g", "gla_norm_g", "mem_norm_g", "mix_post_g", "ffn2_pre_g", "ffn2_post_g", "final_g")
WEIGHTS = ("ffn1_pre_g", "ffn1_w_in", "ffn1_w_out", "ffn1_post_g", "mix_pre_g", "w_in", "w_fu", "b_f", "gla_norm_g", "w_pool",
           "pool_scale", "mem_norm_g", "w_mem_kv", "w_up_gla", "w_up_pool", "w_up_xattn", "w_o", "mix_post_g", "ffn2_pre_g",
           "ffn2_w_in", "ffn2_w_out", "ffn2_post_g", "final_g")
IN_GLA, IN_F, IN_PX, IN_GATES, IN_END = 0, 3072, 3088, 4112, 7184
def _cols_from_shards(g):
    return jnp.transpose(g, (1, 0, 2)).reshape(g.shape[1], 4 * g.shape[2])


def kernel(x, mem, ffn1_pre_g, ffn1_w_in, ffn1_w_out, ffn1_post_g, mix_pre_g, w_in, w_fu, b_f, gla_norm_g, w_pool, pool_scale, mem_norm_g, w_mem_kv, w_up_gla, w_up_pool, w_up_xattn, w_o, mix_post_g, ffn2_pre_g, ffn2_w_in, ffn2_w_out, ffn2_post_g, final_g, loss_target, m_ffn1_pre_g, m_ffn1_w_in, m_ffn1_w_out, m_ffn1_post_g, m_mix_pre_g, m_w_in, m_w_fu, m_b_f, m_gla_norm_g, m_w_pool, m_pool_scale, m_mem_norm_g, m_w_mem_kv, m_w_up_gla, m_w_up_pool, m_w_up_xattn, m_w_o, m_mix_post_g, m_ffn2_pre_g, m_ffn2_w_in, m_ffn2_w_out, m_ffn2_post_g, m_final_g, v_ffn1_pre_g, v_ffn1_w_in, v_ffn1_w_out, v_ffn1_post_g, v_mix_pre_g, v_w_in, v_w_fu, v_b_f, v_gla_norm_g, v_w_pool, v_pool_scale, v_mem_norm_g, v_w_mem_kv, v_w_up_gla, v_w_up_pool, v_w_up_xattn, v_w_o, v_mix_post_g, v_ffn2_pre_g, v_ffn2_w_in, v_ffn2_w_out, v_ffn2_post_g, v_final_g):
    args = dict(locals())
    w = {n: args[n][0] for n in WEIGHTS}
    m = {n: args["m_" + n][0] for n in WEIGHTS}
    v = {n: args["v_" + n][0] for n in WEIGHTS}
    xi, yi, ci = lax.axis_index("x"), lax.axis_index("y"), lax.axis_index("c")
    chip = 2 * xi + yi

    c_arr = jnp.reshape(ci, (1,)).astype(jnp.int32)
    chip_arr = jnp.reshape(chip, (1,)).astype(jnp.int32)
    place_arr = jnp.stack([chip, ci]).astype(jnp.int32)
    shard_of = {n: (jnp.transpose(args[n][0])[None] if n == "w_in" else args[n]) for n in BIG}
    placed = {n: _place_shard(shard_of[n], chip_arr, BF16, name="place_" + n) for n in BIG}
    placed["w_fu"] = _place_shard(args["w_fu"], chip_arr, F32, name="place_w_fu")
    inflight = {}

    def relayout(names, gathered):
        out = {}
        for n, g in zip(names, gathered):
            if n == "w_fu":
                w_fu_full = _cols_from_shards(g)
                out["w_fu_pad"] = jnp.concatenate([w_fu_full, jnp.zeros((LANE - GATE_RANK, 512), F32)], axis=0).astype(BF16)
            elif n == "w_in":
                wt = g.reshape(IN_END, D)
                out["w_gla_t"] = jnp.concatenate([wt[IN_GLA:IN_PX], jnp.zeros((PG_W - IN_PX, D), BF16)], axis=0)
                out["w_px_t"] = wt[IN_PX:IN_GATES]
                out["w_p_t"] = wt[IN_PX:IN_PX + 512]
                out["w_xq_t"] = wt[IN_PX + 512:IN_GATES]
                out["w_gates_t"] = wt[IN_GATES:IN_END]
            else:
                out[n] = _cols_from_shards(g) if n in COL_SHARDED else g.reshape(4 * g.shape[1], g.shape[2])
        return out

    def gather(op, group, after):
        names = GATHER_GROUPS[group]
        if op == "start":
            inflight[group] = _gather_start([placed[n] for n in names], after, name="gather_" + group + "_start")
            return (inflight[group][3],)
        if op == "pass":
            send, recv, bufs, _ = inflight[group]
            inflight[group] = _gather_pass(bufs, send, recv, after, name="gather_" + group + "_pass")
            return (inflight[group][2][0],)
        send, recv, bufs = inflight.pop(group)
        return relayout(names, _gather_finish(bufs, send, recv, after, name="gather_" + group + "_finish"))

    small = {n: w[n].reshape(1, D) for n in GAINS}
    small["b_f"] = w["b_f"].reshape(1, 512)
    small["pool_scale"] = w["pool_scale"].reshape(1, 512)
    small["w_pool_b"] = w["w_pool"].astype(BF16)

    pending, travelling = {}, {}

    def emit(name, grad):
        pending[name] = grad
        group = next((g for g, names in REDUCE_GROUPS.items() if name == names[-1]), None)
        if group is None:
            return ()
        gb = {n: pending.pop(n) for n in REDUCE_GROUPS[group]}
        if group == "mix":
            dwt = jnp.concatenate([gb.pop("w_gla")[0:IN_PX], gb.pop("w_p"), gb.pop("w_xq"), gb.pop("w_gates")], axis=0)
            gb["w_in"] = dwt.reshape(4, IN_END // 4, D)
        names = list(gb)
        contrib = [gb[n] if n in COL_SHARDED else gb[n].reshape(4, gb[n].shape[0] // 4, gb[n].shape[1]) for n in names]
        from_sibling = _pair_exchange(contrib, name="grads_" + group + "_pair_exchange")
        pair = [_pair_sum(g, got, c_arr, name="grads_pair_sum_" + n) for n, g, got in zip(names, contrib, from_sibling)]
        send, recv, pair, lands, token = _chip_exchange_start(pair, (), name="grads_" + group + "_chip_start")
        travelling[group] = (names, send, recv, pair, lands)
        return (token,)

    loss, grad_x, gs = _local_step(x[0], mem[0], loss_target[0], small, gather, emit)
    loss = lax.psum(loss[0, 0], ("x", "y", "c"))

    small_sums = _all_sum_small(gs, name="sum_small_grads")
    halves = {}
    for group, (names, send, recv, pair, lands) in travelling.items():
        pair, from_chips = _chip_exchange_finish(pair, lands, send, recv, (grad_x,), name="grads_" + group + "_chip_finish")
        for n, p, got in zip(names, pair, from_chips):
            halves[n] = _chip_sum(p, got, place_arr, name="grads_chip_sum_" + n)
    reduced = dict(zip(BIG, _pair_join([halves[n] for n in BIG], name="grads_pair_join")))

    grads, delta, new_m, new_v = {}, {}, {}, {}
    for n in BIG:
        if n == "w_in":
            transposed = [jnp.transpose(args[k][0]) for k in (n, "m_" + n, "v_" + n)]
            updated = _adamw(transposed[0], reduced[n], transposed[1], transposed[2], name="adamw_" + n)
            grads[n] = jnp.transpose(reduced[n])[None]
            delta[n], new_m[n], new_v[n] = (jnp.transpose(a)[None] for a in updated)
            continue
        grads[n] = reduced[n][None]
        delta[n], new_m[n], new_v[n] = _adamw(args[n], reduced[n], args["m_" + n], args["v_" + n], name="adamw_" + n)
    small_params = {n: (args[n], args["m_" + n], args["v_" + n]) for n in SMALL}
    for n, (g, d, mn, vn) in _adamw_small(small_sums, small_params, chip_arr, name="adamw_small").items():
        grads[n], delta[n], new_m[n], new_v[n] = g, d, mn, vn

    outs = [loss, grad_x[None]]
    for group in (grads, delta, new_m, new_v):
        outs += [group[n] for n in WEIGHTS]
    return tuple(outs)
```

```python
import functools

import jax
import jax.numpy as jnp
from jax import lax
from jax.experimental import pallas as pl
from jax.experimental.pallas import tpu as pltpu

F32 = jnp.float32
BF16 = jnp.bfloat16
MESH = pl.DeviceIdType.MESH
HIGHEST = lax.Precision.HIGHEST

D = 1024
DFF = 2816
CHUNK = 64
HEADS = 4
HDK = 128
HDV = 256
GATE_TEMP = 16.0
POOL_WINDOWS = (2, 4, 8, 16)
POOL_HALO = 16
XA_HEADS = 4
XA_HD = 128
EPS = 1e-6
Q_SCALE = HDK ** -0.5
XA_SCALE = XA_HD ** -0.5
PG_Q, PG_K, PG_V, PG_G, PG_F, PG_W = 0, 512, 1024, 2048, 3072, 3200
GATE_RANK = 16
ADAM_LR, ADAM_B1, ADAM_B2, ADAM_EPS, ADAM_WD, ADAM_STEP = 0.001, 0.9, 0.999, 1e-08, 0.01, 10

VMEM_LIMIT = 48 * 1024 * 1024
LANE = 128
TS_ROW = 512
TS_GLA = 512
TS_POOL = 512
TS_XA = 512


def _params(sem):
    return pltpu.CompilerParams(dimension_semantics=sem, vmem_limit_bytes=VMEM_LIMIT)


def _tile(n, cap, unit=LANE):
    if n <= cap:
        return n
    best = None
    for t in range(unit, cap + 1, unit):
        if n % t == 0:
            best = t
    assert best is not None, (n, cap)
    return best


def _sigmoid(x):
    return 1.0 / (1.0 + jnp.exp(-x))


def _log_sigmoid(x):
    return jnp.minimum(x, 0.0) - jnp.log(1.0 + jnp.exp(-jnp.abs(x)))


def _rms(x):
    r = lax.rsqrt(jnp.mean(x * x, axis=-1, keepdims=True) + EPS)
    return x * r, r


def _rows(ts, w):
    return pl.BlockSpec((ts, w), lambda i: (i, 0))


def _fixed(shape):
    nd = len(shape)
    return pl.BlockSpec(shape, lambda i: (0,) * nd)


def _mm(a, b, *, ta=False, tb=False, out_dtype=F32, tm=2048, tn=1024, tk=1024, shards=1, after=(), name):
    a_blocked, b_blocked = a.ndim == 3, b.ndim == 3
    assert not (a_blocked and ta) and not (b_blocked and tb)
    if a_blocked:
        m, kdim, tk = a.shape[1], a.shape[0] * a.shape[2], a.shape[2]
    else:
        m, kdim = (a.shape[1], a.shape[0]) if ta else a.shape
    if b_blocked:
        n, tn = b.shape[0] * b.shape[2], b.shape[2]
        assert b.shape[1] == kdim and shards in (1, b.shape[0])
    else:
        n = b.shape[0] if tb else b.shape[1]
        assert (b.shape[1] if tb else b.shape[0]) == kdim, (a.shape, b.shape, ta, tb)
        tn = n // shards if shards > 1 else _tile(n, tn)
    tm = _tile(m, tm)
    tk = tk if a_blocked else _tile(kdim, tk)
    kgroup = 2 if (a_blocked and tb and a.shape[0] % 2 == 0) else 1
    nk = kdim // (tk * kgroup)
    dims = (((0 if ta else 1,), (1 if tb else 0,)), ((), ()))

    def body(a_ref, b_ref, *rest):
        o_ref, *acc = rest[len(after):]
        if kgroup == 1:
            part = lax.dot_general(a_ref[...], b_ref[...], dims, preferred_element_type=F32)
        else:
            part = sum(lax.dot_general(a_ref[g], b_ref[:, g * tk:(g + 1) * tk], dims, preferred_element_type=F32) for g in range(kgroup))
        if nk == 1:
            o_ref[...] = part.astype(o_ref.dtype)
            return
        acc_ref, = acc
        k = pl.program_id(2)

        @pl.when(k == 0)
        def _():
            acc_ref[...] = part

        @pl.when(k > 0)
        def _():
            acc_ref[...] += part

        @pl.when(k == nk - 1)
        def _():
            o_ref[...] = acc_ref[...].astype(o_ref.dtype)

    if a_blocked and kgroup > 1:
        a_spec = pl.BlockSpec((kgroup, tm, tk), lambda i, j, k: (k, i, 0))
    elif a_blocked:
        a_spec = pl.BlockSpec((None, tm, tk), lambda i, j, k: (k, i, 0))
    else:
        a_spec = pl.BlockSpec((tk, tm), lambda i, j, k: (k, i)) if ta else pl.BlockSpec((tm, tk), lambda i, j, k: (i, k))
    if b_blocked:
        b_spec = pl.BlockSpec((None, tk, tn), lambda i, j, k: (j, k, 0))
    else:
        b_spec = pl.BlockSpec((tn, tk * kgroup), lambda i, j, k: (j, k)) if tb else pl.BlockSpec((tk, tn), lambda i, j, k: (k, j))
    if shards > 1:
        out_shape = jax.ShapeDtypeStruct((shards, m, tn), out_dtype)
        o_spec = pl.BlockSpec((None, tm, tn), lambda i, j, k: (j, i, 0))
    else:
        out_shape = jax.ShapeDtypeStruct((m, n), out_dtype)
        o_spec = pl.BlockSpec((tm, tn), lambda i, j, k: (i, j))
    return pl.pallas_call(
        body, grid=(m // tm, n // tn, nk), in_specs=[a_spec, b_spec] + [ANY] * len(after), out_specs=o_spec, out_shape=out_shape,
        scratch_shapes=[pltpu.VMEM((tm, tn), F32)] if nk > 1 else [],
        compiler_params=_params(("parallel", "parallel", "arbitrary")), name=name,
    )(a, b, *after)


def _norm_fwd(x, g, out_dtype, name, after=()):
    s, d = x.shape
    ts = _tile(s, TS_ROW, 8)

    def body(x_ref, g_ref, *rest):
        o_ref = rest[len(after)]
        xh, _ = _rms(x_ref[...])
        o_ref[...] = (xh * g_ref[...]).astype(o_ref.dtype)

    return pl.pallas_call(
        body, grid=(s // ts,), in_specs=[_rows(ts, d), _fixed((1, d))] + [ANY] * len(after), out_specs=_rows(ts, d),
        out_shape=jax.ShapeDtypeStruct((s, d), out_dtype), compiler_params=_params(("parallel",)), name=name,
    )(x, g, *after)


def _resid_norm_fwd(x, f, g_post, alpha, g_next, name, after=()):
    s, d = x.shape
    ts = _tile(s, TS_ROW, 8)
    with_h = g_next is not None

    def body(x_ref, f_ref, gp_ref, *rest):
        rest = rest[:1] + rest[1 + len(after):] if with_h else rest[len(after):]
        fh, _ = _rms(f_ref[...])
        xn = x_ref[...] + alpha * (fh * gp_ref[...])
        if with_h:
            gn_ref, xo_ref, h_ref = rest
            xh, _ = _rms(xn)
            h_ref[...] = (xh * gn_ref[...]).astype(h_ref.dtype)
        else:
            xo_ref, = rest
        xo_ref[...] = xn

    ins = [x, f, g_post] + ([g_next] if with_h else []) + list(after)
    in_specs = [_rows(ts, d), _rows(ts, d), _fixed((1, d))] + ([_fixed((1, d))] if with_h else []) + [ANY] * len(after)
    out_shape = [jax.ShapeDtypeStruct((s, d), F32)] + ([jax.ShapeDtypeStruct((s, d), BF16)] if with_h else [])
    out_specs = [_rows(ts, d)] + ([_rows(ts, d)] if with_h else [])
    out = pl.pallas_call(
        body, grid=(s // ts,), in_specs=in_specs, out_specs=out_specs, out_shape=out_shape,
        compiler_params=_params(("parallel",)), name=name,
    )(*ins)
    return (out[0], out[1]) if with_h else (out[0], None)


def _mm_resid_norm(a, w, x, g_post, alpha, g_next, name, after=(), tm=512):
    s, kdim = a.shape
    d = w.shape[1]
    tm = _tile(s, tm)
    with_h = g_next is not None
    na = len(after)

    def body(a_ref, w_ref, x_ref, gp_ref, *rest):
        rest = rest[int(with_h) + na:] if not with_h else rest[:1] + rest[1 + na:]
        f = jnp.dot(a_ref[...], w_ref[...], preferred_element_type=F32)
        fh, _ = _rms(f)
        xn = x_ref[...] + alpha * (fh * gp_ref[...])
        if with_h:
            gn_ref, f_ref, xo_ref, h_ref = rest
            xh, _ = _rms(xn)
            h_ref[...] = (xh * gn_ref[...]).astype(h_ref.dtype)
        else:
            f_ref, xo_ref = rest
        f_ref[...] = f
        xo_ref[...] = xn

    ins = [a, w, x, g_post] + ([g_next] if with_h else []) + list(after)
    in_specs = [_rows(tm, kdim), _fixed((kdim, d)), _rows(tm, d), _fixed((1, d))] + ([_fixed((1, d))] if with_h else []) + [ANY] * na
    out_shape = [jax.ShapeDtypeStruct((s, d), F32)] * 2 + ([jax.ShapeDtypeStruct((s, d), BF16)] if with_h else [])
    out = pl.pallas_call(
        body, grid=(s // tm,), in_specs=in_specs, out_specs=[_rows(tm, d)] * len(out_shape), out_shape=out_shape,
        compiler_params=_params(("parallel",)), name=name,
    )(*ins)
    return (out[0], out[1], out[2]) if with_h else (out[0], out[1], None)


def _mm_rms_bwd(dab, w_in, x, g, dres, name, after=(), tm=512):
    nb, s, tkb = dab.shape
    d = w_in.shape[0]
    tm = _tile(s, tm)
    kgroup = 2
    nk = nb // kgroup
    na = len(after)

    def body(a_ref, w_ref, x_ref, g_ref, dres_ref, *rest):
        dx_ref, dg_ref, acc_ref = rest[na:]
        i, k = pl.program_id(0), pl.program_id(1)
        part = sum(lax.dot_general(a_ref[q], w_ref[:, q * tkb:(q + 1) * tkb], (((1,), (1,)), ((), ())), preferred_element_type=F32)
                   for q in range(kgroup))

        @pl.when(k == 0)
        def _():
            acc_ref[...] = part

        @pl.when(k > 0)
        def _():
            acc_ref[...] += part

        @pl.when(jnp.logical_and(i == 0, k == nk - 1))
        def _():
            dg_ref[...] = jnp.zeros_like(dg_ref)

        @pl.when(k == nk - 1)
        def _():
            dy = acc_ref[...]
            xh, r = _rms(x_ref[...])
            dg_ref[...] += jnp.sum(dy * xh, axis=0, keepdims=True)
            dyg = dy * g_ref[...]
            dx_ref[...] = r * (dyg - xh * jnp.mean(dyg * xh, axis=-1, keepdims=True)) + dres_ref[...]

    rows = pl.BlockSpec((tm, d), lambda i, k: (i, 0))
    return pl.pallas_call(
        body, grid=(s // tm, nk),
        in_specs=[pl.BlockSpec((kgroup, tm, tkb), lambda i, k: (k, i, 0)), pl.BlockSpec((d, kgroup * tkb), lambda i, k: (0, k)),
                  rows, pl.BlockSpec((1, d), lambda i, k: (0, 0)), rows] + [ANY] * na,
        out_specs=[rows, pl.BlockSpec((1, d), lambda i, k: (0, 0))],
        out_shape=[jax.ShapeDtypeStruct((s, d), F32), jax.ShapeDtypeStruct((1, d), F32)],
        scratch_shapes=[pltpu.VMEM((tm, d), F32)],
        compiler_params=_params(("arbitrary", "arbitrary")), name=name,
    )(dab, w_in, x, g, dres, *after)


def _rms_bwd(x, g, dys, dres, alpha, out_dtype, name):
    s, d = x.shape
    ts = _tile(s, TS_ROW, 8)
    ndy = len(dys)
    with_res = dres is not None

    def body(x_ref, g_ref, *rest):
        dy_refs = rest[:ndy]
        rest = rest[ndy:]
        if with_res:
            dres_ref, dx_ref, dg_ref = rest
        else:
            dx_ref, dg_ref = rest
        xh, r = _rms(x_ref[...])
        dy = dy_refs[0][...].astype(F32)
        for ref in dy_refs[1:]:
            dy = dy + ref[...].astype(F32)
        dy = dy * alpha

        @pl.when(pl.program_id(0) == 0)
        def _():
            dg_ref[...] = jnp.zeros_like(dg_ref)

        dg_ref[...] += jnp.sum(dy * xh, axis=0, keepdims=True)
        dyg = dy * g_ref[...]
        dx = r * (dyg - xh * jnp.mean(dyg * xh, axis=-1, keepdims=True))
        if with_res:
            dx = dx + dres_ref[...]
        dx_ref[...] = dx.astype(dx_ref.dtype)

    ins = [x, g] + list(dys) + ([dres] if with_res else [])
    in_specs = [_rows(ts, d), _fixed((1, d))] + [_rows(ts, d)] * (ndy + int(with_res))
    return pl.pallas_call(
        body, grid=(s // ts,), in_specs=in_specs, out_specs=[_rows(ts, d), _fixed((1, d))],
        out_shape=[jax.ShapeDtypeStruct((s, d), out_dtype), jax.ShapeDtypeStruct((1, d), F32)],
        compiler_params=_params(("arbitrary",)), name=name,
    )(*ins)


def _loss_bwd(x, g, target, name):
    s, d = x.shape
    ts = _tile(s, TS_ROW, 8)

    def body(x_ref, g_ref, t_ref, dx_ref, dg_ref, loss_ref):
        xh, r = _rms(x_ref[...])
        gv = g_ref[...]
        diff = xh * gv - t_ref[...]

        @pl.when(pl.program_id(0) == 0)
        def _():
            dg_ref[...] = jnp.zeros_like(dg_ref)
            loss_ref[...] = jnp.zeros_like(loss_ref)

        sq = jnp.sum(diff * diff, axis=1, keepdims=True)
        loss_ref[...] += (0.5 / d) * jnp.sum(sq, axis=0, keepdims=True)
        dy = diff * (1.0 / d)
        dg_ref[...] += jnp.sum(dy * xh, axis=0, keepdims=True)
        dyg = dy * gv
        dx_ref[...] = r * (dyg - xh * jnp.mean(dyg * xh, axis=-1, keepdims=True))

    return pl.pallas_call(
        body, grid=(s // ts,), in_specs=[_rows(ts, d), _fixed((1, d)), _rows(ts, d)],
        out_specs=[_rows(ts, d), _fixed((1, d)), _fixed((8, LANE))],
        out_shape=[jax.ShapeDtypeStruct((s, d), F32), jax.ShapeDtypeStruct((1, d), F32), jax.ShapeDtypeStruct((8, LANE), F32)],
        compiler_params=_params(("arbitrary",)), name=name,
    )(x, g, target)


HALF_FF = DFF // 2


def _ffn_in_swiglu(x_norm, w_in, name, tm=1024):
    s, d = x_norm.shape
    tm = _tile(s, tm)

    def body(x_ref, wa_ref, wb_ref, ab_ref, u_ref):
        xv = x_ref[...]
        a = jnp.dot(xv, wa_ref[...], preferred_element_type=F32)
        b = jnp.dot(xv, wb_ref[...], preferred_element_type=F32)
        ab_ref[0] = a.astype(ab_ref.dtype)
        ab_ref[1] = b.astype(ab_ref.dtype)
        u_ref[...] = (a * _sigmoid(a) * b).astype(u_ref.dtype)

    ab, u = pl.pallas_call(
        body, grid=(s // tm, 2),
        in_specs=[pl.BlockSpec((tm, d), lambda i, j: (i, 0)), pl.BlockSpec((d, HALF_FF), lambda i, j: (0, j)),
                  pl.BlockSpec((d, HALF_FF), lambda i, j: (0, 2 + j))],
        out_specs=[pl.BlockSpec((2, None, tm, HALF_FF), lambda i, j: (0, j, i, 0)), pl.BlockSpec((tm, HALF_FF), lambda i, j: (i, j))],
        out_shape=[jax.ShapeDtypeStruct((2, 2, s, HALF_FF), BF16), jax.ShapeDtypeStruct((s, DFF), BF16)],
        compiler_params=_params(("parallel", "parallel")), name=name,
    )(x_norm, w_in, w_in)
    return ab.reshape(4, s, HALF_FF), u


def _ffn_out_dx_swiglu(dz, w_out, ab, after, name, tm=1024):
    s, d = dz.shape
    tm = _tile(s, tm)

    def body(dz_ref, w_ref, ab_ref, *rest):
        dab_ref = rest[len(after)]
        du = lax.dot_general(dz_ref[...], w_ref[...], (((1,), (1,)), ((), ())), preferred_element_type=F32)
        a = ab_ref[0].astype(F32)
        b = ab_ref[1].astype(F32)
        sig = _sigmoid(a)
        dab_ref[0] = (du * b * (sig * (1.0 + a * (1.0 - sig)))).astype(dab_ref.dtype)
        dab_ref[1] = (du * a * sig).astype(dab_ref.dtype)

    halves = pl.BlockSpec((2, None, tm, HALF_FF), lambda i, j: (0, j, i, 0))
    dab = pl.pallas_call(
        body, grid=(s // tm, 2),
        in_specs=[pl.BlockSpec((tm, d), lambda i, j: (i, 0)), pl.BlockSpec((HALF_FF, d), lambda i, j: (j, 0)), halves] + [ANY] * len(after),
        out_specs=halves, out_shape=jax.ShapeDtypeStruct((2, 2, s, HALF_FF), BF16),
        compiler_params=_params(("parallel", "parallel")), name=name,
    )(dz, w_out, ab.reshape(2, 2, s, HALF_FF), *after)
    return dab.reshape(4, s, HALF_FF)


def _tri(strict):
    r = lax.broadcasted_iota(jnp.int32, (CHUNK, CHUNK), 0)
    c = lax.broadcasted_iota(jnp.int32, (CHUNK, CHUNK), 1)
    return (r > c).astype(F32) if strict else (r >= c).astype(F32)


def _gla_fwd(pg, wfu, b_f, gnorm, name):
    s = pg.shape[0]
    ts = _tile(s, TS_GLA, CHUNK)
    cpb = ts // CHUNK
    nc = s // CHUNK

    def body(pg_ref, wfu_ref, bf_ref, gn_ref, ya_ref, sp_ref, so_ref, o_ref, st_ref, la_ref, dec_ref, u_ref):
        @pl.when(pl.program_id(0) == 0)
        def _():
            st_ref[...] = jnp.zeros_like(st_ref)

        f = jnp.dot(pg_ref[:, PG_F:PG_W], wfu_ref[...], preferred_element_type=F32) + bf_ref[...]
        la_ref[...] = _log_sigmoid(f) * (1.0 / GATE_TEMP)
        tri = _tri(False)
        chunks = [slice(ci * CHUNK, (ci + 1) * CHUNK) for ci in range(cpb)]
        for ci, rows in enumerate(chunks):
            la = la_ref[rows, :]
            b = jnp.dot(tri, la, precision=HIGHEST, preferred_element_type=F32)
            bend = jnp.sum(la, axis=0, keepdims=True)
            e = jnp.exp(bend - b)
            dec_ref[ci:ci + 1, :] = jnp.exp(bend)
            for hd in range(HEADS):
                k = pg_ref[rows, PG_K + hd * HDK:PG_K + (hd + 1) * HDK]
                v = pg_ref[rows, PG_V + hd * HDV:PG_V + (hd + 1) * HDV]
                kt = (k.astype(F32) * e[:, hd * HDK:(hd + 1) * HDK]).astype(BF16)
                u_ref[ci, hd] = lax.dot_general(v, kt, (((0,), (0,)), ((), ())), preferred_element_type=F32)
        for ci in range(cpb):
            for hd in range(HEADS):
                prev = st_ref[hd]
                sp_ref[ci, hd] = prev
                st = prev * dec_ref[ci:ci + 1, hd * HDK:(hd + 1) * HDK] + u_ref[ci, hd]
                st_ref[hd] = st
                so_ref[ci, hd] = st.astype(so_ref.dtype)
        for ci, rows in enumerate(chunks):
            for hd in range(HEADS):
                vc = slice(hd * HDV, (hd + 1) * HDV)
                q = pg_ref[rows, PG_Q + hd * HDK:PG_Q + (hd + 1) * HDK]
                go = pg_ref[rows, PG_G + hd * HDV:PG_G + (hd + 1) * HDV].astype(F32)
                qs = (q.astype(F32) * Q_SCALE).astype(BF16)
                o = lax.dot_general(qs, so_ref[ci, hd], (((1,), (1,)), ((), ())), preferred_element_type=F32)
                o_ref[rows, vc] = o
                oh, _ = _rms(o)
                ya_ref[rows, vc] = (oh * gn_ref[:, vc] * (go * _sigmoid(go))).astype(ya_ref.dtype)

    return pl.pallas_call(
        body, grid=(s // ts,),
        in_specs=[_rows(ts, PG_W), _fixed((LANE, HEADS * HDK)), _fixed((1, HEADS * HDK)), _fixed((1, HEADS * HDV))],
        out_specs=[_rows(ts, HEADS * HDV), pl.BlockSpec((cpb, HEADS, HDV, HDK), lambda i: (i, 0, 0, 0)),
                   pl.BlockSpec((cpb, HEADS, HDV, HDK), lambda i: (i, 0, 0, 0)), _rows(ts, HEADS * HDV)],
        out_shape=[jax.ShapeDtypeStruct((s, HEADS * HDV), BF16), jax.ShapeDtypeStruct((nc, HEADS, HDV, HDK), F32),
                   jax.ShapeDtypeStruct((nc, HEADS, HDV, HDK), BF16), jax.ShapeDtypeStruct((s, HEADS * HDV), F32)],
        scratch_shapes=[pltpu.VMEM((HEADS, HDV, HDK), F32), pltpu.VMEM((ts, HEADS * HDK), F32),
                        pltpu.VMEM((max(cpb, 8), HEADS * HDK), F32), pltpu.VMEM((cpb, HEADS, HDV, HDK), F32)],
        compiler_params=_params(("arbitrary",)), name=name,
    )(pg, wfu, b_f, gnorm)


def _gla_bwd(pg, sp, so, o, dya, wfu, b_f, gnorm, name):
    s = pg.shape[0]
    ts = _tile(s, TS_GLA, CHUNK)
    cpb = ts // CHUNK
    nblk = s // ts

    def body(pg_ref, sp_ref, so_ref, o_ref, dya_ref, wfu_ref, bf_ref, gn_ref, dpg_ref, dwfu_ref, dbf_ref, dgn_ref,
             dst_ref, la_ref, sg_ref, df_ref, e_ref, ktf_ref, dec_ref, g_ref):
        @pl.when(pl.program_id(0) == 0)
        def _():
            dst_ref[...] = jnp.zeros_like(dst_ref)
            dwfu_ref[...] = jnp.zeros_like(dwfu_ref)
            dbf_ref[...] = jnp.zeros_like(dbf_ref)
            dgn_ref[...] = jnp.zeros_like(dgn_ref)

        flow = pg_ref[:, PG_F:PG_W]
        f = jnp.dot(flow, wfu_ref[...], preferred_element_type=F32) + bf_ref[...]
        la_ref[...] = _log_sigmoid(f) * (1.0 / GATE_TEMP)
        sg_ref[...] = _sigmoid(-f) * (1.0 / GATE_TEMP)
        tri = _tri(False)
        tri_strict = _tri(True)
        chunks = [slice(ci * CHUNK, (ci + 1) * CHUNK) for ci in range(cpb)]
        for ci, rows in enumerate(chunks):
            la = la_ref[rows, :]
            b = jnp.dot(tri, la, precision=HIGHEST, preferred_element_type=F32)
            bend = jnp.sum(la, axis=0, keepdims=True)
            e = jnp.exp(bend - b)
            e_ref[rows, :] = e
            dec = jnp.exp(bend)
            dec_ref[ci:ci + 1, :] = dec
            for hd in range(HEADS):
                kc = slice(hd * HDK, (hd + 1) * HDK)
                vc = slice(hd * HDV, (hd + 1) * HDV)
                q = pg_ref[rows, PG_Q + hd * HDK:PG_Q + (hd + 1) * HDK]
                k = pg_ref[rows, PG_K + hd * HDK:PG_K + (hd + 1) * HDK]
                go = pg_ref[rows, PG_G + hd * HDV:PG_G + (hd + 1) * HDV].astype(F32)
                ktf_ref[rows, kc] = k.astype(F32) * e[:, kc]
                st_b = so_ref[ci, hd]
                qs = (q.astype(F32) * Q_SCALE).astype(BF16)
                oh, r = _rms(o_ref[rows, vc])
                gh = gn_ref[:, vc]
                sig = _sigmoid(go)
                dy = dya_ref[rows, vc].astype(F32)
                don = dy * (go * sig)
                dgn_ref[:, vc] += jnp.sum(don * oh, axis=0, keepdims=True)
                dong = don * gh
                do = (r * (dong - oh * jnp.mean(dong * oh, axis=-1, keepdims=True))).astype(BF16)
                g_ref[ci, hd] = lax.dot_general(do, qs, (((0,), (0,)), ((), ())), preferred_element_type=F32)
                dq = jnp.dot(do, st_b, preferred_element_type=F32) * Q_SCALE
                dpg_ref[rows, PG_Q + hd * HDK:PG_Q + (hd + 1) * HDK] = dq.astype(dpg_ref.dtype)
                dgo = dy * (oh * gh) * (sig * (1.0 + go * (1.0 - sig)))
                dpg_ref[rows, PG_G + hd * HDV:PG_G + (hd + 1) * HDV] = dgo.astype(dpg_ref.dtype)
        for ci in reversed(range(cpb)):
            for hd in range(HEADS):
                dst = dst_ref[hd] + g_ref[ci, hd]
                g_ref[ci, hd] = dst
                dst_ref[hd] = dst * dec_ref[ci:ci + 1, hd * HDK:(hd + 1) * HDK]
        for ci, rows in enumerate(chunks):
            for hd in range(HEADS):
                kc = slice(hd * HDK, (hd + 1) * HDK)
                v = pg_ref[rows, PG_V + hd * HDV:PG_V + (hd + 1) * HDV]
                ktf = ktf_ref[rows, kc]
                dst = g_ref[ci, hd]
                dst_b = dst.astype(BF16)
                dkt = jnp.dot(v, dst_b, preferred_element_type=F32)
                dv = lax.dot_general(ktf.astype(BF16), dst_b, (((1,), (1,)), ((), ())), preferred_element_type=F32)
                dd = jnp.sum(dst * sp_ref[ci, hd], axis=0, keepdims=True)
                dla = jnp.dot(tri_strict, dkt * ktf, precision=HIGHEST, preferred_element_type=F32) + dd * dec_ref[ci:ci + 1, kc]
                df_ref[rows, kc] = dla * sg_ref[rows, kc]
                dpg_ref[rows, PG_K + hd * HDK:PG_K + (hd + 1) * HDK] = (dkt * e_ref[rows, kc]).astype(dpg_ref.dtype)
                dpg_ref[rows, PG_V + hd * HDV:PG_V + (hd + 1) * HDV] = dv.astype(dpg_ref.dtype)
        df = df_ref[...]
        df_b = df.astype(BF16)
        dpg_ref[:, PG_F:PG_W] = lax.dot_general(df_b, wfu_ref[...], (((1,), (1,)), ((), ())), preferred_element_type=F32).astype(dpg_ref.dtype)
        dwfu_ref[...] += lax.dot_general(flow, df_b, (((0,), (0,)), ((), ())), preferred_element_type=F32)
        dbf_ref[...] += jnp.sum(df, axis=0, keepdims=True)

    rev = lambda i: (nblk - 1 - i, 0)
    return pl.pallas_call(
        body, grid=(nblk,),
        in_specs=[pl.BlockSpec((ts, PG_W), rev), pl.BlockSpec((cpb, HEADS, HDV, HDK), lambda i: (nblk - 1 - i, 0, 0, 0)),
                  pl.BlockSpec((cpb, HEADS, HDV, HDK), lambda i: (nblk - 1 - i, 0, 0, 0)), pl.BlockSpec((ts, HEADS * HDV), rev),
                  pl.BlockSpec((ts, HEADS * HDV), rev), _fixed((LANE, HEADS * HDK)), _fixed((1, HEADS * HDK)), _fixed((1, HEADS * HDV))],
        out_specs=[pl.BlockSpec((ts, PG_W), rev), _fixed((LANE, HEADS * HDK)), _fixed((1, HEADS * HDK)), _fixed((1, HEADS * HDV))],
        out_shape=[jax.ShapeDtypeStruct((s, PG_W), BF16), jax.ShapeDtypeStruct((LANE, HEADS * HDK), F32),
                   jax.ShapeDtypeStruct((1, HEADS * HDK), F32), jax.ShapeDtypeStruct((1, HEADS * HDV), F32)],
        scratch_shapes=[pltpu.VMEM((HEADS, HDV, HDK), F32)] + [pltpu.VMEM((ts, HEADS * HDK), F32)] * 5
        + [pltpu.VMEM((max(cpb, 8), HEADS * HDK), F32), pltpu.VMEM((cpb, HEADS, HDV, HDK), F32)],
        compiler_params=_params(("arbitrary",)), name=name,
    )(pg, sp, so, o, dya, wfu, b_f, gnorm)


def _window_sums(ext, sign):
    n = ext.shape[0]
    sums = {1: ext}
    w = 1
    while w < POOL_WINDOWS[-1]:
        sums[2 * w] = sums[w] + pltpu.roll(sums[w], w if sign > 0 else n - w, 0)
        w *= 2
    return [sums[POOL_WINDOWS[g]][:, g * LANE:(g + 1) * LANE] for g in range(len(POOL_WINDOWS))]


def _pool_counts(row0, n):
    pos = (row0 + lax.broadcasted_iota(jnp.int32, (n, 1), 0) + 1).astype(F32)
    return [jnp.minimum(pos, float(w)) for w in POOL_WINDOWS]


def _pool_fwd(ppx, w_pool, pool_scale, name):
    s = ppx.shape[0]
    ts = _tile(s, TS_POOL, POOL_HALO)
    hb = ts // POOL_HALO
    pw = len(POOL_WINDOWS) * LANE

    def body(p_ref, halo_ref, w_ref, sc_ref, y_ref, ext_ref):
        i = pl.program_id(0)
        p = p_ref[...].astype(F32)
        ext_ref[0:POOL_HALO, :] = jnp.where(i > 0, halo_ref[...].astype(F32), 0.0)
        ext_ref[POOL_HALO:, :] = p
        sums = _window_sums(ext_ref[...], +1)
        cnt = _pool_counts(i * ts, ts)
        for g in range(len(POOL_WINDOWS)):
            cols = slice(g * LANE, (g + 1) * LANE)
            mixed = sums[g][POOL_HALO:, :] / cnt[g] - p[:, cols]
            y = jnp.dot(mixed.astype(BF16), w_ref[g], preferred_element_type=F32)
            y_ref[:, cols] = (y * sc_ref[:, cols]).astype(y_ref.dtype)

    return pl.pallas_call(
        body, grid=(s // ts,),
        in_specs=[pl.BlockSpec((ts, pw), lambda i: (i, 0)), pl.BlockSpec((POOL_HALO, pw), lambda i: (jnp.maximum(i * hb - 1, 0), 0)),
                  _fixed((len(POOL_WINDOWS), LANE, LANE)), _fixed((1, pw))],
        out_specs=_rows(ts, pw), out_shape=jax.ShapeDtypeStruct((s, pw), BF16),
        scratch_shapes=[pltpu.VMEM((ts + POOL_HALO, pw), F32)],
        compiler_params=_params(("parallel",)), name=name,
    )(ppx, ppx, w_pool, pool_scale)


def _pool_bwd(dyb, ppx, w_pool, pool_scale, name):
    s = ppx.shape[0]
    ts = _tile(s, TS_POOL, POOL_HALO)
    hb = ts // POOL_HALO
    nblk = s // ts
    last_halo = s // POOL_HALO - 1
    ng = len(POOL_WINDOWS)
    pw = ng * LANE

    def body(p_ref, halo_ref, dy_ref, dyn_ref, w_ref, sc_ref, dp_ref, dw_ref, dsc_ref, ext_ref, dext_ref, dm_ref):
        i = pl.program_id(0)

        @pl.when(i == 0)
        def _():
            dw_ref[...] = jnp.zeros_like(dw_ref)
            dsc_ref[...] = jnp.zeros_like(dsc_ref)

        p = p_ref[...].astype(F32)
        ext_ref[0:POOL_HALO, :] = jnp.where(i > 0, halo_ref[...].astype(F32), 0.0)
        ext_ref[POOL_HALO:, :] = p
        sums = _window_sums(ext_ref[...], +1)
        cnt = _pool_counts(i * ts, ts + POOL_HALO)
        sc = sc_ref[...]
        dy = dy_ref[...].astype(F32)
        dyn = jnp.where(i < nblk - 1, dyn_ref[...].astype(F32), 0.0)
        for g in range(ng):
            cols = slice(g * LANE, (g + 1) * LANE)
            wg = w_ref[g]
            mixed = (sums[g][POOL_HALO:, :] / cnt[g][0:ts] - p[:, cols]).astype(BF16)
            ypre = jnp.dot(mixed, wg, preferred_element_type=F32)
            dsc_ref[:, cols] += jnp.sum(dy[:, cols] * ypre, axis=0, keepdims=True)
            dyp = (dy[:, cols] * sc[:, cols]).astype(BF16)
            dypn = (dyn[:, cols] * sc[:, cols]).astype(BF16)
            dw_ref[g] += lax.dot_general(mixed, dyp, (((0,), (0,)), ((), ())), preferred_element_type=F32)
            dm = lax.dot_general(dyp, wg, (((1,), (1,)), ((), ())), preferred_element_type=F32)
            dmn = lax.dot_general(dypn, wg, (((1,), (1,)), ((), ())), preferred_element_type=F32)
            dext_ref[0:ts, cols] = dm / cnt[g][0:ts]
            dext_ref[ts:, cols] = dmn / cnt[g][ts:]
            dm_ref[:, cols] = dm
        lead = _window_sums(dext_ref[...], -1)
        for g in range(ng):
            cols = slice(g * LANE, (g + 1) * LANE)
            dp_ref[:, cols] = (lead[g][0:ts, :] - dm_ref[:, cols]).astype(dp_ref.dtype)

    return pl.pallas_call(
        body, grid=(nblk,),
        in_specs=[pl.BlockSpec((ts, pw), lambda i: (i, 0)), pl.BlockSpec((POOL_HALO, pw), lambda i: (jnp.maximum(i * hb - 1, 0), 0)),
                  pl.BlockSpec((ts, pw), lambda i: (i, 0)), pl.BlockSpec((POOL_HALO, pw), lambda i: (jnp.minimum((i + 1) * hb, last_halo), 0)),
                  _fixed((ng, LANE, LANE)), _fixed((1, pw))],
        out_specs=[_rows(ts, pw), _fixed((ng, LANE, LANE)), _fixed((1, pw))],
        out_shape=[jax.ShapeDtypeStruct((s, pw), BF16), jax.ShapeDtypeStruct((ng, LANE, LANE), F32), jax.ShapeDtypeStruct((1, pw), F32)],
        scratch_shapes=[pltpu.VMEM((ts + POOL_HALO, pw), F32), pltpu.VMEM((ts + POOL_HALO, pw), F32), pltpu.VMEM((ts, pw), F32)],
        compiler_params=_params(("arbitrary",)), name=name,
    )(ppx, ppx, dyb, dyb, w_pool, pool_scale)


def _xattn_fwd(ppx, kv, name):
    s = ppx.shape[0]
    m = kv.shape[0]
    ts = _tile(s, TS_XA, 8)
    xw = XA_HEADS * XA_HD

    def body(q_ref, kv_ref, o_ref):
        for hd in range(XA_HEADS):
            cols = slice(hd * XA_HD, (hd + 1) * XA_HD)
            k = kv_ref[:, hd * XA_HD:(hd + 1) * XA_HD]
            v = kv_ref[:, xw + hd * XA_HD:xw + (hd + 1) * XA_HD]
            sc = lax.dot_general(q_ref[:, cols], k, (((1,), (1,)), ((), ())), preferred_element_type=F32) * XA_SCALE
            ex = jnp.exp(sc - jnp.max(sc, axis=-1, keepdims=True))
            pr = ex / jnp.sum(ex, axis=-1, keepdims=True)
            o_ref[:, cols] = jnp.dot(pr.astype(BF16), v, preferred_element_type=F32).astype(o_ref.dtype)

    return pl.pallas_call(
        body, grid=(s // ts,), in_specs=[pl.BlockSpec((ts, xw), lambda i: (i, 1)), _fixed((m, 2 * xw))],
        out_specs=_rows(ts, xw), out_shape=jax.ShapeDtypeStruct((s, xw), BF16),
        compiler_params=_params(("parallel",)), name=name,
    )(ppx, kv)


def _xattn_bwd(dxc, ppx, kv, name):
    s = ppx.shape[0]
    m = kv.shape[0]
    ts = _tile(s, TS_XA, 8)
    xw = XA_HEADS * XA_HD

    def body(do_ref, q_ref, kv_ref, dq_ref, dkv_ref):
        @pl.when(pl.program_id(0) == 0)
        def _():
            dkv_ref[...] = jnp.zeros_like(dkv_ref)

        for hd in range(XA_HEADS):
            cols = slice(hd * XA_HD, (hd + 1) * XA_HD)
            vcols = slice(xw + hd * XA_HD, xw + (hd + 1) * XA_HD)
            q = q_ref[:, cols]
            k = kv_ref[:, cols]
            v = kv_ref[:, vcols]
            do = do_ref[:, cols]
            sc = lax.dot_general(q, k, (((1,), (1,)), ((), ())), preferred_element_type=F32) * XA_SCALE
            ex = jnp.exp(sc - jnp.max(sc, axis=-1, keepdims=True))
            pr = ex / jnp.sum(ex, axis=-1, keepdims=True)
            dpr = lax.dot_general(do, v, (((1,), (1,)), ((), ())), preferred_element_type=F32)
            dsc = (pr * (dpr - jnp.sum(dpr * pr, axis=-1, keepdims=True)) * XA_SCALE).astype(BF16)
            dq_ref[:, cols] = jnp.dot(dsc, k, preferred_element_type=F32).astype(dq_ref.dtype)
            dkv_ref[:, cols] += lax.dot_general(dsc, q, (((0,), (0,)), ((), ())), preferred_element_type=F32)
            dkv_ref[:, vcols] += lax.dot_general(pr.astype(BF16), do, (((0,), (0,)), ((), ())), preferred_element_type=F32)

    return pl.pallas_call(
        body, grid=(s // ts,), in_specs=[_rows(ts, xw), pl.BlockSpec((ts, xw), lambda i: (i, 1)), _fixed((m, 2 * xw))],
        out_specs=[_rows(ts, xw), _fixed((m, 2 * xw))],
        out_shape=[jax.ShapeDtypeStruct((s, xw), BF16), jax.ShapeDtypeStruct((m, 2 * xw), F32)],
        compiler_params=_params(("arbitrary",)), name=name,
    )(dxc, ppx, kv)


def _merge_fwd(pgt, ya, yb, yc, name):
    s = pgt.shape[0]
    ts = _tile(s, TS_ROW, 8)

    def body(gt_ref, ya_ref, yb_ref, yc_ref, o_ref):
        acc = _sigmoid(gt_ref[:, 0:D].astype(F32)) * ya_ref[...].astype(F32)
        acc = acc + _sigmoid(gt_ref[:, D:2 * D].astype(F32)) * yb_ref[...].astype(F32)
        acc = acc + _sigmoid(gt_ref[:, 2 * D:3 * D].astype(F32)) * yc_ref[...].astype(F32)
        o_ref[...] = acc.astype(o_ref.dtype)

    return pl.pallas_call(
        body, grid=(s // ts,), in_specs=[_rows(ts, 3 * D)] + [_rows(ts, D)] * 3, out_specs=_rows(ts, D),
        out_shape=jax.ShapeDtypeStruct((s, D), BF16), compiler_params=_params(("parallel",)), name=name,
    )(pgt, ya, yb, yc)


def _merge_bwd(dmerged, pgt, ya, yb, yc, name):
    s = pgt.shape[0]
    ts = _tile(s, TS_ROW, 8)

    def body(dm_ref, gt_ref, ya_ref, yb_ref, yc_ref, dya_ref, dyb_ref, dyc_ref, dgt_ref):
        dm = dm_ref[...].astype(F32)
        for j, (y_ref, dy_ref) in enumerate(((ya_ref, dya_ref), (yb_ref, dyb_ref), (yc_ref, dyc_ref))):
            sig = _sigmoid(gt_ref[:, j * D:(j + 1) * D].astype(F32))
            dy_ref[...] = (dm * sig).astype(dy_ref.dtype)
            dgt_ref[:, j * D:(j + 1) * D] = (dm * y_ref[...].astype(F32) * sig * (1.0 - sig)).astype(dgt_ref.dtype)

    return pl.pallas_call(
        body, grid=(s // ts,), in_specs=[_rows(ts, D), _rows(ts, 3 * D)] + [_rows(ts, D)] * 3,
        out_specs=[_rows(ts, D)] * 3 + [_rows(ts, 3 * D)],
        out_shape=[jax.ShapeDtypeStruct((s, D), BF16)] * 3 + [jax.ShapeDtypeStruct((s, 3 * D), BF16)],
        compiler_params=_params(("parallel",)), name=name,
    )(dmerged, pgt, ya, yb, yc)


def _adam_math(w, g, m, v):
    mn = ADAM_B1 * m + (1.0 - ADAM_B1) * g
    vn = ADAM_B2 * v + (1.0 - ADAM_B2) * (g * g)
    m_hat = mn / (1.0 - ADAM_B1 ** ADAM_STEP)
    v_hat = vn / (1.0 - ADAM_B2 ** ADAM_STEP)
    return -ADAM_LR * (m_hat / (jnp.sqrt(v_hat) + ADAM_EPS) + ADAM_WD * w), mn, vn


def _adamw(w, g, m, v, name):
    r, c = w.shape[-2:]
    tr, tc = _block_of(r, c, cap=512 if r % 16 == 0 else 256)

    def spec(a):
        if a.ndim == 2:
            return pl.BlockSpec((tr, tc), lambda i, j: (i, j))
        return pl.BlockSpec((None, tr, tc), lambda i, j: (0, i, j))

    def body(w_ref, g_ref, m_ref, v_ref, d_ref, mo_ref, vo_ref):
        d_ref[...], mo_ref[...], vo_ref[...] = _adam_math(w_ref[...], g_ref[...], m_ref[...], v_ref[...])

    return pl.pallas_call(
        body, grid=(r // tr, c // tc), in_specs=[spec(a) for a in (w, g, m, v)], out_specs=[spec(w)] * 3,
        out_shape=[jax.ShapeDtypeStruct(w.shape, F32)] * 3, compiler_params=_params(("parallel", "parallel")), name=name,
    )(w, g, m, v)


ANY = pl.BlockSpec(memory_space=pl.ANY)


def _place():
    x, y, c = lax.axis_index("x"), lax.axis_index("y"), lax.axis_index("c")
    chips = [(1 - x, y), (x, 1 - y), (1 - x, 1 - y)]
    return x, y, c, chips


def _half(c, rows):
    h = rows // 2
    return pl.ds(pl.multiple_of(c * h, 8), h)


def _by_cols(rows):
    return rows % 32 != 0 and rows != 16


def _half_of(ref, lead, c):
    r, cols = ref.shape[-2:]
    if _by_cols(r):
        return ref.at[(*lead, slice(None), pl.ds(pl.multiple_of(c * (cols // 2), LANE), cols // 2))]
    return ref.at[(*lead, pl.ds(pl.multiple_of(c * (r // 2), 8), r // 2))]


def _half_shape(shape):
    r, cols = shape[-2:]
    return shape[:-2] + ((r, cols // 2) if _by_cols(r) else (r // 2, cols))


def _block_of(r, cols, cap=256):
    if r % 16 == 0:
        return _tile(r, cap, 16), cols
    return r, _tile(cols, cap)


def _place_shard(shard, chip_arr, out_dtype, name):
    _, r, cols = shard.shape
    tr, tc = _block_of(r, cols)

    def body(chip_ref, s_ref, o_ref):
        o_ref[...] = s_ref[...].astype(o_ref.dtype)

    return pl.pallas_call(
        body,
        grid_spec=pltpu.PrefetchScalarGridSpec(
            num_scalar_prefetch=1, grid=(r // tr, cols // tc),
            in_specs=[pl.BlockSpec((None, tr, tc), lambda i, j, chip_ref: (0, i, j))],
            out_specs=pl.BlockSpec((None, tr, tc), lambda i, j, chip_ref: (chip_ref[0], i, j))),
        out_shape=jax.ShapeDtypeStruct((4, r, cols), out_dtype),
        compiler_params=_params(("parallel", "parallel")), name=name,
    )(chip_arr, shard)


def _gather_shards(bufs, name):
    n = len(bufs)

    def body(*refs):
        outs = refs[n:2 * n]
        send_ici, recv_ici, send_d2d, recv_d2d = refs[2 * n:]
        x, y, c, chips = _place()
        me = 2 * x + y
        sibling = (x, y, 1 - c)

        def ici(w, p, chip_of_block, to):
            rows = _half(c, outs[w].shape[1])
            block = outs[w].at[chip_of_block, rows]
            return pltpu.make_async_remote_copy(
                src_ref=block, dst_ref=block, send_sem=send_ici.at[w, p], recv_sem=recv_ici.at[w, p], device_id=to, device_id_type=MESH)

        def d2d(w, p, chip_of_block, half_of):
            rows = _half(half_of, outs[w].shape[1])
            block = outs[w].at[chip_of_block, rows]
            return pltpu.make_async_remote_copy(
                src_ref=block, dst_ref=block, send_sem=send_d2d.at[w, p], recv_sem=recv_d2d.at[w, p], device_id=sibling, device_id_type=MESH)

        sends = [ici(w, p, me, (*chip, c)) for p, chip in enumerate(chips) for w in range(n)]
        for cp in sends:
            cp.start()
        passed = []
        for p, (px, py) in enumerate(chips):
            for w in range(n):
                ici(w, p, 2 * px + py, (px, py, c)).wait_recv()
                fwd = d2d(w, p, 2 * px + py, c)
                fwd.start()
                passed.append(fwd)
        for p, (px, py) in enumerate(chips):
            for w in range(n):
                d2d(w, p, 2 * px + py, 1 - c).wait_recv()
        for cp in sends + passed:
            cp.wait_send()

    return pl.pallas_call(
        body, in_specs=[ANY] * n, out_specs=[ANY] * n,
        out_shape=[jax.ShapeDtypeStruct(a.shape, a.dtype) for a in bufs],
        input_output_aliases={w: w for w in range(n)},
        scratch_shapes=[pltpu.SemaphoreType.DMA((n, 3))] * 4,
        compiler_params=pltpu.CompilerParams(has_side_effects=True), name=name,
    )(*bufs)


HBM = pl.BlockSpec(memory_space=pltpu.HBM)
SEM = pl.BlockSpec(memory_space=pltpu.SEMAPHORE)
EFFECT = pltpu.SideEffectType.DATAFLOW_SIDE_EFFECTING


def _in_hbm(arrays):
    return [pltpu.with_memory_space_constraint(a, pltpu.HBM) for a in arrays]


def _gather_start(bufs, after, name):
    n, na = len(bufs), len(after)

    def body(*refs):
        send_sem, recv_sem = refs[n + na], refs[n + na + 1]
        outs = refs[n + na + 2:2 * n + na + 2]
        token = refs[2 * n + na + 2]
        x, y, c, chips = _place()
        me = 2 * x + y
        for p, chip in enumerate(chips):
            for w in range(n):
                block = _half_of(outs[w], (me,), c)
                pltpu.make_async_remote_copy(
                    src_ref=block, dst_ref=block, send_sem=send_sem, recv_sem=recv_sem,
                    device_id=(*chip, c), device_id_type=MESH).start()
        token[...] = jnp.zeros_like(token)

    out = pl.pallas_call(
        body, name=name, in_specs=[HBM] * n + [ANY] * na,
        out_specs=[SEM, SEM] + [HBM] * n + [pl.BlockSpec(memory_space=pltpu.VMEM)],
        out_shape=[pltpu.SemaphoreType.DMA(()), pltpu.SemaphoreType.DMA(())]
        + [pltpu.HBM(a.shape, a.dtype) for a in bufs] + [jax.ShapeDtypeStruct((8, LANE), F32)],
        input_output_aliases={w: w + 2 for w in range(n)},
        compiler_params=pltpu.CompilerParams(has_side_effects=EFFECT),
    )(*_in_hbm(bufs), *after)
    return out[0], out[1], list(out[2:2 + n]), out[2 + n]


def _gather_pass(bufs, send_sem, recv_sem, after, name):
    n, na = len(bufs), len(after)

    def body(*refs):
        send1, recv1 = refs[n], refs[n + 1]
        send2, recv2 = refs[n + 2 + na], refs[n + 3 + na]
        outs = refs[n + 4 + na:2 * n + 4 + na]
        x, y, c, chips = _place()
        me = 2 * x + y
        arrivals = [(w, px, py) for px, py in chips for w in range(n)]
        for w, px, py in arrivals:
            first = pltpu.make_async_remote_copy(
                src_ref=_half_of(outs[w], (me,), c), dst_ref=_half_of(outs[w], (2 * px + py,), c), send_sem=send1, recv_sem=recv1,
                device_id=(px, py, c), device_id_type=MESH)
            first.wait_send()
            first.wait_recv()
        for w, px, py in arrivals:
            arrived = _half_of(outs[w], (2 * px + py,), c)
            pltpu.make_async_remote_copy(
                src_ref=arrived, dst_ref=arrived, send_sem=send2, recv_sem=recv2,
                device_id=(x, y, 1 - c), device_id_type=MESH).start()

    out = pl.pallas_call(
        body, name=name, in_specs=[HBM] * n + [SEM, SEM] + [ANY] * na,
        out_specs=[SEM, SEM] + [HBM] * n,
        out_shape=[pltpu.SemaphoreType.DMA(()), pltpu.SemaphoreType.DMA(())] + [pltpu.HBM(a.shape, a.dtype) for a in bufs],
        input_output_aliases={w: w + 2 for w in range(n)},
        compiler_params=pltpu.CompilerParams(has_side_effects=EFFECT),
    )(*bufs, send_sem, recv_sem, *after)
    return out[0], out[1], list(out[2:])


def _gather_finish(bufs, send_sem, recv_sem, after, name):
    n, na = len(bufs), len(after)

    def body(*refs):
        send2, recv2 = refs[n], refs[n + 1]
        outs = refs[n + 2 + na:2 * n + 2 + na]
        x, y, c, chips = _place()
        for p, (px, py) in enumerate(chips):
            for w in range(n):
                passed = pltpu.make_async_remote_copy(
                    src_ref=_half_of(outs[w], (2 * px + py,), c), dst_ref=_half_of(outs[w], (2 * px + py,), 1 - c),
                    send_sem=send2, recv_sem=recv2, device_id=(x, y, 1 - c), device_id_type=MESH)
                passed.wait_send()
                passed.wait_recv()

    out = pl.pallas_call(
        body, name=name, in_specs=[HBM] * n + [SEM, SEM] + [ANY] * na, out_specs=[HBM] * n,
        out_shape=[pltpu.HBM(a.shape, a.dtype) for a in bufs],
        input_output_aliases={w: w for w in range(n)},
        compiler_params=pltpu.CompilerParams(has_side_effects=EFFECT),
    )(*bufs, send_sem, recv_sem, *after)
    return list(out)


def _pair_exchange(grads, name):
    n = len(grads)

    def body(*refs):
        ins, outs = refs[:n], refs[n:2 * n]
        send_sem, recv_sem = refs[2 * n:]
        x, y, c, _ = _place()
        copies = []
        for w in range(n):
            copies.append(pltpu.make_async_remote_copy(
                src_ref=_half_of(ins[w], (slice(None),), 1 - c), dst_ref=outs[w], send_sem=send_sem.at[w], recv_sem=recv_sem.at[w],
                device_id=(x, y, 1 - c), device_id_type=MESH))
        for cp in copies:
            cp.start()
        for cp in copies:
            cp.wait()

    return pl.pallas_call(
        body, in_specs=[ANY] * n, out_specs=[ANY] * n,
        out_shape=[jax.ShapeDtypeStruct(_half_shape(a.shape), a.dtype) for a in grads],
        scratch_shapes=[pltpu.SemaphoreType.DMA((n,))] * 2,
        compiler_params=pltpu.CompilerParams(has_side_effects=True), name=name,
    )(*grads)


def _pair_sum(g, got, c_arr, name):
    _, r, cols = g.shape
    hr, hc = _half_shape((r, cols))
    tr, tc = _block_of(hr, hc)
    nbr, nbc = hr // tr, hc // tc
    by_cols = _by_cols(r)

    def body(c_ref, g_ref, got_ref, o_ref):
        o_ref[...] = (g_ref[...].astype(F32) + got_ref[...].astype(F32)).astype(o_ref.dtype)

    def mine(j, i, k, c_ref):
        return (j, i, c_ref[0] * nbc + k) if by_cols else (j, c_ref[0] * nbr + i, k)

    return pl.pallas_call(
        body,
        grid_spec=pltpu.PrefetchScalarGridSpec(
            num_scalar_prefetch=1, grid=(4, nbr, nbc),
            in_specs=[pl.BlockSpec((None, tr, tc), mine),
                      pl.BlockSpec((None, tr, tc), lambda j, i, k, c_ref: (j, i, k))],
            out_specs=pl.BlockSpec((None, tr, tc), lambda j, i, k, c_ref: (j, i, k))),
        out_shape=jax.ShapeDtypeStruct((4, hr, hc), BF16),
        compiler_params=_params(("parallel", "parallel", "parallel")), name=name,
    )(c_arr, g, got)


def _chip_exchange(parts, name):
    n = len(parts)

    def body(*refs):
        ins, outs = refs[:n], refs[n:2 * n]
        send_sem, recv_sem = refs[2 * n:]
        x, y, c, chips = _place()
        copies = []
        for p, (px, py) in enumerate(chips):
            for w in range(n):
                copies.append(pltpu.make_async_remote_copy(
                    src_ref=ins[w].at[2 * px + py], dst_ref=outs[w].at[p], send_sem=send_sem.at[w, p], recv_sem=recv_sem.at[w, p],
                    device_id=(px, py, c), device_id_type=MESH))
        for cp in copies:
            cp.start()
        for cp in copies:
            cp.wait()

    return pl.pallas_call(
        body, in_specs=[ANY] * n, out_specs=[ANY] * n,
        out_shape=[jax.ShapeDtypeStruct((3,) + a.shape[1:], a.dtype) for a in parts],
        scratch_shapes=[pltpu.SemaphoreType.DMA((n, 3))] * 2,
        compiler_params=pltpu.CompilerParams(has_side_effects=True), name=name,
    )(*parts)


def _chip_exchange_start(parts, after, name):
    n, na = len(parts), len(after)
    lands = [lax.empty((3,) + a.shape[1:], a.dtype) for a in parts]

    def body(*refs):
        send_sem, recv_sem = refs[2 * n + na], refs[2 * n + na + 1]
        srcs = refs[2 * n + na + 2:3 * n + na + 2]
        dsts = refs[3 * n + na + 2:4 * n + na + 2]
        token = refs[4 * n + na + 2]
        x, y, c, chips = _place()
        for p, (px, py) in enumerate(chips):
            for w in range(n):
                pltpu.make_async_remote_copy(
                    src_ref=srcs[w].at[2 * px + py], dst_ref=dsts[w].at[p], send_sem=send_sem, recv_sem=recv_sem,
                    device_id=(px, py, c), device_id_type=MESH).start()
        token[...] = jnp.zeros_like(token)

    out = pl.pallas_call(
        body, name=name, in_specs=[HBM] * (2 * n) + [ANY] * na,
        out_specs=[SEM, SEM] + [HBM] * (2 * n) + [pl.BlockSpec(memory_space=pltpu.VMEM)],
        out_shape=[pltpu.SemaphoreType.DMA(()), pltpu.SemaphoreType.DMA(())]
        + [pltpu.HBM(a.shape, a.dtype) for a in parts + lands] + [jax.ShapeDtypeStruct((8, LANE), F32)],
        input_output_aliases={w: w + 2 for w in range(2 * n)},
        compiler_params=pltpu.CompilerParams(has_side_effects=EFFECT),
    )(*_in_hbm(parts), *_in_hbm(lands), *after)
    return out[0], out[1], list(out[2:2 + n]), list(out[2 + n:2 + 2 * n]), out[2 + 2 * n]


def _chip_exchange_finish(parts, lands, send_sem, recv_sem, after, name):
    n, na = len(parts), len(after)

    def body(*refs):
        send, recv = refs[2 * n], refs[2 * n + 1]
        srcs = refs[2 * n + 2 + na:3 * n + 2 + na]
        dsts = refs[3 * n + 2 + na:4 * n + 2 + na]
        x, y, c, chips = _place()
        for p, (px, py) in enumerate(chips):
            for w in range(n):
                copy = pltpu.make_async_remote_copy(
                    src_ref=srcs[w].at[2 * px + py], dst_ref=dsts[w].at[p], send_sem=send, recv_sem=recv,
                    device_id=(px, py, c), device_id_type=MESH)
                copy.wait_send()
                copy.wait_recv()

    out = pl.pallas_call(
        body, name=name, in_specs=[HBM] * (2 * n) + [SEM, SEM] + [ANY] * na, out_specs=[HBM] * (2 * n),
        out_shape=[pltpu.HBM(a.shape, a.dtype) for a in parts + lands],
        input_output_aliases={w: w for w in range(2 * n)},
        compiler_params=pltpu.CompilerParams(has_side_effects=EFFECT),
    )(*parts, *lands, send_sem, recv_sem, *after)
    return list(out[:n]), list(out[n:])


def _chip_sum(part, got, place_arr, name):
    _, hr, hc = part.shape
    by_cols = _by_cols(hr)
    tr, tc = _block_of(hr, hc)
    nbr, nbc = hr // tr, hc // tc

    def body(place_ref, p_ref, got_ref, o_ref):
        acc = p_ref[...].astype(F32)
        for p in range(3):
            acc = acc + got_ref[p].astype(F32)
        o_ref[...] = acc

    def mine(i, k, place_ref):
        return (i, place_ref[1] * nbc + k) if by_cols else (place_ref[1] * nbr + i, k)

    return pl.pallas_call(
        body,
        grid_spec=pltpu.PrefetchScalarGridSpec(
            num_scalar_prefetch=1, grid=(nbr, nbc),
            in_specs=[pl.BlockSpec((None, tr, tc), lambda i, k, place_ref: (place_ref[0], i, k)),
                      pl.BlockSpec((3, tr, tc), lambda i, k, place_ref: (0, i, k))],
            out_specs=pl.BlockSpec((tr, tc), mine)),
        out_shape=jax.ShapeDtypeStruct((hr, 2 * hc) if by_cols else (2 * hr, hc), F32),
        compiler_params=_params(("parallel", "parallel")), name=name,
    )(place_arr, part, got)


def _pair_join(bufs, name):
    n = len(bufs)

    def body(*refs):
        outs = refs[n:2 * n]
        send_sem, recv_sem = refs[2 * n:]
        x, y, c, _ = _place()
        copies = []
        for w in range(n):
            block = _half_of(outs[w], (), c)
            copies.append(pltpu.make_async_remote_copy(
                src_ref=block, dst_ref=block, send_sem=send_sem.at[w], recv_sem=recv_sem.at[w],
                device_id=(x, y, 1 - c), device_id_type=MESH))
        for cp in copies:
            cp.start()
        for w, cp in enumerate(copies):
            cp.wait_send()
            block = _half_of(outs[w], (), 1 - c)
            pltpu.make_async_remote_copy(
                src_ref=block, dst_ref=block, send_sem=send_sem.at[w], recv_sem=recv_sem.at[w],
                device_id=(x, y, 1 - c), device_id_type=MESH).wait_recv()

    return pl.pallas_call(
        body, in_specs=[ANY] * n, out_specs=[ANY] * n,
        out_shape=[jax.ShapeDtypeStruct(a.shape, a.dtype) for a in bufs],
        input_output_aliases={w: w for w in range(n)},
        scratch_shapes=[pltpu.SemaphoreType.DMA((n,))] * 2,
        compiler_params=pltpu.CompilerParams(has_side_effects=True), name=name,
    )(*bufs)


SMALL = ("ffn1_pre_g", "ffn1_post_g", "mix_pre_g", "gla_norm_g", "mem_norm_g", "mix_post_g", "ffn2_pre_g", "ffn2_post_g", "final_g",
         "b_f", "pool_scale", "w_pool", "w_fu")
N_GAINS = 9
SMALL_PACKS = ((16, D), (24, 512), (4 * LANE, LANE))
W_FU_ROW = 8


def _all_sum_small(gs, name):
    ins = [gs[n] for n in SMALL[:N_GAINS]] + [gs["b_f"], gs["pool_scale"], gs["w_fu_pad"], gs["w_pool"].reshape(4 * LANE, LANE)]

    def body(*refs):
        gain_refs = refs[:N_GAINS]
        bf_ref, ps_ref, wfu_ref, wp_ref = refs[N_GAINS:N_GAINS + 4]
        outs = refs[N_GAINS + 4:N_GAINS + 7]
        mine_a, mine_b, all_a, all_b, all_c, send_sems, recv_sems = refs[N_GAINS + 7:]
        mine_a[...] = jnp.zeros_like(mine_a)
        for i, ref in enumerate(gain_refs):
            mine_a[i:i + 1, :] = ref[...]
        mine_b[...] = jnp.zeros_like(mine_b)
        mine_b[0:1, :] = bf_ref[...]
        mine_b[1:2, :] = ps_ref[...]
        mine_b[W_FU_ROW:W_FU_ROW + GATE_RANK, :] = wfu_ref[0:GATE_RANK, :]
        packs = ((mine_a, all_a), (mine_b, all_b), (wp_ref, all_c))
        x, y, c, chips = _place()
        me, sibling = (x, y, c), (x, y, 1 - c)

        def copy(t, k, block, to, own=False):
            px, py, pc = block
            slot = packs[t][1].at[4 * px + 2 * py + pc]
            return pltpu.make_async_remote_copy(
                src_ref=packs[t][0] if own else slot, dst_ref=slot,
                send_sem=send_sems.at[t, k], recv_sem=recv_sems.at[t, k], device_id=to, device_id_type=MESH)

        started = []
        for t, (mine, everyone) in enumerate(packs):
            everyone[4 * x + 2 * y + c] = mine[...]
            started.append(copy(t, 0, me, sibling, own=True))
            started += [copy(t, 1 + j, me, (*chip, c), own=True) for j, chip in enumerate(chips)]
        for cp in started:
            cp.start()
        passed = []
        for j, chip in enumerate(chips):
            for t in range(len(packs)):
                copy(t, 1 + j, (*chip, c), me).wait_recv()
                fwd = copy(t, 4 + j, (*chip, c), sibling)
                fwd.start()
                passed.append(fwd)
        for t in range(len(packs)):
            copy(t, 0, sibling, me).wait_recv()
            for j, chip in enumerate(chips):
                copy(t, 4 + j, (*chip, 1 - c), me).wait_recv()
        for cp in started + passed:
            cp.wait_send()
        for (_, everyone), o_ref in zip(packs, outs):
            acc = everyone[0]
            for k in range(1, 8):
                acc = acc + everyone[k]
            o_ref[...] = acc

    vmem = pl.BlockSpec(memory_space=pltpu.VMEM)
    return pl.pallas_call(
        body, in_specs=[vmem] * len(ins), out_specs=[vmem] * 3,
        out_shape=[jax.ShapeDtypeStruct(shape, F32) for shape in SMALL_PACKS],
        scratch_shapes=[pltpu.VMEM(SMALL_PACKS[0], F32), pltpu.VMEM(SMALL_PACKS[1], F32)]
        + [pltpu.VMEM((8,) + shape, F32) for shape in SMALL_PACKS]
        + [pltpu.SemaphoreType.DMA((3, 7)), pltpu.SemaphoreType.DMA((3, 7))],
        compiler_params=pltpu.CompilerParams(has_side_effects=True, vmem_limit_bytes=VMEM_LIMIT), name=name,
    )(*ins)


def _adamw_small(sums, params, chip_arr, name):
    flat = [a for n in SMALL for a in params[n]]

    def body(chip_ref, a_ref, b_ref, c_ref, *refs):
        ins, outs = refs[:len(flat)], refs[len(flat):]
        for i, n in enumerate(SMALL):
            w_ref, m_ref, v_ref = ins[3 * i:3 * i + 3]
            g_ref, d_ref, mo_ref, vo_ref = outs[4 * i:4 * i + 4]
            if n == "w_pool":
                pieces = [((0, k), c_ref[k * LANE:(k + 1) * LANE, :]) for k in range(4)]
            elif n == "w_fu":
                mine = pl.ds(pl.multiple_of(chip_ref[0] * LANE, LANE), LANE)
                pieces = [((0,), b_ref[W_FU_ROW:W_FU_ROW + GATE_RANK, mine])]
            elif n == "b_f":
                pieces = [((), b_ref[0:1, :])]
            elif n == "pool_scale":
                pieces = [((), b_ref[1:2, :])]
            else:
                pieces = [((), a_ref[i:i + 1, :])]
            for at, g in pieces:
                d, mn, vn = _adam_math(w_ref[at], g, m_ref[at], v_ref[at])
                g_ref[at] = g
                d_ref[at] = d
                mo_ref[at] = mn
                vo_ref[at] = vn

    def whole(shape):
        return pl.BlockSpec(shape, lambda i, chip_ref: (0,) * len(shape))

    out = pl.pallas_call(
        body,
        grid_spec=pltpu.PrefetchScalarGridSpec(
            num_scalar_prefetch=1, grid=(1,),
            in_specs=[whole(a.shape) for a in list(sums) + flat],
            out_specs=[whole(params[n][0].shape) for n in SMALL for _ in range(4)]),
        out_shape=[jax.ShapeDtypeStruct(params[n][0].shape, F32) for n in SMALL for _ in range(4)],
        compiler_params=_params(("arbitrary",)), name=name,
    )(chip_arr, *sums, *flat)
    return {n: tuple(out[4 * i:4 * i + 4]) for i, n in enumerate(SMALL)}


def _ffn_bwd(dz, x_norm, ab, u, w_in, w_out, x, g_pre, dres, tag, emit, after=()):
    dw_out = _mm(u, dz, ta=True, out_dtype=BF16, tm=1408, tk=2048, after=after, name=tag + "_out_dw")
    behind = emit(tag + "_w_out", dw_out)
    dab = _ffn_out_dx_swiglu(dz, w_out, ab, behind, name=tag + "_out_dx")
    dw_in = _mm(x_norm, dab, ta=True, out_dtype=BF16, tm=512, tk=4096, shards=4, name=tag + "_in_dw")
    behind = emit(tag + "_w_in", dw_in)
    return _mm_rms_bwd(dab, w_in, x, g_pre, dres, after=behind, name=tag + "_in_dx")


def _local_step(x, mem, target, small, gather, emit):
    behind = gather("start", "ffn1i", ())
    gather("start", "ffn1o", behind)
    gather("pass", "ffn1i", ())
    big = gather("finish", "ffn1i", ())
    behind = gather("start", "mixa", (big["ffn1_w_in"],))
    behind = gather("start", "mixb", behind)
    h1 = _norm_fwd(x, small["ffn1_pre_g"], BF16, name="ffn1_pre", after=behind)
    ab1, u1 = _ffn_in_swiglu(h1, big["ffn1_w_in"], name="ffn1_in")
    gather("pass", "ffn1o", (ab1,))
    big.update(gather("finish", "ffn1o", ()))
    behind = gather("pass", "mixa", (u1,))
    f1, x1, h = _mm_resid_norm(u1, big["ffn1_w_out"], x, small["ffn1_post_g"], 0.5, small["mix_pre_g"], name="ffn1_out", after=behind)
    big.update(gather("finish", "mixa", (h,)))
    small = dict(small, w_fu_pad=big["w_fu_pad"])
    pg = _mm(h, big["w_gla_t"], tb=True, out_dtype=BF16, tm=1024, tn=PG_W, name="mix_in_gla")
    behind = gather("pass", "mixb", (pg,))
    ppx = _mm(h, big["w_px_t"], tb=True, out_dtype=BF16, after=behind, name="mix_in_px")
    pgt = _mm(h, big["w_gates_t"], tb=True, out_dtype=BF16, tn=1536, name="mix_in_gates")
    big.update(gather("finish", "mixb", (pgt,)))
    behind = gather("start", "ffn2", (big["w_o"],))
    mem_n = _norm_fwd(mem, small["mem_norm_g"], BF16, name="mem_norm", after=behind)
    kv = _mm(mem_n, big["w_mem_kv"], out_dtype=BF16, name="mem_kv")
    ya_in, sp, so, o_gla = _gla_fwd(pg, small["w_fu_pad"], small["b_f"], small["gla_norm_g"], name="gla_fwd")
    yb_in = _pool_fwd(ppx, small["w_pool_b"], small["pool_scale"], name="pool_fwd")
    xc = _xattn_fwd(ppx, kv, name="xattn_fwd")
    ya = _mm(ya_in, big["w_up_gla"], out_dtype=BF16, name="up_gla")
    yb = _mm(yb_in, big["w_up_pool"], out_dtype=BF16, name="up_pool")
    yc = _mm(xc, big["w_up_xattn"], out_dtype=BF16, name="up_xattn")
    merged = _merge_fwd(pgt, ya, yb, yc, name="merge_fwd")
    behind = gather("pass", "ffn2", (merged,))
    ymix, x2, h2 = _mm_resid_norm(merged, big["w_o"], x1, small["mix_post_g"], 1.0, small["ffn2_pre_g"], name="mix_out", after=behind)
    big.update(gather("finish", "ffn2", (h2,)))
    ab2, u2 = _ffn_in_swiglu(h2, big["ffn2_w_in"], name="ffn2_in")
    f2, x3, _ = _mm_resid_norm(u2, big["ffn2_w_out"], x2, small["ffn2_post_g"], 0.5, None, name="ffn2_out")
    gs = {}
    dx3, gs["final_g"], loss = _loss_bwd(x3, small["final_g"], target, name="loss")
    dz2, gs["ffn2_post_g"] = _rms_bwd(f2, small["ffn2_post_g"], [dx3], None, 0.5, BF16, name="ffn2_post_bwd")
    dx2, gs["ffn2_pre_g"] = _ffn_bwd(dz2, h2, ab2, u2, big["ffn2_w_in"], big["ffn2_w_out"], x2, small["ffn2_pre_g"], dx3, "ffn2", emit)
    dy, gs["mix_post_g"] = _rms_bwd(ymix, small["mix_post_g"], [dx2], None, 1.0, BF16, name="mix_post_bwd")
    dmerged = _mm(dy, big["w_o"], tb=True, out_dtype=BF16, name="mix_out_dx")
    emit("w_o", _mm(merged, dy, ta=True, out_dtype=BF16, tm=512, tk=4096, name="mix_out_dw"))
    dya, dyb, dyc, dgt = _merge_bwd(dmerged, pgt, ya, yb, yc, name="merge_bwd")
    dya_in = _mm(dya, big["w_up_gla"], tb=True, out_dtype=BF16, name="up_gla_dx")
    emit("w_up_gla", _mm(ya_in, dya, ta=True, out_dtype=BF16, tm=512, tk=4096, name="up_gla_dw"))
    dyb_in = _mm(dyb, big["w_up_pool"], tb=True, out_dtype=BF16, name="up_pool_dx")
    emit("w_up_pool", _mm(yb_in, dyb, ta=True, out_dtype=BF16, tm=512, tk=4096, shards=4, name="up_pool_dw"))
    dxc = _mm(dyc, big["w_up_xattn"], tb=True, out_dtype=BF16, name="up_xattn_dx")
    emit("w_up_xattn", _mm(xc, dyc, ta=True, out_dtype=BF16, tm=512, tk=4096, shards=4, name="up_xattn_dw"))
    dpg, gs["w_fu_pad"], gs["b_f"], gs["gla_norm_g"] = _gla_bwd(pg, sp, so, o_gla, dya_in, small["w_fu_pad"], small["b_f"], small["gla_norm_g"], name="gla_bwd")
    dp, gs["w_pool"], gs["pool_scale"] = _pool_bwd(dyb_in, ppx, small["w_pool_b"], small["pool_scale"], name="pool_bwd")
    dxq, dkv = _xattn_bwd(dxc, ppx, kv, name="xattn_bwd")
    dkv = dkv.astype(BF16)
    emit("w_mem_kv", _mm(mem_n, dkv, ta=True, out_dtype=BF16, name="mem_kv_dw"))
    dmem_n = _mm(dkv, big["w_mem_kv"], tb=True, name="mem_kv_dx")
    _, gs["mem_norm_g"] = _rms_bwd(mem, small["mem_norm_g"], [dmem_n], None, 1.0, BF16, name="mem_norm_bwd")
    emit("w_gla", _mm(dpg, h, ta=True, out_dtype=BF16, tm=640, tk=4096, name="mix_in_gla_dw"))
    emit("w_p", _mm(dp, h, ta=True, out_dtype=BF16, tm=512, tk=4096, name="mix_in_p_dw"))
    emit("w_xq", _mm(dxq, h, ta=True, out_dtype=BF16, tm=512, tk=4096, name="mix_in_xq_dw"))
    behind = emit("w_gates", _mm(dgt, h, ta=True, out_dtype=BF16, tm=512, tk=4096, name="mix_in_gates_dw"))
    dh_parts = [
        _mm(dpg, big["w_gla_t"], tm=1024, tk=PG_W, after=behind, name="mix_in_gla_dx"),
        _mm(dp, big["w_p_t"], name="mix_in_p_dx"),
        _mm(dxq, big["w_xq_t"], name="mix_in_xq_dx"),
        _mm(dgt, big["w_gates_t"], tm=1024, tk=3072, name="mix_in_gates_dx"),
    ]
    dx1, gs["mix_pre_g"] = _rms_bwd(x1, small["mix_pre_g"], dh_parts, dx2, 1.0, F32, name="mix_pre_bwd")
    dz1, gs["ffn1_post_g"] = _rms_bwd(f1, small["ffn1_post_g"], [dx1], None, 0.5, BF16, name="ffn1_post_bwd")
    dx0, gs["ffn1_pre_g"] = _ffn_bwd(dz1, h1, ab1, u1, big["ffn1_w_in"], big["ffn1_w_out"], x, small["ffn1_pre_g"], dx1, "ffn1", emit)
    return loss, dx0, gs


BIG = ("ffn1_w_in", "ffn1_w_out", "w_in", "w_mem_kv", "w_up_gla", "w_up_pool", "w_up_xattn", "w_o", "ffn2_w_in", "ffn2_w_out")
COL_SHARDED = ("ffn1_w_in", "w_in", "w_up_pool", "w_up_xattn", "ffn2_w_in")
GATHER_GROUPS = {"ffn1i": ("ffn1_w_in",), "ffn1o": ("ffn1_w_out",), "mixa": ("w_in", "w_fu"),
                 "mixb": ("w_mem_kv", "w_up_gla", "w_up_pool", "w_up_xattn", "w_o"), "ffn2": ("ffn2_w_in", "ffn2_w_out")}
REDUCE_GROUPS = {"ffn2": ("ffn2_w_out", "ffn2_w_in"),
                 "mix": ("w_o", "w_up_gla", "w_up_pool", "w_up_xattn", "w_mem_kv", "w_gla", "w_p", "w_xq", "w_gates"),
                 "ffn1_out": ("ffn1_w_out",),
                 "ffn1_in": ("ffn1_w_in",)}
GAINS = ("ffn1_pre_g", "ffn1_post_g", "mix_pre_g", "gla_norm_g", "mem_norm_g", "mix_post_g", "ffn2_pre_g", "ffn2_post_g", "final_g")
WEIGHTS = ("ffn1_pre_g", "ffn1_w_in", "ffn1_w_out", "ffn1_post_g", "mix_pre_g", "w_in", "w_fu", "b_f", "gla_norm_g", "w_pool",
           "pool_scale", "mem_norm_g", "w_mem_kv", "w_up_gla", "w_up_pool", "w_up_xattn", "w_o", "mix_post_g", "ffn2_pre_g",
           "ffn2_w_in", "ffn2_w_out", "ffn2_post_g", "final_g")
IN_GLA, IN_F, IN_PX, IN_GATES, IN_END = 0, 3072, 3088, 4112, 7184
def _cols_from_shards(g):
    return jnp.transpose(g, (1, 0, 2)).reshape(g.shape[1], 4 * g.shape[2])


def kernel(x, mem, ffn1_pre_g, ffn1_w_in, ffn1_w_out, ffn1_post_g, mix_pre_g, w_in, w_fu, b_f, gla_norm_g, w_pool, pool_scale, mem_norm_g, w_mem_kv, w_up_gla, w_up_pool, w_up_xattn, w_o, mix_post_g, ffn2_pre_g, ffn2_w_in, ffn2_w_out, ffn2_post_g, final_g, loss_target, m_ffn1_pre_g, m_ffn1_w_in, m_ffn1_w_out, m_ffn1_post_g, m_mix_pre_g, m_w_in, m_w_fu, m_b_f, m_gla_norm_g, m_w_pool, m_pool_scale, m_mem_norm_g, m_w_mem_kv, m_w_up_gla, m_w_up_pool, m_w_up_xattn, m_w_o, m_mix_post_g, m_ffn2_pre_g, m_ffn2_w_in, m_ffn2_w_out, m_ffn2_post_g, m_final_g, v_ffn1_pre_g, v_ffn1_w_in, v_ffn1_w_out, v_ffn1_post_g, v_mix_pre_g, v_w_in, v_w_fu, v_b_f, v_gla_norm_g, v_w_pool, v_pool_scale, v_mem_norm_g, v_w_mem_kv, v_w_up_gla, v_w_up_pool, v_w_up_xattn, v_w_o, v_mix_post_g, v_ffn2_pre_g, v_ffn2_w_in, v_ffn2_w_out, v_ffn2_post_g, v_final_g):
    args = dict(locals())
    w = {n: args[n][0] for n in WEIGHTS}
    m = {n: args["m_" + n][0] for n in WEIGHTS}
    v = {n: args["v_" + n][0] for n in WEIGHTS}
    xi, yi, ci = lax.axis_index("x"), lax.axis_index("y"), lax.axis_index("c")
    chip = 2 * xi + yi

    c_arr = jnp.reshape(ci, (1,)).astype(jnp.int32)
    chip_arr = jnp.reshape(chip, (1,)).astype(jnp.int32)
    place_arr = jnp.stack([chip, ci]).astype(jnp.int32)
    shard_of = {n: (jnp.transpose(args[n][0])[None] if n == "w_in" else args[n]) for n in BIG}
    placed = {n: _place_shard(shard_of[n], chip_arr, BF16, name="place_" + n) for n in BIG}
    placed["w_fu"] = _place_shard(args["w_fu"], chip_arr, F32, name="place_w_fu")
    inflight = {}

    def relayout(names, gathered):
        out = {}
        for n, g in zip(names, gathered):
            if n == "w_fu":
                w_fu_full = _cols_from_shards(g)
                out["w_fu_pad"] = jnp.concatenate([w_fu_full, jnp.zeros((LANE - GATE_RANK, 512), F32)], axis=0).astype(BF16)
            elif n == "w_in":
                wt = g.reshape(IN_END, D)
                out["w_gla_t"] = jnp.concatenate([wt[IN_GLA:IN_PX], jnp.zeros((PG_W - IN_PX, D), BF16)], axis=0)
                out["w_px_t"] = wt[IN_PX:IN_GATES]
                out["w_p_t"] = wt[IN_PX:IN_PX + 512]
                out["w_xq_t"] = wt[IN_PX + 512:IN_GATES]
                out["w_gates_t"] = wt[IN_GATES:IN_END]
            else:
                out[n] = _cols_from_shards(g) if n in COL_SHARDED else g.reshape(4 * g.shape[1], g.shape[2])
        return out

    def gather(op, group, after):
        names = GATHER_GROUPS[group]
        if op == "start":
            inflight[group] = _gather_start([placed[n] for n in names], after, name="gather_" + group + "_start")
            return (inflight[group][3],)
        if op == "pass":
            send, recv, bufs, _ = inflight[group]
            inflight[group] = _gather_pass(bufs, send, recv, after, name="gather_" + group + "_pass")
            return (inflight[group][2][0],)
        send, recv, bufs = inflight.pop(group)
        return relayout(names, _gather_finish(bufs, send, recv, after, name="gather_" + group + "_finish"))

    small = {n: w[n].reshape(1, D) for n in GAINS}
    small["b_f"] = w["b_f"].reshape(1, 512)
    small["pool_scale"] = w["pool_scale"].reshape(1, 512)
    small["w_pool_b"] = w["w_pool"].astype(BF16)

    pending, travelling = {}, {}

    def emit(name, grad):
        pending[name] = grad
        group = next((g for g, names in REDUCE_GROUPS.items() if name == names[-1]), None)
        if group is None:
            return ()
        gb = {n: pending.pop(n) for n in REDUCE_GROUPS[group]}
        if group == "mix":
            dwt = jnp.concatenate([gb.pop("w_gla")[0:IN_PX], gb.pop("w_p"), gb.pop("w_xq"), gb.pop("w_gates")], axis=0)
            gb["w_in"] = dwt.reshape(4, IN_END // 4, D)
        names = list(gb)
        contrib = [gb[n] if n in COL_SHARDED else gb[n].reshape(4, gb[n].shape[0] // 4, gb[n].shape[1]) for n in names]
        from_sibling = _pair_exchange(contrib, name="grads_" + group + "_pair_exchange")
        pair = [_pair_sum(g, got, c_arr, name="grads_pair_sum_" + n) for n, g, got in zip(names, contrib, from_sibling)]
        send, recv, pair, lands, token = _chip_exchange_start(pair, (), name="grads_" + group + "_chip_start")
        travelling[group] = (names, send, recv, pair, lands)
        return (token,)

    loss, grad_x, gs = _local_step(x[0], mem[0], loss_target[0], small, gather, emit)
    loss = lax.psum(loss[0, 0], ("x", "y", "c"))

    small_sums = _all_sum_small(gs, name="sum_small_grads")
    halves = {}
    for group, (names, send, recv, pair, lands) in travelling.items():
        pair, from_chips = _chip_exchange_finish(pair, lands, send, recv, (grad_x,), name="grads_" + group + "_chip_finish")
        for n, p, got in zip(names, pair, from_chips):
            halves[n] = _chip_sum(p, got, place_arr, name="grads_chip_sum_" + n)
    reduced = dict(zip(BIG, _pair_join([halves[n] for n in BIG], name="grads_pair_join")))

    grads, delta, new_m, new_v = {}, {}, {}, {}
    for n in BIG:
        if n == "w_in":
            transposed = [jnp.transpose(args[k][0]) for k in (n, "m_" + n, "v_" + n)]
            updated = _adamw(transposed[0], reduced[n], transposed[1], transposed[2], name="adamw_" + n)
            grads[n] = jnp.transpose(reduced[n])[None]
            delta[n], new_m[n], new_v[n] = (jnp.transpose(a)[None] for a in updated)
            continue
        grads[n] = reduced[n][None]
        delta[n], new_m[n], new_v[n] = _adamw(args[n], reduced[n], args["m_" + n], args["v_" + n], name="adamw_" + n)
    small_params = {n: (args[n], args["m_" + n], args["v_" + n]) for n in SMALL}
    for n, (g, d, mn, vn) in _adamw_small(small_sums, small_params, chip_arr, name="adamw_small").items():
        grads[n], delta[n], new_m[n], new_v[n] = g, d, mn, vn

    outs = [loss, grad_x[None]]
    for group in (grads, delta, new_m, new_v):
        outs += [group[n] for n in WEIGHTS]
    return tuple(outs)
```

```python
import functools

import jax
import jax.numpy as jnp
from jax import lax
from jax.experimental import pallas as pl
from jax.experimental.pallas import tpu as pltpu

F32 = jnp.float32
BF16 = jnp.bfloat16
MESH = pl.DeviceIdType.MESH
HIGHEST = lax.Precision.HIGHEST

D = 1024
DFF = 2816
CHUNK = 64
HEADS = 4
HDK = 128
HDV = 256
GATE_TEMP = 16.0
POOL_WINDOWS = (2, 4, 8, 16)
POOL_HALO = 16
XA_HEADS = 4
XA_HD = 128
EPS = 1e-6
Q_SCALE = HDK ** -0.5
XA_SCALE = XA_HD ** -0.5
PG_Q, PG_K, PG_V, PG_G, PG_F, PG_W = 0, 512, 1024, 2048, 3072, 3200
GATE_RANK = 16
ADAM_LR, ADAM_B1, ADAM_B2, ADAM_EPS, ADAM_WD, ADAM_STEP = 0.001, 0.9, 0.999, 1e-08, 0.01, 10

VMEM_LIMIT = 48 * 1024 * 1024
LANE = 128
TS_ROW = 512
TS_GLA = 512
TS_POOL = 512
TS_XA = 512


def _params(sem):
    return pltpu.CompilerParams(dimension_semantics=sem, vmem_limit_bytes=VMEM_LIMIT)


def _tile(n, cap, unit=LANE):
    if n <= cap:
        return n
    best = None
    for t in range(unit, cap + 1, unit):
        if n % t == 0:
            best = t
    assert best is not None, (n, cap)
    return best


def _sigmoid(x):
    return 1.0 / (1.0 + jnp.exp(-x))


def _log_sigmoid(x):
    return jnp.minimum(x, 0.0) - jnp.log(1.0 + jnp.exp(-jnp.abs(x)))


def _rms(x):
    r = lax.rsqrt(jnp.mean(x * x, axis=-1, keepdims=True) + EPS)
    return x * r, r


def _rows(ts, w):
    return pl.BlockSpec((ts, w), lambda i: (i, 0))


def _fixed(shape):
    nd = len(shape)
    return pl.BlockSpec(shape, lambda i: (0,) * nd)


def _mm(a, b, *, ta=False, tb=False, out_dtype=F32, tm=2048, tn=1024, tk=1024, shards=1, after=(), name):
    a_blocked, b_blocked = a.ndim == 3, b.ndim == 3
    assert not (a_blocked and ta) and not (b_blocked and tb)
    if a_blocked:
        m, kdim, tk = a.shape[1], a.shape[0] * a.shape[2], a.shape[2]
    else:
        m, kdim = (a.shape[1], a.shape[0]) if ta else a.shape
    if b_blocked:
        n, tn = b.shape[0] * b.shape[2], b.shape[2]
        assert b.shape[1] == kdim and shards in (1, b.shape[0])
    else:
        n = b.shape[0] if tb else b.shape[1]
        assert (b.shape[1] if tb else b.shape[0]) == kdim, (a.shape, b.shape, ta, tb)
        tn = n // shards if shards > 1 else _tile(n, tn)
    tm = _tile(m, tm)
    tk = tk if a_blocked else _tile(kdim, tk)
    kgroup = 2 if (a_blocked and tb and a.shape[0] % 2 == 0) else 1
    nk = kdim // (tk * kgroup)
    dims = (((0 if ta else 1,), (1 if tb else 0,)), ((), ()))

    def body(a_ref, b_ref, *rest):
        o_ref, *acc = rest[len(after):]
        if kgroup == 1:
            part = lax.dot_general(a_ref[...], b_ref[...], dims, preferred_element_type=F32)
        else:
            part = sum(lax.dot_general(a_ref[g], b_ref[:, g * tk:(g + 1) * tk], dims, preferred_element_type=F32) for g in range(kgroup))
        if nk == 1:
            o_ref[...] = part.astype(o_ref.dtype)
            return
        acc_ref, = acc
        k = pl.program_id(2)

        @pl.when(k == 0)
        def _():
            acc_ref[...] = part

        @pl.when(k > 0)
        def _():
            acc_ref[...] += part

        @pl.when(k == nk - 1)
        def _():
            o_ref[...] = acc_ref[...].astype(o_ref.dtype)

    if a_blocked and kgroup > 1:
        a_spec = pl.BlockSpec((kgroup, tm, tk), lambda i, j, k: (k, i, 0))
    elif a_blocked:
        a_spec = pl.BlockSpec((None, tm, tk), lambda i, j, k: (k, i, 0))
    else:
        a_spec = pl.BlockSpec((tk, tm), lambda i, j, k: (k, i)) if ta else pl.BlockSpec((tm, tk), lambda i, j, k: (i, k))
    if b_blocked:
        b_spec = pl.BlockSpec((None, tk, tn), lambda i, j, k: (j, k, 0))
    else:
        b_spec = pl.BlockSpec((tn, tk * kgroup), lambda i, j, k: (j, k)) if tb else pl.BlockSpec((tk, tn), lambda i, j, k: (k, j))
    if shards > 1:
        out_shape = jax.ShapeDtypeStruct((shards, m, tn), out_dtype)
        o_spec = pl.BlockSpec((None, tm, tn), lambda i, j, k: (j, i, 0))
    else:
        out_shape = jax.ShapeDtypeStruct((m, n), out_dtype)
        o_spec = pl.BlockSpec((tm, tn), lambda i, j, k: (i, j))
    return pl.pallas_call(
        body, grid=(m // tm, n // tn, nk), in_specs=[a_spec, b_spec] + [ANY] * len(after), out_specs=o_spec, out_shape=out_shape,
        scratch_shapes=[pltpu.VMEM((tm, tn), F32)] if nk > 1 else [],
        compiler_params=_params(("parallel", "parallel", "arbitrary")), name=name,
    )(a, b, *after)


def _norm_fwd(x, g, out_dtype, name, after=()):
    s, d = x.shape
    ts = _tile(s, TS_ROW, 8)

    def body(x_ref, g_ref, *rest):
        o_ref = rest[len(after)]
        xh, _ = _rms(x_ref[...])
        o_ref[...] = (xh * g_ref[...]).astype(o_ref.dtype)

    return pl.pallas_call(
        body, grid=(s // ts,), in_specs=[_rows(ts, d), _fixed((1, d))] + [ANY] * len(after), out_specs=_rows(ts, d),
        out_shape=jax.ShapeDtypeStruct((s, d), out_dtype), compiler_params=_params(("parallel",)), name=name,
    )(x, g, *after)


def _resid_norm_fwd(x, f, g_post, alpha, g_next, name, after=()):
    s, d = x.shape
    ts = _tile(s, TS_ROW, 8)
    with_h = g_next is not None

    def body(x_ref, f_ref, gp_ref, *rest):
        rest = rest[:1] + rest[1 + len(after):] if with_h else rest[len(after):]
        fh, _ = _rms(f_ref[...])
        xn = x_ref[...] + alpha * (fh * gp_ref[...])
        if with_h:
            gn_ref, xo_ref, h_ref = rest
            xh, _ = _rms(xn)
            h_ref[...] = (xh * gn_ref[...]).astype(h_ref.dtype)
        else:
            xo_ref, = rest
        xo_ref[...] = xn

    ins = [x, f, g_post] + ([g_next] if with_h else []) + list(after)
    in_specs = [_rows(ts, d), _rows(ts, d), _fixed((1, d))] + ([_fixed((1, d))] if with_h else []) + [ANY] * len(after)
    out_shape = [jax.ShapeDtypeStruct((s, d), F32)] + ([jax.ShapeDtypeStruct((s, d), BF16)] if with_h else [])
    out_specs = [_rows(ts, d)] + ([_rows(ts, d)] if with_h else [])
    out = pl.pallas_call(
        body, grid=(s // ts,), in_specs=in_specs, out_specs=out_specs, out_shape=out_shape,
        compiler_params=_params(("parallel",)), name=name,
    )(*ins)
    return (out[0], out[1]) if with_h else (out[0], None)


def _mm_resid_norm(a, w, x, g_post, alpha, g_next, name, after=(), tm=512):
    s, kdim = a.shape
    d = w.shape[1]
    tm = _tile(s, tm)
    with_h = g_next is not None
    na = len(after)

    def body(a_ref, w_ref, x_ref, gp_ref, *rest):
        rest = rest[int(with_h) + na:] if not with_h else rest[:1] + rest[1 + na:]
        for rows in _sub_blocks(tm):
            f = jnp.dot(a_ref[rows, :], w_ref[...], preferred_element_type=F32)
            fh, _ = _rms(f)
            xn = x_ref[rows, :] + alpha * (fh * gp_ref[...])
            if with_h:
                gn_ref, f_ref, xo_ref, h_ref = rest
                xh, _ = _rms(xn)
                h_ref[rows, :] = (xh * gn_ref[...]).astype(h_ref.dtype)
            else:
                f_ref, xo_ref = rest
            f_ref[rows, :] = f
            xo_ref[rows, :] = xn

    ins = [a, w, x, g_post] + ([g_next] if with_h else []) + list(after)
    in_specs = [_rows(tm, kdim), _fixed((kdim, d)), _rows(tm, d), _fixed((1, d))] + ([_fixed((1, d))] if with_h else []) + [ANY] * na
    out_shape = [jax.ShapeDtypeStruct((s, d), F32)] * 2 + ([jax.ShapeDtypeStruct((s, d), BF16)] if with_h else [])
    out = pl.pallas_call(
        body, grid=(s // tm,), in_specs=in_specs, out_specs=[_rows(tm, d)] * len(out_shape), out_shape=out_shape,
        compiler_params=_params(("parallel",)), name=name,
    )(*ins)
    return (out[0], out[1], out[2]) if with_h else (out[0], out[1], None)


def _mm_rms_bwd(pairs, x, g, dres, name, after=(), tm=512):
    s, d = x.shape
    tm = _tile(s, tm)
    n, na = len(pairs), len(after)

    def body(*refs):
        a_refs, w_refs = refs[0:2 * n:2], refs[1:2 * n:2]
        x_ref, g_ref, dres_ref = refs[2 * n:2 * n + 3]
        dx_ref, dg_ref = refs[2 * n + 3 + na:]
        @pl.when(pl.program_id(0) == 0)
        def _():
            dg_ref[...] = jnp.zeros_like(dg_ref)

        for rows in _sub_blocks(tm):
            dy = None
            for a_ref, w_ref in zip(a_refs, w_refs):
                if len(a_ref.shape) == 3:
                    tkb = a_ref.shape[2]
                    parts = [lax.dot_general(a_ref[q, rows, :], w_ref[:, q * tkb:(q + 1) * tkb], (((1,), (1,)), ((), ())),
                                             preferred_element_type=F32) for q in range(a_ref.shape[0])]
                else:
                    parts = [jnp.dot(a_ref[rows, :], w_ref[...], preferred_element_type=F32)]
                for part in parts:
                    dy = part if dy is None else dy + part
            xh, r = _rms(x_ref[rows, :])
            dg_ref[...] += jnp.sum(dy * xh, axis=0, keepdims=True)
            dyg = dy * g_ref[...]
            dx_ref[rows, :] = r * (dyg - xh * jnp.mean(dyg * xh, axis=-1, keepdims=True)) + dres_ref[rows, :]

    ins, in_specs = [], []
    for a_arr, w_arr in pairs:
        ins += [a_arr, w_arr]
        if a_arr.ndim == 3:
            in_specs.append(pl.BlockSpec((a_arr.shape[0], tm, a_arr.shape[2]), lambda i: (0, i, 0)))
        else:
            in_specs.append(_rows(tm, a_arr.shape[1]))
        in_specs.append(pl.BlockSpec(w_arr.shape, lambda i: (0, 0), pipeline_mode=pl.Buffered(1)))
    return pl.pallas_call(
        body, grid=(s // tm,),
        in_specs=in_specs + [_rows(tm, d), _fixed((1, d)), _rows(tm, d)] + [ANY] * na,
        out_specs=[_rows(tm, d), _fixed((1, d))],
        out_shape=[jax.ShapeDtypeStruct((s, d), F32), jax.ShapeDtypeStruct((1, d), F32)],
        compiler_params=_params(("arbitrary",)), name=name,
    )(*ins, x, g, dres, *after)


def _rms_bwd(x, g, dys, dres, alpha, out_dtype, name):
    s, d = x.shape
    ts = _tile(s, TS_ROW, 8)
    ndy = len(dys)
    with_res = dres is not None

    def body(x_ref, g_ref, *rest):
        dy_refs = rest[:ndy]
        rest = rest[ndy:]
        if with_res:
            dres_ref, dx_ref, dg_ref = rest
        else:
            dx_ref, dg_ref = rest
        xh, r = _rms(x_ref[...])
        dy = dy_refs[0][...].astype(F32)
        for ref in dy_refs[1:]:
            dy = dy + ref[...].astype(F32)
        dy = dy * alpha

        @pl.when(pl.program_id(0) == 0)
        def _():
            dg_ref[...] = jnp.zeros_like(dg_ref)

        dg_ref[...] += jnp.sum(dy * xh, axis=0, keepdims=True)
        dyg = dy * g_ref[...]
        dx = r * (dyg - xh * jnp.mean(dyg * xh, axis=-1, keepdims=True))
        if with_res:
            dx = dx + dres_ref[...]
        dx_ref[...] = dx.astype(dx_ref.dtype)

    ins = [x, g] + list(dys) + ([dres] if with_res else [])
    in_specs = [_rows(ts, d), _fixed((1, d))] + [_rows(ts, d)] * (ndy + int(with_res))
    return pl.pallas_call(
        body, grid=(s // ts,), in_specs=in_specs, out_specs=[_rows(ts, d), _fixed((1, d))],
        out_shape=[jax.ShapeDtypeStruct((s, d), out_dtype), jax.ShapeDtypeStruct((1, d), F32)],
        compiler_params=_params(("arbitrary",)), name=name,
    )(*ins)


def _loss_bwd(x, g, target, name):
    s, d = x.shape
    ts = _tile(s, TS_ROW, 8)

    def body(x_ref, g_ref, t_ref, dx_ref, dg_ref, loss_ref):
        xh, r = _rms(x_ref[...])
        gv = g_ref[...]
        diff = xh * gv - t_ref[...]

        @pl.when(pl.program_id(0) == 0)
        def _():
            dg_ref[...] = jnp.zeros_like(dg_ref)
            loss_ref[...] = jnp.zeros_like(loss_ref)

        sq = jnp.sum(diff * diff, axis=1, keepdims=True)
        loss_ref[...] += (0.5 / d) * jnp.sum(sq, axis=0, keepdims=True)
        dy = diff * (1.0 / d)
        dg_ref[...] += jnp.sum(dy * xh, axis=0, keepdims=True)
        dyg = dy * gv
        dx_ref[...] = r * (dyg - xh * jnp.mean(dyg * xh, axis=-1, keepdims=True))

    return pl.pallas_call(
        body, grid=(s // ts,), in_specs=[_rows(ts, d), _fixed((1, d)), _rows(ts, d)],
        out_specs=[_rows(ts, d), _fixed((1, d)), _fixed((8, LANE))],
        out_shape=[jax.ShapeDtypeStruct((s, d), F32), jax.ShapeDtypeStruct((1, d), F32), jax.ShapeDtypeStruct((8, LANE), F32)],
        compiler_params=_params(("arbitrary",)), name=name,
    )(x, g, target)


HALF_FF = DFF // 2


SUB_ROWS = 256


def _sub_blocks(tm):
    sub = SUB_ROWS if tm % SUB_ROWS == 0 else tm
    return [slice(r0, r0 + sub) for r0 in range(0, tm, sub)]


def _ffn_in_swiglu(x_norm, w_in, name, tm=1024):
    s, d = x_norm.shape
    tm = _tile(s, tm)

    def body(x_ref, wa_ref, wb_ref, ab_ref, u_ref):
        for rows in _sub_blocks(tm):
            xv = x_ref[rows, :]
            a = jnp.dot(xv, wa_ref[...], preferred_element_type=F32)
            b = jnp.dot(xv, wb_ref[...], preferred_element_type=F32)
            ab_ref[0, rows, :] = a.astype(ab_ref.dtype)
            ab_ref[1, rows, :] = b.astype(ab_ref.dtype)
            u_ref[rows, :] = (a * _sigmoid(a) * b).astype(u_ref.dtype)

    ab, u = pl.pallas_call(
        body, grid=(s // tm, 2),
        in_specs=[pl.BlockSpec((tm, d), lambda i, j: (i, 0)), pl.BlockSpec((d, HALF_FF), lambda i, j: (0, j)),
                  pl.BlockSpec((d, HALF_FF), lambda i, j: (0, 2 + j))],
        out_specs=[pl.BlockSpec((2, None, tm, HALF_FF), lambda i, j: (0, j, i, 0)), pl.BlockSpec((tm, HALF_FF), lambda i, j: (i, j))],
        out_shape=[jax.ShapeDtypeStruct((2, 2, s, HALF_FF), BF16), jax.ShapeDtypeStruct((s, DFF), BF16)],
        compiler_params=_params(("parallel", "parallel")), name=name,
    )(x_norm, w_in, w_in)
    return ab.reshape(4, s, HALF_FF), u


def _ffn_out_dx_swiglu(dz, w_out, ab, after, name, tm=1024):
    s, d = dz.shape
    tm = _tile(s, tm)

    def body(dz_ref, w_ref, ab_ref, *rest):
        dab_ref = rest[len(after)]
        for rows in _sub_blocks(tm):
            du = lax.dot_general(dz_ref[rows, :], w_ref[...], (((1,), (1,)), ((), ())), preferred_element_type=F32)
            a = ab_ref[0, rows, :].astype(F32)
            b = ab_ref[1, rows, :].astype(F32)
            sig = _sigmoid(a)
            dab_ref[0, rows, :] = (du * b * (sig * (1.0 + a * (1.0 - sig)))).astype(dab_ref.dtype)
            dab_ref[1, rows, :] = (du * a * sig).astype(dab_ref.dtype)

    halves = pl.BlockSpec((2, None, tm, HALF_FF), lambda i, j: (0, j, i, 0))
    dab = pl.pallas_call(
        body, grid=(s // tm, 2),
        in_specs=[pl.BlockSpec((tm, d), lambda i, j: (i, 0)), pl.BlockSpec((HALF_FF, d), lambda i, j: (j, 0)), halves] + [ANY] * len(after),
        out_specs=halves, out_shape=jax.ShapeDtypeStruct((2, 2, s, HALF_FF), BF16),
        compiler_params=_params(("parallel", "parallel")), name=name,
    )(dz, w_out, ab.reshape(2, 2, s, HALF_FF), *after)
    return dab.reshape(4, s, HALF_FF)


def _tri(strict):
    r = lax.broadcasted_iota(jnp.int32, (CHUNK, CHUNK), 0)
    c = lax.broadcasted_iota(jnp.int32, (CHUNK, CHUNK), 1)
    return (r > c).astype(F32) if strict else (r >= c).astype(F32)


def _gla_fwd(pg, wfu, b_f, gnorm, name):
    s = pg.shape[0]
    ts = _tile(s, TS_GLA, CHUNK)
    cpb = ts // CHUNK
    nc = s // CHUNK

    def body(pg_ref, wfu_ref, bf_ref, gn_ref, ya_ref, sp_ref, so_ref, o_ref, st_ref, la_ref, dec_ref, u_ref):
        @pl.when(pl.program_id(0) == 0)
        def _():
            st_ref[...] = jnp.zeros_like(st_ref)

        f = jnp.dot(pg_ref[:, PG_F:PG_W], wfu_ref[...], preferred_element_type=F32) + bf_ref[...]
        la_ref[...] = _log_sigmoid(f) * (1.0 / GATE_TEMP)
        tri = _tri(False)
        chunks = [slice(ci * CHUNK, (ci + 1) * CHUNK) for ci in range(cpb)]
        for ci, rows in enumerate(chunks):
            la = la_ref[rows, :]
            b = jnp.dot(tri, la, precision=HIGHEST, preferred_element_type=F32)
            bend = jnp.sum(la, axis=0, keepdims=True)
            e = jnp.exp(bend - b)
            dec_ref[ci:ci + 1, :] = jnp.exp(bend)
            for hd in range(HEADS):
                k = pg_ref[rows, PG_K + hd * HDK:PG_K + (hd + 1) * HDK]
                v = pg_ref[rows, PG_V + hd * HDV:PG_V + (hd + 1) * HDV]
                kt = (k.astype(F32) * e[:, hd * HDK:(hd + 1) * HDK]).astype(BF16)
                u_ref[ci, hd] = lax.dot_general(v, kt, (((0,), (0,)), ((), ())), preferred_element_type=F32)
        for ci in range(cpb):
            for hd in range(HEADS):
                prev = st_ref[hd]
                sp_ref[ci, hd] = prev
                st = prev * dec_ref[ci:ci + 1, hd * HDK:(hd + 1) * HDK] + u_ref[ci, hd]
                st_ref[hd] = st
                so_ref[ci, hd] = st.astype(so_ref.dtype)
        for ci, rows in enumerate(chunks):
            for hd in range(HEADS):
                vc = slice(hd * HDV, (hd + 1) * HDV)
                q = pg_ref[rows, PG_Q + hd * HDK:PG_Q + (hd + 1) * HDK]
                go = pg_ref[rows, PG_G + hd * HDV:PG_G + (hd + 1) * HDV].astype(F32)
                qs = (q.astype(F32) * Q_SCALE).astype(BF16)
                o = lax.dot_general(qs, so_ref[ci, hd], (((1,), (1,)), ((), ())), preferred_element_type=F32)
                o_ref[rows, vc] = o
                oh, _ = _rms(o)
                ya_ref[rows, vc] = (oh * gn_ref[:, vc] * (go * _sigmoid(go))).astype(ya_ref.dtype)

    return pl.pallas_call(
        body, grid=(s // ts,),
        in_specs=[_rows(ts, PG_W), _fixed((LANE, HEADS * HDK)), _fixed((1, HEADS * HDK)), _fixed((1, HEADS * HDV))],
        out_specs=[_rows(ts, HEADS * HDV), pl.BlockSpec((cpb, HEADS, HDV, HDK), lambda i: (i, 0, 0, 0)),
                   pl.BlockSpec((cpb, HEADS, HDV, HDK), lambda i: (i, 0, 0, 0)), _rows(ts, HEADS * HDV)],
        out_shape=[jax.ShapeDtypeStruct((s, HEADS * HDV), BF16), jax.ShapeDtypeStruct((nc, HEADS, HDV, HDK), F32),
                   jax.ShapeDtypeStruct((nc, HEADS, HDV, HDK), BF16), jax.ShapeDtypeStruct((s, HEADS * HDV), F32)],
        scratch_shapes=[pltpu.VMEM((HEADS, HDV, HDK), F32), pltpu.VMEM((ts, HEADS * HDK), F32),
                        pltpu.VMEM((max(cpb, 8), HEADS * HDK), F32), pltpu.VMEM((cpb, HEADS, HDV, HDK), F32)],
        compiler_params=_params(("arbitrary",)), name=name,
    )(pg, wfu, b_f, gnorm)


def _gla_bwd(pg, sp, so, o, dya, wfu, b_f, gnorm, name):
    s = pg.shape[0]
    ts = _tile(s, TS_GLA, CHUNK)
    cpb = ts // CHUNK
    nblk = s // ts

    def body(pg_ref, sp_ref, so_ref, o_ref, dya_ref, wfu_ref, bf_ref, gn_ref, dpg_ref, dwfu_ref, dbf_ref, dgn_ref,
             dst_ref, la_ref, sg_ref, df_ref, e_ref, ktf_ref, dec_ref, g_ref):
        @pl.when(pl.program_id(0) == 0)
        def _():
            dst_ref[...] = jnp.zeros_like(dst_ref)
            dwfu_ref[...] = jnp.zeros_like(dwfu_ref)
            dbf_ref[...] = jnp.zeros_like(dbf_ref)
            dgn_ref[...] = jnp.zeros_like(dgn_ref)

        flow = pg_ref[:, PG_F:PG_W]
        f = jnp.dot(flow, wfu_ref[...], preferred_element_type=F32) + bf_ref[...]
        la_ref[...] = _log_sigmoid(f) * (1.0 / GATE_TEMP)
        sg_ref[...] = _sigmoid(-f) * (1.0 / GATE_TEMP)
        tri = _tri(False)
        tri_strict = _tri(True)
        chunks = [slice(ci * CHUNK, (ci + 1) * CHUNK) for ci in range(cpb)]
        for ci, rows in enumerate(chunks):
            la = la_ref[rows, :]
            b = jnp.dot(tri, la, precision=HIGHEST, preferred_element_type=F32)
            bend = jnp.sum(la, axis=0, keepdims=True)
            e = jnp.exp(bend - b)
            e_ref[rows, :] = e
            dec = jnp.exp(bend)
            dec_ref[ci:ci + 1, :] = dec
            for hd in range(HEADS):
                kc = slice(hd * HDK, (hd + 1) * HDK)
                vc = slice(hd * HDV, (hd + 1) * HDV)
                q = pg_ref[rows, PG_Q + hd * HDK:PG_Q + (hd + 1) * HDK]
                k = pg_ref[rows, PG_K + hd * HDK:PG_K + (hd + 1) * HDK]
                go = pg_ref[rows, PG_G + hd * HDV:PG_G + (hd + 1) * HDV].astype(F32)
                ktf_ref[rows, kc] = k.astype(F32) * e[:, kc]
                st_b = so_ref[ci, hd]
                qs = (q.astype(F32) * Q_SCALE).astype(BF16)
                oh, r = _rms(o_ref[rows, vc])
                gh = gn_ref[:, vc]
                sig = _sigmoid(go)
                dy = dya_ref[rows, vc].astype(F32)
                don = dy * (go * sig)
                dgn_ref[:, vc] += jnp.sum(don * oh, axis=0, keepdims=True)
                dong = don * gh
                do = (r * (dong - oh * jnp.mean(dong * oh, axis=-1, keepdims=True))).astype(BF16)
                g_ref[ci, hd] = lax.dot_general(do, qs, (((0,), (0,)), ((), ())), preferred_element_type=F32)
                dq = jnp.dot(do, st_b, preferred_element_type=F32) * Q_SCALE
                dpg_ref[rows, PG_Q + hd * HDK:PG_Q + (hd + 1) * HDK] = dq.astype(dpg_ref.dtype)
                dgo = dy * (oh * gh) * (sig * (1.0 + go * (1.0 - sig)))
                dpg_ref[rows, PG_G + hd * HDV:PG_G + (hd + 1) * HDV] = dgo.astype(dpg_ref.dtype)
        for ci in reversed(range(cpb)):
            for hd in range(HEADS):
                dst = dst_ref[hd] + g_ref[ci, hd]
                g_ref[ci, hd] = dst
                dst_ref[hd] = dst * dec_ref[ci:ci + 1, hd * HDK:(hd + 1) * HDK]
        for ci, rows in enumerate(chunks):
            for hd in range(HEADS):
                kc = slice(hd * HDK, (hd + 1) * HDK)
                v = pg_ref[rows, PG_V + hd * HDV:PG_V + (hd + 1) * HDV]
                ktf = ktf_ref[rows, kc]
                dst = g_ref[ci, hd]
                dst_b = dst.astype(BF16)
                dkt = jnp.dot(v, dst_b, preferred_element_type=F32)
                dv = lax.dot_general(ktf.astype(BF16), dst_b, (((1,), (1,)), ((), ())), preferred_element_type=F32)
                dd = jnp.sum(dst * sp_ref[ci, hd], axis=0, keepdims=True)
                dla = jnp.dot(tri_strict, dkt * ktf, precision=HIGHEST, preferred_element_type=F32) + dd * dec_ref[ci:ci + 1, kc]
                df_ref[rows, kc] = dla * sg_ref[rows, kc]
                dpg_ref[rows, PG_K + hd * HDK:PG_K + (hd + 1) * HDK] = (dkt * e_ref[rows, kc]).astype(dpg_ref.dtype)
                dpg_ref[rows, PG_V + hd * HDV:PG_V + (hd + 1) * HDV] = dv.astype(dpg_ref.dtype)
        df = df_ref[...]
        df_b = df.astype(BF16)
        dpg_ref[:, PG_F:PG_W] = lax.dot_general(df_b, wfu_ref[...], (((1,), (1,)), ((), ())), preferred_element_type=F32).astype(dpg_ref.dtype)
        dwfu_ref[...] += lax.dot_general(flow, df_b, (((0,), (0,)), ((), ())), preferred_element_type=F32)
        dbf_ref[...] += jnp.sum(df, axis=0, keepdims=True)

    rev = lambda i: (nblk - 1 - i, 0)
    return pl.pallas_call(
        body, grid=(nblk,),
        in_specs=[pl.BlockSpec((ts, PG_W), rev), pl.BlockSpec((cpb, HEADS, HDV, HDK), lambda i: (nblk - 1 - i, 0, 0, 0)),
                  pl.BlockSpec((cpb, HEADS, HDV, HDK), lambda i: (nblk - 1 - i, 0, 0, 0)), pl.BlockSpec((ts, HEADS * HDV), rev),
                  pl.BlockSpec((ts, HEADS * HDV), rev), _fixed((LANE, HEADS * HDK)), _fixed((1, HEADS * HDK)), _fixed((1, HEADS * HDV))],
        out_specs=[pl.BlockSpec((ts, PG_W), rev), _fixed((LANE, HEADS * HDK)), _fixed((1, HEADS * HDK)), _fixed((1, HEADS * HDV))],
        out_shape=[jax.ShapeDtypeStruct((s, PG_W), BF16), jax.ShapeDtypeStruct((LANE, HEADS * HDK), F32),
                   jax.ShapeDtypeStruct((1, HEADS * HDK), F32), jax.ShapeDtypeStruct((1, HEADS * HDV), F32)],
        scratch_shapes=[pltpu.VMEM((HEADS, HDV, HDK), F32)] + [pltpu.VMEM((ts, HEADS * HDK), F32)] * 5
        + [pltpu.VMEM((max(cpb, 8), HEADS * HDK), F32), pltpu.VMEM((cpb, HEADS, HDV, HDK), F32)],
        compiler_params=_params(("arbitrary",)), name=name,
    )(pg, sp, so, o, dya, wfu, b_f, gnorm)


def _window_sums(ext, sign):
    n = ext.shape[0]
    sums = {1: ext}
    w = 1
    while w < POOL_WINDOWS[-1]:
        sums[2 * w] = sums[w] + pltpu.roll(sums[w], w if sign > 0 else n - w, 0)
        w *= 2
    return [sums[POOL_WINDOWS[g]][:, g * LANE:(g + 1) * LANE] for g in range(len(POOL_WINDOWS))]


def _pool_counts(row0, n):
    pos = (row0 + lax.broadcasted_iota(jnp.int32, (n, 1), 0) + 1).astype(F32)
    return [jnp.minimum(pos, float(w)) for w in POOL_WINDOWS]


def _pool_fwd(ppx, w_pool, pool_scale, name):
    s = ppx.shape[0]
    ts = _tile(s, TS_POOL, POOL_HALO)
    hb = ts // POOL_HALO
    pw = len(POOL_WINDOWS) * LANE

    def body(p_ref, halo_ref, w_ref, sc_ref, y_ref, ext_ref):
        i = pl.program_id(0)
        p = p_ref[...].astype(F32)
        ext_ref[0:POOL_HALO, :] = jnp.where(i > 0, halo_ref[...].astype(F32), 0.0)
        ext_ref[POOL_HALO:, :] = p
        sums = _window_sums(ext_ref[...], +1)
        cnt = _pool_counts(i * ts, ts)
        for g in range(len(POOL_WINDOWS)):
            cols = slice(g * LANE, (g + 1) * LANE)
            mixed = sums[g][POOL_HALO:, :] / cnt[g] - p[:, cols]
            y = jnp.dot(mixed.astype(BF16), w_ref[g], preferred_element_type=F32)
            y_ref[:, cols] = (y * sc_ref[:, cols]).astype(y_ref.dtype)

    return pl.pallas_call(
        body, grid=(s // ts,),
        in_specs=[pl.BlockSpec((ts, pw), lambda i: (i, 0)), pl.BlockSpec((POOL_HALO, pw), lambda i: (jnp.maximum(i * hb - 1, 0), 0)),
                  _fixed((len(POOL_WINDOWS), LANE, LANE)), _fixed((1, pw))],
        out_specs=_rows(ts, pw), out_shape=jax.ShapeDtypeStruct((s, pw), BF16),
        scratch_shapes=[pltpu.VMEM((ts + POOL_HALO, pw), F32)],
        compiler_params=_params(("parallel",)), name=name,
    )(ppx, ppx, w_pool, pool_scale)


def _pool_bwd(dyb, ppx, w_pool, pool_scale, name):
    s = ppx.shape[0]
    ts = _tile(s, TS_POOL, POOL_HALO)
    hb = ts // POOL_HALO
    nblk = s // ts
    last_halo = s // POOL_HALO - 1
    ng = len(POOL_WINDOWS)
    pw = ng * LANE

    def body(p_ref, halo_ref, dy_ref, dyn_ref, w_ref, sc_ref, dp_ref, dw_ref, dsc_ref, ext_ref, dext_ref, dm_ref):
        i = pl.program_id(0)

        @pl.when(i == 0)
        def _():
            dw_ref[...] = jnp.zeros_like(dw_ref)
            dsc_ref[...] = jnp.zeros_like(dsc_ref)

        p = p_ref[...].astype(F32)
        ext_ref[0:POOL_HALO, :] = jnp.where(i > 0, halo_ref[...].astype(F32), 0.0)
        ext_ref[POOL_HALO:, :] = p
        sums = _window_sums(ext_ref[...], +1)
        cnt = _pool_counts(i * ts, ts + POOL_HALO)
        sc = sc_ref[...]
        dy = dy_ref[...].astype(F32)
        dyn = jnp.where(i < nblk - 1, dyn_ref[...].astype(F32), 0.0)
        for g in range(ng):
            cols = slice(g * LANE, (g + 1) * LANE)
            wg = w_ref[g]
            mixed = (sums[g][POOL_HALO:, :] / cnt[g][0:ts] - p[:, cols]).astype(BF16)
            ypre = jnp.dot(mixed, wg, preferred_element_type=F32)
            dsc_ref[:, cols] += jnp.sum(dy[:, cols] * ypre, axis=0, keepdims=True)
            dyp = (dy[:, cols] * sc[:, cols]).astype(BF16)
            dypn = (dyn[:, cols] * sc[:, cols]).astype(BF16)
            dw_ref[g] += lax.dot_general(mixed, dyp, (((0,), (0,)), ((), ())), preferred_element_type=F32)
            dm = lax.dot_general(dyp, wg, (((1,), (1,)), ((), ())), preferred_element_type=F32)
            dmn = lax.dot_general(dypn, wg, (((1,), (1,)), ((), ())), preferred_element_type=F32)
            dext_ref[0:ts, cols] = dm / cnt[g][0:ts]
            dext_ref[ts:, cols] = dmn / cnt[g][ts:]
            dm_ref[:, cols] = dm
        lead = _window_sums(dext_ref[...], -1)
        for g in range(ng):
            cols = slice(g * LANE, (g + 1) * LANE)
            dp_ref[:, cols] = (lead[g][0:ts, :] - dm_ref[:, cols]).astype(dp_ref.dtype)

    return pl.pallas_call(
        body, grid=(nblk,),
        in_specs=[pl.BlockSpec((ts, pw), lambda i: (i, 0)), pl.BlockSpec((POOL_HALO, pw), lambda i: (jnp.maximum(i * hb - 1, 0), 0)),
                  pl.BlockSpec((ts, pw), lambda i: (i, 0)), pl.BlockSpec((POOL_HALO, pw), lambda i: (jnp.minimum((i + 1) * hb, last_halo), 0)),
                  _fixed((ng, LANE, LANE)), _fixed((1, pw))],
        out_specs=[_rows(ts, pw), _fixed((ng, LANE, LANE)), _fixed((1, pw))],
        out_shape=[jax.ShapeDtypeStruct((s, pw), BF16), jax.ShapeDtypeStruct((ng, LANE, LANE), F32), jax.ShapeDtypeStruct((1, pw), F32)],
        scratch_shapes=[pltpu.VMEM((ts + POOL_HALO, pw), F32), pltpu.VMEM((ts + POOL_HALO, pw), F32), pltpu.VMEM((ts, pw), F32)],
        compiler_params=_params(("arbitrary",)), name=name,
    )(ppx, ppx, dyb, dyb, w_pool, pool_scale)


def _xattn_fwd(ppx, kv, name):
    s = ppx.shape[0]
    m = kv.shape[0]
    ts = _tile(s, TS_XA, 8)
    xw = XA_HEADS * XA_HD

    def body(q_ref, kv_ref, o_ref):
        for hd in range(XA_HEADS):
            cols = slice(hd * XA_HD, (hd + 1) * XA_HD)
            k = kv_ref[:, hd * XA_HD:(hd + 1) * XA_HD]
            v = kv_ref[:, xw + hd * XA_HD:xw + (hd + 1) * XA_HD]
            sc = lax.dot_general(q_ref[:, cols], k, (((1,), (1,)), ((), ())), preferred_element_type=F32) * XA_SCALE
            ex = jnp.exp(sc - jnp.max(sc, axis=-1, keepdims=True))
            pr = ex / jnp.sum(ex, axis=-1, keepdims=True)
            o_ref[:, cols] = jnp.dot(pr.astype(BF16), v, preferred_element_type=F32).astype(o_ref.dtype)

    return pl.pallas_call(
        body, grid=(s // ts,), in_specs=[pl.BlockSpec((ts, xw), lambda i: (i, 1)), _fixed((m, 2 * xw))],
        out_specs=_rows(ts, xw), out_shape=jax.ShapeDtypeStruct((s, xw), BF16),
        compiler_params=_params(("parallel",)), name=name,
    )(ppx, kv)


def _xattn_bwd(dxc, ppx, kv, name):
    s = ppx.shape[0]
    m = kv.shape[0]
    ts = _tile(s, TS_XA, 8)
    xw = XA_HEADS * XA_HD

    def body(do_ref, q_ref, kv_ref, dq_ref, dkv_ref):
        @pl.when(pl.program_id(0) == 0)
        def _():
            dkv_ref[...] = jnp.zeros_like(dkv_ref)

        for hd in range(XA_HEADS):
            cols = slice(hd * XA_HD, (hd + 1) * XA_HD)
            vcols = slice(xw + hd * XA_HD, xw + (hd + 1) * XA_HD)
            q = q_ref[:, cols]
            k = kv_ref[:, cols]
            v = kv_ref[:, vcols]
            do = do_ref[:, cols]
            sc = lax.dot_general(q, k, (((1,), (1,)), ((), ())), preferred_element_type=F32) * XA_SCALE
            ex = jnp.exp(sc - jnp.max(sc, axis=-1, keepdims=True))
            pr = ex / jnp.sum(ex, axis=-1, keepdims=True)
            dpr = lax.dot_general(do, v, (((1,), (1,)), ((), ())), preferred_element_type=F32)
            dsc = (pr * (dpr - jnp.sum(dpr * pr, axis=-1, keepdims=True)) * XA_SCALE).astype(BF16)
            dq_ref[:, cols] = jnp.dot(dsc, k, preferred_element_type=F32).astype(dq_ref.dtype)
            dkv_ref[:, cols] += lax.dot_general(dsc, q, (((0,), (0,)), ((), ())), preferred_element_type=F32)
            dkv_ref[:, vcols] += lax.dot_general(pr.astype(BF16), do, (((0,), (0,)), ((), ())), preferred_element_type=F32)

    return pl.pallas_call(
        body, grid=(s // ts,), in_specs=[_rows(ts, xw), pl.BlockSpec((ts, xw), lambda i: (i, 1)), _fixed((m, 2 * xw))],
        out_specs=[_rows(ts, xw), _fixed((m, 2 * xw))],
        out_shape=[jax.ShapeDtypeStruct((s, xw), BF16), jax.ShapeDtypeStruct((m, 2 * xw), F32)],
        compiler_params=_params(("arbitrary",)), name=name,
    )(dxc, ppx, kv)


def _merge_fwd(pgt, ya, yb, yc, name):
    s = pgt.shape[0]
    ts = _tile(s, TS_ROW, 8)

    def body(gt_ref, ya_ref, yb_ref, yc_ref, o_ref):
        acc = _sigmoid(gt_ref[:, 0:D].astype(F32)) * ya_ref[...].astype(F32)
        acc = acc + _sigmoid(gt_ref[:, D:2 * D].astype(F32)) * yb_ref[...].astype(F32)
        acc = acc + _sigmoid(gt_ref[:, 2 * D:3 * D].astype(F32)) * yc_ref[...].astype(F32)
        o_ref[...] = acc.astype(o_ref.dtype)

    return pl.pallas_call(
        body, grid=(s // ts,), in_specs=[_rows(ts, 3 * D)] + [_rows(ts, D)] * 3, out_specs=_rows(ts, D),
        out_shape=jax.ShapeDtypeStruct((s, D), BF16), compiler_params=_params(("parallel",)), name=name,
    )(pgt, ya, yb, yc)


def _merge_bwd(dmerged, pgt, ya, yb, yc, name):
    s = pgt.shape[0]
    ts = _tile(s, TS_ROW, 8)

    def body(dm_ref, gt_ref, ya_ref, yb_ref, yc_ref, dya_ref, dyb_ref, dyc_ref, dgt_ref):
        dm = dm_ref[...].astype(F32)
        for j, (y_ref, dy_ref) in enumerate(((ya_ref, dya_ref), (yb_ref, dyb_ref), (yc_ref, dyc_ref))):
            sig = _sigmoid(gt_ref[:, j * D:(j + 1) * D].astype(F32))
            dy_ref[...] = (dm * sig).astype(dy_ref.dtype)
            dgt_ref[:, j * D:(j + 1) * D] = (dm * y_ref[...].astype(F32) * sig * (1.0 - sig)).astype(dgt_ref.dtype)

    return pl.pallas_call(
        body, grid=(s // ts,), in_specs=[_rows(ts, D), _rows(ts, 3 * D)] + [_rows(ts, D)] * 3,
        out_specs=[_rows(ts, D)] * 3 + [_rows(ts, 3 * D)],
        out_shape=[jax.ShapeDtypeStruct((s, D), BF16)] * 3 + [jax.ShapeDtypeStruct((s, 3 * D), BF16)],
        compiler_params=_params(("parallel",)), name=name,
    )(dmerged, pgt, ya, yb, yc)


def _adam_math(w, g, m, v):
    mn = ADAM_B1 * m + (1.0 - ADAM_B1) * g
    vn = ADAM_B2 * v + (1.0 - ADAM_B2) * (g * g)
    m_hat = mn / (1.0 - ADAM_B1 ** ADAM_STEP)
    v_hat = vn / (1.0 - ADAM_B2 ** ADAM_STEP)
    return -ADAM_LR * (m_hat / (jnp.sqrt(v_hat) + ADAM_EPS) + ADAM_WD * w), mn, vn


def _adamw(w, g, m, v, name):
    r, c = w.shape[-2:]
    tr, tc = _block_of(r, c, cap=512 if r % 16 == 0 else 256)

    def spec(a):
        if a.ndim == 2:
            return pl.BlockSpec((tr, tc), lambda i, j: (i, j))
        return pl.BlockSpec((None, tr, tc), lambda i, j: (0, i, j))

    def body(w_ref, g_ref, m_ref, v_ref, d_ref, mo_ref, vo_ref):
        d_ref[...], mo_ref[...], vo_ref[...] = _adam_math(w_ref[...], g_ref[...], m_ref[...], v_ref[...])

    return pl.pallas_call(
        body, grid=(r // tr, c // tc), in_specs=[spec(a) for a in (w, g, m, v)], out_specs=[spec(w)] * 3,
        out_shape=[jax.ShapeDtypeStruct(w.shape, F32)] * 3, compiler_params=_params(("parallel", "parallel")), name=name,
    )(w, g, m, v)


ANY = pl.BlockSpec(memory_space=pl.ANY)


def _place():
    x, y, c = lax.axis_index("x"), lax.axis_index("y"), lax.axis_index("c")
    chips = [(1 - x, y), (x, 1 - y), (1 - x, 1 - y)]
    return x, y, c, chips


def _half(c, rows):
    h = rows // 2
    return pl.ds(pl.multiple_of(c * h, 8), h)


def _by_cols(rows):
    return rows % 32 != 0 and rows != 16


def _half_of(ref, lead, c):
    r, cols = ref.shape[-2:]
    if _by_cols(r):
        return ref.at[(*lead, slice(None), pl.ds(pl.multiple_of(c * (cols // 2), LANE), cols // 2))]
    return ref.at[(*lead, pl.ds(pl.multiple_of(c * (r // 2), 8), r // 2))]


def _half_shape(shape):
    r, cols = shape[-2:]
    return shape[:-2] + ((r, cols // 2) if _by_cols(r) else (r // 2, cols))


def _block_of(r, cols, cap=256):
    if r % 16 == 0:
        return _tile(r, cap, 16), cols
    return r, _tile(cols, cap)


def _place_shard(shard, chip_arr, out_dtype, name):
    _, r, cols = shard.shape
    tr, tc = _block_of(r, cols)

    def body(chip_ref, s_ref, o_ref):
        o_ref[...] = s_ref[...].astype(o_ref.dtype)

    return pl.pallas_call(
        body,
        grid_spec=pltpu.PrefetchScalarGridSpec(
            num_scalar_prefetch=1, grid=(r // tr, cols // tc),
            in_specs=[pl.BlockSpec((None, tr, tc), lambda i, j, chip_ref: (0, i, j))],
            out_specs=pl.BlockSpec((None, tr, tc), lambda i, j, chip_ref: (chip_ref[0], i, j))),
        out_shape=jax.ShapeDtypeStruct((4, r, cols), out_dtype),
        compiler_params=_params(("parallel", "parallel")), name=name,
    )(chip_arr, shard)


def _gather_shards(bufs, name):
    n = len(bufs)

    def body(*refs):
        outs = refs[n:2 * n]
        send_ici, recv_ici, send_d2d, recv_d2d = refs[2 * n:]
        x, y, c, chips = _place()
        me = 2 * x + y
        sibling = (x, y, 1 - c)

        def ici(w, p, chip_of_block, to):
            rows = _half(c, outs[w].shape[1])
            block = outs[w].at[chip_of_block, rows]
            return pltpu.make_async_remote_copy(
                src_ref=block, dst_ref=block, send_sem=send_ici.at[w, p], recv_sem=recv_ici.at[w, p], device_id=to, device_id_type=MESH)

        def d2d(w, p, chip_of_block, half_of):
            rows = _half(half_of, outs[w].shape[1])
            block = outs[w].at[chip_of_block, rows]
            return pltpu.make_async_remote_copy(
                src_ref=block, dst_ref=block, send_sem=send_d2d.at[w, p], recv_sem=recv_d2d.at[w, p], device_id=sibling, device_id_type=MESH)

        sends = [ici(w, p, me, (*chip, c)) for p, chip in enumerate(chips) for w in range(n)]
        for cp in sends:
            cp.start()
        passed = []
        for p, (px, py) in enumerate(chips):
            for w in range(n):
                ici(w, p, 2 * px + py, (px, py, c)).wait_recv()
                fwd = d2d(w, p, 2 * px + py, c)
                fwd.start()
                passed.append(fwd)
        for p, (px, py) in enumerate(chips):
            for w in range(n):
                d2d(w, p, 2 * px + py, 1 - c).wait_recv()
        for cp in sends + passed:
            cp.wait_send()

    return pl.pallas_call(
        body, in_specs=[ANY] * n, out_specs=[ANY] * n,
        out_shape=[jax.ShapeDtypeStruct(a.shape, a.dtype) for a in bufs],
        input_output_aliases={w: w for w in range(n)},
        scratch_shapes=[pltpu.SemaphoreType.DMA((n, 3))] * 4,
        compiler_params=pltpu.CompilerParams(has_side_effects=True), name=name,
    )(*bufs)


HBM = pl.BlockSpec(memory_space=pltpu.HBM)
SEM = pl.BlockSpec(memory_space=pltpu.SEMAPHORE)
EFFECT = pltpu.SideEffectType.DATAFLOW_SIDE_EFFECTING


def _in_hbm(arrays):
    return [pltpu.with_memory_space_constraint(a, pltpu.HBM) for a in arrays]


def _gather_start(bufs, after, name):
    n, na = len(bufs), len(after)

    def body(*refs):
        send_sem, recv_sem = refs[n + na], refs[n + na + 1]
        outs = refs[n + na + 2:2 * n + na + 2]
        token = refs[2 * n + na + 2]
        x, y, c, chips = _place()
        me = 2 * x + y
        for p, chip in enumerate(chips):
            for w in range(n):
                block = _half_of(outs[w], (me,), c)
                pltpu.make_async_remote_copy(
                    src_ref=block, dst_ref=block, send_sem=send_sem, recv_sem=recv_sem,
                    device_id=(*chip, c), device_id_type=MESH).start()
        token[...] = jnp.zeros_like(token)

    out = pl.pallas_call(
        body, name=name, in_specs=[HBM] * n + [ANY] * na,
        out_specs=[SEM, SEM] + [HBM] * n + [pl.BlockSpec(memory_space=pltpu.VMEM)],
        out_shape=[pltpu.SemaphoreType.DMA(()), pltpu.SemaphoreType.DMA(())]
        + [pltpu.HBM(a.shape, a.dtype) for a in bufs] + [jax.ShapeDtypeStruct((8, LANE), F32)],
        input_output_aliases={w: w + 2 for w in range(n)},
        compiler_params=pltpu.CompilerParams(has_side_effects=EFFECT),
    )(*_in_hbm(bufs), *after)
    return out[0], out[1], list(out[2:2 + n]), out[2 + n]


def _gather_pass(bufs, send_sem, recv_sem, after, name):
    n, na = len(bufs), len(after)

    def body(*refs):
        send1, recv1 = refs[n], refs[n + 1]
        send2, recv2 = refs[n + 2 + na], refs[n + 3 + na]
        outs = refs[n + 4 + na:2 * n + 4 + na]
        x, y, c, chips = _place()
        me = 2 * x + y
        arrivals = [(w, px, py) for px, py in chips for w in range(n)]
        for w, px, py in arrivals:
            first = pltpu.make_async_remote_copy(
                src_ref=_half_of(outs[w], (me,), c), dst_ref=_half_of(outs[w], (2 * px + py,), c), send_sem=send1, recv_sem=recv1,
                device_id=(px, py, c), device_id_type=MESH)
            first.wait_send()
            first.wait_recv()
        for w, px, py in arrivals:
            arrived = _half_of(outs[w], (2 * px + py,), c)
            pltpu.make_async_remote_copy(
                src_ref=arrived, dst_ref=arrived, send_sem=send2, recv_sem=recv2,
                device_id=(x, y, 1 - c), device_id_type=MESH).start()

    out = pl.pallas_call(
        body, name=name, in_specs=[HBM] * n + [SEM, SEM] + [ANY] * na,
        out_specs=[SEM, SEM] + [HBM] * n,
        out_shape=[pltpu.SemaphoreType.DMA(()), pltpu.SemaphoreType.DMA(())] + [pltpu.HBM(a.shape, a.dtype) for a in bufs],
        input_output_aliases={w: w + 2 for w in range(n)},
        compiler_params=pltpu.CompilerParams(has_side_effects=EFFECT),
    )(*bufs, send_sem, recv_sem, *after)
    return out[0], out[1], list(out[2:])


def _gather_finish(bufs, send_sem, recv_sem, after, name):
    n, na = len(bufs), len(after)

    def body(*refs):
        send2, recv2 = refs[n], refs[n + 1]
        outs = refs[n + 2 + na:2 * n + 2 + na]
        x, y, c, chips = _place()
        for p, (px, py) in enumerate(chips):
            for w in range(n):
                passed = pltpu.make_async_remote_copy(
                    src_ref=_half_of(outs[w], (2 * px + py,), c), dst_ref=_half_of(outs[w], (2 * px + py,), 1 - c),
                    send_sem=send2, recv_sem=recv2, device_id=(x, y, 1 - c), device_id_type=MESH)
                passed.wait_send()
                passed.wait_recv()

    out = pl.pallas_call(
        body, name=name, in_specs=[HBM] * n + [SEM, SEM] + [ANY] * na, out_specs=[HBM] * n,
        out_shape=[pltpu.HBM(a.shape, a.dtype) for a in bufs],
        input_output_aliases={w: w for w in range(n)},
        compiler_params=pltpu.CompilerParams(has_side_effects=EFFECT),
    )(*bufs, send_sem, recv_sem, *after)
    return list(out)


def _pair_exchange(grads, name):
    n = len(grads)

    def body(*refs):
        ins, outs = refs[:n], refs[n:2 * n]
        send_sem, recv_sem = refs[2 * n:]
        x, y, c, _ = _place()
        copies = []
        for w in range(n):
            copies.append(pltpu.make_async_remote_copy(
                src_ref=_half_of(ins[w], (slice(None),), 1 - c), dst_ref=outs[w], send_sem=send_sem.at[w], recv_sem=recv_sem.at[w],
                device_id=(x, y, 1 - c), device_id_type=MESH))
        for cp in copies:
            cp.start()
        for cp in copies:
            cp.wait()

    return pl.pallas_call(
        body, in_specs=[ANY] * n, out_specs=[ANY] * n,
        out_shape=[jax.ShapeDtypeStruct(_half_shape(a.shape), a.dtype) for a in grads],
        scratch_shapes=[pltpu.SemaphoreType.DMA((n,))] * 2,
        compiler_params=pltpu.CompilerParams(has_side_effects=True), name=name,
    )(*grads)


def _pair_sum(g, got, c_arr, name):
    _, r, cols = g.shape
    hr, hc = _half_shape((r, cols))
    tr, tc = _block_of(hr, hc)
    nbr, nbc = hr // tr, hc // tc
    by_cols = _by_cols(r)

    def body(c_ref, g_ref, got_ref, o_ref):
        o_ref[...] = (g_ref[...].astype(F32) + got_ref[...].astype(F32)).astype(o_ref.dtype)

    def mine(j, i, k, c_ref):
        return (j, i, c_ref[0] * nbc + k) if by_cols else (j, c_ref[0] * nbr + i, k)

    return pl.pallas_call(
        body,
        grid_spec=pltpu.PrefetchScalarGridSpec(
            num_scalar_prefetch=1, grid=(4, nbr, nbc),
            in_specs=[pl.BlockSpec((None, tr, tc), mine),
                      pl.BlockSpec((None, tr, tc), lambda j, i, k, c_ref: (j, i, k))],
            out_specs=pl.BlockSpec((None, tr, tc), lambda j, i, k, c_ref: (j, i, k))),
        out_shape=jax.ShapeDtypeStruct((4, hr, hc), BF16),
        compiler_params=_params(("parallel", "parallel", "parallel")), name=name,
    )(c_arr, g, got)


def _chip_exchange(parts, name):
    n = len(parts)

    def body(*refs):
        ins, outs = refs[:n], refs[n:2 * n]
        send_sem, recv_sem = refs[2 * n:]
        x, y, c, chips = _place()
        copies = []
        for p, (px, py) in enumerate(chips):
            for w in range(n):
                copies.append(pltpu.make_async_remote_copy(
                    src_ref=ins[w].at[2 * px + py], dst_ref=outs[w].at[p], send_sem=send_sem.at[w, p], recv_sem=recv_sem.at[w, p],
                    device_id=(px, py, c), device_id_type=MESH))
        for cp in copies:
            cp.start()
        for cp in copies:
            cp.wait()

    return pl.pallas_call(
        body, in_specs=[ANY] * n, out_specs=[ANY] * n,
        out_shape=[jax.ShapeDtypeStruct((3,) + a.shape[1:], a.dtype) for a in parts],
        scratch_shapes=[pltpu.SemaphoreType.DMA((n, 3))] * 2,
        compiler_params=pltpu.CompilerParams(has_side_effects=True), name=name,
    )(*parts)


def _chip_exchange_start(parts, after, name):
    n, na = len(parts), len(after)
    lands = [lax.empty((3,) + a.shape[1:], a.dtype) for a in parts]

    def body(*refs):
        send_sem, recv_sem = refs[2 * n + na], refs[2 * n + na + 1]
        srcs = refs[2 * n + na + 2:3 * n + na + 2]
        dsts = refs[3 * n + na + 2:4 * n + na + 2]
        token = refs[4 * n + na + 2]
        x, y, c, chips = _place()
        for p, (px, py) in enumerate(chips):
            for w in range(n):
                pltpu.make_async_remote_copy(
                    src_ref=srcs[w].at[2 * px + py], dst_ref=dsts[w].at[p], send_sem=send_sem, recv_sem=recv_sem,
                    device_id=(px, py, c), device_id_type=MESH).start()
        token[...] = jnp.zeros_like(token)

    out = pl.pallas_call(
        body, name=name, in_specs=[HBM] * (2 * n) + [ANY] * na,
        out_specs=[SEM, SEM] + [HBM] * (2 * n) + [pl.BlockSpec(memory_space=pltpu.VMEM)],
        out_shape=[pltpu.SemaphoreType.DMA(()), pltpu.SemaphoreType.DMA(())]
        + [pltpu.HBM(a.shape, a.dtype) for a in parts + lands] + [jax.ShapeDtypeStruct((8, LANE), F32)],
        input_output_aliases={w: w + 2 for w in range(2 * n)},
        compiler_params=pltpu.CompilerParams(has_side_effects=EFFECT),
    )(*_in_hbm(parts), *_in_hbm(lands), *after)
    return out[0], out[1], list(out[2:2 + n]), list(out[2 + n:2 + 2 * n]), out[2 + 2 * n]


def _chip_exchange_finish(parts, lands, send_sem, recv_sem, after, name):
    n, na = len(parts), len(after)

    def body(*refs):
        send, recv = refs[2 * n], refs[2 * n + 1]
        srcs = refs[2 * n + 2 + na:3 * n + 2 + na]
        dsts = refs[3 * n + 2 + na:4 * n + 2 + na]
        x, y, c, chips = _place()
        for p, (px, py) in enumerate(chips):
            for w in range(n):
                copy = pltpu.make_async_remote_copy(
                    src_ref=srcs[w].at[2 * px + py], dst_ref=dsts[w].at[p], send_sem=send, recv_sem=recv,
                    device_id=(px, py, c), device_id_type=MESH)
                copy.wait_send()
                copy.wait_recv()

    out = pl.pallas_call(
        body, name=name, in_specs=[HBM] * (2 * n) + [SEM, SEM] + [ANY] * na, out_specs=[HBM] * (2 * n),
        out_shape=[pltpu.HBM(a.shape, a.dtype) for a in parts + lands],
        input_output_aliases={w: w for w in range(2 * n)},
        compiler_params=pltpu.CompilerParams(has_side_effects=EFFECT),
    )(*parts, *lands, send_sem, recv_sem, *after)
    return list(out[:n]), list(out[n:])


def _chip_sum(part, got, place_arr, name):
    _, hr, hc = part.shape
    by_cols = _by_cols(hr)
    tr, tc = _block_of(hr, hc)
    nbr, nbc = hr // tr, hc // tc

    def body(place_ref, p_ref, got_ref, o_ref):
        acc = p_ref[...].astype(F32)
        for p in range(3):
            acc = acc + got_ref[p].astype(F32)
        o_ref[...] = acc

    def mine(i, k, place_ref):
        return (i, place_ref[1] * nbc + k) if by_cols else (place_ref[1] * nbr + i, k)

    return pl.pallas_call(
        body,
        grid_spec=pltpu.PrefetchScalarGridSpec(
            num_scalar_prefetch=1, grid=(nbr, nbc),
            in_specs=[pl.BlockSpec((None, tr, tc), lambda i, k, place_ref: (place_ref[0], i, k)),
                      pl.BlockSpec((3, tr, tc), lambda i, k, place_ref: (0, i, k))],
            out_specs=pl.BlockSpec((tr, tc), mine)),
        out_shape=jax.ShapeDtypeStruct((hr, 2 * hc) if by_cols else (2 * hr, hc), F32),
        compiler_params=_params(("parallel", "parallel")), name=name,
    )(place_arr, part, got)


def _pair_join(bufs, name):
    n = len(bufs)

    def body(*refs):
        outs = refs[n:2 * n]
        send_sem, recv_sem = refs[2 * n:]
        x, y, c, _ = _place()
        copies = []
        for w in range(n):
            block = _half_of(outs[w], (), c)
            copies.append(pltpu.make_async_remote_copy(
                src_ref=block, dst_ref=block, send_sem=send_sem.at[w], recv_sem=recv_sem.at[w],
                device_id=(x, y, 1 - c), device_id_type=MESH))
        for cp in copies:
            cp.start()
        for w, cp in enumerate(copies):
            cp.wait_send()
            block = _half_of(outs[w], (), 1 - c)
            pltpu.make_async_remote_copy(
                src_ref=block, dst_ref=block, send_sem=send_sem.at[w], recv_sem=recv_sem.at[w],
                device_id=(x, y, 1 - c), device_id_type=MESH).wait_recv()

    return pl.pallas_call(
        body, in_specs=[ANY] * n, out_specs=[ANY] * n,
        out_shape=[jax.ShapeDtypeStruct(a.shape, a.dtype) for a in bufs],
        input_output_aliases={w: w for w in range(n)},
        scratch_shapes=[pltpu.SemaphoreType.DMA((n,))] * 2,
        compiler_params=pltpu.CompilerParams(has_side_effects=True), name=name,
    )(*bufs)


SMALL = ("ffn1_pre_g", "ffn1_post_g", "mix_pre_g", "gla_norm_g", "mem_norm_g", "mix_post_g", "ffn2_pre_g", "ffn2_post_g", "final_g",
         "b_f", "pool_scale", "w_pool", "w_fu")
N_GAINS = 9
SMALL_PACKS = ((16, D), (24, 512), (4 * LANE, LANE))
W_FU_ROW = 8


def _all_sum_small(gs, name):
    ins = [gs[n] for n in SMALL[:N_GAINS]] + [gs["b_f"], gs["pool_scale"], gs["w_fu_pad"], gs["w_pool"].reshape(4 * LANE, LANE)]

    def body(*refs):
        gain_refs = refs[:N_GAINS]
        bf_ref, ps_ref, wfu_ref, wp_ref = refs[N_GAINS:N_GAINS + 4]
        outs = refs[N_GAINS + 4:N_GAINS + 7]
        mine_a, mine_b, all_a, all_b, all_c, send_sems, recv_sems = refs[N_GAINS + 7:]
        mine_a[...] = jnp.zeros_like(mine_a)
        for i, ref in enumerate(gain_refs):
            mine_a[i:i + 1, :] = ref[...]
        mine_b[...] = jnp.zeros_like(mine_b)
        mine_b[0:1, :] = bf_ref[...]
        mine_b[1:2, :] = ps_ref[...]
        mine_b[W_FU_ROW:W_FU_ROW + GATE_RANK, :] = wfu_ref[0:GATE_RANK, :]
        packs = ((mine_a, all_a), (mine_b, all_b), (wp_ref, all_c))
        x, y, c, chips = _place()
        me, sibling = (x, y, c), (x, y, 1 - c)

        def copy(t, k, block, to, own=False):
            px, py, pc = block
            slot = packs[t][1].at[4 * px + 2 * py + pc]
            return pltpu.make_async_remote_copy(
                src_ref=packs[t][0] if own else slot, dst_ref=slot,
                send_sem=send_sems.at[t, k], recv_sem=recv_sems.at[t, k], device_id=to, device_id_type=MESH)

        started = []
        for t, (mine, everyone) in enumerate(packs):
            everyone[4 * x + 2 * y + c] = mine[...]
            started.append(copy(t, 0, me, sibling, own=True))
            started += [copy(t, 1 + j, me, (*chip, c), own=True) for j, chip in enumerate(chips)]
        for cp in started:
            cp.start()
        passed = []
        for j, chip in enumerate(chips):
            for t in range(len(packs)):
                copy(t, 1 + j, (*chip, c), me).wait_recv()
                fwd = copy(t, 4 + j, (*chip, c), sibling)
                fwd.start()
                passed.append(fwd)
        for t in range(len(packs)):
            copy(t, 0, sibling, me).wait_recv()
            for j, chip in enumerate(chips):
                copy(t, 4 + j, (*chip, 1 - c), me).wait_recv()
        for cp in started + passed:
            cp.wait_send()
        for (_, everyone), o_ref in zip(packs, outs):
            acc = everyone[0]
            for k in range(1, 8):
                acc = acc + everyone[k]
            o_ref[...] = acc

    vmem = pl.BlockSpec(memory_space=pltpu.VMEM)
    return pl.pallas_call(
        body, in_specs=[vmem] * len(ins), out_specs=[vmem] * 3,
        out_shape=[jax.ShapeDtypeStruct(shape, F32) for shape in SMALL_PACKS],
        scratch_shapes=[pltpu.VMEM(SMALL_PACKS[0], F32), pltpu.VMEM(SMALL_PACKS[1], F32)]
        + [pltpu.VMEM((8,) + shape, F32) for shape in SMALL_PACKS]
        + [pltpu.SemaphoreType.DMA((3, 7)), pltpu.SemaphoreType.DMA((3, 7))],
        compiler_params=pltpu.CompilerParams(has_side_effects=True, vmem_limit_bytes=VMEM_LIMIT), name=name,
    )(*ins)


def _adamw_small(sums, params, chip_arr, name):
    flat = [a for n in SMALL for a in params[n]]

    def body(chip_ref, a_ref, b_ref, c_ref, *refs):
        ins, outs = refs[:len(flat)], refs[len(flat):]
        for i, n in enumerate(SMALL):
            w_ref, m_ref, v_ref = ins[3 * i:3 * i + 3]
            g_ref, d_ref, mo_ref, vo_ref = outs[4 * i:4 * i + 4]
            if n == "w_pool":
                pieces = [((0, k), c_ref[k * LANE:(k + 1) * LANE, :]) for k in range(4)]
            elif n == "w_fu":
                mine = pl.ds(pl.multiple_of(chip_ref[0] * LANE, LANE), LANE)
                pieces = [((0,), b_ref[W_FU_ROW:W_FU_ROW + GATE_RANK, mine])]
            elif n == "b_f":
                pieces = [((), b_ref[0:1, :])]
            elif n == "pool_scale":
                pieces = [((), b_ref[1:2, :])]
            else:
                pieces = [((), a_ref[i:i + 1, :])]
            for at, g in pieces:
                d, mn, vn = _adam_math(w_ref[at], g, m_ref[at], v_ref[at])
                g_ref[at] = g
                d_ref[at] = d
                mo_ref[at] = mn
                vo_ref[at] = vn

    def whole(shape):
        return pl.BlockSpec(shape, lambda i, chip_ref: (0,) * len(shape))

    out = pl.pallas_call(
        body,
        grid_spec=pltpu.PrefetchScalarGridSpec(
            num_scalar_prefetch=1, grid=(1,),
            in_specs=[whole(a.shape) for a in list(sums) + flat],
            out_specs=[whole(params[n][0].shape) for n in SMALL for _ in range(4)]),
        out_shape=[jax.ShapeDtypeStruct(params[n][0].shape, F32) for n in SMALL for _ in range(4)],
        compiler_params=_params(("arbitrary",)), name=name,
    )(chip_arr, *sums, *flat)
    return {n: tuple(out[4 * i:4 * i + 4]) for i, n in enumerate(SMALL)}


def _ffn_bwd(dz, x_norm, ab, u, w_in, w_out, x, g_pre, dres, tag, emit, after=()):
    dw_out = _mm(u, dz, ta=True, out_dtype=BF16, tm=1408, tk=2048, after=after, name=tag + "_out_dw")
    behind = emit(tag + "_w_out", dw_out)
    dab = _ffn_out_dx_swiglu(dz, w_out, ab, behind, name=tag + "_out_dx")
    dw_in = _mm(x_norm, dab, ta=True, out_dtype=BF16, tm=512, tk=4096, shards=4, name=tag + "_in_dw")
    behind = emit(tag + "_w_in", dw_in)
    return _mm_rms_bwd([(dab, w_in)], x, g_pre, dres, after=behind, name=tag + "_in_dx")


def _local_step(x, mem, target, small, gather, emit):
    behind = gather("start", "ffn1i", ())
    gather("start", "ffn1o", behind)
    gather("pass", "ffn1i", ())
    big = gather("finish", "ffn1i", ())
    behind = gather("start", "mixa", (big["ffn1_w_in"],))
    behind = gather("start", "mixb", behind)
    h1 = _norm_fwd(x, small["ffn1_pre_g"], BF16, name="ffn1_pre", after=behind)
    ab1, u1 = _ffn_in_swiglu(h1, big["ffn1_w_in"], name="ffn1_in")
    gather("pass", "ffn1o", (ab1,))
    big.update(gather("finish", "ffn1o", ()))
    behind = gather("pass", "mixa", (u1,))
    f1, x1, h = _mm_resid_norm(u1, big["ffn1_w_out"], x, small["ffn1_post_g"], 0.5, small["mix_pre_g"], name="ffn1_out", after=behind)
    big.update(gather("finish", "mixa", (h,)))
    small = dict(small, w_fu_pad=big["w_fu_pad"])
    pg = _mm(h, big["w_gla_t"], tb=True, out_dtype=BF16, tm=1024, tn=PG_W, name="mix_in_gla")
    behind = gather("pass", "mixb", (pg,))
    ppx = _mm(h, big["w_px_t"], tb=True, out_dtype=BF16, after=behind, name="mix_in_px")
    pgt = _mm(h, big["w_gates_t"], tb=True, out_dtype=BF16, tn=1536, name="mix_in_gates")
    big.update(gather("finish", "mixb", (pgt,)))
    behind = gather("start", "ffn2", (big["w_o"],))
    mem_n = _norm_fwd(mem, small["mem_norm_g"], BF16, name="mem_norm", after=behind)
    kv = _mm(mem_n, big["w_mem_kv"], out_dtype=BF16, name="mem_kv")
    ya_in, sp, so, o_gla = _gla_fwd(pg, small["w_fu_pad"], small["b_f"], small["gla_norm_g"], name="gla_fwd")
    yb_in = _pool_fwd(ppx, small["w_pool_b"], small["pool_scale"], name="pool_fwd")
    xc = _xattn_fwd(ppx, kv, name="xattn_fwd")
    ya = _mm(ya_in, big["w_up_gla"], out_dtype=BF16, name="up_gla")
    yb = _mm(yb_in, big["w_up_pool"], out_dtype=BF16, name="up_pool")
    yc = _mm(xc, big["w_up_xattn"], out_dtype=BF16, name="up_xattn")
    merged = _merge_fwd(pgt, ya, yb, yc, name="merge_fwd")
    behind = gather("pass", "ffn2", (merged,))
    ymix, x2, h2 = _mm_resid_norm(merged, big["w_o"], x1, small["mix_post_g"], 1.0, small["ffn2_pre_g"], name="mix_out", after=behind)
    big.update(gather("finish", "ffn2", (h2,)))
    ab2, u2 = _ffn_in_swiglu(h2, big["ffn2_w_in"], name="ffn2_in")
    f2, x3, _ = _mm_resid_norm(u2, big["ffn2_w_out"], x2, small["ffn2_post_g"], 0.5, None, name="ffn2_out")
    gs = {}
    dx3, gs["final_g"], loss = _loss_bwd(x3, small["final_g"], target, name="loss")
    dz2, gs["ffn2_post_g"] = _rms_bwd(f2, small["ffn2_post_g"], [dx3], None, 0.5, BF16, name="ffn2_post_bwd")
    dx2, gs["ffn2_pre_g"] = _ffn_bwd(dz2, h2, ab2, u2, big["ffn2_w_in"], big["ffn2_w_out"], x2, small["ffn2_pre_g"], dx3, "ffn2", emit)
    dy, gs["mix_post_g"] = _rms_bwd(ymix, small["mix_post_g"], [dx2], None, 1.0, BF16, name="mix_post_bwd")
    dmerged = _mm(dy, big["w_o"], tb=True, out_dtype=BF16, name="mix_out_dx")
    emit("w_o", _mm(merged, dy, ta=True, out_dtype=BF16, tm=512, tk=4096, name="mix_out_dw"))
    dya, dyb, dyc, dgt = _merge_bwd(dmerged, pgt, ya, yb, yc, name="merge_bwd")
    dya_in = _mm(dya, big["w_up_gla"], tb=True, out_dtype=BF16, name="up_gla_dx")
    emit("w_up_gla", _mm(ya_in, dya, ta=True, out_dtype=BF16, tm=512, tk=4096, name="up_gla_dw"))
    dyb_in = _mm(dyb, big["w_up_pool"], tb=True, out_dtype=BF16, name="up_pool_dx")
    emit("w_up_pool", _mm(yb_in, dyb, ta=True, out_dtype=BF16, tm=512, tk=4096, shards=4, name="up_pool_dw"))
    dxc = _mm(dyc, big["w_up_xattn"], tb=True, out_dtype=BF16, name="up_xattn_dx")
    emit("w_up_xattn", _mm(xc, dyc, ta=True, out_dtype=BF16, tm=512, tk=4096, shards=4, name="up_xattn_dw"))
    dpg, gs["w_fu_pad"], gs["b_f"], gs["gla_norm_g"] = _gla_bwd(pg, sp, so, o_gla, dya_in, small["w_fu_pad"], small["b_f"], small["gla_norm_g"], name="gla_bwd")
    dp, gs["w_pool"], gs["pool_scale"] = _pool_bwd(dyb_in, ppx, small["w_pool_b"], small["pool_scale"], name="pool_bwd")
    dxq, dkv = _xattn_bwd(dxc, ppx, kv, name="xattn_bwd")
    dkv = dkv.astype(BF16)
    emit("w_mem_kv", _mm(mem_n, dkv, ta=True, out_dtype=BF16, name="mem_kv_dw"))
    dmem_n = _mm(dkv, big["w_mem_kv"], tb=True, name="mem_kv_dx")
    _, gs["mem_norm_g"] = _rms_bwd(mem, small["mem_norm_g"], [dmem_n], None, 1.0, BF16, name="mem_norm_bwd")
    emit("w_gla", _mm(dpg, h, ta=True, out_dtype=BF16, tm=640, tk=4096, name="mix_in_gla_dw"))
    emit("w_p", _mm(dp, h, ta=True, out_dtype=BF16, tm=512, tk=4096, name="mix_in_p_dw"))
    emit("w_xq", _mm(dxq, h, ta=True, out_dtype=BF16, tm=512, tk=4096, name="mix_in_xq_dw"))
    behind = emit("w_gates", _mm(dgt, h, ta=True, out_dtype=BF16, tm=512, tk=4096, name="mix_in_gates_dw"))
    pairs = [(dpg, big["w_gla_t"]), (dp, big["w_p_t"]), (dxq, big["w_xq_t"]), (dgt, big["w_gates_t"])]
    dx1, gs["mix_pre_g"] = _mm_rms_bwd(pairs, x1, small["mix_pre_g"], dx2, after=behind, name="mix_in_dx")
    dz1, gs["ffn1_post_g"] = _rms_bwd(f1, small["ffn1_post_g"], [dx1], None, 0.5, BF16, name="ffn1_post_bwd")
    dx0, gs["ffn1_pre_g"] = _ffn_bwd(dz1, h1, ab1, u1, big["ffn1_w_in"], big["ffn1_w_out"], x, small["ffn1_pre_g"], dx1, "ffn1", emit)
    return loss, dx0, gs


BIG = ("ffn1_w_in", "ffn1_w_out", "w_in", "w_mem_kv", "w_up_gla", "w_up_pool", "w_up_xattn", "w_o", "ffn2_w_in", "ffn2_w_out")
COL_SHARDED = ("ffn1_w_in", "w_in", "w_up_pool", "w_up_xattn", "ffn2_w_in")
GATHER_GROUPS = {"ffn1i": ("ffn1_w_in",), "ffn1o": ("ffn1_w_out",), "mixa": ("w_in", "w_fu"),
                 "mixb": ("w_mem_kv", "w_up_gla", "w_up_pool", "w_up_xattn", "w_o"), "ffn2": ("ffn2_w_in", "ffn2_w_out")}
REDUCE_GROUPS = {"ffn2": ("ffn2_w_out", "ffn2_w_in"),
                 "mix": ("w_o", "w_up_gla", "w_up_pool", "w_up_xattn", "w_mem_kv", "w_gla", "w_p", "w_xq", "w_gates"),
                 "ffn1_out": ("ffn1_w_out",),
                 "ffn1_in": ("ffn1_w_in",)}
GAINS = ("ffn1_pre_g", "ffn1_post_g", "mix_pre_g", "gla_norm_g", "mem_norm_g", "mix_post_g", "ffn2_pre_g", "ffn2_post_g", "final_g")
WEIGHTS = ("ffn1_pre_g", "ffn1_w_in", "ffn1_w_out", "ffn1_post_g", "mix_pre_g", "w_in", "w_fu", "b_f", "gla_norm_g", "w_pool",
           "pool_scale", "mem_norm_g", "w_mem_kv", "w_up_gla", "w_up_pool", "w_up_xattn", "w_o", "mix_post_g", "ffn2_pre_g",
           "ffn2_w_in", "ffn2_w_out", "ffn2_post_g", "final_g")
IN_GLA, IN_F, IN_PX, IN_GATES, IN_END = 0, 3072, 3088, 4112, 7184
def _cols_from_shards(g):
    return jnp.transpose(g, (1, 0, 2)).reshape(g.shape[1], 4 * g.shape[2])


def kernel(x, mem, ffn1_pre_g, ffn1_w_in, ffn1_w_out, ffn1_post_g, mix_pre_g, w_in, w_fu, b_f, gla_norm_g, w_pool, pool_scale, mem_norm_g, w_mem_kv, w_up_gla, w_up_pool, w_up_xattn, w_o, mix_post_g, ffn2_pre_g, ffn2_w_in, ffn2_w_out, ffn2_post_g, final_g, loss_target, m_ffn1_pre_g, m_ffn1_w_in, m_ffn1_w_out, m_ffn1_post_g, m_mix_pre_g, m_w_in, m_w_fu, m_b_f, m_gla_norm_g, m_w_pool, m_pool_scale, m_mem_norm_g, m_w_mem_kv, m_w_up_gla, m_w_up_pool, m_w_up_xattn, m_w_o, m_mix_post_g, m_ffn2_pre_g, m_ffn2_w_in, m_ffn2_w_out, m_ffn2_post_g, m_final_g, v_ffn1_pre_g, v_ffn1_w_in, v_ffn1_w_out, v_ffn1_post_g, v_mix_pre_g, v_w_in, v_w_fu, v_b_f, v_gla_norm_g, v_w_pool, v_pool_scale, v_mem_norm_g, v_w_mem_kv, v_w_up_gla, v_w_up_pool, v_w_up_xattn, v_w_o, v_mix_post_g, v_ffn2_pre_g, v_ffn2_w_in, v_ffn2_w_out, v_ffn2_post_g, v_final_g):
    args = dict(locals())
    w = {n: args[n][0] for n in WEIGHTS}
    m = {n: args["m_" + n][0] for n in WEIGHTS}
    v = {n: args["v_" + n][0] for n in WEIGHTS}
    xi, yi, ci = lax.axis_index("x"), lax.axis_index("y"), lax.axis_index("c")
    chip = 2 * xi + yi

    c_arr = jnp.reshape(ci, (1,)).astype(jnp.int32)
    chip_arr = jnp.reshape(chip, (1,)).astype(jnp.int32)
    place_arr = jnp.stack([chip, ci]).astype(jnp.int32)
    shard_of = {n: (jnp.transpose(args[n][0])[None] if n == "w_in" else args[n]) for n in BIG}
    placed = {n: _place_shard(shard_of[n], chip_arr, BF16, name="place_" + n) for n in BIG}
    placed["w_fu"] = _place_shard(args["w_fu"], chip_arr, F32, name="place_w_fu")
    inflight = {}

    def relayout(names, gathered):
        out = {}
        for n, g in zip(names, gathered):
            if n == "w_fu":
                w_fu_full = _cols_from_shards(g)
                out["w_fu_pad"] = jnp.concatenate([w_fu_full, jnp.zeros((LANE - GATE_RANK, 512), F32)], axis=0).astype(BF16)
            elif n == "w_in":
                wt = g.reshape(IN_END, D)
                out["w_gla_t"] = jnp.concatenate([wt[IN_GLA:IN_PX], jnp.zeros((PG_W - IN_PX, D), BF16)], axis=0)
                out["w_px_t"] = wt[IN_PX:IN_GATES]
                out["w_p_t"] = wt[IN_PX:IN_PX + 512]
                out["w_xq_t"] = wt[IN_PX + 512:IN_GATES]
                out["w_gates_t"] = wt[IN_GATES:IN_END]
            else:
                out[n] = _cols_from_shards(g) if n in COL_SHARDED else g.reshape(4 * g.shape[1], g.shape[2])
        return out

    def gather(op, group, after):
        names = GATHER_GROUPS[group]
        if op == "start":
            inflight[group] = _gather_start([placed[n] for n in names], after, name="gather_" + group + "_start")
            return (inflight[group][3],)
        if op == "pass":
            send, recv, bufs, _ = inflight[group]
            inflight[group] = _gather_pass(bufs, send, recv, after, name="gather_" + group + "_pass")
            return (inflight[group][2][0],)
        send, recv, bufs = inflight.pop(group)
        return relayout(names, _gather_finish(bufs, send, recv, after, name="gather_" + group + "_finish"))

    small = {n: w[n].reshape(1, D) for n in GAINS}
    small["b_f"] = w["b_f"].reshape(1, 512)
    small["pool_scale"] = w["pool_scale"].reshape(1, 512)
    small["w_pool_b"] = w["w_pool"].astype(BF16)

    pending, travelling = {}, {}

    def emit(name, grad):
        pending[name] = grad
        group = next((g for g, names in REDUCE_GROUPS.items() if name == names[-1]), None)
        if group is None:
            return ()
        gb = {n: pending.pop(n) for n in REDUCE_GROUPS[group]}
        if group == "mix":
            dwt = jnp.concatenate([gb.pop("w_gla")[0:IN_PX], gb.pop("w_p"), gb.pop("w_xq"), gb.pop("w_gates")], axis=0)
            gb["w_in"] = dwt.reshape(4, IN_END // 4, D)
        names = list(gb)
        contrib = [gb[n] if n in COL_SHARDED else gb[n].reshape(4, gb[n].shape[0] // 4, gb[n].shape[1]) for n in names]
        from_sibling = _pair_exchange(contrib, name="grads_" + group + "_pair_exchange")
        pair = [_pair_sum(g, got, c_arr, name="grads_pair_sum_" + n) for n, g, got in zip(names, contrib, from_sibling)]
        send, recv, pair, lands, token = _chip_exchange_start(pair, (), name="grads_" + group + "_chip_start")
        travelling[group] = (names, send, recv, pair, lands)
        return (token,)

    loss, grad_x, gs = _local_step(x[0], mem[0], loss_target[0], small, gather, emit)
    loss = lax.psum(loss[0, 0], ("x", "y", "c"))

    small_sums = _all_sum_small(gs, name="sum_small_grads")
    halves = {}
    for group, (names, send, recv, pair, lands) in travelling.items():
        pair, from_chips = _chip_exchange_finish(pair, lands, send, recv, (grad_x,), name="grads_" + group + "_chip_finish")
        for n, p, got in zip(names, pair, from_chips):
            halves[n] = _chip_sum(p, got, place_arr, name="grads_chip_sum_" + n)
    reduced = dict(zip(BIG, _pair_join([halves[n] for n in BIG], name="grads_pair_join")))

    grads, delta, new_m, new_v = {}, {}, {}, {}
    for n in BIG:
        if n == "w_in":
            transposed = [jnp.transpose(args[k][0]) for k in (n, "m_" + n, "v_" + n)]
            updated = _adamw(transposed[0], reduced[n], transposed[1], transposed[2], name="adamw_" + n)
            grads[n] = jnp.transpose(reduced[n])[None]
            delta[n], new_m[n], new_v[n] = (jnp.transpose(a)[None] for a in updated)
            continue
        grads[n] = reduced[n][None]
        delta[n], new_m[n], new_v[n] = _adamw(args[n], reduced[n], args["m_" + n], args["v_" + n], name="adamw_" + n)
    small_params = {n: (args[n], args["m_" + n], args["v_" + n]) for n in SMALL}
    for n, (g, d, mn, vn) in _adamw_small(small_sums, small_params, chip_arr, name="adamw_small").items():
        grads[n], delta[n], new_m[n], new_v[n] = g, d, mn, vn

    outs = [loss, grad_x[None]]
    for group in (grads, delta, new_m, new_v):
        outs += [group[n] for n in WEIGHTS]
    return tuple(outs)
```

```python
import functools

import jax
import jax.numpy as jnp
from jax import lax
from jax.experimental import pallas as pl
from jax.experimental.pallas import tpu as pltpu

F32 = jnp.float32
BF16 = jnp.bfloat16
MESH = pl.DeviceIdType.MESH
HIGHEST = lax.Precision.HIGHEST

D = 1024
DFF = 2816
CHUNK = 64
HEADS = 4
HDK = 128
HDV = 256
GATE_TEMP = 16.0
POOL_WINDOWS = (2, 4, 8, 16)
POOL_HALO = 16
XA_HEADS = 4
XA_HD = 128
EPS = 1e-6
Q_SCALE = HDK ** -0.5
XA_SCALE = XA_HD ** -0.5
PG_Q, PG_K, PG_V, PG_G, PG_F, PG_W = 0, 512, 1024, 2048, 3072, 3200
GATE_RANK = 16
ADAM_LR, ADAM_B1, ADAM_B2, ADAM_EPS, ADAM_WD, ADAM_STEP = 0.001, 0.9, 0.999, 1e-08, 0.01, 10

VMEM_LIMIT = 48 * 1024 * 1024
LANE = 128
TS_ROW = 512
TS_GLA = 512
TS_POOL = 512
TS_XA = 512


def _params(sem):
    return pltpu.CompilerParams(dimension_semantics=sem, vmem_limit_bytes=VMEM_LIMIT)


def _tile(n, cap, unit=LANE):
    if n <= cap:
        return n
    best = None
    for t in range(unit, cap + 1, unit):
        if n % t == 0:
            best = t
    assert best is not None, (n, cap)
    return best


def _sigmoid(x):
    return 0.5 * jnp.tanh(0.5 * x) + 0.5


def _log_sigmoid(x):
    return jnp.minimum(x, 0.0) - jnp.log(1.0 + jnp.exp(-jnp.abs(x)))


def _rms(x):
    r = lax.rsqrt(jnp.mean(x * x, axis=-1, keepdims=True) + EPS)
    return x * r, r


def _rows(ts, w):
    return pl.BlockSpec((ts, w), lambda i: (i, 0))


def _fixed(shape):
    nd = len(shape)
    return pl.BlockSpec(shape, lambda i: (0,) * nd)


def _mm(a, b, *, ta=False, tb=False, out_dtype=F32, tm=2048, tn=1024, tk=1024, shards=1, after=(), name):
    a_blocked, b_blocked = a.ndim == 3, b.ndim == 3
    assert not (a_blocked and ta) and not (b_blocked and tb)
    if a_blocked:
        m, kdim, tk = a.shape[1], a.shape[0] * a.shape[2], a.shape[2]
    else:
        m, kdim = (a.shape[1], a.shape[0]) if ta else a.shape
    if b_blocked:
        n, tn = b.shape[0] * b.shape[2], b.shape[2]
        assert b.shape[1] == kdim and shards in (1, b.shape[0])
    else:
        n = b.shape[0] if tb else b.shape[1]
        assert (b.shape[1] if tb else b.shape[0]) == kdim, (a.shape, b.shape, ta, tb)
        tn = n // shards if shards > 1 else _tile(n, tn)
    tm = _tile(m, tm)
    tk = tk if a_blocked else _tile(kdim, tk)
    kgroup = 2 if (a_blocked and tb and a.shape[0] % 2 == 0) else 1
    nk = kdim // (tk * kgroup)
    dims = (((0 if ta else 1,), (1 if tb else 0,)), ((), ()))

    def body(a_ref, b_ref, *rest):
        o_ref, *acc = rest[len(after):]
        if kgroup == 1:
            part = lax.dot_general(a_ref[...], b_ref[...], dims, preferred_element_type=F32)
        else:
            part = sum(lax.dot_general(a_ref[g], b_ref[:, g * tk:(g + 1) * tk], dims, preferred_element_type=F32) for g in range(kgroup))
        if nk == 1:
            o_ref[...] = part.astype(o_ref.dtype)
            return
        acc_ref, = acc
        k = pl.program_id(2)

        @pl.when(k == 0)
        def _():
            acc_ref[...] = part

        @pl.when(k > 0)
        def _():
            acc_ref[...] += part

        @pl.when(k == nk - 1)
        def _():
            o_ref[...] = acc_ref[...].astype(o_ref.dtype)

    if a_blocked and kgroup > 1:
        a_spec = pl.BlockSpec((kgroup, tm, tk), lambda i, j, k: (k, i, 0))
    elif a_blocked:
        a_spec = pl.BlockSpec((None, tm, tk), lambda i, j, k: (k, i, 0))
    else:
        a_spec = pl.BlockSpec((tk, tm), lambda i, j, k: (k, i)) if ta else pl.BlockSpec((tm, tk), lambda i, j, k: (i, k))
    if b_blocked:
        b_spec = pl.BlockSpec((None, tk, tn), lambda i, j, k: (j, k, 0))
    else:
        b_spec = pl.BlockSpec((tn, tk * kgroup), lambda i, j, k: (j, k)) if tb else pl.BlockSpec((tk, tn), lambda i, j, k: (k, j))
    if shards > 1:
        out_shape = jax.ShapeDtypeStruct((shards, m, tn), out_dtype)
        o_spec = pl.BlockSpec((None, tm, tn), lambda i, j, k: (j, i, 0))
    else:
        out_shape = jax.ShapeDtypeStruct((m, n), out_dtype)
        o_spec = pl.BlockSpec((tm, tn), lambda i, j, k: (i, j))
    return pl.pallas_call(
        body, grid=(m // tm, n // tn, nk), in_specs=[a_spec, b_spec] + [ANY] * len(after), out_specs=o_spec, out_shape=out_shape,
        scratch_shapes=[pltpu.VMEM((tm, tn), F32)] if nk > 1 else [],
        compiler_params=_params(("parallel", "parallel", "arbitrary")), name=name,
    )(a, b, *after)


def _norm_fwd(x, g, out_dtype, name, after=()):
    s, d = x.shape
    ts = _tile(s, TS_ROW, 8)

    def body(x_ref, g_ref, *rest):
        o_ref = rest[len(after)]
        xh, _ = _rms(x_ref[...])
        o_ref[...] = (xh * g_ref[...]).astype(o_ref.dtype)

    return pl.pallas_call(
        body, grid=(s // ts,), in_specs=[_rows(ts, d), _fixed((1, d))] + [ANY] * len(after), out_specs=_rows(ts, d),
        out_shape=jax.ShapeDtypeStruct((s, d), out_dtype), compiler_params=_params(("parallel",)), name=name,
    )(x, g, *after)


def _resid_norm_fwd(x, f, g_post, alpha, g_next, name, after=()):
    s, d = x.shape
    ts = _tile(s, TS_ROW, 8)
    with_h = g_next is not None

    def body(x_ref, f_ref, gp_ref, *rest):
        rest = rest[:1] + rest[1 + len(after):] if with_h else rest[len(after):]
        fh, _ = _rms(f_ref[...])
        xn = x_ref[...] + alpha * (fh * gp_ref[...])
        if with_h:
            gn_ref, xo_ref, h_ref = rest
            xh, _ = _rms(xn)
            h_ref[...] = (xh * gn_ref[...]).astype(h_ref.dtype)
        else:
            xo_ref, = rest
        xo_ref[...] = xn

    ins = [x, f, g_post] + ([g_next] if with_h else []) + list(after)
    in_specs = [_rows(ts, d), _rows(ts, d), _fixed((1, d))] + ([_fixed((1, d))] if with_h else []) + [ANY] * len(after)
    out_shape = [jax.ShapeDtypeStruct((s, d), F32)] + ([jax.ShapeDtypeStruct((s, d), BF16)] if with_h else [])
    out_specs = [_rows(ts, d)] + ([_rows(ts, d)] if with_h else [])
    out = pl.pallas_call(
        body, grid=(s // ts,), in_specs=in_specs, out_specs=out_specs, out_shape=out_shape,
        compiler_params=_params(("parallel",)), name=name,
    )(*ins)
    return (out[0], out[1]) if with_h else (out[0], None)


def _mm_resid_norm(a, w, x, g_post, alpha, g_next, name, after=(), tm=512):
    s, kdim = a.shape
    d = w.shape[1]
    tm = _tile(s, tm)
    with_h = g_next is not None
    na = len(after)

    def body(a_ref, w_ref, x_ref, gp_ref, *rest):
        rest = rest[int(with_h) + na:] if not with_h else rest[:1] + rest[1 + na:]
        for rows in _sub_blocks(tm):
            f = jnp.dot(a_ref[rows, :], w_ref[...], preferred_element_type=F32)
            fh, _ = _rms(f)
            xn = x_ref[rows, :] + alpha * (fh * gp_ref[...])
            if with_h:
                gn_ref, f_ref, xo_ref, h_ref = rest
                xh, _ = _rms(xn)
                h_ref[rows, :] = (xh * gn_ref[...]).astype(h_ref.dtype)
            else:
                f_ref, xo_ref = rest
            f_ref[rows, :] = f
            xo_ref[rows, :] = xn

    ins = [a, w, x, g_post] + ([g_next] if with_h else []) + list(after)
    in_specs = [_rows(tm, kdim), _fixed((kdim, d)), _rows(tm, d), _fixed((1, d))] + ([_fixed((1, d))] if with_h else []) + [ANY] * na
    out_shape = [jax.ShapeDtypeStruct((s, d), F32)] * 2 + ([jax.ShapeDtypeStruct((s, d), BF16)] if with_h else [])
    out = pl.pallas_call(
        body, grid=(s // tm,), in_specs=in_specs, out_specs=[_rows(tm, d)] * len(out_shape), out_shape=out_shape,
        compiler_params=_params(("parallel",)), name=name,
    )(*ins)
    return (out[0], out[1], out[2]) if with_h else (out[0], out[1], None)


def _mm_rms_bwd(pairs, x, g, dres, name, after=(), tm=512):
    s, d = x.shape
    tm = _tile(s, tm)
    n, na = len(pairs), len(after)

    def body(*refs):
        a_refs, w_refs = refs[0:2 * n:2], refs[1:2 * n:2]
        x_ref, g_ref, dres_ref = refs[2 * n:2 * n + 3]
        dx_ref, dg_ref = refs[2 * n + 3 + na:]
        @pl.when(pl.program_id(0) == 0)
        def _():
            dg_ref[...] = jnp.zeros_like(dg_ref)

        for rows in _sub_blocks(tm):
            dy = None
            for a_ref, w_ref in zip(a_refs, w_refs):
                if len(a_ref.shape) == 3:
                    tkb = a_ref.shape[2]
                    parts = [lax.dot_general(a_ref[q, rows, :], w_ref[:, q * tkb:(q + 1) * tkb], (((1,), (1,)), ((), ())),
                                             preferred_element_type=F32) for q in range(a_ref.shape[0])]
                else:
                    parts = [jnp.dot(a_ref[rows, :], w_ref[...], preferred_element_type=F32)]
                for part in parts:
                    dy = part if dy is None else dy + part
            xh, r = _rms(x_ref[rows, :])
            dg_ref[...] += jnp.sum(dy * xh, axis=0, keepdims=True)
            dyg = dy * g_ref[...]
            dx_ref[rows, :] = r * (dyg - xh * jnp.mean(dyg * xh, axis=-1, keepdims=True)) + dres_ref[rows, :]

    ins, in_specs = [], []
    for a_arr, w_arr in pairs:
        ins += [a_arr, w_arr]
        if a_arr.ndim == 3:
            in_specs.append(pl.BlockSpec((a_arr.shape[0], tm, a_arr.shape[2]), lambda i: (0, i, 0)))
        else:
            in_specs.append(_rows(tm, a_arr.shape[1]))
        in_specs.append(pl.BlockSpec(w_arr.shape, lambda i: (0, 0), pipeline_mode=pl.Buffered(1)))
    return pl.pallas_call(
        body, grid=(s // tm,),
        in_specs=in_specs + [_rows(tm, d), _fixed((1, d)), _rows(tm, d)] + [ANY] * na,
        out_specs=[_rows(tm, d), _fixed((1, d))],
        out_shape=[jax.ShapeDtypeStruct((s, d), F32), jax.ShapeDtypeStruct((1, d), F32)],
        compiler_params=_params(("arbitrary",)), name=name,
    )(*ins, x, g, dres, *after)


def _rms_bwd(x, g, dys, dres, alpha, out_dtype, name):
    s, d = x.shape
    ts = _tile(s, TS_ROW, 8)
    ndy = len(dys)
    with_res = dres is not None

    def body(x_ref, g_ref, *rest):
        dy_refs = rest[:ndy]
        rest = rest[ndy:]
        if with_res:
            dres_ref, dx_ref, dg_ref = rest
        else:
            dx_ref, dg_ref = rest
        xh, r = _rms(x_ref[...])
        dy = dy_refs[0][...].astype(F32)
        for ref in dy_refs[1:]:
            dy = dy + ref[...].astype(F32)
        dy = dy * alpha

        @pl.when(pl.program_id(0) == 0)
        def _():
            dg_ref[...] = jnp.zeros_like(dg_ref)

        dg_ref[...] += jnp.sum(dy * xh, axis=0, keepdims=True)
        dyg = dy * g_ref[...]
        dx = r * (dyg - xh * jnp.mean(dyg * xh, axis=-1, keepdims=True))
        if with_res:
            dx = dx + dres_ref[...]
        dx_ref[...] = dx.astype(dx_ref.dtype)

    ins = [x, g] + list(dys) + ([dres] if with_res else [])
    in_specs = [_rows(ts, d), _fixed((1, d))] + [_rows(ts, d)] * (ndy + int(with_res))
    return pl.pallas_call(
        body, grid=(s // ts,), in_specs=in_specs, out_specs=[_rows(ts, d), _fixed((1, d))],
        out_shape=[jax.ShapeDtypeStruct((s, d), out_dtype), jax.ShapeDtypeStruct((1, d), F32)],
        compiler_params=_params(("arbitrary",)), name=name,
    )(*ins)


def _loss_bwd(x, g, target, name):
    s, d = x.shape
    ts = _tile(s, TS_ROW, 8)

    def body(x_ref, g_ref, t_ref, dx_ref, dg_ref, loss_ref):
        xh, r = _rms(x_ref[...])
        gv = g_ref[...]
        diff = xh * gv - t_ref[...]

        @pl.when(pl.program_id(0) == 0)
        def _():
            dg_ref[...] = jnp.zeros_like(dg_ref)
            loss_ref[...] = jnp.zeros_like(loss_ref)

        sq = jnp.sum(diff * diff, axis=1, keepdims=True)
        loss_ref[...] += (0.5 / d) * jnp.sum(sq, axis=0, keepdims=True)
        dy = diff * (1.0 / d)
        dg_ref[...] += jnp.sum(dy * xh, axis=0, keepdims=True)
        dyg = dy * gv
        dx_ref[...] = r * (dyg - xh * jnp.mean(dyg * xh, axis=-1, keepdims=True))

    return pl.pallas_call(
        body, grid=(s // ts,), in_specs=[_rows(ts, d), _fixed((1, d)), _rows(ts, d)],
        out_specs=[_rows(ts, d), _fixed((1, d)), _fixed((8, LANE))],
        out_shape=[jax.ShapeDtypeStruct((s, d), F32), jax.ShapeDtypeStruct((1, d), F32), jax.ShapeDtypeStruct((8, LANE), F32)],
        compiler_params=_params(("arbitrary",)), name=name,
    )(x, g, target)


HALF_FF = DFF // 2


SUB_ROWS = 256


def _sub_blocks(tm):
    sub = SUB_ROWS if tm % SUB_ROWS == 0 else tm
    return [slice(r0, r0 + sub) for r0 in range(0, tm, sub)]


def _ffn_in_swiglu(x_norm, w_in, name, after=(), tm=1024):
    s, d = x_norm.shape
    tm = _tile(s, tm)

    def body(x_ref, wa_ref, wb_ref, *rest):
        ab_ref, u_ref = rest[len(after):]
        for rows in _sub_blocks(tm):
            xv = x_ref[rows, :]
            a = jnp.dot(xv, wa_ref[...], preferred_element_type=F32)
            b = jnp.dot(xv, wb_ref[...], preferred_element_type=F32)
            ab_ref[0, rows, :] = a.astype(ab_ref.dtype)
            ab_ref[1, rows, :] = b.astype(ab_ref.dtype)
            u_ref[rows, :] = (a * _sigmoid(a) * b).astype(u_ref.dtype)

    ab, u = pl.pallas_call(
        body, grid=(s // tm, 2),
        in_specs=[pl.BlockSpec((tm, d), lambda i, j: (i, 0)), pl.BlockSpec((d, HALF_FF), lambda i, j: (0, j)),
                  pl.BlockSpec((d, HALF_FF), lambda i, j: (0, 2 + j))] + [ANY] * len(after),
        out_specs=[pl.BlockSpec((2, None, tm, HALF_FF), lambda i, j: (0, j, i, 0)), pl.BlockSpec((tm, HALF_FF), lambda i, j: (i, j))],
        out_shape=[jax.ShapeDtypeStruct((2, 2, s, HALF_FF), BF16), jax.ShapeDtypeStruct((s, DFF), BF16)],
        compiler_params=_params(("parallel", "parallel")), name=name,
    )(x_norm, w_in, w_in, *after)
    return ab.reshape(4, s, HALF_FF), u


def _ffn_out_dx_swiglu(dz, w_out, ab, after, name, tm=1024):
    s, d = dz.shape
    tm = _tile(s, tm)

    def body(dz_ref, w_ref, ab_ref, *rest):
        dab_ref = rest[len(after)]
        for rows in _sub_blocks(tm):
            du = lax.dot_general(dz_ref[rows, :], w_ref[...], (((1,), (1,)), ((), ())), preferred_element_type=F32)
            a = ab_ref[0, rows, :].astype(F32)
            b = ab_ref[1, rows, :].astype(F32)
            sig = _sigmoid(a)
            dab_ref[0, rows, :] = (du * b * (sig * (1.0 + a * (1.0 - sig)))).astype(dab_ref.dtype)
            dab_ref[1, rows, :] = (du * a * sig).astype(dab_ref.dtype)

    halves = pl.BlockSpec((2, None, tm, HALF_FF), lambda i, j: (0, j, i, 0))
    dab = pl.pallas_call(
        body, grid=(s // tm, 2),
        in_specs=[pl.BlockSpec((tm, d), lambda i, j: (i, 0)), pl.BlockSpec((HALF_FF, d), lambda i, j: (j, 0)), halves] + [ANY] * len(after),
        out_specs=halves, out_shape=jax.ShapeDtypeStruct((2, 2, s, HALF_FF), BF16),
        compiler_params=_params(("parallel", "parallel")), name=name,
    )(dz, w_out, ab.reshape(2, 2, s, HALF_FF), *after)
    return dab.reshape(4, s, HALF_FF)


def _tri(strict):
    r = lax.broadcasted_iota(jnp.int32, (CHUNK, CHUNK), 0)
    c = lax.broadcasted_iota(jnp.int32, (CHUNK, CHUNK), 1)
    return (r > c).astype(F32) if strict else (r >= c).astype(F32)


def _gla_fwd(pg, wfu, b_f, gnorm, name):
    s = pg.shape[0]
    ts = _tile(s, TS_GLA, CHUNK)
    cpb = ts // CHUNK
    nc = s // CHUNK

    def body(pg_ref, wfu_ref, bf_ref, gn_ref, ya_ref, sp_ref, so_ref, o_ref, st_ref, la_ref, dec_ref, u_ref):
        @pl.when(pl.program_id(0) == 0)
        def _():
            st_ref[...] = jnp.zeros_like(st_ref)

        f = jnp.dot(pg_ref[:, PG_F:PG_W], wfu_ref[...], preferred_element_type=F32) + bf_ref[...]
        la_ref[...] = _log_sigmoid(f) * (1.0 / GATE_TEMP)
        tri = _tri(False)
        chunks = [slice(ci * CHUNK, (ci + 1) * CHUNK) for ci in range(cpb)]
        for ci, rows in enumerate(chunks):
            la = la_ref[rows, :]
            b = jnp.dot(tri, la, precision=HIGHEST, preferred_element_type=F32)
            bend = jnp.sum(la, axis=0, keepdims=True)
            e = jnp.exp(bend - b)
            dec_ref[ci:ci + 1, :] = jnp.exp(bend)
            for hd in range(HEADS):
                k = pg_ref[rows, PG_K + hd * HDK:PG_K + (hd + 1) * HDK]
                v = pg_ref[rows, PG_V + hd * HDV:PG_V + (hd + 1) * HDV]
                kt = (k.astype(F32) * e[:, hd * HDK:(hd + 1) * HDK]).astype(BF16)
                u_ref[ci, hd] = lax.dot_general(v, kt, (((0,), (0,)), ((), ())), preferred_element_type=F32)
        for ci in range(cpb):
            for hd in range(HEADS):
                prev = st_ref[hd]
                sp_ref[ci, hd] = prev
                st = prev * dec_ref[ci:ci + 1, hd * HDK:(hd + 1) * HDK] + u_ref[ci, hd]
                st_ref[hd] = st
                so_ref[ci, hd] = st.astype(so_ref.dtype)
        for ci, rows in enumerate(chunks):
            for hd in range(HEADS):
                vc = slice(hd * HDV, (hd + 1) * HDV)
                q = pg_ref[rows, PG_Q + hd * HDK:PG_Q + (hd + 1) * HDK]
                go = pg_ref[rows, PG_G + hd * HDV:PG_G + (hd + 1) * HDV].astype(F32)
                qs = (q.astype(F32) * Q_SCALE).astype(BF16)
                o = lax.dot_general(qs, so_ref[ci, hd], (((1,), (1,)), ((), ())), preferred_element_type=F32)
                o_ref[rows, vc] = o
                oh, _ = _rms(o)
                ya_ref[rows, vc] = (oh * gn_ref[:, vc] * (go * _sigmoid(go))).astype(ya_ref.dtype)

    return pl.pallas_call(
        body, grid=(s // ts,),
        in_specs=[_rows(ts, PG_W), _fixed((LANE, HEADS * HDK)), _fixed((1, HEADS * HDK)), _fixed((1, HEADS * HDV))],
        out_specs=[_rows(ts, HEADS * HDV), pl.BlockSpec((cpb, HEADS, HDV, HDK), lambda i: (i, 0, 0, 0)),
                   pl.BlockSpec((cpb, HEADS, HDV, HDK), lambda i: (i, 0, 0, 0)), _rows(ts, HEADS * HDV)],
        out_shape=[jax.ShapeDtypeStruct((s, HEADS * HDV), BF16), jax.ShapeDtypeStruct((nc, HEADS, HDV, HDK), F32),
                   jax.ShapeDtypeStruct((nc, HEADS, HDV, HDK), BF16), jax.ShapeDtypeStruct((s, HEADS * HDV), F32)],
        scratch_shapes=[pltpu.VMEM((HEADS, HDV, HDK), F32), pltpu.VMEM((ts, HEADS * HDK), F32),
                        pltpu.VMEM((max(cpb, 8), HEADS * HDK), F32), pltpu.VMEM((cpb, HEADS, HDV, HDK), F32)],
        compiler_params=_params(("arbitrary",)), name=name,
    )(pg, wfu, b_f, gnorm)


def _gla_bwd(pg, sp, so, o, dya, wfu, b_f, gnorm, name):
    s = pg.shape[0]
    ts = _tile(s, TS_GLA, CHUNK)
    cpb = ts // CHUNK
    nblk = s // ts

    def body(pg_ref, sp_ref, so_ref, o_ref, dya_ref, wfu_ref, bf_ref, gn_ref, dpg_ref, dwfu_ref, dbf_ref, dgn_ref,
             dst_ref, la_ref, sg_ref, df_ref, e_ref, ktf_ref, dec_ref, g_ref):
        @pl.when(pl.program_id(0) == 0)
        def _():
            dst_ref[...] = jnp.zeros_like(dst_ref)
            dwfu_ref[...] = jnp.zeros_like(dwfu_ref)
            dbf_ref[...] = jnp.zeros_like(dbf_ref)
            dgn_ref[...] = jnp.zeros_like(dgn_ref)

        flow = pg_ref[:, PG_F:PG_W]
        f = jnp.dot(flow, wfu_ref[...], preferred_element_type=F32) + bf_ref[...]
        la_ref[...] = _log_sigmoid(f) * (1.0 / GATE_TEMP)
        sg_ref[...] = _sigmoid(-f) * (1.0 / GATE_TEMP)
        tri = _tri(False)
        tri_strict = _tri(True)
        chunks = [slice(ci * CHUNK, (ci + 1) * CHUNK) for ci in range(cpb)]
        for ci, rows in enumerate(chunks):
            la = la_ref[rows, :]
            b = jnp.dot(tri, la, precision=HIGHEST, preferred_element_type=F32)
            bend = jnp.sum(la, axis=0, keepdims=True)
            e = jnp.exp(bend - b)
            e_ref[rows, :] = e
            dec = jnp.exp(bend)
            dec_ref[ci:ci + 1, :] = dec
            for hd in range(HEADS):
                kc = slice(hd * HDK, (hd + 1) * HDK)
                vc = slice(hd * HDV, (hd + 1) * HDV)
                q = pg_ref[rows, PG_Q + hd * HDK:PG_Q + (hd + 1) * HDK]
                k = pg_ref[rows, PG_K + hd * HDK:PG_K + (hd + 1) * HDK]
                go = pg_ref[rows, PG_G + hd * HDV:PG_G + (hd + 1) * HDV].astype(F32)
                ktf_ref[rows, kc] = k.astype(F32) * e[:, kc]
                st_b = so_ref[ci, hd]
                qs = (q.astype(F32) * Q_SCALE).astype(BF16)
                oh, r = _rms(o_ref[rows, vc])
                gh = gn_ref[:, vc]
                sig = _sigmoid(go)
                dy = dya_ref[rows, vc].astype(F32)
                don = dy * (go * sig)
                dgn_ref[:, vc] += jnp.sum(don * oh, axis=0, keepdims=True)
                dong = don * gh
                do = (r * (dong - oh * jnp.mean(dong * oh, axis=-1, keepdims=True))).astype(BF16)
                g_ref[ci, hd] = lax.dot_general(do, qs, (((0,), (0,)), ((), ())), preferred_element_type=F32)
                dq = jnp.dot(do, st_b, preferred_element_type=F32) * Q_SCALE
                dpg_ref[rows, PG_Q + hd * HDK:PG_Q + (hd + 1) * HDK] = dq.astype(dpg_ref.dtype)
                dgo = dy * (oh * gh) * (sig * (1.0 + go * (1.0 - sig)))
                dpg_ref[rows, PG_G + hd * HDV:PG_G + (hd + 1) * HDV] = dgo.astype(dpg_ref.dtype)
        for ci in reversed(range(cpb)):
            for hd in range(HEADS):
                dst = dst_ref[hd] + g_ref[ci, hd]
                g_ref[ci, hd] = dst
                dst_ref[hd] = dst * dec_ref[ci:ci + 1, hd * HDK:(hd + 1) * HDK]
        for ci, rows in enumerate(chunks):
            for hd in range(HEADS):
                kc = slice(hd * HDK, (hd + 1) * HDK)
                v = pg_ref[rows, PG_V + hd * HDV:PG_V + (hd + 1) * HDV]
                ktf = ktf_ref[rows, kc]
                dst = g_ref[ci, hd]
                dst_b = dst.astype(BF16)
                dkt = jnp.dot(v, dst_b, preferred_element_type=F32)
                dv = lax.dot_general(ktf.astype(BF16), dst_b, (((1,), (1,)), ((), ())), preferred_element_type=F32)
                dd = jnp.sum(dst * sp_ref[ci, hd], axis=0, keepdims=True)
                dla = jnp.dot(tri_strict, dkt * ktf, precision=HIGHEST, preferred_element_type=F32) + dd * dec_ref[ci:ci + 1, kc]
                df_ref[rows, kc] = dla * sg_ref[rows, kc]
                dpg_ref[rows, PG_K + hd * HDK:PG_K + (hd + 1) * HDK] = (dkt * e_ref[rows, kc]).astype(dpg_ref.dtype)
                dpg_ref[rows, PG_V + hd * HDV:PG_V + (hd + 1) * HDV] = dv.astype(dpg_ref.dtype)
        df = df_ref[...]
        df_b = df.astype(BF16)
        dpg_ref[:, PG_F:PG_W] = lax.dot_general(df_b, wfu_ref[...], (((1,), (1,)), ((), ())), preferred_element_type=F32).astype(dpg_ref.dtype)
        dwfu_ref[...] += lax.dot_general(flow, df_b, (((0,), (0,)), ((), ())), preferred_element_type=F32)
        dbf_ref[...] += jnp.sum(df, axis=0, keepdims=True)

    rev = lambda i: (nblk - 1 - i, 0)
    return pl.pallas_call(
        body, grid=(nblk,),
        in_specs=[pl.BlockSpec((ts, PG_W), rev), pl.BlockSpec((cpb, HEADS, HDV, HDK), lambda i: (nblk - 1 - i, 0, 0, 0)),
                  pl.BlockSpec((cpb, HEADS, HDV, HDK), lambda i: (nblk - 1 - i, 0, 0, 0)), pl.BlockSpec((ts, HEADS * HDV), rev),
                  pl.BlockSpec((ts, HEADS * HDV), rev), _fixed((LANE, HEADS * HDK)), _fixed((1, HEADS * HDK)), _fixed((1, HEADS * HDV))],
        out_specs=[pl.BlockSpec((ts, PG_W), rev), _fixed((LANE, HEADS * HDK)), _fixed((1, HEADS * HDK)), _fixed((1, HEADS * HDV))],
        out_shape=[jax.ShapeDtypeStruct((s, PG_W), BF16), jax.ShapeDtypeStruct((LANE, HEADS * HDK), F32),
                   jax.ShapeDtypeStruct((1, HEADS * HDK), F32), jax.ShapeDtypeStruct((1, HEADS * HDV), F32)],
        scratch_shapes=[pltpu.VMEM((HEADS, HDV, HDK), F32)] + [pltpu.VMEM((ts, HEADS * HDK), F32)] * 5
        + [pltpu.VMEM((max(cpb, 8), HEADS * HDK), F32), pltpu.VMEM((cpb, HEADS, HDV, HDK), F32)],
        compiler_params=_params(("arbitrary",)), name=name,
    )(pg, sp, so, o, dya, wfu, b_f, gnorm)


def _window_sums(ext, sign):
    n = ext.shape[0]
    sums = {1: ext}
    w = 1
    while w < POOL_WINDOWS[-1]:
        sums[2 * w] = sums[w] + pltpu.roll(sums[w], w if sign > 0 else n - w, 0)
        w *= 2
    return [sums[POOL_WINDOWS[g]][:, g * LANE:(g + 1) * LANE] for g in range(len(POOL_WINDOWS))]


def _pool_counts(row0, n):
    pos = (row0 + lax.broadcasted_iota(jnp.int32, (n, 1), 0) + 1).astype(F32)
    return [1.0 / jnp.minimum(pos, float(w)) for w in POOL_WINDOWS]


def _pool_fwd(ppx, w_pool, pool_scale, name):
    s = ppx.shape[0]
    ts = _tile(s, TS_POOL, POOL_HALO)
    hb = ts // POOL_HALO
    pw = len(POOL_WINDOWS) * LANE

    def body(p_ref, halo_ref, w_ref, sc_ref, y_ref, ext_ref):
        i = pl.program_id(0)
        p = p_ref[...].astype(F32)
        ext_ref[0:POOL_HALO, :] = jnp.where(i > 0, halo_ref[...].astype(F32), 0.0)
        ext_ref[POOL_HALO:, :] = p
        sums = _window_sums(ext_ref[...], +1)
        cnt = _pool_counts(i * ts, ts)
        for g in range(len(POOL_WINDOWS)):
            cols = slice(g * LANE, (g + 1) * LANE)
            mixed = sums[g][POOL_HALO:, :] * cnt[g] - p[:, cols]
            y = jnp.dot(mixed.astype(BF16), w_ref[g], preferred_element_type=F32)
            y_ref[:, cols] = (y * sc_ref[:, cols]).astype(y_ref.dtype)

    return pl.pallas_call(
        body, grid=(s // ts,),
        in_specs=[pl.BlockSpec((ts, pw), lambda i: (i, 0)), pl.BlockSpec((POOL_HALO, pw), lambda i: (jnp.maximum(i * hb - 1, 0), 0)),
                  _fixed((len(POOL_WINDOWS), LANE, LANE)), _fixed((1, pw))],
        out_specs=_rows(ts, pw), out_shape=jax.ShapeDtypeStruct((s, pw), BF16),
        scratch_shapes=[pltpu.VMEM((ts + POOL_HALO, pw), F32)],
        compiler_params=_params(("parallel",)), name=name,
    )(ppx, ppx, w_pool, pool_scale)


def _pool_bwd(dyb, ppx, w_pool, pool_scale, name):
    s = ppx.shape[0]
    ts = _tile(s, TS_POOL, POOL_HALO)
    hb = ts // POOL_HALO
    nblk = s // ts
    last_halo = s // POOL_HALO - 1
    ng = len(POOL_WINDOWS)
    pw = ng * LANE

    def body(p_ref, halo_ref, dy_ref, dyn_ref, w_ref, sc_ref, dp_ref, dw_ref, dsc_ref, ext_ref, dext_ref, dm_ref):
        i = pl.program_id(0)

        @pl.when(i == 0)
        def _():
            dw_ref[...] = jnp.zeros_like(dw_ref)
            dsc_ref[...] = jnp.zeros_like(dsc_ref)

        p = p_ref[...].astype(F32)
        ext_ref[0:POOL_HALO, :] = jnp.where(i > 0, halo_ref[...].astype(F32), 0.0)
        ext_ref[POOL_HALO:, :] = p
        sums = _window_sums(ext_ref[...], +1)
        cnt = _pool_counts(i * ts, ts + POOL_HALO)
        sc = sc_ref[...]
        dy = dy_ref[...].astype(F32)
        dyn = jnp.where(i < nblk - 1, dyn_ref[...].astype(F32), 0.0)
        for g in range(ng):
            cols = slice(g * LANE, (g + 1) * LANE)
            wg = w_ref[g]
            mixed = (sums[g][POOL_HALO:, :] * cnt[g][0:ts] - p[:, cols]).astype(BF16)
            ypre = jnp.dot(mixed, wg, preferred_element_type=F32)
            dsc_ref[:, cols] += jnp.sum(dy[:, cols] * ypre, axis=0, keepdims=True)
            dyp = (dy[:, cols] * sc[:, cols]).astype(BF16)
            dypn = (dyn[:, cols] * sc[:, cols]).astype(BF16)
            dw_ref[g] += lax.dot_general(mixed, dyp, (((0,), (0,)), ((), ())), preferred_element_type=F32)
            dm = lax.dot_general(dyp, wg, (((1,), (1,)), ((), ())), preferred_element_type=F32)
            dmn = lax.dot_general(dypn, wg, (((1,), (1,)), ((), ())), preferred_element_type=F32)
            dext_ref[0:ts, cols] = dm * cnt[g][0:ts]
            dext_ref[ts:, cols] = dmn * cnt[g][ts:]
            dm_ref[:, cols] = dm
        lead = _window_sums(dext_ref[...], -1)
        for g in range(ng):
            cols = slice(g * LANE, (g + 1) * LANE)
            dp_ref[:, cols] = (lead[g][0:ts, :] - dm_ref[:, cols]).astype(dp_ref.dtype)

    return pl.pallas_call(
        body, grid=(nblk,),
        in_specs=[pl.BlockSpec((ts, pw), lambda i: (i, 0)), pl.BlockSpec((POOL_HALO, pw), lambda i: (jnp.maximum(i * hb - 1, 0), 0)),
                  pl.BlockSpec((ts, pw), lambda i: (i, 0)), pl.BlockSpec((POOL_HALO, pw), lambda i: (jnp.minimum((i + 1) * hb, last_halo), 0)),
                  _fixed((ng, LANE, LANE)), _fixed((1, pw))],
        out_specs=[_rows(ts, pw), _fixed((ng, LANE, LANE)), _fixed((1, pw))],
        out_shape=[jax.ShapeDtypeStruct((s, pw), BF16), jax.ShapeDtypeStruct((ng, LANE, LANE), F32), jax.ShapeDtypeStruct((1, pw), F32)],
        scratch_shapes=[pltpu.VMEM((ts + POOL_HALO, pw), F32), pltpu.VMEM((ts + POOL_HALO, pw), F32), pltpu.VMEM((ts, pw), F32)],
        compiler_params=_params(("arbitrary",)), name=name,
    )(ppx, ppx, dyb, dyb, w_pool, pool_scale)


def _xattn_fwd(ppx, kv, name):
    s = ppx.shape[0]
    m = kv.shape[0]
    ts = _tile(s, TS_XA, 8)
    xw = XA_HEADS * XA_HD

    def body(q_ref, kv_ref, o_ref):
        for hd in range(XA_HEADS):
            cols = slice(hd * XA_HD, (hd + 1) * XA_HD)
            k = kv_ref[:, hd * XA_HD:(hd + 1) * XA_HD]
            v = kv_ref[:, xw + hd * XA_HD:xw + (hd + 1) * XA_HD]
            sc = lax.dot_general(q_ref[:, cols], k, (((1,), (1,)), ((), ())), preferred_element_type=F32) * XA_SCALE
            ex = jnp.exp(sc - jnp.max(sc, axis=-1, keepdims=True))
            pr = ex * (1.0 / jnp.sum(ex, axis=-1, keepdims=True))
            o_ref[:, cols] = jnp.dot(pr.astype(BF16), v, preferred_element_type=F32).astype(o_ref.dtype)

    return pl.pallas_call(
        body, grid=(s // ts,), in_specs=[pl.BlockSpec((ts, xw), lambda i: (i, 1)), _fixed((m, 2 * xw))],
        out_specs=_rows(ts, xw), out_shape=jax.ShapeDtypeStruct((s, xw), BF16),
        compiler_params=_params(("parallel",)), name=name,
    )(ppx, kv)


def _xattn_bwd(dxc, ppx, kv, name):
    s = ppx.shape[0]
    m = kv.shape[0]
    ts = _tile(s, TS_XA, 8)
    xw = XA_HEADS * XA_HD

    def body(do_ref, q_ref, kv_ref, dq_ref, dkv_ref):
        @pl.when(pl.program_id(0) == 0)
        def _():
            dkv_ref[...] = jnp.zeros_like(dkv_ref)

        for hd in range(XA_HEADS):
            cols = slice(hd * XA_HD, (hd + 1) * XA_HD)
            vcols = slice(xw + hd * XA_HD, xw + (hd + 1) * XA_HD)
            q = q_ref[:, cols]
            k = kv_ref[:, cols]
            v = kv_ref[:, vcols]
            do = do_ref[:, cols]
            sc = lax.dot_general(q, k, (((1,), (1,)), ((), ())), preferred_element_type=F32) * XA_SCALE
            ex = jnp.exp(sc - jnp.max(sc, axis=-1, keepdims=True))
            pr = ex * (1.0 / jnp.sum(ex, axis=-1, keepdims=True))
            dpr = lax.dot_general(do, v, (((1,), (1,)), ((), ())), preferred_element_type=F32)
            dsc = (pr * (dpr - jnp.sum(dpr * pr, axis=-1, keepdims=True)) * XA_SCALE).astype(BF16)
            dq_ref[:, cols] = jnp.dot(dsc, k, preferred_element_type=F32).astype(dq_ref.dtype)
            dkv_ref[:, cols] += lax.dot_general(dsc, q, (((0,), (0,)), ((), ())), preferred_element_type=F32)
            dkv_ref[:, vcols] += lax.dot_general(pr.astype(BF16), do, (((0,), (0,)), ((), ())), preferred_element_type=F32)

    return pl.pallas_call(
        body, grid=(s // ts,), in_specs=[_rows(ts, xw), pl.BlockSpec((ts, xw), lambda i: (i, 1)), _fixed((m, 2 * xw))],
        out_specs=[_rows(ts, xw), _fixed((m, 2 * xw))],
        out_shape=[jax.ShapeDtypeStruct((s, xw), BF16), jax.ShapeDtypeStruct((m, 2 * xw), F32)],
        compiler_params=_params(("arbitrary",)), name=name,
    )(dxc, ppx, kv)


def _merge_fwd(pgt, ya, yb, yc, name):
    s = pgt.shape[0]
    ts = _tile(s, TS_ROW, 8)

    def body(gt_ref, ya_ref, yb_ref, yc_ref, o_ref):
        acc = _sigmoid(gt_ref[:, 0:D].astype(F32)) * ya_ref[...].astype(F32)
        acc = acc + _sigmoid(gt_ref[:, D:2 * D].astype(F32)) * yb_ref[...].astype(F32)
        acc = acc + _sigmoid(gt_ref[:, 2 * D:3 * D].astype(F32)) * yc_ref[...].astype(F32)
        o_ref[...] = acc.astype(o_ref.dtype)

    return pl.pallas_call(
        body, grid=(s // ts,), in_specs=[_rows(ts, 3 * D)] + [_rows(ts, D)] * 3, out_specs=_rows(ts, D),
        out_shape=jax.ShapeDtypeStruct((s, D), BF16), compiler_params=_params(("parallel",)), name=name,
    )(pgt, ya, yb, yc)


def _merge_bwd(dmerged, pgt, ya, yb, yc, name):
    s = pgt.shape[0]
    ts = _tile(s, TS_ROW, 8)

    def body(dm_ref, gt_ref, ya_ref, yb_ref, yc_ref, dya_ref, dyb_ref, dyc_ref, dgt_ref):
        dm = dm_ref[...].astype(F32)
        for j, (y_ref, dy_ref) in enumerate(((ya_ref, dya_ref), (yb_ref, dyb_ref), (yc_ref, dyc_ref))):
            sig = _sigmoid(gt_ref[:, j * D:(j + 1) * D].astype(F32))
            dy_ref[...] = (dm * sig).astype(dy_ref.dtype)
            dgt_ref[:, j * D:(j + 1) * D] = (dm * y_ref[...].astype(F32) * sig * (1.0 - sig)).astype(dgt_ref.dtype)

    return pl.pallas_call(
        body, grid=(s // ts,), in_specs=[_rows(ts, D), _rows(ts, 3 * D)] + [_rows(ts, D)] * 3,
        out_specs=[_rows(ts, D)] * 3 + [_rows(ts, 3 * D)],
        out_shape=[jax.ShapeDtypeStruct((s, D), BF16)] * 3 + [jax.ShapeDtypeStruct((s, 3 * D), BF16)],
        compiler_params=_params(("parallel",)), name=name,
    )(dmerged, pgt, ya, yb, yc)


def _adam_math(w, g, m, v):
    mn = ADAM_B1 * m + (1.0 - ADAM_B1) * g
    vn = ADAM_B2 * v + (1.0 - ADAM_B2) * (g * g)
    m_hat = mn / (1.0 - ADAM_B1 ** ADAM_STEP)
    v_hat = vn / (1.0 - ADAM_B2 ** ADAM_STEP)
    return -ADAM_LR * (m_hat / (jnp.sqrt(v_hat) + ADAM_EPS) + ADAM_WD * w), mn, vn


def _adamw(w, g, m, v, name):
    r, c = w.shape[-2:]
    tr, tc = _block_of(r, c, cap=512 if r % 16 == 0 else 256)

    def spec(a):
        if a.ndim == 2:
            return pl.BlockSpec((tr, tc), lambda i, j: (i, j))
        return pl.BlockSpec((None, tr, tc), lambda i, j: (0, i, j))

    def body(w_ref, g_ref, m_ref, v_ref, d_ref, mo_ref, vo_ref):
        d_ref[...], mo_ref[...], vo_ref[...] = _adam_math(w_ref[...], g_ref[...], m_ref[...], v_ref[...])

    return pl.pallas_call(
        body, grid=(r // tr, c // tc), in_specs=[spec(a) for a in (w, g, m, v)], out_specs=[spec(w)] * 3,
        out_shape=[jax.ShapeDtypeStruct(w.shape, F32)] * 3, compiler_params=_params(("parallel", "parallel")), name=name,
    )(w, g, m, v)


ANY = pl.BlockSpec(memory_space=pl.ANY)


def _place():
    x, y, c = lax.axis_index("x"), lax.axis_index("y"), lax.axis_index("c")
    chips = [(1 - x, y), (x, 1 - y), (1 - x, 1 - y)]
    return x, y, c, chips


def _half(c, rows):
    h = rows // 2
    return pl.ds(pl.multiple_of(c * h, 8), h)


def _by_cols(rows):
    return rows % 32 != 0 and rows != 16


def _half_of(ref, lead, c):
    r, cols = ref.shape[-2:]
    if _by_cols(r):
        return ref.at[(*lead, slice(None), pl.ds(pl.multiple_of(c * (cols // 2), LANE), cols // 2))]
    return ref.at[(*lead, pl.ds(pl.multiple_of(c * (r // 2), 8), r // 2))]


def _half_shape(shape):
    r, cols = shape[-2:]
    return shape[:-2] + ((r, cols // 2) if _by_cols(r) else (r // 2, cols))


def _block_of(r, cols, cap=256):
    if r % 16 == 0:
        return _tile(r, cap, 16), cols
    return r, _tile(cols, cap)


def _place_shard(shard, chip_arr, out_dtype, name, after=()):
    _, r, cols = shard.shape
    tr, tc = _block_of(r, cols)

    def body(chip_ref, s_ref, *rest):
        o_ref = rest[len(after)]
        o_ref[...] = s_ref[...].astype(o_ref.dtype)

    return pl.pallas_call(
        body,
        grid_spec=pltpu.PrefetchScalarGridSpec(
            num_scalar_prefetch=1, grid=(r // tr, cols // tc),
            in_specs=[pl.BlockSpec((None, tr, tc), lambda i, j, chip_ref: (0, i, j))] + [ANY] * len(after),
            out_specs=pl.BlockSpec((None, tr, tc), lambda i, j, chip_ref: (chip_ref[0], i, j))),
        out_shape=jax.ShapeDtypeStruct((4, r, cols), out_dtype),
        compiler_params=_params(("parallel", "parallel")), name=name,
    )(chip_arr, shard, *after)


def _gather_shards(bufs, name):
    n = len(bufs)

    def body(*refs):
        outs = refs[n:2 * n]
        send_ici, recv_ici, send_d2d, recv_d2d = refs[2 * n:]
        x, y, c, chips = _place()
        me = 2 * x + y
        sibling = (x, y, 1 - c)

        def ici(w, p, chip_of_block, to):
            rows = _half(c, outs[w].shape[1])
            block = outs[w].at[chip_of_block, rows]
            return pltpu.make_async_remote_copy(
                src_ref=block, dst_ref=block, send_sem=send_ici.at[w, p], recv_sem=recv_ici.at[w, p], device_id=to, device_id_type=MESH)

        def d2d(w, p, chip_of_block, half_of):
            rows = _half(half_of, outs[w].shape[1])
            block = outs[w].at[chip_of_block, rows]
            return pltpu.make_async_remote_copy(
                src_ref=block, dst_ref=block, send_sem=send_d2d.at[w, p], recv_sem=recv_d2d.at[w, p], device_id=sibling, device_id_type=MESH)

        sends = [ici(w, p, me, (*chip, c)) for p, chip in enumerate(chips) for w in range(n)]
        for cp in sends:
            cp.start()
        passed = []
        for p, (px, py) in enumerate(chips):
            for w in range(n):
                ici(w, p, 2 * px + py, (px, py, c)).wait_recv()
                fwd = d2d(w, p, 2 * px + py, c)
                fwd.start()
                passed.append(fwd)
        for p, (px, py) in enumerate(chips):
            for w in range(n):
                d2d(w, p, 2 * px + py, 1 - c).wait_recv()
        for cp in sends + passed:
            cp.wait_send()

    return pl.pallas_call(
        body, in_specs=[ANY] * n, out_specs=[ANY] * n,
        out_shape=[jax.ShapeDtypeStruct(a.shape, a.dtype) for a in bufs],
        input_output_aliases={w: w for w in range(n)},
        scratch_shapes=[pltpu.SemaphoreType.DMA((n, 3))] * 4,
        compiler_params=pltpu.CompilerParams(has_side_effects=True), name=name,
    )(*bufs)


HBM = pl.BlockSpec(memory_space=pltpu.HBM)
SEM = pl.BlockSpec(memory_space=pltpu.SEMAPHORE)
EFFECT = pltpu.SideEffectType.DATAFLOW_SIDE_EFFECTING


def _in_hbm(arrays):
    return [pltpu.with_memory_space_constraint(a, pltpu.HBM) for a in arrays]


def _gather_start(bufs, after, name):
    n, na = len(bufs), len(after)

    def body(*refs):
        send_sem, recv_sem = refs[n + na], refs[n + na + 1]
        outs = refs[n + na + 2:2 * n + na + 2]
        token = refs[2 * n + na + 2]
        x, y, c, chips = _place()
        me = 2 * x + y
        for p, chip in enumerate(chips):
            for w in range(n):
                block = _half_of(outs[w], (me,), c)
                pltpu.make_async_remote_copy(
                    src_ref=block, dst_ref=block, send_sem=send_sem, recv_sem=recv_sem,
                    device_id=(*chip, c), device_id_type=MESH).start()
        token[...] = jnp.zeros_like(token)

    out = pl.pallas_call(
        body, name=name, in_specs=[HBM] * n + [ANY] * na,
        out_specs=[SEM, SEM] + [HBM] * n + [pl.BlockSpec(memory_space=pltpu.VMEM)],
        out_shape=[pltpu.SemaphoreType.DMA(()), pltpu.SemaphoreType.DMA(())]
        + [pltpu.HBM(a.shape, a.dtype) for a in bufs] + [jax.ShapeDtypeStruct((8, LANE), F32)],
        input_output_aliases={w: w + 2 for w in range(n)},
        compiler_params=pltpu.CompilerParams(has_side_effects=EFFECT),
    )(*_in_hbm(bufs), *after)
    return out[0], out[1], list(out[2:2 + n]), out[2 + n]


def _gather_pass(bufs, send_sem, recv_sem, after, name):
    n, na = len(bufs), len(after)

    def body(*refs):
        send1, recv1 = refs[n], refs[n + 1]
        send2, recv2 = refs[n + 2 + na], refs[n + 3 + na]
        outs = refs[n + 4 + na:2 * n + 4 + na]
        x, y, c, chips = _place()
        me = 2 * x + y
        arrivals = [(w, px, py) for px, py in chips for w in range(n)]
        for w, px, py in arrivals:
            first = pltpu.make_async_remote_copy(
                src_ref=_half_of(outs[w], (me,), c), dst_ref=_half_of(outs[w], (2 * px + py,), c), send_sem=send1, recv_sem=recv1,
                device_id=(px, py, c), device_id_type=MESH)
            first.wait_send()
            first.wait_recv()
        for w, px, py in arrivals:
            arrived = _half_of(outs[w], (2 * px + py,), c)
            pltpu.make_async_remote_copy(
                src_ref=arrived, dst_ref=arrived, send_sem=send2, recv_sem=recv2,
                device_id=(x, y, 1 - c), device_id_type=MESH).start()

    out = pl.pallas_call(
        body, name=name, in_specs=[HBM] * n + [SEM, SEM] + [ANY] * na,
        out_specs=[SEM, SEM] + [HBM] * n,
        out_shape=[pltpu.SemaphoreType.DMA(()), pltpu.SemaphoreType.DMA(())] + [pltpu.HBM(a.shape, a.dtype) for a in bufs],
        input_output_aliases={w: w + 2 for w in range(n)},
        compiler_params=pltpu.CompilerParams(has_side_effects=EFFECT),
    )(*bufs, send_sem, recv_sem, *after)
    return out[0], out[1], list(out[2:])


def _gather_finish(bufs, send_sem, recv_sem, after, name):
    n, na = len(bufs), len(after)

    def body(*refs):
        send2, recv2 = refs[n], refs[n + 1]
        outs = refs[n + 2 + na:2 * n + 2 + na]
        x, y, c, chips = _place()
        for p, (px, py) in enumerate(chips):
            for w in range(n):
                passed = pltpu.make_async_remote_copy(
                    src_ref=_half_of(outs[w], (2 * px + py,), c), dst_ref=_half_of(outs[w], (2 * px + py,), 1 - c),
                    send_sem=send2, recv_sem=recv2, device_id=(x, y, 1 - c), device_id_type=MESH)
                passed.wait_send()
                passed.wait_recv()

    out = pl.pallas_call(
        body, name=name, in_specs=[HBM] * n + [SEM, SEM] + [ANY] * na, out_specs=[HBM] * n,
        out_shape=[pltpu.HBM(a.shape, a.dtype) for a in bufs],
        input_output_aliases={w: w for w in range(n)},
        compiler_params=pltpu.CompilerParams(has_side_effects=EFFECT),
    )(*bufs, send_sem, recv_sem, *after)
    return list(out)


def _pair_exchange(grads, name):
    n = len(grads)

    def body(*refs):
        ins, outs = refs[:n], refs[n:2 * n]
        send_sem, recv_sem = refs[2 * n:]
        x, y, c, _ = _place()
        copies = []
        for w in range(n):
            copies.append(pltpu.make_async_remote_copy(
                src_ref=_half_of(ins[w], (slice(None),), 1 - c), dst_ref=outs[w], send_sem=send_sem.at[w], recv_sem=recv_sem.at[w],
                device_id=(x, y, 1 - c), device_id_type=MESH))
        for cp in copies:
            cp.start()
        for cp in copies:
            cp.wait()

    return pl.pallas_call(
        body, in_specs=[ANY] * n, out_specs=[ANY] * n,
        out_shape=[jax.ShapeDtypeStruct(_half_shape(a.shape), a.dtype) for a in grads],
        scratch_shapes=[pltpu.SemaphoreType.DMA((n,))] * 2,
        compiler_params=pltpu.CompilerParams(has_side_effects=True), name=name,
    )(*grads)


def _pair_sum(g, got, c_arr, name):
    _, r, cols = g.shape
    hr, hc = _half_shape((r, cols))
    tr, tc = _block_of(hr, hc)
    nbr, nbc = hr // tr, hc // tc
    by_cols = _by_cols(r)

    def body(c_ref, g_ref, got_ref, o_ref):
        o_ref[...] = (g_ref[...].astype(F32) + got_ref[...].astype(F32)).astype(o_ref.dtype)

    def mine(j, i, k, c_ref):
        return (j, i, c_ref[0] * nbc + k) if by_cols else (j, c_ref[0] * nbr + i, k)

    return pl.pallas_call(
        body,
        grid_spec=pltpu.PrefetchScalarGridSpec(
            num_scalar_prefetch=1, grid=(4, nbr, nbc),
            in_specs=[pl.BlockSpec((None, tr, tc), mine),
                      pl.BlockSpec((None, tr, tc), lambda j, i, k, c_ref: (j, i, k))],
            out_specs=pl.BlockSpec((None, tr, tc), lambda j, i, k, c_ref: (j, i, k))),
        out_shape=jax.ShapeDtypeStruct((4, hr, hc), BF16),
        compiler_params=_params(("parallel", "parallel", "parallel")), name=name,
    )(c_arr, g, got)


def _chip_exchange(parts, name):
    n = len(parts)

    def body(*refs):
        ins, outs = refs[:n], refs[n:2 * n]
        send_sem, recv_sem = refs[2 * n:]
        x, y, c, chips = _place()
        copies = []
        for p, (px, py) in enumerate(chips):
            for w in range(n):
                copies.append(pltpu.make_async_remote_copy(
                    src_ref=ins[w].at[2 * px + py], dst_ref=outs[w].at[p], send_sem=send_sem.at[w, p], recv_sem=recv_sem.at[w, p],
                    device_id=(px, py, c), device_id_type=MESH))
        for cp in copies:
            cp.start()
        for cp in copies:
            cp.wait()

    return pl.pallas_call(
        body, in_specs=[ANY] * n, out_specs=[ANY] * n,
        out_shape=[jax.ShapeDtypeStruct((3,) + a.shape[1:], a.dtype) for a in parts],
        scratch_shapes=[pltpu.SemaphoreType.DMA((n, 3))] * 2,
        compiler_params=pltpu.CompilerParams(has_side_effects=True), name=name,
    )(*parts)


def _chip_exchange_start(parts, after, name):
    n, na = len(parts), len(after)
    lands = [lax.empty((3,) + a.shape[1:], a.dtype) for a in parts]

    def body(*refs):
        send_sem, recv_sem = refs[2 * n + na], refs[2 * n + na + 1]
        srcs = refs[2 * n + na + 2:3 * n + na + 2]
        dsts = refs[3 * n + na + 2:4 * n + na + 2]
        token = refs[4 * n + na + 2]
        x, y, c, chips = _place()
        for p, (px, py) in enumerate(chips):
            for w in range(n):
                pltpu.make_async_remote_copy(
                    src_ref=srcs[w].at[2 * px + py], dst_ref=dsts[w].at[p], send_sem=send_sem, recv_sem=recv_sem,
                    device_id=(px, py, c), device_id_type=MESH).start()
        token[...] = jnp.zeros_like(token)

    out = pl.pallas_call(
        body, name=name, in_specs=[HBM] * (2 * n) + [ANY] * na,
        out_specs=[SEM, SEM] + [HBM] * (2 * n) + [pl.BlockSpec(memory_space=pltpu.VMEM)],
        out_shape=[pltpu.SemaphoreType.DMA(()), pltpu.SemaphoreType.DMA(())]
        + [pltpu.HBM(a.shape, a.dtype) for a in parts + lands] + [jax.ShapeDtypeStruct((8, LANE), F32)],
        input_output_aliases={w: w + 2 for w in range(2 * n)},
        compiler_params=pltpu.CompilerParams(has_side_effects=EFFECT),
    )(*_in_hbm(parts), *_in_hbm(lands), *after)
    return out[0], out[1], list(out[2:2 + n]), list(out[2 + n:2 + 2 * n]), out[2 + 2 * n]


def _chip_exchange_finish(parts, lands, send_sem, recv_sem, after, name):
    n, na = len(parts), len(after)

    def body(*refs):
        send, recv = refs[2 * n], refs[2 * n + 1]
        srcs = refs[2 * n + 2 + na:3 * n + 2 + na]
        dsts = refs[3 * n + 2 + na:4 * n + 2 + na]
        x, y, c, chips = _place()
        for p, (px, py) in enumerate(chips):
            for w in range(n):
                copy = pltpu.make_async_remote_copy(
                    src_ref=srcs[w].at[2 * px + py], dst_ref=dsts[w].at[p], send_sem=send, recv_sem=recv,
                    device_id=(px, py, c), device_id_type=MESH)
                copy.wait_send()
                copy.wait_recv()

    out = pl.pallas_call(
        body, name=name, in_specs=[HBM] * (2 * n) + [SEM, SEM] + [ANY] * na, out_specs=[HBM] * (2 * n),
        out_shape=[pltpu.HBM(a.shape, a.dtype) for a in parts + lands],
        input_output_aliases={w: w for w in range(2 * n)},
        compiler_params=pltpu.CompilerParams(has_side_effects=EFFECT),
    )(*parts, *lands, send_sem, recv_sem, *after)
    return list(out[:n]), list(out[n:])


def _chip_sum(part, got, place_arr, name):
    _, hr, hc = part.shape
    by_cols = _by_cols(hr)
    tr, tc = _block_of(hr, hc)
    nbr, nbc = hr // tr, hc // tc

    def body(place_ref, p_ref, got_ref, o_ref):
        acc = p_ref[...].astype(F32)
        for p in range(3):
            acc = acc + got_ref[p].astype(F32)
        o_ref[...] = acc

    def mine(i, k, place_ref):
        return (i, place_ref[1] * nbc + k) if by_cols else (place_ref[1] * nbr + i, k)

    return pl.pallas_call(
        body,
        grid_spec=pltpu.PrefetchScalarGridSpec(
            num_scalar_prefetch=1, grid=(nbr, nbc),
            in_specs=[pl.BlockSpec((None, tr, tc), lambda i, k, place_ref: (place_ref[0], i, k)),
                      pl.BlockSpec((3, tr, tc), lambda i, k, place_ref: (0, i, k))],
            out_specs=pl.BlockSpec((tr, tc), mine)),
        out_shape=jax.ShapeDtypeStruct((hr, 2 * hc) if by_cols else (2 * hr, hc), F32),
        compiler_params=_params(("parallel", "parallel")), name=name,
    )(place_arr, part, got)


def _pair_join_start(bufs, name):
    n = len(bufs)

    def body(*refs):
        send_sem, recv_sem = refs[n], refs[n + 1]
        outs = refs[n + 2:2 * n + 2]
        token = refs[2 * n + 2]
        x, y, c, _ = _place()
        for w in range(n):
            block = _half_of(outs[w], (), c)
            pltpu.make_async_remote_copy(
                src_ref=block, dst_ref=block, send_sem=send_sem, recv_sem=recv_sem,
                device_id=(x, y, 1 - c), device_id_type=MESH).start()
        token[...] = jnp.zeros_like(token)

    out = pl.pallas_call(
        body, name=name, in_specs=[HBM] * n,
        out_specs=[SEM, SEM] + [HBM] * n + [pl.BlockSpec(memory_space=pltpu.VMEM)],
        out_shape=[pltpu.SemaphoreType.DMA(()), pltpu.SemaphoreType.DMA(())]
        + [pltpu.HBM(a.shape, a.dtype) for a in bufs] + [jax.ShapeDtypeStruct((8, LANE), F32)],
        input_output_aliases={w: w + 2 for w in range(n)},
        compiler_params=pltpu.CompilerParams(has_side_effects=EFFECT),
    )(*_in_hbm(bufs))
    return out[0], out[1], list(out[2:2 + n]), out[2 + n]


def _pair_join_finish(bufs, send_sem, recv_sem, after, name):
    n, na = len(bufs), len(after)

    def body(*refs):
        send, recv = refs[n], refs[n + 1]
        outs = refs[n + 2 + na:2 * n + 2 + na]
        x, y, c, _ = _place()
        for w in range(n):
            copy = pltpu.make_async_remote_copy(
                src_ref=_half_of(outs[w], (), c), dst_ref=_half_of(outs[w], (), 1 - c), send_sem=send, recv_sem=recv,
                device_id=(x, y, 1 - c), device_id_type=MESH)
            copy.wait_send()
            copy.wait_recv()

    out = pl.pallas_call(
        body, name=name, in_specs=[HBM] * n + [SEM, SEM] + [ANY] * na, out_specs=[HBM] * n,
        out_shape=[pltpu.HBM(a.shape, a.dtype) for a in bufs],
        input_output_aliases={w: w for w in range(n)},
        compiler_params=pltpu.CompilerParams(has_side_effects=EFFECT),
    )(*bufs, send_sem, recv_sem, *after)
    return list(out)


SMALL = ("ffn1_pre_g", "ffn1_post_g", "mix_pre_g", "gla_norm_g", "mem_norm_g", "mix_post_g", "ffn2_pre_g", "ffn2_post_g", "final_g",
         "b_f", "pool_scale", "w_pool", "w_fu")
N_GAINS = 9
SMALL_PACKS = ((16, D), (24, 512), (4 * LANE, LANE))
W_FU_ROW = 8


def _all_sum_small(gs, name, after=()):
    ins = [gs[n] for n in SMALL[:N_GAINS]] + [gs["b_f"], gs["pool_scale"], gs["w_fu_pad"], gs["w_pool"].reshape(4 * LANE, LANE)]

    def body(*refs):
        gain_refs = refs[:N_GAINS]
        bf_ref, ps_ref, wfu_ref, wp_ref = refs[N_GAINS:N_GAINS + 4]
        outs = refs[N_GAINS + 4 + len(after):N_GAINS + 7 + len(after)]
        mine_a, mine_b, all_a, all_b, all_c, send_sems, recv_sems = refs[N_GAINS + 7 + len(after):]
        mine_a[...] = jnp.zeros_like(mine_a)
        for i, ref in enumerate(gain_refs):
            mine_a[i:i + 1, :] = ref[...]
        mine_b[...] = jnp.zeros_like(mine_b)
        mine_b[0:1, :] = bf_ref[...]
        mine_b[1:2, :] = ps_ref[...]
        mine_b[W_FU_ROW:W_FU_ROW + GATE_RANK, :] = wfu_ref[0:GATE_RANK, :]
        packs = ((mine_a, all_a), (mine_b, all_b), (wp_ref, all_c))
        x, y, c, chips = _place()
        me, sibling = (x, y, c), (x, y, 1 - c)

        def copy(t, k, block, to, own=False):
            px, py, pc = block
            slot = packs[t][1].at[4 * px + 2 * py + pc]
            return pltpu.make_async_remote_copy(
                src_ref=packs[t][0] if own else slot, dst_ref=slot,
                send_sem=send_sems.at[t, k], recv_sem=recv_sems.at[t, k], device_id=to, device_id_type=MESH)

        started = []
        for t, (mine, everyone) in enumerate(packs):
            everyone[4 * x + 2 * y + c] = mine[...]
            started.append(copy(t, 0, me, sibling, own=True))
            started += [copy(t, 1 + j, me, (*chip, c), own=True) for j, chip in enumerate(chips)]
        for cp in started:
            cp.start()
        passed = []
        for j, chip in enumerate(chips):
            for t in range(len(packs)):
                copy(t, 1 + j, (*chip, c), me).wait_recv()
                fwd = copy(t, 4 + j, (*chip, c), sibling)
                fwd.start()
                passed.append(fwd)
        for t in range(len(packs)):
            copy(t, 0, sibling, me).wait_recv()
            for j, chip in enumerate(chips):
                copy(t, 4 + j, (*chip, 1 - c), me).wait_recv()
        for cp in started + passed:
            cp.wait_send()
        for (_, everyone), o_ref in zip(packs, outs):
            acc = everyone[0]
            for k in range(1, 8):
                acc = acc + everyone[k]
            o_ref[...] = acc

    vmem = pl.BlockSpec(memory_space=pltpu.VMEM)
    return pl.pallas_call(
        body, in_specs=[vmem] * len(ins) + [ANY] * len(after), out_specs=[vmem] * 3,
        out_shape=[jax.ShapeDtypeStruct(shape, F32) for shape in SMALL_PACKS],
        scratch_shapes=[pltpu.VMEM(SMALL_PACKS[0], F32), pltpu.VMEM(SMALL_PACKS[1], F32)]
        + [pltpu.VMEM((8,) + shape, F32) for shape in SMALL_PACKS]
        + [pltpu.SemaphoreType.DMA((3, 7)), pltpu.SemaphoreType.DMA((3, 7))],
        compiler_params=pltpu.CompilerParams(has_side_effects=True, vmem_limit_bytes=VMEM_LIMIT), name=name,
    )(*ins, *after)


def _adamw_small(sums, params, chip_arr, name):
    flat = [a for n in SMALL for a in params[n]]

    def body(chip_ref, a_ref, b_ref, c_ref, *refs):
        ins, outs = refs[:len(flat)], refs[len(flat):]
        for i, n in enumerate(SMALL):
            w_ref, m_ref, v_ref = ins[3 * i:3 * i + 3]
            g_ref, d_ref, mo_ref, vo_ref = outs[4 * i:4 * i + 4]
            if n == "w_pool":
                pieces = [((0, k), c_ref[k * LANE:(k + 1) * LANE, :]) for k in range(4)]
            elif n == "w_fu":
                mine = pl.ds(pl.multiple_of(chip_ref[0] * LANE, LANE), LANE)
                pieces = [((0,), b_ref[W_FU_ROW:W_FU_ROW + GATE_RANK, mine])]
            elif n == "b_f":
                pieces = [((), b_ref[0:1, :])]
            elif n == "pool_scale":
                pieces = [((), b_ref[1:2, :])]
            else:
                pieces = [((), a_ref[i:i + 1, :])]
            for at, g in pieces:
                d, mn, vn = _adam_math(w_ref[at], g, m_ref[at], v_ref[at])
                g_ref[at] = g
                d_ref[at] = d
                mo_ref[at] = mn
                vo_ref[at] = vn

    def whole(shape):
        return pl.BlockSpec(shape, lambda i, chip_ref: (0,) * len(shape))

    out = pl.pallas_call(
        body,
        grid_spec=pltpu.PrefetchScalarGridSpec(
            num_scalar_prefetch=1, grid=(1,),
            in_specs=[whole(a.shape) for a in list(sums) + flat],
            out_specs=[whole(params[n][0].shape) for n in SMALL for _ in range(4)]),
        out_shape=[jax.ShapeDtypeStruct(params[n][0].shape, F32) for n in SMALL for _ in range(4)],
        compiler_params=_params(("arbitrary",)), name=name,
    )(chip_arr, *sums, *flat)
    return {n: tuple(out[4 * i:4 * i + 4]) for i, n in enumerate(SMALL)}


def _ffn_bwd(dz, x_norm, ab, u, w_in, w_out, x, g_pre, dres, tag, emit, after=()):
    dw_out = _mm(u, dz, ta=True, out_dtype=BF16, tm=1408, tk=2048, after=after, name=tag + "_out_dw")
    behind = emit(tag + "_w_out", dw_out)
    dab = _ffn_out_dx_swiglu(dz, w_out, ab, behind, name=tag + "_out_dx")
    dw_in = _mm(x_norm, dab, ta=True, out_dtype=BF16, tm=512, tk=4096, shards=4, name=tag + "_in_dw")
    behind = emit(tag + "_w_in", dw_in)
    return _mm_rms_bwd([(dab, w_in)], x, g_pre, dres, after=behind, name=tag + "_in_dx")


def _local_step(x, mem, target, small, gather, emit):
    behind = gather("start", "ffn1i", ())
    behind = gather("start", "ffn1o", behind)
    h1 = _norm_fwd(x, small["ffn1_pre_g"], BF16, name="ffn1_pre", after=behind)
    gather("pass", "ffn1i", (h1,))
    big = gather("finish", "ffn1i", ())
    behind = gather("start", "mixa", (big["ffn1_w_in"],))
    behind = gather("start", "mixb", behind)
    ab1, u1 = _ffn_in_swiglu(h1, big["ffn1_w_in"], name="ffn1_in", after=behind)
    gather("pass", "ffn1o", (ab1,))
    big.update(gather("finish", "ffn1o", ()))
    behind = gather("pass", "mixa", (u1,))
    f1, x1, h = _mm_resid_norm(u1, big["ffn1_w_out"], x, small["ffn1_post_g"], 0.5, small["mix_pre_g"], name="ffn1_out", after=behind)
    big.update(gather("finish", "mixa", (h,)))
    small = dict(small, w_fu_pad=big["w_fu_pad"])
    pg = _mm(h, big["w_gla_t"], tb=True, out_dtype=BF16, tm=1024, tn=PG_W, name="mix_in_gla")
    behind = gather("pass", "mixb", (pg,))
    ppx = _mm(h, big["w_px_t"], tb=True, out_dtype=BF16, after=behind, name="mix_in_px")
    pgt = _mm(h, big["w_gates_t"], tb=True, out_dtype=BF16, tn=1536, name="mix_in_gates")
    big.update(gather("finish", "mixb", (pgt,)))
    behind = gather("start", "ffn2", (big["w_o"],))
    mem_n = _norm_fwd(mem, small["mem_norm_g"], BF16, name="mem_norm", after=behind)
    kv = _mm(mem_n, big["w_mem_kv"], out_dtype=BF16, name="mem_kv")
    ya_in, sp, so, o_gla = _gla_fwd(pg, small["w_fu_pad"], small["b_f"], small["gla_norm_g"], name="gla_fwd")
    yb_in = _pool_fwd(ppx, small["w_pool_b"], small["pool_scale"], name="pool_fwd")
    xc = _xattn_fwd(ppx, kv, name="xattn_fwd")
    ya = _mm(ya_in, big["w_up_gla"], out_dtype=BF16, name="up_gla")
    yb = _mm(yb_in, big["w_up_pool"], out_dtype=BF16, name="up_pool")
    yc = _mm(xc, big["w_up_xattn"], out_dtype=BF16, name="up_xattn")
    merged = _merge_fwd(pgt, ya, yb, yc, name="merge_fwd")
    behind = gather("pass", "ffn2", (merged,))
    ymix, x2, h2 = _mm_resid_norm(merged, big["w_o"], x1, small["mix_post_g"], 1.0, small["ffn2_pre_g"], name="mix_out", after=behind)
    big.update(gather("finish", "ffn2", (h2,)))
    ab2, u2 = _ffn_in_swiglu(h2, big["ffn2_w_in"], name="ffn2_in")
    f2, x3, _ = _mm_resid_norm(u2, big["ffn2_w_out"], x2, small["ffn2_post_g"], 0.5, None, name="ffn2_out")
    gs = {}
    dx3, gs["final_g"], loss = _loss_bwd(x3, small["final_g"], target, name="loss")
    dz2, gs["ffn2_post_g"] = _rms_bwd(f2, small["ffn2_post_g"], [dx3], None, 0.5, BF16, name="ffn2_post_bwd")
    dx2, gs["ffn2_pre_g"] = _ffn_bwd(dz2, h2, ab2, u2, big["ffn2_w_in"], big["ffn2_w_out"], x2, small["ffn2_pre_g"], dx3, "ffn2", emit)
    dy, gs["mix_post_g"] = _rms_bwd(ymix, small["mix_post_g"], [dx2], None, 1.0, BF16, name="mix_post_bwd")
    dmerged = _mm(dy, big["w_o"], tb=True, out_dtype=BF16, name="mix_out_dx")
    emit("w_o", _mm(merged, dy, ta=True, out_dtype=BF16, tm=512, tk=4096, name="mix_out_dw"))
    dya, dyb, dyc, dgt = _merge_bwd(dmerged, pgt, ya, yb, yc, name="merge_bwd")
    dya_in = _mm(dya, big["w_up_gla"], tb=True, out_dtype=BF16, name="up_gla_dx")
    emit("w_up_gla", _mm(ya_in, dya, ta=True, out_dtype=BF16, tm=512, tk=4096, name="up_gla_dw"))
    dyb_in = _mm(dyb, big["w_up_pool"], tb=True, out_dtype=BF16, name="up_pool_dx")
    emit("w_up_pool", _mm(yb_in, dyb, ta=True, out_dtype=BF16, tm=512, tk=4096, shards=4, name="up_pool_dw"))
    dxc = _mm(dyc, big["w_up_xattn"], tb=True, out_dtype=BF16, name="up_xattn_dx")
    emit("w_up_xattn", _mm(xc, dyc, ta=True, out_dtype=BF16, tm=512, tk=4096, shards=4, name="up_xattn_dw"))
    dpg, gs["w_fu_pad"], gs["b_f"], gs["gla_norm_g"] = _gla_bwd(pg, sp, so, o_gla, dya_in, small["w_fu_pad"], small["b_f"], small["gla_norm_g"], name="gla_bwd")
    dp, gs["w_pool"], gs["pool_scale"] = _pool_bwd(dyb_in, ppx, small["w_pool_b"], small["pool_scale"], name="pool_bwd")
    dxq, dkv = _xattn_bwd(dxc, ppx, kv, name="xattn_bwd")
    dkv = dkv.astype(BF16)
    emit("w_mem_kv", _mm(mem_n, dkv, ta=True, out_dtype=BF16, name="mem_kv_dw"))
    dmem_n = _mm(dkv, big["w_mem_kv"], tb=True, name="mem_kv_dx")
    _, gs["mem_norm_g"] = _rms_bwd(mem, small["mem_norm_g"], [dmem_n], None, 1.0, BF16, name="mem_norm_bwd")
    emit("w_gla", _mm(dpg, h, ta=True, out_dtype=BF16, tm=640, tk=4096, name="mix_in_gla_dw"))
    emit("w_p", _mm(dp, h, ta=True, out_dtype=BF16, tm=512, tk=4096, name="mix_in_p_dw"))
    emit("w_xq", _mm(dxq, h, ta=True, out_dtype=BF16, tm=512, tk=4096, name="mix_in_xq_dw"))
    behind = emit("w_gates", _mm(dgt, h, ta=True, out_dtype=BF16, tm=512, tk=4096, name="mix_in_gates_dw"))
    pairs = [(dpg, big["w_gla_t"]), (dp, big["w_p_t"]), (dxq, big["w_xq_t"]), (dgt, big["w_gates_t"])]
    dx1, gs["mix_pre_g"] = _mm_rms_bwd(pairs, x1, small["mix_pre_g"], dx2, after=behind, name="mix_in_dx")
    dz1, gs["ffn1_post_g"] = _rms_bwd(f1, small["ffn1_post_g"], [dx1], None, 0.5, BF16, name="ffn1_post_bwd")
    dx0, gs["ffn1_pre_g"] = _ffn_bwd(dz1, h1, ab1, u1, big["ffn1_w_in"], big["ffn1_w_out"], x, small["ffn1_pre_g"], dx1, "ffn1", emit)
    return loss, dx0, gs


BIG = ("ffn1_w_in", "ffn1_w_out", "w_in", "w_mem_kv", "w_up_gla", "w_up_pool", "w_up_xattn", "w_o", "ffn2_w_in", "ffn2_w_out")
COL_SHARDED = ("ffn1_w_in", "w_in", "w_up_pool", "w_up_xattn", "ffn2_w_in")
GATHER_GROUPS = {"ffn1i": ("ffn1_w_in",), "ffn1o": ("ffn1_w_out",), "mixa": ("w_in", "w_fu"),
                 "mixb": ("w_mem_kv", "w_up_gla", "w_up_pool", "w_up_xattn", "w_o"), "ffn2": ("ffn2_w_in", "ffn2_w_out")}
REDUCE_GROUPS = {"ffn2": ("ffn2_w_out", "ffn2_w_in"),
                 "mix": ("w_o", "w_up_gla", "w_up_pool", "w_up_xattn", "w_mem_kv", "w_gla", "w_p", "w_xq", "w_gates"),
                 "ffn1_out": ("ffn1_w_out",),
                 "ffn1_in": ("ffn1_w_in",)}
GAINS = ("ffn1_pre_g", "ffn1_post_g", "mix_pre_g", "gla_norm_g", "mem_norm_g", "mix_post_g", "ffn2_pre_g", "ffn2_post_g", "final_g")
WEIGHTS = ("ffn1_pre_g", "ffn1_w_in", "ffn1_w_out", "ffn1_post_g", "mix_pre_g", "w_in", "w_fu", "b_f", "gla_norm_g", "w_pool",
           "pool_scale", "mem_norm_g", "w_mem_kv", "w_up_gla", "w_up_pool", "w_up_xattn", "w_o", "mix_post_g", "ffn2_pre_g",
           "ffn2_w_in", "ffn2_w_out", "ffn2_post_g", "final_g")
IN_GLA, IN_F, IN_PX, IN_GATES, IN_END = 0, 3072, 3088, 4112, 7184
def _cols_from_shards(g):
    return jnp.transpose(g, (1, 0, 2)).reshape(g.shape[1], 4 * g.shape[2])


def kernel(x, mem, ffn1_pre_g, ffn1_w_in, ffn1_w_out, ffn1_post_g, mix_pre_g, w_in, w_fu, b_f, gla_norm_g, w_pool, pool_scale, mem_norm_g, w_mem_kv, w_up_gla, w_up_pool, w_up_xattn, w_o, mix_post_g, ffn2_pre_g, ffn2_w_in, ffn2_w_out, ffn2_post_g, final_g, loss_target, m_ffn1_pre_g, m_ffn1_w_in, m_ffn1_w_out, m_ffn1_post_g, m_mix_pre_g, m_w_in, m_w_fu, m_b_f, m_gla_norm_g, m_w_pool, m_pool_scale, m_mem_norm_g, m_w_mem_kv, m_w_up_gla, m_w_up_pool, m_w_up_xattn, m_w_o, m_mix_post_g, m_ffn2_pre_g, m_ffn2_w_in, m_ffn2_w_out, m_ffn2_post_g, m_final_g, v_ffn1_pre_g, v_ffn1_w_in, v_ffn1_w_out, v_ffn1_post_g, v_mix_pre_g, v_w_in, v_w_fu, v_b_f, v_gla_norm_g, v_w_pool, v_pool_scale, v_mem_norm_g, v_w_mem_kv, v_w_up_gla, v_w_up_pool, v_w_up_xattn, v_w_o, v_mix_post_g, v_ffn2_pre_g, v_ffn2_w_in, v_ffn2_w_out, v_ffn2_post_g, v_final_g):
    args = dict(locals())
    w = {n: args[n][0] for n in WEIGHTS}
    m = {n: args["m_" + n][0] for n in WEIGHTS}
    v = {n: args["v_" + n][0] for n in WEIGHTS}
    xi, yi, ci = lax.axis_index("x"), lax.axis_index("y"), lax.axis_index("c")
    chip = 2 * xi + yi

    c_arr = jnp.reshape(ci, (1,)).astype(jnp.int32)
    chip_arr = jnp.reshape(chip, (1,)).astype(jnp.int32)
    place_arr = jnp.stack([chip, ci]).astype(jnp.int32)
    shard_of = {n: (jnp.transpose(args[n][0])[None] if n == "w_in" else args[n]) for n in BIG}
    shard_of["w_fu"] = args["w_fu"]
    placed, inflight = {}, {}

    def place(names, after):
        for n in names:
            if n not in placed:
                placed[n] = _place_shard(shard_of[n], chip_arr, F32 if n == "w_fu" else BF16, name="place_" + n, after=after)

    def relayout(names, gathered):
        out = {}
        for n, g in zip(names, gathered):
            if n == "w_fu":
                w_fu_full = _cols_from_shards(g)
                out["w_fu_pad"] = jnp.concatenate([w_fu_full, jnp.zeros((LANE - GATE_RANK, 512), F32)], axis=0).astype(BF16)
            elif n == "w_in":
                wt = g.reshape(IN_END, D)
                out["w_gla_t"] = jnp.concatenate([wt[IN_GLA:IN_PX], jnp.zeros((PG_W - IN_PX, D), BF16)], axis=0)
                out["w_px_t"] = wt[IN_PX:IN_GATES]
                out["w_p_t"] = wt[IN_PX:IN_PX + 512]
                out["w_xq_t"] = wt[IN_PX + 512:IN_GATES]
                out["w_gates_t"] = wt[IN_GATES:IN_END]
            else:
                out[n] = _cols_from_shards(g) if n in COL_SHARDED else g.reshape(4 * g.shape[1], g.shape[2])
        return out

    def gather(op, group, after):
        names = GATHER_GROUPS[group]
        if op == "start":
            place(names, ())
            inflight[group] = _gather_start([placed[n] for n in names], after, name="gather_" + group + "_start")
            behind = (inflight[group][3],)
            if group == "ffn1o":
                place(shard_of, behind)
            return behind
        if op == "pass":
            send, recv, bufs, _ = inflight[group]
            inflight[group] = _gather_pass(bufs, send, recv, after, name="gather_" + group + "_pass")
            return (inflight[group][2][0],)
        send, recv, bufs = inflight.pop(group)
        return relayout(names, _gather_finish(bufs, send, recv, after, name="gather_" + group + "_finish"))

    small = {n: w[n].reshape(1, D) for n in GAINS}
    small["b_f"] = w["b_f"].reshape(1, 512)
    small["pool_scale"] = w["pool_scale"].reshape(1, 512)
    small["w_pool_b"] = w["w_pool"].astype(BF16)

    pending, travelling = {}, {}

    def emit(name, grad):
        pending[name] = grad
        group = next((g for g, names in REDUCE_GROUPS.items() if name == names[-1]), None)
        if group is None:
            return ()
        gb = {n: pending.pop(n) for n in REDUCE_GROUPS[group]}
        if group == "mix":
            dwt = jnp.concatenate([gb.pop("w_gla")[0:IN_PX], gb.pop("w_p"), gb.pop("w_xq"), gb.pop("w_gates")], axis=0)
            gb["w_in"] = dwt.reshape(4, IN_END // 4, D)
        names = list(gb)
        contrib = [gb[n] if n in COL_SHARDED else gb[n].reshape(4, gb[n].shape[0] // 4, gb[n].shape[1]) for n in names]
        from_sibling = _pair_exchange(contrib, name="grads_" + group + "_pair_exchange")
        pair = [_pair_sum(g, got, c_arr, name="grads_pair_sum_" + n) for n, g, got in zip(names, contrib, from_sibling)]
        send, recv, pair, lands, token = _chip_exchange_start(pair, (), name="grads_" + group + "_chip_start")
        travelling[group] = (names, send, recv, pair, lands)
        return (token,)

    loss, grad_x, gs = _local_step(x[0], mem[0], loss_target[0], small, gather, emit)
    loss = lax.psum(loss[0, 0], ("x", "y", "c"))

    halves = {}
    for group, (names, send, recv, pair, lands) in travelling.items():
        pair, from_chips = _chip_exchange_finish(pair, lands, send, recv, (grad_x,), name="grads_" + group + "_chip_finish")
        for n, p, got in zip(names, pair, from_chips):
            halves[n] = _chip_sum(p, got, place_arr, name="grads_chip_sum_" + n)
    send, recv, joining, token = _pair_join_start([halves[n] for n in BIG], name="grads_pair_join_start")
    small_sums = _all_sum_small(gs, name="sum_small_grads", after=(token,))
    reduced = dict(zip(BIG, _pair_join_finish(joining, send, recv, (small_sums[0],), name="grads_pair_join_finish")))

    grads, delta, new_m, new_v = {}, {}, {}, {}
    for n in BIG:
        if n == "w_in":
            transposed = [jnp.transpose(args[k][0]) for k in (n, "m_" + n, "v_" + n)]
            updated = _adamw(transposed[0], reduced[n], transposed[1], transposed[2], name="adamw_" + n)
            grads[n] = jnp.transpose(reduced[n])[None]
            delta[n], new_m[n], new_v[n] = (jnp.transpose(a)[None] for a in updated)
            continue
        grads[n] = reduced[n][None]
        delta[n], new_m[n], new_v[n] = _adamw(args[n], reduced[n], args["m_" + n], args["v_" + n], name="adamw_" + n)
    small_params = {n: (args[n], args["m_" + n], args["v_" + n]) for n in SMALL}
    for n, (g, d, mn, vn) in _adamw_small(small_sums, small_params, chip_arr, name="adamw_small").items():
        grads[n], delta[n], new_m[n], new_v[n] = g, d, mn, vn

    outs = [loss, grad_x[None]]
    for group in (grads, delta, new_m, new_v):
        outs += [group[n] for n in WEIGHTS]
    return tuple(outs)
```

```python
import functools

import jax
import jax.numpy as jnp
from jax import lax
from jax.experimental import pallas as pl
from jax.experimental.pallas import tpu as pltpu

F32 = jnp.float32
BF16 = jnp.bfloat16
MESH = pl.DeviceIdType.MESH
HIGHEST = lax.Precision.HIGHEST

D = 1024
DFF = 2816
CHUNK = 64
HEADS = 4
HDK = 128
HDV = 256
GATE_TEMP = 16.0
POOL_WINDOWS = (2, 4, 8, 16)
POOL_HALO = 16
XA_HEADS = 4
XA_HD = 128
EPS = 1e-6
Q_SCALE = HDK ** -0.5
XA_SCALE = XA_HD ** -0.5
PG_Q, PG_K, PG_V, PG_G, PG_F, PG_W = 0, 512, 1024, 2048, 3072, 3200
GATE_RANK = 16
ADAM_LR, ADAM_B1, ADAM_B2, ADAM_EPS, ADAM_WD, ADAM_STEP = 0.001, 0.9, 0.999, 1e-08, 0.01, 10

VMEM_LIMIT = 48 * 1024 * 1024
LANE = 128
TS_ROW = 512
TS_GLA = 512
TS_POOL = 512
TS_XA = 512


def _params(sem):
    return pltpu.CompilerParams(dimension_semantics=sem, vmem_limit_bytes=VMEM_LIMIT)


def _tile(n, cap, unit=LANE):
    if n <= cap:
        return n
    best = None
    for t in range(unit, cap + 1, unit):
        if n % t == 0:
            best = t
    assert best is not None, (n, cap)
    return best


def _sigmoid(x):
    return 0.5 * jnp.tanh(0.5 * x) + 0.5


def _log_sigmoid(x):
    return jnp.minimum(x, 0.0) - jnp.log(1.0 + jnp.exp(-jnp.abs(x)))


def _rms(x):
    r = lax.rsqrt(jnp.mean(x * x, axis=-1, keepdims=True) + EPS)
    return x * r, r


def _rows(ts, w):
    return pl.BlockSpec((ts, w), lambda i: (i, 0))


def _fixed(shape):
    nd = len(shape)
    return pl.BlockSpec(shape, lambda i: (0,) * nd)


def _mm(a, b, *, ta=False, tb=False, out_dtype=F32, tm=2048, tn=1024, tk=1024, shards=1, after=(), name):
    a_blocked, b_blocked = a.ndim == 3, b.ndim == 3
    assert not (a_blocked and ta) and not (b_blocked and tb)
    if a_blocked:
        m, kdim, tk = a.shape[1], a.shape[0] * a.shape[2], a.shape[2]
    else:
        m, kdim = (a.shape[1], a.shape[0]) if ta else a.shape
    if b_blocked:
        n, tn = b.shape[0] * b.shape[2], b.shape[2]
        assert b.shape[1] == kdim and shards in (1, b.shape[0])
    else:
        n = b.shape[0] if tb else b.shape[1]
        assert (b.shape[1] if tb else b.shape[0]) == kdim, (a.shape, b.shape, ta, tb)
        tn = n // shards if shards > 1 else _tile(n, tn)
    tm = _tile(m, tm)
    tk = tk if a_blocked else _tile(kdim, tk)
    kgroup = 2 if (a_blocked and tb and a.shape[0] % 2 == 0) else 1
    nk = kdim // (tk * kgroup)
    dims = (((0 if ta else 1,), (1 if tb else 0,)), ((), ()))

    def body(a_ref, b_ref, *rest):
        o_ref, *acc = rest[len(after):]
        if kgroup == 1:
            part = lax.dot_general(a_ref[...], b_ref[...], dims, preferred_element_type=F32)
        else:
            part = sum(lax.dot_general(a_ref[g], b_ref[:, g * tk:(g + 1) * tk], dims, preferred_element_type=F32) for g in range(kgroup))
        if nk == 1:
            o_ref[...] = part.astype(o_ref.dtype)
            return
        acc_ref, = acc
        k = pl.program_id(2)

        @pl.when(k == 0)
        def _():
            acc_ref[...] = part

        @pl.when(k > 0)
        def _():
            acc_ref[...] += part

        @pl.when(k == nk - 1)
        def _():
            o_ref[...] = acc_ref[...].astype(o_ref.dtype)

    if a_blocked and kgroup > 1:
        a_spec = pl.BlockSpec((kgroup, tm, tk), lambda i, j, k: (k, i, 0))
    elif a_blocked:
        a_spec = pl.BlockSpec((None, tm, tk), lambda i, j, k: (k, i, 0))
    else:
        a_spec = pl.BlockSpec((tk, tm), lambda i, j, k: (k, i)) if ta else pl.BlockSpec((tm, tk), lambda i, j, k: (i, k))
    if b_blocked:
        b_spec = pl.BlockSpec((None, tk, tn), lambda i, j, k: (j, k, 0))
    else:
        b_spec = pl.BlockSpec((tn, tk * kgroup), lambda i, j, k: (j, k)) if tb else pl.BlockSpec((tk, tn), lambda i, j, k: (k, j))
    if shards > 1:
        out_shape = jax.ShapeDtypeStruct((shards, m, tn), out_dtype)
        o_spec = pl.BlockSpec((None, tm, tn), lambda i, j, k: (j, i, 0))
    else:
        out_shape = jax.ShapeDtypeStruct((m, n), out_dtype)
        o_spec = pl.BlockSpec((tm, tn), lambda i, j, k: (i, j))
    return pl.pallas_call(
        body, grid=(m // tm, n // tn, nk), in_specs=[a_spec, b_spec] + [ANY] * len(after), out_specs=o_spec, out_shape=out_shape,
        scratch_shapes=[pltpu.VMEM((tm, tn), F32)] if nk > 1 else [],
        compiler_params=_params(("parallel", "parallel", "arbitrary")), name=name,
    )(a, b, *after)


def _norm_fwd(x, g, out_dtype, name, after=()):
    s, d = x.shape
    ts = _tile(s, TS_ROW, 8)

    def body(x_ref, g_ref, *rest):
        o_ref = rest[len(after)]
        xh, _ = _rms(x_ref[...])
        o_ref[...] = (xh * g_ref[...]).astype(o_ref.dtype)

    return pl.pallas_call(
        body, grid=(s // ts,), in_specs=[_rows(ts, d), _fixed((1, d))] + [ANY] * len(after), out_specs=_rows(ts, d),
        out_shape=jax.ShapeDtypeStruct((s, d), out_dtype), compiler_params=_params(("parallel",)), name=name,
    )(x, g, *after)


def _resid_norm_fwd(x, f, g_post, alpha, g_next, name, after=()):
    s, d = x.shape
    ts = _tile(s, TS_ROW, 8)
    with_h = g_next is not None

    def body(x_ref, f_ref, gp_ref, *rest):
        rest = rest[:1] + rest[1 + len(after):] if with_h else rest[len(after):]
        fh, _ = _rms(f_ref[...])
        xn = x_ref[...] + alpha * (fh * gp_ref[...])
        if with_h:
            gn_ref, xo_ref, h_ref = rest
            xh, _ = _rms(xn)
            h_ref[...] = (xh * gn_ref[...]).astype(h_ref.dtype)
        else:
            xo_ref, = rest
        xo_ref[...] = xn

    ins = [x, f, g_post] + ([g_next] if with_h else []) + list(after)
    in_specs = [_rows(ts, d), _rows(ts, d), _fixed((1, d))] + ([_fixed((1, d))] if with_h else []) + [ANY] * len(after)
    out_shape = [jax.ShapeDtypeStruct((s, d), F32)] + ([jax.ShapeDtypeStruct((s, d), BF16)] if with_h else [])
    out_specs = [_rows(ts, d)] + ([_rows(ts, d)] if with_h else [])
    out = pl.pallas_call(
        body, grid=(s // ts,), in_specs=in_specs, out_specs=out_specs, out_shape=out_shape,
        compiler_params=_params(("parallel",)), name=name,
    )(*ins)
    return (out[0], out[1]) if with_h else (out[0], None)


def _mm_resid_norm(a, w, x, g_post, alpha, g_next, name, after=(), tm=512):
    s, kdim = a.shape
    d = w.shape[1]
    tm = _tile(s, tm)
    with_h = g_next is not None
    na = len(after)

    def body(a_ref, w_ref, x_ref, gp_ref, *rest):
        rest = rest[int(with_h) + na:] if not with_h else rest[:1] + rest[1 + na:]
        for rows in _sub_blocks(tm):
            f = jnp.dot(a_ref[rows, :], w_ref[...], preferred_element_type=F32)
            fh, _ = _rms(f)
            xn = x_ref[rows, :] + alpha * (fh * gp_ref[...])
            if with_h:
                gn_ref, f_ref, xo_ref, h_ref = rest
                xh, _ = _rms(xn)
                h_ref[rows, :] = (xh * gn_ref[...]).astype(h_ref.dtype)
            else:
                f_ref, xo_ref = rest
            f_ref[rows, :] = f
            xo_ref[rows, :] = xn

    ins = [a, w, x, g_post] + ([g_next] if with_h else []) + list(after)
    in_specs = [_rows(tm, kdim), _fixed((kdim, d)), _rows(tm, d), _fixed((1, d))] + ([_fixed((1, d))] if with_h else []) + [ANY] * na
    out_shape = [jax.ShapeDtypeStruct((s, d), F32)] * 2 + ([jax.ShapeDtypeStruct((s, d), BF16)] if with_h else [])
    out = pl.pallas_call(
        body, grid=(s // tm,), in_specs=in_specs, out_specs=[_rows(tm, d)] * len(out_shape), out_shape=out_shape,
        compiler_params=_params(("parallel",)), name=name,
    )(*ins)
    return (out[0], out[1], out[2]) if with_h else (out[0], out[1], None)


def _mm_rms_bwd(pairs, x, g, dres, name, after=(), tm=512):
    s, d = x.shape
    tm = _tile(s, tm)
    n, na = len(pairs), len(after)

    def body(*refs):
        a_refs, w_refs = refs[0:2 * n:2], refs[1:2 * n:2]
        x_ref, g_ref, dres_ref = refs[2 * n:2 * n + 3]
        dx_ref, dg_ref = refs[2 * n + 3 + na:]
        @pl.when(pl.program_id(0) == 0)
        def _():
            dg_ref[...] = jnp.zeros_like(dg_ref)

        for rows in _sub_blocks(tm):
            dy = None
            for a_ref, w_ref in zip(a_refs, w_refs):
                if len(a_ref.shape) == 3:
                    tkb = a_ref.shape[2]
                    parts = [lax.dot_general(a_ref[q, rows, :], w_ref[:, q * tkb:(q + 1) * tkb], (((1,), (1,)), ((), ())),
                                             preferred_element_type=F32) for q in range(a_ref.shape[0])]
                else:
                    parts = [jnp.dot(a_ref[rows, :], w_ref[...], preferred_element_type=F32)]
                for part in parts:
                    dy = part if dy is None else dy + part
            xh, r = _rms(x_ref[rows, :])
            dg_ref[...] += jnp.sum(dy * xh, axis=0, keepdims=True)
            dyg = dy * g_ref[...]
            dx_ref[rows, :] = r * (dyg - xh * jnp.mean(dyg * xh, axis=-1, keepdims=True)) + dres_ref[rows, :]

    ins, in_specs = [], []
    for a_arr, w_arr in pairs:
        ins += [a_arr, w_arr]
        if a_arr.ndim == 3:
            in_specs.append(pl.BlockSpec((a_arr.shape[0], tm, a_arr.shape[2]), lambda i: (0, i, 0)))
        else:
            in_specs.append(_rows(tm, a_arr.shape[1]))
        in_specs.append(pl.BlockSpec(w_arr.shape, lambda i: (0, 0), pipeline_mode=pl.Buffered(1)))
    return pl.pallas_call(
        body, grid=(s // tm,),
        in_specs=in_specs + [_rows(tm, d), _fixed((1, d)), _rows(tm, d)] + [ANY] * na,
        out_specs=[_rows(tm, d), _fixed((1, d))],
        out_shape=[jax.ShapeDtypeStruct((s, d), F32), jax.ShapeDtypeStruct((1, d), F32)],
        compiler_params=_params(("arbitrary",)), name=name,
    )(*ins, x, g, dres, *after)


def _rms_bwd(x, g, dys, dres, alpha, out_dtype, name, after=()):
    s, d = x.shape
    ts = _tile(s, TS_ROW, 8)
    ndy = len(dys)
    with_res = dres is not None

    def body(x_ref, g_ref, *rest):
        dy_refs = rest[:ndy]
        rest = rest[ndy:]
        if with_res:
            dres_ref = rest[0]
        dx_ref, dg_ref = rest[int(with_res) + len(after):]
        xh, r = _rms(x_ref[...])
        dy = dy_refs[0][...].astype(F32)
        for ref in dy_refs[1:]:
            dy = dy + ref[...].astype(F32)
        dy = dy * alpha

        @pl.when(pl.program_id(0) == 0)
        def _():
            dg_ref[...] = jnp.zeros_like(dg_ref)

        dg_ref[...] += jnp.sum(dy * xh, axis=0, keepdims=True)
        dyg = dy * g_ref[...]
        dx = r * (dyg - xh * jnp.mean(dyg * xh, axis=-1, keepdims=True))
        if with_res:
            dx = dx + dres_ref[...]
        dx_ref[...] = dx.astype(dx_ref.dtype)

    ins = [x, g] + list(dys) + ([dres] if with_res else []) + list(after)
    in_specs = [_rows(ts, d), _fixed((1, d))] + [_rows(ts, d)] * (ndy + int(with_res)) + [ANY] * len(after)
    return pl.pallas_call(
        body, grid=(s // ts,), in_specs=in_specs, out_specs=[_rows(ts, d), _fixed((1, d))],
        out_shape=[jax.ShapeDtypeStruct((s, d), out_dtype), jax.ShapeDtypeStruct((1, d), F32)],
        compiler_params=_params(("arbitrary",)), name=name,
    )(*ins)


def _loss_bwd(x, g, target, name):
    s, d = x.shape
    ts = _tile(s, TS_ROW, 8)

    def body(x_ref, g_ref, t_ref, dx_ref, dg_ref, loss_ref):
        xh, r = _rms(x_ref[...])
        gv = g_ref[...]
        diff = xh * gv - t_ref[...]

        @pl.when(pl.program_id(0) == 0)
        def _():
            dg_ref[...] = jnp.zeros_like(dg_ref)
            loss_ref[...] = jnp.zeros_like(loss_ref)

        sq = jnp.sum(diff * diff, axis=1, keepdims=True)
        loss_ref[...] += (0.5 / d) * jnp.sum(sq, axis=0, keepdims=True)
        dy = diff * (1.0 / d)
        dg_ref[...] += jnp.sum(dy * xh, axis=0, keepdims=True)
        dyg = dy * gv
        dx_ref[...] = r * (dyg - xh * jnp.mean(dyg * xh, axis=-1, keepdims=True))

    return pl.pallas_call(
        body, grid=(s // ts,), in_specs=[_rows(ts, d), _fixed((1, d)), _rows(ts, d)],
        out_specs=[_rows(ts, d), _fixed((1, d)), _fixed((8, LANE))],
        out_shape=[jax.ShapeDtypeStruct((s, d), F32), jax.ShapeDtypeStruct((1, d), F32), jax.ShapeDtypeStruct((8, LANE), F32)],
        compiler_params=_params(("arbitrary",)), name=name,
    )(x, g, target)


HALF_FF = DFF // 2


SUB_ROWS = 256


def _sub_blocks(tm):
    sub = SUB_ROWS if tm % SUB_ROWS == 0 else tm
    return [slice(r0, r0 + sub) for r0 in range(0, tm, sub)]


def _ffn_in_swiglu(x_norm, w_in, name, after=(), tm=1024):
    s, d = x_norm.shape
    tm = _tile(s, tm)

    def body(x_ref, wa_ref, wb_ref, *rest):
        ab_ref, u_ref = rest[len(after):]
        for rows in _sub_blocks(tm):
            xv = x_ref[rows, :]
            a = jnp.dot(xv, wa_ref[...], preferred_element_type=F32)
            b = jnp.dot(xv, wb_ref[...], preferred_element_type=F32)
            ab_ref[0, rows, :] = a.astype(ab_ref.dtype)
            ab_ref[1, rows, :] = b.astype(ab_ref.dtype)
            u_ref[rows, :] = (a * _sigmoid(a) * b).astype(u_ref.dtype)

    ab, u = pl.pallas_call(
        body, grid=(s // tm, 2),
        in_specs=[pl.BlockSpec((tm, d), lambda i, j: (i, 0)), pl.BlockSpec((d, HALF_FF), lambda i, j: (0, j)),
                  pl.BlockSpec((d, HALF_FF), lambda i, j: (0, 2 + j))] + [ANY] * len(after),
        out_specs=[pl.BlockSpec((2, None, tm, HALF_FF), lambda i, j: (0, j, i, 0)), pl.BlockSpec((tm, HALF_FF), lambda i, j: (i, j))],
        out_shape=[jax.ShapeDtypeStruct((2, 2, s, HALF_FF), BF16), jax.ShapeDtypeStruct((s, DFF), BF16)],
        compiler_params=_params(("parallel", "parallel")), name=name,
    )(x_norm, w_in, w_in, *after)
    return ab.reshape(4, s, HALF_FF), u


def _ffn_out_dx_swiglu(dz, w_out, ab, after, name, tm=1024):
    s, d = dz.shape
    tm = _tile(s, tm)

    def body(dz_ref, w_ref, ab_ref, *rest):
        dab_ref = rest[len(after)]
        for rows in _sub_blocks(tm):
            du = lax.dot_general(dz_ref[rows, :], w_ref[...], (((1,), (1,)), ((), ())), preferred_element_type=F32)
            a = ab_ref[0, rows, :].astype(F32)
            b = ab_ref[1, rows, :].astype(F32)
            sig = _sigmoid(a)
            dab_ref[0, rows, :] = (du * b * (sig * (1.0 + a * (1.0 - sig)))).astype(dab_ref.dtype)
            dab_ref[1, rows, :] = (du * a * sig).astype(dab_ref.dtype)

    halves = pl.BlockSpec((2, None, tm, HALF_FF), lambda i, j: (0, j, i, 0))
    dab = pl.pallas_call(
        body, grid=(s // tm, 2),
        in_specs=[pl.BlockSpec((tm, d), lambda i, j: (i, 0)), pl.BlockSpec((HALF_FF, d), lambda i, j: (j, 0)), halves] + [ANY] * len(after),
        out_specs=halves, out_shape=jax.ShapeDtypeStruct((2, 2, s, HALF_FF), BF16),
        compiler_params=_params(("parallel", "parallel")), name=name,
    )(dz, w_out, ab.reshape(2, 2, s, HALF_FF), *after)
    return dab.reshape(4, s, HALF_FF)


def _tri(strict):
    r = lax.broadcasted_iota(jnp.int32, (CHUNK, CHUNK), 0)
    c = lax.broadcasted_iota(jnp.int32, (CHUNK, CHUNK), 1)
    return (r > c).astype(F32) if strict else (r >= c).astype(F32)


def _gla_fwd(pg, wfu, b_f, gnorm, name):
    s = pg.shape[0]
    ts = _tile(s, TS_GLA, CHUNK)
    cpb = ts // CHUNK
    nc = s // CHUNK

    def body(pg_ref, wfu_ref, bf_ref, gn_ref, ya_ref, sp_ref, so_ref, o_ref, st_ref, la_ref, dec_ref, u_ref):
        @pl.when(pl.program_id(0) == 0)
        def _():
            st_ref[...] = jnp.zeros_like(st_ref)

        f = jnp.dot(pg_ref[:, PG_F:PG_W], wfu_ref[...], preferred_element_type=F32) + bf_ref[...]
        la_ref[...] = _log_sigmoid(f) * (1.0 / GATE_TEMP)
        tri = _tri(False)
        chunks = [slice(ci * CHUNK, (ci + 1) * CHUNK) for ci in range(cpb)]
        for ci, rows in enumerate(chunks):
            la = la_ref[rows, :]
            b = jnp.dot(tri, la, precision=HIGHEST, preferred_element_type=F32)
            bend = jnp.sum(la, axis=0, keepdims=True)
            e = jnp.exp(bend - b)
            dec_ref[ci:ci + 1, :] = jnp.exp(bend)
            for hd in range(HEADS):
                k = pg_ref[rows, PG_K + hd * HDK:PG_K + (hd + 1) * HDK]
                v = pg_ref[rows, PG_V + hd * HDV:PG_V + (hd + 1) * HDV]
                kt = (k.astype(F32) * e[:, hd * HDK:(hd + 1) * HDK]).astype(BF16)
                u_ref[ci, hd] = lax.dot_general(v, kt, (((0,), (0,)), ((), ())), preferred_element_type=F32)
        for ci in range(cpb):
            for hd in range(HEADS):
                prev = st_ref[hd]
                sp_ref[ci, hd] = prev
                st = prev * dec_ref[ci:ci + 1, hd * HDK:(hd + 1) * HDK] + u_ref[ci, hd]
                st_ref[hd] = st
                so_ref[ci, hd] = st.astype(so_ref.dtype)
        for ci, rows in enumerate(chunks):
            for hd in range(HEADS):
                vc = slice(hd * HDV, (hd + 1) * HDV)
                q = pg_ref[rows, PG_Q + hd * HDK:PG_Q + (hd + 1) * HDK]
                go = pg_ref[rows, PG_G + hd * HDV:PG_G + (hd + 1) * HDV].astype(F32)
                qs = (q.astype(F32) * Q_SCALE).astype(BF16)
                o = lax.dot_general(qs, so_ref[ci, hd], (((1,), (1,)), ((), ())), preferred_element_type=F32)
                o_ref[rows, vc] = o
                oh, _ = _rms(o)
                ya_ref[rows, vc] = (oh * gn_ref[:, vc] * (go * _sigmoid(go))).astype(ya_ref.dtype)

    return pl.pallas_call(
        body, grid=(s // ts,),
        in_specs=[_rows(ts, PG_W), _fixed((LANE, HEADS * HDK)), _fixed((1, HEADS * HDK)), _fixed((1, HEADS * HDV))],
        out_specs=[_rows(ts, HEADS * HDV), pl.BlockSpec((cpb, HEADS, HDV, HDK), lambda i: (i, 0, 0, 0)),
                   pl.BlockSpec((cpb, HEADS, HDV, HDK), lambda i: (i, 0, 0, 0)), _rows(ts, HEADS * HDV)],
        out_shape=[jax.ShapeDtypeStruct((s, HEADS * HDV), BF16), jax.ShapeDtypeStruct((nc, HEADS, HDV, HDK), F32),
                   jax.ShapeDtypeStruct((nc, HEADS, HDV, HDK), BF16), jax.ShapeDtypeStruct((s, HEADS * HDV), F32)],
        scratch_shapes=[pltpu.VMEM((HEADS, HDV, HDK), F32), pltpu.VMEM((ts, HEADS * HDK), F32),
                        pltpu.VMEM((max(cpb, 8), HEADS * HDK), F32), pltpu.VMEM((cpb, HEADS, HDV, HDK), F32)],
        compiler_params=_params(("arbitrary",)), name=name,
    )(pg, wfu, b_f, gnorm)


def _gla_bwd(pg, sp, so, o, dya, wfu, b_f, gnorm, name):
    s = pg.shape[0]
    ts = _tile(s, TS_GLA, CHUNK)
    cpb = ts // CHUNK
    nblk = s // ts

    def body(pg_ref, sp_ref, so_ref, o_ref, dya_ref, wfu_ref, bf_ref, gn_ref, dpg_ref, dwfu_ref, dbf_ref, dgn_ref,
             dst_ref, la_ref, sg_ref, df_ref, e_ref, ktf_ref, dec_ref, g_ref):
        @pl.when(pl.program_id(0) == 0)
        def _():
            dst_ref[...] = jnp.zeros_like(dst_ref)
            dwfu_ref[...] = jnp.zeros_like(dwfu_ref)
            dbf_ref[...] = jnp.zeros_like(dbf_ref)
            dgn_ref[...] = jnp.zeros_like(dgn_ref)

        flow = pg_ref[:, PG_F:PG_W]
        f = jnp.dot(flow, wfu_ref[...], preferred_element_type=F32) + bf_ref[...]
        la_ref[...] = _log_sigmoid(f) * (1.0 / GATE_TEMP)
        sg_ref[...] = _sigmoid(-f) * (1.0 / GATE_TEMP)
        tri = _tri(False)
        tri_strict = _tri(True)
        chunks = [slice(ci * CHUNK, (ci + 1) * CHUNK) for ci in range(cpb)]
        for ci, rows in enumerate(chunks):
            la = la_ref[rows, :]
            b = jnp.dot(tri, la, precision=HIGHEST, preferred_element_type=F32)
            bend = jnp.sum(la, axis=0, keepdims=True)
            e = jnp.exp(bend - b)
            e_ref[rows, :] = e
            dec = jnp.exp(bend)
            dec_ref[ci:ci + 1, :] = dec
            for hd in range(HEADS):
                kc = slice(hd * HDK, (hd + 1) * HDK)
                vc = slice(hd * HDV, (hd + 1) * HDV)
                q = pg_ref[rows, PG_Q + hd * HDK:PG_Q + (hd + 1) * HDK]
                k = pg_ref[rows, PG_K + hd * HDK:PG_K + (hd + 1) * HDK]
                go = pg_ref[rows, PG_G + hd * HDV:PG_G + (hd + 1) * HDV].astype(F32)
                ktf_ref[rows, kc] = k.astype(F32) * e[:, kc]
                st_b = so_ref[ci, hd]
                qs = (q.astype(F32) * Q_SCALE).astype(BF16)
                oh, r = _rms(o_ref[rows, vc])
                gh = gn_ref[:, vc]
                sig = _sigmoid(go)
                dy = dya_ref[rows, vc].astype(F32)
                don = dy * (go * sig)
                dgn_ref[:, vc] += jnp.sum(don * oh, axis=0, keepdims=True)
                dong = don * gh
                do = (r * (dong - oh * jnp.mean(dong * oh, axis=-1, keepdims=True))).astype(BF16)
                g_ref[ci, hd] = lax.dot_general(do, qs, (((0,), (0,)), ((), ())), preferred_element_type=F32)
                dq = jnp.dot(do, st_b, preferred_element_type=F32) * Q_SCALE
                dpg_ref[rows, PG_Q + hd * HDK:PG_Q + (hd + 1) * HDK] = dq.astype(dpg_ref.dtype)
                dgo = dy * (oh * gh) * (sig * (1.0 + go * (1.0 - sig)))
                dpg_ref[rows, PG_G + hd * HDV:PG_G + (hd + 1) * HDV] = dgo.astype(dpg_ref.dtype)
        for ci in reversed(range(cpb)):
            for hd in range(HEADS):
                dst = dst_ref[hd] + g_ref[ci, hd]
                g_ref[ci, hd] = dst
                dst_ref[hd] = dst * dec_ref[ci:ci + 1, hd * HDK:(hd + 1) * HDK]
        for ci, rows in enumerate(chunks):
            for hd in range(HEADS):
                kc = slice(hd * HDK, (hd + 1) * HDK)
                v = pg_ref[rows, PG_V + hd * HDV:PG_V + (hd + 1) * HDV]
                ktf = ktf_ref[rows, kc]
                dst = g_ref[ci, hd]
                dst_b = dst.astype(BF16)
                dkt = jnp.dot(v, dst_b, preferred_element_type=F32)
                dv = lax.dot_general(ktf.astype(BF16), dst_b, (((1,), (1,)), ((), ())), preferred_element_type=F32)
                dd = jnp.sum(dst * sp_ref[ci, hd], axis=0, keepdims=True)
                dla = jnp.dot(tri_strict, dkt * ktf, precision=HIGHEST, preferred_element_type=F32) + dd * dec_ref[ci:ci + 1, kc]
                df_ref[rows, kc] = dla * sg_ref[rows, kc]
                dpg_ref[rows, PG_K + hd * HDK:PG_K + (hd + 1) * HDK] = (dkt * e_ref[rows, kc]).astype(dpg_ref.dtype)
                dpg_ref[rows, PG_V + hd * HDV:PG_V + (hd + 1) * HDV] = dv.astype(dpg_ref.dtype)
        df = df_ref[...]
        df_b = df.astype(BF16)
        dpg_ref[:, PG_F:PG_W] = lax.dot_general(df_b, wfu_ref[...], (((1,), (1,)), ((), ())), preferred_element_type=F32).astype(dpg_ref.dtype)
        dwfu_ref[...] += lax.dot_general(flow, df_b, (((0,), (0,)), ((), ())), preferred_element_type=F32)
        dbf_ref[...] += jnp.sum(df, axis=0, keepdims=True)

    rev = lambda i: (nblk - 1 - i, 0)
    return pl.pallas_call(
        body, grid=(nblk,),
        in_specs=[pl.BlockSpec((ts, PG_W), rev), pl.BlockSpec((cpb, HEADS, HDV, HDK), lambda i: (nblk - 1 - i, 0, 0, 0)),
                  pl.BlockSpec((cpb, HEADS, HDV, HDK), lambda i: (nblk - 1 - i, 0, 0, 0)), pl.BlockSpec((ts, HEADS * HDV), rev),
                  pl.BlockSpec((ts, HEADS * HDV), rev), _fixed((LANE, HEADS * HDK)), _fixed((1, HEADS * HDK)), _fixed((1, HEADS * HDV))],
        out_specs=[pl.BlockSpec((ts, PG_W), rev), _fixed((LANE, HEADS * HDK)), _fixed((1, HEADS * HDK)), _fixed((1, HEADS * HDV))],
        out_shape=[jax.ShapeDtypeStruct((s, PG_W), BF16), jax.ShapeDtypeStruct((LANE, HEADS * HDK), F32),
                   jax.ShapeDtypeStruct((1, HEADS * HDK), F32), jax.ShapeDtypeStruct((1, HEADS * HDV), F32)],
        scratch_shapes=[pltpu.VMEM((HEADS, HDV, HDK), F32)] + [pltpu.VMEM((ts, HEADS * HDK), F32)] * 5
        + [pltpu.VMEM((max(cpb, 8), HEADS * HDK), F32), pltpu.VMEM((cpb, HEADS, HDV, HDK), F32)],
        compiler_params=_params(("arbitrary",)), name=name,
    )(pg, sp, so, o, dya, wfu, b_f, gnorm)


def _window_sums(ext, sign):
    n = ext.shape[0]
    sums = {1: ext}
    w = 1
    while w < POOL_WINDOWS[-1]:
        sums[2 * w] = sums[w] + pltpu.roll(sums[w], w if sign > 0 else n - w, 0)
        w *= 2
    return [sums[POOL_WINDOWS[g]][:, g * LANE:(g + 1) * LANE] for g in range(len(POOL_WINDOWS))]


def _pool_counts(row0, n):
    pos = (row0 + lax.broadcasted_iota(jnp.int32, (n, 1), 0) + 1).astype(F32)
    return [1.0 / jnp.minimum(pos, float(w)) for w in POOL_WINDOWS]


def _pool_fwd(ppx, w_pool, pool_scale, name):
    s = ppx.shape[0]
    ts = _tile(s, TS_POOL, POOL_HALO)
    hb = ts // POOL_HALO
    pw = len(POOL_WINDOWS) * LANE

    def body(p_ref, halo_ref, w_ref, sc_ref, y_ref, ext_ref):
        i = pl.program_id(0)
        p = p_ref[...].astype(F32)
        ext_ref[0:POOL_HALO, :] = jnp.where(i > 0, halo_ref[...].astype(F32), 0.0)
        ext_ref[POOL_HALO:, :] = p
        sums = _window_sums(ext_ref[...], +1)
        cnt = _pool_counts(i * ts, ts)
        for g in range(len(POOL_WINDOWS)):
            cols = slice(g * LANE, (g + 1) * LANE)
            mixed = sums[g][POOL_HALO:, :] * cnt[g] - p[:, cols]
            y = jnp.dot(mixed.astype(BF16), w_ref[g], preferred_element_type=F32)
            y_ref[:, cols] = (y * sc_ref[:, cols]).astype(y_ref.dtype)

    return pl.pallas_call(
        body, grid=(s // ts,),
        in_specs=[pl.BlockSpec((ts, pw), lambda i: (i, 0)), pl.BlockSpec((POOL_HALO, pw), lambda i: (jnp.maximum(i * hb - 1, 0), 0)),
                  _fixed((len(POOL_WINDOWS), LANE, LANE)), _fixed((1, pw))],
        out_specs=_rows(ts, pw), out_shape=jax.ShapeDtypeStruct((s, pw), BF16),
        scratch_shapes=[pltpu.VMEM((ts + POOL_HALO, pw), F32)],
        compiler_params=_params(("parallel",)), name=name,
    )(ppx, ppx, w_pool, pool_scale)


def _pool_bwd(dyb, ppx, w_pool, pool_scale, name):
    s = ppx.shape[0]
    ts = _tile(s, TS_POOL, POOL_HALO)
    hb = ts // POOL_HALO
    nblk = s // ts
    last_halo = s // POOL_HALO - 1
    ng = len(POOL_WINDOWS)
    pw = ng * LANE

    def body(p_ref, halo_ref, dy_ref, dyn_ref, w_ref, sc_ref, dp_ref, dw_ref, dsc_ref, ext_ref, dext_ref, dm_ref):
        i = pl.program_id(0)

        @pl.when(i == 0)
        def _():
            dw_ref[...] = jnp.zeros_like(dw_ref)
            dsc_ref[...] = jnp.zeros_like(dsc_ref)

        p = p_ref[...].astype(F32)
        ext_ref[0:POOL_HALO, :] = jnp.where(i > 0, halo_ref[...].astype(F32), 0.0)
        ext_ref[POOL_HALO:, :] = p
        sums = _window_sums(ext_ref[...], +1)
        cnt = _pool_counts(i * ts, ts + POOL_HALO)
        sc = sc_ref[...]
        dy = dy_ref[...].astype(F32)
        dyn = jnp.where(i < nblk - 1, dyn_ref[...].astype(F32), 0.0)
        for g in range(ng):
            cols = slice(g * LANE, (g + 1) * LANE)
            wg = w_ref[g]
            mixed = (sums[g][POOL_HALO:, :] * cnt[g][0:ts] - p[:, cols]).astype(BF16)
            ypre = jnp.dot(mixed, wg, preferred_element_type=F32)
            dsc_ref[:, cols] += jnp.sum(dy[:, cols] * ypre, axis=0, keepdims=True)
            dyp = (dy[:, cols] * sc[:, cols]).astype(BF16)
            dypn = (dyn[:, cols] * sc[:, cols]).astype(BF16)
            dw_ref[g] += lax.dot_general(mixed, dyp, (((0,), (0,)), ((), ())), preferred_element_type=F32)
            dm = lax.dot_general(dyp, wg, (((1,), (1,)), ((), ())), preferred_element_type=F32)
            dmn = lax.dot_general(dypn, wg, (((1,), (1,)), ((), ())), preferred_element_type=F32)
            dext_ref[0:ts, cols] = dm * cnt[g][0:ts]
            dext_ref[ts:, cols] = dmn * cnt[g][ts:]
            dm_ref[:, cols] = dm
        lead = _window_sums(dext_ref[...], -1)
        for g in range(ng):
            cols = slice(g * LANE, (g + 1) * LANE)
            dp_ref[:, cols] = (lead[g][0:ts, :] - dm_ref[:, cols]).astype(dp_ref.dtype)

    return pl.pallas_call(
        body, grid=(nblk,),
        in_specs=[pl.BlockSpec((ts, pw), lambda i: (i, 0)), pl.BlockSpec((POOL_HALO, pw), lambda i: (jnp.maximum(i * hb - 1, 0), 0)),
                  pl.BlockSpec((ts, pw), lambda i: (i, 0)), pl.BlockSpec((POOL_HALO, pw), lambda i: (jnp.minimum((i + 1) * hb, last_halo), 0)),
                  _fixed((ng, LANE, LANE)), _fixed((1, pw))],
        out_specs=[_rows(ts, pw), _fixed((ng, LANE, LANE)), _fixed((1, pw))],
        out_shape=[jax.ShapeDtypeStruct((s, pw), BF16), jax.ShapeDtypeStruct((ng, LANE, LANE), F32), jax.ShapeDtypeStruct((1, pw), F32)],
        scratch_shapes=[pltpu.VMEM((ts + POOL_HALO, pw), F32), pltpu.VMEM((ts + POOL_HALO, pw), F32), pltpu.VMEM((ts, pw), F32)],
        compiler_params=_params(("arbitrary",)), name=name,
    )(ppx, ppx, dyb, dyb, w_pool, pool_scale)


def _xattn_fwd(ppx, kv, name):
    s = ppx.shape[0]
    m = kv.shape[0]
    ts = _tile(s, TS_XA, 8)
    xw = XA_HEADS * XA_HD

    def body(q_ref, kv_ref, o_ref):
        for hd in range(XA_HEADS):
            cols = slice(hd * XA_HD, (hd + 1) * XA_HD)
            k = kv_ref[:, hd * XA_HD:(hd + 1) * XA_HD]
            v = kv_ref[:, xw + hd * XA_HD:xw + (hd + 1) * XA_HD]
            sc = lax.dot_general(q_ref[:, cols], k, (((1,), (1,)), ((), ())), preferred_element_type=F32) * XA_SCALE
            ex = jnp.exp(sc - jnp.max(sc, axis=-1, keepdims=True))
            pr = ex * (1.0 / jnp.sum(ex, axis=-1, keepdims=True))
            o_ref[:, cols] = jnp.dot(pr.astype(BF16), v, preferred_element_type=F32).astype(o_ref.dtype)

    return pl.pallas_call(
        body, grid=(s // ts,), in_specs=[pl.BlockSpec((ts, xw), lambda i: (i, 1)), _fixed((m, 2 * xw))],
        out_specs=_rows(ts, xw), out_shape=jax.ShapeDtypeStruct((s, xw), BF16),
        compiler_params=_params(("parallel",)), name=name,
    )(ppx, kv)


def _xattn_bwd(dxc, ppx, kv, name):
    s = ppx.shape[0]
    m = kv.shape[0]
    ts = _tile(s, TS_XA, 8)
    xw = XA_HEADS * XA_HD

    def body(do_ref, q_ref, kv_ref, dq_ref, dkv_ref):
        @pl.when(pl.program_id(0) == 0)
        def _():
            dkv_ref[...] = jnp.zeros_like(dkv_ref)

        for hd in range(XA_HEADS):
            cols = slice(hd * XA_HD, (hd + 1) * XA_HD)
            vcols = slice(xw + hd * XA_HD, xw + (hd + 1) * XA_HD)
            q = q_ref[:, cols]
            k = kv_ref[:, cols]
            v = kv_ref[:, vcols]
            do = do_ref[:, cols]
            sc = lax.dot_general(q, k, (((1,), (1,)), ((), ())), preferred_element_type=F32) * XA_SCALE
            ex = jnp.exp(sc - jnp.max(sc, axis=-1, keepdims=True))
            pr = ex * (1.0 / jnp.sum(ex, axis=-1, keepdims=True))
            dpr = lax.dot_general(do, v, (((1,), (1,)), ((), ())), preferred_element_type=F32)
            dsc = (pr * (dpr - jnp.sum(dpr * pr, axis=-1, keepdims=True)) * XA_SCALE).astype(BF16)
            dq_ref[:, cols] = jnp.dot(dsc, k, preferred_element_type=F32).astype(dq_ref.dtype)
            dkv_ref[:, cols] += lax.dot_general(dsc, q, (((0,), (0,)), ((), ())), preferred_element_type=F32)
            dkv_ref[:, vcols] += lax.dot_general(pr.astype(BF16), do, (((0,), (0,)), ((), ())), preferred_element_type=F32)

    return pl.pallas_call(
        body, grid=(s // ts,), in_specs=[_rows(ts, xw), pl.BlockSpec((ts, xw), lambda i: (i, 1)), _fixed((m, 2 * xw))],
        out_specs=[_rows(ts, xw), _fixed((m, 2 * xw))],
        out_shape=[jax.ShapeDtypeStruct((s, xw), BF16), jax.ShapeDtypeStruct((m, 2 * xw), F32)],
        compiler_params=_params(("arbitrary",)), name=name,
    )(dxc, ppx, kv)


def _merge_fwd(pgt, ya, yb, yc, name):
    s = pgt.shape[0]
    ts = _tile(s, TS_ROW, 8)

    def body(gt_ref, ya_ref, yb_ref, yc_ref, o_ref):
        acc = _sigmoid(gt_ref[:, 0:D].astype(F32)) * ya_ref[...].astype(F32)
        acc = acc + _sigmoid(gt_ref[:, D:2 * D].astype(F32)) * yb_ref[...].astype(F32)
        acc = acc + _sigmoid(gt_ref[:, 2 * D:3 * D].astype(F32)) * yc_ref[...].astype(F32)
        o_ref[...] = acc.astype(o_ref.dtype)

    return pl.pallas_call(
        body, grid=(s // ts,), in_specs=[_rows(ts, 3 * D)] + [_rows(ts, D)] * 3, out_specs=_rows(ts, D),
        out_shape=jax.ShapeDtypeStruct((s, D), BF16), compiler_params=_params(("parallel",)), name=name,
    )(pgt, ya, yb, yc)


def _merge_bwd(dmerged, pgt, ya, yb, yc, name):
    s = pgt.shape[0]
    ts = _tile(s, TS_ROW, 8)

    def body(dm_ref, gt_ref, ya_ref, yb_ref, yc_ref, dya_ref, dyb_ref, dyc_ref, dgt_ref):
        dm = dm_ref[...].astype(F32)
        for j, (y_ref, dy_ref) in enumerate(((ya_ref, dya_ref), (yb_ref, dyb_ref), (yc_ref, dyc_ref))):
            sig = _sigmoid(gt_ref[:, j * D:(j + 1) * D].astype(F32))
            dy_ref[...] = (dm * sig).astype(dy_ref.dtype)
            dgt_ref[:, j * D:(j + 1) * D] = (dm * y_ref[...].astype(F32) * sig * (1.0 - sig)).astype(dgt_ref.dtype)

    return pl.pallas_call(
        body, grid=(s // ts,), in_specs=[_rows(ts, D), _rows(ts, 3 * D)] + [_rows(ts, D)] * 3,
        out_specs=[_rows(ts, D)] * 3 + [_rows(ts, 3 * D)],
        out_shape=[jax.ShapeDtypeStruct((s, D), BF16)] * 3 + [jax.ShapeDtypeStruct((s, 3 * D), BF16)],
        compiler_params=_params(("parallel",)), name=name,
    )(dmerged, pgt, ya, yb, yc)


def _adam_math(w, g, m, v):
    mn = ADAM_B1 * m + (1.0 - ADAM_B1) * g
    vn = ADAM_B2 * v + (1.0 - ADAM_B2) * (g * g)
    m_hat = mn / (1.0 - ADAM_B1 ** ADAM_STEP)
    v_hat = vn / (1.0 - ADAM_B2 ** ADAM_STEP)
    return -ADAM_LR * (m_hat / (jnp.sqrt(v_hat) + ADAM_EPS) + ADAM_WD * w), mn, vn


def _adamw(w, g, m, v, name):
    r, c = w.shape[-2:]
    tr, tc = _block_of(r, c, cap=512 if r % 16 == 0 else 256)

    def spec(a):
        if a.ndim == 2:
            return pl.BlockSpec((tr, tc), lambda i, j: (i, j))
        return pl.BlockSpec((None, tr, tc), lambda i, j: (0, i, j))

    def body(w_ref, g_ref, m_ref, v_ref, d_ref, mo_ref, vo_ref):
        d_ref[...], mo_ref[...], vo_ref[...] = _adam_math(w_ref[...], g_ref[...], m_ref[...], v_ref[...])

    return pl.pallas_call(
        body, grid=(r // tr, c // tc), in_specs=[spec(a) for a in (w, g, m, v)], out_specs=[spec(w)] * 3,
        out_shape=[jax.ShapeDtypeStruct(w.shape, F32)] * 3, compiler_params=_params(("parallel", "parallel")), name=name,
    )(w, g, m, v)


ANY = pl.BlockSpec(memory_space=pl.ANY)


def _place():
    x, y, c = lax.axis_index("x"), lax.axis_index("y"), lax.axis_index("c")
    chips = [(1 - x, y), (x, 1 - y), (1 - x, 1 - y)]
    return x, y, c, chips


def _half(c, rows):
    h = rows // 2
    return pl.ds(pl.multiple_of(c * h, 8), h)


def _by_cols(rows):
    return rows % 32 != 0 and rows != 16


def _half_of(ref, lead, c):
    r, cols = ref.shape[-2:]
    if _by_cols(r):
        return ref.at[(*lead, slice(None), pl.ds(pl.multiple_of(c * (cols // 2), LANE), cols // 2))]
    return ref.at[(*lead, pl.ds(pl.multiple_of(c * (r // 2), 8), r // 2))]


def _half_shape(shape):
    r, cols = shape[-2:]
    return shape[:-2] + ((r, cols // 2) if _by_cols(r) else (r // 2, cols))


def _block_of(r, cols, cap=256):
    if r % 16 == 0:
        return _tile(r, cap, 16), cols
    return r, _tile(cols, cap)


def _place_shard(shard, chip_arr, out_dtype, name, after=()):
    _, r, cols = shard.shape
    tr, tc = _block_of(r, cols)

    def body(chip_ref, s_ref, *rest):
        o_ref = rest[len(after)]
        o_ref[...] = s_ref[...].astype(o_ref.dtype)

    return pl.pallas_call(
        body,
        grid_spec=pltpu.PrefetchScalarGridSpec(
            num_scalar_prefetch=1, grid=(r // tr, cols // tc),
            in_specs=[pl.BlockSpec((None, tr, tc), lambda i, j, chip_ref: (0, i, j))] + [ANY] * len(after),
            out_specs=pl.BlockSpec((None, tr, tc), lambda i, j, chip_ref: (chip_ref[0], i, j))),
        out_shape=jax.ShapeDtypeStruct((4, r, cols), out_dtype),
        compiler_params=_params(("parallel", "parallel")), name=name,
    )(chip_arr, shard, *after)


def _gather_shards(bufs, name):
    n = len(bufs)

    def body(*refs):
        outs = refs[n:2 * n]
        send_ici, recv_ici, send_d2d, recv_d2d = refs[2 * n:]
        x, y, c, chips = _place()
        me = 2 * x + y
        sibling = (x, y, 1 - c)

        def ici(w, p, chip_of_block, to):
            rows = _half(c, outs[w].shape[1])
            block = outs[w].at[chip_of_block, rows]
            return pltpu.make_async_remote_copy(
                src_ref=block, dst_ref=block, send_sem=send_ici.at[w, p], recv_sem=recv_ici.at[w, p], device_id=to, device_id_type=MESH)

        def d2d(w, p, chip_of_block, half_of):
            rows = _half(half_of, outs[w].shape[1])
            block = outs[w].at[chip_of_block, rows]
            return pltpu.make_async_remote_copy(
                src_ref=block, dst_ref=block, send_sem=send_d2d.at[w, p], recv_sem=recv_d2d.at[w, p], device_id=sibling, device_id_type=MESH)

        sends = [ici(w, p, me, (*chip, c)) for p, chip in enumerate(chips) for w in range(n)]
        for cp in sends:
            cp.start()
        passed = []
        for p, (px, py) in enumerate(chips):
            for w in range(n):
                ici(w, p, 2 * px + py, (px, py, c)).wait_recv()
                fwd = d2d(w, p, 2 * px + py, c)
                fwd.start()
                passed.append(fwd)
        for p, (px, py) in enumerate(chips):
            for w in range(n):
                d2d(w, p, 2 * px + py, 1 - c).wait_recv()
        for cp in sends + passed:
            cp.wait_send()

    return pl.pallas_call(
        body, in_specs=[ANY] * n, out_specs=[ANY] * n,
        out_shape=[jax.ShapeDtypeStruct(a.shape, a.dtype) for a in bufs],
        input_output_aliases={w: w for w in range(n)},
        scratch_shapes=[pltpu.SemaphoreType.DMA((n, 3))] * 4,
        compiler_params=pltpu.CompilerParams(has_side_effects=True), name=name,
    )(*bufs)


HBM = pl.BlockSpec(memory_space=pltpu.HBM)
SEM = pl.BlockSpec(memory_space=pltpu.SEMAPHORE)
EFFECT = pltpu.SideEffectType.DATAFLOW_SIDE_EFFECTING


def _in_hbm(arrays):
    return [pltpu.with_memory_space_constraint(a, pltpu.HBM) for a in arrays]


def _gather_start(bufs, after, name):
    n, na = len(bufs), len(after)

    def body(*refs):
        send_sem, recv_sem = refs[n + na], refs[n + na + 1]
        outs = refs[n + na + 2:2 * n + na + 2]
        token = refs[2 * n + na + 2]
        x, y, c, chips = _place()
        me = 2 * x + y
        for p, chip in enumerate(chips):
            for w in range(n):
                block = _half_of(outs[w], (me,), c)
                pltpu.make_async_remote_copy(
                    src_ref=block, dst_ref=block, send_sem=send_sem, recv_sem=recv_sem,
                    device_id=(*chip, c), device_id_type=MESH).start()
        token[...] = jnp.zeros_like(token)

    out = pl.pallas_call(
        body, name=name, in_specs=[HBM] * n + [ANY] * na,
        out_specs=[SEM, SEM] + [HBM] * n + [pl.BlockSpec(memory_space=pltpu.VMEM)],
        out_shape=[pltpu.SemaphoreType.DMA(()), pltpu.SemaphoreType.DMA(())]
        + [pltpu.HBM(a.shape, a.dtype) for a in bufs] + [jax.ShapeDtypeStruct((8, LANE), F32)],
        input_output_aliases={w: w + 2 for w in range(n)},
        compiler_params=pltpu.CompilerParams(has_side_effects=EFFECT),
    )(*_in_hbm(bufs), *after)
    return out[0], out[1], list(out[2:2 + n]), out[2 + n]


def _gather_pass(bufs, send_sem, recv_sem, after, name):
    n, na = len(bufs), len(after)

    def body(*refs):
        send1, recv1 = refs[n], refs[n + 1]
        send2, recv2 = refs[n + 2 + na], refs[n + 3 + na]
        outs = refs[n + 4 + na:2 * n + 4 + na]
        x, y, c, chips = _place()
        me = 2 * x + y
        arrivals = [(w, px, py) for px, py in chips for w in range(n)]
        for w, px, py in arrivals:
            first = pltpu.make_async_remote_copy(
                src_ref=_half_of(outs[w], (me,), c), dst_ref=_half_of(outs[w], (2 * px + py,), c), send_sem=send1, recv_sem=recv1,
                device_id=(px, py, c), device_id_type=MESH)
            first.wait_send()
            first.wait_recv()
        for w, px, py in arrivals:
            arrived = _half_of(outs[w], (2 * px + py,), c)
            pltpu.make_async_remote_copy(
                src_ref=arrived, dst_ref=arrived, send_sem=send2, recv_sem=recv2,
                device_id=(x, y, 1 - c), device_id_type=MESH).start()

    out = pl.pallas_call(
        body, name=name, in_specs=[HBM] * n + [SEM, SEM] + [ANY] * na,
        out_specs=[SEM, SEM] + [HBM] * n,
        out_shape=[pltpu.SemaphoreType.DMA(()), pltpu.SemaphoreType.DMA(())] + [pltpu.HBM(a.shape, a.dtype) for a in bufs],
        input_output_aliases={w: w + 2 for w in range(n)},
        compiler_params=pltpu.CompilerParams(has_side_effects=EFFECT),
    )(*bufs, send_sem, recv_sem, *after)
    return out[0], out[1], list(out[2:])


def _gather_finish(bufs, send_sem, recv_sem, after, name):
    n, na = len(bufs), len(after)

    def body(*refs):
        send2, recv2 = refs[n], refs[n + 1]
        outs = refs[n + 2 + na:2 * n + 2 + na]
        x, y, c, chips = _place()
        for p, (px, py) in enumerate(chips):
            for w in range(n):
                passed = pltpu.make_async_remote_copy(
                    src_ref=_half_of(outs[w], (2 * px + py,), c), dst_ref=_half_of(outs[w], (2 * px + py,), 1 - c),
                    send_sem=send2, recv_sem=recv2, device_id=(x, y, 1 - c), device_id_type=MESH)
                passed.wait_send()
                passed.wait_recv()

    out = pl.pallas_call(
        body, name=name, in_specs=[HBM] * n + [SEM, SEM] + [ANY] * na, out_specs=[HBM] * n,
        out_shape=[pltpu.HBM(a.shape, a.dtype) for a in bufs],
        input_output_aliases={w: w for w in range(n)},
        compiler_params=pltpu.CompilerParams(has_side_effects=EFFECT),
    )(*bufs, send_sem, recv_sem, *after)
    return list(out)


def _pair_exchange(grads, name):
    n = len(grads)

    def body(*refs):
        ins, outs = refs[:n], refs[n:2 * n]
        send_sem, recv_sem = refs[2 * n:]
        x, y, c, _ = _place()
        copies = []
        for w in range(n):
            copies.append(pltpu.make_async_remote_copy(
                src_ref=_half_of(ins[w], (slice(None),), 1 - c), dst_ref=outs[w], send_sem=send_sem.at[w], recv_sem=recv_sem.at[w],
                device_id=(x, y, 1 - c), device_id_type=MESH))
        for cp in copies:
            cp.start()
        for cp in copies:
            cp.wait()

    return pl.pallas_call(
        body, in_specs=[ANY] * n, out_specs=[ANY] * n,
        out_shape=[jax.ShapeDtypeStruct(_half_shape(a.shape), a.dtype) for a in grads],
        scratch_shapes=[pltpu.SemaphoreType.DMA((n,))] * 2,
        compiler_params=pltpu.CompilerParams(has_side_effects=True), name=name,
    )(*grads)


def _pair_exchange_start(grads, after, name):
    n, na = len(grads), len(after)
    lands = [lax.empty(_half_shape(a.shape), a.dtype) for a in grads]

    def body(*refs):
        send_sem, recv_sem = refs[2 * n + na], refs[2 * n + na + 1]
        srcs = refs[2 * n + na + 2:3 * n + na + 2]
        dsts = refs[3 * n + na + 2:4 * n + na + 2]
        token = refs[4 * n + na + 2]
        x, y, c, _ = _place()
        for w in range(n):
            pltpu.make_async_remote_copy(
                src_ref=_half_of(srcs[w], (slice(None),), 1 - c), dst_ref=dsts[w], send_sem=send_sem, recv_sem=recv_sem,
                device_id=(x, y, 1 - c), device_id_type=MESH).start()
        token[...] = jnp.zeros_like(token)

    out = pl.pallas_call(
        body, name=name, in_specs=[HBM] * (2 * n) + [ANY] * na,
        out_specs=[SEM, SEM] + [HBM] * (2 * n) + [pl.BlockSpec(memory_space=pltpu.VMEM)],
        out_shape=[pltpu.SemaphoreType.DMA(()), pltpu.SemaphoreType.DMA(())]
        + [pltpu.HBM(a.shape, a.dtype) for a in grads + lands] + [jax.ShapeDtypeStruct((8, LANE), F32)],
        input_output_aliases={w: w + 2 for w in range(2 * n)},
        compiler_params=pltpu.CompilerParams(has_side_effects=EFFECT),
    )(*_in_hbm(grads), *_in_hbm(lands), *after)
    return out[0], out[1], list(out[2:2 + n]), list(out[2 + n:2 + 2 * n]), out[2 + 2 * n]


def _pair_exchange_finish(grads, lands, send_sem, recv_sem, after, name):
    n, na = len(grads), len(after)

    def body(*refs):
        send, recv = refs[2 * n], refs[2 * n + 1]
        srcs = refs[2 * n + 2 + na:3 * n + 2 + na]
        dsts = refs[3 * n + 2 + na:4 * n + 2 + na]
        x, y, c, _ = _place()
        for w in range(n):
            copy = pltpu.make_async_remote_copy(
                src_ref=_half_of(srcs[w], (slice(None),), 1 - c), dst_ref=dsts[w], send_sem=send, recv_sem=recv,
                device_id=(x, y, 1 - c), device_id_type=MESH)
            copy.wait_send()
            copy.wait_recv()

    out = pl.pallas_call(
        body, name=name, in_specs=[HBM] * (2 * n) + [SEM, SEM] + [ANY] * na, out_specs=[HBM] * (2 * n),
        out_shape=[pltpu.HBM(a.shape, a.dtype) for a in grads + lands],
        input_output_aliases={w: w for w in range(2 * n)},
        compiler_params=pltpu.CompilerParams(has_side_effects=EFFECT),
    )(*grads, *lands, send_sem, recv_sem, *after)
    return list(out[:n]), list(out[n:])


def _pair_sum(g, got, c_arr, name):
    _, r, cols = g.shape
    hr, hc = _half_shape((r, cols))
    tr, tc = _block_of(hr, hc)
    nbr, nbc = hr // tr, hc // tc
    by_cols = _by_cols(r)

    def body(c_ref, g_ref, got_ref, o_ref):
        o_ref[...] = (g_ref[...].astype(F32) + got_ref[...].astype(F32)).astype(o_ref.dtype)

    def mine(j, i, k, c_ref):
        return (j, i, c_ref[0] * nbc + k) if by_cols else (j, c_ref[0] * nbr + i, k)

    return pl.pallas_call(
        body,
        grid_spec=pltpu.PrefetchScalarGridSpec(
            num_scalar_prefetch=1, grid=(4, nbr, nbc),
            in_specs=[pl.BlockSpec((None, tr, tc), mine),
                      pl.BlockSpec((None, tr, tc), lambda j, i, k, c_ref: (j, i, k))],
            out_specs=pl.BlockSpec((None, tr, tc), lambda j, i, k, c_ref: (j, i, k))),
        out_shape=jax.ShapeDtypeStruct((4, hr, hc), BF16),
        compiler_params=_params(("parallel", "parallel", "parallel")), name=name,
    )(c_arr, g, got)


def _chip_exchange(parts, name):
    n = len(parts)

    def body(*refs):
        ins, outs = refs[:n], refs[n:2 * n]
        send_sem, recv_sem = refs[2 * n:]
        x, y, c, chips = _place()
        copies = []
        for p, (px, py) in enumerate(chips):
            for w in range(n):
                copies.append(pltpu.make_async_remote_copy(
                    src_ref=ins[w].at[2 * px + py], dst_ref=outs[w].at[p], send_sem=send_sem.at[w, p], recv_sem=recv_sem.at[w, p],
                    device_id=(px, py, c), device_id_type=MESH))
        for cp in copies:
            cp.start()
        for cp in copies:
            cp.wait()

    return pl.pallas_call(
        body, in_specs=[ANY] * n, out_specs=[ANY] * n,
        out_shape=[jax.ShapeDtypeStruct((3,) + a.shape[1:], a.dtype) for a in parts],
        scratch_shapes=[pltpu.SemaphoreType.DMA((n, 3))] * 2,
        compiler_params=pltpu.CompilerParams(has_side_effects=True), name=name,
    )(*parts)


def _chip_exchange_start(parts, after, name):
    n, na = len(parts), len(after)
    lands = [lax.empty((3,) + a.shape[1:], a.dtype) for a in parts]

    def body(*refs):
        send_sem, recv_sem = refs[2 * n + na], refs[2 * n + na + 1]
        srcs = refs[2 * n + na + 2:3 * n + na + 2]
        dsts = refs[3 * n + na + 2:4 * n + na + 2]
        token = refs[4 * n + na + 2]
        x, y, c, chips = _place()
        for p, (px, py) in enumerate(chips):
            for w in range(n):
                pltpu.make_async_remote_copy(
                    src_ref=srcs[w].at[2 * px + py], dst_ref=dsts[w].at[p], send_sem=send_sem, recv_sem=recv_sem,
                    device_id=(px, py, c), device_id_type=MESH).start()
        token[...] = jnp.zeros_like(token)

    out = pl.pallas_call(
        body, name=name, in_specs=[HBM] * (2 * n) + [ANY] * na,
        out_specs=[SEM, SEM] + [HBM] * (2 * n) + [pl.BlockSpec(memory_space=pltpu.VMEM)],
        out_shape=[pltpu.SemaphoreType.DMA(()), pltpu.SemaphoreType.DMA(())]
        + [pltpu.HBM(a.shape, a.dtype) for a in parts + lands] + [jax.ShapeDtypeStruct((8, LANE), F32)],
        input_output_aliases={w: w + 2 for w in range(2 * n)},
        compiler_params=pltpu.CompilerParams(has_side_effects=EFFECT),
    )(*_in_hbm(parts), *_in_hbm(lands), *after)
    return out[0], out[1], list(out[2:2 + n]), list(out[2 + n:2 + 2 * n]), out[2 + 2 * n]


def _chip_exchange_finish(parts, lands, send_sem, recv_sem, after, name):
    n, na = len(parts), len(after)

    def body(*refs):
        send, recv = refs[2 * n], refs[2 * n + 1]
        srcs = refs[2 * n + 2 + na:3 * n + 2 + na]
        dsts = refs[3 * n + 2 + na:4 * n + 2 + na]
        x, y, c, chips = _place()
        for p, (px, py) in enumerate(chips):
            for w in range(n):
                copy = pltpu.make_async_remote_copy(
                    src_ref=srcs[w].at[2 * px + py], dst_ref=dsts[w].at[p], send_sem=send, recv_sem=recv,
                    device_id=(px, py, c), device_id_type=MESH)
                copy.wait_send()
                copy.wait_recv()

    out = pl.pallas_call(
        body, name=name, in_specs=[HBM] * (2 * n) + [SEM, SEM] + [ANY] * na, out_specs=[HBM] * (2 * n),
        out_shape=[pltpu.HBM(a.shape, a.dtype) for a in parts + lands],
        input_output_aliases={w: w for w in range(2 * n)},
        compiler_params=pltpu.CompilerParams(has_side_effects=EFFECT),
    )(*parts, *lands, send_sem, recv_sem, *after)
    return list(out[:n]), list(out[n:])


def _chip_sum(part, got, place_arr, name):
    _, hr, hc = part.shape
    by_cols = _by_cols(hr)
    tr, tc = _block_of(hr, hc)
    nbr, nbc = hr // tr, hc // tc

    def body(place_ref, p_ref, got_ref, o_ref):
        acc = p_ref[...].astype(F32)
        for p in range(3):
            acc = acc + got_ref[p].astype(F32)
        o_ref[...] = acc

    def mine(i, k, place_ref):
        return (i, place_ref[1] * nbc + k) if by_cols else (place_ref[1] * nbr + i, k)

    return pl.pallas_call(
        body,
        grid_spec=pltpu.PrefetchScalarGridSpec(
            num_scalar_prefetch=1, grid=(nbr, nbc),
            in_specs=[pl.BlockSpec((None, tr, tc), lambda i, k, place_ref: (place_ref[0], i, k)),
                      pl.BlockSpec((3, tr, tc), lambda i, k, place_ref: (0, i, k))],
            out_specs=pl.BlockSpec((tr, tc), mine)),
        out_shape=jax.ShapeDtypeStruct((hr, 2 * hc) if by_cols else (2 * hr, hc), F32),
        compiler_params=_params(("parallel", "parallel")), name=name,
    )(place_arr, part, got)


def _pair_join_start(bufs, name):
    n = len(bufs)

    def body(*refs):
        send_sem, recv_sem = refs[n], refs[n + 1]
        outs = refs[n + 2:2 * n + 2]
        token = refs[2 * n + 2]
        x, y, c, _ = _place()
        for w in range(n):
            block = _half_of(outs[w], (), c)
            pltpu.make_async_remote_copy(
                src_ref=block, dst_ref=block, send_sem=send_sem, recv_sem=recv_sem,
                device_id=(x, y, 1 - c), device_id_type=MESH).start()
        token[...] = jnp.zeros_like(token)

    out = pl.pallas_call(
        body, name=name, in_specs=[HBM] * n,
        out_specs=[SEM, SEM] + [HBM] * n + [pl.BlockSpec(memory_space=pltpu.VMEM)],
        out_shape=[pltpu.SemaphoreType.DMA(()), pltpu.SemaphoreType.DMA(())]
        + [pltpu.HBM(a.shape, a.dtype) for a in bufs] + [jax.ShapeDtypeStruct((8, LANE), F32)],
        input_output_aliases={w: w + 2 for w in range(n)},
        compiler_params=pltpu.CompilerParams(has_side_effects=EFFECT),
    )(*_in_hbm(bufs))
    return out[0], out[1], list(out[2:2 + n]), out[2 + n]


def _pair_join_finish(bufs, send_sem, recv_sem, after, name):
    n, na = len(bufs), len(after)

    def body(*refs):
        send, recv = refs[n], refs[n + 1]
        outs = refs[n + 2 + na:2 * n + 2 + na]
        x, y, c, _ = _place()
        for w in range(n):
            copy = pltpu.make_async_remote_copy(
                src_ref=_half_of(outs[w], (), c), dst_ref=_half_of(outs[w], (), 1 - c), send_sem=send, recv_sem=recv,
                device_id=(x, y, 1 - c), device_id_type=MESH)
            copy.wait_send()
            copy.wait_recv()

    out = pl.pallas_call(
        body, name=name, in_specs=[HBM] * n + [SEM, SEM] + [ANY] * na, out_specs=[HBM] * n,
        out_shape=[pltpu.HBM(a.shape, a.dtype) for a in bufs],
        input_output_aliases={w: w for w in range(n)},
        compiler_params=pltpu.CompilerParams(has_side_effects=EFFECT),
    )(*bufs, send_sem, recv_sem, *after)
    return list(out)


SMALL = ("ffn1_pre_g", "ffn1_post_g", "mix_pre_g", "gla_norm_g", "mem_norm_g", "mix_post_g", "ffn2_pre_g", "ffn2_post_g", "final_g",
         "b_f", "pool_scale", "w_pool", "w_fu")
N_GAINS = 9
SMALL_PACKS = ((16, D), (24, 512), (4 * LANE, LANE))
W_FU_ROW = 8


def _all_sum_small(gs, name, after=()):
    ins = [gs[n] for n in SMALL[:N_GAINS]] + [gs["b_f"], gs["pool_scale"], gs["w_fu_pad"], gs["w_pool"].reshape(4 * LANE, LANE)]

    def body(*refs):
        gain_refs = refs[:N_GAINS]
        bf_ref, ps_ref, wfu_ref, wp_ref = refs[N_GAINS:N_GAINS + 4]
        outs = refs[N_GAINS + 4 + len(after):N_GAINS + 7 + len(after)]
        mine_a, mine_b, all_a, all_b, all_c, send_sems, recv_sems = refs[N_GAINS + 7 + len(after):]
        mine_a[...] = jnp.zeros_like(mine_a)
        for i, ref in enumerate(gain_refs):
            mine_a[i:i + 1, :] = ref[...]
        mine_b[...] = jnp.zeros_like(mine_b)
        mine_b[0:1, :] = bf_ref[...]
        mine_b[1:2, :] = ps_ref[...]
        mine_b[W_FU_ROW:W_FU_ROW + GATE_RANK, :] = wfu_ref[0:GATE_RANK, :]
        packs = ((mine_a, all_a), (mine_b, all_b), (wp_ref, all_c))
        x, y, c, chips = _place()
        me, sibling = (x, y, c), (x, y, 1 - c)

        def copy(t, k, block, to, own=False):
            px, py, pc = block
            slot = packs[t][1].at[4 * px + 2 * py + pc]
            return pltpu.make_async_remote_copy(
                src_ref=packs[t][0] if own else slot, dst_ref=slot,
                send_sem=send_sems.at[t, k], recv_sem=recv_sems.at[t, k], device_id=to, device_id_type=MESH)

        started = []
        for t, (mine, everyone) in enumerate(packs):
            everyone[4 * x + 2 * y + c] = mine[...]
            started.append(copy(t, 0, me, sibling, own=True))
            started += [copy(t, 1 + j, me, (*chip, c), own=True) for j, chip in enumerate(chips)]
        for cp in started:
            cp.start()
        passed = []
        for j, chip in enumerate(chips):
            for t in range(len(packs)):
                copy(t, 1 + j, (*chip, c), me).wait_recv()
                fwd = copy(t, 4 + j, (*chip, c), sibling)
                fwd.start()
                passed.append(fwd)
        for t in range(len(packs)):
            copy(t, 0, sibling, me).wait_recv()
            for j, chip in enumerate(chips):
                copy(t, 4 + j, (*chip, 1 - c), me).wait_recv()
        for cp in started + passed:
            cp.wait_send()
        for (_, everyone), o_ref in zip(packs, outs):
            acc = everyone[0]
            for k in range(1, 8):
                acc = acc + everyone[k]
            o_ref[...] = acc

    vmem = pl.BlockSpec(memory_space=pltpu.VMEM)
    return pl.pallas_call(
        body, in_specs=[vmem] * len(ins) + [ANY] * len(after), out_specs=[vmem] * 3,
        out_shape=[jax.ShapeDtypeStruct(shape, F32) for shape in SMALL_PACKS],
        scratch_shapes=[pltpu.VMEM(SMALL_PACKS[0], F32), pltpu.VMEM(SMALL_PACKS[1], F32)]
        + [pltpu.VMEM((8,) + shape, F32) for shape in SMALL_PACKS]
        + [pltpu.SemaphoreType.DMA((3, 7)), pltpu.SemaphoreType.DMA((3, 7))],
        compiler_params=pltpu.CompilerParams(has_side_effects=True, vmem_limit_bytes=VMEM_LIMIT), name=name,
    )(*ins, *after)


def _adamw_small(sums, params, chip_arr, name):
    flat = [a for n in SMALL for a in params[n]]

    def body(chip_ref, a_ref, b_ref, c_ref, *refs):
        ins, outs = refs[:len(flat)], refs[len(flat):]
        for i, n in enumerate(SMALL):
            w_ref, m_ref, v_ref = ins[3 * i:3 * i + 3]
            g_ref, d_ref, mo_ref, vo_ref = outs[4 * i:4 * i + 4]
            if n == "w_pool":
                pieces = [((0, k), c_ref[k * LANE:(k + 1) * LANE, :]) for k in range(4)]
            elif n == "w_fu":
                mine = pl.ds(pl.multiple_of(chip_ref[0] * LANE, LANE), LANE)
                pieces = [((0,), b_ref[W_FU_ROW:W_FU_ROW + GATE_RANK, mine])]
            elif n == "b_f":
                pieces = [((), b_ref[0:1, :])]
            elif n == "pool_scale":
                pieces = [((), b_ref[1:2, :])]
            else:
                pieces = [((), a_ref[i:i + 1, :])]
            for at, g in pieces:
                d, mn, vn = _adam_math(w_ref[at], g, m_ref[at], v_ref[at])
                g_ref[at] = g
                d_ref[at] = d
                mo_ref[at] = mn
                vo_ref[at] = vn

    def whole(shape):
        return pl.BlockSpec(shape, lambda i, chip_ref: (0,) * len(shape))

    out = pl.pallas_call(
        body,
        grid_spec=pltpu.PrefetchScalarGridSpec(
            num_scalar_prefetch=1, grid=(1,),
            in_specs=[whole(a.shape) for a in list(sums) + flat],
            out_specs=[whole(params[n][0].shape) for n in SMALL for _ in range(4)]),
        out_shape=[jax.ShapeDtypeStruct(params[n][0].shape, F32) for n in SMALL for _ in range(4)],
        compiler_params=_params(("arbitrary",)), name=name,
    )(chip_arr, *sums, *flat)
    return {n: tuple(out[4 * i:4 * i + 4]) for i, n in enumerate(SMALL)}


def _ffn_bwd(dz, x_norm, ab, u, w_in, w_out, x, g_pre, dres, tag, emit, advance, after=()):
    dw_out = _mm(u, dz, ta=True, out_dtype=BF16, tm=1408, tk=2048, after=after, name=tag + "_out_dw")
    behind = emit(tag + "_w_out", dw_out)
    dab = _ffn_out_dx_swiglu(dz, w_out, ab, behind, name=tag + "_out_dx")
    behind = advance((dab,))
    dw_in = _mm(x_norm, dab, ta=True, out_dtype=BF16, tm=512, tk=4096, shards=4, after=behind, name=tag + "_in_dw")
    behind = emit(tag + "_w_in", dw_in)
    dx, dg = _mm_rms_bwd([(dab, w_in)], x, g_pre, dres, after=behind, name=tag + "_in_dx")
    return dx, dg, advance((dx,))


def _local_step(x, mem, target, small, gather, emit, advance):
    behind = gather("start", "ffn1i", ())
    behind = gather("start", "ffn1o", behind)
    h1 = _norm_fwd(x, small["ffn1_pre_g"], BF16, name="ffn1_pre", after=behind)
    gather("pass", "ffn1i", (h1,))
    big = gather("finish", "ffn1i", ())
    behind = gather("start", "mixa", (big["ffn1_w_in"],))
    behind = gather("start", "mixb", behind)
    ab1, u1 = _ffn_in_swiglu(h1, big["ffn1_w_in"], name="ffn1_in", after=behind)
    gather("pass", "ffn1o", (ab1,))
    big.update(gather("finish", "ffn1o", ()))
    behind = gather("pass", "mixa", (u1,))
    f1, x1, h = _mm_resid_norm(u1, big["ffn1_w_out"], x, small["ffn1_post_g"], 0.5, small["mix_pre_g"], name="ffn1_out", after=behind)
    big.update(gather("finish", "mixa", (h,)))
    small = dict(small, w_fu_pad=big["w_fu_pad"])
    pg = _mm(h, big["w_gla_t"], tb=True, out_dtype=BF16, tm=1024, tn=PG_W, name="mix_in_gla")
    behind = gather("pass", "mixb", (pg,))
    ppx = _mm(h, big["w_px_t"], tb=True, out_dtype=BF16, after=behind, name="mix_in_px")
    pgt = _mm(h, big["w_gates_t"], tb=True, out_dtype=BF16, tn=1536, name="mix_in_gates")
    big.update(gather("finish", "mixb", (pgt,)))
    behind = gather("start", "ffn2", (big["w_o"],))
    mem_n = _norm_fwd(mem, small["mem_norm_g"], BF16, name="mem_norm", after=behind)
    kv = _mm(mem_n, big["w_mem_kv"], out_dtype=BF16, name="mem_kv")
    ya_in, sp, so, o_gla = _gla_fwd(pg, small["w_fu_pad"], small["b_f"], small["gla_norm_g"], name="gla_fwd")
    yb_in = _pool_fwd(ppx, small["w_pool_b"], small["pool_scale"], name="pool_fwd")
    xc = _xattn_fwd(ppx, kv, name="xattn_fwd")
    ya = _mm(ya_in, big["w_up_gla"], out_dtype=BF16, name="up_gla")
    yb = _mm(yb_in, big["w_up_pool"], out_dtype=BF16, name="up_pool")
    yc = _mm(xc, big["w_up_xattn"], out_dtype=BF16, name="up_xattn")
    merged = _merge_fwd(pgt, ya, yb, yc, name="merge_fwd")
    behind = gather("pass", "ffn2", (merged,))
    ymix, x2, h2 = _mm_resid_norm(merged, big["w_o"], x1, small["mix_post_g"], 1.0, small["ffn2_pre_g"], name="mix_out", after=behind)
    big.update(gather("finish", "ffn2", (h2,)))
    ab2, u2 = _ffn_in_swiglu(h2, big["ffn2_w_in"], name="ffn2_in")
    f2, x3, _ = _mm_resid_norm(u2, big["ffn2_w_out"], x2, small["ffn2_post_g"], 0.5, None, name="ffn2_out")
    gs = {}
    dx3, gs["final_g"], loss = _loss_bwd(x3, small["final_g"], target, name="loss")
    dz2, gs["ffn2_post_g"] = _rms_bwd(f2, small["ffn2_post_g"], [dx3], None, 0.5, BF16, name="ffn2_post_bwd")
    dx2, gs["ffn2_pre_g"], behind = _ffn_bwd(dz2, h2, ab2, u2, big["ffn2_w_in"], big["ffn2_w_out"], x2, small["ffn2_pre_g"], dx3,
                                             "ffn2", emit, advance)
    dy, gs["mix_post_g"] = _rms_bwd(ymix, small["mix_post_g"], [dx2], None, 1.0, BF16, name="mix_post_bwd", after=behind)
    dmerged = _mm(dy, big["w_o"], tb=True, out_dtype=BF16, name="mix_out_dx")
    emit("w_o", _mm(merged, dy, ta=True, out_dtype=BF16, tm=512, tk=4096, name="mix_out_dw"))
    dya, dyb, dyc, dgt = _merge_bwd(dmerged, pgt, ya, yb, yc, name="merge_bwd")
    dya_in = _mm(dya, big["w_up_gla"], tb=True, out_dtype=BF16, name="up_gla_dx")
    emit("w_up_gla", _mm(ya_in, dya, ta=True, out_dtype=BF16, tm=512, tk=4096, name="up_gla_dw"))
    dyb_in = _mm(dyb, big["w_up_pool"], tb=True, out_dtype=BF16, name="up_pool_dx")
    emit("w_up_pool", _mm(yb_in, dyb, ta=True, out_dtype=BF16, tm=512, tk=4096, shards=4, name="up_pool_dw"))
    dxc = _mm(dyc, big["w_up_xattn"], tb=True, out_dtype=BF16, name="up_xattn_dx")
    emit("w_up_xattn", _mm(xc, dyc, ta=True, out_dtype=BF16, tm=512, tk=4096, shards=4, name="up_xattn_dw"))
    dpg, gs["w_fu_pad"], gs["b_f"], gs["gla_norm_g"] = _gla_bwd(pg, sp, so, o_gla, dya_in, small["w_fu_pad"], small["b_f"], small["gla_norm_g"], name="gla_bwd")
    dp, gs["w_pool"], gs["pool_scale"] = _pool_bwd(dyb_in, ppx, small["w_pool_b"], small["pool_scale"], name="pool_bwd")
    dxq, dkv = _xattn_bwd(dxc, ppx, kv, name="xattn_bwd")
    dkv = dkv.astype(BF16)
    emit("w_mem_kv", _mm(mem_n, dkv, ta=True, out_dtype=BF16, name="mem_kv_dw"))
    dmem_n = _mm(dkv, big["w_mem_kv"], tb=True, name="mem_kv_dx")
    _, gs["mem_norm_g"] = _rms_bwd(mem, small["mem_norm_g"], [dmem_n], None, 1.0, BF16, name="mem_norm_bwd")
    emit("w_gla", _mm(dpg, h, ta=True, out_dtype=BF16, tm=640, tk=4096, name="mix_in_gla_dw"))
    emit("w_p", _mm(dp, h, ta=True, out_dtype=BF16, tm=512, tk=4096, name="mix_in_p_dw"))
    emit("w_xq", _mm(dxq, h, ta=True, out_dtype=BF16, tm=512, tk=4096, name="mix_in_xq_dw"))
    behind = emit("w_gates", _mm(dgt, h, ta=True, out_dtype=BF16, tm=512, tk=4096, name="mix_in_gates_dw"))
    pairs = [(dpg, big["w_gla_t"]), (dp, big["w_p_t"]), (dxq, big["w_xq_t"]), (dgt, big["w_gates_t"])]
    dx1, gs["mix_pre_g"] = _mm_rms_bwd(pairs, x1, small["mix_pre_g"], dx2, after=behind, name="mix_in_dx")
    behind = advance((dx1,))
    dz1, gs["ffn1_post_g"] = _rms_bwd(f1, small["ffn1_post_g"], [dx1], None, 0.5, BF16, name="ffn1_post_bwd", after=behind)
    dx0, gs["ffn1_pre_g"], _ = _ffn_bwd(dz1, h1, ab1, u1, big["ffn1_w_in"], big["ffn1_w_out"], x, small["ffn1_pre_g"], dx1,
                                        "ffn1", emit, advance)
    return loss, dx0, gs


BIG = ("ffn1_w_in", "ffn1_w_out", "w_in", "w_mem_kv", "w_up_gla", "w_up_pool", "w_up_xattn", "w_o", "ffn2_w_in", "ffn2_w_out")
COL_SHARDED = ("ffn1_w_in", "w_in", "w_up_pool", "w_up_xattn", "ffn2_w_in")
GATHER_GROUPS = {"ffn1i": ("ffn1_w_in",), "ffn1o": ("ffn1_w_out",), "mixa": ("w_in", "w_fu"),
                 "mixb": ("w_mem_kv", "w_up_gla", "w_up_pool", "w_up_xattn", "w_o"), "ffn2": ("ffn2_w_in", "ffn2_w_out")}
REDUCE_GROUPS = {"ffn2": ("ffn2_w_out", "ffn2_w_in"),
                 "mix": ("w_o", "w_up_gla", "w_up_pool", "w_up_xattn", "w_mem_kv", "w_gla", "w_p", "w_xq", "w_gates"),
                 "ffn1_out": ("ffn1_w_out",),
                 "ffn1_in": ("ffn1_w_in",)}
REDUCE_LAST = "ffn1_in"
GAINS = ("ffn1_pre_g", "ffn1_post_g", "mix_pre_g", "gla_norm_g", "mem_norm_g", "mix_post_g", "ffn2_pre_g", "ffn2_post_g", "final_g")
WEIGHTS = ("ffn1_pre_g", "ffn1_w_in", "ffn1_w_out", "ffn1_post_g", "mix_pre_g", "w_in", "w_fu", "b_f", "gla_norm_g", "w_pool",
           "pool_scale", "mem_norm_g", "w_mem_kv", "w_up_gla", "w_up_pool", "w_up_xattn", "w_o", "mix_post_g", "ffn2_pre_g",
           "ffn2_w_in", "ffn2_w_out", "ffn2_post_g", "final_g")
IN_GLA, IN_F, IN_PX, IN_GATES, IN_END = 0, 3072, 3088, 4112, 7184
def _cols_from_shards(g):
    return jnp.transpose(g, (1, 0, 2)).reshape(g.shape[1], 4 * g.shape[2])


def kernel(x, mem, ffn1_pre_g, ffn1_w_in, ffn1_w_out, ffn1_post_g, mix_pre_g, w_in, w_fu, b_f, gla_norm_g, w_pool, pool_scale, mem_norm_g, w_mem_kv, w_up_gla, w_up_pool, w_up_xattn, w_o, mix_post_g, ffn2_pre_g, ffn2_w_in, ffn2_w_out, ffn2_post_g, final_g, loss_target, m_ffn1_pre_g, m_ffn1_w_in, m_ffn1_w_out, m_ffn1_post_g, m_mix_pre_g, m_w_in, m_w_fu, m_b_f, m_gla_norm_g, m_w_pool, m_pool_scale, m_mem_norm_g, m_w_mem_kv, m_w_up_gla, m_w_up_pool, m_w_up_xattn, m_w_o, m_mix_post_g, m_ffn2_pre_g, m_ffn2_w_in, m_ffn2_w_out, m_ffn2_post_g, m_final_g, v_ffn1_pre_g, v_ffn1_w_in, v_ffn1_w_out, v_ffn1_post_g, v_mix_pre_g, v_w_in, v_w_fu, v_b_f, v_gla_norm_g, v_w_pool, v_pool_scale, v_mem_norm_g, v_w_mem_kv, v_w_up_gla, v_w_up_pool, v_w_up_xattn, v_w_o, v_mix_post_g, v_ffn2_pre_g, v_ffn2_w_in, v_ffn2_w_out, v_ffn2_post_g, v_final_g):
    args = dict(locals())
    w = {n: args[n][0] for n in WEIGHTS}
    m = {n: args["m_" + n][0] for n in WEIGHTS}
    v = {n: args["v_" + n][0] for n in WEIGHTS}
    xi, yi, ci = lax.axis_index("x"), lax.axis_index("y"), lax.axis_index("c")
    chip = 2 * xi + yi

    c_arr = jnp.reshape(ci, (1,)).astype(jnp.int32)
    chip_arr = jnp.reshape(chip, (1,)).astype(jnp.int32)
    place_arr = jnp.stack([chip, ci]).astype(jnp.int32)
    shard_of = {n: (jnp.transpose(args[n][0])[None] if n == "w_in" else args[n]) for n in BIG}
    shard_of["w_fu"] = args["w_fu"]
    placed, inflight = {}, {}

    def place(names, after):
        for n in names:
            if n not in placed:
                placed[n] = _place_shard(shard_of[n], chip_arr, F32 if n == "w_fu" else BF16, name="place_" + n, after=after)

    def relayout(names, gathered):
        out = {}
        for n, g in zip(names, gathered):
            if n == "w_fu":
                w_fu_full = _cols_from_shards(g)
                out["w_fu_pad"] = jnp.concatenate([w_fu_full, jnp.zeros((LANE - GATE_RANK, 512), F32)], axis=0).astype(BF16)
            elif n == "w_in":
                wt = g.reshape(IN_END, D)
                out["w_gla_t"] = jnp.concatenate([wt[IN_GLA:IN_PX], jnp.zeros((PG_W - IN_PX, D), BF16)], axis=0)
                out["w_px_t"] = wt[IN_PX:IN_GATES]
                out["w_p_t"] = wt[IN_PX:IN_PX + 512]
                out["w_xq_t"] = wt[IN_PX + 512:IN_GATES]
                out["w_gates_t"] = wt[IN_GATES:IN_END]
            else:
                out[n] = _cols_from_shards(g) if n in COL_SHARDED else g.reshape(4 * g.shape[1], g.shape[2])
        return out

    def gather(op, group, after):
        names = GATHER_GROUPS[group]
        if op == "start":
            place(names, ())
            inflight[group] = _gather_start([placed[n] for n in names], after, name="gather_" + group + "_start")
            behind = (inflight[group][3],)
            if group == "ffn1o":
                place(shard_of, behind)
            return behind
        if op == "pass":
            send, recv, bufs, _ = inflight[group]
            inflight[group] = _gather_pass(bufs, send, recv, after, name="gather_" + group + "_pass")
            return (inflight[group][2][0],)
        send, recv, bufs = inflight.pop(group)
        return relayout(names, _gather_finish(bufs, send, recv, after, name="gather_" + group + "_finish"))

    small = {n: w[n].reshape(1, D) for n in GAINS}
    small["b_f"] = w["b_f"].reshape(1, 512)
    small["pool_scale"] = w["pool_scale"].reshape(1, 512)
    small["w_pool_b"] = w["w_pool"].astype(BF16)

    pending, crossing, travelling = {}, {}, {}

    def emit(name, grad):
        pending[name] = grad
        group = next((g for g, names in REDUCE_GROUPS.items() if name == names[-1]), None)
        if group is None:
            return ()
        gb = {n: pending.pop(n) for n in REDUCE_GROUPS[group]}
        if group == "mix":
            dwt = jnp.concatenate([gb.pop("w_gla")[0:IN_PX], gb.pop("w_p"), gb.pop("w_xq"), gb.pop("w_gates")], axis=0)
            gb["w_in"] = dwt.reshape(4, IN_END // 4, D)
        names = list(gb)
        contrib = [gb[n] if n in COL_SHARDED else gb[n].reshape(4, gb[n].shape[0] // 4, gb[n].shape[1]) for n in names]
        if group == REDUCE_LAST:
            from_sibling = _pair_exchange(contrib, name="grads_" + group + "_pair_exchange")
            return over_chips(group, names, contrib, from_sibling)
        send, recv, contrib, lands, token = _pair_exchange_start(contrib, (), name="grads_" + group + "_pair_start")
        crossing[group] = (names, contrib, lands, send, recv)
        return (token,)

    def over_chips(group, names, contrib, from_sibling):
        pair = [_pair_sum(g, got, c_arr, name="grads_pair_sum_" + n) for n, g, got in zip(names, contrib, from_sibling)]
        send, recv, pair, lands, token = _chip_exchange_start(pair, (), name="grads_" + group + "_chip_start")
        travelling[group] = (names, send, recv, pair, lands)
        return (token,)

    def advance(after):
        behind = ()
        for group in list(crossing):
            names, contrib, lands, send, recv = crossing.pop(group)
            contrib, from_sibling = _pair_exchange_finish(contrib, lands, send, recv, after, name="grads_" + group + "_pair_finish")
            behind = over_chips(group, names, contrib, from_sibling)
        return behind

    loss, grad_x, gs = _local_step(x[0], mem[0], loss_target[0], small, gather, emit, advance)
    loss = lax.psum(loss[0, 0], ("x", "y", "c"))

    halves = {}
    for group, (names, send, recv, pair, lands) in travelling.items():
        pair, from_chips = _chip_exchange_finish(pair, lands, send, recv, (grad_x,), name="grads_" + group + "_chip_finish")
        for n, p, got in zip(names, pair, from_chips):
            halves[n] = _chip_sum(p, got, place_arr, name="grads_chip_sum_" + n)
    send, recv, joining, token = _pair_join_start([halves[n] for n in BIG], name="grads_pair_join_start")
    small_sums = _all_sum_small(gs, name="sum_small_grads", after=(token,))
    reduced = dict(zip(BIG, _pair_join_finish(joining, send, recv, (small_sums[0],), name="grads_pair_join_finish")))

    grads, delta, new_m, new_v = {}, {}, {}, {}
    for n in BIG:
        if n == "w_in":
            transposed = [jnp.transpose(args[k][0]) for k in (n, "m_" + n, "v_" + n)]
            updated = _adamw(transposed[0], reduced[n], transposed[1], transposed[2], name="adamw_" + n)
            grads[n] = jnp.transpose(reduced[n])[None]
            delta[n], new_m[n], new_v[n] = (jnp.transpose(a)[None] for a in updated)
            continue
        grads[n] = reduced[n][None]
        delta[n], new_m[n], new_v[n] = _adamw(args[n], reduced[n], args["m_" + n], args["v_" + n], name="adamw_" + n)
    small_params = {n: (args[n], args["m_" + n], args["v_" + n]) for n in SMALL}
    for n, (g, d, mn, vn) in _adamw_small(small_sums, small_params, chip_arr, name="adamw_small").items():
        grads[n], delta[n], new_m[n], new_v[n] = g, d, mn, vn

    outs = [loss, grad_x[None]]
    for group in (grads, delta, new_m, new_v):
        outs += [group[n] for n in WEIGHTS]
    return tuple(outs)
```

```python
import functools

import jax
import jax.numpy as jnp
from jax import lax
from jax.experimental import pallas as pl
from jax.experimental.pallas import tpu as pltpu

F32 = jnp.float32
BF16 = jnp.bfloat16
MESH = pl.DeviceIdType.MESH
HIGHEST = lax.Precision.HIGHEST

D = 1024
DFF = 2816
CHUNK = 64
HEADS = 4
HDK = 128
HDV = 256
GATE_TEMP = 16.0
POOL_WINDOWS = (2, 4, 8, 16)
POOL_HALO = 16
XA_HEADS = 4
XA_HD = 128
EPS = 1e-6
Q_SCALE = HDK ** -0.5
XA_SCALE = XA_HD ** -0.5
PG_Q, PG_K, PG_V, PG_G, PG_F, PG_W = 0, 512, 1024, 2048, 3072, 3200
GATE_RANK = 16
ADAM_LR, ADAM_B1, ADAM_B2, ADAM_EPS, ADAM_WD, ADAM_STEP = 0.001, 0.9, 0.999, 1e-08, 0.01, 10

VMEM_LIMIT = 48 * 1024 * 1024
LANE = 128
TS_ROW = 512
TS_GLA = 512
TS_POOL = 512
TS_XA = 512


def _params(sem):
    return pltpu.CompilerParams(dimension_semantics=sem, vmem_limit_bytes=VMEM_LIMIT)


def _tile(n, cap, unit=LANE):
    if n <= cap:
        return n
    best = None
    for t in range(unit, cap + 1, unit):
        if n % t == 0:
            best = t
    assert best is not None, (n, cap)
    return best


def _sigmoid(x):
    return 0.5 * jnp.tanh(0.5 * x) + 0.5


def _log_sigmoid(x):
    return jnp.minimum(x, 0.0) - jnp.log(1.0 + jnp.exp(-jnp.abs(x)))


def _rms(x):
    r = lax.rsqrt(jnp.mean(x * x, axis=-1, keepdims=True) + EPS)
    return x * r, r


def _rows(ts, w):
    return pl.BlockSpec((ts, w), lambda i: (i, 0))


def _fixed(shape):
    nd = len(shape)
    return pl.BlockSpec(shape, lambda i: (0,) * nd)


def _mm(a, b, *, ta=False, tb=False, out_dtype=F32, tm=2048, tn=1024, tk=1024, shards=1, after=(), name):
    a_blocked, b_blocked = a.ndim == 3, b.ndim == 3
    assert not (a_blocked and ta) and not (b_blocked and tb)
    if a_blocked:
        m, kdim, tk = a.shape[1], a.shape[0] * a.shape[2], a.shape[2]
    else:
        m, kdim = (a.shape[1], a.shape[0]) if ta else a.shape
    if b_blocked:
        n, tn = b.shape[0] * b.shape[2], b.shape[2]
        assert b.shape[1] == kdim and shards in (1, b.shape[0])
    else:
        n = b.shape[0] if tb else b.shape[1]
        assert (b.shape[1] if tb else b.shape[0]) == kdim, (a.shape, b.shape, ta, tb)
        tn = n // shards if shards > 1 else _tile(n, tn)
    tm = _tile(m, tm)
    tk = tk if a_blocked else _tile(kdim, tk)
    kgroup = 2 if (a_blocked and tb and a.shape[0] % 2 == 0) else 1
    nk = kdim // (tk * kgroup)
    dims = (((0 if ta else 1,), (1 if tb else 0,)), ((), ()))

    def body(a_ref, b_ref, *rest):
        o_ref, *acc = rest[len(after):]
        if kgroup == 1:
            part = lax.dot_general(a_ref[...], b_ref[...], dims, preferred_element_type=F32)
        else:
            part = sum(lax.dot_general(a_ref[g], b_ref[:, g * tk:(g + 1) * tk], dims, preferred_element_type=F32) for g in range(kgroup))
        if nk == 1:
            o_ref[...] = part.astype(o_ref.dtype)
            return
        acc_ref, = acc
        k = pl.program_id(2)

        @pl.when(k == 0)
        def _():
            acc_ref[...] = part

        @pl.when(k > 0)
        def _():
            acc_ref[...] += part

        @pl.when(k == nk - 1)
        def _():
            o_ref[...] = acc_ref[...].astype(o_ref.dtype)

    if a_blocked and kgroup > 1:
        a_spec = pl.BlockSpec((kgroup, tm, tk), lambda i, j, k: (k, i, 0))
    elif a_blocked:
        a_spec = pl.BlockSpec((None, tm, tk), lambda i, j, k: (k, i, 0))
    else:
        a_spec = pl.BlockSpec((tk, tm), lambda i, j, k: (k, i)) if ta else pl.BlockSpec((tm, tk), lambda i, j, k: (i, k))
    if b_blocked:
        b_spec = pl.BlockSpec((None, tk, tn), lambda i, j, k: (j, k, 0))
    else:
        b_spec = pl.BlockSpec((tn, tk * kgroup), lambda i, j, k: (j, k)) if tb else pl.BlockSpec((tk, tn), lambda i, j, k: (k, j))
    if shards > 1:
        out_shape = jax.ShapeDtypeStruct((shards, m, tn), out_dtype)
        o_spec = pl.BlockSpec((None, tm, tn), lambda i, j, k: (j, i, 0))
    else:
        out_shape = jax.ShapeDtypeStruct((m, n), out_dtype)
        o_spec = pl.BlockSpec((tm, tn), lambda i, j, k: (i, j))
    return pl.pallas_call(
        body, grid=(m // tm, n // tn, nk), in_specs=[a_spec, b_spec] + [ANY] * len(after), out_specs=o_spec, out_shape=out_shape,
        scratch_shapes=[pltpu.VMEM((tm, tn), F32)] if nk > 1 else [],
        compiler_params=_params(("parallel", "parallel", "arbitrary")), name=name,
    )(a, b, *after)


def _norm_fwd(x, g, out_dtype, name, after=()):
    s, d = x.shape
    ts = _tile(s, TS_ROW, 8)

    def body(x_ref, g_ref, *rest):
        o_ref = rest[len(after)]
        xh, _ = _rms(x_ref[...])
        o_ref[...] = (xh * g_ref[...]).astype(o_ref.dtype)

    return pl.pallas_call(
        body, grid=(s // ts,), in_specs=[_rows(ts, d), _fixed((1, d))] + [ANY] * len(after), out_specs=_rows(ts, d),
        out_shape=jax.ShapeDtypeStruct((s, d), out_dtype), compiler_params=_params(("parallel",)), name=name,
    )(x, g, *after)


def _resid_norm_fwd(x, f, g_post, alpha, g_next, name, after=()):
    s, d = x.shape
    ts = _tile(s, TS_ROW, 8)
    with_h = g_next is not None

    def body(x_ref, f_ref, gp_ref, *rest):
        rest = rest[:1] + rest[1 + len(after):] if with_h else rest[len(after):]
        fh, _ = _rms(f_ref[...])
        xn = x_ref[...] + alpha * (fh * gp_ref[...])
        if with_h:
            gn_ref, xo_ref, h_ref = rest
            xh, _ = _rms(xn)
            h_ref[...] = (xh * gn_ref[...]).astype(h_ref.dtype)
        else:
            xo_ref, = rest
        xo_ref[...] = xn

    ins = [x, f, g_post] + ([g_next] if with_h else []) + list(after)
    in_specs = [_rows(ts, d), _rows(ts, d), _fixed((1, d))] + ([_fixed((1, d))] if with_h else []) + [ANY] * len(after)
    out_shape = [jax.ShapeDtypeStruct((s, d), F32)] + ([jax.ShapeDtypeStruct((s, d), BF16)] if with_h else [])
    out_specs = [_rows(ts, d)] + ([_rows(ts, d)] if with_h else [])
    out = pl.pallas_call(
        body, grid=(s // ts,), in_specs=in_specs, out_specs=out_specs, out_shape=out_shape,
        compiler_params=_params(("parallel",)), name=name,
    )(*ins)
    return (out[0], out[1]) if with_h else (out[0], None)


def _mm_resid_norm(a, w, x, g_post, alpha, g_next, name, after=(), tm=512):
    s, kdim = a.shape
    d = w.shape[1]
    tm = _tile(s, tm)
    with_h = g_next is not None
    na = len(after)

    def body(a_ref, w_ref, x_ref, gp_ref, *rest):
        rest = rest[int(with_h) + na:] if not with_h else rest[:1] + rest[1 + na:]
        for rows in _sub_blocks(tm):
            f = jnp.dot(a_ref[rows, :], w_ref[...], preferred_element_type=F32)
            fh, _ = _rms(f)
            xn = x_ref[rows, :] + alpha * (fh * gp_ref[...])
            if with_h:
                gn_ref, f_ref, xo_ref, h_ref = rest
                xh, _ = _rms(xn)
                h_ref[rows, :] = (xh * gn_ref[...]).astype(h_ref.dtype)
            else:
                f_ref, xo_ref = rest
            f_ref[rows, :] = f
            xo_ref[rows, :] = xn

    ins = [a, w, x, g_post] + ([g_next] if with_h else []) + list(after)
    in_specs = [_rows(tm, kdim), _fixed((kdim, d)), _rows(tm, d), _fixed((1, d))] + ([_fixed((1, d))] if with_h else []) + [ANY] * na
    out_shape = [jax.ShapeDtypeStruct((s, d), F32)] * 2 + ([jax.ShapeDtypeStruct((s, d), BF16)] if with_h else [])
    out = pl.pallas_call(
        body, grid=(s // tm,), in_specs=in_specs, out_specs=[_rows(tm, d)] * len(out_shape), out_shape=out_shape,
        compiler_params=_params(("parallel",)), name=name,
    )(*ins)
    return (out[0], out[1], out[2]) if with_h else (out[0], out[1], None)


def _mm_rms_bwd(pairs, x, g, dres, name, after=(), tm=512):
    s, d = x.shape
    tm = _tile(s, tm)
    n, na = len(pairs), len(after)

    def body(*refs):
        a_refs, w_refs = refs[0:2 * n:2], refs[1:2 * n:2]
        x_ref, g_ref, dres_ref = refs[2 * n:2 * n + 3]
        dx_ref, dg_ref = refs[2 * n + 3 + na:]
        @pl.when(pl.program_id(0) == 0)
        def _():
            dg_ref[...] = jnp.zeros_like(dg_ref)

        for rows in _sub_blocks(tm):
            dy = None
            for a_ref, w_ref in zip(a_refs, w_refs):
                if len(a_ref.shape) == 3:
                    tkb = a_ref.shape[2]
                    parts = [lax.dot_general(a_ref[q, rows, :], w_ref[:, q * tkb:(q + 1) * tkb], (((1,), (1,)), ((), ())),
                                             preferred_element_type=F32) for q in range(a_ref.shape[0])]
                else:
                    parts = [jnp.dot(a_ref[rows, :], w_ref[...], preferred_element_type=F32)]
                for part in parts:
                    dy = part if dy is None else dy + part
            xh, r = _rms(x_ref[rows, :])
            dg_ref[...] += jnp.sum(dy * xh, axis=0, keepdims=True)
            dyg = dy * g_ref[...]
            dx_ref[rows, :] = r * (dyg - xh * jnp.mean(dyg * xh, axis=-1, keepdims=True)) + dres_ref[rows, :]

    ins, in_specs = [], []
    for a_arr, w_arr in pairs:
        ins += [a_arr, w_arr]
        if a_arr.ndim == 3:
            in_specs.append(pl.BlockSpec((a_arr.shape[0], tm, a_arr.shape[2]), lambda i: (0, i, 0)))
        else:
            in_specs.append(_rows(tm, a_arr.shape[1]))
        in_specs.append(pl.BlockSpec(w_arr.shape, lambda i: (0, 0), pipeline_mode=pl.Buffered(1)))
    return pl.pallas_call(
        body, grid=(s // tm,),
        in_specs=in_specs + [_rows(tm, d), _fixed((1, d)), _rows(tm, d)] + [ANY] * na,
        out_specs=[_rows(tm, d), _fixed((1, d))],
        out_shape=[jax.ShapeDtypeStruct((s, d), F32), jax.ShapeDtypeStruct((1, d), F32)],
        compiler_params=_params(("arbitrary",)), name=name,
    )(*ins, x, g, dres, *after)


def _rms_bwd(x, g, dys, dres, alpha, out_dtype, name, after=()):
    s, d = x.shape
    ts = _tile(s, TS_ROW, 8)
    ndy = len(dys)
    with_res = dres is not None

    def body(x_ref, g_ref, *rest):
        dy_refs = rest[:ndy]
        rest = rest[ndy:]
        if with_res:
            dres_ref = rest[0]
        dx_ref, dg_ref = rest[int(with_res) + len(after):]
        xh, r = _rms(x_ref[...])
        dy = dy_refs[0][...].astype(F32)
        for ref in dy_refs[1:]:
            dy = dy + ref[...].astype(F32)
        dy = dy * alpha

        @pl.when(pl.program_id(0) == 0)
        def _():
            dg_ref[...] = jnp.zeros_like(dg_ref)

        dg_ref[...] += jnp.sum(dy * xh, axis=0, keepdims=True)
        dyg = dy * g_ref[...]
        dx = r * (dyg - xh * jnp.mean(dyg * xh, axis=-1, keepdims=True))
        if with_res:
            dx = dx + dres_ref[...]
        dx_ref[...] = dx.astype(dx_ref.dtype)

    ins = [x, g] + list(dys) + ([dres] if with_res else []) + list(after)
    in_specs = [_rows(ts, d), _fixed((1, d))] + [_rows(ts, d)] * (ndy + int(with_res)) + [ANY] * len(after)
    return pl.pallas_call(
        body, grid=(s // ts,), in_specs=in_specs, out_specs=[_rows(ts, d), _fixed((1, d))],
        out_shape=[jax.ShapeDtypeStruct((s, d), out_dtype), jax.ShapeDtypeStruct((1, d), F32)],
        compiler_params=_params(("arbitrary",)), name=name,
    )(*ins)


def _loss_bwd(x, g, target, name):
    s, d = x.shape
    ts = _tile(s, TS_ROW, 8)

    def body(x_ref, g_ref, t_ref, dx_ref, dg_ref, loss_ref):
        xh, r = _rms(x_ref[...])
        gv = g_ref[...]
        diff = xh * gv - t_ref[...]

        @pl.when(pl.program_id(0) == 0)
        def _():
            dg_ref[...] = jnp.zeros_like(dg_ref)
            loss_ref[...] = jnp.zeros_like(loss_ref)

        sq = jnp.sum(diff * diff, axis=1, keepdims=True)
        loss_ref[...] += (0.5 / d) * jnp.sum(sq, axis=0, keepdims=True)
        dy = diff * (1.0 / d)
        dg_ref[...] += jnp.sum(dy * xh, axis=0, keepdims=True)
        dyg = dy * gv
        dx_ref[...] = r * (dyg - xh * jnp.mean(dyg * xh, axis=-1, keepdims=True))

    return pl.pallas_call(
        body, grid=(s // ts,), in_specs=[_rows(ts, d), _fixed((1, d)), _rows(ts, d)],
        out_specs=[_rows(ts, d), _fixed((1, d)), _fixed((8, LANE))],
        out_shape=[jax.ShapeDtypeStruct((s, d), F32), jax.ShapeDtypeStruct((1, d), F32), jax.ShapeDtypeStruct((8, LANE), F32)],
        compiler_params=_params(("arbitrary",)), name=name,
    )(x, g, target)


HALF_FF = DFF // 2


SUB_ROWS = 256


def _sub_blocks(tm):
    sub = SUB_ROWS if tm % SUB_ROWS == 0 else tm
    return [slice(r0, r0 + sub) for r0 in range(0, tm, sub)]


def _ffn_in_swiglu(x_norm, w_in, name, after=(), tm=1024):
    s, d = x_norm.shape
    tm = _tile(s, tm)

    def body(x_ref, wa_ref, wb_ref, *rest):
        ab_ref, u_ref = rest[len(after):]
        for rows in _sub_blocks(tm):
            xv = x_ref[rows, :]
            a = jnp.dot(xv, wa_ref[...], preferred_element_type=F32)
            b = jnp.dot(xv, wb_ref[...], preferred_element_type=F32)
            ab_ref[0, rows, :] = a.astype(ab_ref.dtype)
            ab_ref[1, rows, :] = b.astype(ab_ref.dtype)
            u_ref[rows, :] = (a * _sigmoid(a) * b).astype(u_ref.dtype)

    ab, u = pl.pallas_call(
        body, grid=(s // tm, 2),
        in_specs=[pl.BlockSpec((tm, d), lambda i, j: (i, 0)), pl.BlockSpec((d, HALF_FF), lambda i, j: (0, j)),
                  pl.BlockSpec((d, HALF_FF), lambda i, j: (0, 2 + j))] + [ANY] * len(after),
        out_specs=[pl.BlockSpec((2, None, tm, HALF_FF), lambda i, j: (0, j, i, 0)), pl.BlockSpec((tm, HALF_FF), lambda i, j: (i, j))],
        out_shape=[jax.ShapeDtypeStruct((2, 2, s, HALF_FF), BF16), jax.ShapeDtypeStruct((s, DFF), BF16)],
        compiler_params=_params(("parallel", "parallel")), name=name,
    )(x_norm, w_in, w_in, *after)
    return ab.reshape(4, s, HALF_FF), u


def _ffn_out_dx_swiglu(dz, w_out, ab, after, name, tm=1024):
    s, d = dz.shape
    tm = _tile(s, tm)

    def body(dz_ref, w_ref, ab_ref, *rest):
        dab_ref = rest[len(after)]
        for rows in _sub_blocks(tm):
            du = lax.dot_general(dz_ref[rows, :], w_ref[...], (((1,), (1,)), ((), ())), preferred_element_type=F32)
            a = ab_ref[0, rows, :].astype(F32)
            b = ab_ref[1, rows, :].astype(F32)
            sig = _sigmoid(a)
            dab_ref[0, rows, :] = (du * b * (sig * (1.0 + a * (1.0 - sig)))).astype(dab_ref.dtype)
            dab_ref[1, rows, :] = (du * a * sig).astype(dab_ref.dtype)

    halves = pl.BlockSpec((2, None, tm, HALF_FF), lambda i, j: (0, j, i, 0))
    dab = pl.pallas_call(
        body, grid=(s // tm, 2),
        in_specs=[pl.BlockSpec((tm, d), lambda i, j: (i, 0)), pl.BlockSpec((HALF_FF, d), lambda i, j: (j, 0)), halves] + [ANY] * len(after),
        out_specs=halves, out_shape=jax.ShapeDtypeStruct((2, 2, s, HALF_FF), BF16),
        compiler_params=_params(("parallel", "parallel")), name=name,
    )(dz, w_out, ab.reshape(2, 2, s, HALF_FF), *after)
    return dab.reshape(4, s, HALF_FF)


def _tri(strict):
    r = lax.broadcasted_iota(jnp.int32, (CHUNK, CHUNK), 0)
    c = lax.broadcasted_iota(jnp.int32, (CHUNK, CHUNK), 1)
    return (r > c).astype(F32) if strict else (r >= c).astype(F32)


def _gla_fwd(pg, wfu, b_f, gnorm, name):
    s = pg.shape[0]
    ts = _tile(s, TS_GLA, CHUNK)
    cpb = ts // CHUNK
    nc = s // CHUNK

    def body(pg_ref, wfu_ref, bf_ref, gn_ref, ya_ref, sp_ref, so_ref, o_ref, st_ref, la_ref, dec_ref, u_ref):
        @pl.when(pl.program_id(0) == 0)
        def _():
            st_ref[...] = jnp.zeros_like(st_ref)

        f = jnp.dot(pg_ref[:, PG_F:PG_W], wfu_ref[...], preferred_element_type=F32) + bf_ref[...]
        la_ref[...] = _log_sigmoid(f) * (1.0 / GATE_TEMP)
        tri = _tri(False)
        chunks = [slice(ci * CHUNK, (ci + 1) * CHUNK) for ci in range(cpb)]
        for ci, rows in enumerate(chunks):
            la = la_ref[rows, :]
            b = jnp.dot(tri, la, precision=HIGHEST, preferred_element_type=F32)
            bend = jnp.sum(la, axis=0, keepdims=True)
            e = jnp.exp(bend - b)
            dec_ref[ci:ci + 1, :] = jnp.exp(bend)
            for hd in range(HEADS):
                k = pg_ref[rows, PG_K + hd * HDK:PG_K + (hd + 1) * HDK]
                v = pg_ref[rows, PG_V + hd * HDV:PG_V + (hd + 1) * HDV]
                kt = (k.astype(F32) * e[:, hd * HDK:(hd + 1) * HDK]).astype(BF16)
                u_ref[ci, hd] = lax.dot_general(v, kt, (((0,), (0,)), ((), ())), preferred_element_type=F32)
        for ci in range(cpb):
            for hd in range(HEADS):
                prev = st_ref[hd]
                sp_ref[ci, hd] = prev
                st = prev * dec_ref[ci:ci + 1, hd * HDK:(hd + 1) * HDK] + u_ref[ci, hd]
                st_ref[hd] = st
                so_ref[ci, hd] = st.astype(so_ref.dtype)
        for ci, rows in enumerate(chunks):
            for hd in range(HEADS):
                vc = slice(hd * HDV, (hd + 1) * HDV)
                q = pg_ref[rows, PG_Q + hd * HDK:PG_Q + (hd + 1) * HDK]
                go = pg_ref[rows, PG_G + hd * HDV:PG_G + (hd + 1) * HDV].astype(F32)
                qs = (q.astype(F32) * Q_SCALE).astype(BF16)
                o = lax.dot_general(qs, so_ref[ci, hd], (((1,), (1,)), ((), ())), preferred_element_type=F32)
                o_ref[rows, vc] = o
                oh, _ = _rms(o)
                ya_ref[rows, vc] = (oh * gn_ref[:, vc] * (go * _sigmoid(go))).astype(ya_ref.dtype)

    return pl.pallas_call(
        body, grid=(s // ts,),
        in_specs=[_rows(ts, PG_W), _fixed((LANE, HEADS * HDK)), _fixed((1, HEADS * HDK)), _fixed((1, HEADS * HDV))],
        out_specs=[_rows(ts, HEADS * HDV), pl.BlockSpec((cpb, HEADS, HDV, HDK), lambda i: (i, 0, 0, 0)),
                   pl.BlockSpec((cpb, HEADS, HDV, HDK), lambda i: (i, 0, 0, 0)), _rows(ts, HEADS * HDV)],
        out_shape=[jax.ShapeDtypeStruct((s, HEADS * HDV), BF16), jax.ShapeDtypeStruct((nc, HEADS, HDV, HDK), F32),
                   jax.ShapeDtypeStruct((nc, HEADS, HDV, HDK), BF16), jax.ShapeDtypeStruct((s, HEADS * HDV), F32)],
        scratch_shapes=[pltpu.VMEM((HEADS, HDV, HDK), F32), pltpu.VMEM((ts, HEADS * HDK), F32),
                        pltpu.VMEM((max(cpb, 8), HEADS * HDK), F32), pltpu.VMEM((cpb, HEADS, HDV, HDK), F32)],
        compiler_params=_params(("arbitrary",)), name=name,
    )(pg, wfu, b_f, gnorm)


def _gla_bwd(pg, sp, so, o, dya, wfu, b_f, gnorm, name):
    s = pg.shape[0]
    ts = _tile(s, TS_GLA, CHUNK)
    cpb = ts // CHUNK
    nblk = s // ts

    def body(pg_ref, sp_ref, so_ref, o_ref, dya_ref, wfu_ref, bf_ref, gn_ref, dpg_ref, dwfu_ref, dbf_ref, dgn_ref,
             dst_ref, la_ref, sg_ref, df_ref, e_ref, ktf_ref, dec_ref, g_ref):
        @pl.when(pl.program_id(0) == 0)
        def _():
            dst_ref[...] = jnp.zeros_like(dst_ref)
            dwfu_ref[...] = jnp.zeros_like(dwfu_ref)
            dbf_ref[...] = jnp.zeros_like(dbf_ref)
            dgn_ref[...] = jnp.zeros_like(dgn_ref)

        flow = pg_ref[:, PG_F:PG_W]
        f = jnp.dot(flow, wfu_ref[...], preferred_element_type=F32) + bf_ref[...]
        la_ref[...] = _log_sigmoid(f) * (1.0 / GATE_TEMP)
        sg_ref[...] = _sigmoid(-f) * (1.0 / GATE_TEMP)
        tri = _tri(False)
        tri_strict = _tri(True)
        chunks = [slice(ci * CHUNK, (ci + 1) * CHUNK) for ci in range(cpb)]
        for ci, rows in enumerate(chunks):
            la = la_ref[rows, :]
            b = jnp.dot(tri, la, precision=HIGHEST, preferred_element_type=F32)
            bend = jnp.sum(la, axis=0, keepdims=True)
            e = jnp.exp(bend - b)
            e_ref[rows, :] = e
            dec = jnp.exp(bend)
            dec_ref[ci:ci + 1, :] = dec
            for hd in range(HEADS):
                kc = slice(hd * HDK, (hd + 1) * HDK)
                vc = slice(hd * HDV, (hd + 1) * HDV)
                q = pg_ref[rows, PG_Q + hd * HDK:PG_Q + (hd + 1) * HDK]
                k = pg_ref[rows, PG_K + hd * HDK:PG_K + (hd + 1) * HDK]
                go = pg_ref[rows, PG_G + hd * HDV:PG_G + (hd + 1) * HDV].astype(F32)
                ktf_ref[rows, kc] = k.astype(F32) * e[:, kc]
                st_b = so_ref[ci, hd]
                qs = (q.astype(F32) * Q_SCALE).astype(BF16)
                oh, r = _rms(o_ref[rows, vc])
                gh = gn_ref[:, vc]
                sig = _sigmoid(go)
                dy = dya_ref[rows, vc].astype(F32)
                don = dy * (go * sig)
                dgn_ref[:, vc] += jnp.sum(don * oh, axis=0, keepdims=True)
                dong = don * gh
                do = (r * (dong - oh * jnp.mean(dong * oh, axis=-1, keepdims=True))).astype(BF16)
                g_ref[ci, hd] = lax.dot_general(do, qs, (((0,), (0,)), ((), ())), preferred_element_type=F32)
                dq = jnp.dot(do, st_b, preferred_element_type=F32) * Q_SCALE
                dpg_ref[rows, PG_Q + hd * HDK:PG_Q + (hd + 1) * HDK] = dq.astype(dpg_ref.dtype)
                dgo = dy * (oh * gh) * (sig * (1.0 + go * (1.0 - sig)))
                dpg_ref[rows, PG_G + hd * HDV:PG_G + (hd + 1) * HDV] = dgo.astype(dpg_ref.dtype)
        for ci in reversed(range(cpb)):
            for hd in range(HEADS):
                dst = dst_ref[hd] + g_ref[ci, hd]
                g_ref[ci, hd] = dst
                dst_ref[hd] = dst * dec_ref[ci:ci + 1, hd * HDK:(hd + 1) * HDK]
        for ci, rows in enumerate(chunks):
            for hd in range(HEADS):
                kc = slice(hd * HDK, (hd + 1) * HDK)
                v = pg_ref[rows, PG_V + hd * HDV:PG_V + (hd + 1) * HDV]
                ktf = ktf_ref[rows, kc]
                dst = g_ref[ci, hd]
                dst_b = dst.astype(BF16)
                dkt = jnp.dot(v, dst_b, preferred_element_type=F32)
                dv = lax.dot_general(ktf.astype(BF16), dst_b, (((1,), (1,)), ((), ())), preferred_element_type=F32)
                dd = jnp.sum(dst * sp_ref[ci, hd], axis=0, keepdims=True)
                dla = jnp.dot(tri_strict, dkt * ktf, precision=HIGHEST, preferred_element_type=F32) + dd * dec_ref[ci:ci + 1, kc]
                df_ref[rows, kc] = dla * sg_ref[rows, kc]
                dpg_ref[rows, PG_K + hd * HDK:PG_K + (hd + 1) * HDK] = (dkt * e_ref[rows, kc]).astype(dpg_ref.dtype)
                dpg_ref[rows, PG_V + hd * HDV:PG_V + (hd + 1) * HDV] = dv.astype(dpg_ref.dtype)
        df = df_ref[...]
        df_b = df.astype(BF16)
        dpg_ref[:, PG_F:PG_W] = lax.dot_general(df_b, wfu_ref[...], (((1,), (1,)), ((), ())), preferred_element_type=F32).astype(dpg_ref.dtype)
        dwfu_ref[...] += lax.dot_general(flow, df_b, (((0,), (0,)), ((), ())), preferred_element_type=F32)
        dbf_ref[...] += jnp.sum(df, axis=0, keepdims=True)

    rev = lambda i: (nblk - 1 - i, 0)
    return pl.pallas_call(
        body, grid=(nblk,),
        in_specs=[pl.BlockSpec((ts, PG_W), rev), pl.BlockSpec((cpb, HEADS, HDV, HDK), lambda i: (nblk - 1 - i, 0, 0, 0)),
                  pl.BlockSpec((cpb, HEADS, HDV, HDK), lambda i: (nblk - 1 - i, 0, 0, 0)), pl.BlockSpec((ts, HEADS * HDV), rev),
                  pl.BlockSpec((ts, HEADS * HDV), rev), _fixed((LANE, HEADS * HDK)), _fixed((1, HEADS * HDK)), _fixed((1, HEADS * HDV))],
        out_specs=[pl.BlockSpec((ts, PG_W), rev), _fixed((LANE, HEADS * HDK)), _fixed((1, HEADS * HDK)), _fixed((1, HEADS * HDV))],
        out_shape=[jax.ShapeDtypeStruct((s, PG_W), BF16), jax.ShapeDtypeStruct((LANE, HEADS * HDK), F32),
                   jax.ShapeDtypeStruct((1, HEADS * HDK), F32), jax.ShapeDtypeStruct((1, HEADS * HDV), F32)],
        scratch_shapes=[pltpu.VMEM((HEADS, HDV, HDK), F32)] + [pltpu.VMEM((ts, HEADS * HDK), F32)] * 5
        + [pltpu.VMEM((max(cpb, 8), HEADS * HDK), F32), pltpu.VMEM((cpb, HEADS, HDV, HDK), F32)],
        compiler_params=_params(("arbitrary",)), name=name,
    )(pg, sp, so, o, dya, wfu, b_f, gnorm)


def _window_sums(ext, sign):
    n = ext.shape[0]
    sums = {1: ext}
    w = 1
    while w < POOL_WINDOWS[-1]:
        sums[2 * w] = sums[w] + pltpu.roll(sums[w], w if sign > 0 else n - w, 0)
        w *= 2
    return [sums[POOL_WINDOWS[g]][:, g * LANE:(g + 1) * LANE] for g in range(len(POOL_WINDOWS))]


def _pool_counts(row0, n):
    pos = (row0 + lax.broadcasted_iota(jnp.int32, (n, 1), 0) + 1).astype(F32)
    return [1.0 / jnp.minimum(pos, float(w)) for w in POOL_WINDOWS]


def _pool_fwd(ppx, w_pool, pool_scale, name):
    s = ppx.shape[0]
    ts = _tile(s, TS_POOL, POOL_HALO)
    hb = ts // POOL_HALO
    pw = len(POOL_WINDOWS) * LANE

    def body(p_ref, halo_ref, w_ref, sc_ref, y_ref, ext_ref):
        i = pl.program_id(0)
        p = p_ref[...].astype(F32)
        ext_ref[0:POOL_HALO, :] = jnp.where(i > 0, halo_ref[...].astype(F32), 0.0)
        ext_ref[POOL_HALO:, :] = p
        sums = _window_sums(ext_ref[...], +1)
        cnt = _pool_counts(i * ts, ts)
        for g in range(len(POOL_WINDOWS)):
            cols = slice(g * LANE, (g + 1) * LANE)
            mixed = sums[g][POOL_HALO:, :] * cnt[g] - p[:, cols]
            y = jnp.dot(mixed.astype(BF16), w_ref[g], preferred_element_type=F32)
            y_ref[:, cols] = (y * sc_ref[:, cols]).astype(y_ref.dtype)

    return pl.pallas_call(
        body, grid=(s // ts,),
        in_specs=[pl.BlockSpec((ts, pw), lambda i: (i, 0)), pl.BlockSpec((POOL_HALO, pw), lambda i: (jnp.maximum(i * hb - 1, 0), 0)),
                  _fixed((len(POOL_WINDOWS), LANE, LANE)), _fixed((1, pw))],
        out_specs=_rows(ts, pw), out_shape=jax.ShapeDtypeStruct((s, pw), BF16),
        scratch_shapes=[pltpu.VMEM((ts + POOL_HALO, pw), F32)],
        compiler_params=_params(("parallel",)), name=name,
    )(ppx, ppx, w_pool, pool_scale)


def _pool_bwd(dyb, ppx, w_pool, pool_scale, name):
    s = ppx.shape[0]
    ts = _tile(s, TS_POOL, POOL_HALO)
    hb = ts // POOL_HALO
    nblk = s // ts
    last_halo = s // POOL_HALO - 1
    ng = len(POOL_WINDOWS)
    pw = ng * LANE

    def body(p_ref, halo_ref, dy_ref, dyn_ref, w_ref, sc_ref, dp_ref, dw_ref, dsc_ref, ext_ref, dext_ref, dm_ref):
        i = pl.program_id(0)

        @pl.when(i == 0)
        def _():
            dw_ref[...] = jnp.zeros_like(dw_ref)
            dsc_ref[...] = jnp.zeros_like(dsc_ref)

        p = p_ref[...].astype(F32)
        ext_ref[0:POOL_HALO, :] = jnp.where(i > 0, halo_ref[...].astype(F32), 0.0)
        ext_ref[POOL_HALO:, :] = p
        sums = _window_sums(ext_ref[...], +1)
        cnt = _pool_counts(i * ts, ts + POOL_HALO)
        sc = sc_ref[...]
        dy = dy_ref[...].astype(F32)
        dyn = jnp.where(i < nblk - 1, dyn_ref[...].astype(F32), 0.0)
        for g in range(ng):
            cols = slice(g * LANE, (g + 1) * LANE)
            wg = w_ref[g]
            mixed = (sums[g][POOL_HALO:, :] * cnt[g][0:ts] - p[:, cols]).astype(BF16)
            ypre = jnp.dot(mixed, wg, preferred_element_type=F32)
            dsc_ref[:, cols] += jnp.sum(dy[:, cols] * ypre, axis=0, keepdims=True)
            dyp = (dy[:, cols] * sc[:, cols]).astype(BF16)
            dypn = (dyn[:, cols] * sc[:, cols]).astype(BF16)
            dw_ref[g] += lax.dot_general(mixed, dyp, (((0,), (0,)), ((), ())), preferred_element_type=F32)
            dm = lax.dot_general(dyp, wg, (((1,), (1,)), ((), ())), preferred_element_type=F32)
            dmn = lax.dot_general(dypn, wg, (((1,), (1,)), ((), ())), preferred_element_type=F32)
            dext_ref[0:ts, cols] = dm * cnt[g][0:ts]
            dext_ref[ts:, cols] = dmn * cnt[g][ts:]
            dm_ref[:, cols] = dm
        lead = _window_sums(dext_ref[...], -1)
        for g in range(ng):
            cols = slice(g * LANE, (g + 1) * LANE)
            dp_ref[:, cols] = (lead[g][0:ts, :] - dm_ref[:, cols]).astype(dp_ref.dtype)

    return pl.pallas_call(
        body, grid=(nblk,),
        in_specs=[pl.BlockSpec((ts, pw), lambda i: (i, 0)), pl.BlockSpec((POOL_HALO, pw), lambda i: (jnp.maximum(i * hb - 1, 0), 0)),
                  pl.BlockSpec((ts, pw), lambda i: (i, 0)), pl.BlockSpec((POOL_HALO, pw), lambda i: (jnp.minimum((i + 1) * hb, last_halo), 0)),
                  _fixed((ng, LANE, LANE)), _fixed((1, pw))],
        out_specs=[_rows(ts, pw), _fixed((ng, LANE, LANE)), _fixed((1, pw))],
        out_shape=[jax.ShapeDtypeStruct((s, pw), BF16), jax.ShapeDtypeStruct((ng, LANE, LANE), F32), jax.ShapeDtypeStruct((1, pw), F32)],
        scratch_shapes=[pltpu.VMEM((ts + POOL_HALO, pw), F32), pltpu.VMEM((ts + POOL_HALO, pw), F32), pltpu.VMEM((ts, pw), F32)],
        compiler_params=_params(("arbitrary",)), name=name,
    )(ppx, ppx, dyb, dyb, w_pool, pool_scale)


def _xattn_fwd(ppx, kv, name):
    s = ppx.shape[0]
    m = kv.shape[0]
    ts = _tile(s, TS_XA, 8)
    xw = XA_HEADS * XA_HD

    def body(q_ref, kv_ref, o_ref):
        for hd in range(XA_HEADS):
            cols = slice(hd * XA_HD, (hd + 1) * XA_HD)
            k = kv_ref[:, hd * XA_HD:(hd + 1) * XA_HD]
            v = kv_ref[:, xw + hd * XA_HD:xw + (hd + 1) * XA_HD]
            sc = lax.dot_general(q_ref[:, cols], k, (((1,), (1,)), ((), ())), preferred_element_type=F32) * XA_SCALE
            ex = jnp.exp(sc - jnp.max(sc, axis=-1, keepdims=True))
            pr = ex * (1.0 / jnp.sum(ex, axis=-1, keepdims=True))
            o_ref[:, cols] = jnp.dot(pr.astype(BF16), v, preferred_element_type=F32).astype(o_ref.dtype)

    return pl.pallas_call(
        body, grid=(s // ts,), in_specs=[pl.BlockSpec((ts, xw), lambda i: (i, 1)), _fixed((m, 2 * xw))],
        out_specs=_rows(ts, xw), out_shape=jax.ShapeDtypeStruct((s, xw), BF16),
        compiler_params=_params(("parallel",)), name=name,
    )(ppx, kv)


def _xattn_bwd(dxc, ppx, kv, name):
    s = ppx.shape[0]
    m = kv.shape[0]
    ts = _tile(s, TS_XA, 8)
    xw = XA_HEADS * XA_HD

    def body(do_ref, q_ref, kv_ref, dq_ref, dkv_ref):
        @pl.when(pl.program_id(0) == 0)
        def _():
            dkv_ref[...] = jnp.zeros_like(dkv_ref)

        for hd in range(XA_HEADS):
            cols = slice(hd * XA_HD, (hd + 1) * XA_HD)
            vcols = slice(xw + hd * XA_HD, xw + (hd + 1) * XA_HD)
            q = q_ref[:, cols]
            k = kv_ref[:, cols]
            v = kv_ref[:, vcols]
            do = do_ref[:, cols]
            sc = lax.dot_general(q, k, (((1,), (1,)), ((), ())), preferred_element_type=F32) * XA_SCALE
            ex = jnp.exp(sc - jnp.max(sc, axis=-1, keepdims=True))
            pr = ex * (1.0 / jnp.sum(ex, axis=-1, keepdims=True))
            dpr = lax.dot_general(do, v, (((1,), (1,)), ((), ())), preferred_element_type=F32)
            dsc = (pr * (dpr - jnp.sum(dpr * pr, axis=-1, keepdims=True)) * XA_SCALE).astype(BF16)
            dq_ref[:, cols] = jnp.dot(dsc, k, preferred_element_type=F32).astype(dq_ref.dtype)
            dkv_ref[:, cols] += lax.dot_general(dsc, q, (((0,), (0,)), ((), ())), preferred_element_type=F32)
            dkv_ref[:, vcols] += lax.dot_general(pr.astype(BF16), do, (((0,), (0,)), ((), ())), preferred_element_type=F32)

    return pl.pallas_call(
        body, grid=(s // ts,), in_specs=[_rows(ts, xw), pl.BlockSpec((ts, xw), lambda i: (i, 1)), _fixed((m, 2 * xw))],
        out_specs=[_rows(ts, xw), _fixed((m, 2 * xw))],
        out_shape=[jax.ShapeDtypeStruct((s, xw), BF16), jax.ShapeDtypeStruct((m, 2 * xw), F32)],
        compiler_params=_params(("arbitrary",)), name=name,
    )(dxc, ppx, kv)


def _merge_fwd(pgt, ya, yb, yc, name):
    s = pgt.shape[0]
    ts = _tile(s, TS_ROW, 8)

    def body(gt_ref, ya_ref, yb_ref, yc_ref, o_ref):
        acc = _sigmoid(gt_ref[:, 0:D].astype(F32)) * ya_ref[...].astype(F32)
        acc = acc + _sigmoid(gt_ref[:, D:2 * D].astype(F32)) * yb_ref[...].astype(F32)
        acc = acc + _sigmoid(gt_ref[:, 2 * D:3 * D].astype(F32)) * yc_ref[...].astype(F32)
        o_ref[...] = acc.astype(o_ref.dtype)

    return pl.pallas_call(
        body, grid=(s // ts,), in_specs=[_rows(ts, 3 * D)] + [_rows(ts, D)] * 3, out_specs=_rows(ts, D),
        out_shape=jax.ShapeDtypeStruct((s, D), BF16), compiler_params=_params(("parallel",)), name=name,
    )(pgt, ya, yb, yc)


def _merge_bwd(dmerged, pgt, ya, yb, yc, name):
    s = pgt.shape[0]
    ts = _tile(s, TS_ROW, 8)

    def body(dm_ref, gt_ref, ya_ref, yb_ref, yc_ref, dya_ref, dyb_ref, dyc_ref, dgt_ref):
        dm = dm_ref[...].astype(F32)
        for j, (y_ref, dy_ref) in enumerate(((ya_ref, dya_ref), (yb_ref, dyb_ref), (yc_ref, dyc_ref))):
            sig = _sigmoid(gt_ref[:, j * D:(j + 1) * D].astype(F32))
            dy_ref[...] = (dm * sig).astype(dy_ref.dtype)
            dgt_ref[:, j * D:(j + 1) * D] = (dm * y_ref[...].astype(F32) * sig * (1.0 - sig)).astype(dgt_ref.dtype)

    return pl.pallas_call(
        body, grid=(s // ts,), in_specs=[_rows(ts, D), _rows(ts, 3 * D)] + [_rows(ts, D)] * 3,
        out_specs=[_rows(ts, D)] * 3 + [_rows(ts, 3 * D)],
        out_shape=[jax.ShapeDtypeStruct((s, D), BF16)] * 3 + [jax.ShapeDtypeStruct((s, 3 * D), BF16)],
        compiler_params=_params(("parallel",)), name=name,
    )(dmerged, pgt, ya, yb, yc)


def _resident(shape):
    nd = len(shape)
    return pl.BlockSpec(shape, lambda i: (0,) * nd, pipeline_mode=pl.Buffered(1))


def _mix_tail_fwd(ya_in, yb_in, xc, pgt, w_ups, w_o, x, g_post, g_next, after, name, tm=512):
    s, d = x.shape
    tm = _tile(s, tm)
    na = len(after)
    branch_ins = (ya_in, yb_in, xc)

    def body(a_ref, b_ref, c_ref, gt_ref, wa_ref, wb_ref, wc_ref, wo_ref, x_ref, gp_ref, gn_ref, *rest):
        ya_ref, yb_ref, yc_ref, m_ref, y_ref, xo_ref, h_ref = rest[na:]
        for rows in _sub_blocks(tm):
            merged = None
            for j, (in_ref, w_ref, out_ref) in enumerate(((a_ref, wa_ref, ya_ref), (b_ref, wb_ref, yb_ref), (c_ref, wc_ref, yc_ref))):
                yj = jnp.dot(in_ref[rows, :], w_ref[...], preferred_element_type=F32)
                out_ref[rows, :] = yj.astype(out_ref.dtype)
                part = _sigmoid(gt_ref[rows, j * D:(j + 1) * D].astype(F32)) * yj
                merged = part if merged is None else merged + part
            merged_b = merged.astype(m_ref.dtype)
            m_ref[rows, :] = merged_b
            y = jnp.dot(merged_b, wo_ref[...], preferred_element_type=F32)
            y_ref[rows, :] = y
            yh, _ = _rms(y)
            xn = x_ref[rows, :] + yh * gp_ref[...]
            xo_ref[rows, :] = xn
            xh, _ = _rms(xn)
            h_ref[rows, :] = (xh * gn_ref[...]).astype(h_ref.dtype)

    bf = lambda: jax.ShapeDtypeStruct((s, d), BF16)
    f32 = lambda: jax.ShapeDtypeStruct((s, d), F32)
    return pl.pallas_call(
        body, grid=(s // tm,),
        in_specs=[_rows(tm, a.shape[1]) for a in branch_ins] + [_rows(tm, 3 * d)] + [_resident(w.shape) for w in w_ups]
        + [_resident(w_o.shape), _rows(tm, d), _fixed((1, d)), _fixed((1, d))] + [ANY] * na,
        out_specs=[_rows(tm, d)] * 7,
        out_shape=[bf(), bf(), bf(), bf(), f32(), f32(), bf()],
        compiler_params=_params(("parallel",)), name=name,
    )(*branch_ins, pgt, *w_ups, w_o, x, g_post, g_next, *after)


def _mix_tail_bwd(dy, pgt, ys, w_ups, w_o, after, name, tm=512):
    s, d = dy.shape
    tm = _tile(s, tm)
    na = len(after)
    widths = [w.shape[0] for w in w_ups]

    def body(dy_ref, gt_ref, ya_ref, yb_ref, yc_ref, wa_ref, wb_ref, wc_ref, wo_ref, *rest):
        dya_ref, dyb_ref, dyc_ref, dgt_ref, da_ref, db_ref, dc_ref = rest[na:]
        nt = (((1,), (1,)), ((), ()))
        for rows in _sub_blocks(tm):
            dm = lax.dot_general(dy_ref[rows, :], wo_ref[...], nt, preferred_element_type=F32)
            for j, (y_ref, dyj_ref, w_ref, din_ref) in enumerate(((ya_ref, dya_ref, wa_ref, da_ref), (yb_ref, dyb_ref, wb_ref, db_ref),
                                                                   (yc_ref, dyc_ref, wc_ref, dc_ref))):
                sig = _sigmoid(gt_ref[rows, j * D:(j + 1) * D].astype(F32))
                dyj = (dm * sig).astype(dyj_ref.dtype)
                dyj_ref[rows, :] = dyj
                dgt_ref[rows, j * D:(j + 1) * D] = (dm * y_ref[rows, :].astype(F32) * sig * (1.0 - sig)).astype(dgt_ref.dtype)
                din_ref[rows, :] = lax.dot_general(dyj, w_ref[...], nt, preferred_element_type=F32).astype(din_ref.dtype)

    bf = lambda w: jax.ShapeDtypeStruct((s, w), BF16)
    return pl.pallas_call(
        body, grid=(s // tm,),
        in_specs=[_rows(tm, d), _rows(tm, 3 * d)] + [_rows(tm, d)] * 3 + [_resident(w.shape) for w in w_ups] + [_resident(w_o.shape)]
        + [ANY] * na,
        out_specs=[_rows(tm, d)] * 3 + [_rows(tm, 3 * d)] + [_rows(tm, w) for w in widths],
        out_shape=[bf(d), bf(d), bf(d), bf(3 * d)] + [bf(w) for w in widths],
        compiler_params=_params(("parallel",)), name=name,
    )(dy, pgt, *ys, *w_ups, w_o, *after)


def _adam_math(w, g, m, v):
    mn = ADAM_B1 * m + (1.0 - ADAM_B1) * g
    vn = ADAM_B2 * v + (1.0 - ADAM_B2) * (g * g)
    m_hat = mn / (1.0 - ADAM_B1 ** ADAM_STEP)
    v_hat = vn / (1.0 - ADAM_B2 ** ADAM_STEP)
    return -ADAM_LR * (m_hat / (jnp.sqrt(v_hat) + ADAM_EPS) + ADAM_WD * w), mn, vn


def _adamw(w, g, m, v, name):
    r, c = w.shape[-2:]
    tr, tc = _block_of(r, c, cap=512 if r % 16 == 0 else 256)

    def spec(a):
        if a.ndim == 2:
            return pl.BlockSpec((tr, tc), lambda i, j: (i, j))
        return pl.BlockSpec((None, tr, tc), lambda i, j: (0, i, j))

    def body(w_ref, g_ref, m_ref, v_ref, d_ref, mo_ref, vo_ref):
        d_ref[...], mo_ref[...], vo_ref[...] = _adam_math(w_ref[...], g_ref[...], m_ref[...], v_ref[...])

    return pl.pallas_call(
        body, grid=(r // tr, c // tc), in_specs=[spec(a) for a in (w, g, m, v)], out_specs=[spec(w)] * 3,
        out_shape=[jax.ShapeDtypeStruct(w.shape, F32)] * 3, compiler_params=_params(("parallel", "parallel")), name=name,
    )(w, g, m, v)


ANY = pl.BlockSpec(memory_space=pl.ANY)


def _place():
    x, y, c = lax.axis_index("x"), lax.axis_index("y"), lax.axis_index("c")
    chips = [(1 - x, y), (x, 1 - y), (1 - x, 1 - y)]
    return x, y, c, chips


def _half(c, rows):
    h = rows // 2
    return pl.ds(pl.multiple_of(c * h, 8), h)


def _by_cols(rows):
    return rows % 32 != 0 and rows != 16


def _half_of(ref, lead, c):
    r, cols = ref.shape[-2:]
    if _by_cols(r):
        return ref.at[(*lead, slice(None), pl.ds(pl.multiple_of(c * (cols // 2), LANE), cols // 2))]
    return ref.at[(*lead, pl.ds(pl.multiple_of(c * (r // 2), 8), r // 2))]


def _half_shape(shape):
    r, cols = shape[-2:]
    return shape[:-2] + ((r, cols // 2) if _by_cols(r) else (r // 2, cols))


def _block_of(r, cols, cap=256):
    if r % 16 == 0:
        return _tile(r, cap, 16), cols
    return r, _tile(cols, cap)


def _place_shard(shard, chip_arr, out_dtype, name, after=()):
    _, r, cols = shard.shape
    tr, tc = _block_of(r, cols)

    def body(chip_ref, s_ref, *rest):
        o_ref = rest[len(after)]
        o_ref[...] = s_ref[...].astype(o_ref.dtype)

    return pl.pallas_call(
        body,
        grid_spec=pltpu.PrefetchScalarGridSpec(
            num_scalar_prefetch=1, grid=(r // tr, cols // tc),
            in_specs=[pl.BlockSpec((None, tr, tc), lambda i, j, chip_ref: (0, i, j))] + [ANY] * len(after),
            out_specs=pl.BlockSpec((None, tr, tc), lambda i, j, chip_ref: (chip_ref[0], i, j))),
        out_shape=jax.ShapeDtypeStruct((4, r, cols), out_dtype),
        compiler_params=_params(("parallel", "parallel")), name=name,
    )(chip_arr, shard, *after)


def _gather_shards(bufs, name):
    n = len(bufs)

    def body(*refs):
        outs = refs[n:2 * n]
        send_ici, recv_ici, send_d2d, recv_d2d = refs[2 * n:]
        x, y, c, chips = _place()
        me = 2 * x + y
        sibling = (x, y, 1 - c)

        def ici(w, p, chip_of_block, to):
            rows = _half(c, outs[w].shape[1])
            block = outs[w].at[chip_of_block, rows]
            return pltpu.make_async_remote_copy(
                src_ref=block, dst_ref=block, send_sem=send_ici.at[w, p], recv_sem=recv_ici.at[w, p], device_id=to, device_id_type=MESH)

        def d2d(w, p, chip_of_block, half_of):
            rows = _half(half_of, outs[w].shape[1])
            block = outs[w].at[chip_of_block, rows]
            return pltpu.make_async_remote_copy(
                src_ref=block, dst_ref=block, send_sem=send_d2d.at[w, p], recv_sem=recv_d2d.at[w, p], device_id=sibling, device_id_type=MESH)

        sends = [ici(w, p, me, (*chip, c)) for p, chip in enumerate(chips) for w in range(n)]
        for cp in sends:
            cp.start()
        passed = []
        for p, (px, py) in enumerate(chips):
            for w in range(n):
                ici(w, p, 2 * px + py, (px, py, c)).wait_recv()
                fwd = d2d(w, p, 2 * px + py, c)
                fwd.start()
                passed.append(fwd)
        for p, (px, py) in enumerate(chips):
            for w in range(n):
                d2d(w, p, 2 * px + py, 1 - c).wait_recv()
        for cp in sends + passed:
            cp.wait_send()

    return pl.pallas_call(
        body, in_specs=[ANY] * n, out_specs=[ANY] * n,
        out_shape=[jax.ShapeDtypeStruct(a.shape, a.dtype) for a in bufs],
        input_output_aliases={w: w for w in range(n)},
        scratch_shapes=[pltpu.SemaphoreType.DMA((n, 3))] * 4,
        compiler_params=pltpu.CompilerParams(has_side_effects=True), name=name,
    )(*bufs)


HBM = pl.BlockSpec(memory_space=pltpu.HBM)
SEM = pl.BlockSpec(memory_space=pltpu.SEMAPHORE)
EFFECT = pltpu.SideEffectType.DATAFLOW_SIDE_EFFECTING


def _in_hbm(arrays):
    return [pltpu.with_memory_space_constraint(a, pltpu.HBM) for a in arrays]


def _gather_start(bufs, after, name):
    n, na = len(bufs), len(after)

    def body(*refs):
        send_sem, recv_sem = refs[n + na], refs[n + na + 1]
        outs = refs[n + na + 2:2 * n + na + 2]
        token = refs[2 * n + na + 2]
        x, y, c, chips = _place()
        me = 2 * x + y
        for p, chip in enumerate(chips):
            for w in range(n):
                block = _half_of(outs[w], (me,), c)
                pltpu.make_async_remote_copy(
                    src_ref=block, dst_ref=block, send_sem=send_sem, recv_sem=recv_sem,
                    device_id=(*chip, c), device_id_type=MESH).start()
        token[...] = jnp.zeros_like(token)

    out = pl.pallas_call(
        body, name=name, in_specs=[HBM] * n + [ANY] * na,
        out_specs=[SEM, SEM] + [HBM] * n + [pl.BlockSpec(memory_space=pltpu.VMEM)],
        out_shape=[pltpu.SemaphoreType.DMA(()), pltpu.SemaphoreType.DMA(())]
        + [pltpu.HBM(a.shape, a.dtype) for a in bufs] + [jax.ShapeDtypeStruct((8, LANE), F32)],
        input_output_aliases={w: w + 2 for w in range(n)},
        compiler_params=pltpu.CompilerParams(has_side_effects=EFFECT),
    )(*_in_hbm(bufs), *after)
    return out[0], out[1], list(out[2:2 + n]), out[2 + n]


def _gather_pass(bufs, send_sem, recv_sem, after, name):
    n, na = len(bufs), len(after)

    def body(*refs):
        send1, recv1 = refs[n], refs[n + 1]
        send2, recv2 = refs[n + 2 + na], refs[n + 3 + na]
        outs = refs[n + 4 + na:2 * n + 4 + na]
        x, y, c, chips = _place()
        me = 2 * x + y
        arrivals = [(w, px, py) for px, py in chips for w in range(n)]
        for w, px, py in arrivals:
            first = pltpu.make_async_remote_copy(
                src_ref=_half_of(outs[w], (me,), c), dst_ref=_half_of(outs[w], (2 * px + py,), c), send_sem=send1, recv_sem=recv1,
                device_id=(px, py, c), device_id_type=MESH)
            first.wait_send()
            first.wait_recv()
        for w, px, py in arrivals:
            arrived = _half_of(outs[w], (2 * px + py,), c)
            pltpu.make_async_remote_copy(
                src_ref=arrived, dst_ref=arrived, send_sem=send2, recv_sem=recv2,
                device_id=(x, y, 1 - c), device_id_type=MESH).start()

    out = pl.pallas_call(
        body, name=name, in_specs=[HBM] * n + [SEM, SEM] + [ANY] * na,
        out_specs=[SEM, SEM] + [HBM] * n,
        out_shape=[pltpu.SemaphoreType.DMA(()), pltpu.SemaphoreType.DMA(())] + [pltpu.HBM(a.shape, a.dtype) for a in bufs],
        input_output_aliases={w: w + 2 for w in range(n)},
        compiler_params=pltpu.CompilerParams(has_side_effects=EFFECT),
    )(*bufs, send_sem, recv_sem, *after)
    return out[0], out[1], list(out[2:])


def _gather_finish(bufs, send_sem, recv_sem, after, name):
    n, na = len(bufs), len(after)

    def body(*refs):
        send2, recv2 = refs[n], refs[n + 1]
        outs = refs[n + 2 + na:2 * n + 2 + na]
        x, y, c, chips = _place()
        for p, (px, py) in enumerate(chips):
            for w in range(n):
                passed = pltpu.make_async_remote_copy(
                    src_ref=_half_of(outs[w], (2 * px + py,), c), dst_ref=_half_of(outs[w], (2 * px + py,), 1 - c),
                    send_sem=send2, recv_sem=recv2, device_id=(x, y, 1 - c), device_id_type=MESH)
                passed.wait_send()
                passed.wait_recv()

    out = pl.pallas_call(
        body, name=name, in_specs=[HBM] * n + [SEM, SEM] + [ANY] * na, out_specs=[HBM] * n,
        out_shape=[pltpu.HBM(a.shape, a.dtype) for a in bufs],
        input_output_aliases={w: w for w in range(n)},
        compiler_params=pltpu.CompilerParams(has_side_effects=EFFECT),
    )(*bufs, send_sem, recv_sem, *after)
    return list(out)


def _pair_exchange(grads, name):
    n = len(grads)

    def body(*refs):
        ins, outs = refs[:n], refs[n:2 * n]
        send_sem, recv_sem = refs[2 * n:]
        x, y, c, _ = _place()
        copies = []
        for w in range(n):
            copies.append(pltpu.make_async_remote_copy(
                src_ref=_half_of(ins[w], (slice(None),), 1 - c), dst_ref=outs[w], send_sem=send_sem.at[w], recv_sem=recv_sem.at[w],
                device_id=(x, y, 1 - c), device_id_type=MESH))
        for cp in copies:
            cp.start()
        for cp in copies:
            cp.wait()

    return pl.pallas_call(
        body, in_specs=[ANY] * n, out_specs=[ANY] * n,
        out_shape=[jax.ShapeDtypeStruct(_half_shape(a.shape), a.dtype) for a in grads],
        scratch_shapes=[pltpu.SemaphoreType.DMA((n,))] * 2,
        compiler_params=pltpu.CompilerParams(has_side_effects=True), name=name,
    )(*grads)


def _pair_exchange_start(grads, after, name):
    n, na = len(grads), len(after)
    lands = [lax.empty(_half_shape(a.shape), a.dtype) for a in grads]

    def body(*refs):
        send_sem, recv_sem = refs[2 * n + na], refs[2 * n + na + 1]
        srcs = refs[2 * n + na + 2:3 * n + na + 2]
        dsts = refs[3 * n + na + 2:4 * n + na + 2]
        token = refs[4 * n + na + 2]
        x, y, c, _ = _place()
        for w in range(n):
            pltpu.make_async_remote_copy(
                src_ref=_half_of(srcs[w], (slice(None),), 1 - c), dst_ref=dsts[w], send_sem=send_sem, recv_sem=recv_sem,
                device_id=(x, y, 1 - c), device_id_type=MESH).start()
        token[...] = jnp.zeros_like(token)

    out = pl.pallas_call(
        body, name=name, in_specs=[HBM] * (2 * n) + [ANY] * na,
        out_specs=[SEM, SEM] + [HBM] * (2 * n) + [pl.BlockSpec(memory_space=pltpu.VMEM)],
        out_shape=[pltpu.SemaphoreType.DMA(()), pltpu.SemaphoreType.DMA(())]
        + [pltpu.HBM(a.shape, a.dtype) for a in grads + lands] + [jax.ShapeDtypeStruct((8, LANE), F32)],
        input_output_aliases={w: w + 2 for w in range(2 * n)},
        compiler_params=pltpu.CompilerParams(has_side_effects=EFFECT),
    )(*_in_hbm(grads), *_in_hbm(lands), *after)
    return out[0], out[1], list(out[2:2 + n]), list(out[2 + n:2 + 2 * n]), out[2 + 2 * n]


def _pair_exchange_finish(grads, lands, send_sem, recv_sem, after, name):
    n, na = len(grads), len(after)

    def body(*refs):
        send, recv = refs[2 * n], refs[2 * n + 1]
        srcs = refs[2 * n + 2 + na:3 * n + 2 + na]
        dsts = refs[3 * n + 2 + na:4 * n + 2 + na]
        x, y, c, _ = _place()
        for w in range(n):
            copy = pltpu.make_async_remote_copy(
                src_ref=_half_of(srcs[w], (slice(None),), 1 - c), dst_ref=dsts[w], send_sem=send, recv_sem=recv,
                device_id=(x, y, 1 - c), device_id_type=MESH)
            copy.wait_send()
            copy.wait_recv()

    out = pl.pallas_call(
        body, name=name, in_specs=[HBM] * (2 * n) + [SEM, SEM] + [ANY] * na, out_specs=[HBM] * (2 * n),
        out_shape=[pltpu.HBM(a.shape, a.dtype) for a in grads + lands],
        input_output_aliases={w: w for w in range(2 * n)},
        compiler_params=pltpu.CompilerParams(has_side_effects=EFFECT),
    )(*grads, *lands, send_sem, recv_sem, *after)
    return list(out[:n]), list(out[n:])


def _pair_sum(g, got, c_arr, name):
    _, r, cols = g.shape
    hr, hc = _half_shape((r, cols))
    tr, tc = _block_of(hr, hc)
    nbr, nbc = hr // tr, hc // tc
    by_cols = _by_cols(r)

    def body(c_ref, g_ref, got_ref, o_ref):
        o_ref[...] = (g_ref[...].astype(F32) + got_ref[...].astype(F32)).astype(o_ref.dtype)

    def mine(j, i, k, c_ref):
        return (j, i, c_ref[0] * nbc + k) if by_cols else (j, c_ref[0] * nbr + i, k)

    return pl.pallas_call(
        body,
        grid_spec=pltpu.PrefetchScalarGridSpec(
            num_scalar_prefetch=1, grid=(4, nbr, nbc),
            in_specs=[pl.BlockSpec((None, tr, tc), mine),
                      pl.BlockSpec((None, tr, tc), lambda j, i, k, c_ref: (j, i, k))],
            out_specs=pl.BlockSpec((None, tr, tc), lambda j, i, k, c_ref: (j, i, k))),
        out_shape=jax.ShapeDtypeStruct((4, hr, hc), BF16),
        compiler_params=_params(("parallel", "parallel", "parallel")), name=name,
    )(c_arr, g, got)


def _chip_exchange(parts, name):
    n = len(parts)

    def body(*refs):
        ins, outs = refs[:n], refs[n:2 * n]
        send_sem, recv_sem = refs[2 * n:]
        x, y, c, chips = _place()
        copies = []
        for p, (px, py) in enumerate(chips):
            for w in range(n):
                copies.append(pltpu.make_async_remote_copy(
                    src_ref=ins[w].at[2 * px + py], dst_ref=outs[w].at[p], send_sem=send_sem.at[w, p], recv_sem=recv_sem.at[w, p],
                    device_id=(px, py, c), device_id_type=MESH))
        for cp in copies:
            cp.start()
        for cp in copies:
            cp.wait()

    return pl.pallas_call(
        body, in_specs=[ANY] * n, out_specs=[ANY] * n,
        out_shape=[jax.ShapeDtypeStruct((3,) + a.shape[1:], a.dtype) for a in parts],
        scratch_shapes=[pltpu.SemaphoreType.DMA((n, 3))] * 2,
        compiler_params=pltpu.CompilerParams(has_side_effects=True), name=name,
    )(*parts)


def _chip_exchange_start(parts, after, name):
    n, na = len(parts), len(after)
    lands = [lax.empty((3,) + a.shape[1:], a.dtype) for a in parts]

    def body(*refs):
        send_sem, recv_sem = refs[2 * n + na], refs[2 * n + na + 1]
        srcs = refs[2 * n + na + 2:3 * n + na + 2]
        dsts = refs[3 * n + na + 2:4 * n + na + 2]
        token = refs[4 * n + na + 2]
        x, y, c, chips = _place()
        for p, (px, py) in enumerate(chips):
            for w in range(n):
                pltpu.make_async_remote_copy(
                    src_ref=srcs[w].at[2 * px + py], dst_ref=dsts[w].at[p], send_sem=send_sem, recv_sem=recv_sem,
                    device_id=(px, py, c), device_id_type=MESH).start()
        token[...] = jnp.zeros_like(token)

    out = pl.pallas_call(
        body, name=name, in_specs=[HBM] * (2 * n) + [ANY] * na,
        out_specs=[SEM, SEM] + [HBM] * (2 * n) + [pl.BlockSpec(memory_space=pltpu.VMEM)],
        out_shape=[pltpu.SemaphoreType.DMA(()), pltpu.SemaphoreType.DMA(())]
        + [pltpu.HBM(a.shape, a.dtype) for a in parts + lands] + [jax.ShapeDtypeStruct((8, LANE), F32)],
        input_output_aliases={w: w + 2 for w in range(2 * n)},
        compiler_params=pltpu.CompilerParams(has_side_effects=EFFECT),
    )(*_in_hbm(parts), *_in_hbm(lands), *after)
    return out[0], out[1], list(out[2:2 + n]), list(out[2 + n:2 + 2 * n]), out[2 + 2 * n]


def _chip_exchange_finish(parts, lands, send_sem, recv_sem, after, name):
    n, na = len(parts), len(after)

    def body(*refs):
        send, recv = refs[2 * n], refs[2 * n + 1]
        srcs = refs[2 * n + 2 + na:3 * n + 2 + na]
        dsts = refs[3 * n + 2 + na:4 * n + 2 + na]
        x, y, c, chips = _place()
        for p, (px, py) in enumerate(chips):
            for w in range(n):
                copy = pltpu.make_async_remote_copy(
                    src_ref=srcs[w].at[2 * px + py], dst_ref=dsts[w].at[p], send_sem=send, recv_sem=recv,
                    device_id=(px, py, c), device_id_type=MESH)
                copy.wait_send()
                copy.wait_recv()

    out = pl.pallas_call(
        body, name=name, in_specs=[HBM] * (2 * n) + [SEM, SEM] + [ANY] * na, out_specs=[HBM] * (2 * n),
        out_shape=[pltpu.HBM(a.shape, a.dtype) for a in parts + lands],
        input_output_aliases={w: w for w in range(2 * n)},
        compiler_params=pltpu.CompilerParams(has_side_effects=EFFECT),
    )(*parts, *lands, send_sem, recv_sem, *after)
    return list(out[:n]), list(out[n:])


def _chip_sum(part, got, place_arr, name):
    _, hr, hc = part.shape
    by_cols = _by_cols(hr)
    tr, tc = _block_of(hr, hc)
    nbr, nbc = hr // tr, hc // tc

    def body(place_ref, p_ref, got_ref, o_ref):
        acc = p_ref[...].astype(F32)
        for p in range(3):
            acc = acc + got_ref[p].astype(F32)
        o_ref[...] = acc

    def mine(i, k, place_ref):
        return (i, place_ref[1] * nbc + k) if by_cols else (place_ref[1] * nbr + i, k)

    return pl.pallas_call(
        body,
        grid_spec=pltpu.PrefetchScalarGridSpec(
            num_scalar_prefetch=1, grid=(nbr, nbc),
            in_specs=[pl.BlockSpec((None, tr, tc), lambda i, k, place_ref: (place_ref[0], i, k)),
                      pl.BlockSpec((3, tr, tc), lambda i, k, place_ref: (0, i, k))],
            out_specs=pl.BlockSpec((tr, tc), mine)),
        out_shape=jax.ShapeDtypeStruct((hr, 2 * hc) if by_cols else (2 * hr, hc), F32),
        compiler_params=_params(("parallel", "parallel")), name=name,
    )(place_arr, part, got)


def _pair_join_start(bufs, name):
    n = len(bufs)

    def body(*refs):
        send_sem, recv_sem = refs[n], refs[n + 1]
        outs = refs[n + 2:2 * n + 2]
        token = refs[2 * n + 2]
        x, y, c, _ = _place()
        for w in range(n):
            block = _half_of(outs[w], (), c)
            pltpu.make_async_remote_copy(
                src_ref=block, dst_ref=block, send_sem=send_sem, recv_sem=recv_sem,
                device_id=(x, y, 1 - c), device_id_type=MESH).start()
        token[...] = jnp.zeros_like(token)

    out = pl.pallas_call(
        body, name=name, in_specs=[HBM] * n,
        out_specs=[SEM, SEM] + [HBM] * n + [pl.BlockSpec(memory_space=pltpu.VMEM)],
        out_shape=[pltpu.SemaphoreType.DMA(()), pltpu.SemaphoreType.DMA(())]
        + [pltpu.HBM(a.shape, a.dtype) for a in bufs] + [jax.ShapeDtypeStruct((8, LANE), F32)],
        input_output_aliases={w: w + 2 for w in range(n)},
        compiler_params=pltpu.CompilerParams(has_side_effects=EFFECT),
    )(*_in_hbm(bufs))
    return out[0], out[1], list(out[2:2 + n]), out[2 + n]


def _pair_join_finish(bufs, send_sem, recv_sem, after, name):
    n, na = len(bufs), len(after)

    def body(*refs):
        send, recv = refs[n], refs[n + 1]
        outs = refs[n + 2 + na:2 * n + 2 + na]
        x, y, c, _ = _place()
        for w in range(n):
            copy = pltpu.make_async_remote_copy(
                src_ref=_half_of(outs[w], (), c), dst_ref=_half_of(outs[w], (), 1 - c), send_sem=send, recv_sem=recv,
                device_id=(x, y, 1 - c), device_id_type=MESH)
            copy.wait_send()
            copy.wait_recv()

    out = pl.pallas_call(
        body, name=name, in_specs=[HBM] * n + [SEM, SEM] + [ANY] * na, out_specs=[HBM] * n,
        out_shape=[pltpu.HBM(a.shape, a.dtype) for a in bufs],
        input_output_aliases={w: w for w in range(n)},
        compiler_params=pltpu.CompilerParams(has_side_effects=EFFECT),
    )(*bufs, send_sem, recv_sem, *after)
    return list(out)


SMALL = ("ffn1_pre_g", "ffn1_post_g", "mix_pre_g", "gla_norm_g", "mem_norm_g", "mix_post_g", "ffn2_pre_g", "ffn2_post_g", "final_g",
         "b_f", "pool_scale", "w_pool", "w_fu")
N_GAINS = 9
SMALL_PACKS = ((16, D), (24, 512), (4 * LANE, LANE))
W_FU_ROW = 8


def _all_sum_small(gs, name, after=()):
    ins = [gs[n] for n in SMALL[:N_GAINS]] + [gs["b_f"], gs["pool_scale"], gs["w_fu_pad"], gs["w_pool"].reshape(4 * LANE, LANE)]

    def body(*refs):
        gain_refs = refs[:N_GAINS]
        bf_ref, ps_ref, wfu_ref, wp_ref = refs[N_GAINS:N_GAINS + 4]
        outs = refs[N_GAINS + 4 + len(after):N_GAINS + 7 + len(after)]
        mine_a, mine_b, all_a, all_b, all_c, send_sems, recv_sems = refs[N_GAINS + 7 + len(after):]
        mine_a[...] = jnp.zeros_like(mine_a)
        for i, ref in enumerate(gain_refs):
            mine_a[i:i + 1, :] = ref[...]
        mine_b[...] = jnp.zeros_like(mine_b)
        mine_b[0:1, :] = bf_ref[...]
        mine_b[1:2, :] = ps_ref[...]
        mine_b[W_FU_ROW:W_FU_ROW + GATE_RANK, :] = wfu_ref[0:GATE_RANK, :]
        packs = ((mine_a, all_a), (mine_b, all_b), (wp_ref, all_c))
        x, y, c, chips = _place()
        me, sibling = (x, y, c), (x, y, 1 - c)

        def copy(t, k, block, to, own=False):
            px, py, pc = block
            slot = packs[t][1].at[4 * px + 2 * py + pc]
            return pltpu.make_async_remote_copy(
                src_ref=packs[t][0] if own else slot, dst_ref=slot,
                send_sem=send_sems.at[t, k], recv_sem=recv_sems.at[t, k], device_id=to, device_id_type=MESH)

        started = []
        for t, (mine, everyone) in enumerate(packs):
            everyone[4 * x + 2 * y + c] = mine[...]
            started.append(copy(t, 0, me, sibling, own=True))
            started += [copy(t, 1 + j, me, (*chip, c), own=True) for j, chip in enumerate(chips)]
        for cp in started:
            cp.start()
        passed = []
        for j, chip in enumerate(chips):
            for t in range(len(packs)):
                copy(t, 1 + j, (*chip, c), me).wait_recv()
                fwd = copy(t, 4 + j, (*chip, c), sibling)
                fwd.start()
                passed.append(fwd)
        for t in range(len(packs)):
            copy(t, 0, sibling, me).wait_recv()
            for j, chip in enumerate(chips):
                copy(t, 4 + j, (*chip, 1 - c), me).wait_recv()
        for cp in started + passed:
            cp.wait_send()
        for (_, everyone), o_ref in zip(packs, outs):
            acc = everyone[0]
            for k in range(1, 8):
                acc = acc + everyone[k]
            o_ref[...] = acc

    vmem = pl.BlockSpec(memory_space=pltpu.VMEM)
    return pl.pallas_call(
        body, in_specs=[vmem] * len(ins) + [ANY] * len(after), out_specs=[vmem] * 3,
        out_shape=[jax.ShapeDtypeStruct(shape, F32) for shape in SMALL_PACKS],
        scratch_shapes=[pltpu.VMEM(SMALL_PACKS[0], F32), pltpu.VMEM(SMALL_PACKS[1], F32)]
        + [pltpu.VMEM((8,) + shape, F32) for shape in SMALL_PACKS]
        + [pltpu.SemaphoreType.DMA((3, 7)), pltpu.SemaphoreType.DMA((3, 7))],
        compiler_params=pltpu.CompilerParams(has_side_effects=True, vmem_limit_bytes=VMEM_LIMIT), name=name,
    )(*ins, *after)


def _adamw_small(sums, params, chip_arr, name):
    flat = [a for n in SMALL for a in params[n]]

    def body(chip_ref, a_ref, b_ref, c_ref, *refs):
        ins, outs = refs[:len(flat)], refs[len(flat):]
        for i, n in enumerate(SMALL):
            w_ref, m_ref, v_ref = ins[3 * i:3 * i + 3]
            g_ref, d_ref, mo_ref, vo_ref = outs[4 * i:4 * i + 4]
            if n == "w_pool":
                pieces = [((0, k), c_ref[k * LANE:(k + 1) * LANE, :]) for k in range(4)]
            elif n == "w_fu":
                mine = pl.ds(pl.multiple_of(chip_ref[0] * LANE, LANE), LANE)
                pieces = [((0,), b_ref[W_FU_ROW:W_FU_ROW + GATE_RANK, mine])]
            elif n == "b_f":
                pieces = [((), b_ref[0:1, :])]
            elif n == "pool_scale":
                pieces = [((), b_ref[1:2, :])]
            else:
                pieces = [((), a_ref[i:i + 1, :])]
            for at, g in pieces:
                d, mn, vn = _adam_math(w_ref[at], g, m_ref[at], v_ref[at])
                g_ref[at] = g
                d_ref[at] = d
                mo_ref[at] = mn
                vo_ref[at] = vn

    def whole(shape):
        return pl.BlockSpec(shape, lambda i, chip_ref: (0,) * len(shape))

    out = pl.pallas_call(
        body,
        grid_spec=pltpu.PrefetchScalarGridSpec(
            num_scalar_prefetch=1, grid=(1,),
            in_specs=[whole(a.shape) for a in list(sums) + flat],
            out_specs=[whole(params[n][0].shape) for n in SMALL for _ in range(4)]),
        out_shape=[jax.ShapeDtypeStruct(params[n][0].shape, F32) for n in SMALL for _ in range(4)],
        compiler_params=_params(("arbitrary",)), name=name,
    )(chip_arr, *sums, *flat)
    return {n: tuple(out[4 * i:4 * i + 4]) for i, n in enumerate(SMALL)}


def _ffn_bwd(dz, x_norm, ab, u, w_in, w_out, x, g_pre, dres, tag, emit, advance, after=()):
    dw_out = _mm(u, dz, ta=True, out_dtype=BF16, tm=1408, tk=2048, after=after, name=tag + "_out_dw")
    behind = emit(tag + "_w_out", dw_out)
    dab = _ffn_out_dx_swiglu(dz, w_out, ab, behind, name=tag + "_out_dx")
    behind = advance((dab,))
    dw_in = _mm(x_norm, dab, ta=True, out_dtype=BF16, tm=512, tk=4096, shards=4, after=behind, name=tag + "_in_dw")
    behind = emit(tag + "_w_in", dw_in)
    dx, dg = _mm_rms_bwd([(dab, w_in)], x, g_pre, dres, after=behind, name=tag + "_in_dx")
    return dx, dg, advance((dx,))


def _local_step(x, mem, target, small, gather, emit, advance):
    behind = gather("start", "ffn1i", ())
    behind = gather("start", "ffn1o", behind)
    h1 = _norm_fwd(x, small["ffn1_pre_g"], BF16, name="ffn1_pre", after=behind)
    gather("pass", "ffn1i", (h1,))
    big = gather("finish", "ffn1i", ())
    behind = gather("start", "mixa", (big["ffn1_w_in"],))
    behind = gather("start", "mixb", behind)
    behind = gather("start", "ffn2", behind)
    ab1, u1 = _ffn_in_swiglu(h1, big["ffn1_w_in"], name="ffn1_in", after=behind)
    gather("pass", "ffn1o", (ab1,))
    big.update(gather("finish", "ffn1o", ()))
    behind = gather("pass", "mixa", (u1,))
    f1, x1, h = _mm_resid_norm(u1, big["ffn1_w_out"], x, small["ffn1_post_g"], 0.5, small["mix_pre_g"], name="ffn1_out", after=behind)
    big.update(gather("finish", "mixa", (h,)))
    small = dict(small, w_fu_pad=big["w_fu_pad"])
    pg = _mm(h, big["w_gla_t"], tb=True, out_dtype=BF16, tm=1024, tn=PG_W, name="mix_in_gla")
    behind = gather("pass", "mixb", (pg,))
    ppx = _mm(h, big["w_px_t"], tb=True, out_dtype=BF16, after=behind, name="mix_in_px")
    pgt = _mm(h, big["w_gates_t"], tb=True, out_dtype=BF16, tn=1536, name="mix_in_gates")
    big.update(gather("finish", "mixb", (pgt,)))
    mem_n = _norm_fwd(mem, small["mem_norm_g"], BF16, name="mem_norm")
    kv = _mm(mem_n, big["w_mem_kv"], out_dtype=BF16, name="mem_kv")
    ya_in, sp, so, o_gla = _gla_fwd(pg, small["w_fu_pad"], small["b_f"], small["gla_norm_g"], name="gla_fwd")
    yb_in = _pool_fwd(ppx, small["w_pool_b"], small["pool_scale"], name="pool_fwd")
    xc = _xattn_fwd(ppx, kv, name="xattn_fwd")
    behind = gather("pass", "ffn2", (xc,))
    w_ups = [big["w_up_gla"], big["w_up_pool"], big["w_up_xattn"]]
    ya, yb, yc, merged, ymix, x2, h2 = _mix_tail_fwd(ya_in, yb_in, xc, pgt, w_ups, big["w_o"], x1, small["mix_post_g"],
                                                     small["ffn2_pre_g"], behind, name="mix_tail")
    big.update(gather("finish", "ffn2", (h2,)))
    ab2, u2 = _ffn_in_swiglu(h2, big["ffn2_w_in"], name="ffn2_in")
    f2, x3, _ = _mm_resid_norm(u2, big["ffn2_w_out"], x2, small["ffn2_post_g"], 0.5, None, name="ffn2_out")
    gs = {}
    dx3, gs["final_g"], loss = _loss_bwd(x3, small["final_g"], target, name="loss")
    dz2, gs["ffn2_post_g"] = _rms_bwd(f2, small["ffn2_post_g"], [dx3], None, 0.5, BF16, name="ffn2_post_bwd")
    dx2, gs["ffn2_pre_g"], behind = _ffn_bwd(dz2, h2, ab2, u2, big["ffn2_w_in"], big["ffn2_w_out"], x2, small["ffn2_pre_g"], dx3,
                                             "ffn2", emit, advance)
    dy, gs["mix_post_g"] = _rms_bwd(ymix, small["mix_post_g"], [dx2], None, 1.0, BF16, name="mix_post_bwd", after=behind)
    emit("w_o", _mm(merged, dy, ta=True, out_dtype=BF16, tm=512, tk=4096, name="mix_out_dw"))
    dya, dyb, dyc, dgt, dya_in, dyb_in, dxc = _mix_tail_bwd(dy, pgt, (ya, yb, yc), w_ups, big["w_o"], (), name="mix_tail_bwd")
    emit("w_up_gla", _mm(ya_in, dya, ta=True, out_dtype=BF16, tm=512, tk=4096, name="up_gla_dw"))
    emit("w_up_pool", _mm(yb_in, dyb, ta=True, out_dtype=BF16, tm=512, tk=4096, shards=4, name="up_pool_dw"))
    emit("w_up_xattn", _mm(xc, dyc, ta=True, out_dtype=BF16, tm=512, tk=4096, shards=4, name="up_xattn_dw"))
    dpg, gs["w_fu_pad"], gs["b_f"], gs["gla_norm_g"] = _gla_bwd(pg, sp, so, o_gla, dya_in, small["w_fu_pad"], small["b_f"], small["gla_norm_g"], name="gla_bwd")
    dp, gs["w_pool"], gs["pool_scale"] = _pool_bwd(dyb_in, ppx, small["w_pool_b"], small["pool_scale"], name="pool_bwd")
    dxq, dkv = _xattn_bwd(dxc, ppx, kv, name="xattn_bwd")
    dkv = dkv.astype(BF16)
    emit("w_mem_kv", _mm(mem_n, dkv, ta=True, out_dtype=BF16, name="mem_kv_dw"))
    dmem_n = _mm(dkv, big["w_mem_kv"], tb=True, name="mem_kv_dx")
    _, gs["mem_norm_g"] = _rms_bwd(mem, small["mem_norm_g"], [dmem_n], None, 1.0, BF16, name="mem_norm_bwd")
    emit("w_gla", _mm(dpg, h, ta=True, out_dtype=BF16, tm=640, tk=4096, name="mix_in_gla_dw"))
    emit("w_p", _mm(dp, h, ta=True, out_dtype=BF16, tm=512, tk=4096, name="mix_in_p_dw"))
    emit("w_xq", _mm(dxq, h, ta=True, out_dtype=BF16, tm=512, tk=4096, name="mix_in_xq_dw"))
    behind = emit("w_gates", _mm(dgt, h, ta=True, out_dtype=BF16, tm=512, tk=4096, name="mix_in_gates_dw"))
    pairs = [(dpg, big["w_gla_t"]), (dp, big["w_p_t"]), (dxq, big["w_xq_t"]), (dgt, big["w_gates_t"])]
    dx1, gs["mix_pre_g"] = _mm_rms_bwd(pairs, x1, small["mix_pre_g"], dx2, after=behind, name="mix_in_dx")
    behind = advance((dx1,))
    dz1, gs["ffn1_post_g"] = _rms_bwd(f1, small["ffn1_post_g"], [dx1], None, 0.5, BF16, name="ffn1_post_bwd", after=behind)
    dx0, gs["ffn1_pre_g"], _ = _ffn_bwd(dz1, h1, ab1, u1, big["ffn1_w_in"], big["ffn1_w_out"], x, small["ffn1_pre_g"], dx1,
                                        "ffn1", emit, advance)
    return loss, dx0, gs


BIG = ("ffn1_w_in", "ffn1_w_out", "w_in", "w_mem_kv", "w_up_gla", "w_up_pool", "w_up_xattn", "w_o", "ffn2_w_in", "ffn2_w_out")
COL_SHARDED = ("ffn1_w_in", "w_in", "w_up_pool", "w_up_xattn", "ffn2_w_in")
GATHER_GROUPS = {"ffn1i": ("ffn1_w_in",), "ffn1o": ("ffn1_w_out",), "mixa": ("w_in", "w_fu"),
                 "mixb": ("w_mem_kv", "w_up_gla", "w_up_pool", "w_up_xattn", "w_o"), "ffn2": ("ffn2_w_in", "ffn2_w_out")}
REDUCE_GROUPS = {"ffn2": ("ffn2_w_out", "ffn2_w_in"),
                 "mix": ("w_o", "w_up_gla", "w_up_pool", "w_up_xattn", "w_mem_kv", "w_gla", "w_p", "w_xq", "w_gates"),
                 "ffn1_out": ("ffn1_w_out",),
                 "ffn1_in": ("ffn1_w_in",)}
REDUCE_LAST = "ffn1_in"
GAINS = ("ffn1_pre_g", "ffn1_post_g", "mix_pre_g", "gla_norm_g", "mem_norm_g", "mix_post_g", "ffn2_pre_g", "ffn2_post_g", "final_g")
WEIGHTS = ("ffn1_pre_g", "ffn1_w_in", "ffn1_w_out", "ffn1_post_g", "mix_pre_g", "w_in", "w_fu", "b_f", "gla_norm_g", "w_pool",
           "pool_scale", "mem_norm_g", "w_mem_kv", "w_up_gla", "w_up_pool", "w_up_xattn", "w_o", "mix_post_g", "ffn2_pre_g",
           "ffn2_w_in", "ffn2_w_out", "ffn2_post_g", "final_g")
IN_GLA, IN_F, IN_PX, IN_GATES, IN_END = 0, 3072, 3088, 4112, 7184
def _cols_from_shards(g):
    return jnp.transpose(g, (1, 0, 2)).reshape(g.shape[1], 4 * g.shape[2])


def kernel(x, mem, ffn1_pre_g, ffn1_w_in, ffn1_w_out, ffn1_post_g, mix_pre_g, w_in, w_fu, b_f, gla_norm_g, w_pool, pool_scale, mem_norm_g, w_mem_kv, w_up_gla, w_up_pool, w_up_xattn, w_o, mix_post_g, ffn2_pre_g, ffn2_w_in, ffn2_w_out, ffn2_post_g, final_g, loss_target, m_ffn1_pre_g, m_ffn1_w_in, m_ffn1_w_out, m_ffn1_post_g, m_mix_pre_g, m_w_in, m_w_fu, m_b_f, m_gla_norm_g, m_w_pool, m_pool_scale, m_mem_norm_g, m_w_mem_kv, m_w_up_gla, m_w_up_pool, m_w_up_xattn, m_w_o, m_mix_post_g, m_ffn2_pre_g, m_ffn2_w_in, m_ffn2_w_out, m_ffn2_post_g, m_final_g, v_ffn1_pre_g, v_ffn1_w_in, v_ffn1_w_out, v_ffn1_post_g, v_mix_pre_g, v_w_in, v_w_fu, v_b_f, v_gla_norm_g, v_w_pool, v_pool_scale, v_mem_norm_g, v_w_mem_kv, v_w_up_gla, v_w_up_pool, v_w_up_xattn, v_w_o, v_mix_post_g, v_ffn2_pre_g, v_ffn2_w_in, v_ffn2_w_out, v_ffn2_post_g, v_final_g):
    args = dict(locals())
    w = {n: args[n][0] for n in WEIGHTS}
    m = {n: args["m_" + n][0] for n in WEIGHTS}
    v = {n: args["v_" + n][0] for n in WEIGHTS}
    xi, yi, ci = lax.axis_index("x"), lax.axis_index("y"), lax.axis_index("c")
    chip = 2 * xi + yi

    c_arr = jnp.reshape(ci, (1,)).astype(jnp.int32)
    chip_arr = jnp.reshape(chip, (1,)).astype(jnp.int32)
    place_arr = jnp.stack([chip, ci]).astype(jnp.int32)
    shard_of = {n: (jnp.transpose(args[n][0])[None] if n == "w_in" else args[n]) for n in BIG}
    shard_of["w_fu"] = args["w_fu"]
    placed, inflight = {}, {}

    def place(names, after):
        for n in names:
            if n not in placed:
                placed[n] = _place_shard(shard_of[n], chip_arr, F32 if n == "w_fu" else BF16, name="place_" + n, after=after)

    def relayout(names, gathered):
        out = {}
        for n, g in zip(names, gathered):
            if n == "w_fu":
                w_fu_full = _cols_from_shards(g)
                out["w_fu_pad"] = jnp.concatenate([w_fu_full, jnp.zeros((LANE - GATE_RANK, 512), F32)], axis=0).astype(BF16)
            elif n == "w_in":
                wt = g.reshape(IN_END, D)
                out["w_gla_t"] = jnp.concatenate([wt[IN_GLA:IN_PX], jnp.zeros((PG_W - IN_PX, D), BF16)], axis=0)
                out["w_px_t"] = wt[IN_PX:IN_GATES]
                out["w_p_t"] = wt[IN_PX:IN_PX + 512]
                out["w_xq_t"] = wt[IN_PX + 512:IN_GATES]
                out["w_gates_t"] = wt[IN_GATES:IN_END]
            else:
                out[n] = _cols_from_shards(g) if n in COL_SHARDED else g.reshape(4 * g.shape[1], g.shape[2])
        return out

    def gather(op, group, after):
        names = GATHER_GROUPS[group]
        if op == "start":
            place(names, ())
            inflight[group] = _gather_start([placed[n] for n in names], after, name="gather_" + group + "_start")
            behind = (inflight[group][3],)
            if group == "ffn1o":
                place(shard_of, behind)
            return behind
        if op == "pass":
            send, recv, bufs, _ = inflight[group]
            inflight[group] = _gather_pass(bufs, send, recv, after, name="gather_" + group + "_pass")
            return (inflight[group][2][0],)
        send, recv, bufs = inflight.pop(group)
        return relayout(names, _gather_finish(bufs, send, recv, after, name="gather_" + group + "_finish"))

    small = {n: w[n].reshape(1, D) for n in GAINS}
    small["b_f"] = w["b_f"].reshape(1, 512)
    small["pool_scale"] = w["pool_scale"].reshape(1, 512)
    small["w_pool_b"] = w["w_pool"].astype(BF16)

    pending, crossing, travelling = {}, {}, {}

    def emit(name, grad):
        pending[name] = grad
        group = next((g for g, names in REDUCE_GROUPS.items() if name == names[-1]), None)
        if group is None:
            return ()
        gb = {n: pending.pop(n) for n in REDUCE_GROUPS[group]}
        if group == "mix":
            dwt = jnp.concatenate([gb.pop("w_gla")[0:IN_PX], gb.pop("w_p"), gb.pop("w_xq"), gb.pop("w_gates")], axis=0)
            gb["w_in"] = dwt.reshape(4, IN_END // 4, D)
        names = list(gb)
        contrib = [gb[n] if n in COL_SHARDED else gb[n].reshape(4, gb[n].shape[0] // 4, gb[n].shape[1]) for n in names]
        if group == REDUCE_LAST:
            from_sibling = _pair_exchange(contrib, name="grads_" + group + "_pair_exchange")
            return over_chips(group, names, contrib, from_sibling)
        send, recv, contrib, lands, token = _pair_exchange_start(contrib, (), name="grads_" + group + "_pair_start")
        crossing[group] = (names, contrib, lands, send, recv)
        return (token,)

    def over_chips(group, names, contrib, from_sibling):
        pair = [_pair_sum(g, got, c_arr, name="grads_pair_sum_" + n) for n, g, got in zip(names, contrib, from_sibling)]
        send, recv, pair, lands, token = _chip_exchange_start(pair, (), name="grads_" + group + "_chip_start")
        travelling[group] = (names, send, recv, pair, lands)
        return (token,)

    def advance(after):
        behind = ()
        for group in list(crossing):
            names, contrib, lands, send, recv = crossing.pop(group)
            contrib, from_sibling = _pair_exchange_finish(contrib, lands, send, recv, after, name="grads_" + group + "_pair_finish")
            behind = over_chips(group, names, contrib, from_sibling)
        return behind

    loss, grad_x, gs = _local_step(x[0], mem[0], loss_target[0], small, gather, emit, advance)
    loss = lax.psum(loss[0, 0], ("x", "y", "c"))

    halves = {}
    for group, (names, send, recv, pair, lands) in travelling.items():
        pair, from_chips = _chip_exchange_finish(pair, lands, send, recv, (grad_x,), name="grads_" + group + "_chip_finish")
        for n, p, got in zip(names, pair, from_chips):
            halves[n] = _chip_sum(p, got, place_arr, name="grads_chip_sum_" + n)
    send, recv, joining, token = _pair_join_start([halves[n] for n in BIG], name="grads_pair_join_start")
    small_sums = _all_sum_small(gs, name="sum_small_grads", after=(token,))
    reduced = dict(zip(BIG, _pair_join_finish(joining, send, recv, (small_sums[0],), name="grads_pair_join_finish")))

    grads, delta, new_m, new_v = {}, {}, {}, {}
    for n in BIG:
        if n == "w_in":
            transposed = [jnp.transpose(args[k][0]) for k in (n, "m_" + n, "v_" + n)]
            updated = _adamw(transposed[0], reduced[n], transposed[1], transposed[2], name="adamw_" + n)
            grads[n] = jnp.transpose(reduced[n])[None]
            delta[n], new_m[n], new_v[n] = (jnp.transpose(a)[None] for a in updated)
            continue
        grads[n] = reduced[n][None]
        delta[n], new_m[n], new_v[n] = _adamw(args[n], reduced[n], args["m_" + n], args["v_" + n], name="adamw_" + n)
    small_params = {n: (args[n], args["m_" + n], args["v_" + n]) for n in SMALL}
    for n, (g, d, mn, vn) in _adamw_small(small_sums, small_params, chip_arr, name="adamw_small").items():
        grads[n], delta[n], new_m[n], new_v[n] = g, d, mn, vn

    outs = [loss, grad_x[None]]
    for group in (grads, delta, new_m, new_v):
        outs += [group[n] for n in WEIGHTS]
    return tuple(outs)
```

```python
import functools

import jax
import jax.numpy as jnp
from jax import lax
from jax.experimental import pallas as pl
from jax.experimental.pallas import tpu as pltpu

F32 = jnp.float32
BF16 = jnp.bfloat16
MESH = pl.DeviceIdType.MESH
HIGHEST = lax.Precision.HIGHEST

D = 1024
DFF = 2816
CHUNK = 64
HEADS = 4
HDK = 128
HDV = 256
GATE_TEMP = 16.0
POOL_WINDOWS = (2, 4, 8, 16)
POOL_HALO = 16
XA_HEADS = 4
XA_HD = 128
EPS = 1e-6
Q_SCALE = HDK ** -0.5
XA_SCALE = XA_HD ** -0.5
PG_Q, PG_K, PG_V, PG_G, PG_F, PG_W = 0, 512, 1024, 2048, 3072, 3200
GATE_RANK = 16
ADAM_LR, ADAM_B1, ADAM_B2, ADAM_EPS, ADAM_WD, ADAM_STEP = 0.001, 0.9, 0.999, 1e-08, 0.01, 10

VMEM_LIMIT = 48 * 1024 * 1024
LANE = 128
TS_ROW = 512
TS_GLA = 512
TS_POOL = 512
TS_XA = 512


def _params(sem):
    return pltpu.CompilerParams(dimension_semantics=sem, vmem_limit_bytes=VMEM_LIMIT)


def _tile(n, cap, unit=LANE):
    if n <= cap:
        return n
    best = None
    for t in range(unit, cap + 1, unit):
        if n % t == 0:
            best = t
    assert best is not None, (n, cap)
    return best


def _sigmoid(x):
    return 0.5 * jnp.tanh(0.5 * x) + 0.5


def _log_sigmoid(x):
    return jnp.minimum(x, 0.0) - jnp.log(1.0 + jnp.exp(-jnp.abs(x)))


def _rms(x):
    r = lax.rsqrt(jnp.mean(x * x, axis=-1, keepdims=True) + EPS)
    return x * r, r


def _rows(ts, w):
    return pl.BlockSpec((ts, w), lambda i: (i, 0))


def _fixed(shape):
    nd = len(shape)
    return pl.BlockSpec(shape, lambda i: (0,) * nd)


def _mm(a, b, *, ta=False, tb=False, out_dtype=F32, tm=2048, tn=1024, tk=1024, shards=1, after=(), name):
    a_blocked, b_blocked = a.ndim == 3, b.ndim == 3
    assert not (a_blocked and ta) and not (b_blocked and tb)
    if a_blocked:
        m, kdim, tk = a.shape[1], a.shape[0] * a.shape[2], a.shape[2]
    else:
        m, kdim = (a.shape[1], a.shape[0]) if ta else a.shape
    if b_blocked:
        n, tn = b.shape[0] * b.shape[2], b.shape[2]
        assert b.shape[1] == kdim and shards in (1, b.shape[0])
    else:
        n = b.shape[0] if tb else b.shape[1]
        assert (b.shape[1] if tb else b.shape[0]) == kdim, (a.shape, b.shape, ta, tb)
        tn = n // shards if shards > 1 else _tile(n, tn)
    tm = _tile(m, tm)
    tk = tk if a_blocked else _tile(kdim, tk)
    kgroup = 2 if (a_blocked and tb and a.shape[0] % 2 == 0) else 1
    nk = kdim // (tk * kgroup)
    dims = (((0 if ta else 1,), (1 if tb else 0,)), ((), ()))

    def body(a_ref, b_ref, *rest):
        o_ref, *acc = rest[len(after):]
        if kgroup == 1:
            part = lax.dot_general(a_ref[...], b_ref[...], dims, preferred_element_type=F32)
        else:
            part = sum(lax.dot_general(a_ref[g], b_ref[:, g * tk:(g + 1) * tk], dims, preferred_element_type=F32) for g in range(kgroup))
        if nk == 1:
            o_ref[...] = part.astype(o_ref.dtype)
            return
        acc_ref, = acc
        k = pl.program_id(2)

        @pl.when(k == 0)
        def _():
            acc_ref[...] = part

        @pl.when(k > 0)
        def _():
            acc_ref[...] += part

        @pl.when(k == nk - 1)
        def _():
            o_ref[...] = acc_ref[...].astype(o_ref.dtype)

    if a_blocked and kgroup > 1:
        a_spec = pl.BlockSpec((kgroup, tm, tk), lambda i, j, k: (k, i, 0))
    elif a_blocked:
        a_spec = pl.BlockSpec((None, tm, tk), lambda i, j, k: (k, i, 0))
    else:
        a_spec = pl.BlockSpec((tk, tm), lambda i, j, k: (k, i)) if ta else pl.BlockSpec((tm, tk), lambda i, j, k: (i, k))
    if b_blocked:
        b_spec = pl.BlockSpec((None, tk, tn), lambda i, j, k: (j, k, 0))
    else:
        b_spec = pl.BlockSpec((tn, tk * kgroup), lambda i, j, k: (j, k)) if tb else pl.BlockSpec((tk, tn), lambda i, j, k: (k, j))
    if shards > 1:
        out_shape = jax.ShapeDtypeStruct((shards, m, tn), out_dtype)
        o_spec = pl.BlockSpec((None, tm, tn), lambda i, j, k: (j, i, 0))
    else:
        out_shape = jax.ShapeDtypeStruct((m, n), out_dtype)
        o_spec = pl.BlockSpec((tm, tn), lambda i, j, k: (i, j))
    return pl.pallas_call(
        body, grid=(m // tm, n // tn, nk), in_specs=[a_spec, b_spec] + [ANY] * len(after), out_specs=o_spec, out_shape=out_shape,
        scratch_shapes=[pltpu.VMEM((tm, tn), F32)] if nk > 1 else [],
        compiler_params=_params(("parallel", "parallel", "arbitrary")), name=name,
    )(a, b, *after)


def _norm_fwd(x, g, out_dtype, name, after=()):
    s, d = x.shape
    ts = _tile(s, TS_ROW, 8)

    def body(x_ref, g_ref, *rest):
        o_ref = rest[len(after)]
        xh, _ = _rms(x_ref[...])
        o_ref[...] = (xh * g_ref[...]).astype(o_ref.dtype)

    return pl.pallas_call(
        body, grid=(s // ts,), in_specs=[_rows(ts, d), _fixed((1, d))] + [ANY] * len(after), out_specs=_rows(ts, d),
        out_shape=jax.ShapeDtypeStruct((s, d), out_dtype), compiler_params=_params(("parallel",)), name=name,
    )(x, g, *after)


def _resid_norm_fwd(x, f, g_post, alpha, g_next, name, after=()):
    s, d = x.shape
    ts = _tile(s, TS_ROW, 8)
    with_h = g_next is not None

    def body(x_ref, f_ref, gp_ref, *rest):
        rest = rest[:1] + rest[1 + len(after):] if with_h else rest[len(after):]
        fh, _ = _rms(f_ref[...])
        xn = x_ref[...] + alpha * (fh * gp_ref[...])
        if with_h:
            gn_ref, xo_ref, h_ref = rest
            xh, _ = _rms(xn)
            h_ref[...] = (xh * gn_ref[...]).astype(h_ref.dtype)
        else:
            xo_ref, = rest
        xo_ref[...] = xn

    ins = [x, f, g_post] + ([g_next] if with_h else []) + list(after)
    in_specs = [_rows(ts, d), _rows(ts, d), _fixed((1, d))] + ([_fixed((1, d))] if with_h else []) + [ANY] * len(after)
    out_shape = [jax.ShapeDtypeStruct((s, d), F32)] + ([jax.ShapeDtypeStruct((s, d), BF16)] if with_h else [])
    out_specs = [_rows(ts, d)] + ([_rows(ts, d)] if with_h else [])
    out = pl.pallas_call(
        body, grid=(s // ts,), in_specs=in_specs, out_specs=out_specs, out_shape=out_shape,
        compiler_params=_params(("parallel",)), name=name,
    )(*ins)
    return (out[0], out[1]) if with_h else (out[0], None)


def _mm_resid_norm(a, w, x, g_post, alpha, g_next, name, after=(), tm=512):
    s, kdim = a.shape
    d = w.shape[1]
    tm = _tile(s, tm)
    with_h = g_next is not None
    na = len(after)

    def body(a_ref, w_ref, x_ref, gp_ref, *rest):
        rest = rest[int(with_h) + na:] if not with_h else rest[:1] + rest[1 + na:]
        for rows in _sub_blocks(tm):
            f = jnp.dot(a_ref[rows, :], w_ref[...], preferred_element_type=F32)
            fh, _ = _rms(f)
            xn = x_ref[rows, :] + alpha * (fh * gp_ref[...])
            if with_h:
                gn_ref, f_ref, xo_ref, h_ref = rest
                xh, _ = _rms(xn)
                h_ref[rows, :] = (xh * gn_ref[...]).astype(h_ref.dtype)
            else:
                f_ref, xo_ref = rest
            f_ref[rows, :] = f
            xo_ref[rows, :] = xn

    ins = [a, w, x, g_post] + ([g_next] if with_h else []) + list(after)
    in_specs = [_rows(tm, kdim), _fixed((kdim, d)), _rows(tm, d), _fixed((1, d))] + ([_fixed((1, d))] if with_h else []) + [ANY] * na
    out_shape = [jax.ShapeDtypeStruct((s, d), F32)] * 2 + ([jax.ShapeDtypeStruct((s, d), BF16)] if with_h else [])
    out = pl.pallas_call(
        body, grid=(s // tm,), in_specs=in_specs, out_specs=[_rows(tm, d)] * len(out_shape), out_shape=out_shape,
        compiler_params=_params(("parallel",)), name=name,
    )(*ins)
    return (out[0], out[1], out[2]) if with_h else (out[0], out[1], None)


def _mm_rms_bwd(pairs, x, g, dres, name, after=(), post=None, tm=512):
    s, d = x.shape
    tm = _tile(s, tm)
    n, na = len(pairs), len(after)

    def body(*refs):
        a_refs, w_refs = refs[0:2 * n:2], refs[1:2 * n:2]
        x_ref, g_ref, dres_ref = refs[2 * n:2 * n + 3]
        if post is not None:
            f_ref, gp_ref = refs[2 * n + 3:2 * n + 5]
            dx_ref, dg_ref, df_ref, dgp_ref = refs[2 * n + 5 + na:]
        else:
            dx_ref, dg_ref = refs[2 * n + 3 + na:]
        @pl.when(pl.program_id(0) == 0)
        def _():
            dg_ref[...] = jnp.zeros_like(dg_ref)
            if post is not None:
                dgp_ref[...] = jnp.zeros_like(dgp_ref)

        for rows in _sub_blocks(tm):
            dy = None
            for a_ref, w_ref in zip(a_refs, w_refs):
                if len(a_ref.shape) == 3:
                    tkb = a_ref.shape[2]
                    parts = [lax.dot_general(a_ref[q, rows, :], w_ref[:, q * tkb:(q + 1) * tkb], (((1,), (1,)), ((), ())),
                                             preferred_element_type=F32) for q in range(a_ref.shape[0])]
                else:
                    parts = [jnp.dot(a_ref[rows, :], w_ref[...], preferred_element_type=F32)]
                for part in parts:
                    dy = part if dy is None else dy + part
            xh, r = _rms(x_ref[rows, :])
            dg_ref[...] += jnp.sum(dy * xh, axis=0, keepdims=True)
            dyg = dy * g_ref[...]
            dx = r * (dyg - xh * jnp.mean(dyg * xh, axis=-1, keepdims=True)) + dres_ref[rows, :]
            dx_ref[rows, :] = dx
            if post is not None:
                fh, rf = _rms(f_ref[rows, :])
                dz = dx * post[2]
                dgp_ref[...] += jnp.sum(dz * fh, axis=0, keepdims=True)
                dzg = dz * gp_ref[...]
                df_ref[rows, :] = (rf * (dzg - fh * jnp.mean(dzg * fh, axis=-1, keepdims=True))).astype(df_ref.dtype)

    ins, in_specs = [], []
    for a_arr, w_arr in pairs:
        ins += [a_arr, w_arr]
        if a_arr.ndim == 3:
            in_specs.append(pl.BlockSpec((a_arr.shape[0], tm, a_arr.shape[2]), lambda i: (0, i, 0)))
        else:
            in_specs.append(_rows(tm, a_arr.shape[1]))
        in_specs.append(pl.BlockSpec(w_arr.shape, lambda i: (0, 0), pipeline_mode=pl.Buffered(1)))
    with_post = post is not None
    return pl.pallas_call(
        body, grid=(s // tm,),
        in_specs=in_specs + [_rows(tm, d), _fixed((1, d)), _rows(tm, d)] + ([_rows(tm, d), _fixed((1, d))] if with_post else [])
        + [ANY] * na,
        out_specs=[_rows(tm, d), _fixed((1, d))] + ([_rows(tm, d), _fixed((1, d))] if with_post else []),
        out_shape=[jax.ShapeDtypeStruct((s, d), F32), jax.ShapeDtypeStruct((1, d), F32)]
        + ([jax.ShapeDtypeStruct((s, d), BF16), jax.ShapeDtypeStruct((1, d), F32)] if with_post else []),
        compiler_params=_params(("arbitrary",)), name=name,
    )(*ins, x, g, dres, *(post[:2] if with_post else ()), *after)


def _ffn_out_loss(u, w_out, x, g_post, alpha, g_final, target, name, tm=512):
    s, kdim = u.shape
    d = w_out.shape[1]
    tm = _tile(s, tm)

    def body(u_ref, w_ref, x_ref, gp_ref, gf_ref, t_ref, df_ref, dx_ref, dgp_ref, dgf_ref, loss_ref):
        @pl.when(pl.program_id(0) == 0)
        def _():
            dgp_ref[...] = jnp.zeros_like(dgp_ref)
            dgf_ref[...] = jnp.zeros_like(dgf_ref)
            loss_ref[...] = jnp.zeros_like(loss_ref)

        for rows in _sub_blocks(tm):
            f = jnp.dot(u_ref[rows, :], w_ref[...], preferred_element_type=F32)
            fh, rf = _rms(f)
            xn = x_ref[rows, :] + alpha * (fh * gp_ref[...])
            xh, rx = _rms(xn)
            gf = gf_ref[...]
            diff = xh * gf - t_ref[rows, :]
            sq = jnp.sum(diff * diff, axis=1, keepdims=True)
            loss_ref[...] += (0.5 / d) * jnp.sum(sq, axis=0, keepdims=True)
            dy = diff * (1.0 / d)
            dgf_ref[...] += jnp.sum(dy * xh, axis=0, keepdims=True)
            dyg = dy * gf
            dxn = rx * (dyg - xh * jnp.mean(dyg * xh, axis=-1, keepdims=True))
            dx_ref[rows, :] = dxn
            dz = dxn * alpha
            dgp_ref[...] += jnp.sum(dz * fh, axis=0, keepdims=True)
            dzg = dz * gp_ref[...]
            df_ref[rows, :] = (rf * (dzg - fh * jnp.mean(dzg * fh, axis=-1, keepdims=True))).astype(df_ref.dtype)

    return pl.pallas_call(
        body, grid=(s // tm,),
        in_specs=[_rows(tm, kdim), _resident(w_out.shape), _rows(tm, d), _fixed((1, d)), _fixed((1, d)), _rows(tm, d)],
        out_specs=[_rows(tm, d), _rows(tm, d), _fixed((1, d)), _fixed((1, d)), _fixed((8, LANE))],
        out_shape=[jax.ShapeDtypeStruct((s, d), BF16), jax.ShapeDtypeStruct((s, d), F32), jax.ShapeDtypeStruct((1, d), F32),
                   jax.ShapeDtypeStruct((1, d), F32), jax.ShapeDtypeStruct((8, LANE), F32)],
        compiler_params=_params(("arbitrary",)), name=name,
    )(u, w_out, x, g_post, g_final, target)


def _rms_bwd(x, g, dys, dres, alpha, out_dtype, name, after=()):
    s, d = x.shape
    ts = _tile(s, TS_ROW, 8)
    ndy = len(dys)
    with_res = dres is not None

    def body(x_ref, g_ref, *rest):
        dy_refs = rest[:ndy]
        rest = rest[ndy:]
        if with_res:
            dres_ref = rest[0]
        dx_ref, dg_ref = rest[int(with_res) + len(after):]
        xh, r = _rms(x_ref[...])
        dy = dy_refs[0][...].astype(F32)
        for ref in dy_refs[1:]:
            dy = dy + ref[...].astype(F32)
        dy = dy * alpha

        @pl.when(pl.program_id(0) == 0)
        def _():
            dg_ref[...] = jnp.zeros_like(dg_ref)

        dg_ref[...] += jnp.sum(dy * xh, axis=0, keepdims=True)
        dyg = dy * g_ref[...]
        dx = r * (dyg - xh * jnp.mean(dyg * xh, axis=-1, keepdims=True))
        if with_res:
            dx = dx + dres_ref[...]
        dx_ref[...] = dx.astype(dx_ref.dtype)

    ins = [x, g] + list(dys) + ([dres] if with_res else []) + list(after)
    in_specs = [_rows(ts, d), _fixed((1, d))] + [_rows(ts, d)] * (ndy + int(with_res)) + [ANY] * len(after)
    return pl.pallas_call(
        body, grid=(s // ts,), in_specs=in_specs, out_specs=[_rows(ts, d), _fixed((1, d))],
        out_shape=[jax.ShapeDtypeStruct((s, d), out_dtype), jax.ShapeDtypeStruct((1, d), F32)],
        compiler_params=_params(("arbitrary",)), name=name,
    )(*ins)


def _loss_bwd(x, g, target, name):
    s, d = x.shape
    ts = _tile(s, TS_ROW, 8)

    def body(x_ref, g_ref, t_ref, dx_ref, dg_ref, loss_ref):
        xh, r = _rms(x_ref[...])
        gv = g_ref[...]
        diff = xh * gv - t_ref[...]

        @pl.when(pl.program_id(0) == 0)
        def _():
            dg_ref[...] = jnp.zeros_like(dg_ref)
            loss_ref[...] = jnp.zeros_like(loss_ref)

        sq = jnp.sum(diff * diff, axis=1, keepdims=True)
        loss_ref[...] += (0.5 / d) * jnp.sum(sq, axis=0, keepdims=True)
        dy = diff * (1.0 / d)
        dg_ref[...] += jnp.sum(dy * xh, axis=0, keepdims=True)
        dyg = dy * gv
        dx_ref[...] = r * (dyg - xh * jnp.mean(dyg * xh, axis=-1, keepdims=True))

    return pl.pallas_call(
        body, grid=(s // ts,), in_specs=[_rows(ts, d), _fixed((1, d)), _rows(ts, d)],
        out_specs=[_rows(ts, d), _fixed((1, d)), _fixed((8, LANE))],
        out_shape=[jax.ShapeDtypeStruct((s, d), F32), jax.ShapeDtypeStruct((1, d), F32), jax.ShapeDtypeStruct((8, LANE), F32)],
        compiler_params=_params(("arbitrary",)), name=name,
    )(x, g, target)


HALF_FF = DFF // 2


SUB_ROWS = 256


def _sub_blocks(tm):
    sub = SUB_ROWS if tm % SUB_ROWS == 0 else tm
    return [slice(r0, r0 + sub) for r0 in range(0, tm, sub)]


def _ffn_in_swiglu(x_norm, w_in, name, after=(), tm=1024):
    s, d = x_norm.shape
    tm = _tile(s, tm)

    def body(x_ref, wa_ref, wb_ref, *rest):
        ab_ref, u_ref = rest[len(after):]
        for rows in _sub_blocks(tm):
            xv = x_ref[rows, :]
            a = jnp.dot(xv, wa_ref[...], preferred_element_type=F32)
            b = jnp.dot(xv, wb_ref[...], preferred_element_type=F32)
            ab_ref[0, rows, :] = a.astype(ab_ref.dtype)
            ab_ref[1, rows, :] = b.astype(ab_ref.dtype)
            u_ref[rows, :] = (a * _sigmoid(a) * b).astype(u_ref.dtype)

    ab, u = pl.pallas_call(
        body, grid=(s // tm, 2),
        in_specs=[pl.BlockSpec((tm, d), lambda i, j: (i, 0)), pl.BlockSpec((d, HALF_FF), lambda i, j: (0, j)),
                  pl.BlockSpec((d, HALF_FF), lambda i, j: (0, 2 + j))] + [ANY] * len(after),
        out_specs=[pl.BlockSpec((2, None, tm, HALF_FF), lambda i, j: (0, j, i, 0)), pl.BlockSpec((tm, HALF_FF), lambda i, j: (i, j))],
        out_shape=[jax.ShapeDtypeStruct((2, 2, s, HALF_FF), BF16), jax.ShapeDtypeStruct((s, DFF), BF16)],
        compiler_params=_params(("parallel", "parallel")), name=name,
    )(x_norm, w_in, w_in, *after)
    return ab.reshape(4, s, HALF_FF), u


def _ffn_out_dx_swiglu(dz, w_out, ab, after, name, tm=1024):
    s, d = dz.shape
    tm = _tile(s, tm)

    def body(dz_ref, w_ref, ab_ref, *rest):
        dab_ref = rest[len(after)]
        for rows in _sub_blocks(tm):
            du = lax.dot_general(dz_ref[rows, :], w_ref[...], (((1,), (1,)), ((), ())), preferred_element_type=F32)
            a = ab_ref[0, rows, :].astype(F32)
            b = ab_ref[1, rows, :].astype(F32)
            sig = _sigmoid(a)
            dab_ref[0, rows, :] = (du * b * (sig * (1.0 + a * (1.0 - sig)))).astype(dab_ref.dtype)
            dab_ref[1, rows, :] = (du * a * sig).astype(dab_ref.dtype)

    halves = pl.BlockSpec((2, None, tm, HALF_FF), lambda i, j: (0, j, i, 0))
    dab = pl.pallas_call(
        body, grid=(s // tm, 2),
        in_specs=[pl.BlockSpec((tm, d), lambda i, j: (i, 0)), pl.BlockSpec((HALF_FF, d), lambda i, j: (j, 0)), halves] + [ANY] * len(after),
        out_specs=halves, out_shape=jax.ShapeDtypeStruct((2, 2, s, HALF_FF), BF16),
        compiler_params=_params(("parallel", "parallel")), name=name,
    )(dz, w_out, ab.reshape(2, 2, s, HALF_FF), *after)
    return dab.reshape(4, s, HALF_FF)


def _tri(strict):
    r = lax.broadcasted_iota(jnp.int32, (CHUNK, CHUNK), 0)
    c = lax.broadcasted_iota(jnp.int32, (CHUNK, CHUNK), 1)
    return (r > c).astype(F32) if strict else (r >= c).astype(F32)


def _gla_fwd(pg, wfu, b_f, gnorm, name):
    s = pg.shape[0]
    ts = _tile(s, TS_GLA, CHUNK)
    cpb = ts // CHUNK
    nc = s // CHUNK

    def body(pg_ref, wfu_ref, bf_ref, gn_ref, ya_ref, sp_ref, so_ref, o_ref, st_ref, la_ref, dec_ref, u_ref):
        @pl.when(pl.program_id(0) == 0)
        def _():
            st_ref[...] = jnp.zeros_like(st_ref)

        f = jnp.dot(pg_ref[:, PG_F:PG_W], wfu_ref[...], preferred_element_type=F32) + bf_ref[...]
        la_ref[...] = _log_sigmoid(f) * (1.0 / GATE_TEMP)
        tri = _tri(False)
        chunks = [slice(ci * CHUNK, (ci + 1) * CHUNK) for ci in range(cpb)]
        for ci, rows in enumerate(chunks):
            la = la_ref[rows, :]
            b = jnp.dot(tri, la, precision=HIGHEST, preferred_element_type=F32)
            bend = jnp.sum(la, axis=0, keepdims=True)
            e = jnp.exp(bend - b)
            dec_ref[ci:ci + 1, :] = jnp.exp(bend)
            for hd in range(HEADS):
                k = pg_ref[rows, PG_K + hd * HDK:PG_K + (hd + 1) * HDK]
                v = pg_ref[rows, PG_V + hd * HDV:PG_V + (hd + 1) * HDV]
                kt = (k.astype(F32) * e[:, hd * HDK:(hd + 1) * HDK]).astype(BF16)
                u_ref[ci, hd] = lax.dot_general(v, kt, (((0,), (0,)), ((), ())), preferred_element_type=F32)
        for ci in range(cpb):
            for hd in range(HEADS):
                prev = st_ref[hd]
                sp_ref[ci, hd] = prev
                st = prev * dec_ref[ci:ci + 1, hd * HDK:(hd + 1) * HDK] + u_ref[ci, hd]
                st_ref[hd] = st
                so_ref[ci, hd] = st.astype(so_ref.dtype)
        for ci, rows in enumerate(chunks):
            for hd in range(HEADS):
                vc = slice(hd * HDV, (hd + 1) * HDV)
                q = pg_ref[rows, PG_Q + hd * HDK:PG_Q + (hd + 1) * HDK]
                go = pg_ref[rows, PG_G + hd * HDV:PG_G + (hd + 1) * HDV].astype(F32)
                qs = (q.astype(F32) * Q_SCALE).astype(BF16)
                o = lax.dot_general(qs, so_ref[ci, hd], (((1,), (1,)), ((), ())), preferred_element_type=F32)
                o_ref[rows, vc] = o
                oh, _ = _rms(o)
                ya_ref[rows, vc] = (oh * gn_ref[:, vc] * (go * _sigmoid(go))).astype(ya_ref.dtype)

    return pl.pallas_call(
        body, grid=(s // ts,),
        in_specs=[_rows(ts, PG_W), _fixed((LANE, HEADS * HDK)), _fixed((1, HEADS * HDK)), _fixed((1, HEADS * HDV))],
        out_specs=[_rows(ts, HEADS * HDV), pl.BlockSpec((cpb, HEADS, HDV, HDK), lambda i: (i, 0, 0, 0)),
                   pl.BlockSpec((cpb, HEADS, HDV, HDK), lambda i: (i, 0, 0, 0)), _rows(ts, HEADS * HDV)],
        out_shape=[jax.ShapeDtypeStruct((s, HEADS * HDV), BF16), jax.ShapeDtypeStruct((nc, HEADS, HDV, HDK), F32),
                   jax.ShapeDtypeStruct((nc, HEADS, HDV, HDK), BF16), jax.ShapeDtypeStruct((s, HEADS * HDV), F32)],
        scratch_shapes=[pltpu.VMEM((HEADS, HDV, HDK), F32), pltpu.VMEM((ts, HEADS * HDK), F32),
                        pltpu.VMEM((max(cpb, 8), HEADS * HDK), F32), pltpu.VMEM((cpb, HEADS, HDV, HDK), F32)],
        compiler_params=_params(("arbitrary",)), name=name,
    )(pg, wfu, b_f, gnorm)


def _gla_bwd(pg, sp, so, o, dya, wfu, b_f, gnorm, name):
    s = pg.shape[0]
    ts = _tile(s, TS_GLA, CHUNK)
    cpb = ts // CHUNK
    nblk = s // ts

    def body(pg_ref, sp_ref, so_ref, o_ref, dya_ref, wfu_ref, bf_ref, gn_ref, dpg_ref, dwfu_ref, dbf_ref, dgn_ref,
             dst_ref, la_ref, sg_ref, df_ref, e_ref, ktf_ref, dec_ref, g_ref):
        @pl.when(pl.program_id(0) == 0)
        def _():
            dst_ref[...] = jnp.zeros_like(dst_ref)
            dwfu_ref[...] = jnp.zeros_like(dwfu_ref)
            dbf_ref[...] = jnp.zeros_like(dbf_ref)
            dgn_ref[...] = jnp.zeros_like(dgn_ref)

        flow = pg_ref[:, PG_F:PG_W]
        f = jnp.dot(flow, wfu_ref[...], preferred_element_type=F32) + bf_ref[...]
        la_ref[...] = _log_sigmoid(f) * (1.0 / GATE_TEMP)
        sg_ref[...] = _sigmoid(-f) * (1.0 / GATE_TEMP)
        tri = _tri(False)
        tri_strict = _tri(True)
        chunks = [slice(ci * CHUNK, (ci + 1) * CHUNK) for ci in range(cpb)]
        for ci, rows in enumerate(chunks):
            la = la_ref[rows, :]
            b = jnp.dot(tri, la, precision=HIGHEST, preferred_element_type=F32)
            bend = jnp.sum(la, axis=0, keepdims=True)
            e = jnp.exp(bend - b)
            e_ref[rows, :] = e
            dec = jnp.exp(bend)
            dec_ref[ci:ci + 1, :] = dec
            for hd in range(HEADS):
                kc = slice(hd * HDK, (hd + 1) * HDK)
                vc = slice(hd * HDV, (hd + 1) * HDV)
                q = pg_ref[rows, PG_Q + hd * HDK:PG_Q + (hd + 1) * HDK]
                k = pg_ref[rows, PG_K + hd * HDK:PG_K + (hd + 1) * HDK]
                go = pg_ref[rows, PG_G + hd * HDV:PG_G + (hd + 1) * HDV].astype(F32)
                ktf_ref[rows, kc] = k.astype(F32) * e[:, kc]
                st_b = so_ref[ci, hd]
                qs = (q.astype(F32) * Q_SCALE).astype(BF16)
                oh, r = _rms(o_ref[rows, vc])
                gh = gn_ref[:, vc]
                sig = _sigmoid(go)
                dy = dya_ref[rows, vc].astype(F32)
                don = dy * (go * sig)
                dgn_ref[:, vc] += jnp.sum(don * oh, axis=0, keepdims=True)
                dong = don * gh
                do = (r * (dong - oh * jnp.mean(dong * oh, axis=-1, keepdims=True))).astype(BF16)
                g_ref[ci, hd] = lax.dot_general(do, qs, (((0,), (0,)), ((), ())), preferred_element_type=F32)
                dq = jnp.dot(do, st_b, preferred_element_type=F32) * Q_SCALE
                dpg_ref[rows, PG_Q + hd * HDK:PG_Q + (hd + 1) * HDK] = dq.astype(dpg_ref.dtype)
                dgo = dy * (oh * gh) * (sig * (1.0 + go * (1.0 - sig)))
                dpg_ref[rows, PG_G + hd * HDV:PG_G + (hd + 1) * HDV] = dgo.astype(dpg_ref.dtype)
        for ci in reversed(range(cpb)):
            for hd in range(HEADS):
                dst = dst_ref[hd] + g_ref[ci, hd]
                g_ref[ci, hd] = dst
                dst_ref[hd] = dst * dec_ref[ci:ci + 1, hd * HDK:(hd + 1) * HDK]
        for ci, rows in enumerate(chunks):
            for hd in range(HEADS):
                kc = slice(hd * HDK, (hd + 1) * HDK)
                v = pg_ref[rows, PG_V + hd * HDV:PG_V + (hd + 1) * HDV]
                ktf = ktf_ref[rows, kc]
                dst = g_ref[ci, hd]
                dst_b = dst.astype(BF16)
                dkt = jnp.dot(v, dst_b, preferred_element_type=F32)
                dv = lax.dot_general(ktf.astype(BF16), dst_b, (((1,), (1,)), ((), ())), preferred_element_type=F32)
                dd = jnp.sum(dst * sp_ref[ci, hd], axis=0, keepdims=True)
                dla = jnp.dot(tri_strict, dkt * ktf, precision=HIGHEST, preferred_element_type=F32) + dd * dec_ref[ci:ci + 1, kc]
                df_ref[rows, kc] = dla * sg_ref[rows, kc]
                dpg_ref[rows, PG_K + hd * HDK:PG_K + (hd + 1) * HDK] = (dkt * e_ref[rows, kc]).astype(dpg_ref.dtype)
                dpg_ref[rows, PG_V + hd * HDV:PG_V + (hd + 1) * HDV] = dv.astype(dpg_ref.dtype)
        df = df_ref[...]
        df_b = df.astype(BF16)
        dpg_ref[:, PG_F:PG_W] = lax.dot_general(df_b, wfu_ref[...], (((1,), (1,)), ((), ())), preferred_element_type=F32).astype(dpg_ref.dtype)
        dwfu_ref[...] += lax.dot_general(flow, df_b, (((0,), (0,)), ((), ())), preferred_element_type=F32)
        dbf_ref[...] += jnp.sum(df, axis=0, keepdims=True)

    rev = lambda i: (nblk - 1 - i, 0)
    return pl.pallas_call(
        body, grid=(nblk,),
        in_specs=[pl.BlockSpec((ts, PG_W), rev), pl.BlockSpec((cpb, HEADS, HDV, HDK), lambda i: (nblk - 1 - i, 0, 0, 0)),
                  pl.BlockSpec((cpb, HEADS, HDV, HDK), lambda i: (nblk - 1 - i, 0, 0, 0)), pl.BlockSpec((ts, HEADS * HDV), rev),
                  pl.BlockSpec((ts, HEADS * HDV), rev), _fixed((LANE, HEADS * HDK)), _fixed((1, HEADS * HDK)), _fixed((1, HEADS * HDV))],
        out_specs=[pl.BlockSpec((ts, PG_W), rev), _fixed((LANE, HEADS * HDK)), _fixed((1, HEADS * HDK)), _fixed((1, HEADS * HDV))],
        out_shape=[jax.ShapeDtypeStruct((s, PG_W), BF16), jax.ShapeDtypeStruct((LANE, HEADS * HDK), F32),
                   jax.ShapeDtypeStruct((1, HEADS * HDK), F32), jax.ShapeDtypeStruct((1, HEADS * HDV), F32)],
        scratch_shapes=[pltpu.VMEM((HEADS, HDV, HDK), F32)] + [pltpu.VMEM((ts, HEADS * HDK), F32)] * 5
        + [pltpu.VMEM((max(cpb, 8), HEADS * HDK), F32), pltpu.VMEM((cpb, HEADS, HDV, HDK), F32)],
        compiler_params=_params(("arbitrary",)), name=name,
    )(pg, sp, so, o, dya, wfu, b_f, gnorm)


def _window_sums(ext, sign):
    n = ext.shape[0]
    sums = {1: ext}
    w = 1
    while w < POOL_WINDOWS[-1]:
        sums[2 * w] = sums[w] + pltpu.roll(sums[w], w if sign > 0 else n - w, 0)
        w *= 2
    return [sums[POOL_WINDOWS[g]][:, g * LANE:(g + 1) * LANE] for g in range(len(POOL_WINDOWS))]


def _pool_counts(row0, n):
    pos = (row0 + lax.broadcasted_iota(jnp.int32, (n, 1), 0) + 1).astype(F32)
    return [1.0 / jnp.minimum(pos, float(w)) for w in POOL_WINDOWS]


def _pool_fwd(ppx, w_pool, pool_scale, name):
    s = ppx.shape[0]
    ts = _tile(s, TS_POOL, POOL_HALO)
    hb = ts // POOL_HALO
    pw = len(POOL_WINDOWS) * LANE

    def body(p_ref, halo_ref, w_ref, sc_ref, y_ref, ext_ref):
        i = pl.program_id(0)
        p = p_ref[...].astype(F32)
        ext_ref[0:POOL_HALO, :] = jnp.where(i > 0, halo_ref[...].astype(F32), 0.0)
        ext_ref[POOL_HALO:, :] = p
        sums = _window_sums(ext_ref[...], +1)
        cnt = _pool_counts(i * ts, ts)
        for g in range(len(POOL_WINDOWS)):
            cols = slice(g * LANE, (g + 1) * LANE)
            mixed = sums[g][POOL_HALO:, :] * cnt[g] - p[:, cols]
            y = jnp.dot(mixed.astype(BF16), w_ref[g], preferred_element_type=F32)
            y_ref[:, cols] = (y * sc_ref[:, cols]).astype(y_ref.dtype)

    return pl.pallas_call(
        body, grid=(s // ts,),
        in_specs=[pl.BlockSpec((ts, pw), lambda i: (i, 0)), pl.BlockSpec((POOL_HALO, pw), lambda i: (jnp.maximum(i * hb - 1, 0), 0)),
                  _fixed((len(POOL_WINDOWS), LANE, LANE)), _fixed((1, pw))],
        out_specs=_rows(ts, pw), out_shape=jax.ShapeDtypeStruct((s, pw), BF16),
        scratch_shapes=[pltpu.VMEM((ts + POOL_HALO, pw), F32)],
        compiler_params=_params(("parallel",)), name=name,
    )(ppx, ppx, w_pool, pool_scale)


def _pool_bwd(dyb, ppx, w_pool, pool_scale, name):
    s = ppx.shape[0]
    ts = _tile(s, TS_POOL, POOL_HALO)
    hb = ts // POOL_HALO
    nblk = s // ts
    last_halo = s // POOL_HALO - 1
    ng = len(POOL_WINDOWS)
    pw = ng * LANE

    def body(p_ref, halo_ref, dy_ref, dyn_ref, w_ref, sc_ref, dp_ref, dw_ref, dsc_ref, ext_ref, dext_ref, dm_ref):
        i = pl.program_id(0)

        @pl.when(i == 0)
        def _():
            dw_ref[...] = jnp.zeros_like(dw_ref)
            dsc_ref[...] = jnp.zeros_like(dsc_ref)

        p = p_ref[...].astype(F32)
        ext_ref[0:POOL_HALO, :] = jnp.where(i > 0, halo_ref[...].astype(F32), 0.0)
        ext_ref[POOL_HALO:, :] = p
        sums = _window_sums(ext_ref[...], +1)
        cnt = _pool_counts(i * ts, ts + POOL_HALO)
        sc = sc_ref[...]
        dy = dy_ref[...].astype(F32)
        dyn = jnp.where(i < nblk - 1, dyn_ref[...].astype(F32), 0.0)
        for g in range(ng):
            cols = slice(g * LANE, (g + 1) * LANE)
            wg = w_ref[g]
            mixed = (sums[g][POOL_HALO:, :] * cnt[g][0:ts] - p[:, cols]).astype(BF16)
            ypre = jnp.dot(mixed, wg, preferred_element_type=F32)
            dsc_ref[:, cols] += jnp.sum(dy[:, cols] * ypre, axis=0, keepdims=True)
            dyp = (dy[:, cols] * sc[:, cols]).astype(BF16)
            dypn = (dyn[:, cols] * sc[:, cols]).astype(BF16)
            dw_ref[g] += lax.dot_general(mixed, dyp, (((0,), (0,)), ((), ())), preferred_element_type=F32)
            dm = lax.dot_general(dyp, wg, (((1,), (1,)), ((), ())), preferred_element_type=F32)
            dmn = lax.dot_general(dypn, wg, (((1,), (1,)), ((), ())), preferred_element_type=F32)
            dext_ref[0:ts, cols] = dm * cnt[g][0:ts]
            dext_ref[ts:, cols] = dmn * cnt[g][ts:]
            dm_ref[:, cols] = dm
        lead = _window_sums(dext_ref[...], -1)
        for g in range(ng):
            cols = slice(g * LANE, (g + 1) * LANE)
            dp_ref[:, cols] = (lead[g][0:ts, :] - dm_ref[:, cols]).astype(dp_ref.dtype)

    return pl.pallas_call(
        body, grid=(nblk,),
        in_specs=[pl.BlockSpec((ts, pw), lambda i: (i, 0)), pl.BlockSpec((POOL_HALO, pw), lambda i: (jnp.maximum(i * hb - 1, 0), 0)),
                  pl.BlockSpec((ts, pw), lambda i: (i, 0)), pl.BlockSpec((POOL_HALO, pw), lambda i: (jnp.minimum((i + 1) * hb, last_halo), 0)),
                  _fixed((ng, LANE, LANE)), _fixed((1, pw))],
        out_specs=[_rows(ts, pw), _fixed((ng, LANE, LANE)), _fixed((1, pw))],
        out_shape=[jax.ShapeDtypeStruct((s, pw), BF16), jax.ShapeDtypeStruct((ng, LANE, LANE), F32), jax.ShapeDtypeStruct((1, pw), F32)],
        scratch_shapes=[pltpu.VMEM((ts + POOL_HALO, pw), F32), pltpu.VMEM((ts + POOL_HALO, pw), F32), pltpu.VMEM((ts, pw), F32)],
        compiler_params=_params(("arbitrary",)), name=name,
    )(ppx, ppx, dyb, dyb, w_pool, pool_scale)


def _xattn_fwd(ppx, kv, name):
    s = ppx.shape[0]
    m = kv.shape[0]
    ts = _tile(s, TS_XA, 8)
    xw = XA_HEADS * XA_HD

    def body(q_ref, kv_ref, o_ref):
        for hd in range(XA_HEADS):
            cols = slice(hd * XA_HD, (hd + 1) * XA_HD)
            k = kv_ref[:, hd * XA_HD:(hd + 1) * XA_HD]
            v = kv_ref[:, xw + hd * XA_HD:xw + (hd + 1) * XA_HD]
            sc = lax.dot_general(q_ref[:, cols], k, (((1,), (1,)), ((), ())), preferred_element_type=F32) * XA_SCALE
            ex = jnp.exp(sc - jnp.max(sc, axis=-1, keepdims=True))
            pr = ex * (1.0 / jnp.sum(ex, axis=-1, keepdims=True))
            o_ref[:, cols] = jnp.dot(pr.astype(BF16), v, preferred_element_type=F32).astype(o_ref.dtype)

    return pl.pallas_call(
        body, grid=(s // ts,), in_specs=[pl.BlockSpec((ts, xw), lambda i: (i, 1)), _fixed((m, 2 * xw))],
        out_specs=_rows(ts, xw), out_shape=jax.ShapeDtypeStruct((s, xw), BF16),
        compiler_params=_params(("parallel",)), name=name,
    )(ppx, kv)


def _xattn_bwd(dxc, ppx, kv, name):
    s = ppx.shape[0]
    m = kv.shape[0]
    ts = _tile(s, TS_XA, 8)
    xw = XA_HEADS * XA_HD

    def body(do_ref, q_ref, kv_ref, dq_ref, dkv_ref):
        @pl.when(pl.program_id(0) == 0)
        def _():
            dkv_ref[...] = jnp.zeros_like(dkv_ref)

        for hd in range(XA_HEADS):
            cols = slice(hd * XA_HD, (hd + 1) * XA_HD)
            vcols = slice(xw + hd * XA_HD, xw + (hd + 1) * XA_HD)
            q = q_ref[:, cols]
            k = kv_ref[:, cols]
            v = kv_ref[:, vcols]
            do = do_ref[:, cols]
            sc = lax.dot_general(q, k, (((1,), (1,)), ((), ())), preferred_element_type=F32) * XA_SCALE
            ex = jnp.exp(sc - jnp.max(sc, axis=-1, keepdims=True))
            pr = ex * (1.0 / jnp.sum(ex, axis=-1, keepdims=True))
            dpr = lax.dot_general(do, v, (((1,), (1,)), ((), ())), preferred_element_type=F32)
            dsc = (pr * (dpr - jnp.sum(dpr * pr, axis=-1, keepdims=True)) * XA_SCALE).astype(BF16)
            dq_ref[:, cols] = jnp.dot(dsc, k, preferred_element_type=F32).astype(dq_ref.dtype)
            dkv_ref[:, cols] += lax.dot_general(dsc, q, (((0,), (0,)), ((), ())), preferred_element_type=F32)
            dkv_ref[:, vcols] += lax.dot_general(pr.astype(BF16), do, (((0,), (0,)), ((), ())), preferred_element_type=F32)

    return pl.pallas_call(
        body, grid=(s // ts,), in_specs=[_rows(ts, xw), pl.BlockSpec((ts, xw), lambda i: (i, 1)), _fixed((m, 2 * xw))],
        out_specs=[_rows(ts, xw), _fixed((m, 2 * xw))],
        out_shape=[jax.ShapeDtypeStruct((s, xw), BF16), jax.ShapeDtypeStruct((m, 2 * xw), F32)],
        compiler_params=_params(("arbitrary",)), name=name,
    )(dxc, ppx, kv)


def _merge_fwd(pgt, ya, yb, yc, name):
    s = pgt.shape[0]
    ts = _tile(s, TS_ROW, 8)

    def body(gt_ref, ya_ref, yb_ref, yc_ref, o_ref):
        acc = _sigmoid(gt_ref[:, 0:D].astype(F32)) * ya_ref[...].astype(F32)
        acc = acc + _sigmoid(gt_ref[:, D:2 * D].astype(F32)) * yb_ref[...].astype(F32)
        acc = acc + _sigmoid(gt_ref[:, 2 * D:3 * D].astype(F32)) * yc_ref[...].astype(F32)
        o_ref[...] = acc.astype(o_ref.dtype)

    return pl.pallas_call(
        body, grid=(s // ts,), in_specs=[_rows(ts, 3 * D)] + [_rows(ts, D)] * 3, out_specs=_rows(ts, D),
        out_shape=jax.ShapeDtypeStruct((s, D), BF16), compiler_params=_params(("parallel",)), name=name,
    )(pgt, ya, yb, yc)


def _merge_bwd(dmerged, pgt, ya, yb, yc, name):
    s = pgt.shape[0]
    ts = _tile(s, TS_ROW, 8)

    def body(dm_ref, gt_ref, ya_ref, yb_ref, yc_ref, dya_ref, dyb_ref, dyc_ref, dgt_ref):
        dm = dm_ref[...].astype(F32)
        for j, (y_ref, dy_ref) in enumerate(((ya_ref, dya_ref), (yb_ref, dyb_ref), (yc_ref, dyc_ref))):
            sig = _sigmoid(gt_ref[:, j * D:(j + 1) * D].astype(F32))
            dy_ref[...] = (dm * sig).astype(dy_ref.dtype)
            dgt_ref[:, j * D:(j + 1) * D] = (dm * y_ref[...].astype(F32) * sig * (1.0 - sig)).astype(dgt_ref.dtype)

    return pl.pallas_call(
        body, grid=(s // ts,), in_specs=[_rows(ts, D), _rows(ts, 3 * D)] + [_rows(ts, D)] * 3,
        out_specs=[_rows(ts, D)] * 3 + [_rows(ts, 3 * D)],
        out_shape=[jax.ShapeDtypeStruct((s, D), BF16)] * 3 + [jax.ShapeDtypeStruct((s, 3 * D), BF16)],
        compiler_params=_params(("parallel",)), name=name,
    )(dmerged, pgt, ya, yb, yc)


def _resident(shape):
    nd = len(shape)
    return pl.BlockSpec(shape, lambda i: (0,) * nd, pipeline_mode=pl.Buffered(1))


def _mix_tail_fwd(ya_in, yb_in, xc, pgt, w_ups, w_o, x, g_post, g_next, after, name, tm=512):
    s, d = x.shape
    tm = _tile(s, tm)
    na = len(after)
    branch_ins = (ya_in, yb_in, xc)

    def body(a_ref, b_ref, c_ref, gt_ref, wa_ref, wb_ref, wc_ref, wo_ref, x_ref, gp_ref, gn_ref, *rest):
        ya_ref, yb_ref, yc_ref, m_ref, y_ref, xo_ref, h_ref = rest[na:]
        for rows in _sub_blocks(tm):
            merged = None
            for j, (in_ref, w_ref, out_ref) in enumerate(((a_ref, wa_ref, ya_ref), (b_ref, wb_ref, yb_ref), (c_ref, wc_ref, yc_ref))):
                yj = jnp.dot(in_ref[rows, :], w_ref[...], preferred_element_type=F32)
                out_ref[rows, :] = yj.astype(out_ref.dtype)
                part = _sigmoid(gt_ref[rows, j * D:(j + 1) * D].astype(F32)) * yj
                merged = part if merged is None else merged + part
            merged_b = merged.astype(m_ref.dtype)
            m_ref[rows, :] = merged_b
            y = jnp.dot(merged_b, wo_ref[...], preferred_element_type=F32)
            y_ref[rows, :] = y
            yh, _ = _rms(y)
            xn = x_ref[rows, :] + yh * gp_ref[...]
            xo_ref[rows, :] = xn
            xh, _ = _rms(xn)
            h_ref[rows, :] = (xh * gn_ref[...]).astype(h_ref.dtype)

    bf = lambda: jax.ShapeDtypeStruct((s, d), BF16)
    f32 = lambda: jax.ShapeDtypeStruct((s, d), F32)
    return pl.pallas_call(
        body, grid=(s // tm,),
        in_specs=[_rows(tm, a.shape[1]) for a in branch_ins] + [_rows(tm, 3 * d)] + [_resident(w.shape) for w in w_ups]
        + [_resident(w_o.shape), _rows(tm, d), _fixed((1, d)), _fixed((1, d))] + [ANY] * na,
        out_specs=[_rows(tm, d)] * 7,
        out_shape=[bf(), bf(), bf(), bf(), f32(), f32(), bf()],
        compiler_params=_params(("parallel",)), name=name,
    )(*branch_ins, pgt, *w_ups, w_o, x, g_post, g_next, *after)


def _mix_tail_bwd(dy, pgt, ys, w_ups, w_o, after, name, tm=512):
    s, d = dy.shape
    tm = _tile(s, tm)
    na = len(after)
    widths = [w.shape[0] for w in w_ups]

    def body(dy_ref, gt_ref, ya_ref, yb_ref, yc_ref, wa_ref, wb_ref, wc_ref, wo_ref, *rest):
        dya_ref, dyb_ref, dyc_ref, dgt_ref, da_ref, db_ref, dc_ref = rest[na:]
        nt = (((1,), (1,)), ((), ()))
        for rows in _sub_blocks(tm):
            dm = lax.dot_general(dy_ref[rows, :], wo_ref[...], nt, preferred_element_type=F32)
            for j, (y_ref, dyj_ref, w_ref, din_ref) in enumerate(((ya_ref, dya_ref, wa_ref, da_ref), (yb_ref, dyb_ref, wb_ref, db_ref),
                                                                   (yc_ref, dyc_ref, wc_ref, dc_ref))):
                sig = _sigmoid(gt_ref[rows, j * D:(j + 1) * D].astype(F32))
                dyj = (dm * sig).astype(dyj_ref.dtype)
                dyj_ref[rows, :] = dyj
                dgt_ref[rows, j * D:(j + 1) * D] = (dm * y_ref[rows, :].astype(F32) * sig * (1.0 - sig)).astype(dgt_ref.dtype)
                din_ref[rows, :] = lax.dot_general(dyj, w_ref[...], nt, preferred_element_type=F32).astype(din_ref.dtype)

    bf = lambda w: jax.ShapeDtypeStruct((s, w), BF16)
    return pl.pallas_call(
        body, grid=(s // tm,),
        in_specs=[_rows(tm, d), _rows(tm, 3 * d)] + [_rows(tm, d)] * 3 + [_resident(w.shape) for w in w_ups] + [_resident(w_o.shape)]
        + [ANY] * na,
        out_specs=[_rows(tm, d)] * 3 + [_rows(tm, 3 * d)] + [_rows(tm, w) for w in widths],
        out_shape=[bf(d), bf(d), bf(d), bf(3 * d)] + [bf(w) for w in widths],
        compiler_params=_params(("parallel",)), name=name,
    )(dy, pgt, *ys, *w_ups, w_o, *after)


def _adam_math(w, g, m, v):
    mn = ADAM_B1 * m + (1.0 - ADAM_B1) * g
    vn = ADAM_B2 * v + (1.0 - ADAM_B2) * (g * g)
    m_hat = mn / (1.0 - ADAM_B1 ** ADAM_STEP)
    v_hat = vn / (1.0 - ADAM_B2 ** ADAM_STEP)
    return -ADAM_LR * (m_hat / (jnp.sqrt(v_hat) + ADAM_EPS) + ADAM_WD * w), mn, vn


def _adamw(w, g, m, v, name):
    r, c = w.shape[-2:]
    tr, tc = _block_of(r, c, cap=512 if r % 16 == 0 else 256)

    def spec(a):
        if a.ndim == 2:
            return pl.BlockSpec((tr, tc), lambda i, j: (i, j))
        return pl.BlockSpec((None, tr, tc), lambda i, j: (0, i, j))

    def body(w_ref, g_ref, m_ref, v_ref, d_ref, mo_ref, vo_ref):
        d_ref[...], mo_ref[...], vo_ref[...] = _adam_math(w_ref[...], g_ref[...], m_ref[...], v_ref[...])

    return pl.pallas_call(
        body, grid=(r // tr, c // tc), in_specs=[spec(a) for a in (w, g, m, v)], out_specs=[spec(w)] * 3,
        out_shape=[jax.ShapeDtypeStruct(w.shape, F32)] * 3, compiler_params=_params(("parallel", "parallel")), name=name,
    )(w, g, m, v)


ANY = pl.BlockSpec(memory_space=pl.ANY)


def _place():
    x, y, c = lax.axis_index("x"), lax.axis_index("y"), lax.axis_index("c")
    chips = [(1 - x, y), (x, 1 - y), (1 - x, 1 - y)]
    return x, y, c, chips


def _half(c, rows):
    h = rows // 2
    return pl.ds(pl.multiple_of(c * h, 8), h)


def _by_cols(rows):
    return rows % 32 != 0 and rows != 16


def _half_of(ref, lead, c):
    r, cols = ref.shape[-2:]
    if _by_cols(r):
        return ref.at[(*lead, slice(None), pl.ds(pl.multiple_of(c * (cols // 2), LANE), cols // 2))]
    return ref.at[(*lead, pl.ds(pl.multiple_of(c * (r // 2), 8), r // 2))]


def _half_shape(shape):
    r, cols = shape[-2:]
    return shape[:-2] + ((r, cols // 2) if _by_cols(r) else (r // 2, cols))


def _block_of(r, cols, cap=256):
    if r % 16 == 0:
        return _tile(r, cap, 16), cols
    return r, _tile(cols, cap)


def _place_shard(shard, chip_arr, out_dtype, name, after=()):
    _, r, cols = shard.shape
    tr, tc = _block_of(r, cols)

    def body(chip_ref, s_ref, *rest):
        o_ref = rest[len(after)]
        o_ref[...] = s_ref[...].astype(o_ref.dtype)

    return pl.pallas_call(
        body,
        grid_spec=pltpu.PrefetchScalarGridSpec(
            num_scalar_prefetch=1, grid=(r // tr, cols // tc),
            in_specs=[pl.BlockSpec((None, tr, tc), lambda i, j, chip_ref: (0, i, j))] + [ANY] * len(after),
            out_specs=pl.BlockSpec((None, tr, tc), lambda i, j, chip_ref: (chip_ref[0], i, j))),
        out_shape=jax.ShapeDtypeStruct((4, r, cols), out_dtype),
        compiler_params=_params(("parallel", "parallel")), name=name,
    )(chip_arr, shard, *after)


def _gather_shards(bufs, name):
    n = len(bufs)

    def body(*refs):
        outs = refs[n:2 * n]
        send_ici, recv_ici, send_d2d, recv_d2d = refs[2 * n:]
        x, y, c, chips = _place()
        me = 2 * x + y
        sibling = (x, y, 1 - c)

        def ici(w, p, chip_of_block, to):
            rows = _half(c, outs[w].shape[1])
            block = outs[w].at[chip_of_block, rows]
            return pltpu.make_async_remote_copy(
                src_ref=block, dst_ref=block, send_sem=send_ici.at[w, p], recv_sem=recv_ici.at[w, p], device_id=to, device_id_type=MESH)

        def d2d(w, p, chip_of_block, half_of):
            rows = _half(half_of, outs[w].shape[1])
            block = outs[w].at[chip_of_block, rows]
            return pltpu.make_async_remote_copy(
                src_ref=block, dst_ref=block, send_sem=send_d2d.at[w, p], recv_sem=recv_d2d.at[w, p], device_id=sibling, device_id_type=MESH)

        sends = [ici(w, p, me, (*chip, c)) for p, chip in enumerate(chips) for w in range(n)]
        for cp in sends:
            cp.start()
        passed = []
        for p, (px, py) in enumerate(chips):
            for w in range(n):
                ici(w, p, 2 * px + py, (px, py, c)).wait_recv()
                fwd = d2d(w, p, 2 * px + py, c)
                fwd.start()
                passed.append(fwd)
        for p, (px, py) in enumerate(chips):
            for w in range(n):
                d2d(w, p, 2 * px + py, 1 - c).wait_recv()
        for cp in sends + passed:
            cp.wait_send()

    return pl.pallas_call(
        body, in_specs=[ANY] * n, out_specs=[ANY] * n,
        out_shape=[jax.ShapeDtypeStruct(a.shape, a.dtype) for a in bufs],
        input_output_aliases={w: w for w in range(n)},
        scratch_shapes=[pltpu.SemaphoreType.DMA((n, 3))] * 4,
        compiler_params=pltpu.CompilerParams(has_side_effects=True), name=name,
    )(*bufs)


HBM = pl.BlockSpec(memory_space=pltpu.HBM)
SEM = pl.BlockSpec(memory_space=pltpu.SEMAPHORE)
EFFECT = pltpu.SideEffectType.DATAFLOW_SIDE_EFFECTING


def _in_hbm(arrays):
    return [pltpu.with_memory_space_constraint(a, pltpu.HBM) for a in arrays]


def _gather_start(bufs, after, name):
    n, na = len(bufs), len(after)

    def body(*refs):
        send_sem, recv_sem = refs[n + na], refs[n + na + 1]
        outs = refs[n + na + 2:2 * n + na + 2]
        token = refs[2 * n + na + 2]
        x, y, c, chips = _place()
        me = 2 * x + y
        for p, chip in enumerate(chips):
            for w in range(n):
                block = _half_of(outs[w], (me,), c)
                pltpu.make_async_remote_copy(
                    src_ref=block, dst_ref=block, send_sem=send_sem, recv_sem=recv_sem,
                    device_id=(*chip, c), device_id_type=MESH).start()
        token[...] = jnp.zeros_like(token)

    out = pl.pallas_call(
        body, name=name, in_specs=[HBM] * n + [ANY] * na,
        out_specs=[SEM, SEM] + [HBM] * n + [pl.BlockSpec(memory_space=pltpu.VMEM)],
        out_shape=[pltpu.SemaphoreType.DMA(()), pltpu.SemaphoreType.DMA(())]
        + [pltpu.HBM(a.shape, a.dtype) for a in bufs] + [jax.ShapeDtypeStruct((8, LANE), F32)],
        input_output_aliases={w: w + 2 for w in range(n)},
        compiler_params=pltpu.CompilerParams(has_side_effects=EFFECT),
    )(*_in_hbm(bufs), *after)
    return out[0], out[1], list(out[2:2 + n]), out[2 + n]


def _gather_pass(bufs, send_sem, recv_sem, after, name):
    n, na = len(bufs), len(after)

    def body(*refs):
        send1, recv1 = refs[n], refs[n + 1]
        send2, recv2 = refs[n + 2 + na], refs[n + 3 + na]
        outs = refs[n + 4 + na:2 * n + 4 + na]
        x, y, c, chips = _place()
        me = 2 * x + y
        arrivals = [(w, px, py) for px, py in chips for w in range(n)]
        for w, px, py in arrivals:
            first = pltpu.make_async_remote_copy(
                src_ref=_half_of(outs[w], (me,), c), dst_ref=_half_of(outs[w], (2 * px + py,), c), send_sem=send1, recv_sem=recv1,
                device_id=(px, py, c), device_id_type=MESH)
            first.wait_send()
            first.wait_recv()
        for w, px, py in arrivals:
            arrived = _half_of(outs[w], (2 * px + py,), c)
            pltpu.make_async_remote_copy(
                src_ref=arrived, dst_ref=arrived, send_sem=send2, recv_sem=recv2,
                device_id=(x, y, 1 - c), device_id_type=MESH).start()

    out = pl.pallas_call(
        body, name=name, in_specs=[HBM] * n + [SEM, SEM] + [ANY] * na,
        out_specs=[SEM, SEM] + [HBM] * n,
        out_shape=[pltpu.SemaphoreType.DMA(()), pltpu.SemaphoreType.DMA(())] + [pltpu.HBM(a.shape, a.dtype) for a in bufs],
        input_output_aliases={w: w + 2 for w in range(n)},
        compiler_params=pltpu.CompilerParams(has_side_effects=EFFECT),
    )(*bufs, send_sem, recv_sem, *after)
    return out[0], out[1], list(out[2:])


def _gather_finish(bufs, send_sem, recv_sem, after, name):
    n, na = len(bufs), len(after)

    def body(*refs):
        send2, recv2 = refs[n], refs[n + 1]
        outs = refs[n + 2 + na:2 * n + 2 + na]
        x, y, c, chips = _place()
        for p, (px, py) in enumerate(chips):
            for w in range(n):
                passed = pltpu.make_async_remote_copy(
                    src_ref=_half_of(outs[w], (2 * px + py,), c), dst_ref=_half_of(outs[w], (2 * px + py,), 1 - c),
                    send_sem=send2, recv_sem=recv2, device_id=(x, y, 1 - c), device_id_type=MESH)
                passed.wait_send()
                passed.wait_recv()

    out = pl.pallas_call(
        body, name=name, in_specs=[HBM] * n + [SEM, SEM] + [ANY] * na, out_specs=[HBM] * n,
        out_shape=[pltpu.HBM(a.shape, a.dtype) for a in bufs],
        input_output_aliases={w: w for w in range(n)},
        compiler_params=pltpu.CompilerParams(has_side_effects=EFFECT),
    )(*bufs, send_sem, recv_sem, *after)
    return list(out)


def _pair_exchange(grads, name):
    n = len(grads)

    def body(*refs):
        ins, outs = refs[:n], refs[n:2 * n]
        send_sem, recv_sem = refs[2 * n:]
        x, y, c, _ = _place()
        copies = []
        for w in range(n):
            copies.append(pltpu.make_async_remote_copy(
                src_ref=_half_of(ins[w], (slice(None),), 1 - c), dst_ref=outs[w], send_sem=send_sem.at[w], recv_sem=recv_sem.at[w],
                device_id=(x, y, 1 - c), device_id_type=MESH))
        for cp in copies:
            cp.start()
        for cp in copies:
            cp.wait()

    return pl.pallas_call(
        body, in_specs=[ANY] * n, out_specs=[ANY] * n,
        out_shape=[jax.ShapeDtypeStruct(_half_shape(a.shape), a.dtype) for a in grads],
        scratch_shapes=[pltpu.SemaphoreType.DMA((n,))] * 2,
        compiler_params=pltpu.CompilerParams(has_side_effects=True), name=name,
    )(*grads)


def _pair_exchange_start(grads, after, name):
    n, na = len(grads), len(after)
    lands = [lax.empty(_half_shape(a.shape), a.dtype) for a in grads]

    def body(*refs):
        send_sem, recv_sem = refs[2 * n + na], refs[2 * n + na + 1]
        srcs = refs[2 * n + na + 2:3 * n + na + 2]
        dsts = refs[3 * n + na + 2:4 * n + na + 2]
        token = refs[4 * n + na + 2]
        x, y, c, _ = _place()
        for w in range(n):
            pltpu.make_async_remote_copy(
                src_ref=_half_of(srcs[w], (slice(None),), 1 - c), dst_ref=dsts[w], send_sem=send_sem, recv_sem=recv_sem,
                device_id=(x, y, 1 - c), device_id_type=MESH).start()
        token[...] = jnp.zeros_like(token)

    out = pl.pallas_call(
        body, name=name, in_specs=[HBM] * (2 * n) + [ANY] * na,
        out_specs=[SEM, SEM] + [HBM] * (2 * n) + [pl.BlockSpec(memory_space=pltpu.VMEM)],
        out_shape=[pltpu.SemaphoreType.DMA(()), pltpu.SemaphoreType.DMA(())]
        + [pltpu.HBM(a.shape, a.dtype) for a in grads + lands] + [jax.ShapeDtypeStruct((8, LANE), F32)],
        input_output_aliases={w: w + 2 for w in range(2 * n)},
        compiler_params=pltpu.CompilerParams(has_side_effects=EFFECT),
    )(*_in_hbm(grads), *_in_hbm(lands), *after)
    return out[0], out[1], list(out[2:2 + n]), list(out[2 + n:2 + 2 * n]), out[2 + 2 * n]


def _pair_exchange_finish(grads, lands, send_sem, recv_sem, after, name):
    n, na = len(grads), len(after)

    def body(*refs):
        send, recv = refs[2 * n], refs[2 * n + 1]
        srcs = refs[2 * n + 2 + na:3 * n + 2 + na]
        dsts = refs[3 * n + 2 + na:4 * n + 2 + na]
        x, y, c, _ = _place()
        for w in range(n):
            copy = pltpu.make_async_remote_copy(
                src_ref=_half_of(srcs[w], (slice(None),), 1 - c), dst_ref=dsts[w], send_sem=send, recv_sem=recv,
                device_id=(x, y, 1 - c), device_id_type=MESH)
            copy.wait_send()
            copy.wait_recv()

    out = pl.pallas_call(
        body, name=name, in_specs=[HBM] * (2 * n) + [SEM, SEM] + [ANY] * na, out_specs=[HBM] * (2 * n),
        out_shape=[pltpu.HBM(a.shape, a.dtype) for a in grads + lands],
        input_output_aliases={w: w for w in range(2 * n)},
        compiler_params=pltpu.CompilerParams(has_side_effects=EFFECT),
    )(*grads, *lands, send_sem, recv_sem, *after)
    return list(out[:n]), list(out[n:])


def _pair_sum(g, got, c_arr, name):
    _, r, cols = g.shape
    hr, hc = _half_shape((r, cols))
    tr, tc = _block_of(hr, hc)
    nbr, nbc = hr // tr, hc // tc
    by_cols = _by_cols(r)

    def body(c_ref, g_ref, got_ref, o_ref):
        o_ref[...] = (g_ref[...].astype(F32) + got_ref[...].astype(F32)).astype(o_ref.dtype)

    def mine(j, i, k, c_ref):
        return (j, i, c_ref[0] * nbc + k) if by_cols else (j, c_ref[0] * nbr + i, k)

    return pl.pallas_call(
        body,
        grid_spec=pltpu.PrefetchScalarGridSpec(
            num_scalar_prefetch=1, grid=(4, nbr, nbc),
            in_specs=[pl.BlockSpec((None, tr, tc), mine),
                      pl.BlockSpec((None, tr, tc), lambda j, i, k, c_ref: (j, i, k))],
            out_specs=pl.BlockSpec((None, tr, tc), lambda j, i, k, c_ref: (j, i, k))),
        out_shape=jax.ShapeDtypeStruct((4, hr, hc), BF16),
        compiler_params=_params(("parallel", "parallel", "parallel")), name=name,
    )(c_arr, g, got)


def _chip_exchange(parts, name):
    n = len(parts)

    def body(*refs):
        ins, outs = refs[:n], refs[n:2 * n]
        send_sem, recv_sem = refs[2 * n:]
        x, y, c, chips = _place()
        copies = []
        for p, (px, py) in enumerate(chips):
            for w in range(n):
                copies.append(pltpu.make_async_remote_copy(
                    src_ref=ins[w].at[2 * px + py], dst_ref=outs[w].at[p], send_sem=send_sem.at[w, p], recv_sem=recv_sem.at[w, p],
                    device_id=(px, py, c), device_id_type=MESH))
        for cp in copies:
            cp.start()
        for cp in copies:
            cp.wait()

    return pl.pallas_call(
        body, in_specs=[ANY] * n, out_specs=[ANY] * n,
        out_shape=[jax.ShapeDtypeStruct((3,) + a.shape[1:], a.dtype) for a in parts],
        scratch_shapes=[pltpu.SemaphoreType.DMA((n, 3))] * 2,
        compiler_params=pltpu.CompilerParams(has_side_effects=True), name=name,
    )(*parts)


def _chip_exchange_start(parts, after, name):
    n, na = len(parts), len(after)
    lands = [lax.empty((3,) + a.shape[1:], a.dtype) for a in parts]

    def body(*refs):
        send_sem, recv_sem = refs[2 * n + na], refs[2 * n + na + 1]
        srcs = refs[2 * n + na + 2:3 * n + na + 2]
        dsts = refs[3 * n + na + 2:4 * n + na + 2]
        token = refs[4 * n + na + 2]
        x, y, c, chips = _place()
        for p, (px, py) in enumerate(chips):
            for w in range(n):
                pltpu.make_async_remote_copy(
                    src_ref=srcs[w].at[2 * px + py], dst_ref=dsts[w].at[p], send_sem=send_sem, recv_sem=recv_sem,
                    device_id=(px, py, c), device_id_type=MESH).start()
        token[...] = jnp.zeros_like(token)

    out = pl.pallas_call(
        body, name=name, in_specs=[HBM] * (2 * n) + [ANY] * na,
        out_specs=[SEM, SEM] + [HBM] * (2 * n) + [pl.BlockSpec(memory_space=pltpu.VMEM)],
        out_shape=[pltpu.SemaphoreType.DMA(()), pltpu.SemaphoreType.DMA(())]
        + [pltpu.HBM(a.shape, a.dtype) for a in parts + lands] + [jax.ShapeDtypeStruct((8, LANE), F32)],
        input_output_aliases={w: w + 2 for w in range(2 * n)},
        compiler_params=pltpu.CompilerParams(has_side_effects=EFFECT),
    )(*_in_hbm(parts), *_in_hbm(lands), *after)
    return out[0], out[1], list(out[2:2 + n]), list(out[2 + n:2 + 2 * n]), out[2 + 2 * n]


def _chip_exchange_finish(parts, lands, send_sem, recv_sem, after, name):
    n, na = len(parts), len(after)

    def body(*refs):
        send, recv = refs[2 * n], refs[2 * n + 1]
        srcs = refs[2 * n + 2 + na:3 * n + 2 + na]
        dsts = refs[3 * n + 2 + na:4 * n + 2 + na]
        x, y, c, chips = _place()
        for p, (px, py) in enumerate(chips):
            for w in range(n):
                copy = pltpu.make_async_remote_copy(
                    src_ref=srcs[w].at[2 * px + py], dst_ref=dsts[w].at[p], send_sem=send, recv_sem=recv,
                    device_id=(px, py, c), device_id_type=MESH)
                copy.wait_send()
                copy.wait_recv()

    out = pl.pallas_call(
        body, name=name, in_specs=[HBM] * (2 * n) + [SEM, SEM] + [ANY] * na, out_specs=[HBM] * (2 * n),
        out_shape=[pltpu.HBM(a.shape, a.dtype) for a in parts + lands],
        input_output_aliases={w: w for w in range(2 * n)},
        compiler_params=pltpu.CompilerParams(has_side_effects=EFFECT),
    )(*parts, *lands, send_sem, recv_sem, *after)
    return list(out[:n]), list(out[n:])


def _chip_sum(part, got, place_arr, name):
    _, hr, hc = part.shape
    by_cols = _by_cols(hr)
    tr, tc = _block_of(hr, hc)
    nbr, nbc = hr // tr, hc // tc

    def body(place_ref, p_ref, got_ref, o_ref):
        acc = p_ref[...].astype(F32)
        for p in range(3):
            acc = acc + got_ref[p].astype(F32)
        o_ref[...] = acc

    def mine(i, k, place_ref):
        return (i, place_ref[1] * nbc + k) if by_cols else (place_ref[1] * nbr + i, k)

    return pl.pallas_call(
        body,
        grid_spec=pltpu.PrefetchScalarGridSpec(
            num_scalar_prefetch=1, grid=(nbr, nbc),
            in_specs=[pl.BlockSpec((None, tr, tc), lambda i, k, place_ref: (place_ref[0], i, k)),
                      pl.BlockSpec((3, tr, tc), lambda i, k, place_ref: (0, i, k))],
            out_specs=pl.BlockSpec((tr, tc), mine)),
        out_shape=jax.ShapeDtypeStruct((hr, 2 * hc) if by_cols else (2 * hr, hc), F32),
        compiler_params=_params(("parallel", "parallel")), name=name,
    )(place_arr, part, got)


def _pair_join_start(bufs, name):
    n = len(bufs)

    def body(*refs):
        send_sem, recv_sem = refs[n], refs[n + 1]
        outs = refs[n + 2:2 * n + 2]
        token = refs[2 * n + 2]
        x, y, c, _ = _place()
        for w in range(n):
            block = _half_of(outs[w], (), c)
            pltpu.make_async_remote_copy(
                src_ref=block, dst_ref=block, send_sem=send_sem, recv_sem=recv_sem,
                device_id=(x, y, 1 - c), device_id_type=MESH).start()
        token[...] = jnp.zeros_like(token)

    out = pl.pallas_call(
        body, name=name, in_specs=[HBM] * n,
        out_specs=[SEM, SEM] + [HBM] * n + [pl.BlockSpec(memory_space=pltpu.VMEM)],
        out_shape=[pltpu.SemaphoreType.DMA(()), pltpu.SemaphoreType.DMA(())]
        + [pltpu.HBM(a.shape, a.dtype) for a in bufs] + [jax.ShapeDtypeStruct((8, LANE), F32)],
        input_output_aliases={w: w + 2 for w in range(n)},
        compiler_params=pltpu.CompilerParams(has_side_effects=EFFECT),
    )(*_in_hbm(bufs))
    return out[0], out[1], list(out[2:2 + n]), out[2 + n]


def _pair_join_finish(bufs, send_sem, recv_sem, after, name):
    n, na = len(bufs), len(after)

    def body(*refs):
        send, recv = refs[n], refs[n + 1]
        outs = refs[n + 2 + na:2 * n + 2 + na]
        x, y, c, _ = _place()
        for w in range(n):
            copy = pltpu.make_async_remote_copy(
                src_ref=_half_of(outs[w], (), c), dst_ref=_half_of(outs[w], (), 1 - c), send_sem=send, recv_sem=recv,
                device_id=(x, y, 1 - c), device_id_type=MESH)
            copy.wait_send()
            copy.wait_recv()

    out = pl.pallas_call(
        body, name=name, in_specs=[HBM] * n + [SEM, SEM] + [ANY] * na, out_specs=[HBM] * n,
        out_shape=[pltpu.HBM(a.shape, a.dtype) for a in bufs],
        input_output_aliases={w: w for w in range(n)},
        compiler_params=pltpu.CompilerParams(has_side_effects=EFFECT),
    )(*bufs, send_sem, recv_sem, *after)
    return list(out)


SMALL = ("ffn1_pre_g", "ffn1_post_g", "mix_pre_g", "gla_norm_g", "mem_norm_g", "mix_post_g", "ffn2_pre_g", "ffn2_post_g", "final_g",
         "b_f", "pool_scale", "w_pool", "w_fu")
N_GAINS = 9
SMALL_PACKS = ((16, D), (24, 512), (4 * LANE, LANE))
W_FU_ROW = 8


def _all_sum_small(gs, name, after=()):
    ins = [gs[n] for n in SMALL[:N_GAINS]] + [gs["b_f"], gs["pool_scale"], gs["w_fu_pad"], gs["w_pool"].reshape(4 * LANE, LANE)]

    def body(*refs):
        gain_refs = refs[:N_GAINS]
        bf_ref, ps_ref, wfu_ref, wp_ref = refs[N_GAINS:N_GAINS + 4]
        outs = refs[N_GAINS + 4 + len(after):N_GAINS + 7 + len(after)]
        mine_a, mine_b, all_a, all_b, all_c, send_sems, recv_sems = refs[N_GAINS + 7 + len(after):]
        mine_a[...] = jnp.zeros_like(mine_a)
        for i, ref in enumerate(gain_refs):
            mine_a[i:i + 1, :] = ref[...]
        mine_b[...] = jnp.zeros_like(mine_b)
        mine_b[0:1, :] = bf_ref[...]
        mine_b[1:2, :] = ps_ref[...]
        mine_b[W_FU_ROW:W_FU_ROW + GATE_RANK, :] = wfu_ref[0:GATE_RANK, :]
        packs = ((mine_a, all_a), (mine_b, all_b), (wp_ref, all_c))
        x, y, c, chips = _place()
        me, sibling = (x, y, c), (x, y, 1 - c)

        def copy(t, k, block, to, own=False):
            px, py, pc = block
            slot = packs[t][1].at[4 * px + 2 * py + pc]
            return pltpu.make_async_remote_copy(
                src_ref=packs[t][0] if own else slot, dst_ref=slot,
                send_sem=send_sems.at[t, k], recv_sem=recv_sems.at[t, k], device_id=to, device_id_type=MESH)

        started = []
        for t, (mine, everyone) in enumerate(packs):
            everyone[4 * x + 2 * y + c] = mine[...]
            started.append(copy(t, 0, me, sibling, own=True))
            started += [copy(t, 1 + j, me, (*chip, c), own=True) for j, chip in enumerate(chips)]
        for cp in started:
            cp.start()
        passed = []
        for j, chip in enumerate(chips):
            for t in range(len(packs)):
                copy(t, 1 + j, (*chip, c), me).wait_recv()
                fwd = copy(t, 4 + j, (*chip, c), sibling)
                fwd.start()
                passed.append(fwd)
        for t in range(len(packs)):
            copy(t, 0, sibling, me).wait_recv()
            for j, chip in enumerate(chips):
                copy(t, 4 + j, (*chip, 1 - c), me).wait_recv()
        for cp in started + passed:
            cp.wait_send()
        for (_, everyone), o_ref in zip(packs, outs):
            acc = everyone[0]
            for k in range(1, 8):
                acc = acc + everyone[k]
            o_ref[...] = acc

    vmem = pl.BlockSpec(memory_space=pltpu.VMEM)
    return pl.pallas_call(
        body, in_specs=[vmem] * len(ins) + [ANY] * len(after), out_specs=[vmem] * 3,
        out_shape=[jax.ShapeDtypeStruct(shape, F32) for shape in SMALL_PACKS],
        scratch_shapes=[pltpu.VMEM(SMALL_PACKS[0], F32), pltpu.VMEM(SMALL_PACKS[1], F32)]
        + [pltpu.VMEM((8,) + shape, F32) for shape in SMALL_PACKS]
        + [pltpu.SemaphoreType.DMA((3, 7)), pltpu.SemaphoreType.DMA((3, 7))],
        compiler_params=pltpu.CompilerParams(has_side_effects=True, vmem_limit_bytes=VMEM_LIMIT), name=name,
    )(*ins, *after)


def _adamw_small(sums, params, chip_arr, name):
    flat = [a for n in SMALL for a in params[n]]

    def body(chip_ref, a_ref, b_ref, c_ref, *refs):
        ins, outs = refs[:len(flat)], refs[len(flat):]
        for i, n in enumerate(SMALL):
            w_ref, m_ref, v_ref = ins[3 * i:3 * i + 3]
            g_ref, d_ref, mo_ref, vo_ref = outs[4 * i:4 * i + 4]
            if n == "w_pool":
                pieces = [((0, k), c_ref[k * LANE:(k + 1) * LANE, :]) for k in range(4)]
            elif n == "w_fu":
                mine = pl.ds(pl.multiple_of(chip_ref[0] * LANE, LANE), LANE)
                pieces = [((0,), b_ref[W_FU_ROW:W_FU_ROW + GATE_RANK, mine])]
            elif n == "b_f":
                pieces = [((), b_ref[0:1, :])]
            elif n == "pool_scale":
                pieces = [((), b_ref[1:2, :])]
            else:
                pieces = [((), a_ref[i:i + 1, :])]
            for at, g in pieces:
                d, mn, vn = _adam_math(w_ref[at], g, m_ref[at], v_ref[at])
                g_ref[at] = g
                d_ref[at] = d
                mo_ref[at] = mn
                vo_ref[at] = vn

    def whole(shape):
        return pl.BlockSpec(shape, lambda i, chip_ref: (0,) * len(shape))

    out = pl.pallas_call(
        body,
        grid_spec=pltpu.PrefetchScalarGridSpec(
            num_scalar_prefetch=1, grid=(1,),
            in_specs=[whole(a.shape) for a in list(sums) + flat],
            out_specs=[whole(params[n][0].shape) for n in SMALL for _ in range(4)]),
        out_shape=[jax.ShapeDtypeStruct(params[n][0].shape, F32) for n in SMALL for _ in range(4)],
        compiler_params=_params(("arbitrary",)), name=name,
    )(chip_arr, *sums, *flat)
    return {n: tuple(out[4 * i:4 * i + 4]) for i, n in enumerate(SMALL)}


def _ffn_bwd(dz, x_norm, ab, u, w_in, w_out, x, g_pre, dres, tag, emit, advance, after=(), post=None):
    dw_out = _mm(u, dz, ta=True, out_dtype=BF16, tm=1408, tk=2048, after=after, name=tag + "_out_dw")
    behind = emit(tag + "_w_out", dw_out)
    dab = _ffn_out_dx_swiglu(dz, w_out, ab, behind, name=tag + "_out_dx")
    behind = advance((dab,))
    dw_in = _mm(x_norm, dab, ta=True, out_dtype=BF16, tm=512, tk=4096, shards=4, after=behind, name=tag + "_in_dw")
    behind = emit(tag + "_w_in", dw_in)
    out = _mm_rms_bwd([(dab, w_in)], x, g_pre, dres, after=behind, post=post, name=tag + "_in_dx")
    return (*out, advance((out[0],)))


def _local_step(x, mem, target, small, gather, emit, advance):
    behind = gather("start", "ffn1i", ())
    behind = gather("start", "ffn1o", behind)
    h1 = _norm_fwd(x, small["ffn1_pre_g"], BF16, name="ffn1_pre", after=behind)
    gather("pass", "ffn1i", (h1,))
    big = gather("finish", "ffn1i", ())
    behind = gather("start", "mixa", (big["ffn1_w_in"],))
    behind = gather("start", "mixb", behind)
    behind = gather("start", "ffn2", behind)
    ab1, u1 = _ffn_in_swiglu(h1, big["ffn1_w_in"], name="ffn1_in", after=behind)
    gather("pass", "ffn1o", (ab1,))
    big.update(gather("finish", "ffn1o", ()))
    behind = gather("pass", "mixa", (u1,))
    f1, x1, h = _mm_resid_norm(u1, big["ffn1_w_out"], x, small["ffn1_post_g"], 0.5, small["mix_pre_g"], name="ffn1_out", after=behind)
    big.update(gather("finish", "mixa", (h,)))
    small = dict(small, w_fu_pad=big["w_fu_pad"])
    pg = _mm(h, big["w_gla_t"], tb=True, out_dtype=BF16, tm=1024, tn=PG_W, name="mix_in_gla")
    behind = gather("pass", "mixb", (pg,))
    ppx = _mm(h, big["w_px_t"], tb=True, out_dtype=BF16, after=behind, name="mix_in_px")
    pgt = _mm(h, big["w_gates_t"], tb=True, out_dtype=BF16, tn=1536, name="mix_in_gates")
    big.update(gather("finish", "mixb", (pgt,)))
    mem_n = _norm_fwd(mem, small["mem_norm_g"], BF16, name="mem_norm")
    kv = _mm(mem_n, big["w_mem_kv"], out_dtype=BF16, name="mem_kv")
    ya_in, sp, so, o_gla = _gla_fwd(pg, small["w_fu_pad"], small["b_f"], small["gla_norm_g"], name="gla_fwd")
    yb_in = _pool_fwd(ppx, small["w_pool_b"], small["pool_scale"], name="pool_fwd")
    xc = _xattn_fwd(ppx, kv, name="xattn_fwd")
    behind = gather("pass", "ffn2", (xc,))
    w_ups = [big["w_up_gla"], big["w_up_pool"], big["w_up_xattn"]]
    ya, yb, yc, merged, ymix, x2, h2 = _mix_tail_fwd(ya_in, yb_in, xc, pgt, w_ups, big["w_o"], x1, small["mix_post_g"],
                                                     small["ffn2_pre_g"], behind, name="mix_tail")
    big.update(gather("finish", "ffn2", (h2,)))
    ab2, u2 = _ffn_in_swiglu(h2, big["ffn2_w_in"], name="ffn2_in")
    gs = {}
    dz2, dx3, gs["ffn2_post_g"], gs["final_g"], loss = _ffn_out_loss(u2, big["ffn2_w_out"], x2, small["ffn2_post_g"], 0.5,
                                                                    small["final_g"], target, name="ffn2_out_loss")
    dx2, gs["ffn2_pre_g"], dy, gs["mix_post_g"], behind = _ffn_bwd(
        dz2, h2, ab2, u2, big["ffn2_w_in"], big["ffn2_w_out"], x2, small["ffn2_pre_g"], dx3, "ffn2", emit, advance,
        post=(ymix, small["mix_post_g"], 1.0))
    emit("w_o", _mm(merged, dy, ta=True, out_dtype=BF16, tm=512, tk=4096, after=behind, name="mix_out_dw"))
    dya, dyb, dyc, dgt, dya_in, dyb_in, dxc = _mix_tail_bwd(dy, pgt, (ya, yb, yc), w_ups, big["w_o"], (), name="mix_tail_bwd")
    emit("w_up_gla", _mm(ya_in, dya, ta=True, out_dtype=BF16, tm=512, tk=4096, name="up_gla_dw"))
    emit("w_up_pool", _mm(yb_in, dyb, ta=True, out_dtype=BF16, tm=512, tk=4096, shards=4, name="up_pool_dw"))
    emit("w_up_xattn", _mm(xc, dyc, ta=True, out_dtype=BF16, tm=512, tk=4096, shards=4, name="up_xattn_dw"))
    dpg, gs["w_fu_pad"], gs["b_f"], gs["gla_norm_g"] = _gla_bwd(pg, sp, so, o_gla, dya_in, small["w_fu_pad"], small["b_f"], small["gla_norm_g"], name="gla_bwd")
    dp, gs["w_pool"], gs["pool_scale"] = _pool_bwd(dyb_in, ppx, small["w_pool_b"], small["pool_scale"], name="pool_bwd")
    dxq, dkv = _xattn_bwd(dxc, ppx, kv, name="xattn_bwd")
    dkv = dkv.astype(BF16)
    emit("w_mem_kv", _mm(mem_n, dkv, ta=True, out_dtype=BF16, name="mem_kv_dw"))
    dmem_n = _mm(dkv, big["w_mem_kv"], tb=True, name="mem_kv_dx")
    _, gs["mem_norm_g"] = _rms_bwd(mem, small["mem_norm_g"], [dmem_n], None, 1.0, BF16, name="mem_norm_bwd")
    emit("w_gla", _mm(dpg, h, ta=True, out_dtype=BF16, tm=640, tk=4096, name="mix_in_gla_dw"))
    emit("w_p", _mm(dp, h, ta=True, out_dtype=BF16, tm=512, tk=4096, name="mix_in_p_dw"))
    emit("w_xq", _mm(dxq, h, ta=True, out_dtype=BF16, tm=512, tk=4096, name="mix_in_xq_dw"))
    behind = emit("w_gates", _mm(dgt, h, ta=True, out_dtype=BF16, tm=512, tk=4096, name="mix_in_gates_dw"))
    pairs = [(dpg, big["w_gla_t"]), (dp, big["w_p_t"]), (dxq, big["w_xq_t"]), (dgt, big["w_gates_t"])]
    dx1, gs["mix_pre_g"], dz1, gs["ffn1_post_g"] = _mm_rms_bwd(pairs, x1, small["mix_pre_g"], dx2, after=behind,
                                                               post=(f1, small["ffn1_post_g"], 0.5), tm=256, name="mix_in_dx")
    behind = advance((dx1,))
    dx0, gs["ffn1_pre_g"], _ = _ffn_bwd(dz1, h1, ab1, u1, big["ffn1_w_in"], big["ffn1_w_out"], x, small["ffn1_pre_g"], dx1,
                                        "ffn1", emit, advance, after=behind)
    return loss, dx0, gs


BIG = ("ffn1_w_in", "ffn1_w_out", "w_in", "w_mem_kv", "w_up_gla", "w_up_pool", "w_up_xattn", "w_o", "ffn2_w_in", "ffn2_w_out")
COL_SHARDED = ("ffn1_w_in", "w_in", "w_up_pool", "w_up_xattn", "ffn2_w_in")
GATHER_GROUPS = {"ffn1i": ("ffn1_w_in",), "ffn1o": ("ffn1_w_out",), "mixa": ("w_in", "w_fu"),
                 "mixb": ("w_mem_kv", "w_up_gla", "w_up_pool", "w_up_xattn", "w_o"), "ffn2": ("ffn2_w_in", "ffn2_w_out")}
REDUCE_GROUPS = {"ffn2": ("ffn2_w_out", "ffn2_w_in"),
                 "mix": ("w_o", "w_up_gla", "w_up_pool", "w_up_xattn", "w_mem_kv", "w_gla", "w_p", "w_xq", "w_gates"),
                 "ffn1_out": ("ffn1_w_out",),
                 "ffn1_in": ("ffn1_w_in",)}
REDUCE_LAST = "ffn1_in"
GAINS = ("ffn1_pre_g", "ffn1_post_g", "mix_pre_g", "gla_norm_g", "mem_norm_g", "mix_post_g", "ffn2_pre_g", "ffn2_post_g", "final_g")
WEIGHTS = ("ffn1_pre_g", "ffn1_w_in", "ffn1_w_out", "ffn1_post_g", "mix_pre_g", "w_in", "w_fu", "b_f", "gla_norm_g", "w_pool",
           "pool_scale", "mem_norm_g", "w_mem_kv", "w_up_gla", "w_up_pool", "w_up_xattn", "w_o", "mix_post_g", "ffn2_pre_g",
           "ffn2_w_in", "ffn2_w_out", "ffn2_post_g", "final_g")
IN_GLA, IN_F, IN_PX, IN_GATES, IN_END = 0, 3072, 3088, 4112, 7184
def _cols_from_shards(g):
    return jnp.transpose(g, (1, 0, 2)).reshape(g.shape[1], 4 * g.shape[2])


def kernel(x, mem, ffn1_pre_g, ffn1_w_in, ffn1_w_out, ffn1_post_g, mix_pre_g, w_in, w_fu, b_f, gla_norm_g, w_pool, pool_scale, mem_norm_g, w_mem_kv, w_up_gla, w_up_pool, w_up_xattn, w_o, mix_post_g, ffn2_pre_g, ffn2_w_in, ffn2_w_out, ffn2_post_g, final_g, loss_target, m_ffn1_pre_g, m_ffn1_w_in, m_ffn1_w_out, m_ffn1_post_g, m_mix_pre_g, m_w_in, m_w_fu, m_b_f, m_gla_norm_g, m_w_pool, m_pool_scale, m_mem_norm_g, m_w_mem_kv, m_w_up_gla, m_w_up_pool, m_w_up_xattn, m_w_o, m_mix_post_g, m_ffn2_pre_g, m_ffn2_w_in, m_ffn2_w_out, m_ffn2_post_g, m_final_g, v_ffn1_pre_g, v_ffn1_w_in, v_ffn1_w_out, v_ffn1_post_g, v_mix_pre_g, v_w_in, v_w_fu, v_b_f, v_gla_norm_g, v_w_pool, v_pool_scale, v_mem_norm_g, v_w_mem_kv, v_w_up_gla, v_w_up_pool, v_w_up_xattn, v_w_o, v_mix_post_g, v_ffn2_pre_g, v_ffn2_w_in, v_ffn2_w_out, v_ffn2_post_g, v_final_g):
    args = dict(locals())
    w = {n: args[n][0] for n in WEIGHTS}
    m = {n: args["m_" + n][0] for n in WEIGHTS}
    v = {n: args["v_" + n][0] for n in WEIGHTS}
    xi, yi, ci = lax.axis_index("x"), lax.axis_index("y"), lax.axis_index("c")
    chip = 2 * xi + yi

    c_arr = jnp.reshape(ci, (1,)).astype(jnp.int32)
    chip_arr = jnp.reshape(chip, (1,)).astype(jnp.int32)
    place_arr = jnp.stack([chip, ci]).astype(jnp.int32)
    shard_of = {n: (jnp.transpose(args[n][0])[None] if n == "w_in" else args[n]) for n in BIG}
    shard_of["w_fu"] = args["w_fu"]
    placed, inflight = {}, {}

    def place(names, after):
        for n in names:
            if n not in placed:
                placed[n] = _place_shard(shard_of[n], chip_arr, F32 if n == "w_fu" else BF16, name="place_" + n, after=after)

    def relayout(names, gathered):
        out = {}
        for n, g in zip(names, gathered):
            if n == "w_fu":
                w_fu_full = _cols_from_shards(g)
                out["w_fu_pad"] = jnp.concatenate([w_fu_full, jnp.zeros((LANE - GATE_RANK, 512), F32)], axis=0).astype(BF16)
            elif n == "w_in":
                wt = g.reshape(IN_END, D)
                out["w_gla_t"] = jnp.concatenate([wt[IN_GLA:IN_PX], jnp.zeros((PG_W - IN_PX, D), BF16)], axis=0)
                out["w_px_t"] = wt[IN_PX:IN_GATES]
                out["w_p_t"] = wt[IN_PX:IN_PX + 512]
                out["w_xq_t"] = wt[IN_PX + 512:IN_GATES]
                out["w_gates_t"] = wt[IN_GATES:IN_END]
            else:
                out[n] = _cols_from_shards(g) if n in COL_SHARDED else g.reshape(4 * g.shape[1], g.shape[2])
        return out

    def gather(op, group, after):
        names = GATHER_GROUPS[group]
        if op == "start":
            place(names, ())
            inflight[group] = _gather_start([placed[n] for n in names], after, name="gather_" + group + "_start")
            behind = (inflight[group][3],)
            if group == "ffn1o":
                place(shard_of, behind)
            return behind
        if op == "pass":
            send, recv, bufs, _ = inflight[group]
            inflight[group] = _gather_pass(bufs, send, recv, after, name="gather_" + group + "_pass")
            return (inflight[group][2][0],)
        send, recv, bufs = inflight.pop(group)
        return relayout(names, _gather_finish(bufs, send, recv, after, name="gather_" + group + "_finish"))

    small = {n: w[n].reshape(1, D) for n in GAINS}
    small["b_f"] = w["b_f"].reshape(1, 512)
    small["pool_scale"] = w["pool_scale"].reshape(1, 512)
    small["w_pool_b"] = w["w_pool"].astype(BF16)

    pending, crossing, travelling = {}, {}, {}

    def emit(name, grad):
        pending[name] = grad
        group = next((g for g, names in REDUCE_GROUPS.items() if name == names[-1]), None)
        if group is None:
            return ()
        gb = {n: pending.pop(n) for n in REDUCE_GROUPS[group]}
        if group == "mix":
            dwt = jnp.concatenate([gb.pop("w_gla")[0:IN_PX], gb.pop("w_p"), gb.pop("w_xq"), gb.pop("w_gates")], axis=0)
            gb["w_in"] = dwt.reshape(4, IN_END // 4, D)
        names = list(gb)
        contrib = [gb[n] if n in COL_SHARDED else gb[n].reshape(4, gb[n].shape[0] // 4, gb[n].shape[1]) for n in names]
        if group == REDUCE_LAST:
            from_sibling = _pair_exchange(contrib, name="grads_" + group + "_pair_exchange")
            return over_chips(group, names, contrib, from_sibling)
        send, recv, contrib, lands, token = _pair_exchange_start(contrib, (), name="grads_" + group + "_pair_start")
        crossing[group] = (names, contrib, lands, send, recv)
        return (token,)

    def over_chips(group, names, contrib, from_sibling):
        pair = [_pair_sum(g, got, c_arr, name="grads_pair_sum_" + n) for n, g, got in zip(names, contrib, from_sibling)]
        send, recv, pair, lands, token = _chip_exchange_start(pair, (), name="grads_" + group + "_chip_start")
        travelling[group] = (names, send, recv, pair, lands)
        return (token,)

    def advance(after):
        behind = ()
        for group in list(crossing):
            names, contrib, lands, send, recv = crossing.pop(group)
            contrib, from_sibling = _pair_exchange_finish(contrib, lands, send, recv, after, name="grads_" + group + "_pair_finish")
            behind = over_chips(group, names, contrib, from_sibling)
        return behind

    loss, grad_x, gs = _local_step(x[0], mem[0], loss_target[0], small, gather, emit, advance)
    loss = lax.psum(loss[0, 0], ("x", "y", "c"))

    halves = {}
    for group, (names, send, recv, pair, lands) in travelling.items():
        pair, from_chips = _chip_exchange_finish(pair, lands, send, recv, (grad_x,), name="grads_" + group + "_chip_finish")
        for n, p, got in zip(names, pair, from_chips):
            halves[n] = _chip_sum(p, got, place_arr, name="grads_chip_sum_" + n)
    send, recv, joining, token = _pair_join_start([halves[n] for n in BIG], name="grads_pair_join_start")
    small_sums = _all_sum_small(gs, name="sum_small_grads", after=(token,))
    reduced = dict(zip(BIG, _pair_join_finish(joining, send, recv, (small_sums[0],), name="grads_pair_join_finish")))

    grads, delta, new_m, new_v = {}, {}, {}, {}
    for n in BIG:
        if n == "w_in":
            transposed = [jnp.transpose(args[k][0]) for k in (n, "m_" + n, "v_" + n)]
            updated = _adamw(transposed[0], reduced[n], transposed[1], transposed[2], name="adamw_" + n)
            grads[n] = jnp.transpose(reduced[n])[None]
            delta[n], new_m[n], new_v[n] = (jnp.transpose(a)[None] for a in updated)
            continue
        grads[n] = reduced[n][None]
        delta[n], new_m[n], new_v[n] = _adamw(args[n], reduced[n], args["m_" + n], args["v_" + n], name="adamw_" + n)
    small_params = {n: (args[n], args["m_" + n], args["v_" + n]) for n in SMALL}
    for n, (g, d, mn, vn) in _adamw_small(small_sums, small_params, chip_arr, name="adamw_small").items():
        grads[n], delta[n], new_m[n], new_v[n] = g, d, mn, vn

    outs = [loss, grad_x[None]]
    for group in (grads, delta, new_m, new_v):
        outs += [group[n] for n in WEIGHTS]
    return tuple(outs)
```

```python
import functools

import jax
import jax.numpy as jnp
from jax import lax
from jax.experimental import pallas as pl
from jax.experimental.pallas import tpu as pltpu

F32 = jnp.float32
BF16 = jnp.bfloat16
MESH = pl.DeviceIdType.MESH
HIGHEST = lax.Precision.HIGHEST

D = 1024
DFF = 2816
CHUNK = 64
HEADS = 4
HDK = 128
HDV = 256
GATE_TEMP = 16.0
POOL_WINDOWS = (2, 4, 8, 16)
POOL_HALO = 16
XA_HEADS = 4
XA_HD = 128
EPS = 1e-6
Q_SCALE = HDK ** -0.5
XA_SCALE = XA_HD ** -0.5
PG_Q, PG_K, PG_V, PG_G, PG_F, PG_W = 0, 512, 1024, 2048, 3072, 3200
GATE_RANK = 16
ADAM_LR, ADAM_B1, ADAM_B2, ADAM_EPS, ADAM_WD, ADAM_STEP = 0.001, 0.9, 0.999, 1e-08, 0.01, 10

VMEM_LIMIT = 48 * 1024 * 1024
LANE = 128
TS_ROW = 512
TS_GLA = 512
TS_POOL = 512
TS_XA = 512


def _params(sem):
    return pltpu.CompilerParams(dimension_semantics=sem, vmem_limit_bytes=VMEM_LIMIT)


def _tile(n, cap, unit=LANE):
    if n <= cap:
        return n
    best = None
    for t in range(unit, cap + 1, unit):
        if n % t == 0:
            best = t
    assert best is not None, (n, cap)
    return best


def _sigmoid(x):
    return 0.5 * jnp.tanh(0.5 * x) + 0.5


def _log_sigmoid(x):
    return jnp.minimum(x, 0.0) - jnp.log(1.0 + jnp.exp(-jnp.abs(x)))


def _rms(x):
    r = lax.rsqrt(jnp.mean(x * x, axis=-1, keepdims=True) + EPS)
    return x * r, r


def _rows(ts, w):
    return pl.BlockSpec((ts, w), lambda i: (i, 0))


def _fixed(shape):
    nd = len(shape)
    return pl.BlockSpec(shape, lambda i: (0,) * nd)


def _mm(a, b, *, ta=False, tb=False, out_dtype=F32, tm=2048, tn=1024, tk=1024, shards=1, after=(), name):
    a_blocked, b_blocked = a.ndim == 3, b.ndim == 3
    assert not (a_blocked and ta) and not (b_blocked and tb)
    if a_blocked:
        m, kdim, tk = a.shape[1], a.shape[0] * a.shape[2], a.shape[2]
    else:
        m, kdim = (a.shape[1], a.shape[0]) if ta else a.shape
    if b_blocked:
        n, tn = b.shape[0] * b.shape[2], b.shape[2]
        assert b.shape[1] == kdim and shards in (1, b.shape[0])
    else:
        n = b.shape[0] if tb else b.shape[1]
        assert (b.shape[1] if tb else b.shape[0]) == kdim, (a.shape, b.shape, ta, tb)
        tn = n // shards if shards > 1 else _tile(n, tn)
    tm = _tile(m, tm)
    tk = tk if a_blocked else _tile(kdim, tk)
    kgroup = 2 if (a_blocked and tb and a.shape[0] % 2 == 0) else 1
    nk = kdim // (tk * kgroup)
    dims = (((0 if ta else 1,), (1 if tb else 0,)), ((), ()))

    def body(a_ref, b_ref, *rest):
        o_ref, *acc = rest[len(after):]
        if kgroup == 1:
            part = lax.dot_general(a_ref[...], b_ref[...], dims, preferred_element_type=F32)
        else:
            part = sum(lax.dot_general(a_ref[g], b_ref[:, g * tk:(g + 1) * tk], dims, preferred_element_type=F32) for g in range(kgroup))
        if nk == 1:
            o_ref[...] = part.astype(o_ref.dtype)
            return
        acc_ref, = acc
        k = pl.program_id(2)

        @pl.when(k == 0)
        def _():
            acc_ref[...] = part

        @pl.when(k > 0)
        def _():
            acc_ref[...] += part

        @pl.when(k == nk - 1)
        def _():
            o_ref[...] = acc_ref[...].astype(o_ref.dtype)

    if a_blocked and kgroup > 1:
        a_spec = pl.BlockSpec((kgroup, tm, tk), lambda i, j, k: (k, i, 0))
    elif a_blocked:
        a_spec = pl.BlockSpec((None, tm, tk), lambda i, j, k: (k, i, 0))
    else:
        a_spec = pl.BlockSpec((tk, tm), lambda i, j, k: (k, i)) if ta else pl.BlockSpec((tm, tk), lambda i, j, k: (i, k))
    if b_blocked:
        b_spec = pl.BlockSpec((None, tk, tn), lambda i, j, k: (j, k, 0))
    else:
        b_spec = pl.BlockSpec((tn, tk * kgroup), lambda i, j, k: (j, k)) if tb else pl.BlockSpec((tk, tn), lambda i, j, k: (k, j))
    if shards > 1:
        out_shape = jax.ShapeDtypeStruct((shards, m, tn), out_dtype)
        o_spec = pl.BlockSpec((None, tm, tn), lambda i, j, k: (j, i, 0))
    else:
        out_shape = jax.ShapeDtypeStruct((m, n), out_dtype)
        o_spec = pl.BlockSpec((tm, tn), lambda i, j, k: (i, j))
    return pl.pallas_call(
        body, grid=(m // tm, n // tn, nk), in_specs=[a_spec, b_spec] + [ANY] * len(after), out_specs=o_spec, out_shape=out_shape,
        scratch_shapes=[pltpu.VMEM((tm, tn), F32)] if nk > 1 else [],
        compiler_params=_params(("parallel", "parallel", "arbitrary")), name=name,
    )(a, b, *after)


def _norm_fwd(x, g, out_dtype, name, after=()):
    s, d = x.shape
    ts = _tile(s, TS_ROW, 8)

    def body(x_ref, g_ref, *rest):
        o_ref = rest[len(after)]
        xh, _ = _rms(x_ref[...])
        o_ref[...] = (xh * g_ref[...]).astype(o_ref.dtype)

    return pl.pallas_call(
        body, grid=(s // ts,), in_specs=[_rows(ts, d), _fixed((1, d))] + [ANY] * len(after), out_specs=_rows(ts, d),
        out_shape=jax.ShapeDtypeStruct((s, d), out_dtype), compiler_params=_params(("parallel",)), name=name,
    )(x, g, *after)


def _resid_norm_fwd(x, f, g_post, alpha, g_next, name, after=()):
    s, d = x.shape
    ts = _tile(s, TS_ROW, 8)
    with_h = g_next is not None

    def body(x_ref, f_ref, gp_ref, *rest):
        rest = rest[:1] + rest[1 + len(after):] if with_h else rest[len(after):]
        fh, _ = _rms(f_ref[...])
        xn = x_ref[...] + alpha * (fh * gp_ref[...])
        if with_h:
            gn_ref, xo_ref, h_ref = rest
            xh, _ = _rms(xn)
            h_ref[...] = (xh * gn_ref[...]).astype(h_ref.dtype)
        else:
            xo_ref, = rest
        xo_ref[...] = xn

    ins = [x, f, g_post] + ([g_next] if with_h else []) + list(after)
    in_specs = [_rows(ts, d), _rows(ts, d), _fixed((1, d))] + ([_fixed((1, d))] if with_h else []) + [ANY] * len(after)
    out_shape = [jax.ShapeDtypeStruct((s, d), F32)] + ([jax.ShapeDtypeStruct((s, d), BF16)] if with_h else [])
    out_specs = [_rows(ts, d)] + ([_rows(ts, d)] if with_h else [])
    out = pl.pallas_call(
        body, grid=(s // ts,), in_specs=in_specs, out_specs=out_specs, out_shape=out_shape,
        compiler_params=_params(("parallel",)), name=name,
    )(*ins)
    return (out[0], out[1]) if with_h else (out[0], None)


def _mm_resid_norm(a, w, x, g_post, alpha, g_next, name, after=(), tm=512):
    s, kdim = a.shape
    d = w.shape[1]
    tm = _tile(s, tm)
    with_h = g_next is not None
    na = len(after)

    def body(a_ref, w_ref, x_ref, gp_ref, *rest):
        rest = rest[int(with_h) + na:] if not with_h else rest[:1] + rest[1 + na:]
        for rows in _sub_blocks(tm):
            f = jnp.dot(a_ref[rows, :], w_ref[...], preferred_element_type=F32)
            fh, _ = _rms(f)
            xn = x_ref[rows, :] + alpha * (fh * gp_ref[...])
            if with_h:
                gn_ref, f_ref, xo_ref, h_ref = rest
                xh, _ = _rms(xn)
                h_ref[rows, :] = (xh * gn_ref[...]).astype(h_ref.dtype)
            else:
                f_ref, xo_ref = rest
            f_ref[rows, :] = f
            xo_ref[rows, :] = xn

    ins = [a, w, x, g_post] + ([g_next] if with_h else []) + list(after)
    in_specs = [_rows(tm, kdim), _fixed((kdim, d)), _rows(tm, d), _fixed((1, d))] + ([_fixed((1, d))] if with_h else []) + [ANY] * na
    out_shape = [jax.ShapeDtypeStruct((s, d), F32)] * 2 + ([jax.ShapeDtypeStruct((s, d), BF16)] if with_h else [])
    out = pl.pallas_call(
        body, grid=(s // tm,), in_specs=in_specs, out_specs=[_rows(tm, d)] * len(out_shape), out_shape=out_shape,
        compiler_params=_params(("parallel",)), name=name,
    )(*ins)
    return (out[0], out[1], out[2]) if with_h else (out[0], out[1], None)


def _mm_rms_bwd(pairs, x, g, dres, name, after=(), post=None, tm=512):
    s, d = x.shape
    tm = _tile(s, tm)
    n, na = len(pairs), len(after)

    def body(*refs):
        a_refs, w_refs = refs[0:2 * n:2], refs[1:2 * n:2]
        x_ref, g_ref, dres_ref = refs[2 * n:2 * n + 3]
        if post is not None:
            f_ref, gp_ref = refs[2 * n + 3:2 * n + 5]
            dx_ref, dg_ref, df_ref, dgp_ref = refs[2 * n + 5 + na:]
        else:
            dx_ref, dg_ref = refs[2 * n + 3 + na:]
        @pl.when(pl.program_id(0) == 0)
        def _():
            dg_ref[...] = jnp.zeros_like(dg_ref)
            if post is not None:
                dgp_ref[...] = jnp.zeros_like(dgp_ref)

        for rows in _sub_blocks(tm):
            dy = None
            for a_ref, w_ref in zip(a_refs, w_refs):
                if len(a_ref.shape) == 3:
                    tkb = a_ref.shape[2]
                    parts = [lax.dot_general(a_ref[q, rows, :], w_ref[:, q * tkb:(q + 1) * tkb], (((1,), (1,)), ((), ())),
                                             preferred_element_type=F32) for q in range(a_ref.shape[0])]
                else:
                    parts = [jnp.dot(a_ref[rows, :], w_ref[...], preferred_element_type=F32)]
                for part in parts:
                    dy = part if dy is None else dy + part
            xh, r = _rms(x_ref[rows, :])
            dg_ref[...] += jnp.sum(dy * xh, axis=0, keepdims=True)
            dyg = dy * g_ref[...]
            dx = r * (dyg - xh * jnp.mean(dyg * xh, axis=-1, keepdims=True)) + dres_ref[rows, :]
            dx_ref[rows, :] = dx
            if post is not None:
                fh, rf = _rms(f_ref[rows, :])
                dz = dx * post[2]
                dgp_ref[...] += jnp.sum(dz * fh, axis=0, keepdims=True)
                dzg = dz * gp_ref[...]
                df_ref[rows, :] = (rf * (dzg - fh * jnp.mean(dzg * fh, axis=-1, keepdims=True))).astype(df_ref.dtype)

    ins, in_specs = [], []
    for a_arr, w_arr in pairs:
        ins += [a_arr, w_arr]
        if a_arr.ndim == 3:
            in_specs.append(pl.BlockSpec((a_arr.shape[0], tm, a_arr.shape[2]), lambda i: (0, i, 0)))
        else:
            in_specs.append(_rows(tm, a_arr.shape[1]))
        in_specs.append(pl.BlockSpec(w_arr.shape, lambda i: (0, 0), pipeline_mode=pl.Buffered(1)))
    with_post = post is not None
    return pl.pallas_call(
        body, grid=(s // tm,),
        in_specs=in_specs + [_rows(tm, d), _fixed((1, d)), _rows(tm, d)] + ([_rows(tm, d), _fixed((1, d))] if with_post else [])
        + [ANY] * na,
        out_specs=[_rows(tm, d), _fixed((1, d))] + ([_rows(tm, d), _fixed((1, d))] if with_post else []),
        out_shape=[jax.ShapeDtypeStruct((s, d), F32), jax.ShapeDtypeStruct((1, d), F32)]
        + ([jax.ShapeDtypeStruct((s, d), BF16), jax.ShapeDtypeStruct((1, d), F32)] if with_post else []),
        compiler_params=_params(("arbitrary",)), name=name,
    )(*ins, x, g, dres, *(post[:2] if with_post else ()), *after)


def _ffn_out_loss(u, w_out, x, g_post, alpha, g_final, target, name, tm=512):
    s, kdim = u.shape
    d = w_out.shape[1]
    tm = _tile(s, tm)

    def body(u_ref, w_ref, x_ref, gp_ref, gf_ref, t_ref, df_ref, dx_ref, dgp_ref, dgf_ref, loss_ref):
        @pl.when(pl.program_id(0) == 0)
        def _():
            dgp_ref[...] = jnp.zeros_like(dgp_ref)
            dgf_ref[...] = jnp.zeros_like(dgf_ref)
            loss_ref[...] = jnp.zeros_like(loss_ref)

        for rows in _sub_blocks(tm):
            f = jnp.dot(u_ref[rows, :], w_ref[...], preferred_element_type=F32)
            fh, rf = _rms(f)
            xn = x_ref[rows, :] + alpha * (fh * gp_ref[...])
            xh, rx = _rms(xn)
            gf = gf_ref[...]
            diff = xh * gf - t_ref[rows, :]
            sq = jnp.sum(diff * diff, axis=1, keepdims=True)
            loss_ref[...] += (0.5 / d) * jnp.sum(sq, axis=0, keepdims=True)
            dy = diff * (1.0 / d)
            dgf_ref[...] += jnp.sum(dy * xh, axis=0, keepdims=True)
            dyg = dy * gf
            dxn = rx * (dyg - xh * jnp.mean(dyg * xh, axis=-1, keepdims=True))
            dx_ref[rows, :] = dxn
            dz = dxn * alpha
            dgp_ref[...] += jnp.sum(dz * fh, axis=0, keepdims=True)
            dzg = dz * gp_ref[...]
            df_ref[rows, :] = (rf * (dzg - fh * jnp.mean(dzg * fh, axis=-1, keepdims=True))).astype(df_ref.dtype)

    return pl.pallas_call(
        body, grid=(s // tm,),
        in_specs=[_rows(tm, kdim), _resident(w_out.shape), _rows(tm, d), _fixed((1, d)), _fixed((1, d)), _rows(tm, d)],
        out_specs=[_rows(tm, d), _rows(tm, d), _fixed((1, d)), _fixed((1, d)), _fixed((8, LANE))],
        out_shape=[jax.ShapeDtypeStruct((s, d), BF16), jax.ShapeDtypeStruct((s, d), F32), jax.ShapeDtypeStruct((1, d), F32),
                   jax.ShapeDtypeStruct((1, d), F32), jax.ShapeDtypeStruct((8, LANE), F32)],
        compiler_params=_params(("arbitrary",)), name=name,
    )(u, w_out, x, g_post, g_final, target)


def _rms_bwd(x, g, dys, dres, alpha, out_dtype, name, after=()):
    s, d = x.shape
    ts = _tile(s, TS_ROW, 8)
    ndy = len(dys)
    with_res = dres is not None

    def body(x_ref, g_ref, *rest):
        dy_refs = rest[:ndy]
        rest = rest[ndy:]
        if with_res:
            dres_ref = rest[0]
        dx_ref, dg_ref = rest[int(with_res) + len(after):]
        xh, r = _rms(x_ref[...])
        dy = dy_refs[0][...].astype(F32)
        for ref in dy_refs[1:]:
            dy = dy + ref[...].astype(F32)
        dy = dy * alpha

        @pl.when(pl.program_id(0) == 0)
        def _():
            dg_ref[...] = jnp.zeros_like(dg_ref)

        dg_ref[...] += jnp.sum(dy * xh, axis=0, keepdims=True)
        dyg = dy * g_ref[...]
        dx = r * (dyg - xh * jnp.mean(dyg * xh, axis=-1, keepdims=True))
        if with_res:
            dx = dx + dres_ref[...]
        dx_ref[...] = dx.astype(dx_ref.dtype)

    ins = [x, g] + list(dys) + ([dres] if with_res else []) + list(after)
    in_specs = [_rows(ts, d), _fixed((1, d))] + [_rows(ts, d)] * (ndy + int(with_res)) + [ANY] * len(after)
    return pl.pallas_call(
        body, grid=(s // ts,), in_specs=in_specs, out_specs=[_rows(ts, d), _fixed((1, d))],
        out_shape=[jax.ShapeDtypeStruct((s, d), out_dtype), jax.ShapeDtypeStruct((1, d), F32)],
        compiler_params=_params(("arbitrary",)), name=name,
    )(*ins)


def _loss_bwd(x, g, target, name):
    s, d = x.shape
    ts = _tile(s, TS_ROW, 8)

    def body(x_ref, g_ref, t_ref, dx_ref, dg_ref, loss_ref):
        xh, r = _rms(x_ref[...])
        gv = g_ref[...]
        diff = xh * gv - t_ref[...]

        @pl.when(pl.program_id(0) == 0)
        def _():
            dg_ref[...] = jnp.zeros_like(dg_ref)
            loss_ref[...] = jnp.zeros_like(loss_ref)

        sq = jnp.sum(diff * diff, axis=1, keepdims=True)
        loss_ref[...] += (0.5 / d) * jnp.sum(sq, axis=0, keepdims=True)
        dy = diff * (1.0 / d)
        dg_ref[...] += jnp.sum(dy * xh, axis=0, keepdims=True)
        dyg = dy * gv
        dx_ref[...] = r * (dyg - xh * jnp.mean(dyg * xh, axis=-1, keepdims=True))

    return pl.pallas_call(
        body, grid=(s // ts,), in_specs=[_rows(ts, d), _fixed((1, d)), _rows(ts, d)],
        out_specs=[_rows(ts, d), _fixed((1, d)), _fixed((8, LANE))],
        out_shape=[jax.ShapeDtypeStruct((s, d), F32), jax.ShapeDtypeStruct((1, d), F32), jax.ShapeDtypeStruct((8, LANE), F32)],
        compiler_params=_params(("arbitrary",)), name=name,
    )(x, g, target)


HALF_FF = DFF // 2


SUB_ROWS = 256


def _sub_blocks(tm):
    sub = SUB_ROWS if tm % SUB_ROWS == 0 else tm
    return [slice(r0, r0 + sub) for r0 in range(0, tm, sub)]


def _ffn_in_swiglu(x_norm, w_in, name, after=(), tm=1024):
    s, d = x_norm.shape
    tm = _tile(s, tm)

    def body(x_ref, wa_ref, wb_ref, *rest):
        ab_ref, u_ref = rest[len(after):]
        for rows in _sub_blocks(tm):
            xv = x_ref[rows, :]
            a = jnp.dot(xv, wa_ref[...], preferred_element_type=F32)
            b = jnp.dot(xv, wb_ref[...], preferred_element_type=F32)
            ab_ref[0, rows, :] = a.astype(ab_ref.dtype)
            ab_ref[1, rows, :] = b.astype(ab_ref.dtype)
            u_ref[rows, :] = (a * _sigmoid(a) * b).astype(u_ref.dtype)

    ab, u = pl.pallas_call(
        body, grid=(s // tm, 2),
        in_specs=[pl.BlockSpec((tm, d), lambda i, j: (i, 0)), pl.BlockSpec((d, HALF_FF), lambda i, j: (0, j)),
                  pl.BlockSpec((d, HALF_FF), lambda i, j: (0, 2 + j))] + [ANY] * len(after),
        out_specs=[pl.BlockSpec((2, None, tm, HALF_FF), lambda i, j: (0, j, i, 0)), pl.BlockSpec((tm, HALF_FF), lambda i, j: (i, j))],
        out_shape=[jax.ShapeDtypeStruct((2, 2, s, HALF_FF), BF16), jax.ShapeDtypeStruct((s, DFF), BF16)],
        compiler_params=_params(("parallel", "parallel")), name=name,
    )(x_norm, w_in, w_in, *after)
    return ab.reshape(4, s, HALF_FF), u


def _ffn_out_dx_swiglu(dz, w_out, ab, after, name, tm=1024):
    s, d = dz.shape
    tm = _tile(s, tm)

    def body(dz_ref, w_ref, ab_ref, *rest):
        dab_ref = rest[len(after)]
        for rows in _sub_blocks(tm):
            du = lax.dot_general(dz_ref[rows, :], w_ref[...], (((1,), (1,)), ((), ())), preferred_element_type=F32)
            a = ab_ref[0, rows, :].astype(F32)
            b = ab_ref[1, rows, :].astype(F32)
            sig = _sigmoid(a)
            dab_ref[0, rows, :] = (du * b * (sig * (1.0 + a * (1.0 - sig)))).astype(dab_ref.dtype)
            dab_ref[1, rows, :] = (du * a * sig).astype(dab_ref.dtype)

    halves = pl.BlockSpec((2, None, tm, HALF_FF), lambda i, j: (0, j, i, 0))
    dab = pl.pallas_call(
        body, grid=(s // tm, 2),
        in_specs=[pl.BlockSpec((tm, d), lambda i, j: (i, 0)), pl.BlockSpec((HALF_FF, d), lambda i, j: (j, 0)), halves] + [ANY] * len(after),
        out_specs=halves, out_shape=jax.ShapeDtypeStruct((2, 2, s, HALF_FF), BF16),
        compiler_params=_params(("parallel", "parallel")), name=name,
    )(dz, w_out, ab.reshape(2, 2, s, HALF_FF), *after)
    return dab.reshape(4, s, HALF_FF)


def _tri(strict):
    r = lax.broadcasted_iota(jnp.int32, (CHUNK, CHUNK), 0)
    c = lax.broadcasted_iota(jnp.int32, (CHUNK, CHUNK), 1)
    return (r > c).astype(F32) if strict else (r >= c).astype(F32)


def _gla_fwd(pg, wfu, b_f, gnorm, name):
    s = pg.shape[0]
    ts = _tile(s, TS_GLA, CHUNK)
    cpb = ts // CHUNK
    nc = s // CHUNK

    def body(pg_ref, wfu_ref, bf_ref, gn_ref, ya_ref, sp_ref, so_ref, o_ref, st_ref, la_ref, dec_ref, u_ref):
        @pl.when(pl.program_id(0) == 0)
        def _():
            st_ref[...] = jnp.zeros_like(st_ref)

        f = jnp.dot(pg_ref[:, PG_F:PG_W], wfu_ref[...], preferred_element_type=F32) + bf_ref[...]
        la_ref[...] = _log_sigmoid(f) * (1.0 / GATE_TEMP)
        tri = _tri(False)
        chunks = [slice(ci * CHUNK, (ci + 1) * CHUNK) for ci in range(cpb)]
        for ci, rows in enumerate(chunks):
            la = la_ref[rows, :]
            b = jnp.dot(tri, la, precision=HIGHEST, preferred_element_type=F32)
            bend = jnp.sum(la, axis=0, keepdims=True)
            e = jnp.exp(bend - b)
            dec_ref[ci:ci + 1, :] = jnp.exp(bend)
            for hd in range(HEADS):
                k = pg_ref[rows, PG_K + hd * HDK:PG_K + (hd + 1) * HDK]
                v = pg_ref[rows, PG_V + hd * HDV:PG_V + (hd + 1) * HDV]
                kt = (k.astype(F32) * e[:, hd * HDK:(hd + 1) * HDK]).astype(BF16)
                u_ref[ci, hd] = lax.dot_general(v, kt, (((0,), (0,)), ((), ())), preferred_element_type=F32)
        for ci in range(cpb):
            for hd in range(HEADS):
                prev = st_ref[hd]
                sp_ref[ci, hd] = prev
                st = prev * dec_ref[ci:ci + 1, hd * HDK:(hd + 1) * HDK] + u_ref[ci, hd]
                st_ref[hd] = st
                so_ref[ci, hd] = st.astype(so_ref.dtype)
        for ci, rows in enumerate(chunks):
            for hd in range(HEADS):
                vc = slice(hd * HDV, (hd + 1) * HDV)
                q = pg_ref[rows, PG_Q + hd * HDK:PG_Q + (hd + 1) * HDK]
                go = pg_ref[rows, PG_G + hd * HDV:PG_G + (hd + 1) * HDV].astype(F32)
                qs = (q.astype(F32) * Q_SCALE).astype(BF16)
                o = lax.dot_general(qs, so_ref[ci, hd], (((1,), (1,)), ((), ())), preferred_element_type=F32)
                o_ref[rows, vc] = o
                oh, _ = _rms(o)
                ya_ref[rows, vc] = (oh * gn_ref[:, vc] * (go * _sigmoid(go))).astype(ya_ref.dtype)

    return pl.pallas_call(
        body, grid=(s // ts,),
        in_specs=[_rows(ts, PG_W), _fixed((LANE, HEADS * HDK)), _fixed((1, HEADS * HDK)), _fixed((1, HEADS * HDV))],
        out_specs=[_rows(ts, HEADS * HDV), pl.BlockSpec((cpb, HEADS, HDV, HDK), lambda i: (i, 0, 0, 0)),
                   pl.BlockSpec((cpb, HEADS, HDV, HDK), lambda i: (i, 0, 0, 0)), _rows(ts, HEADS * HDV)],
        out_shape=[jax.ShapeDtypeStruct((s, HEADS * HDV), BF16), jax.ShapeDtypeStruct((nc, HEADS, HDV, HDK), F32),
                   jax.ShapeDtypeStruct((nc, HEADS, HDV, HDK), BF16), jax.ShapeDtypeStruct((s, HEADS * HDV), F32)],
        scratch_shapes=[pltpu.VMEM((HEADS, HDV, HDK), F32), pltpu.VMEM((ts, HEADS * HDK), F32),
                        pltpu.VMEM((max(cpb, 8), HEADS * HDK), F32), pltpu.VMEM((cpb, HEADS, HDV, HDK), F32)],
        compiler_params=_params(("arbitrary",)), name=name,
    )(pg, wfu, b_f, gnorm)


def _gla_bwd(pg, sp, so, o, dya, wfu, b_f, gnorm, name):
    s = pg.shape[0]
    ts = _tile(s, TS_GLA, CHUNK)
    cpb = ts // CHUNK
    nblk = s // ts

    def body(pg_ref, sp_ref, so_ref, o_ref, dya_ref, wfu_ref, bf_ref, gn_ref, dpg_ref, dwfu_ref, dbf_ref, dgn_ref,
             dst_ref, la_ref, sg_ref, df_ref, e_ref, ktf_ref, dec_ref, g_ref):
        @pl.when(pl.program_id(0) == 0)
        def _():
            dst_ref[...] = jnp.zeros_like(dst_ref)
            dwfu_ref[...] = jnp.zeros_like(dwfu_ref)
            dbf_ref[...] = jnp.zeros_like(dbf_ref)
            dgn_ref[...] = jnp.zeros_like(dgn_ref)

        flow = pg_ref[:, PG_F:PG_W]
        f = jnp.dot(flow, wfu_ref[...], preferred_element_type=F32) + bf_ref[...]
        la_ref[...] = _log_sigmoid(f) * (1.0 / GATE_TEMP)
        sg_ref[...] = _sigmoid(-f) * (1.0 / GATE_TEMP)
        tri = _tri(False)
        tri_strict = _tri(True)
        chunks = [slice(ci * CHUNK, (ci + 1) * CHUNK) for ci in range(cpb)]
        for ci, rows in enumerate(chunks):
            la = la_ref[rows, :]
            b = jnp.dot(tri, la, precision=HIGHEST, preferred_element_type=F32)
            bend = jnp.sum(la, axis=0, keepdims=True)
            e = jnp.exp(bend - b)
            e_ref[rows, :] = e
            dec = jnp.exp(bend)
            dec_ref[ci:ci + 1, :] = dec
            for hd in range(HEADS):
                kc = slice(hd * HDK, (hd + 1) * HDK)
                vc = slice(hd * HDV, (hd + 1) * HDV)
                q = pg_ref[rows, PG_Q + hd * HDK:PG_Q + (hd + 1) * HDK]
                k = pg_ref[rows, PG_K + hd * HDK:PG_K + (hd + 1) * HDK]
                go = pg_ref[rows, PG_G + hd * HDV:PG_G + (hd + 1) * HDV].astype(F32)
                ktf_ref[rows, kc] = k.astype(F32) * e[:, kc]
                st_b = so_ref[ci, hd]
                qs = (q.astype(F32) * Q_SCALE).astype(BF16)
                oh, r = _rms(o_ref[rows, vc])
                gh = gn_ref[:, vc]
                sig = _sigmoid(go)
                dy = dya_ref[rows, vc].astype(F32)
                don = dy * (go * sig)
                dgn_ref[:, vc] += jnp.sum(don * oh, axis=0, keepdims=True)
                dong = don * gh
                do = (r * (dong - oh * jnp.mean(dong * oh, axis=-1, keepdims=True))).astype(BF16)
                g_ref[ci, hd] = lax.dot_general(do, qs, (((0,), (0,)), ((), ())), preferred_element_type=F32)
                dq = jnp.dot(do, st_b, preferred_element_type=F32) * Q_SCALE
                dpg_ref[rows, PG_Q + hd * HDK:PG_Q + (hd + 1) * HDK] = dq.astype(dpg_ref.dtype)
                dgo = dy * (oh * gh) * (sig * (1.0 + go * (1.0 - sig)))
                dpg_ref[rows, PG_G + hd * HDV:PG_G + (hd + 1) * HDV] = dgo.astype(dpg_ref.dtype)
        for ci in reversed(range(cpb)):
            for hd in range(HEADS):
                dst = dst_ref[hd] + g_ref[ci, hd]
                g_ref[ci, hd] = dst
                dst_ref[hd] = dst * dec_ref[ci:ci + 1, hd * HDK:(hd + 1) * HDK]
        for ci, rows in enumerate(chunks):
            for hd in range(HEADS):
                kc = slice(hd * HDK, (hd + 1) * HDK)
                v = pg_ref[rows, PG_V + hd * HDV:PG_V + (hd + 1) * HDV]
                ktf = ktf_ref[rows, kc]
                dst = g_ref[ci, hd]
                dst_b = dst.astype(BF16)
                dkt = jnp.dot(v, dst_b, preferred_element_type=F32)
                dv = lax.dot_general(ktf.astype(BF16), dst_b, (((1,), (1,)), ((), ())), preferred_element_type=F32)
                dd = jnp.sum(dst * sp_ref[ci, hd], axis=0, keepdims=True)
                dla = jnp.dot(tri_strict, dkt * ktf, precision=HIGHEST, preferred_element_type=F32) + dd * dec_ref[ci:ci + 1, kc]
                df_ref[rows, kc] = dla * sg_ref[rows, kc]
                dpg_ref[rows, PG_K + hd * HDK:PG_K + (hd + 1) * HDK] = (dkt * e_ref[rows, kc]).astype(dpg_ref.dtype)
                dpg_ref[rows, PG_V + hd * HDV:PG_V + (hd + 1) * HDV] = dv.astype(dpg_ref.dtype)
        df = df_ref[...]
        df_b = df.astype(BF16)
        dpg_ref[:, PG_F:PG_W] = lax.dot_general(df_b, wfu_ref[...], (((1,), (1,)), ((), ())), preferred_element_type=F32).astype(dpg_ref.dtype)
        dwfu_ref[...] += lax.dot_general(flow, df_b, (((0,), (0,)), ((), ())), preferred_element_type=F32)
        dbf_ref[...] += jnp.sum(df, axis=0, keepdims=True)

    rev = lambda i: (nblk - 1 - i, 0)
    return pl.pallas_call(
        body, grid=(nblk,),
        in_specs=[pl.BlockSpec((ts, PG_W), rev), pl.BlockSpec((cpb, HEADS, HDV, HDK), lambda i: (nblk - 1 - i, 0, 0, 0)),
                  pl.BlockSpec((cpb, HEADS, HDV, HDK), lambda i: (nblk - 1 - i, 0, 0, 0)), pl.BlockSpec((ts, HEADS * HDV), rev),
                  pl.BlockSpec((ts, HEADS * HDV), rev), _fixed((LANE, HEADS * HDK)), _fixed((1, HEADS * HDK)), _fixed((1, HEADS * HDV))],
        out_specs=[pl.BlockSpec((ts, PG_W), rev), _fixed((LANE, HEADS * HDK)), _fixed((1, HEADS * HDK)), _fixed((1, HEADS * HDV))],
        out_shape=[jax.ShapeDtypeStruct((s, PG_W), BF16), jax.ShapeDtypeStruct((LANE, HEADS * HDK), F32),
                   jax.ShapeDtypeStruct((1, HEADS * HDK), F32), jax.ShapeDtypeStruct((1, HEADS * HDV), F32)],
        scratch_shapes=[pltpu.VMEM((HEADS, HDV, HDK), F32)] + [pltpu.VMEM((ts, HEADS * HDK), F32)] * 5
        + [pltpu.VMEM((max(cpb, 8), HEADS * HDK), F32), pltpu.VMEM((cpb, HEADS, HDV, HDK), F32)],
        compiler_params=_params(("arbitrary",)), name=name,
    )(pg, sp, so, o, dya, wfu, b_f, gnorm)


def _window_sums(ext, sign):
    n = ext.shape[0]
    sums = {1: ext}
    w = 1
    while w < POOL_WINDOWS[-1]:
        sums[2 * w] = sums[w] + pltpu.roll(sums[w], w if sign > 0 else n - w, 0)
        w *= 2
    return [sums[POOL_WINDOWS[g]][:, g * LANE:(g + 1) * LANE] for g in range(len(POOL_WINDOWS))]


def _pool_counts(row0, n):
    pos = (row0 + lax.broadcasted_iota(jnp.int32, (n, 1), 0) + 1).astype(F32)
    return [1.0 / jnp.minimum(pos, float(w)) for w in POOL_WINDOWS]


def _pool_fwd(ppx, w_pool, pool_scale, name):
    s = ppx.shape[0]
    ts = _tile(s, TS_POOL, POOL_HALO)
    hb = ts // POOL_HALO
    pw = len(POOL_WINDOWS) * LANE

    def body(p_ref, halo_ref, w_ref, sc_ref, y_ref, ext_ref):
        i = pl.program_id(0)
        p = p_ref[...].astype(F32)
        ext_ref[0:POOL_HALO, :] = jnp.where(i > 0, halo_ref[...].astype(F32), 0.0)
        ext_ref[POOL_HALO:, :] = p
        sums = _window_sums(ext_ref[...], +1)
        cnt = _pool_counts(i * ts, ts)
        for g in range(len(POOL_WINDOWS)):
            cols = slice(g * LANE, (g + 1) * LANE)
            mixed = sums[g][POOL_HALO:, :] * cnt[g] - p[:, cols]
            y = jnp.dot(mixed.astype(BF16), w_ref[g], preferred_element_type=F32)
            y_ref[:, cols] = (y * sc_ref[:, cols]).astype(y_ref.dtype)

    return pl.pallas_call(
        body, grid=(s // ts,),
        in_specs=[pl.BlockSpec((ts, pw), lambda i: (i, 0)), pl.BlockSpec((POOL_HALO, pw), lambda i: (jnp.maximum(i * hb - 1, 0), 0)),
                  _fixed((len(POOL_WINDOWS), LANE, LANE)), _fixed((1, pw))],
        out_specs=_rows(ts, pw), out_shape=jax.ShapeDtypeStruct((s, pw), BF16),
        scratch_shapes=[pltpu.VMEM((ts + POOL_HALO, pw), F32)],
        compiler_params=_params(("parallel",)), name=name,
    )(ppx, ppx, w_pool, pool_scale)


def _pool_bwd(dyb, ppx, w_pool, pool_scale, name):
    s = ppx.shape[0]
    ts = _tile(s, TS_POOL, POOL_HALO)
    hb = ts // POOL_HALO
    nblk = s // ts
    last_halo = s // POOL_HALO - 1
    ng = len(POOL_WINDOWS)
    pw = ng * LANE

    def body(p_ref, halo_ref, dy_ref, dyn_ref, w_ref, sc_ref, dp_ref, dw_ref, dsc_ref, ext_ref, dext_ref, dm_ref):
        i = pl.program_id(0)

        @pl.when(i == 0)
        def _():
            dw_ref[...] = jnp.zeros_like(dw_ref)
            dsc_ref[...] = jnp.zeros_like(dsc_ref)

        p = p_ref[...].astype(F32)
        ext_ref[0:POOL_HALO, :] = jnp.where(i > 0, halo_ref[...].astype(F32), 0.0)
        ext_ref[POOL_HALO:, :] = p
        sums = _window_sums(ext_ref[...], +1)
        cnt = _pool_counts(i * ts, ts + POOL_HALO)
        sc = sc_ref[...]
        dy = dy_ref[...].astype(F32)
        dyn = jnp.where(i < nblk - 1, dyn_ref[...].astype(F32), 0.0)
        for g in range(ng):
            cols = slice(g * LANE, (g + 1) * LANE)
            wg = w_ref[g]
            mixed = (sums[g][POOL_HALO:, :] * cnt[g][0:ts] - p[:, cols]).astype(BF16)
            ypre = jnp.dot(mixed, wg, preferred_element_type=F32)
            dsc_ref[:, cols] += jnp.sum(dy[:, cols] * ypre, axis=0, keepdims=True)
            dyp = (dy[:, cols] * sc[:, cols]).astype(BF16)
            dypn = (dyn[:, cols] * sc[:, cols]).astype(BF16)
            dw_ref[g] += lax.dot_general(mixed, dyp, (((0,), (0,)), ((), ())), preferred_element_type=F32)
            dm = lax.dot_general(dyp, wg, (((1,), (1,)), ((), ())), preferred_element_type=F32)
            dmn = lax.dot_general(dypn, wg, (((1,), (1,)), ((), ())), preferred_element_type=F32)
            dext_ref[0:ts, cols] = dm * cnt[g][0:ts]
            dext_ref[ts:, cols] = dmn * cnt[g][ts:]
            dm_ref[:, cols] = dm
        lead = _window_sums(dext_ref[...], -1)
        for g in range(ng):
            cols = slice(g * LANE, (g + 1) * LANE)
            dp_ref[:, cols] = (lead[g][0:ts, :] - dm_ref[:, cols]).astype(dp_ref.dtype)

    return pl.pallas_call(
        body, grid=(nblk,),
        in_specs=[pl.BlockSpec((ts, pw), lambda i: (i, 0)), pl.BlockSpec((POOL_HALO, pw), lambda i: (jnp.maximum(i * hb - 1, 0), 0)),
                  pl.BlockSpec((ts, pw), lambda i: (i, 0)), pl.BlockSpec((POOL_HALO, pw), lambda i: (jnp.minimum((i + 1) * hb, last_halo), 0)),
                  _fixed((ng, LANE, LANE)), _fixed((1, pw))],
        out_specs=[_rows(ts, pw), _fixed((ng, LANE, LANE)), _fixed((1, pw))],
        out_shape=[jax.ShapeDtypeStruct((s, pw), BF16), jax.ShapeDtypeStruct((ng, LANE, LANE), F32), jax.ShapeDtypeStruct((1, pw), F32)],
        scratch_shapes=[pltpu.VMEM((ts + POOL_HALO, pw), F32), pltpu.VMEM((ts + POOL_HALO, pw), F32), pltpu.VMEM((ts, pw), F32)],
        compiler_params=_params(("arbitrary",)), name=name,
    )(ppx, ppx, dyb, dyb, w_pool, pool_scale)


def _xattn_fwd(ppx, kv, name):
    s = ppx.shape[0]
    m = kv.shape[0]
    ts = _tile(s, TS_XA, 8)
    xw = XA_HEADS * XA_HD

    def body(q_ref, kv_ref, o_ref):
        for hd in range(XA_HEADS):
            cols = slice(hd * XA_HD, (hd + 1) * XA_HD)
            k = kv_ref[:, hd * XA_HD:(hd + 1) * XA_HD]
            v = kv_ref[:, xw + hd * XA_HD:xw + (hd + 1) * XA_HD]
            sc = lax.dot_general(q_ref[:, cols], k, (((1,), (1,)), ((), ())), preferred_element_type=F32) * XA_SCALE
            ex = jnp.exp(sc - jnp.max(sc, axis=-1, keepdims=True))
            pr = ex * (1.0 / jnp.sum(ex, axis=-1, keepdims=True))
            o_ref[:, cols] = jnp.dot(pr.astype(BF16), v, preferred_element_type=F32).astype(o_ref.dtype)

    return pl.pallas_call(
        body, grid=(s // ts,), in_specs=[pl.BlockSpec((ts, xw), lambda i: (i, 1)), _fixed((m, 2 * xw))],
        out_specs=_rows(ts, xw), out_shape=jax.ShapeDtypeStruct((s, xw), BF16),
        compiler_params=_params(("parallel",)), name=name,
    )(ppx, kv)


def _xattn_bwd(dxc, ppx, kv, name):
    s = ppx.shape[0]
    m = kv.shape[0]
    ts = _tile(s, TS_XA, 8)
    xw = XA_HEADS * XA_HD

    def body(do_ref, q_ref, kv_ref, dq_ref, dkv_ref):
        @pl.when(pl.program_id(0) == 0)
        def _():
            dkv_ref[...] = jnp.zeros_like(dkv_ref)

        for hd in range(XA_HEADS):
            cols = slice(hd * XA_HD, (hd + 1) * XA_HD)
            vcols = slice(xw + hd * XA_HD, xw + (hd + 1) * XA_HD)
            q = q_ref[:, cols]
            k = kv_ref[:, cols]
            v = kv_ref[:, vcols]
            do = do_ref[:, cols]
            sc = lax.dot_general(q, k, (((1,), (1,)), ((), ())), preferred_element_type=F32) * XA_SCALE
            ex = jnp.exp(sc - jnp.max(sc, axis=-1, keepdims=True))
            pr = ex * (1.0 / jnp.sum(ex, axis=-1, keepdims=True))
            dpr = lax.dot_general(do, v, (((1,), (1,)), ((), ())), preferred_element_type=F32)
            dsc = (pr * (dpr - jnp.sum(dpr * pr, axis=-1, keepdims=True)) * XA_SCALE).astype(BF16)
            dq_ref[:, cols] = jnp.dot(dsc, k, preferred_element_type=F32).astype(dq_ref.dtype)
            dkv_ref[:, cols] += lax.dot_general(dsc, q, (((0,), (0,)), ((), ())), preferred_element_type=F32)
            dkv_ref[:, vcols] += lax.dot_general(pr.astype(BF16), do, (((0,), (0,)), ((), ())), preferred_element_type=F32)

    return pl.pallas_call(
        body, grid=(s // ts,), in_specs=[_rows(ts, xw), pl.BlockSpec((ts, xw), lambda i: (i, 1)), _fixed((m, 2 * xw))],
        out_specs=[_rows(ts, xw), _fixed((m, 2 * xw))],
        out_shape=[jax.ShapeDtypeStruct((s, xw), BF16), jax.ShapeDtypeStruct((m, 2 * xw), F32)],
        compiler_params=_params(("arbitrary",)), name=name,
    )(dxc, ppx, kv)


def _merge_fwd(pgt, ya, yb, yc, name):
    s = pgt.shape[0]
    ts = _tile(s, TS_ROW, 8)

    def body(gt_ref, ya_ref, yb_ref, yc_ref, o_ref):
        acc = _sigmoid(gt_ref[:, 0:D].astype(F32)) * ya_ref[...].astype(F32)
        acc = acc + _sigmoid(gt_ref[:, D:2 * D].astype(F32)) * yb_ref[...].astype(F32)
        acc = acc + _sigmoid(gt_ref[:, 2 * D:3 * D].astype(F32)) * yc_ref[...].astype(F32)
        o_ref[...] = acc.astype(o_ref.dtype)

    return pl.pallas_call(
        body, grid=(s // ts,), in_specs=[_rows(ts, 3 * D)] + [_rows(ts, D)] * 3, out_specs=_rows(ts, D),
        out_shape=jax.ShapeDtypeStruct((s, D), BF16), compiler_params=_params(("parallel",)), name=name,
    )(pgt, ya, yb, yc)


def _merge_bwd(dmerged, pgt, ya, yb, yc, name):
    s = pgt.shape[0]
    ts = _tile(s, TS_ROW, 8)

    def body(dm_ref, gt_ref, ya_ref, yb_ref, yc_ref, dya_ref, dyb_ref, dyc_ref, dgt_ref):
        dm = dm_ref[...].astype(F32)
        for j, (y_ref, dy_ref) in enumerate(((ya_ref, dya_ref), (yb_ref, dyb_ref), (yc_ref, dyc_ref))):
            sig = _sigmoid(gt_ref[:, j * D:(j + 1) * D].astype(F32))
            dy_ref[...] = (dm * sig).astype(dy_ref.dtype)
            dgt_ref[:, j * D:(j + 1) * D] = (dm * y_ref[...].astype(F32) * sig * (1.0 - sig)).astype(dgt_ref.dtype)

    return pl.pallas_call(
        body, grid=(s // ts,), in_specs=[_rows(ts, D), _rows(ts, 3 * D)] + [_rows(ts, D)] * 3,
        out_specs=[_rows(ts, D)] * 3 + [_rows(ts, 3 * D)],
        out_shape=[jax.ShapeDtypeStruct((s, D), BF16)] * 3 + [jax.ShapeDtypeStruct((s, 3 * D), BF16)],
        compiler_params=_params(("parallel",)), name=name,
    )(dmerged, pgt, ya, yb, yc)


def _resident(shape):
    nd = len(shape)
    return pl.BlockSpec(shape, lambda i: (0,) * nd, pipeline_mode=pl.Buffered(1))


def _mix_in_fwd(h, w_ts, after, name, tm=512):
    s, d = h.shape
    tm = _tile(s, tm)
    n, na = len(w_ts), len(after)

    def body(h_ref, *refs):
        w_refs, o_refs = refs[:n], refs[n + na:]
        for rows in _sub_blocks(tm):
            hv = h_ref[rows, :]
            for w_ref, o_ref in zip(w_refs, o_refs):
                o_ref[rows, :] = lax.dot_general(hv, w_ref[...], (((1,), (1,)), ((), ())), preferred_element_type=F32).astype(o_ref.dtype)

    return pl.pallas_call(
        body, grid=(s // tm,), in_specs=[_rows(tm, d)] + [_resident(w.shape) for w in w_ts] + [ANY] * na,
        out_specs=[_rows(tm, w.shape[0]) for w in w_ts],
        out_shape=[jax.ShapeDtypeStruct((s, w.shape[0]), BF16) for w in w_ts],
        compiler_params=_params(("parallel",)), name=name,
    )(h, *w_ts, *after)


def _mix_tail_fwd(ya_in, yb_in, xc, pgt, w_ups, w_o, x, g_post, g_next, after, name, tm=512):
    s, d = x.shape
    tm = _tile(s, tm)
    na = len(after)
    branch_ins = (ya_in, yb_in, xc)

    def body(a_ref, b_ref, c_ref, gt_ref, wa_ref, wb_ref, wc_ref, wo_ref, x_ref, gp_ref, gn_ref, *rest):
        ya_ref, yb_ref, yc_ref, m_ref, y_ref, xo_ref, h_ref = rest[na:]
        for rows in _sub_blocks(tm):
            merged = None
            for j, (in_ref, w_ref, out_ref) in enumerate(((a_ref, wa_ref, ya_ref), (b_ref, wb_ref, yb_ref), (c_ref, wc_ref, yc_ref))):
                yj = jnp.dot(in_ref[rows, :], w_ref[...], preferred_element_type=F32)
                out_ref[rows, :] = yj.astype(out_ref.dtype)
                part = _sigmoid(gt_ref[rows, j * D:(j + 1) * D].astype(F32)) * yj
                merged = part if merged is None else merged + part
            merged_b = merged.astype(m_ref.dtype)
            m_ref[rows, :] = merged_b
            y = jnp.dot(merged_b, wo_ref[...], preferred_element_type=F32)
            y_ref[rows, :] = y
            yh, _ = _rms(y)
            xn = x_ref[rows, :] + yh * gp_ref[...]
            xo_ref[rows, :] = xn
            xh, _ = _rms(xn)
            h_ref[rows, :] = (xh * gn_ref[...]).astype(h_ref.dtype)

    bf = lambda: jax.ShapeDtypeStruct((s, d), BF16)
    f32 = lambda: jax.ShapeDtypeStruct((s, d), F32)
    return pl.pallas_call(
        body, grid=(s // tm,),
        in_specs=[_rows(tm, a.shape[1]) for a in branch_ins] + [_rows(tm, 3 * d)] + [_resident(w.shape) for w in w_ups]
        + [_resident(w_o.shape), _rows(tm, d), _fixed((1, d)), _fixed((1, d))] + [ANY] * na,
        out_specs=[_rows(tm, d)] * 7,
        out_shape=[bf(), bf(), bf(), bf(), f32(), f32(), bf()],
        compiler_params=_params(("parallel",)), name=name,
    )(*branch_ins, pgt, *w_ups, w_o, x, g_post, g_next, *after)


def _mix_tail_bwd(dy, pgt, ys, w_ups, w_o, after, name, tm=512):
    s, d = dy.shape
    tm = _tile(s, tm)
    na = len(after)
    widths = [w.shape[0] for w in w_ups]

    def body(dy_ref, gt_ref, ya_ref, yb_ref, yc_ref, wa_ref, wb_ref, wc_ref, wo_ref, *rest):
        dya_ref, dyb_ref, dyc_ref, dgt_ref, da_ref, db_ref, dc_ref = rest[na:]
        nt = (((1,), (1,)), ((), ()))
        for rows in _sub_blocks(tm):
            dm = lax.dot_general(dy_ref[rows, :], wo_ref[...], nt, preferred_element_type=F32)
            for j, (y_ref, dyj_ref, w_ref, din_ref) in enumerate(((ya_ref, dya_ref, wa_ref, da_ref), (yb_ref, dyb_ref, wb_ref, db_ref),
                                                                   (yc_ref, dyc_ref, wc_ref, dc_ref))):
                sig = _sigmoid(gt_ref[rows, j * D:(j + 1) * D].astype(F32))
                dyj = (dm * sig).astype(dyj_ref.dtype)
                dyj_ref[rows, :] = dyj
                dgt_ref[rows, j * D:(j + 1) * D] = (dm * y_ref[rows, :].astype(F32) * sig * (1.0 - sig)).astype(dgt_ref.dtype)
                din_ref[rows, :] = lax.dot_general(dyj, w_ref[...], nt, preferred_element_type=F32).astype(din_ref.dtype)

    bf = lambda w: jax.ShapeDtypeStruct((s, w), BF16)
    return pl.pallas_call(
        body, grid=(s // tm,),
        in_specs=[_rows(tm, d), _rows(tm, 3 * d)] + [_rows(tm, d)] * 3 + [_resident(w.shape) for w in w_ups] + [_resident(w_o.shape)]
        + [ANY] * na,
        out_specs=[_rows(tm, d)] * 3 + [_rows(tm, 3 * d)] + [_rows(tm, w) for w in widths],
        out_shape=[bf(d), bf(d), bf(d), bf(3 * d)] + [bf(w) for w in widths],
        compiler_params=_params(("parallel",)), name=name,
    )(dy, pgt, *ys, *w_ups, w_o, *after)


def _adam_math(w, g, m, v):
    mn = ADAM_B1 * m + (1.0 - ADAM_B1) * g
    vn = ADAM_B2 * v + (1.0 - ADAM_B2) * (g * g)
    m_hat = mn / (1.0 - ADAM_B1 ** ADAM_STEP)
    v_hat = vn / (1.0 - ADAM_B2 ** ADAM_STEP)
    return -ADAM_LR * (m_hat / (jnp.sqrt(v_hat) + ADAM_EPS) + ADAM_WD * w), mn, vn


def _adamw(w, g, m, v, name):
    r, c = w.shape[-2:]
    tr, tc = _block_of(r, c, cap=512 if r % 16 == 0 else 256)

    def spec(a):
        if a.ndim == 2:
            return pl.BlockSpec((tr, tc), lambda i, j: (i, j))
        return pl.BlockSpec((None, tr, tc), lambda i, j: (0, i, j))

    def body(w_ref, g_ref, m_ref, v_ref, d_ref, mo_ref, vo_ref):
        d_ref[...], mo_ref[...], vo_ref[...] = _adam_math(w_ref[...], g_ref[...], m_ref[...], v_ref[...])

    return pl.pallas_call(
        body, grid=(r // tr, c // tc), in_specs=[spec(a) for a in (w, g, m, v)], out_specs=[spec(w)] * 3,
        out_shape=[jax.ShapeDtypeStruct(w.shape, F32)] * 3, compiler_params=_params(("parallel", "parallel")), name=name,
    )(w, g, m, v)


ANY = pl.BlockSpec(memory_space=pl.ANY)


def _place():
    x, y, c = lax.axis_index("x"), lax.axis_index("y"), lax.axis_index("c")
    chips = [(1 - x, y), (x, 1 - y), (1 - x, 1 - y)]
    return x, y, c, chips


def _half(c, rows):
    h = rows // 2
    return pl.ds(pl.multiple_of(c * h, 8), h)


def _by_cols(rows):
    return rows % 32 != 0 and rows != 16


def _half_of(ref, lead, c):
    r, cols = ref.shape[-2:]
    if _by_cols(r):
        return ref.at[(*lead, slice(None), pl.ds(pl.multiple_of(c * (cols // 2), LANE), cols // 2))]
    return ref.at[(*lead, pl.ds(pl.multiple_of(c * (r // 2), 8), r // 2))]


def _half_shape(shape):
    r, cols = shape[-2:]
    return shape[:-2] + ((r, cols // 2) if _by_cols(r) else (r // 2, cols))


def _block_of(r, cols, cap=256):
    if r % 16 == 0:
        return _tile(r, cap, 16), cols
    return r, _tile(cols, cap)


def _place_shard(shard, chip_arr, out_dtype, name, after=()):
    _, r, cols = shard.shape
    tr, tc = _block_of(r, cols)

    def body(chip_ref, s_ref, *rest):
        o_ref = rest[len(after)]
        o_ref[...] = s_ref[...].astype(o_ref.dtype)

    return pl.pallas_call(
        body,
        grid_spec=pltpu.PrefetchScalarGridSpec(
            num_scalar_prefetch=1, grid=(r // tr, cols // tc),
            in_specs=[pl.BlockSpec((None, tr, tc), lambda i, j, chip_ref: (0, i, j))] + [ANY] * len(after),
            out_specs=pl.BlockSpec((None, tr, tc), lambda i, j, chip_ref: (chip_ref[0], i, j))),
        out_shape=jax.ShapeDtypeStruct((4, r, cols), out_dtype),
        compiler_params=_params(("parallel", "parallel")), name=name,
    )(chip_arr, shard, *after)


def _gather_shards(bufs, name):
    n = len(bufs)

    def body(*refs):
        outs = refs[n:2 * n]
        send_ici, recv_ici, send_d2d, recv_d2d = refs[2 * n:]
        x, y, c, chips = _place()
        me = 2 * x + y
        sibling = (x, y, 1 - c)

        def ici(w, p, chip_of_block, to):
            rows = _half(c, outs[w].shape[1])
            block = outs[w].at[chip_of_block, rows]
            return pltpu.make_async_remote_copy(
                src_ref=block, dst_ref=block, send_sem=send_ici.at[w, p], recv_sem=recv_ici.at[w, p], device_id=to, device_id_type=MESH)

        def d2d(w, p, chip_of_block, half_of):
            rows = _half(half_of, outs[w].shape[1])
            block = outs[w].at[chip_of_block, rows]
            return pltpu.make_async_remote_copy(
                src_ref=block, dst_ref=block, send_sem=send_d2d.at[w, p], recv_sem=recv_d2d.at[w, p], device_id=sibling, device_id_type=MESH)

        sends = [ici(w, p, me, (*chip, c)) for p, chip in enumerate(chips) for w in range(n)]
        for cp in sends:
            cp.start()
        passed = []
        for p, (px, py) in enumerate(chips):
            for w in range(n):
                ici(w, p, 2 * px + py, (px, py, c)).wait_recv()
                fwd = d2d(w, p, 2 * px + py, c)
                fwd.start()
                passed.append(fwd)
        for p, (px, py) in enumerate(chips):
            for w in range(n):
                d2d(w, p, 2 * px + py, 1 - c).wait_recv()
        for cp in sends + passed:
            cp.wait_send()

    return pl.pallas_call(
        body, in_specs=[ANY] * n, out_specs=[ANY] * n,
        out_shape=[jax.ShapeDtypeStruct(a.shape, a.dtype) for a in bufs],
        input_output_aliases={w: w for w in range(n)},
        scratch_shapes=[pltpu.SemaphoreType.DMA((n, 3))] * 4,
        compiler_params=pltpu.CompilerParams(has_side_effects=True), name=name,
    )(*bufs)


HBM = pl.BlockSpec(memory_space=pltpu.HBM)
SEM = pl.BlockSpec(memory_space=pltpu.SEMAPHORE)
EFFECT = pltpu.SideEffectType.DATAFLOW_SIDE_EFFECTING


def _in_hbm(arrays):
    return [pltpu.with_memory_space_constraint(a, pltpu.HBM) for a in arrays]


def _gather_start(bufs, after, name):
    n, na = len(bufs), len(after)

    def body(*refs):
        send_sem, recv_sem = refs[n + na], refs[n + na + 1]
        outs = refs[n + na + 2:2 * n + na + 2]
        token = refs[2 * n + na + 2]
        x, y, c, chips = _place()
        me = 2 * x + y
        for p, chip in enumerate(chips):
            for w in range(n):
                block = _half_of(outs[w], (me,), c)
                pltpu.make_async_remote_copy(
                    src_ref=block, dst_ref=block, send_sem=send_sem, recv_sem=recv_sem,
                    device_id=(*chip, c), device_id_type=MESH).start()
        token[...] = jnp.zeros_like(token)

    out = pl.pallas_call(
        body, name=name, in_specs=[HBM] * n + [ANY] * na,
        out_specs=[SEM, SEM] + [HBM] * n + [pl.BlockSpec(memory_space=pltpu.VMEM)],
        out_shape=[pltpu.SemaphoreType.DMA(()), pltpu.SemaphoreType.DMA(())]
        + [pltpu.HBM(a.shape, a.dtype) for a in bufs] + [jax.ShapeDtypeStruct((8, LANE), F32)],
        input_output_aliases={w: w + 2 for w in range(n)},
        compiler_params=pltpu.CompilerParams(has_side_effects=EFFECT),
    )(*_in_hbm(bufs), *after)
    return out[0], out[1], list(out[2:2 + n]), out[2 + n]


def _gather_pass(bufs, send_sem, recv_sem, after, name):
    n, na = len(bufs), len(after)

    def body(*refs):
        send1, recv1 = refs[n], refs[n + 1]
        send2, recv2 = refs[n + 2 + na], refs[n + 3 + na]
        outs = refs[n + 4 + na:2 * n + 4 + na]
        x, y, c, chips = _place()
        me = 2 * x + y
        arrivals = [(w, px, py) for px, py in chips for w in range(n)]
        for w, px, py in arrivals:
            first = pltpu.make_async_remote_copy(
                src_ref=_half_of(outs[w], (me,), c), dst_ref=_half_of(outs[w], (2 * px + py,), c), send_sem=send1, recv_sem=recv1,
                device_id=(px, py, c), device_id_type=MESH)
            first.wait_send()
            first.wait_recv()
        for w, px, py in arrivals:
            arrived = _half_of(outs[w], (2 * px + py,), c)
            pltpu.make_async_remote_copy(
                src_ref=arrived, dst_ref=arrived, send_sem=send2, recv_sem=recv2,
                device_id=(x, y, 1 - c), device_id_type=MESH).start()

    out = pl.pallas_call(
        body, name=name, in_specs=[HBM] * n + [SEM, SEM] + [ANY] * na,
        out_specs=[SEM, SEM] + [HBM] * n,
        out_shape=[pltpu.SemaphoreType.DMA(()), pltpu.SemaphoreType.DMA(())] + [pltpu.HBM(a.shape, a.dtype) for a in bufs],
        input_output_aliases={w: w + 2 for w in range(n)},
        compiler_params=pltpu.CompilerParams(has_side_effects=EFFECT),
    )(*bufs, send_sem, recv_sem, *after)
    return out[0], out[1], list(out[2:])


def _gather_finish(bufs, send_sem, recv_sem, after, name):
    n, na = len(bufs), len(after)

    def body(*refs):
        send2, recv2 = refs[n], refs[n + 1]
        outs = refs[n + 2 + na:2 * n + 2 + na]
        x, y, c, chips = _place()
        for p, (px, py) in enumerate(chips):
            for w in range(n):
                passed = pltpu.make_async_remote_copy(
                    src_ref=_half_of(outs[w], (2 * px + py,), c), dst_ref=_half_of(outs[w], (2 * px + py,), 1 - c),
                    send_sem=send2, recv_sem=recv2, device_id=(x, y, 1 - c), device_id_type=MESH)
                passed.wait_send()
                passed.wait_recv()

    out = pl.pallas_call(
        body, name=name, in_specs=[HBM] * n + [SEM, SEM] + [ANY] * na, out_specs=[HBM] * n,
        out_shape=[pltpu.HBM(a.shape, a.dtype) for a in bufs],
        input_output_aliases={w: w for w in range(n)},
        compiler_params=pltpu.CompilerParams(has_side_effects=EFFECT),
    )(*bufs, send_sem, recv_sem, *after)
    return list(out)


def _pair_exchange(grads, name):
    n = len(grads)

    def body(*refs):
        ins, outs = refs[:n], refs[n:2 * n]
        send_sem, recv_sem = refs[2 * n:]
        x, y, c, _ = _place()
        copies = []
        for w in range(n):
            copies.append(pltpu.make_async_remote_copy(
                src_ref=_half_of(ins[w], (slice(None),), 1 - c), dst_ref=outs[w], send_sem=send_sem.at[w], recv_sem=recv_sem.at[w],
                device_id=(x, y, 1 - c), device_id_type=MESH))
        for cp in copies:
            cp.start()
        for cp in copies:
            cp.wait()

    return pl.pallas_call(
        body, in_specs=[ANY] * n, out_specs=[ANY] * n,
        out_shape=[jax.ShapeDtypeStruct(_half_shape(a.shape), a.dtype) for a in grads],
        scratch_shapes=[pltpu.SemaphoreType.DMA((n,))] * 2,
        compiler_params=pltpu.CompilerParams(has_side_effects=True), name=name,
    )(*grads)


def _pair_exchange_start(grads, after, name):
    n, na = len(grads), len(after)
    lands = [lax.empty(_half_shape(a.shape), a.dtype) for a in grads]

    def body(*refs):
        send_sem, recv_sem = refs[2 * n + na], refs[2 * n + na + 1]
        srcs = refs[2 * n + na + 2:3 * n + na + 2]
        dsts = refs[3 * n + na + 2:4 * n + na + 2]
        token = refs[4 * n + na + 2]
        x, y, c, _ = _place()
        for w in range(n):
            pltpu.make_async_remote_copy(
                src_ref=_half_of(srcs[w], (slice(None),), 1 - c), dst_ref=dsts[w], send_sem=send_sem, recv_sem=recv_sem,
                device_id=(x, y, 1 - c), device_id_type=MESH).start()
        token[...] = jnp.zeros_like(token)

    out = pl.pallas_call(
        body, name=name, in_specs=[HBM] * (2 * n) + [ANY] * na,
        out_specs=[SEM, SEM] + [HBM] * (2 * n) + [pl.BlockSpec(memory_space=pltpu.VMEM)],
        out_shape=[pltpu.SemaphoreType.DMA(()), pltpu.SemaphoreType.DMA(())]
        + [pltpu.HBM(a.shape, a.dtype) for a in grads + lands] + [jax.ShapeDtypeStruct((8, LANE), F32)],
        input_output_aliases={w: w + 2 for w in range(2 * n)},
        compiler_params=pltpu.CompilerParams(has_side_effects=EFFECT),
    )(*_in_hbm(grads), *_in_hbm(lands), *after)
    return out[0], out[1], list(out[2:2 + n]), list(out[2 + n:2 + 2 * n]), out[2 + 2 * n]


def _pair_exchange_finish(grads, lands, send_sem, recv_sem, after, name):
    n, na = len(grads), len(after)

    def body(*refs):
        send, recv = refs[2 * n], refs[2 * n + 1]
        srcs = refs[2 * n + 2 + na:3 * n + 2 + na]
        dsts = refs[3 * n + 2 + na:4 * n + 2 + na]
        x, y, c, _ = _place()
        for w in range(n):
            copy = pltpu.make_async_remote_copy(
                src_ref=_half_of(srcs[w], (slice(None),), 1 - c), dst_ref=dsts[w], send_sem=send, recv_sem=recv,
                device_id=(x, y, 1 - c), device_id_type=MESH)
            copy.wait_send()
            copy.wait_recv()

    out = pl.pallas_call(
        body, name=name, in_specs=[HBM] * (2 * n) + [SEM, SEM] + [ANY] * na, out_specs=[HBM] * (2 * n),
        out_shape=[pltpu.HBM(a.shape, a.dtype) for a in grads + lands],
        input_output_aliases={w: w for w in range(2 * n)},
        compiler_params=pltpu.CompilerParams(has_side_effects=EFFECT),
    )(*grads, *lands, send_sem, recv_sem, *after)
    return list(out[:n]), list(out[n:])


def _pair_sum(g, got, c_arr, name):
    _, r, cols = g.shape
    hr, hc = _half_shape((r, cols))
    tr, tc = _block_of(hr, hc)
    nbr, nbc = hr // tr, hc // tc
    by_cols = _by_cols(r)

    def body(c_ref, g_ref, got_ref, o_ref):
        o_ref[...] = (g_ref[...].astype(F32) + got_ref[...].astype(F32)).astype(o_ref.dtype)

    def mine(j, i, k, c_ref):
        return (j, i, c_ref[0] * nbc + k) if by_cols else (j, c_ref[0] * nbr + i, k)

    return pl.pallas_call(
        body,
        grid_spec=pltpu.PrefetchScalarGridSpec(
            num_scalar_prefetch=1, grid=(4, nbr, nbc),
            in_specs=[pl.BlockSpec((None, tr, tc), mine),
                      pl.BlockSpec((None, tr, tc), lambda j, i, k, c_ref: (j, i, k))],
            out_specs=pl.BlockSpec((None, tr, tc), lambda j, i, k, c_ref: (j, i, k))),
        out_shape=jax.ShapeDtypeStruct((4, hr, hc), BF16),
        compiler_params=_params(("parallel", "parallel", "parallel")), name=name,
    )(c_arr, g, got)


def _chip_exchange(parts, name):
    n = len(parts)

    def body(*refs):
        ins, outs = refs[:n], refs[n:2 * n]
        send_sem, recv_sem = refs[2 * n:]
        x, y, c, chips = _place()
        copies = []
        for p, (px, py) in enumerate(chips):
            for w in range(n):
                copies.append(pltpu.make_async_remote_copy(
                    src_ref=ins[w].at[2 * px + py], dst_ref=outs[w].at[p], send_sem=send_sem.at[w, p], recv_sem=recv_sem.at[w, p],
                    device_id=(px, py, c), device_id_type=MESH))
        for cp in copies:
            cp.start()
        for cp in copies:
            cp.wait()

    return pl.pallas_call(
        body, in_specs=[ANY] * n, out_specs=[ANY] * n,
        out_shape=[jax.ShapeDtypeStruct((3,) + a.shape[1:], a.dtype) for a in parts],
        scratch_shapes=[pltpu.SemaphoreType.DMA((n, 3))] * 2,
        compiler_params=pltpu.CompilerParams(has_side_effects=True), name=name,
    )(*parts)


def _chip_exchange_start(parts, after, name):
    n, na = len(parts), len(after)
    lands = [lax.empty((3,) + a.shape[1:], a.dtype) for a in parts]

    def body(*refs):
        send_sem, recv_sem = refs[2 * n + na], refs[2 * n + na + 1]
        srcs = refs[2 * n + na + 2:3 * n + na + 2]
        dsts = refs[3 * n + na + 2:4 * n + na + 2]
        token = refs[4 * n + na + 2]
        x, y, c, chips = _place()
        for p, (px, py) in enumerate(chips):
            for w in range(n):
                pltpu.make_async_remote_copy(
                    src_ref=srcs[w].at[2 * px + py], dst_ref=dsts[w].at[p], send_sem=send_sem, recv_sem=recv_sem,
                    device_id=(px, py, c), device_id_type=MESH).start()
        token[...] = jnp.zeros_like(token)

    out = pl.pallas_call(
        body, name=name, in_specs=[HBM] * (2 * n) + [ANY] * na,
        out_specs=[SEM, SEM] + [HBM] * (2 * n) + [pl.BlockSpec(memory_space=pltpu.VMEM)],
        out_shape=[pltpu.SemaphoreType.DMA(()), pltpu.SemaphoreType.DMA(())]
        + [pltpu.HBM(a.shape, a.dtype) for a in parts + lands] + [jax.ShapeDtypeStruct((8, LANE), F32)],
        input_output_aliases={w: w + 2 for w in range(2 * n)},
        compiler_params=pltpu.CompilerParams(has_side_effects=EFFECT),
    )(*_in_hbm(parts), *_in_hbm(lands), *after)
    return out[0], out[1], list(out[2:2 + n]), list(out[2 + n:2 + 2 * n]), out[2 + 2 * n]


def _chip_exchange_finish(parts, lands, send_sem, recv_sem, after, name):
    n, na = len(parts), len(after)

    def body(*refs):
        send, recv = refs[2 * n], refs[2 * n + 1]
        srcs = refs[2 * n + 2 + na:3 * n + 2 + na]
        dsts = refs[3 * n + 2 + na:4 * n + 2 + na]
        x, y, c, chips = _place()
        for p, (px, py) in enumerate(chips):
            for w in range(n):
                copy = pltpu.make_async_remote_copy(
                    src_ref=srcs[w].at[2 * px + py], dst_ref=dsts[w].at[p], send_sem=send, recv_sem=recv,
                    device_id=(px, py, c), device_id_type=MESH)
                copy.wait_send()
                copy.wait_recv()

    out = pl.pallas_call(
        body, name=name, in_specs=[HBM] * (2 * n) + [SEM, SEM] + [ANY] * na, out_specs=[HBM] * (2 * n),
        out_shape=[pltpu.HBM(a.shape, a.dtype) for a in parts + lands],
        input_output_aliases={w: w for w in range(2 * n)},
        compiler_params=pltpu.CompilerParams(has_side_effects=EFFECT),
    )(*parts, *lands, send_sem, recv_sem, *after)
    return list(out[:n]), list(out[n:])


def _chip_sum(part, got, place_arr, name):
    _, hr, hc = part.shape
    by_cols = _by_cols(hr)
    tr, tc = _block_of(hr, hc)
    nbr, nbc = hr // tr, hc // tc

    def body(place_ref, p_ref, got_ref, o_ref):
        acc = p_ref[...].astype(F32)
        for p in range(3):
            acc = acc + got_ref[p].astype(F32)
        o_ref[...] = acc

    def mine(i, k, place_ref):
        return (i, place_ref[1] * nbc + k) if by_cols else (place_ref[1] * nbr + i, k)

    return pl.pallas_call(
        body,
        grid_spec=pltpu.PrefetchScalarGridSpec(
            num_scalar_prefetch=1, grid=(nbr, nbc),
            in_specs=[pl.BlockSpec((None, tr, tc), lambda i, k, place_ref: (place_ref[0], i, k)),
                      pl.BlockSpec((3, tr, tc), lambda i, k, place_ref: (0, i, k))],
            out_specs=pl.BlockSpec((tr, tc), mine)),
        out_shape=jax.ShapeDtypeStruct((hr, 2 * hc) if by_cols else (2 * hr, hc), F32),
        compiler_params=_params(("parallel", "parallel")), name=name,
    )(place_arr, part, got)


def _pair_join_start(bufs, name):
    n = len(bufs)

    def body(*refs):
        send_sem, recv_sem = refs[n], refs[n + 1]
        outs = refs[n + 2:2 * n + 2]
        token = refs[2 * n + 2]
        x, y, c, _ = _place()
        for w in range(n):
            block = _half_of(outs[w], (), c)
            pltpu.make_async_remote_copy(
                src_ref=block, dst_ref=block, send_sem=send_sem, recv_sem=recv_sem,
                device_id=(x, y, 1 - c), device_id_type=MESH).start()
        token[...] = jnp.zeros_like(token)

    out = pl.pallas_call(
        body, name=name, in_specs=[HBM] * n,
        out_specs=[SEM, SEM] + [HBM] * n + [pl.BlockSpec(memory_space=pltpu.VMEM)],
        out_shape=[pltpu.SemaphoreType.DMA(()), pltpu.SemaphoreType.DMA(())]
        + [pltpu.HBM(a.shape, a.dtype) for a in bufs] + [jax.ShapeDtypeStruct((8, LANE), F32)],
        input_output_aliases={w: w + 2 for w in range(n)},
        compiler_params=pltpu.CompilerParams(has_side_effects=EFFECT),
    )(*_in_hbm(bufs))
    return out[0], out[1], list(out[2:2 + n]), out[2 + n]


def _pair_join_finish(bufs, send_sem, recv_sem, after, name):
    n, na = len(bufs), len(after)

    def body(*refs):
        send, recv = refs[n], refs[n + 1]
        outs = refs[n + 2 + na:2 * n + 2 + na]
        x, y, c, _ = _place()
        for w in range(n):
            copy = pltpu.make_async_remote_copy(
                src_ref=_half_of(outs[w], (), c), dst_ref=_half_of(outs[w], (), 1 - c), send_sem=send, recv_sem=recv,
                device_id=(x, y, 1 - c), device_id_type=MESH)
            copy.wait_send()
            copy.wait_recv()

    out = pl.pallas_call(
        body, name=name, in_specs=[HBM] * n + [SEM, SEM] + [ANY] * na, out_specs=[HBM] * n,
        out_shape=[pltpu.HBM(a.shape, a.dtype) for a in bufs],
        input_output_aliases={w: w for w in range(n)},
        compiler_params=pltpu.CompilerParams(has_side_effects=EFFECT),
    )(*bufs, send_sem, recv_sem, *after)
    return list(out)


SMALL = ("ffn1_pre_g", "ffn1_post_g", "mix_pre_g", "gla_norm_g", "mem_norm_g", "mix_post_g", "ffn2_pre_g", "ffn2_post_g", "final_g",
         "b_f", "pool_scale", "w_pool", "w_fu")
N_GAINS = 9
SMALL_PACKS = ((16, D), (24, 512), (4 * LANE, LANE))
W_FU_ROW = 8


def _all_sum_small(gs, name, after=()):
    ins = [gs[n] for n in SMALL[:N_GAINS]] + [gs["b_f"], gs["pool_scale"], gs["w_fu_pad"], gs["w_pool"].reshape(4 * LANE, LANE)]

    def body(*refs):
        gain_refs = refs[:N_GAINS]
        bf_ref, ps_ref, wfu_ref, wp_ref = refs[N_GAINS:N_GAINS + 4]
        outs = refs[N_GAINS + 4 + len(after):N_GAINS + 7 + len(after)]
        mine_a, mine_b, all_a, all_b, all_c, send_sems, recv_sems = refs[N_GAINS + 7 + len(after):]
        mine_a[...] = jnp.zeros_like(mine_a)
        for i, ref in enumerate(gain_refs):
            mine_a[i:i + 1, :] = ref[...]
        mine_b[...] = jnp.zeros_like(mine_b)
        mine_b[0:1, :] = bf_ref[...]
        mine_b[1:2, :] = ps_ref[...]
        mine_b[W_FU_ROW:W_FU_ROW + GATE_RANK, :] = wfu_ref[0:GATE_RANK, :]
        packs = ((mine_a, all_a), (mine_b, all_b), (wp_ref, all_c))
        x, y, c, chips = _place()
        me, sibling = (x, y, c), (x, y, 1 - c)

        def copy(t, k, block, to, own=False):
            px, py, pc = block
            slot = packs[t][1].at[4 * px + 2 * py + pc]
            return pltpu.make_async_remote_copy(
                src_ref=packs[t][0] if own else slot, dst_ref=slot,
                send_sem=send_sems.at[t, k], recv_sem=recv_sems.at[t, k], device_id=to, device_id_type=MESH)

        started = []
        for t, (mine, everyone) in enumerate(packs):
            everyone[4 * x + 2 * y + c] = mine[...]
            started.append(copy(t, 0, me, sibling, own=True))
            started += [copy(t, 1 + j, me, (*chip, c), own=True) for j, chip in enumerate(chips)]
        for cp in started:
            cp.start()
        passed = []
        for j, chip in enumerate(chips):
            for t in range(len(packs)):
                copy(t, 1 + j, (*chip, c), me).wait_recv()
                fwd = copy(t, 4 + j, (*chip, c), sibling)
                fwd.start()
                passed.append(fwd)
        for t in range(len(packs)):
            copy(t, 0, sibling, me).wait_recv()
            for j, chip in enumerate(chips):
                copy(t, 4 + j, (*chip, 1 - c), me).wait_recv()
        for cp in started + passed:
            cp.wait_send()
        for (_, everyone), o_ref in zip(packs, outs):
            acc = everyone[0]
            for k in range(1, 8):
                acc = acc + everyone[k]
            o_ref[...] = acc

    vmem = pl.BlockSpec(memory_space=pltpu.VMEM)
    return pl.pallas_call(
        body, in_specs=[vmem] * len(ins) + [ANY] * len(after), out_specs=[vmem] * 3,
        out_shape=[jax.ShapeDtypeStruct(shape, F32) for shape in SMALL_PACKS],
        scratch_shapes=[pltpu.VMEM(SMALL_PACKS[0], F32), pltpu.VMEM(SMALL_PACKS[1], F32)]
        + [pltpu.VMEM((8,) + shape, F32) for shape in SMALL_PACKS]
        + [pltpu.SemaphoreType.DMA((3, 7)), pltpu.SemaphoreType.DMA((3, 7))],
        compiler_params=pltpu.CompilerParams(has_side_effects=True, vmem_limit_bytes=VMEM_LIMIT), name=name,
    )(*ins, *after)


def _adamw_small(sums, params, chip_arr, name):
    flat = [a for n in SMALL for a in params[n]]

    def body(chip_ref, a_ref, b_ref, c_ref, *refs):
        ins, outs = refs[:len(flat)], refs[len(flat):]
        for i, n in enumerate(SMALL):
            w_ref, m_ref, v_ref = ins[3 * i:3 * i + 3]
            g_ref, d_ref, mo_ref, vo_ref = outs[4 * i:4 * i + 4]
            if n == "w_pool":
                pieces = [((0, k), c_ref[k * LANE:(k + 1) * LANE, :]) for k in range(4)]
            elif n == "w_fu":
                mine = pl.ds(pl.multiple_of(chip_ref[0] * LANE, LANE), LANE)
                pieces = [((0,), b_ref[W_FU_ROW:W_FU_ROW + GATE_RANK, mine])]
            elif n == "b_f":
                pieces = [((), b_ref[0:1, :])]
            elif n == "pool_scale":
                pieces = [((), b_ref[1:2, :])]
            else:
                pieces = [((), a_ref[i:i + 1, :])]
            for at, g in pieces:
                d, mn, vn = _adam_math(w_ref[at], g, m_ref[at], v_ref[at])
                g_ref[at] = g
                d_ref[at] = d
                mo_ref[at] = mn
                vo_ref[at] = vn

    def whole(shape):
        return pl.BlockSpec(shape, lambda i, chip_ref: (0,) * len(shape))

    out = pl.pallas_call(
        body,
        grid_spec=pltpu.PrefetchScalarGridSpec(
            num_scalar_prefetch=1, grid=(1,),
            in_specs=[whole(a.shape) for a in list(sums) + flat],
            out_specs=[whole(params[n][0].shape) for n in SMALL for _ in range(4)]),
        out_shape=[jax.ShapeDtypeStruct(params[n][0].shape, F32) for n in SMALL for _ in range(4)],
        compiler_params=_params(("arbitrary",)), name=name,
    )(chip_arr, *sums, *flat)
    return {n: tuple(out[4 * i:4 * i + 4]) for i, n in enumerate(SMALL)}


def _ffn_bwd(dz, x_norm, ab, u, w_in, w_out, x, g_pre, dres, tag, emit, advance, after=(), post=None):
    dw_out = _mm(u, dz, ta=True, out_dtype=BF16, tm=1408, tk=2048, after=after, name=tag + "_out_dw")
    behind = emit(tag + "_w_out", dw_out)
    dab = _ffn_out_dx_swiglu(dz, w_out, ab, behind, name=tag + "_out_dx")
    behind = advance((dab,))
    dw_in = _mm(x_norm, dab, ta=True, out_dtype=BF16, tm=512, tk=4096, shards=4, after=behind, name=tag + "_in_dw")
    behind = emit(tag + "_w_in", dw_in)
    out = _mm_rms_bwd([(dab, w_in)], x, g_pre, dres, after=behind, post=post, name=tag + "_in_dx")
    return (*out, advance((out[0],)))


def _local_step(x, mem, target, small, gather, emit, advance):
    behind = gather("start", "ffn1i", ())
    behind = gather("start", "ffn1o", behind)
    h1 = _norm_fwd(x, small["ffn1_pre_g"], BF16, name="ffn1_pre", after=behind)
    gather("pass", "ffn1i", (h1,))
    big = gather("finish", "ffn1i", ())
    behind = gather("start", "mixa", (big["ffn1_w_in"],))
    behind = gather("start", "mixb", behind)
    behind = gather("start", "ffn2", behind)
    ab1, u1 = _ffn_in_swiglu(h1, big["ffn1_w_in"], name="ffn1_in", after=behind)
    gather("pass", "ffn1o", (ab1,))
    big.update(gather("finish", "ffn1o", ()))
    behind = gather("pass", "mixa", (u1,))
    f1, x1, h = _mm_resid_norm(u1, big["ffn1_w_out"], x, small["ffn1_post_g"], 0.5, small["mix_pre_g"], name="ffn1_out", after=behind)
    big.update(gather("finish", "mixa", (h,)))
    small = dict(small, w_fu_pad=big["w_fu_pad"])
    behind = gather("pass", "mixb", (h,))
    pg, ppx, pgt = _mix_in_fwd(h, [big["w_gla_t"], big["w_px_t"], big["w_gates_t"]], behind, name="mix_in")
    big.update(gather("finish", "mixb", (pgt,)))
    mem_n = _norm_fwd(mem, small["mem_norm_g"], BF16, name="mem_norm")
    kv = _mm(mem_n, big["w_mem_kv"], out_dtype=BF16, name="mem_kv")
    ya_in, sp, so, o_gla = _gla_fwd(pg, small["w_fu_pad"], small["b_f"], small["gla_norm_g"], name="gla_fwd")
    yb_in = _pool_fwd(ppx, small["w_pool_b"], small["pool_scale"], name="pool_fwd")
    xc = _xattn_fwd(ppx, kv, name="xattn_fwd")
    behind = gather("pass", "ffn2", (xc,))
    w_ups = [big["w_up_gla"], big["w_up_pool"], big["w_up_xattn"]]
    ya, yb, yc, merged, ymix, x2, h2 = _mix_tail_fwd(ya_in, yb_in, xc, pgt, w_ups, big["w_o"], x1, small["mix_post_g"],
                                                     small["ffn2_pre_g"], behind, name="mix_tail")
    big.update(gather("finish", "ffn2", (h2,)))
    ab2, u2 = _ffn_in_swiglu(h2, big["ffn2_w_in"], name="ffn2_in")
    gs = {}
    dz2, dx3, gs["ffn2_post_g"], gs["final_g"], loss = _ffn_out_loss(u2, big["ffn2_w_out"], x2, small["ffn2_post_g"], 0.5,
                                                                    small["final_g"], target, name="ffn2_out_loss")
    dx2, gs["ffn2_pre_g"], dy, gs["mix_post_g"], behind = _ffn_bwd(
        dz2, h2, ab2, u2, big["ffn2_w_in"], big["ffn2_w_out"], x2, small["ffn2_pre_g"], dx3, "ffn2", emit, advance,
        post=(ymix, small["mix_post_g"], 1.0))
    emit("w_o", _mm(merged, dy, ta=True, out_dtype=BF16, tm=512, tk=4096, after=behind, name="mix_out_dw"))
    dya, dyb, dyc, dgt, dya_in, dyb_in, dxc = _mix_tail_bwd(dy, pgt, (ya, yb, yc), w_ups, big["w_o"], (), name="mix_tail_bwd")
    emit("w_up_gla", _mm(ya_in, dya, ta=True, out_dtype=BF16, tm=512, tk=4096, name="up_gla_dw"))
    emit("w_up_pool", _mm(yb_in, dyb, ta=True, out_dtype=BF16, tm=512, tk=4096, shards=4, name="up_pool_dw"))
    emit("w_up_xattn", _mm(xc, dyc, ta=True, out_dtype=BF16, tm=512, tk=4096, shards=4, name="up_xattn_dw"))
    dpg, gs["w_fu_pad"], gs["b_f"], gs["gla_norm_g"] = _gla_bwd(pg, sp, so, o_gla, dya_in, small["w_fu_pad"], small["b_f"], small["gla_norm_g"], name="gla_bwd")
    dp, gs["w_pool"], gs["pool_scale"] = _pool_bwd(dyb_in, ppx, small["w_pool_b"], small["pool_scale"], name="pool_bwd")
    dxq, dkv = _xattn_bwd(dxc, ppx, kv, name="xattn_bwd")
    dkv = dkv.astype(BF16)
    emit("w_mem_kv", _mm(mem_n, dkv, ta=True, out_dtype=BF16, name="mem_kv_dw"))
    dmem_n = _mm(dkv, big["w_mem_kv"], tb=True, name="mem_kv_dx")
    _, gs["mem_norm_g"] = _rms_bwd(mem, small["mem_norm_g"], [dmem_n], None, 1.0, BF16, name="mem_norm_bwd")
    emit("w_gla", _mm(dpg, h, ta=True, out_dtype=BF16, tm=640, tk=4096, name="mix_in_gla_dw"))
    emit("w_p", _mm(dp, h, ta=True, out_dtype=BF16, tm=512, tk=4096, name="mix_in_p_dw"))
    emit("w_xq", _mm(dxq, h, ta=True, out_dtype=BF16, tm=512, tk=4096, name="mix_in_xq_dw"))
    behind = emit("w_gates", _mm(dgt, h, ta=True, out_dtype=BF16, tm=512, tk=4096, name="mix_in_gates_dw"))
    pairs = [(dpg, big["w_gla_t"]), (dp, big["w_p_t"]), (dxq, big["w_xq_t"]), (dgt, big["w_gates_t"])]
    dx1, gs["mix_pre_g"], dz1, gs["ffn1_post_g"] = _mm_rms_bwd(pairs, x1, small["mix_pre_g"], dx2, after=behind,
                                                               post=(f1, small["ffn1_post_g"], 0.5), tm=256, name="mix_in_dx")
    behind = advance((dx1,))
    dx0, gs["ffn1_pre_g"], _ = _ffn_bwd(dz1, h1, ab1, u1, big["ffn1_w_in"], big["ffn1_w_out"], x, small["ffn1_pre_g"], dx1,
                                        "ffn1", emit, advance, after=behind)
    return loss, dx0, gs


BIG = ("ffn1_w_in", "ffn1_w_out", "w_in", "w_mem_kv", "w_up_gla", "w_up_pool", "w_up_xattn", "w_o", "ffn2_w_in", "ffn2_w_out")
COL_SHARDED = ("ffn1_w_in", "w_in", "w_up_pool", "w_up_xattn", "ffn2_w_in")
GATHER_GROUPS = {"ffn1i": ("ffn1_w_in",), "ffn1o": ("ffn1_w_out",), "mixa": ("w_in", "w_fu"),
                 "mixb": ("w_mem_kv", "w_up_gla", "w_up_pool", "w_up_xattn", "w_o"), "ffn2": ("ffn2_w_in", "ffn2_w_out")}
REDUCE_GROUPS = {"ffn2": ("ffn2_w_out", "ffn2_w_in"),
                 "mix": ("w_o", "w_up_gla", "w_up_pool", "w_up_xattn", "w_mem_kv", "w_gla", "w_p", "w_xq", "w_gates"),
                 "ffn1_out": ("ffn1_w_out",),
                 "ffn1_in": ("ffn1_w_in",)}
REDUCE_LAST = "ffn1_in"
GAINS = ("ffn1_pre_g", "ffn1_post_g", "mix_pre_g", "gla_norm_g", "mem_norm_g", "mix_post_g", "ffn2_pre_g", "ffn2_post_g", "final_g")
WEIGHTS = ("ffn1_pre_g", "ffn1_w_in", "ffn1_w_out", "ffn1_post_g", "mix_pre_g", "w_in", "w_fu", "b_f", "gla_norm_g", "w_pool",
           "pool_scale", "mem_norm_g", "w_mem_kv", "w_up_gla", "w_up_pool", "w_up_xattn", "w_o", "mix_post_g", "ffn2_pre_g",
           "ffn2_w_in", "ffn2_w_out", "ffn2_post_g", "final_g")
IN_GLA, IN_F, IN_PX, IN_GATES, IN_END = 0, 3072, 3088, 4112, 7184
def _cols_from_shards(g):
    return jnp.transpose(g, (1, 0, 2)).reshape(g.shape[1], 4 * g.shape[2])


def kernel(x, mem, ffn1_pre_g, ffn1_w_in, ffn1_w_out, ffn1_post_g, mix_pre_g, w_in, w_fu, b_f, gla_norm_g, w_pool, pool_scale, mem_norm_g, w_mem_kv, w_up_gla, w_up_pool, w_up_xattn, w_o, mix_post_g, ffn2_pre_g, ffn2_w_in, ffn2_w_out, ffn2_post_g, final_g, loss_target, m_ffn1_pre_g, m_ffn1_w_in, m_ffn1_w_out, m_ffn1_post_g, m_mix_pre_g, m_w_in, m_w_fu, m_b_f, m_gla_norm_g, m_w_pool, m_pool_scale, m_mem_norm_g, m_w_mem_kv, m_w_up_gla, m_w_up_pool, m_w_up_xattn, m_w_o, m_mix_post_g, m_ffn2_pre_g, m_ffn2_w_in, m_ffn2_w_out, m_ffn2_post_g, m_final_g, v_ffn1_pre_g, v_ffn1_w_in, v_ffn1_w_out, v_ffn1_post_g, v_mix_pre_g, v_w_in, v_w_fu, v_b_f, v_gla_norm_g, v_w_pool, v_pool_scale, v_mem_norm_g, v_w_mem_kv, v_w_up_gla, v_w_up_pool, v_w_up_xattn, v_w_o, v_mix_post_g, v_ffn2_pre_g, v_ffn2_w_in, v_ffn2_w_out, v_ffn2_post_g, v_final_g):
    args = dict(locals())
    w = {n: args[n][0] for n in WEIGHTS}
    m = {n: args["m_" + n][0] for n in WEIGHTS}
    v = {n: args["v_" + n][0] for n in WEIGHTS}
    xi, yi, ci = lax.axis_index("x"), lax.axis_index("y"), lax.axis_index("c")
    chip = 2 * xi + yi

    c_arr = jnp.reshape(ci, (1,)).astype(jnp.int32)
    chip_arr = jnp.reshape(chip, (1,)).astype(jnp.int32)
    place_arr = jnp.stack([chip, ci]).astype(jnp.int32)
    shard_of = {n: (jnp.transpose(args[n][0])[None] if n == "w_in" else args[n]) for n in BIG}
    shard_of["w_fu"] = args["w_fu"]
    placed, inflight = {}, {}

    def place(names, after):
        for n in names:
            if n not in placed:
                placed[n] = _place_shard(shard_of[n], chip_arr, F32 if n == "w_fu" else BF16, name="place_" + n, after=after)

    def relayout(names, gathered):
        out = {}
        for n, g in zip(names, gathered):
            if n == "w_fu":
                w_fu_full = _cols_from_shards(g)
                out["w_fu_pad"] = jnp.concatenate([w_fu_full, jnp.zeros((LANE - GATE_RANK, 512), F32)], axis=0).astype(BF16)
            elif n == "w_in":
                wt = g.reshape(IN_END, D)
                out["w_gla_t"] = jnp.concatenate([wt[IN_GLA:IN_PX], jnp.zeros((PG_W - IN_PX, D), BF16)], axis=0)
                out["w_px_t"] = wt[IN_PX:IN_GATES]
                out["w_p_t"] = wt[IN_PX:IN_PX + 512]
                out["w_xq_t"] = wt[IN_PX + 512:IN_GATES]
                out["w_gates_t"] = wt[IN_GATES:IN_END]
            else:
                out[n] = _cols_from_shards(g) if n in COL_SHARDED else g.reshape(4 * g.shape[1], g.shape[2])
        return out

    def gather(op, group, after):
        names = GATHER_GROUPS[group]
        if op == "start":
            place(names, ())
            inflight[group] = _gather_start([placed[n] for n in names], after, name="gather_" + group + "_start")
            behind = (inflight[group][3],)
            if group == "ffn1o":
                place(shard_of, behind)
            return behind
        if op == "pass":
            send, recv, bufs, _ = inflight[group]
            inflight[group] = _gather_pass(bufs, send, recv, after, name="gather_" + group + "_pass")
            return (inflight[group][2][0],)
        send, recv, bufs = inflight.pop(group)
        return relayout(names, _gather_finish(bufs, send, recv, after, name="gather_" + group + "_finish"))

    small = {n: w[n].reshape(1, D) for n in GAINS}
    small["b_f"] = w["b_f"].reshape(1, 512)
    small["pool_scale"] = w["pool_scale"].reshape(1, 512)
    small["w_pool_b"] = w["w_pool"].astype(BF16)

    pending, crossing, travelling = {}, {}, {}

    def emit(name, grad):
        pending[name] = grad
        group = next((g for g, names in REDUCE_GROUPS.items() if name == names[-1]), None)
        if group is None:
            return ()
        gb = {n: pending.pop(n) for n in REDUCE_GROUPS[group]}
        if group == "mix":
            dwt = jnp.concatenate([gb.pop("w_gla")[0:IN_PX], gb.pop("w_p"), gb.pop("w_xq"), gb.pop("w_gates")], axis=0)
            gb["w_in"] = dwt.reshape(4, IN_END // 4, D)
        names = list(gb)
        contrib = [gb[n] if n in COL_SHARDED else gb[n].reshape(4, gb[n].shape[0] // 4, gb[n].shape[1]) for n in names]
        if group == REDUCE_LAST:
            from_sibling = _pair_exchange(contrib, name="grads_" + group + "_pair_exchange")
            return over_chips(group, names, contrib, from_sibling)
        send, recv, contrib, lands, token = _pair_exchange_start(contrib, (), name="grads_" + group + "_pair_start")
        crossing[group] = (names, contrib, lands, send, recv)
        return (token,)

    def over_chips(group, names, contrib, from_sibling):
        pair = [_pair_sum(g, got, c_arr, name="grads_pair_sum_" + n) for n, g, got in zip(names, contrib, from_sibling)]
        send, recv, pair, lands, token = _chip_exchange_start(pair, (), name="grads_" + group + "_chip_start")
        travelling[group] = (names, send, recv, pair, lands)
        return (token,)

    def advance(after):
        behind = ()
        for group in list(crossing):
            names, contrib, lands, send, recv = crossing.pop(group)
            contrib, from_sibling = _pair_exchange_finish(contrib, lands, send, recv, after, name="grads_" + group + "_pair_finish")
            behind = over_chips(group, names, contrib, from_sibling)
        return behind

    loss, grad_x, gs = _local_step(x[0], mem[0], loss_target[0], small, gather, emit, advance)
    loss = lax.psum(loss[0, 0], ("x", "y", "c"))

    halves = {}
    for group, (names, send, recv, pair, lands) in travelling.items():
        pair, from_chips = _chip_exchange_finish(pair, lands, send, recv, (grad_x,), name="grads_" + group + "_chip_finish")
        for n, p, got in zip(names, pair, from_chips):
            halves[n] = _chip_sum(p, got, place_arr, name="grads_chip_sum_" + n)
    send, recv, joining, token = _pair_join_start([halves[n] for n in BIG], name="grads_pair_join_start")
    small_sums = _all_sum_small(gs, name="sum_small_grads", after=(token,))
    reduced = dict(zip(BIG, _pair_join_finish(joining, send, recv, (small_sums[0],), name="grads_pair_join_finish")))

    grads, delta, new_m, new_v = {}, {}, {}, {}
    for n in BIG:
        if n == "w_in":
            transposed = [jnp.transpose(args[k][0]) for k in (n, "m_" + n, "v_" + n)]
            updated = _adamw(transposed[0], reduced[n], transposed[1], transposed[2], name="adamw_" + n)
            grads[n] = jnp.transpose(reduced[n])[None]
            delta[n], new_m[n], new_v[n] = (jnp.transpose(a)[None] for a in updated)
            continue
        grads[n] = reduced[n][None]
        delta[n], new_m[n], new_v[n] = _adamw(args[n], reduced[n], args["m_" + n], args["v_" + n], name="adamw_" + n)
    small_params = {n: (args[n], args["m_" + n], args["v_" + n]) for n in SMALL}
    for n, (g, d, mn, vn) in _adamw_small(small_sums, small_params, chip_arr, name="adamw_small").items():
        grads[n], delta[n], new_m[n], new_v[n] = g, d, mn, vn

    outs = [loss, grad_x[None]]
    for group in (grads, delta, new_m, new_v):
        outs += [group[n] for n in WEIGHTS]
    return tuple(outs)
```

```python
import functools

import jax
import jax.numpy as jnp
from jax import lax
from jax.experimental import pallas as pl
from jax.experimental.pallas import tpu as pltpu

F32 = jnp.float32
BF16 = jnp.bfloat16
MESH = pl.DeviceIdType.MESH
HIGHEST = lax.Precision.HIGHEST

D = 1024
DFF = 2816
CHUNK = 64
HEADS = 4
HDK = 128
HDV = 256
GATE_TEMP = 16.0
POOL_WINDOWS = (2, 4, 8, 16)
POOL_HALO = 16
XA_HEADS = 4
XA_HD = 128
EPS = 1e-6
Q_SCALE = HDK ** -0.5
XA_SCALE = XA_HD ** -0.5
PG_Q, PG_K, PG_V, PG_G, PG_F, PG_W = 0, 512, 1024, 2048, 3072, 3200
GATE_RANK = 16
ADAM_LR, ADAM_B1, ADAM_B2, ADAM_EPS, ADAM_WD, ADAM_STEP = 0.001, 0.9, 0.999, 1e-08, 0.01, 10

VMEM_LIMIT = 48 * 1024 * 1024
LANE = 128
TS_ROW = 512
TS_GLA = 512
TS_POOL = 512
TS_XA = 512


def _params(sem):
    return pltpu.CompilerParams(dimension_semantics=sem, vmem_limit_bytes=VMEM_LIMIT)


def _tile(n, cap, unit=LANE):
    if n <= cap:
        return n
    best = None
    for t in range(unit, cap + 1, unit):
        if n % t == 0:
            best = t
    assert best is not None, (n, cap)
    return best


def _sigmoid(x):
    return 0.5 * jnp.tanh(0.5 * x) + 0.5


def _log_sigmoid(x):
    return jnp.minimum(x, 0.0) - jnp.log(1.0 + jnp.exp(-jnp.abs(x)))


def _rms(x):
    r = lax.rsqrt(jnp.mean(x * x, axis=-1, keepdims=True) + EPS)
    return x * r, r


def _rows(ts, w):
    return pl.BlockSpec((ts, w), lambda i: (i, 0))


def _fixed(shape):
    nd = len(shape)
    return pl.BlockSpec(shape, lambda i: (0,) * nd)


def _mm(a, b, *, ta=False, tb=False, out_dtype=F32, tm=2048, tn=1024, tk=1024, shards=1, after=(), name):
    a_blocked, b_blocked = a.ndim == 3, b.ndim == 3
    assert not (a_blocked and ta) and not (b_blocked and tb)
    if a_blocked:
        m, kdim, tk = a.shape[1], a.shape[0] * a.shape[2], a.shape[2]
    else:
        m, kdim = (a.shape[1], a.shape[0]) if ta else a.shape
    if b_blocked:
        n, tn = b.shape[0] * b.shape[2], b.shape[2]
        assert b.shape[1] == kdim and shards in (1, b.shape[0])
    else:
        n = b.shape[0] if tb else b.shape[1]
        assert (b.shape[1] if tb else b.shape[0]) == kdim, (a.shape, b.shape, ta, tb)
        tn = n // shards if shards > 1 else _tile(n, tn)
    tm = _tile(m, tm)
    tk = tk if a_blocked else _tile(kdim, tk)
    kgroup = 2 if (a_blocked and tb and a.shape[0] % 2 == 0) else 1
    nk = kdim // (tk * kgroup)
    dims = (((0 if ta else 1,), (1 if tb else 0,)), ((), ()))

    def body(a_ref, b_ref, *rest):
        o_ref, *acc = rest[len(after):]
        if kgroup == 1:
            part = lax.dot_general(a_ref[...], b_ref[...], dims, preferred_element_type=F32)
        else:
            part = sum(lax.dot_general(a_ref[g], b_ref[:, g * tk:(g + 1) * tk], dims, preferred_element_type=F32) for g in range(kgroup))
        if nk == 1:
            o_ref[...] = part.astype(o_ref.dtype)
            return
        acc_ref, = acc
        k = pl.program_id(2)

        @pl.when(k == 0)
        def _():
            acc_ref[...] = part

        @pl.when(k > 0)
        def _():
            acc_ref[...] += part

        @pl.when(k == nk - 1)
        def _():
            o_ref[...] = acc_ref[...].astype(o_ref.dtype)

    if a_blocked and kgroup > 1:
        a_spec = pl.BlockSpec((kgroup, tm, tk), lambda i, j, k: (k, i, 0))
    elif a_blocked:
        a_spec = pl.BlockSpec((None, tm, tk), lambda i, j, k: (k, i, 0))
    else:
        a_spec = pl.BlockSpec((tk, tm), lambda i, j, k: (k, i)) if ta else pl.BlockSpec((tm, tk), lambda i, j, k: (i, k))
    if b_blocked:
        b_spec = pl.BlockSpec((None, tk, tn), lambda i, j, k: (j, k, 0))
    else:
        b_spec = pl.BlockSpec((tn, tk * kgroup), lambda i, j, k: (j, k)) if tb else pl.BlockSpec((tk, tn), lambda i, j, k: (k, j))
    if shards > 1:
        out_shape = jax.ShapeDtypeStruct((shards, m, tn), out_dtype)
        o_spec = pl.BlockSpec((None, tm, tn), lambda i, j, k: (j, i, 0))
    else:
        out_shape = jax.ShapeDtypeStruct((m, n), out_dtype)
        o_spec = pl.BlockSpec((tm, tn), lambda i, j, k: (i, j))
    return pl.pallas_call(
        body, grid=(m // tm, n // tn, nk), in_specs=[a_spec, b_spec] + [ANY] * len(after), out_specs=o_spec, out_shape=out_shape,
        scratch_shapes=[pltpu.VMEM((tm, tn), F32)] if nk > 1 else [],
        compiler_params=_params(("parallel", "parallel", "arbitrary")), name=name,
    )(a, b, *after)


def _norm_fwd(x, g, out_dtype, name, after=()):
    s, d = x.shape
    ts = _tile(s, TS_ROW, 8)

    def body(x_ref, g_ref, *rest):
        o_ref = rest[len(after)]
        xh, _ = _rms(x_ref[...])
        o_ref[...] = (xh * g_ref[...]).astype(o_ref.dtype)

    return pl.pallas_call(
        body, grid=(s // ts,), in_specs=[_rows(ts, d), _fixed((1, d))] + [ANY] * len(after), out_specs=_rows(ts, d),
        out_shape=jax.ShapeDtypeStruct((s, d), out_dtype), compiler_params=_params(("parallel",)), name=name,
    )(x, g, *after)


def _resid_norm_fwd(x, f, g_post, alpha, g_next, name, after=()):
    s, d = x.shape
    ts = _tile(s, TS_ROW, 8)
    with_h = g_next is not None

    def body(x_ref, f_ref, gp_ref, *rest):
        rest = rest[:1] + rest[1 + len(after):] if with_h else rest[len(after):]
        fh, _ = _rms(f_ref[...])
        xn = x_ref[...] + alpha * (fh * gp_ref[...])
        if with_h:
            gn_ref, xo_ref, h_ref = rest
            xh, _ = _rms(xn)
            h_ref[...] = (xh * gn_ref[...]).astype(h_ref.dtype)
        else:
            xo_ref, = rest
        xo_ref[...] = xn

    ins = [x, f, g_post] + ([g_next] if with_h else []) + list(after)
    in_specs = [_rows(ts, d), _rows(ts, d), _fixed((1, d))] + ([_fixed((1, d))] if with_h else []) + [ANY] * len(after)
    out_shape = [jax.ShapeDtypeStruct((s, d), F32)] + ([jax.ShapeDtypeStruct((s, d), BF16)] if with_h else [])
    out_specs = [_rows(ts, d)] + ([_rows(ts, d)] if with_h else [])
    out = pl.pallas_call(
        body, grid=(s // ts,), in_specs=in_specs, out_specs=out_specs, out_shape=out_shape,
        compiler_params=_params(("parallel",)), name=name,
    )(*ins)
    return (out[0], out[1]) if with_h else (out[0], None)


def _mm_resid_norm(a, w, x, g_post, alpha, g_next, name, after=(), tm=512):
    s, kdim = a.shape
    d = w.shape[1]
    tm = _tile(s, tm)
    with_h = g_next is not None
    na = len(after)

    def body(a_ref, w_ref, x_ref, gp_ref, *rest):
        rest = rest[int(with_h) + na:] if not with_h else rest[:1] + rest[1 + na:]
        for rows in _sub_blocks(tm):
            f = jnp.dot(a_ref[rows, :], w_ref[...], preferred_element_type=F32)
            fh, _ = _rms(f)
            xn = x_ref[rows, :] + alpha * (fh * gp_ref[...])
            if with_h:
                gn_ref, f_ref, xo_ref, h_ref = rest
                xh, _ = _rms(xn)
                h_ref[rows, :] = (xh * gn_ref[...]).astype(h_ref.dtype)
            else:
                f_ref, xo_ref = rest
            f_ref[rows, :] = f
            xo_ref[rows, :] = xn

    ins = [a, w, x, g_post] + ([g_next] if with_h else []) + list(after)
    in_specs = [_rows(tm, kdim), _fixed((kdim, d)), _rows(tm, d), _fixed((1, d))] + ([_fixed((1, d))] if with_h else []) + [ANY] * na
    out_shape = [jax.ShapeDtypeStruct((s, d), F32)] * 2 + ([jax.ShapeDtypeStruct((s, d), BF16)] if with_h else [])
    out = pl.pallas_call(
        body, grid=(s // tm,), in_specs=in_specs, out_specs=[_rows(tm, d)] * len(out_shape), out_shape=out_shape,
        compiler_params=_params(("parallel",)), name=name,
    )(*ins)
    return (out[0], out[1], out[2]) if with_h else (out[0], out[1], None)


def _mm_rms_bwd(pairs, x, g, dres, name, after=(), post=None, tm=512):
    s, d = x.shape
    tm = _tile(s, tm)
    n, na = len(pairs), len(after)

    def body(*refs):
        a_refs, w_refs = refs[0:2 * n:2], refs[1:2 * n:2]
        x_ref, g_ref, dres_ref = refs[2 * n:2 * n + 3]
        if post is not None:
            f_ref, gp_ref = refs[2 * n + 3:2 * n + 5]
            dx_ref, dg_ref, df_ref, dgp_ref = refs[2 * n + 5 + na:]
        else:
            dx_ref, dg_ref = refs[2 * n + 3 + na:]
        @pl.when(pl.program_id(0) == 0)
        def _():
            dg_ref[...] = jnp.zeros_like(dg_ref)
            if post is not None:
                dgp_ref[...] = jnp.zeros_like(dgp_ref)

        for rows in _sub_blocks(tm):
            dy = None
            for a_ref, w_ref in zip(a_refs, w_refs):
                if len(a_ref.shape) == 3:
                    tkb = a_ref.shape[2]
                    parts = [lax.dot_general(a_ref[q, rows, :], w_ref[:, q * tkb:(q + 1) * tkb], (((1,), (1,)), ((), ())),
                                             preferred_element_type=F32) for q in range(a_ref.shape[0])]
                else:
                    parts = [jnp.dot(a_ref[rows, :], w_ref[...], preferred_element_type=F32)]
                for part in parts:
                    dy = part if dy is None else dy + part
            xh, r = _rms(x_ref[rows, :])
            dg_ref[...] += jnp.sum(dy * xh, axis=0, keepdims=True)
            dyg = dy * g_ref[...]
            dx = r * (dyg - xh * jnp.mean(dyg * xh, axis=-1, keepdims=True)) + dres_ref[rows, :]
            dx_ref[rows, :] = dx
            if post is not None:
                fh, rf = _rms(f_ref[rows, :])
                dz = dx * post[2]
                dgp_ref[...] += jnp.sum(dz * fh, axis=0, keepdims=True)
                dzg = dz * gp_ref[...]
                df_ref[rows, :] = (rf * (dzg - fh * jnp.mean(dzg * fh, axis=-1, keepdims=True))).astype(df_ref.dtype)

    ins, in_specs = [], []
    for a_arr, w_arr in pairs:
        ins += [a_arr, w_arr]
        if a_arr.ndim == 3:
            in_specs.append(pl.BlockSpec((a_arr.shape[0], tm, a_arr.shape[2]), lambda i: (0, i, 0)))
        else:
            in_specs.append(_rows(tm, a_arr.shape[1]))
        in_specs.append(pl.BlockSpec(w_arr.shape, lambda i: (0, 0), pipeline_mode=pl.Buffered(1)))
    with_post = post is not None
    return pl.pallas_call(
        body, grid=(s // tm,),
        in_specs=in_specs + [_rows(tm, d), _fixed((1, d)), _rows(tm, d)] + ([_rows(tm, d), _fixed((1, d))] if with_post else [])
        + [ANY] * na,
        out_specs=[_rows(tm, d), _fixed((1, d))] + ([_rows(tm, d), _fixed((1, d))] if with_post else []),
        out_shape=[jax.ShapeDtypeStruct((s, d), F32), jax.ShapeDtypeStruct((1, d), F32)]
        + ([jax.ShapeDtypeStruct((s, d), BF16), jax.ShapeDtypeStruct((1, d), F32)] if with_post else []),
        compiler_params=_params(("arbitrary",)), name=name,
    )(*ins, x, g, dres, *(post[:2] if with_post else ()), *after)


def _ffn_out_loss(u, w_out, x, g_post, alpha, g_final, target, name, tm=512):
    s, kdim = u.shape
    d = w_out.shape[1]
    tm = _tile(s, tm)

    def body(u_ref, w_ref, x_ref, gp_ref, gf_ref, t_ref, df_ref, dx_ref, dgp_ref, dgf_ref, loss_ref):
        @pl.when(pl.program_id(0) == 0)
        def _():
            dgp_ref[...] = jnp.zeros_like(dgp_ref)
            dgf_ref[...] = jnp.zeros_like(dgf_ref)
            loss_ref[...] = jnp.zeros_like(loss_ref)

        for rows in _sub_blocks(tm):
            f = jnp.dot(u_ref[rows, :], w_ref[...], preferred_element_type=F32)
            fh, rf = _rms(f)
            xn = x_ref[rows, :] + alpha * (fh * gp_ref[...])
            xh, rx = _rms(xn)
            gf = gf_ref[...]
            diff = xh * gf - t_ref[rows, :]
            sq = jnp.sum(diff * diff, axis=1, keepdims=True)
            loss_ref[...] += (0.5 / d) * jnp.sum(sq, axis=0, keepdims=True)
            dy = diff * (1.0 / d)
            dgf_ref[...] += jnp.sum(dy * xh, axis=0, keepdims=True)
            dyg = dy * gf
            dxn = rx * (dyg - xh * jnp.mean(dyg * xh, axis=-1, keepdims=True))
            dx_ref[rows, :] = dxn
            dz = dxn * alpha
            dgp_ref[...] += jnp.sum(dz * fh, axis=0, keepdims=True)
            dzg = dz * gp_ref[...]
            df_ref[rows, :] = (rf * (dzg - fh * jnp.mean(dzg * fh, axis=-1, keepdims=True))).astype(df_ref.dtype)

    return pl.pallas_call(
        body, grid=(s // tm,),
        in_specs=[_rows(tm, kdim), _resident(w_out.shape), _rows(tm, d), _fixed((1, d)), _fixed((1, d)), _rows(tm, d)],
        out_specs=[_rows(tm, d), _rows(tm, d), _fixed((1, d)), _fixed((1, d)), _fixed((8, LANE))],
        out_shape=[jax.ShapeDtypeStruct((s, d), BF16), jax.ShapeDtypeStruct((s, d), F32), jax.ShapeDtypeStruct((1, d), F32),
                   jax.ShapeDtypeStruct((1, d), F32), jax.ShapeDtypeStruct((8, LANE), F32)],
        compiler_params=_params(("arbitrary",)), name=name,
    )(u, w_out, x, g_post, g_final, target)


def _rms_bwd(x, g, dys, dres, alpha, out_dtype, name, after=()):
    s, d = x.shape
    ts = _tile(s, TS_ROW, 8)
    ndy = len(dys)
    with_res = dres is not None

    def body(x_ref, g_ref, *rest):
        dy_refs = rest[:ndy]
        rest = rest[ndy:]
        if with_res:
            dres_ref = rest[0]
        dx_ref, dg_ref = rest[int(with_res) + len(after):]
        xh, r = _rms(x_ref[...])
        dy = dy_refs[0][...].astype(F32)
        for ref in dy_refs[1:]:
            dy = dy + ref[...].astype(F32)
        dy = dy * alpha

        @pl.when(pl.program_id(0) == 0)
        def _():
            dg_ref[...] = jnp.zeros_like(dg_ref)

        dg_ref[...] += jnp.sum(dy * xh, axis=0, keepdims=True)
        dyg = dy * g_ref[...]
        dx = r * (dyg - xh * jnp.mean(dyg * xh, axis=-1, keepdims=True))
        if with_res:
            dx = dx + dres_ref[...]
        dx_ref[...] = dx.astype(dx_ref.dtype)

    ins = [x, g] + list(dys) + ([dres] if with_res else []) + list(after)
    in_specs = [_rows(ts, d), _fixed((1, d))] + [_rows(ts, d)] * (ndy + int(with_res)) + [ANY] * len(after)
    return pl.pallas_call(
        body, grid=(s // ts,), in_specs=in_specs, out_specs=[_rows(ts, d), _fixed((1, d))],
        out_shape=[jax.ShapeDtypeStruct((s, d), out_dtype), jax.ShapeDtypeStruct((1, d), F32)],
        compiler_params=_params(("arbitrary",)), name=name,
    )(*ins)


def _loss_bwd(x, g, target, name):
    s, d = x.shape
    ts = _tile(s, TS_ROW, 8)

    def body(x_ref, g_ref, t_ref, dx_ref, dg_ref, loss_ref):
        xh, r = _rms(x_ref[...])
        gv = g_ref[...]
        diff = xh * gv - t_ref[...]

        @pl.when(pl.program_id(0) == 0)
        def _():
            dg_ref[...] = jnp.zeros_like(dg_ref)
            loss_ref[...] = jnp.zeros_like(loss_ref)

        sq = jnp.sum(diff * diff, axis=1, keepdims=True)
        loss_ref[...] += (0.5 / d) * jnp.sum(sq, axis=0, keepdims=True)
        dy = diff * (1.0 / d)
        dg_ref[...] += jnp.sum(dy * xh, axis=0, keepdims=True)
        dyg = dy * gv
        dx_ref[...] = r * (dyg - xh * jnp.mean(dyg * xh, axis=-1, keepdims=True))

    return pl.pallas_call(
        body, grid=(s // ts,), in_specs=[_rows(ts, d), _fixed((1, d)), _rows(ts, d)],
        out_specs=[_rows(ts, d), _fixed((1, d)), _fixed((8, LANE))],
        out_shape=[jax.ShapeDtypeStruct((s, d), F32), jax.ShapeDtypeStruct((1, d), F32), jax.ShapeDtypeStruct((8, LANE), F32)],
        compiler_params=_params(("arbitrary",)), name=name,
    )(x, g, target)


HALF_FF = DFF // 2


SUB_ROWS = 256


def _sub_blocks(tm):
    sub = SUB_ROWS if tm % SUB_ROWS == 0 else tm
    return [slice(r0, r0 + sub) for r0 in range(0, tm, sub)]


def _ffn_in_swiglu(x_norm, w_in, name, after=(), tm=1024):
    s, d = x_norm.shape
    tm = _tile(s, tm)

    def body(x_ref, wa_ref, wb_ref, *rest):
        ab_ref, u_ref = rest[len(after):]
        for rows in _sub_blocks(tm):
            xv = x_ref[rows, :]
            a = jnp.dot(xv, wa_ref[...], preferred_element_type=F32)
            b = jnp.dot(xv, wb_ref[...], preferred_element_type=F32)
            ab_ref[0, rows, :] = a.astype(ab_ref.dtype)
            ab_ref[1, rows, :] = b.astype(ab_ref.dtype)
            u_ref[rows, :] = (a * _sigmoid(a) * b).astype(u_ref.dtype)

    ab, u = pl.pallas_call(
        body, grid=(s // tm, 2),
        in_specs=[pl.BlockSpec((tm, d), lambda i, j: (i, 0)), pl.BlockSpec((d, HALF_FF), lambda i, j: (0, j)),
                  pl.BlockSpec((d, HALF_FF), lambda i, j: (0, 2 + j))] + [ANY] * len(after),
        out_specs=[pl.BlockSpec((2, None, tm, HALF_FF), lambda i, j: (0, j, i, 0)), pl.BlockSpec((tm, HALF_FF), lambda i, j: (i, j))],
        out_shape=[jax.ShapeDtypeStruct((2, 2, s, HALF_FF), BF16), jax.ShapeDtypeStruct((s, DFF), BF16)],
        compiler_params=_params(("parallel", "parallel")), name=name,
    )(x_norm, w_in, w_in, *after)
    return ab.reshape(4, s, HALF_FF), u


def _ffn_out_dx_swiglu(dz, w_out, ab, after, name, tm=1024):
    s, d = dz.shape
    tm = _tile(s, tm)

    def body(dz_ref, w_ref, ab_ref, *rest):
        dab_ref = rest[len(after)]
        for rows in _sub_blocks(tm):
            du = lax.dot_general(dz_ref[rows, :], w_ref[...], (((1,), (1,)), ((), ())), preferred_element_type=F32)
            a = ab_ref[0, rows, :].astype(F32)
            b = ab_ref[1, rows, :].astype(F32)
            sig = _sigmoid(a)
            dab_ref[0, rows, :] = (du * b * (sig * (1.0 + a * (1.0 - sig)))).astype(dab_ref.dtype)
            dab_ref[1, rows, :] = (du * a * sig).astype(dab_ref.dtype)

    halves = pl.BlockSpec((2, None, tm, HALF_FF), lambda i, j: (0, j, i, 0))
    dab = pl.pallas_call(
        body, grid=(s // tm, 2),
        in_specs=[pl.BlockSpec((tm, d), lambda i, j: (i, 0)), pl.BlockSpec((HALF_FF, d), lambda i, j: (j, 0)), halves] + [ANY] * len(after),
        out_specs=halves, out_shape=jax.ShapeDtypeStruct((2, 2, s, HALF_FF), BF16),
        compiler_params=_params(("parallel", "parallel")), name=name,
    )(dz, w_out, ab.reshape(2, 2, s, HALF_FF), *after)
    return dab.reshape(4, s, HALF_FF)


def _tri(strict):
    r = lax.broadcasted_iota(jnp.int32, (CHUNK, CHUNK), 0)
    c = lax.broadcasted_iota(jnp.int32, (CHUNK, CHUNK), 1)
    return (r > c).astype(F32) if strict else (r >= c).astype(F32)


def _gla_fwd(pg, wfu, b_f, gnorm, name):
    s = pg.shape[0]
    ts = _tile(s, TS_GLA, CHUNK)
    cpb = ts // CHUNK
    nc = s // CHUNK

    def body(pg_ref, wfu_ref, bf_ref, gn_ref, ya_ref, sp_ref, so_ref, o_ref, st_ref, la_ref, dec_ref, u_ref):
        @pl.when(pl.program_id(0) == 0)
        def _():
            st_ref[...] = jnp.zeros_like(st_ref)

        f = jnp.dot(pg_ref[:, PG_F:PG_W], wfu_ref[...], preferred_element_type=F32) + bf_ref[...]
        la_ref[...] = _log_sigmoid(f) * (1.0 / GATE_TEMP)
        tri = _tri(False)
        chunks = [slice(ci * CHUNK, (ci + 1) * CHUNK) for ci in range(cpb)]
        for ci, rows in enumerate(chunks):
            la = la_ref[rows, :]
            b = jnp.dot(tri, la, precision=HIGHEST, preferred_element_type=F32)
            bend = jnp.sum(la, axis=0, keepdims=True)
            e = jnp.exp(bend - b)
            dec_ref[ci:ci + 1, :] = jnp.exp(bend)
            for hd in range(HEADS):
                k = pg_ref[rows, PG_K + hd * HDK:PG_K + (hd + 1) * HDK]
                v = pg_ref[rows, PG_V + hd * HDV:PG_V + (hd + 1) * HDV]
                kt = (k.astype(F32) * e[:, hd * HDK:(hd + 1) * HDK]).astype(BF16)
                u_ref[ci, hd] = lax.dot_general(v, kt, (((0,), (0,)), ((), ())), preferred_element_type=F32)
        for ci in range(cpb):
            for hd in range(HEADS):
                prev = st_ref[hd]
                sp_ref[ci, hd] = prev
                st = prev * dec_ref[ci:ci + 1, hd * HDK:(hd + 1) * HDK] + u_ref[ci, hd]
                st_ref[hd] = st
                so_ref[ci, hd] = st.astype(so_ref.dtype)
        for ci, rows in enumerate(chunks):
            for hd in range(HEADS):
                vc = slice(hd * HDV, (hd + 1) * HDV)
                q = pg_ref[rows, PG_Q + hd * HDK:PG_Q + (hd + 1) * HDK]
                go = pg_ref[rows, PG_G + hd * HDV:PG_G + (hd + 1) * HDV].astype(F32)
                qs = (q.astype(F32) * Q_SCALE).astype(BF16)
                o = lax.dot_general(qs, so_ref[ci, hd], (((1,), (1,)), ((), ())), preferred_element_type=F32)
                o_ref[rows, vc] = o
                oh, _ = _rms(o)
                ya_ref[rows, vc] = (oh * gn_ref[:, vc] * (go * _sigmoid(go))).astype(ya_ref.dtype)

    return pl.pallas_call(
        body, grid=(s // ts,),
        in_specs=[_rows(ts, PG_W), _fixed((LANE, HEADS * HDK)), _fixed((1, HEADS * HDK)), _fixed((1, HEADS * HDV))],
        out_specs=[_rows(ts, HEADS * HDV), pl.BlockSpec((cpb, HEADS, HDV, HDK), lambda i: (i, 0, 0, 0)),
                   pl.BlockSpec((cpb, HEADS, HDV, HDK), lambda i: (i, 0, 0, 0)), _rows(ts, HEADS * HDV)],
        out_shape=[jax.ShapeDtypeStruct((s, HEADS * HDV), BF16), jax.ShapeDtypeStruct((nc, HEADS, HDV, HDK), F32),
                   jax.ShapeDtypeStruct((nc, HEADS, HDV, HDK), BF16), jax.ShapeDtypeStruct((s, HEADS * HDV), F32)],
        scratch_shapes=[pltpu.VMEM((HEADS, HDV, HDK), F32), pltpu.VMEM((ts, HEADS * HDK), F32),
                        pltpu.VMEM((max(cpb, 8), HEADS * HDK), F32), pltpu.VMEM((cpb, HEADS, HDV, HDK), F32)],
        compiler_params=_params(("arbitrary",)), name=name,
    )(pg, wfu, b_f, gnorm)


def _gla_bwd(pg, sp, so, o, dya, wfu, b_f, gnorm, name):
    s = pg.shape[0]
    ts = _tile(s, TS_GLA, CHUNK)
    cpb = ts // CHUNK
    nblk = s // ts

    def body(pg_ref, sp_ref, so_ref, o_ref, dya_ref, wfu_ref, bf_ref, gn_ref, dpg_ref, dwfu_ref, dbf_ref, dgn_ref,
             dst_ref, la_ref, sg_ref, df_ref, e_ref, ktf_ref, dec_ref, g_ref):
        @pl.when(pl.program_id(0) == 0)
        def _():
            dst_ref[...] = jnp.zeros_like(dst_ref)
            dwfu_ref[...] = jnp.zeros_like(dwfu_ref)
            dbf_ref[...] = jnp.zeros_like(dbf_ref)
            dgn_ref[...] = jnp.zeros_like(dgn_ref)

        flow = pg_ref[:, PG_F:PG_W]
        f = jnp.dot(flow, wfu_ref[...], preferred_element_type=F32) + bf_ref[...]
        la_ref[...] = _log_sigmoid(f) * (1.0 / GATE_TEMP)
        sg_ref[...] = _sigmoid(-f) * (1.0 / GATE_TEMP)
        tri = _tri(False)
        tri_strict = _tri(True)
        chunks = [slice(ci * CHUNK, (ci + 1) * CHUNK) for ci in range(cpb)]
        for ci, rows in enumerate(chunks):
            la = la_ref[rows, :]
            b = jnp.dot(tri, la, precision=HIGHEST, preferred_element_type=F32)
            bend = jnp.sum(la, axis=0, keepdims=True)
            e = jnp.exp(bend - b)
            e_ref[rows, :] = e
            dec = jnp.exp(bend)
            dec_ref[ci:ci + 1, :] = dec
            for hd in range(HEADS):
                kc = slice(hd * HDK, (hd + 1) * HDK)
                vc = slice(hd * HDV, (hd + 1) * HDV)
                q = pg_ref[rows, PG_Q + hd * HDK:PG_Q + (hd + 1) * HDK]
                k = pg_ref[rows, PG_K + hd * HDK:PG_K + (hd + 1) * HDK]
                go = pg_ref[rows, PG_G + hd * HDV:PG_G + (hd + 1) * HDV].astype(F32)
                ktf_ref[rows, kc] = k.astype(F32) * e[:, kc]
                st_b = so_ref[ci, hd]
                qs = (q.astype(F32) * Q_SCALE).astype(BF16)
                oh, r = _rms(o_ref[rows, vc])
                gh = gn_ref[:, vc]
                sig = _sigmoid(go)
                dy = dya_ref[rows, vc].astype(F32)
                don = dy * (go * sig)
                dgn_ref[:, vc] += jnp.sum(don * oh, axis=0, keepdims=True)
                dong = don * gh
                do = (r * (dong - oh * jnp.mean(dong * oh, axis=-1, keepdims=True))).astype(BF16)
                g_ref[ci, hd] = lax.dot_general(do, qs, (((0,), (0,)), ((), ())), preferred_element_type=F32)
                dq = jnp.dot(do, st_b, preferred_element_type=F32) * Q_SCALE
                dpg_ref[rows, PG_Q + hd * HDK:PG_Q + (hd + 1) * HDK] = dq.astype(dpg_ref.dtype)
                dgo = dy * (oh * gh) * (sig * (1.0 + go * (1.0 - sig)))
                dpg_ref[rows, PG_G + hd * HDV:PG_G + (hd + 1) * HDV] = dgo.astype(dpg_ref.dtype)
        for ci in reversed(range(cpb)):
            for hd in range(HEADS):
                dst = dst_ref[hd] + g_ref[ci, hd]
                g_ref[ci, hd] = dst
                dst_ref[hd] = dst * dec_ref[ci:ci + 1, hd * HDK:(hd + 1) * HDK]
        for ci, rows in enumerate(chunks):
            for hd in range(HEADS):
                kc = slice(hd * HDK, (hd + 1) * HDK)
                v = pg_ref[rows, PG_V + hd * HDV:PG_V + (hd + 1) * HDV]
                ktf = ktf_ref[rows, kc]
                dst = g_ref[ci, hd]
                dst_b = dst.astype(BF16)
                dkt = jnp.dot(v, dst_b, preferred_element_type=F32)
                dv = lax.dot_general(ktf.astype(BF16), dst_b, (((1,), (1,)), ((), ())), preferred_element_type=F32)
                dd = jnp.sum(dst * sp_ref[ci, hd], axis=0, keepdims=True)
                dla = jnp.dot(tri_strict, dkt * ktf, precision=HIGHEST, preferred_element_type=F32) + dd * dec_ref[ci:ci + 1, kc]
                df_ref[rows, kc] = dla * sg_ref[rows, kc]
                dpg_ref[rows, PG_K + hd * HDK:PG_K + (hd + 1) * HDK] = (dkt * e_ref[rows, kc]).astype(dpg_ref.dtype)
                dpg_ref[rows, PG_V + hd * HDV:PG_V + (hd + 1) * HDV] = dv.astype(dpg_ref.dtype)
        df = df_ref[...]
        df_b = df.astype(BF16)
        dpg_ref[:, PG_F:PG_W] = lax.dot_general(df_b, wfu_ref[...], (((1,), (1,)), ((), ())), preferred_element_type=F32).astype(dpg_ref.dtype)
        dwfu_ref[...] += lax.dot_general(flow, df_b, (((0,), (0,)), ((), ())), preferred_element_type=F32)
        dbf_ref[...] += jnp.sum(df, axis=0, keepdims=True)

    rev = lambda i: (nblk - 1 - i, 0)
    return pl.pallas_call(
        body, grid=(nblk,),
        in_specs=[pl.BlockSpec((ts, PG_W), rev), pl.BlockSpec((cpb, HEADS, HDV, HDK), lambda i: (nblk - 1 - i, 0, 0, 0)),
                  pl.BlockSpec((cpb, HEADS, HDV, HDK), lambda i: (nblk - 1 - i, 0, 0, 0)), pl.BlockSpec((ts, HEADS * HDV), rev),
                  pl.BlockSpec((ts, HEADS * HDV), rev), _fixed((LANE, HEADS * HDK)), _fixed((1, HEADS * HDK)), _fixed((1, HEADS * HDV))],
        out_specs=[pl.BlockSpec((ts, PG_W), rev), _fixed((LANE, HEADS * HDK)), _fixed((1, HEADS * HDK)), _fixed((1, HEADS * HDV))],
        out_shape=[jax.ShapeDtypeStruct((s, PG_W), BF16), jax.ShapeDtypeStruct((LANE, HEADS * HDK), F32),
                   jax.ShapeDtypeStruct((1, HEADS * HDK), F32), jax.ShapeDtypeStruct((1, HEADS * HDV), F32)],
        scratch_shapes=[pltpu.VMEM((HEADS, HDV, HDK), F32)] + [pltpu.VMEM((ts, HEADS * HDK), F32)] * 5
        + [pltpu.VMEM((max(cpb, 8), HEADS * HDK), F32), pltpu.VMEM((cpb, HEADS, HDV, HDK), F32)],
        compiler_params=_params(("arbitrary",)), name=name,
    )(pg, sp, so, o, dya, wfu, b_f, gnorm)


def _window_sums(ext, sign):
    n = ext.shape[0]
    sums = {1: ext}
    w = 1
    while w < POOL_WINDOWS[-1]:
        sums[2 * w] = sums[w] + pltpu.roll(sums[w], w if sign > 0 else n - w, 0)
        w *= 2
    return [sums[POOL_WINDOWS[g]][:, g * LANE:(g + 1) * LANE] for g in range(len(POOL_WINDOWS))]


def _pool_counts(row0, n):
    pos = (row0 + lax.broadcasted_iota(jnp.int32, (n, 1), 0) + 1).astype(F32)
    return [1.0 / jnp.minimum(pos, float(w)) for w in POOL_WINDOWS]


def _pool_fwd(ppx, w_pool, pool_scale, name):
    s = ppx.shape[0]
    ts = _tile(s, TS_POOL, POOL_HALO)
    hb = ts // POOL_HALO
    pw = len(POOL_WINDOWS) * LANE

    def body(p_ref, halo_ref, w_ref, sc_ref, y_ref, ext_ref):
        i = pl.program_id(0)
        p = p_ref[...].astype(F32)
        ext_ref[0:POOL_HALO, :] = jnp.where(i > 0, halo_ref[...].astype(F32), 0.0)
        ext_ref[POOL_HALO:, :] = p
        sums = _window_sums(ext_ref[...], +1)
        cnt = _pool_counts(i * ts, ts)
        for g in range(len(POOL_WINDOWS)):
            cols = slice(g * LANE, (g + 1) * LANE)
            mixed = sums[g][POOL_HALO:, :] * cnt[g] - p[:, cols]
            y = jnp.dot(mixed.astype(BF16), w_ref[g], preferred_element_type=F32)
            y_ref[:, cols] = (y * sc_ref[:, cols]).astype(y_ref.dtype)

    return pl.pallas_call(
        body, grid=(s // ts,),
        in_specs=[pl.BlockSpec((ts, pw), lambda i: (i, 0)), pl.BlockSpec((POOL_HALO, pw), lambda i: (jnp.maximum(i * hb - 1, 0), 0)),
                  _fixed((len(POOL_WINDOWS), LANE, LANE)), _fixed((1, pw))],
        out_specs=_rows(ts, pw), out_shape=jax.ShapeDtypeStruct((s, pw), BF16),
        scratch_shapes=[pltpu.VMEM((ts + POOL_HALO, pw), F32)],
        compiler_params=_params(("parallel",)), name=name,
    )(ppx, ppx, w_pool, pool_scale)


def _pool_bwd(dyb, ppx, w_pool, pool_scale, name):
    s = ppx.shape[0]
    ts = _tile(s, TS_POOL, POOL_HALO)
    hb = ts // POOL_HALO
    nblk = s // ts
    last_halo = s // POOL_HALO - 1
    ng = len(POOL_WINDOWS)
    pw = ng * LANE

    def body(p_ref, halo_ref, dy_ref, dyn_ref, w_ref, sc_ref, dp_ref, dw_ref, dsc_ref, ext_ref, dext_ref, dm_ref):
        i = pl.program_id(0)

        @pl.when(i == 0)
        def _():
            dw_ref[...] = jnp.zeros_like(dw_ref)
            dsc_ref[...] = jnp.zeros_like(dsc_ref)

        p = p_ref[...].astype(F32)
        ext_ref[0:POOL_HALO, :] = jnp.where(i > 0, halo_ref[...].astype(F32), 0.0)
        ext_ref[POOL_HALO:, :] = p
        sums = _window_sums(ext_ref[...], +1)
        cnt = _pool_counts(i * ts, ts + POOL_HALO)
        sc = sc_ref[...]
        dy = dy_ref[...].astype(F32)
        dyn = jnp.where(i < nblk - 1, dyn_ref[...].astype(F32), 0.0)
        for g in range(ng):
            cols = slice(g * LANE, (g + 1) * LANE)
            wg = w_ref[g]
            mixed = (sums[g][POOL_HALO:, :] * cnt[g][0:ts] - p[:, cols]).astype(BF16)
            ypre = jnp.dot(mixed, wg, preferred_element_type=F32)
            dsc_ref[:, cols] += jnp.sum(dy[:, cols] * ypre, axis=0, keepdims=True)
            dyp = (dy[:, cols] * sc[:, cols]).astype(BF16)
            dypn = (dyn[:, cols] * sc[:, cols]).astype(BF16)
            dw_ref[g] += lax.dot_general(mixed, dyp, (((0,), (0,)), ((), ())), preferred_element_type=F32)
            dm = lax.dot_general(dyp, wg, (((1,), (1,)), ((), ())), preferred_element_type=F32)
            dmn = lax.dot_general(dypn, wg, (((1,), (1,)), ((), ())), preferred_element_type=F32)
            dext_ref[0:ts, cols] = dm * cnt[g][0:ts]
            dext_ref[ts:, cols] = dmn * cnt[g][ts:]
            dm_ref[:, cols] = dm
        lead = _window_sums(dext_ref[...], -1)
        for g in range(ng):
            cols = slice(g * LANE, (g + 1) * LANE)
            dp_ref[:, cols] = (lead[g][0:ts, :] - dm_ref[:, cols]).astype(dp_ref.dtype)

    return pl.pallas_call(
        body, grid=(nblk,),
        in_specs=[pl.BlockSpec((ts, pw), lambda i: (i, 0)), pl.BlockSpec((POOL_HALO, pw), lambda i: (jnp.maximum(i * hb - 1, 0), 0)),
                  pl.BlockSpec((ts, pw), lambda i: (i, 0)), pl.BlockSpec((POOL_HALO, pw), lambda i: (jnp.minimum((i + 1) * hb, last_halo), 0)),
                  _fixed((ng, LANE, LANE)), _fixed((1, pw))],
        out_specs=[_rows(ts, pw), _fixed((ng, LANE, LANE)), _fixed((1, pw))],
        out_shape=[jax.ShapeDtypeStruct((s, pw), BF16), jax.ShapeDtypeStruct((ng, LANE, LANE), F32), jax.ShapeDtypeStruct((1, pw), F32)],
        scratch_shapes=[pltpu.VMEM((ts + POOL_HALO, pw), F32), pltpu.VMEM((ts + POOL_HALO, pw), F32), pltpu.VMEM((ts, pw), F32)],
        compiler_params=_params(("arbitrary",)), name=name,
    )(ppx, ppx, dyb, dyb, w_pool, pool_scale)


def _xattn_fwd(ppx, kv, name):
    s = ppx.shape[0]
    m = kv.shape[0]
    ts = _tile(s, TS_XA, 8)
    xw = XA_HEADS * XA_HD

    def body(q_ref, kv_ref, o_ref):
        for hd in range(XA_HEADS):
            cols = slice(hd * XA_HD, (hd + 1) * XA_HD)
            k = kv_ref[:, hd * XA_HD:(hd + 1) * XA_HD]
            v = kv_ref[:, xw + hd * XA_HD:xw + (hd + 1) * XA_HD]
            sc = lax.dot_general(q_ref[:, cols], k, (((1,), (1,)), ((), ())), preferred_element_type=F32) * XA_SCALE
            ex = jnp.exp(sc - jnp.max(sc, axis=-1, keepdims=True))
            pr = ex * (1.0 / jnp.sum(ex, axis=-1, keepdims=True))
            o_ref[:, cols] = jnp.dot(pr.astype(BF16), v, preferred_element_type=F32).astype(o_ref.dtype)

    return pl.pallas_call(
        body, grid=(s // ts,), in_specs=[pl.BlockSpec((ts, xw), lambda i: (i, 1)), _fixed((m, 2 * xw))],
        out_specs=_rows(ts, xw), out_shape=jax.ShapeDtypeStruct((s, xw), BF16),
        compiler_params=_params(("parallel",)), name=name,
    )(ppx, kv)


def _xattn_bwd(dxc, ppx, kv, name):
    s = ppx.shape[0]
    m = kv.shape[0]
    ts = _tile(s, TS_XA, 8)
    xw = XA_HEADS * XA_HD

    def body(do_ref, q_ref, kv_ref, dq_ref, dkv_ref):
        @pl.when(pl.program_id(0) == 0)
        def _():
            dkv_ref[...] = jnp.zeros_like(dkv_ref)

        for hd in range(XA_HEADS):
            cols = slice(hd * XA_HD, (hd + 1) * XA_HD)
            vcols = slice(xw + hd * XA_HD, xw + (hd + 1) * XA_HD)
            q = q_ref[:, cols]
            k = kv_ref[:, cols]
            v = kv_ref[:, vcols]
            do = do_ref[:, cols]
            sc = lax.dot_general(q, k, (((1,), (1,)), ((), ())), preferred_element_type=F32) * XA_SCALE
            ex = jnp.exp(sc - jnp.max(sc, axis=-1, keepdims=True))
            pr = ex * (1.0 / jnp.sum(ex, axis=-1, keepdims=True))
            dpr = lax.dot_general(do, v, (((1,), (1,)), ((), ())), preferred_element_type=F32)
            dsc = (pr * (dpr - jnp.sum(dpr * pr, axis=-1, keepdims=True)) * XA_SCALE).astype(BF16)
            dq_ref[:, cols] = jnp.dot(dsc, k, preferred_element_type=F32).astype(dq_ref.dtype)
            dkv_ref[:, cols] += lax.dot_general(dsc, q, (((0,), (0,)), ((), ())), preferred_element_type=F32)
            dkv_ref[:, vcols] += lax.dot_general(pr.astype(BF16), do, (((0,), (0,)), ((), ())), preferred_element_type=F32)

    return pl.pallas_call(
        body, grid=(s // ts,), in_specs=[_rows(ts, xw), pl.BlockSpec((ts, xw), lambda i: (i, 1)), _fixed((m, 2 * xw))],
        out_specs=[_rows(ts, xw), _fixed((m, 2 * xw))],
        out_shape=[jax.ShapeDtypeStruct((s, xw), BF16), jax.ShapeDtypeStruct((m, 2 * xw), F32)],
        compiler_params=_params(("arbitrary",)), name=name,
    )(dxc, ppx, kv)


def _merge_fwd(pgt, ya, yb, yc, name):
    s = pgt.shape[0]
    ts = _tile(s, TS_ROW, 8)

    def body(gt_ref, ya_ref, yb_ref, yc_ref, o_ref):
        acc = _sigmoid(gt_ref[:, 0:D].astype(F32)) * ya_ref[...].astype(F32)
        acc = acc + _sigmoid(gt_ref[:, D:2 * D].astype(F32)) * yb_ref[...].astype(F32)
        acc = acc + _sigmoid(gt_ref[:, 2 * D:3 * D].astype(F32)) * yc_ref[...].astype(F32)
        o_ref[...] = acc.astype(o_ref.dtype)

    return pl.pallas_call(
        body, grid=(s // ts,), in_specs=[_rows(ts, 3 * D)] + [_rows(ts, D)] * 3, out_specs=_rows(ts, D),
        out_shape=jax.ShapeDtypeStruct((s, D), BF16), compiler_params=_params(("parallel",)), name=name,
    )(pgt, ya, yb, yc)


def _merge_bwd(dmerged, pgt, ya, yb, yc, name):
    s = pgt.shape[0]
    ts = _tile(s, TS_ROW, 8)

    def body(dm_ref, gt_ref, ya_ref, yb_ref, yc_ref, dya_ref, dyb_ref, dyc_ref, dgt_ref):
        dm = dm_ref[...].astype(F32)
        for j, (y_ref, dy_ref) in enumerate(((ya_ref, dya_ref), (yb_ref, dyb_ref), (yc_ref, dyc_ref))):
            sig = _sigmoid(gt_ref[:, j * D:(j + 1) * D].astype(F32))
            dy_ref[...] = (dm * sig).astype(dy_ref.dtype)
            dgt_ref[:, j * D:(j + 1) * D] = (dm * y_ref[...].astype(F32) * sig * (1.0 - sig)).astype(dgt_ref.dtype)

    return pl.pallas_call(
        body, grid=(s // ts,), in_specs=[_rows(ts, D), _rows(ts, 3 * D)] + [_rows(ts, D)] * 3,
        out_specs=[_rows(ts, D)] * 3 + [_rows(ts, 3 * D)],
        out_shape=[jax.ShapeDtypeStruct((s, D), BF16)] * 3 + [jax.ShapeDtypeStruct((s, 3 * D), BF16)],
        compiler_params=_params(("parallel",)), name=name,
    )(dmerged, pgt, ya, yb, yc)


def _resident(shape):
    nd = len(shape)
    return pl.BlockSpec(shape, lambda i: (0,) * nd, pipeline_mode=pl.Buffered(1))


def _mix_in_fwd(h, w_ts, after, name, tm=512):
    s, d = h.shape
    tm = _tile(s, tm)
    n, na = len(w_ts), len(after)

    def body(h_ref, *refs):
        w_refs, o_refs = refs[:n], refs[n + na:]
        for rows in _sub_blocks(tm):
            hv = h_ref[rows, :]
            for w_ref, o_ref in zip(w_refs, o_refs):
                o_ref[rows, :] = lax.dot_general(hv, w_ref[...], (((1,), (1,)), ((), ())), preferred_element_type=F32).astype(o_ref.dtype)

    return pl.pallas_call(
        body, grid=(s // tm,), in_specs=[_rows(tm, d)] + [_resident(w.shape) for w in w_ts] + [ANY] * na,
        out_specs=[_rows(tm, w.shape[0]) for w in w_ts],
        out_shape=[jax.ShapeDtypeStruct((s, w.shape[0]), BF16) for w in w_ts],
        compiler_params=_params(("parallel",)), name=name,
    )(h, *w_ts, *after)


def _mix_tail_fwd(ya_in, yb_in, xc, pgt, w_ups, w_o, x, g_post, g_next, after, name, tm=512):
    s, d = x.shape
    tm = _tile(s, tm)
    na = len(after)
    branch_ins = (ya_in, yb_in, xc)

    def body(a_ref, b_ref, c_ref, gt_ref, wa_ref, wb_ref, wc_ref, wo_ref, x_ref, gp_ref, gn_ref, *rest):
        ya_ref, yb_ref, yc_ref, m_ref, y_ref, xo_ref, h_ref = rest[na:]
        for rows in _sub_blocks(tm):
            merged = None
            for j, (in_ref, w_ref, out_ref) in enumerate(((a_ref, wa_ref, ya_ref), (b_ref, wb_ref, yb_ref), (c_ref, wc_ref, yc_ref))):
                yj = jnp.dot(in_ref[rows, :], w_ref[...], preferred_element_type=F32)
                out_ref[rows, :] = yj.astype(out_ref.dtype)
                part = _sigmoid(gt_ref[rows, j * D:(j + 1) * D].astype(F32)) * yj
                merged = part if merged is None else merged + part
            merged_b = merged.astype(m_ref.dtype)
            m_ref[rows, :] = merged_b
            y = jnp.dot(merged_b, wo_ref[...], preferred_element_type=F32)
            y_ref[rows, :] = y
            yh, _ = _rms(y)
            xn = x_ref[rows, :] + yh * gp_ref[...]
            xo_ref[rows, :] = xn
            xh, _ = _rms(xn)
            h_ref[rows, :] = (xh * gn_ref[...]).astype(h_ref.dtype)

    bf = lambda: jax.ShapeDtypeStruct((s, d), BF16)
    f32 = lambda: jax.ShapeDtypeStruct((s, d), F32)
    return pl.pallas_call(
        body, grid=(s // tm,),
        in_specs=[_rows(tm, a.shape[1]) for a in branch_ins] + [_rows(tm, 3 * d)] + [_resident(w.shape) for w in w_ups]
        + [_resident(w_o.shape), _rows(tm, d), _fixed((1, d)), _fixed((1, d))] + [ANY] * na,
        out_specs=[_rows(tm, d)] * 7,
        out_shape=[bf(), bf(), bf(), bf(), f32(), f32(), bf()],
        compiler_params=_params(("parallel",)), name=name,
    )(*branch_ins, pgt, *w_ups, w_o, x, g_post, g_next, *after)


def _mix_tail_bwd(dy, pgt, ys, w_ups, w_o, after, name, tm=512):
    s, d = dy.shape
    tm = _tile(s, tm)
    na = len(after)
    widths = [w.shape[0] for w in w_ups]

    def body(dy_ref, gt_ref, ya_ref, yb_ref, yc_ref, wa_ref, wb_ref, wc_ref, wo_ref, *rest):
        dya_ref, dyb_ref, dyc_ref, dgt_ref, da_ref, db_ref, dc_ref = rest[na:]
        nt = (((1,), (1,)), ((), ()))
        for rows in _sub_blocks(tm):
            dm = lax.dot_general(dy_ref[rows, :], wo_ref[...], nt, preferred_element_type=F32)
            for j, (y_ref, dyj_ref, w_ref, din_ref) in enumerate(((ya_ref, dya_ref, wa_ref, da_ref), (yb_ref, dyb_ref, wb_ref, db_ref),
                                                                   (yc_ref, dyc_ref, wc_ref, dc_ref))):
                sig = _sigmoid(gt_ref[rows, j * D:(j + 1) * D].astype(F32))
                dyj = (dm * sig).astype(dyj_ref.dtype)
                dyj_ref[rows, :] = dyj
                dgt_ref[rows, j * D:(j + 1) * D] = (dm * y_ref[rows, :].astype(F32) * sig * (1.0 - sig)).astype(dgt_ref.dtype)
                din_ref[rows, :] = lax.dot_general(dyj, w_ref[...], nt, preferred_element_type=F32).astype(din_ref.dtype)

    bf = lambda w: jax.ShapeDtypeStruct((s, w), BF16)
    return pl.pallas_call(
        body, grid=(s // tm,),
        in_specs=[_rows(tm, d), _rows(tm, 3 * d)] + [_rows(tm, d)] * 3 + [_resident(w.shape) for w in w_ups] + [_resident(w_o.shape)]
        + [ANY] * na,
        out_specs=[_rows(tm, d)] * 3 + [_rows(tm, 3 * d)] + [_rows(tm, w) for w in widths],
        out_shape=[bf(d), bf(d), bf(d), bf(3 * d)] + [bf(w) for w in widths],
        compiler_params=_params(("parallel",)), name=name,
    )(dy, pgt, *ys, *w_ups, w_o, *after)


def _adam_math(w, g, m, v):
    mn = ADAM_B1 * m + (1.0 - ADAM_B1) * g
    vn = ADAM_B2 * v + (1.0 - ADAM_B2) * (g * g)
    m_hat = mn / (1.0 - ADAM_B1 ** ADAM_STEP)
    v_hat = vn / (1.0 - ADAM_B2 ** ADAM_STEP)
    return -ADAM_LR * (m_hat / (jnp.sqrt(v_hat) + ADAM_EPS) + ADAM_WD * w), mn, vn


def _adamw(w, g, m, v, name):
    r, c = w.shape[-2:]
    tr, tc = _block_of(r, c, cap=512 if r % 16 == 0 else 256)

    def spec(a):
        if a.ndim == 2:
            return pl.BlockSpec((tr, tc), lambda i, j: (i, j))
        return pl.BlockSpec((None, tr, tc), lambda i, j: (0, i, j))

    def body(w_ref, g_ref, m_ref, v_ref, d_ref, mo_ref, vo_ref):
        d_ref[...], mo_ref[...], vo_ref[...] = _adam_math(w_ref[...], g_ref[...], m_ref[...], v_ref[...])

    return pl.pallas_call(
        body, grid=(r // tr, c // tc), in_specs=[spec(a) for a in (w, g, m, v)], out_specs=[spec(w)] * 3,
        out_shape=[jax.ShapeDtypeStruct(w.shape, F32)] * 3, compiler_params=_params(("parallel", "parallel")), name=name,
    )(w, g, m, v)


ANY = pl.BlockSpec(memory_space=pl.ANY)


def _place():
    x, y, c = lax.axis_index("x"), lax.axis_index("y"), lax.axis_index("c")
    chips = [(1 - x, y), (x, 1 - y), (1 - x, 1 - y)]
    return x, y, c, chips


def _half(c, rows):
    h = rows // 2
    return pl.ds(pl.multiple_of(c * h, 8), h)


def _by_cols(rows):
    return rows % 32 != 0 and rows != 16


def _half_of(ref, lead, c):
    r, cols = ref.shape[-2:]
    if _by_cols(r):
        return ref.at[(*lead, slice(None), pl.ds(pl.multiple_of(c * (cols // 2), LANE), cols // 2))]
    return ref.at[(*lead, pl.ds(pl.multiple_of(c * (r // 2), 8), r // 2))]


def _half_shape(shape):
    r, cols = shape[-2:]
    return shape[:-2] + ((r, cols // 2) if _by_cols(r) else (r // 2, cols))


def _block_of(r, cols, cap=256):
    if r % 16 == 0:
        return _tile(r, cap, 16), cols
    return r, _tile(cols, cap)


def _place_shard(shard, chip_arr, out_dtype, name, after=()):
    _, r, cols = shard.shape
    tr, tc = _block_of(r, cols)

    def body(chip_ref, s_ref, *rest):
        o_ref = rest[len(after)]
        o_ref[...] = s_ref[...].astype(o_ref.dtype)

    return pl.pallas_call(
        body,
        grid_spec=pltpu.PrefetchScalarGridSpec(
            num_scalar_prefetch=1, grid=(r // tr, cols // tc),
            in_specs=[pl.BlockSpec((None, tr, tc), lambda i, j, chip_ref: (0, i, j))] + [ANY] * len(after),
            out_specs=pl.BlockSpec((None, tr, tc), lambda i, j, chip_ref: (chip_ref[0], i, j))),
        out_shape=jax.ShapeDtypeStruct((4, r, cols), out_dtype),
        compiler_params=_params(("parallel", "parallel")), name=name,
    )(chip_arr, shard, *after)


def _gather_shards(bufs, name):
    n = len(bufs)

    def body(*refs):
        outs = refs[n:2 * n]
        send_ici, recv_ici, send_d2d, recv_d2d = refs[2 * n:]
        x, y, c, chips = _place()
        me = 2 * x + y
        sibling = (x, y, 1 - c)

        def ici(w, p, chip_of_block, to):
            rows = _half(c, outs[w].shape[1])
            block = outs[w].at[chip_of_block, rows]
            return pltpu.make_async_remote_copy(
                src_ref=block, dst_ref=block, send_sem=send_ici.at[w, p], recv_sem=recv_ici.at[w, p], device_id=to, device_id_type=MESH)

        def d2d(w, p, chip_of_block, half_of):
            rows = _half(half_of, outs[w].shape[1])
            block = outs[w].at[chip_of_block, rows]
            return pltpu.make_async_remote_copy(
                src_ref=block, dst_ref=block, send_sem=send_d2d.at[w, p], recv_sem=recv_d2d.at[w, p], device_id=sibling, device_id_type=MESH)

        sends = [ici(w, p, me, (*chip, c)) for p, chip in enumerate(chips) for w in range(n)]
        for cp in sends:
            cp.start()
        passed = []
        for p, (px, py) in enumerate(chips):
            for w in range(n):
                ici(w, p, 2 * px + py, (px, py, c)).wait_recv()
                fwd = d2d(w, p, 2 * px + py, c)
                fwd.start()
                passed.append(fwd)
        for p, (px, py) in enumerate(chips):
            for w in range(n):
                d2d(w, p, 2 * px + py, 1 - c).wait_recv()
        for cp in sends + passed:
            cp.wait_send()

    return pl.pallas_call(
        body, in_specs=[ANY] * n, out_specs=[ANY] * n,
        out_shape=[jax.ShapeDtypeStruct(a.shape, a.dtype) for a in bufs],
        input_output_aliases={w: w for w in range(n)},
        scratch_shapes=[pltpu.SemaphoreType.DMA((n, 3))] * 4,
        compiler_params=pltpu.CompilerParams(has_side_effects=True), name=name,
    )(*bufs)


HBM = pl.BlockSpec(memory_space=pltpu.HBM)
SEM = pl.BlockSpec(memory_space=pltpu.SEMAPHORE)
EFFECT = pltpu.SideEffectType.DATAFLOW_SIDE_EFFECTING


def _in_hbm(arrays):
    return [pltpu.with_memory_space_constraint(a, pltpu.HBM) for a in arrays]


def _gather_start(bufs, after, name):
    n, na = len(bufs), len(after)

    def body(*refs):
        send_sem, recv_sem = refs[n + na], refs[n + na + 1]
        outs = refs[n + na + 2:2 * n + na + 2]
        token = refs[2 * n + na + 2]
        x, y, c, chips = _place()
        me = 2 * x + y
        for p, chip in enumerate(chips):
            for w in range(n):
                block = _half_of(outs[w], (me,), c)
                pltpu.make_async_remote_copy(
                    src_ref=block, dst_ref=block, send_sem=send_sem, recv_sem=recv_sem,
                    device_id=(*chip, c), device_id_type=MESH).start()
        token[...] = jnp.zeros_like(token)

    out = pl.pallas_call(
        body, name=name, in_specs=[HBM] * n + [ANY] * na,
        out_specs=[SEM, SEM] + [HBM] * n + [pl.BlockSpec(memory_space=pltpu.VMEM)],
        out_shape=[pltpu.SemaphoreType.DMA(()), pltpu.SemaphoreType.DMA(())]
        + [pltpu.HBM(a.shape, a.dtype) for a in bufs] + [jax.ShapeDtypeStruct((8, LANE), F32)],
        input_output_aliases={w: w + 2 for w in range(n)},
        compiler_params=pltpu.CompilerParams(has_side_effects=EFFECT),
    )(*_in_hbm(bufs), *after)
    return out[0], out[1], list(out[2:2 + n]), out[2 + n]


def _gather_pass(bufs, send_sem, recv_sem, after, name):
    n, na = len(bufs), len(after)

    def body(*refs):
        send1, recv1 = refs[n], refs[n + 1]
        send2, recv2 = refs[n + 2 + na], refs[n + 3 + na]
        outs = refs[n + 4 + na:2 * n + 4 + na]
        x, y, c, chips = _place()
        me = 2 * x + y
        arrivals = [(w, px, py) for px, py in chips for w in range(n)]
        for w, px, py in arrivals:
            first = pltpu.make_async_remote_copy(
                src_ref=_half_of(outs[w], (me,), c), dst_ref=_half_of(outs[w], (2 * px + py,), c), send_sem=send1, recv_sem=recv1,
                device_id=(px, py, c), device_id_type=MESH)
            first.wait_send()
            first.wait_recv()
        for w, px, py in arrivals:
            arrived = _half_of(outs[w], (2 * px + py,), c)
            pltpu.make_async_remote_copy(
                src_ref=arrived, dst_ref=arrived, send_sem=send2, recv_sem=recv2,
                device_id=(x, y, 1 - c), device_id_type=MESH).start()

    out = pl.pallas_call(
        body, name=name, in_specs=[HBM] * n + [SEM, SEM] + [ANY] * na,
        out_specs=[SEM, SEM] + [HBM] * n,
        out_shape=[pltpu.SemaphoreType.DMA(()), pltpu.SemaphoreType.DMA(())] + [pltpu.HBM(a.shape, a.dtype) for a in bufs],
        input_output_aliases={w: w + 2 for w in range(n)},
        compiler_params=pltpu.CompilerParams(has_side_effects=EFFECT),
    )(*bufs, send_sem, recv_sem, *after)
    return out[0], out[1], list(out[2:])


def _gather_finish(bufs, send_sem, recv_sem, after, name):
    n, na = len(bufs), len(after)

    def body(*refs):
        send2, recv2 = refs[n], refs[n + 1]
        outs = refs[n + 2 + na:2 * n + 2 + na]
        x, y, c, chips = _place()
        for p, (px, py) in enumerate(chips):
            for w in range(n):
                passed = pltpu.make_async_remote_copy(
                    src_ref=_half_of(outs[w], (2 * px + py,), c), dst_ref=_half_of(outs[w], (2 * px + py,), 1 - c),
                    send_sem=send2, recv_sem=recv2, device_id=(x, y, 1 - c), device_id_type=MESH)
                passed.wait_send()
                passed.wait_recv()

    out = pl.pallas_call(
        body, name=name, in_specs=[HBM] * n + [SEM, SEM] + [ANY] * na, out_specs=[HBM] * n,
        out_shape=[pltpu.HBM(a.shape, a.dtype) for a in bufs],
        input_output_aliases={w: w for w in range(n)},
        compiler_params=pltpu.CompilerParams(has_side_effects=EFFECT),
    )(*bufs, send_sem, recv_sem, *after)
    return list(out)


def _pair_exchange(grads, name):
    n = len(grads)

    def body(*refs):
        ins, outs = refs[:n], refs[n:2 * n]
        send_sem, recv_sem = refs[2 * n:]
        x, y, c, _ = _place()
        copies = []
        for w in range(n):
            copies.append(pltpu.make_async_remote_copy(
                src_ref=_half_of(ins[w], (slice(None),), 1 - c), dst_ref=outs[w], send_sem=send_sem.at[w], recv_sem=recv_sem.at[w],
                device_id=(x, y, 1 - c), device_id_type=MESH))
        for cp in copies:
            cp.start()
        for cp in copies:
            cp.wait()

    return pl.pallas_call(
        body, in_specs=[ANY] * n, out_specs=[ANY] * n,
        out_shape=[jax.ShapeDtypeStruct(_half_shape(a.shape), a.dtype) for a in grads],
        scratch_shapes=[pltpu.SemaphoreType.DMA((n,))] * 2,
        compiler_params=pltpu.CompilerParams(has_side_effects=True), name=name,
    )(*grads)


def _pair_exchange_start(grads, after, name):
    n, na = len(grads), len(after)
    lands = [lax.empty(_half_shape(a.shape), a.dtype) for a in grads]

    def body(*refs):
        send_sem, recv_sem = refs[2 * n + na], refs[2 * n + na + 1]
        srcs = refs[2 * n + na + 2:3 * n + na + 2]
        dsts = refs[3 * n + na + 2:4 * n + na + 2]
        token = refs[4 * n + na + 2]
        x, y, c, _ = _place()
        for w in range(n):
            pltpu.make_async_remote_copy(
                src_ref=_half_of(srcs[w], (slice(None),), 1 - c), dst_ref=dsts[w], send_sem=send_sem, recv_sem=recv_sem,
                device_id=(x, y, 1 - c), device_id_type=MESH).start()
        token[...] = jnp.zeros_like(token)

    out = pl.pallas_call(
        body, name=name, in_specs=[HBM] * (2 * n) + [ANY] * na,
        out_specs=[SEM, SEM] + [HBM] * (2 * n) + [pl.BlockSpec(memory_space=pltpu.VMEM)],
        out_shape=[pltpu.SemaphoreType.DMA(()), pltpu.SemaphoreType.DMA(())]
        + [pltpu.HBM(a.shape, a.dtype) for a in grads + lands] + [jax.ShapeDtypeStruct((8, LANE), F32)],
        input_output_aliases={w: w + 2 for w in range(2 * n)},
        compiler_params=pltpu.CompilerParams(has_side_effects=EFFECT),
    )(*_in_hbm(grads), *_in_hbm(lands), *after)
    return out[0], out[1], list(out[2:2 + n]), list(out[2 + n:2 + 2 * n]), out[2 + 2 * n]


def _pair_exchange_finish(grads, lands, send_sem, recv_sem, after, name):
    n, na = len(grads), len(after)

    def body(*refs):
        send, recv = refs[2 * n], refs[2 * n + 1]
        srcs = refs[2 * n + 2 + na:3 * n + 2 + na]
        dsts = refs[3 * n + 2 + na:4 * n + 2 + na]
        x, y, c, _ = _place()
        for w in range(n):
            copy = pltpu.make_async_remote_copy(
                src_ref=_half_of(srcs[w], (slice(None),), 1 - c), dst_ref=dsts[w], send_sem=send, recv_sem=recv,
                device_id=(x, y, 1 - c), device_id_type=MESH)
            copy.wait_send()
            copy.wait_recv()

    out = pl.pallas_call(
        body, name=name, in_specs=[HBM] * (2 * n) + [SEM, SEM] + [ANY] * na, out_specs=[HBM] * (2 * n),
        out_shape=[pltpu.HBM(a.shape, a.dtype) for a in grads + lands],
        input_output_aliases={w: w for w in range(2 * n)},
        compiler_params=pltpu.CompilerParams(has_side_effects=EFFECT),
    )(*grads, *lands, send_sem, recv_sem, *after)
    return list(out[:n]), list(out[n:])


def _pair_sum(g, got, c_arr, name):
    _, r, cols = g.shape
    hr, hc = _half_shape((r, cols))
    tr, tc = _block_of(hr, hc)
    nbr, nbc = hr // tr, hc // tc
    by_cols = _by_cols(r)

    def body(c_ref, g_ref, got_ref, o_ref):
        o_ref[...] = (g_ref[...].astype(F32) + got_ref[...].astype(F32)).astype(o_ref.dtype)

    def mine(j, i, k, c_ref):
        return (j, i, c_ref[0] * nbc + k) if by_cols else (j, c_ref[0] * nbr + i, k)

    return pl.pallas_call(
        body,
        grid_spec=pltpu.PrefetchScalarGridSpec(
            num_scalar_prefetch=1, grid=(4, nbr, nbc),
            in_specs=[pl.BlockSpec((None, tr, tc), mine),
                      pl.BlockSpec((None, tr, tc), lambda j, i, k, c_ref: (j, i, k))],
            out_specs=pl.BlockSpec((None, tr, tc), lambda j, i, k, c_ref: (j, i, k))),
        out_shape=jax.ShapeDtypeStruct((4, hr, hc), BF16),
        compiler_params=_params(("parallel", "parallel", "parallel")), name=name,
    )(c_arr, *_in_hbm([g, got]))


def _chip_exchange(parts, name):
    n = len(parts)

    def body(*refs):
        ins, outs = refs[:n], refs[n:2 * n]
        send_sem, recv_sem = refs[2 * n:]
        x, y, c, chips = _place()
        copies = []
        for p, (px, py) in enumerate(chips):
            for w in range(n):
                copies.append(pltpu.make_async_remote_copy(
                    src_ref=ins[w].at[2 * px + py], dst_ref=outs[w].at[p], send_sem=send_sem.at[w, p], recv_sem=recv_sem.at[w, p],
                    device_id=(px, py, c), device_id_type=MESH))
        for cp in copies:
            cp.start()
        for cp in copies:
            cp.wait()

    return pl.pallas_call(
        body, in_specs=[ANY] * n, out_specs=[ANY] * n,
        out_shape=[jax.ShapeDtypeStruct((3,) + a.shape[1:], a.dtype) for a in parts],
        scratch_shapes=[pltpu.SemaphoreType.DMA((n, 3))] * 2,
        compiler_params=pltpu.CompilerParams(has_side_effects=True), name=name,
    )(*parts)


def _chip_exchange_start(parts, after, name):
    n, na = len(parts), len(after)
    lands = [lax.empty((3,) + a.shape[1:], a.dtype) for a in parts]

    def body(*refs):
        send_sem, recv_sem = refs[2 * n + na], refs[2 * n + na + 1]
        srcs = refs[2 * n + na + 2:3 * n + na + 2]
        dsts = refs[3 * n + na + 2:4 * n + na + 2]
        token = refs[4 * n + na + 2]
        x, y, c, chips = _place()
        for p, (px, py) in enumerate(chips):
            for w in range(n):
                pltpu.make_async_remote_copy(
                    src_ref=srcs[w].at[2 * px + py], dst_ref=dsts[w].at[p], send_sem=send_sem, recv_sem=recv_sem,
                    device_id=(px, py, c), device_id_type=MESH).start()
        token[...] = jnp.zeros_like(token)

    out = pl.pallas_call(
        body, name=name, in_specs=[HBM] * (2 * n) + [ANY] * na,
        out_specs=[SEM, SEM] + [HBM] * (2 * n) + [pl.BlockSpec(memory_space=pltpu.VMEM)],
        out_shape=[pltpu.SemaphoreType.DMA(()), pltpu.SemaphoreType.DMA(())]
        + [pltpu.HBM(a.shape, a.dtype) for a in parts + lands] + [jax.ShapeDtypeStruct((8, LANE), F32)],
        input_output_aliases={w: w + 2 for w in range(2 * n)},
        compiler_params=pltpu.CompilerParams(has_side_effects=EFFECT),
    )(*_in_hbm(parts), *_in_hbm(lands), *after)
    return out[0], out[1], list(out[2:2 + n]), list(out[2 + n:2 + 2 * n]), out[2 + 2 * n]


def _chip_exchange_finish(parts, lands, send_sem, recv_sem, after, name):
    n, na = len(parts), len(after)

    def body(*refs):
        send, recv = refs[2 * n], refs[2 * n + 1]
        srcs = refs[2 * n + 2 + na:3 * n + 2 + na]
        dsts = refs[3 * n + 2 + na:4 * n + 2 + na]
        x, y, c, chips = _place()
        for p, (px, py) in enumerate(chips):
            for w in range(n):
                copy = pltpu.make_async_remote_copy(
                    src_ref=srcs[w].at[2 * px + py], dst_ref=dsts[w].at[p], send_sem=send, recv_sem=recv,
                    device_id=(px, py, c), device_id_type=MESH)
                copy.wait_send()
                copy.wait_recv()

    out = pl.pallas_call(
        body, name=name, in_specs=[HBM] * (2 * n) + [SEM, SEM] + [ANY] * na, out_specs=[HBM] * (2 * n),
        out_shape=[pltpu.HBM(a.shape, a.dtype) for a in parts + lands],
        input_output_aliases={w: w for w in range(2 * n)},
        compiler_params=pltpu.CompilerParams(has_side_effects=EFFECT),
    )(*parts, *lands, send_sem, recv_sem, *after)
    return list(out[:n]), list(out[n:])


def _chip_sum(part, got, place_arr, name):
    _, hr, hc = part.shape
    by_cols = _by_cols(hr)
    tr, tc = _block_of(hr, hc)
    nbr, nbc = hr // tr, hc // tc

    def body(place_ref, p_ref, got_ref, o_ref):
        acc = p_ref[...].astype(F32)
        for p in range(3):
            acc = acc + got_ref[p].astype(F32)
        o_ref[...] = acc

    def mine(i, k, place_ref):
        return (i, place_ref[1] * nbc + k) if by_cols else (place_ref[1] * nbr + i, k)

    return pl.pallas_call(
        body,
        grid_spec=pltpu.PrefetchScalarGridSpec(
            num_scalar_prefetch=1, grid=(nbr, nbc),
            in_specs=[pl.BlockSpec((None, tr, tc), lambda i, k, place_ref: (place_ref[0], i, k)),
                      pl.BlockSpec((3, tr, tc), lambda i, k, place_ref: (0, i, k))],
            out_specs=pl.BlockSpec((tr, tc), mine)),
        out_shape=jax.ShapeDtypeStruct((hr, 2 * hc) if by_cols else (2 * hr, hc), F32),
        compiler_params=_params(("parallel", "parallel")), name=name,
    )(place_arr, *_in_hbm([part, got]))


def _pair_join_start(bufs, name):
    n = len(bufs)

    def body(*refs):
        send_sem, recv_sem = refs[n], refs[n + 1]
        outs = refs[n + 2:2 * n + 2]
        token = refs[2 * n + 2]
        x, y, c, _ = _place()
        for w in range(n):
            block = _half_of(outs[w], (), c)
            pltpu.make_async_remote_copy(
                src_ref=block, dst_ref=block, send_sem=send_sem, recv_sem=recv_sem,
                device_id=(x, y, 1 - c), device_id_type=MESH).start()
        token[...] = jnp.zeros_like(token)

    out = pl.pallas_call(
        body, name=name, in_specs=[HBM] * n,
        out_specs=[SEM, SEM] + [HBM] * n + [pl.BlockSpec(memory_space=pltpu.VMEM)],
        out_shape=[pltpu.SemaphoreType.DMA(()), pltpu.SemaphoreType.DMA(())]
        + [pltpu.HBM(a.shape, a.dtype) for a in bufs] + [jax.ShapeDtypeStruct((8, LANE), F32)],
        input_output_aliases={w: w + 2 for w in range(n)},
        compiler_params=pltpu.CompilerParams(has_side_effects=EFFECT),
    )(*_in_hbm(bufs))
    return out[0], out[1], list(out[2:2 + n]), out[2 + n]


def _pair_join_finish(bufs, send_sem, recv_sem, after, name):
    n, na = len(bufs), len(after)

    def body(*refs):
        send, recv = refs[n], refs[n + 1]
        outs = refs[n + 2 + na:2 * n + 2 + na]
        x, y, c, _ = _place()
        for w in range(n):
            copy = pltpu.make_async_remote_copy(
                src_ref=_half_of(outs[w], (), c), dst_ref=_half_of(outs[w], (), 1 - c), send_sem=send, recv_sem=recv,
                device_id=(x, y, 1 - c), device_id_type=MESH)
            copy.wait_send()
            copy.wait_recv()

    out = pl.pallas_call(
        body, name=name, in_specs=[HBM] * n + [SEM, SEM] + [ANY] * na, out_specs=[HBM] * n,
        out_shape=[pltpu.HBM(a.shape, a.dtype) for a in bufs],
        input_output_aliases={w: w for w in range(n)},
        compiler_params=pltpu.CompilerParams(has_side_effects=EFFECT),
    )(*bufs, send_sem, recv_sem, *after)
    return list(out)


SMALL = ("ffn1_pre_g", "ffn1_post_g", "mix_pre_g", "gla_norm_g", "mem_norm_g", "mix_post_g", "ffn2_pre_g", "ffn2_post_g", "final_g",
         "b_f", "pool_scale", "w_pool", "w_fu")
N_GAINS = 9
SMALL_PACKS = ((16, D), (24, 512), (4 * LANE, LANE))
W_FU_ROW = 8


LOSS_ROW = 2


def _all_sum_small(gs, loss, name, after=()):
    ins = [gs[n] for n in SMALL[:N_GAINS]] + [gs["b_f"], gs["pool_scale"], gs["w_fu_pad"], gs["w_pool"].reshape(4 * LANE, LANE), loss]

    def body(*refs):
        gain_refs = refs[:N_GAINS]
        bf_ref, ps_ref, wfu_ref, wp_ref, loss_ref = refs[N_GAINS:N_GAINS + 5]
        outs = refs[N_GAINS + 5 + len(after):N_GAINS + 8 + len(after)]
        mine_a, mine_b, all_a, all_b, all_c, send_sems, recv_sems = refs[N_GAINS + 8 + len(after):]
        mine_a[...] = jnp.zeros_like(mine_a)
        for i, ref in enumerate(gain_refs):
            mine_a[i:i + 1, :] = ref[...]
        mine_b[...] = jnp.zeros_like(mine_b)
        mine_b[0:1, :] = bf_ref[...]
        mine_b[1:2, :] = ps_ref[...]
        mine_b[LOSS_ROW:LOSS_ROW + 1, 0:LANE] = loss_ref[0:1, :]
        mine_b[W_FU_ROW:W_FU_ROW + GATE_RANK, :] = wfu_ref[0:GATE_RANK, :]
        packs = ((mine_a, all_a), (mine_b, all_b), (wp_ref, all_c))
        x, y, c, chips = _place()
        me, sibling = (x, y, c), (x, y, 1 - c)

        def copy(t, k, block, to, own=False):
            px, py, pc = block
            slot = packs[t][1].at[4 * px + 2 * py + pc]
            return pltpu.make_async_remote_copy(
                src_ref=packs[t][0] if own else slot, dst_ref=slot,
                send_sem=send_sems.at[t, k], recv_sem=recv_sems.at[t, k], device_id=to, device_id_type=MESH)

        started = []
        for t, (mine, everyone) in enumerate(packs):
            everyone[4 * x + 2 * y + c] = mine[...]
            started.append(copy(t, 0, me, sibling, own=True))
            started += [copy(t, 1 + j, me, (*chip, c), own=True) for j, chip in enumerate(chips)]
        for cp in started:
            cp.start()
        passed = []
        for j, chip in enumerate(chips):
            for t in range(len(packs)):
                copy(t, 1 + j, (*chip, c), me).wait_recv()
                fwd = copy(t, 4 + j, (*chip, c), sibling)
                fwd.start()
                passed.append(fwd)
        for t in range(len(packs)):
            copy(t, 0, sibling, me).wait_recv()
            for j, chip in enumerate(chips):
                copy(t, 4 + j, (*chip, 1 - c), me).wait_recv()
        for cp in started + passed:
            cp.wait_send()
        for (_, everyone), o_ref in zip(packs, outs):
            acc = everyone[0]
            for k in range(1, 8):
                acc = acc + everyone[k]
            o_ref[...] = acc

    vmem = pl.BlockSpec(memory_space=pltpu.VMEM)
    return pl.pallas_call(
        body, in_specs=[vmem] * len(ins) + [ANY] * len(after), out_specs=[vmem] * 3,
        out_shape=[jax.ShapeDtypeStruct(shape, F32) for shape in SMALL_PACKS],
        scratch_shapes=[pltpu.VMEM(SMALL_PACKS[0], F32), pltpu.VMEM(SMALL_PACKS[1], F32)]
        + [pltpu.VMEM((8,) + shape, F32) for shape in SMALL_PACKS]
        + [pltpu.SemaphoreType.DMA((3, 7)), pltpu.SemaphoreType.DMA((3, 7))],
        compiler_params=pltpu.CompilerParams(has_side_effects=True, vmem_limit_bytes=VMEM_LIMIT), name=name,
    )(*ins, *after)


def _adamw_small(sums, params, chip_arr, name):
    flat = [a for n in SMALL for a in params[n]]

    def body(chip_ref, a_ref, b_ref, c_ref, *refs):
        ins, outs = refs[:len(flat)], refs[len(flat):]
        for i, n in enumerate(SMALL):
            w_ref, m_ref, v_ref = ins[3 * i:3 * i + 3]
            g_ref, d_ref, mo_ref, vo_ref = outs[4 * i:4 * i + 4]
            if n == "w_pool":
                pieces = [((0, k), c_ref[k * LANE:(k + 1) * LANE, :]) for k in range(4)]
            elif n == "w_fu":
                mine = pl.ds(pl.multiple_of(chip_ref[0] * LANE, LANE), LANE)
                pieces = [((0,), b_ref[W_FU_ROW:W_FU_ROW + GATE_RANK, mine])]
            elif n == "b_f":
                pieces = [((), b_ref[0:1, :])]
            elif n == "pool_scale":
                pieces = [((), b_ref[1:2, :])]
            else:
                pieces = [((), a_ref[i:i + 1, :])]
            for at, g in pieces:
                d, mn, vn = _adam_math(w_ref[at], g, m_ref[at], v_ref[at])
                g_ref[at] = g
                d_ref[at] = d
                mo_ref[at] = mn
                vo_ref[at] = vn

    def whole(shape):
        return pl.BlockSpec(shape, lambda i, chip_ref: (0,) * len(shape))

    out = pl.pallas_call(
        body,
        grid_spec=pltpu.PrefetchScalarGridSpec(
            num_scalar_prefetch=1, grid=(1,),
            in_specs=[whole(a.shape) for a in list(sums) + flat],
            out_specs=[whole(params[n][0].shape) for n in SMALL for _ in range(4)]),
        out_shape=[jax.ShapeDtypeStruct(params[n][0].shape, F32) for n in SMALL for _ in range(4)],
        compiler_params=_params(("arbitrary",)), name=name,
    )(chip_arr, *sums, *flat)
    return {n: tuple(out[4 * i:4 * i + 4]) for i, n in enumerate(SMALL)}


def _ffn_bwd(dz, x_norm, ab, u, w_in, w_out, x, g_pre, dres, tag, emit, advance, after=(), post=None):
    dw_out = _mm(u, dz, ta=True, out_dtype=BF16, tm=1408, tk=2048, after=after, name=tag + "_out_dw")
    behind = emit(tag + "_w_out", dw_out)
    dab = _ffn_out_dx_swiglu(dz, w_out, ab, behind, name=tag + "_out_dx")
    behind = advance((dab,))
    dw_in = _mm(x_norm, dab, ta=True, out_dtype=BF16, tm=512, tk=4096, shards=4, after=behind, name=tag + "_in_dw")
    behind = emit(tag + "_w_in", dw_in)
    out = _mm_rms_bwd([(dab, w_in)], x, g_pre, dres, after=behind, post=post, name=tag + "_in_dx")
    return (*out, advance((out[0],)))


def _local_step(x, mem, target, small, gather, emit, advance):
    behind = gather("start", "ffn1i", ())
    behind = gather("start", "ffn1o", behind)
    h1 = _norm_fwd(x, small["ffn1_pre_g"], BF16, name="ffn1_pre", after=behind)
    gather("pass", "ffn1i", (h1,))
    big = gather("finish", "ffn1i", ())
    behind = gather("start", "mixa", (big["ffn1_w_in"],))
    behind = gather("start", "mixb", behind)
    behind = gather("start", "ffn2", behind)
    ab1, u1 = _ffn_in_swiglu(h1, big["ffn1_w_in"], name="ffn1_in", after=behind)
    gather("pass", "ffn1o", (ab1,))
    big.update(gather("finish", "ffn1o", ()))
    behind = gather("pass", "mixa", (u1,))
    f1, x1, h = _mm_resid_norm(u1, big["ffn1_w_out"], x, small["ffn1_post_g"], 0.5, small["mix_pre_g"], name="ffn1_out", after=behind)
    big.update(gather("finish", "mixa", (h,)))
    small = dict(small, w_fu_pad=big["w_fu_pad"])
    behind = gather("pass", "mixb", (h,))
    pg, ppx, pgt = _mix_in_fwd(h, [big["w_gla_t"], big["w_px_t"], big["w_gates_t"]], behind, name="mix_in")
    big.update(gather("finish", "mixb", (pgt,)))
    mem_n = _norm_fwd(mem, small["mem_norm_g"], BF16, name="mem_norm")
    kv = _mm(mem_n, big["w_mem_kv"], out_dtype=BF16, name="mem_kv")
    ya_in, sp, so, o_gla = _gla_fwd(pg, small["w_fu_pad"], small["b_f"], small["gla_norm_g"], name="gla_fwd")
    yb_in = _pool_fwd(ppx, small["w_pool_b"], small["pool_scale"], name="pool_fwd")
    xc = _xattn_fwd(ppx, kv, name="xattn_fwd")
    behind = gather("pass", "ffn2", (xc,))
    w_ups = [big["w_up_gla"], big["w_up_pool"], big["w_up_xattn"]]
    ya, yb, yc, merged, ymix, x2, h2 = _mix_tail_fwd(ya_in, yb_in, xc, pgt, w_ups, big["w_o"], x1, small["mix_post_g"],
                                                     small["ffn2_pre_g"], behind, name="mix_tail")
    big.update(gather("finish", "ffn2", (h2,)))
    ab2, u2 = _ffn_in_swiglu(h2, big["ffn2_w_in"], name="ffn2_in")
    gs = {}
    dz2, dx3, gs["ffn2_post_g"], gs["final_g"], loss = _ffn_out_loss(u2, big["ffn2_w_out"], x2, small["ffn2_post_g"], 0.5,
                                                                    small["final_g"], target, name="ffn2_out_loss")
    dx2, gs["ffn2_pre_g"], dy, gs["mix_post_g"], behind = _ffn_bwd(
        dz2, h2, ab2, u2, big["ffn2_w_in"], big["ffn2_w_out"], x2, small["ffn2_pre_g"], dx3, "ffn2", emit, advance,
        post=(ymix, small["mix_post_g"], 1.0))
    emit("w_o", _mm(merged, dy, ta=True, out_dtype=BF16, tm=512, tk=4096, after=behind, name="mix_out_dw"))
    dya, dyb, dyc, dgt, dya_in, dyb_in, dxc = _mix_tail_bwd(dy, pgt, (ya, yb, yc), w_ups, big["w_o"], (), name="mix_tail_bwd")
    emit("w_up_gla", _mm(ya_in, dya, ta=True, out_dtype=BF16, tm=512, tk=4096, name="up_gla_dw"))
    emit("w_up_pool", _mm(yb_in, dyb, ta=True, out_dtype=BF16, tm=512, tk=4096, shards=4, name="up_pool_dw"))
    emit("w_up_xattn", _mm(xc, dyc, ta=True, out_dtype=BF16, tm=512, tk=4096, shards=4, name="up_xattn_dw"))
    dpg, gs["w_fu_pad"], gs["b_f"], gs["gla_norm_g"] = _gla_bwd(pg, sp, so, o_gla, dya_in, small["w_fu_pad"], small["b_f"], small["gla_norm_g"], name="gla_bwd")
    dp, gs["w_pool"], gs["pool_scale"] = _pool_bwd(dyb_in, ppx, small["w_pool_b"], small["pool_scale"], name="pool_bwd")
    dxq, dkv = _xattn_bwd(dxc, ppx, kv, name="xattn_bwd")
    dkv = dkv.astype(BF16)
    emit("w_mem_kv", _mm(mem_n, dkv, ta=True, out_dtype=BF16, name="mem_kv_dw"))
    dmem_n = _mm(dkv, big["w_mem_kv"], tb=True, name="mem_kv_dx")
    _, gs["mem_norm_g"] = _rms_bwd(mem, small["mem_norm_g"], [dmem_n], None, 1.0, BF16, name="mem_norm_bwd")
    emit("w_gla", _mm(dpg, h, ta=True, out_dtype=BF16, tm=640, tk=4096, name="mix_in_gla_dw"))
    emit("w_p", _mm(dp, h, ta=True, out_dtype=BF16, tm=512, tk=4096, name="mix_in_p_dw"))
    emit("w_xq", _mm(dxq, h, ta=True, out_dtype=BF16, tm=512, tk=4096, name="mix_in_xq_dw"))
    behind = emit("w_gates", _mm(dgt, h, ta=True, out_dtype=BF16, tm=512, tk=4096, name="mix_in_gates_dw"))
    pairs = [(dpg, big["w_gla_t"]), (dp, big["w_p_t"]), (dxq, big["w_xq_t"]), (dgt, big["w_gates_t"])]
    dx1, gs["mix_pre_g"], dz1, gs["ffn1_post_g"] = _mm_rms_bwd(pairs, x1, small["mix_pre_g"], dx2, after=behind,
                                                               post=(f1, small["ffn1_post_g"], 0.5), tm=256, name="mix_in_dx")
    behind = advance((dx1,))
    dx0, gs["ffn1_pre_g"], _ = _ffn_bwd(dz1, h1, ab1, u1, big["ffn1_w_in"], big["ffn1_w_out"], x, small["ffn1_pre_g"], dx1,
                                        "ffn1", emit, advance, after=behind)
    return loss, dx0, gs


BIG = ("ffn1_w_in", "ffn1_w_out", "w_in", "w_mem_kv", "w_up_gla", "w_up_pool", "w_up_xattn", "w_o", "ffn2_w_in", "ffn2_w_out")
COL_SHARDED = ("ffn1_w_in", "w_in", "w_up_pool", "w_up_xattn", "ffn2_w_in")
GATHER_GROUPS = {"ffn1i": ("ffn1_w_in",), "ffn1o": ("ffn1_w_out",), "mixa": ("w_in", "w_fu"),
                 "mixb": ("w_mem_kv", "w_up_gla", "w_up_pool", "w_up_xattn", "w_o"), "ffn2": ("ffn2_w_in", "ffn2_w_out")}
REDUCE_GROUPS = {"ffn2": ("ffn2_w_out", "ffn2_w_in"),
                 "mix": ("w_o", "w_up_gla", "w_up_pool", "w_up_xattn", "w_mem_kv", "w_gla", "w_p", "w_xq", "w_gates"),
                 "ffn1_out": ("ffn1_w_out",),
                 "ffn1_in": ("ffn1_w_in",)}
REDUCE_LAST = "ffn1_in"
GAINS = ("ffn1_pre_g", "ffn1_post_g", "mix_pre_g", "gla_norm_g", "mem_norm_g", "mix_post_g", "ffn2_pre_g", "ffn2_post_g", "final_g")
WEIGHTS = ("ffn1_pre_g", "ffn1_w_in", "ffn1_w_out", "ffn1_post_g", "mix_pre_g", "w_in", "w_fu", "b_f", "gla_norm_g", "w_pool",
           "pool_scale", "mem_norm_g", "w_mem_kv", "w_up_gla", "w_up_pool", "w_up_xattn", "w_o", "mix_post_g", "ffn2_pre_g",
           "ffn2_w_in", "ffn2_w_out", "ffn2_post_g", "final_g")
IN_GLA, IN_F, IN_PX, IN_GATES, IN_END = 0, 3072, 3088, 4112, 7184
def _cols_from_shards(g):
    return jnp.transpose(g, (1, 0, 2)).reshape(g.shape[1], 4 * g.shape[2])


def kernel(x, mem, ffn1_pre_g, ffn1_w_in, ffn1_w_out, ffn1_post_g, mix_pre_g, w_in, w_fu, b_f, gla_norm_g, w_pool, pool_scale, mem_norm_g, w_mem_kv, w_up_gla, w_up_pool, w_up_xattn, w_o, mix_post_g, ffn2_pre_g, ffn2_w_in, ffn2_w_out, ffn2_post_g, final_g, loss_target, m_ffn1_pre_g, m_ffn1_w_in, m_ffn1_w_out, m_ffn1_post_g, m_mix_pre_g, m_w_in, m_w_fu, m_b_f, m_gla_norm_g, m_w_pool, m_pool_scale, m_mem_norm_g, m_w_mem_kv, m_w_up_gla, m_w_up_pool, m_w_up_xattn, m_w_o, m_mix_post_g, m_ffn2_pre_g, m_ffn2_w_in, m_ffn2_w_out, m_ffn2_post_g, m_final_g, v_ffn1_pre_g, v_ffn1_w_in, v_ffn1_w_out, v_ffn1_post_g, v_mix_pre_g, v_w_in, v_w_fu, v_b_f, v_gla_norm_g, v_w_pool, v_pool_scale, v_mem_norm_g, v_w_mem_kv, v_w_up_gla, v_w_up_pool, v_w_up_xattn, v_w_o, v_mix_post_g, v_ffn2_pre_g, v_ffn2_w_in, v_ffn2_w_out, v_ffn2_post_g, v_final_g):
    args = dict(locals())
    w = {n: args[n][0] for n in WEIGHTS}
    m = {n: args["m_" + n][0] for n in WEIGHTS}
    v = {n: args["v_" + n][0] for n in WEIGHTS}
    xi, yi, ci = lax.axis_index("x"), lax.axis_index("y"), lax.axis_index("c")
    chip = 2 * xi + yi

    c_arr = jnp.reshape(ci, (1,)).astype(jnp.int32)
    chip_arr = jnp.reshape(chip, (1,)).astype(jnp.int32)
    place_arr = jnp.stack([chip, ci]).astype(jnp.int32)
    w_in_t = []
    shard_of = {n: args[n] for n in BIG if n != "w_in"}
    shard_of["w_fu"] = args["w_fu"]
    placed, inflight = {}, {}

    def place(names, after):
        for n in names:
            if n not in placed:
                placed[n] = _place_shard(shard_of[n], chip_arr, F32 if n == "w_fu" else BF16, name="place_" + n, after=after)

    def relayout(names, gathered):
        out = {}
        for n, g in zip(names, gathered):
            if n == "w_fu":
                w_fu_full = _cols_from_shards(g)
                out["w_fu_pad"] = jnp.concatenate([w_fu_full, jnp.zeros((LANE - GATE_RANK, 512), F32)], axis=0).astype(BF16)
            elif n == "w_in":
                wt = g.reshape(IN_END, D)
                out["w_gla_t"] = jnp.concatenate([wt[IN_GLA:IN_PX], jnp.zeros((PG_W - IN_PX, D), BF16)], axis=0)
                out["w_px_t"] = wt[IN_PX:IN_GATES]
                out["w_p_t"] = wt[IN_PX:IN_PX + 512]
                out["w_xq_t"] = wt[IN_PX + 512:IN_GATES]
                out["w_gates_t"] = wt[IN_GATES:IN_END]
            else:
                out[n] = _cols_from_shards(g) if n in COL_SHARDED else g.reshape(4 * g.shape[1], g.shape[2])
        return out

    def gather(op, group, after):
        names = GATHER_GROUPS[group]
        if op == "start":
            place(names, ())
            inflight[group] = _gather_start([placed[n] for n in names], after, name="gather_" + group + "_start")
            behind = (inflight[group][3],)
            if group == "ffn1o":
                tied = lax.optimization_barrier((behind, tuple(args[k] for k in ("w_in", "m_w_in", "v_w_in"))))[1]
                w_in_t.extend(jnp.transpose(a[0]) for a in tied)
                shard_of["w_in"] = w_in_t[0][None]
                place(shard_of, behind)
            return behind
        if op == "pass":
            if group == "ffn1i":
                not_started = [n for g in GATHER_GROUPS if g not in inflight for n in GATHER_GROUPS[g]]
                after = tuple(after) + tuple(w_in_t[1:]) + tuple(placed[n] for n in not_started)
            send, recv, bufs, _ = inflight[group]
            inflight[group] = _gather_pass(bufs, send, recv, after, name="gather_" + group + "_pass")
            return (inflight[group][2][0],)
        send, recv, bufs = inflight.pop(group)
        return relayout(names, _gather_finish(bufs, send, recv, after, name="gather_" + group + "_finish"))

    small = {n: w[n].reshape(1, D) for n in GAINS}
    small["b_f"] = w["b_f"].reshape(1, 512)
    small["pool_scale"] = w["pool_scale"].reshape(1, 512)
    small["w_pool_b"] = w["w_pool"].astype(BF16)

    pending, crossing, travelling = {}, {}, {}

    def emit(name, grad):
        pending[name] = grad
        group = next((g for g, names in REDUCE_GROUPS.items() if name == names[-1]), None)
        if group is None:
            return ()
        gb = {n: pending.pop(n) for n in REDUCE_GROUPS[group]}
        if group == "mix":
            dwt = jnp.concatenate([gb.pop("w_gla")[0:IN_PX], gb.pop("w_p"), gb.pop("w_xq"), gb.pop("w_gates")], axis=0)
            gb["w_in"] = dwt.reshape(4, IN_END // 4, D)
        names = list(gb)
        contrib = [gb[n] if n in COL_SHARDED else gb[n].reshape(4, gb[n].shape[0] // 4, gb[n].shape[1]) for n in names]
        if group == REDUCE_LAST:
            from_sibling = _pair_exchange(contrib, name="grads_" + group + "_pair_exchange")
            return over_chips(group, names, contrib, from_sibling)
        send, recv, contrib, lands, token = _pair_exchange_start(contrib, (), name="grads_" + group + "_pair_start")
        crossing[group] = (names, contrib, lands, send, recv)
        return (token,)

    def over_chips(group, names, contrib, from_sibling):
        pair = [_pair_sum(g, got, c_arr, name="grads_pair_sum_" + n) for n, g, got in zip(names, contrib, from_sibling)]
        send, recv, pair, lands, token = _chip_exchange_start(pair, (), name="grads_" + group + "_chip_start")
        travelling[group] = (names, send, recv, pair, lands)
        return (token,)

    def advance(after):
        behind = ()
        for group in list(crossing):
            names, contrib, lands, send, recv = crossing.pop(group)
            contrib, from_sibling = _pair_exchange_finish(contrib, lands, send, recv, after, name="grads_" + group + "_pair_finish")
            behind = over_chips(group, names, contrib, from_sibling)
        return behind

    loss, grad_x, gs = _local_step(x[0], mem[0], loss_target[0], small, gather, emit, advance)

    halves = {}
    for group, (names, send, recv, pair, lands) in travelling.items():
        pair, from_chips = _chip_exchange_finish(pair, lands, send, recv, (grad_x,), name="grads_" + group + "_chip_finish")
        for n, p, got in zip(names, pair, from_chips):
            halves[n] = _chip_sum(p, got, place_arr, name="grads_chip_sum_" + n)
    send, recv, joining, token = _pair_join_start([halves[n] for n in BIG], name="grads_pair_join_start")
    small_sums = _all_sum_small(gs, loss, name="sum_small_grads", after=(token,))
    loss = small_sums[1][LOSS_ROW, 0]
    reduced = dict(zip(BIG, _pair_join_finish(joining, send, recv, (small_sums[0],), name="grads_pair_join_finish")))

    grads, delta, new_m, new_v = {}, {}, {}, {}
    for n in BIG:
        if n == "w_in":
            updated = _adamw(w_in_t[0], reduced[n], w_in_t[1], w_in_t[2], name="adamw_" + n)
            grads[n] = jnp.transpose(reduced[n])[None]
            delta[n], new_m[n], new_v[n] = (jnp.transpose(a)[None] for a in updated)
            continue
        grads[n] = reduced[n][None]
        delta[n], new_m[n], new_v[n] = _adamw(args[n], reduced[n], args["m_" + n], args["v_" + n], name="adamw_" + n)
    small_params = {n: (args[n], args["m_" + n], args["v_" + n]) for n in SMALL}
    for n, (g, d, mn, vn) in _adamw_small(small_sums, small_params, chip_arr, name="adamw_small").items():
        grads[n], delta[n], new_m[n], new_v[n] = g, d, mn, vn

    outs = [loss, grad_x[None]]
    for group in (grads, delta, new_m, new_v):
        outs += [group[n] for n in WEIGHTS]
    return tuple(outs)
```

```python
import jax
import jax.numpy as jnp
from jax import lax
from jax.experimental import pallas as pl
from jax.experimental.pallas import tpu as pltpu

F32 = jnp.float32
BF16 = jnp.bfloat16
MESH = pl.DeviceIdType.MESH
HIGHEST = lax.Precision.HIGHEST

D = 1024
DFF = 2816
CHUNK = 64
HEADS = 4
HDK = 128
HDV = 256
GATE_TEMP = 16.0
POOL_WINDOWS = (2, 4, 8, 16)
POOL_HALO = 16
XA_HEADS = 4
XA_HD = 128
EPS = 1e-6
Q_SCALE = HDK ** -0.5
XA_SCALE = XA_HD ** -0.5
PG_Q, PG_K, PG_V, PG_G, PG_F, PG_W = 0, 512, 1024, 2048, 3072, 3200
GATE_RANK = 16
ADAM_LR, ADAM_B1, ADAM_B2, ADAM_EPS, ADAM_WD, ADAM_STEP = 0.001, 0.9, 0.999, 1e-08, 0.01, 10

VMEM_LIMIT = 48 * 1024 * 1024
LANE = 128
TS_ROW = 512
TS_GLA = 512
TS_POOL = 512
TS_XA = 512


def _params(sem):
    return pltpu.CompilerParams(dimension_semantics=sem, vmem_limit_bytes=VMEM_LIMIT)


def _tile(n, cap, unit=LANE):
    if n <= cap:
        return n
    best = None
    for t in range(unit, cap + 1, unit):
        if n % t == 0:
            best = t
    assert best is not None, (n, cap)
    return best


def _sigmoid(x):
    return 0.5 * jnp.tanh(0.5 * x) + 0.5


def _log_sigmoid(x):
    return jnp.minimum(x, 0.0) - jnp.log(1.0 + jnp.exp(-jnp.abs(x)))


def _rms(x):
    r = lax.rsqrt(jnp.mean(x * x, axis=-1, keepdims=True) + EPS)
    return x * r, r


def _rows(ts, w):
    return pl.BlockSpec((ts, w), lambda i: (i, 0))


def _fixed(shape):
    nd = len(shape)
    return pl.BlockSpec(shape, lambda i: (0,) * nd)


def _mm(a, b, *, ta=False, tb=False, out_dtype=F32, tm=2048, tn=1024, tk=1024, shards=1, after=(), name):
    b_blocked = b.ndim == 3
    assert not (b_blocked and tb)
    m, kdim = (a.shape[1], a.shape[0]) if ta else a.shape
    if b_blocked:
        n, tn = b.shape[0] * b.shape[2], b.shape[2]
        assert b.shape[1] == kdim and shards in (1, b.shape[0])
    else:
        n = b.shape[0] if tb else b.shape[1]
        assert (b.shape[1] if tb else b.shape[0]) == kdim, (a.shape, b.shape, ta, tb)
        tn = n // shards if shards > 1 else _tile(n, tn)
    tm = _tile(m, tm)
    tk = _tile(kdim, tk)
    nk = kdim // tk
    dims = (((0 if ta else 1,), (1 if tb else 0,)), ((), ()))

    def body(a_ref, b_ref, *rest):
        o_ref, *acc = rest[len(after):]
        part = lax.dot_general(a_ref[...], b_ref[...], dims, preferred_element_type=F32)
        if nk == 1:
            o_ref[...] = part.astype(o_ref.dtype)
            return
        acc_ref, = acc
        k = pl.program_id(2)

        @pl.when(k == 0)
        def _():
            acc_ref[...] = part

        @pl.when(k > 0)
        def _():
            acc_ref[...] += part

        @pl.when(k == nk - 1)
        def _():
            o_ref[...] = acc_ref[...].astype(o_ref.dtype)

    a_spec = pl.BlockSpec((tk, tm), lambda i, j, k: (k, i)) if ta else pl.BlockSpec((tm, tk), lambda i, j, k: (i, k))
    if b_blocked:
        b_spec = pl.BlockSpec((None, tk, tn), lambda i, j, k: (j, k, 0))
    else:
        b_spec = pl.BlockSpec((tn, tk), lambda i, j, k: (j, k)) if tb else pl.BlockSpec((tk, tn), lambda i, j, k: (k, j))
    if shards > 1:
        out_shape = jax.ShapeDtypeStruct((shards, m, tn), out_dtype)
        o_spec = pl.BlockSpec((None, tm, tn), lambda i, j, k: (j, i, 0))
    else:
        out_shape = jax.ShapeDtypeStruct((m, n), out_dtype)
        o_spec = pl.BlockSpec((tm, tn), lambda i, j, k: (i, j))
    return pl.pallas_call(
        body, grid=(m // tm, n // tn, nk), in_specs=[a_spec, b_spec] + [ANY] * len(after), out_specs=o_spec, out_shape=out_shape,
        scratch_shapes=[pltpu.VMEM((tm, tn), F32)] if nk > 1 else [],
        compiler_params=_params(("parallel", "parallel", "arbitrary")), name=name,
    )(a, b, *after)


def _norm_fwd(x, g, out_dtype, name, after=()):
    s, d = x.shape
    ts = _tile(s, TS_ROW, 8)

    def body(x_ref, g_ref, *rest):
        o_ref = rest[len(after)]
        xh, _ = _rms(x_ref[...])
        o_ref[...] = (xh * g_ref[...]).astype(o_ref.dtype)

    return pl.pallas_call(
        body, grid=(s // ts,), in_specs=[_rows(ts, d), _fixed((1, d))] + [ANY] * len(after), out_specs=_rows(ts, d),
        out_shape=jax.ShapeDtypeStruct((s, d), out_dtype), compiler_params=_params(("parallel",)), name=name,
    )(x, g, *after)


def _mm_resid_norm(a, w, x, g_post, alpha, g_next, name, after=(), tm=512):
    s, kdim = a.shape
    d = w.shape[1]
    tm = _tile(s, tm)
    with_h = g_next is not None
    na = len(after)

    def body(a_ref, w_ref, x_ref, gp_ref, *rest):
        rest = rest[int(with_h) + na:] if not with_h else rest[:1] + rest[1 + na:]
        for rows in _sub_blocks(tm):
            f = jnp.dot(a_ref[rows, :], w_ref[...], preferred_element_type=F32)
            fh, _ = _rms(f)
            xn = x_ref[rows, :] + alpha * (fh * gp_ref[...])
            if with_h:
                gn_ref, f_ref, xo_ref, h_ref = rest
                xh, _ = _rms(xn)
                h_ref[rows, :] = (xh * gn_ref[...]).astype(h_ref.dtype)
            else:
                f_ref, xo_ref = rest
            f_ref[rows, :] = f
            xo_ref[rows, :] = xn

    ins = [a, w, x, g_post] + ([g_next] if with_h else []) + list(after)
    in_specs = [_rows(tm, kdim), _fixed((kdim, d)), _rows(tm, d), _fixed((1, d))] + ([_fixed((1, d))] if with_h else []) + [ANY] * na
    out_shape = [jax.ShapeDtypeStruct((s, d), F32)] * 2 + ([jax.ShapeDtypeStruct((s, d), BF16)] if with_h else [])
    out = pl.pallas_call(
        body, grid=(s // tm,), in_specs=in_specs, out_specs=[_rows(tm, d)] * len(out_shape), out_shape=out_shape,
        compiler_params=_params(("parallel",)), name=name,
    )(*ins)
    return (out[0], out[1], out[2]) if with_h else (out[0], out[1], None)


def _mm_rms_bwd(pairs, x, g, dres, name, after=(), post=None, tm=512):
    s, d = x.shape
    tm = _tile(s, tm)
    n, na = len(pairs), len(after)

    def body(*refs):
        a_refs, w_refs = refs[0:2 * n:2], refs[1:2 * n:2]
        x_ref, g_ref, dres_ref = refs[2 * n:2 * n + 3]
        if post is not None:
            f_ref, gp_ref = refs[2 * n + 3:2 * n + 5]
            dx_ref, dg_ref, df_ref, dgp_ref = refs[2 * n + 5 + na:]
        else:
            dx_ref, dg_ref = refs[2 * n + 3 + na:]
        @pl.when(pl.program_id(0) == 0)
        def _():
            dg_ref[...] = jnp.zeros_like(dg_ref)
            if post is not None:
                dgp_ref[...] = jnp.zeros_like(dgp_ref)

        for rows in _sub_blocks(tm):
            dy = None
            for a_ref, w_ref in zip(a_refs, w_refs):
                if len(a_ref.shape) == 3:
                    tkb = a_ref.shape[2]
                    parts = [lax.dot_general(a_ref[q, rows, :], w_ref[:, q * tkb:(q + 1) * tkb], (((1,), (1,)), ((), ())),
                                             preferred_element_type=F32) for q in range(a_ref.shape[0])]
                else:
                    parts = [jnp.dot(a_ref[rows, :], w_ref[...], preferred_element_type=F32)]
                for part in parts:
                    dy = part if dy is None else dy + part
            xh, r = _rms(x_ref[rows, :])
            dg_ref[...] += jnp.sum(dy * xh, axis=0, keepdims=True)
            dyg = dy * g_ref[...]
            dx = r * (dyg - xh * jnp.mean(dyg * xh, axis=-1, keepdims=True)) + dres_ref[rows, :]
            dx_ref[rows, :] = dx
            if post is not None:
                fh, rf = _rms(f_ref[rows, :])
                dz = dx * post[2]
                dgp_ref[...] += jnp.sum(dz * fh, axis=0, keepdims=True)
                dzg = dz * gp_ref[...]
                df_ref[rows, :] = (rf * (dzg - fh * jnp.mean(dzg * fh, axis=-1, keepdims=True))).astype(df_ref.dtype)

    ins, in_specs = [], []
    for a_arr, w_arr in pairs:
        ins += [a_arr, w_arr]
        if a_arr.ndim == 3:
            in_specs.append(pl.BlockSpec((a_arr.shape[0], tm, a_arr.shape[2]), lambda i: (0, i, 0)))
        else:
            in_specs.append(_rows(tm, a_arr.shape[1]))
        in_specs.append(pl.BlockSpec(w_arr.shape, lambda i: (0, 0), pipeline_mode=pl.Buffered(1)))
    with_post = post is not None
    return pl.pallas_call(
        body, grid=(s // tm,),
        in_specs=in_specs + [_rows(tm, d), _fixed((1, d)), _rows(tm, d)] + ([_rows(tm, d), _fixed((1, d))] if with_post else [])
        + [ANY] * na,
        out_specs=[_rows(tm, d), _fixed((1, d))] + ([_rows(tm, d), _fixed((1, d))] if with_post else []),
        out_shape=[jax.ShapeDtypeStruct((s, d), F32), jax.ShapeDtypeStruct((1, d), F32)]
        + ([jax.ShapeDtypeStruct((s, d), BF16), jax.ShapeDtypeStruct((1, d), F32)] if with_post else []),
        compiler_params=_params(("arbitrary",)), name=name,
    )(*ins, x, g, dres, *(post[:2] if with_post else ()), *after)


def _ffn_out_loss(u, w_out, x, g_post, alpha, g_final, target, name, tm=512):
    s, kdim = u.shape
    d = w_out.shape[1]
    tm = _tile(s, tm)

    def body(u_ref, w_ref, x_ref, gp_ref, gf_ref, t_ref, df_ref, dx_ref, dgp_ref, dgf_ref, loss_ref):
        @pl.when(pl.program_id(0) == 0)
        def _():
            dgp_ref[...] = jnp.zeros_like(dgp_ref)
            dgf_ref[...] = jnp.zeros_like(dgf_ref)
            loss_ref[...] = jnp.zeros_like(loss_ref)

        for rows in _sub_blocks(tm):
            f = jnp.dot(u_ref[rows, :], w_ref[...], preferred_element_type=F32)
            fh, rf = _rms(f)
            xn = x_ref[rows, :] + alpha * (fh * gp_ref[...])
            xh, rx = _rms(xn)
            gf = gf_ref[...]
            diff = xh * gf - t_ref[rows, :]
            sq = jnp.sum(diff * diff, axis=1, keepdims=True)
            loss_ref[...] += (0.5 / d) * jnp.sum(sq, axis=0, keepdims=True)
            dy = diff * (1.0 / d)
            dgf_ref[...] += jnp.sum(dy * xh, axis=0, keepdims=True)
            dyg = dy * gf
            dxn = rx * (dyg - xh * jnp.mean(dyg * xh, axis=-1, keepdims=True))
            dx_ref[rows, :] = dxn
            dz = dxn * alpha
            dgp_ref[...] += jnp.sum(dz * fh, axis=0, keepdims=True)
            dzg = dz * gp_ref[...]
            df_ref[rows, :] = (rf * (dzg - fh * jnp.mean(dzg * fh, axis=-1, keepdims=True))).astype(df_ref.dtype)

    return pl.pallas_call(
        body, grid=(s // tm,),
        in_specs=[_rows(tm, kdim), _resident(w_out.shape), _rows(tm, d), _fixed((1, d)), _fixed((1, d)), _rows(tm, d)],
        out_specs=[_rows(tm, d), _rows(tm, d), _fixed((1, d)), _fixed((1, d)), _fixed((8, LANE))],
        out_shape=[jax.ShapeDtypeStruct((s, d), BF16), jax.ShapeDtypeStruct((s, d), F32), jax.ShapeDtypeStruct((1, d), F32),
                   jax.ShapeDtypeStruct((1, d), F32), jax.ShapeDtypeStruct((8, LANE), F32)],
        compiler_params=_params(("arbitrary",)), name=name,
    )(u, w_out, x, g_post, g_final, target)


def _rms_bwd(x, g, dys, dres, alpha, out_dtype, name, after=()):
    s, d = x.shape
    ts = _tile(s, TS_ROW, 8)
    ndy = len(dys)
    with_res = dres is not None

    def body(x_ref, g_ref, *rest):
        dy_refs = rest[:ndy]
        rest = rest[ndy:]
        if with_res:
            dres_ref = rest[0]
        dx_ref, dg_ref = rest[int(with_res) + len(after):]
        xh, r = _rms(x_ref[...])
        dy = dy_refs[0][...].astype(F32)
        for ref in dy_refs[1:]:
            dy = dy + ref[...].astype(F32)
        dy = dy * alpha

        @pl.when(pl.program_id(0) == 0)
        def _():
            dg_ref[...] = jnp.zeros_like(dg_ref)

        dg_ref[...] += jnp.sum(dy * xh, axis=0, keepdims=True)
        dyg = dy * g_ref[...]
        dx = r * (dyg - xh * jnp.mean(dyg * xh, axis=-1, keepdims=True))
        if with_res:
            dx = dx + dres_ref[...]
        dx_ref[...] = dx.astype(dx_ref.dtype)

    ins = [x, g] + list(dys) + ([dres] if with_res else []) + list(after)
    in_specs = [_rows(ts, d), _fixed((1, d))] + [_rows(ts, d)] * (ndy + int(with_res)) + [ANY] * len(after)
    return pl.pallas_call(
        body, grid=(s // ts,), in_specs=in_specs, out_specs=[_rows(ts, d), _fixed((1, d))],
        out_shape=[jax.ShapeDtypeStruct((s, d), out_dtype), jax.ShapeDtypeStruct((1, d), F32)],
        compiler_params=_params(("arbitrary",)), name=name,
    )(*ins)


HALF_FF = DFF // 2


SUB_ROWS = 256


def _sub_blocks(tm):
    sub = SUB_ROWS if tm % SUB_ROWS == 0 else tm
    return [slice(r0, r0 + sub) for r0 in range(0, tm, sub)]


def _ffn_in_swiglu(x_norm, w_in, name, after=(), tm=1024):
    s, d = x_norm.shape
    tm = _tile(s, tm)

    def body(x_ref, wa_ref, wb_ref, *rest):
        ab_ref, u_ref = rest[len(after):]
        for rows in _sub_blocks(tm):
            xv = x_ref[rows, :]
            a = jnp.dot(xv, wa_ref[...], preferred_element_type=F32)
            b = jnp.dot(xv, wb_ref[...], preferred_element_type=F32)
            ab_ref[0, rows, :] = a.astype(ab_ref.dtype)
            ab_ref[1, rows, :] = b.astype(ab_ref.dtype)
            u_ref[rows, :] = (a * _sigmoid(a) * b).astype(u_ref.dtype)

    ab, u = pl.pallas_call(
        body, grid=(2, s // tm),
        in_specs=[pl.BlockSpec((tm, d), lambda j, i: (i, 0)), pl.BlockSpec((d, HALF_FF), lambda j, i: (0, j)),
                  pl.BlockSpec((d, HALF_FF), lambda j, i: (0, 2 + j))] + [ANY] * len(after),
        out_specs=[pl.BlockSpec((2, None, tm, HALF_FF), lambda j, i: (0, j, i, 0)), pl.BlockSpec((tm, HALF_FF), lambda j, i: (i, j))],
        out_shape=[jax.ShapeDtypeStruct((2, 2, s, HALF_FF), BF16), jax.ShapeDtypeStruct((s, DFF), BF16)],
        compiler_params=_params(("parallel", "parallel")), name=name,
    )(x_norm, w_in, w_in, *after)
    return ab.reshape(4, s, HALF_FF), u


def _ffn_out_dx_swiglu(dz, w_out, ab, after, name, tm=1024):
    s, d = dz.shape
    tm = _tile(s, tm)

    def body(dz_ref, w_ref, ab_ref, *rest):
        dab_ref = rest[len(after)]
        for rows in _sub_blocks(tm):
            du = lax.dot_general(dz_ref[rows, :], w_ref[...], (((1,), (1,)), ((), ())), preferred_element_type=F32)
            a = ab_ref[0, rows, :].astype(F32)
            b = ab_ref[1, rows, :].astype(F32)
            sig = _sigmoid(a)
            dab_ref[0, rows, :] = (du * b * (sig * (1.0 + a * (1.0 - sig)))).astype(dab_ref.dtype)
            dab_ref[1, rows, :] = (du * a * sig).astype(dab_ref.dtype)

    halves = pl.BlockSpec((2, None, tm, HALF_FF), lambda j, i: (0, j, i, 0))
    dab = pl.pallas_call(
        body, grid=(2, s // tm),
        in_specs=[pl.BlockSpec((tm, d), lambda j, i: (i, 0)), pl.BlockSpec((HALF_FF, d), lambda j, i: (j, 0)), halves] + [ANY] * len(after),
        out_specs=halves, out_shape=jax.ShapeDtypeStruct((2, 2, s, HALF_FF), BF16),
        compiler_params=_params(("parallel", "parallel")), name=name,
    )(dz, w_out, ab.reshape(2, 2, s, HALF_FF), *after)
    return dab.reshape(4, s, HALF_FF)


def _tri(strict):
    r = lax.broadcasted_iota(jnp.int32, (CHUNK, CHUNK), 0)
    c = lax.broadcasted_iota(jnp.int32, (CHUNK, CHUNK), 1)
    return (r > c).astype(F32) if strict else (r >= c).astype(F32)


def _gla_fwd(pg, wfu, b_f, gnorm, name):
    s = pg.shape[0]
    ts = _tile(s, TS_GLA, CHUNK)
    cpb = ts // CHUNK
    nc = s // CHUNK

    def body(pg_ref, wfu_ref, bf_ref, gn_ref, ya_ref, sp_ref, so_ref, o_ref, st_ref, la_ref, dec_ref, u_ref):
        @pl.when(pl.program_id(0) == 0)
        def _():
            st_ref[...] = jnp.zeros_like(st_ref)

        f = jnp.dot(pg_ref[:, PG_F:PG_W], wfu_ref[...], preferred_element_type=F32) + bf_ref[...]
        la_ref[...] = _log_sigmoid(f) * (1.0 / GATE_TEMP)
        tri = _tri(False)
        chunks = [slice(ci * CHUNK, (ci + 1) * CHUNK) for ci in range(cpb)]
        for ci, rows in enumerate(chunks):
            la = la_ref[rows, :]
            b = jnp.dot(tri, la, precision=HIGHEST, preferred_element_type=F32)
            bend = jnp.sum(la, axis=0, keepdims=True)
            e = jnp.exp(bend - b)
            dec_ref[ci:ci + 1, :] = jnp.exp(bend)
            for hd in range(HEADS):
                k = pg_ref[rows, PG_K + hd * HDK:PG_K + (hd + 1) * HDK]
                v = pg_ref[rows, PG_V + hd * HDV:PG_V + (hd + 1) * HDV]
                kt = (k.astype(F32) * e[:, hd * HDK:(hd + 1) * HDK]).astype(BF16)
                u_ref[ci, hd] = lax.dot_general(v, kt, (((0,), (0,)), ((), ())), preferred_element_type=F32)
        for ci in range(cpb):
            for hd in range(HEADS):
                prev = st_ref[hd]
                sp_ref[ci, hd] = prev
                st = prev * dec_ref[ci:ci + 1, hd * HDK:(hd + 1) * HDK] + u_ref[ci, hd]
                st_ref[hd] = st
                so_ref[ci, hd] = st.astype(so_ref.dtype)
        for ci, rows in enumerate(chunks):
            for hd in range(HEADS):
                vc = slice(hd * HDV, (hd + 1) * HDV)
                q = pg_ref[rows, PG_Q + hd * HDK:PG_Q + (hd + 1) * HDK]
                go = pg_ref[rows, PG_G + hd * HDV:PG_G + (hd + 1) * HDV].astype(F32)
                qs = (q.astype(F32) * Q_SCALE).astype(BF16)
                o = lax.dot_general(qs, so_ref[ci, hd], (((1,), (1,)), ((), ())), preferred_element_type=F32)
                o_ref[rows, vc] = o
                oh, _ = _rms(o)
                ya_ref[rows, vc] = (oh * gn_ref[:, vc] * (go * _sigmoid(go))).astype(ya_ref.dtype)

    return pl.pallas_call(
        body, grid=(s // ts,),
        in_specs=[_rows(ts, PG_W), _fixed((LANE, HEADS * HDK)), _fixed((1, HEADS * HDK)), _fixed((1, HEADS * HDV))],
        out_specs=[_rows(ts, HEADS * HDV), pl.BlockSpec((cpb, HEADS, HDV, HDK), lambda i: (i, 0, 0, 0)),
                   pl.BlockSpec((cpb, HEADS, HDV, HDK), lambda i: (i, 0, 0, 0)), _rows(ts, HEADS * HDV)],
        out_shape=[jax.ShapeDtypeStruct((s, HEADS * HDV), BF16), jax.ShapeDtypeStruct((nc, HEADS, HDV, HDK), F32),
                   jax.ShapeDtypeStruct((nc, HEADS, HDV, HDK), BF16), jax.ShapeDtypeStruct((s, HEADS * HDV), F32)],
        scratch_shapes=[pltpu.VMEM((HEADS, HDV, HDK), F32), pltpu.VMEM((ts, HEADS * HDK), F32),
                        pltpu.VMEM((max(cpb, 8), HEADS * HDK), F32), pltpu.VMEM((cpb, HEADS, HDV, HDK), F32)],
        compiler_params=_params(("arbitrary",)), name=name,
    )(pg, wfu, b_f, gnorm)


def _gla_bwd(pg, sp, so, o, dya, wfu, b_f, gnorm, name):
    s = pg.shape[0]
    ts = _tile(s, TS_GLA, CHUNK)
    cpb = ts // CHUNK
    nblk = s // ts

    def body(pg_ref, sp_ref, so_ref, o_ref, dya_ref, wfu_ref, bf_ref, gn_ref, dpg_ref, dwfu_ref, dbf_ref, dgn_ref,
             dst_ref, la_ref, sg_ref, df_ref, e_ref, ktf_ref, dec_ref, g_ref):
        @pl.when(pl.program_id(0) == 0)
        def _():
            dst_ref[...] = jnp.zeros_like(dst_ref)
            dwfu_ref[...] = jnp.zeros_like(dwfu_ref)
            dbf_ref[...] = jnp.zeros_like(dbf_ref)
            dgn_ref[...] = jnp.zeros_like(dgn_ref)

        flow = pg_ref[:, PG_F:PG_W]
        f = jnp.dot(flow, wfu_ref[...], preferred_element_type=F32) + bf_ref[...]
        la_ref[...] = _log_sigmoid(f) * (1.0 / GATE_TEMP)
        sg_ref[...] = _sigmoid(-f) * (1.0 / GATE_TEMP)
        tri = _tri(False)
        tri_strict = _tri(True)
        chunks = [slice(ci * CHUNK, (ci + 1) * CHUNK) for ci in range(cpb)]
        for ci, rows in enumerate(chunks):
            la = la_ref[rows, :]
            b = jnp.dot(tri, la, precision=HIGHEST, preferred_element_type=F32)
            bend = jnp.sum(la, axis=0, keepdims=True)
            e = jnp.exp(bend - b)
            e_ref[rows, :] = e
            dec = jnp.exp(bend)
            dec_ref[ci:ci + 1, :] = dec
            for hd in range(HEADS):
                kc = slice(hd * HDK, (hd + 1) * HDK)
                vc = slice(hd * HDV, (hd + 1) * HDV)
                q = pg_ref[rows, PG_Q + hd * HDK:PG_Q + (hd + 1) * HDK]
                k = pg_ref[rows, PG_K + hd * HDK:PG_K + (hd + 1) * HDK]
                go = pg_ref[rows, PG_G + hd * HDV:PG_G + (hd + 1) * HDV].astype(F32)
                ktf_ref[rows, kc] = k.astype(F32) * e[:, kc]
                st_b = so_ref[ci, hd]
                qs = (q.astype(F32) * Q_SCALE).astype(BF16)
                oh, r = _rms(o_ref[rows, vc])
                gh = gn_ref[:, vc]
                sig = _sigmoid(go)
                dy = dya_ref[rows, vc].astype(F32)
                don = dy * (go * sig)
                dgn_ref[:, vc] += jnp.sum(don * oh, axis=0, keepdims=True)
                dong = don * gh
                do = (r * (dong - oh * jnp.mean(dong * oh, axis=-1, keepdims=True))).astype(BF16)
                g_ref[ci, hd] = lax.dot_general(do, qs, (((0,), (0,)), ((), ())), preferred_element_type=F32)
                dq = jnp.dot(do, st_b, preferred_element_type=F32) * Q_SCALE
                dpg_ref[rows, PG_Q + hd * HDK:PG_Q + (hd + 1) * HDK] = dq.astype(dpg_ref.dtype)
                dgo = dy * (oh * gh) * (sig * (1.0 + go * (1.0 - sig)))
                dpg_ref[rows, PG_G + hd * HDV:PG_G + (hd + 1) * HDV] = dgo.astype(dpg_ref.dtype)
        for ci in reversed(range(cpb)):
            for hd in range(HEADS):
                dst = dst_ref[hd] + g_ref[ci, hd]
                g_ref[ci, hd] = dst
                dst_ref[hd] = dst * dec_ref[ci:ci + 1, hd * HDK:(hd + 1) * HDK]
        for ci, rows in enumerate(chunks):
            for hd in range(HEADS):
                kc = slice(hd * HDK, (hd + 1) * HDK)
                v = pg_ref[rows, PG_V + hd * HDV:PG_V + (hd + 1) * HDV]
                ktf = ktf_ref[rows, kc]
                dst = g_ref[ci, hd]
                dst_b = dst.astype(BF16)
                dkt = jnp.dot(v, dst_b, preferred_element_type=F32)
                dv = lax.dot_general(ktf.astype(BF16), dst_b, (((1,), (1,)), ((), ())), preferred_element_type=F32)
                dd = jnp.sum(dst * sp_ref[ci, hd], axis=0, keepdims=True)
                dla = jnp.dot(tri_strict, dkt * ktf, precision=HIGHEST, preferred_element_type=F32) + dd * dec_ref[ci:ci + 1, kc]
                df_ref[rows, kc] = dla * sg_ref[rows, kc]
                dpg_ref[rows, PG_K + hd * HDK:PG_K + (hd + 1) * HDK] = (dkt * e_ref[rows, kc]).astype(dpg_ref.dtype)
                dpg_ref[rows, PG_V + hd * HDV:PG_V + (hd + 1) * HDV] = dv.astype(dpg_ref.dtype)
        df = df_ref[...]
        df_b = df.astype(BF16)
        dpg_ref[:, PG_F:PG_W] = lax.dot_general(df_b, wfu_ref[...], (((1,), (1,)), ((), ())), preferred_element_type=F32).astype(dpg_ref.dtype)
        dwfu_ref[...] += lax.dot_general(flow, df_b, (((0,), (0,)), ((), ())), preferred_element_type=F32)
        dbf_ref[...] += jnp.sum(df, axis=0, keepdims=True)

    rev = lambda i: (nblk - 1 - i, 0)
    return pl.pallas_call(
        body, grid=(nblk,),
        in_specs=[pl.BlockSpec((ts, PG_W), rev), pl.BlockSpec((cpb, HEADS, HDV, HDK), lambda i: (nblk - 1 - i, 0, 0, 0)),
                  pl.BlockSpec((cpb, HEADS, HDV, HDK), lambda i: (nblk - 1 - i, 0, 0, 0)), pl.BlockSpec((ts, HEADS * HDV), rev),
                  pl.BlockSpec((ts, HEADS * HDV), rev), _fixed((LANE, HEADS * HDK)), _fixed((1, HEADS * HDK)), _fixed((1, HEADS * HDV))],
        out_specs=[pl.BlockSpec((ts, PG_W), rev), _fixed((LANE, HEADS * HDK)), _fixed((1, HEADS * HDK)), _fixed((1, HEADS * HDV))],
        out_shape=[jax.ShapeDtypeStruct((s, PG_W), BF16), jax.ShapeDtypeStruct((LANE, HEADS * HDK), F32),
                   jax.ShapeDtypeStruct((1, HEADS * HDK), F32), jax.ShapeDtypeStruct((1, HEADS * HDV), F32)],
        scratch_shapes=[pltpu.VMEM((HEADS, HDV, HDK), F32)] + [pltpu.VMEM((ts, HEADS * HDK), F32)] * 5
        + [pltpu.VMEM((max(cpb, 8), HEADS * HDK), F32), pltpu.VMEM((cpb, HEADS, HDV, HDK), F32)],
        compiler_params=_params(("arbitrary",)), name=name,
    )(pg, sp, so, o, dya, wfu, b_f, gnorm)


def _window_sums(ext, sign):
    n = ext.shape[0]
    sums = {1: ext}
    w = 1
    while w < POOL_WINDOWS[-1]:
        sums[2 * w] = sums[w] + pltpu.roll(sums[w], w if sign > 0 else n - w, 0)
        w *= 2
    return [sums[POOL_WINDOWS[g]][:, g * LANE:(g + 1) * LANE] for g in range(len(POOL_WINDOWS))]


def _pool_counts(row0, n):
    pos = (row0 + lax.broadcasted_iota(jnp.int32, (n, 1), 0) + 1).astype(F32)
    return [1.0 / jnp.minimum(pos, float(w)) for w in POOL_WINDOWS]


def _pool_fwd(ppx, w_pool, pool_scale, name):
    s = ppx.shape[0]
    ts = _tile(s, TS_POOL, POOL_HALO)
    hb = ts // POOL_HALO
    pw = len(POOL_WINDOWS) * LANE

    def body(p_ref, halo_ref, w_ref, sc_ref, y_ref, ext_ref):
        i = pl.program_id(0)
        p = p_ref[...].astype(F32)
        ext_ref[0:POOL_HALO, :] = jnp.where(i > 0, halo_ref[...].astype(F32), 0.0)
        ext_ref[POOL_HALO:, :] = p
        sums = _window_sums(ext_ref[...], +1)
        cnt = _pool_counts(i * ts, ts)
        for g in range(len(POOL_WINDOWS)):
            cols = slice(g * LANE, (g + 1) * LANE)
            mixed = sums[g][POOL_HALO:, :] * cnt[g] - p[:, cols]
            y = jnp.dot(mixed.astype(BF16), w_ref[g], preferred_element_type=F32)
            y_ref[:, cols] = (y * sc_ref[:, cols]).astype(y_ref.dtype)

    return pl.pallas_call(
        body, grid=(s // ts,),
        in_specs=[pl.BlockSpec((ts, pw), lambda i: (i, 0)), pl.BlockSpec((POOL_HALO, pw), lambda i: (jnp.maximum(i * hb - 1, 0), 0)),
                  _fixed((len(POOL_WINDOWS), LANE, LANE)), _fixed((1, pw))],
        out_specs=_rows(ts, pw), out_shape=jax.ShapeDtypeStruct((s, pw), BF16),
        scratch_shapes=[pltpu.VMEM((ts + POOL_HALO, pw), F32)],
        compiler_params=_params(("parallel",)), name=name,
    )(ppx, ppx, w_pool, pool_scale)


def _pool_bwd(dyb, ppx, w_pool, pool_scale, name):
    s = ppx.shape[0]
    ts = _tile(s, TS_POOL, POOL_HALO)
    hb = ts // POOL_HALO
    nblk = s // ts
    last_halo = s // POOL_HALO - 1
    ng = len(POOL_WINDOWS)
    pw = ng * LANE

    def body(p_ref, halo_ref, dy_ref, dyn_ref, w_ref, sc_ref, dp_ref, dw_ref, dsc_ref, ext_ref, dext_ref, dm_ref):
        i = pl.program_id(0)

        @pl.when(i == 0)
        def _():
            dw_ref[...] = jnp.zeros_like(dw_ref)
            dsc_ref[...] = jnp.zeros_like(dsc_ref)

        p = p_ref[...].astype(F32)
        ext_ref[0:POOL_HALO, :] = jnp.where(i > 0, halo_ref[...].astype(F32), 0.0)
        ext_ref[POOL_HALO:, :] = p
        sums = _window_sums(ext_ref[...], +1)
        cnt = _pool_counts(i * ts, ts + POOL_HALO)
        sc = sc_ref[...]
        dy = dy_ref[...].astype(F32)
        dyn = jnp.where(i < nblk - 1, dyn_ref[...].astype(F32), 0.0)
        for g in range(ng):
            cols = slice(g * LANE, (g + 1) * LANE)
            wg = w_ref[g]
            mixed = (sums[g][POOL_HALO:, :] * cnt[g][0:ts] - p[:, cols]).astype(BF16)
            ypre = jnp.dot(mixed, wg, preferred_element_type=F32)
            dsc_ref[:, cols] += jnp.sum(dy[:, cols] * ypre, axis=0, keepdims=True)
            dyp = (dy[:, cols] * sc[:, cols]).astype(BF16)
            dypn = (dyn[:, cols] * sc[:, cols]).astype(BF16)
            dw_ref[g] += lax.dot_general(mixed, dyp, (((0,), (0,)), ((), ())), preferred_element_type=F32)
            dm = lax.dot_general(dyp, wg, (((1,), (1,)), ((), ())), preferred_element_type=F32)
            dmn = lax.dot_general(dypn, wg, (((1,), (1,)), ((), ())), preferred_element_type=F32)
            dext_ref[0:ts, cols] = dm * cnt[g][0:ts]
            dext_ref[ts:, cols] = dmn * cnt[g][ts:]
            dm_ref[:, cols] = dm
        lead = _window_sums(dext_ref[...], -1)
        for g in range(ng):
            cols = slice(g * LANE, (g + 1) * LANE)
            dp_ref[:, cols] = (lead[g][0:ts, :] - dm_ref[:, cols]).astype(dp_ref.dtype)

    return pl.pallas_call(
        body, grid=(nblk,),
        in_specs=[pl.BlockSpec((ts, pw), lambda i: (i, 0)), pl.BlockSpec((POOL_HALO, pw), lambda i: (jnp.maximum(i * hb - 1, 0), 0)),
                  pl.BlockSpec((ts, pw), lambda i: (i, 0)), pl.BlockSpec((POOL_HALO, pw), lambda i: (jnp.minimum((i + 1) * hb, last_halo), 0)),
                  _fixed((ng, LANE, LANE)), _fixed((1, pw))],
        out_specs=[_rows(ts, pw), _fixed((ng, LANE, LANE)), _fixed((1, pw))],
        out_shape=[jax.ShapeDtypeStruct((s, pw), BF16), jax.ShapeDtypeStruct((ng, LANE, LANE), F32), jax.ShapeDtypeStruct((1, pw), F32)],
        scratch_shapes=[pltpu.VMEM((ts + POOL_HALO, pw), F32), pltpu.VMEM((ts + POOL_HALO, pw), F32), pltpu.VMEM((ts, pw), F32)],
        compiler_params=_params(("arbitrary",)), name=name,
    )(ppx, ppx, dyb, dyb, w_pool, pool_scale)


def _xattn_fwd(ppx, kv, name):
    s = ppx.shape[0]
    m = kv.shape[0]
    ts = _tile(s, TS_XA, 8)
    xw = XA_HEADS * XA_HD

    def body(q_ref, kv_ref, o_ref):
        for hd in range(XA_HEADS):
            cols = slice(hd * XA_HD, (hd + 1) * XA_HD)
            k = kv_ref[:, hd * XA_HD:(hd + 1) * XA_HD]
            v = kv_ref[:, xw + hd * XA_HD:xw + (hd + 1) * XA_HD]
            sc = lax.dot_general(q_ref[:, cols], k, (((1,), (1,)), ((), ())), preferred_element_type=F32) * XA_SCALE
            ex = jnp.exp(sc - jnp.max(sc, axis=-1, keepdims=True))
            pr = ex * (1.0 / jnp.sum(ex, axis=-1, keepdims=True))
            o_ref[:, cols] = jnp.dot(pr.astype(BF16), v, preferred_element_type=F32).astype(o_ref.dtype)

    return pl.pallas_call(
        body, grid=(s // ts,), in_specs=[pl.BlockSpec((ts, xw), lambda i: (i, 1)), _fixed((m, 2 * xw))],
        out_specs=_rows(ts, xw), out_shape=jax.ShapeDtypeStruct((s, xw), BF16),
        compiler_params=_params(("parallel",)), name=name,
    )(ppx, kv)


def _xattn_bwd(dxc, ppx, kv, name):
    s = ppx.shape[0]
    m = kv.shape[0]
    ts = _tile(s, TS_XA, 8)
    xw = XA_HEADS * XA_HD

    def body(do_ref, q_ref, kv_ref, dq_ref, dkv_ref):
        @pl.when(pl.program_id(0) == 0)
        def _():
            dkv_ref[...] = jnp.zeros_like(dkv_ref)

        for hd in range(XA_HEADS):
            cols = slice(hd * XA_HD, (hd + 1) * XA_HD)
            vcols = slice(xw + hd * XA_HD, xw + (hd + 1) * XA_HD)
            q = q_ref[:, cols]
            k = kv_ref[:, cols]
            v = kv_ref[:, vcols]
            do = do_ref[:, cols]
            sc = lax.dot_general(q, k, (((1,), (1,)), ((), ())), preferred_element_type=F32) * XA_SCALE
            ex = jnp.exp(sc - jnp.max(sc, axis=-1, keepdims=True))
            pr = ex * (1.0 / jnp.sum(ex, axis=-1, keepdims=True))
            dpr = lax.dot_general(do, v, (((1,), (1,)), ((), ())), preferred_element_type=F32)
            dsc = (pr * (dpr - jnp.sum(dpr * pr, axis=-1, keepdims=True)) * XA_SCALE).astype(BF16)
            dq_ref[:, cols] = jnp.dot(dsc, k, preferred_element_type=F32).astype(dq_ref.dtype)
            dkv_ref[:, cols] += lax.dot_general(dsc, q, (((0,), (0,)), ((), ())), preferred_element_type=F32)
            dkv_ref[:, vcols] += lax.dot_general(pr.astype(BF16), do, (((0,), (0,)), ((), ())), preferred_element_type=F32)

    return pl.pallas_call(
        body, grid=(s // ts,), in_specs=[_rows(ts, xw), pl.BlockSpec((ts, xw), lambda i: (i, 1)), _fixed((m, 2 * xw))],
        out_specs=[_rows(ts, xw), _fixed((m, 2 * xw))],
        out_shape=[jax.ShapeDtypeStruct((s, xw), BF16), jax.ShapeDtypeStruct((m, 2 * xw), F32)],
        compiler_params=_params(("arbitrary",)), name=name,
    )(dxc, ppx, kv)


def _resident(shape):
    nd = len(shape)
    return pl.BlockSpec(shape, lambda i: (0,) * nd, pipeline_mode=pl.Buffered(1))


def _mix_in_fwd(h, w_ts, after, name, tm=512):
    s, d = h.shape
    tm = _tile(s, tm)
    n, na = len(w_ts), len(after)

    def body(h_ref, *refs):
        w_refs, o_refs = refs[:n], refs[n + na:]
        for rows in _sub_blocks(tm):
            hv = h_ref[rows, :]
            for w_ref, o_ref in zip(w_refs, o_refs):
                o_ref[rows, :] = lax.dot_general(hv, w_ref[...], (((1,), (1,)), ((), ())), preferred_element_type=F32).astype(o_ref.dtype)

    return pl.pallas_call(
        body, grid=(s // tm,), in_specs=[_rows(tm, d)] + [_resident(w.shape) for w in w_ts] + [ANY] * na,
        out_specs=[_rows(tm, w.shape[0]) for w in w_ts],
        out_shape=[jax.ShapeDtypeStruct((s, w.shape[0]), BF16) for w in w_ts],
        compiler_params=_params(("parallel",)), name=name,
    )(h, *w_ts, *after)


def _mix_tail_fwd(ya_in, yb_in, xc, pgt, w_ups, w_o, x, g_post, g_next, after, name, tm=512):
    s, d = x.shape
    tm = _tile(s, tm)
    na = len(after)
    branch_ins = (ya_in, yb_in, xc)

    def body(a_ref, b_ref, c_ref, gt_ref, wa_ref, wb_ref, wc_ref, wo_ref, x_ref, gp_ref, gn_ref, *rest):
        ya_ref, yb_ref, yc_ref, m_ref, y_ref, xo_ref, h_ref = rest[na:]
        for rows in _sub_blocks(tm):
            merged = None
            for j, (in_ref, w_ref, out_ref) in enumerate(((a_ref, wa_ref, ya_ref), (b_ref, wb_ref, yb_ref), (c_ref, wc_ref, yc_ref))):
                yj = jnp.dot(in_ref[rows, :], w_ref[...], preferred_element_type=F32)
                out_ref[rows, :] = yj.astype(out_ref.dtype)
                part = _sigmoid(gt_ref[rows, j * D:(j + 1) * D].astype(F32)) * yj
                merged = part if merged is None else merged + part
            merged_b = merged.astype(m_ref.dtype)
            m_ref[rows, :] = merged_b
            y = jnp.dot(merged_b, wo_ref[...], preferred_element_type=F32)
            y_ref[rows, :] = y
            yh, _ = _rms(y)
            xn = x_ref[rows, :] + yh * gp_ref[...]
            xo_ref[rows, :] = xn
            xh, _ = _rms(xn)
            h_ref[rows, :] = (xh * gn_ref[...]).astype(h_ref.dtype)

    bf = lambda: jax.ShapeDtypeStruct((s, d), BF16)
    f32 = lambda: jax.ShapeDtypeStruct((s, d), F32)
    return pl.pallas_call(
        body, grid=(s // tm,),
        in_specs=[_rows(tm, a.shape[1]) for a in branch_ins] + [_rows(tm, 3 * d)] + [_resident(w.shape) for w in w_ups]
        + [_resident(w_o.shape), _rows(tm, d), _fixed((1, d)), _fixed((1, d))] + [ANY] * na,
        out_specs=[_rows(tm, d)] * 7,
        out_shape=[bf(), bf(), bf(), bf(), f32(), f32(), bf()],
        compiler_params=_params(("parallel",)), name=name,
    )(*branch_ins, pgt, *w_ups, w_o, x, g_post, g_next, *after)


def _mix_tail_bwd(dy, pgt, ys, w_ups, w_o, after, name, tm=512):
    s, d = dy.shape
    tm = _tile(s, tm)
    na = len(after)
    widths = [w.shape[0] for w in w_ups]

    def body(dy_ref, gt_ref, ya_ref, yb_ref, yc_ref, wa_ref, wb_ref, wc_ref, wo_ref, *rest):
        dya_ref, dyb_ref, dyc_ref, dgt_ref, da_ref, db_ref, dc_ref = rest[na:]
        nt = (((1,), (1,)), ((), ()))
        for rows in _sub_blocks(tm):
            dm = lax.dot_general(dy_ref[rows, :], wo_ref[...], nt, preferred_element_type=F32)
            for j, (y_ref, dyj_ref, w_ref, din_ref) in enumerate(((ya_ref, dya_ref, wa_ref, da_ref), (yb_ref, dyb_ref, wb_ref, db_ref),
                                                                   (yc_ref, dyc_ref, wc_ref, dc_ref))):
                sig = _sigmoid(gt_ref[rows, j * D:(j + 1) * D].astype(F32))
                dyj = (dm * sig).astype(dyj_ref.dtype)
                dyj_ref[rows, :] = dyj
                dgt_ref[rows, j * D:(j + 1) * D] = (dm * y_ref[rows, :].astype(F32) * sig * (1.0 - sig)).astype(dgt_ref.dtype)
                din_ref[rows, :] = lax.dot_general(dyj, w_ref[...], nt, preferred_element_type=F32).astype(din_ref.dtype)

    bf = lambda w: jax.ShapeDtypeStruct((s, w), BF16)
    return pl.pallas_call(
        body, grid=(s // tm,),
        in_specs=[_rows(tm, d), _rows(tm, 3 * d)] + [_rows(tm, d)] * 3 + [_resident(w.shape) for w in w_ups] + [_resident(w_o.shape)]
        + [ANY] * na,
        out_specs=[_rows(tm, d)] * 3 + [_rows(tm, 3 * d)] + [_rows(tm, w) for w in widths],
        out_shape=[bf(d), bf(d), bf(d), bf(3 * d)] + [bf(w) for w in widths],
        compiler_params=_params(("parallel",)), name=name,
    )(dy, pgt, *ys, *w_ups, w_o, *after)


def _adam_math(w, g, m, v):
    mn = ADAM_B1 * m + (1.0 - ADAM_B1) * g
    vn = ADAM_B2 * v + (1.0 - ADAM_B2) * (g * g)
    m_hat = mn / (1.0 - ADAM_B1 ** ADAM_STEP)
    v_hat = vn / (1.0 - ADAM_B2 ** ADAM_STEP)
    return -ADAM_LR * (m_hat / (jnp.sqrt(v_hat) + ADAM_EPS) + ADAM_WD * w), mn, vn


def _adamw(w, g, m, v, name):
    r, c = w.shape[-2:]
    tr, tc = _block_of(r, c, cap=512 if r % 16 == 0 else 256)

    def spec(a):
        if a.ndim == 2:
            return pl.BlockSpec((tr, tc), lambda i, j: (i, j))
        return pl.BlockSpec((None, tr, tc), lambda i, j: (0, i, j))

    def body(w_ref, g_ref, m_ref, v_ref, d_ref, mo_ref, vo_ref):
        d_ref[...], mo_ref[...], vo_ref[...] = _adam_math(w_ref[...], g_ref[...], m_ref[...], v_ref[...])

    return pl.pallas_call(
        body, grid=(r // tr, c // tc), in_specs=[spec(a) for a in (w, g, m, v)], out_specs=[spec(w)] * 3,
        out_shape=[jax.ShapeDtypeStruct(w.shape, F32)] * 3, compiler_params=_params(("parallel", "parallel")), name=name,
    )(w, g, m, v)


ANY = pl.BlockSpec(memory_space=pl.ANY)


def _place():
    x, y, c = lax.axis_index("x"), lax.axis_index("y"), lax.axis_index("c")
    chips = [(1 - x, y), (x, 1 - y), (1 - x, 1 - y)]
    return x, y, c, chips


def _by_cols(rows):
    return rows % 32 != 0 and rows != 16


def _half_of(ref, lead, c):
    r, cols = ref.shape[-2:]
    if _by_cols(r):
        return ref.at[(*lead, slice(None), pl.ds(pl.multiple_of(c * (cols // 2), LANE), cols // 2))]
    return ref.at[(*lead, pl.ds(pl.multiple_of(c * (r // 2), 8), r // 2))]


def _half_shape(shape):
    r, cols = shape[-2:]
    return shape[:-2] + ((r, cols // 2) if _by_cols(r) else (r // 2, cols))


def _block_of(r, cols, cap=256):
    if r % 16 == 0:
        return _tile(r, cap, 16), cols
    return r, _tile(cols, cap)


def _place_shard(shard, chip_arr, out_dtype, name, after=()):
    _, r, cols = shard.shape
    tr, tc = _block_of(r, cols)

    def body(chip_ref, s_ref, *rest):
        o_ref = rest[len(after)]
        o_ref[...] = s_ref[...].astype(o_ref.dtype)

    return pl.pallas_call(
        body,
        grid_spec=pltpu.PrefetchScalarGridSpec(
            num_scalar_prefetch=1, grid=(r // tr, cols // tc),
            in_specs=[pl.BlockSpec((None, tr, tc), lambda i, j, chip_ref: (0, i, j))] + [ANY] * len(after),
            out_specs=pl.BlockSpec((None, tr, tc), lambda i, j, chip_ref: (chip_ref[0], i, j))),
        out_shape=jax.ShapeDtypeStruct((4, r, cols), out_dtype),
        compiler_params=_params(("parallel", "parallel")), name=name,
    )(chip_arr, shard, *after)


HBM = pl.BlockSpec(memory_space=pltpu.HBM)
SEM = pl.BlockSpec(memory_space=pltpu.SEMAPHORE)
EFFECT = pltpu.SideEffectType.DATAFLOW_SIDE_EFFECTING


def _in_hbm(arrays):
    return [pltpu.with_memory_space_constraint(a, pltpu.HBM) for a in arrays]


def _gather_start(bufs, after, name):
    n, na = len(bufs), len(after)

    def body(*refs):
        send_sem, recv_sem = refs[n + na], refs[n + na + 1]
        outs = refs[n + na + 2:2 * n + na + 2]
        token = refs[2 * n + na + 2]
        x, y, c, chips = _place()
        me = 2 * x + y
        for p, chip in enumerate(chips):
            for w in range(n):
                block = _half_of(outs[w], (me,), c)
                pltpu.make_async_remote_copy(
                    src_ref=block, dst_ref=block, send_sem=send_sem, recv_sem=recv_sem,
                    device_id=(*chip, c), device_id_type=MESH).start()
        token[...] = jnp.zeros_like(token)

    out = pl.pallas_call(
        body, name=name, in_specs=[HBM] * n + [ANY] * na,
        out_specs=[SEM, SEM] + [HBM] * n + [pl.BlockSpec(memory_space=pltpu.VMEM)],
        out_shape=[pltpu.SemaphoreType.DMA(()), pltpu.SemaphoreType.DMA(())]
        + [pltpu.HBM(a.shape, a.dtype) for a in bufs] + [jax.ShapeDtypeStruct((8, LANE), F32)],
        input_output_aliases={w: w + 2 for w in range(n)},
        compiler_params=pltpu.CompilerParams(has_side_effects=EFFECT),
    )(*_in_hbm(bufs), *after)
    return out[0], out[1], list(out[2:2 + n]), out[2 + n]


def _gather_pass(bufs, send_sem, recv_sem, after, name):
    n, na = len(bufs), len(after)

    def body(*refs):
        send1, recv1 = refs[n], refs[n + 1]
        send2, recv2 = refs[n + 2 + na], refs[n + 3 + na]
        outs = refs[n + 4 + na:2 * n + 4 + na]
        x, y, c, chips = _place()
        me = 2 * x + y
        arrivals = [(w, px, py) for px, py in chips for w in range(n)]
        for w, px, py in arrivals:
            first = pltpu.make_async_remote_copy(
                src_ref=_half_of(outs[w], (me,), c), dst_ref=_half_of(outs[w], (2 * px + py,), c), send_sem=send1, recv_sem=recv1,
                device_id=(px, py, c), device_id_type=MESH)
            first.wait_send()
            first.wait_recv()
        for w, px, py in arrivals:
            arrived = _half_of(outs[w], (2 * px + py,), c)
            pltpu.make_async_remote_copy(
                src_ref=arrived, dst_ref=arrived, send_sem=send2, recv_sem=recv2,
                device_id=(x, y, 1 - c), device_id_type=MESH).start()

    out = pl.pallas_call(
        body, name=name, in_specs=[HBM] * n + [SEM, SEM] + [ANY] * na,
        out_specs=[SEM, SEM] + [HBM] * n,
        out_shape=[pltpu.SemaphoreType.DMA(()), pltpu.SemaphoreType.DMA(())] + [pltpu.HBM(a.shape, a.dtype) for a in bufs],
        input_output_aliases={w: w + 2 for w in range(n)},
        compiler_params=pltpu.CompilerParams(has_side_effects=EFFECT),
    )(*bufs, send_sem, recv_sem, *after)
    return out[0], out[1], list(out[2:])


def _gather_finish(bufs, send_sem, recv_sem, after, name):
    n, na = len(bufs), len(after)

    def body(*refs):
        send2, recv2 = refs[n], refs[n + 1]
        outs = refs[n + 2 + na:2 * n + 2 + na]
        x, y, c, chips = _place()
        for p, (px, py) in enumerate(chips):
            for w in range(n):
                passed = pltpu.make_async_remote_copy(
                    src_ref=_half_of(outs[w], (2 * px + py,), c), dst_ref=_half_of(outs[w], (2 * px + py,), 1 - c),
                    send_sem=send2, recv_sem=recv2, device_id=(x, y, 1 - c), device_id_type=MESH)
                passed.wait_send()
                passed.wait_recv()

    out = pl.pallas_call(
        body, name=name, in_specs=[HBM] * n + [SEM, SEM] + [ANY] * na, out_specs=[HBM] * n,
        out_shape=[pltpu.HBM(a.shape, a.dtype) for a in bufs],
        input_output_aliases={w: w for w in range(n)},
        compiler_params=pltpu.CompilerParams(has_side_effects=EFFECT),
    )(*bufs, send_sem, recv_sem, *after)
    return list(out)


def _pair_exchange(grads, name):
    n = len(grads)

    def body(*refs):
        ins, outs = refs[:n], refs[n:2 * n]
        send_sem, recv_sem = refs[2 * n:]
        x, y, c, _ = _place()
        copies = []
        for w in range(n):
            copies.append(pltpu.make_async_remote_copy(
                src_ref=_half_of(ins[w], (slice(None),), 1 - c), dst_ref=outs[w], send_sem=send_sem.at[w], recv_sem=recv_sem.at[w],
                device_id=(x, y, 1 - c), device_id_type=MESH))
        for cp in copies:
            cp.start()
        for cp in copies:
            cp.wait()

    return pl.pallas_call(
        body, in_specs=[ANY] * n, out_specs=[ANY] * n,
        out_shape=[jax.ShapeDtypeStruct(_half_shape(a.shape), a.dtype) for a in grads],
        scratch_shapes=[pltpu.SemaphoreType.DMA((n,))] * 2,
        compiler_params=pltpu.CompilerParams(has_side_effects=True), name=name,
    )(*grads)


def _pair_exchange_start(grads, after, name):
    n, na = len(grads), len(after)
    lands = [lax.empty(_half_shape(a.shape), a.dtype) for a in grads]

    def body(*refs):
        send_sem, recv_sem = refs[2 * n + na], refs[2 * n + na + 1]
        srcs = refs[2 * n + na + 2:3 * n + na + 2]
        dsts = refs[3 * n + na + 2:4 * n + na + 2]
        token = refs[4 * n + na + 2]
        x, y, c, _ = _place()
        for w in range(n):
            pltpu.make_async_remote_copy(
                src_ref=_half_of(srcs[w], (slice(None),), 1 - c), dst_ref=dsts[w], send_sem=send_sem, recv_sem=recv_sem,
                device_id=(x, y, 1 - c), device_id_type=MESH).start()
        token[...] = jnp.zeros_like(token)

    out = pl.pallas_call(
        body, name=name, in_specs=[HBM] * (2 * n) + [ANY] * na,
        out_specs=[SEM, SEM] + [HBM] * (2 * n) + [pl.BlockSpec(memory_space=pltpu.VMEM)],
        out_shape=[pltpu.SemaphoreType.DMA(()), pltpu.SemaphoreType.DMA(())]
        + [pltpu.HBM(a.shape, a.dtype) for a in grads + lands] + [jax.ShapeDtypeStruct((8, LANE), F32)],
        input_output_aliases={w: w + 2 for w in range(2 * n)},
        compiler_params=pltpu.CompilerParams(has_side_effects=EFFECT),
    )(*_in_hbm(grads), *_in_hbm(lands), *after)
    return out[0], out[1], list(out[2:2 + n]), list(out[2 + n:2 + 2 * n]), out[2 + 2 * n]


def _pair_exchange_finish(grads, lands, send_sem, recv_sem, after, name):
    n, na = len(grads), len(after)

    def body(*refs):
        send, recv = refs[2 * n], refs[2 * n + 1]
        srcs = refs[2 * n + 2 + na:3 * n + 2 + na]
        dsts = refs[3 * n + 2 + na:4 * n + 2 + na]
        x, y, c, _ = _place()
        for w in range(n):
            copy = pltpu.make_async_remote_copy(
                src_ref=_half_of(srcs[w], (slice(None),), 1 - c), dst_ref=dsts[w], send_sem=send, recv_sem=recv,
                device_id=(x, y, 1 - c), device_id_type=MESH)
            copy.wait_send()
            copy.wait_recv()

    out = pl.pallas_call(
        body, name=name, in_specs=[HBM] * (2 * n) + [SEM, SEM] + [ANY] * na, out_specs=[HBM] * (2 * n),
        out_shape=[pltpu.HBM(a.shape, a.dtype) for a in grads + lands],
        input_output_aliases={w: w for w in range(2 * n)},
        compiler_params=pltpu.CompilerParams(has_side_effects=EFFECT),
    )(*grads, *lands, send_sem, recv_sem, *after)
    return list(out[:n]), list(out[n:])


def _pair_sum(g, got, c_arr, name):
    _, r, cols = g.shape
    hr, hc = _half_shape((r, cols))
    tr, tc = _block_of(hr, hc)
    nbr, nbc = hr // tr, hc // tc
    by_cols = _by_cols(r)

    def body(c_ref, g_ref, got_ref, o_ref):
        o_ref[...] = (g_ref[...].astype(F32) + got_ref[...].astype(F32)).astype(o_ref.dtype)

    def mine(j, i, k, c_ref):
        return (j, i, c_ref[0] * nbc + k) if by_cols else (j, c_ref[0] * nbr + i, k)

    return pl.pallas_call(
        body,
        grid_spec=pltpu.PrefetchScalarGridSpec(
            num_scalar_prefetch=1, grid=(4, nbr, nbc),
            in_specs=[pl.BlockSpec((None, tr, tc), mine),
                      pl.BlockSpec((None, tr, tc), lambda j, i, k, c_ref: (j, i, k))],
            out_specs=pl.BlockSpec((None, tr, tc), lambda j, i, k, c_ref: (j, i, k))),
        out_shape=jax.ShapeDtypeStruct((4, hr, hc), BF16),
        compiler_params=_params(("parallel", "parallel", "parallel")), name=name,
    )(c_arr, *_in_hbm([g, got]))


def _chip_exchange_start(parts, after, name):
    n, na = len(parts), len(after)
    lands = [lax.empty((3,) + a.shape[1:], a.dtype) for a in parts]

    def body(*refs):
        send_sem, recv_sem = refs[2 * n + na], refs[2 * n + na + 1]
        srcs = refs[2 * n + na + 2:3 * n + na + 2]
        dsts = refs[3 * n + na + 2:4 * n + na + 2]
        token = refs[4 * n + na + 2]
        x, y, c, chips = _place()
        for p, (px, py) in enumerate(chips):
            for w in range(n):
                pltpu.make_async_remote_copy(
                    src_ref=srcs[w].at[2 * px + py], dst_ref=dsts[w].at[p], send_sem=send_sem, recv_sem=recv_sem,
                    device_id=(px, py, c), device_id_type=MESH).start()
        token[...] = jnp.zeros_like(token)

    out = pl.pallas_call(
        body, name=name, in_specs=[HBM] * (2 * n) + [ANY] * na,
        out_specs=[SEM, SEM] + [HBM] * (2 * n) + [pl.BlockSpec(memory_space=pltpu.VMEM)],
        out_shape=[pltpu.SemaphoreType.DMA(()), pltpu.SemaphoreType.DMA(())]
        + [pltpu.HBM(a.shape, a.dtype) for a in parts + lands] + [jax.ShapeDtypeStruct((8, LANE), F32)],
        input_output_aliases={w: w + 2 for w in range(2 * n)},
        compiler_params=pltpu.CompilerParams(has_side_effects=EFFECT),
    )(*_in_hbm(parts), *_in_hbm(lands), *after)
    return out[0], out[1], list(out[2:2 + n]), list(out[2 + n:2 + 2 * n]), out[2 + 2 * n]


def _chip_exchange_finish(parts, lands, send_sem, recv_sem, after, name):
    n, na = len(parts), len(after)

    def body(*refs):
        send, recv = refs[2 * n], refs[2 * n + 1]
        srcs = refs[2 * n + 2 + na:3 * n + 2 + na]
        dsts = refs[3 * n + 2 + na:4 * n + 2 + na]
        x, y, c, chips = _place()
        for p, (px, py) in enumerate(chips):
            for w in range(n):
                copy = pltpu.make_async_remote_copy(
                    src_ref=srcs[w].at[2 * px + py], dst_ref=dsts[w].at[p], send_sem=send, recv_sem=recv,
                    device_id=(px, py, c), device_id_type=MESH)
                copy.wait_send()
                copy.wait_recv()

    out = pl.pallas_call(
        body, name=name, in_specs=[HBM] * (2 * n) + [SEM, SEM] + [ANY] * na, out_specs=[HBM] * (2 * n),
        out_shape=[pltpu.HBM(a.shape, a.dtype) for a in parts + lands],
        input_output_aliases={w: w for w in range(2 * n)},
        compiler_params=pltpu.CompilerParams(has_side_effects=EFFECT),
    )(*parts, *lands, send_sem, recv_sem, *after)
    return list(out[:n]), list(out[n:])


def _chip_sum(part, got, place_arr, name):
    _, hr, hc = part.shape
    by_cols = _by_cols(hr)
    tr, tc = _block_of(hr, hc)
    nbr, nbc = hr // tr, hc // tc

    def body(place_ref, p_ref, got_ref, o_ref):
        acc = p_ref[...].astype(F32)
        for p in range(3):
            acc = acc + got_ref[p].astype(F32)
        o_ref[...] = acc

    def mine(i, k, place_ref):
        return (i, place_ref[1] * nbc + k) if by_cols else (place_ref[1] * nbr + i, k)

    return pl.pallas_call(
        body,
        grid_spec=pltpu.PrefetchScalarGridSpec(
            num_scalar_prefetch=1, grid=(nbr, nbc),
            in_specs=[pl.BlockSpec((None, tr, tc), lambda i, k, place_ref: (place_ref[0], i, k)),
                      pl.BlockSpec((3, tr, tc), lambda i, k, place_ref: (0, i, k))],
            out_specs=pl.BlockSpec((tr, tc), mine)),
        out_shape=jax.ShapeDtypeStruct((hr, 2 * hc) if by_cols else (2 * hr, hc), F32),
        compiler_params=_params(("parallel", "parallel")), name=name,
    )(place_arr, *_in_hbm([part, got]))


def _pair_join_start(bufs, name):
    n = len(bufs)

    def body(*refs):
        send_sem, recv_sem = refs[n], refs[n + 1]
        outs = refs[n + 2:2 * n + 2]
        token = refs[2 * n + 2]
        x, y, c, _ = _place()
        for w in range(n):
            block = _half_of(outs[w], (), c)
            pltpu.make_async_remote_copy(
                src_ref=block, dst_ref=block, send_sem=send_sem, recv_sem=recv_sem,
                device_id=(x, y, 1 - c), device_id_type=MESH).start()
        token[...] = jnp.zeros_like(token)

    out = pl.pallas_call(
        body, name=name, in_specs=[HBM] * n,
        out_specs=[SEM, SEM] + [HBM] * n + [pl.BlockSpec(memory_space=pltpu.VMEM)],
        out_shape=[pltpu.SemaphoreType.DMA(()), pltpu.SemaphoreType.DMA(())]
        + [pltpu.HBM(a.shape, a.dtype) for a in bufs] + [jax.ShapeDtypeStruct((8, LANE), F32)],
        input_output_aliases={w: w + 2 for w in range(n)},
        compiler_params=pltpu.CompilerParams(has_side_effects=EFFECT),
    )(*_in_hbm(bufs))
    return out[0], out[1], list(out[2:2 + n]), out[2 + n]


def _pair_join_finish(bufs, send_sem, recv_sem, after, name):
    n, na = len(bufs), len(after)

    def body(*refs):
        send, recv = refs[n], refs[n + 1]
        outs = refs[n + 2 + na:2 * n + 2 + na]
        x, y, c, _ = _place()
        for w in range(n):
            copy = pltpu.make_async_remote_copy(
                src_ref=_half_of(outs[w], (), c), dst_ref=_half_of(outs[w], (), 1 - c), send_sem=send, recv_sem=recv,
                device_id=(x, y, 1 - c), device_id_type=MESH)
            copy.wait_send()
            copy.wait_recv()

    out = pl.pallas_call(
        body, name=name, in_specs=[HBM] * n + [SEM, SEM] + [ANY] * na, out_specs=[HBM] * n,
        out_shape=[pltpu.HBM(a.shape, a.dtype) for a in bufs],
        input_output_aliases={w: w for w in range(n)},
        compiler_params=pltpu.CompilerParams(has_side_effects=EFFECT),
    )(*bufs, send_sem, recv_sem, *after)
    return list(out)


SMALL = ("ffn1_pre_g", "ffn1_post_g", "mix_pre_g", "gla_norm_g", "mem_norm_g", "mix_post_g", "ffn2_pre_g", "ffn2_post_g", "final_g",
         "b_f", "pool_scale", "w_pool", "w_fu")
N_GAINS = 9
SMALL_PACKS = ((16, D), (24, 512), (4 * LANE, LANE))
W_FU_ROW = 8


LOSS_ROW = 2


def _all_sum_small(gs, loss, name, after=()):
    ins = [gs[n] for n in SMALL[:N_GAINS]] + [gs["b_f"], gs["pool_scale"], gs["w_fu_pad"], gs["w_pool"].reshape(4 * LANE, LANE), loss]

    def body(*refs):
        gain_refs = refs[:N_GAINS]
        bf_ref, ps_ref, wfu_ref, wp_ref, loss_ref = refs[N_GAINS:N_GAINS + 5]
        outs = refs[N_GAINS + 5 + len(after):N_GAINS + 8 + len(after)]
        mine_a, mine_b, all_a, all_b, all_c, send_sems, recv_sems = refs[N_GAINS + 8 + len(after):]
        mine_a[...] = jnp.zeros_like(mine_a)
        for i, ref in enumerate(gain_refs):
            mine_a[i:i + 1, :] = ref[...]
        mine_b[...] = jnp.zeros_like(mine_b)
        mine_b[0:1, :] = bf_ref[...]
        mine_b[1:2, :] = ps_ref[...]
        mine_b[LOSS_ROW:LOSS_ROW + 1, 0:LANE] = loss_ref[0:1, :]
        mine_b[W_FU_ROW:W_FU_ROW + GATE_RANK, :] = wfu_ref[0:GATE_RANK, :]
        packs = ((mine_a, all_a), (mine_b, all_b), (wp_ref, all_c))
        x, y, c, chips = _place()
        me, sibling = (x, y, c), (x, y, 1 - c)

        def copy(t, k, block, to, own=False):
            px, py, pc = block
            slot = packs[t][1].at[4 * px + 2 * py + pc]
            return pltpu.make_async_remote_copy(
                src_ref=packs[t][0] if own else slot, dst_ref=slot,
                send_sem=send_sems.at[t, k], recv_sem=recv_sems.at[t, k], device_id=to, device_id_type=MESH)

        started = []
        for t, (mine, everyone) in enumerate(packs):
            everyone[4 * x + 2 * y + c] = mine[...]
            started.append(copy(t, 0, me, sibling, own=True))
            started += [copy(t, 1 + j, me, (*chip, c), own=True) for j, chip in enumerate(chips)]
        for cp in started:
            cp.start()
        passed = []
        for j, chip in enumerate(chips):
            for t in range(len(packs)):
                copy(t, 1 + j, (*chip, c), me).wait_recv()
                fwd = copy(t, 4 + j, (*chip, c), sibling)
                fwd.start()
                passed.append(fwd)
        for t in range(len(packs)):
            copy(t, 0, sibling, me).wait_recv()
            for j, chip in enumerate(chips):
                copy(t, 4 + j, (*chip, 1 - c), me).wait_recv()
        for cp in started + passed:
            cp.wait_send()
        for (_, everyone), o_ref in zip(packs, outs):
            acc = everyone[0]
            for k in range(1, 8):
                acc = acc + everyone[k]
            o_ref[...] = acc

    vmem = pl.BlockSpec(memory_space=pltpu.VMEM)
    return pl.pallas_call(
        body, in_specs=[vmem] * len(ins) + [ANY] * len(after), out_specs=[vmem] * 3,
        out_shape=[jax.ShapeDtypeStruct(shape, F32) for shape in SMALL_PACKS],
        scratch_shapes=[pltpu.VMEM(SMALL_PACKS[0], F32), pltpu.VMEM(SMALL_PACKS[1], F32)]
        + [pltpu.VMEM((8,) + shape, F32) for shape in SMALL_PACKS]
        + [pltpu.SemaphoreType.DMA((3, 7)), pltpu.SemaphoreType.DMA((3, 7))],
        compiler_params=pltpu.CompilerParams(has_side_effects=True, vmem_limit_bytes=VMEM_LIMIT), name=name,
    )(*ins, *after)


def _adamw_small(sums, params, chip_arr, name):
    flat = [a for n in SMALL for a in params[n]]

    def body(chip_ref, a_ref, b_ref, c_ref, *refs):
        ins, outs = refs[:len(flat)], refs[len(flat):]
        for i, n in enumerate(SMALL):
            w_ref, m_ref, v_ref = ins[3 * i:3 * i + 3]
            g_ref, d_ref, mo_ref, vo_ref = outs[4 * i:4 * i + 4]
            if n == "w_pool":
                pieces = [((0, k), c_ref[k * LANE:(k + 1) * LANE, :]) for k in range(4)]
            elif n == "w_fu":
                mine = pl.ds(pl.multiple_of(chip_ref[0] * LANE, LANE), LANE)
                pieces = [((0,), b_ref[W_FU_ROW:W_FU_ROW + GATE_RANK, mine])]
            elif n == "b_f":
                pieces = [((), b_ref[0:1, :])]
            elif n == "pool_scale":
                pieces = [((), b_ref[1:2, :])]
            else:
                pieces = [((), a_ref[i:i + 1, :])]
            for at, g in pieces:
                d, mn, vn = _adam_math(w_ref[at], g, m_ref[at], v_ref[at])
                g_ref[at] = g
                d_ref[at] = d
                mo_ref[at] = mn
                vo_ref[at] = vn

    def whole(shape):
        return pl.BlockSpec(shape, lambda i, chip_ref: (0,) * len(shape))

    out = pl.pallas_call(
        body,
        grid_spec=pltpu.PrefetchScalarGridSpec(
            num_scalar_prefetch=1, grid=(1,),
            in_specs=[whole(a.shape) for a in list(sums) + flat],
            out_specs=[whole(params[n][0].shape) for n in SMALL for _ in range(4)]),
        out_shape=[jax.ShapeDtypeStruct(params[n][0].shape, F32) for n in SMALL for _ in range(4)],
        compiler_params=_params(("arbitrary",)), name=name,
    )(chip_arr, *sums, *flat)
    return {n: tuple(out[4 * i:4 * i + 4]) for i, n in enumerate(SMALL)}


def _ffn_bwd(dz, x_norm, ab, u, w_in, w_out, x, g_pre, dres, tag, emit, advance, after=(), post=None):
    dw_out = _mm(u, dz, ta=True, out_dtype=BF16, tm=1408, tk=2048, after=after, name=tag + "_out_dw")
    behind = emit(tag + "_w_out", dw_out)
    dab = _ffn_out_dx_swiglu(dz, w_out, ab, behind, name=tag + "_out_dx")
    behind = advance((dab,))
    dw_in = _mm(x_norm, dab, ta=True, out_dtype=BF16, tm=512, tk=4096, shards=4, after=behind, name=tag + "_in_dw")
    behind = emit(tag + "_w_in", dw_in)
    out = _mm_rms_bwd([(dab, w_in)], x, g_pre, dres, after=behind, post=post, name=tag + "_in_dx")
    return (*out, advance((out[0],)))


def _local_step(x, mem, target, small, gather, emit, advance):
    behind = gather("start", "ffn1i", ())
    behind = gather("start", "ffn1o", behind)
    h1 = _norm_fwd(x, small["ffn1_pre_g"], BF16, name="ffn1_pre", after=behind)
    gather("pass", "ffn1i", (h1,))
    big = gather("finish", "ffn1i", ())
    behind = gather("start", "mixa", (big["ffn1_w_in"],))
    behind = gather("start", "mixb", behind)
    behind = gather("start", "ffn2", behind)
    ab1, u1 = _ffn_in_swiglu(h1, big["ffn1_w_in"], name="ffn1_in", after=behind)
    gather("pass", "ffn1o", (ab1,))
    big.update(gather("finish", "ffn1o", ()))
    behind = gather("pass", "mixa", (u1,))
    f1, x1, h = _mm_resid_norm(u1, big["ffn1_w_out"], x, small["ffn1_post_g"], 0.5, small["mix_pre_g"], name="ffn1_out", after=behind)
    big.update(gather("finish", "mixa", (h,)))
    small = dict(small, w_fu_pad=big["w_fu_pad"])
    behind = gather("pass", "mixb", (h,))
    pg, ppx, pgt = _mix_in_fwd(h, [big["w_gla_t"], big["w_px_t"], big["w_gates_t"]], behind, name="mix_in")
    big.update(gather("finish", "mixb", (pgt,)))
    mem_n = _norm_fwd(mem, small["mem_norm_g"], BF16, name="mem_norm")
    kv = _mm(mem_n, big["w_mem_kv"], out_dtype=BF16, name="mem_kv")
    ya_in, sp, so, o_gla = _gla_fwd(pg, small["w_fu_pad"], small["b_f"], small["gla_norm_g"], name="gla_fwd")
    yb_in = _pool_fwd(ppx, small["w_pool_b"], small["pool_scale"], name="pool_fwd")
    xc = _xattn_fwd(ppx, kv, name="xattn_fwd")
    behind = gather("pass", "ffn2", (xc,))
    w_ups = [big["w_up_gla"], big["w_up_pool"], big["w_up_xattn"]]
    ya, yb, yc, merged, ymix, x2, h2 = _mix_tail_fwd(ya_in, yb_in, xc, pgt, w_ups, big["w_o"], x1, small["mix_post_g"],
                                                     small["ffn2_pre_g"], behind, name="mix_tail")
    big.update(gather("finish", "ffn2", (h2,)))
    ab2, u2 = _ffn_in_swiglu(h2, big["ffn2_w_in"], name="ffn2_in")
    gs = {}
    dz2, dx3, gs["ffn2_post_g"], gs["final_g"], loss = _ffn_out_loss(u2, big["ffn2_w_out"], x2, small["ffn2_post_g"], 0.5,
                                                                    small["final_g"], target, name="ffn2_out_loss")
    dx2, gs["ffn2_pre_g"], dy, gs["mix_post_g"], behind = _ffn_bwd(
        dz2, h2, ab2, u2, big["ffn2_w_in"], big["ffn2_w_out"], x2, small["ffn2_pre_g"], dx3, "ffn2", emit, advance,
        post=(ymix, small["mix_post_g"], 1.0))
    emit("w_o", _mm(merged, dy, ta=True, out_dtype=BF16, tm=512, tk=4096, after=behind, name="mix_out_dw"))
    dya, dyb, dyc, dgt, dya_in, dyb_in, dxc = _mix_tail_bwd(dy, pgt, (ya, yb, yc), w_ups, big["w_o"], (), name="mix_tail_bwd")
    emit("w_up_gla", _mm(ya_in, dya, ta=True, out_dtype=BF16, tm=512, tk=4096, name="up_gla_dw"))
    emit("w_up_pool", _mm(yb_in, dyb, ta=True, out_dtype=BF16, tm=512, tk=4096, shards=4, name="up_pool_dw"))
    emit("w_up_xattn", _mm(xc, dyc, ta=True, out_dtype=BF16, tm=512, tk=4096, shards=4, name="up_xattn_dw"))
    dpg, gs["w_fu_pad"], gs["b_f"], gs["gla_norm_g"] = _gla_bwd(pg, sp, so, o_gla, dya_in, small["w_fu_pad"], small["b_f"], small["gla_norm_g"], name="gla_bwd")
    dp, gs["w_pool"], gs["pool_scale"] = _pool_bwd(dyb_in, ppx, small["w_pool_b"], small["pool_scale"], name="pool_bwd")
    dxq, dkv = _xattn_bwd(dxc, ppx, kv, name="xattn_bwd")
    dkv = dkv.astype(BF16)
    emit("w_mem_kv", _mm(mem_n, dkv, ta=True, out_dtype=BF16, name="mem_kv_dw"))
    dmem_n = _mm(dkv, big["w_mem_kv"], tb=True, name="mem_kv_dx")
    _, gs["mem_norm_g"] = _rms_bwd(mem, small["mem_norm_g"], [dmem_n], None, 1.0, BF16, name="mem_norm_bwd")
    emit("w_gla", _mm(dpg, h, ta=True, out_dtype=BF16, tm=640, tk=4096, name="mix_in_gla_dw"))
    emit("w_p", _mm(dp, h, ta=True, out_dtype=BF16, tm=512, tk=4096, name="mix_in_p_dw"))
    emit("w_xq", _mm(dxq, h, ta=True, out_dtype=BF16, tm=512, tk=4096, name="mix_in_xq_dw"))
    behind = emit("w_gates", _mm(dgt, h, ta=True, out_dtype=BF16, tm=512, tk=4096, name="mix_in_gates_dw"))
    pairs = [(dpg, big["w_gla_t"]), (dp, big["w_p_t"]), (dxq, big["w_xq_t"]), (dgt, big["w_gates_t"])]
    dx1, gs["mix_pre_g"], dz1, gs["ffn1_post_g"] = _mm_rms_bwd(pairs, x1, small["mix_pre_g"], dx2, after=behind,
                                                               post=(f1, small["ffn1_post_g"], 0.5), tm=256, name="mix_in_dx")
    behind = advance((dx1,))
    dx0, gs["ffn1_pre_g"], _ = _ffn_bwd(dz1, h1, ab1, u1, big["ffn1_w_in"], big["ffn1_w_out"], x, small["ffn1_pre_g"], dx1,
                                        "ffn1", emit, advance, after=behind)
    return loss, dx0, gs


BIG = ("ffn1_w_in", "ffn1_w_out", "w_in", "w_mem_kv", "w_up_gla", "w_up_pool", "w_up_xattn", "w_o", "ffn2_w_in", "ffn2_w_out")
COL_SHARDED = ("ffn1_w_in", "w_in", "w_up_pool", "w_up_xattn", "ffn2_w_in")
GATHER_GROUPS = {"ffn1i": ("ffn1_w_in",), "ffn1o": ("ffn1_w_out",), "mixa": ("w_in", "w_fu"),
                 "mixb": ("w_mem_kv", "w_up_gla", "w_up_pool", "w_up_xattn", "w_o"), "ffn2": ("ffn2_w_in", "ffn2_w_out")}
REDUCE_GROUPS = {"ffn2": ("ffn2_w_out", "ffn2_w_in"),
                 "mix": ("w_o", "w_up_gla", "w_up_pool", "w_up_xattn", "w_mem_kv", "w_gla", "w_p", "w_xq", "w_gates"),
                 "ffn1_out": ("ffn1_w_out",),
                 "ffn1_in": ("ffn1_w_in",)}
REDUCE_LAST = "ffn1_in"
GAINS = ("ffn1_pre_g", "ffn1_post_g", "mix_pre_g", "gla_norm_g", "mem_norm_g", "mix_post_g", "ffn2_pre_g", "ffn2_post_g", "final_g")
WEIGHTS = ("ffn1_pre_g", "ffn1_w_in", "ffn1_w_out", "ffn1_post_g", "mix_pre_g", "w_in", "w_fu", "b_f", "gla_norm_g", "w_pool",
           "pool_scale", "mem_norm_g", "w_mem_kv", "w_up_gla", "w_up_pool", "w_up_xattn", "w_o", "mix_post_g", "ffn2_pre_g",
           "ffn2_w_in", "ffn2_w_out", "ffn2_post_g", "final_g")
IN_GLA, IN_F, IN_PX, IN_GATES, IN_END = 0, 3072, 3088, 4112, 7184
def _cols_from_shards(g):
    return jnp.transpose(g, (1, 0, 2)).reshape(g.shape[1], 4 * g.shape[2])


def kernel(x, mem, ffn1_pre_g, ffn1_w_in, ffn1_w_out, ffn1_post_g, mix_pre_g, w_in, w_fu, b_f, gla_norm_g, w_pool, pool_scale, mem_norm_g, w_mem_kv, w_up_gla, w_up_pool, w_up_xattn, w_o, mix_post_g, ffn2_pre_g, ffn2_w_in, ffn2_w_out, ffn2_post_g, final_g, loss_target, m_ffn1_pre_g, m_ffn1_w_in, m_ffn1_w_out, m_ffn1_post_g, m_mix_pre_g, m_w_in, m_w_fu, m_b_f, m_gla_norm_g, m_w_pool, m_pool_scale, m_mem_norm_g, m_w_mem_kv, m_w_up_gla, m_w_up_pool, m_w_up_xattn, m_w_o, m_mix_post_g, m_ffn2_pre_g, m_ffn2_w_in, m_ffn2_w_out, m_ffn2_post_g, m_final_g, v_ffn1_pre_g, v_ffn1_w_in, v_ffn1_w_out, v_ffn1_post_g, v_mix_pre_g, v_w_in, v_w_fu, v_b_f, v_gla_norm_g, v_w_pool, v_pool_scale, v_mem_norm_g, v_w_mem_kv, v_w_up_gla, v_w_up_pool, v_w_up_xattn, v_w_o, v_mix_post_g, v_ffn2_pre_g, v_ffn2_w_in, v_ffn2_w_out, v_ffn2_post_g, v_final_g):
    args = dict(locals())
    w = {n: args[n][0] for n in WEIGHTS}
    m = {n: args["m_" + n][0] for n in WEIGHTS}
    v = {n: args["v_" + n][0] for n in WEIGHTS}
    xi, yi, ci = lax.axis_index("x"), lax.axis_index("y"), lax.axis_index("c")
    chip = 2 * xi + yi

    c_arr = jnp.reshape(ci, (1,)).astype(jnp.int32)
    chip_arr = jnp.reshape(chip, (1,)).astype(jnp.int32)
    place_arr = jnp.stack([chip, ci]).astype(jnp.int32)
    w_in_t = []
    shard_of = {n: args[n] for n in BIG if n != "w_in"}
    shard_of["w_fu"] = args["w_fu"]
    placed, inflight = {}, {}

    def place(names, after):
        for n in names:
            if n not in placed:
                placed[n] = _place_shard(shard_of[n], chip_arr, F32 if n == "w_fu" else BF16, name="place_" + n, after=after)

    def relayout(names, gathered):
        out = {}
        for n, g in zip(names, gathered):
            if n == "w_fu":
                w_fu_full = _cols_from_shards(g)
                out["w_fu_pad"] = jnp.concatenate([w_fu_full, jnp.zeros((LANE - GATE_RANK, 512), F32)], axis=0).astype(BF16)
            elif n == "w_in":
                wt = g.reshape(IN_END, D)
                out["w_gla_t"] = jnp.concatenate([wt[IN_GLA:IN_PX], jnp.zeros((PG_W - IN_PX, D), BF16)], axis=0)
                out["w_px_t"] = wt[IN_PX:IN_GATES]
                out["w_p_t"] = wt[IN_PX:IN_PX + 512]
                out["w_xq_t"] = wt[IN_PX + 512:IN_GATES]
                out["w_gates_t"] = wt[IN_GATES:IN_END]
            else:
                out[n] = _cols_from_shards(g) if n in COL_SHARDED else g.reshape(4 * g.shape[1], g.shape[2])
        return out

    def gather(op, group, after):
        names = GATHER_GROUPS[group]
        if op == "start":
            place(names, ())
            inflight[group] = _gather_start([placed[n] for n in names], after, name="gather_" + group + "_start")
            behind = (inflight[group][3],)
            if group == "ffn1o":
                tied = lax.optimization_barrier((behind, tuple(args[k] for k in ("w_in", "m_w_in", "v_w_in"))))[1]
                w_in_t.extend(jnp.transpose(a[0]) for a in tied)
                shard_of["w_in"] = w_in_t[0][None]
                place(shard_of, behind)
            return behind
        if op == "pass":
            if group == "ffn1i":
                not_started = [n for g in GATHER_GROUPS if g not in inflight for n in GATHER_GROUPS[g]]
                after = tuple(after) + tuple(w_in_t[1:]) + tuple(placed[n] for n in not_started)
            send, recv, bufs, _ = inflight[group]
            inflight[group] = _gather_pass(bufs, send, recv, after, name="gather_" + group + "_pass")
            return (inflight[group][2][0],)
        send, recv, bufs = inflight.pop(group)
        return relayout(names, _gather_finish(bufs, send, recv, after, name="gather_" + group + "_finish"))

    small = {n: w[n].reshape(1, D) for n in GAINS}
    small["b_f"] = w["b_f"].reshape(1, 512)
    small["pool_scale"] = w["pool_scale"].reshape(1, 512)
    small["w_pool_b"] = w["w_pool"].astype(BF16)

    pending, crossing, travelling = {}, {}, {}

    def emit(name, grad):
        pending[name] = grad
        group = next((g for g, names in REDUCE_GROUPS.items() if name == names[-1]), None)
        if group is None:
            return ()
        gb = {n: pending.pop(n) for n in REDUCE_GROUPS[group]}
        if group == "mix":
            dwt = jnp.concatenate([gb.pop("w_gla")[0:IN_PX], gb.pop("w_p"), gb.pop("w_xq"), gb.pop("w_gates")], axis=0)
            gb["w_in"] = dwt.reshape(4, IN_END // 4, D)
        names = list(gb)
        contrib = [gb[n] if n in COL_SHARDED else gb[n].reshape(4, gb[n].shape[0] // 4, gb[n].shape[1]) for n in names]
        if group == REDUCE_LAST:
            from_sibling = _pair_exchange(contrib, name="grads_" + group + "_pair_exchange")
            return over_chips(group, names, contrib, from_sibling)
        send, recv, contrib, lands, token = _pair_exchange_start(contrib, (), name="grads_" + group + "_pair_start")
        crossing[group] = (names, contrib, lands, send, recv)
        return (token,)

    def over_chips(group, names, contrib, from_sibling):
        pair = [_pair_sum(g, got, c_arr, name="grads_pair_sum_" + n) for n, g, got in zip(names, contrib, from_sibling)]
        send, recv, pair, lands, token = _chip_exchange_start(pair, (), name="grads_" + group + "_chip_start")
        travelling[group] = (names, send, recv, pair, lands)
        return (token,)

    def advance(after):
        behind = ()
        for group in list(crossing):
            names, contrib, lands, send, recv = crossing.pop(group)
            contrib, from_sibling = _pair_exchange_finish(contrib, lands, send, recv, after, name="grads_" + group + "_pair_finish")
            behind = over_chips(group, names, contrib, from_sibling)
        return behind

    loss, grad_x, gs = _local_step(x[0], mem[0], loss_target[0], small, gather, emit, advance)

    halves = {}
    for group, (names, send, recv, pair, lands) in travelling.items():
        pair, from_chips = _chip_exchange_finish(pair, lands, send, recv, (grad_x,), name="grads_" + group + "_chip_finish")
        for n, p, got in zip(names, pair, from_chips):
            halves[n] = _chip_sum(p, got, place_arr, name="grads_chip_sum_" + n)
    send, recv, joining, token = _pair_join_start([halves[n] for n in BIG], name="grads_pair_join_start")
    small_sums = _all_sum_small(gs, loss, name="sum_small_grads", after=(token,))
    loss = small_sums[1][LOSS_ROW, 0]
    reduced = dict(zip(BIG, _pair_join_finish(joining, send, recv, (small_sums[0],), name="grads_pair_join_finish")))

    grads, delta, new_m, new_v = {}, {}, {}, {}
    for n in BIG:
        if n == "w_in":
            updated = _adamw(w_in_t[0], reduced[n], w_in_t[1], w_in_t[2], name="adamw_" + n)
            grads[n] = jnp.transpose(reduced[n])[None]
            delta[n], new_m[n], new_v[n] = (jnp.transpose(a)[None] for a in updated)
            continue
        grads[n] = reduced[n][None]
        delta[n], new_m[n], new_v[n] = _adamw(args[n], reduced[n], args["m_" + n], args["v_" + n], name="adamw_" + n)
    small_params = {n: (args[n], args["m_" + n], args["v_" + n]) for n in SMALL}
    for n, (g, d, mn, vn) in _adamw_small(small_sums, small_params, chip_arr, name="adamw_small").items():
        grads[n], delta[n], new_m[n], new_v[n] = g, d, mn, vn

    outs = [loss, grad_x[None]]
    for group in (grads, delta, new_m, new_v):
        outs += [group[n] for n in WEIGHTS]
    return tuple(outs)
```

```python
import jax
import jax.numpy as jnp
from jax import lax
from jax.experimental import pallas as pl
from jax.experimental.pallas import tpu as pltpu

F32 = jnp.float32
BF16 = jnp.bfloat16
MESH = pl.DeviceIdType.MESH
HIGHEST = lax.Precision.HIGHEST

D = 1024
DFF = 2816
CHUNK = 64
HEADS = 4
HDK = 128
HDV = 256
GATE_TEMP = 16.0
POOL_WINDOWS = (2, 4, 8, 16)
POOL_HALO = 16
XA_HEADS = 4
XA_HD = 128
EPS = 1e-6
Q_SCALE = HDK ** -0.5
XA_SCALE = XA_HD ** -0.5
PG_Q, PG_K, PG_V, PG_G, PG_F, PG_W = 0, 512, 1024, 2048, 3072, 3200
GATE_RANK = 16
ADAM_LR, ADAM_B1, ADAM_B2, ADAM_EPS, ADAM_WD, ADAM_STEP = 0.001, 0.9, 0.999, 1e-08, 0.01, 10

VMEM_LIMIT = 48 * 1024 * 1024
LANE = 128
TS_ROW = 512
TS_GLA = 512
TS_POOL = 512
TS_XA = 512


def _params(sem):
    return pltpu.CompilerParams(dimension_semantics=sem, vmem_limit_bytes=VMEM_LIMIT)


def _tile(n, cap, unit=LANE):
    if n <= cap:
        return n
    best = None
    for t in range(unit, cap + 1, unit):
        if n % t == 0:
            best = t
    assert best is not None, (n, cap)
    return best


def _sigmoid(x):
    return 0.5 * jnp.tanh(0.5 * x) + 0.5


def _log_sigmoid(x):
    return jnp.minimum(x, 0.0) - jnp.log(1.0 + jnp.exp(-jnp.abs(x)))


def _rms(x):
    r = lax.rsqrt(jnp.mean(x * x, axis=-1, keepdims=True) + EPS)
    return x * r, r


def _rows(ts, w):
    return pl.BlockSpec((ts, w), lambda i: (i, 0))


def _fixed(shape):
    nd = len(shape)
    return pl.BlockSpec(shape, lambda i: (0,) * nd)


def _mm(a, b, *, ta=False, tb=False, out_dtype=F32, tm=2048, tn=1024, tk=1024, shards=1, after=(), name):
    b_blocked = b.ndim == 3
    assert not (b_blocked and tb)
    m, kdim = (a.shape[1], a.shape[0]) if ta else a.shape
    if b_blocked:
        n, tn = b.shape[0] * b.shape[2], b.shape[2]
        assert b.shape[1] == kdim and shards in (1, b.shape[0])
    else:
        n = b.shape[0] if tb else b.shape[1]
        assert (b.shape[1] if tb else b.shape[0]) == kdim, (a.shape, b.shape, ta, tb)
        tn = n // shards if shards > 1 else _tile(n, tn)
    tm = _tile(m, tm)
    tk = _tile(kdim, tk)
    nk = kdim // tk
    dims = (((0 if ta else 1,), (1 if tb else 0,)), ((), ()))

    def body(a_ref, b_ref, *rest):
        o_ref, *acc = rest[len(after):]
        part = lax.dot_general(a_ref[...], b_ref[...], dims, preferred_element_type=F32)
        if nk == 1:
            o_ref[...] = part.astype(o_ref.dtype)
            return
        acc_ref, = acc
        k = pl.program_id(2)

        @pl.when(k == 0)
        def _():
            acc_ref[...] = part

        @pl.when(k > 0)
        def _():
            acc_ref[...] += part

        @pl.when(k == nk - 1)
        def _():
            o_ref[...] = acc_ref[...].astype(o_ref.dtype)

    a_spec = pl.BlockSpec((tk, tm), lambda i, j, k: (k, i)) if ta else pl.BlockSpec((tm, tk), lambda i, j, k: (i, k))
    if b_blocked:
        b_spec = pl.BlockSpec((None, tk, tn), lambda i, j, k: (j, k, 0))
    else:
        b_spec = pl.BlockSpec((tn, tk), lambda i, j, k: (j, k)) if tb else pl.BlockSpec((tk, tn), lambda i, j, k: (k, j))
    if shards > 1:
        out_shape = jax.ShapeDtypeStruct((shards, m, tn), out_dtype)
        o_spec = pl.BlockSpec((None, tm, tn), lambda i, j, k: (j, i, 0))
    else:
        out_shape = jax.ShapeDtypeStruct((m, n), out_dtype)
        o_spec = pl.BlockSpec((tm, tn), lambda i, j, k: (i, j))
    return pl.pallas_call(
        body, grid=(m // tm, n // tn, nk), in_specs=[a_spec, b_spec] + [ANY] * len(after), out_specs=o_spec, out_shape=out_shape,
        scratch_shapes=[pltpu.VMEM((tm, tn), F32)] if nk > 1 else [],
        compiler_params=_params(("parallel", "parallel", "arbitrary")), name=name,
    )(a, b, *after)


def _norm_fwd(x, g, out_dtype, name, after=()):
    s, d = x.shape
    ts = _tile(s, TS_ROW, 8)

    def body(x_ref, g_ref, *rest):
        o_ref = rest[len(after)]
        xh, _ = _rms(x_ref[...])
        o_ref[...] = (xh * g_ref[...]).astype(o_ref.dtype)

    return pl.pallas_call(
        body, grid=(s // ts,), in_specs=[_rows(ts, d), _fixed((1, d))] + [ANY] * len(after), out_specs=_rows(ts, d),
        out_shape=jax.ShapeDtypeStruct((s, d), out_dtype), compiler_params=_params(("parallel",)), name=name,
    )(x, g, *after)


def _mm_resid_norm(a, w, x, g_post, alpha, g_next, name, after=(), tm=512):
    s, kdim = a.shape
    d = w.shape[1]
    tm = _tile(s, tm)
    with_h = g_next is not None
    na = len(after)

    def body(a_ref, w_ref, x_ref, gp_ref, *rest):
        rest = rest[int(with_h) + na:] if not with_h else rest[:1] + rest[1 + na:]
        for rows in _sub_blocks(tm):
            f = jnp.dot(a_ref[rows, :], w_ref[...], preferred_element_type=F32)
            fh, _ = _rms(f)
            xn = x_ref[rows, :] + alpha * (fh * gp_ref[...])
            if with_h:
                gn_ref, f_ref, xo_ref, h_ref = rest
                xh, _ = _rms(xn)
                h_ref[rows, :] = (xh * gn_ref[...]).astype(h_ref.dtype)
            else:
                f_ref, xo_ref = rest
            f_ref[rows, :] = f
            xo_ref[rows, :] = xn

    ins = [a, w, x, g_post] + ([g_next] if with_h else []) + list(after)
    in_specs = [_rows(tm, kdim), _fixed((kdim, d)), _rows(tm, d), _fixed((1, d))] + ([_fixed((1, d))] if with_h else []) + [ANY] * na
    out_shape = [jax.ShapeDtypeStruct((s, d), F32)] * 2 + ([jax.ShapeDtypeStruct((s, d), BF16)] if with_h else [])
    out = pl.pallas_call(
        body, grid=(s // tm,), in_specs=in_specs, out_specs=[_rows(tm, d)] * len(out_shape), out_shape=out_shape,
        compiler_params=_params(("parallel",)), name=name,
    )(*ins)
    return (out[0], out[1], out[2]) if with_h else (out[0], out[1], None)


def _mm_rms_bwd(pairs, x, g, dres, name, after=(), post=None, tm=512):
    s, d = x.shape
    tm = _tile(s, tm)
    n, na = len(pairs), len(after)

    def body(*refs):
        a_refs, w_refs = refs[0:2 * n:2], refs[1:2 * n:2]
        x_ref, g_ref, dres_ref = refs[2 * n:2 * n + 3]
        if post is not None:
            f_ref, gp_ref = refs[2 * n + 3:2 * n + 5]
            dx_ref, dg_ref, df_ref, dgp_ref = refs[2 * n + 5 + na:]
        else:
            dx_ref, dg_ref = refs[2 * n + 3 + na:]
        @pl.when(pl.program_id(0) == 0)
        def _():
            dg_ref[...] = jnp.zeros_like(dg_ref)
            if post is not None:
                dgp_ref[...] = jnp.zeros_like(dgp_ref)

        for rows in _sub_blocks(tm):
            dy = None
            for a_ref, w_ref in zip(a_refs, w_refs):
                if len(a_ref.shape) == 3:
                    tkb = a_ref.shape[2]
                    parts = [lax.dot_general(a_ref[q, rows, :], w_ref[:, q * tkb:(q + 1) * tkb], (((1,), (1,)), ((), ())),
                                             preferred_element_type=F32) for q in range(a_ref.shape[0])]
                else:
                    parts = [jnp.dot(a_ref[rows, :], w_ref[...], preferred_element_type=F32)]
                for part in parts:
                    dy = part if dy is None else dy + part
            xh, r = _rms(x_ref[rows, :])
            dg_ref[...] += jnp.sum(dy * xh, axis=0, keepdims=True)
            dyg = dy * g_ref[...]
            dx = r * (dyg - xh * jnp.mean(dyg * xh, axis=-1, keepdims=True)) + dres_ref[rows, :]
            dx_ref[rows, :] = dx
            if post is not None:
                fh, rf = _rms(f_ref[rows, :])
                dz = dx * post[2]
                dgp_ref[...] += jnp.sum(dz * fh, axis=0, keepdims=True)
                dzg = dz * gp_ref[...]
                df_ref[rows, :] = (rf * (dzg - fh * jnp.mean(dzg * fh, axis=-1, keepdims=True))).astype(df_ref.dtype)

    ins, in_specs = [], []
    for a_arr, w_arr in pairs:
        ins += [a_arr, w_arr]
        if a_arr.ndim == 3:
            in_specs.append(pl.BlockSpec((a_arr.shape[0], tm, a_arr.shape[2]), lambda i: (0, i, 0)))
        else:
            in_specs.append(_rows(tm, a_arr.shape[1]))
        in_specs.append(pl.BlockSpec(w_arr.shape, lambda i: (0, 0), pipeline_mode=pl.Buffered(1)))
    with_post = post is not None
    return pl.pallas_call(
        body, grid=(s // tm,),
        in_specs=in_specs + [_rows(tm, d), _fixed((1, d)), _rows(tm, d)] + ([_rows(tm, d), _fixed((1, d))] if with_post else [])
        + [ANY] * na,
        out_specs=[_rows(tm, d), _fixed((1, d))] + ([_rows(tm, d), _fixed((1, d))] if with_post else []),
        out_shape=[jax.ShapeDtypeStruct((s, d), F32), jax.ShapeDtypeStruct((1, d), F32)]
        + ([jax.ShapeDtypeStruct((s, d), BF16), jax.ShapeDtypeStruct((1, d), F32)] if with_post else []),
        compiler_params=_params(("arbitrary",)), name=name,
    )(*ins, x, g, dres, *(post[:2] if with_post else ()), *after)


def _ffn_out_loss(u, w_out, x, g_post, alpha, g_final, target, name, tm=512):
    s, kdim = u.shape
    d = w_out.shape[1]
    tm = _tile(s, tm)

    def body(u_ref, w_ref, x_ref, gp_ref, gf_ref, t_ref, df_ref, dx_ref, dgp_ref, dgf_ref, loss_ref):
        @pl.when(pl.program_id(0) == 0)
        def _():
            dgp_ref[...] = jnp.zeros_like(dgp_ref)
            dgf_ref[...] = jnp.zeros_like(dgf_ref)
            loss_ref[...] = jnp.zeros_like(loss_ref)

        for rows in _sub_blocks(tm):
            f = jnp.dot(u_ref[rows, :], w_ref[...], preferred_element_type=F32)
            fh, rf = _rms(f)
            xn = x_ref[rows, :] + alpha * (fh * gp_ref[...])
            xh, rx = _rms(xn)
            gf = gf_ref[...]
            diff = xh * gf - t_ref[rows, :]
            sq = jnp.sum(diff * diff, axis=1, keepdims=True)
            loss_ref[...] += (0.5 / d) * jnp.sum(sq, axis=0, keepdims=True)
            dy = diff * (1.0 / d)
            dgf_ref[...] += jnp.sum(dy * xh, axis=0, keepdims=True)
            dyg = dy * gf
            dxn = rx * (dyg - xh * jnp.mean(dyg * xh, axis=-1, keepdims=True))
            dx_ref[rows, :] = dxn
            dz = dxn * alpha
            dgp_ref[...] += jnp.sum(dz * fh, axis=0, keepdims=True)
            dzg = dz * gp_ref[...]
            df_ref[rows, :] = (rf * (dzg - fh * jnp.mean(dzg * fh, axis=-1, keepdims=True))).astype(df_ref.dtype)

    return pl.pallas_call(
        body, grid=(s // tm,),
        in_specs=[_rows(tm, kdim), _resident(w_out.shape), _rows(tm, d), _fixed((1, d)), _fixed((1, d)), _rows(tm, d)],
        out_specs=[_rows(tm, d), _rows(tm, d), _fixed((1, d)), _fixed((1, d)), _fixed((8, LANE))],
        out_shape=[jax.ShapeDtypeStruct((s, d), BF16), jax.ShapeDtypeStruct((s, d), F32), jax.ShapeDtypeStruct((1, d), F32),
                   jax.ShapeDtypeStruct((1, d), F32), jax.ShapeDtypeStruct((8, LANE), F32)],
        compiler_params=_params(("arbitrary",)), name=name,
    )(u, w_out, x, g_post, g_final, target)


def _rms_bwd(x, g, dys, dres, alpha, out_dtype, name, after=()):
    s, d = x.shape
    ts = _tile(s, TS_ROW, 8)
    ndy = len(dys)
    with_res = dres is not None

    def body(x_ref, g_ref, *rest):
        dy_refs = rest[:ndy]
        rest = rest[ndy:]
        if with_res:
            dres_ref = rest[0]
        dx_ref, dg_ref = rest[int(with_res) + len(after):]
        xh, r = _rms(x_ref[...])
        dy = dy_refs[0][...].astype(F32)
        for ref in dy_refs[1:]:
            dy = dy + ref[...].astype(F32)
        dy = dy * alpha

        @pl.when(pl.program_id(0) == 0)
        def _():
            dg_ref[...] = jnp.zeros_like(dg_ref)

        dg_ref[...] += jnp.sum(dy * xh, axis=0, keepdims=True)
        dyg = dy * g_ref[...]
        dx = r * (dyg - xh * jnp.mean(dyg * xh, axis=-1, keepdims=True))
        if with_res:
            dx = dx + dres_ref[...]
        dx_ref[...] = dx.astype(dx_ref.dtype)

    ins = [x, g] + list(dys) + ([dres] if with_res else []) + list(after)
    in_specs = [_rows(ts, d), _fixed((1, d))] + [_rows(ts, d)] * (ndy + int(with_res)) + [ANY] * len(after)
    return pl.pallas_call(
        body, grid=(s // ts,), in_specs=in_specs, out_specs=[_rows(ts, d), _fixed((1, d))],
        out_shape=[jax.ShapeDtypeStruct((s, d), out_dtype), jax.ShapeDtypeStruct((1, d), F32)],
        compiler_params=_params(("arbitrary",)), name=name,
    )(*ins)


HALF_FF = DFF // 2


SUB_ROWS = 256


def _sub_blocks(tm):
    sub = SUB_ROWS if tm % SUB_ROWS == 0 else tm
    return [slice(r0, r0 + sub) for r0 in range(0, tm, sub)]


def _ffn_in_swiglu(x_norm, w_in, name, after=(), tm=1024):
    s, d = x_norm.shape
    tm = _tile(s, tm)

    def body(x_ref, wa_ref, wb_ref, *rest):
        ab_ref, u_ref = rest[len(after):]
        for rows in _sub_blocks(tm):
            xv = x_ref[rows, :]
            a = jnp.dot(xv, wa_ref[...], preferred_element_type=F32)
            b = jnp.dot(xv, wb_ref[...], preferred_element_type=F32)
            ab_ref[0, rows, :] = a.astype(ab_ref.dtype)
            ab_ref[1, rows, :] = b.astype(ab_ref.dtype)
            u_ref[rows, :] = (a * _sigmoid(a) * b).astype(u_ref.dtype)

    ab, u = pl.pallas_call(
        body, grid=(2, s // tm),
        in_specs=[pl.BlockSpec((tm, d), lambda j, i: (i, 0)), pl.BlockSpec((d, HALF_FF), lambda j, i: (0, j)),
                  pl.BlockSpec((d, HALF_FF), lambda j, i: (0, 2 + j))] + [ANY] * len(after),
        out_specs=[pl.BlockSpec((2, None, tm, HALF_FF), lambda j, i: (0, j, i, 0)), pl.BlockSpec((tm, HALF_FF), lambda j, i: (i, j))],
        out_shape=[jax.ShapeDtypeStruct((2, 2, s, HALF_FF), BF16), jax.ShapeDtypeStruct((s, DFF), BF16)],
        compiler_params=_params(("parallel", "parallel")), name=name,
    )(x_norm, w_in, w_in, *after)
    return ab.reshape(4, s, HALF_FF), u


def _ffn_out_dx_swiglu(dz, w_out, ab, after, name, tm=1024):
    s, d = dz.shape
    tm = _tile(s, tm)

    def body(dz_ref, w_ref, ab_ref, *rest):
        dab_ref = rest[len(after)]
        for rows in _sub_blocks(tm):
            du = lax.dot_general(dz_ref[rows, :], w_ref[...], (((1,), (1,)), ((), ())), preferred_element_type=F32)
            a = ab_ref[0, rows, :].astype(F32)
            b = ab_ref[1, rows, :].astype(F32)
            sig = _sigmoid(a)
            dab_ref[0, rows, :] = (du * b * (sig * (1.0 + a * (1.0 - sig)))).astype(dab_ref.dtype)
            dab_ref[1, rows, :] = (du * a * sig).astype(dab_ref.dtype)

    halves = pl.BlockSpec((2, None, tm, HALF_FF), lambda j, i: (0, j, i, 0))
    dab = pl.pallas_call(
        body, grid=(2, s // tm),
        in_specs=[pl.BlockSpec((tm, d), lambda j, i: (i, 0)), pl.BlockSpec((HALF_FF, d), lambda j, i: (j, 0)), halves] + [ANY] * len(after),
        out_specs=halves, out_shape=jax.ShapeDtypeStruct((2, 2, s, HALF_FF), BF16),
        compiler_params=_params(("parallel", "parallel")), name=name,
    )(dz, w_out, ab.reshape(2, 2, s, HALF_FF), *after)
    return dab.reshape(4, s, HALF_FF)


def _tri(strict):
    r = lax.broadcasted_iota(jnp.int32, (CHUNK, CHUNK), 0)
    c = lax.broadcasted_iota(jnp.int32, (CHUNK, CHUNK), 1)
    return (r > c).astype(F32) if strict else (r >= c).astype(F32)


def _gla_fwd(pg, wfu, b_f, gnorm, name):
    s = pg.shape[0]
    ts = _tile(s, TS_GLA, CHUNK)
    cpb = ts // CHUNK
    nc = s // CHUNK

    def body(pg_ref, wfu_ref, bf_ref, gn_ref, ya_ref, sp_ref, so_ref, o_ref, st_ref, la_ref, dec_ref, u_ref):
        @pl.when(pl.program_id(0) == 0)
        def _():
            st_ref[...] = jnp.zeros_like(st_ref)

        f = jnp.dot(pg_ref[:, PG_F:PG_W], wfu_ref[...], preferred_element_type=F32) + bf_ref[...]
        la_ref[...] = _log_sigmoid(f) * (1.0 / GATE_TEMP)
        tri = _tri(False)
        chunks = [slice(ci * CHUNK, (ci + 1) * CHUNK) for ci in range(cpb)]
        for ci, rows in enumerate(chunks):
            la = la_ref[rows, :]
            b = jnp.dot(tri, la, precision=HIGHEST, preferred_element_type=F32)
            bend = jnp.sum(la, axis=0, keepdims=True)
            e = jnp.exp(bend - b)
            dec_ref[ci:ci + 1, :] = jnp.exp(bend)
            for hd in range(HEADS):
                k = pg_ref[rows, PG_K + hd * HDK:PG_K + (hd + 1) * HDK]
                v = pg_ref[rows, PG_V + hd * HDV:PG_V + (hd + 1) * HDV]
                kt = (k.astype(F32) * e[:, hd * HDK:(hd + 1) * HDK]).astype(BF16)
                u_ref[ci, hd] = lax.dot_general(v, kt, (((0,), (0,)), ((), ())), preferred_element_type=F32)
        for ci in range(cpb):
            for hd in range(HEADS):
                prev = st_ref[hd]
                sp_ref[ci, hd] = prev
                st = prev * dec_ref[ci:ci + 1, hd * HDK:(hd + 1) * HDK] + u_ref[ci, hd]
                st_ref[hd] = st
                so_ref[ci, hd] = st.astype(so_ref.dtype)
        for ci, rows in enumerate(chunks):
            for hd in range(HEADS):
                vc = slice(hd * HDV, (hd + 1) * HDV)
                q = pg_ref[rows, PG_Q + hd * HDK:PG_Q + (hd + 1) * HDK]
                go = pg_ref[rows, PG_G + hd * HDV:PG_G + (hd + 1) * HDV].astype(F32)
                qs = (q.astype(F32) * Q_SCALE).astype(BF16)
                o = lax.dot_general(qs, so_ref[ci, hd], (((1,), (1,)), ((), ())), preferred_element_type=F32)
                o_ref[rows, vc] = o
                oh, _ = _rms(o)
                ya_ref[rows, vc] = (oh * gn_ref[:, vc] * (go * _sigmoid(go))).astype(ya_ref.dtype)

    return pl.pallas_call(
        body, grid=(s // ts,),
        in_specs=[_rows(ts, PG_W), _fixed((LANE, HEADS * HDK)), _fixed((1, HEADS * HDK)), _fixed((1, HEADS * HDV))],
        out_specs=[_rows(ts, HEADS * HDV), pl.BlockSpec((cpb, HEADS, HDV, HDK), lambda i: (i, 0, 0, 0)),
                   pl.BlockSpec((cpb, HEADS, HDV, HDK), lambda i: (i, 0, 0, 0)), _rows(ts, HEADS * HDV)],
        out_shape=[jax.ShapeDtypeStruct((s, HEADS * HDV), BF16), jax.ShapeDtypeStruct((nc, HEADS, HDV, HDK), F32),
                   jax.ShapeDtypeStruct((nc, HEADS, HDV, HDK), BF16), jax.ShapeDtypeStruct((s, HEADS * HDV), F32)],
        scratch_shapes=[pltpu.VMEM((HEADS, HDV, HDK), F32), pltpu.VMEM((ts, HEADS * HDK), F32),
                        pltpu.VMEM((max(cpb, 8), HEADS * HDK), F32), pltpu.VMEM((cpb, HEADS, HDV, HDK), F32)],
        compiler_params=_params(("arbitrary",)), name=name,
    )(pg, wfu, b_f, gnorm)


def _gla_bwd(pg, sp, so, o, dya, wfu, b_f, gnorm, name):
    s = pg.shape[0]
    ts = _tile(s, TS_GLA, CHUNK)
    cpb = ts // CHUNK
    nblk = s // ts

    def body(pg_ref, sp_ref, so_ref, o_ref, dya_ref, wfu_ref, bf_ref, gn_ref, dpg_ref, dwfu_ref, dbf_ref, dgn_ref,
             dst_ref, la_ref, sg_ref, df_ref, e_ref, ktf_ref, dec_ref, g_ref):
        @pl.when(pl.program_id(0) == 0)
        def _():
            dst_ref[...] = jnp.zeros_like(dst_ref)
            dwfu_ref[...] = jnp.zeros_like(dwfu_ref)
            dbf_ref[...] = jnp.zeros_like(dbf_ref)
            dgn_ref[...] = jnp.zeros_like(dgn_ref)

        flow = pg_ref[:, PG_F:PG_W]
        f = jnp.dot(flow, wfu_ref[...], preferred_element_type=F32) + bf_ref[...]
        la_ref[...] = _log_sigmoid(f) * (1.0 / GATE_TEMP)
        sg_ref[...] = _sigmoid(-f) * (1.0 / GATE_TEMP)
        tri = _tri(False)
        tri_strict = _tri(True)
        chunks = [slice(ci * CHUNK, (ci + 1) * CHUNK) for ci in range(cpb)]
        for ci, rows in enumerate(chunks):
            la = la_ref[rows, :]
            b = jnp.dot(tri, la, precision=HIGHEST, preferred_element_type=F32)
            bend = jnp.sum(la, axis=0, keepdims=True)
            e = jnp.exp(bend - b)
            e_ref[rows, :] = e
            dec = jnp.exp(bend)
            dec_ref[ci:ci + 1, :] = dec
            for hd in range(HEADS):
                kc = slice(hd * HDK, (hd + 1) * HDK)
                vc = slice(hd * HDV, (hd + 1) * HDV)
                q = pg_ref[rows, PG_Q + hd * HDK:PG_Q + (hd + 1) * HDK]
                k = pg_ref[rows, PG_K + hd * HDK:PG_K + (hd + 1) * HDK]
                go = pg_ref[rows, PG_G + hd * HDV:PG_G + (hd + 1) * HDV].astype(F32)
                ktf_ref[rows, kc] = k.astype(F32) * e[:, kc]
                st_b = so_ref[ci, hd]
                qs = (q.astype(F32) * Q_SCALE).astype(BF16)
                oh, r = _rms(o_ref[rows, vc])
                gh = gn_ref[:, vc]
                sig = _sigmoid(go)
                dy = dya_ref[rows, vc].astype(F32)
                don = dy * (go * sig)
                dgn_ref[:, vc] += jnp.sum(don * oh, axis=0, keepdims=True)
                dong = don * gh
                do = (r * (dong - oh * jnp.mean(dong * oh, axis=-1, keepdims=True))).astype(BF16)
                g_ref[ci, hd] = lax.dot_general(do, qs, (((0,), (0,)), ((), ())), preferred_element_type=F32)
                dq = jnp.dot(do, st_b, preferred_element_type=F32) * Q_SCALE
                dpg_ref[rows, PG_Q + hd * HDK:PG_Q + (hd + 1) * HDK] = dq.astype(dpg_ref.dtype)
                dgo = dy * (oh * gh) * (sig * (1.0 + go * (1.0 - sig)))
                dpg_ref[rows, PG_G + hd * HDV:PG_G + (hd + 1) * HDV] = dgo.astype(dpg_ref.dtype)
        for ci in reversed(range(cpb)):
            for hd in range(HEADS):
                dst = dst_ref[hd] + g_ref[ci, hd]
                g_ref[ci, hd] = dst
                dst_ref[hd] = dst * dec_ref[ci:ci + 1, hd * HDK:(hd + 1) * HDK]
        for ci, rows in enumerate(chunks):
            for hd in range(HEADS):
                kc = slice(hd * HDK, (hd + 1) * HDK)
                v = pg_ref[rows, PG_V + hd * HDV:PG_V + (hd + 1) * HDV]
                ktf = ktf_ref[rows, kc]
                dst = g_ref[ci, hd]
                dst_b = dst.astype(BF16)
                dkt = jnp.dot(v, dst_b, preferred_element_type=F32)
                dv = lax.dot_general(ktf.astype(BF16), dst_b, (((1,), (1,)), ((), ())), preferred_element_type=F32)
                dd = jnp.sum(dst * sp_ref[ci, hd], axis=0, keepdims=True)
                dla = jnp.dot(tri_strict, dkt * ktf, precision=HIGHEST, preferred_element_type=F32) + dd * dec_ref[ci:ci + 1, kc]
                df_ref[rows, kc] = dla * sg_ref[rows, kc]
                dpg_ref[rows, PG_K + hd * HDK:PG_K + (hd + 1) * HDK] = (dkt * e_ref[rows, kc]).astype(dpg_ref.dtype)
                dpg_ref[rows, PG_V + hd * HDV:PG_V + (hd + 1) * HDV] = dv.astype(dpg_ref.dtype)
        df = df_ref[...]
        df_b = df.astype(BF16)
        dpg_ref[:, PG_F:PG_W] = lax.dot_general(df_b, wfu_ref[...], (((1,), (1,)), ((), ())), preferred_element_type=F32).astype(dpg_ref.dtype)
        dwfu_ref[...] += lax.dot_general(flow, df_b, (((0,), (0,)), ((), ())), preferred_element_type=F32)
        dbf_ref[...] += jnp.sum(df, axis=0, keepdims=True)

    rev = lambda i: (nblk - 1 - i, 0)
    return pl.pallas_call(
        body, grid=(nblk,),
        in_specs=[pl.BlockSpec((ts, PG_W), rev), pl.BlockSpec((cpb, HEADS, HDV, HDK), lambda i: (nblk - 1 - i, 0, 0, 0)),
                  pl.BlockSpec((cpb, HEADS, HDV, HDK), lambda i: (nblk - 1 - i, 0, 0, 0)), pl.BlockSpec((ts, HEADS * HDV), rev),
                  pl.BlockSpec((ts, HEADS * HDV), rev), _fixed((LANE, HEADS * HDK)), _fixed((1, HEADS * HDK)), _fixed((1, HEADS * HDV))],
        out_specs=[pl.BlockSpec((ts, PG_W), rev), _fixed((LANE, HEADS * HDK)), _fixed((1, HEADS * HDK)), _fixed((1, HEADS * HDV))],
        out_shape=[jax.ShapeDtypeStruct((s, PG_W), BF16), jax.ShapeDtypeStruct((LANE, HEADS * HDK), F32),
                   jax.ShapeDtypeStruct((1, HEADS * HDK), F32), jax.ShapeDtypeStruct((1, HEADS * HDV), F32)],
        scratch_shapes=[pltpu.VMEM((HEADS, HDV, HDK), F32)] + [pltpu.VMEM((ts, HEADS * HDK), F32)] * 5
        + [pltpu.VMEM((max(cpb, 8), HEADS * HDK), F32), pltpu.VMEM((cpb, HEADS, HDV, HDK), F32)],
        compiler_params=_params(("arbitrary",)), name=name,
    )(pg, sp, so, o, dya, wfu, b_f, gnorm)


def _window_sums(ext, sign):
    n = ext.shape[0]
    sums = {1: ext}
    w = 1
    while w < POOL_WINDOWS[-1]:
        sums[2 * w] = sums[w] + pltpu.roll(sums[w], w if sign > 0 else n - w, 0)
        w *= 2
    return [sums[POOL_WINDOWS[g]][:, g * LANE:(g + 1) * LANE] for g in range(len(POOL_WINDOWS))]


def _pool_counts(row0, n):
    pos = (row0 + lax.broadcasted_iota(jnp.int32, (n, 1), 0) + 1).astype(F32)
    return [1.0 / jnp.minimum(pos, float(w)) for w in POOL_WINDOWS]


def _pool_fwd(ppx, w_pool, pool_scale, name):
    s = ppx.shape[0]
    ts = _tile(s, TS_POOL, POOL_HALO)
    hb = ts // POOL_HALO
    pw = len(POOL_WINDOWS) * LANE

    def body(p_ref, halo_ref, w_ref, sc_ref, y_ref, ext_ref):
        i = pl.program_id(0)
        p = p_ref[...].astype(F32)
        ext_ref[0:POOL_HALO, :] = jnp.where(i > 0, halo_ref[...].astype(F32), 0.0)
        ext_ref[POOL_HALO:, :] = p
        sums = _window_sums(ext_ref[...], +1)
        cnt = _pool_counts(i * ts, ts)
        for g in range(len(POOL_WINDOWS)):
            cols = slice(g * LANE, (g + 1) * LANE)
            mixed = sums[g][POOL_HALO:, :] * cnt[g] - p[:, cols]
            y = jnp.dot(mixed.astype(BF16), w_ref[g], preferred_element_type=F32)
            y_ref[:, cols] = (y * sc_ref[:, cols]).astype(y_ref.dtype)

    return pl.pallas_call(
        body, grid=(s // ts,),
        in_specs=[pl.BlockSpec((ts, pw), lambda i: (i, 0)), pl.BlockSpec((POOL_HALO, pw), lambda i: (jnp.maximum(i * hb - 1, 0), 0)),
                  _fixed((len(POOL_WINDOWS), LANE, LANE)), _fixed((1, pw))],
        out_specs=_rows(ts, pw), out_shape=jax.ShapeDtypeStruct((s, pw), BF16),
        scratch_shapes=[pltpu.VMEM((ts + POOL_HALO, pw), F32)],
        compiler_params=_params(("parallel",)), name=name,
    )(ppx, ppx, w_pool, pool_scale)


def _pool_bwd(dyb, ppx, w_pool, pool_scale, name):
    s = ppx.shape[0]
    ts = _tile(s, TS_POOL, POOL_HALO)
    hb = ts // POOL_HALO
    nblk = s // ts
    last_halo = s // POOL_HALO - 1
    ng = len(POOL_WINDOWS)
    pw = ng * LANE

    def body(p_ref, halo_ref, dy_ref, dyn_ref, w_ref, sc_ref, dp_ref, dw_ref, dsc_ref, ext_ref, dext_ref, dm_ref):
        i = pl.program_id(0)

        @pl.when(i == 0)
        def _():
            dw_ref[...] = jnp.zeros_like(dw_ref)
            dsc_ref[...] = jnp.zeros_like(dsc_ref)

        p = p_ref[...].astype(F32)
        ext_ref[0:POOL_HALO, :] = jnp.where(i > 0, halo_ref[...].astype(F32), 0.0)
        ext_ref[POOL_HALO:, :] = p
        sums = _window_sums(ext_ref[...], +1)
        cnt = _pool_counts(i * ts, ts + POOL_HALO)
        sc = sc_ref[...]
        dy = dy_ref[...].astype(F32)
        dyn = jnp.where(i < nblk - 1, dyn_ref[...].astype(F32), 0.0)
        for g in range(ng):
            cols = slice(g * LANE, (g + 1) * LANE)
            wg = w_ref[g]
            mixed = (sums[g][POOL_HALO:, :] * cnt[g][0:ts] - p[:, cols]).astype(BF16)
            ypre = jnp.dot(mixed, wg, preferred_element_type=F32)
            dsc_ref[:, cols] += jnp.sum(dy[:, cols] * ypre, axis=0, keepdims=True)
            dyp = (dy[:, cols] * sc[:, cols]).astype(BF16)
            dypn = (dyn[:, cols] * sc[:, cols]).astype(BF16)
            dw_ref[g] += lax.dot_general(mixed, dyp, (((0,), (0,)), ((), ())), preferred_element_type=F32)
            dm = lax.dot_general(dyp, wg, (((1,), (1,)), ((), ())), preferred_element_type=F32)
            dmn = lax.dot_general(dypn, wg, (((1,), (1,)), ((), ())), preferred_element_type=F32)
            dext_ref[0:ts, cols] = dm * cnt[g][0:ts]
            dext_ref[ts:, cols] = dmn * cnt[g][ts:]
            dm_ref[:, cols] = dm
        lead = _window_sums(dext_ref[...], -1)
        for g in range(ng):
            cols = slice(g * LANE, (g + 1) * LANE)
            dp_ref[:, cols] = (lead[g][0:ts, :] - dm_ref[:, cols]).astype(dp_ref.dtype)

    return pl.pallas_call(
        body, grid=(nblk,),
        in_specs=[pl.BlockSpec((ts, pw), lambda i: (i, 0)), pl.BlockSpec((POOL_HALO, pw), lambda i: (jnp.maximum(i * hb - 1, 0), 0)),
                  pl.BlockSpec((ts, pw), lambda i: (i, 0)), pl.BlockSpec((POOL_HALO, pw), lambda i: (jnp.minimum((i + 1) * hb, last_halo), 0)),
                  _fixed((ng, LANE, LANE)), _fixed((1, pw))],
        out_specs=[_rows(ts, pw), _fixed((ng, LANE, LANE)), _fixed((1, pw))],
        out_shape=[jax.ShapeDtypeStruct((s, pw), BF16), jax.ShapeDtypeStruct((ng, LANE, LANE), F32), jax.ShapeDtypeStruct((1, pw), F32)],
        scratch_shapes=[pltpu.VMEM((ts + POOL_HALO, pw), F32), pltpu.VMEM((ts + POOL_HALO, pw), F32), pltpu.VMEM((ts, pw), F32)],
        compiler_params=_params(("arbitrary",)), name=name,
    )(ppx, ppx, dyb, dyb, w_pool, pool_scale)


def _xattn_fwd(ppx, kv, name):
    s = ppx.shape[0]
    m = kv.shape[0]
    ts = _tile(s, TS_XA, 8)
    xw = XA_HEADS * XA_HD

    def body(q_ref, kv_ref, o_ref):
        for hd in range(XA_HEADS):
            cols = slice(hd * XA_HD, (hd + 1) * XA_HD)
            k = kv_ref[:, hd * XA_HD:(hd + 1) * XA_HD]
            v = kv_ref[:, xw + hd * XA_HD:xw + (hd + 1) * XA_HD]
            sc = lax.dot_general(q_ref[:, cols], k, (((1,), (1,)), ((), ())), preferred_element_type=F32) * XA_SCALE
            ex = jnp.exp(sc - jnp.max(sc, axis=-1, keepdims=True))
            pr = ex * (1.0 / jnp.sum(ex, axis=-1, keepdims=True))
            o_ref[:, cols] = jnp.dot(pr.astype(BF16), v, preferred_element_type=F32).astype(o_ref.dtype)

    return pl.pallas_call(
        body, grid=(s // ts,), in_specs=[pl.BlockSpec((ts, xw), lambda i: (i, 1)), _fixed((m, 2 * xw))],
        out_specs=_rows(ts, xw), out_shape=jax.ShapeDtypeStruct((s, xw), BF16),
        compiler_params=_params(("parallel",)), name=name,
    )(ppx, kv)


def _xattn_bwd(dxc, ppx, kv, name):
    s = ppx.shape[0]
    m = kv.shape[0]
    ts = _tile(s, TS_XA, 8)
    xw = XA_HEADS * XA_HD

    def body(do_ref, q_ref, kv_ref, dq_ref, dkv_ref):
        @pl.when(pl.program_id(0) == 0)
        def _():
            dkv_ref[...] = jnp.zeros_like(dkv_ref)

        for hd in range(XA_HEADS):
            cols = slice(hd * XA_HD, (hd + 1) * XA_HD)
            vcols = slice(xw + hd * XA_HD, xw + (hd + 1) * XA_HD)
            q = q_ref[:, cols]
            k = kv_ref[:, cols]
            v = kv_ref[:, vcols]
            do = do_ref[:, cols]
            sc = lax.dot_general(q, k, (((1,), (1,)), ((), ())), preferred_element_type=F32) * XA_SCALE
            ex = jnp.exp(sc - jnp.max(sc, axis=-1, keepdims=True))
            pr = ex * (1.0 / jnp.sum(ex, axis=-1, keepdims=True))
            dpr = lax.dot_general(do, v, (((1,), (1,)), ((), ())), preferred_element_type=F32)
            dsc = (pr * (dpr - jnp.sum(dpr * pr, axis=-1, keepdims=True)) * XA_SCALE).astype(BF16)
            dq_ref[:, cols] = jnp.dot(dsc, k, preferred_element_type=F32).astype(dq_ref.dtype)
            dkv_ref[:, cols] += lax.dot_general(dsc, q, (((0,), (0,)), ((), ())), preferred_element_type=F32)
            dkv_ref[:, vcols] += lax.dot_general(pr.astype(BF16), do, (((0,), (0,)), ((), ())), preferred_element_type=F32)

    return pl.pallas_call(
        body, grid=(s // ts,), in_specs=[_rows(ts, xw), pl.BlockSpec((ts, xw), lambda i: (i, 1)), _fixed((m, 2 * xw))],
        out_specs=[_rows(ts, xw), _fixed((m, 2 * xw))],
        out_shape=[jax.ShapeDtypeStruct((s, xw), BF16), jax.ShapeDtypeStruct((m, 2 * xw), F32)],
        compiler_params=_params(("arbitrary",)), name=name,
    )(dxc, ppx, kv)


def _resident(shape):
    nd = len(shape)
    return pl.BlockSpec(shape, lambda i: (0,) * nd, pipeline_mode=pl.Buffered(1))


def _mix_in_fwd(h, w_ts, after, name, tm=512):
    s, d = h.shape
    tm = _tile(s, tm)
    n, na = len(w_ts), len(after)

    def body(h_ref, *refs):
        w_refs, o_refs = refs[:n], refs[n + na:]
        for rows in _sub_blocks(tm):
            hv = h_ref[rows, :]
            for w_ref, o_ref in zip(w_refs, o_refs):
                o_ref[rows, :] = lax.dot_general(hv, w_ref[...], (((1,), (1,)), ((), ())), preferred_element_type=F32).astype(o_ref.dtype)

    return pl.pallas_call(
        body, grid=(s // tm,), in_specs=[_rows(tm, d)] + [_resident(w.shape) for w in w_ts] + [ANY] * na,
        out_specs=[_rows(tm, w.shape[0]) for w in w_ts],
        out_shape=[jax.ShapeDtypeStruct((s, w.shape[0]), BF16) for w in w_ts],
        compiler_params=_params(("parallel",)), name=name,
    )(h, *w_ts, *after)


def _mix_tail_fwd(ya_in, yb_in, xc, pgt, w_ups, w_o, x, g_post, g_next, after, name, tm=512):
    s, d = x.shape
    tm = _tile(s, tm)
    na = len(after)
    branch_ins = (ya_in, yb_in, xc)

    def body(a_ref, b_ref, c_ref, gt_ref, wa_ref, wb_ref, wc_ref, wo_ref, x_ref, gp_ref, gn_ref, *rest):
        ya_ref, yb_ref, yc_ref, m_ref, y_ref, xo_ref, h_ref = rest[na:]
        for rows in _sub_blocks(tm):
            merged = None
            for j, (in_ref, w_ref, out_ref) in enumerate(((a_ref, wa_ref, ya_ref), (b_ref, wb_ref, yb_ref), (c_ref, wc_ref, yc_ref))):
                yj = jnp.dot(in_ref[rows, :], w_ref[...], preferred_element_type=F32)
                out_ref[rows, :] = yj.astype(out_ref.dtype)
                part = _sigmoid(gt_ref[rows, j * D:(j + 1) * D].astype(F32)) * yj
                merged = part if merged is None else merged + part
            merged_b = merged.astype(m_ref.dtype)
            m_ref[rows, :] = merged_b
            y = jnp.dot(merged_b, wo_ref[...], preferred_element_type=F32)
            y_ref[rows, :] = y
            yh, _ = _rms(y)
            xn = x_ref[rows, :] + yh * gp_ref[...]
            xo_ref[rows, :] = xn
            xh, _ = _rms(xn)
            h_ref[rows, :] = (xh * gn_ref[...]).astype(h_ref.dtype)

    bf = lambda: jax.ShapeDtypeStruct((s, d), BF16)
    f32 = lambda: jax.ShapeDtypeStruct((s, d), F32)
    return pl.pallas_call(
        body, grid=(s // tm,),
        in_specs=[_rows(tm, a.shape[1]) for a in branch_ins] + [_rows(tm, 3 * d)] + [_resident(w.shape) for w in w_ups]
        + [_resident(w_o.shape), _rows(tm, d), _fixed((1, d)), _fixed((1, d))] + [ANY] * na,
        out_specs=[_rows(tm, d)] * 7,
        out_shape=[bf(), bf(), bf(), bf(), f32(), f32(), bf()],
        compiler_params=_params(("parallel",)), name=name,
    )(*branch_ins, pgt, *w_ups, w_o, x, g_post, g_next, *after)


def _mix_tail_bwd(dy, pgt, ys, w_ups, w_o, after, name, tm=512):
    s, d = dy.shape
    tm = _tile(s, tm)
    na = len(after)
    widths = [w.shape[0] for w in w_ups]

    def body(dy_ref, gt_ref, ya_ref, yb_ref, yc_ref, wa_ref, wb_ref, wc_ref, wo_ref, *rest):
        dya_ref, dyb_ref, dyc_ref, dgt_ref, da_ref, db_ref, dc_ref = rest[na:]
        nt = (((1,), (1,)), ((), ()))
        for rows in _sub_blocks(tm):
            dm = lax.dot_general(dy_ref[rows, :], wo_ref[...], nt, preferred_element_type=F32)
            for j, (y_ref, dyj_ref, w_ref, din_ref) in enumerate(((ya_ref, dya_ref, wa_ref, da_ref), (yb_ref, dyb_ref, wb_ref, db_ref),
                                                                   (yc_ref, dyc_ref, wc_ref, dc_ref))):
                sig = _sigmoid(gt_ref[rows, j * D:(j + 1) * D].astype(F32))
                dyj = (dm * sig).astype(dyj_ref.dtype)
                dyj_ref[rows, :] = dyj
                dgt_ref[rows, j * D:(j + 1) * D] = (dm * y_ref[rows, :].astype(F32) * sig * (1.0 - sig)).astype(dgt_ref.dtype)
                din_ref[rows, :] = lax.dot_general(dyj, w_ref[...], nt, preferred_element_type=F32).astype(din_ref.dtype)

    bf = lambda w: jax.ShapeDtypeStruct((s, w), BF16)
    return pl.pallas_call(
        body, grid=(s // tm,),
        in_specs=[_rows(tm, d), _rows(tm, 3 * d)] + [_rows(tm, d)] * 3 + [_resident(w.shape) for w in w_ups] + [_resident(w_o.shape)]
        + [ANY] * na,
        out_specs=[_rows(tm, d)] * 3 + [_rows(tm, 3 * d)] + [_rows(tm, w) for w in widths],
        out_shape=[bf(d), bf(d), bf(d), bf(3 * d)] + [bf(w) for w in widths],
        compiler_params=_params(("parallel",)), name=name,
    )(dy, pgt, *ys, *w_ups, w_o, *after)


def _adam_math(w, g, m, v):
    mn = ADAM_B1 * m + (1.0 - ADAM_B1) * g
    vn = ADAM_B2 * v + (1.0 - ADAM_B2) * (g * g)
    m_hat = mn / (1.0 - ADAM_B1 ** ADAM_STEP)
    v_hat = vn / (1.0 - ADAM_B2 ** ADAM_STEP)
    return -ADAM_LR * (m_hat / (jnp.sqrt(v_hat) + ADAM_EPS) + ADAM_WD * w), mn, vn


def _adamw(w, g, m, v, name):
    r, c = w.shape[-2:]
    tr, tc = _block_of(r, c, cap=512 if r % 16 == 0 else 256)

    def spec(a):
        if a.ndim == 2:
            return pl.BlockSpec((tr, tc), lambda i, j: (i, j))
        return pl.BlockSpec((None, tr, tc), lambda i, j: (0, i, j))

    def body(w_ref, g_ref, m_ref, v_ref, d_ref, mo_ref, vo_ref):
        d_ref[...], mo_ref[...], vo_ref[...] = _adam_math(w_ref[...], g_ref[...], m_ref[...], v_ref[...])

    return pl.pallas_call(
        body, grid=(r // tr, c // tc), in_specs=[spec(a) for a in (w, g, m, v)], out_specs=[spec(w)] * 3,
        out_shape=[jax.ShapeDtypeStruct(w.shape, F32)] * 3, compiler_params=_params(("parallel", "parallel")), name=name,
    )(w, g, m, v)


ANY = pl.BlockSpec(memory_space=pl.ANY)


def _place():
    x, y, c = lax.axis_index("x"), lax.axis_index("y"), lax.axis_index("c")
    chips = [(1 - x, y), (x, 1 - y), (1 - x, 1 - y)]
    return x, y, c, chips


def _by_cols(rows):
    return rows % 32 != 0 and rows != 16


def _half_of(ref, lead, c):
    r, cols = ref.shape[-2:]
    if _by_cols(r):
        return ref.at[(*lead, slice(None), pl.ds(pl.multiple_of(c * (cols // 2), LANE), cols // 2))]
    return ref.at[(*lead, pl.ds(pl.multiple_of(c * (r // 2), 8), r // 2))]


def _half_shape(shape):
    r, cols = shape[-2:]
    return shape[:-2] + ((r, cols // 2) if _by_cols(r) else (r // 2, cols))


def _block_of(r, cols, cap=256):
    if r % 16 == 0:
        return _tile(r, cap, 16), cols
    return r, _tile(cols, cap)


def _place_shard(shard, chip_arr, out_dtype, name, after=()):
    _, r, cols = shard.shape
    tr, tc = _block_of(r, cols)

    def body(chip_ref, s_ref, *rest):
        o_ref = rest[len(after)]
        o_ref[...] = s_ref[...].astype(o_ref.dtype)

    return pl.pallas_call(
        body,
        grid_spec=pltpu.PrefetchScalarGridSpec(
            num_scalar_prefetch=1, grid=(r // tr, cols // tc),
            in_specs=[pl.BlockSpec((None, tr, tc), lambda i, j, chip_ref: (0, i, j))] + [ANY] * len(after),
            out_specs=pl.BlockSpec((None, tr, tc), lambda i, j, chip_ref: (chip_ref[0], i, j))),
        out_shape=jax.ShapeDtypeStruct((4, r, cols), out_dtype),
        compiler_params=_params(("parallel", "parallel")), name=name,
    )(chip_arr, shard, *after)


HBM = pl.BlockSpec(memory_space=pltpu.HBM)
SEM = pl.BlockSpec(memory_space=pltpu.SEMAPHORE)
EFFECT = pltpu.SideEffectType.DATAFLOW_SIDE_EFFECTING


def _in_hbm(arrays):
    return [pltpu.with_memory_space_constraint(a, pltpu.HBM) for a in arrays]


def _gather_start(bufs, after, name):
    n, na = len(bufs), len(after)

    def body(*refs):
        send_sem, recv_sem = refs[n + na], refs[n + na + 1]
        outs = refs[n + na + 2:2 * n + na + 2]
        token = refs[2 * n + na + 2]
        x, y, c, chips = _place()
        me = 2 * x + y
        for p, chip in enumerate(chips):
            for w in range(n):
                block = _half_of(outs[w], (me,), c)
                pltpu.make_async_remote_copy(
                    src_ref=block, dst_ref=block, send_sem=send_sem, recv_sem=recv_sem,
                    device_id=(*chip, c), device_id_type=MESH).start()
        token[...] = jnp.zeros_like(token)

    out = pl.pallas_call(
        body, name=name, in_specs=[HBM] * n + [ANY] * na,
        out_specs=[SEM, SEM] + [HBM] * n + [pl.BlockSpec(memory_space=pltpu.VMEM)],
        out_shape=[pltpu.SemaphoreType.DMA(()), pltpu.SemaphoreType.DMA(())]
        + [pltpu.HBM(a.shape, a.dtype) for a in bufs] + [jax.ShapeDtypeStruct((8, LANE), F32)],
        input_output_aliases={w: w + 2 for w in range(n)},
        compiler_params=pltpu.CompilerParams(has_side_effects=EFFECT),
    )(*_in_hbm(bufs), *after)
    return out[0], out[1], list(out[2:2 + n]), out[2 + n]


def _gather_pass(bufs, send_sem, recv_sem, after, name):
    n, na = len(bufs), len(after)

    def body(*refs):
        send1, recv1 = refs[n], refs[n + 1]
        send2, recv2 = refs[n + 2 + na], refs[n + 3 + na]
        outs = refs[n + 4 + na:2 * n + 4 + na]
        x, y, c, chips = _place()
        me = 2 * x + y
        arrivals = [(w, px, py) for px, py in chips for w in range(n)]
        for w, px, py in arrivals:
            first = pltpu.make_async_remote_copy(
                src_ref=_half_of(outs[w], (me,), c), dst_ref=_half_of(outs[w], (2 * px + py,), c), send_sem=send1, recv_sem=recv1,
                device_id=(px, py, c), device_id_type=MESH)
            first.wait_send()
            first.wait_recv()
        for w, px, py in arrivals:
            arrived = _half_of(outs[w], (2 * px + py,), c)
            pltpu.make_async_remote_copy(
                src_ref=arrived, dst_ref=arrived, send_sem=send2, recv_sem=recv2,
                device_id=(x, y, 1 - c), device_id_type=MESH).start()

    out = pl.pallas_call(
        body, name=name, in_specs=[HBM] * n + [SEM, SEM] + [ANY] * na,
        out_specs=[SEM, SEM] + [HBM] * n,
        out_shape=[pltpu.SemaphoreType.DMA(()), pltpu.SemaphoreType.DMA(())] + [pltpu.HBM(a.shape, a.dtype) for a in bufs],
        input_output_aliases={w: w + 2 for w in range(n)},
        compiler_params=pltpu.CompilerParams(has_side_effects=EFFECT),
    )(*bufs, send_sem, recv_sem, *after)
    return out[0], out[1], list(out[2:])


def _gather_finish(bufs, send_sem, recv_sem, after, name):
    n, na = len(bufs), len(after)

    def body(*refs):
        send2, recv2 = refs[n], refs[n + 1]
        outs = refs[n + 2 + na:2 * n + 2 + na]
        x, y, c, chips = _place()
        for p, (px, py) in enumerate(chips):
            for w in range(n):
                passed = pltpu.make_async_remote_copy(
                    src_ref=_half_of(outs[w], (2 * px + py,), c), dst_ref=_half_of(outs[w], (2 * px + py,), 1 - c),
                    send_sem=send2, recv_sem=recv2, device_id=(x, y, 1 - c), device_id_type=MESH)
                passed.wait_send()
                passed.wait_recv()

    out = pl.pallas_call(
        body, name=name, in_specs=[HBM] * n + [SEM, SEM] + [ANY] * na, out_specs=[HBM] * n,
        out_shape=[pltpu.HBM(a.shape, a.dtype) for a in bufs],
        input_output_aliases={w: w for w in range(n)},
        compiler_params=pltpu.CompilerParams(has_side_effects=EFFECT),
    )(*bufs, send_sem, recv_sem, *after)
    return list(out)


def _pair_exchange(grads, name):
    n = len(grads)

    def body(*refs):
        ins, outs = refs[:n], refs[n:2 * n]
        send_sem, recv_sem = refs[2 * n:]
        x, y, c, _ = _place()
        copies = []
        for w in range(n):
            copies.append(pltpu.make_async_remote_copy(
                src_ref=_half_of(ins[w], (slice(None),), 1 - c), dst_ref=outs[w], send_sem=send_sem.at[w], recv_sem=recv_sem.at[w],
                device_id=(x, y, 1 - c), device_id_type=MESH))
        for cp in copies:
            cp.start()
        for cp in copies:
            cp.wait()

    return pl.pallas_call(
        body, in_specs=[ANY] * n, out_specs=[ANY] * n,
        out_shape=[jax.ShapeDtypeStruct(_half_shape(a.shape), a.dtype) for a in grads],
        scratch_shapes=[pltpu.SemaphoreType.DMA((n,))] * 2,
        compiler_params=pltpu.CompilerParams(has_side_effects=True), name=name,
    )(*grads)


def _pair_exchange_start(grads, after, name):
    n, na = len(grads), len(after)
    lands = [lax.empty(_half_shape(a.shape), a.dtype) for a in grads]

    def body(*refs):
        send_sem, recv_sem = refs[2 * n + na], refs[2 * n + na + 1]
        srcs = refs[2 * n + na + 2:3 * n + na + 2]
        dsts = refs[3 * n + na + 2:4 * n + na + 2]
        token = refs[4 * n + na + 2]
        x, y, c, _ = _place()
        for w in range(n):
            pltpu.make_async_remote_copy(
                src_ref=_half_of(srcs[w], (slice(None),), 1 - c), dst_ref=dsts[w], send_sem=send_sem, recv_sem=recv_sem,
                device_id=(x, y, 1 - c), device_id_type=MESH).start()
        token[...] = jnp.zeros_like(token)

    out = pl.pallas_call(
        body, name=name, in_specs=[HBM] * (2 * n) + [ANY] * na,
        out_specs=[SEM, SEM] + [HBM] * (2 * n) + [pl.BlockSpec(memory_space=pltpu.VMEM)],
        out_shape=[pltpu.SemaphoreType.DMA(()), pltpu.SemaphoreType.DMA(())]
        + [pltpu.HBM(a.shape, a.dtype) for a in grads + lands] + [jax.ShapeDtypeStruct((8, LANE), F32)],
        input_output_aliases={w: w + 2 for w in range(2 * n)},
        compiler_params=pltpu.CompilerParams(has_side_effects=EFFECT),
    )(*_in_hbm(grads), *_in_hbm(lands), *after)
    return out[0], out[1], list(out[2:2 + n]), list(out[2 + n:2 + 2 * n]), out[2 + 2 * n]


def _pair_exchange_finish(grads, lands, send_sem, recv_sem, after, name):
    n, na = len(grads), len(after)

    def body(*refs):
        send, recv = refs[2 * n], refs[2 * n + 1]
        srcs = refs[2 * n + 2 + na:3 * n + 2 + na]
        dsts = refs[3 * n + 2 + na:4 * n + 2 + na]
        x, y, c, _ = _place()
        for w in range(n):
            copy = pltpu.make_async_remote_copy(
                src_ref=_half_of(srcs[w], (slice(None),), 1 - c), dst_ref=dsts[w], send_sem=send, recv_sem=recv,
                device_id=(x, y, 1 - c), device_id_type=MESH)
            copy.wait_send()
            copy.wait_recv()

    out = pl.pallas_call(
        body, name=name, in_specs=[HBM] * (2 * n) + [SEM, SEM] + [ANY] * na, out_specs=[HBM] * (2 * n),
        out_shape=[pltpu.HBM(a.shape, a.dtype) for a in grads + lands],
        input_output_aliases={w: w for w in range(2 * n)},
        compiler_params=pltpu.CompilerParams(has_side_effects=EFFECT),
    )(*grads, *lands, send_sem, recv_sem, *after)
    return list(out[:n]), list(out[n:])


def _pair_sum(g, got, c_arr, name):
    _, r, cols = g.shape
    hr, hc = _half_shape((r, cols))
    tr, tc = _block_of(hr, hc)
    nbr, nbc = hr // tr, hc // tc
    by_cols = _by_cols(r)

    def body(c_ref, g_ref, got_ref, o_ref):
        o_ref[...] = (g_ref[...].astype(F32) + got_ref[...].astype(F32)).astype(o_ref.dtype)

    def mine(j, i, k, c_ref):
        return (j, i, c_ref[0] * nbc + k) if by_cols else (j, c_ref[0] * nbr + i, k)

    return pl.pallas_call(
        body,
        grid_spec=pltpu.PrefetchScalarGridSpec(
            num_scalar_prefetch=1, grid=(4, nbr, nbc),
            in_specs=[pl.BlockSpec((None, tr, tc), mine),
                      pl.BlockSpec((None, tr, tc), lambda j, i, k, c_ref: (j, i, k))],
            out_specs=pl.BlockSpec((None, tr, tc), lambda j, i, k, c_ref: (j, i, k))),
        out_shape=jax.ShapeDtypeStruct((4, hr, hc), BF16),
        compiler_params=_params(("parallel", "parallel", "parallel")), name=name,
    )(c_arr, *_in_hbm([g, got]))


def _chip_exchange_start(parts, after, name):
    n, na = len(parts), len(after)
    lands = [lax.empty((3,) + a.shape[1:], a.dtype) for a in parts]

    def body(*refs):
        send_sem, recv_sem = refs[2 * n + na], refs[2 * n + na + 1]
        srcs = refs[2 * n + na + 2:3 * n + na + 2]
        dsts = refs[3 * n + na + 2:4 * n + na + 2]
        token = refs[4 * n + na + 2]
        x, y, c, chips = _place()
        for p, (px, py) in enumerate(chips):
            for w in range(n):
                pltpu.make_async_remote_copy(
                    src_ref=srcs[w].at[2 * px + py], dst_ref=dsts[w].at[p], send_sem=send_sem, recv_sem=recv_sem,
                    device_id=(px, py, c), device_id_type=MESH).start()
        token[...] = jnp.zeros_like(token)

    out = pl.pallas_call(
        body, name=name, in_specs=[HBM] * (2 * n) + [ANY] * na,
        out_specs=[SEM, SEM] + [HBM] * (2 * n) + [pl.BlockSpec(memory_space=pltpu.VMEM)],
        out_shape=[pltpu.SemaphoreType.DMA(()), pltpu.SemaphoreType.DMA(())]
        + [pltpu.HBM(a.shape, a.dtype) for a in parts + lands] + [jax.ShapeDtypeStruct((8, LANE), F32)],
        input_output_aliases={w: w + 2 for w in range(2 * n)},
        compiler_params=pltpu.CompilerParams(has_side_effects=EFFECT),
    )(*_in_hbm(parts), *_in_hbm(lands), *after)
    return out[0], out[1], list(out[2:2 + n]), list(out[2 + n:2 + 2 * n]), out[2 + 2 * n]


def _chip_exchange_finish(parts, lands, send_sem, recv_sem, after, name):
    n, na = len(parts), len(after)

    def body(*refs):
        send, recv = refs[2 * n], refs[2 * n + 1]
        srcs = refs[2 * n + 2 + na:3 * n + 2 + na]
        dsts = refs[3 * n + 2 + na:4 * n + 2 + na]
        x, y, c, chips = _place()
        for p, (px, py) in enumerate(chips):
            for w in range(n):
                copy = pltpu.make_async_remote_copy(
                    src_ref=srcs[w].at[2 * px + py], dst_ref=dsts[w].at[p], send_sem=send, recv_sem=recv,
                    device_id=(px, py, c), device_id_type=MESH)
                copy.wait_send()
                copy.wait_recv()

    out = pl.pallas_call(
        body, name=name, in_specs=[HBM] * (2 * n) + [SEM, SEM] + [ANY] * na, out_specs=[HBM] * (2 * n),
        out_shape=[pltpu.HBM(a.shape, a.dtype) for a in parts + lands],
        input_output_aliases={w: w for w in range(2 * n)},
        compiler_params=pltpu.CompilerParams(has_side_effects=EFFECT),
    )(*parts, *lands, send_sem, recv_sem, *after)
    return list(out[:n]), list(out[n:])


def _chip_sum(part, got, place_arr, name):
    _, hr, hc = part.shape
    by_cols = _by_cols(hr)
    tr, tc = _block_of(hr, hc)
    nbr, nbc = hr // tr, hc // tc

    def body(place_ref, p_ref, got_ref, o_ref):
        acc = p_ref[...].astype(F32)
        for p in range(3):
            acc = acc + got_ref[p].astype(F32)
        o_ref[...] = acc

    def mine(i, k, place_ref):
        return (i, place_ref[1] * nbc + k) if by_cols else (place_ref[1] * nbr + i, k)

    return pl.pallas_call(
        body,
        grid_spec=pltpu.PrefetchScalarGridSpec(
            num_scalar_prefetch=1, grid=(nbr, nbc),
            in_specs=[pl.BlockSpec((None, tr, tc), lambda i, k, place_ref: (place_ref[0], i, k)),
                      pl.BlockSpec((3, tr, tc), lambda i, k, place_ref: (0, i, k))],
            out_specs=pl.BlockSpec((tr, tc), mine)),
        out_shape=jax.ShapeDtypeStruct((hr, 2 * hc) if by_cols else (2 * hr, hc), F32),
        compiler_params=_params(("parallel", "parallel")), name=name,
    )(place_arr, *_in_hbm([part, got]))


def _pair_join_start(bufs, name):
    n = len(bufs)

    def body(*refs):
        send_sem, recv_sem = refs[n], refs[n + 1]
        outs = refs[n + 2:2 * n + 2]
        token = refs[2 * n + 2]
        x, y, c, _ = _place()
        for w in range(n):
            block = _half_of(outs[w], (), c)
            pltpu.make_async_remote_copy(
                src_ref=block, dst_ref=block, send_sem=send_sem, recv_sem=recv_sem,
                device_id=(x, y, 1 - c), device_id_type=MESH).start()
        token[...] = jnp.zeros_like(token)

    out = pl.pallas_call(
        body, name=name, in_specs=[HBM] * n,
        out_specs=[SEM, SEM] + [HBM] * n + [pl.BlockSpec(memory_space=pltpu.VMEM)],
        out_shape=[pltpu.SemaphoreType.DMA(()), pltpu.SemaphoreType.DMA(())]
        + [pltpu.HBM(a.shape, a.dtype) for a in bufs] + [jax.ShapeDtypeStruct((8, LANE), F32)],
        input_output_aliases={w: w + 2 for w in range(n)},
        compiler_params=pltpu.CompilerParams(has_side_effects=EFFECT),
    )(*_in_hbm(bufs))
    return out[0], out[1], list(out[2:2 + n]), out[2 + n]


def _pair_join_finish(bufs, send_sem, recv_sem, after, name):
    n, na = len(bufs), len(after)

    def body(*refs):
        send, recv = refs[n], refs[n + 1]
        outs = refs[n + 2 + na:2 * n + 2 + na]
        x, y, c, _ = _place()
        for w in range(n):
            copy = pltpu.make_async_remote_copy(
                src_ref=_half_of(outs[w], (), c), dst_ref=_half_of(outs[w], (), 1 - c), send_sem=send, recv_sem=recv,
                device_id=(x, y, 1 - c), device_id_type=MESH)
            copy.wait_send()
            copy.wait_recv()

    out = pl.pallas_call(
        body, name=name, in_specs=[HBM] * n + [SEM, SEM] + [ANY] * na, out_specs=[HBM] * n,
        out_shape=[pltpu.HBM(a.shape, a.dtype) for a in bufs],
        input_output_aliases={w: w for w in range(n)},
        compiler_params=pltpu.CompilerParams(has_side_effects=EFFECT),
    )(*bufs, send_sem, recv_sem, *after)
    return list(out)


SMALL = ("ffn1_pre_g", "ffn1_post_g", "mix_pre_g", "gla_norm_g", "mem_norm_g", "mix_post_g", "ffn2_pre_g", "ffn2_post_g", "final_g",
         "b_f", "pool_scale", "w_pool", "w_fu")
N_GAINS = 9
SMALL_PACKS = ((16, D), (24, 512), (4 * LANE, LANE))
W_FU_ROW = 8


LOSS_ROW = 2


def _all_sum_small(gs, loss, name, after=()):
    ins = [gs[n] for n in SMALL[:N_GAINS]] + [gs["b_f"], gs["pool_scale"], gs["w_fu_pad"], gs["w_pool"].reshape(4 * LANE, LANE), loss]

    def body(*refs):
        gain_refs = refs[:N_GAINS]
        bf_ref, ps_ref, wfu_ref, wp_ref, loss_ref = refs[N_GAINS:N_GAINS + 5]
        outs = refs[N_GAINS + 5 + len(after):N_GAINS + 8 + len(after)]
        mine_a, mine_b, all_a, all_b, all_c, send_sems, recv_sems = refs[N_GAINS + 8 + len(after):]
        mine_a[...] = jnp.zeros_like(mine_a)
        for i, ref in enumerate(gain_refs):
            mine_a[i:i + 1, :] = ref[...]
        mine_b[...] = jnp.zeros_like(mine_b)
        mine_b[0:1, :] = bf_ref[...]
        mine_b[1:2, :] = ps_ref[...]
        mine_b[LOSS_ROW:LOSS_ROW + 1, 0:LANE] = loss_ref[0:1, :]
        mine_b[W_FU_ROW:W_FU_ROW + GATE_RANK, :] = wfu_ref[0:GATE_RANK, :]
        packs = ((mine_a, all_a), (mine_b, all_b), (wp_ref, all_c))
        x, y, c, chips = _place()
        me, sibling = (x, y, c), (x, y, 1 - c)

        def copy(t, k, block, to, own=False):
            px, py, pc = block
            slot = packs[t][1].at[4 * px + 2 * py + pc]
            return pltpu.make_async_remote_copy(
                src_ref=packs[t][0] if own else slot, dst_ref=slot,
                send_sem=send_sems.at[t, k], recv_sem=recv_sems.at[t, k], device_id=to, device_id_type=MESH)

        started = []
        for t, (mine, everyone) in enumerate(packs):
            everyone[4 * x + 2 * y + c] = mine[...]
            started.append(copy(t, 0, me, sibling, own=True))
            started += [copy(t, 1 + j, me, (*chip, c), own=True) for j, chip in enumerate(chips)]
        for cp in started:
            cp.start()
        passed = []
        for j, chip in enumerate(chips):
            for t in range(len(packs)):
                copy(t, 1 + j, (*chip, c), me).wait_recv()
                fwd = copy(t, 4 + j, (*chip, c), sibling)
                fwd.start()
                passed.append(fwd)
        for t in range(len(packs)):
            copy(t, 0, sibling, me).wait_recv()
            for j, chip in enumerate(chips):
                copy(t, 4 + j, (*chip, 1 - c), me).wait_recv()
        for cp in started + passed:
            cp.wait_send()
        for (_, everyone), o_ref in zip(packs, outs):
            acc = everyone[0]
            for k in range(1, 8):
                acc = acc + everyone[k]
            o_ref[...] = acc

    vmem = pl.BlockSpec(memory_space=pltpu.VMEM)
    return pl.pallas_call(
        body, in_specs=[vmem] * len(ins) + [ANY] * len(after), out_specs=[vmem] * 3,
        out_shape=[jax.ShapeDtypeStruct(shape, F32) for shape in SMALL_PACKS],
        scratch_shapes=[pltpu.VMEM(SMALL_PACKS[0], F32), pltpu.VMEM(SMALL_PACKS[1], F32)]
        + [pltpu.VMEM((8,) + shape, F32) for shape in SMALL_PACKS]
        + [pltpu.SemaphoreType.DMA((3, 7)), pltpu.SemaphoreType.DMA((3, 7))],
        compiler_params=pltpu.CompilerParams(has_side_effects=True, vmem_limit_bytes=VMEM_LIMIT), name=name,
    )(*ins, *after)


def _adamw_small(sums, params, chip_arr, name):
    flat = [a for n in SMALL for a in params[n]]

    def body(chip_ref, a_ref, b_ref, c_ref, *refs):
        ins, outs = refs[:len(flat)], refs[len(flat):]
        for i, n in enumerate(SMALL):
            w_ref, m_ref, v_ref = ins[3 * i:3 * i + 3]
            g_ref, d_ref, mo_ref, vo_ref = outs[4 * i:4 * i + 4]
            if n == "w_pool":
                pieces = [((0, k), c_ref[k * LANE:(k + 1) * LANE, :]) for k in range(4)]
            elif n == "w_fu":
                mine = pl.ds(pl.multiple_of(chip_ref[0] * LANE, LANE), LANE)
                pieces = [((0,), b_ref[W_FU_ROW:W_FU_ROW + GATE_RANK, mine])]
            elif n == "b_f":
                pieces = [((), b_ref[0:1, :])]
            elif n == "pool_scale":
                pieces = [((), b_ref[1:2, :])]
            else:
                pieces = [((), a_ref[i:i + 1, :])]
            for at, g in pieces:
                d, mn, vn = _adam_math(w_ref[at], g, m_ref[at], v_ref[at])
                g_ref[at] = g
                d_ref[at] = d
                mo_ref[at] = mn
                vo_ref[at] = vn

    def whole(shape):
        return pl.BlockSpec(shape, lambda i, chip_ref: (0,) * len(shape))

    out = pl.pallas_call(
        body,
        grid_spec=pltpu.PrefetchScalarGridSpec(
            num_scalar_prefetch=1, grid=(1,),
            in_specs=[whole(a.shape) for a in list(sums) + flat],
            out_specs=[whole(params[n][0].shape) for n in SMALL for _ in range(4)]),
        out_shape=[jax.ShapeDtypeStruct(params[n][0].shape, F32) for n in SMALL for _ in range(4)],
        compiler_params=_params(("arbitrary",)), name=name,
    )(chip_arr, *sums, *flat)
    return {n: tuple(out[4 * i:4 * i + 4]) for i, n in enumerate(SMALL)}


def _ffn_bwd(dz, x_norm, ab, u, w_in, w_out, x, g_pre, dres, tag, emit, advance, after=(), post=None):
    dw_out = _mm(u, dz, ta=True, out_dtype=BF16, tm=1408, tk=2048, after=after, name=tag + "_out_dw")
    behind = emit(tag + "_w_out", dw_out)
    dab = _ffn_out_dx_swiglu(dz, w_out, ab, behind, name=tag + "_out_dx")
    behind = advance((dab,))
    dw_in = _mm(x_norm, dab, ta=True, out_dtype=BF16, tm=512, tk=4096, shards=4, after=behind, name=tag + "_in_dw")
    behind = emit(tag + "_w_in", dw_in)
    out = _mm_rms_bwd([(dab, w_in)], x, g_pre, dres, after=behind, post=post, name=tag + "_in_dx")
    return (*out, advance((out[0],)))


def _local_step(x, mem, target, small, gather, emit, advance):
    behind = gather("start", "ffn1i", ())
    behind = gather("start", "ffn1o", behind)
    h1 = _norm_fwd(x, small["ffn1_pre_g"], BF16, name="ffn1_pre", after=behind)
    gather("pass", "ffn1i", (h1,))
    big = gather("finish", "ffn1i", ())
    behind = gather("start", "mixa", (big["ffn1_w_in"],))
    behind = gather("start", "mixb", behind)
    behind = gather("start", "ffn2", behind)
    ab1, u1 = _ffn_in_swiglu(h1, big["ffn1_w_in"], name="ffn1_in", after=behind)
    gather("pass", "ffn1o", (ab1,))
    big.update(gather("finish", "ffn1o", ()))
    behind = gather("pass", "mixa", (u1,))
    f1, x1, h = _mm_resid_norm(u1, big["ffn1_w_out"], x, small["ffn1_post_g"], 0.5, small["mix_pre_g"], name="ffn1_out", after=behind)
    big.update(gather("finish", "mixa", (h,)))
    small = dict(small, w_fu_pad=big["w_fu_pad"])
    behind = gather("pass", "mixb", (h,))
    pg, ppx, pgt = _mix_in_fwd(h, [big["w_gla_t"], big["w_px_t"], big["w_gates_t"]], behind, name="mix_in")
    big.update(gather("finish", "mixb", (pgt,)))
    mem_n = _norm_fwd(mem, small["mem_norm_g"], BF16, name="mem_norm")
    kv = _mm(mem_n, big["w_mem_kv"], out_dtype=BF16, name="mem_kv")
    ya_in, sp, so, o_gla = _gla_fwd(pg, small["w_fu_pad"], small["b_f"], small["gla_norm_g"], name="gla_fwd")
    yb_in = _pool_fwd(ppx, small["w_pool_b"], small["pool_scale"], name="pool_fwd")
    xc = _xattn_fwd(ppx, kv, name="xattn_fwd")
    behind = gather("pass", "ffn2", (xc,))
    w_ups = [big["w_up_gla"], big["w_up_pool"], big["w_up_xattn"]]
    ya, yb, yc, merged, ymix, x2, h2 = _mix_tail_fwd(ya_in, yb_in, xc, pgt, w_ups, big["w_o"], x1, small["mix_post_g"],
                                                     small["ffn2_pre_g"], behind, name="mix_tail")
    big.update(gather("finish", "ffn2", (h2,)))
    ab2, u2 = _ffn_in_swiglu(h2, big["ffn2_w_in"], name="ffn2_in")
    gs = {}
    dz2, dx3, gs["ffn2_post_g"], gs["final_g"], loss = _ffn_out_loss(u2, big["ffn2_w_out"], x2, small["ffn2_post_g"], 0.5,
                                                                    small["final_g"], target, name="ffn2_out_loss")
    dx2, gs["ffn2_pre_g"], dy, gs["mix_post_g"], behind = _ffn_bwd(
        dz2, h2, ab2, u2, big["ffn2_w_in"], big["ffn2_w_out"], x2, small["ffn2_pre_g"], dx3, "ffn2", emit, advance,
        post=(ymix, small["mix_post_g"], 1.0))
    emit("w_o", _mm(merged, dy, ta=True, out_dtype=BF16, tm=512, tk=4096, after=behind, name="mix_out_dw"))
    dya, dyb, dyc, dgt, dya_in, dyb_in, dxc = _mix_tail_bwd(dy, pgt, (ya, yb, yc), w_ups, big["w_o"], (), name="mix_tail_bwd")
    emit("w_up_gla", _mm(ya_in, dya, ta=True, out_dtype=BF16, tm=512, tk=4096, name="up_gla_dw"))
    emit("w_up_pool", _mm(yb_in, dyb, ta=True, out_dtype=BF16, tm=512, tk=4096, shards=4, name="up_pool_dw"))
    emit("w_up_xattn", _mm(xc, dyc, ta=True, out_dtype=BF16, tm=512, tk=4096, shards=4, name="up_xattn_dw"))
    dpg, gs["w_fu_pad"], gs["b_f"], gs["gla_norm_g"] = _gla_bwd(pg, sp, so, o_gla, dya_in, small["w_fu_pad"], small["b_f"], small["gla_norm_g"], name="gla_bwd")
    dp, gs["w_pool"], gs["pool_scale"] = _pool_bwd(dyb_in, ppx, small["w_pool_b"], small["pool_scale"], name="pool_bwd")
    dxq, dkv = _xattn_bwd(dxc, ppx, kv, name="xattn_bwd")
    dkv = dkv.astype(BF16)
    emit("w_mem_kv", _mm(mem_n, dkv, ta=True, out_dtype=BF16, name="mem_kv_dw"))
    dmem_n = _mm(dkv, big["w_mem_kv"], tb=True, name="mem_kv_dx")
    _, gs["mem_norm_g"] = _rms_bwd(mem, small["mem_norm_g"], [dmem_n], None, 1.0, BF16, name="mem_norm_bwd")
    emit("w_gla", _mm(dpg, h, ta=True, out_dtype=BF16, tm=640, tk=4096, name="mix_in_gla_dw"))
    emit("w_p", _mm(dp, h, ta=True, out_dtype=BF16, tm=512, tk=4096, name="mix_in_p_dw"))
    emit("w_xq", _mm(dxq, h, ta=True, out_dtype=BF16, tm=512, tk=4096, name="mix_in_xq_dw"))
    behind = emit("w_gates", _mm(dgt, h, ta=True, out_dtype=BF16, tm=512, tk=4096, name="mix_in_gates_dw"))
    pairs = [(dpg, big["w_gla_t"]), (dp, big["w_p_t"]), (dxq, big["w_xq_t"]), (dgt, big["w_gates_t"])]
    dx1, gs["mix_pre_g"], dz1, gs["ffn1_post_g"] = _mm_rms_bwd(pairs, x1, small["mix_pre_g"], dx2, after=behind,
                                                               post=(f1, small["ffn1_post_g"], 0.5), tm=256, name="mix_in_dx")
    behind = advance((dx1,))
    dx0, gs["ffn1_pre_g"], _ = _ffn_bwd(dz1, h1, ab1, u1, big["ffn1_w_in"], big["ffn1_w_out"], x, small["ffn1_pre_g"], dx1,
                                        "ffn1", emit, advance, after=behind)
    return loss, dx0, gs


BIG = ("ffn1_w_in", "ffn1_w_out", "w_in", "w_mem_kv", "w_up_gla", "w_up_pool", "w_up_xattn", "w_o", "ffn2_w_in", "ffn2_w_out")
COL_SHARDED = ("ffn1_w_in", "w_in", "w_up_pool", "w_up_xattn", "ffn2_w_in")
GATHER_GROUPS = {"ffn1i": ("ffn1_w_in",), "ffn1o": ("ffn1_w_out",), "mixa": ("w_in", "w_fu"),
                 "mixb": ("w_mem_kv", "w_up_gla", "w_up_pool", "w_up_xattn", "w_o"), "ffn2": ("ffn2_w_in", "ffn2_w_out")}
REDUCE_GROUPS = {"ffn2": ("ffn2_w_out", "ffn2_w_in"),
                 "mix": ("w_o", "w_up_gla", "w_up_pool", "w_up_xattn", "w_mem_kv", "w_gla", "w_p", "w_xq", "w_gates"),
                 "ffn1_out": ("ffn1_w_out",),
                 "ffn1_in": ("ffn1_w_in",)}
REDUCE_LAST = "ffn1_in"
GAINS = ("ffn1_pre_g", "ffn1_post_g", "mix_pre_g", "gla_norm_g", "mem_norm_g", "mix_post_g", "ffn2_pre_g", "ffn2_post_g", "final_g")
WEIGHTS = ("ffn1_pre_g", "ffn1_w_in", "ffn1_w_out", "ffn1_post_g", "mix_pre_g", "w_in", "w_fu", "b_f", "gla_norm_g", "w_pool",
           "pool_scale", "mem_norm_g", "w_mem_kv", "w_up_gla", "w_up_pool", "w_up_xattn", "w_o", "mix_post_g", "ffn2_pre_g",
           "ffn2_w_in", "ffn2_w_out", "ffn2_post_g", "final_g")
IN_GLA, IN_F, IN_PX, IN_GATES, IN_END = 0, 3072, 3088, 4112, 7184
def _cols_from_shards(g):
    return jnp.transpose(g, (1, 0, 2)).reshape(g.shape[1], 4 * g.shape[2])


def _laid_end_to_end(pieces, rows):
    out, start = None, 0
    for p in pieces:
        padded = jnp.pad(p, ((start, rows - start - p.shape[0]), (0, 0)))
        out = padded if out is None else out + padded
        start += p.shape[0]
    return out


def _rows_of_blocks(g, lo, hi, rows=None):
    q = g.shape[1]
    cuts = [(j, max(lo, j * q) - j * q, min(hi, (j + 1) * q) - j * q) for j in range(g.shape[0])]
    return _laid_end_to_end([g[j, a:b] for j, a, b in cuts if a < b], rows or hi - lo)


def _blocks_of_rows(parts, blocks):
    q = sum(p.shape[0] for p in parts) // blocks
    out = []
    for j in range(blocks):
        pieces, start = [], 0
        for p in parts:
            a, b = max(j * q, start), min((j + 1) * q, start + p.shape[0])
            if a < b:
                pieces.append(p[a - start:b - start])
            start += p.shape[0]
        out.append(_laid_end_to_end(pieces, q))
    return jnp.stack(out)


def kernel(x, mem, ffn1_pre_g, ffn1_w_in, ffn1_w_out, ffn1_post_g, mix_pre_g, w_in, w_fu, b_f, gla_norm_g, w_pool, pool_scale, mem_norm_g, w_mem_kv, w_up_gla, w_up_pool, w_up_xattn, w_o, mix_post_g, ffn2_pre_g, ffn2_w_in, ffn2_w_out, ffn2_post_g, final_g, loss_target, m_ffn1_pre_g, m_ffn1_w_in, m_ffn1_w_out, m_ffn1_post_g, m_mix_pre_g, m_w_in, m_w_fu, m_b_f, m_gla_norm_g, m_w_pool, m_pool_scale, m_mem_norm_g, m_w_mem_kv, m_w_up_gla, m_w_up_pool, m_w_up_xattn, m_w_o, m_mix_post_g, m_ffn2_pre_g, m_ffn2_w_in, m_ffn2_w_out, m_ffn2_post_g, m_final_g, v_ffn1_pre_g, v_ffn1_w_in, v_ffn1_w_out, v_ffn1_post_g, v_mix_pre_g, v_w_in, v_w_fu, v_b_f, v_gla_norm_g, v_w_pool, v_pool_scale, v_mem_norm_g, v_w_mem_kv, v_w_up_gla, v_w_up_pool, v_w_up_xattn, v_w_o, v_mix_post_g, v_ffn2_pre_g, v_ffn2_w_in, v_ffn2_w_out, v_ffn2_post_g, v_final_g):
    args = dict(locals())
    w = {n: args[n][0] for n in WEIGHTS}
    m = {n: args["m_" + n][0] for n in WEIGHTS}
    v = {n: args["v_" + n][0] for n in WEIGHTS}
    xi, yi, ci = lax.axis_index("x"), lax.axis_index("y"), lax.axis_index("c")
    chip = 2 * xi + yi

    c_arr = jnp.reshape(ci, (1,)).astype(jnp.int32)
    chip_arr = jnp.reshape(chip, (1,)).astype(jnp.int32)
    place_arr = jnp.stack([chip, ci]).astype(jnp.int32)
    w_in_t = []
    shard_of = {n: args[n] for n in BIG if n != "w_in"}
    shard_of["w_fu"] = args["w_fu"]
    placed, inflight = {}, {}

    def place(names, after):
        for n in names:
            if n not in placed:
                placed[n] = _place_shard(shard_of[n], chip_arr, F32 if n == "w_fu" else BF16, name="place_" + n, after=after)

    def relayout(names, gathered):
        out = {}
        for n, g in zip(names, gathered):
            if n == "w_fu":
                w_fu_full = _cols_from_shards(g)
                out["w_fu_pad"] = jnp.concatenate([w_fu_full, jnp.zeros((LANE - GATE_RANK, 512), F32)], axis=0).astype(BF16)
            elif n == "w_in":
                out["w_gla_t"] = _rows_of_blocks(g, IN_GLA, IN_PX, rows=PG_W)
                out["w_px_t"] = _rows_of_blocks(g, IN_PX, IN_GATES)
                out["w_p_t"] = _rows_of_blocks(g, IN_PX, IN_PX + 512)
                out["w_xq_t"] = _rows_of_blocks(g, IN_PX + 512, IN_GATES)
                out["w_gates_t"] = _rows_of_blocks(g, IN_GATES, IN_END)
            else:
                out[n] = _cols_from_shards(g) if n in COL_SHARDED else g.reshape(4 * g.shape[1], g.shape[2])
        return out

    def gather(op, group, after):
        names = GATHER_GROUPS[group]
        if op == "start":
            place(names, ())
            inflight[group] = _gather_start([placed[n] for n in names], after, name="gather_" + group + "_start")
            behind = (inflight[group][3],)
            if group == "ffn1o":
                tied = lax.optimization_barrier((behind, tuple(args[k] for k in ("w_in", "m_w_in", "v_w_in"))))[1]
                w_in_t.extend(jnp.transpose(a[0]) for a in tied)
                shard_of["w_in"] = w_in_t[0][None]
                place(shard_of, behind)
            return behind
        if op == "pass":
            if group == "ffn1i":
                not_started = [n for g in GATHER_GROUPS if g not in inflight for n in GATHER_GROUPS[g]]
                after = tuple(after) + tuple(w_in_t[1:]) + tuple(placed[n] for n in not_started)
            send, recv, bufs, _ = inflight[group]
            inflight[group] = _gather_pass(bufs, send, recv, after, name="gather_" + group + "_pass")
            return (inflight[group][2][0],)
        send, recv, bufs = inflight.pop(group)
        return relayout(names, _gather_finish(bufs, send, recv, after, name="gather_" + group + "_finish"))

    small = {n: w[n].reshape(1, D) for n in GAINS}
    small["b_f"] = w["b_f"].reshape(1, 512)
    small["pool_scale"] = w["pool_scale"].reshape(1, 512)
    small["w_pool_b"] = w["w_pool"].astype(BF16)

    pending, crossing, travelling = {}, {}, {}

    def emit(name, grad):
        pending[name] = grad
        group = next((g for g, names in REDUCE_GROUPS.items() if name == names[-1]), None)
        if group is None:
            return ()
        gb = {n: pending.pop(n) for n in REDUCE_GROUPS[group]}
        if group == "mix":
            gb["w_in"] = _blocks_of_rows([gb.pop("w_gla")[0:IN_PX], gb.pop("w_p"), gb.pop("w_xq"), gb.pop("w_gates")], 4)
        names = list(gb)
        contrib = [gb[n] if n in COL_SHARDED else gb[n].reshape(4, gb[n].shape[0] // 4, gb[n].shape[1]) for n in names]
        if group == REDUCE_LAST:
            from_sibling = _pair_exchange(contrib, name="grads_" + group + "_pair_exchange")
            return over_chips(group, names, contrib, from_sibling)
        send, recv, contrib, lands, token = _pair_exchange_start(contrib, (), name="grads_" + group + "_pair_start")
        crossing[group] = (names, contrib, lands, send, recv)
        return (token,)

    def over_chips(group, names, contrib, from_sibling):
        pair = [_pair_sum(g, got, c_arr, name="grads_pair_sum_" + n) for n, g, got in zip(names, contrib, from_sibling)]
        send, recv, pair, lands, token = _chip_exchange_start(pair, (), name="grads_" + group + "_chip_start")
        travelling[group] = (names, send, recv, pair, lands)
        return (token,)

    def advance(after):
        behind = ()
        for group in list(crossing):
            names, contrib, lands, send, recv = crossing.pop(group)
            contrib, from_sibling = _pair_exchange_finish(contrib, lands, send, recv, after, name="grads_" + group + "_pair_finish")
            behind = over_chips(group, names, contrib, from_sibling)
        return behind

    loss, grad_x, gs = _local_step(x[0], mem[0], loss_target[0], small, gather, emit, advance)

    halves = {}
    for group, (names, send, recv, pair, lands) in travelling.items():
        pair, from_chips = _chip_exchange_finish(pair, lands, send, recv, (grad_x,), name="grads_" + group + "_chip_finish")
        for n, p, got in zip(names, pair, from_chips):
            halves[n] = _chip_sum(p, got, place_arr, name="grads_chip_sum_" + n)
    send, recv, joining, token = _pair_join_start([halves[n] for n in BIG], name="grads_pair_join_start")
    small_sums = _all_sum_small(gs, loss, name="sum_small_grads", after=(token,))
    loss = small_sums[1][LOSS_ROW, 0]
    reduced = dict(zip(BIG, _pair_join_finish(joining, send, recv, (small_sums[0],), name="grads_pair_join_finish")))

    grads, delta, new_m, new_v = {}, {}, {}, {}
    for n in BIG:
        if n == "w_in":
            updated = _adamw(w_in_t[0], reduced[n], w_in_t[1], w_in_t[2], name="adamw_" + n)
            grads[n] = jnp.transpose(reduced[n])[None]
            delta[n], new_m[n], new_v[n] = (jnp.transpose(a)[None] for a in updated)
            continue
        grads[n] = reduced[n][None]
        delta[n], new_m[n], new_v[n] = _adamw(args[n], reduced[n], args["m_" + n], args["v_" + n], name="adamw_" + n)
    small_params = {n: (args[n], args["m_" + n], args["v_" + n]) for n in SMALL}
    for n, (g, d, mn, vn) in _adamw_small(small_sums, small_params, chip_arr, name="adamw_small").items():
        grads[n], delta[n], new_m[n], new_v[n] = g, d, mn, vn

    outs = [loss, grad_x[None]]
    for group in (grads, delta, new_m, new_v):
        outs += [group[n] for n in WEIGHTS]
    return tuple(outs)
```

```python
import jax
import jax.numpy as jnp
from jax import lax
from jax.experimental import pallas as pl
from jax.experimental.pallas import tpu as pltpu

F32 = jnp.float32
BF16 = jnp.bfloat16
MESH = pl.DeviceIdType.MESH
HIGHEST = lax.Precision.HIGHEST

D = 1024
DFF = 2816
CHUNK = 64
HEADS = 4
HDK = 128
HDV = 256
GATE_TEMP = 16.0
POOL_WINDOWS = (2, 4, 8, 16)
POOL_HALO = 16
XA_HEADS = 4
XA_HD = 128
EPS = 1e-6
Q_SCALE = HDK ** -0.5
XA_SCALE = XA_HD ** -0.5
PG_Q, PG_K, PG_V, PG_G, PG_F, PG_W = 0, 512, 1024, 2048, 3072, 3200
GATE_RANK = 16
ADAM_LR, ADAM_B1, ADAM_B2, ADAM_EPS, ADAM_WD, ADAM_STEP = 0.001, 0.9, 0.999, 1e-08, 0.01, 10

VMEM_LIMIT = 48 * 1024 * 1024
LANE = 128
TS_ROW = 512
TS_GLA = 512
TS_POOL = 512
TS_XA = 512


def _params(sem):
    return pltpu.CompilerParams(dimension_semantics=sem, vmem_limit_bytes=VMEM_LIMIT)


def _tile(n, cap, unit=LANE):
    if n <= cap:
        return n
    best = None
    for t in range(unit, cap + 1, unit):
        if n % t == 0:
            best = t
    assert best is not None, (n, cap)
    return best


def _sigmoid(x):
    return 0.5 * jnp.tanh(0.5 * x) + 0.5


def _log_sigmoid(x):
    return jnp.minimum(x, 0.0) - jnp.log(1.0 + jnp.exp(-jnp.abs(x)))


def _rms(x):
    r = lax.rsqrt(jnp.mean(x * x, axis=-1, keepdims=True) + EPS)
    return x * r, r


def _rows(ts, w):
    return pl.BlockSpec((ts, w), lambda i: (i, 0))


def _fixed(shape):
    nd = len(shape)
    return pl.BlockSpec(shape, lambda i: (0,) * nd)


def _mm(a, b, *, ta=False, tb=False, out_dtype=F32, tm=2048, tn=1024, tk=1024, shards=1, after=(), name):
    b_blocked = b.ndim == 3
    assert not (b_blocked and tb)
    m, kdim = (a.shape[1], a.shape[0]) if ta else a.shape
    if b_blocked:
        n, tn = b.shape[0] * b.shape[2], b.shape[2]
        assert b.shape[1] == kdim and shards in (1, b.shape[0])
    else:
        n = b.shape[0] if tb else b.shape[1]
        assert (b.shape[1] if tb else b.shape[0]) == kdim, (a.shape, b.shape, ta, tb)
        tn = n // shards if shards > 1 else _tile(n, tn)
    tm = _tile(m, tm)
    tk = _tile(kdim, tk)
    nk = kdim // tk
    dims = (((0 if ta else 1,), (1 if tb else 0,)), ((), ()))

    def body(a_ref, b_ref, *rest):
        o_ref, *acc = rest[len(after):]
        part = lax.dot_general(a_ref[...], b_ref[...], dims, preferred_element_type=F32)
        if nk == 1:
            o_ref[...] = part.astype(o_ref.dtype)
            return
        acc_ref, = acc
        k = pl.program_id(2)

        @pl.when(k == 0)
        def _():
            acc_ref[...] = part

        @pl.when(k > 0)
        def _():
            acc_ref[...] += part

        @pl.when(k == nk - 1)
        def _():
            o_ref[...] = acc_ref[...].astype(o_ref.dtype)

    a_spec = pl.BlockSpec((tk, tm), lambda i, j, k: (k, i)) if ta else pl.BlockSpec((tm, tk), lambda i, j, k: (i, k))
    if b_blocked:
        b_spec = pl.BlockSpec((None, tk, tn), lambda i, j, k: (j, k, 0))
    else:
        b_spec = pl.BlockSpec((tn, tk), lambda i, j, k: (j, k)) if tb else pl.BlockSpec((tk, tn), lambda i, j, k: (k, j))
    if shards > 1:
        out_shape = jax.ShapeDtypeStruct((shards, m, tn), out_dtype)
        o_spec = pl.BlockSpec((None, tm, tn), lambda i, j, k: (j, i, 0))
    else:
        out_shape = jax.ShapeDtypeStruct((m, n), out_dtype)
        o_spec = pl.BlockSpec((tm, tn), lambda i, j, k: (i, j))
    return pl.pallas_call(
        body, grid=(m // tm, n // tn, nk), in_specs=[a_spec, b_spec] + [ANY] * len(after), out_specs=o_spec, out_shape=out_shape,
        scratch_shapes=[pltpu.VMEM((tm, tn), F32)] if nk > 1 else [],
        compiler_params=_params(("parallel", "parallel", "arbitrary")), name=name,
    )(a, b, *after)


def _norm_fwd(x, g, out_dtype, name, after=()):
    s, d = x.shape
    ts = _tile(s, TS_ROW, 8)

    def body(x_ref, g_ref, *rest):
        o_ref = rest[len(after)]
        xh, _ = _rms(x_ref[...])
        o_ref[...] = (xh * g_ref[...]).astype(o_ref.dtype)

    return pl.pallas_call(
        body, grid=(s // ts,), in_specs=[_rows(ts, d), _fixed((1, d))] + [ANY] * len(after), out_specs=_rows(ts, d),
        out_shape=jax.ShapeDtypeStruct((s, d), out_dtype), compiler_params=_params(("parallel",)), name=name,
    )(x, g, *after)


def _mm_resid_norm(a, w, x, g_post, alpha, g_next, name, after=(), tm=512):
    s, kdim = a.shape
    d = w.shape[1]
    tm = _tile(s, tm)
    with_h = g_next is not None
    na = len(after)

    def body(a_ref, w_ref, x_ref, gp_ref, *rest):
        rest = rest[int(with_h) + na:] if not with_h else rest[:1] + rest[1 + na:]
        for rows in _sub_blocks(tm):
            f = jnp.dot(a_ref[rows, :], w_ref[...], preferred_element_type=F32)
            fh, _ = _rms(f)
            xn = x_ref[rows, :] + alpha * (fh * gp_ref[...])
            if with_h:
                gn_ref, f_ref, xo_ref, h_ref = rest
                xh, _ = _rms(xn)
                h_ref[rows, :] = (xh * gn_ref[...]).astype(h_ref.dtype)
            else:
                f_ref, xo_ref = rest
            f_ref[rows, :] = f
            xo_ref[rows, :] = xn

    ins = [a, w, x, g_post] + ([g_next] if with_h else []) + list(after)
    in_specs = [_rows(tm, kdim), _fixed((kdim, d)), _rows(tm, d), _fixed((1, d))] + ([_fixed((1, d))] if with_h else []) + [ANY] * na
    out_shape = [jax.ShapeDtypeStruct((s, d), F32)] * 2 + ([jax.ShapeDtypeStruct((s, d), BF16)] if with_h else [])
    out = pl.pallas_call(
        body, grid=(s // tm,), in_specs=in_specs, out_specs=[_rows(tm, d)] * len(out_shape), out_shape=out_shape,
        compiler_params=_params(("parallel",)), name=name,
    )(*ins)
    return (out[0], out[1], out[2]) if with_h else (out[0], out[1], None)


def _mm_rms_bwd(pairs, x, g, dres, name, after=(), post=None, tm=512):
    s, d = x.shape
    tm = _tile(s, tm)
    n, na = len(pairs), len(after)

    def body(*refs):
        a_refs, w_refs = refs[0:2 * n:2], refs[1:2 * n:2]
        x_ref, g_ref, dres_ref = refs[2 * n:2 * n + 3]
        if post is not None:
            f_ref, gp_ref = refs[2 * n + 3:2 * n + 5]
            dx_ref, dg_ref, df_ref, dgp_ref = refs[2 * n + 5 + na:]
        else:
            dx_ref, dg_ref = refs[2 * n + 3 + na:]
        @pl.when(pl.program_id(0) == 0)
        def _():
            dg_ref[...] = jnp.zeros_like(dg_ref)
            if post is not None:
                dgp_ref[...] = jnp.zeros_like(dgp_ref)

        for rows in _sub_blocks(tm):
            dy = None
            for a_ref, w_ref in zip(a_refs, w_refs):
                if len(a_ref.shape) == 3:
                    tkb = a_ref.shape[2]
                    parts = [lax.dot_general(a_ref[q, rows, :], w_ref[:, q * tkb:(q + 1) * tkb], (((1,), (1,)), ((), ())),
                                             preferred_element_type=F32) for q in range(a_ref.shape[0])]
                else:
                    parts = [jnp.dot(a_ref[rows, :], w_ref[...], preferred_element_type=F32)]
                for part in parts:
                    dy = part if dy is None else dy + part
            xh, r = _rms(x_ref[rows, :])
            dg_ref[...] += jnp.sum(dy * xh, axis=0, keepdims=True)
            dyg = dy * g_ref[...]
            dx = r * (dyg - xh * jnp.mean(dyg * xh, axis=-1, keepdims=True)) + dres_ref[rows, :]
            dx_ref[rows, :] = dx
            if post is not None:
                fh, rf = _rms(f_ref[rows, :])
                dz = dx * post[2]
                dgp_ref[...] += jnp.sum(dz * fh, axis=0, keepdims=True)
                dzg = dz * gp_ref[...]
                df_ref[rows, :] = (rf * (dzg - fh * jnp.mean(dzg * fh, axis=-1, keepdims=True))).astype(df_ref.dtype)

    ins, in_specs = [], []
    for a_arr, w_arr in pairs:
        ins += [a_arr, w_arr]
        if a_arr.ndim == 3:
            in_specs.append(pl.BlockSpec((a_arr.shape[0], tm, a_arr.shape[2]), lambda i: (0, i, 0)))
        else:
            in_specs.append(_rows(tm, a_arr.shape[1]))
        in_specs.append(pl.BlockSpec(w_arr.shape, lambda i: (0, 0), pipeline_mode=pl.Buffered(1)))
    with_post = post is not None
    return pl.pallas_call(
        body, grid=(s // tm,),
        in_specs=in_specs + [_rows(tm, d), _fixed((1, d)), _rows(tm, d)] + ([_rows(tm, d), _fixed((1, d))] if with_post else [])
        + [ANY] * na,
        out_specs=[_rows(tm, d), _fixed((1, d))] + ([_rows(tm, d), _fixed((1, d))] if with_post else []),
        out_shape=[jax.ShapeDtypeStruct((s, d), F32), jax.ShapeDtypeStruct((1, d), F32)]
        + ([jax.ShapeDtypeStruct((s, d), BF16), jax.ShapeDtypeStruct((1, d), F32)] if with_post else []),
        compiler_params=_params(("arbitrary",)), name=name,
    )(*ins, x, g, dres, *(post[:2] if with_post else ()), *after)


def _ffn_out_loss(u, w_out, x, g_post, alpha, g_final, target, name, tm=512):
    s, kdim = u.shape
    d = w_out.shape[1]
    tm = _tile(s, tm)

    def body(u_ref, w_ref, x_ref, gp_ref, gf_ref, t_ref, df_ref, dx_ref, dgp_ref, dgf_ref, loss_ref):
        @pl.when(pl.program_id(0) == 0)
        def _():
            dgp_ref[...] = jnp.zeros_like(dgp_ref)
            dgf_ref[...] = jnp.zeros_like(dgf_ref)
            loss_ref[...] = jnp.zeros_like(loss_ref)

        for rows in _sub_blocks(tm):
            f = jnp.dot(u_ref[rows, :], w_ref[...], preferred_element_type=F32)
            fh, rf = _rms(f)
            xn = x_ref[rows, :] + alpha * (fh * gp_ref[...])
            xh, rx = _rms(xn)
            gf = gf_ref[...]
            diff = xh * gf - t_ref[rows, :]
            sq = jnp.sum(diff * diff, axis=1, keepdims=True)
            loss_ref[...] += (0.5 / d) * jnp.sum(sq, axis=0, keepdims=True)
            dy = diff * (1.0 / d)
            dgf_ref[...] += jnp.sum(dy * xh, axis=0, keepdims=True)
            dyg = dy * gf
            dxn = rx * (dyg - xh * jnp.mean(dyg * xh, axis=-1, keepdims=True))
            dx_ref[rows, :] = dxn
            dz = dxn * alpha
            dgp_ref[...] += jnp.sum(dz * fh, axis=0, keepdims=True)
            dzg = dz * gp_ref[...]
            df_ref[rows, :] = (rf * (dzg - fh * jnp.mean(dzg * fh, axis=-1, keepdims=True))).astype(df_ref.dtype)

    return pl.pallas_call(
        body, grid=(s // tm,),
        in_specs=[_rows(tm, kdim), _resident(w_out.shape), _rows(tm, d), _fixed((1, d)), _fixed((1, d)), _rows(tm, d)],
        out_specs=[_rows(tm, d), _rows(tm, d), _fixed((1, d)), _fixed((1, d)), _fixed((8, LANE))],
        out_shape=[jax.ShapeDtypeStruct((s, d), BF16), jax.ShapeDtypeStruct((s, d), F32), jax.ShapeDtypeStruct((1, d), F32),
                   jax.ShapeDtypeStruct((1, d), F32), jax.ShapeDtypeStruct((8, LANE), F32)],
        compiler_params=_params(("arbitrary",)), name=name,
    )(u, w_out, x, g_post, g_final, target)


def _rms_bwd(x, g, dys, dres, alpha, out_dtype, name, after=()):
    s, d = x.shape
    ts = _tile(s, TS_ROW, 8)
    ndy = len(dys)
    with_res = dres is not None

    def body(x_ref, g_ref, *rest):
        dy_refs = rest[:ndy]
        rest = rest[ndy:]
        if with_res:
            dres_ref = rest[0]
        dx_ref, dg_ref = rest[int(with_res) + len(after):]
        xh, r = _rms(x_ref[...])
        dy = dy_refs[0][...].astype(F32)
        for ref in dy_refs[1:]:
            dy = dy + ref[...].astype(F32)
        dy = dy * alpha

        @pl.when(pl.program_id(0) == 0)
        def _():
            dg_ref[...] = jnp.zeros_like(dg_ref)

        dg_ref[...] += jnp.sum(dy * xh, axis=0, keepdims=True)
        dyg = dy * g_ref[...]
        dx = r * (dyg - xh * jnp.mean(dyg * xh, axis=-1, keepdims=True))
        if with_res:
            dx = dx + dres_ref[...]
        dx_ref[...] = dx.astype(dx_ref.dtype)

    ins = [x, g] + list(dys) + ([dres] if with_res else []) + list(after)
    in_specs = [_rows(ts, d), _fixed((1, d))] + [_rows(ts, d)] * (ndy + int(with_res)) + [ANY] * len(after)
    return pl.pallas_call(
        body, grid=(s // ts,), in_specs=in_specs, out_specs=[_rows(ts, d), _fixed((1, d))],
        out_shape=[jax.ShapeDtypeStruct((s, d), out_dtype), jax.ShapeDtypeStruct((1, d), F32)],
        compiler_params=_params(("arbitrary",)), name=name,
    )(*ins)


HALF_FF = DFF // 2


SUB_ROWS = 256


def _sub_blocks(tm):
    sub = SUB_ROWS if tm % SUB_ROWS == 0 else tm
    return [slice(r0, r0 + sub) for r0 in range(0, tm, sub)]


def _ffn_in_swiglu(x_norm, w_in, name, after=(), tm=1024):
    s, d = x_norm.shape
    tm = _tile(s, tm)

    def body(x_ref, wa_ref, wb_ref, *rest):
        ab_ref, u_ref = rest[len(after):]
        for rows in _sub_blocks(tm):
            xv = x_ref[rows, :]
            a = jnp.dot(xv, wa_ref[...], preferred_element_type=F32)
            b = jnp.dot(xv, wb_ref[...], preferred_element_type=F32)
            ab_ref[0, rows, :] = a.astype(ab_ref.dtype)
            ab_ref[1, rows, :] = b.astype(ab_ref.dtype)
            u_ref[rows, :] = (a * _sigmoid(a) * b).astype(u_ref.dtype)

    ab, u = pl.pallas_call(
        body, grid=(2, s // tm),
        in_specs=[pl.BlockSpec((tm, d), lambda j, i: (i, 0)), pl.BlockSpec((d, HALF_FF), lambda j, i: (0, j)),
                  pl.BlockSpec((d, HALF_FF), lambda j, i: (0, 2 + j))] + [ANY] * len(after),
        out_specs=[pl.BlockSpec((2, None, tm, HALF_FF), lambda j, i: (0, j, i, 0)), pl.BlockSpec((tm, HALF_FF), lambda j, i: (i, j))],
        out_shape=[jax.ShapeDtypeStruct((2, 2, s, HALF_FF), BF16), jax.ShapeDtypeStruct((s, DFF), BF16)],
        compiler_params=_params(("parallel", "parallel")), name=name,
    )(x_norm, w_in, w_in, *after)
    return ab.reshape(4, s, HALF_FF), u


def _ffn_out_dx_swiglu(dz, w_out, ab, after, name, tm=1024):
    s, d = dz.shape
    tm = _tile(s, tm)

    def body(dz_ref, w_ref, ab_ref, *rest):
        dab_ref = rest[len(after)]
        for rows in _sub_blocks(tm):
            du = lax.dot_general(dz_ref[rows, :], w_ref[...], (((1,), (1,)), ((), ())), preferred_element_type=F32)
            a = ab_ref[0, rows, :].astype(F32)
            b = ab_ref[1, rows, :].astype(F32)
            sig = _sigmoid(a)
            dab_ref[0, rows, :] = (du * b * (sig * (1.0 + a * (1.0 - sig)))).astype(dab_ref.dtype)
            dab_ref[1, rows, :] = (du * a * sig).astype(dab_ref.dtype)

    halves = pl.BlockSpec((2, None, tm, HALF_FF), lambda j, i: (0, j, i, 0))
    dab = pl.pallas_call(
        body, grid=(2, s // tm),
        in_specs=[pl.BlockSpec((tm, d), lambda j, i: (i, 0)), pl.BlockSpec((HALF_FF, d), lambda j, i: (j, 0)), halves] + [ANY] * len(after),
        out_specs=halves, out_shape=jax.ShapeDtypeStruct((2, 2, s, HALF_FF), BF16),
        compiler_params=_params(("parallel", "parallel")), name=name,
    )(dz, w_out, ab.reshape(2, 2, s, HALF_FF), *after)
    return dab.reshape(4, s, HALF_FF)


def _tri(strict):
    r = lax.broadcasted_iota(jnp.int32, (CHUNK, CHUNK), 0)
    c = lax.broadcasted_iota(jnp.int32, (CHUNK, CHUNK), 1)
    return (r > c).astype(F32) if strict else (r >= c).astype(F32)


def _gla_fwd(pg, wfu, b_f, gnorm, name):
    s = pg.shape[0]
    ts = _tile(s, TS_GLA, CHUNK)
    cpb = ts // CHUNK
    nc = s // CHUNK

    def body(pg_ref, wfu_ref, bf_ref, gn_ref, ya_ref, sp_ref, so_ref, o_ref, st_ref, la_ref, dec_ref, u_ref):
        @pl.when(pl.program_id(0) == 0)
        def _():
            st_ref[...] = jnp.zeros_like(st_ref)

        f = jnp.dot(pg_ref[:, PG_F:PG_W], wfu_ref[...], preferred_element_type=F32) + bf_ref[...]
        la_ref[...] = _log_sigmoid(f) * (1.0 / GATE_TEMP)
        tri = _tri(False)
        chunks = [slice(ci * CHUNK, (ci + 1) * CHUNK) for ci in range(cpb)]
        for ci, rows in enumerate(chunks):
            la = la_ref[rows, :]
            b = jnp.dot(tri, la, precision=HIGHEST, preferred_element_type=F32)
            bend = jnp.sum(la, axis=0, keepdims=True)
            e = jnp.exp(bend - b)
            dec_ref[ci:ci + 1, :] = jnp.exp(bend)
            for hd in range(HEADS):
                k = pg_ref[rows, PG_K + hd * HDK:PG_K + (hd + 1) * HDK]
                v = pg_ref[rows, PG_V + hd * HDV:PG_V + (hd + 1) * HDV]
                kt = (k.astype(F32) * e[:, hd * HDK:(hd + 1) * HDK]).astype(BF16)
                u_ref[ci, hd] = lax.dot_general(v, kt, (((0,), (0,)), ((), ())), preferred_element_type=F32)
        for ci in range(cpb):
            for hd in range(HEADS):
                prev = st_ref[hd]
                sp_ref[ci, hd] = prev
                st = prev * dec_ref[ci:ci + 1, hd * HDK:(hd + 1) * HDK] + u_ref[ci, hd]
                st_ref[hd] = st
                so_ref[ci, hd] = st.astype(so_ref.dtype)
        for ci, rows in enumerate(chunks):
            for hd in range(HEADS):
                vc = slice(hd * HDV, (hd + 1) * HDV)
                q = pg_ref[rows, PG_Q + hd * HDK:PG_Q + (hd + 1) * HDK]
                go = pg_ref[rows, PG_G + hd * HDV:PG_G + (hd + 1) * HDV].astype(F32)
                qs = (q.astype(F32) * Q_SCALE).astype(BF16)
                o = lax.dot_general(qs, so_ref[ci, hd], (((1,), (1,)), ((), ())), preferred_element_type=F32)
                o_ref[rows, vc] = o
                oh, _ = _rms(o)
                ya_ref[rows, vc] = (oh * gn_ref[:, vc] * (go * _sigmoid(go))).astype(ya_ref.dtype)

    return pl.pallas_call(
        body, grid=(s // ts,),
        in_specs=[_rows(ts, PG_W), _fixed((LANE, HEADS * HDK)), _fixed((1, HEADS * HDK)), _fixed((1, HEADS * HDV))],
        out_specs=[_rows(ts, HEADS * HDV), pl.BlockSpec((cpb, HEADS, HDV, HDK), lambda i: (i, 0, 0, 0)),
                   pl.BlockSpec((cpb, HEADS, HDV, HDK), lambda i: (i, 0, 0, 0)), _rows(ts, HEADS * HDV)],
        out_shape=[jax.ShapeDtypeStruct((s, HEADS * HDV), BF16), jax.ShapeDtypeStruct((nc, HEADS, HDV, HDK), F32),
                   jax.ShapeDtypeStruct((nc, HEADS, HDV, HDK), BF16), jax.ShapeDtypeStruct((s, HEADS * HDV), F32)],
        scratch_shapes=[pltpu.VMEM((HEADS, HDV, HDK), F32), pltpu.VMEM((ts, HEADS * HDK), F32),
                        pltpu.VMEM((max(cpb, 8), HEADS * HDK), F32), pltpu.VMEM((cpb, HEADS, HDV, HDK), F32)],
        compiler_params=_params(("arbitrary",)), name=name,
    )(pg, wfu, b_f, gnorm)


def _gla_bwd(pg, sp, so, o, dya, wfu, b_f, gnorm, name):
    s = pg.shape[0]
    ts = _tile(s, TS_GLA, CHUNK)
    cpb = ts // CHUNK
    nblk = s // ts

    def body(pg_ref, sp_ref, so_ref, o_ref, dya_ref, wfu_ref, bf_ref, gn_ref, dpg_ref, dwfu_ref, dbf_ref, dgn_ref,
             dst_ref, la_ref, sg_ref, df_ref, e_ref, ktf_ref, dec_ref, g_ref):
        @pl.when(pl.program_id(0) == 0)
        def _():
            dst_ref[...] = jnp.zeros_like(dst_ref)
            dwfu_ref[...] = jnp.zeros_like(dwfu_ref)
            dbf_ref[...] = jnp.zeros_like(dbf_ref)
            dgn_ref[...] = jnp.zeros_like(dgn_ref)

        flow = pg_ref[:, PG_F:PG_W]
        f = jnp.dot(flow, wfu_ref[...], preferred_element_type=F32) + bf_ref[...]
        la_ref[...] = _log_sigmoid(f) * (1.0 / GATE_TEMP)
        sg_ref[...] = _sigmoid(-f) * (1.0 / GATE_TEMP)
        tri = _tri(False)
        tri_strict = _tri(True)
        chunks = [slice(ci * CHUNK, (ci + 1) * CHUNK) for ci in range(cpb)]
        for ci, rows in enumerate(chunks):
            la = la_ref[rows, :]
            b = jnp.dot(tri, la, precision=HIGHEST, preferred_element_type=F32)
            bend = jnp.sum(la, axis=0, keepdims=True)
            e = jnp.exp(bend - b)
            e_ref[rows, :] = e
            dec = jnp.exp(bend)
            dec_ref[ci:ci + 1, :] = dec
            for hd in range(HEADS):
                kc = slice(hd * HDK, (hd + 1) * HDK)
                vc = slice(hd * HDV, (hd + 1) * HDV)
                q = pg_ref[rows, PG_Q + hd * HDK:PG_Q + (hd + 1) * HDK]
                k = pg_ref[rows, PG_K + hd * HDK:PG_K + (hd + 1) * HDK]
                go = pg_ref[rows, PG_G + hd * HDV:PG_G + (hd + 1) * HDV].astype(F32)
                ktf_ref[rows, kc] = k.astype(F32) * e[:, kc]
                st_b = so_ref[ci, hd]
                qs = (q.astype(F32) * Q_SCALE).astype(BF16)
                oh, r = _rms(o_ref[rows, vc])
                gh = gn_ref[:, vc]
                sig = _sigmoid(go)
                dy = dya_ref[rows, vc].astype(F32)
                don = dy * (go * sig)
                dgn_ref[:, vc] += jnp.sum(don * oh, axis=0, keepdims=True)
                dong = don * gh
                do = (r * (dong - oh * jnp.mean(dong * oh, axis=-1, keepdims=True))).astype(BF16)
                g_ref[ci, hd] = lax.dot_general(do, qs, (((0,), (0,)), ((), ())), preferred_element_type=F32)
                dq = jnp.dot(do, st_b, preferred_element_type=F32) * Q_SCALE
                dpg_ref[rows, PG_Q + hd * HDK:PG_Q + (hd + 1) * HDK] = dq.astype(dpg_ref.dtype)
                dgo = dy * (oh * gh) * (sig * (1.0 + go * (1.0 - sig)))
                dpg_ref[rows, PG_G + hd * HDV:PG_G + (hd + 1) * HDV] = dgo.astype(dpg_ref.dtype)
        for ci in reversed(range(cpb)):
            for hd in range(HEADS):
                dst = dst_ref[hd] + g_ref[ci, hd]
                g_ref[ci, hd] = dst
                dst_ref[hd] = dst * dec_ref[ci:ci + 1, hd * HDK:(hd + 1) * HDK]
        for ci, rows in enumerate(chunks):
            for hd in range(HEADS):
                kc = slice(hd * HDK, (hd + 1) * HDK)
                v = pg_ref[rows, PG_V + hd * HDV:PG_V + (hd + 1) * HDV]
                ktf = ktf_ref[rows, kc]
                dst = g_ref[ci, hd]
                dst_b = dst.astype(BF16)
                dkt = jnp.dot(v, dst_b, preferred_element_type=F32)
                dv = lax.dot_general(ktf.astype(BF16), dst_b, (((1,), (1,)), ((), ())), preferred_element_type=F32)
                dd = jnp.sum(dst * sp_ref[ci, hd], axis=0, keepdims=True)
                dla = jnp.dot(tri_strict, dkt * ktf, precision=HIGHEST, preferred_element_type=F32) + dd * dec_ref[ci:ci + 1, kc]
                df_ref[rows, kc] = dla * sg_ref[rows, kc]
                dpg_ref[rows, PG_K + hd * HDK:PG_K + (hd + 1) * HDK] = (dkt * e_ref[rows, kc]).astype(dpg_ref.dtype)
                dpg_ref[rows, PG_V + hd * HDV:PG_V + (hd + 1) * HDV] = dv.astype(dpg_ref.dtype)
        df = df_ref[...]
        df_b = df.astype(BF16)
        dpg_ref[:, PG_F:PG_W] = lax.dot_general(df_b, wfu_ref[...], (((1,), (1,)), ((), ())), preferred_element_type=F32).astype(dpg_ref.dtype)
        dwfu_ref[...] += lax.dot_general(flow, df_b, (((0,), (0,)), ((), ())), preferred_element_type=F32)
        dbf_ref[...] += jnp.sum(df, axis=0, keepdims=True)

    rev = lambda i: (nblk - 1 - i, 0)
    return pl.pallas_call(
        body, grid=(nblk,),
        in_specs=[pl.BlockSpec((ts, PG_W), rev), pl.BlockSpec((cpb, HEADS, HDV, HDK), lambda i: (nblk - 1 - i, 0, 0, 0)),
                  pl.BlockSpec((cpb, HEADS, HDV, HDK), lambda i: (nblk - 1 - i, 0, 0, 0)), pl.BlockSpec((ts, HEADS * HDV), rev),
                  pl.BlockSpec((ts, HEADS * HDV), rev), _fixed((LANE, HEADS * HDK)), _fixed((1, HEADS * HDK)), _fixed((1, HEADS * HDV))],
        out_specs=[pl.BlockSpec((ts, PG_W), rev), _fixed((LANE, HEADS * HDK)), _fixed((1, HEADS * HDK)), _fixed((1, HEADS * HDV))],
        out_shape=[jax.ShapeDtypeStruct((s, PG_W), BF16), jax.ShapeDtypeStruct((LANE, HEADS * HDK), F32),
                   jax.ShapeDtypeStruct((1, HEADS * HDK), F32), jax.ShapeDtypeStruct((1, HEADS * HDV), F32)],
        scratch_shapes=[pltpu.VMEM((HEADS, HDV, HDK), F32)] + [pltpu.VMEM((ts, HEADS * HDK), F32)] * 5
        + [pltpu.VMEM((max(cpb, 8), HEADS * HDK), F32), pltpu.VMEM((cpb, HEADS, HDV, HDK), F32)],
        compiler_params=_params(("arbitrary",)), name=name,
    )(pg, sp, so, o, dya, wfu, b_f, gnorm)


def _window_sums(ext, sign):
    n = ext.shape[0]
    sums = {1: ext}
    w = 1
    while w < POOL_WINDOWS[-1]:
        sums[2 * w] = sums[w] + pltpu.roll(sums[w], w if sign > 0 else n - w, 0)
        w *= 2
    return [sums[POOL_WINDOWS[g]][:, g * LANE:(g + 1) * LANE] for g in range(len(POOL_WINDOWS))]


def _pool_counts(row0, n):
    pos = (row0 + lax.broadcasted_iota(jnp.int32, (n, 1), 0) + 1).astype(F32)
    return [1.0 / jnp.minimum(pos, float(w)) for w in POOL_WINDOWS]


def _pool_fwd(ppx, w_pool, pool_scale, name):
    s = ppx.shape[0]
    ts = _tile(s, TS_POOL, POOL_HALO)
    hb = ts // POOL_HALO
    pw = len(POOL_WINDOWS) * LANE

    def body(p_ref, halo_ref, w_ref, sc_ref, y_ref, ext_ref):
        i = pl.program_id(0)
        p = p_ref[...].astype(F32)
        ext_ref[0:POOL_HALO, :] = jnp.where(i > 0, halo_ref[...].astype(F32), 0.0)
        ext_ref[POOL_HALO:, :] = p
        sums = _window_sums(ext_ref[...], +1)
        cnt = _pool_counts(i * ts, ts)
        for g in range(len(POOL_WINDOWS)):
            cols = slice(g * LANE, (g + 1) * LANE)
            mixed = sums[g][POOL_HALO:, :] * cnt[g] - p[:, cols]
            y = jnp.dot(mixed.astype(BF16), w_ref[g], preferred_element_type=F32)
            y_ref[:, cols] = (y * sc_ref[:, cols]).astype(y_ref.dtype)

    return pl.pallas_call(
        body, grid=(s // ts,),
        in_specs=[pl.BlockSpec((ts, pw), lambda i: (i, 0)), pl.BlockSpec((POOL_HALO, pw), lambda i: (jnp.maximum(i * hb - 1, 0), 0)),
                  _fixed((len(POOL_WINDOWS), LANE, LANE)), _fixed((1, pw))],
        out_specs=_rows(ts, pw), out_shape=jax.ShapeDtypeStruct((s, pw), BF16),
        scratch_shapes=[pltpu.VMEM((ts + POOL_HALO, pw), F32)],
        compiler_params=_params(("parallel",)), name=name,
    )(ppx, ppx, w_pool, pool_scale)


def _pool_bwd(dyb, ppx, w_pool, pool_scale, name):
    s = ppx.shape[0]
    ts = _tile(s, TS_POOL, POOL_HALO)
    hb = ts // POOL_HALO
    nblk = s // ts
    last_halo = s // POOL_HALO - 1
    ng = len(POOL_WINDOWS)
    pw = ng * LANE

    def body(p_ref, halo_ref, dy_ref, dyn_ref, w_ref, sc_ref, dp_ref, dw_ref, dsc_ref, ext_ref, dext_ref, dm_ref):
        i = pl.program_id(0)

        @pl.when(i == 0)
        def _():
            dw_ref[...] = jnp.zeros_like(dw_ref)
            dsc_ref[...] = jnp.zeros_like(dsc_ref)

        p = p_ref[...].astype(F32)
        ext_ref[0:POOL_HALO, :] = jnp.where(i > 0, halo_ref[...].astype(F32), 0.0)
        ext_ref[POOL_HALO:, :] = p
        sums = _window_sums(ext_ref[...], +1)
        cnt = _pool_counts(i * ts, ts + POOL_HALO)
        sc = sc_ref[...]
        dy = dy_ref[...].astype(F32)
        dyn = jnp.where(i < nblk - 1, dyn_ref[...].astype(F32), 0.0)
        for g in range(ng):
            cols = slice(g * LANE, (g + 1) * LANE)
            wg = w_ref[g]
            mixed = (sums[g][POOL_HALO:, :] * cnt[g][0:ts] - p[:, cols]).astype(BF16)
            ypre = jnp.dot(mixed, wg, preferred_element_type=F32)
            dsc_ref[:, cols] += jnp.sum(dy[:, cols] * ypre, axis=0, keepdims=True)
            dyp = (dy[:, cols] * sc[:, cols]).astype(BF16)
            dypn = (dyn[:, cols] * sc[:, cols]).astype(BF16)
            dw_ref[g] += lax.dot_general(mixed, dyp, (((0,), (0,)), ((), ())), preferred_element_type=F32)
            dm = lax.dot_general(dyp, wg, (((1,), (1,)), ((), ())), preferred_element_type=F32)
            dmn = lax.dot_general(dypn, wg, (((1,), (1,)), ((), ())), preferred_element_type=F32)
            dext_ref[0:ts, cols] = dm * cnt[g][0:ts]
            dext_ref[ts:, cols] = dmn * cnt[g][ts:]
            dm_ref[:, cols] = dm
        lead = _window_sums(dext_ref[...], -1)
        for g in range(ng):
            cols = slice(g * LANE, (g + 1) * LANE)
            dp_ref[:, cols] = (lead[g][0:ts, :] - dm_ref[:, cols]).astype(dp_ref.dtype)

    return pl.pallas_call(
        body, grid=(nblk,),
        in_specs=[pl.BlockSpec((ts, pw), lambda i: (i, 0)), pl.BlockSpec((POOL_HALO, pw), lambda i: (jnp.maximum(i * hb - 1, 0), 0)),
                  pl.BlockSpec((ts, pw), lambda i: (i, 0)), pl.BlockSpec((POOL_HALO, pw), lambda i: (jnp.minimum((i + 1) * hb, last_halo), 0)),
                  _fixed((ng, LANE, LANE)), _fixed((1, pw))],
        out_specs=[_rows(ts, pw), _fixed((ng, LANE, LANE)), _fixed((1, pw))],
        out_shape=[jax.ShapeDtypeStruct((s, pw), BF16), jax.ShapeDtypeStruct((ng, LANE, LANE), F32), jax.ShapeDtypeStruct((1, pw), F32)],
        scratch_shapes=[pltpu.VMEM((ts + POOL_HALO, pw), F32), pltpu.VMEM((ts + POOL_HALO, pw), F32), pltpu.VMEM((ts, pw), F32)],
        compiler_params=_params(("arbitrary",)), name=name,
    )(ppx, ppx, dyb, dyb, w_pool, pool_scale)


def _xattn_fwd(ppx, kv, name):
    s = ppx.shape[0]
    m = kv.shape[0]
    ts = _tile(s, TS_XA, 8)
    xw = XA_HEADS * XA_HD

    def body(q_ref, kv_ref, o_ref):
        for hd in range(XA_HEADS):
            cols = slice(hd * XA_HD, (hd + 1) * XA_HD)
            k = kv_ref[:, hd * XA_HD:(hd + 1) * XA_HD]
            v = kv_ref[:, xw + hd * XA_HD:xw + (hd + 1) * XA_HD]
            sc = lax.dot_general(q_ref[:, cols], k, (((1,), (1,)), ((), ())), preferred_element_type=F32) * XA_SCALE
            ex = jnp.exp(sc - jnp.max(sc, axis=-1, keepdims=True))
            pr = ex * (1.0 / jnp.sum(ex, axis=-1, keepdims=True))
            o_ref[:, cols] = jnp.dot(pr.astype(BF16), v, preferred_element_type=F32).astype(o_ref.dtype)

    return pl.pallas_call(
        body, grid=(s // ts,), in_specs=[pl.BlockSpec((ts, xw), lambda i: (i, 1)), _fixed((m, 2 * xw))],
        out_specs=_rows(ts, xw), out_shape=jax.ShapeDtypeStruct((s, xw), BF16),
        compiler_params=_params(("parallel",)), name=name,
    )(ppx, kv)


def _xattn_bwd(dxc, ppx, kv, name):
    s = ppx.shape[0]
    m = kv.shape[0]
    ts = _tile(s, TS_XA, 8)
    xw = XA_HEADS * XA_HD

    def body(do_ref, q_ref, kv_ref, dq_ref, dkv_ref):
        @pl.when(pl.program_id(0) == 0)
        def _():
            dkv_ref[...] = jnp.zeros_like(dkv_ref)

        for hd in range(XA_HEADS):
            cols = slice(hd * XA_HD, (hd + 1) * XA_HD)
            vcols = slice(xw + hd * XA_HD, xw + (hd + 1) * XA_HD)
            q = q_ref[:, cols]
            k = kv_ref[:, cols]
            v = kv_ref[:, vcols]
            do = do_ref[:, cols]
            sc = lax.dot_general(q, k, (((1,), (1,)), ((), ())), preferred_element_type=F32) * XA_SCALE
            ex = jnp.exp(sc - jnp.max(sc, axis=-1, keepdims=True))
            pr = ex * (1.0 / jnp.sum(ex, axis=-1, keepdims=True))
            dpr = lax.dot_general(do, v, (((1,), (1,)), ((), ())), preferred_element_type=F32)
            dsc = (pr * (dpr - jnp.sum(dpr * pr, axis=-1, keepdims=True)) * XA_SCALE).astype(BF16)
            dq_ref[:, cols] = jnp.dot(dsc, k, preferred_element_type=F32).astype(dq_ref.dtype)
            dkv_ref[:, cols] += lax.dot_general(dsc, q, (((0,), (0,)), ((), ())), preferred_element_type=F32)
            dkv_ref[:, vcols] += lax.dot_general(pr.astype(BF16), do, (((0,), (0,)), ((), ())), preferred_element_type=F32)

    return pl.pallas_call(
        body, grid=(s // ts,), in_specs=[_rows(ts, xw), pl.BlockSpec((ts, xw), lambda i: (i, 1)), _fixed((m, 2 * xw))],
        out_specs=[_rows(ts, xw), _fixed((m, 2 * xw))],
        out_shape=[jax.ShapeDtypeStruct((s, xw), BF16), jax.ShapeDtypeStruct((m, 2 * xw), F32)],
        compiler_params=_params(("arbitrary",)), name=name,
    )(dxc, ppx, kv)


def _resident(shape):
    nd = len(shape)
    return pl.BlockSpec(shape, lambda i: (0,) * nd, pipeline_mode=pl.Buffered(1))


def _mix_in_fwd(h, w_ts, after, name, tm=512):
    s, d = h.shape
    tm = _tile(s, tm)
    n, na = len(w_ts), len(after)

    def body(h_ref, *refs):
        w_refs, o_refs = refs[:n], refs[n + na:]
        for rows in _sub_blocks(tm):
            hv = h_ref[rows, :]
            for w_ref, o_ref in zip(w_refs, o_refs):
                o_ref[rows, :] = lax.dot_general(hv, w_ref[...], (((1,), (1,)), ((), ())), preferred_element_type=F32).astype(o_ref.dtype)

    return pl.pallas_call(
        body, grid=(s // tm,), in_specs=[_rows(tm, d)] + [_resident(w.shape) for w in w_ts] + [ANY] * na,
        out_specs=[_rows(tm, w.shape[0]) for w in w_ts],
        out_shape=[jax.ShapeDtypeStruct((s, w.shape[0]), BF16) for w in w_ts],
        compiler_params=_params(("parallel",)), name=name,
    )(h, *w_ts, *after)


def _mix_tail_fwd(ya_in, yb_in, xc, pgt, w_ups, w_o, x, g_post, g_next, after, name, tm=512):
    s, d = x.shape
    tm = _tile(s, tm)
    na = len(after)
    branch_ins = (ya_in, yb_in, xc)

    def body(a_ref, b_ref, c_ref, gt_ref, wa_ref, wb_ref, wc_ref, wo_ref, x_ref, gp_ref, gn_ref, *rest):
        ya_ref, yb_ref, yc_ref, m_ref, y_ref, xo_ref, h_ref = rest[na:]
        for rows in _sub_blocks(tm):
            merged = None
            for j, (in_ref, w_ref, out_ref) in enumerate(((a_ref, wa_ref, ya_ref), (b_ref, wb_ref, yb_ref), (c_ref, wc_ref, yc_ref))):
                yj = jnp.dot(in_ref[rows, :], w_ref[...], preferred_element_type=F32)
                out_ref[rows, :] = yj.astype(out_ref.dtype)
                part = _sigmoid(gt_ref[rows, j * D:(j + 1) * D].astype(F32)) * yj
                merged = part if merged is None else merged + part
            merged_b = merged.astype(m_ref.dtype)
            m_ref[rows, :] = merged_b
            y = jnp.dot(merged_b, wo_ref[...], preferred_element_type=F32)
            y_ref[rows, :] = y
            yh, _ = _rms(y)
            xn = x_ref[rows, :] + yh * gp_ref[...]
            xo_ref[rows, :] = xn
            xh, _ = _rms(xn)
            h_ref[rows, :] = (xh * gn_ref[...]).astype(h_ref.dtype)

    bf = lambda: jax.ShapeDtypeStruct((s, d), BF16)
    f32 = lambda: jax.ShapeDtypeStruct((s, d), F32)
    return pl.pallas_call(
        body, grid=(s // tm,),
        in_specs=[_rows(tm, a.shape[1]) for a in branch_ins] + [_rows(tm, 3 * d)] + [_resident(w.shape) for w in w_ups]
        + [_resident(w_o.shape), _rows(tm, d), _fixed((1, d)), _fixed((1, d))] + [ANY] * na,
        out_specs=[_rows(tm, d)] * 7,
        out_shape=[bf(), bf(), bf(), bf(), f32(), f32(), bf()],
        compiler_params=_params(("parallel",)), name=name,
    )(*branch_ins, pgt, *w_ups, w_o, x, g_post, g_next, *after)


def _mix_tail_bwd(dy, pgt, ys, w_ups, w_o, after, name, tm=512):
    s, d = dy.shape
    tm = _tile(s, tm)
    na = len(after)
    widths = [w.shape[0] for w in w_ups]

    def body(dy_ref, gt_ref, ya_ref, yb_ref, yc_ref, wa_ref, wb_ref, wc_ref, wo_ref, *rest):
        dya_ref, dyb_ref, dyc_ref, dgt_ref, da_ref, db_ref, dc_ref = rest[na:]
        nt = (((1,), (1,)), ((), ()))
        for rows in _sub_blocks(tm):
            dm = lax.dot_general(dy_ref[rows, :], wo_ref[...], nt, preferred_element_type=F32)
            for j, (y_ref, dyj_ref, w_ref, din_ref) in enumerate(((ya_ref, dya_ref, wa_ref, da_ref), (yb_ref, dyb_ref, wb_ref, db_ref),
                                                                   (yc_ref, dyc_ref, wc_ref, dc_ref))):
                sig = _sigmoid(gt_ref[rows, j * D:(j + 1) * D].astype(F32))
                dyj = (dm * sig).astype(dyj_ref.dtype)
                dyj_ref[rows, :] = dyj
                dgt_ref[rows, j * D:(j + 1) * D] = (dm * y_ref[rows, :].astype(F32) * sig * (1.0 - sig)).astype(dgt_ref.dtype)
                din_ref[rows, :] = lax.dot_general(dyj, w_ref[...], nt, preferred_element_type=F32).astype(din_ref.dtype)

    bf = lambda w: jax.ShapeDtypeStruct((s, w), BF16)
    return pl.pallas_call(
        body, grid=(s // tm,),
        in_specs=[_rows(tm, d), _rows(tm, 3 * d)] + [_rows(tm, d)] * 3 + [_resident(w.shape) for w in w_ups] + [_resident(w_o.shape)]
        + [ANY] * na,
        out_specs=[_rows(tm, d)] * 3 + [_rows(tm, 3 * d)] + [_rows(tm, w) for w in widths],
        out_shape=[bf(d), bf(d), bf(d), bf(3 * d)] + [bf(w) for w in widths],
        compiler_params=_params(("parallel",)), name=name,
    )(dy, pgt, *ys, *w_ups, w_o, *after)


def _adam_math(w, g, m, v):
    mn = ADAM_B1 * m + (1.0 - ADAM_B1) * g
    vn = ADAM_B2 * v + (1.0 - ADAM_B2) * (g * g)
    m_hat = mn / (1.0 - ADAM_B1 ** ADAM_STEP)
    v_hat = vn / (1.0 - ADAM_B2 ** ADAM_STEP)
    return -ADAM_LR * (m_hat / (jnp.sqrt(v_hat) + ADAM_EPS) + ADAM_WD * w), mn, vn


def _adamw(w, g, m, v, name):
    r, c = w.shape[-2:]
    tr, tc = _block_of(r, c, cap=512 if r % 16 == 0 else 256)

    def spec(a):
        if a.ndim == 2:
            return pl.BlockSpec((tr, tc), lambda i, j: (i, j))
        return pl.BlockSpec((None, tr, tc), lambda i, j: (0, i, j))

    def body(w_ref, g_ref, m_ref, v_ref, d_ref, mo_ref, vo_ref):
        d_ref[...], mo_ref[...], vo_ref[...] = _adam_math(w_ref[...], g_ref[...], m_ref[...], v_ref[...])

    return pl.pallas_call(
        body, grid=(r // tr, c // tc), in_specs=[spec(a) for a in (w, g, m, v)], out_specs=[spec(w)] * 3,
        out_shape=[jax.ShapeDtypeStruct(w.shape, F32)] * 3, compiler_params=_params(("parallel", "parallel")), name=name,
    )(w, g, m, v)


ANY = pl.BlockSpec(memory_space=pl.ANY)


def _place():
    x, y, c = lax.axis_index("x"), lax.axis_index("y"), lax.axis_index("c")
    chips = [(1 - x, y), (x, 1 - y), (1 - x, 1 - y)]
    return x, y, c, chips


def _by_cols(rows):
    return rows % 32 != 0 and rows != 16


def _half_of(ref, lead, c):
    r, cols = ref.shape[-2:]
    if _by_cols(r):
        return ref.at[(*lead, slice(None), pl.ds(pl.multiple_of(c * (cols // 2), LANE), cols // 2))]
    return ref.at[(*lead, pl.ds(pl.multiple_of(c * (r // 2), 8), r // 2))]


def _half_shape(shape):
    r, cols = shape[-2:]
    return shape[:-2] + ((r, cols // 2) if _by_cols(r) else (r // 2, cols))


def _block_of(r, cols, cap=256):
    if r % 16 == 0:
        return _tile(r, cap, 16), cols
    return r, _tile(cols, cap)


def _place_shard(shard, chip_arr, out_dtype, name, after=()):
    _, r, cols = shard.shape
    tr, tc = _block_of(r, cols)

    def body(chip_ref, s_ref, *rest):
        o_ref = rest[len(after)]
        o_ref[...] = s_ref[...].astype(o_ref.dtype)

    return pl.pallas_call(
        body,
        grid_spec=pltpu.PrefetchScalarGridSpec(
            num_scalar_prefetch=1, grid=(r // tr, cols // tc),
            in_specs=[pl.BlockSpec((None, tr, tc), lambda i, j, chip_ref: (0, i, j))] + [ANY] * len(after),
            out_specs=pl.BlockSpec((None, tr, tc), lambda i, j, chip_ref: (chip_ref[0], i, j))),
        out_shape=jax.ShapeDtypeStruct((4, r, cols), out_dtype),
        compiler_params=_params(("parallel", "parallel")), name=name,
    )(chip_arr, shard, *after)


HBM = pl.BlockSpec(memory_space=pltpu.HBM)
SEM = pl.BlockSpec(memory_space=pltpu.SEMAPHORE)
EFFECT = pltpu.SideEffectType.DATAFLOW_SIDE_EFFECTING


def _in_hbm(arrays):
    return [pltpu.with_memory_space_constraint(a, pltpu.HBM) for a in arrays]


def _gather_start(bufs, after, name):
    n, na = len(bufs), len(after)

    def body(*refs):
        send_sem, recv_sem = refs[n + na], refs[n + na + 1]
        outs = refs[n + na + 2:2 * n + na + 2]
        token = refs[2 * n + na + 2]
        x, y, c, chips = _place()
        me = 2 * x + y
        for p, chip in enumerate(chips):
            for w in range(n):
                block = _half_of(outs[w], (me,), c)
                pltpu.make_async_remote_copy(
                    src_ref=block, dst_ref=block, send_sem=send_sem, recv_sem=recv_sem,
                    device_id=(*chip, c), device_id_type=MESH).start()
        token[...] = jnp.zeros_like(token)

    out = pl.pallas_call(
        body, name=name, in_specs=[HBM] * n + [ANY] * na,
        out_specs=[SEM, SEM] + [HBM] * n + [pl.BlockSpec(memory_space=pltpu.VMEM)],
        out_shape=[pltpu.SemaphoreType.DMA(()), pltpu.SemaphoreType.DMA(())]
        + [pltpu.HBM(a.shape, a.dtype) for a in bufs] + [jax.ShapeDtypeStruct((8, LANE), F32)],
        input_output_aliases={w: w + 2 for w in range(n)},
        compiler_params=pltpu.CompilerParams(has_side_effects=EFFECT),
    )(*_in_hbm(bufs), *after)
    return out[0], out[1], list(out[2:2 + n]), out[2 + n]


def _gather_pass(bufs, send_sem, recv_sem, after, name):
    n, na = len(bufs), len(after)

    def body(*refs):
        send1, recv1 = refs[n], refs[n + 1]
        send2, recv2 = refs[n + 2 + na], refs[n + 3 + na]
        outs = refs[n + 4 + na:2 * n + 4 + na]
        x, y, c, chips = _place()
        me = 2 * x + y
        arrivals = [(w, px, py) for px, py in chips for w in range(n)]
        for w, px, py in arrivals:
            first = pltpu.make_async_remote_copy(
                src_ref=_half_of(outs[w], (me,), c), dst_ref=_half_of(outs[w], (2 * px + py,), c), send_sem=send1, recv_sem=recv1,
                device_id=(px, py, c), device_id_type=MESH)
            first.wait_send()
            first.wait_recv()
        for w, px, py in arrivals:
            arrived = _half_of(outs[w], (2 * px + py,), c)
            pltpu.make_async_remote_copy(
                src_ref=arrived, dst_ref=arrived, send_sem=send2, recv_sem=recv2,
                device_id=(x, y, 1 - c), device_id_type=MESH).start()

    out = pl.pallas_call(
        body, name=name, in_specs=[HBM] * n + [SEM, SEM] + [ANY] * na,
        out_specs=[SEM, SEM] + [HBM] * n,
        out_shape=[pltpu.SemaphoreType.DMA(()), pltpu.SemaphoreType.DMA(())] + [pltpu.HBM(a.shape, a.dtype) for a in bufs],
        input_output_aliases={w: w + 2 for w in range(n)},
        compiler_params=pltpu.CompilerParams(has_side_effects=EFFECT),
    )(*bufs, send_sem, recv_sem, *after)
    return out[0], out[1], list(out[2:])


def _gather_finish(bufs, send_sem, recv_sem, after, name):
    n, na = len(bufs), len(after)

    def body(*refs):
        send2, recv2 = refs[n], refs[n + 1]
        outs = refs[n + 2 + na:2 * n + 2 + na]
        x, y, c, chips = _place()
        for p, (px, py) in enumerate(chips):
            for w in range(n):
                passed = pltpu.make_async_remote_copy(
                    src_ref=_half_of(outs[w], (2 * px + py,), c), dst_ref=_half_of(outs[w], (2 * px + py,), 1 - c),
                    send_sem=send2, recv_sem=recv2, device_id=(x, y, 1 - c), device_id_type=MESH)
                passed.wait_send()
                passed.wait_recv()

    out = pl.pallas_call(
        body, name=name, in_specs=[HBM] * n + [SEM, SEM] + [ANY] * na, out_specs=[HBM] * n,
        out_shape=[pltpu.HBM(a.shape, a.dtype) for a in bufs],
        input_output_aliases={w: w for w in range(n)},
        compiler_params=pltpu.CompilerParams(has_side_effects=EFFECT),
    )(*bufs, send_sem, recv_sem, *after)
    return list(out)


def _pair_exchange_start(grads, after, name):
    n, na = len(grads), len(after)
    lands = [lax.empty(_half_shape(a.shape), a.dtype) for a in grads]

    def body(*refs):
        send_sem, recv_sem = refs[2 * n + na], refs[2 * n + na + 1]
        srcs = refs[2 * n + na + 2:3 * n + na + 2]
        dsts = refs[3 * n + na + 2:4 * n + na + 2]
        token = refs[4 * n + na + 2]
        x, y, c, _ = _place()
        for w in range(n):
            pltpu.make_async_remote_copy(
                src_ref=_half_of(srcs[w], (slice(None),), 1 - c), dst_ref=dsts[w], send_sem=send_sem, recv_sem=recv_sem,
                device_id=(x, y, 1 - c), device_id_type=MESH).start()
        token[...] = jnp.zeros_like(token)

    out = pl.pallas_call(
        body, name=name, in_specs=[HBM] * (2 * n) + [ANY] * na,
        out_specs=[SEM, SEM] + [HBM] * (2 * n) + [pl.BlockSpec(memory_space=pltpu.VMEM)],
        out_shape=[pltpu.SemaphoreType.DMA(()), pltpu.SemaphoreType.DMA(())]
        + [pltpu.HBM(a.shape, a.dtype) for a in grads + lands] + [jax.ShapeDtypeStruct((8, LANE), F32)],
        input_output_aliases={w: w + 2 for w in range(2 * n)},
        compiler_params=pltpu.CompilerParams(has_side_effects=EFFECT),
    )(*_in_hbm(grads), *_in_hbm(lands), *after)
    return out[0], out[1], list(out[2:2 + n]), list(out[2 + n:2 + 2 * n]), out[2 + 2 * n]


def _pair_exchange_finish(grads, lands, send_sem, recv_sem, after, name):
    n, na = len(grads), len(after)

    def body(*refs):
        send, recv = refs[2 * n], refs[2 * n + 1]
        srcs = refs[2 * n + 2 + na:3 * n + 2 + na]
        dsts = refs[3 * n + 2 + na:4 * n + 2 + na]
        x, y, c, _ = _place()
        for w in range(n):
            copy = pltpu.make_async_remote_copy(
                src_ref=_half_of(srcs[w], (slice(None),), 1 - c), dst_ref=dsts[w], send_sem=send, recv_sem=recv,
                device_id=(x, y, 1 - c), device_id_type=MESH)
            copy.wait_send()
            copy.wait_recv()

    out = pl.pallas_call(
        body, name=name, in_specs=[HBM] * (2 * n) + [SEM, SEM] + [ANY] * na, out_specs=[HBM] * (2 * n),
        out_shape=[pltpu.HBM(a.shape, a.dtype) for a in grads + lands],
        input_output_aliases={w: w for w in range(2 * n)},
        compiler_params=pltpu.CompilerParams(has_side_effects=EFFECT),
    )(*grads, *lands, send_sem, recv_sem, *after)
    return list(out[:n]), list(out[n:])


def _pair_sum(g, got, c_arr, name):
    _, r, cols = g.shape
    hr, hc = _half_shape((r, cols))
    tr, tc = _block_of(hr, hc)
    nbr, nbc = hr // tr, hc // tc
    by_cols = _by_cols(r)

    def body(c_ref, g_ref, got_ref, o_ref):
        o_ref[...] = (g_ref[...].astype(F32) + got_ref[...].astype(F32)).astype(o_ref.dtype)

    def mine(j, i, k, c_ref):
        return (j, i, c_ref[0] * nbc + k) if by_cols else (j, c_ref[0] * nbr + i, k)

    return pl.pallas_call(
        body,
        grid_spec=pltpu.PrefetchScalarGridSpec(
            num_scalar_prefetch=1, grid=(4, nbr, nbc),
            in_specs=[pl.BlockSpec((None, tr, tc), mine),
                      pl.BlockSpec((None, tr, tc), lambda j, i, k, c_ref: (j, i, k))],
            out_specs=pl.BlockSpec((None, tr, tc), lambda j, i, k, c_ref: (j, i, k))),
        out_shape=jax.ShapeDtypeStruct((4, hr, hc), BF16),
        compiler_params=_params(("parallel", "parallel", "parallel")), name=name,
    )(c_arr, *_in_hbm([g, got]))


def _chip_exchange_start(parts, after, name):
    n, na = len(parts), len(after)
    lands = [lax.empty((3,) + a.shape[1:], a.dtype) for a in parts]

    def body(*refs):
        send_sem, recv_sem = refs[2 * n + na], refs[2 * n + na + 1]
        srcs = refs[2 * n + na + 2:3 * n + na + 2]
        dsts = refs[3 * n + na + 2:4 * n + na + 2]
        token = refs[4 * n + na + 2]
        x, y, c, chips = _place()
        for p, (px, py) in enumerate(chips):
            for w in range(n):
                pltpu.make_async_remote_copy(
                    src_ref=srcs[w].at[2 * px + py], dst_ref=dsts[w].at[p], send_sem=send_sem, recv_sem=recv_sem,
                    device_id=(px, py, c), device_id_type=MESH).start()
        token[...] = jnp.zeros_like(token)

    out = pl.pallas_call(
        body, name=name, in_specs=[HBM] * (2 * n) + [ANY] * na,
        out_specs=[SEM, SEM] + [HBM] * (2 * n) + [pl.BlockSpec(memory_space=pltpu.VMEM)],
        out_shape=[pltpu.SemaphoreType.DMA(()), pltpu.SemaphoreType.DMA(())]
        + [pltpu.HBM(a.shape, a.dtype) for a in parts + lands] + [jax.ShapeDtypeStruct((8, LANE), F32)],
        input_output_aliases={w: w + 2 for w in range(2 * n)},
        compiler_params=pltpu.CompilerParams(has_side_effects=EFFECT),
    )(*_in_hbm(parts), *_in_hbm(lands), *after)
    return out[0], out[1], list(out[2:2 + n]), list(out[2 + n:2 + 2 * n]), out[2 + 2 * n]


def _chip_exchange_finish(parts, lands, send_sem, recv_sem, after, name):
    n, na = len(parts), len(after)

    def body(*refs):
        send, recv = refs[2 * n], refs[2 * n + 1]
        srcs = refs[2 * n + 2 + na:3 * n + 2 + na]
        dsts = refs[3 * n + 2 + na:4 * n + 2 + na]
        x, y, c, chips = _place()
        for p, (px, py) in enumerate(chips):
            for w in range(n):
                copy = pltpu.make_async_remote_copy(
                    src_ref=srcs[w].at[2 * px + py], dst_ref=dsts[w].at[p], send_sem=send, recv_sem=recv,
                    device_id=(px, py, c), device_id_type=MESH)
                copy.wait_send()
                copy.wait_recv()

    out = pl.pallas_call(
        body, name=name, in_specs=[HBM] * (2 * n) + [SEM, SEM] + [ANY] * na, out_specs=[HBM] * (2 * n),
        out_shape=[pltpu.HBM(a.shape, a.dtype) for a in parts + lands],
        input_output_aliases={w: w for w in range(2 * n)},
        compiler_params=pltpu.CompilerParams(has_side_effects=EFFECT),
    )(*parts, *lands, send_sem, recv_sem, *after)
    return list(out[:n]), list(out[n:])


def _chip_sum(part, got, place_arr, name):
    _, hr, hc = part.shape
    by_cols = _by_cols(hr)
    tr, tc = _block_of(hr, hc)
    nbr, nbc = hr // tr, hc // tc

    def body(place_ref, p_ref, got_ref, o_ref):
        acc = p_ref[...].astype(F32)
        for p in range(3):
            acc = acc + got_ref[p].astype(F32)
        o_ref[...] = acc

    def mine(i, k, place_ref):
        return (i, place_ref[1] * nbc + k) if by_cols else (place_ref[1] * nbr + i, k)

    return pl.pallas_call(
        body,
        grid_spec=pltpu.PrefetchScalarGridSpec(
            num_scalar_prefetch=1, grid=(nbr, nbc),
            in_specs=[pl.BlockSpec((None, tr, tc), lambda i, k, place_ref: (place_ref[0], i, k)),
                      pl.BlockSpec((3, tr, tc), lambda i, k, place_ref: (0, i, k))],
            out_specs=pl.BlockSpec((tr, tc), mine)),
        out_shape=jax.ShapeDtypeStruct((hr, 2 * hc) if by_cols else (2 * hr, hc), F32),
        compiler_params=_params(("parallel", "parallel")), name=name,
    )(place_arr, *_in_hbm([part, got]))


def _pair_join_start(bufs, name):
    n = len(bufs)

    def body(*refs):
        send_sem, recv_sem = refs[n], refs[n + 1]
        outs = refs[n + 2:2 * n + 2]
        token = refs[2 * n + 2]
        x, y, c, _ = _place()
        for w in range(n):
            block = _half_of(outs[w], (), c)
            pltpu.make_async_remote_copy(
                src_ref=block, dst_ref=block, send_sem=send_sem, recv_sem=recv_sem,
                device_id=(x, y, 1 - c), device_id_type=MESH).start()
        token[...] = jnp.zeros_like(token)

    out = pl.pallas_call(
        body, name=name, in_specs=[HBM] * n,
        out_specs=[SEM, SEM] + [HBM] * n + [pl.BlockSpec(memory_space=pltpu.VMEM)],
        out_shape=[pltpu.SemaphoreType.DMA(()), pltpu.SemaphoreType.DMA(())]
        + [pltpu.HBM(a.shape, a.dtype) for a in bufs] + [jax.ShapeDtypeStruct((8, LANE), F32)],
        input_output_aliases={w: w + 2 for w in range(n)},
        compiler_params=pltpu.CompilerParams(has_side_effects=EFFECT),
    )(*_in_hbm(bufs))
    return out[0], out[1], list(out[2:2 + n]), out[2 + n]


def _pair_join_finish(bufs, send_sem, recv_sem, after, name):
    n, na = len(bufs), len(after)

    def body(*refs):
        send, recv = refs[n], refs[n + 1]
        outs = refs[n + 2 + na:2 * n + 2 + na]
        x, y, c, _ = _place()
        for w in range(n):
            copy = pltpu.make_async_remote_copy(
                src_ref=_half_of(outs[w], (), c), dst_ref=_half_of(outs[w], (), 1 - c), send_sem=send, recv_sem=recv,
                device_id=(x, y, 1 - c), device_id_type=MESH)
            copy.wait_send()
            copy.wait_recv()

    out = pl.pallas_call(
        body, name=name, in_specs=[HBM] * n + [SEM, SEM] + [ANY] * na, out_specs=[HBM] * n,
        out_shape=[pltpu.HBM(a.shape, a.dtype) for a in bufs],
        input_output_aliases={w: w for w in range(n)},
        compiler_params=pltpu.CompilerParams(has_side_effects=EFFECT),
    )(*bufs, send_sem, recv_sem, *after)
    return list(out)


SMALL = ("ffn1_pre_g", "ffn1_post_g", "mix_pre_g", "gla_norm_g", "mem_norm_g", "mix_post_g", "ffn2_pre_g", "ffn2_post_g", "final_g",
         "b_f", "pool_scale", "w_pool", "w_fu")
N_GAINS = 9
SMALL_PACKS = ((16, D), (24, 512), (4 * LANE, LANE))
W_FU_ROW = 8


LOSS_ROW = 2


def _all_sum_small(gs, loss, name, after=()):
    ins = [gs[n] for n in SMALL[:N_GAINS]] + [gs["b_f"], gs["pool_scale"], gs["w_fu_pad"], gs["w_pool"].reshape(4 * LANE, LANE), loss]

    def body(*refs):
        gain_refs = refs[:N_GAINS]
        bf_ref, ps_ref, wfu_ref, wp_ref, loss_ref = refs[N_GAINS:N_GAINS + 5]
        outs = refs[N_GAINS + 5 + len(after):N_GAINS + 8 + len(after)]
        mine_a, mine_b, all_a, all_b, all_c, send_sems, recv_sems = refs[N_GAINS + 8 + len(after):]
        mine_a[...] = jnp.zeros_like(mine_a)
        for i, ref in enumerate(gain_refs):
            mine_a[i:i + 1, :] = ref[...]
        mine_b[...] = jnp.zeros_like(mine_b)
        mine_b[0:1, :] = bf_ref[...]
        mine_b[1:2, :] = ps_ref[...]
        mine_b[LOSS_ROW:LOSS_ROW + 1, 0:LANE] = loss_ref[0:1, :]
        mine_b[W_FU_ROW:W_FU_ROW + GATE_RANK, :] = wfu_ref[0:GATE_RANK, :]
        packs = ((mine_a, all_a), (mine_b, all_b), (wp_ref, all_c))
        x, y, c, chips = _place()
        me, sibling = (x, y, c), (x, y, 1 - c)

        def copy(t, k, block, to, own=False):
            px, py, pc = block
            slot = packs[t][1].at[4 * px + 2 * py + pc]
            return pltpu.make_async_remote_copy(
                src_ref=packs[t][0] if own else slot, dst_ref=slot,
                send_sem=send_sems.at[t, k], recv_sem=recv_sems.at[t, k], device_id=to, device_id_type=MESH)

        started = []
        for t, (mine, everyone) in enumerate(packs):
            everyone[4 * x + 2 * y + c] = mine[...]
            started.append(copy(t, 0, me, sibling, own=True))
            started += [copy(t, 1 + j, me, (*chip, c), own=True) for j, chip in enumerate(chips)]
        for cp in started:
            cp.start()
        passed = []
        for j, chip in enumerate(chips):
            for t in range(len(packs)):
                copy(t, 1 + j, (*chip, c), me).wait_recv()
                fwd = copy(t, 4 + j, (*chip, c), sibling)
                fwd.start()
                passed.append(fwd)
        for t in range(len(packs)):
            copy(t, 0, sibling, me).wait_recv()
            for j, chip in enumerate(chips):
                copy(t, 4 + j, (*chip, 1 - c), me).wait_recv()
        for cp in started + passed:
            cp.wait_send()
        for (_, everyone), o_ref in zip(packs, outs):
            acc = everyone[0]
            for k in range(1, 8):
                acc = acc + everyone[k]
            o_ref[...] = acc

    vmem = pl.BlockSpec(memory_space=pltpu.VMEM)
    return pl.pallas_call(
        body, in_specs=[vmem] * len(ins) + [ANY] * len(after), out_specs=[vmem] * 3,
        out_shape=[jax.ShapeDtypeStruct(shape, F32) for shape in SMALL_PACKS],
        scratch_shapes=[pltpu.VMEM(SMALL_PACKS[0], F32), pltpu.VMEM(SMALL_PACKS[1], F32)]
        + [pltpu.VMEM((8,) + shape, F32) for shape in SMALL_PACKS]
        + [pltpu.SemaphoreType.DMA((3, 7)), pltpu.SemaphoreType.DMA((3, 7))],
        compiler_params=pltpu.CompilerParams(has_side_effects=True, vmem_limit_bytes=VMEM_LIMIT), name=name,
    )(*ins, *after)


def _adamw_small(sums, params, chip_arr, name):
    flat = [a for n in SMALL for a in params[n]]

    def body(chip_ref, a_ref, b_ref, c_ref, *refs):
        ins, outs = refs[:len(flat)], refs[len(flat):]
        for i, n in enumerate(SMALL):
            w_ref, m_ref, v_ref = ins[3 * i:3 * i + 3]
            g_ref, d_ref, mo_ref, vo_ref = outs[4 * i:4 * i + 4]
            if n == "w_pool":
                pieces = [((0, k), c_ref[k * LANE:(k + 1) * LANE, :]) for k in range(4)]
            elif n == "w_fu":
                mine = pl.ds(pl.multiple_of(chip_ref[0] * LANE, LANE), LANE)
                pieces = [((0,), b_ref[W_FU_ROW:W_FU_ROW + GATE_RANK, mine])]
            elif n == "b_f":
                pieces = [((), b_ref[0:1, :])]
            elif n == "pool_scale":
                pieces = [((), b_ref[1:2, :])]
            else:
                pieces = [((), a_ref[i:i + 1, :])]
            for at, g in pieces:
                d, mn, vn = _adam_math(w_ref[at], g, m_ref[at], v_ref[at])
                g_ref[at] = g
                d_ref[at] = d
                mo_ref[at] = mn
                vo_ref[at] = vn

    def whole(shape):
        return pl.BlockSpec(shape, lambda i, chip_ref: (0,) * len(shape))

    out = pl.pallas_call(
        body,
        grid_spec=pltpu.PrefetchScalarGridSpec(
            num_scalar_prefetch=1, grid=(1,),
            in_specs=[whole(a.shape) for a in list(sums) + flat],
            out_specs=[whole(params[n][0].shape) for n in SMALL for _ in range(4)]),
        out_shape=[jax.ShapeDtypeStruct(params[n][0].shape, F32) for n in SMALL for _ in range(4)],
        compiler_params=_params(("arbitrary",)), name=name,
    )(chip_arr, *sums, *flat)
    return {n: tuple(out[4 * i:4 * i + 4]) for i, n in enumerate(SMALL)}


def _ffn_bwd(dz, x_norm, ab, u, w_in, w_out, x, g_pre, dres, tag, emit, advance, after=(), post=None):
    dw_out = _mm(u, dz, ta=True, out_dtype=BF16, tm=1408, tk=2048, after=after, name=tag + "_out_dw")
    behind = emit(tag + "_w_out", dw_out)
    dab = _ffn_out_dx_swiglu(dz, w_out, ab, behind, name=tag + "_out_dx")
    behind = advance((dab,))
    dw_in = _mm(x_norm, dab, ta=True, out_dtype=BF16, tm=512, tk=4096, shards=4, after=behind, name=tag + "_in_dw")
    behind = emit(tag + "_w_in", dw_in)
    out = _mm_rms_bwd([(dab, w_in)], x, g_pre, dres, after=behind, post=post, name=tag + "_in_dx")
    return (*out, advance((out[0],)))


def _local_step(x, mem, target, small, gather, emit, advance):
    behind = gather("start", "ffn1i", ())
    behind = gather("start", "ffn1o", behind)
    h1 = _norm_fwd(x, small["ffn1_pre_g"], BF16, name="ffn1_pre", after=behind)
    gather("pass", "ffn1i", (h1,))
    big = gather("finish", "ffn1i", ())
    behind = gather("start", "mixa", (big["ffn1_w_in"],))
    behind = gather("start", "mixb", behind)
    behind = gather("start", "ffn2", behind)
    ab1, u1 = _ffn_in_swiglu(h1, big["ffn1_w_in"], name="ffn1_in", after=behind)
    gather("pass", "ffn1o", (ab1,))
    big.update(gather("finish", "ffn1o", ()))
    behind = gather("pass", "mixa", (u1,))
    f1, x1, h = _mm_resid_norm(u1, big["ffn1_w_out"], x, small["ffn1_post_g"], 0.5, small["mix_pre_g"], name="ffn1_out", after=behind)
    big.update(gather("finish", "mixa", (h,)))
    small = dict(small, w_fu_pad=big["w_fu_pad"])
    behind = gather("pass", "mixb", (h,))
    pg, ppx, pgt = _mix_in_fwd(h, [big["w_gla_t"], big["w_px_t"], big["w_gates_t"]], behind, name="mix_in")
    big.update(gather("finish", "mixb", (pgt,)))
    mem_n = _norm_fwd(mem, small["mem_norm_g"], BF16, name="mem_norm")
    kv = _mm(mem_n, big["w_mem_kv"], out_dtype=BF16, name="mem_kv")
    ya_in, sp, so, o_gla = _gla_fwd(pg, small["w_fu_pad"], small["b_f"], small["gla_norm_g"], name="gla_fwd")
    yb_in = _pool_fwd(ppx, small["w_pool_b"], small["pool_scale"], name="pool_fwd")
    xc = _xattn_fwd(ppx, kv, name="xattn_fwd")
    behind = gather("pass", "ffn2", (xc,))
    w_ups = [big["w_up_gla"], big["w_up_pool"], big["w_up_xattn"]]
    ya, yb, yc, merged, ymix, x2, h2 = _mix_tail_fwd(ya_in, yb_in, xc, pgt, w_ups, big["w_o"], x1, small["mix_post_g"],
                                                     small["ffn2_pre_g"], behind, name="mix_tail")
    big.update(gather("finish", "ffn2", (h2,)))
    ab2, u2 = _ffn_in_swiglu(h2, big["ffn2_w_in"], name="ffn2_in")
    gs = {}
    dz2, dx3, gs["ffn2_post_g"], gs["final_g"], loss = _ffn_out_loss(u2, big["ffn2_w_out"], x2, small["ffn2_post_g"], 0.5,
                                                                    small["final_g"], target, name="ffn2_out_loss")
    dx2, gs["ffn2_pre_g"], dy, gs["mix_post_g"], behind = _ffn_bwd(
        dz2, h2, ab2, u2, big["ffn2_w_in"], big["ffn2_w_out"], x2, small["ffn2_pre_g"], dx3, "ffn2", emit, advance,
        post=(ymix, small["mix_post_g"], 1.0))
    emit("w_o", _mm(merged, dy, ta=True, out_dtype=BF16, tm=512, tk=4096, after=behind, name="mix_out_dw"))
    dya, dyb, dyc, dgt, dya_in, dyb_in, dxc = _mix_tail_bwd(dy, pgt, (ya, yb, yc), w_ups, big["w_o"], (), name="mix_tail_bwd")
    emit("w_up_gla", _mm(ya_in, dya, ta=True, out_dtype=BF16, tm=512, tk=4096, name="up_gla_dw"))
    emit("w_up_pool", _mm(yb_in, dyb, ta=True, out_dtype=BF16, tm=512, tk=4096, shards=4, name="up_pool_dw"))
    emit("w_up_xattn", _mm(xc, dyc, ta=True, out_dtype=BF16, tm=512, tk=4096, shards=4, name="up_xattn_dw"))
    dpg, gs["w_fu_pad"], gs["b_f"], gs["gla_norm_g"] = _gla_bwd(pg, sp, so, o_gla, dya_in, small["w_fu_pad"], small["b_f"], small["gla_norm_g"], name="gla_bwd")
    dp, gs["w_pool"], gs["pool_scale"] = _pool_bwd(dyb_in, ppx, small["w_pool_b"], small["pool_scale"], name="pool_bwd")
    dxq, dkv = _xattn_bwd(dxc, ppx, kv, name="xattn_bwd")
    dkv = dkv.astype(BF16)
    emit("w_mem_kv", _mm(mem_n, dkv, ta=True, out_dtype=BF16, name="mem_kv_dw"))
    dmem_n = _mm(dkv, big["w_mem_kv"], tb=True, name="mem_kv_dx")
    _, gs["mem_norm_g"] = _rms_bwd(mem, small["mem_norm_g"], [dmem_n], None, 1.0, BF16, name="mem_norm_bwd")
    emit("w_gla", _mm(dpg, h, ta=True, out_dtype=BF16, tm=640, tk=4096, name="mix_in_gla_dw"))
    emit("w_p", _mm(dp, h, ta=True, out_dtype=BF16, tm=512, tk=4096, name="mix_in_p_dw"))
    emit("w_xq", _mm(dxq, h, ta=True, out_dtype=BF16, tm=512, tk=4096, name="mix_in_xq_dw"))
    behind = emit("w_gates", _mm(dgt, h, ta=True, out_dtype=BF16, tm=512, tk=4096, name="mix_in_gates_dw"))
    pairs = [(dpg, big["w_gla_t"]), (dp, big["w_p_t"]), (dxq, big["w_xq_t"]), (dgt, big["w_gates_t"])]
    dx1, gs["mix_pre_g"], dz1, gs["ffn1_post_g"] = _mm_rms_bwd(pairs, x1, small["mix_pre_g"], dx2, after=behind,
                                                               post=(f1, small["ffn1_post_g"], 0.5), tm=256, name="mix_in_dx")
    behind = advance((dx1,))
    dx0, gs["ffn1_pre_g"], _ = _ffn_bwd(dz1, h1, ab1, u1, big["ffn1_w_in"], big["ffn1_w_out"], x, small["ffn1_pre_g"], dx1,
                                        "ffn1", emit, advance, after=behind)
    return loss, dx0, gs


BIG = ("ffn1_w_in", "ffn1_w_out", "w_in", "w_mem_kv", "w_up_gla", "w_up_pool", "w_up_xattn", "w_o", "ffn2_w_in", "ffn2_w_out")
COL_SHARDED = ("ffn1_w_in", "w_in", "w_up_pool", "w_up_xattn", "ffn2_w_in")
GATHER_GROUPS = {"ffn1i": ("ffn1_w_in",), "ffn1o": ("ffn1_w_out",), "mixa": ("w_in", "w_fu"),
                 "mixb": ("w_mem_kv", "w_up_gla", "w_up_pool", "w_up_xattn", "w_o"), "ffn2": ("ffn2_w_in", "ffn2_w_out")}
REDUCE_GROUPS = {"ffn2": ("ffn2_w_out", "ffn2_w_in"),
                 "mix": ("w_o", "w_up_gla", "w_up_pool", "w_up_xattn", "w_mem_kv", "w_gla", "w_p", "w_xq", "w_gates"),
                 "ffn1_out": ("ffn1_w_out",),
                 "ffn1_in": ("ffn1_w_in",)}
REDUCE_FIRST, REDUCE_LAST = "ffn2", "ffn1_in"
GAINS = ("ffn1_pre_g", "ffn1_post_g", "mix_pre_g", "gla_norm_g", "mem_norm_g", "mix_post_g", "ffn2_pre_g", "ffn2_post_g", "final_g")
WEIGHTS = ("ffn1_pre_g", "ffn1_w_in", "ffn1_w_out", "ffn1_post_g", "mix_pre_g", "w_in", "w_fu", "b_f", "gla_norm_g", "w_pool",
           "pool_scale", "mem_norm_g", "w_mem_kv", "w_up_gla", "w_up_pool", "w_up_xattn", "w_o", "mix_post_g", "ffn2_pre_g",
           "ffn2_w_in", "ffn2_w_out", "ffn2_post_g", "final_g")
IN_GLA, IN_F, IN_PX, IN_GATES, IN_END = 0, 3072, 3088, 4112, 7184
def _cols_from_shards(g):
    return jnp.transpose(g, (1, 0, 2)).reshape(g.shape[1], 4 * g.shape[2])


def _laid_end_to_end(pieces, rows):
    out, start = None, 0
    for p in pieces:
        padded = jnp.pad(p, ((start, rows - start - p.shape[0]), (0, 0)))
        out = padded if out is None else out + padded
        start += p.shape[0]
    return out


def _rows_of_blocks(g, lo, hi, rows=None):
    q = g.shape[1]
    cuts = [(j, max(lo, j * q) - j * q, min(hi, (j + 1) * q) - j * q) for j in range(g.shape[0])]
    return _laid_end_to_end([g[j, a:b] for j, a, b in cuts if a < b], rows or hi - lo)


def _blocks_of_rows(parts, blocks):
    q = sum(p.shape[0] for p in parts) // blocks
    out = []
    for j in range(blocks):
        pieces, start = [], 0
        for p in parts:
            a, b = max(j * q, start), min((j + 1) * q, start + p.shape[0])
            if a < b:
                pieces.append(p[a - start:b - start])
            start += p.shape[0]
        out.append(_laid_end_to_end(pieces, q))
    return jnp.stack(out)


def kernel(x, mem, ffn1_pre_g, ffn1_w_in, ffn1_w_out, ffn1_post_g, mix_pre_g, w_in, w_fu, b_f, gla_norm_g, w_pool, pool_scale, mem_norm_g, w_mem_kv, w_up_gla, w_up_pool, w_up_xattn, w_o, mix_post_g, ffn2_pre_g, ffn2_w_in, ffn2_w_out, ffn2_post_g, final_g, loss_target, m_ffn1_pre_g, m_ffn1_w_in, m_ffn1_w_out, m_ffn1_post_g, m_mix_pre_g, m_w_in, m_w_fu, m_b_f, m_gla_norm_g, m_w_pool, m_pool_scale, m_mem_norm_g, m_w_mem_kv, m_w_up_gla, m_w_up_pool, m_w_up_xattn, m_w_o, m_mix_post_g, m_ffn2_pre_g, m_ffn2_w_in, m_ffn2_w_out, m_ffn2_post_g, m_final_g, v_ffn1_pre_g, v_ffn1_w_in, v_ffn1_w_out, v_ffn1_post_g, v_mix_pre_g, v_w_in, v_w_fu, v_b_f, v_gla_norm_g, v_w_pool, v_pool_scale, v_mem_norm_g, v_w_mem_kv, v_w_up_gla, v_w_up_pool, v_w_up_xattn, v_w_o, v_mix_post_g, v_ffn2_pre_g, v_ffn2_w_in, v_ffn2_w_out, v_ffn2_post_g, v_final_g):
    args = dict(locals())
    w = {n: args[n][0] for n in WEIGHTS}
    m = {n: args["m_" + n][0] for n in WEIGHTS}
    v = {n: args["v_" + n][0] for n in WEIGHTS}
    xi, yi, ci = lax.axis_index("x"), lax.axis_index("y"), lax.axis_index("c")
    chip = 2 * xi + yi

    c_arr = jnp.reshape(ci, (1,)).astype(jnp.int32)
    chip_arr = jnp.reshape(chip, (1,)).astype(jnp.int32)
    place_arr = jnp.stack([chip, ci]).astype(jnp.int32)
    w_in_t = []
    shard_of = {n: args[n] for n in BIG if n != "w_in"}
    shard_of["w_fu"] = args["w_fu"]
    placed, inflight = {}, {}

    def place(names, after):
        for n in names:
            if n not in placed:
                placed[n] = _place_shard(shard_of[n], chip_arr, F32 if n == "w_fu" else BF16, name="place_" + n, after=after)

    def relayout(names, gathered):
        out = {}
        for n, g in zip(names, gathered):
            if n == "w_fu":
                w_fu_full = _cols_from_shards(g)
                out["w_fu_pad"] = jnp.concatenate([w_fu_full, jnp.zeros((LANE - GATE_RANK, 512), F32)], axis=0).astype(BF16)
            elif n == "w_in":
                out["w_gla_t"] = _rows_of_blocks(g, IN_GLA, IN_PX, rows=PG_W)
                out["w_px_t"] = _rows_of_blocks(g, IN_PX, IN_GATES)
                out["w_p_t"] = _rows_of_blocks(g, IN_PX, IN_PX + 512)
                out["w_xq_t"] = _rows_of_blocks(g, IN_PX + 512, IN_GATES)
                out["w_gates_t"] = _rows_of_blocks(g, IN_GATES, IN_END)
            else:
                out[n] = _cols_from_shards(g) if n in COL_SHARDED else g.reshape(4 * g.shape[1], g.shape[2])
        return out

    def gather(op, group, after):
        names = GATHER_GROUPS[group]
        if op == "start":
            place(names, ())
            inflight[group] = _gather_start([placed[n] for n in names], after, name="gather_" + group + "_start")
            behind = (inflight[group][3],)
            if group == "ffn1o":
                tied = lax.optimization_barrier((behind, tuple(args[k] for k in ("w_in", "m_w_in", "v_w_in"))))[1]
                w_in_t.extend(jnp.transpose(a[0]) for a in tied)
                shard_of["w_in"] = w_in_t[0][None]
                place(shard_of, behind)
            return behind
        if op == "pass":
            if group == "ffn1i":
                not_started = [n for g in GATHER_GROUPS if g not in inflight for n in GATHER_GROUPS[g]]
                after = tuple(after) + tuple(w_in_t[1:]) + tuple(placed[n] for n in not_started)
            send, recv, bufs, _ = inflight[group]
            inflight[group] = _gather_pass(bufs, send, recv, after, name="gather_" + group + "_pass")
            return (inflight[group][2][0],)
        send, recv, bufs = inflight.pop(group)
        return relayout(names, _gather_finish(bufs, send, recv, after, name="gather_" + group + "_finish"))

    small = {n: w[n].reshape(1, D) for n in GAINS}
    small["b_f"] = w["b_f"].reshape(1, 512)
    small["pool_scale"] = w["pool_scale"].reshape(1, 512)
    small["w_pool_b"] = w["w_pool"].astype(BF16)

    pending, crossing, travelling = {}, {}, {}

    def emit(name, grad):
        pending[name] = grad
        group = next((g for g, names in REDUCE_GROUPS.items() if name == names[-1]), None)
        if group is None:
            return ()
        gb = {n: pending.pop(n) for n in REDUCE_GROUPS[group]}
        if group == "mix":
            gb["w_in"] = _blocks_of_rows([gb.pop("w_gla")[0:IN_PX], gb.pop("w_p"), gb.pop("w_xq"), gb.pop("w_gates")], 4)
        names = list(gb)
        contrib = [gb[n] if n in COL_SHARDED else gb[n].reshape(4, gb[n].shape[0] // 4, gb[n].shape[1]) for n in names]
        send, recv, contrib, lands, token = _pair_exchange_start(contrib, (), name="grads_" + group + "_pair_start")
        if group == REDUCE_LAST:
            behind = finish_chips(REDUCE_FIRST, (token,))
            contrib, from_sibling = _pair_exchange_finish(contrib, lands, send, recv, behind, name="grads_" + group + "_pair_finish")
            return over_chips(group, names, contrib, from_sibling)
        crossing[group] = (names, contrib, lands, send, recv)
        return (token,)

    def over_chips(group, names, contrib, from_sibling):
        pair = [_pair_sum(g, got, c_arr, name="grads_pair_sum_" + n) for n, g, got in zip(names, contrib, from_sibling)]
        send, recv, pair, lands, token = _chip_exchange_start(pair, (), name="grads_" + group + "_chip_start")
        travelling[group] = (names, send, recv, pair, lands)
        return (token,)

    def advance(after):
        behind = ()
        for group in list(crossing):
            names, contrib, lands, send, recv = crossing.pop(group)
            contrib, from_sibling = _pair_exchange_finish(contrib, lands, send, recv, after, name="grads_" + group + "_pair_finish")
            behind = over_chips(group, names, contrib, from_sibling)
        return behind

    halves = {}

    def finish_chips(group, after):
        names, send, recv, pair, lands = travelling.pop(group)
        pair, from_chips = _chip_exchange_finish(pair, lands, send, recv, after, name="grads_" + group + "_chip_finish")
        for n, p, got in zip(names, pair, from_chips):
            halves[n] = _chip_sum(p, got, place_arr, name="grads_chip_sum_" + n)
        return (halves[names[-1]],)

    loss, grad_x, gs = _local_step(x[0], mem[0], loss_target[0], small, gather, emit, advance)

    for group in list(travelling):
        finish_chips(group, (grad_x,))
    send, recv, joining, token = _pair_join_start([halves[n] for n in BIG], name="grads_pair_join_start")
    small_sums = _all_sum_small(gs, loss, name="sum_small_grads", after=(token,))
    loss = small_sums[1][LOSS_ROW, 0]
    reduced = dict(zip(BIG, _pair_join_finish(joining, send, recv, (small_sums[0],), name="grads_pair_join_finish")))

    grads, delta, new_m, new_v = {}, {}, {}, {}
    for n in BIG:
        if n == "w_in":
            updated = _adamw(w_in_t[0], reduced[n], w_in_t[1], w_in_t[2], name="adamw_" + n)
            grads[n] = jnp.transpose(reduced[n])[None]
            delta[n], new_m[n], new_v[n] = (jnp.transpose(a)[None] for a in updated)
            continue
        grads[n] = reduced[n][None]
        delta[n], new_m[n], new_v[n] = _adamw(args[n], reduced[n], args["m_" + n], args["v_" + n], name="adamw_" + n)
    small_params = {n: (args[n], args["m_" + n], args["v_" + n]) for n in SMALL}
    for n, (g, d, mn, vn) in _adamw_small(small_sums, small_params, chip_arr, name="adamw_small").items():
        grads[n], delta[n], new_m[n], new_v[n] = g, d, mn, vn

    outs = [loss, grad_x[None]]
    for group in (grads, delta, new_m, new_v):
        outs += [group[n] for n in WEIGHTS]
    return tuple(outs)
```

```python
import jax
import jax.numpy as jnp
from jax import lax
from jax.experimental import pallas as pl
from jax.experimental.pallas import tpu as pltpu

F32 = jnp.float32
BF16 = jnp.bfloat16
MESH = pl.DeviceIdType.MESH
HIGHEST = lax.Precision.HIGHEST

D = 1024
DFF = 2816
CHUNK = 64
HEADS = 4
HDK = 128
HDV = 256
GATE_TEMP = 16.0
POOL_WINDOWS = (2, 4, 8, 16)
POOL_HALO = 16
XA_HEADS = 4
XA_HD = 128
EPS = 1e-6
Q_SCALE = HDK ** -0.5
XA_SCALE = XA_HD ** -0.5
PG_Q, PG_K, PG_V, PG_G, PG_F, PG_W = 0, 512, 1024, 2048, 3072, 3200
GATE_RANK = 16
ADAM_LR, ADAM_B1, ADAM_B2, ADAM_EPS, ADAM_WD, ADAM_STEP = 0.001, 0.9, 0.999, 1e-08, 0.01, 10

VMEM_LIMIT = 48 * 1024 * 1024
LANE = 128
TS_ROW = 512
TS_GLA = 512
TS_POOL = 512
TS_XA = 512


def _params(sem):
    return pltpu.CompilerParams(dimension_semantics=sem, vmem_limit_bytes=VMEM_LIMIT)


def _tile(n, cap, unit=LANE):
    if n <= cap:
        return n
    best = None
    for t in range(unit, cap + 1, unit):
        if n % t == 0:
            best = t
    assert best is not None, (n, cap)
    return best


def _sigmoid(x):
    return 0.5 * jnp.tanh(0.5 * x) + 0.5


def _log_sigmoid(x):
    return jnp.minimum(x, 0.0) - jnp.log(1.0 + jnp.exp(-jnp.abs(x)))


def _rms(x):
    r = lax.rsqrt(jnp.mean(x * x, axis=-1, keepdims=True) + EPS)
    return x * r, r


def _rows(ts, w):
    return pl.BlockSpec((ts, w), lambda i: (i, 0))


def _fixed(shape):
    nd = len(shape)
    return pl.BlockSpec(shape, lambda i: (0,) * nd)


def _mm(a, b, *, ta=False, tb=False, out_dtype=F32, tm=2048, tn=1024, tk=1024, shards=1, after=(), name):
    b_blocked = b.ndim == 3
    assert not (b_blocked and tb)
    m, kdim = (a.shape[1], a.shape[0]) if ta else a.shape
    if b_blocked:
        n, tn = b.shape[0] * b.shape[2], b.shape[2]
        assert b.shape[1] == kdim and shards in (1, b.shape[0])
    else:
        n = b.shape[0] if tb else b.shape[1]
        assert (b.shape[1] if tb else b.shape[0]) == kdim, (a.shape, b.shape, ta, tb)
        tn = n // shards if shards > 1 else _tile(n, tn)
    tm = _tile(m, tm)
    tk = _tile(kdim, tk)
    nk = kdim // tk
    dims = (((0 if ta else 1,), (1 if tb else 0,)), ((), ()))

    def body(a_ref, b_ref, *rest):
        o_ref, *acc = rest[len(after):]
        part = lax.dot_general(a_ref[...], b_ref[...], dims, preferred_element_type=F32)
        if nk == 1:
            o_ref[...] = part.astype(o_ref.dtype)
            return
        acc_ref, = acc
        k = pl.program_id(2)

        @pl.when(k == 0)
        def _():
            acc_ref[...] = part

        @pl.when(k > 0)
        def _():
            acc_ref[...] += part

        @pl.when(k == nk - 1)
        def _():
            o_ref[...] = acc_ref[...].astype(o_ref.dtype)

    a_spec = pl.BlockSpec((tk, tm), lambda i, j, k: (k, i)) if ta else pl.BlockSpec((tm, tk), lambda i, j, k: (i, k))
    if b_blocked:
        b_spec = pl.BlockSpec((None, tk, tn), lambda i, j, k: (j, k, 0))
    else:
        b_spec = pl.BlockSpec((tn, tk), lambda i, j, k: (j, k)) if tb else pl.BlockSpec((tk, tn), lambda i, j, k: (k, j))
    if shards > 1:
        out_shape = jax.ShapeDtypeStruct((shards, m, tn), out_dtype)
        o_spec = pl.BlockSpec((None, tm, tn), lambda i, j, k: (j, i, 0))
    else:
        out_shape = jax.ShapeDtypeStruct((m, n), out_dtype)
        o_spec = pl.BlockSpec((tm, tn), lambda i, j, k: (i, j))
    return pl.pallas_call(
        body, grid=(m // tm, n // tn, nk), in_specs=[a_spec, b_spec] + [ANY] * len(after), out_specs=o_spec, out_shape=out_shape,
        scratch_shapes=[pltpu.VMEM((tm, tn), F32)] if nk > 1 else [],
        compiler_params=_params(("parallel", "parallel", "arbitrary")), name=name,
    )(a, b, *after)


def _norm_fwd(x, g, out_dtype, name, after=()):
    s, d = x.shape
    ts = _tile(s, TS_ROW, 8)

    def body(x_ref, g_ref, *rest):
        o_ref = rest[len(after)]
        xh, _ = _rms(x_ref[...])
        o_ref[...] = (xh * g_ref[...]).astype(o_ref.dtype)

    return pl.pallas_call(
        body, grid=(s // ts,), in_specs=[_rows(ts, d), _fixed((1, d))] + [ANY] * len(after), out_specs=_rows(ts, d),
        out_shape=jax.ShapeDtypeStruct((s, d), out_dtype), compiler_params=_params(("parallel",)), name=name,
    )(x, g, *after)


def _mm_resid_norm(a, w, x, g_post, alpha, g_next, name, after=(), tm=512):
    s, kdim = a.shape
    d = w.shape[1]
    tm = _tile(s, tm)
    with_h = g_next is not None
    na = len(after)

    def body(a_ref, w_ref, x_ref, gp_ref, *rest):
        rest = rest[int(with_h) + na:] if not with_h else rest[:1] + rest[1 + na:]
        for rows in _sub_blocks(tm):
            f = jnp.dot(a_ref[rows, :], w_ref[...], preferred_element_type=F32)
            fh, _ = _rms(f)
            xn = x_ref[rows, :] + alpha * (fh * gp_ref[...])
            if with_h:
                gn_ref, f_ref, xo_ref, h_ref = rest
                xh, _ = _rms(xn)
                h_ref[rows, :] = (xh * gn_ref[...]).astype(h_ref.dtype)
            else:
                f_ref, xo_ref = rest
            f_ref[rows, :] = f
            xo_ref[rows, :] = xn

    ins = [a, w, x, g_post] + ([g_next] if with_h else []) + list(after)
    in_specs = [_rows(tm, kdim), _fixed((kdim, d)), _rows(tm, d), _fixed((1, d))] + ([_fixed((1, d))] if with_h else []) + [ANY] * na
    out_shape = [jax.ShapeDtypeStruct((s, d), F32)] * 2 + ([jax.ShapeDtypeStruct((s, d), BF16)] if with_h else [])
    out = pl.pallas_call(
        body, grid=(s // tm,), in_specs=in_specs, out_specs=[_rows(tm, d)] * len(out_shape), out_shape=out_shape,
        compiler_params=_params(("parallel",)), name=name,
    )(*ins)
    return (out[0], out[1], out[2]) if with_h else (out[0], out[1], None)


def _mm_rms_bwd(pairs, x, g, dres, name, after=(), post=None, tm=512):
    s, d = x.shape
    tm = _tile(s, tm)
    n, na = len(pairs), len(after)

    def body(*refs):
        a_refs, w_refs = refs[0:2 * n:2], refs[1:2 * n:2]
        x_ref, g_ref, dres_ref = refs[2 * n:2 * n + 3]
        if post is not None:
            f_ref, gp_ref = refs[2 * n + 3:2 * n + 5]
            dx_ref, dg_ref, df_ref, dgp_ref = refs[2 * n + 5 + na:]
        else:
            dx_ref, dg_ref = refs[2 * n + 3 + na:]
        @pl.when(pl.program_id(0) == 0)
        def _():
            dg_ref[...] = jnp.zeros_like(dg_ref)
            if post is not None:
                dgp_ref[...] = jnp.zeros_like(dgp_ref)

        for rows in _sub_blocks(tm):
            dy = None
            for a_ref, w_ref in zip(a_refs, w_refs):
                if len(a_ref.shape) == 3:
                    tkb = a_ref.shape[2]
                    parts = [lax.dot_general(a_ref[q, rows, :], w_ref[:, q * tkb:(q + 1) * tkb], (((1,), (1,)), ((), ())),
                                             preferred_element_type=F32) for q in range(a_ref.shape[0])]
                else:
                    parts = [jnp.dot(a_ref[rows, :], w_ref[...], preferred_element_type=F32)]
                for part in parts:
                    dy = part if dy is None else dy + part
            xh, r = _rms(x_ref[rows, :])
            dg_ref[...] += jnp.sum(dy * xh, axis=0, keepdims=True)
            dyg = dy * g_ref[...]
            dx = r * (dyg - xh * jnp.mean(dyg * xh, axis=-1, keepdims=True)) + dres_ref[rows, :]
            dx_ref[rows, :] = dx
            if post is not None:
                fh, rf = _rms(f_ref[rows, :])
                dz = dx * post[2]
                dgp_ref[...] += jnp.sum(dz * fh, axis=0, keepdims=True)
                dzg = dz * gp_ref[...]
                df_ref[rows, :] = (rf * (dzg - fh * jnp.mean(dzg * fh, axis=-1, keepdims=True))).astype(df_ref.dtype)

    ins, in_specs = [], []
    for a_arr, w_arr in pairs:
        ins += [a_arr, w_arr]
        if a_arr.ndim == 3:
            in_specs.append(pl.BlockSpec((a_arr.shape[0], tm, a_arr.shape[2]), lambda i: (0, i, 0)))
        else:
            in_specs.append(_rows(tm, a_arr.shape[1]))
        in_specs.append(pl.BlockSpec(w_arr.shape, lambda i: (0, 0), pipeline_mode=pl.Buffered(1)))
    with_post = post is not None
    return pl.pallas_call(
        body, grid=(s // tm,),
        in_specs=in_specs + [_rows(tm, d), _fixed((1, d)), _rows(tm, d)] + ([_rows(tm, d), _fixed((1, d))] if with_post else [])
        + [ANY] * na,
        out_specs=[_rows(tm, d), _fixed((1, d))] + ([_rows(tm, d), _fixed((1, d))] if with_post else []),
        out_shape=[jax.ShapeDtypeStruct((s, d), F32), jax.ShapeDtypeStruct((1, d), F32)]
        + ([jax.ShapeDtypeStruct((s, d), BF16), jax.ShapeDtypeStruct((1, d), F32)] if with_post else []),
        compiler_params=_params(("arbitrary",)), name=name,
    )(*ins, x, g, dres, *(post[:2] if with_post else ()), *after)


def _ffn_out_loss(u, w_out, x, g_post, alpha, g_final, target, name, tm=512):
    s, kdim = u.shape
    d = w_out.shape[1]
    tm = _tile(s, tm)

    def body(u_ref, w_ref, x_ref, gp_ref, gf_ref, t_ref, df_ref, dx_ref, dgp_ref, dgf_ref, loss_ref):
        @pl.when(pl.program_id(0) == 0)
        def _():
            dgp_ref[...] = jnp.zeros_like(dgp_ref)
            dgf_ref[...] = jnp.zeros_like(dgf_ref)
            loss_ref[...] = jnp.zeros_like(loss_ref)

        for rows in _sub_blocks(tm):
            f = jnp.dot(u_ref[rows, :], w_ref[...], preferred_element_type=F32)
            fh, rf = _rms(f)
            xn = x_ref[rows, :] + alpha * (fh * gp_ref[...])
            xh, rx = _rms(xn)
            gf = gf_ref[...]
            diff = xh * gf - t_ref[rows, :]
            sq = jnp.sum(diff * diff, axis=1, keepdims=True)
            loss_ref[...] += (0.5 / d) * jnp.sum(sq, axis=0, keepdims=True)
            dy = diff * (1.0 / d)
            dgf_ref[...] += jnp.sum(dy * xh, axis=0, keepdims=True)
            dyg = dy * gf
            dxn = rx * (dyg - xh * jnp.mean(dyg * xh, axis=-1, keepdims=True))
            dx_ref[rows, :] = dxn
            dz = dxn * alpha
            dgp_ref[...] += jnp.sum(dz * fh, axis=0, keepdims=True)
            dzg = dz * gp_ref[...]
            df_ref[rows, :] = (rf * (dzg - fh * jnp.mean(dzg * fh, axis=-1, keepdims=True))).astype(df_ref.dtype)

    return pl.pallas_call(
        body, grid=(s // tm,),
        in_specs=[_rows(tm, kdim), _resident(w_out.shape), _rows(tm, d), _fixed((1, d)), _fixed((1, d)), _rows(tm, d)],
        out_specs=[_rows(tm, d), _rows(tm, d), _fixed((1, d)), _fixed((1, d)), _fixed((8, LANE))],
        out_shape=[jax.ShapeDtypeStruct((s, d), BF16), jax.ShapeDtypeStruct((s, d), F32), jax.ShapeDtypeStruct((1, d), F32),
                   jax.ShapeDtypeStruct((1, d), F32), jax.ShapeDtypeStruct((8, LANE), F32)],
        compiler_params=_params(("arbitrary",)), name=name,
    )(u, w_out, x, g_post, g_final, target)


def _rms_bwd(x, g, dys, dres, alpha, out_dtype, name, after=()):
    s, d = x.shape
    ts = _tile(s, TS_ROW, 8)
    ndy = len(dys)
    with_res = dres is not None

    def body(x_ref, g_ref, *rest):
        dy_refs = rest[:ndy]
        rest = rest[ndy:]
        if with_res:
            dres_ref = rest[0]
        dx_ref, dg_ref = rest[int(with_res) + len(after):]
        xh, r = _rms(x_ref[...])
        dy = dy_refs[0][...].astype(F32)
        for ref in dy_refs[1:]:
            dy = dy + ref[...].astype(F32)
        dy = dy * alpha

        @pl.when(pl.program_id(0) == 0)
        def _():
            dg_ref[...] = jnp.zeros_like(dg_ref)

        dg_ref[...] += jnp.sum(dy * xh, axis=0, keepdims=True)
        dyg = dy * g_ref[...]
        dx = r * (dyg - xh * jnp.mean(dyg * xh, axis=-1, keepdims=True))
        if with_res:
            dx = dx + dres_ref[...]
        dx_ref[...] = dx.astype(dx_ref.dtype)

    ins = [x, g] + list(dys) + ([dres] if with_res else []) + list(after)
    in_specs = [_rows(ts, d), _fixed((1, d))] + [_rows(ts, d)] * (ndy + int(with_res)) + [ANY] * len(after)
    return pl.pallas_call(
        body, grid=(s // ts,), in_specs=in_specs, out_specs=[_rows(ts, d), _fixed((1, d))],
        out_shape=[jax.ShapeDtypeStruct((s, d), out_dtype), jax.ShapeDtypeStruct((1, d), F32)],
        compiler_params=_params(("arbitrary",)), name=name,
    )(*ins)


HALF_FF = DFF // 2


SUB_ROWS = 256
RING_SLOTS = 3


def _sub_blocks(tm):
    sub = SUB_ROWS if tm % SUB_ROWS == 0 else tm
    return [slice(r0, r0 + sub) for r0 in range(0, tm, sub)]


def _ffn_in_swiglu(x_norm, w_in, name, after=(), tm=1024):
    s, d = x_norm.shape
    tm = _tile(s, tm)

    def body(x_ref, wa_ref, wb_ref, *rest):
        ab_ref, u_ref = rest[len(after):]
        for rows in _sub_blocks(tm):
            xv = x_ref[rows, :]
            a = jnp.dot(xv, wa_ref[...], preferred_element_type=F32)
            b = jnp.dot(xv, wb_ref[...], preferred_element_type=F32)
            ab_ref[0, rows, :] = a.astype(ab_ref.dtype)
            ab_ref[1, rows, :] = b.astype(ab_ref.dtype)
            u_ref[rows, :] = (a * _sigmoid(a) * b).astype(u_ref.dtype)

    ab, u = pl.pallas_call(
        body, grid=(2, s // tm),
        in_specs=[pl.BlockSpec((tm, d), lambda j, i: (i, 0)), pl.BlockSpec((d, HALF_FF), lambda j, i: (0, j)),
                  pl.BlockSpec((d, HALF_FF), lambda j, i: (0, 2 + j))] + [ANY] * len(after),
        out_specs=[pl.BlockSpec((2, None, tm, HALF_FF), lambda j, i: (0, j, i, 0)), pl.BlockSpec((tm, HALF_FF), lambda j, i: (i, j))],
        out_shape=[jax.ShapeDtypeStruct((2, 2, s, HALF_FF), BF16), jax.ShapeDtypeStruct((s, DFF), BF16)],
        compiler_params=_params(("parallel", "parallel")), name=name,
    )(x_norm, w_in, w_in, *after)
    return ab.reshape(4, s, HALF_FF), u


def _ffn_out_dx_swiglu(dz, w_out, ab, after, name, tm=1024):
    s, d = dz.shape
    tm = _tile(s, tm)
    ni = s // tm
    steps = 2 * ni

    def body(dz_ref, w_ref, ab_hbm, *rest):
        dab_ref, ring, sem = rest[len(after):]
        t = pl.program_id(0) * ni + pl.program_id(1)

        def fetch(step):
            return pltpu.make_async_copy(ab_hbm.at[:, step // ni, pl.ds((step % ni) * tm, tm), :], ring.at[step % RING_SLOTS],
                                         sem.at[step % RING_SLOTS])

        @pl.when(t == 0)
        def _():
            for step in range(min(RING_SLOTS - 1, steps)):
                fetch(step).start()

        @pl.when(t + (RING_SLOTS - 1) < steps)
        def _():
            fetch(t + (RING_SLOTS - 1)).start()

        fetch(t).wait()
        ab_ref = ring.at[t % RING_SLOTS]
        for rows in _sub_blocks(tm):
            du = lax.dot_general(dz_ref[rows, :], w_ref[...], (((1,), (1,)), ((), ())), preferred_element_type=F32)
            a = ab_ref[0, rows, :].astype(F32)
            b = ab_ref[1, rows, :].astype(F32)
            sig = _sigmoid(a)
            dab_ref[0, rows, :] = (du * b * (sig * (1.0 + a * (1.0 - sig)))).astype(dab_ref.dtype)
            dab_ref[1, rows, :] = (du * a * sig).astype(dab_ref.dtype)

    halves = pl.BlockSpec((2, None, tm, HALF_FF), lambda j, i: (0, j, i, 0))
    dab = pl.pallas_call(
        body, grid=(2, s // tm),
        in_specs=[pl.BlockSpec((tm, d), lambda j, i: (i, 0)), pl.BlockSpec((HALF_FF, d), lambda j, i: (j, 0)), ANY] + [ANY] * len(after),
        out_specs=halves, out_shape=jax.ShapeDtypeStruct((2, 2, s, HALF_FF), BF16),
        scratch_shapes=[pltpu.VMEM((RING_SLOTS, 2, tm, HALF_FF), BF16), pltpu.SemaphoreType.DMA((RING_SLOTS,))],
        compiler_params=_params(("arbitrary", "arbitrary")), name=name,
    )(dz, w_out, ab.reshape(2, 2, s, HALF_FF), *after)
    return dab.reshape(4, s, HALF_FF)


def _tri(strict):
    r = lax.broadcasted_iota(jnp.int32, (CHUNK, CHUNK), 0)
    c = lax.broadcasted_iota(jnp.int32, (CHUNK, CHUNK), 1)
    return (r > c).astype(F32) if strict else (r >= c).astype(F32)


def _gla_fwd(pg, wfu, b_f, gnorm, name):
    s = pg.shape[0]
    ts = _tile(s, TS_GLA, CHUNK)
    cpb = ts // CHUNK
    nc = s // CHUNK

    def body(pg_ref, wfu_ref, bf_ref, gn_ref, ya_ref, sp_ref, so_ref, o_ref, st_ref, la_ref, dec_ref, u_ref):
        @pl.when(pl.program_id(0) == 0)
        def _():
            st_ref[...] = jnp.zeros_like(st_ref)

        f = jnp.dot(pg_ref[:, PG_F:PG_W], wfu_ref[...], preferred_element_type=F32) + bf_ref[...]
        la_ref[...] = _log_sigmoid(f) * (1.0 / GATE_TEMP)
        tri = _tri(False)
        chunks = [slice(ci * CHUNK, (ci + 1) * CHUNK) for ci in range(cpb)]
        for ci, rows in enumerate(chunks):
            la = la_ref[rows, :]
            b = jnp.dot(tri, la, precision=HIGHEST, preferred_element_type=F32)
            bend = jnp.sum(la, axis=0, keepdims=True)
            e = jnp.exp(bend - b)
            dec_ref[ci:ci + 1, :] = jnp.exp(bend)
            for hd in range(HEADS):
                k = pg_ref[rows, PG_K + hd * HDK:PG_K + (hd + 1) * HDK]
                v = pg_ref[rows, PG_V + hd * HDV:PG_V + (hd + 1) * HDV]
                kt = (k.astype(F32) * e[:, hd * HDK:(hd + 1) * HDK]).astype(BF16)
                u_ref[ci, hd] = lax.dot_general(v, kt, (((0,), (0,)), ((), ())), preferred_element_type=F32)
        for ci in range(cpb):
            for hd in range(HEADS):
                prev = st_ref[hd]
                sp_ref[ci, hd] = prev
                st = prev * dec_ref[ci:ci + 1, hd * HDK:(hd + 1) * HDK] + u_ref[ci, hd]
                st_ref[hd] = st
                so_ref[ci, hd] = st.astype(so_ref.dtype)
        for ci, rows in enumerate(chunks):
            for hd in range(HEADS):
                vc = slice(hd * HDV, (hd + 1) * HDV)
                q = pg_ref[rows, PG_Q + hd * HDK:PG_Q + (hd + 1) * HDK]
                go = pg_ref[rows, PG_G + hd * HDV:PG_G + (hd + 1) * HDV].astype(F32)
                qs = (q.astype(F32) * Q_SCALE).astype(BF16)
                o = lax.dot_general(qs, so_ref[ci, hd], (((1,), (1,)), ((), ())), preferred_element_type=F32)
                o_ref[rows, vc] = o
                oh, _ = _rms(o)
                ya_ref[rows, vc] = (oh * gn_ref[:, vc] * (go * _sigmoid(go))).astype(ya_ref.dtype)

    return pl.pallas_call(
        body, grid=(s // ts,),
        in_specs=[_rows(ts, PG_W), _fixed((LANE, HEADS * HDK)), _fixed((1, HEADS * HDK)), _fixed((1, HEADS * HDV))],
        out_specs=[_rows(ts, HEADS * HDV), pl.BlockSpec((cpb, HEADS, HDV, HDK), lambda i: (i, 0, 0, 0)),
                   pl.BlockSpec((cpb, HEADS, HDV, HDK), lambda i: (i, 0, 0, 0)), _rows(ts, HEADS * HDV)],
        out_shape=[jax.ShapeDtypeStruct((s, HEADS * HDV), BF16), jax.ShapeDtypeStruct((nc, HEADS, HDV, HDK), F32),
                   jax.ShapeDtypeStruct((nc, HEADS, HDV, HDK), BF16), jax.ShapeDtypeStruct((s, HEADS * HDV), F32)],
        scratch_shapes=[pltpu.VMEM((HEADS, HDV, HDK), F32), pltpu.VMEM((ts, HEADS * HDK), F32),
                        pltpu.VMEM((max(cpb, 8), HEADS * HDK), F32), pltpu.VMEM((cpb, HEADS, HDV, HDK), F32)],
        compiler_params=_params(("arbitrary",)), name=name,
    )(pg, wfu, b_f, gnorm)


def _gla_bwd(pg, sp, so, o, dya, wfu, b_f, gnorm, name):
    s = pg.shape[0]
    ts = _tile(s, TS_GLA, CHUNK)
    cpb = ts // CHUNK
    nblk = s // ts

    def body(pg_ref, sp_ref, so_ref, o_ref, dya_ref, wfu_ref, bf_ref, gn_ref, dpg_ref, dwfu_ref, dbf_ref, dgn_ref,
             dst_ref, la_ref, sg_ref, df_ref, e_ref, ktf_ref, dec_ref, g_ref):
        @pl.when(pl.program_id(0) == 0)
        def _():
            dst_ref[...] = jnp.zeros_like(dst_ref)
            dwfu_ref[...] = jnp.zeros_like(dwfu_ref)
            dbf_ref[...] = jnp.zeros_like(dbf_ref)
            dgn_ref[...] = jnp.zeros_like(dgn_ref)

        flow = pg_ref[:, PG_F:PG_W]
        f = jnp.dot(flow, wfu_ref[...], preferred_element_type=F32) + bf_ref[...]
        la_ref[...] = _log_sigmoid(f) * (1.0 / GATE_TEMP)
        sg_ref[...] = _sigmoid(-f) * (1.0 / GATE_TEMP)
        tri = _tri(False)
        tri_strict = _tri(True)
        chunks = [slice(ci * CHUNK, (ci + 1) * CHUNK) for ci in range(cpb)]
        for ci, rows in enumerate(chunks):
            la = la_ref[rows, :]
            b = jnp.dot(tri, la, precision=HIGHEST, preferred_element_type=F32)
            bend = jnp.sum(la, axis=0, keepdims=True)
            e = jnp.exp(bend - b)
            e_ref[rows, :] = e
            dec = jnp.exp(bend)
            dec_ref[ci:ci + 1, :] = dec
            for hd in range(HEADS):
                kc = slice(hd * HDK, (hd + 1) * HDK)
                vc = slice(hd * HDV, (hd + 1) * HDV)
                q = pg_ref[rows, PG_Q + hd * HDK:PG_Q + (hd + 1) * HDK]
                k = pg_ref[rows, PG_K + hd * HDK:PG_K + (hd + 1) * HDK]
                go = pg_ref[rows, PG_G + hd * HDV:PG_G + (hd + 1) * HDV].astype(F32)
                ktf_ref[rows, kc] = k.astype(F32) * e[:, kc]
                st_b = so_ref[ci, hd]
                qs = (q.astype(F32) * Q_SCALE).astype(BF16)
                oh, r = _rms(o_ref[rows, vc])
                gh = gn_ref[:, vc]
                sig = _sigmoid(go)
                dy = dya_ref[rows, vc].astype(F32)
                don = dy * (go * sig)
                dgn_ref[:, vc] += jnp.sum(don * oh, axis=0, keepdims=True)
                dong = don * gh
                do = (r * (dong - oh * jnp.mean(dong * oh, axis=-1, keepdims=True))).astype(BF16)
                g_ref[ci, hd] = lax.dot_general(do, qs, (((0,), (0,)), ((), ())), preferred_element_type=F32)
                dq = jnp.dot(do, st_b, preferred_element_type=F32) * Q_SCALE
                dpg_ref[rows, PG_Q + hd * HDK:PG_Q + (hd + 1) * HDK] = dq.astype(dpg_ref.dtype)
                dgo = dy * (oh * gh) * (sig * (1.0 + go * (1.0 - sig)))
                dpg_ref[rows, PG_G + hd * HDV:PG_G + (hd + 1) * HDV] = dgo.astype(dpg_ref.dtype)
        for ci in reversed(range(cpb)):
            for hd in range(HEADS):
                dst = dst_ref[hd] + g_ref[ci, hd]
                g_ref[ci, hd] = dst
                dst_ref[hd] = dst * dec_ref[ci:ci + 1, hd * HDK:(hd + 1) * HDK]
        for ci, rows in enumerate(chunks):
            for hd in range(HEADS):
                kc = slice(hd * HDK, (hd + 1) * HDK)
                v = pg_ref[rows, PG_V + hd * HDV:PG_V + (hd + 1) * HDV]
                ktf = ktf_ref[rows, kc]
                dst = g_ref[ci, hd]
                dst_b = dst.astype(BF16)
                dkt = jnp.dot(v, dst_b, preferred_element_type=F32)
                dv = lax.dot_general(ktf.astype(BF16), dst_b, (((1,), (1,)), ((), ())), preferred_element_type=F32)
                dd = jnp.sum(dst * sp_ref[ci, hd], axis=0, keepdims=True)
                dla = jnp.dot(tri_strict, dkt * ktf, precision=HIGHEST, preferred_element_type=F32) + dd * dec_ref[ci:ci + 1, kc]
                df_ref[rows, kc] = dla * sg_ref[rows, kc]
                dpg_ref[rows, PG_K + hd * HDK:PG_K + (hd + 1) * HDK] = (dkt * e_ref[rows, kc]).astype(dpg_ref.dtype)
                dpg_ref[rows, PG_V + hd * HDV:PG_V + (hd + 1) * HDV] = dv.astype(dpg_ref.dtype)
        df = df_ref[...]
        df_b = df.astype(BF16)
        dpg_ref[:, PG_F:PG_W] = lax.dot_general(df_b, wfu_ref[...], (((1,), (1,)), ((), ())), preferred_element_type=F32).astype(dpg_ref.dtype)
        dwfu_ref[...] += lax.dot_general(flow, df_b, (((0,), (0,)), ((), ())), preferred_element_type=F32)
        dbf_ref[...] += jnp.sum(df, axis=0, keepdims=True)

    rev = lambda i: (nblk - 1 - i, 0)
    return pl.pallas_call(
        body, grid=(nblk,),
        in_specs=[pl.BlockSpec((ts, PG_W), rev), pl.BlockSpec((cpb, HEADS, HDV, HDK), lambda i: (nblk - 1 - i, 0, 0, 0)),
                  pl.BlockSpec((cpb, HEADS, HDV, HDK), lambda i: (nblk - 1 - i, 0, 0, 0)), pl.BlockSpec((ts, HEADS * HDV), rev),
                  pl.BlockSpec((ts, HEADS * HDV), rev), _fixed((LANE, HEADS * HDK)), _fixed((1, HEADS * HDK)), _fixed((1, HEADS * HDV))],
        out_specs=[pl.BlockSpec((ts, PG_W), rev), _fixed((LANE, HEADS * HDK)), _fixed((1, HEADS * HDK)), _fixed((1, HEADS * HDV))],
        out_shape=[jax.ShapeDtypeStruct((s, PG_W), BF16), jax.ShapeDtypeStruct((LANE, HEADS * HDK), F32),
                   jax.ShapeDtypeStruct((1, HEADS * HDK), F32), jax.ShapeDtypeStruct((1, HEADS * HDV), F32)],
        scratch_shapes=[pltpu.VMEM((HEADS, HDV, HDK), F32)] + [pltpu.VMEM((ts, HEADS * HDK), F32)] * 5
        + [pltpu.VMEM((max(cpb, 8), HEADS * HDK), F32), pltpu.VMEM((cpb, HEADS, HDV, HDK), F32)],
        compiler_params=_params(("arbitrary",)), name=name,
    )(pg, sp, so, o, dya, wfu, b_f, gnorm)


def _window_sums(ext, sign):
    n = ext.shape[0]
    sums = {1: ext}
    w = 1
    while w < POOL_WINDOWS[-1]:
        sums[2 * w] = sums[w] + pltpu.roll(sums[w], w if sign > 0 else n - w, 0)
        w *= 2
    return [sums[POOL_WINDOWS[g]][:, g * LANE:(g + 1) * LANE] for g in range(len(POOL_WINDOWS))]


def _pool_counts(row0, n):
    pos = (row0 + lax.broadcasted_iota(jnp.int32, (n, 1), 0) + 1).astype(F32)
    return [1.0 / jnp.minimum(pos, float(w)) for w in POOL_WINDOWS]


def _pool_fwd(ppx, w_pool, pool_scale, name):
    s = ppx.shape[0]
    ts = _tile(s, TS_POOL, POOL_HALO)
    hb = ts // POOL_HALO
    pw = len(POOL_WINDOWS) * LANE

    def body(p_ref, halo_ref, w_ref, sc_ref, y_ref, ext_ref):
        i = pl.program_id(0)
        p = p_ref[...].astype(F32)
        ext_ref[0:POOL_HALO, :] = jnp.where(i > 0, halo_ref[...].astype(F32), 0.0)
        ext_ref[POOL_HALO:, :] = p
        sums = _window_sums(ext_ref[...], +1)
        cnt = _pool_counts(i * ts, ts)
        for g in range(len(POOL_WINDOWS)):
            cols = slice(g * LANE, (g + 1) * LANE)
            mixed = sums[g][POOL_HALO:, :] * cnt[g] - p[:, cols]
            y = jnp.dot(mixed.astype(BF16), w_ref[g], preferred_element_type=F32)
            y_ref[:, cols] = (y * sc_ref[:, cols]).astype(y_ref.dtype)

    return pl.pallas_call(
        body, grid=(s // ts,),
        in_specs=[pl.BlockSpec((ts, pw), lambda i: (i, 0)), pl.BlockSpec((POOL_HALO, pw), lambda i: (jnp.maximum(i * hb - 1, 0), 0)),
                  _fixed((len(POOL_WINDOWS), LANE, LANE)), _fixed((1, pw))],
        out_specs=_rows(ts, pw), out_shape=jax.ShapeDtypeStruct((s, pw), BF16),
        scratch_shapes=[pltpu.VMEM((ts + POOL_HALO, pw), F32)],
        compiler_params=_params(("parallel",)), name=name,
    )(ppx, ppx, w_pool, pool_scale)


def _pool_bwd(dyb, ppx, w_pool, pool_scale, name):
    s = ppx.shape[0]
    ts = _tile(s, TS_POOL, POOL_HALO)
    hb = ts // POOL_HALO
    nblk = s // ts
    last_halo = s // POOL_HALO - 1
    ng = len(POOL_WINDOWS)
    pw = ng * LANE

    def body(p_ref, halo_ref, dy_ref, dyn_ref, w_ref, sc_ref, dp_ref, dw_ref, dsc_ref, ext_ref, dext_ref, dm_ref):
        i = pl.program_id(0)

        @pl.when(i == 0)
        def _():
            dw_ref[...] = jnp.zeros_like(dw_ref)
            dsc_ref[...] = jnp.zeros_like(dsc_ref)

        p = p_ref[...].astype(F32)
        ext_ref[0:POOL_HALO, :] = jnp.where(i > 0, halo_ref[...].astype(F32), 0.0)
        ext_ref[POOL_HALO:, :] = p
        sums = _window_sums(ext_ref[...], +1)
        cnt = _pool_counts(i * ts, ts + POOL_HALO)
        sc = sc_ref[...]
        dy = dy_ref[...].astype(F32)
        dyn = jnp.where(i < nblk - 1, dyn_ref[...].astype(F32), 0.0)
        for g in range(ng):
            cols = slice(g * LANE, (g + 1) * LANE)
            wg = w_ref[g]
            mixed = (sums[g][POOL_HALO:, :] * cnt[g][0:ts] - p[:, cols]).astype(BF16)
            ypre = jnp.dot(mixed, wg, preferred_element_type=F32)
            dsc_ref[:, cols] += jnp.sum(dy[:, cols] * ypre, axis=0, keepdims=True)
            dyp = (dy[:, cols] * sc[:, cols]).astype(BF16)
            dypn = (dyn[:, cols] * sc[:, cols]).astype(BF16)
            dw_ref[g] += lax.dot_general(mixed, dyp, (((0,), (0,)), ((), ())), preferred_element_type=F32)
            dm = lax.dot_general(dyp, wg, (((1,), (1,)), ((), ())), preferred_element_type=F32)
            dmn = lax.dot_general(dypn, wg, (((1,), (1,)), ((), ())), preferred_element_type=F32)
            dext_ref[0:ts, cols] = dm * cnt[g][0:ts]
            dext_ref[ts:, cols] = dmn * cnt[g][ts:]
            dm_ref[:, cols] = dm
        lead = _window_sums(dext_ref[...], -1)
        for g in range(ng):
            cols = slice(g * LANE, (g + 1) * LANE)
            dp_ref[:, cols] = (lead[g][0:ts, :] - dm_ref[:, cols]).astype(dp_ref.dtype)

    return pl.pallas_call(
        body, grid=(nblk,),
        in_specs=[pl.BlockSpec((ts, pw), lambda i: (i, 0)), pl.BlockSpec((POOL_HALO, pw), lambda i: (jnp.maximum(i * hb - 1, 0), 0)),
                  pl.BlockSpec((ts, pw), lambda i: (i, 0)), pl.BlockSpec((POOL_HALO, pw), lambda i: (jnp.minimum((i + 1) * hb, last_halo), 0)),
                  _fixed((ng, LANE, LANE)), _fixed((1, pw))],
        out_specs=[_rows(ts, pw), _fixed((ng, LANE, LANE)), _fixed((1, pw))],
        out_shape=[jax.ShapeDtypeStruct((s, pw), BF16), jax.ShapeDtypeStruct((ng, LANE, LANE), F32), jax.ShapeDtypeStruct((1, pw), F32)],
        scratch_shapes=[pltpu.VMEM((ts + POOL_HALO, pw), F32), pltpu.VMEM((ts + POOL_HALO, pw), F32), pltpu.VMEM((ts, pw), F32)],
        compiler_params=_params(("arbitrary",)), name=name,
    )(ppx, ppx, dyb, dyb, w_pool, pool_scale)


def _xattn_fwd(ppx, kv, name):
    s = ppx.shape[0]
    m = kv.shape[0]
    ts = _tile(s, TS_XA, 8)
    xw = XA_HEADS * XA_HD

    def body(q_ref, kv_ref, o_ref):
        for hd in range(XA_HEADS):
            cols = slice(hd * XA_HD, (hd + 1) * XA_HD)
            k = kv_ref[:, hd * XA_HD:(hd + 1) * XA_HD]
            v = kv_ref[:, xw + hd * XA_HD:xw + (hd + 1) * XA_HD]
            sc = lax.dot_general(q_ref[:, cols], k, (((1,), (1,)), ((), ())), preferred_element_type=F32) * XA_SCALE
            ex = jnp.exp(sc - jnp.max(sc, axis=-1, keepdims=True))
            pr = ex * (1.0 / jnp.sum(ex, axis=-1, keepdims=True))
            o_ref[:, cols] = jnp.dot(pr.astype(BF16), v, preferred_element_type=F32).astype(o_ref.dtype)

    return pl.pallas_call(
        body, grid=(s // ts,), in_specs=[pl.BlockSpec((ts, xw), lambda i: (i, 1)), _fixed((m, 2 * xw))],
        out_specs=_rows(ts, xw), out_shape=jax.ShapeDtypeStruct((s, xw), BF16),
        compiler_params=_params(("parallel",)), name=name,
    )(ppx, kv)


def _xattn_bwd(dxc, ppx, kv, name):
    s = ppx.shape[0]
    m = kv.shape[0]
    ts = _tile(s, TS_XA, 8)
    xw = XA_HEADS * XA_HD

    def body(do_ref, q_ref, kv_ref, dq_ref, dkv_ref):
        @pl.when(pl.program_id(0) == 0)
        def _():
            dkv_ref[...] = jnp.zeros_like(dkv_ref)

        for hd in range(XA_HEADS):
            cols = slice(hd * XA_HD, (hd + 1) * XA_HD)
            vcols = slice(xw + hd * XA_HD, xw + (hd + 1) * XA_HD)
            q = q_ref[:, cols]
            k = kv_ref[:, cols]
            v = kv_ref[:, vcols]
            do = do_ref[:, cols]
            sc = lax.dot_general(q, k, (((1,), (1,)), ((), ())), preferred_element_type=F32) * XA_SCALE
            ex = jnp.exp(sc - jnp.max(sc, axis=-1, keepdims=True))
            pr = ex * (1.0 / jnp.sum(ex, axis=-1, keepdims=True))
            dpr = lax.dot_general(do, v, (((1,), (1,)), ((), ())), preferred_element_type=F32)
            dsc = (pr * (dpr - jnp.sum(dpr * pr, axis=-1, keepdims=True)) * XA_SCALE).astype(BF16)
            dq_ref[:, cols] = jnp.dot(dsc, k, preferred_element_type=F32).astype(dq_ref.dtype)
            dkv_ref[:, cols] += lax.dot_general(dsc, q, (((0,), (0,)), ((), ())), preferred_element_type=F32)
            dkv_ref[:, vcols] += lax.dot_general(pr.astype(BF16), do, (((0,), (0,)), ((), ())), preferred_element_type=F32)

    return pl.pallas_call(
        body, grid=(s // ts,), in_specs=[_rows(ts, xw), pl.BlockSpec((ts, xw), lambda i: (i, 1)), _fixed((m, 2 * xw))],
        out_specs=[_rows(ts, xw), _fixed((m, 2 * xw))],
        out_shape=[jax.ShapeDtypeStruct((s, xw), BF16), jax.ShapeDtypeStruct((m, 2 * xw), F32)],
        compiler_params=_params(("arbitrary",)), name=name,
    )(dxc, ppx, kv)


def _resident(shape):
    nd = len(shape)
    return pl.BlockSpec(shape, lambda i: (0,) * nd, pipeline_mode=pl.Buffered(1))


def _mix_in_fwd(h, w_ts, after, name, tm=512):
    s, d = h.shape
    tm = _tile(s, tm)
    n, na = len(w_ts), len(after)

    def body(h_ref, *refs):
        w_refs, o_refs = refs[:n], refs[n + na:]
        for rows in _sub_blocks(tm):
            hv = h_ref[rows, :]
            for w_ref, o_ref in zip(w_refs, o_refs):
                o_ref[rows, :] = lax.dot_general(hv, w_ref[...], (((1,), (1,)), ((), ())), preferred_element_type=F32).astype(o_ref.dtype)

    return pl.pallas_call(
        body, grid=(s // tm,), in_specs=[_rows(tm, d)] + [_resident(w.shape) for w in w_ts] + [ANY] * na,
        out_specs=[_rows(tm, w.shape[0]) for w in w_ts],
        out_shape=[jax.ShapeDtypeStruct((s, w.shape[0]), BF16) for w in w_ts],
        compiler_params=_params(("parallel",)), name=name,
    )(h, *w_ts, *after)


def _mix_tail_fwd(ya_in, yb_in, xc, pgt, w_ups, w_o, x, g_post, g_next, after, name, tm=512):
    s, d = x.shape
    tm = _tile(s, tm)
    na = len(after)
    branch_ins = (ya_in, yb_in, xc)

    def body(a_ref, b_ref, c_ref, gt_ref, wa_ref, wb_ref, wc_ref, wo_ref, x_ref, gp_ref, gn_ref, *rest):
        ya_ref, yb_ref, yc_ref, m_ref, y_ref, xo_ref, h_ref = rest[na:]
        for rows in _sub_blocks(tm):
            merged = None
            for j, (in_ref, w_ref, out_ref) in enumerate(((a_ref, wa_ref, ya_ref), (b_ref, wb_ref, yb_ref), (c_ref, wc_ref, yc_ref))):
                yj = jnp.dot(in_ref[rows, :], w_ref[...], preferred_element_type=F32)
                out_ref[rows, :] = yj.astype(out_ref.dtype)
                part = _sigmoid(gt_ref[rows, j * D:(j + 1) * D].astype(F32)) * yj
                merged = part if merged is None else merged + part
            merged_b = merged.astype(m_ref.dtype)
            m_ref[rows, :] = merged_b
            y = jnp.dot(merged_b, wo_ref[...], preferred_element_type=F32)
            y_ref[rows, :] = y
            yh, _ = _rms(y)
            xn = x_ref[rows, :] + yh * gp_ref[...]
            xo_ref[rows, :] = xn
            xh, _ = _rms(xn)
            h_ref[rows, :] = (xh * gn_ref[...]).astype(h_ref.dtype)

    bf = lambda: jax.ShapeDtypeStruct((s, d), BF16)
    f32 = lambda: jax.ShapeDtypeStruct((s, d), F32)
    return pl.pallas_call(
        body, grid=(s // tm,),
        in_specs=[_rows(tm, a.shape[1]) for a in branch_ins] + [_rows(tm, 3 * d)] + [_resident(w.shape) for w in w_ups]
        + [_resident(w_o.shape), _rows(tm, d), _fixed((1, d)), _fixed((1, d))] + [ANY] * na,
        out_specs=[_rows(tm, d)] * 7,
        out_shape=[bf(), bf(), bf(), bf(), f32(), f32(), bf()],
        compiler_params=_params(("parallel",)), name=name,
    )(*branch_ins, pgt, *w_ups, w_o, x, g_post, g_next, *after)


def _mix_tail_bwd(dy, pgt, ys, w_ups, w_o, after, name, tm=512):
    s, d = dy.shape
    tm = _tile(s, tm)
    na = len(after)
    widths = [w.shape[0] for w in w_ups]

    def body(dy_ref, gt_ref, ya_ref, yb_ref, yc_ref, wa_ref, wb_ref, wc_ref, wo_ref, *rest):
        dya_ref, dyb_ref, dyc_ref, dgt_ref, da_ref, db_ref, dc_ref = rest[na:]
        nt = (((1,), (1,)), ((), ()))
        for rows in _sub_blocks(tm):
            dm = lax.dot_general(dy_ref[rows, :], wo_ref[...], nt, preferred_element_type=F32)
            for j, (y_ref, dyj_ref, w_ref, din_ref) in enumerate(((ya_ref, dya_ref, wa_ref, da_ref), (yb_ref, dyb_ref, wb_ref, db_ref),
                                                                   (yc_ref, dyc_ref, wc_ref, dc_ref))):
                sig = _sigmoid(gt_ref[rows, j * D:(j + 1) * D].astype(F32))
                dyj = (dm * sig).astype(dyj_ref.dtype)
                dyj_ref[rows, :] = dyj
                dgt_ref[rows, j * D:(j + 1) * D] = (dm * y_ref[rows, :].astype(F32) * sig * (1.0 - sig)).astype(dgt_ref.dtype)
                din_ref[rows, :] = lax.dot_general(dyj, w_ref[...], nt, preferred_element_type=F32).astype(din_ref.dtype)

    bf = lambda w: jax.ShapeDtypeStruct((s, w), BF16)
    return pl.pallas_call(
        body, grid=(s // tm,),
        in_specs=[_rows(tm, d), _rows(tm, 3 * d)] + [_rows(tm, d)] * 3 + [_resident(w.shape) for w in w_ups] + [_resident(w_o.shape)]
        + [ANY] * na,
        out_specs=[_rows(tm, d)] * 3 + [_rows(tm, 3 * d)] + [_rows(tm, w) for w in widths],
        out_shape=[bf(d), bf(d), bf(d), bf(3 * d)] + [bf(w) for w in widths],
        compiler_params=_params(("parallel",)), name=name,
    )(dy, pgt, *ys, *w_ups, w_o, *after)


def _adam_math(w, g, m, v):
    mn = ADAM_B1 * m + (1.0 - ADAM_B1) * g
    vn = ADAM_B2 * v + (1.0 - ADAM_B2) * (g * g)
    m_hat = mn / (1.0 - ADAM_B1 ** ADAM_STEP)
    v_hat = vn / (1.0 - ADAM_B2 ** ADAM_STEP)
    return -ADAM_LR * (m_hat / (jnp.sqrt(v_hat) + ADAM_EPS) + ADAM_WD * w), mn, vn


def _adamw(w, g, m, v, name):
    r, c = w.shape[-2:]
    tr, tc = _block_of(r, c, cap=512 if r % 16 == 0 else 256)

    def spec(a):
        if a.ndim == 2:
            return pl.BlockSpec((tr, tc), lambda i, j: (i, j))
        return pl.BlockSpec((None, tr, tc), lambda i, j: (0, i, j))

    def body(w_ref, g_ref, m_ref, v_ref, d_ref, mo_ref, vo_ref):
        d_ref[...], mo_ref[...], vo_ref[...] = _adam_math(w_ref[...], g_ref[...], m_ref[...], v_ref[...])

    return pl.pallas_call(
        body, grid=(r // tr, c // tc), in_specs=[spec(a) for a in (w, g, m, v)], out_specs=[spec(w)] * 3,
        out_shape=[jax.ShapeDtypeStruct(w.shape, F32)] * 3, compiler_params=_params(("parallel", "parallel")), name=name,
    )(w, g, m, v)


ANY = pl.BlockSpec(memory_space=pl.ANY)


def _place():
    x, y, c = lax.axis_index("x"), lax.axis_index("y"), lax.axis_index("c")
    chips = [(1 - x, y), (x, 1 - y), (1 - x, 1 - y)]
    return x, y, c, chips


def _by_cols(rows):
    return rows % 32 != 0 and rows != 16


def _half_of(ref, lead, c):
    r, cols = ref.shape[-2:]
    if _by_cols(r):
        return ref.at[(*lead, slice(None), pl.ds(pl.multiple_of(c * (cols // 2), LANE), cols // 2))]
    return ref.at[(*lead, pl.ds(pl.multiple_of(c * (r // 2), 8), r // 2))]


def _half_shape(shape):
    r, cols = shape[-2:]
    return shape[:-2] + ((r, cols // 2) if _by_cols(r) else (r // 2, cols))


def _block_of(r, cols, cap=256):
    if r % 16 == 0:
        return _tile(r, cap, 16), cols
    return r, _tile(cols, cap)


def _place_shard(shard, chip_arr, out_dtype, name, after=()):
    _, r, cols = shard.shape
    tr, tc = _block_of(r, cols)

    def body(chip_ref, s_ref, *rest):
        o_ref = rest[len(after)]
        o_ref[...] = s_ref[...].astype(o_ref.dtype)

    return pl.pallas_call(
        body,
        grid_spec=pltpu.PrefetchScalarGridSpec(
            num_scalar_prefetch=1, grid=(r // tr, cols // tc),
            in_specs=[pl.BlockSpec((None, tr, tc), lambda i, j, chip_ref: (0, i, j))] + [ANY] * len(after),
            out_specs=pl.BlockSpec((None, tr, tc), lambda i, j, chip_ref: (chip_ref[0], i, j))),
        out_shape=jax.ShapeDtypeStruct((4, r, cols), out_dtype),
        compiler_params=_params(("parallel", "parallel")), name=name,
    )(chip_arr, shard, *after)


HBM = pl.BlockSpec(memory_space=pltpu.HBM)
SEM = pl.BlockSpec(memory_space=pltpu.SEMAPHORE)
EFFECT = pltpu.SideEffectType.DATAFLOW_SIDE_EFFECTING


def _in_hbm(arrays):
    return [pltpu.with_memory_space_constraint(a, pltpu.HBM) for a in arrays]


def _gather_start(bufs, after, name):
    n, na = len(bufs), len(after)

    def body(*refs):
        send_sem, recv_sem = refs[n + na], refs[n + na + 1]
        outs = refs[n + na + 2:2 * n + na + 2]
        token = refs[2 * n + na + 2]
        x, y, c, chips = _place()
        me = 2 * x + y
        for p, chip in enumerate(chips):
            for w in range(n):
                block = _half_of(outs[w], (me,), c)
                pltpu.make_async_remote_copy(
                    src_ref=block, dst_ref=block, send_sem=send_sem, recv_sem=recv_sem,
                    device_id=(*chip, c), device_id_type=MESH).start()
        token[...] = jnp.zeros_like(token)

    out = pl.pallas_call(
        body, name=name, in_specs=[HBM] * n + [ANY] * na,
        out_specs=[SEM, SEM] + [HBM] * n + [pl.BlockSpec(memory_space=pltpu.VMEM)],
        out_shape=[pltpu.SemaphoreType.DMA(()), pltpu.SemaphoreType.DMA(())]
        + [pltpu.HBM(a.shape, a.dtype) for a in bufs] + [jax.ShapeDtypeStruct((8, LANE), F32)],
        input_output_aliases={w: w + 2 for w in range(n)},
        compiler_params=pltpu.CompilerParams(has_side_effects=EFFECT),
    )(*_in_hbm(bufs), *after)
    return out[0], out[1], list(out[2:2 + n]), out[2 + n]


def _gather_pass(bufs, send_sem, recv_sem, after, name):
    n, na = len(bufs), len(after)

    def body(*refs):
        send1, recv1 = refs[n], refs[n + 1]
        send2, recv2 = refs[n + 2 + na], refs[n + 3 + na]
        outs = refs[n + 4 + na:2 * n + 4 + na]
        x, y, c, chips = _place()
        me = 2 * x + y
        arrivals = [(w, px, py) for px, py in chips for w in range(n)]
        for w, px, py in arrivals:
            first = pltpu.make_async_remote_copy(
                src_ref=_half_of(outs[w], (me,), c), dst_ref=_half_of(outs[w], (2 * px + py,), c), send_sem=send1, recv_sem=recv1,
                device_id=(px, py, c), device_id_type=MESH)
            first.wait_send()
            first.wait_recv()
        for w, px, py in arrivals:
            arrived = _half_of(outs[w], (2 * px + py,), c)
            pltpu.make_async_remote_copy(
                src_ref=arrived, dst_ref=arrived, send_sem=send2, recv_sem=recv2,
                device_id=(x, y, 1 - c), device_id_type=MESH).start()

    out = pl.pallas_call(
        body, name=name, in_specs=[HBM] * n + [SEM, SEM] + [ANY] * na,
        out_specs=[SEM, SEM] + [HBM] * n,
        out_shape=[pltpu.SemaphoreType.DMA(()), pltpu.SemaphoreType.DMA(())] + [pltpu.HBM(a.shape, a.dtype) for a in bufs],
        input_output_aliases={w: w + 2 for w in range(n)},
        compiler_params=pltpu.CompilerParams(has_side_effects=EFFECT),
    )(*bufs, send_sem, recv_sem, *after)
    return out[0], out[1], list(out[2:])


def _gather_finish(bufs, send_sem, recv_sem, after, name):
    n, na = len(bufs), len(after)

    def body(*refs):
        send2, recv2 = refs[n], refs[n + 1]
        outs = refs[n + 2 + na:2 * n + 2 + na]
        x, y, c, chips = _place()
        for p, (px, py) in enumerate(chips):
            for w in range(n):
                passed = pltpu.make_async_remote_copy(
                    src_ref=_half_of(outs[w], (2 * px + py,), c), dst_ref=_half_of(outs[w], (2 * px + py,), 1 - c),
                    send_sem=send2, recv_sem=recv2, device_id=(x, y, 1 - c), device_id_type=MESH)
                passed.wait_send()
                passed.wait_recv()

    out = pl.pallas_call(
        body, name=name, in_specs=[HBM] * n + [SEM, SEM] + [ANY] * na, out_specs=[HBM] * n,
        out_shape=[pltpu.HBM(a.shape, a.dtype) for a in bufs],
        input_output_aliases={w: w for w in range(n)},
        compiler_params=pltpu.CompilerParams(has_side_effects=EFFECT),
    )(*bufs, send_sem, recv_sem, *after)
    return list(out)


def _pair_exchange_start(grads, after, name):
    n, na = len(grads), len(after)
    lands = [lax.empty(_half_shape(a.shape), a.dtype) for a in grads]

    def body(*refs):
        send_sem, recv_sem = refs[2 * n + na], refs[2 * n + na + 1]
        srcs = refs[2 * n + na + 2:3 * n + na + 2]
        dsts = refs[3 * n + na + 2:4 * n + na + 2]
        token = refs[4 * n + na + 2]
        x, y, c, _ = _place()
        for w in range(n):
            pltpu.make_async_remote_copy(
                src_ref=_half_of(srcs[w], (slice(None),), 1 - c), dst_ref=dsts[w], send_sem=send_sem, recv_sem=recv_sem,
                device_id=(x, y, 1 - c), device_id_type=MESH).start()
        token[...] = jnp.zeros_like(token)

    out = pl.pallas_call(
        body, name=name, in_specs=[HBM] * (2 * n) + [ANY] * na,
        out_specs=[SEM, SEM] + [HBM] * (2 * n) + [pl.BlockSpec(memory_space=pltpu.VMEM)],
        out_shape=[pltpu.SemaphoreType.DMA(()), pltpu.SemaphoreType.DMA(())]
        + [pltpu.HBM(a.shape, a.dtype) for a in grads + lands] + [jax.ShapeDtypeStruct((8, LANE), F32)],
        input_output_aliases={w: w + 2 for w in range(2 * n)},
        compiler_params=pltpu.CompilerParams(has_side_effects=EFFECT),
    )(*_in_hbm(grads), *_in_hbm(lands), *after)
    return out[0], out[1], list(out[2:2 + n]), list(out[2 + n:2 + 2 * n]), out[2 + 2 * n]


def _pair_exchange_finish(grads, lands, send_sem, recv_sem, after, name):
    n, na = len(grads), len(after)

    def body(*refs):
        send, recv = refs[2 * n], refs[2 * n + 1]
        srcs = refs[2 * n + 2 + na:3 * n + 2 + na]
        dsts = refs[3 * n + 2 + na:4 * n + 2 + na]
        x, y, c, _ = _place()
        for w in range(n):
            copy = pltpu.make_async_remote_copy(
                src_ref=_half_of(srcs[w], (slice(None),), 1 - c), dst_ref=dsts[w], send_sem=send, recv_sem=recv,
                device_id=(x, y, 1 - c), device_id_type=MESH)
            copy.wait_send()
            copy.wait_recv()

    out = pl.pallas_call(
        body, name=name, in_specs=[HBM] * (2 * n) + [SEM, SEM] + [ANY] * na, out_specs=[HBM] * (2 * n),
        out_shape=[pltpu.HBM(a.shape, a.dtype) for a in grads + lands],
        input_output_aliases={w: w for w in range(2 * n)},
        compiler_params=pltpu.CompilerParams(has_side_effects=EFFECT),
    )(*grads, *lands, send_sem, recv_sem, *after)
    return list(out[:n]), list(out[n:])


def _pair_sum(g, got, c_arr, name):
    _, r, cols = g.shape
    hr, hc = _half_shape((r, cols))
    tr, tc = _block_of(hr, hc)
    nbr, nbc = hr // tr, hc // tc
    by_cols = _by_cols(r)

    def body(c_ref, g_ref, got_ref, o_ref):
        o_ref[...] = (g_ref[...].astype(F32) + got_ref[...].astype(F32)).astype(o_ref.dtype)

    def mine(j, i, k, c_ref):
        return (j, i, c_ref[0] * nbc + k) if by_cols else (j, c_ref[0] * nbr + i, k)

    return pl.pallas_call(
        body,
        grid_spec=pltpu.PrefetchScalarGridSpec(
            num_scalar_prefetch=1, grid=(4, nbr, nbc),
            in_specs=[pl.BlockSpec((None, tr, tc), mine),
                      pl.BlockSpec((None, tr, tc), lambda j, i, k, c_ref: (j, i, k))],
            out_specs=pl.BlockSpec((None, tr, tc), lambda j, i, k, c_ref: (j, i, k))),
        out_shape=jax.ShapeDtypeStruct((4, hr, hc), BF16),
        compiler_params=_params(("parallel", "parallel", "parallel")), name=name,
    )(c_arr, *_in_hbm([g, got]))


def _chip_exchange_start(parts, after, name):
    n, na = len(parts), len(after)
    lands = [lax.empty((3,) + a.shape[1:], a.dtype) for a in parts]

    def body(*refs):
        send_sem, recv_sem = refs[2 * n + na], refs[2 * n + na + 1]
        srcs = refs[2 * n + na + 2:3 * n + na + 2]
        dsts = refs[3 * n + na + 2:4 * n + na + 2]
        token = refs[4 * n + na + 2]
        x, y, c, chips = _place()
        for p, (px, py) in enumerate(chips):
            for w in range(n):
                pltpu.make_async_remote_copy(
                    src_ref=srcs[w].at[2 * px + py], dst_ref=dsts[w].at[p], send_sem=send_sem, recv_sem=recv_sem,
                    device_id=(px, py, c), device_id_type=MESH).start()
        token[...] = jnp.zeros_like(token)

    out = pl.pallas_call(
        body, name=name, in_specs=[HBM] * (2 * n) + [ANY] * na,
        out_specs=[SEM, SEM] + [HBM] * (2 * n) + [pl.BlockSpec(memory_space=pltpu.VMEM)],
        out_shape=[pltpu.SemaphoreType.DMA(()), pltpu.SemaphoreType.DMA(())]
        + [pltpu.HBM(a.shape, a.dtype) for a in parts + lands] + [jax.ShapeDtypeStruct((8, LANE), F32)],
        input_output_aliases={w: w + 2 for w in range(2 * n)},
        compiler_params=pltpu.CompilerParams(has_side_effects=EFFECT),
    )(*_in_hbm(parts), *_in_hbm(lands), *after)
    return out[0], out[1], list(out[2:2 + n]), list(out[2 + n:2 + 2 * n]), out[2 + 2 * n]


def _chip_exchange_finish(parts, lands, send_sem, recv_sem, after, name):
    n, na = len(parts), len(after)

    def body(*refs):
        send, recv = refs[2 * n], refs[2 * n + 1]
        srcs = refs[2 * n + 2 + na:3 * n + 2 + na]
        dsts = refs[3 * n + 2 + na:4 * n + 2 + na]
        x, y, c, chips = _place()
        for p, (px, py) in enumerate(chips):
            for w in range(n):
                copy = pltpu.make_async_remote_copy(
                    src_ref=srcs[w].at[2 * px + py], dst_ref=dsts[w].at[p], send_sem=send, recv_sem=recv,
                    device_id=(px, py, c), device_id_type=MESH)
                copy.wait_send()
                copy.wait_recv()

    out = pl.pallas_call(
        body, name=name, in_specs=[HBM] * (2 * n) + [SEM, SEM] + [ANY] * na, out_specs=[HBM] * (2 * n),
        out_shape=[pltpu.HBM(a.shape, a.dtype) for a in parts + lands],
        input_output_aliases={w: w for w in range(2 * n)},
        compiler_params=pltpu.CompilerParams(has_side_effects=EFFECT),
    )(*parts, *lands, send_sem, recv_sem, *after)
    return list(out[:n]), list(out[n:])


def _chip_sum(part, got, place_arr, name):
    _, hr, hc = part.shape
    by_cols = _by_cols(hr)
    tr, tc = _block_of(hr, hc)
    nbr, nbc = hr // tr, hc // tc

    def body(place_ref, p_ref, got_ref, o_ref):
        acc = p_ref[...].astype(F32)
        for p in range(3):
            acc = acc + got_ref[p].astype(F32)
        o_ref[...] = acc

    def mine(i, k, place_ref):
        return (i, place_ref[1] * nbc + k) if by_cols else (place_ref[1] * nbr + i, k)

    return pl.pallas_call(
        body,
        grid_spec=pltpu.PrefetchScalarGridSpec(
            num_scalar_prefetch=1, grid=(nbr, nbc),
            in_specs=[pl.BlockSpec((None, tr, tc), lambda i, k, place_ref: (place_ref[0], i, k)),
                      pl.BlockSpec((3, tr, tc), lambda i, k, place_ref: (0, i, k))],
            out_specs=pl.BlockSpec((tr, tc), mine)),
        out_shape=jax.ShapeDtypeStruct((hr, 2 * hc) if by_cols else (2 * hr, hc), F32),
        compiler_params=_params(("parallel", "parallel")), name=name,
    )(place_arr, *_in_hbm([part, got]))


def _pair_join_start(bufs, name):
    n = len(bufs)

    def body(*refs):
        send_sem, recv_sem = refs[n], refs[n + 1]
        outs = refs[n + 2:2 * n + 2]
        token = refs[2 * n + 2]
        x, y, c, _ = _place()
        for w in range(n):
            block = _half_of(outs[w], (), c)
            pltpu.make_async_remote_copy(
                src_ref=block, dst_ref=block, send_sem=send_sem, recv_sem=recv_sem,
                device_id=(x, y, 1 - c), device_id_type=MESH).start()
        token[...] = jnp.zeros_like(token)

    out = pl.pallas_call(
        body, name=name, in_specs=[HBM] * n,
        out_specs=[SEM, SEM] + [HBM] * n + [pl.BlockSpec(memory_space=pltpu.VMEM)],
        out_shape=[pltpu.SemaphoreType.DMA(()), pltpu.SemaphoreType.DMA(())]
        + [pltpu.HBM(a.shape, a.dtype) for a in bufs] + [jax.ShapeDtypeStruct((8, LANE), F32)],
        input_output_aliases={w: w + 2 for w in range(n)},
        compiler_params=pltpu.CompilerParams(has_side_effects=EFFECT),
    )(*_in_hbm(bufs))
    return out[0], out[1], list(out[2:2 + n]), out[2 + n]


def _pair_join_finish(bufs, send_sem, recv_sem, after, name):
    n, na = len(bufs), len(after)

    def body(*refs):
        send, recv = refs[n], refs[n + 1]
        outs = refs[n + 2 + na:2 * n + 2 + na]
        x, y, c, _ = _place()
        for w in range(n):
            copy = pltpu.make_async_remote_copy(
                src_ref=_half_of(outs[w], (), c), dst_ref=_half_of(outs[w], (), 1 - c), send_sem=send, recv_sem=recv,
                device_id=(x, y, 1 - c), device_id_type=MESH)
            copy.wait_send()
            copy.wait_recv()

    out = pl.pallas_call(
        body, name=name, in_specs=[HBM] * n + [SEM, SEM] + [ANY] * na, out_specs=[HBM] * n,
        out_shape=[pltpu.HBM(a.shape, a.dtype) for a in bufs],
        input_output_aliases={w: w for w in range(n)},
        compiler_params=pltpu.CompilerParams(has_side_effects=EFFECT),
    )(*bufs, send_sem, recv_sem, *after)
    return list(out)


SMALL = ("ffn1_pre_g", "ffn1_post_g", "mix_pre_g", "gla_norm_g", "mem_norm_g", "mix_post_g", "ffn2_pre_g", "ffn2_post_g", "final_g",
         "b_f", "pool_scale", "w_pool", "w_fu")
N_GAINS = 9
SMALL_PACKS = ((16, D), (24, 512), (4 * LANE, LANE))
W_FU_ROW = 8


LOSS_ROW = 2


def _all_sum_small(gs, loss, name, after=()):
    ins = [gs[n] for n in SMALL[:N_GAINS]] + [gs["b_f"], gs["pool_scale"], gs["w_fu_pad"], gs["w_pool"].reshape(4 * LANE, LANE), loss]

    def body(*refs):
        gain_refs = refs[:N_GAINS]
        bf_ref, ps_ref, wfu_ref, wp_ref, loss_ref = refs[N_GAINS:N_GAINS + 5]
        outs = refs[N_GAINS + 5 + len(after):N_GAINS + 8 + len(after)]
        mine_a, mine_b, all_a, all_b, all_c, send_sems, recv_sems = refs[N_GAINS + 8 + len(after):]
        mine_a[...] = jnp.zeros_like(mine_a)
        for i, ref in enumerate(gain_refs):
            mine_a[i:i + 1, :] = ref[...]
        mine_b[...] = jnp.zeros_like(mine_b)
        mine_b[0:1, :] = bf_ref[...]
        mine_b[1:2, :] = ps_ref[...]
        mine_b[LOSS_ROW:LOSS_ROW + 1, 0:LANE] = loss_ref[0:1, :]
        mine_b[W_FU_ROW:W_FU_ROW + GATE_RANK, :] = wfu_ref[0:GATE_RANK, :]
        packs = ((mine_a, all_a), (mine_b, all_b), (wp_ref, all_c))
        x, y, c, chips = _place()
        me, sibling = (x, y, c), (x, y, 1 - c)

        def copy(t, k, block, to, own=False):
            px, py, pc = block
            slot = packs[t][1].at[4 * px + 2 * py + pc]
            return pltpu.make_async_remote_copy(
                src_ref=packs[t][0] if own else slot, dst_ref=slot,
                send_sem=send_sems.at[t, k], recv_sem=recv_sems.at[t, k], device_id=to, device_id_type=MESH)

        started = []
        for t, (mine, everyone) in enumerate(packs):
            everyone[4 * x + 2 * y + c] = mine[...]
            started.append(copy(t, 0, me, sibling, own=True))
            started += [copy(t, 1 + j, me, (*chip, c), own=True) for j, chip in enumerate(chips)]
        for cp in started:
            cp.start()
        passed = []
        for j, chip in enumerate(chips):
            for t in range(len(packs)):
                copy(t, 1 + j, (*chip, c), me).wait_recv()
                fwd = copy(t, 4 + j, (*chip, c), sibling)
                fwd.start()
                passed.append(fwd)
        for t in range(len(packs)):
            copy(t, 0, sibling, me).wait_recv()
            for j, chip in enumerate(chips):
                copy(t, 4 + j, (*chip, 1 - c), me).wait_recv()
        for cp in started + passed:
            cp.wait_send()
        for (_, everyone), o_ref in zip(packs, outs):
            acc = everyone[0]
            for k in range(1, 8):
                acc = acc + everyone[k]
            o_ref[...] = acc

    vmem = pl.BlockSpec(memory_space=pltpu.VMEM)
    return pl.pallas_call(
        body, in_specs=[vmem] * len(ins) + [ANY] * len(after), out_specs=[vmem] * 3,
        out_shape=[jax.ShapeDtypeStruct(shape, F32) for shape in SMALL_PACKS],
        scratch_shapes=[pltpu.VMEM(SMALL_PACKS[0], F32), pltpu.VMEM(SMALL_PACKS[1], F32)]
        + [pltpu.VMEM((8,) + shape, F32) for shape in SMALL_PACKS]
        + [pltpu.SemaphoreType.DMA((3, 7)), pltpu.SemaphoreType.DMA((3, 7))],
        compiler_params=pltpu.CompilerParams(has_side_effects=True, vmem_limit_bytes=VMEM_LIMIT), name=name,
    )(*ins, *after)


def _adamw_small(sums, params, chip_arr, name):
    flat = [a for n in SMALL for a in params[n]]

    def body(chip_ref, a_ref, b_ref, c_ref, *refs):
        ins, outs = refs[:len(flat)], refs[len(flat):]
        for i, n in enumerate(SMALL):
            w_ref, m_ref, v_ref = ins[3 * i:3 * i + 3]
            g_ref, d_ref, mo_ref, vo_ref = outs[4 * i:4 * i + 4]
            if n == "w_pool":
                pieces = [((0, k), c_ref[k * LANE:(k + 1) * LANE, :]) for k in range(4)]
            elif n == "w_fu":
                mine = pl.ds(pl.multiple_of(chip_ref[0] * LANE, LANE), LANE)
                pieces = [((0,), b_ref[W_FU_ROW:W_FU_ROW + GATE_RANK, mine])]
            elif n == "b_f":
                pieces = [((), b_ref[0:1, :])]
            elif n == "pool_scale":
                pieces = [((), b_ref[1:2, :])]
            else:
                pieces = [((), a_ref[i:i + 1, :])]
            for at, g in pieces:
                d, mn, vn = _adam_math(w_ref[at], g, m_ref[at], v_ref[at])
                g_ref[at] = g
                d_ref[at] = d
                mo_ref[at] = mn
                vo_ref[at] = vn

    def whole(shape):
        return pl.BlockSpec(shape, lambda i, chip_ref: (0,) * len(shape))

    out = pl.pallas_call(
        body,
        grid_spec=pltpu.PrefetchScalarGridSpec(
            num_scalar_prefetch=1, grid=(1,),
            in_specs=[whole(a.shape) for a in list(sums) + flat],
            out_specs=[whole(params[n][0].shape) for n in SMALL for _ in range(4)]),
        out_shape=[jax.ShapeDtypeStruct(params[n][0].shape, F32) for n in SMALL for _ in range(4)],
        compiler_params=_params(("arbitrary",)), name=name,
    )(chip_arr, *sums, *flat)
    return {n: tuple(out[4 * i:4 * i + 4]) for i, n in enumerate(SMALL)}


def _ffn_bwd(dz, x_norm, ab, u, w_in, w_out, x, g_pre, dres, tag, emit, advance, after=(), post=None):
    dw_out = _mm(u, dz, ta=True, out_dtype=BF16, tm=1408, tk=2048, after=after, name=tag + "_out_dw")
    behind = emit(tag + "_w_out", dw_out)
    dab = _ffn_out_dx_swiglu(dz, w_out, ab, behind, name=tag + "_out_dx")
    behind = advance((dab,))
    dw_in = _mm(x_norm, dab, ta=True, out_dtype=BF16, tm=512, tk=4096, shards=4, after=behind, name=tag + "_in_dw")
    behind = emit(tag + "_w_in", dw_in)
    out = _mm_rms_bwd([(dab, w_in)], x, g_pre, dres, after=behind, post=post, name=tag + "_in_dx")
    return (*out, advance((out[0],)))


def _local_step(x, mem, target, small, gather, emit, advance):
    behind = gather("start", "ffn1i", ())
    behind = gather("start", "ffn1o", behind)
    h1 = _norm_fwd(x, small["ffn1_pre_g"], BF16, name="ffn1_pre", after=behind)
    gather("pass", "ffn1i", (h1,))
    big = gather("finish", "ffn1i", ())
    behind = gather("start", "mixa", (big["ffn1_w_in"],))
    behind = gather("start", "mixb", behind)
    behind = gather("start", "ffn2", behind)
    ab1, u1 = _ffn_in_swiglu(h1, big["ffn1_w_in"], name="ffn1_in", after=behind)
    gather("pass", "ffn1o", (ab1,))
    big.update(gather("finish", "ffn1o", ()))
    behind = gather("pass", "mixa", (u1,))
    f1, x1, h = _mm_resid_norm(u1, big["ffn1_w_out"], x, small["ffn1_post_g"], 0.5, small["mix_pre_g"], name="ffn1_out", after=behind)
    big.update(gather("finish", "mixa", (h,)))
    small = dict(small, w_fu_pad=big["w_fu_pad"])
    behind = gather("pass", "mixb", (h,))
    pg, ppx, pgt = _mix_in_fwd(h, [big["w_gla_t"], big["w_px_t"], big["w_gates_t"]], behind, name="mix_in")
    big.update(gather("finish", "mixb", (pgt,)))
    mem_n = _norm_fwd(mem, small["mem_norm_g"], BF16, name="mem_norm")
    kv = _mm(mem_n, big["w_mem_kv"], out_dtype=BF16, name="mem_kv")
    ya_in, sp, so, o_gla = _gla_fwd(pg, small["w_fu_pad"], small["b_f"], small["gla_norm_g"], name="gla_fwd")
    yb_in = _pool_fwd(ppx, small["w_pool_b"], small["pool_scale"], name="pool_fwd")
    xc = _xattn_fwd(ppx, kv, name="xattn_fwd")
    behind = gather("pass", "ffn2", (xc,))
    w_ups = [big["w_up_gla"], big["w_up_pool"], big["w_up_xattn"]]
    ya, yb, yc, merged, ymix, x2, h2 = _mix_tail_fwd(ya_in, yb_in, xc, pgt, w_ups, big["w_o"], x1, small["mix_post_g"],
                                                     small["ffn2_pre_g"], behind, name="mix_tail")
    big.update(gather("finish", "ffn2", (h2,)))
    ab2, u2 = _ffn_in_swiglu(h2, big["ffn2_w_in"], name="ffn2_in")
    gs = {}
    dz2, dx3, gs["ffn2_post_g"], gs["final_g"], loss = _ffn_out_loss(u2, big["ffn2_w_out"], x2, small["ffn2_post_g"], 0.5,
                                                                    small["final_g"], target, name="ffn2_out_loss")
    dx2, gs["ffn2_pre_g"], dy, gs["mix_post_g"], behind = _ffn_bwd(
        dz2, h2, ab2, u2, big["ffn2_w_in"], big["ffn2_w_out"], x2, small["ffn2_pre_g"], dx3, "ffn2", emit, advance,
        post=(ymix, small["mix_post_g"], 1.0))
    emit("w_o", _mm(merged, dy, ta=True, out_dtype=BF16, tm=512, tk=4096, after=behind, name="mix_out_dw"))
    dya, dyb, dyc, dgt, dya_in, dyb_in, dxc = _mix_tail_bwd(dy, pgt, (ya, yb, yc), w_ups, big["w_o"], (), name="mix_tail_bwd")
    emit("w_up_gla", _mm(ya_in, dya, ta=True, out_dtype=BF16, tm=512, tk=4096, name="up_gla_dw"))
    emit("w_up_pool", _mm(yb_in, dyb, ta=True, out_dtype=BF16, tm=512, tk=4096, shards=4, name="up_pool_dw"))
    emit("w_up_xattn", _mm(xc, dyc, ta=True, out_dtype=BF16, tm=512, tk=4096, shards=4, name="up_xattn_dw"))
    dpg, gs["w_fu_pad"], gs["b_f"], gs["gla_norm_g"] = _gla_bwd(pg, sp, so, o_gla, dya_in, small["w_fu_pad"], small["b_f"], small["gla_norm_g"], name="gla_bwd")
    dp, gs["w_pool"], gs["pool_scale"] = _pool_bwd(dyb_in, ppx, small["w_pool_b"], small["pool_scale"], name="pool_bwd")
    dxq, dkv = _xattn_bwd(dxc, ppx, kv, name="xattn_bwd")
    dkv = dkv.astype(BF16)
    emit("w_mem_kv", _mm(mem_n, dkv, ta=True, out_dtype=BF16, name="mem_kv_dw"))
    dmem_n = _mm(dkv, big["w_mem_kv"], tb=True, name="mem_kv_dx")
    _, gs["mem_norm_g"] = _rms_bwd(mem, small["mem_norm_g"], [dmem_n], None, 1.0, BF16, name="mem_norm_bwd")
    emit("w_gla", _mm(dpg, h, ta=True, out_dtype=BF16, tm=640, tk=4096, name="mix_in_gla_dw"))
    emit("w_p", _mm(dp, h, ta=True, out_dtype=BF16, tm=512, tk=4096, name="mix_in_p_dw"))
    emit("w_xq", _mm(dxq, h, ta=True, out_dtype=BF16, tm=512, tk=4096, name="mix_in_xq_dw"))
    behind = emit("w_gates", _mm(dgt, h, ta=True, out_dtype=BF16, tm=512, tk=4096, name="mix_in_gates_dw"))
    pairs = [(dpg, big["w_gla_t"]), (dp, big["w_p_t"]), (dxq, big["w_xq_t"]), (dgt, big["w_gates_t"])]
    dx1, gs["mix_pre_g"], dz1, gs["ffn1_post_g"] = _mm_rms_bwd(pairs, x1, small["mix_pre_g"], dx2, after=behind,
                                                               post=(f1, small["ffn1_post_g"], 0.5), tm=256, name="mix_in_dx")
    behind = advance((dx1,))
    dx0, gs["ffn1_pre_g"], _ = _ffn_bwd(dz1, h1, ab1, u1, big["ffn1_w_in"], big["ffn1_w_out"], x, small["ffn1_pre_g"], dx1,
                                        "ffn1", emit, advance, after=behind)
    return loss, dx0, gs


BIG = ("ffn1_w_in", "ffn1_w_out", "w_in", "w_mem_kv", "w_up_gla", "w_up_pool", "w_up_xattn", "w_o", "ffn2_w_in", "ffn2_w_out")
COL_SHARDED = ("ffn1_w_in", "w_in", "w_up_pool", "w_up_xattn", "ffn2_w_in")
GATHER_GROUPS = {"ffn1i": ("ffn1_w_in",), "ffn1o": ("ffn1_w_out",), "mixa": ("w_in", "w_fu"),
                 "mixb": ("w_mem_kv", "w_up_gla", "w_up_pool", "w_up_xattn", "w_o"), "ffn2": ("ffn2_w_in", "ffn2_w_out")}
REDUCE_GROUPS = {"ffn2": ("ffn2_w_out", "ffn2_w_in"),
                 "mix": ("w_o", "w_up_gla", "w_up_pool", "w_up_xattn", "w_mem_kv", "w_gla", "w_p", "w_xq", "w_gates"),
                 "ffn1_out": ("ffn1_w_out",),
                 "ffn1_in": ("ffn1_w_in",)}
REDUCE_FIRST, REDUCE_LAST = "ffn2", "ffn1_in"
GAINS = ("ffn1_pre_g", "ffn1_post_g", "mix_pre_g", "gla_norm_g", "mem_norm_g", "mix_post_g", "ffn2_pre_g", "ffn2_post_g", "final_g")
WEIGHTS = ("ffn1_pre_g", "ffn1_w_in", "ffn1_w_out", "ffn1_post_g", "mix_pre_g", "w_in", "w_fu", "b_f", "gla_norm_g", "w_pool",
           "pool_scale", "mem_norm_g", "w_mem_kv", "w_up_gla", "w_up_pool", "w_up_xattn", "w_o", "mix_post_g", "ffn2_pre_g",
           "ffn2_w_in", "ffn2_w_out", "ffn2_post_g", "final_g")
IN_GLA, IN_F, IN_PX, IN_GATES, IN_END = 0, 3072, 3088, 4112, 7184
def _cols_from_shards(g):
    return jnp.transpose(g, (1, 0, 2)).reshape(g.shape[1], 4 * g.shape[2])


def _laid_end_to_end(pieces, rows):
    out, start = None, 0
    for p in pieces:
        padded = jnp.pad(p, ((start, rows - start - p.shape[0]), (0, 0)))
        out = padded if out is None else out + padded
        start += p.shape[0]
    return out


def _rows_of_blocks(g, lo, hi, rows=None):
    q = g.shape[1]
    cuts = [(j, max(lo, j * q) - j * q, min(hi, (j + 1) * q) - j * q) for j in range(g.shape[0])]
    return _laid_end_to_end([g[j, a:b] for j, a, b in cuts if a < b], rows or hi - lo)


def _blocks_of_rows(parts, blocks):
    q = sum(p.shape[0] for p in parts) // blocks
    out = []
    for j in range(blocks):
        pieces, start = [], 0
        for p in parts:
            a, b = max(j * q, start), min((j + 1) * q, start + p.shape[0])
            if a < b:
                pieces.append(p[a - start:b - start])
            start += p.shape[0]
        out.append(_laid_end_to_end(pieces, q))
    return jnp.stack(out)


def kernel(x, mem, ffn1_pre_g, ffn1_w_in, ffn1_w_out, ffn1_post_g, mix_pre_g, w_in, w_fu, b_f, gla_norm_g, w_pool, pool_scale, mem_norm_g, w_mem_kv, w_up_gla, w_up_pool, w_up_xattn, w_o, mix_post_g, ffn2_pre_g, ffn2_w_in, ffn2_w_out, ffn2_post_g, final_g, loss_target, m_ffn1_pre_g, m_ffn1_w_in, m_ffn1_w_out, m_ffn1_post_g, m_mix_pre_g, m_w_in, m_w_fu, m_b_f, m_gla_norm_g, m_w_pool, m_pool_scale, m_mem_norm_g, m_w_mem_kv, m_w_up_gla, m_w_up_pool, m_w_up_xattn, m_w_o, m_mix_post_g, m_ffn2_pre_g, m_ffn2_w_in, m_ffn2_w_out, m_ffn2_post_g, m_final_g, v_ffn1_pre_g, v_ffn1_w_in, v_ffn1_w_out, v_ffn1_post_g, v_mix_pre_g, v_w_in, v_w_fu, v_b_f, v_gla_norm_g, v_w_pool, v_pool_scale, v_mem_norm_g, v_w_mem_kv, v_w_up_gla, v_w_up_pool, v_w_up_xattn, v_w_o, v_mix_post_g, v_ffn2_pre_g, v_ffn2_w_in, v_ffn2_w_out, v_ffn2_post_g, v_final_g):
    args = dict(locals())
    w = {n: args[n][0] for n in WEIGHTS}
    m = {n: args["m_" + n][0] for n in WEIGHTS}
    v = {n: args["v_" + n][0] for n in WEIGHTS}
    xi, yi, ci = lax.axis_index("x"), lax.axis_index("y"), lax.axis_index("c")
    chip = 2 * xi + yi

    c_arr = jnp.reshape(ci, (1,)).astype(jnp.int32)
    chip_arr = jnp.reshape(chip, (1,)).astype(jnp.int32)
    place_arr = jnp.stack([chip, ci]).astype(jnp.int32)
    w_in_t = []
    shard_of = {n: args[n] for n in BIG if n != "w_in"}
    shard_of["w_fu"] = args["w_fu"]
    placed, inflight = {}, {}

    def place(names, after):
        for n in names:
            if n not in placed:
                placed[n] = _place_shard(shard_of[n], chip_arr, F32 if n == "w_fu" else BF16, name="place_" + n, after=after)

    def relayout(names, gathered):
        out = {}
        for n, g in zip(names, gathered):
            if n == "w_fu":
                w_fu_full = _cols_from_shards(g)
                out["w_fu_pad"] = jnp.concatenate([w_fu_full, jnp.zeros((LANE - GATE_RANK, 512), F32)], axis=0).astype(BF16)
            elif n == "w_in":
                out["w_gla_t"] = _rows_of_blocks(g, IN_GLA, IN_PX, rows=PG_W)
                out["w_px_t"] = _rows_of_blocks(g, IN_PX, IN_GATES)
                out["w_p_t"] = _rows_of_blocks(g, IN_PX, IN_PX + 512)
                out["w_xq_t"] = _rows_of_blocks(g, IN_PX + 512, IN_GATES)
                out["w_gates_t"] = _rows_of_blocks(g, IN_GATES, IN_END)
            else:
                out[n] = _cols_from_shards(g) if n in COL_SHARDED else g.reshape(4 * g.shape[1], g.shape[2])
        return out

    def gather(op, group, after):
        names = GATHER_GROUPS[group]
        if op == "start":
            place(names, ())
            inflight[group] = _gather_start([placed[n] for n in names], after, name="gather_" + group + "_start")
            behind = (inflight[group][3],)
            if group == "ffn1o":
                tied = lax.optimization_barrier((behind, tuple(args[k] for k in ("w_in", "m_w_in", "v_w_in"))))[1]
                w_in_t.extend(jnp.transpose(a[0]) for a in tied)
                shard_of["w_in"] = w_in_t[0][None]
                place(shard_of, behind)
            return behind
        if op == "pass":
            if group == "ffn1i":
                not_started = [n for g in GATHER_GROUPS if g not in inflight for n in GATHER_GROUPS[g]]
                after = tuple(after) + tuple(w_in_t[1:]) + tuple(placed[n] for n in not_started)
            send, recv, bufs, _ = inflight[group]
            inflight[group] = _gather_pass(bufs, send, recv, after, name="gather_" + group + "_pass")
            return (inflight[group][2][0],)
        send, recv, bufs = inflight.pop(group)
        return relayout(names, _gather_finish(bufs, send, recv, after, name="gather_" + group + "_finish"))

    small = {n: w[n].reshape(1, D) for n in GAINS}
    small["b_f"] = w["b_f"].reshape(1, 512)
    small["pool_scale"] = w["pool_scale"].reshape(1, 512)
    small["w_pool_b"] = w["w_pool"].astype(BF16)

    pending, crossing, travelling = {}, {}, {}

    def emit(name, grad):
        pending[name] = grad
        group = next((g for g, names in REDUCE_GROUPS.items() if name == names[-1]), None)
        if group is None:
            return ()
        gb = {n: pending.pop(n) for n in REDUCE_GROUPS[group]}
        if group == "mix":
            gb["w_in"] = _blocks_of_rows([gb.pop("w_gla")[0:IN_PX], gb.pop("w_p"), gb.pop("w_xq"), gb.pop("w_gates")], 4)
        names = list(gb)
        contrib = [gb[n] if n in COL_SHARDED else gb[n].reshape(4, gb[n].shape[0] // 4, gb[n].shape[1]) for n in names]
        send, recv, contrib, lands, token = _pair_exchange_start(contrib, (), name="grads_" + group + "_pair_start")
        if group == REDUCE_LAST:
            behind = finish_chips(REDUCE_FIRST, (token,))
            contrib, from_sibling = _pair_exchange_finish(contrib, lands, send, recv, behind, name="grads_" + group + "_pair_finish")
            return over_chips(group, names, contrib, from_sibling)
        crossing[group] = (names, contrib, lands, send, recv)
        return (token,)

    def over_chips(group, names, contrib, from_sibling):
        pair = [_pair_sum(g, got, c_arr, name="grads_pair_sum_" + n) for n, g, got in zip(names, contrib, from_sibling)]
        send, recv, pair, lands, token = _chip_exchange_start(pair, (), name="grads_" + group + "_chip_start")
        travelling[group] = (names, send, recv, pair, lands)
        return (token,)

    def advance(after):
        behind = ()
        for group in list(crossing):
            names, contrib, lands, send, recv = crossing.pop(group)
            contrib, from_sibling = _pair_exchange_finish(contrib, lands, send, recv, after, name="grads_" + group + "_pair_finish")
            behind = over_chips(group, names, contrib, from_sibling)
        return behind

    halves = {}

    def finish_chips(group, after):
        names, send, recv, pair, lands = travelling.pop(group)
        pair, from_chips = _chip_exchange_finish(pair, lands, send, recv, after, name="grads_" + group + "_chip_finish")
        for n, p, got in zip(names, pair, from_chips):
            halves[n] = _chip_sum(p, got, place_arr, name="grads_chip_sum_" + n)
        return (halves[names[-1]],)

    loss, grad_x, gs = _local_step(x[0], mem[0], loss_target[0], small, gather, emit, advance)

    for group in list(travelling):
        finish_chips(group, (grad_x,))
    send, recv, joining, token = _pair_join_start([halves[n] for n in BIG], name="grads_pair_join_start")
    small_sums = _all_sum_small(gs, loss, name="sum_small_grads", after=(token,))
    loss = small_sums[1][LOSS_ROW, 0]
    reduced = dict(zip(BIG, _pair_join_finish(joining, send, recv, (small_sums[0],), name="grads_pair_join_finish")))

    grads, delta, new_m, new_v = {}, {}, {}, {}
    for n in BIG:
        if n == "w_in":
            updated = _adamw(w_in_t[0], reduced[n], w_in_t[1], w_in_t[2], name="adamw_" + n)
            grads[n] = jnp.transpose(reduced[n])[None]
            delta[n], new_m[n], new_v[n] = (jnp.transpose(a)[None] for a in updated)
            continue
        grads[n] = reduced[n][None]
        delta[n], new_m[n], new_v[n] = _adamw(args[n], reduced[n], args["m_" + n], args["v_" + n], name="adamw_" + n)
    small_params = {n: (args[n], args["m_" + n], args["v_" + n]) for n in SMALL}
    for n, (g, d, mn, vn) in _adamw_small(small_sums, small_params, chip_arr, name="adamw_small").items():
        grads[n], delta[n], new_m[n], new_v[n] = g, d, mn, vn

    outs = [loss, grad_x[None]]
    for group in (grads, delta, new_m, new_v):
        outs += [group[n] for n in WEIGHTS]
    return tuple(outs)
```

```python
import jax
import jax.numpy as jnp
from jax import lax
from jax.experimental import pallas as pl
from jax.experimental.pallas import tpu as pltpu

F32 = jnp.float32
BF16 = jnp.bfloat16
MESH = pl.DeviceIdType.MESH

D = 1024
DFF = 2816
CHUNK = 64
HEADS = 4
HDK = 128
HDV = 256
GATE_TEMP = 16.0
POOL_WINDOWS = (2, 4, 8, 16)
POOL_HALO = 16
XA_HEADS = 4
XA_HD = 128
EPS = 1e-6
Q_SCALE = HDK ** -0.5
XA_SCALE = XA_HD ** -0.5
PG_Q, PG_K, PG_V, PG_G, PG_F, PG_W = 0, 512, 1024, 2048, 3072, 3200
GATE_RANK = 16
ADAM_LR, ADAM_B1, ADAM_B2, ADAM_EPS, ADAM_WD, ADAM_STEP = 0.001, 0.9, 0.999, 1e-08, 0.01, 10

VMEM_LIMIT = 48 * 1024 * 1024
LANE = 128
TS_ROW = 512
TS_GLA = 512
TS_POOL = 512
TS_XA = 512


def _params(sem):
    return pltpu.CompilerParams(dimension_semantics=sem, vmem_limit_bytes=VMEM_LIMIT)


def _tile(n, cap, unit=LANE):
    if n <= cap:
        return n
    best = None
    for t in range(unit, cap + 1, unit):
        if n % t == 0:
            best = t
    assert best is not None, (n, cap)
    return best


def _sigmoid(x):
    return 0.5 * jnp.tanh(0.5 * x) + 0.5


def _log_sigmoid(x):
    return jnp.minimum(x, 0.0) - jnp.log(1.0 + jnp.exp(-jnp.abs(x)))


def _rms(x):
    r = lax.rsqrt(jnp.mean(x * x, axis=-1, keepdims=True) + EPS)
    return x * r, r


def _rows(ts, w):
    return pl.BlockSpec((ts, w), lambda i: (i, 0))


def _fixed(shape):
    nd = len(shape)
    return pl.BlockSpec(shape, lambda i: (0,) * nd)


def _mm(a, b, *, ta=False, tb=False, out_dtype=F32, tm=2048, tn=1024, tk=1024, shards=1, after=(), name):
    b_blocked = b.ndim == 3
    assert not (b_blocked and tb)
    m, kdim = (a.shape[1], a.shape[0]) if ta else a.shape
    if b_blocked:
        n, tn = b.shape[0] * b.shape[2], b.shape[2]
        assert b.shape[1] == kdim and shards in (1, b.shape[0])
    else:
        n = b.shape[0] if tb else b.shape[1]
        assert (b.shape[1] if tb else b.shape[0]) == kdim, (a.shape, b.shape, ta, tb)
        tn = n // shards if shards > 1 else _tile(n, tn)
    tm = _tile(m, tm)
    tk = _tile(kdim, tk)
    nk = kdim // tk
    dims = (((0 if ta else 1,), (1 if tb else 0,)), ((), ()))

    def body(a_ref, b_ref, *rest):
        o_ref, *acc = rest[len(after):]
        part = lax.dot_general(a_ref[...], b_ref[...], dims, preferred_element_type=F32)
        if nk == 1:
            o_ref[...] = part.astype(o_ref.dtype)
            return
        acc_ref, = acc
        k = pl.program_id(2)

        @pl.when(k == 0)
        def _():
            acc_ref[...] = part

        @pl.when(k > 0)
        def _():
            acc_ref[...] += part

        @pl.when(k == nk - 1)
        def _():
            o_ref[...] = acc_ref[...].astype(o_ref.dtype)

    a_spec = pl.BlockSpec((tk, tm), lambda i, j, k: (k, i)) if ta else pl.BlockSpec((tm, tk), lambda i, j, k: (i, k))
    if b_blocked:
        b_spec = pl.BlockSpec((None, tk, tn), lambda i, j, k: (j, k, 0))
    else:
        b_spec = pl.BlockSpec((tn, tk), lambda i, j, k: (j, k)) if tb else pl.BlockSpec((tk, tn), lambda i, j, k: (k, j))
    if shards > 1:
        out_shape = jax.ShapeDtypeStruct((shards, m, tn), out_dtype)
        o_spec = pl.BlockSpec((None, tm, tn), lambda i, j, k: (j, i, 0))
    else:
        out_shape = jax.ShapeDtypeStruct((m, n), out_dtype)
        o_spec = pl.BlockSpec((tm, tn), lambda i, j, k: (i, j))
    return pl.pallas_call(
        body, grid=(m // tm, n // tn, nk), in_specs=[a_spec, b_spec] + [ANY] * len(after), out_specs=o_spec, out_shape=out_shape,
        scratch_shapes=[pltpu.VMEM((tm, tn), F32)] if nk > 1 else [],
        compiler_params=_params(("parallel", "parallel", "arbitrary")), name=name,
    )(a, b, *after)


def _norm_fwd(x, g, out_dtype, name, after=()):
    s, d = x.shape
    ts = _tile(s, TS_ROW, 8)

    def body(x_ref, g_ref, *rest):
        o_ref = rest[len(after)]
        xh, _ = _rms(x_ref[...])
        o_ref[...] = (xh * g_ref[...]).astype(o_ref.dtype)

    return pl.pallas_call(
        body, grid=(s // ts,), in_specs=[_rows(ts, d), _fixed((1, d))] + [ANY] * len(after), out_specs=_rows(ts, d),
        out_shape=jax.ShapeDtypeStruct((s, d), out_dtype), compiler_params=_params(("parallel",)), name=name,
    )(x, g, *after)


def _mm_resid_norm(a, w, x, g_post, alpha, g_next, name, after=(), tm=512):
    s, kdim = a.shape
    d = w.shape[1]
    tm = _tile(s, tm)
    with_h = g_next is not None
    na = len(after)

    def body(a_ref, w_ref, x_ref, gp_ref, *rest):
        rest = rest[int(with_h) + na:] if not with_h else rest[:1] + rest[1 + na:]
        for rows in _sub_blocks(tm):
            f = jnp.dot(a_ref[rows, :], w_ref[...], preferred_element_type=F32)
            fh, _ = _rms(f)
            xn = x_ref[rows, :] + alpha * (fh * gp_ref[...])
            if with_h:
                gn_ref, f_ref, xo_ref, h_ref = rest
                xh, _ = _rms(xn)
                h_ref[rows, :] = (xh * gn_ref[...]).astype(h_ref.dtype)
            else:
                f_ref, xo_ref = rest
            f_ref[rows, :] = f
            xo_ref[rows, :] = xn

    ins = [a, w, x, g_post] + ([g_next] if with_h else []) + list(after)
    in_specs = [_rows(tm, kdim), _fixed((kdim, d)), _rows(tm, d), _fixed((1, d))] + ([_fixed((1, d))] if with_h else []) + [ANY] * na
    out_shape = [jax.ShapeDtypeStruct((s, d), F32)] * 2 + ([jax.ShapeDtypeStruct((s, d), BF16)] if with_h else [])
    out = pl.pallas_call(
        body, grid=(s // tm,), in_specs=in_specs, out_specs=[_rows(tm, d)] * len(out_shape), out_shape=out_shape,
        compiler_params=_params(("parallel",)), name=name,
    )(*ins)
    return (out[0], out[1], out[2]) if with_h else (out[0], out[1], None)


def _mm_rms_bwd(pairs, x, g, dres, name, after=(), post=None, tm=512):
    s, d = x.shape
    tm = _tile(s, tm)
    n, na = len(pairs), len(after)

    def body(*refs):
        a_refs, w_refs = refs[0:2 * n:2], refs[1:2 * n:2]
        x_ref, g_ref, dres_ref = refs[2 * n:2 * n + 3]
        if post is not None:
            f_ref, gp_ref = refs[2 * n + 3:2 * n + 5]
            dx_ref, dg_ref, df_ref, dgp_ref = refs[2 * n + 5 + na:]
        else:
            dx_ref, dg_ref = refs[2 * n + 3 + na:]
        @pl.when(pl.program_id(0) == 0)
        def _():
            dg_ref[...] = jnp.zeros_like(dg_ref)
            if post is not None:
                dgp_ref[...] = jnp.zeros_like(dgp_ref)

        for rows in _sub_blocks(tm):
            dy = None
            for a_ref, w_ref in zip(a_refs, w_refs):
                if len(a_ref.shape) == 3:
                    tkb = a_ref.shape[2]
                    parts = [lax.dot_general(a_ref[q, rows, :], w_ref[:, q * tkb:(q + 1) * tkb], (((1,), (1,)), ((), ())),
                                             preferred_element_type=F32) for q in range(a_ref.shape[0])]
                else:
                    parts = [jnp.dot(a_ref[rows, :], w_ref[...], preferred_element_type=F32)]
                for part in parts:
                    dy = part if dy is None else dy + part
            xh, r = _rms(x_ref[rows, :])
            dg_ref[...] += jnp.sum(dy * xh, axis=0, keepdims=True)
            dyg = dy * g_ref[...]
            dx = r * (dyg - xh * jnp.mean(dyg * xh, axis=-1, keepdims=True)) + dres_ref[rows, :]
            dx_ref[rows, :] = dx
            if post is not None:
                fh, rf = _rms(f_ref[rows, :])
                dz = dx * post[2]
                dgp_ref[...] += jnp.sum(dz * fh, axis=0, keepdims=True)
                dzg = dz * gp_ref[...]
                df_ref[rows, :] = (rf * (dzg - fh * jnp.mean(dzg * fh, axis=-1, keepdims=True))).astype(df_ref.dtype)

    ins, in_specs = [], []
    for a_arr, w_arr in pairs:
        ins += [a_arr, w_arr]
        if a_arr.ndim == 3:
            in_specs.append(pl.BlockSpec((a_arr.shape[0], tm, a_arr.shape[2]), lambda i: (0, i, 0)))
        else:
            in_specs.append(_rows(tm, a_arr.shape[1]))
        in_specs.append(pl.BlockSpec(w_arr.shape, lambda i: (0, 0), pipeline_mode=pl.Buffered(1)))
    with_post = post is not None
    return pl.pallas_call(
        body, grid=(s // tm,),
        in_specs=in_specs + [_rows(tm, d), _fixed((1, d)), _rows(tm, d)] + ([_rows(tm, d), _fixed((1, d))] if with_post else [])
        + [ANY] * na,
        out_specs=[_rows(tm, d), _fixed((1, d))] + ([_rows(tm, d), _fixed((1, d))] if with_post else []),
        out_shape=[jax.ShapeDtypeStruct((s, d), F32), jax.ShapeDtypeStruct((1, d), F32)]
        + ([jax.ShapeDtypeStruct((s, d), BF16), jax.ShapeDtypeStruct((1, d), F32)] if with_post else []),
        compiler_params=_params(("arbitrary",)), name=name,
    )(*ins, x, g, dres, *(post[:2] if with_post else ()), *after)


def _ffn_out_loss(u, w_out, x, g_post, alpha, g_final, target, name, tm=512):
    s, kdim = u.shape
    d = w_out.shape[1]
    tm = _tile(s, tm)

    def body(u_ref, w_ref, x_ref, gp_ref, gf_ref, t_ref, df_ref, dx_ref, dgp_ref, dgf_ref, loss_ref):
        @pl.when(pl.program_id(0) == 0)
        def _():
            dgp_ref[...] = jnp.zeros_like(dgp_ref)
            dgf_ref[...] = jnp.zeros_like(dgf_ref)
            loss_ref[...] = jnp.zeros_like(loss_ref)

        for rows in _sub_blocks(tm):
            f = jnp.dot(u_ref[rows, :], w_ref[...], preferred_element_type=F32)
            fh, rf = _rms(f)
            xn = x_ref[rows, :] + alpha * (fh * gp_ref[...])
            xh, rx = _rms(xn)
            gf = gf_ref[...]
            diff = xh * gf - t_ref[rows, :]
            sq = jnp.sum(diff * diff, axis=1, keepdims=True)
            loss_ref[...] += (0.5 / d) * jnp.sum(sq, axis=0, keepdims=True)
            dy = diff * (1.0 / d)
            dgf_ref[...] += jnp.sum(dy * xh, axis=0, keepdims=True)
            dyg = dy * gf
            dxn = rx * (dyg - xh * jnp.mean(dyg * xh, axis=-1, keepdims=True))
            dx_ref[rows, :] = dxn
            dz = dxn * alpha
            dgp_ref[...] += jnp.sum(dz * fh, axis=0, keepdims=True)
            dzg = dz * gp_ref[...]
            df_ref[rows, :] = (rf * (dzg - fh * jnp.mean(dzg * fh, axis=-1, keepdims=True))).astype(df_ref.dtype)

    return pl.pallas_call(
        body, grid=(s // tm,),
        in_specs=[_rows(tm, kdim), _resident(w_out.shape), _rows(tm, d), _fixed((1, d)), _fixed((1, d)), _rows(tm, d)],
        out_specs=[_rows(tm, d), _rows(tm, d), _fixed((1, d)), _fixed((1, d)), _fixed((8, LANE))],
        out_shape=[jax.ShapeDtypeStruct((s, d), BF16), jax.ShapeDtypeStruct((s, d), F32), jax.ShapeDtypeStruct((1, d), F32),
                   jax.ShapeDtypeStruct((1, d), F32), jax.ShapeDtypeStruct((8, LANE), F32)],
        compiler_params=_params(("arbitrary",)), name=name,
    )(u, w_out, x, g_post, g_final, target)


def _rms_bwd(x, g, dys, dres, alpha, out_dtype, name, after=()):
    s, d = x.shape
    ts = _tile(s, TS_ROW, 8)
    ndy = len(dys)
    with_res = dres is not None

    def body(x_ref, g_ref, *rest):
        dy_refs = rest[:ndy]
        rest = rest[ndy:]
        if with_res:
            dres_ref = rest[0]
        dx_ref, dg_ref = rest[int(with_res) + len(after):]
        xh, r = _rms(x_ref[...])
        dy = dy_refs[0][...].astype(F32)
        for ref in dy_refs[1:]:
            dy = dy + ref[...].astype(F32)
        dy = dy * alpha

        @pl.when(pl.program_id(0) == 0)
        def _():
            dg_ref[...] = jnp.zeros_like(dg_ref)

        dg_ref[...] += jnp.sum(dy * xh, axis=0, keepdims=True)
        dyg = dy * g_ref[...]
        dx = r * (dyg - xh * jnp.mean(dyg * xh, axis=-1, keepdims=True))
        if with_res:
            dx = dx + dres_ref[...]
        dx_ref[...] = dx.astype(dx_ref.dtype)

    ins = [x, g] + list(dys) + ([dres] if with_res else []) + list(after)
    in_specs = [_rows(ts, d), _fixed((1, d))] + [_rows(ts, d)] * (ndy + int(with_res)) + [ANY] * len(after)
    return pl.pallas_call(
        body, grid=(s // ts,), in_specs=in_specs, out_specs=[_rows(ts, d), _fixed((1, d))],
        out_shape=[jax.ShapeDtypeStruct((s, d), out_dtype), jax.ShapeDtypeStruct((1, d), F32)],
        compiler_params=_params(("arbitrary",)), name=name,
    )(*ins)


HALF_FF = DFF // 2


SUB_ROWS = 256


def _sub_blocks(tm):
    sub = SUB_ROWS if tm % SUB_ROWS == 0 else tm
    return [slice(r0, r0 + sub) for r0 in range(0, tm, sub)]


def _ffn_in_swiglu(x_norm, w_in, name, after=(), tm=1024):
    s, d = x_norm.shape
    tm = _tile(s, tm)

    def body(x_ref, wa_ref, wb_ref, *rest):
        ab_ref, u_ref = rest[len(after):]
        for rows in _sub_blocks(tm):
            xv = x_ref[rows, :]
            a = jnp.dot(xv, wa_ref[...], preferred_element_type=F32)
            b = jnp.dot(xv, wb_ref[...], preferred_element_type=F32)
            ab_ref[0, rows, :] = a.astype(ab_ref.dtype)
            ab_ref[1, rows, :] = b.astype(ab_ref.dtype)
            u_ref[rows, :] = (a * _sigmoid(a) * b).astype(u_ref.dtype)

    ab, u = pl.pallas_call(
        body, grid=(2, s // tm),
        in_specs=[pl.BlockSpec((tm, d), lambda j, i: (i, 0)), pl.BlockSpec((d, HALF_FF), lambda j, i: (0, j)),
                  pl.BlockSpec((d, HALF_FF), lambda j, i: (0, 2 + j))] + [ANY] * len(after),
        out_specs=[pl.BlockSpec((2, None, tm, HALF_FF), lambda j, i: (0, j, i, 0)), pl.BlockSpec((tm, HALF_FF), lambda j, i: (i, j))],
        out_shape=[jax.ShapeDtypeStruct((2, 2, s, HALF_FF), BF16), jax.ShapeDtypeStruct((s, DFF), BF16)],
        compiler_params=_params(("parallel", "parallel")), name=name,
    )(x_norm, w_in, w_in, *after)
    return ab.reshape(4, s, HALF_FF), u


def _ffn_out_dx_swiglu(dz, w_out, ab, after, name, tm=1024):
    s, d = dz.shape
    tm = _tile(s, tm)

    def body(dz_ref, w_ref, ab_ref, *rest):
        dab_ref = rest[len(after)]
        for rows in _sub_blocks(tm):
            du = lax.dot_general(dz_ref[rows, :], w_ref[...], (((1,), (1,)), ((), ())), preferred_element_type=F32)
            a = ab_ref[0, rows, :].astype(F32)
            b = ab_ref[1, rows, :].astype(F32)
            sig = _sigmoid(a)
            dab_ref[0, rows, :] = (du * b * (sig * (1.0 + a * (1.0 - sig)))).astype(dab_ref.dtype)
            dab_ref[1, rows, :] = (du * a * sig).astype(dab_ref.dtype)

    halves = pl.BlockSpec((2, None, tm, HALF_FF), lambda j, i: (0, j, i, 0))
    dab = pl.pallas_call(
        body, grid=(2, s // tm),
        in_specs=[pl.BlockSpec((tm, d), lambda j, i: (i, 0)), pl.BlockSpec((HALF_FF, d), lambda j, i: (j, 0)), halves] + [ANY] * len(after),
        out_specs=halves, out_shape=jax.ShapeDtypeStruct((2, 2, s, HALF_FF), BF16),
        compiler_params=_params(("parallel", "parallel")), name=name,
    )(dz, w_out, ab.reshape(2, 2, s, HALF_FF), *after)
    return dab.reshape(4, s, HALF_FF)


def _tri(strict):
    r = lax.broadcasted_iota(jnp.int32, (CHUNK, CHUNK), 0)
    c = lax.broadcasted_iota(jnp.int32, (CHUNK, CHUNK), 1)
    return (r > c).astype(F32) if strict else (r >= c).astype(F32)


def _tri_dot(tri, x):
    t = tri.astype(BF16)
    hi = x.astype(BF16)
    rest = x - hi.astype(F32)
    mid = rest.astype(BF16)
    lo = (rest - mid.astype(F32)).astype(BF16)
    return (jnp.dot(t, hi, preferred_element_type=F32) + jnp.dot(t, mid, preferred_element_type=F32)
            + jnp.dot(t, lo, preferred_element_type=F32))


def _gla_fwd(pg, wfu, b_f, gnorm, name):
    s = pg.shape[0]
    ts = _tile(s, TS_GLA, CHUNK)
    cpb = ts // CHUNK
    nc = s // CHUNK

    def body(pg_ref, wfu_ref, bf_ref, gn_ref, ya_ref, sp_ref, so_ref, o_ref, st_ref, la_ref, dec_ref, u_ref):
        @pl.when(pl.program_id(0) == 0)
        def _():
            st_ref[...] = jnp.zeros_like(st_ref)

        f = jnp.dot(pg_ref[:, PG_F:PG_W], wfu_ref[...], preferred_element_type=F32) + bf_ref[...]
        la_ref[...] = _log_sigmoid(f) * (1.0 / GATE_TEMP)
        tri = _tri(False)
        chunks = [slice(ci * CHUNK, (ci + 1) * CHUNK) for ci in range(cpb)]
        for ci, rows in enumerate(chunks):
            la = la_ref[rows, :]
            b = _tri_dot(tri, la)
            bend = jnp.sum(la, axis=0, keepdims=True)
            e = jnp.exp(bend - b)
            dec_ref[ci:ci + 1, :] = jnp.exp(bend)
            for hd in range(HEADS):
                k = pg_ref[rows, PG_K + hd * HDK:PG_K + (hd + 1) * HDK]
                v = pg_ref[rows, PG_V + hd * HDV:PG_V + (hd + 1) * HDV]
                kt = (k.astype(F32) * e[:, hd * HDK:(hd + 1) * HDK]).astype(BF16)
                u_ref[ci, hd] = lax.dot_general(v, kt, (((0,), (0,)), ((), ())), preferred_element_type=F32)
        for ci in range(cpb):
            for hd in range(HEADS):
                prev = st_ref[hd]
                sp_ref[ci, hd] = prev
                st = prev * dec_ref[ci:ci + 1, hd * HDK:(hd + 1) * HDK] + u_ref[ci, hd]
                st_ref[hd] = st
                so_ref[ci, hd] = st.astype(so_ref.dtype)
        for ci, rows in enumerate(chunks):
            for hd in range(HEADS):
                vc = slice(hd * HDV, (hd + 1) * HDV)
                q = pg_ref[rows, PG_Q + hd * HDK:PG_Q + (hd + 1) * HDK]
                go = pg_ref[rows, PG_G + hd * HDV:PG_G + (hd + 1) * HDV].astype(F32)
                qs = (q.astype(F32) * Q_SCALE).astype(BF16)
                o = lax.dot_general(qs, so_ref[ci, hd], (((1,), (1,)), ((), ())), preferred_element_type=F32)
                o_ref[rows, vc] = o
                oh, _ = _rms(o)
                ya_ref[rows, vc] = (oh * gn_ref[:, vc] * (go * _sigmoid(go))).astype(ya_ref.dtype)

    return pl.pallas_call(
        body, grid=(s // ts,),
        in_specs=[_rows(ts, PG_W), _fixed((LANE, HEADS * HDK)), _fixed((1, HEADS * HDK)), _fixed((1, HEADS * HDV))],
        out_specs=[_rows(ts, HEADS * HDV), pl.BlockSpec((cpb, HEADS, HDV, HDK), lambda i: (i, 0, 0, 0)),
                   pl.BlockSpec((cpb, HEADS, HDV, HDK), lambda i: (i, 0, 0, 0)), _rows(ts, HEADS * HDV)],
        out_shape=[jax.ShapeDtypeStruct((s, HEADS * HDV), BF16), jax.ShapeDtypeStruct((nc, HEADS, HDV, HDK), F32),
                   jax.ShapeDtypeStruct((nc, HEADS, HDV, HDK), BF16), jax.ShapeDtypeStruct((s, HEADS * HDV), F32)],
        scratch_shapes=[pltpu.VMEM((HEADS, HDV, HDK), F32), pltpu.VMEM((ts, HEADS * HDK), F32),
                        pltpu.VMEM((max(cpb, 8), HEADS * HDK), F32), pltpu.VMEM((cpb, HEADS, HDV, HDK), F32)],
        compiler_params=_params(("arbitrary",)), name=name,
    )(pg, wfu, b_f, gnorm)


def _gla_bwd(pg, sp, so, o, dya, wfu, b_f, gnorm, name):
    s = pg.shape[0]
    ts = _tile(s, TS_GLA, CHUNK)
    cpb = ts // CHUNK
    nblk = s // ts

    def body(pg_ref, sp_ref, so_ref, o_ref, dya_ref, wfu_ref, bf_ref, gn_ref, dpg_ref, dwfu_ref, dbf_ref, dgn_ref,
             dst_ref, la_ref, sg_ref, df_ref, e_ref, ktf_ref, dec_ref, g_ref):
        @pl.when(pl.program_id(0) == 0)
        def _():
            dst_ref[...] = jnp.zeros_like(dst_ref)
            dwfu_ref[...] = jnp.zeros_like(dwfu_ref)
            dbf_ref[...] = jnp.zeros_like(dbf_ref)
            dgn_ref[...] = jnp.zeros_like(dgn_ref)

        flow = pg_ref[:, PG_F:PG_W]
        f = jnp.dot(flow, wfu_ref[...], preferred_element_type=F32) + bf_ref[...]
        la_ref[...] = _log_sigmoid(f) * (1.0 / GATE_TEMP)
        sg_ref[...] = _sigmoid(-f) * (1.0 / GATE_TEMP)
        tri = _tri(False)
        tri_strict = _tri(True)
        chunks = [slice(ci * CHUNK, (ci + 1) * CHUNK) for ci in range(cpb)]
        for ci, rows in enumerate(chunks):
            la = la_ref[rows, :]
            b = _tri_dot(tri, la)
            bend = jnp.sum(la, axis=0, keepdims=True)
            e = jnp.exp(bend - b)
            e_ref[rows, :] = e
            dec = jnp.exp(bend)
            dec_ref[ci:ci + 1, :] = dec
            for hd in range(HEADS):
                kc = slice(hd * HDK, (hd + 1) * HDK)
                vc = slice(hd * HDV, (hd + 1) * HDV)
                q = pg_ref[rows, PG_Q + hd * HDK:PG_Q + (hd + 1) * HDK]
                k = pg_ref[rows, PG_K + hd * HDK:PG_K + (hd + 1) * HDK]
                go = pg_ref[rows, PG_G + hd * HDV:PG_G + (hd + 1) * HDV].astype(F32)
                ktf_ref[rows, kc] = k.astype(F32) * e[:, kc]
                st_b = so_ref[ci, hd]
                qs = (q.astype(F32) * Q_SCALE).astype(BF16)
                oh, r = _rms(o_ref[rows, vc])
                gh = gn_ref[:, vc]
                sig = _sigmoid(go)
                dy = dya_ref[rows, vc].astype(F32)
                don = dy * (go * sig)
                dgn_ref[:, vc] += jnp.sum(don * oh, axis=0, keepdims=True)
                dong = don * gh
                do = (r * (dong - oh * jnp.mean(dong * oh, axis=-1, keepdims=True))).astype(BF16)
                g_ref[ci, hd] = lax.dot_general(do, qs, (((0,), (0,)), ((), ())), preferred_element_type=F32)
                dq = jnp.dot(do, st_b, preferred_element_type=F32) * Q_SCALE
                dpg_ref[rows, PG_Q + hd * HDK:PG_Q + (hd + 1) * HDK] = dq.astype(dpg_ref.dtype)
                dgo = dy * (oh * gh) * (sig * (1.0 + go * (1.0 - sig)))
                dpg_ref[rows, PG_G + hd * HDV:PG_G + (hd + 1) * HDV] = dgo.astype(dpg_ref.dtype)
        for ci in reversed(range(cpb)):
            for hd in range(HEADS):
                dst = dst_ref[hd] + g_ref[ci, hd]
                g_ref[ci, hd] = dst
                dst_ref[hd] = dst * dec_ref[ci:ci + 1, hd * HDK:(hd + 1) * HDK]
        for ci, rows in enumerate(chunks):
            for hd in range(HEADS):
                kc = slice(hd * HDK, (hd + 1) * HDK)
                v = pg_ref[rows, PG_V + hd * HDV:PG_V + (hd + 1) * HDV]
                ktf = ktf_ref[rows, kc]
                dst = g_ref[ci, hd]
                dst_b = dst.astype(BF16)
                dkt = jnp.dot(v, dst_b, preferred_element_type=F32)
                dv = lax.dot_general(ktf.astype(BF16), dst_b, (((1,), (1,)), ((), ())), preferred_element_type=F32)
                dd = jnp.sum(dst * sp_ref[ci, hd], axis=0, keepdims=True)
                dla = _tri_dot(tri_strict, dkt * ktf) + dd * dec_ref[ci:ci + 1, kc]
                df_ref[rows, kc] = dla * sg_ref[rows, kc]
                dpg_ref[rows, PG_K + hd * HDK:PG_K + (hd + 1) * HDK] = (dkt * e_ref[rows, kc]).astype(dpg_ref.dtype)
                dpg_ref[rows, PG_V + hd * HDV:PG_V + (hd + 1) * HDV] = dv.astype(dpg_ref.dtype)
        df = df_ref[...]
        df_b = df.astype(BF16)
        dpg_ref[:, PG_F:PG_W] = lax.dot_general(df_b, wfu_ref[...], (((1,), (1,)), ((), ())), preferred_element_type=F32).astype(dpg_ref.dtype)
        dwfu_ref[...] += lax.dot_general(flow, df_b, (((0,), (0,)), ((), ())), preferred_element_type=F32)
        dbf_ref[...] += jnp.sum(df, axis=0, keepdims=True)

    rev = lambda i: (nblk - 1 - i, 0)
    return pl.pallas_call(
        body, grid=(nblk,),
        in_specs=[pl.BlockSpec((ts, PG_W), rev), pl.BlockSpec((cpb, HEADS, HDV, HDK), lambda i: (nblk - 1 - i, 0, 0, 0)),
                  pl.BlockSpec((cpb, HEADS, HDV, HDK), lambda i: (nblk - 1 - i, 0, 0, 0)), pl.BlockSpec((ts, HEADS * HDV), rev),
                  pl.BlockSpec((ts, HEADS * HDV), rev), _fixed((LANE, HEADS * HDK)), _fixed((1, HEADS * HDK)), _fixed((1, HEADS * HDV))],
        out_specs=[pl.BlockSpec((ts, PG_W), rev), _fixed((LANE, HEADS * HDK)), _fixed((1, HEADS * HDK)), _fixed((1, HEADS * HDV))],
        out_shape=[jax.ShapeDtypeStruct((s, PG_W), BF16), jax.ShapeDtypeStruct((LANE, HEADS * HDK), F32),
                   jax.ShapeDtypeStruct((1, HEADS * HDK), F32), jax.ShapeDtypeStruct((1, HEADS * HDV), F32)],
        scratch_shapes=[pltpu.VMEM((HEADS, HDV, HDK), F32)] + [pltpu.VMEM((ts, HEADS * HDK), F32)] * 5
        + [pltpu.VMEM((max(cpb, 8), HEADS * HDK), F32), pltpu.VMEM((cpb, HEADS, HDV, HDK), F32)],
        compiler_params=_params(("arbitrary",)), name=name,
    )(pg, sp, so, o, dya, wfu, b_f, gnorm)


def _window_sums(ext, sign):
    n = ext.shape[0]
    sums = {1: ext}
    w = 1
    while w < POOL_WINDOWS[-1]:
        sums[2 * w] = sums[w] + pltpu.roll(sums[w], w if sign > 0 else n - w, 0)
        w *= 2
    return [sums[POOL_WINDOWS[g]][:, g * LANE:(g + 1) * LANE] for g in range(len(POOL_WINDOWS))]


def _pool_counts(row0, n):
    pos = (row0 + lax.broadcasted_iota(jnp.int32, (n, 1), 0) + 1).astype(F32)
    return [1.0 / jnp.minimum(pos, float(w)) for w in POOL_WINDOWS]


def _pool_fwd(ppx, w_pool, pool_scale, name):
    s = ppx.shape[0]
    ts = _tile(s, TS_POOL, POOL_HALO)
    hb = ts // POOL_HALO
    pw = len(POOL_WINDOWS) * LANE

    def body(p_ref, halo_ref, w_ref, sc_ref, y_ref, ext_ref):
        i = pl.program_id(0)
        p = p_ref[...].astype(F32)
        ext_ref[0:POOL_HALO, :] = jnp.where(i > 0, halo_ref[...].astype(F32), 0.0)
        ext_ref[POOL_HALO:, :] = p
        sums = _window_sums(ext_ref[...], +1)
        cnt = _pool_counts(i * ts, ts)
        for g in range(len(POOL_WINDOWS)):
            cols = slice(g * LANE, (g + 1) * LANE)
            mixed = sums[g][POOL_HALO:, :] * cnt[g] - p[:, cols]
            y = jnp.dot(mixed.astype(BF16), w_ref[g], preferred_element_type=F32)
            y_ref[:, cols] = (y * sc_ref[:, cols]).astype(y_ref.dtype)

    return pl.pallas_call(
        body, grid=(s // ts,),
        in_specs=[pl.BlockSpec((ts, pw), lambda i: (i, 0)), pl.BlockSpec((POOL_HALO, pw), lambda i: (jnp.maximum(i * hb - 1, 0), 0)),
                  _fixed((len(POOL_WINDOWS), LANE, LANE)), _fixed((1, pw))],
        out_specs=_rows(ts, pw), out_shape=jax.ShapeDtypeStruct((s, pw), BF16),
        scratch_shapes=[pltpu.VMEM((ts + POOL_HALO, pw), F32)],
        compiler_params=_params(("parallel",)), name=name,
    )(ppx, ppx, w_pool, pool_scale)


def _pool_bwd(dyb, ppx, w_pool, pool_scale, name):
    s = ppx.shape[0]
    ts = _tile(s, TS_POOL, POOL_HALO)
    hb = ts // POOL_HALO
    nblk = s // ts
    last_halo = s // POOL_HALO - 1
    ng = len(POOL_WINDOWS)
    pw = ng * LANE

    def body(p_ref, halo_ref, dy_ref, dyn_ref, w_ref, sc_ref, dp_ref, dw_ref, dsc_ref, ext_ref, dext_ref, dm_ref):
        i = pl.program_id(0)

        @pl.when(i == 0)
        def _():
            dw_ref[...] = jnp.zeros_like(dw_ref)
            dsc_ref[...] = jnp.zeros_like(dsc_ref)

        p = p_ref[...].astype(F32)
        ext_ref[0:POOL_HALO, :] = jnp.where(i > 0, halo_ref[...].astype(F32), 0.0)
        ext_ref[POOL_HALO:, :] = p
        sums = _window_sums(ext_ref[...], +1)
        cnt = _pool_counts(i * ts, ts + POOL_HALO)
        sc = sc_ref[...]
        dy = dy_ref[...].astype(F32)
        dyn = jnp.where(i < nblk - 1, dyn_ref[...].astype(F32), 0.0)
        for g in range(ng):
            cols = slice(g * LANE, (g + 1) * LANE)
            wg = w_ref[g]
            mixed = (sums[g][POOL_HALO:, :] * cnt[g][0:ts] - p[:, cols]).astype(BF16)
            ypre = jnp.dot(mixed, wg, preferred_element_type=F32)
            dsc_ref[:, cols] += jnp.sum(dy[:, cols] * ypre, axis=0, keepdims=True)
            dyp = (dy[:, cols] * sc[:, cols]).astype(BF16)
            dypn = (dyn[:, cols] * sc[:, cols]).astype(BF16)
            dw_ref[g] += lax.dot_general(mixed, dyp, (((0,), (0,)), ((), ())), preferred_element_type=F32)
            dm = lax.dot_general(dyp, wg, (((1,), (1,)), ((), ())), preferred_element_type=F32)
            dmn = lax.dot_general(dypn, wg, (((1,), (1,)), ((), ())), preferred_element_type=F32)
            dext_ref[0:ts, cols] = dm * cnt[g][0:ts]
            dext_ref[ts:, cols] = dmn * cnt[g][ts:]
            dm_ref[:, cols] = dm
        lead = _window_sums(dext_ref[...], -1)
        for g in range(ng):
            cols = slice(g * LANE, (g + 1) * LANE)
            dp_ref[:, cols] = (lead[g][0:ts, :] - dm_ref[:, cols]).astype(dp_ref.dtype)

    return pl.pallas_call(
        body, grid=(nblk,),
        in_specs=[pl.BlockSpec((ts, pw), lambda i: (i, 0)), pl.BlockSpec((POOL_HALO, pw), lambda i: (jnp.maximum(i * hb - 1, 0), 0)),
                  pl.BlockSpec((ts, pw), lambda i: (i, 0)), pl.BlockSpec((POOL_HALO, pw), lambda i: (jnp.minimum((i + 1) * hb, last_halo), 0)),
                  _fixed((ng, LANE, LANE)), _fixed((1, pw))],
        out_specs=[_rows(ts, pw), _fixed((ng, LANE, LANE)), _fixed((1, pw))],
        out_shape=[jax.ShapeDtypeStruct((s, pw), BF16), jax.ShapeDtypeStruct((ng, LANE, LANE), F32), jax.ShapeDtypeStruct((1, pw), F32)],
        scratch_shapes=[pltpu.VMEM((ts + POOL_HALO, pw), F32), pltpu.VMEM((ts + POOL_HALO, pw), F32), pltpu.VMEM((ts, pw), F32)],
        compiler_params=_params(("arbitrary",)), name=name,
    )(ppx, ppx, dyb, dyb, w_pool, pool_scale)


def _xattn_fwd(ppx, kv, name):
    s = ppx.shape[0]
    m = kv.shape[0]
    ts = _tile(s, TS_XA, 8)
    xw = XA_HEADS * XA_HD

    def body(q_ref, kv_ref, o_ref):
        for hd in range(XA_HEADS):
            cols = slice(hd * XA_HD, (hd + 1) * XA_HD)
            k = kv_ref[:, hd * XA_HD:(hd + 1) * XA_HD]
            v = kv_ref[:, xw + hd * XA_HD:xw + (hd + 1) * XA_HD]
            sc = lax.dot_general(q_ref[:, cols], k, (((1,), (1,)), ((), ())), preferred_element_type=F32) * XA_SCALE
            ex = jnp.exp(sc - jnp.max(sc, axis=-1, keepdims=True))
            pr = ex * (1.0 / jnp.sum(ex, axis=-1, keepdims=True))
            o_ref[:, cols] = jnp.dot(pr.astype(BF16), v, preferred_element_type=F32).astype(o_ref.dtype)

    return pl.pallas_call(
        body, grid=(s // ts,), in_specs=[pl.BlockSpec((ts, xw), lambda i: (i, 1)), _fixed((m, 2 * xw))],
        out_specs=_rows(ts, xw), out_shape=jax.ShapeDtypeStruct((s, xw), BF16),
        compiler_params=_params(("parallel",)), name=name,
    )(ppx, kv)


def _xattn_bwd(dxc, ppx, kv, name):
    s = ppx.shape[0]
    m = kv.shape[0]
    ts = _tile(s, TS_XA, 8)
    xw = XA_HEADS * XA_HD

    def body(do_ref, q_ref, kv_ref, dq_ref, dkv_ref):
        @pl.when(pl.program_id(0) == 0)
        def _():
            dkv_ref[...] = jnp.zeros_like(dkv_ref)

        for hd in range(XA_HEADS):
            cols = slice(hd * XA_HD, (hd + 1) * XA_HD)
            vcols = slice(xw + hd * XA_HD, xw + (hd + 1) * XA_HD)
            q = q_ref[:, cols]
            k = kv_ref[:, cols]
            v = kv_ref[:, vcols]
            do = do_ref[:, cols]
            sc = lax.dot_general(q, k, (((1,), (1,)), ((), ())), preferred_element_type=F32) * XA_SCALE
            ex = jnp.exp(sc - jnp.max(sc, axis=-1, keepdims=True))
            pr = ex * (1.0 / jnp.sum(ex, axis=-1, keepdims=True))
            dpr = lax.dot_general(do, v, (((1,), (1,)), ((), ())), preferred_element_type=F32)
            dsc = (pr * (dpr - jnp.sum(dpr * pr, axis=-1, keepdims=True)) * XA_SCALE).astype(BF16)
            dq_ref[:, cols] = jnp.dot(dsc, k, preferred_element_type=F32).astype(dq_ref.dtype)
            dkv_ref[:, cols] += lax.dot_general(dsc, q, (((0,), (0,)), ((), ())), preferred_element_type=F32)
            dkv_ref[:, vcols] += lax.dot_general(pr.astype(BF16), do, (((0,), (0,)), ((), ())), preferred_element_type=F32)

    return pl.pallas_call(
        body, grid=(s // ts,), in_specs=[_rows(ts, xw), pl.BlockSpec((ts, xw), lambda i: (i, 1)), _fixed((m, 2 * xw))],
        out_specs=[_rows(ts, xw), _fixed((m, 2 * xw))],
        out_shape=[jax.ShapeDtypeStruct((s, xw), BF16), jax.ShapeDtypeStruct((m, 2 * xw), F32)],
        compiler_params=_params(("arbitrary",)), name=name,
    )(dxc, ppx, kv)


def _resident(shape):
    nd = len(shape)
    return pl.BlockSpec(shape, lambda i: (0,) * nd, pipeline_mode=pl.Buffered(1))


def _mix_in_fwd(h, w_ts, after, name, tm=512):
    s, d = h.shape
    tm = _tile(s, tm)
    n, na = len(w_ts), len(after)

    def body(h_ref, *refs):
        w_refs, o_refs = refs[:n], refs[n + na:]
        for rows in _sub_blocks(tm):
            hv = h_ref[rows, :]
            for w_ref, o_ref in zip(w_refs, o_refs):
                o_ref[rows, :] = lax.dot_general(hv, w_ref[...], (((1,), (1,)), ((), ())), preferred_element_type=F32).astype(o_ref.dtype)

    return pl.pallas_call(
        body, grid=(s // tm,), in_specs=[_rows(tm, d)] + [_resident(w.shape) for w in w_ts] + [ANY] * na,
        out_specs=[_rows(tm, w.shape[0]) for w in w_ts],
        out_shape=[jax.ShapeDtypeStruct((s, w.shape[0]), BF16) for w in w_ts],
        compiler_params=_params(("parallel",)), name=name,
    )(h, *w_ts, *after)


def _mix_tail_fwd(ya_in, yb_in, xc, pgt, w_ups, w_o, x, g_post, g_next, after, name, tm=512):
    s, d = x.shape
    tm = _tile(s, tm)
    na = len(after)
    branch_ins = (ya_in, yb_in, xc)

    def body(a_ref, b_ref, c_ref, gt_ref, wa_ref, wb_ref, wc_ref, wo_ref, x_ref, gp_ref, gn_ref, *rest):
        ya_ref, yb_ref, yc_ref, m_ref, y_ref, xo_ref, h_ref = rest[na:]
        for rows in _sub_blocks(tm):
            merged = None
            for j, (in_ref, w_ref, out_ref) in enumerate(((a_ref, wa_ref, ya_ref), (b_ref, wb_ref, yb_ref), (c_ref, wc_ref, yc_ref))):
                yj = jnp.dot(in_ref[rows, :], w_ref[...], preferred_element_type=F32)
                out_ref[rows, :] = yj.astype(out_ref.dtype)
                part = _sigmoid(gt_ref[rows, j * D:(j + 1) * D].astype(F32)) * yj
                merged = part if merged is None else merged + part
            merged_b = merged.astype(m_ref.dtype)
            m_ref[rows, :] = merged_b
            y = jnp.dot(merged_b, wo_ref[...], preferred_element_type=F32)
            y_ref[rows, :] = y
            yh, _ = _rms(y)
            xn = x_ref[rows, :] + yh * gp_ref[...]
            xo_ref[rows, :] = xn
            xh, _ = _rms(xn)
            h_ref[rows, :] = (xh * gn_ref[...]).astype(h_ref.dtype)

    bf = lambda: jax.ShapeDtypeStruct((s, d), BF16)
    f32 = lambda: jax.ShapeDtypeStruct((s, d), F32)
    return pl.pallas_call(
        body, grid=(s // tm,),
        in_specs=[_rows(tm, a.shape[1]) for a in branch_ins] + [_rows(tm, 3 * d)] + [_resident(w.shape) for w in w_ups]
        + [_resident(w_o.shape), _rows(tm, d), _fixed((1, d)), _fixed((1, d))] + [ANY] * na,
        out_specs=[_rows(tm, d)] * 7,
        out_shape=[bf(), bf(), bf(), bf(), f32(), f32(), bf()],
        compiler_params=_params(("parallel",)), name=name,
    )(*branch_ins, pgt, *w_ups, w_o, x, g_post, g_next, *after)


def _mix_tail_bwd(dy, pgt, ys, w_ups, w_o, after, name, tm=512):
    s, d = dy.shape
    tm = _tile(s, tm)
    na = len(after)
    widths = [w.shape[0] for w in w_ups]

    def body(dy_ref, gt_ref, ya_ref, yb_ref, yc_ref, wa_ref, wb_ref, wc_ref, wo_ref, *rest):
        dya_ref, dyb_ref, dyc_ref, dgt_ref, da_ref, db_ref, dc_ref = rest[na:]
        nt = (((1,), (1,)), ((), ()))
        for rows in _sub_blocks(tm):
            dm = lax.dot_general(dy_ref[rows, :], wo_ref[...], nt, preferred_element_type=F32)
            for j, (y_ref, dyj_ref, w_ref, din_ref) in enumerate(((ya_ref, dya_ref, wa_ref, da_ref), (yb_ref, dyb_ref, wb_ref, db_ref),
                                                                   (yc_ref, dyc_ref, wc_ref, dc_ref))):
                sig = _sigmoid(gt_ref[rows, j * D:(j + 1) * D].astype(F32))
                dyj = (dm * sig).astype(dyj_ref.dtype)
                dyj_ref[rows, :] = dyj
                dgt_ref[rows, j * D:(j + 1) * D] = (dm * y_ref[rows, :].astype(F32) * sig * (1.0 - sig)).astype(dgt_ref.dtype)
                din_ref[rows, :] = lax.dot_general(dyj, w_ref[...], nt, preferred_element_type=F32).astype(din_ref.dtype)

    bf = lambda w: jax.ShapeDtypeStruct((s, w), BF16)
    return pl.pallas_call(
        body, grid=(s // tm,),
        in_specs=[_rows(tm, d), _rows(tm, 3 * d)] + [_rows(tm, d)] * 3 + [_resident(w.shape) for w in w_ups] + [_resident(w_o.shape)]
        + [ANY] * na,
        out_specs=[_rows(tm, d)] * 3 + [_rows(tm, 3 * d)] + [_rows(tm, w) for w in widths],
        out_shape=[bf(d), bf(d), bf(d), bf(3 * d)] + [bf(w) for w in widths],
        compiler_params=_params(("parallel",)), name=name,
    )(dy, pgt, *ys, *w_ups, w_o, *after)


def _adam_math(w, g, m, v):
    mn = ADAM_B1 * m + (1.0 - ADAM_B1) * g
    vn = ADAM_B2 * v + (1.0 - ADAM_B2) * (g * g)
    m_hat = mn / (1.0 - ADAM_B1 ** ADAM_STEP)
    v_hat = vn / (1.0 - ADAM_B2 ** ADAM_STEP)
    return -ADAM_LR * (m_hat / (jnp.sqrt(v_hat) + ADAM_EPS) + ADAM_WD * w), mn, vn


def _adamw(w, g, m, v, name):
    r, c = w.shape[-2:]
    tr, tc = _block_of(r, c, cap=512 if r % 16 == 0 else 256)

    def spec(a):
        if a.ndim == 2:
            return pl.BlockSpec((tr, tc), lambda i, j: (i, j))
        return pl.BlockSpec((None, tr, tc), lambda i, j: (0, i, j))

    def body(w_ref, g_ref, m_ref, v_ref, d_ref, mo_ref, vo_ref):
        d_ref[...], mo_ref[...], vo_ref[...] = _adam_math(w_ref[...], g_ref[...], m_ref[...], v_ref[...])

    return pl.pallas_call(
        body, grid=(r // tr, c // tc), in_specs=[spec(a) for a in (w, g, m, v)], out_specs=[spec(w)] * 3,
        out_shape=[jax.ShapeDtypeStruct(w.shape, F32)] * 3, compiler_params=_params(("parallel", "parallel")), name=name,
    )(w, g, m, v)


ANY = pl.BlockSpec(memory_space=pl.ANY)


def _place():
    x, y, c = lax.axis_index("x"), lax.axis_index("y"), lax.axis_index("c")
    chips = [(1 - x, y), (x, 1 - y), (1 - x, 1 - y)]
    return x, y, c, chips


def _by_cols(rows):
    return rows % 32 != 0 and rows != 16


def _half_of(ref, lead, c):
    r, cols = ref.shape[-2:]
    if _by_cols(r):
        return ref.at[(*lead, slice(None), pl.ds(pl.multiple_of(c * (cols // 2), LANE), cols // 2))]
    return ref.at[(*lead, pl.ds(pl.multiple_of(c * (r // 2), 8), r // 2))]


def _half_shape(shape):
    r, cols = shape[-2:]
    return shape[:-2] + ((r, cols // 2) if _by_cols(r) else (r // 2, cols))


def _block_of(r, cols, cap=256):
    if r % 16 == 0:
        return _tile(r, cap, 16), cols
    return r, _tile(cols, cap)


def _place_shard(shard, chip_arr, out_dtype, name, after=()):
    _, r, cols = shard.shape
    tr, tc = _block_of(r, cols)

    def body(chip_ref, s_ref, *rest):
        o_ref = rest[len(after)]
        o_ref[...] = s_ref[...].astype(o_ref.dtype)

    return pl.pallas_call(
        body,
        grid_spec=pltpu.PrefetchScalarGridSpec(
            num_scalar_prefetch=1, grid=(r // tr, cols // tc),
            in_specs=[pl.BlockSpec((None, tr, tc), lambda i, j, chip_ref: (0, i, j))] + [ANY] * len(after),
            out_specs=pl.BlockSpec((None, tr, tc), lambda i, j, chip_ref: (chip_ref[0], i, j))),
        out_shape=jax.ShapeDtypeStruct((4, r, cols), out_dtype),
        compiler_params=_params(("parallel", "parallel")), name=name,
    )(chip_arr, shard, *after)


HBM = pl.BlockSpec(memory_space=pltpu.HBM)
SEM = pl.BlockSpec(memory_space=pltpu.SEMAPHORE)
EFFECT = pltpu.SideEffectType.DATAFLOW_SIDE_EFFECTING


def _in_hbm(arrays):
    return [pltpu.with_memory_space_constraint(a, pltpu.HBM) for a in arrays]


def _gather_start(bufs, after, name):
    n, na = len(bufs), len(after)

    def body(*refs):
        send_sem, recv_sem = refs[n + na], refs[n + na + 1]
        outs = refs[n + na + 2:2 * n + na + 2]
        token = refs[2 * n + na + 2]
        x, y, c, chips = _place()
        me = 2 * x + y
        for p, chip in enumerate(chips):
            for w in range(n):
                block = _half_of(outs[w], (me,), c)
                pltpu.make_async_remote_copy(
                    src_ref=block, dst_ref=block, send_sem=send_sem, recv_sem=recv_sem,
                    device_id=(*chip, c), device_id_type=MESH).start()
        token[...] = jnp.zeros_like(token)

    out = pl.pallas_call(
        body, name=name, in_specs=[HBM] * n + [ANY] * na,
        out_specs=[SEM, SEM] + [HBM] * n + [pl.BlockSpec(memory_space=pltpu.VMEM)],
        out_shape=[pltpu.SemaphoreType.DMA(()), pltpu.SemaphoreType.DMA(())]
        + [pltpu.HBM(a.shape, a.dtype) for a in bufs] + [jax.ShapeDtypeStruct((8, LANE), F32)],
        input_output_aliases={w: w + 2 for w in range(n)},
        compiler_params=pltpu.CompilerParams(has_side_effects=EFFECT),
    )(*_in_hbm(bufs), *after)
    return out[0], out[1], list(out[2:2 + n]), out[2 + n]


def _gather_pass(bufs, send_sem, recv_sem, after, name):
    n, na = len(bufs), len(after)

    def body(*refs):
        send1, recv1 = refs[n], refs[n + 1]
        send2, recv2 = refs[n + 2 + na], refs[n + 3 + na]
        outs = refs[n + 4 + na:2 * n + 4 + na]
        x, y, c, chips = _place()
        me = 2 * x + y
        arrivals = [(w, px, py) for px, py in chips for w in range(n)]
        for w, px, py in arrivals:
            first = pltpu.make_async_remote_copy(
                src_ref=_half_of(outs[w], (me,), c), dst_ref=_half_of(outs[w], (2 * px + py,), c), send_sem=send1, recv_sem=recv1,
                device_id=(px, py, c), device_id_type=MESH)
            first.wait_send()
            first.wait_recv()
        for w, px, py in arrivals:
            arrived = _half_of(outs[w], (2 * px + py,), c)
            pltpu.make_async_remote_copy(
                src_ref=arrived, dst_ref=arrived, send_sem=send2, recv_sem=recv2,
                device_id=(x, y, 1 - c), device_id_type=MESH).start()

    out = pl.pallas_call(
        body, name=name, in_specs=[HBM] * n + [SEM, SEM] + [ANY] * na,
        out_specs=[SEM, SEM] + [HBM] * n,
        out_shape=[pltpu.SemaphoreType.DMA(()), pltpu.SemaphoreType.DMA(())] + [pltpu.HBM(a.shape, a.dtype) for a in bufs],
        input_output_aliases={w: w + 2 for w in range(n)},
        compiler_params=pltpu.CompilerParams(has_side_effects=EFFECT),
    )(*bufs, send_sem, recv_sem, *after)
    return out[0], out[1], list(out[2:])


def _gather_finish(bufs, send_sem, recv_sem, after, name):
    n, na = len(bufs), len(after)

    def body(*refs):
        send2, recv2 = refs[n], refs[n + 1]
        outs = refs[n + 2 + na:2 * n + 2 + na]
        x, y, c, chips = _place()
        for p, (px, py) in enumerate(chips):
            for w in range(n):
                passed = pltpu.make_async_remote_copy(
                    src_ref=_half_of(outs[w], (2 * px + py,), c), dst_ref=_half_of(outs[w], (2 * px + py,), 1 - c),
                    send_sem=send2, recv_sem=recv2, device_id=(x, y, 1 - c), device_id_type=MESH)
                passed.wait_send()
                passed.wait_recv()

    out = pl.pallas_call(
        body, name=name, in_specs=[HBM] * n + [SEM, SEM] + [ANY] * na, out_specs=[HBM] * n,
        out_shape=[pltpu.HBM(a.shape, a.dtype) for a in bufs],
        input_output_aliases={w: w for w in range(n)},
        compiler_params=pltpu.CompilerParams(has_side_effects=EFFECT),
    )(*bufs, send_sem, recv_sem, *after)
    return list(out)


def _pair_exchange_start(grads, after, name):
    n, na = len(grads), len(after)
    lands = [lax.empty(_half_shape(a.shape), a.dtype) for a in grads]

    def body(*refs):
        send_sem, recv_sem = refs[2 * n + na], refs[2 * n + na + 1]
        srcs = refs[2 * n + na + 2:3 * n + na + 2]
        dsts = refs[3 * n + na + 2:4 * n + na + 2]
        token = refs[4 * n + na + 2]
        x, y, c, _ = _place()
        for w in range(n):
            pltpu.make_async_remote_copy(
                src_ref=_half_of(srcs[w], (slice(None),), 1 - c), dst_ref=dsts[w], send_sem=send_sem, recv_sem=recv_sem,
                device_id=(x, y, 1 - c), device_id_type=MESH).start()
        token[...] = jnp.zeros_like(token)

    out = pl.pallas_call(
        body, name=name, in_specs=[HBM] * (2 * n) + [ANY] * na,
        out_specs=[SEM, SEM] + [HBM] * (2 * n) + [pl.BlockSpec(memory_space=pltpu.VMEM)],
        out_shape=[pltpu.SemaphoreType.DMA(()), pltpu.SemaphoreType.DMA(())]
        + [pltpu.HBM(a.shape, a.dtype) for a in grads + lands] + [jax.ShapeDtypeStruct((8, LANE), F32)],
        input_output_aliases={w: w + 2 for w in range(2 * n)},
        compiler_params=pltpu.CompilerParams(has_side_effects=EFFECT),
    )(*_in_hbm(grads), *_in_hbm(lands), *after)
    return out[0], out[1], list(out[2:2 + n]), list(out[2 + n:2 + 2 * n]), out[2 + 2 * n]


def _pair_exchange_finish(grads, lands, send_sem, recv_sem, after, name):
    n, na = len(grads), len(after)

    def body(*refs):
        send, recv = refs[2 * n], refs[2 * n + 1]
        srcs = refs[2 * n + 2 + na:3 * n + 2 + na]
        dsts = refs[3 * n + 2 + na:4 * n + 2 + na]
        x, y, c, _ = _place()
        for w in range(n):
            copy = pltpu.make_async_remote_copy(
                src_ref=_half_of(srcs[w], (slice(None),), 1 - c), dst_ref=dsts[w], send_sem=send, recv_sem=recv,
                device_id=(x, y, 1 - c), device_id_type=MESH)
            copy.wait_send()
            copy.wait_recv()

    out = pl.pallas_call(
        body, name=name, in_specs=[HBM] * (2 * n) + [SEM, SEM] + [ANY] * na, out_specs=[HBM] * (2 * n),
        out_shape=[pltpu.HBM(a.shape, a.dtype) for a in grads + lands],
        input_output_aliases={w: w for w in range(2 * n)},
        compiler_params=pltpu.CompilerParams(has_side_effects=EFFECT),
    )(*grads, *lands, send_sem, recv_sem, *after)
    return list(out[:n]), list(out[n:])


def _pair_sum(g, got, c_arr, name):
    _, r, cols = g.shape
    hr, hc = _half_shape((r, cols))
    tr, tc = _block_of(hr, hc)
    nbr, nbc = hr // tr, hc // tc
    by_cols = _by_cols(r)

    def body(c_ref, g_ref, got_ref, o_ref):
        o_ref[...] = (g_ref[...].astype(F32) + got_ref[...].astype(F32)).astype(o_ref.dtype)

    def mine(j, i, k, c_ref):
        return (j, i, c_ref[0] * nbc + k) if by_cols else (j, c_ref[0] * nbr + i, k)

    return pl.pallas_call(
        body,
        grid_spec=pltpu.PrefetchScalarGridSpec(
            num_scalar_prefetch=1, grid=(4, nbr, nbc),
            in_specs=[pl.BlockSpec((None, tr, tc), mine),
                      pl.BlockSpec((None, tr, tc), lambda j, i, k, c_ref: (j, i, k))],
            out_specs=pl.BlockSpec((None, tr, tc), lambda j, i, k, c_ref: (j, i, k))),
        out_shape=jax.ShapeDtypeStruct((4, hr, hc), BF16),
        compiler_params=_params(("parallel", "parallel", "parallel")), name=name,
    )(c_arr, *_in_hbm([g, got]))


def _chip_exchange_start(parts, after, name):
    n, na = len(parts), len(after)
    lands = [lax.empty((3,) + a.shape[1:], a.dtype) for a in parts]

    def body(*refs):
        send_sem, recv_sem = refs[2 * n + na], refs[2 * n + na + 1]
        srcs = refs[2 * n + na + 2:3 * n + na + 2]
        dsts = refs[3 * n + na + 2:4 * n + na + 2]
        token = refs[4 * n + na + 2]
        x, y, c, chips = _place()
        for p, (px, py) in enumerate(chips):
            for w in range(n):
                pltpu.make_async_remote_copy(
                    src_ref=srcs[w].at[2 * px + py], dst_ref=dsts[w].at[p], send_sem=send_sem, recv_sem=recv_sem,
                    device_id=(px, py, c), device_id_type=MESH).start()
        token[...] = jnp.zeros_like(token)

    out = pl.pallas_call(
        body, name=name, in_specs=[HBM] * (2 * n) + [ANY] * na,
        out_specs=[SEM, SEM] + [HBM] * (2 * n) + [pl.BlockSpec(memory_space=pltpu.VMEM)],
        out_shape=[pltpu.SemaphoreType.DMA(()), pltpu.SemaphoreType.DMA(())]
        + [pltpu.HBM(a.shape, a.dtype) for a in parts + lands] + [jax.ShapeDtypeStruct((8, LANE), F32)],
        input_output_aliases={w: w + 2 for w in range(2 * n)},
        compiler_params=pltpu.CompilerParams(has_side_effects=EFFECT),
    )(*_in_hbm(parts), *_in_hbm(lands), *after)
    return out[0], out[1], list(out[2:2 + n]), list(out[2 + n:2 + 2 * n]), out[2 + 2 * n]


def _chip_exchange_finish(parts, lands, send_sem, recv_sem, after, name):
    n, na = len(parts), len(after)

    def body(*refs):
        send, recv = refs[2 * n], refs[2 * n + 1]
        srcs = refs[2 * n + 2 + na:3 * n + 2 + na]
        dsts = refs[3 * n + 2 + na:4 * n + 2 + na]
        x, y, c, chips = _place()
        for p, (px, py) in enumerate(chips):
            for w in range(n):
                copy = pltpu.make_async_remote_copy(
                    src_ref=srcs[w].at[2 * px + py], dst_ref=dsts[w].at[p], send_sem=send, recv_sem=recv,
                    device_id=(px, py, c), device_id_type=MESH)
                copy.wait_send()
                copy.wait_recv()

    out = pl.pallas_call(
        body, name=name, in_specs=[HBM] * (2 * n) + [SEM, SEM] + [ANY] * na, out_specs=[HBM] * (2 * n),
        out_shape=[pltpu.HBM(a.shape, a.dtype) for a in parts + lands],
        input_output_aliases={w: w for w in range(2 * n)},
        compiler_params=pltpu.CompilerParams(has_side_effects=EFFECT),
    )(*parts, *lands, send_sem, recv_sem, *after)
    return list(out[:n]), list(out[n:])


def _chip_sum(part, got, place_arr, name):
    _, hr, hc = part.shape
    by_cols = _by_cols(hr)
    tr, tc = _block_of(hr, hc)
    nbr, nbc = hr // tr, hc // tc

    def body(place_ref, p_ref, got_ref, o_ref):
        acc = p_ref[...].astype(F32)
        for p in range(3):
            acc = acc + got_ref[p].astype(F32)
        o_ref[...] = acc

    def mine(i, k, place_ref):
        return (i, place_ref[1] * nbc + k) if by_cols else (place_ref[1] * nbr + i, k)

    return pl.pallas_call(
        body,
        grid_spec=pltpu.PrefetchScalarGridSpec(
            num_scalar_prefetch=1, grid=(nbr, nbc),
            in_specs=[pl.BlockSpec((None, tr, tc), lambda i, k, place_ref: (place_ref[0], i, k)),
                      pl.BlockSpec((3, tr, tc), lambda i, k, place_ref: (0, i, k))],
            out_specs=pl.BlockSpec((tr, tc), mine)),
        out_shape=jax.ShapeDtypeStruct((hr, 2 * hc) if by_cols else (2 * hr, hc), F32),
        compiler_params=_params(("parallel", "parallel")), name=name,
    )(place_arr, *_in_hbm([part, got]))


def _pair_join_start(bufs, name):
    n = len(bufs)

    def body(*refs):
        send_sem, recv_sem = refs[n], refs[n + 1]
        outs = refs[n + 2:2 * n + 2]
        token = refs[2 * n + 2]
        x, y, c, _ = _place()
        for w in range(n):
            block = _half_of(outs[w], (), c)
            pltpu.make_async_remote_copy(
                src_ref=block, dst_ref=block, send_sem=send_sem, recv_sem=recv_sem,
                device_id=(x, y, 1 - c), device_id_type=MESH).start()
        token[...] = jnp.zeros_like(token)

    out = pl.pallas_call(
        body, name=name, in_specs=[HBM] * n,
        out_specs=[SEM, SEM] + [HBM] * n + [pl.BlockSpec(memory_space=pltpu.VMEM)],
        out_shape=[pltpu.SemaphoreType.DMA(()), pltpu.SemaphoreType.DMA(())]
        + [pltpu.HBM(a.shape, a.dtype) for a in bufs] + [jax.ShapeDtypeStruct((8, LANE), F32)],
        input_output_aliases={w: w + 2 for w in range(n)},
        compiler_params=pltpu.CompilerParams(has_side_effects=EFFECT),
    )(*_in_hbm(bufs))
    return out[0], out[1], list(out[2:2 + n]), out[2 + n]


def _pair_join_finish(bufs, send_sem, recv_sem, after, name):
    n, na = len(bufs), len(after)

    def body(*refs):
        send, recv = refs[n], refs[n + 1]
        outs = refs[n + 2 + na:2 * n + 2 + na]
        x, y, c, _ = _place()
        for w in range(n):
            copy = pltpu.make_async_remote_copy(
                src_ref=_half_of(outs[w], (), c), dst_ref=_half_of(outs[w], (), 1 - c), send_sem=send, recv_sem=recv,
                device_id=(x, y, 1 - c), device_id_type=MESH)
            copy.wait_send()
            copy.wait_recv()

    out = pl.pallas_call(
        body, name=name, in_specs=[HBM] * n + [SEM, SEM] + [ANY] * na, out_specs=[HBM] * n,
        out_shape=[pltpu.HBM(a.shape, a.dtype) for a in bufs],
        input_output_aliases={w: w for w in range(n)},
        compiler_params=pltpu.CompilerParams(has_side_effects=EFFECT),
    )(*bufs, send_sem, recv_sem, *after)
    return list(out)


SMALL = ("ffn1_pre_g", "ffn1_post_g", "mix_pre_g", "gla_norm_g", "mem_norm_g", "mix_post_g", "ffn2_pre_g", "ffn2_post_g", "final_g",
         "b_f", "pool_scale", "w_pool", "w_fu")
N_GAINS = 9
SMALL_PACKS = ((16, D), (24, 512), (4 * LANE, LANE))
W_FU_ROW = 8


LOSS_ROW = 2


def _all_sum_small(gs, loss, name, after=()):
    ins = [gs[n] for n in SMALL[:N_GAINS]] + [gs["b_f"], gs["pool_scale"], gs["w_fu_pad"], gs["w_pool"].reshape(4 * LANE, LANE), loss]

    def body(*refs):
        gain_refs = refs[:N_GAINS]
        bf_ref, ps_ref, wfu_ref, wp_ref, loss_ref = refs[N_GAINS:N_GAINS + 5]
        outs = refs[N_GAINS + 5 + len(after):N_GAINS + 8 + len(after)]
        mine_a, mine_b, all_a, all_b, all_c, send_sems, recv_sems = refs[N_GAINS + 8 + len(after):]
        mine_a[...] = jnp.zeros_like(mine_a)
        for i, ref in enumerate(gain_refs):
            mine_a[i:i + 1, :] = ref[...]
        mine_b[...] = jnp.zeros_like(mine_b)
        mine_b[0:1, :] = bf_ref[...]
        mine_b[1:2, :] = ps_ref[...]
        mine_b[LOSS_ROW:LOSS_ROW + 1, 0:LANE] = loss_ref[0:1, :]
        mine_b[W_FU_ROW:W_FU_ROW + GATE_RANK, :] = wfu_ref[0:GATE_RANK, :]
        packs = ((mine_a, all_a), (mine_b, all_b), (wp_ref, all_c))
        x, y, c, chips = _place()
        me, sibling = (x, y, c), (x, y, 1 - c)

        def copy(t, k, block, to, own=False):
            px, py, pc = block
            slot = packs[t][1].at[4 * px + 2 * py + pc]
            return pltpu.make_async_remote_copy(
                src_ref=packs[t][0] if own else slot, dst_ref=slot,
                send_sem=send_sems.at[t, k], recv_sem=recv_sems.at[t, k], device_id=to, device_id_type=MESH)

        started = []
        for t, (mine, everyone) in enumerate(packs):
            everyone[4 * x + 2 * y + c] = mine[...]
            started.append(copy(t, 0, me, sibling, own=True))
            started += [copy(t, 1 + j, me, (*chip, c), own=True) for j, chip in enumerate(chips)]
        for cp in started:
            cp.start()
        passed = []
        for j, chip in enumerate(chips):
            for t in range(len(packs)):
                copy(t, 1 + j, (*chip, c), me).wait_recv()
                fwd = copy(t, 4 + j, (*chip, c), sibling)
                fwd.start()
                passed.append(fwd)
        for t in range(len(packs)):
            copy(t, 0, sibling, me).wait_recv()
            for j, chip in enumerate(chips):
                copy(t, 4 + j, (*chip, 1 - c), me).wait_recv()
        for cp in started + passed:
            cp.wait_send()
        for (_, everyone), o_ref in zip(packs, outs):
            acc = everyone[0]
            for k in range(1, 8):
                acc = acc + everyone[k]
            o_ref[...] = acc

    vmem = pl.BlockSpec(memory_space=pltpu.VMEM)
    return pl.pallas_call(
        body, in_specs=[vmem] * len(ins) + [ANY] * len(after), out_specs=[vmem] * 3,
        out_shape=[jax.ShapeDtypeStruct(shape, F32) for shape in SMALL_PACKS],
        scratch_shapes=[pltpu.VMEM(SMALL_PACKS[0], F32), pltpu.VMEM(SMALL_PACKS[1], F32)]
        + [pltpu.VMEM((8,) + shape, F32) for shape in SMALL_PACKS]
        + [pltpu.SemaphoreType.DMA((3, 7)), pltpu.SemaphoreType.DMA((3, 7))],
        compiler_params=pltpu.CompilerParams(has_side_effects=True, vmem_limit_bytes=VMEM_LIMIT), name=name,
    )(*ins, *after)


def _adamw_small(sums, params, chip_arr, name):
    flat = [a for n in SMALL for a in params[n]]

    def body(chip_ref, a_ref, b_ref, c_ref, *refs):
        ins, outs = refs[:len(flat)], refs[len(flat):]
        for i, n in enumerate(SMALL):
            w_ref, m_ref, v_ref = ins[3 * i:3 * i + 3]
            g_ref, d_ref, mo_ref, vo_ref = outs[4 * i:4 * i + 4]
            if n == "w_pool":
                pieces = [((0, k), c_ref[k * LANE:(k + 1) * LANE, :]) for k in range(4)]
            elif n == "w_fu":
                mine = pl.ds(pl.multiple_of(chip_ref[0] * LANE, LANE), LANE)
                pieces = [((0,), b_ref[W_FU_ROW:W_FU_ROW + GATE_RANK, mine])]
            elif n == "b_f":
                pieces = [((), b_ref[0:1, :])]
            elif n == "pool_scale":
                pieces = [((), b_ref[1:2, :])]
            else:
                pieces = [((), a_ref[i:i + 1, :])]
            for at, g in pieces:
                d, mn, vn = _adam_math(w_ref[at], g, m_ref[at], v_ref[at])
                g_ref[at] = g
                d_ref[at] = d
                mo_ref[at] = mn
                vo_ref[at] = vn

    def whole(shape):
        return pl.BlockSpec(shape, lambda i, chip_ref: (0,) * len(shape))

    out = pl.pallas_call(
        body,
        grid_spec=pltpu.PrefetchScalarGridSpec(
            num_scalar_prefetch=1, grid=(1,),
            in_specs=[whole(a.shape) for a in list(sums) + flat],
            out_specs=[whole(params[n][0].shape) for n in SMALL for _ in range(4)]),
        out_shape=[jax.ShapeDtypeStruct(params[n][0].shape, F32) for n in SMALL for _ in range(4)],
        compiler_params=_params(("arbitrary",)), name=name,
    )(chip_arr, *sums, *flat)
    return {n: tuple(out[4 * i:4 * i + 4]) for i, n in enumerate(SMALL)}


def _ffn_bwd(dz, x_norm, ab, u, w_in, w_out, x, g_pre, dres, tag, emit, advance, after=(), post=None):
    dw_out = _mm(u, dz, ta=True, out_dtype=BF16, tm=1408, tk=2048, after=after, name=tag + "_out_dw")
    behind = emit(tag + "_w_out", dw_out)
    dab = _ffn_out_dx_swiglu(dz, w_out, ab, behind, name=tag + "_out_dx")
    behind = advance((dab,))
    dw_in = _mm(x_norm, dab, ta=True, out_dtype=BF16, tm=512, tk=4096, shards=4, after=behind, name=tag + "_in_dw")
    behind = emit(tag + "_w_in", dw_in)
    out = _mm_rms_bwd([(dab, w_in)], x, g_pre, dres, after=behind, post=post, name=tag + "_in_dx")
    return (*out, advance((out[0],)))


def _local_step(x, mem, target, small, gather, emit, advance):
    behind = gather("start", "ffn1i", ())
    behind = gather("start", "ffn1o", behind)
    h1 = _norm_fwd(x, small["ffn1_pre_g"], BF16, name="ffn1_pre", after=behind)
    gather("pass", "ffn1i", (h1,))
    big = gather("finish", "ffn1i", ())
    behind = gather("start", "mixa", (big["ffn1_w_in"],))
    behind = gather("start", "mixb", behind)
    behind = gather("start", "ffn2", behind)
    ab1, u1 = _ffn_in_swiglu(h1, big["ffn1_w_in"], name="ffn1_in", after=behind)
    gather("pass", "ffn1o", (ab1,))
    big.update(gather("finish", "ffn1o", ()))
    behind = gather("pass", "mixa", (u1,))
    f1, x1, h = _mm_resid_norm(u1, big["ffn1_w_out"], x, small["ffn1_post_g"], 0.5, small["mix_pre_g"], name="ffn1_out", after=behind)
    big.update(gather("finish", "mixa", (h,)))
    small = dict(small, w_fu_pad=big["w_fu_pad"])
    behind = gather("pass", "mixb", (h,))
    pg, ppx, pgt = _mix_in_fwd(h, [big["w_gla_t"], big["w_px_t"], big["w_gates_t"]], behind, name="mix_in")
    big.update(gather("finish", "mixb", (pgt,)))
    mem_n = _norm_fwd(mem, small["mem_norm_g"], BF16, name="mem_norm")
    kv = _mm(mem_n, big["w_mem_kv"], out_dtype=BF16, name="mem_kv")
    ya_in, sp, so, o_gla = _gla_fwd(pg, small["w_fu_pad"], small["b_f"], small["gla_norm_g"], name="gla_fwd")
    yb_in = _pool_fwd(ppx, small["w_pool_b"], small["pool_scale"], name="pool_fwd")
    xc = _xattn_fwd(ppx, kv, name="xattn_fwd")
    behind = gather("pass", "ffn2", (xc,))
    w_ups = [big["w_up_gla"], big["w_up_pool"], big["w_up_xattn"]]
    ya, yb, yc, merged, ymix, x2, h2 = _mix_tail_fwd(ya_in, yb_in, xc, pgt, w_ups, big["w_o"], x1, small["mix_post_g"],
                                                     small["ffn2_pre_g"], behind, name="mix_tail")
    big.update(gather("finish", "ffn2", (h2,)))
    ab2, u2 = _ffn_in_swiglu(h2, big["ffn2_w_in"], name="ffn2_in")
    gs = {}
    dz2, dx3, gs["ffn2_post_g"], gs["final_g"], loss = _ffn_out_loss(u2, big["ffn2_w_out"], x2, small["ffn2_post_g"], 0.5,
                                                                    small["final_g"], target, name="ffn2_out_loss")
    dx2, gs["ffn2_pre_g"], dy, gs["mix_post_g"], behind = _ffn_bwd(
        dz2, h2, ab2, u2, big["ffn2_w_in"], big["ffn2_w_out"], x2, small["ffn2_pre_g"], dx3, "ffn2", emit, advance,
        post=(ymix, small["mix_post_g"], 1.0))
    emit("w_o", _mm(merged, dy, ta=True, out_dtype=BF16, tm=512, tk=4096, after=behind, name="mix_out_dw"))
    dya, dyb, dyc, dgt, dya_in, dyb_in, dxc = _mix_tail_bwd(dy, pgt, (ya, yb, yc), w_ups, big["w_o"], (), name="mix_tail_bwd")
    emit("w_up_gla", _mm(ya_in, dya, ta=True, out_dtype=BF16, tm=512, tk=4096, name="up_gla_dw"))
    emit("w_up_pool", _mm(yb_in, dyb, ta=True, out_dtype=BF16, tm=512, tk=4096, shards=4, name="up_pool_dw"))
    emit("w_up_xattn", _mm(xc, dyc, ta=True, out_dtype=BF16, tm=512, tk=4096, shards=4, name="up_xattn_dw"))
    dpg, gs["w_fu_pad"], gs["b_f"], gs["gla_norm_g"] = _gla_bwd(pg, sp, so, o_gla, dya_in, small["w_fu_pad"], small["b_f"], small["gla_norm_g"], name="gla_bwd")
    dp, gs["w_pool"], gs["pool_scale"] = _pool_bwd(dyb_in, ppx, small["w_pool_b"], small["pool_scale"], name="pool_bwd")
    dxq, dkv = _xattn_bwd(dxc, ppx, kv, name="xattn_bwd")
    dkv = dkv.astype(BF16)
    emit("w_mem_kv", _mm(mem_n, dkv, ta=True, out_dtype=BF16, name="mem_kv_dw"))
    dmem_n = _mm(dkv, big["w_mem_kv"], tb=True, name="mem_kv_dx")
    _, gs["mem_norm_g"] = _rms_bwd(mem, small["mem_norm_g"], [dmem_n], None, 1.0, BF16, name="mem_norm_bwd")
    emit("w_gla", _mm(dpg, h, ta=True, out_dtype=BF16, tm=640, tk=4096, name="mix_in_gla_dw"))
    emit("w_p", _mm(dp, h, ta=True, out_dtype=BF16, tm=512, tk=4096, name="mix_in_p_dw"))
    emit("w_xq", _mm(dxq, h, ta=True, out_dtype=BF16, tm=512, tk=4096, name="mix_in_xq_dw"))
    behind = emit("w_gates", _mm(dgt, h, ta=True, out_dtype=BF16, tm=512, tk=4096, name="mix_in_gates_dw"))
    pairs = [(dpg, big["w_gla_t"]), (dp, big["w_p_t"]), (dxq, big["w_xq_t"]), (dgt, big["w_gates_t"])]
    dx1, gs["mix_pre_g"], dz1, gs["ffn1_post_g"] = _mm_rms_bwd(pairs, x1, small["mix_pre_g"], dx2, after=behind,
                                                               post=(f1, small["ffn1_post_g"], 0.5), tm=256, name="mix_in_dx")
    behind = advance((dx1,))
    dx0, gs["ffn1_pre_g"], _ = _ffn_bwd(dz1, h1, ab1, u1, big["ffn1_w_in"], big["ffn1_w_out"], x, small["ffn1_pre_g"], dx1,
                                        "ffn1", emit, advance, after=behind)
    return loss, dx0, gs


BIG = ("ffn1_w_in", "ffn1_w_out", "w_in", "w_mem_kv", "w_up_gla", "w_up_pool", "w_up_xattn", "w_o", "ffn2_w_in", "ffn2_w_out")
COL_SHARDED = ("ffn1_w_in", "w_in", "w_up_pool", "w_up_xattn", "ffn2_w_in")
GATHER_GROUPS = {"ffn1i": ("ffn1_w_in",), "ffn1o": ("ffn1_w_out",), "mixa": ("w_in", "w_fu"),
                 "mixb": ("w_mem_kv", "w_up_gla", "w_up_pool", "w_up_xattn", "w_o"), "ffn2": ("ffn2_w_in", "ffn2_w_out")}
REDUCE_GROUPS = {"ffn2": ("ffn2_w_out", "ffn2_w_in"),
                 "mix": ("w_o", "w_up_gla", "w_up_pool", "w_up_xattn", "w_mem_kv", "w_gla", "w_p", "w_xq", "w_gates"),
                 "ffn1_out": ("ffn1_w_out",),
                 "ffn1_in": ("ffn1_w_in",)}
REDUCE_FIRST, REDUCE_LAST = "ffn2", "ffn1_in"
GAINS = ("ffn1_pre_g", "ffn1_post_g", "mix_pre_g", "gla_norm_g", "mem_norm_g", "mix_post_g", "ffn2_pre_g", "ffn2_post_g", "final_g")
WEIGHTS = ("ffn1_pre_g", "ffn1_w_in", "ffn1_w_out", "ffn1_post_g", "mix_pre_g", "w_in", "w_fu", "b_f", "gla_norm_g", "w_pool",
           "pool_scale", "mem_norm_g", "w_mem_kv", "w_up_gla", "w_up_pool", "w_up_xattn", "w_o", "mix_post_g", "ffn2_pre_g",
           "ffn2_w_in", "ffn2_w_out", "ffn2_post_g", "final_g")
IN_GLA, IN_F, IN_PX, IN_GATES, IN_END = 0, 3072, 3088, 4112, 7184
def _cols_from_shards(g):
    return jnp.transpose(g, (1, 0, 2)).reshape(g.shape[1], 4 * g.shape[2])


def _laid_end_to_end(pieces, rows):
    out, start = None, 0
    for p in pieces:
        padded = jnp.pad(p, ((start, rows - start - p.shape[0]), (0, 0)))
        out = padded if out is None else out + padded
        start += p.shape[0]
    return out


def _rows_of_blocks(g, lo, hi, rows=None):
    q = g.shape[1]
    cuts = [(j, max(lo, j * q) - j * q, min(hi, (j + 1) * q) - j * q) for j in range(g.shape[0])]
    return _laid_end_to_end([g[j, a:b] for j, a, b in cuts if a < b], rows or hi - lo)


def _blocks_of_rows(parts, blocks):
    q = sum(p.shape[0] for p in parts) // blocks
    out = []
    for j in range(blocks):
        pieces, start = [], 0
        for p in parts:
            a, b = max(j * q, start), min((j + 1) * q, start + p.shape[0])
            if a < b:
                pieces.append(p[a - start:b - start])
            start += p.shape[0]
        out.append(_laid_end_to_end(pieces, q))
    return jnp.stack(out)


def kernel(x, mem, ffn1_pre_g, ffn1_w_in, ffn1_w_out, ffn1_post_g, mix_pre_g, w_in, w_fu, b_f, gla_norm_g, w_pool, pool_scale, mem_norm_g, w_mem_kv, w_up_gla, w_up_pool, w_up_xattn, w_o, mix_post_g, ffn2_pre_g, ffn2_w_in, ffn2_w_out, ffn2_post_g, final_g, loss_target, m_ffn1_pre_g, m_ffn1_w_in, m_ffn1_w_out, m_ffn1_post_g, m_mix_pre_g, m_w_in, m_w_fu, m_b_f, m_gla_norm_g, m_w_pool, m_pool_scale, m_mem_norm_g, m_w_mem_kv, m_w_up_gla, m_w_up_pool, m_w_up_xattn, m_w_o, m_mix_post_g, m_ffn2_pre_g, m_ffn2_w_in, m_ffn2_w_out, m_ffn2_post_g, m_final_g, v_ffn1_pre_g, v_ffn1_w_in, v_ffn1_w_out, v_ffn1_post_g, v_mix_pre_g, v_w_in, v_w_fu, v_b_f, v_gla_norm_g, v_w_pool, v_pool_scale, v_mem_norm_g, v_w_mem_kv, v_w_up_gla, v_w_up_pool, v_w_up_xattn, v_w_o, v_mix_post_g, v_ffn2_pre_g, v_ffn2_w_in, v_ffn2_w_out, v_ffn2_post_g, v_final_g):
    args = dict(locals())
    w = {n: args[n][0] for n in WEIGHTS}
    m = {n: args["m_" + n][0] for n in WEIGHTS}
    v = {n: args["v_" + n][0] for n in WEIGHTS}
    xi, yi, ci = lax.axis_index("x"), lax.axis_index("y"), lax.axis_index("c")
    chip = 2 * xi + yi

    c_arr = jnp.reshape(ci, (1,)).astype(jnp.int32)
    chip_arr = jnp.reshape(chip, (1,)).astype(jnp.int32)
    place_arr = jnp.stack([chip, ci]).astype(jnp.int32)
    w_in_t = []
    shard_of = {n: args[n] for n in BIG if n != "w_in"}
    shard_of["w_fu"] = args["w_fu"]
    placed, inflight = {}, {}

    def place(names, after):
        for n in names:
            if n not in placed:
                placed[n] = _place_shard(shard_of[n], chip_arr, F32 if n == "w_fu" else BF16, name="place_" + n, after=after)

    def relayout(names, gathered):
        out = {}
        for n, g in zip(names, gathered):
            if n == "w_fu":
                w_fu_full = _cols_from_shards(g)
                out["w_fu_pad"] = jnp.concatenate([w_fu_full, jnp.zeros((LANE - GATE_RANK, 512), F32)], axis=0).astype(BF16)
            elif n == "w_in":
                out["w_gla_t"] = _rows_of_blocks(g, IN_GLA, IN_PX, rows=PG_W)
                out["w_px_t"] = _rows_of_blocks(g, IN_PX, IN_GATES)
                out["w_p_t"] = _rows_of_blocks(g, IN_PX, IN_PX + 512)
                out["w_xq_t"] = _rows_of_blocks(g, IN_PX + 512, IN_GATES)
                out["w_gates_t"] = _rows_of_blocks(g, IN_GATES, IN_END)
            else:
                out[n] = _cols_from_shards(g) if n in COL_SHARDED else g.reshape(4 * g.shape[1], g.shape[2])
        return out

    def gather(op, group, after):
        names = GATHER_GROUPS[group]
        if op == "start":
            place(names, ())
            inflight[group] = _gather_start([placed[n] for n in names], after, name="gather_" + group + "_start")
            behind = (inflight[group][3],)
            if group == "ffn1o":
                tied = lax.optimization_barrier((behind, tuple(args[k] for k in ("w_in", "m_w_in", "v_w_in"))))[1]
                w_in_t.extend(jnp.transpose(a[0]) for a in tied)
                shard_of["w_in"] = w_in_t[0][None]
                place(shard_of, behind)
            return behind
        if op == "pass":
            if group == "ffn1i":
                not_started = [n for g in GATHER_GROUPS if g not in inflight for n in GATHER_GROUPS[g]]
                after = tuple(after) + tuple(w_in_t[1:]) + tuple(placed[n] for n in not_started)
            send, recv, bufs, _ = inflight[group]
            inflight[group] = _gather_pass(bufs, send, recv, after, name="gather_" + group + "_pass")
            return (inflight[group][2][0],)
        send, recv, bufs = inflight.pop(group)
        return relayout(names, _gather_finish(bufs, send, recv, after, name="gather_" + group + "_finish"))

    small = {n: w[n].reshape(1, D) for n in GAINS}
    small["b_f"] = w["b_f"].reshape(1, 512)
    small["pool_scale"] = w["pool_scale"].reshape(1, 512)
    small["w_pool_b"] = w["w_pool"].astype(BF16)

    pending, crossing, travelling = {}, {}, {}

    def emit(name, grad):
        pending[name] = grad
        group = next((g for g, names in REDUCE_GROUPS.items() if name == names[-1]), None)
        if group is None:
            return ()
        gb = {n: pending.pop(n) for n in REDUCE_GROUPS[group]}
        if group == "mix":
            gb["w_in"] = _blocks_of_rows([gb.pop("w_gla")[0:IN_PX], gb.pop("w_p"), gb.pop("w_xq"), gb.pop("w_gates")], 4)
        names = list(gb)
        contrib = [gb[n] if n in COL_SHARDED else gb[n].reshape(4, gb[n].shape[0] // 4, gb[n].shape[1]) for n in names]
        send, recv, contrib, lands, token = _pair_exchange_start(contrib, (), name="grads_" + group + "_pair_start")
        if group == REDUCE_LAST:
            behind = finish_chips(REDUCE_FIRST, (token,))
            contrib, from_sibling = _pair_exchange_finish(contrib, lands, send, recv, behind, name="grads_" + group + "_pair_finish")
            return over_chips(group, names, contrib, from_sibling)
        crossing[group] = (names, contrib, lands, send, recv)
        return (token,)

    def over_chips(group, names, contrib, from_sibling):
        pair = [_pair_sum(g, got, c_arr, name="grads_pair_sum_" + n) for n, g, got in zip(names, contrib, from_sibling)]
        send, recv, pair, lands, token = _chip_exchange_start(pair, (), name="grads_" + group + "_chip_start")
        travelling[group] = (names, send, recv, pair, lands)
        return (token,)

    def advance(after):
        behind = ()
        for group in list(crossing):
            names, contrib, lands, send, recv = crossing.pop(group)
            contrib, from_sibling = _pair_exchange_finish(contrib, lands, send, recv, after, name="grads_" + group + "_pair_finish")
            behind = over_chips(group, names, contrib, from_sibling)
        return behind

    halves = {}

    def finish_chips(group, after):
        names, send, recv, pair, lands = travelling.pop(group)
        pair, from_chips = _chip_exchange_finish(pair, lands, send, recv, after, name="grads_" + group + "_chip_finish")
        for n, p, got in zip(names, pair, from_chips):
            halves[n] = _chip_sum(p, got, place_arr, name="grads_chip_sum_" + n)
        return (halves[names[-1]],)

    loss, grad_x, gs = _local_step(x[0], mem[0], loss_target[0], small, gather, emit, advance)

    for group in list(travelling):
        finish_chips(group, (grad_x,))
    send, recv, joining, token = _pair_join_start([halves[n] for n in BIG], name="grads_pair_join_start")
    small_sums = _all_sum_small(gs, loss, name="sum_small_grads", after=(token,))
    loss = small_sums[1][LOSS_ROW, 0]
    reduced = dict(zip(BIG, _pair_join_finish(joining, send, recv, (small_sums[0],), name="grads_pair_join_finish")))

    grads, delta, new_m, new_v = {}, {}, {}, {}
    for n in BIG:
        if n == "w_in":
            updated = _adamw(w_in_t[0], reduced[n], w_in_t[1], w_in_t[2], name="adamw_" + n)
            grads[n] = jnp.transpose(reduced[n])[None]
            delta[n], new_m[n], new_v[n] = (jnp.transpose(a)[None] for a in updated)
            continue
        grads[n] = reduced[n][None]
        delta[n], new_m[n], new_v[n] = _adamw(args[n], reduced[n], args["m_" + n], args["v_" + n], name="adamw_" + n)
    small_params = {n: (args[n], args["m_" + n], args["v_" + n]) for n in SMALL}
    for n, (g, d, mn, vn) in _adamw_small(small_sums, small_params, chip_arr, name="adamw_small").items():
        grads[n], delta[n], new_m[n], new_v[n] = g, d, mn, vn

    outs = [loss, grad_x[None]]
    for group in (grads, delta, new_m, new_v):
        outs += [group[n] for n in WEIGHTS]
    return tuple(outs)
```

```python
import jax
import jax.numpy as jnp
from jax import lax
from jax.experimental import pallas as pl
from jax.experimental.pallas import tpu as pltpu

F32 = jnp.float32
BF16 = jnp.bfloat16
MESH = pl.DeviceIdType.MESH

D = 1024
DFF = 2816
CHUNK = 64
HEADS = 4
HDK = 128
HDV = 256
GATE_TEMP = 16.0
POOL_WINDOWS = (2, 4, 8, 16)
POOL_HALO = 16
XA_HEADS = 4
XA_HD = 128
EPS = 1e-6
Q_SCALE = HDK ** -0.5
XA_SCALE = XA_HD ** -0.5
PG_Q, PG_K, PG_V, PG_G, PG_F, PG_W = 0, 512, 1024, 2048, 3072, 3200
GATE_RANK = 16
ADAM_LR, ADAM_B1, ADAM_B2, ADAM_EPS, ADAM_WD, ADAM_STEP = 0.001, 0.9, 0.999, 1e-08, 0.01, 10

VMEM_LIMIT = 48 * 1024 * 1024
LANE = 128
TS_ROW = 512
TS_GLA = 512
TS_POOL = 512
TS_XA = 512


def _params(sem):
    return pltpu.CompilerParams(dimension_semantics=sem, vmem_limit_bytes=VMEM_LIMIT)


def _tile(n, cap, unit=LANE):
    if n <= cap:
        return n
    best = None
    for t in range(unit, cap + 1, unit):
        if n % t == 0:
            best = t
    assert best is not None, (n, cap)
    return best


def _sigmoid(x):
    return 0.5 * jnp.tanh(0.5 * x) + 0.5


def _log_sigmoid(x):
    return jnp.minimum(x, 0.0) - jnp.log(1.0 + jnp.exp(-jnp.abs(x)))


def _rms(x):
    r = lax.rsqrt(jnp.mean(x * x, axis=-1, keepdims=True) + EPS)
    return x * r, r


def _rows(ts, w):
    return pl.BlockSpec((ts, w), lambda i: (i, 0))


def _fixed(shape):
    nd = len(shape)
    return pl.BlockSpec(shape, lambda i: (0,) * nd)


def _mm(a, b, *, ta=False, tb=False, out_dtype=F32, tm=2048, tn=1024, tk=1024, shards=1, after=(), name):
    b_blocked = b.ndim == 3
    assert not (b_blocked and tb)
    m, kdim = (a.shape[1], a.shape[0]) if ta else a.shape
    if b_blocked:
        n, tn = b.shape[0] * b.shape[2], b.shape[2]
        assert b.shape[1] == kdim and shards in (1, b.shape[0])
    else:
        n = b.shape[0] if tb else b.shape[1]
        assert (b.shape[1] if tb else b.shape[0]) == kdim, (a.shape, b.shape, ta, tb)
        tn = n // shards if shards > 1 else _tile(n, tn)
    tm = _tile(m, tm)
    tk = _tile(kdim, tk)
    nk = kdim // tk
    dims = (((0 if ta else 1,), (1 if tb else 0,)), ((), ()))

    def body(a_ref, b_ref, *rest):
        o_ref, *acc = rest[len(after):]
        part = lax.dot_general(a_ref[...], b_ref[...], dims, preferred_element_type=F32)
        if nk == 1:
            o_ref[...] = part.astype(o_ref.dtype)
            return
        acc_ref, = acc
        k = pl.program_id(2)

        @pl.when(k == 0)
        def _():
            acc_ref[...] = part

        @pl.when(k > 0)
        def _():
            acc_ref[...] += part

        @pl.when(k == nk - 1)
        def _():
            o_ref[...] = acc_ref[...].astype(o_ref.dtype)

    a_spec = pl.BlockSpec((tk, tm), lambda i, j, k: (k, i)) if ta else pl.BlockSpec((tm, tk), lambda i, j, k: (i, k))
    if b_blocked:
        b_spec = pl.BlockSpec((None, tk, tn), lambda i, j, k: (j, k, 0))
    else:
        b_spec = pl.BlockSpec((tn, tk), lambda i, j, k: (j, k)) if tb else pl.BlockSpec((tk, tn), lambda i, j, k: (k, j))
    if shards > 1:
        out_shape = jax.ShapeDtypeStruct((shards, m, tn), out_dtype)
        o_spec = pl.BlockSpec((None, tm, tn), lambda i, j, k: (j, i, 0))
    else:
        out_shape = jax.ShapeDtypeStruct((m, n), out_dtype)
        o_spec = pl.BlockSpec((tm, tn), lambda i, j, k: (i, j))
    return pl.pallas_call(
        body, grid=(m // tm, n // tn, nk), in_specs=[a_spec, b_spec] + [ANY] * len(after), out_specs=o_spec, out_shape=out_shape,
        scratch_shapes=[pltpu.VMEM((tm, tn), F32)] if nk > 1 else [],
        compiler_params=_params(("parallel", "parallel", "arbitrary")), name=name,
    )(a, b, *after)


def _norm_fwd(x, g, out_dtype, name, after=()):
    s, d = x.shape
    ts = _tile(s, TS_ROW, 8)

    def body(x_ref, g_ref, *rest):
        o_ref = rest[len(after)]
        xh, _ = _rms(x_ref[...])
        o_ref[...] = (xh * g_ref[...]).astype(o_ref.dtype)

    return pl.pallas_call(
        body, grid=(s // ts,), in_specs=[_rows(ts, d), _fixed((1, d))] + [ANY] * len(after), out_specs=_rows(ts, d),
        out_shape=jax.ShapeDtypeStruct((s, d), out_dtype), compiler_params=_params(("parallel",)), name=name,
    )(x, g, *after)


def _mm_resid_norm(a, w, x, g_post, alpha, g_next, name, after=(), tm=512):
    s, kdim = a.shape
    d = w.shape[1]
    tm = _tile(s, tm)
    with_h = g_next is not None
    na = len(after)

    def body(a_ref, w_ref, x_ref, gp_ref, *rest):
        rest = rest[int(with_h) + na:] if not with_h else rest[:1] + rest[1 + na:]
        for rows in _sub_blocks(tm):
            f = jnp.dot(a_ref[rows, :], w_ref[...], preferred_element_type=F32)
            fh, _ = _rms(f)
            xn = x_ref[rows, :] + alpha * (fh * gp_ref[...])
            if with_h:
                gn_ref, f_ref, xo_ref, h_ref = rest
                xh, _ = _rms(xn)
                h_ref[rows, :] = (xh * gn_ref[...]).astype(h_ref.dtype)
            else:
                f_ref, xo_ref = rest
            f_ref[rows, :] = f
            xo_ref[rows, :] = xn

    ins = [a, w, x, g_post] + ([g_next] if with_h else []) + list(after)
    in_specs = [_rows(tm, kdim), _fixed((kdim, d)), _rows(tm, d), _fixed((1, d))] + ([_fixed((1, d))] if with_h else []) + [ANY] * na
    out_shape = [jax.ShapeDtypeStruct((s, d), F32)] * 2 + ([jax.ShapeDtypeStruct((s, d), BF16)] if with_h else [])
    out = pl.pallas_call(
        body, grid=(s // tm,), in_specs=in_specs, out_specs=[_rows(tm, d)] * len(out_shape), out_shape=out_shape,
        compiler_params=_params(("parallel",)), name=name,
    )(*ins)
    return (out[0], out[1], out[2]) if with_h else (out[0], out[1], None)


def _mm_rms_bwd(pairs, x, g, dres, name, after=(), post=None, tm=512):
    s, d = x.shape
    tm = _tile(s, tm)
    n, na = len(pairs), len(after)

    def body(*refs):
        a_refs, w_refs = refs[0:2 * n:2], refs[1:2 * n:2]
        x_ref, g_ref, dres_ref = refs[2 * n:2 * n + 3]
        if post is not None:
            f_ref, gp_ref = refs[2 * n + 3:2 * n + 5]
            dx_ref, dg_ref, df_ref, dgp_ref = refs[2 * n + 5 + na:]
        else:
            dx_ref, dg_ref = refs[2 * n + 3 + na:]
        @pl.when(pl.program_id(0) == 0)
        def _():
            dg_ref[...] = jnp.zeros_like(dg_ref)
            if post is not None:
                dgp_ref[...] = jnp.zeros_like(dgp_ref)

        for rows in _sub_blocks(tm):
            dy = None
            for a_ref, w_ref in zip(a_refs, w_refs):
                if len(a_ref.shape) == 3:
                    tkb = a_ref.shape[2]
                    parts = [lax.dot_general(a_ref[q, rows, :], w_ref[:, q * tkb:(q + 1) * tkb], (((1,), (1,)), ((), ())),
                                             preferred_element_type=F32) for q in range(a_ref.shape[0])]
                else:
                    parts = [jnp.dot(a_ref[rows, :], w_ref[...], preferred_element_type=F32)]
                for part in parts:
                    dy = part if dy is None else dy + part
            xh, r = _rms(x_ref[rows, :])
            dg_ref[...] += jnp.sum(dy * xh, axis=0, keepdims=True)
            dyg = dy * g_ref[...]
            dx = r * (dyg - xh * jnp.mean(dyg * xh, axis=-1, keepdims=True)) + dres_ref[rows, :]
            dx_ref[rows, :] = dx
            if post is not None:
                fh, rf = _rms(f_ref[rows, :])
                dz = dx * post[2]
                dgp_ref[...] += jnp.sum(dz * fh, axis=0, keepdims=True)
                dzg = dz * gp_ref[...]
                df_ref[rows, :] = (rf * (dzg - fh * jnp.mean(dzg * fh, axis=-1, keepdims=True))).astype(df_ref.dtype)

    ins, in_specs = [], []
    for a_arr, w_arr in pairs:
        ins += [a_arr, w_arr]
        if a_arr.ndim == 3:
            in_specs.append(pl.BlockSpec((a_arr.shape[0], tm, a_arr.shape[2]), lambda i: (0, i, 0)))
        else:
            in_specs.append(_rows(tm, a_arr.shape[1]))
        in_specs.append(pl.BlockSpec(w_arr.shape, lambda i: (0, 0), pipeline_mode=pl.Buffered(1)))
    with_post = post is not None
    return pl.pallas_call(
        body, grid=(s // tm,),
        in_specs=in_specs + [_rows(tm, d), _fixed((1, d)), _rows(tm, d)] + ([_rows(tm, d), _fixed((1, d))] if with_post else [])
        + [ANY] * na,
        out_specs=[_rows(tm, d), _fixed((1, d))] + ([_rows(tm, d), _fixed((1, d))] if with_post else []),
        out_shape=[jax.ShapeDtypeStruct((s, d), F32), jax.ShapeDtypeStruct((1, d), F32)]
        + ([jax.ShapeDtypeStruct((s, d), BF16), jax.ShapeDtypeStruct((1, d), F32)] if with_post else []),
        compiler_params=_params(("arbitrary",)), name=name,
    )(*ins, x, g, dres, *(post[:2] if with_post else ()), *after)


def _ffn_out_loss(u, w_out, x, g_post, alpha, g_final, target, name, tm=512):
    s, kdim = u.shape
    d = w_out.shape[1]
    tm = _tile(s, tm)

    def body(u_ref, w_ref, x_ref, gp_ref, gf_ref, t_ref, df_ref, dx_ref, dgp_ref, dgf_ref, loss_ref):
        @pl.when(pl.program_id(0) == 0)
        def _():
            dgp_ref[...] = jnp.zeros_like(dgp_ref)
            dgf_ref[...] = jnp.zeros_like(dgf_ref)
            loss_ref[...] = jnp.zeros_like(loss_ref)

        for rows in _sub_blocks(tm):
            f = jnp.dot(u_ref[rows, :], w_ref[...], preferred_element_type=F32)
            fh, rf = _rms(f)
            xn = x_ref[rows, :] + alpha * (fh * gp_ref[...])
            xh, rx = _rms(xn)
            gf = gf_ref[...]
            diff = xh * gf - t_ref[rows, :]
            sq = jnp.sum(diff * diff, axis=1, keepdims=True)
            loss_ref[...] += (0.5 / d) * jnp.sum(sq, axis=0, keepdims=True)
            dy = diff * (1.0 / d)
            dgf_ref[...] += jnp.sum(dy * xh, axis=0, keepdims=True)
            dyg = dy * gf
            dxn = rx * (dyg - xh * jnp.mean(dyg * xh, axis=-1, keepdims=True))
            dx_ref[rows, :] = dxn
            dz = dxn * alpha
            dgp_ref[...] += jnp.sum(dz * fh, axis=0, keepdims=True)
            dzg = dz * gp_ref[...]
            df_ref[rows, :] = (rf * (dzg - fh * jnp.mean(dzg * fh, axis=-1, keepdims=True))).astype(df_ref.dtype)

    return pl.pallas_call(
        body, grid=(s // tm,),
        in_specs=[_rows(tm, kdim), _resident(w_out.shape), _rows(tm, d), _fixed((1, d)), _fixed((1, d)), _rows(tm, d)],
        out_specs=[_rows(tm, d), _rows(tm, d), _fixed((1, d)), _fixed((1, d)), _fixed((8, LANE))],
        out_shape=[jax.ShapeDtypeStruct((s, d), BF16), jax.ShapeDtypeStruct((s, d), F32), jax.ShapeDtypeStruct((1, d), F32),
                   jax.ShapeDtypeStruct((1, d), F32), jax.ShapeDtypeStruct((8, LANE), F32)],
        compiler_params=_params(("arbitrary",)), name=name,
    )(u, w_out, x, g_post, g_final, target)


def _rms_bwd(x, g, dys, dres, alpha, out_dtype, name, after=()):
    s, d = x.shape
    ts = _tile(s, TS_ROW, 8)
    ndy = len(dys)
    with_res = dres is not None

    def body(x_ref, g_ref, *rest):
        dy_refs = rest[:ndy]
        rest = rest[ndy:]
        if with_res:
            dres_ref = rest[0]
        dx_ref, dg_ref = rest[int(with_res) + len(after):]
        xh, r = _rms(x_ref[...])
        dy = dy_refs[0][...].astype(F32)
        for ref in dy_refs[1:]:
            dy = dy + ref[...].astype(F32)
        dy = dy * alpha

        @pl.when(pl.program_id(0) == 0)
        def _():
            dg_ref[...] = jnp.zeros_like(dg_ref)

        dg_ref[...] += jnp.sum(dy * xh, axis=0, keepdims=True)
        dyg = dy * g_ref[...]
        dx = r * (dyg - xh * jnp.mean(dyg * xh, axis=-1, keepdims=True))
        if with_res:
            dx = dx + dres_ref[...]
        dx_ref[...] = dx.astype(dx_ref.dtype)

    ins = [x, g] + list(dys) + ([dres] if with_res else []) + list(after)
    in_specs = [_rows(ts, d), _fixed((1, d))] + [_rows(ts, d)] * (ndy + int(with_res)) + [ANY] * len(after)
    return pl.pallas_call(
        body, grid=(s // ts,), in_specs=in_specs, out_specs=[_rows(ts, d), _fixed((1, d))],
        out_shape=[jax.ShapeDtypeStruct((s, d), out_dtype), jax.ShapeDtypeStruct((1, d), F32)],
        compiler_params=_params(("arbitrary",)), name=name,
    )(*ins)


HALF_FF = DFF // 2


SUB_ROWS = 256


def _sub_blocks(tm):
    sub = SUB_ROWS if tm % SUB_ROWS == 0 else tm
    return [slice(r0, r0 + sub) for r0 in range(0, tm, sub)]


def _ffn_in_swiglu(x_norm, w_in, name, after=(), tm=1024):
    s, d = x_norm.shape
    tm = _tile(s, tm)

    def body(x_ref, wa_ref, wb_ref, *rest):
        ab_ref, u_ref = rest[len(after):]
        for rows in _sub_blocks(tm):
            xv = x_ref[rows, :]
            a = jnp.dot(xv, wa_ref[...], preferred_element_type=F32)
            b = jnp.dot(xv, wb_ref[...], preferred_element_type=F32)
            ab_ref[0, rows, :] = a.astype(ab_ref.dtype)
            ab_ref[1, rows, :] = b.astype(ab_ref.dtype)
            u_ref[rows, :] = (a * _sigmoid(a) * b).astype(u_ref.dtype)

    ab, u = pl.pallas_call(
        body, grid=(2, s // tm),
        in_specs=[pl.BlockSpec((tm, d), lambda j, i: (i, 0)), pl.BlockSpec((d, HALF_FF), lambda j, i: (0, j)),
                  pl.BlockSpec((d, HALF_FF), lambda j, i: (0, 2 + j))] + [ANY] * len(after),
        out_specs=[pl.BlockSpec((2, None, tm, HALF_FF), lambda j, i: (0, j, i, 0)), pl.BlockSpec((tm, HALF_FF), lambda j, i: (i, j))],
        out_shape=[jax.ShapeDtypeStruct((2, 2, s, HALF_FF), BF16), jax.ShapeDtypeStruct((s, DFF), BF16)],
        compiler_params=_params(("parallel", "parallel")), name=name,
    )(x_norm, w_in, w_in, *after)
    return ab.reshape(4, s, HALF_FF), u


def _ffn_out_dx_swiglu(dz, w_out, ab, after, name, tm=1024):
    s, d = dz.shape
    tm = _tile(s, tm)

    def body(dz_ref, w_ref, ab_ref, *rest):
        dab_ref = rest[len(after)]
        for rows in _sub_blocks(tm):
            du = lax.dot_general(dz_ref[rows, :], w_ref[...], (((1,), (1,)), ((), ())), preferred_element_type=F32)
            a = ab_ref[0, rows, :].astype(F32)
            b = ab_ref[1, rows, :].astype(F32)
            sig = _sigmoid(a)
            dab_ref[0, rows, :] = (du * b * (sig * (1.0 + a * (1.0 - sig)))).astype(dab_ref.dtype)
            dab_ref[1, rows, :] = (du * a * sig).astype(dab_ref.dtype)

    halves = pl.BlockSpec((2, None, tm, HALF_FF), lambda j, i: (0, j, i, 0))
    dab = pl.pallas_call(
        body, grid=(2, s // tm),
        in_specs=[pl.BlockSpec((tm, d), lambda j, i: (i, 0)), pl.BlockSpec((HALF_FF, d), lambda j, i: (j, 0)), halves] + [ANY] * len(after),
        out_specs=halves, out_shape=jax.ShapeDtypeStruct((2, 2, s, HALF_FF), BF16),
        compiler_params=_params(("parallel", "parallel")), name=name,
    )(dz, w_out, ab.reshape(2, 2, s, HALF_FF), *after)
    return dab.reshape(4, s, HALF_FF)


def _tri(strict):
    r = lax.broadcasted_iota(jnp.int32, (CHUNK, CHUNK), 0)
    c = lax.broadcasted_iota(jnp.int32, (CHUNK, CHUNK), 1)
    return (r > c).astype(F32) if strict else (r >= c).astype(F32)


def _tri_dot(tri, x):
    t = tri.astype(BF16)
    hi = x.astype(BF16)
    rest = x - hi.astype(F32)
    mid = rest.astype(BF16)
    lo = (rest - mid.astype(F32)).astype(BF16)
    return (jnp.dot(t, hi, preferred_element_type=F32) + jnp.dot(t, mid, preferred_element_type=F32)
            + jnp.dot(t, lo, preferred_element_type=F32))


def _gla_fwd(pg, wfu, b_f, gnorm, name):
    s = pg.shape[0]
    ts = _tile(s, TS_GLA, CHUNK)
    cpb = ts // CHUNK
    nc = s // CHUNK

    def body(pg_ref, wfu_ref, bf_ref, gn_ref, ya_ref, sp_ref, so_ref, o_ref, st_ref, la_ref, dec_ref, u_ref):
        @pl.when(pl.program_id(0) == 0)
        def _():
            st_ref[...] = jnp.zeros_like(st_ref)

        f = jnp.dot(pg_ref[:, PG_F:PG_W], wfu_ref[...], preferred_element_type=F32) + bf_ref[...]
        la_ref[...] = _log_sigmoid(f) * (1.0 / GATE_TEMP)
        tri = _tri(False)
        chunks = [slice(ci * CHUNK, (ci + 1) * CHUNK) for ci in range(cpb)]
        for ci, rows in enumerate(chunks):
            la = la_ref[rows, :]
            b = _tri_dot(tri, la)
            bend = jnp.sum(la, axis=0, keepdims=True)
            e = jnp.exp(bend - b)
            dec_ref[ci:ci + 1, :] = jnp.exp(bend)
            for hd in range(HEADS):
                k = pg_ref[rows, PG_K + hd * HDK:PG_K + (hd + 1) * HDK]
                v = pg_ref[rows, PG_V + hd * HDV:PG_V + (hd + 1) * HDV]
                kt = (k.astype(F32) * e[:, hd * HDK:(hd + 1) * HDK]).astype(BF16)
                u_ref[ci, hd] = lax.dot_general(v, kt, (((0,), (0,)), ((), ())), preferred_element_type=F32)
        for ci in range(cpb):
            for hd in range(HEADS):
                prev = st_ref[hd]
                sp_ref[ci, hd] = prev
                st = prev * dec_ref[ci:ci + 1, hd * HDK:(hd + 1) * HDK] + u_ref[ci, hd]
                st_ref[hd] = st
                so_ref[ci, hd] = st.astype(so_ref.dtype)
        for ci, rows in enumerate(chunks):
            for hd in range(HEADS):
                vc = slice(hd * HDV, (hd + 1) * HDV)
                q = pg_ref[rows, PG_Q + hd * HDK:PG_Q + (hd + 1) * HDK]
                go = pg_ref[rows, PG_G + hd * HDV:PG_G + (hd + 1) * HDV].astype(F32)
                qs = (q.astype(F32) * Q_SCALE).astype(BF16)
                o = lax.dot_general(qs, so_ref[ci, hd], (((1,), (1,)), ((), ())), preferred_element_type=F32)
                o_ref[rows, vc] = o
                oh, _ = _rms(o)
                ya_ref[rows, vc] = (oh * gn_ref[:, vc] * (go * _sigmoid(go))).astype(ya_ref.dtype)

    return pl.pallas_call(
        body, grid=(s // ts,),
        in_specs=[_rows(ts, PG_W), _fixed((LANE, HEADS * HDK)), _fixed((1, HEADS * HDK)), _fixed((1, HEADS * HDV))],
        out_specs=[_rows(ts, HEADS * HDV), pl.BlockSpec((cpb, HEADS, HDV, HDK), lambda i: (i, 0, 0, 0)),
                   pl.BlockSpec((cpb, HEADS, HDV, HDK), lambda i: (i, 0, 0, 0)), _rows(ts, HEADS * HDV)],
        out_shape=[jax.ShapeDtypeStruct((s, HEADS * HDV), BF16), jax.ShapeDtypeStruct((nc, HEADS, HDV, HDK), F32),
                   jax.ShapeDtypeStruct((nc, HEADS, HDV, HDK), BF16), jax.ShapeDtypeStruct((s, HEADS * HDV), F32)],
        scratch_shapes=[pltpu.VMEM((HEADS, HDV, HDK), F32), pltpu.VMEM((ts, HEADS * HDK), F32),
                        pltpu.VMEM((max(cpb, 8), HEADS * HDK), F32), pltpu.VMEM((cpb, HEADS, HDV, HDK), F32)],
        compiler_params=_params(("arbitrary",)), name=name,
    )(pg, wfu, b_f, gnorm)


def _gla_bwd(pg, sp, so, o, dya, wfu, b_f, gnorm, name):
    s = pg.shape[0]
    ts = _tile(s, TS_GLA, CHUNK)
    cpb = ts // CHUNK
    nblk = s // ts

    def body(pg_ref, sp_ref, so_ref, o_ref, dya_ref, wfu_ref, bf_ref, gn_ref, dpg_ref, dwfu_ref, dbf_ref, dgn_ref,
             dst_ref, la_ref, sg_ref, df_ref, e_ref, ktf_ref, dec_ref, g_ref):
        @pl.when(pl.program_id(0) == 0)
        def _():
            dst_ref[...] = jnp.zeros_like(dst_ref)
            dwfu_ref[...] = jnp.zeros_like(dwfu_ref)
            dbf_ref[...] = jnp.zeros_like(dbf_ref)
            dgn_ref[...] = jnp.zeros_like(dgn_ref)

        flow = pg_ref[:, PG_F:PG_W]
        f = jnp.dot(flow, wfu_ref[...], preferred_element_type=F32) + bf_ref[...]
        la_ref[...] = _log_sigmoid(f) * (1.0 / GATE_TEMP)
        sg_ref[...] = _sigmoid(-f) * (1.0 / GATE_TEMP)
        tri = _tri(False)
        tri_strict = _tri(True)
        chunks = [slice(ci * CHUNK, (ci + 1) * CHUNK) for ci in range(cpb)]
        for ci, rows in enumerate(chunks):
            la = la_ref[rows, :]
            b = _tri_dot(tri, la)
            bend = jnp.sum(la, axis=0, keepdims=True)
            e = jnp.exp(bend - b)
            e_ref[rows, :] = e
            dec = jnp.exp(bend)
            dec_ref[ci:ci + 1, :] = dec
            for hd in range(HEADS):
                kc = slice(hd * HDK, (hd + 1) * HDK)
                vc = slice(hd * HDV, (hd + 1) * HDV)
                q = pg_ref[rows, PG_Q + hd * HDK:PG_Q + (hd + 1) * HDK]
                k = pg_ref[rows, PG_K + hd * HDK:PG_K + (hd + 1) * HDK]
                go = pg_ref[rows, PG_G + hd * HDV:PG_G + (hd + 1) * HDV].astype(F32)
                ktf_ref[rows, kc] = k.astype(F32) * e[:, kc]
                st_b = so_ref[ci, hd]
                qs = (q.astype(F32) * Q_SCALE).astype(BF16)
                oh, r = _rms(o_ref[rows, vc])
                gh = gn_ref[:, vc]
                sig = _sigmoid(go)
                dy = dya_ref[rows, vc].astype(F32)
                don = dy * (go * sig)
                dgn_ref[:, vc] += jnp.sum(don * oh, axis=0, keepdims=True)
                dong = don * gh
                do = (r * (dong - oh * jnp.mean(dong * oh, axis=-1, keepdims=True))).astype(BF16)
                g_ref[ci, hd] = lax.dot_general(do, qs, (((0,), (0,)), ((), ())), preferred_element_type=F32)
                dq = jnp.dot(do, st_b, preferred_element_type=F32) * Q_SCALE
                dpg_ref[rows, PG_Q + hd * HDK:PG_Q + (hd + 1) * HDK] = dq.astype(dpg_ref.dtype)
                dgo = dy * (oh * gh) * (sig * (1.0 + go * (1.0 - sig)))
                dpg_ref[rows, PG_G + hd * HDV:PG_G + (hd + 1) * HDV] = dgo.astype(dpg_ref.dtype)
        for ci in reversed(range(cpb)):
            for hd in range(HEADS):
                dst = dst_ref[hd] + g_ref[ci, hd]
                g_ref[ci, hd] = dst
                dst_ref[hd] = dst * dec_ref[ci:ci + 1, hd * HDK:(hd + 1) * HDK]
        for ci, rows in enumerate(chunks):
            for hd in range(HEADS):
                kc = slice(hd * HDK, (hd + 1) * HDK)
                v = pg_ref[rows, PG_V + hd * HDV:PG_V + (hd + 1) * HDV]
                ktf = ktf_ref[rows, kc]
                dst = g_ref[ci, hd]
                dst_b = dst.astype(BF16)
                dkt = jnp.dot(v, dst_b, preferred_element_type=F32)
                dv = lax.dot_general(ktf.astype(BF16), dst_b, (((1,), (1,)), ((), ())), preferred_element_type=F32)
                dd = jnp.sum(dst * sp_ref[ci, hd], axis=0, keepdims=True)
                dla = jnp.dot(tri_strict, dkt * ktf, precision=lax.Precision.HIGHEST, preferred_element_type=F32) + dd * dec_ref[ci:ci + 1, kc]
                df_ref[rows, kc] = dla * sg_ref[rows, kc]
                dpg_ref[rows, PG_K + hd * HDK:PG_K + (hd + 1) * HDK] = (dkt * e_ref[rows, kc]).astype(dpg_ref.dtype)
                dpg_ref[rows, PG_V + hd * HDV:PG_V + (hd + 1) * HDV] = dv.astype(dpg_ref.dtype)
        df = df_ref[...]
        df_b = df.astype(BF16)
        dpg_ref[:, PG_F:PG_W] = lax.dot_general(df_b, wfu_ref[...], (((1,), (1,)), ((), ())), preferred_element_type=F32).astype(dpg_ref.dtype)
        dwfu_ref[...] += lax.dot_general(flow, df_b, (((0,), (0,)), ((), ())), preferred_element_type=F32)
        dbf_ref[...] += jnp.sum(df, axis=0, keepdims=True)

    rev = lambda i: (nblk - 1 - i, 0)
    return pl.pallas_call(
        body, grid=(nblk,),
        in_specs=[pl.BlockSpec((ts, PG_W), rev), pl.BlockSpec((cpb, HEADS, HDV, HDK), lambda i: (nblk - 1 - i, 0, 0, 0)),
                  pl.BlockSpec((cpb, HEADS, HDV, HDK), lambda i: (nblk - 1 - i, 0, 0, 0)), pl.BlockSpec((ts, HEADS * HDV), rev),
                  pl.BlockSpec((ts, HEADS * HDV), rev), _fixed((LANE, HEADS * HDK)), _fixed((1, HEADS * HDK)), _fixed((1, HEADS * HDV))],
        out_specs=[pl.BlockSpec((ts, PG_W), rev), _fixed((LANE, HEADS * HDK)), _fixed((1, HEADS * HDK)), _fixed((1, HEADS * HDV))],
        out_shape=[jax.ShapeDtypeStruct((s, PG_W), BF16), jax.ShapeDtypeStruct((LANE, HEADS * HDK), F32),
                   jax.ShapeDtypeStruct((1, HEADS * HDK), F32), jax.ShapeDtypeStruct((1, HEADS * HDV), F32)],
        scratch_shapes=[pltpu.VMEM((HEADS, HDV, HDK), F32)] + [pltpu.VMEM((ts, HEADS * HDK), F32)] * 5
        + [pltpu.VMEM((max(cpb, 8), HEADS * HDK), F32), pltpu.VMEM((cpb, HEADS, HDV, HDK), F32)],
        compiler_params=_params(("arbitrary",)), name=name,
    )(pg, sp, so, o, dya, wfu, b_f, gnorm)


def _window_sums(ext, sign):
    n = ext.shape[0]
    sums = {1: ext}
    w = 1
    while w < POOL_WINDOWS[-1]:
        sums[2 * w] = sums[w] + pltpu.roll(sums[w], w if sign > 0 else n - w, 0)
        w *= 2
    return [sums[POOL_WINDOWS[g]][:, g * LANE:(g + 1) * LANE] for g in range(len(POOL_WINDOWS))]


def _pool_counts(row0, n):
    pos = (row0 + lax.broadcasted_iota(jnp.int32, (n, 1), 0) + 1).astype(F32)
    return [1.0 / jnp.minimum(pos, float(w)) for w in POOL_WINDOWS]


def _pool_fwd(ppx, w_pool, pool_scale, name):
    s = ppx.shape[0]
    ts = _tile(s, TS_POOL, POOL_HALO)
    hb = ts // POOL_HALO
    pw = len(POOL_WINDOWS) * LANE

    def body(p_ref, halo_ref, w_ref, sc_ref, y_ref, ext_ref):
        i = pl.program_id(0)
        p = p_ref[...].astype(F32)
        ext_ref[0:POOL_HALO, :] = jnp.where(i > 0, halo_ref[...].astype(F32), 0.0)
        ext_ref[POOL_HALO:, :] = p
        sums = _window_sums(ext_ref[...], +1)
        cnt = _pool_counts(i * ts, ts)
        for g in range(len(POOL_WINDOWS)):
            cols = slice(g * LANE, (g + 1) * LANE)
            mixed = sums[g][POOL_HALO:, :] * cnt[g] - p[:, cols]
            y = jnp.dot(mixed.astype(BF16), w_ref[g], preferred_element_type=F32)
            y_ref[:, cols] = (y * sc_ref[:, cols]).astype(y_ref.dtype)

    return pl.pallas_call(
        body, grid=(s // ts,),
        in_specs=[pl.BlockSpec((ts, pw), lambda i: (i, 0)), pl.BlockSpec((POOL_HALO, pw), lambda i: (jnp.maximum(i * hb - 1, 0), 0)),
                  _fixed((len(POOL_WINDOWS), LANE, LANE)), _fixed((1, pw))],
        out_specs=_rows(ts, pw), out_shape=jax.ShapeDtypeStruct((s, pw), BF16),
        scratch_shapes=[pltpu.VMEM((ts + POOL_HALO, pw), F32)],
        compiler_params=_params(("parallel",)), name=name,
    )(ppx, ppx, w_pool, pool_scale)


def _pool_bwd(dyb, ppx, w_pool, pool_scale, name):
    s = ppx.shape[0]
    ts = _tile(s, TS_POOL, POOL_HALO)
    hb = ts // POOL_HALO
    nblk = s // ts
    last_halo = s // POOL_HALO - 1
    ng = len(POOL_WINDOWS)
    pw = ng * LANE

    def body(p_ref, halo_ref, dy_ref, dyn_ref, w_ref, sc_ref, dp_ref, dw_ref, dsc_ref, ext_ref, dext_ref, dm_ref):
        i = pl.program_id(0)

        @pl.when(i == 0)
        def _():
            dw_ref[...] = jnp.zeros_like(dw_ref)
            dsc_ref[...] = jnp.zeros_like(dsc_ref)

        p = p_ref[...].astype(F32)
        ext_ref[0:POOL_HALO, :] = jnp.where(i > 0, halo_ref[...].astype(F32), 0.0)
        ext_ref[POOL_HALO:, :] = p
        sums = _window_sums(ext_ref[...], +1)
        cnt = _pool_counts(i * ts, ts + POOL_HALO)
        sc = sc_ref[...]
        dy = dy_ref[...].astype(F32)
        dyn = jnp.where(i < nblk - 1, dyn_ref[...].astype(F32), 0.0)
        for g in range(ng):
            cols = slice(g * LANE, (g + 1) * LANE)
            wg = w_ref[g]
            mixed = (sums[g][POOL_HALO:, :] * cnt[g][0:ts] - p[:, cols]).astype(BF16)
            ypre = jnp.dot(mixed, wg, preferred_element_type=F32)
            dsc_ref[:, cols] += jnp.sum(dy[:, cols] * ypre, axis=0, keepdims=True)
            dyp = (dy[:, cols] * sc[:, cols]).astype(BF16)
            dypn = (dyn[:, cols] * sc[:, cols]).astype(BF16)
            dw_ref[g] += lax.dot_general(mixed, dyp, (((0,), (0,)), ((), ())), preferred_element_type=F32)
            dm = lax.dot_general(dyp, wg, (((1,), (1,)), ((), ())), preferred_element_type=F32)
            dmn = lax.dot_general(dypn, wg, (((1,), (1,)), ((), ())), preferred_element_type=F32)
            dext_ref[0:ts, cols] = dm * cnt[g][0:ts]
            dext_ref[ts:, cols] = dmn * cnt[g][ts:]
            dm_ref[:, cols] = dm
        lead = _window_sums(dext_ref[...], -1)
        for g in range(ng):
            cols = slice(g * LANE, (g + 1) * LANE)
            dp_ref[:, cols] = (lead[g][0:ts, :] - dm_ref[:, cols]).astype(dp_ref.dtype)

    return pl.pallas_call(
        body, grid=(nblk,),
        in_specs=[pl.BlockSpec((ts, pw), lambda i: (i, 0)), pl.BlockSpec((POOL_HALO, pw), lambda i: (jnp.maximum(i * hb - 1, 0), 0)),
                  pl.BlockSpec((ts, pw), lambda i: (i, 0)), pl.BlockSpec((POOL_HALO, pw), lambda i: (jnp.minimum((i + 1) * hb, last_halo), 0)),
                  _fixed((ng, LANE, LANE)), _fixed((1, pw))],
        out_specs=[_rows(ts, pw), _fixed((ng, LANE, LANE)), _fixed((1, pw))],
        out_shape=[jax.ShapeDtypeStruct((s, pw), BF16), jax.ShapeDtypeStruct((ng, LANE, LANE), F32), jax.ShapeDtypeStruct((1, pw), F32)],
        scratch_shapes=[pltpu.VMEM((ts + POOL_HALO, pw), F32), pltpu.VMEM((ts + POOL_HALO, pw), F32), pltpu.VMEM((ts, pw), F32)],
        compiler_params=_params(("arbitrary",)), name=name,
    )(ppx, ppx, dyb, dyb, w_pool, pool_scale)


def _xattn_fwd(ppx, kv, name):
    s = ppx.shape[0]
    m = kv.shape[0]
    ts = _tile(s, TS_XA, 8)
    xw = XA_HEADS * XA_HD

    def body(q_ref, kv_ref, o_ref):
        for hd in range(XA_HEADS):
            cols = slice(hd * XA_HD, (hd + 1) * XA_HD)
            k = kv_ref[:, hd * XA_HD:(hd + 1) * XA_HD]
            v = kv_ref[:, xw + hd * XA_HD:xw + (hd + 1) * XA_HD]
            sc = lax.dot_general(q_ref[:, cols], k, (((1,), (1,)), ((), ())), preferred_element_type=F32) * XA_SCALE
            ex = jnp.exp(sc - jnp.max(sc, axis=-1, keepdims=True))
            pr = ex * (1.0 / jnp.sum(ex, axis=-1, keepdims=True))
            o_ref[:, cols] = jnp.dot(pr.astype(BF16), v, preferred_element_type=F32).astype(o_ref.dtype)

    return pl.pallas_call(
        body, grid=(s // ts,), in_specs=[pl.BlockSpec((ts, xw), lambda i: (i, 1)), _fixed((m, 2 * xw))],
        out_specs=_rows(ts, xw), out_shape=jax.ShapeDtypeStruct((s, xw), BF16),
        compiler_params=_params(("parallel",)), name=name,
    )(ppx, kv)


def _xattn_bwd(dxc, ppx, kv, name):
    s = ppx.shape[0]
    m = kv.shape[0]
    ts = _tile(s, TS_XA, 8)
    xw = XA_HEADS * XA_HD

    def body(do_ref, q_ref, kv_ref, dq_ref, dkv_ref):
        @pl.when(pl.program_id(0) == 0)
        def _():
            dkv_ref[...] = jnp.zeros_like(dkv_ref)

        for hd in range(XA_HEADS):
            cols = slice(hd * XA_HD, (hd + 1) * XA_HD)
            vcols = slice(xw + hd * XA_HD, xw + (hd + 1) * XA_HD)
            q = q_ref[:, cols]
            k = kv_ref[:, cols]
            v = kv_ref[:, vcols]
            do = do_ref[:, cols]
            sc = lax.dot_general(q, k, (((1,), (1,)), ((), ())), preferred_element_type=F32) * XA_SCALE
            ex = jnp.exp(sc - jnp.max(sc, axis=-1, keepdims=True))
            pr = ex * (1.0 / jnp.sum(ex, axis=-1, keepdims=True))
            dpr = lax.dot_general(do, v, (((1,), (1,)), ((), ())), preferred_element_type=F32)
            dsc = (pr * (dpr - jnp.sum(dpr * pr, axis=-1, keepdims=True)) * XA_SCALE).astype(BF16)
            dq_ref[:, cols] = jnp.dot(dsc, k, preferred_element_type=F32).astype(dq_ref.dtype)
            dkv_ref[:, cols] += lax.dot_general(dsc, q, (((0,), (0,)), ((), ())), preferred_element_type=F32)
            dkv_ref[:, vcols] += lax.dot_general(pr.astype(BF16), do, (((0,), (0,)), ((), ())), preferred_element_type=F32)

    return pl.pallas_call(
        body, grid=(s // ts,), in_specs=[_rows(ts, xw), pl.BlockSpec((ts, xw), lambda i: (i, 1)), _fixed((m, 2 * xw))],
        out_specs=[_rows(ts, xw), _fixed((m, 2 * xw))],
        out_shape=[jax.ShapeDtypeStruct((s, xw), BF16), jax.ShapeDtypeStruct((m, 2 * xw), F32)],
        compiler_params=_params(("arbitrary",)), name=name,
    )(dxc, ppx, kv)


def _resident(shape):
    nd = len(shape)
    return pl.BlockSpec(shape, lambda i: (0,) * nd, pipeline_mode=pl.Buffered(1))


def _mix_in_fwd(h, w_ts, after, name, tm=512):
    s, d = h.shape
    tm = _tile(s, tm)
    n, na = len(w_ts), len(after)

    def body(h_ref, *refs):
        w_refs, o_refs = refs[:n], refs[n + na:]
        for rows in _sub_blocks(tm):
            hv = h_ref[rows, :]
            for w_ref, o_ref in zip(w_refs, o_refs):
                o_ref[rows, :] = lax.dot_general(hv, w_ref[...], (((1,), (1,)), ((), ())), preferred_element_type=F32).astype(o_ref.dtype)

    return pl.pallas_call(
        body, grid=(s // tm,), in_specs=[_rows(tm, d)] + [_resident(w.shape) for w in w_ts] + [ANY] * na,
        out_specs=[_rows(tm, w.shape[0]) for w in w_ts],
        out_shape=[jax.ShapeDtypeStruct((s, w.shape[0]), BF16) for w in w_ts],
        compiler_params=_params(("parallel",)), name=name,
    )(h, *w_ts, *after)


def _mix_tail_fwd(ya_in, yb_in, xc, pgt, w_ups, w_o, x, g_post, g_next, after, name, tm=512):
    s, d = x.shape
    tm = _tile(s, tm)
    na = len(after)
    branch_ins = (ya_in, yb_in, xc)

    def body(a_ref, b_ref, c_ref, gt_ref, wa_ref, wb_ref, wc_ref, wo_ref, x_ref, gp_ref, gn_ref, *rest):
        ya_ref, yb_ref, yc_ref, m_ref, y_ref, xo_ref, h_ref = rest[na:]
        for rows in _sub_blocks(tm):
            merged = None
            for j, (in_ref, w_ref, out_ref) in enumerate(((a_ref, wa_ref, ya_ref), (b_ref, wb_ref, yb_ref), (c_ref, wc_ref, yc_ref))):
                yj = jnp.dot(in_ref[rows, :], w_ref[...], preferred_element_type=F32)
                out_ref[rows, :] = yj.astype(out_ref.dtype)
                part = _sigmoid(gt_ref[rows, j * D:(j + 1) * D].astype(F32)) * yj
                merged = part if merged is None else merged + part
            merged_b = merged.astype(m_ref.dtype)
            m_ref[rows, :] = merged_b
            y = jnp.dot(merged_b, wo_ref[...], preferred_element_type=F32)
            y_ref[rows, :] = y
            yh, _ = _rms(y)
            xn = x_ref[rows, :] + yh * gp_ref[...]
            xo_ref[rows, :] = xn
            xh, _ = _rms(xn)
            h_ref[rows, :] = (xh * gn_ref[...]).astype(h_ref.dtype)

    bf = lambda: jax.ShapeDtypeStruct((s, d), BF16)
    f32 = lambda: jax.ShapeDtypeStruct((s, d), F32)
    return pl.pallas_call(
        body, grid=(s // tm,),
        in_specs=[_rows(tm, a.shape[1]) for a in branch_ins] + [_rows(tm, 3 * d)] + [_resident(w.shape) for w in w_ups]
        + [_resident(w_o.shape), _rows(tm, d), _fixed((1, d)), _fixed((1, d))] + [ANY] * na,
        out_specs=[_rows(tm, d)] * 7,
        out_shape=[bf(), bf(), bf(), bf(), f32(), f32(), bf()],
        compiler_params=_params(("parallel",)), name=name,
    )(*branch_ins, pgt, *w_ups, w_o, x, g_post, g_next, *after)


def _mix_tail_bwd(dy, pgt, ys, w_ups, w_o, after, name, tm=512):
    s, d = dy.shape
    tm = _tile(s, tm)
    na = len(after)
    widths = [w.shape[0] for w in w_ups]

    def body(dy_ref, gt_ref, ya_ref, yb_ref, yc_ref, wa_ref, wb_ref, wc_ref, wo_ref, *rest):
        dya_ref, dyb_ref, dyc_ref, dgt_ref, da_ref, db_ref, dc_ref = rest[na:]
        nt = (((1,), (1,)), ((), ()))
        for rows in _sub_blocks(tm):
            dm = lax.dot_general(dy_ref[rows, :], wo_ref[...], nt, preferred_element_type=F32)
            for j, (y_ref, dyj_ref, w_ref, din_ref) in enumerate(((ya_ref, dya_ref, wa_ref, da_ref), (yb_ref, dyb_ref, wb_ref, db_ref),
                                                                   (yc_ref, dyc_ref, wc_ref, dc_ref))):
                sig = _sigmoid(gt_ref[rows, j * D:(j + 1) * D].astype(F32))
                dyj = (dm * sig).astype(dyj_ref.dtype)
                dyj_ref[rows, :] = dyj
                dgt_ref[rows, j * D:(j + 1) * D] = (dm * y_ref[rows, :].astype(F32) * sig * (1.0 - sig)).astype(dgt_ref.dtype)
                din_ref[rows, :] = lax.dot_general(dyj, w_ref[...], nt, preferred_element_type=F32).astype(din_ref.dtype)

    bf = lambda w: jax.ShapeDtypeStruct((s, w), BF16)
    return pl.pallas_call(
        body, grid=(s // tm,),
        in_specs=[_rows(tm, d), _rows(tm, 3 * d)] + [_rows(tm, d)] * 3 + [_resident(w.shape) for w in w_ups] + [_resident(w_o.shape)]
        + [ANY] * na,
        out_specs=[_rows(tm, d)] * 3 + [_rows(tm, 3 * d)] + [_rows(tm, w) for w in widths],
        out_shape=[bf(d), bf(d), bf(d), bf(3 * d)] + [bf(w) for w in widths],
        compiler_params=_params(("parallel",)), name=name,
    )(dy, pgt, *ys, *w_ups, w_o, *after)


def _adam_math(w, g, m, v):
    mn = ADAM_B1 * m + (1.0 - ADAM_B1) * g
    vn = ADAM_B2 * v + (1.0 - ADAM_B2) * (g * g)
    m_hat = mn / (1.0 - ADAM_B1 ** ADAM_STEP)
    v_hat = vn / (1.0 - ADAM_B2 ** ADAM_STEP)
    return -ADAM_LR * (m_hat / (jnp.sqrt(v_hat) + ADAM_EPS) + ADAM_WD * w), mn, vn


def _adamw(w, g, m, v, name):
    r, c = w.shape[-2:]
    tr, tc = _block_of(r, c, cap=512 if r % 16 == 0 else 256)

    def spec(a):
        if a.ndim == 2:
            return pl.BlockSpec((tr, tc), lambda i, j: (i, j))
        return pl.BlockSpec((None, tr, tc), lambda i, j: (0, i, j))

    def body(w_ref, g_ref, m_ref, v_ref, d_ref, mo_ref, vo_ref):
        d_ref[...], mo_ref[...], vo_ref[...] = _adam_math(w_ref[...], g_ref[...], m_ref[...], v_ref[...])

    return pl.pallas_call(
        body, grid=(r // tr, c // tc), in_specs=[spec(a) for a in (w, g, m, v)], out_specs=[spec(w)] * 3,
        out_shape=[jax.ShapeDtypeStruct(w.shape, F32)] * 3, compiler_params=_params(("parallel", "parallel")), name=name,
    )(w, g, m, v)


ANY = pl.BlockSpec(memory_space=pl.ANY)


def _place():
    x, y, c = lax.axis_index("x"), lax.axis_index("y"), lax.axis_index("c")
    chips = [(1 - x, y), (x, 1 - y), (1 - x, 1 - y)]
    return x, y, c, chips


def _by_cols(rows):
    return rows % 32 != 0 and rows != 16


def _half_of(ref, lead, c):
    r, cols = ref.shape[-2:]
    if _by_cols(r):
        return ref.at[(*lead, slice(None), pl.ds(pl.multiple_of(c * (cols // 2), LANE), cols // 2))]
    return ref.at[(*lead, pl.ds(pl.multiple_of(c * (r // 2), 8), r // 2))]


def _half_shape(shape):
    r, cols = shape[-2:]
    return shape[:-2] + ((r, cols // 2) if _by_cols(r) else (r // 2, cols))


def _block_of(r, cols, cap=256):
    if r % 16 == 0:
        return _tile(r, cap, 16), cols
    return r, _tile(cols, cap)


def _place_shard(shard, chip_arr, out_dtype, name, after=()):
    _, r, cols = shard.shape
    tr, tc = _block_of(r, cols)

    def body(chip_ref, s_ref, *rest):
        o_ref = rest[len(after)]
        o_ref[...] = s_ref[...].astype(o_ref.dtype)

    return pl.pallas_call(
        body,
        grid_spec=pltpu.PrefetchScalarGridSpec(
            num_scalar_prefetch=1, grid=(r // tr, cols // tc),
            in_specs=[pl.BlockSpec((None, tr, tc), lambda i, j, chip_ref: (0, i, j))] + [ANY] * len(after),
            out_specs=pl.BlockSpec((None, tr, tc), lambda i, j, chip_ref: (chip_ref[0], i, j))),
        out_shape=jax.ShapeDtypeStruct((4, r, cols), out_dtype),
        compiler_params=_params(("parallel", "parallel")), name=name,
    )(chip_arr, shard, *after)


HBM = pl.BlockSpec(memory_space=pltpu.HBM)
SEM = pl.BlockSpec(memory_space=pltpu.SEMAPHORE)
EFFECT = pltpu.SideEffectType.DATAFLOW_SIDE_EFFECTING


def _in_hbm(arrays):
    return [pltpu.with_memory_space_constraint(a, pltpu.HBM) for a in arrays]


def _gather_start(bufs, after, name):
    n, na = len(bufs), len(after)

    def body(*refs):
        send_sem, recv_sem = refs[n + na], refs[n + na + 1]
        outs = refs[n + na + 2:2 * n + na + 2]
        token = refs[2 * n + na + 2]
        x, y, c, chips = _place()
        me = 2 * x + y
        for p, chip in enumerate(chips):
            for w in range(n):
                block = _half_of(outs[w], (me,), c)
                pltpu.make_async_remote_copy(
                    src_ref=block, dst_ref=block, send_sem=send_sem, recv_sem=recv_sem,
                    device_id=(*chip, c), device_id_type=MESH).start()
        token[...] = jnp.zeros_like(token)

    out = pl.pallas_call(
        body, name=name, in_specs=[HBM] * n + [ANY] * na,
        out_specs=[SEM, SEM] + [HBM] * n + [pl.BlockSpec(memory_space=pltpu.VMEM)],
        out_shape=[pltpu.SemaphoreType.DMA(()), pltpu.SemaphoreType.DMA(())]
        + [pltpu.HBM(a.shape, a.dtype) for a in bufs] + [jax.ShapeDtypeStruct((8, LANE), F32)],
        input_output_aliases={w: w + 2 for w in range(n)},
        compiler_params=pltpu.CompilerParams(has_side_effects=EFFECT),
    )(*_in_hbm(bufs), *after)
    return out[0], out[1], list(out[2:2 + n]), out[2 + n]


def _gather_pass(bufs, send_sem, recv_sem, after, name):
    n, na = len(bufs), len(after)

    def body(*refs):
        send1, recv1 = refs[n], refs[n + 1]
        send2, recv2 = refs[n + 2 + na], refs[n + 3 + na]
        outs = refs[n + 4 + na:2 * n + 4 + na]
        x, y, c, chips = _place()
        me = 2 * x + y
        arrivals = [(w, px, py) for px, py in chips for w in range(n)]
        for w, px, py in arrivals:
            first = pltpu.make_async_remote_copy(
                src_ref=_half_of(outs[w], (me,), c), dst_ref=_half_of(outs[w], (2 * px + py,), c), send_sem=send1, recv_sem=recv1,
                device_id=(px, py, c), device_id_type=MESH)
            first.wait_send()
            first.wait_recv()
        for w, px, py in arrivals:
            arrived = _half_of(outs[w], (2 * px + py,), c)
            pltpu.make_async_remote_copy(
                src_ref=arrived, dst_ref=arrived, send_sem=send2, recv_sem=recv2,
                device_id=(x, y, 1 - c), device_id_type=MESH).start()

    out = pl.pallas_call(
        body, name=name, in_specs=[HBM] * n + [SEM, SEM] + [ANY] * na,
        out_specs=[SEM, SEM] + [HBM] * n,
        out_shape=[pltpu.SemaphoreType.DMA(()), pltpu.SemaphoreType.DMA(())] + [pltpu.HBM(a.shape, a.dtype) for a in bufs],
        input_output_aliases={w: w + 2 for w in range(n)},
        compiler_params=pltpu.CompilerParams(has_side_effects=EFFECT),
    )(*bufs, send_sem, recv_sem, *after)
    return out[0], out[1], list(out[2:])


def _gather_finish(bufs, send_sem, recv_sem, after, name):
    n, na = len(bufs), len(after)

    def body(*refs):
        send2, recv2 = refs[n], refs[n + 1]
        outs = refs[n + 2 + na:2 * n + 2 + na]
        x, y, c, chips = _place()
        for p, (px, py) in enumerate(chips):
            for w in range(n):
                passed = pltpu.make_async_remote_copy(
                    src_ref=_half_of(outs[w], (2 * px + py,), c), dst_ref=_half_of(outs[w], (2 * px + py,), 1 - c),
                    send_sem=send2, recv_sem=recv2, device_id=(x, y, 1 - c), device_id_type=MESH)
                passed.wait_send()
                passed.wait_recv()

    out = pl.pallas_call(
        body, name=name, in_specs=[HBM] * n + [SEM, SEM] + [ANY] * na, out_specs=[HBM] * n,
        out_shape=[pltpu.HBM(a.shape, a.dtype) for a in bufs],
        input_output_aliases={w: w for w in range(n)},
        compiler_params=pltpu.CompilerParams(has_side_effects=EFFECT),
    )(*bufs, send_sem, recv_sem, *after)
    return list(out)


def _pair_exchange_start(grads, after, name):
    n, na = len(grads), len(after)
    lands = [lax.empty(_half_shape(a.shape), a.dtype) for a in grads]

    def body(*refs):
        send_sem, recv_sem = refs[2 * n + na], refs[2 * n + na + 1]
        srcs = refs[2 * n + na + 2:3 * n + na + 2]
        dsts = refs[3 * n + na + 2:4 * n + na + 2]
        token = refs[4 * n + na + 2]
        x, y, c, _ = _place()
        for w in range(n):
            pltpu.make_async_remote_copy(
                src_ref=_half_of(srcs[w], (slice(None),), 1 - c), dst_ref=dsts[w], send_sem=send_sem, recv_sem=recv_sem,
                device_id=(x, y, 1 - c), device_id_type=MESH).start()
        token[...] = jnp.zeros_like(token)

    out = pl.pallas_call(
        body, name=name, in_specs=[HBM] * (2 * n) + [ANY] * na,
        out_specs=[SEM, SEM] + [HBM] * (2 * n) + [pl.BlockSpec(memory_space=pltpu.VMEM)],
        out_shape=[pltpu.SemaphoreType.DMA(()), pltpu.SemaphoreType.DMA(())]
        + [pltpu.HBM(a.shape, a.dtype) for a in grads + lands] + [jax.ShapeDtypeStruct((8, LANE), F32)],
        input_output_aliases={w: w + 2 for w in range(2 * n)},
        compiler_params=pltpu.CompilerParams(has_side_effects=EFFECT),
    )(*_in_hbm(grads), *_in_hbm(lands), *after)
    return out[0], out[1], list(out[2:2 + n]), list(out[2 + n:2 + 2 * n]), out[2 + 2 * n]


def _pair_exchange_finish(grads, lands, send_sem, recv_sem, after, name):
    n, na = len(grads), len(after)

    def body(*refs):
        send, recv = refs[2 * n], refs[2 * n + 1]
        srcs = refs[2 * n + 2 + na:3 * n + 2 + na]
        dsts = refs[3 * n + 2 + na:4 * n + 2 + na]
        x, y, c, _ = _place()
        for w in range(n):
            copy = pltpu.make_async_remote_copy(
                src_ref=_half_of(srcs[w], (slice(None),), 1 - c), dst_ref=dsts[w], send_sem=send, recv_sem=recv,
                device_id=(x, y, 1 - c), device_id_type=MESH)
            copy.wait_send()
            copy.wait_recv()

    out = pl.pallas_call(
        body, name=name, in_specs=[HBM] * (2 * n) + [SEM, SEM] + [ANY] * na, out_specs=[HBM] * (2 * n),
        out_shape=[pltpu.HBM(a.shape, a.dtype) for a in grads + lands],
        input_output_aliases={w: w for w in range(2 * n)},
        compiler_params=pltpu.CompilerParams(has_side_effects=EFFECT),
    )(*grads, *lands, send_sem, recv_sem, *after)
    return list(out[:n]), list(out[n:])


def _pair_sum(g, got, c_arr, name):
    _, r, cols = g.shape
    hr, hc = _half_shape((r, cols))
    tr, tc = _block_of(hr, hc)
    nbr, nbc = hr // tr, hc // tc
    by_cols = _by_cols(r)

    def body(c_ref, g_ref, got_ref, o_ref):
        o_ref[...] = (g_ref[...].astype(F32) + got_ref[...].astype(F32)).astype(o_ref.dtype)

    def mine(j, i, k, c_ref):
        return (j, i, c_ref[0] * nbc + k) if by_cols else (j, c_ref[0] * nbr + i, k)

    return pl.pallas_call(
        body,
        grid_spec=pltpu.PrefetchScalarGridSpec(
            num_scalar_prefetch=1, grid=(4, nbr, nbc),
            in_specs=[pl.BlockSpec((None, tr, tc), mine),
                      pl.BlockSpec((None, tr, tc), lambda j, i, k, c_ref: (j, i, k))],
            out_specs=pl.BlockSpec((None, tr, tc), lambda j, i, k, c_ref: (j, i, k))),
        out_shape=jax.ShapeDtypeStruct((4, hr, hc), BF16),
        compiler_params=_params(("parallel", "parallel", "parallel")), name=name,
    )(c_arr, *_in_hbm([g, got]))


def _chip_exchange_start(parts, after, name):
    n, na = len(parts), len(after)
    lands = [lax.empty((3,) + a.shape[1:], a.dtype) for a in parts]

    def body(*refs):
        send_sem, recv_sem = refs[2 * n + na], refs[2 * n + na + 1]
        srcs = refs[2 * n + na + 2:3 * n + na + 2]
        dsts = refs[3 * n + na + 2:4 * n + na + 2]
        token = refs[4 * n + na + 2]
        x, y, c, chips = _place()
        for p, (px, py) in enumerate(chips):
            for w in range(n):
                pltpu.make_async_remote_copy(
                    src_ref=srcs[w].at[2 * px + py], dst_ref=dsts[w].at[p], send_sem=send_sem, recv_sem=recv_sem,
                    device_id=(px, py, c), device_id_type=MESH).start()
        token[...] = jnp.zeros_like(token)

    out = pl.pallas_call(
        body, name=name, in_specs=[HBM] * (2 * n) + [ANY] * na,
        out_specs=[SEM, SEM] + [HBM] * (2 * n) + [pl.BlockSpec(memory_space=pltpu.VMEM)],
        out_shape=[pltpu.SemaphoreType.DMA(()), pltpu.SemaphoreType.DMA(())]
        + [pltpu.HBM(a.shape, a.dtype) for a in parts + lands] + [jax.ShapeDtypeStruct((8, LANE), F32)],
        input_output_aliases={w: w + 2 for w in range(2 * n)},
        compiler_params=pltpu.CompilerParams(has_side_effects=EFFECT),
    )(*_in_hbm(parts), *_in_hbm(lands), *after)
    return out[0], out[1], list(out[2:2 + n]), list(out[2 + n:2 + 2 * n]), out[2 + 2 * n]


def _chip_exchange_finish(parts, lands, send_sem, recv_sem, after, name):
    n, na = len(parts), len(after)

    def body(*refs):
        send, recv = refs[2 * n], refs[2 * n + 1]
        srcs = refs[2 * n + 2 + na:3 * n + 2 + na]
        dsts = refs[3 * n + 2 + na:4 * n + 2 + na]
        x, y, c, chips = _place()
        for p, (px, py) in enumerate(chips):
            for w in range(n):
                copy = pltpu.make_async_remote_copy(
                    src_ref=srcs[w].at[2 * px + py], dst_ref=dsts[w].at[p], send_sem=send, recv_sem=recv,
                    device_id=(px, py, c), device_id_type=MESH)
                copy.wait_send()
                copy.wait_recv()

    out = pl.pallas_call(
        body, name=name, in_specs=[HBM] * (2 * n) + [SEM, SEM] + [ANY] * na, out_specs=[HBM] * (2 * n),
        out_shape=[pltpu.HBM(a.shape, a.dtype) for a in parts + lands],
        input_output_aliases={w: w for w in range(2 * n)},
        compiler_params=pltpu.CompilerParams(has_side_effects=EFFECT),
    )(*parts, *lands, send_sem, recv_sem, *after)
    return list(out[:n]), list(out[n:])


def _chip_sum(part, got, place_arr, name):
    _, hr, hc = part.shape
    by_cols = _by_cols(hr)
    tr, tc = _block_of(hr, hc)
    nbr, nbc = hr // tr, hc // tc

    def body(place_ref, p_ref, got_ref, o_ref):
        acc = p_ref[...].astype(F32)
        for p in range(3):
            acc = acc + got_ref[p].astype(F32)
        o_ref[...] = acc

    def mine(i, k, place_ref):
        return (i, place_ref[1] * nbc + k) if by_cols else (place_ref[1] * nbr + i, k)

    return pl.pallas_call(
        body,
        grid_spec=pltpu.PrefetchScalarGridSpec(
            num_scalar_prefetch=1, grid=(nbr, nbc),
            in_specs=[pl.BlockSpec((None, tr, tc), lambda i, k, place_ref: (place_ref[0], i, k)),
                      pl.BlockSpec((3, tr, tc), lambda i, k, place_ref: (0, i, k))],
            out_specs=pl.BlockSpec((tr, tc), mine)),
        out_shape=jax.ShapeDtypeStruct((hr, 2 * hc) if by_cols else (2 * hr, hc), F32),
        compiler_params=_params(("parallel", "parallel")), name=name,
    )(place_arr, *_in_hbm([part, got]))


def _pair_join_start(bufs, name):
    n = len(bufs)

    def body(*refs):
        send_sem, recv_sem = refs[n], refs[n + 1]
        outs = refs[n + 2:2 * n + 2]
        token = refs[2 * n + 2]
        x, y, c, _ = _place()
        for w in range(n):
            block = _half_of(outs[w], (), c)
            pltpu.make_async_remote_copy(
                src_ref=block, dst_ref=block, send_sem=send_sem, recv_sem=recv_sem,
                device_id=(x, y, 1 - c), device_id_type=MESH).start()
        token[...] = jnp.zeros_like(token)

    out = pl.pallas_call(
        body, name=name, in_specs=[HBM] * n,
        out_specs=[SEM, SEM] + [HBM] * n + [pl.BlockSpec(memory_space=pltpu.VMEM)],
        out_shape=[pltpu.SemaphoreType.DMA(()), pltpu.SemaphoreType.DMA(())]
        + [pltpu.HBM(a.shape, a.dtype) for a in bufs] + [jax.ShapeDtypeStruct((8, LANE), F32)],
        input_output_aliases={w: w + 2 for w in range(n)},
        compiler_params=pltpu.CompilerParams(has_side_effects=EFFECT),
    )(*_in_hbm(bufs))
    return out[0], out[1], list(out[2:2 + n]), out[2 + n]


def _pair_join_finish(bufs, send_sem, recv_sem, after, name):
    n, na = len(bufs), len(after)

    def body(*refs):
        send, recv = refs[n], refs[n + 1]
        outs = refs[n + 2 + na:2 * n + 2 + na]
        x, y, c, _ = _place()
        for w in range(n):
            copy = pltpu.make_async_remote_copy(
                src_ref=_half_of(outs[w], (), c), dst_ref=_half_of(outs[w], (), 1 - c), send_sem=send, recv_sem=recv,
                device_id=(x, y, 1 - c), device_id_type=MESH)
            copy.wait_send()
            copy.wait_recv()

    out = pl.pallas_call(
        body, name=name, in_specs=[HBM] * n + [SEM, SEM] + [ANY] * na, out_specs=[HBM] * n,
        out_shape=[pltpu.HBM(a.shape, a.dtype) for a in bufs],
        input_output_aliases={w: w for w in range(n)},
        compiler_params=pltpu.CompilerParams(has_side_effects=EFFECT),
    )(*bufs, send_sem, recv_sem, *after)
    return list(out)


SMALL = ("ffn1_pre_g", "ffn1_post_g", "mix_pre_g", "gla_norm_g", "mem_norm_g", "mix_post_g", "ffn2_pre_g", "ffn2_post_g", "final_g",
         "b_f", "pool_scale", "w_pool", "w_fu")
N_GAINS = 9
SMALL_PACKS = ((16, D), (24, 512), (4 * LANE, LANE))
W_FU_ROW = 8


LOSS_ROW = 2


def _all_sum_small(gs, loss, name, after=()):
    ins = [gs[n] for n in SMALL[:N_GAINS]] + [gs["b_f"], gs["pool_scale"], gs["w_fu_pad"], gs["w_pool"].reshape(4 * LANE, LANE), loss]

    def body(*refs):
        gain_refs = refs[:N_GAINS]
        bf_ref, ps_ref, wfu_ref, wp_ref, loss_ref = refs[N_GAINS:N_GAINS + 5]
        outs = refs[N_GAINS + 5 + len(after):N_GAINS + 8 + len(after)]
        mine_a, mine_b, all_a, all_b, all_c, send_sems, recv_sems = refs[N_GAINS + 8 + len(after):]
        mine_a[...] = jnp.zeros_like(mine_a)
        for i, ref in enumerate(gain_refs):
            mine_a[i:i + 1, :] = ref[...]
        mine_b[...] = jnp.zeros_like(mine_b)
        mine_b[0:1, :] = bf_ref[...]
        mine_b[1:2, :] = ps_ref[...]
        mine_b[LOSS_ROW:LOSS_ROW + 1, 0:LANE] = loss_ref[0:1, :]
        mine_b[W_FU_ROW:W_FU_ROW + GATE_RANK, :] = wfu_ref[0:GATE_RANK, :]
        packs = ((mine_a, all_a), (mine_b, all_b), (wp_ref, all_c))
        x, y, c, chips = _place()
        me, sibling = (x, y, c), (x, y, 1 - c)

        def copy(t, k, block, to, own=False):
            px, py, pc = block
            slot = packs[t][1].at[4 * px + 2 * py + pc]
            return pltpu.make_async_remote_copy(
                src_ref=packs[t][0] if own else slot, dst_ref=slot,
                send_sem=send_sems.at[t, k], recv_sem=recv_sems.at[t, k], device_id=to, device_id_type=MESH)

        started = []
        for t, (mine, everyone) in enumerate(packs):
            everyone[4 * x + 2 * y + c] = mine[...]
            started.append(copy(t, 0, me, sibling, own=True))
            started += [copy(t, 1 + j, me, (*chip, c), own=True) for j, chip in enumerate(chips)]
        for cp in started:
            cp.start()
        passed = []
        for j, chip in enumerate(chips):
            for t in range(len(packs)):
                copy(t, 1 + j, (*chip, c), me).wait_recv()
                fwd = copy(t, 4 + j, (*chip, c), sibling)
                fwd.start()
                passed.append(fwd)
        for t in range(len(packs)):
            copy(t, 0, sibling, me).wait_recv()
            for j, chip in enumerate(chips):
                copy(t, 4 + j, (*chip, 1 - c), me).wait_recv()
        for cp in started + passed:
            cp.wait_send()
        for (_, everyone), o_ref in zip(packs, outs):
            acc = everyone[0]
            for k in range(1, 8):
                acc = acc + everyone[k]
            o_ref[...] = acc

    vmem = pl.BlockSpec(memory_space=pltpu.VMEM)
    return pl.pallas_call(
        body, in_specs=[vmem] * len(ins) + [ANY] * len(after), out_specs=[vmem] * 3,
        out_shape=[jax.ShapeDtypeStruct(shape, F32) for shape in SMALL_PACKS],
        scratch_shapes=[pltpu.VMEM(SMALL_PACKS[0], F32), pltpu.VMEM(SMALL_PACKS[1], F32)]
        + [pltpu.VMEM((8,) + shape, F32) for shape in SMALL_PACKS]
        + [pltpu.SemaphoreType.DMA((3, 7)), pltpu.SemaphoreType.DMA((3, 7))],
        compiler_params=pltpu.CompilerParams(has_side_effects=True, vmem_limit_bytes=VMEM_LIMIT), name=name,
    )(*ins, *after)


def _adamw_small(sums, params, chip_arr, name):
    flat = [a for n in SMALL for a in params[n]]

    def body(chip_ref, a_ref, b_ref, c_ref, *refs):
        ins, outs = refs[:len(flat)], refs[len(flat):]
        for i, n in enumerate(SMALL):
            w_ref, m_ref, v_ref = ins[3 * i:3 * i + 3]
            g_ref, d_ref, mo_ref, vo_ref = outs[4 * i:4 * i + 4]
            if n == "w_pool":
                pieces = [((0, k), c_ref[k * LANE:(k + 1) * LANE, :]) for k in range(4)]
            elif n == "w_fu":
                mine = pl.ds(pl.multiple_of(chip_ref[0] * LANE, LANE), LANE)
                pieces = [((0,), b_ref[W_FU_ROW:W_FU_ROW + GATE_RANK, mine])]
            elif n == "b_f":
                pieces = [((), b_ref[0:1, :])]
            elif n == "pool_scale":
                pieces = [((), b_ref[1:2, :])]
            else:
                pieces = [((), a_ref[i:i + 1, :])]
            for at, g in pieces:
                d, mn, vn = _adam_math(w_ref[at], g, m_ref[at], v_ref[at])
                g_ref[at] = g
                d_ref[at] = d
                mo_ref[at] = mn
                vo_ref[at] = vn

    def whole(shape):
        return pl.BlockSpec(shape, lambda i, chip_ref: (0,) * len(shape))

    out = pl.pallas_call(
        body,
        grid_spec=pltpu.PrefetchScalarGridSpec(
            num_scalar_prefetch=1, grid=(1,),
            in_specs=[whole(a.shape) for a in list(sums) + flat],
            out_specs=[whole(params[n][0].shape) for n in SMALL for _ in range(4)]),
        out_shape=[jax.ShapeDtypeStruct(params[n][0].shape, F32) for n in SMALL for _ in range(4)],
        compiler_params=_params(("arbitrary",)), name=name,
    )(chip_arr, *sums, *flat)
    return {n: tuple(out[4 * i:4 * i + 4]) for i, n in enumerate(SMALL)}


def _ffn_bwd(dz, x_norm, ab, u, w_in, w_out, x, g_pre, dres, tag, emit, advance, after=(), post=None):
    dw_out = _mm(u, dz, ta=True, out_dtype=BF16, tm=1408, tk=2048, after=after, name=tag + "_out_dw")
    behind = emit(tag + "_w_out", dw_out)
    dab = _ffn_out_dx_swiglu(dz, w_out, ab, behind, name=tag + "_out_dx")
    behind = advance((dab,))
    dw_in = _mm(x_norm, dab, ta=True, out_dtype=BF16, tm=512, tk=4096, shards=4, after=behind, name=tag + "_in_dw")
    behind = emit(tag + "_w_in", dw_in)
    out = _mm_rms_bwd([(dab, w_in)], x, g_pre, dres, after=behind, post=post, name=tag + "_in_dx")
    return (*out, advance((out[0],)))


def _local_step(x, mem, target, small, gather, emit, advance):
    behind = gather("start", "ffn1i", ())
    behind = gather("start", "ffn1o", behind)
    h1 = _norm_fwd(x, small["ffn1_pre_g"], BF16, name="ffn1_pre", after=behind)
    gather("pass", "ffn1i", (h1,))
    big = gather("finish", "ffn1i", ())
    behind = gather("start", "mixa", (big["ffn1_w_in"],))
    behind = gather("start", "mixb", behind)
    behind = gather("start", "ffn2", behind)
    ab1, u1 = _ffn_in_swiglu(h1, big["ffn1_w_in"], name="ffn1_in", after=behind)
    gather("pass", "ffn1o", (ab1,))
    big.update(gather("finish", "ffn1o", ()))
    behind = gather("pass", "mixa", (u1,))
    f1, x1, h = _mm_resid_norm(u1, big["ffn1_w_out"], x, small["ffn1_post_g"], 0.5, small["mix_pre_g"], name="ffn1_out", after=behind)
    big.update(gather("finish", "mixa", (h,)))
    small = dict(small, w_fu_pad=big["w_fu_pad"])
    behind = gather("pass", "mixb", (h,))
    pg, ppx, pgt = _mix_in_fwd(h, [big["w_gla_t"], big["w_px_t"], big["w_gates_t"]], behind, name="mix_in")
    big.update(gather("finish", "mixb", (pgt,)))
    mem_n = _norm_fwd(mem, small["mem_norm_g"], BF16, name="mem_norm")
    kv = _mm(mem_n, big["w_mem_kv"], out_dtype=BF16, name="mem_kv")
    ya_in, sp, so, o_gla = _gla_fwd(pg, small["w_fu_pad"], small["b_f"], small["gla_norm_g"], name="gla_fwd")
    yb_in = _pool_fwd(ppx, small["w_pool_b"], small["pool_scale"], name="pool_fwd")
    xc = _xattn_fwd(ppx, kv, name="xattn_fwd")
    behind = gather("pass", "ffn2", (xc,))
    w_ups = [big["w_up_gla"], big["w_up_pool"], big["w_up_xattn"]]
    ya, yb, yc, merged, ymix, x2, h2 = _mix_tail_fwd(ya_in, yb_in, xc, pgt, w_ups, big["w_o"], x1, small["mix_post_g"],
                                                     small["ffn2_pre_g"], behind, name="mix_tail")
    big.update(gather("finish", "ffn2", (h2,)))
    ab2, u2 = _ffn_in_swiglu(h2, big["ffn2_w_in"], name="ffn2_in")
    gs = {}
    dz2, dx3, gs["ffn2_post_g"], gs["final_g"], loss = _ffn_out_loss(u2, big["ffn2_w_out"], x2, small["ffn2_post_g"], 0.5,
                                                                    small["final_g"], target, name="ffn2_out_loss")
    dx2, gs["ffn2_pre_g"], dy, gs["mix_post_g"], behind = _ffn_bwd(
        dz2, h2, ab2, u2, big["ffn2_w_in"], big["ffn2_w_out"], x2, small["ffn2_pre_g"], dx3, "ffn2", emit, advance,
        post=(ymix, small["mix_post_g"], 1.0))
    emit("w_o", _mm(merged, dy, ta=True, out_dtype=BF16, tm=512, tk=4096, after=behind, name="mix_out_dw"))
    dya, dyb, dyc, dgt, dya_in, dyb_in, dxc = _mix_tail_bwd(dy, pgt, (ya, yb, yc), w_ups, big["w_o"], (), name="mix_tail_bwd")
    emit("w_up_gla", _mm(ya_in, dya, ta=True, out_dtype=BF16, tm=512, tk=4096, name="up_gla_dw"))
    emit("w_up_pool", _mm(yb_in, dyb, ta=True, out_dtype=BF16, tm=512, tk=4096, shards=4, name="up_pool_dw"))
    emit("w_up_xattn", _mm(xc, dyc, ta=True, out_dtype=BF16, tm=512, tk=4096, shards=4, name="up_xattn_dw"))
    dpg, gs["w_fu_pad"], gs["b_f"], gs["gla_norm_g"] = _gla_bwd(pg, sp, so, o_gla, dya_in, small["w_fu_pad"], small["b_f"], small["gla_norm_g"], name="gla_bwd")
    dp, gs["w_pool"], gs["pool_scale"] = _pool_bwd(dyb_in, ppx, small["w_pool_b"], small["pool_scale"], name="pool_bwd")
    dxq, dkv = _xattn_bwd(dxc, ppx, kv, name="xattn_bwd")
    dkv = dkv.astype(BF16)
    emit("w_mem_kv", _mm(mem_n, dkv, ta=True, out_dtype=BF16, name="mem_kv_dw"))
    dmem_n = _mm(dkv, big["w_mem_kv"], tb=True, name="mem_kv_dx")
    _, gs["mem_norm_g"] = _rms_bwd(mem, small["mem_norm_g"], [dmem_n], None, 1.0, BF16, name="mem_norm_bwd")
    emit("w_gla", _mm(dpg, h, ta=True, out_dtype=BF16, tm=640, tk=4096, name="mix_in_gla_dw"))
    emit("w_p", _mm(dp, h, ta=True, out_dtype=BF16, tm=512, tk=4096, name="mix_in_p_dw"))
    emit("w_xq", _mm(dxq, h, ta=True, out_dtype=BF16, tm=512, tk=4096, name="mix_in_xq_dw"))
    behind = emit("w_gates", _mm(dgt, h, ta=True, out_dtype=BF16, tm=512, tk=4096, name="mix_in_gates_dw"))
    pairs = [(dpg, big["w_gla_t"]), (dp, big["w_p_t"]), (dxq, big["w_xq_t"]), (dgt, big["w_gates_t"])]
    dx1, gs["mix_pre_g"], dz1, gs["ffn1_post_g"] = _mm_rms_bwd(pairs, x1, small["mix_pre_g"], dx2, after=behind,
                                                               post=(f1, small["ffn1_post_g"], 0.5), tm=256, name="mix_in_dx")
    behind = advance((dx1,))
    dx0, gs["ffn1_pre_g"], _ = _ffn_bwd(dz1, h1, ab1, u1, big["ffn1_w_in"], big["ffn1_w_out"], x, small["ffn1_pre_g"], dx1,
                                        "ffn1", emit, advance, after=behind)
    return loss, dx0, gs


BIG = ("ffn1_w_in", "ffn1_w_out", "w_in", "w_mem_kv", "w_up_gla", "w_up_pool", "w_up_xattn", "w_o", "ffn2_w_in", "ffn2_w_out")
COL_SHARDED = ("ffn1_w_in", "w_in", "w_up_pool", "w_up_xattn", "ffn2_w_in")
GATHER_GROUPS = {"ffn1i": ("ffn1_w_in",), "ffn1o": ("ffn1_w_out",), "mixa": ("w_in", "w_fu"),
                 "mixb": ("w_mem_kv", "w_up_gla", "w_up_pool", "w_up_xattn", "w_o"), "ffn2": ("ffn2_w_in", "ffn2_w_out")}
REDUCE_GROUPS = {"ffn2": ("ffn2_w_out", "ffn2_w_in"),
                 "mix": ("w_o", "w_up_gla", "w_up_pool", "w_up_xattn", "w_mem_kv", "w_gla", "w_p", "w_xq", "w_gates"),
                 "ffn1_out": ("ffn1_w_out",),
                 "ffn1_in": ("ffn1_w_in",)}
REDUCE_FIRST, REDUCE_LAST = "ffn2", "ffn1_in"
GAINS = ("ffn1_pre_g", "ffn1_post_g", "mix_pre_g", "gla_norm_g", "mem_norm_g", "mix_post_g", "ffn2_pre_g", "ffn2_post_g", "final_g")
WEIGHTS = ("ffn1_pre_g", "ffn1_w_in", "ffn1_w_out", "ffn1_post_g", "mix_pre_g", "w_in", "w_fu", "b_f", "gla_norm_g", "w_pool",
           "pool_scale", "mem_norm_g", "w_mem_kv", "w_up_gla", "w_up_pool", "w_up_xattn", "w_o", "mix_post_g", "ffn2_pre_g",
           "ffn2_w_in", "ffn2_w_out", "ffn2_post_g", "final_g")
IN_GLA, IN_F, IN_PX, IN_GATES, IN_END = 0, 3072, 3088, 4112, 7184
def _cols_from_shards(g):
    return jnp.transpose(g, (1, 0, 2)).reshape(g.shape[1], 4 * g.shape[2])


def _laid_end_to_end(pieces, rows):
    out, start = None, 0
    for p in pieces:
        padded = jnp.pad(p, ((start, rows - start - p.shape[0]), (0, 0)))
        out = padded if out is None else out + padded
        start += p.shape[0]
    return out


def _rows_of_blocks(g, lo, hi, rows=None):
    q = g.shape[1]
    cuts = [(j, max(lo, j * q) - j * q, min(hi, (j + 1) * q) - j * q) for j in range(g.shape[0])]
    return _laid_end_to_end([g[j, a:b] for j, a, b in cuts if a < b], rows or hi - lo)


def _blocks_of_rows(parts, blocks):
    q = sum(p.shape[0] for p in parts) // blocks
    out = []
    for j in range(blocks):
        pieces, start = [], 0
        for p in parts:
            a, b = max(j * q, start), min((j + 1) * q, start + p.shape[0])
            if a < b:
                pieces.append(p[a - start:b - start])
            start += p.shape[0]
        out.append(_laid_end_to_end(pieces, q))
    return jnp.stack(out)


def kernel(x, mem, ffn1_pre_g, ffn1_w_in, ffn1_w_out, ffn1_post_g, mix_pre_g, w_in, w_fu, b_f, gla_norm_g, w_pool, pool_scale, mem_norm_g, w_mem_kv, w_up_gla, w_up_pool, w_up_xattn, w_o, mix_post_g, ffn2_pre_g, ffn2_w_in, ffn2_w_out, ffn2_post_g, final_g, loss_target, m_ffn1_pre_g, m_ffn1_w_in, m_ffn1_w_out, m_ffn1_post_g, m_mix_pre_g, m_w_in, m_w_fu, m_b_f, m_gla_norm_g, m_w_pool, m_pool_scale, m_mem_norm_g, m_w_mem_kv, m_w_up_gla, m_w_up_pool, m_w_up_xattn, m_w_o, m_mix_post_g, m_ffn2_pre_g, m_ffn2_w_in, m_ffn2_w_out, m_ffn2_post_g, m_final_g, v_ffn1_pre_g, v_ffn1_w_in, v_ffn1_w_out, v_ffn1_post_g, v_mix_pre_g, v_w_in, v_w_fu, v_b_f, v_gla_norm_g, v_w_pool, v_pool_scale, v_mem_norm_g, v_w_mem_kv, v_w_up_gla, v_w_up_pool, v_w_up_xattn, v_w_o, v_mix_post_g, v_ffn2_pre_g, v_ffn2_w_in, v_ffn2_w_out, v_ffn2_post_g, v_final_g):
    args = dict(locals())
    w = {n: args[n][0] for n in WEIGHTS}
    m = {n: args["m_" + n][0] for n in WEIGHTS}
    v = {n: args["v_" + n][0] for n in WEIGHTS}
    xi, yi, ci = lax.axis_index("x"), lax.axis_index("y"), lax.axis_index("c")
    chip = 2 * xi + yi

    c_arr = jnp.reshape(ci, (1,)).astype(jnp.int32)
    chip_arr = jnp.reshape(chip, (1,)).astype(jnp.int32)
    place_arr = jnp.stack([chip, ci]).astype(jnp.int32)
    w_in_t = []
    shard_of = {n: args[n] for n in BIG if n != "w_in"}
    shard_of["w_fu"] = args["w_fu"]
    placed, inflight = {}, {}

    def place(names, after):
        for n in names:
            if n not in placed:
                placed[n] = _place_shard(shard_of[n], chip_arr, F32 if n == "w_fu" else BF16, name="place_" + n, after=after)

    def relayout(names, gathered):
        out = {}
        for n, g in zip(names, gathered):
            if n == "w_fu":
                w_fu_full = _cols_from_shards(g)
                out["w_fu_pad"] = jnp.concatenate([w_fu_full, jnp.zeros((LANE - GATE_RANK, 512), F32)], axis=0).astype(BF16)
            elif n == "w_in":
                out["w_gla_t"] = _rows_of_blocks(g, IN_GLA, IN_PX, rows=PG_W)
                out["w_px_t"] = _rows_of_blocks(g, IN_PX, IN_GATES)
                out["w_p_t"] = _rows_of_blocks(g, IN_PX, IN_PX + 512)
                out["w_xq_t"] = _rows_of_blocks(g, IN_PX + 512, IN_GATES)
                out["w_gates_t"] = _rows_of_blocks(g, IN_GATES, IN_END)
            else:
                out[n] = _cols_from_shards(g) if n in COL_SHARDED else g.reshape(4 * g.shape[1], g.shape[2])
        return out

    def gather(op, group, after):
        names = GATHER_GROUPS[group]
        if op == "start":
            place(names, ())
            inflight[group] = _gather_start([placed[n] for n in names], after, name="gather_" + group + "_start")
            behind = (inflight[group][3],)
            if group == "ffn1o":
                tied = lax.optimization_barrier((behind, tuple(args[k] for k in ("w_in", "m_w_in", "v_w_in"))))[1]
                w_in_t.extend(jnp.transpose(a[0]) for a in tied)
                shard_of["w_in"] = w_in_t[0][None]
                place(shard_of, behind)
            return behind
        if op == "pass":
            if group == "ffn1i":
                not_started = [n for g in GATHER_GROUPS if g not in inflight for n in GATHER_GROUPS[g]]
                after = tuple(after) + tuple(w_in_t[1:]) + tuple(placed[n] for n in not_started)
            send, recv, bufs, _ = inflight[group]
            inflight[group] = _gather_pass(bufs, send, recv, after, name="gather_" + group + "_pass")
            return (inflight[group][2][0],)
        send, recv, bufs = inflight.pop(group)
        return relayout(names, _gather_finish(bufs, send, recv, after, name="gather_" + group + "_finish"))

    small = {n: w[n].reshape(1, D) for n in GAINS}
    small["b_f"] = w["b_f"].reshape(1, 512)
    small["pool_scale"] = w["pool_scale"].reshape(1, 512)
    small["w_pool_b"] = w["w_pool"].astype(BF16)

    pending, crossing, travelling = {}, {}, {}

    def emit(name, grad):
        pending[name] = grad
        group = next((g for g, names in REDUCE_GROUPS.items() if name == names[-1]), None)
        if group is None:
            return ()
        gb = {n: pending.pop(n) for n in REDUCE_GROUPS[group]}
        if group == "mix":
            gb["w_in"] = _blocks_of_rows([gb.pop("w_gla")[0:IN_PX], gb.pop("w_p"), gb.pop("w_xq"), gb.pop("w_gates")], 4)
        names = list(gb)
        contrib = [gb[n] if n in COL_SHARDED else gb[n].reshape(4, gb[n].shape[0] // 4, gb[n].shape[1]) for n in names]
        send, recv, contrib, lands, token = _pair_exchange_start(contrib, (), name="grads_" + group + "_pair_start")
        if group == REDUCE_LAST:
            behind = finish_chips(REDUCE_FIRST, (token,))
            contrib, from_sibling = _pair_exchange_finish(contrib, lands, send, recv, behind, name="grads_" + group + "_pair_finish")
            return over_chips(group, names, contrib, from_sibling)
        crossing[group] = (names, contrib, lands, send, recv)
        return (token,)

    def over_chips(group, names, contrib, from_sibling):
        pair = [_pair_sum(g, got, c_arr, name="grads_pair_sum_" + n) for n, g, got in zip(names, contrib, from_sibling)]
        send, recv, pair, lands, token = _chip_exchange_start(pair, (), name="grads_" + group + "_chip_start")
        travelling[group] = (names, send, recv, pair, lands)
        return (token,)

    def advance(after):
        behind = ()
        for group in list(crossing):
            names, contrib, lands, send, recv = crossing.pop(group)
            contrib, from_sibling = _pair_exchange_finish(contrib, lands, send, recv, after, name="grads_" + group + "_pair_finish")
            behind = over_chips(group, names, contrib, from_sibling)
        return behind

    halves = {}

    def finish_chips(group, after):
        names, send, recv, pair, lands = travelling.pop(group)
        pair, from_chips = _chip_exchange_finish(pair, lands, send, recv, after, name="grads_" + group + "_chip_finish")
        for n, p, got in zip(names, pair, from_chips):
            halves[n] = _chip_sum(p, got, place_arr, name="grads_chip_sum_" + n)
        return (halves[names[-1]],)

    loss, grad_x, gs = _local_step(x[0], mem[0], loss_target[0], small, gather, emit, advance)

    for group in list(travelling):
        finish_chips(group, (grad_x,))
    send, recv, joining, token = _pair_join_start([halves[n] for n in BIG], name="grads_pair_join_start")
    small_sums = _all_sum_small(gs, loss, name="sum_small_grads", after=(token,))
    loss = small_sums[1][LOSS_ROW, 0]
    reduced = dict(zip(BIG, _pair_join_finish(joining, send, recv, (small_sums[0],), name="grads_pair_join_finish")))

    grads, delta, new_m, new_v = {}, {}, {}, {}
    for n in BIG:
        if n == "w_in":
            updated = _adamw(w_in_t[0], reduced[n], w_in_t[1], w_in_t[2], name="adamw_" + n)
            grads[n] = jnp.transpose(reduced[n])[None]
            delta[n], new_m[n], new_v[n] = (jnp.transpose(a)[None] for a in updated)
            continue
        grads[n] = reduced[n][None]
        delta[n], new_m[n], new_v[n] = _adamw(args[n], reduced[n], args["m_" + n], args["v_" + n], name="adamw_" + n)
    small_params = {n: (args[n], args["m_" + n], args["v_" + n]) for n in SMALL}
    for n, (g, d, mn, vn) in _adamw_small(small_sums, small_params, chip_arr, name="adamw_small").items():
        grads[n], delta[n], new_m[n], new_v[n] = g, d, mn, vn

    outs = [loss, grad_x[None]]
    for group in (grads, delta, new_m, new_v):
        outs += [group[n] for n in WEIGHTS]
    return tuple(outs)
```
